```python
import math
import jax, jax.numpy as jnp
from jax import lax
import numpy as np

D_MODEL = 1024
BATCH = 8
SEQ = 2048
DEPTH = 1

GDN_HEADS = 4
GDN_HEAD_DIM = 128
GDN_WIDTH = GDN_HEADS * GDN_HEAD_DIM
GDN_CONV = 4
GDN_CHUNK = 64
DIL_HEADS = 8
DIL_HEAD_DIM = 64
DIL_WIDTH = DIL_HEADS * DIL_HEAD_DIM
DIL_PATTERNS = ((128, 1), (512, 4), (2048, 16))
BAND_BLOCK = 128
MIX_WIDTH = GDN_WIDTH + DIL_WIDTH
IN_COLS = 3 * GDN_WIDTH + GDN_WIDTH + 2 * GDN_HEADS + 3 * DIL_WIDTH
D_FF = 2816
FFN_CONV = 3
EPS = 1e-6

kernel_name = "hymba_gdn_dilated_convffn"


def rmsnorm(x, w):
    xf = x.astype(jnp.float32)
    y = xf * lax.rsqrt(jnp.mean(xf * xf, axis=-1, keepdims=True) + EPS)
    return (y * w.astype(jnp.float32)).astype(x.dtype)


def l2norm(x):
    return x * lax.rsqrt(jnp.sum(x * x, axis=-1, keepdims=True) + EPS)


def causal_dwconv(x, w):
    K = w.shape[0]
    S = x.shape[1]
    xp = jnp.pad(x, ((0, 0), (K - 1, 0), (0, 0)))
    out = xp[:, 0:S, :] * w[0]
    for i in range(1, K):
        out = out + xp[:, i:i + S, :] * w[i]
    return out


def gated_delta_rule(q, k, v, g, beta):
    B, S, H, Dk = q.shape
    Dv = v.shape[-1]
    C = GDN_CHUNK
    nc = S // C

    def chunk_vec(t):
        return t.reshape(B, nc, C, H, t.shape[-1]).transpose(1, 0, 3, 2, 4)

    def chunk_sc(t):
        return t.reshape(B, nc, C, H).transpose(1, 0, 3, 2)

    qc, kc, vc = chunk_vec(q), chunk_vec(k), chunk_vec(v)
    bc = chunk_sc(beta)
    gc = jnp.cumsum(chunk_sc(g), axis=-1)
    idx = jnp.arange(C)
    incl = idx[:, None] >= idx[None, :]
    strict = idx[:, None] > idx[None, :]
    diff = gc[..., :, None] - gc[..., None, :]
    dec_incl = jnp.where(incl, jnp.exp(jnp.where(incl, diff, 0.0)), 0.0)
    dec_strict = jnp.where(strict, dec_incl, 0.0)
    kk = jnp.einsum('nbhtd,nbhjd->nbhtj', kc, kc)
    lmat = dec_strict * kk * bc[..., None, :]
    gam = jnp.exp(gc)[..., None]
    rhs = jnp.concatenate([vc, gam * kc], axis=-1)
    sol = lax.linalg.triangular_solve(lmat, rhs, left_side=True, lower=True, unit_diagonal=True)
    u_v, w_k = sol[..., :Dv], sol[..., Dv:]
    attn = dec_incl * jnp.einsum('nbhtd,nbhjd->nbhtj', qc, kc) * bc[..., None, :]
    q_dec = gam * qc
    k_end = kc * (jnp.exp(gc[..., -1:] - gc) * bc)[..., None]
    g_end = jnp.exp(gc[..., -1])

    def step(state, xs):
        uv, wk, qd, at, ke, ge = xs
        u = uv - jnp.einsum('bhck,bhkv->bhcv', wk, state)
        o = jnp.einsum('bhck,bhkv->bhcv', qd, state) + jnp.einsum('bhcj,bhjv->bhcv', at, u)
        state = ge[..., None, None] * state + jnp.einsum('bhck,bhcv->bhkv', ke, u)
        return state, o

    s0 = jnp.zeros((B, H, Dk, Dv), jnp.float32)
    _, o = lax.scan(step, s0, (u_v, w_k, q_dec, attn, k_end, g_end))
    return o.transpose(1, 0, 3, 2, 4).reshape(B, S, H, Dv)


def band_attention(q, k, v, n_back):
    L, Dh = q.shape[-2], q.shape[-1]
    nb = -(-L // BAND_BLOCK)
    pad = nb * BAND_BLOCK - L
    padcfg = [(0, 0)] * (q.ndim - 2) + [(0, pad), (0, 0)]
    lead = q.shape[:-2]

    def blocks(t):
        return jnp.pad(t, padcfg).reshape(*lead, nb, BAND_BLOCK, Dh)

    qb, kb, vb = blocks(q), blocks(k), blocks(v)

    def with_prev(t):
        prev = jnp.concatenate([jnp.zeros_like(t[..., :1, :, :]), t[..., :-1, :, :]], axis=-3)
        return jnp.concatenate([prev, t], axis=-2)

    kk, vv = with_prev(kb), with_prev(vb)
    s = jnp.einsum('...nqd,...nkd->...nqk', qb, kk)
    blk = jnp.arange(nb)[:, None, None] * BAND_BLOCK
    qpos = blk + jnp.arange(BAND_BLOCK)[None, :, None]
    kpos = blk - BAND_BLOCK + jnp.arange(2 * BAND_BLOCK)[None, None, :]
    dist = qpos - kpos
    valid = (dist >= 0) & (dist <= n_back) & (kpos >= 0)
    s = jnp.where(valid, s, -jnp.inf)
    m = jnp.max(s, axis=-1)
    p = jnp.exp(s - m[..., None])
    den = jnp.sum(p, axis=-1)
    num = jnp.einsum('...nqk,...nkd->...nqd', p, vv)
    num = num.reshape(*lead, nb * BAND_BLOCK, Dh)[..., :L, :]
    den = den.reshape(*lead, nb * BAND_BLOCK)[..., :L]
    m = m.reshape(*lead, nb * BAND_BLOCK)[..., :L]
    return num, den, m


def dilated_attention(q, k, v):
    B, S, H, Dh = q.shape
    qf = q.astype(jnp.float32).transpose(0, 2, 1, 3) * (Dh ** -0.5)
    kf = k.astype(jnp.float32).transpose(0, 2, 1, 3)
    vf = v.astype(jnp.float32).transpose(0, 2, 1, 3)
    nums, dens, ms = [], [], []
    for window, dil in DIL_PATTERNS:
        L = S // dil

        def stride(t):
            return t.reshape(B, H, L, dil, Dh).transpose(0, 1, 3, 2, 4)

        num, den, m = band_attention(stride(qf), stride(kf), stride(vf), window // dil)
        nums.append(num.transpose(0, 1, 3, 2, 4).reshape(B, H, S, Dh))
        dens.append(den.transpose(0, 1, 3, 2).reshape(B, H, S))
        ms.append(m.transpose(0, 1, 3, 2).reshape(B, H, S))
    m_all = jnp.maximum(jnp.maximum(ms[0], ms[1]), ms[2])
    wts = [jnp.exp(mi - m_all) for mi in ms]
    num_tot = wts[0][..., None] * nums[0] + wts[1][..., None] * nums[1] + wts[2][..., None] * nums[2]
    den_tot = wts[0] * dens[0] + wts[1] * dens[1] + wts[2] * dens[2]
    out = num_tot / den_tot[..., None]
    return out.transpose(0, 2, 1, 3).reshape(B, S, H * Dh)


def hybrid_layer(x, norm1_w, w_in, conv_qkv_w, a_log, dt_bias, gdn_norm_w, w_out,
                 norm2_w, w_up, ffn_conv_w, w_down):
    B, S, _ = x.shape
    h = rmsnorm(x, norm1_w)
    proj = h @ w_in
    o1 = 3 * GDN_WIDTH
    o2 = o1 + GDN_WIDTH
    o3 = o2 + GDN_HEADS
    o4 = o3 + GDN_HEADS
    qkv_a, z_a, b_a, a_a, qkv_b = (proj[..., :o1], proj[..., o1:o2], proj[..., o2:o3],
                                   proj[..., o3:o4], proj[..., o4:])
    qkv_a = jax.nn.silu(causal_dwconv(qkv_a, conv_qkv_w)).astype(jnp.float32)
    qa = qkv_a[..., :GDN_WIDTH].reshape(B, S, GDN_HEADS, GDN_HEAD_DIM)
    ka = qkv_a[..., GDN_WIDTH:2 * GDN_WIDTH].reshape(B, S, GDN_HEADS, GDN_HEAD_DIM)
    va = qkv_a[..., 2 * GDN_WIDTH:].reshape(B, S, GDN_HEADS, GDN_HEAD_DIM)
    qa = l2norm(qa) * (GDN_HEAD_DIM ** -0.5)
    ka = l2norm(ka)
    beta = jax.nn.sigmoid(b_a.astype(jnp.float32))
    g = -jnp.exp(a_log.astype(jnp.float32)) * jax.nn.softplus(
        a_a.astype(jnp.float32) + dt_bias.astype(jnp.float32))
    o_a = gated_delta_rule(qa, ka, va, g, beta)
    z = z_a.astype(jnp.float32).reshape(B, S, GDN_HEADS, GDN_HEAD_DIM)
    o_a = (rmsnorm(o_a, gdn_norm_w) * jax.nn.silu(z)).reshape(B, S, GDN_WIDTH).astype(x.dtype)
    qb = qkv_b[..., :DIL_WIDTH].reshape(B, S, DIL_HEADS, DIL_HEAD_DIM)
    kb = qkv_b[..., DIL_WIDTH:2 * DIL_WIDTH].reshape(B, S, DIL_HEADS, DIL_HEAD_DIM)
    vb = qkv_b[..., 2 * DIL_WIDTH:].reshape(B, S, DIL_HEADS, DIL_HEAD_DIM)
    o_b = dilated_attention(qb, kb, vb).astype(x.dtype)
    x = x + jnp.concatenate([o_a, o_b], axis=-1) @ w_out
    h = rmsnorm(x, norm2_w)
    u = causal_dwconv(h @ w_up, ffn_conv_w)
    gate, up = u[..., :D_FF], u[..., D_FF:]
    x = x + (jax.nn.silu(gate) * up) @ w_down
    return x


def _fwd_setup_inputs(seed: int = 0) -> dict:
    key = jax.random.key(seed)
    ks = jax.random.split(key, 14)
    f32 = jnp.float32
    x = jax.random.normal(ks[0], (BATCH, SEQ, D_MODEL), f32)
    norm1_w = 1.0 + 0.02 * jax.random.normal(ks[1], (DEPTH, D_MODEL), f32)
    w_in = jax.random.normal(ks[2], (DEPTH, D_MODEL, IN_COLS), f32) * D_MODEL ** -0.5
    conv_qkv_w = jax.random.normal(ks[3], (DEPTH, GDN_CONV, 3 * GDN_WIDTH), f32) * GDN_CONV ** -0.5
    a_log = jnp.log(jax.random.uniform(ks[4], (DEPTH, GDN_HEADS), f32, 1.0, 16.0))
    dt = jnp.exp(jax.random.uniform(ks[5], (DEPTH, GDN_HEADS), f32, math.log(1e-3), math.log(1e-1)))
    dt_bias = dt + jnp.log(-jnp.expm1(-dt))
    gdn_norm_w = 1.0 + 0.02 * jax.random.normal(ks[6], (DEPTH, GDN_HEAD_DIM), f32)
    w_out = jax.random.normal(ks[7], (DEPTH, MIX_WIDTH, D_MODEL), f32) * MIX_WIDTH ** -0.5
    norm2_w = 1.0 + 0.02 * jax.random.normal(ks[8], (DEPTH, D_MODEL), f32)
    w_up = jax.random.normal(ks[9], (DEPTH, D_MODEL, 2 * D_FF), f32) * D_MODEL ** -0.5
    ffn_conv_w = jax.random.normal(ks[10], (DEPTH, FFN_CONV, 2 * D_FF), f32) * FFN_CONV ** -0.5
    w_down = jax.random.normal(ks[11], (DEPTH, D_FF, D_MODEL), f32) * D_FF ** -0.5
    final_norm_w = 1.0 + 0.02 * jax.random.normal(ks[12], (D_MODEL,), f32)
    return {"x": x, "norm1_w": norm1_w, "w_in": w_in, "conv_qkv_w": conv_qkv_w,
            "a_log": a_log, "dt_bias": dt_bias, "gdn_norm_w": gdn_norm_w, "w_out": w_out,
            "norm2_w": norm2_w, "w_up": w_up, "ffn_conv_w": ffn_conv_w, "w_down": w_down,
            "final_norm_w": final_norm_w}


def _fwd_reference(x, norm1_w, w_in, conv_qkv_w, a_log, dt_bias, gdn_norm_w, w_out,
              norm2_w, w_up, ffn_conv_w, w_down, final_norm_w):
    for l in range(DEPTH):
        x = hybrid_layer(x, norm1_w[l], w_in[l], conv_qkv_w[l], a_log[l], dt_bias[l],
                         gdn_norm_w[l], w_out[l], norm2_w[l], w_up[l], ffn_conv_w[l], w_down[l])
    return rmsnorm(x, final_norm_w)


import jax as _jax
import jax.numpy as _jnp

TWIN_FORMAT = 'train_step'
FWD_PARAMS = ['x', 'norm1_w', 'w_in', 'conv_qkv_w', 'a_log', 'dt_bias', 'gdn_norm_w', 'w_out', 'norm2_w', 'w_up', 'ffn_conv_w', 'w_down', 'final_norm_w']
TWIN_WEIGHTS = ['norm1_w', 'w_in', 'conv_qkv_w', 'a_log', 'dt_bias', 'gdn_norm_w', 'w_out', 'norm2_w', 'w_up', 'ffn_conv_w', 'w_down', 'final_norm_w']
TWIN_DIFF_INPUT = 'x'
TWIN_INPUTS = ['x', 'norm1_w', 'w_in', 'conv_qkv_w', 'a_log', 'dt_bias', 'gdn_norm_w', 'w_out', 'norm2_w', 'w_up', 'ffn_conv_w', 'w_down', 'final_norm_w', 'loss_target', 'm_norm1_w', 'm_w_in', 'm_conv_qkv_w', 'm_a_log', 'm_dt_bias', 'm_gdn_norm_w', 'm_w_out', 'm_norm2_w', 'm_w_up', 'm_ffn_conv_w', 'm_w_down', 'm_final_norm_w', 'v_norm1_w', 'v_w_in', 'v_conv_qkv_w', 'v_a_log', 'v_dt_bias', 'v_gdn_norm_w', 'v_w_out', 'v_norm2_w', 'v_w_up', 'v_ffn_conv_w', 'v_w_down', 'v_final_norm_w']
TWIN_OUTPUTS = ['loss', 'grad_x', 'grad_norm1_w', 'grad_w_in', 'grad_conv_qkv_w', 'grad_a_log', 'grad_dt_bias', 'grad_gdn_norm_w', 'grad_w_out', 'grad_norm2_w', 'grad_w_up', 'grad_ffn_conv_w', 'grad_w_down', 'grad_final_norm_w', 'delta_norm1_w', 'delta_w_in', 'delta_conv_qkv_w', 'delta_a_log', 'delta_dt_bias', 'delta_gdn_norm_w', 'delta_w_out', 'delta_norm2_w', 'delta_w_up', 'delta_ffn_conv_w', 'delta_w_down', 'delta_final_norm_w', 'new_m_norm1_w', 'new_m_w_in', 'new_m_conv_qkv_w', 'new_m_a_log', 'new_m_dt_bias', 'new_m_gdn_norm_w', 'new_m_w_out', 'new_m_norm2_w', 'new_m_w_up', 'new_m_ffn_conv_w', 'new_m_w_down', 'new_m_final_norm_w', 'new_v_norm1_w', 'new_v_w_in', 'new_v_conv_qkv_w', 'new_v_a_log', 'new_v_dt_bias', 'new_v_gdn_norm_w', 'new_v_w_out', 'new_v_norm2_w', 'new_v_w_up', 'new_v_ffn_conv_w', 'new_v_w_down', 'new_v_final_norm_w']
TWIN_LEAF_KINDS = {'loss': 'loss', 'grad_x': 'grad_x', 'grad_norm1_w': 'grad_w', 'grad_w_in': 'grad_w', 'grad_conv_qkv_w': 'grad_w', 'grad_a_log': 'grad_w', 'grad_dt_bias': 'grad_w', 'grad_gdn_norm_w': 'grad_w', 'grad_w_out': 'grad_w', 'grad_norm2_w': 'grad_w', 'grad_w_up': 'grad_w', 'grad_ffn_conv_w': 'grad_w', 'grad_w_down': 'grad_w', 'grad_final_norm_w': 'grad_w', 'delta_norm1_w': 'delta_w', 'delta_w_in': 'delta_w', 'delta_conv_qkv_w': 'delta_w', 'delta_a_log': 'delta_w', 'delta_dt_bias': 'delta_w', 'delta_gdn_norm_w': 'delta_w', 'delta_w_out': 'delta_w', 'delta_norm2_w': 'delta_w', 'delta_w_up': 'delta_w', 'delta_ffn_conv_w': 'delta_w', 'delta_w_down': 'delta_w', 'delta_final_norm_w': 'delta_w', 'new_m_norm1_w': 'new_m', 'new_m_w_in': 'new_m', 'new_m_conv_qkv_w': 'new_m', 'new_m_a_log': 'new_m', 'new_m_dt_bias': 'new_m', 'new_m_gdn_norm_w': 'new_m', 'new_m_w_out': 'new_m', 'new_m_norm2_w': 'new_m', 'new_m_w_up': 'new_m', 'new_m_ffn_conv_w': 'new_m', 'new_m_w_down': 'new_m', 'new_m_final_norm_w': 'new_m', 'new_v_norm1_w': 'new_v', 'new_v_w_in': 'new_v', 'new_v_conv_qkv_w': 'new_v', 'new_v_a_log': 'new_v', 'new_v_dt_bias': 'new_v', 'new_v_gdn_norm_w': 'new_v', 'new_v_w_out': 'new_v', 'new_v_norm2_w': 'new_v', 'new_v_w_up': 'new_v', 'new_v_ffn_conv_w': 'new_v', 'new_v_w_down': 'new_v', 'new_v_final_norm_w': 'new_v'}


def _forward(args):
    return _fwd_reference(*[args[k] for k in FWD_PARAMS])


def _output_shape():
    out = _jax.eval_shape(lambda: _forward(_fwd_setup_inputs(0)))
    return out.shape, out.dtype

N_MICROBATCH = 1
ADAM_LR = 0.001
ADAM_B1 = 0.9
ADAM_B2 = 0.999
ADAM_EPS = 1e-08
ADAM_WD = 0.01
ADAM_STEP = 10
PER_EXAMPLE_BATCH_AXIS = {'x': 0, 'loss_target': 0}
SHARED_INPUTS = []
_WEIGHT_DTYPES = {'norm1_w': _jnp.float32, 'w_in': _jnp.float32, 'conv_qkv_w': _jnp.float32, 'a_log': _jnp.float32, 'dt_bias': _jnp.float32, 'gdn_norm_w': _jnp.float32, 'w_out': _jnp.float32, 'norm2_w': _jnp.float32, 'w_up': _jnp.float32, 'ffn_conv_w': _jnp.float32, 'w_down': _jnp.float32, 'final_norm_w': _jnp.float32}
MOMENT_SCALE = {'norm1_w': 9.352926e-02, 'w_in': 4.994012e-02, 'conv_qkv_w': 5.882907e-02, 'a_log': 2.478705e-01, 'dt_bias': 2.327122e-01, 'gdn_norm_w': 1.620029e-01, 'w_out': 5.610397e-02, 'norm2_w': 8.982774e-02, 'w_up': 3.790295e-02, 'ffn_conv_w': 3.778220e-02, 'w_down': 6.175959e-02, 'final_norm_w': 1.601902e+01}


def _to_microbatches(a, axis):
    t = _jnp.moveaxis(a, axis, 0)
    t = t.reshape((N_MICROBATCH, t.shape[0] // N_MICROBATCH) + t.shape[1:])
    return _jnp.moveaxis(t, 1, axis + 1)


def setup_inputs(seed: int = 0) -> dict:
    inp = _fwd_setup_inputs(seed)
    key = _jax.random.fold_in(_jax.random.key(seed), 7919)
    shape, _ = _output_shape()
    out = dict(inp)
    out["loss_target"] = _jax.random.normal(_jax.random.fold_in(key, 0), shape, _jnp.float32)
    for i, name in enumerate(TWIN_WEIGHTS):
        w = inp[name].astype(_jnp.float32)
        if MOMENT_SCALE is None:
            s = _jnp.sqrt(_jnp.mean(_jnp.square(w)) + 1e-30)
        else:
            s = MOMENT_SCALE[name]
        km, kv = _jax.random.split(_jax.random.fold_in(key, i + 1))
        out[name] = w
        out["m_" + name] = s * _jax.random.normal(km, w.shape, _jnp.float32)
        out["v_" + name] = (s * s) * _jax.random.uniform(kv, w.shape, _jnp.float32, 0.5, 1.5)
    if N_MICROBATCH > 1:
        for name, axis in PER_EXAMPLE_BATCH_AXIS.items():
            out[name] = _to_microbatches(out[name], axis)
    return {'x': out['x'], 'norm1_w': out['norm1_w'], 'w_in': out['w_in'], 'conv_qkv_w': out['conv_qkv_w'], 'a_log': out['a_log'], 'dt_bias': out['dt_bias'], 'gdn_norm_w': out['gdn_norm_w'], 'w_out': out['w_out'], 'norm2_w': out['norm2_w'], 'w_up': out['w_up'], 'ffn_conv_w': out['ffn_conv_w'], 'w_down': out['w_down'], 'final_norm_w': out['final_norm_w'], 'loss_target': out['loss_target'], 'm_norm1_w': out['m_norm1_w'], 'm_w_in': out['m_w_in'], 'm_conv_qkv_w': out['m_conv_qkv_w'], 'm_a_log': out['m_a_log'], 'm_dt_bias': out['m_dt_bias'], 'm_gdn_norm_w': out['m_gdn_norm_w'], 'm_w_out': out['m_w_out'], 'm_norm2_w': out['m_norm2_w'], 'm_w_up': out['m_w_up'], 'm_ffn_conv_w': out['m_ffn_conv_w'], 'm_w_down': out['m_w_down'], 'm_final_norm_w': out['m_final_norm_w'], 'v_norm1_w': out['v_norm1_w'], 'v_w_in': out['v_w_in'], 'v_conv_qkv_w': out['v_conv_qkv_w'], 'v_a_log': out['v_a_log'], 'v_dt_bias': out['v_dt_bias'], 'v_gdn_norm_w': out['v_gdn_norm_w'], 'v_w_out': out['v_w_out'], 'v_norm2_w': out['v_norm2_w'], 'v_w_up': out['v_w_up'], 'v_ffn_conv_w': out['v_ffn_conv_w'], 'v_w_down': out['v_w_down'], 'v_final_norm_w': out['v_final_norm_w']}


def _loss(weights, diff, rest, loss_target):
    with _jax.named_scope("forward"):
        args = {**rest, TWIN_DIFF_INPUT: diff, **{k: w.astype(_WEIGHT_DTYPES[k]) for k, w in weights.items()}}
        y = _forward(args)
    with _jax.named_scope("loss_head"):
        err = _jnp.square(y.astype(_jnp.float32) - loss_target)
        return 0.5 * _jnp.sum(_jnp.mean(err, axis=-1)) if err.ndim else 0.5 * err


def _adamw(w, g, m, v):
    m = ADAM_B1 * m + (1.0 - ADAM_B1) * g
    v = ADAM_B2 * v + (1.0 - ADAM_B2) * _jnp.square(g)
    m_hat = m / (1.0 - ADAM_B1 ** ADAM_STEP)
    v_hat = v / (1.0 - ADAM_B2 ** ADAM_STEP)
    delta = -ADAM_LR * (m_hat / (_jnp.sqrt(v_hat) + ADAM_EPS) + ADAM_WD * w)
    return delta, m, v


def reference(x, norm1_w, w_in, conv_qkv_w, a_log, dt_bias, gdn_norm_w, w_out, norm2_w, w_up, ffn_conv_w, w_down, final_norm_w, loss_target, m_norm1_w, m_w_in, m_conv_qkv_w, m_a_log, m_dt_bias, m_gdn_norm_w, m_w_out, m_norm2_w, m_w_up, m_ffn_conv_w, m_w_down, m_final_norm_w, v_norm1_w, v_w_in, v_conv_qkv_w, v_a_log, v_dt_bias, v_gdn_norm_w, v_w_out, v_norm2_w, v_w_up, v_ffn_conv_w, v_w_down, v_final_norm_w):
    given = dict(x=x, norm1_w=norm1_w, w_in=w_in, conv_qkv_w=conv_qkv_w, a_log=a_log, dt_bias=dt_bias, gdn_norm_w=gdn_norm_w, w_out=w_out, norm2_w=norm2_w, w_up=w_up, ffn_conv_w=ffn_conv_w, w_down=w_down, final_norm_w=final_norm_w, loss_target=loss_target, m_norm1_w=m_norm1_w, m_w_in=m_w_in, m_conv_qkv_w=m_conv_qkv_w, m_a_log=m_a_log, m_dt_bias=m_dt_bias, m_gdn_norm_w=m_gdn_norm_w, m_w_out=m_w_out, m_norm2_w=m_norm2_w, m_w_up=m_w_up, m_ffn_conv_w=m_ffn_conv_w, m_w_down=m_w_down, m_final_norm_w=m_final_norm_w, v_norm1_w=v_norm1_w, v_w_in=v_w_in, v_conv_qkv_w=v_conv_qkv_w, v_a_log=v_a_log, v_dt_bias=v_dt_bias, v_gdn_norm_w=v_gdn_norm_w, v_w_out=v_w_out, v_norm2_w=v_norm2_w, v_w_up=v_w_up, v_ffn_conv_w=v_ffn_conv_w, v_w_down=v_w_down, v_final_norm_w=v_final_norm_w)
    weights = {n: given[n] for n in TWIN_WEIGHTS}
    shared = {n: given[n] for n in SHARED_INPUTS}
    per_example = {n: given[n] for n in ['x']}
    grad_fn = _jax.value_and_grad(_loss, argnums=(0, 1))

    def one_microbatch(ex, loss_target):
        ex = dict(ex)
        diff = ex.pop(TWIN_DIFF_INPUT)
        return grad_fn(weights, diff, {**shared, **ex}, loss_target)

    if N_MICROBATCH == 1:
        loss, (grad_w, grad_x) = one_microbatch(per_example, given["loss_target"])
    else:
        def body(carry, xs):
            loss_sum, grad_sum = carry
            l_k, (gw_k, gx_k) = one_microbatch(xs[0], xs[1])
            with _jax.named_scope("update"):
                return (loss_sum + l_k, _jax.tree.map(_jnp.add, grad_sum, gw_k)), gx_k

        init = (_jnp.zeros((), _jnp.float32), _jax.tree.map(_jnp.zeros_like, weights))
        (loss, grad_w), grad_x = _jax.lax.scan(body, init, (per_example, given["loss_target"]))
    with _jax.named_scope("update"):
        delta_w, new_m, new_v = {}, {}, {}
        for n in TWIN_WEIGHTS:
            delta_w[n], new_m[n], new_v[n] = _adamw(weights[n], grad_w[n], given["m_" + n], given["v_" + n])
    return (loss, grad_x, *[grad_w[n] for n in TWIN_WEIGHTS], *[delta_w[n] for n in TWIN_WEIGHTS],
            *[new_m[n] for n in TWIN_WEIGHTS], *[new_v[n] for n in TWIN_WEIGHTS])
```

```python
import functools

import jax
import jax.numpy as jnp
from jax import lax
from jax.experimental import pallas as pl
from jax.experimental.pallas import tpu as pltpu

F32 = jnp.float32
BF16 = jnp.bfloat16

N_DEV = 8
D_MODEL = 1024
GDN_HEADS = 4
GDN_DIM = 128
GDN_WIDTH = GDN_HEADS * GDN_DIM
GDN_CONV = 4
CHUNK = 64
DIL_HEADS = 8
DIL_DIM = 64
DIL_WIDTH = DIL_HEADS * DIL_DIM
DIL_PAIRS = DIL_HEADS // 2
DILATIONS = (1, 4, 16)
BAND = 128
D_FF = 2816
FFN_CONV = 3
EPS = 1e-6
A_COLS = 3 * GDN_WIDTH + 128
HALO = 8

ADAM_LR = 0.001
ADAM_B1 = 0.9
ADAM_B2 = 0.999
ADAM_EPS = 1e-08
ADAM_WD = 0.01
ADAM_STEP = 10

VMEM_LIMIT_BYTES = 56 * 1024 * 1024
NEG_BIG = -1e30


def _params(sem=None):
    return pltpu.CompilerParams(dimension_semantics=sem, vmem_limit_bytes=VMEM_LIMIT_BYTES)


def _sds(shape, dtype=F32):
    return jax.ShapeDtypeStruct(shape, dtype)


def _bdot(a, b):
    return jnp.dot(a.astype(BF16), b.astype(BF16), preferred_element_type=F32)


def _bdot_nt(a, b):
    return lax.dot_general(a.astype(BF16), b.astype(BF16), (((1,), (1,)), ((), ())), preferred_element_type=F32)


def _bdot_tn(a, b):
    return lax.dot_general(a.astype(BF16), b.astype(BF16), (((0,), (0,)), ((), ())), preferred_element_type=F32)


def _split(a):
    hi = a.astype(BF16)
    lo = (a - hi.astype(F32)).astype(BF16)
    return hi, lo


def _dot3(a, b, dims):
    ah, al = _split(a)
    bh, bl = _split(b)
    d = functools.partial(lax.dot_general, dimension_numbers=(dims, ((), ())), preferred_element_type=F32)
    return d(ah, bh) + (d(al, bh) + d(ah, bl))


def _exact_tri_dot(tri, g):
    g1 = g.astype(BF16)
    r1 = g - g1.astype(F32)
    g2 = r1.astype(BF16)
    g3 = (r1 - g2.astype(F32)).astype(BF16)
    t = tri.astype(BF16)
    d = functools.partial(jnp.dot, preferred_element_type=F32)
    return d(t, g1) + (d(t, g2) + d(t, g3))


def _sigmoid(x):
    return 1.0 / (1.0 + jnp.exp(-x))


def _dsilu(x, sg):
    return sg * (1.0 + x * (1.0 - sg))


def _mm(a, b, *, name, ta=False, tb=False, res=None, out_dtype=F32, tm=512, tn=512, tk=512):
    if ta:
        K, M = a.shape
    else:
        M, K = a.shape
    if tb:
        N, Kb = b.shape
    else:
        Kb, N = b.shape
    assert K == Kb, (a.shape, b.shape)
    tm, tn, tk = min(tm, M), min(tn, N), min(tk, K)
    assert M % tm == 0 and N % tn == 0 and K % tk == 0, (name, M, N, K, tm, tn, tk)
    nk = K // tk
    dims = (((0 if ta else 1,), (1 if tb else 0,)), ((), ()))
    has_res = res is not None

    def body(*refs):
        if has_res:
            a_ref, b_ref, r_ref, o_ref, acc_ref = refs
        else:
            a_ref, b_ref, o_ref, acc_ref = refs
        k = pl.program_id(2)
        part = lax.dot_general(a_ref[...].astype(BF16), b_ref[...].astype(BF16), dims, preferred_element_type=F32)

        @pl.when(k == 0)
        def _():
            acc_ref[...] = part

        @pl.when(k > 0)
        def _():
            acc_ref[...] += part

        @pl.when(k == nk - 1)
        def _():
            r = acc_ref[...]
            if has_res:
                r = r + r_ref[...]
            o_ref[...] = r.astype(out_dtype)

    a_spec = pl.BlockSpec((tk, tm), lambda i, j, k: (k, i)) if ta else pl.BlockSpec((tm, tk), lambda i, j, k: (i, k))
    b_spec = pl.BlockSpec((tn, tk), lambda i, j, k: (j, k)) if tb else pl.BlockSpec((tk, tn), lambda i, j, k: (k, j))
    o_spec = pl.BlockSpec((tm, tn), lambda i, j, k: (i, j))
    in_specs = [a_spec, b_spec] + ([o_spec] if has_res else [])
    args = (a, b) + ((res,) if has_res else ())
    return pl.pallas_call(
        body, name=name, grid=(M // tm, N // tn, nk), in_specs=in_specs, out_specs=o_spec,
        out_shape=_sds((M, N), out_dtype), scratch_shapes=[pltpu.VMEM((tm, tn), F32)],
        compiler_params=_params(("parallel", "parallel", "arbitrary")),
    )(*args)


def _rms_fwd(x, w, *, name, tm=256):
    S, D = x.shape

    def body(x_ref, w_ref, h_ref):
        xv = x_ref[...]
        r = lax.rsqrt(jnp.mean(xv * xv, axis=-1, keepdims=True) + EPS)
        h_ref[...] = (xv * r * w_ref[...]).astype(BF16)

    return pl.pallas_call(
        body, name=name, grid=(S // tm,),
        in_specs=[pl.BlockSpec((tm, D), lambda i: (i, 0)), pl.BlockSpec((1, D), lambda i: (0, 0))],
        out_specs=pl.BlockSpec((tm, D), lambda i: (i, 0)), out_shape=_sds((S, D), BF16),
        compiler_params=_params(("parallel",)),
    )(x, w)


def _rms_bwd(dh, x, w, res, *, name, tm=256):
    S, D = x.shape

    def body(dh_ref, x_ref, w_ref, res_ref, dx_ref, dw_ref):
        i = pl.program_id(0)
        xv = x_ref[...]
        g = dh_ref[...]
        r = lax.rsqrt(jnp.mean(xv * xv, axis=-1, keepdims=True) + EPS)
        xh = xv * r
        gw = g * w_ref[...]
        dx_ref[...] = res_ref[...] + r * (gw - xh * jnp.mean(gw * xh, axis=-1, keepdims=True))
        part = jnp.sum(g * xh, axis=0, keepdims=True)

        @pl.when(i == 0)
        def _():
            dw_ref[...] = part

        @pl.when(i > 0)
        def _():
            dw_ref[...] += part

    row = pl.BlockSpec((tm, D), lambda i: (i, 0))
    one = pl.BlockSpec((1, D), lambda i: (0, 0))
    return pl.pallas_call(
        body, name=name, grid=(S // tm,), in_specs=[row, row, one, row], out_specs=[row, one],
        out_shape=[_sds((S, D)), _sds((1, D))], compiler_params=_params(("arbitrary",)),
    )(dh, x, w, res)


def _loss_head(x2, w, tgt, *, name, tm=256):
    S, D = x2.shape

    def body(x_ref, w_ref, t_ref, dx_ref, dw_ref, loss_ref):
        i = pl.program_id(0)
        xv = x_ref[...]
        wv = w_ref[...]
        r = lax.rsqrt(jnp.mean(xv * xv, axis=-1, keepdims=True) + EPS)
        xh = xv * r
        err = xh * wv - t_ref[...]
        lrow = jnp.sum(err * err, axis=-1, keepdims=True)
        lsum = jnp.sum(lrow, axis=0, keepdims=True) * (0.5 / D)
        g = err * (1.0 / D)
        gw = g * wv
        dx_ref[...] = r * (gw - xh * jnp.mean(gw * xh, axis=-1, keepdims=True))
        part = jnp.sum(g * xh, axis=0, keepdims=True)
        lpart = jnp.broadcast_to(lsum, (1, 128))

        @pl.when(i == 0)
        def _():
            dw_ref[...] = part
            loss_ref[...] = lpart

        @pl.when(i > 0)
        def _():
            dw_ref[...] += part
            loss_ref[...] += lpart

    row = pl.BlockSpec((tm, D), lambda i: (i, 0))
    one = pl.BlockSpec((1, D), lambda i: (0, 0))
    return pl.pallas_call(
        body, name=name, grid=(S // tm,), in_specs=[row, one, row],
        out_specs=[row, one, pl.BlockSpec((1, 128), lambda i: (0, 0))],
        out_shape=[_sds((S, D)), _sds((1, D)), _sds((1, 128))], compiler_params=_params(("arbitrary",)),
    )(x2, w, tgt)


def _conv_rows(prev, cur, w, taps):
    n = cur.shape[0]
    xs = jnp.concatenate([prev, cur], axis=0)
    base = HALO - (taps - 1)
    out = xs[base:base + n] * w[0:1]
    for i in range(1, taps):
        out = out + xs[base + i:base + i + n] * w[i:i + 1]
    return out


def _conv_rows_bwd(cur_d, next_d, prev_x, cur_x, w, taps):
    n = cur_d.shape[0]
    ds = jnp.concatenate([cur_d, next_d], axis=0)
    dx = ds[taps - 1:taps - 1 + n] * w[0:1]
    for i in range(1, taps):
        dx = dx + ds[taps - 1 - i:taps - 1 - i + n] * w[i:i + 1]
    xs = jnp.concatenate([prev_x, cur_x], axis=0)
    base = HALO - (taps - 1)
    dws = [jnp.sum(cur_d * xs[base + i:base + i + n], axis=0, keepdims=True) for i in range(taps)]
    return dx, jnp.concatenate(dws, axis=0)


def _halo_specs(tm, width, col, nblk):
    per = tm // HALO
    prev = pl.BlockSpec((HALO, width), lambda i, *_: (jnp.maximum(i * per - 1, 0), col))
    nxt = pl.BlockSpec((HALO, width), lambda i, *_: (jnp.minimum((i + 1) * per, nblk * per - 1), col))
    return prev, nxt


def _softplus(x):
    return jnp.maximum(x, 0.0) + jnp.log1p(jnp.exp(-jnp.abs(x)))


def _chunk_tri(tm, upper=False):
    r = lax.broadcasted_iota(jnp.int32, (tm, tm), 0)
    c = lax.broadcasted_iota(jnp.int32, (tm, tm), 1)
    same = lax.div(r, CHUNK) == lax.div(c, CHUNK)
    order = (c >= r) if upper else (c <= r)
    return jnp.where(same & order, 1.0, 0.0)


def _gdn_prep_fwd(proj_a, conv_w, a_log, dt_bias, *, name, tm=256):
    S = proj_a.shape[0]
    nblk = S // tm
    W3 = 3 * GDN_WIDTH

    def body(cur_ref, prev_ref, ba_ref, cw_ref, al_ref, dt_ref, qn_ref, kn_ref, v_ref, gcb_ref, bb_ref):
        i = pl.program_id(0)
        prev = jnp.where(i > 0, prev_ref[...], 0.0)
        c = _conv_rows(prev, cur_ref[...], cw_ref[...], GDN_CONV)
        a = c * _sigmoid(c)
        ba = ba_ref[...]
        lane = lax.broadcasted_iota(jnp.int32, (tm, 128), 1)
        g4 = jnp.zeros((tm, 128), F32)
        for h in range(GDN_HEADS):
            sl = slice(GDN_DIM * h, GDN_DIM * (h + 1))
            qh = a[:, GDN_DIM * h:GDN_DIM * (h + 1)]
            kh = a[:, GDN_WIDTH + GDN_DIM * h:GDN_WIDTH + GDN_DIM * (h + 1)]
            qn_ref[:, sl] = qh * (lax.rsqrt(jnp.sum(qh * qh, axis=-1, keepdims=True) + EPS) * (GDN_DIM ** -0.5))
            kn_ref[:, sl] = kh * lax.rsqrt(jnp.sum(kh * kh, axis=-1, keepdims=True) + EPS)
            beta = _sigmoid(ba[:, h:h + 1])
            bb_ref[:, sl] = jnp.broadcast_to(beta, (tm, GDN_DIM))
            g = -jnp.exp(al_ref[0:1, h:h + 1]) * _softplus(ba[:, GDN_HEADS + h:GDN_HEADS + h + 1] + dt_ref[0:1, h:h + 1])
            g4 = jnp.where(lane == h, g, g4)
        v_ref[...] = a[:, 2 * GDN_WIDTH:]
        gc = _exact_tri_dot(_chunk_tri(tm), g4)
        for h in range(GDN_HEADS):
            gcb_ref[:, GDN_DIM * h:GDN_DIM * (h + 1)] = jnp.broadcast_to(gc[:, h:h + 1], (tm, GDN_DIM))

    prev_spec, _ = _halo_specs(tm, W3, 0, nblk)
    row = pl.BlockSpec((tm, GDN_WIDTH), lambda i: (i, 0))
    small = lambda a: pl.BlockSpec(a.shape, lambda i: (0, 0))
    return pl.pallas_call(
        body, name=name, grid=(nblk,),
        in_specs=[pl.BlockSpec((tm, W3), lambda i: (i, 0)), prev_spec,
                  pl.BlockSpec((tm, 128), lambda i: (i, W3 // 128)), small(conv_w), small(a_log), small(dt_bias)],
        out_specs=[row] * 5, out_shape=[_sds((S, GDN_WIDTH))] * 5, compiler_params=_params(("parallel",)),
    )(proj_a, proj_a, proj_a, conv_w, a_log, dt_bias)


def _chunk_masks():
    r = lax.broadcasted_iota(jnp.int32, (CHUNK, CHUNK), 0)
    c = lax.broadcasted_iota(jnp.int32, (CHUNK, CHUNK), 1)
    return r >= c, r > c, r == c


def _chunk_decay(gcb_h, bb_h, incl):
    G = gcb_h[:, 0:CHUNK]
    diff = G - G.T
    dec = jnp.where(incl, jnp.exp(jnp.where(incl, diff, 0.0)), 0.0)
    return dec, bb_h[:, 0:CHUNK].T


def _gdn_chunk_fwd(qn, kn, v, gcb, bb, *, name):
    S = qn.shape[0]
    nc = S // CHUNK

    def body(qn_ref, kn_ref, v_ref, gcb_ref, bb_ref, uv_ref, wk_ref, at_ref, t_ref):
        incl, strict, diag = _chunk_masks()
        ats, ts = [], []
        for h in range(GDN_HEADS):
            sl = slice(GDN_DIM * h, GDN_DIM * (h + 1))
            q, k, vv, gh = qn_ref[:, sl], kn_ref[:, sl], v_ref[:, sl], gcb_ref[:, sl]
            dec, bt = _chunk_decay(gh, bb_ref[:, sl], incl)
            lmat = jnp.where(strict, dec * _bdot_nt(k, k) * bt, 0.0)
            p = -lmat
            t = jnp.where(diag, 1.0, 0.0) + p
            for _ in range(5):
                p = _bdot(p, p)
                t = t + _bdot(t, p)
            rhs = jnp.concatenate([vv, jnp.exp(gh) * k], axis=1)
            sol = _dot3(t, rhs, ((1,), (0,)))
            uv_ref[:, sl] = sol[:, :GDN_DIM]
            wk_ref[:, sl] = sol[:, GDN_DIM:]
            ats.append(dec * _bdot_nt(q, k) * bt)
            ts.append(t)
        at_ref[...] = jnp.concatenate(ats, axis=1)
        t_ref[...] = jnp.concatenate(ts, axis=1)

    row = pl.BlockSpec((CHUNK, GDN_WIDTH), lambda n: (n, 0))
    sq = pl.BlockSpec((CHUNK, GDN_HEADS * CHUNK), lambda n: (n, 0))
    return pl.pallas_call(
        body, name=name, grid=(nc,), in_specs=[row] * 5, out_specs=[row, row, sq, sq],
        out_shape=[_sds((S, GDN_WIDTH)), _sds((S, GDN_WIDTH)), _sds((S, GDN_HEADS * CHUNK)), _sds((S, GDN_HEADS * CHUNK))],
        compiler_params=_params(("parallel",)),
    )(qn, kn, v, gcb, bb)


def _gdn_scan_fwd(uv, wk, at, qn, kn, gcb, bb, proj_z, gnw, *, name):
    S = uv.shape[0]
    nc = S // CHUNK

    def body(uv_ref, wk_ref, at_ref, qn_ref, kn_ref, gcb_ref, bb_ref, z_ref, gnw_ref, o_ref, u_ref, sp_ref, oa_ref, st_ref):
        n = pl.program_id(0)

        @pl.when(n == 0)
        def _():
            st_ref[...] = jnp.zeros_like(st_ref)

        oas = []
        for h in range(GDN_HEADS):
            sl = slice(GDN_DIM * h, GDN_DIM * (h + 1))
            st = st_ref[h]
            sp_ref[sl, :] = st
            gh = gcb_ref[:, sl]
            glast = gcb_ref[CHUNK - 1:CHUNK, sl]
            u = uv_ref[:, sl] - _bdot(wk_ref[:, sl], st)
            o = _bdot(qn_ref[:, sl] * jnp.exp(gh), st) + _bdot(at_ref[:, CHUNK * h:CHUNK * (h + 1)], u)
            ke = kn_ref[:, sl] * jnp.exp(glast - gh) * bb_ref[:, sl]
            st_ref[h] = jnp.exp(glast) * st + _bdot_tn(ke, u)
            u_ref[:, sl] = u
            o_ref[:, sl] = o
            z = z_ref[:, sl]
            r = lax.rsqrt(jnp.mean(o * o, axis=-1, keepdims=True) + EPS)
            oas.append(o * r * gnw_ref[...] * (z * _sigmoid(z)))
        oa_ref[...] = jnp.concatenate(oas, axis=1).astype(BF16)

    row = pl.BlockSpec((CHUNK, GDN_WIDTH), lambda n: (n, 0))
    sq = pl.BlockSpec((CHUNK, GDN_HEADS * CHUNK), lambda n: (n, 0))
    return pl.pallas_call(
        body, name=name, grid=(nc,),
        in_specs=[row, row, sq, row, row, row, row, row, pl.BlockSpec((1, GDN_DIM), lambda n: (0, 0))],
        out_specs=[row, row, pl.BlockSpec((GDN_WIDTH, GDN_DIM), lambda n: (n, 0)), row],
        out_shape=[_sds((S, GDN_WIDTH)), _sds((S, GDN_WIDTH)), _sds((nc * GDN_WIDTH, GDN_DIM)), _sds((S, 2 * GDN_WIDTH), BF16)],
        scratch_shapes=[pltpu.VMEM((GDN_HEADS, GDN_DIM, GDN_DIM), F32)],
        compiler_params=_params(("arbitrary",)),
    )(uv, wk, at, qn, kn, gcb, bb, proj_z, gnw)


def _gdn_scan_bwd(d_oab, o, proj_z, gnw, sp, u, wk, at, qn, kn, gcb, bb, *, name):
    S = o.shape[0]
    nc = S // CHUNK

    def body(do_ref, o_ref, z_ref, gnw_ref, sp_ref, u_ref, wk_ref, at_ref, qn_ref, kn_ref, gcb_ref, bb_ref,
             dz_ref, dgn_ref, du_ref, dwk_ref, dat_ref, dqd_ref, dke_ref, dgl_ref, ds_ref):
        n = pl.program_id(0)

        @pl.when(n == 0)
        def _():
            ds_ref[...] = jnp.zeros_like(ds_ref)
            dgn_ref[...] = jnp.zeros_like(dgn_ref)

        dats, dgn = [], jnp.zeros((1, GDN_DIM), F32)
        for h in range(GDN_HEADS):
            sl = slice(GDN_DIM * h, GDN_DIM * (h + 1))
            oo = o_ref[:, sl]
            z = z_ref[:, sl]
            gw = gnw_ref[...]
            sg = _sigmoid(z)
            r = lax.rsqrt(jnp.mean(oo * oo, axis=-1, keepdims=True) + EPS)
            xh = oo * r
            d_oa = do_ref[:, sl]
            dy = d_oa * (z * sg)
            dz_ref[:, sl] = (d_oa * (xh * gw) * _dsilu(z, sg)).astype(BF16)
            dgn = dgn + jnp.sum(dy * xh, axis=0, keepdims=True)
            dxh = dy * gw
            do = r * (dxh - xh * jnp.mean(dxh * xh, axis=-1, keepdims=True))

            st = sp_ref[sl, :]
            dst = ds_ref[h]
            gh = gcb_ref[:, sl]
            glast = gcb_ref[CHUNK - 1:CHUNK, sl]
            uu = u_ref[:, sl]
            ath = at_ref[:, CHUNK * h:CHUNK * (h + 1)]
            qd = qn_ref[:, sl] * jnp.exp(gh)
            ke = kn_ref[:, sl] * jnp.exp(glast - gh) * bb_ref[:, sl]
            ge = jnp.exp(glast)
            dqd_ref[:, sl] = _bdot_nt(do, st)
            dats.append(_bdot_nt(do, uu))
            du = _bdot_tn(ath, do) + _bdot(ke, dst)
            dke_ref[:, sl] = _bdot_nt(uu, dst)
            dge = jnp.sum(jnp.sum(dst * st, axis=1, keepdims=True), axis=0, keepdims=True)
            dgl_ref[0, :, sl] = jnp.broadcast_to(dge * ge, (8, GDN_DIM))
            ds_ref[h] = _bdot_tn(qd, do) + ge * dst - _bdot_tn(wk_ref[:, sl], du)
            du_ref[:, sl] = du
            dwk_ref[:, sl] = -_bdot_nt(du, st)
        dat_ref[...] = jnp.concatenate(dats, axis=1)
        dgn_ref[...] += dgn

    rev = lambda n: (nc - 1 - n, 0)
    row = pl.BlockSpec((CHUNK, GDN_WIDTH), rev)
    sq = pl.BlockSpec((CHUNK, GDN_HEADS * CHUNK), rev)
    one = pl.BlockSpec((1, GDN_DIM), lambda n: (0, 0))
    return pl.pallas_call(
        body, name=name, grid=(nc,),
        in_specs=[row, row, row, one, pl.BlockSpec((GDN_WIDTH, GDN_DIM), rev), row, row, sq, row, row, row, row],
        out_specs=[row, one, row, row, sq, row, row, pl.BlockSpec((1, 8, GDN_WIDTH), lambda n: (nc - 1 - n, 0, 0))],
        out_shape=[_sds((S, GDN_WIDTH), BF16), _sds((1, GDN_DIM)), _sds((S, GDN_WIDTH)), _sds((S, GDN_WIDTH)),
                   _sds((S, GDN_HEADS * CHUNK)), _sds((S, GDN_WIDTH)), _sds((S, GDN_WIDTH)), _sds((nc, 8, GDN_WIDTH))],
        scratch_shapes=[pltpu.VMEM((GDN_HEADS, GDN_DIM, GDN_DIM), F32)],
        compiler_params=_params(("arbitrary",)),
    )(d_oab, o, proj_z, gnw, sp, u, wk, at, qn, kn, gcb, bb)


def _gdn_chunk_bwd(qn, kn, v, gcb, bb, tmat, uv, wk, du, dwk, dat, dqd, dke, dgl, *, name):
    S = qn.shape[0]
    nc = S // CHUNK

    def body(qn_ref, kn_ref, v_ref, gcb_ref, bb_ref, t_ref, uv_ref, wk_ref, du_ref, dwk_ref, dat_ref, dqd_ref, dke_ref,
             dgl_ref, dq_ref, dk_ref, dv_ref, dg_ref, dbeta_ref):
        incl, strict, _ = _chunk_masks()
        lane = lax.broadcasted_iota(jnp.int32, (CHUNK, 128), 1)
        rowi = lax.broadcasted_iota(jnp.int32, (CHUNK, 1), 0)
        dgc4 = jnp.zeros((CHUNK, 128), F32)
        db4 = jnp.zeros((CHUNK, 128), F32)
        for h in range(GDN_HEADS):
            sl = slice(GDN_DIM * h, GDN_DIM * (h + 1))
            sq = slice(CHUNK * h, CHUNK * (h + 1))
            q, k, gh, bh = qn_ref[:, sl], kn_ref[:, sl], gcb_ref[:, sl], bb_ref[:, sl]
            dec, bt = _chunk_decay(gh, bh, incl)
            kk = _bdot_nt(k, k)
            qk = _bdot_nt(q, k)
            t = t_ref[:, sq]
            d_sol = jnp.concatenate([du_ref[:, sl], dwk_ref[:, sl]], axis=1)
            d_rhs = _dot3(t, d_sol, ((0,), (0,)))
            sol = jnp.concatenate([uv_ref[:, sl], wk_ref[:, sl]], axis=1)
            d_l = jnp.where(strict, -_dot3(d_rhs, sol, ((1,), (1,))), 0.0)
            d_a = jnp.where(incl, dat_ref[:, sq], 0.0)
            gam = jnp.exp(gh)
            glast = gcb_ref[CHUNK - 1:CHUNK, sl]
            e = jnp.exp(glast - gh)
            d_gk = d_rhs[:, GDN_DIM:]
            dqd = dqd_ref[:, sl]
            dke = dke_ref[:, sl]
            ml = d_l * dec * bt
            ma = d_a * dec * bt
            dq_ref[:, sl] = _bdot(ma, k) + dqd * gam
            dk_ref[:, sl] = (_bdot(ml, k) + _bdot_tn(ml, k) + _bdot_tn(ma, q)) + d_gk * gam + dke * (e * bh)
            dv_ref[:, sl] = d_rhs[:, :GDN_DIM]
            wb = d_l * dec * kk + d_a * dec * qk
            ew = wb * bt
            s_ke = jnp.sum(dke * k * (e * bh), axis=-1, keepdims=True)
            dbeta = jnp.sum(wb.T, axis=-1, keepdims=True) + jnp.sum(dke * k * e, axis=-1, keepdims=True)
            dgc = (jnp.sum(ew, axis=-1, keepdims=True) - jnp.sum(ew.T, axis=-1, keepdims=True)
                   + jnp.sum(dqd * q * gam, axis=-1, keepdims=True) + jnp.sum(d_gk * k * gam, axis=-1, keepdims=True) - s_ke)
            tail = jnp.sum(s_ke, axis=0, keepdims=True) + dgl_ref[0, 0:1, GDN_DIM * h:GDN_DIM * h + 1]
            dgc = dgc + jnp.where(rowi == CHUNK - 1, tail, 0.0)
            dgc4 = jnp.where(lane == h, dgc, dgc4)
            db4 = jnp.where(lane == h, dbeta, db4)
        dg_ref[...] = _exact_tri_dot(_chunk_tri(CHUNK, upper=True), dgc4)
        dbeta_ref[...] = db4

    row = pl.BlockSpec((CHUNK, GDN_WIDTH), lambda n: (n, 0))
    sq = pl.BlockSpec((CHUNK, GDN_HEADS * CHUNK), lambda n: (n, 0))
    col = pl.BlockSpec((CHUNK, 128), lambda n: (n, 0))
    return pl.pallas_call(
        body, name=name, grid=(nc,),
        in_specs=[row] * 5 + [sq, row, row, row, row, sq, row, row, pl.BlockSpec((1, 8, GDN_WIDTH), lambda n: (n, 0, 0))],
        out_specs=[row, row, row, col, col],
        out_shape=[_sds((S, GDN_WIDTH))] * 3 + [_sds((S, 128))] * 2, compiler_params=_params(("parallel",)),
    )(qn, kn, v, gcb, bb, tmat, uv, wk, du, dwk, dat, dqd, dke, dgl)


def _gdn_prep_bwd(dqn, dkn, dv, dg, dbeta, proj_a, conv_w, a_log, dt_bias, *, name, tm=256):
    S = proj_a.shape[0]
    nblk = S // tm
    W3 = 3 * GDN_WIDTH

    def body(dqn_ref, dkn_ref, dv_ref, dg_ref, dbeta_ref, cur_ref, prev_ref, ba_ref, cw_ref, al_ref, dt_ref,
             dc_ref, dba_ref, sm_ref):
        i = pl.program_id(0)
        prev = jnp.where(i > 0, prev_ref[...], 0.0)
        c = _conv_rows(prev, cur_ref[...], cw_ref[...], GDN_CONV)
        sg = _sigmoid(c)
        a = c * sg
        dsl = _dsilu(c, sg)
        ba = ba_ref[...]
        lane = lax.broadcasted_iota(jnp.int32, (tm, 128), 1)
        lane1 = lax.broadcasted_iota(jnp.int32, (1, 128), 1)
        dba = jnp.zeros((tm, 128), F32)
        sm = jnp.zeros((1, 128), F32)
        for h in range(GDN_HEADS):
            sl = slice(GDN_DIM * h, GDN_DIM * (h + 1))
            ks = slice(GDN_WIDTH + GDN_DIM * h, GDN_WIDTH + GDN_DIM * (h + 1))
            qh, kh = a[:, sl], a[:, ks]
            rq = lax.rsqrt(jnp.sum(qh * qh, axis=-1, keepdims=True) + EPS)
            rk = lax.rsqrt(jnp.sum(kh * kh, axis=-1, keepdims=True) + EPS)
            qhat, khat = qh * rq, kh * rk
            dyq = dqn_ref[:, sl] * (GDN_DIM ** -0.5)
            dyk = dkn_ref[:, sl]
            dq = rq * (dyq - qhat * jnp.sum(dyq * qhat, axis=-1, keepdims=True))
            dk = rk * (dyk - khat * jnp.sum(dyk * khat, axis=-1, keepdims=True))
            dc_ref[:, sl] = dq * dsl[:, sl]
            dc_ref[:, ks] = dk * dsl[:, ks]
            beta = _sigmoid(ba[:, h:h + 1])
            db = dbeta_ref[:, h:h + 1] * beta * (1.0 - beta)
            aneg = -jnp.exp(al_ref[0:1, h:h + 1])
            xa = ba[:, GDN_HEADS + h:GDN_HEADS + h + 1] + dt_ref[0:1, h:h + 1]
            dgh = dg_ref[:, h:h + 1]
            dxa = dgh * aneg * _sigmoid(xa)
            dba = jnp.where(lane == h, db, dba)
            dba = jnp.where(lane == GDN_HEADS + h, dxa, dba)
            d_alog = jnp.sum(dgh * _softplus(xa), axis=0, keepdims=True) * aneg
            sm = jnp.where(lane1 == h, d_alog, sm)
            sm = jnp.where(lane1 == GDN_HEADS + h, jnp.sum(dxa, axis=0, keepdims=True), sm)
        vs = slice(2 * GDN_WIDTH, W3)
        dc_ref[:, vs] = dv_ref[...] * dsl[:, vs]
        dba_ref[...] = dba

        @pl.when(i == 0)
        def _():
            sm_ref[...] = sm

        @pl.when(i > 0)
        def _():
            sm_ref[...] += sm

    prev_spec, _ = _halo_specs(tm, W3, 0, nblk)
    row = pl.BlockSpec((tm, GDN_WIDTH), lambda i: (i, 0))
    col = pl.BlockSpec((tm, 128), lambda i: (i, 0))
    small = lambda a: pl.BlockSpec(a.shape, lambda i: (0, 0))
    return pl.pallas_call(
        body, name=name, grid=(nblk,),
        in_specs=[row, row, row, col, col, pl.BlockSpec((tm, W3), lambda i: (i, 0)), prev_spec,
                  pl.BlockSpec((tm, 128), lambda i: (i, W3 // 128)), small(conv_w), small(a_log), small(dt_bias)],
        out_specs=[pl.BlockSpec((tm, W3), lambda i: (i, 0)), col, pl.BlockSpec((1, 128), lambda i: (0, 0))],
        out_shape=[_sds((S, W3)), _sds((S, 128)), _sds((1, 128))], compiler_params=_params(("arbitrary",)),
    )(dqn, dkn, dv, dg, dbeta, proj_a, proj_a, proj_a, conv_w, a_log, dt_bias)


def _gdn_conv_bwd(dc, dba, proj_a, conv_w, *, name, tm=256):
    S = proj_a.shape[0]
    nblk = S // tm
    W3 = 3 * GDN_WIDTH

    def body(dc_ref, dnext_ref, dba_ref, cur_ref, prev_ref, cw_ref, da_ref, dcw_ref):
        i = pl.program_id(0)
        prev = jnp.where(i > 0, prev_ref[...], 0.0)
        nxt = jnp.where(i < nblk - 1, dnext_ref[...], 0.0)
        dx, dw = _conv_rows_bwd(dc_ref[...], nxt, prev, cur_ref[...], cw_ref[...], GDN_CONV)
        da_ref[:, 0:W3] = dx.astype(BF16)
        da_ref[:, W3:] = dba_ref[...].astype(BF16)

        @pl.when(i == 0)
        def _():
            dcw_ref[...] = dw

        @pl.when(i > 0)
        def _():
            dcw_ref[...] += dw

    prev_spec, next_spec = _halo_specs(tm, W3, 0, nblk)
    wide = pl.BlockSpec((tm, W3), lambda i: (i, 0))
    return pl.pallas_call(
        body, name=name, grid=(nblk,),
        in_specs=[wide, next_spec, pl.BlockSpec((tm, 128), lambda i: (i, 0)), wide, prev_spec,
                  pl.BlockSpec(conv_w.shape, lambda i: (0, 0))],
        out_specs=[pl.BlockSpec((tm, A_COLS), lambda i: (i, 0)), pl.BlockSpec(conv_w.shape, lambda i: (0, 0))],
        out_shape=[_sds((S, A_COLS), BF16), _sds(conv_w.shape)], compiler_params=_params(("arbitrary",)),
    )(dc, dc, dba, proj_a, proj_a, conv_w)


def _band_mask(nk):
    i = lax.broadcasted_iota(jnp.int32, (BAND, nk), 0)
    j = lax.broadcasted_iota(jnp.int32, (BAND, nk), 1)
    if nk == BAND:
        return j <= i
    return (j >= i) & (j <= i + BAND)


def _rows(start, size, stride):
    return pl.ds(start, size) if stride == 1 else pl.ds(start, size, stride=stride)


def _attn_blocks(S, visit):
    for d in DILATIONS:
        nb = S // (d * BAND)

        def per_residue(r, carry, d=d, nb=nb):
            visit(d, r, 0, True)
            if nb > 1:
                def per_block(n, c):
                    visit(d, r, n, False)
                    return c
                lax.fori_loop(1, nb, per_block, 0)
            return carry

        if d == 1:
            per_residue(0, 0)
        else:
            lax.fori_loop(0, d, per_residue, 0)


def _attn_fwd(proj_b, oab, *, name):
    S = proj_b.shape[0]
    scale = DIL_DIM ** -0.5

    def body(q_ref, k_ref, v_ref, oab_in_ref, ob_ref, lse_ref, m_ref, l_ref, acc_ref):
        del oab_in_ref
        lane = lax.broadcasted_iota(jnp.int32, (BAND, 128), 1)
        lo = lane < DIL_DIM
        m_ref[...] = jnp.full_like(m_ref, NEG_BIG)
        l_ref[...] = jnp.zeros_like(l_ref)
        acc_ref[...] = jnp.zeros_like(acc_ref)

        def visit(d, r, n, first):
            nk = BAND if first else 2 * BAND
            qs = r + n * (BAND * d)
            ks = r if first else r + (n - 1) * (BAND * d)
            qrows = _rows(qs, BAND, d)
            krows = _rows(ks, nk, d)
            q = q_ref[qrows, :] * scale
            k = k_ref[krows, :].astype(BF16)
            v = v_ref[krows, :].astype(BF16)
            valid = _band_mask(nk)
            s0 = jnp.where(valid, _bdot_nt(jnp.where(lo, q, 0.0), k), NEG_BIG)
            s1 = jnp.where(valid, _bdot_nt(jnp.where(lo, 0.0, q), k), NEG_BIG)
            m_old = m_ref[qrows, :]
            mb = jnp.where(lo, jnp.max(s0, axis=-1, keepdims=True), jnp.max(s1, axis=-1, keepdims=True))
            m_new = jnp.maximum(m_old, mb)
            alpha = jnp.exp(m_old - m_new)
            p0 = jnp.exp(s0 - m_new[:, 0:1])
            p1 = jnp.exp(s1 - m_new[:, DIL_DIM:DIL_DIM + 1])
            psum = jnp.where(lo, jnp.sum(p0, axis=-1, keepdims=True), jnp.sum(p1, axis=-1, keepdims=True))
            pv = jnp.where(lo, _bdot(p0, v), _bdot(p1, v))
            m_ref[qrows, :] = m_new
            l_ref[qrows, :] = alpha * l_ref[qrows, :] + psum
            acc_ref[qrows, :] = alpha * acc_ref[qrows, :] + pv

        _attn_blocks(S, visit)
        ob_ref[...] = (acc_ref[...] / l_ref[...]).astype(BF16)
        lse_ref[...] = m_ref[...] + jnp.log(l_ref[...])

    part = lambda t: pl.BlockSpec((S, 128), lambda p: (0, 3 * p + t))
    return pl.pallas_call(
        body, name=name, grid=(DIL_PAIRS,),
        in_specs=[part(0), part(1), part(2), pl.BlockSpec(memory_space=pl.ANY)],
        out_specs=[pl.BlockSpec((S, 128), lambda p: (0, GDN_WIDTH // 128 + p)), pl.BlockSpec((S, 128), lambda p: (0, p))],
        out_shape=[_sds(oab.shape, BF16), _sds((S, DIL_WIDTH))],
        scratch_shapes=[pltpu.VMEM((S, 128), F32)] * 3, input_output_aliases={3: 0},
        compiler_params=_params(("parallel",)),
    )(proj_b, proj_b, proj_b, oab)


def _attn_bwd(proj_b, oab, d_oab, lse, *, name):
    S = proj_b.shape[0]
    scale = DIL_DIM ** -0.5

    def body(q_ref, k_ref, v_ref, o_ref, do_ref, lse_ref, dqkv_ref, dq_ref, dk_ref, dv_ref, delta_ref):
        lane = lax.broadcasted_iota(jnp.int32, (BAND, 128), 1)
        lo = lane < DIL_DIM
        dq_ref[...] = jnp.zeros_like(dq_ref)
        dk_ref[...] = jnp.zeros_like(dk_ref)
        dv_ref[...] = jnp.zeros_like(dv_ref)
        prod = do_ref[...] * o_ref[...].astype(F32)
        lo_all = lax.broadcasted_iota(jnp.int32, (S, 128), 1) < DIL_DIM
        delta_ref[...] = jnp.where(lo_all, jnp.sum(jnp.where(lo_all, prod, 0.0), axis=-1, keepdims=True),
                                   jnp.sum(jnp.where(lo_all, 0.0, prod), axis=-1, keepdims=True))

        def visit(d, r, n, first):
            nk = BAND if first else 2 * BAND
            qs = r + n * (BAND * d)
            ks = r if first else r + (n - 1) * (BAND * d)
            qrows = _rows(qs, BAND, d)
            krows = _rows(ks, nk, d)
            q = q_ref[qrows, :] * scale
            k = k_ref[krows, :]
            v = v_ref[krows, :]
            do = do_ref[qrows, :]
            delta_b = delta_ref[qrows, :]
            lse_b = lse_ref[qrows, :]
            valid = _band_mask(nk)
            lkl = lax.broadcasted_iota(jnp.int32, (nk, 128), 1) < DIL_DIM
            dq = jnp.zeros((BAND, 128), F32)
            dk = jnp.zeros((nk, 128), F32)
            dv = jnp.zeros((nk, 128), F32)
            for sel, ksel, lcol in ((lo, lkl, 0), (~lo, ~lkl, DIL_DIM)):
                qh = jnp.where(sel, q, 0.0)
                doh = jnp.where(sel, do, 0.0)
                s = _bdot_nt(qh, k)
                p = jnp.where(valid, jnp.exp(s - lse_b[:, lcol:lcol + 1]), 0.0)
                dp = _bdot_nt(doh, v)
                ds = p * (dp - delta_b[:, lcol:lcol + 1])
                dq = dq + _bdot(ds, jnp.where(ksel, k, 0.0))
                dk = dk + _bdot_tn(ds, qh)
                dv = dv + _bdot_tn(p, doh)
            dq_ref[qrows, :] += dq * scale
            dk_ref[krows, :] += dk
            dv_ref[krows, :] += dv

        _attn_blocks(S, visit)
        dqkv_ref[:, 0:128] = dq_ref[...].astype(BF16)
        dqkv_ref[:, 128:256] = dk_ref[...].astype(BF16)
        dqkv_ref[:, 256:384] = dv_ref[...].astype(BF16)

    half = lambda p: (0, GDN_WIDTH // 128 + p)
    part = lambda t: pl.BlockSpec((S, 128), lambda p: (0, 3 * p + t))
    return pl.pallas_call(
        body, name=name, grid=(DIL_PAIRS,),
        in_specs=[part(0), part(1), part(2), pl.BlockSpec((S, 128), half), pl.BlockSpec((S, 128), half),
                  pl.BlockSpec((S, 128), lambda p: (0, p))],
        out_specs=pl.BlockSpec((S, 384), lambda p: (0, p)), out_shape=_sds((S, 3 * DIL_WIDTH), BF16),
        scratch_shapes=[pltpu.VMEM((S, 128), F32)] * 4, compiler_params=_params(("parallel",)),
    )(proj_b, proj_b, proj_b, oab, d_oab, lse)


def _ffn_act(u_pre, conv_w, *, name, tm=256, tc=256):
    S = u_pre.shape[0]
    nblk, ncol = S // tm, D_FF // tc

    def body(g_ref, gp_ref, u_ref, up_ref, wg_ref, wu_ref, act_ref):
        i = pl.program_id(1)
        zero = lambda ref: jnp.where(i > 0, ref[...], 0.0)
        gate = _conv_rows(zero(gp_ref), g_ref[...], wg_ref[...], FFN_CONV)
        up = _conv_rows(zero(up_ref), u_ref[...], wu_ref[...], FFN_CONV)
        act_ref[...] = (gate * _sigmoid(gate) * up).astype(BF16)

    per = tm // HALO
    prev = lambda off: pl.BlockSpec((HALO, tc), lambda j, i: (jnp.maximum(i * per - 1, 0), j + off))
    cur = lambda off: pl.BlockSpec((tm, tc), lambda j, i: (i, j + off))
    wsp = lambda off: pl.BlockSpec((FFN_CONV, tc), lambda j, i: (0, j + off))
    return pl.pallas_call(
        body, name=name, grid=(ncol, nblk),
        in_specs=[cur(0), prev(0), cur(ncol), prev(ncol), wsp(0), wsp(ncol)],
        out_specs=pl.BlockSpec((tm, tc), lambda j, i: (i, j)), out_shape=_sds((S, D_FF), BF16),
        compiler_params=_params(("parallel", "parallel")),
    )(u_pre, u_pre, u_pre, u_pre, conv_w, conv_w)


def _ffn_act_bwd(d_act, u_pre, conv_w, *, name, tm=256, tc=256):
    S = u_pre.shape[0]
    nblk, ncol = S // tm, D_FF // tc

    def body(da_ref, g_ref, gp_ref, u_ref, up_ref, wg_ref, wu_ref, dg_ref, du_ref):
        i = pl.program_id(1)
        zero = lambda ref: jnp.where(i > 0, ref[...], 0.0)
        gate = _conv_rows(zero(gp_ref), g_ref[...], wg_ref[...], FFN_CONV)
        up = _conv_rows(zero(up_ref), u_ref[...], wu_ref[...], FFN_CONV)
        sg = _sigmoid(gate)
        da = da_ref[...]
        dg_ref[...] = (da * up * _dsilu(gate, sg)).astype(BF16)
        du_ref[...] = (da * gate * sg).astype(BF16)

    per = tm // HALO
    prev = lambda off: pl.BlockSpec((HALO, tc), lambda j, i: (jnp.maximum(i * per - 1, 0), j + off))
    cur = lambda off: pl.BlockSpec((tm, tc), lambda j, i: (i, j + off))
    wsp = lambda off: pl.BlockSpec((FFN_CONV, tc), lambda j, i: (0, j + off))
    outs = pl.pallas_call(
        body, name=name, grid=(ncol, nblk),
        in_specs=[cur(0), cur(0), prev(0), cur(ncol), prev(ncol), wsp(0), wsp(ncol)],
        out_specs=[cur(0), cur(0)], out_shape=[_sds((S, D_FF), BF16)] * 2,
        compiler_params=_params(("parallel", "parallel")),
    )(d_act, u_pre, u_pre, u_pre, u_pre, conv_w, conv_w)
    return outs


def _ffn_conv_bwd(dc, u_pre, conv_w, col_off, *, name, tm=256, tc=256):
    S = u_pre.shape[0]
    nblk, ncol = S // tm, D_FF // tc
    off = col_off // tc

    def body(dc_ref, dn_ref, x_ref, xp_ref, w_ref, dx_ref, dw_ref):
        i = pl.program_id(1)
        prev = jnp.where(i > 0, xp_ref[...], 0.0)
        nxt = jnp.where(i < nblk - 1, dn_ref[...].astype(F32), 0.0)
        dx, dw = _conv_rows_bwd(dc_ref[...].astype(F32), nxt[0:HALO], prev, x_ref[...], w_ref[...], FFN_CONV)
        dx_ref[...] = dx.astype(BF16)

        @pl.when(i == 0)
        def _():
            dw_ref[...] = dw

        @pl.when(i > 0)
        def _():
            dw_ref[...] += dw

    per = tm // HALO
    hb = 2 * HALO
    perb = tm // hb
    return pl.pallas_call(
        body, name=name, grid=(ncol, nblk),
        in_specs=[pl.BlockSpec((tm, tc), lambda j, i: (i, j)),
                  pl.BlockSpec((hb, tc), lambda j, i: (jnp.minimum((i + 1) * perb, nblk * perb - 1), j)),
                  pl.BlockSpec((tm, tc), lambda j, i: (i, j + off)),
                  pl.BlockSpec((HALO, tc), lambda j, i: (jnp.maximum(i * per - 1, 0), j + off)),
                  pl.BlockSpec((FFN_CONV, tc), lambda j, i: (0, j + off))],
        out_specs=[pl.BlockSpec((tm, tc), lambda j, i: (i, j)), pl.BlockSpec((FFN_CONV, tc), lambda j, i: (0, j))],
        out_shape=[_sds((S, D_FF), BF16), _sds((FFN_CONV, D_FF))],
        compiler_params=_params(("parallel", "arbitrary")),
    )(dc, dc, u_pre, u_pre, conv_w)


def _local_step(x, tgt, norm1_w, w_a, w_z, w_b, conv_a, a_log, dt_bias, gnw, w_out, norm2_w, w_up, conv_f, w_down, final_w):
    wgrad = functools.partial(_mm, ta=True, out_dtype=BF16)
    h1 = _rms_fwd(x, norm1_w, name="rms1_fwd")
    proj_a = _mm(h1, w_a, name="proj_a", tn=A_COLS, tk=1024)
    proj_z = _mm(h1, w_z, name="proj_z", tk=1024)
    proj_b = _mm(h1, w_b, name="proj_b", tn=768, tk=1024)
    qn, kn, v, gcb, bb = _gdn_prep_fwd(proj_a, conv_a, a_log, dt_bias, name="gdn_prep_fwd")
    uv, wk, at, tmat = _gdn_chunk_fwd(qn, kn, v, gcb, bb, name="gdn_chunk_fwd")
    o, u, sp, oab = _gdn_scan_fwd(uv, wk, at, qn, kn, gcb, bb, proj_z, gnw, name="gdn_scan_fwd")
    oab, lse = _attn_fwd(proj_b, oab, name="attn_fwd")
    x1 = _mm(oab, w_out, res=x, name="out_proj", tk=1024)
    h2 = _rms_fwd(x1, norm2_w, name="rms2_fwd")
    u_pre = _mm(h2, w_up, name="ffn_up", tk=1024)
    act = _ffn_act(u_pre, conv_f, name="ffn_act")
    x2 = _mm(act, w_down, res=x1, name="ffn_down", tk=1408)
    dx2, d_final, loss = _loss_head(x2, final_w, tgt, name="loss_head")
    d_act = _mm(dx2, w_down, tb=True, name="ffn_down_dx", tn=1408, tk=1024)
    g_down = wgrad(act, dx2, name="ffn_down_dw", tm=1408, tk=512)
    dgate, dup = _ffn_act_bwd(d_act, u_pre, conv_f, name="ffn_act_bwd")
    du_g, dcw_g = _ffn_conv_bwd(dgate, u_pre, conv_f, 0, name="ffn_conv_bwd_gate")
    du_u, dcw_u = _ffn_conv_bwd(dup, u_pre, conv_f, D_FF, name="ffn_conv_bwd_up")
    dh2 = _mm(du_g, w_up[:, :D_FF], tb=True, name="ffn_up_dx_gate", tk=1408)
    dh2 = _mm(du_u, w_up[:, D_FF:], tb=True, res=dh2, name="ffn_up_dx_up", tk=1408)
    g_up_g = wgrad(h2, du_g, name="ffn_up_dw_gate", tn=1408)
    g_up_u = wgrad(h2, du_u, name="ffn_up_dw_up", tn=1408)
    dx1, d_norm2 = _rms_bwd(dh2, x1, norm2_w, dx2, name="rms2_bwd")
    d_oab = _mm(dx1, w_out, tb=True, name="out_proj_dx", tk=1024)
    g_out = wgrad(oab, dx1, name="out_proj_dw")
    dz, d_gnw, du, dwk, dat, dqd, dke, dgl = _gdn_scan_bwd(d_oab, o, proj_z, gnw, sp, u, wk, at, qn, kn, gcb, bb, name="gdn_scan_bwd")
    dqn, dkn, dv, dg, dbeta = _gdn_chunk_bwd(qn, kn, v, gcb, bb, tmat, uv, wk, du, dwk, dat, dqd, dke, dgl, name="gdn_chunk_bwd")
    dc, dba, d_small = _gdn_prep_bwd(dqn, dkn, dv, dg, dbeta, proj_a, conv_a, a_log, dt_bias, name="gdn_prep_bwd")
    d_pa, d_conv_a = _gdn_conv_bwd(dc, dba, proj_a, conv_a, name="gdn_conv_bwd")
    d_pb = _attn_bwd(proj_b, oab, d_oab, lse, name="attn_bwd")
    dh1 = _mm(d_pa, w_a, tb=True, name="proj_a_dx", tk=A_COLS)
    dh1 = _mm(dz, w_z, tb=True, res=dh1, name="proj_z_dx")
    dh1 = _mm(d_pb, w_b, tb=True, res=dh1, name="proj_b_dx", tk=1536)
    g_a = wgrad(h1, d_pa, name="proj_a_dw", tn=A_COLS)
    g_z = wgrad(h1, dz, name="proj_z_dw")
    g_b = wgrad(h1, d_pb, name="proj_b_dw", tn=768)
    grad_x, d_norm1 = _rms_bwd(dh1, x, norm1_w, dx1, name="rms1_bwd")
    grads = dict(norm1=d_norm1, w_a=g_a, w_z=g_z, w_b=g_b, conv_a=d_conv_a, small=d_small, gnw=d_gnw, w_out=g_out,
                 norm2=d_norm2, w_up=jnp.concatenate([g_up_g, g_up_u], axis=1),
                 conv_f=jnp.concatenate([dcw_g, dcw_u], axis=1), w_down=g_down, final=d_final)
    return loss, grad_x, grads


_O1 = 3 * GDN_WIDTH
_O2 = _O1 + GDN_WIDTH
_O3 = _O2 + 2 * GDN_HEADS


def _split_w_in(w_in):
    d = w_in.shape[0]
    pad = jnp.zeros((d, A_COLS - _O1 - 2 * GDN_HEADS), w_in.dtype)
    w_a = jnp.concatenate([w_in[:, :_O1], w_in[:, _O2:_O3], pad], axis=1)
    w_b = w_in[:, _O3:].reshape(d, 3, DIL_PAIRS, 128).transpose(0, 2, 1, 3).reshape(d, 3 * DIL_WIDTH)
    return w_a, w_in[:, _O1:_O2], w_b


def _merge_g_in(g_a, g_z, g_b):
    d = g_a.shape[0]
    g_b = g_b.reshape(d, DIL_PAIRS, 3, 128).transpose(0, 2, 1, 3).reshape(d, 3 * DIL_WIDTH)
    return jnp.concatenate([g_a[:, :_O1], g_z, g_a[:, _O1:_O1 + 2 * GDN_HEADS], g_b], axis=1)


MESH = pl.DeviceIdType.MESH
ANY = pl.BlockSpec(memory_space=pl.ANY)


def _position():
    return lax.axis_index("x"), lax.axis_index("y"), lax.axis_index("c")


def _slot(p):
    return 4 * p[0] + 2 * p[1] + p[2]


def _all_gather(blocks, *, name):
    n = len(blocks)

    def body(*refs):
        ins, outs = refs[:n], refs[n:2 * n]
        send_sems, recv_sems, local_sems = refs[2 * n:]
        x, y, c = _position()
        me, sibling = (x, y, c), (x, y, 1 - c)
        chips = [(1 - x, y), (x, 1 - y), (1 - x, 1 - y)]

        def copy(a, k, block, to, src=None):
            dst = outs[a].at[_slot(block)]
            return pltpu.make_async_remote_copy(
                src_ref=dst if src is None else src, dst_ref=dst, send_sem=send_sems.at[a, k], recv_sem=recv_sems.at[a, k],
                device_id=to, device_id_type=MESH)

        mine = [pltpu.make_async_copy(ins[a], outs[a].at[_slot(me)], local_sems.at[a]) for a in range(n)]
        for cp in mine:
            cp.start()
        first = []
        for a in range(n):
            first.append(copy(a, 0, me, sibling, src=ins[a]))
            first += [copy(a, 1 + j, me, (*chip, c), src=ins[a]) for j, chip in enumerate(chips)]
        for cp in first:
            cp.start()
        passed = []
        for j, chip in enumerate(chips):
            for a in range(n):
                copy(a, 1 + j, (*chip, c), me).wait_recv()
                fwd = copy(a, 4 + j, (*chip, c), sibling)
                fwd.start()
                passed.append(fwd)
        for a in range(n):
            copy(a, 0, sibling, me).wait_recv()
            for j, chip in enumerate(chips):
                copy(a, 4 + j, (*chip, 1 - c), me).wait_recv()
        for cp in first + passed:
            cp.wait_send()
        for cp in mine:
            cp.wait()

    return pl.pallas_call(
        body, name=name, in_specs=[ANY] * n, out_specs=[ANY] * n,
        out_shape=[_sds((N_DEV,) + b.shape, b.dtype) for b in blocks],
        scratch_shapes=[pltpu.SemaphoreType.DMA((n, 7)), pltpu.SemaphoreType.DMA((n, 7)), pltpu.SemaphoreType.DMA((n,))],
    )(*blocks)


def _exchange_slabs(slabs, *, name):
    n = len(slabs)

    def body(*refs):
        ins, outs = refs[:n], refs[n:2 * n]
        send_sems, recv_sems, local_sems = refs[2 * n:]
        x, y, c = _position()
        me = _slot((x, y, c))
        mine = [pltpu.make_async_copy(ins[a].at[me], outs[a].at[me], local_sems.at[a]) for a in range(n)]
        for cp in mine:
            cp.start()
        copies = []
        for k in range(1, N_DEV):
            peer = (1 - x if k & 4 else x, 1 - y if k & 2 else y, 1 - c if k & 1 else c)
            for a in range(n):
                copies.append(pltpu.make_async_remote_copy(
                    src_ref=ins[a].at[_slot(peer)], dst_ref=outs[a].at[me], send_sem=send_sems.at[a, k - 1],
                    recv_sem=recv_sems.at[a, k - 1], device_id=peer, device_id_type=MESH))
        for cp in copies:
            cp.start()
        for cp in copies:
            cp.wait()
        for cp in mine:
            cp.wait()

    return pl.pallas_call(
        body, name=name, in_specs=[ANY] * n, out_specs=[ANY] * n,
        out_shape=[_sds(s.shape, s.dtype) for s in slabs],
        scratch_shapes=[pltpu.SemaphoreType.DMA((n, 7)), pltpu.SemaphoreType.DMA((n, 7)), pltpu.SemaphoreType.DMA((n,))],
    )(*slabs)


def _adamw(parts, w, m, v, *, name, tr=None):
    R, C = w.shape
    tr = R if tr is None else tr
    assert R % tr == 0
    c1 = 1.0 - ADAM_B1 ** ADAM_STEP
    c2 = 1.0 - ADAM_B2 ** ADAM_STEP

    def body(p_ref, w_ref, m_ref, v_ref, g_ref, d_ref, nm_ref, nv_ref):
        g = p_ref[0].astype(F32)
        for s in range(1, N_DEV):
            g = g + p_ref[s].astype(F32)
        nm = ADAM_B1 * m_ref[...] + (1.0 - ADAM_B1) * g
        nv = ADAM_B2 * v_ref[...] + (1.0 - ADAM_B2) * (g * g)
        g_ref[...] = g
        nm_ref[...] = nm
        nv_ref[...] = nv
        d_ref[...] = -ADAM_LR * ((nm / c1) / (jnp.sqrt(nv / c2) + ADAM_EPS) + ADAM_WD * w_ref[...])

    blk = pl.BlockSpec((tr, C), lambda i: (i, 0))
    return pl.pallas_call(
        body, name=name, grid=(R // tr,), in_specs=[pl.BlockSpec((N_DEV, tr, C), lambda i: (0, i, 0)), blk, blk, blk],
        out_specs=[blk] * 4, out_shape=[_sds((R, C))] * 4, compiler_params=_params(("parallel",)),
    )(parts, w, m, v)


_SMALL_ROWS = 8


def _pack_small(norm1, norm2, final, gnw, a_log, dt_bias, loss=None):
    loss = jnp.zeros((1, 128), F32) if loss is None else loss
    row3 = jnp.concatenate([gnw, a_log, dt_bias, jnp.zeros((1, 128 - 2 * GDN_HEADS), F32), loss,
                            jnp.zeros((1, D_MODEL - 3 * 128), F32)], axis=1)
    return jnp.concatenate([norm1, norm2, final, row3, jnp.zeros((_SMALL_ROWS - 4, D_MODEL), F32)], axis=0)


def _unpack_small(p):
    return (p[0:1], p[1:2], p[2], p[3:4, 0:128], p[3:4, 128:128 + GDN_HEADS], p[3:4, 128 + GDN_HEADS:128 + 2 * GDN_HEADS])


def _slabs_by_cols(g):
    r = g.shape[0]
    return g.reshape(r, N_DEV, -1).transpose(1, 0, 2)


def _cols_from_slabs(s):
    return s.transpose(1, 0, 2).reshape(s.shape[1], -1)


def kernel(x, norm1_w, w_in, conv_qkv_w, a_log, dt_bias, gdn_norm_w, w_out, norm2_w, w_up, ffn_conv_w, w_down, final_norm_w, loss_target, m_norm1_w, m_w_in, m_conv_qkv_w, m_a_log, m_dt_bias, m_gdn_norm_w, m_w_out, m_norm2_w, m_w_up, m_ffn_conv_w, m_w_down, m_final_norm_w, v_norm1_w, v_w_in, v_conv_qkv_w, v_a_log, v_dt_bias, v_gdn_norm_w, v_w_out, v_norm2_w, v_w_up, v_ffn_conv_w, v_w_down, v_final_norm_w):
    bf = lambda a: a.astype(BF16)
    gw_in, gw_out, gw_up, gw_down, g_conv_a, g_conv_f = _all_gather(
        [bf(w_in[0]), bf(w_out[0]), bf(w_up[0]), bf(w_down[0]), conv_qkv_w[0], ffn_conv_w[0]], name="gather_weights")
    w_a, w_z, w_b = _split_w_in(_cols_from_slabs(gw_in))
    loss, grad_x, g = _local_step(
        x[0], loss_target[0], norm1_w, w_a, w_z, w_b, _cols_from_slabs(g_conv_a), a_log, dt_bias, gdn_norm_w,
        gw_out.reshape(D_MODEL, D_MODEL), norm2_w, _cols_from_slabs(gw_up), _cols_from_slabs(g_conv_f),
        gw_down.reshape(D_FF, D_MODEL), final_norm_w[None])
    r_in, r_out, r_up, r_down, r_conv_a, r_conv_f = _exchange_slabs(
        [_slabs_by_cols(_merge_g_in(g["w_a"], g["w_z"], g["w_b"])), g["w_out"].reshape(N_DEV, -1, D_MODEL),
         _slabs_by_cols(g["w_up"]), g["w_down"].reshape(N_DEV, -1, D_MODEL), _slabs_by_cols(g["conv_a"]),
         _slabs_by_cols(g["conv_f"])], name="exchange_grads")
    (small_all,) = _all_gather(
        [_pack_small(g["norm1"], g["norm2"], g["final"], g["gnw"], g["small"][:, 0:GDN_HEADS],
                     g["small"][:, GDN_HEADS:2 * GDN_HEADS], loss)], name="gather_small")
    o_in = _adamw(r_in, w_in[0], m_w_in[0], v_w_in[0], name="adamw_w_in", tr=128)
    o_out = _adamw(r_out, w_out[0], m_w_out[0], v_w_out[0], name="adamw_w_out")
    o_up = _adamw(r_up, w_up[0], m_w_up[0], v_w_up[0], name="adamw_w_up", tr=128)
    o_down = _adamw(r_down, w_down[0], m_w_down[0], v_w_down[0], name="adamw_w_down", tr=176)
    o_ca = _adamw(r_conv_a, conv_qkv_w[0], m_conv_qkv_w[0], v_conv_qkv_w[0], name="adamw_conv_a")
    o_cf = _adamw(r_conv_f, ffn_conv_w[0], m_ffn_conv_w[0], v_ffn_conv_w[0], name="adamw_conv_f")
    o_small = _adamw(
        small_all, _pack_small(norm1_w, norm2_w, final_norm_w[None], gdn_norm_w, a_log, dt_bias),
        _pack_small(m_norm1_w, m_norm2_w, m_final_norm_w[None], m_gdn_norm_w, m_a_log, m_dt_bias),
        _pack_small(v_norm1_w, v_norm2_w, v_final_norm_w[None], v_gdn_norm_w, v_a_log, v_dt_bias), name="adamw_small")
    total_loss = o_small[0][3, 256]
    outs = [total_loss, grad_x[None]]
    for k in range(4):
        n1, n2, fin, gn, al, dt = _unpack_small(o_small[k])
        outs += [n1, o_in[k][None], o_ca[k][None], al, dt, gn, o_out[k][None], n2, o_up[k][None], o_cf[k][None], o_down[k][None], fin]
    return tuple(outs)
```

```python
import functools

import jax
import jax.numpy as jnp
from jax import lax
from jax.experimental import pallas as pl
from jax.experimental.pallas import tpu as pltpu

F32 = jnp.float32
BF16 = jnp.bfloat16

N_DEV = 8
D_MODEL = 1024
GDN_HEADS = 4
GDN_DIM = 128
GDN_WIDTH = GDN_HEADS * GDN_DIM
GDN_CONV = 4
CHUNK = 64
DIL_HEADS = 8
DIL_DIM = 64
DIL_WIDTH = DIL_HEADS * DIL_DIM
DIL_PAIRS = DIL_HEADS // 2
DILATIONS = (1, 4, 16)
BAND = 128
D_FF = 2816
FFN_CONV = 3
EPS = 1e-6
A_COLS = 3 * GDN_WIDTH + 128
HALO = 8

ADAM_LR = 0.001
ADAM_B1 = 0.9
ADAM_B2 = 0.999
ADAM_EPS = 1e-08
ADAM_WD = 0.01
ADAM_STEP = 10

VMEM_LIMIT_BYTES = 56 * 1024 * 1024
NEG_BIG = -1e30


def _params(sem=None):
    return pltpu.CompilerParams(dimension_semantics=sem, vmem_limit_bytes=VMEM_LIMIT_BYTES)


def _sds(shape, dtype=F32):
    return jax.ShapeDtypeStruct(shape, dtype)


def _bdot(a, b):
    return jnp.dot(a.astype(BF16), b.astype(BF16), preferred_element_type=F32)


def _bdot_nt(a, b):
    return lax.dot_general(a.astype(BF16), b.astype(BF16), (((1,), (1,)), ((), ())), preferred_element_type=F32)


def _bdot_tn(a, b):
    return lax.dot_general(a.astype(BF16), b.astype(BF16), (((0,), (0,)), ((), ())), preferred_element_type=F32)


def _split(a):
    hi = a.astype(BF16)
    lo = (a - hi.astype(F32)).astype(BF16)
    return hi, lo


def _dot3(a, b, dims):
    ah, al = _split(a)
    bh, bl = _split(b)
    d = functools.partial(lax.dot_general, dimension_numbers=(dims, ((), ())), preferred_element_type=F32)
    return d(ah, bh) + (d(al, bh) + d(ah, bl))


def _exact_tri_dot(tri, g):
    g1 = g.astype(BF16)
    r1 = g - g1.astype(F32)
    g2 = r1.astype(BF16)
    g3 = (r1 - g2.astype(F32)).astype(BF16)
    t = tri.astype(BF16)
    d = functools.partial(jnp.dot, preferred_element_type=F32)
    return d(t, g1) + (d(t, g2) + d(t, g3))


def _sigmoid(x):
    return 1.0 / (1.0 + jnp.exp(-x))


def _dsilu(x, sg):
    return sg * (1.0 + x * (1.0 - sg))


def _mm(a, b, *, name, ta=False, tb=False, res=None, out_dtype=F32, tm=512, tn=512, tk=512):
    if ta:
        K, M = a.shape
    else:
        M, K = a.shape
    if tb:
        N, Kb = b.shape
    else:
        Kb, N = b.shape
    assert K == Kb, (a.shape, b.shape)
    tm, tn, tk = min(tm, M), min(tn, N), min(tk, K)
    assert M % tm == 0 and N % tn == 0 and K % tk == 0, (name, M, N, K, tm, tn, tk)
    nk = K // tk
    dims = (((0 if ta else 1,), (1 if tb else 0,)), ((), ()))
    has_res = res is not None

    def body(*refs):
        if has_res:
            a_ref, b_ref, r_ref, o_ref, acc_ref = refs
        else:
            a_ref, b_ref, o_ref, acc_ref = refs
        k = pl.program_id(2)
        part = lax.dot_general(a_ref[...].astype(BF16), b_ref[...].astype(BF16), dims, preferred_element_type=F32)

        @pl.when(k == 0)
        def _():
            acc_ref[...] = part

        @pl.when(k > 0)
        def _():
            acc_ref[...] += part

        @pl.when(k == nk - 1)
        def _():
            r = acc_ref[...]
            if has_res:
                r = r + r_ref[...]
            o_ref[...] = r.astype(out_dtype)

    a_spec = pl.BlockSpec((tk, tm), lambda i, j, k: (k, i)) if ta else pl.BlockSpec((tm, tk), lambda i, j, k: (i, k))
    b_spec = pl.BlockSpec((tn, tk), lambda i, j, k: (j, k)) if tb else pl.BlockSpec((tk, tn), lambda i, j, k: (k, j))
    o_spec = pl.BlockSpec((tm, tn), lambda i, j, k: (i, j))
    in_specs = [a_spec, b_spec] + ([o_spec] if has_res else [])
    args = (a, b) + ((res,) if has_res else ())
    return pl.pallas_call(
        body, name=name, grid=(M // tm, N // tn, nk), in_specs=in_specs, out_specs=o_spec,
        out_shape=_sds((M, N), out_dtype), scratch_shapes=[pltpu.VMEM((tm, tn), F32)],
        compiler_params=_params(("parallel", "parallel", "arbitrary")),
    )(*args)


def _rms_fwd(x, w, *, name, tm=256):
    S, D = x.shape

    def body(x_ref, w_ref, h_ref):
        xv = x_ref[...]
        r = lax.rsqrt(jnp.mean(xv * xv, axis=-1, keepdims=True) + EPS)
        h_ref[...] = (xv * r * w_ref[...]).astype(BF16)

    return pl.pallas_call(
        body, name=name, grid=(S // tm,),
        in_specs=[pl.BlockSpec((tm, D), lambda i: (i, 0)), pl.BlockSpec((1, D), lambda i: (0, 0))],
        out_specs=pl.BlockSpec((tm, D), lambda i: (i, 0)), out_shape=_sds((S, D), BF16),
        compiler_params=_params(("parallel",)),
    )(x, w)


def _rms_bwd(dh, x, w, res, *, name, tm=256):
    S, D = x.shape

    def body(dh_ref, x_ref, w_ref, res_ref, dx_ref, dw_ref):
        i = pl.program_id(0)
        xv = x_ref[...]
        g = dh_ref[...]
        r = lax.rsqrt(jnp.mean(xv * xv, axis=-1, keepdims=True) + EPS)
        xh = xv * r
        gw = g * w_ref[...]
        dx_ref[...] = res_ref[...] + r * (gw - xh * jnp.mean(gw * xh, axis=-1, keepdims=True))
        part = jnp.sum(g * xh, axis=0, keepdims=True)

        @pl.when(i == 0)
        def _():
            dw_ref[...] = part

        @pl.when(i > 0)
        def _():
            dw_ref[...] += part

    row = pl.BlockSpec((tm, D), lambda i: (i, 0))
    one = pl.BlockSpec((1, D), lambda i: (0, 0))
    return pl.pallas_call(
        body, name=name, grid=(S // tm,), in_specs=[row, row, one, row], out_specs=[row, one],
        out_shape=[_sds((S, D)), _sds((1, D))], compiler_params=_params(("arbitrary",)),
    )(dh, x, w, res)


def _loss_head(x2, w, tgt, *, name, tm=256):
    S, D = x2.shape

    def body(x_ref, w_ref, t_ref, dx_ref, dw_ref, loss_ref):
        i = pl.program_id(0)
        xv = x_ref[...]
        wv = w_ref[...]
        r = lax.rsqrt(jnp.mean(xv * xv, axis=-1, keepdims=True) + EPS)
        xh = xv * r
        err = xh * wv - t_ref[...]
        lrow = jnp.sum(err * err, axis=-1, keepdims=True)
        lsum = jnp.sum(lrow, axis=0, keepdims=True) * (0.5 / D)
        g = err * (1.0 / D)
        gw = g * wv
        dx_ref[...] = r * (gw - xh * jnp.mean(gw * xh, axis=-1, keepdims=True))
        part = jnp.sum(g * xh, axis=0, keepdims=True)
        lpart = jnp.broadcast_to(lsum, (1, 128))

        @pl.when(i == 0)
        def _():
            dw_ref[...] = part
            loss_ref[...] = lpart

        @pl.when(i > 0)
        def _():
            dw_ref[...] += part
            loss_ref[...] += lpart

    row = pl.BlockSpec((tm, D), lambda i: (i, 0))
    one = pl.BlockSpec((1, D), lambda i: (0, 0))
    return pl.pallas_call(
        body, name=name, grid=(S // tm,), in_specs=[row, one, row],
        out_specs=[row, one, pl.BlockSpec((1, 128), lambda i: (0, 0))],
        out_shape=[_sds((S, D)), _sds((1, D)), _sds((1, 128))], compiler_params=_params(("arbitrary",)),
    )(x2, w, tgt)


def _conv_rows(prev, cur, w, taps):
    n = cur.shape[0]
    xs = jnp.concatenate([prev, cur], axis=0)
    base = HALO - (taps - 1)
    out = xs[base:base + n] * w[0:1]
    for i in range(1, taps):
        out = out + xs[base + i:base + i + n] * w[i:i + 1]
    return out


def _conv_rows_bwd(cur_d, next_d, prev_x, cur_x, w, taps):
    n = cur_d.shape[0]
    ds = jnp.concatenate([cur_d, next_d], axis=0)
    dx = ds[taps - 1:taps - 1 + n] * w[0:1]
    for i in range(1, taps):
        dx = dx + ds[taps - 1 - i:taps - 1 - i + n] * w[i:i + 1]
    xs = jnp.concatenate([prev_x, cur_x], axis=0)
    base = HALO - (taps - 1)
    dws = [jnp.sum(cur_d * xs[base + i:base + i + n], axis=0, keepdims=True) for i in range(taps)]
    return dx, jnp.concatenate(dws, axis=0)


def _halo_specs(tm, width, col, nblk):
    per = tm // HALO
    prev = pl.BlockSpec((HALO, width), lambda i, *_: (jnp.maximum(i * per - 1, 0), col))
    nxt = pl.BlockSpec((HALO, width), lambda i, *_: (jnp.minimum((i + 1) * per, nblk * per - 1), col))
    return prev, nxt


def _softplus(x):
    return jnp.maximum(x, 0.0) + jnp.log1p(jnp.exp(-jnp.abs(x)))


def _chunk_tri(tm, upper=False):
    r = lax.broadcasted_iota(jnp.int32, (tm, tm), 0)
    c = lax.broadcasted_iota(jnp.int32, (tm, tm), 1)
    same = lax.div(r, CHUNK) == lax.div(c, CHUNK)
    order = (c >= r) if upper else (c <= r)
    return jnp.where(same & order, 1.0, 0.0)


def _gdn_prep_fwd(proj_a, conv_w, a_log, dt_bias, *, name, tm=256):
    S = proj_a.shape[0]
    nblk = S // tm
    W3 = 3 * GDN_WIDTH

    def body(cur_ref, prev_ref, ba_ref, cw_ref, al_ref, dt_ref, qn_ref, kn_ref, v_ref, gcb_ref, bb_ref):
        i = pl.program_id(0)
        prev = jnp.where(i > 0, prev_ref[...], 0.0)
        c = _conv_rows(prev, cur_ref[...], cw_ref[...], GDN_CONV)
        a = c * _sigmoid(c)
        ba = ba_ref[...]
        lane = lax.broadcasted_iota(jnp.int32, (tm, 128), 1)
        g4 = jnp.zeros((tm, 128), F32)
        for h in range(GDN_HEADS):
            sl = slice(GDN_DIM * h, GDN_DIM * (h + 1))
            qh = a[:, GDN_DIM * h:GDN_DIM * (h + 1)]
            kh = a[:, GDN_WIDTH + GDN_DIM * h:GDN_WIDTH + GDN_DIM * (h + 1)]
            qn_ref[:, sl] = qh * (lax.rsqrt(jnp.sum(qh * qh, axis=-1, keepdims=True) + EPS) * (GDN_DIM ** -0.5))
            kn_ref[:, sl] = kh * lax.rsqrt(jnp.sum(kh * kh, axis=-1, keepdims=True) + EPS)
            beta = _sigmoid(ba[:, h:h + 1])
            bb_ref[:, sl] = jnp.broadcast_to(beta, (tm, GDN_DIM))
            g = -jnp.exp(al_ref[0:1, h:h + 1]) * _softplus(ba[:, GDN_HEADS + h:GDN_HEADS + h + 1] + dt_ref[0:1, h:h + 1])
            g4 = jnp.where(lane == h, g, g4)
        v_ref[...] = a[:, 2 * GDN_WIDTH:]
        gc = _exact_tri_dot(_chunk_tri(tm), g4)
        for h in range(GDN_HEADS):
            gcb_ref[:, GDN_DIM * h:GDN_DIM * (h + 1)] = jnp.broadcast_to(gc[:, h:h + 1], (tm, GDN_DIM))

    prev_spec, _ = _halo_specs(tm, W3, 0, nblk)
    row = pl.BlockSpec((tm, GDN_WIDTH), lambda i: (i, 0))
    small = lambda a: pl.BlockSpec(a.shape, lambda i: (0, 0))
    return pl.pallas_call(
        body, name=name, grid=(nblk,),
        in_specs=[pl.BlockSpec((tm, W3), lambda i: (i, 0)), prev_spec,
                  pl.BlockSpec((tm, 128), lambda i: (i, W3 // 128)), small(conv_w), small(a_log), small(dt_bias)],
        out_specs=[row] * 5, out_shape=[_sds((S, GDN_WIDTH))] * 5, compiler_params=_params(("parallel",)),
    )(proj_a, proj_a, proj_a, conv_w, a_log, dt_bias)


def _chunk_masks():
    r = lax.broadcasted_iota(jnp.int32, (CHUNK, CHUNK), 0)
    c = lax.broadcasted_iota(jnp.int32, (CHUNK, CHUNK), 1)
    return r >= c, r > c, r == c


def _chunk_decay(gcb_h, bb_h, incl):
    G = gcb_h[:, 0:CHUNK]
    diff = G - G.T
    dec = jnp.where(incl, jnp.exp(jnp.where(incl, diff, 0.0)), 0.0)
    return dec, bb_h[:, 0:CHUNK].T


def _gdn_chunk_fwd(qn, kn, v, gcb, bb, *, name):
    S = qn.shape[0]
    nc = S // CHUNK

    def body(qn_ref, kn_ref, v_ref, gcb_ref, bb_ref, uv_ref, wk_ref, at_ref, t_ref):
        incl, strict, diag = _chunk_masks()
        ats, ts = [], []
        for h in range(GDN_HEADS):
            sl = slice(GDN_DIM * h, GDN_DIM * (h + 1))
            q, k, vv, gh = qn_ref[:, sl], kn_ref[:, sl], v_ref[:, sl], gcb_ref[:, sl]
            dec, bt = _chunk_decay(gh, bb_ref[:, sl], incl)
            lmat = jnp.where(strict, dec * _bdot_nt(k, k) * bt, 0.0)
            p = -lmat
            t = jnp.where(diag, 1.0, 0.0) + p
            for _ in range(5):
                p = _bdot(p, p)
                t = t + _bdot(t, p)
            rhs = jnp.concatenate([vv, jnp.exp(gh) * k], axis=1)
            sol = _dot3(t, rhs, ((1,), (0,)))
            uv_ref[:, sl] = sol[:, :GDN_DIM]
            wk_ref[:, sl] = sol[:, GDN_DIM:]
            ats.append(dec * _bdot_nt(q, k) * bt)
            ts.append(t)
        at_ref[...] = jnp.concatenate(ats, axis=1)
        t_ref[...] = jnp.concatenate(ts, axis=1)

    row = pl.BlockSpec((CHUNK, GDN_WIDTH), lambda n: (n, 0))
    sq = pl.BlockSpec((CHUNK, GDN_HEADS * CHUNK), lambda n: (n, 0))
    return pl.pallas_call(
        body, name=name, grid=(nc,), in_specs=[row] * 5, out_specs=[row, row, sq, sq],
        out_shape=[_sds((S, GDN_WIDTH)), _sds((S, GDN_WIDTH)), _sds((S, GDN_HEADS * CHUNK)), _sds((S, GDN_HEADS * CHUNK))],
        compiler_params=_params(("parallel",)),
    )(qn, kn, v, gcb, bb)


def _gdn_scan_fwd(uv, wk, at, qn, kn, gcb, bb, proj_z, gnw, *, name):
    S = uv.shape[0]
    nc = S // CHUNK

    def body(uv_ref, wk_ref, at_ref, qn_ref, kn_ref, gcb_ref, bb_ref, z_ref, gnw_ref, o_ref, u_ref, sp_ref, oa_ref, st_ref):
        n = pl.program_id(0)

        @pl.when(n == 0)
        def _():
            st_ref[...] = jnp.zeros_like(st_ref)

        oas = []
        for h in range(GDN_HEADS):
            sl = slice(GDN_DIM * h, GDN_DIM * (h + 1))
            st = st_ref[h]
            sp_ref[sl, :] = st
            gh = gcb_ref[:, sl]
            glast = gcb_ref[CHUNK - 1:CHUNK, sl]
            u = uv_ref[:, sl] - _bdot(wk_ref[:, sl], st)
            o = _bdot(qn_ref[:, sl] * jnp.exp(gh), st) + _bdot(at_ref[:, CHUNK * h:CHUNK * (h + 1)], u)
            ke = kn_ref[:, sl] * jnp.exp(glast - gh) * bb_ref[:, sl]
            st_ref[h] = jnp.exp(glast) * st + _bdot_tn(ke, u)
            u_ref[:, sl] = u
            o_ref[:, sl] = o
            z = z_ref[:, sl]
            r = lax.rsqrt(jnp.mean(o * o, axis=-1, keepdims=True) + EPS)
            oas.append(o * r * gnw_ref[...] * (z * _sigmoid(z)))
        oa_ref[...] = jnp.concatenate(oas, axis=1).astype(BF16)

    row = pl.BlockSpec((CHUNK, GDN_WIDTH), lambda n: (n, 0))
    sq = pl.BlockSpec((CHUNK, GDN_HEADS * CHUNK), lambda n: (n, 0))
    return pl.pallas_call(
        body, name=name, grid=(nc,),
        in_specs=[row, row, sq, row, row, row, row, row, pl.BlockSpec((1, GDN_DIM), lambda n: (0, 0))],
        out_specs=[row, row, pl.BlockSpec((GDN_WIDTH, GDN_DIM), lambda n: (n, 0)), row],
        out_shape=[_sds((S, GDN_WIDTH)), _sds((S, GDN_WIDTH)), _sds((nc * GDN_WIDTH, GDN_DIM)), _sds((S, 2 * GDN_WIDTH), BF16)],
        scratch_shapes=[pltpu.VMEM((GDN_HEADS, GDN_DIM, GDN_DIM), F32)],
        compiler_params=_params(("arbitrary",)),
    )(uv, wk, at, qn, kn, gcb, bb, proj_z, gnw)


def _gdn_scan_bwd(d_oab, o, proj_z, gnw, sp, u, wk, at, qn, kn, gcb, bb, *, name):
    S = o.shape[0]
    nc = S // CHUNK

    def body(do_ref, o_ref, z_ref, gnw_ref, sp_ref, u_ref, wk_ref, at_ref, qn_ref, kn_ref, gcb_ref, bb_ref,
             dz_ref, dgn_ref, du_ref, dwk_ref, dat_ref, dqd_ref, dke_ref, dgl_ref, ds_ref):
        n = pl.program_id(0)

        @pl.when(n == 0)
        def _():
            ds_ref[...] = jnp.zeros_like(ds_ref)
            dgn_ref[...] = jnp.zeros_like(dgn_ref)

        dats, dgn = [], jnp.zeros((1, GDN_DIM), F32)
        for h in range(GDN_HEADS):
            sl = slice(GDN_DIM * h, GDN_DIM * (h + 1))
            oo = o_ref[:, sl]
            z = z_ref[:, sl]
            gw = gnw_ref[...]
            sg = _sigmoid(z)
            r = lax.rsqrt(jnp.mean(oo * oo, axis=-1, keepdims=True) + EPS)
            xh = oo * r
            d_oa = do_ref[:, sl]
            dy = d_oa * (z * sg)
            dz_ref[:, sl] = (d_oa * (xh * gw) * _dsilu(z, sg)).astype(BF16)
            dgn = dgn + jnp.sum(dy * xh, axis=0, keepdims=True)
            dxh = dy * gw
            do = r * (dxh - xh * jnp.mean(dxh * xh, axis=-1, keepdims=True))

            st = sp_ref[sl, :]
            dst = ds_ref[h]
            gh = gcb_ref[:, sl]
            glast = gcb_ref[CHUNK - 1:CHUNK, sl]
            uu = u_ref[:, sl]
            ath = at_ref[:, CHUNK * h:CHUNK * (h + 1)]
            qd = qn_ref[:, sl] * jnp.exp(gh)
            ke = kn_ref[:, sl] * jnp.exp(glast - gh) * bb_ref[:, sl]
            ge = jnp.exp(glast)
            dqd_ref[:, sl] = _bdot_nt(do, st)
            dats.append(_bdot_nt(do, uu))
            du = _bdot_tn(ath, do) + _bdot(ke, dst)
            dke_ref[:, sl] = _bdot_nt(uu, dst)
            dge = jnp.sum(jnp.sum(dst * st, axis=1, keepdims=True), axis=0, keepdims=True)
            dgl_ref[0, :, sl] = jnp.broadcast_to(dge * ge, (8, GDN_DIM))
            ds_ref[h] = _bdot_tn(qd, do) + ge * dst - _bdot_tn(wk_ref[:, sl], du)
            du_ref[:, sl] = du
            dwk_ref[:, sl] = -_bdot_nt(du, st)
        dat_ref[...] = jnp.concatenate(dats, axis=1)
        dgn_ref[...] += dgn

    rev = lambda n: (nc - 1 - n, 0)
    row = pl.BlockSpec((CHUNK, GDN_WIDTH), rev)
    sq = pl.BlockSpec((CHUNK, GDN_HEADS * CHUNK), rev)
    one = pl.BlockSpec((1, GDN_DIM), lambda n: (0, 0))
    return pl.pallas_call(
        body, name=name, grid=(nc,),
        in_specs=[row, row, row, one, pl.BlockSpec((GDN_WIDTH, GDN_DIM), rev), row, row, sq, row, row, row, row],
        out_specs=[row, one, row, row, sq, row, row, pl.BlockSpec((1, 8, GDN_WIDTH), lambda n: (nc - 1 - n, 0, 0))],
        out_shape=[_sds((S, GDN_WIDTH), BF16), _sds((1, GDN_DIM)), _sds((S, GDN_WIDTH)), _sds((S, GDN_WIDTH)),
                   _sds((S, GDN_HEADS * CHUNK)), _sds((S, GDN_WIDTH)), _sds((S, GDN_WIDTH)), _sds((nc, 8, GDN_WIDTH))],
        scratch_shapes=[pltpu.VMEM((GDN_HEADS, GDN_DIM, GDN_DIM), F32)],
        compiler_params=_params(("arbitrary",)),
    )(d_oab, o, proj_z, gnw, sp, u, wk, at, qn, kn, gcb, bb)


def _gdn_chunk_bwd(qn, kn, v, gcb, bb, tmat, uv, wk, du, dwk, dat, dqd, dke, dgl, *, name):
    S = qn.shape[0]
    nc = S // CHUNK

    def body(qn_ref, kn_ref, v_ref, gcb_ref, bb_ref, t_ref, uv_ref, wk_ref, du_ref, dwk_ref, dat_ref, dqd_ref, dke_ref,
             dgl_ref, dq_ref, dk_ref, dv_ref, dg_ref, dbeta_ref):
        incl, strict, _ = _chunk_masks()
        lane = lax.broadcasted_iota(jnp.int32, (CHUNK, 128), 1)
        rowi = lax.broadcasted_iota(jnp.int32, (CHUNK, 1), 0)
        dgc4 = jnp.zeros((CHUNK, 128), F32)
        db4 = jnp.zeros((CHUNK, 128), F32)
        for h in range(GDN_HEADS):
            sl = slice(GDN_DIM * h, GDN_DIM * (h + 1))
            sq = slice(CHUNK * h, CHUNK * (h + 1))
            q, k, gh, bh = qn_ref[:, sl], kn_ref[:, sl], gcb_ref[:, sl], bb_ref[:, sl]
            dec, bt = _chunk_decay(gh, bh, incl)
            kk = _bdot_nt(k, k)
            qk = _bdot_nt(q, k)
            t = t_ref[:, sq]
            d_sol = jnp.concatenate([du_ref[:, sl], dwk_ref[:, sl]], axis=1)
            d_rhs = _dot3(t, d_sol, ((0,), (0,)))
            sol = jnp.concatenate([uv_ref[:, sl], wk_ref[:, sl]], axis=1)
            d_l = jnp.where(strict, -_dot3(d_rhs, sol, ((1,), (1,))), 0.0)
            d_a = jnp.where(incl, dat_ref[:, sq], 0.0)
            gam = jnp.exp(gh)
            glast = gcb_ref[CHUNK - 1:CHUNK, sl]
            e = jnp.exp(glast - gh)
            d_gk = d_rhs[:, GDN_DIM:]
            dqd = dqd_ref[:, sl]
            dke = dke_ref[:, sl]
            ml = d_l * dec * bt
            ma = d_a * dec * bt
            dq_ref[:, sl] = _bdot(ma, k) + dqd * gam
            dk_ref[:, sl] = (_bdot(ml, k) + _bdot_tn(ml, k) + _bdot_tn(ma, q)) + d_gk * gam + dke * (e * bh)
            dv_ref[:, sl] = d_rhs[:, :GDN_DIM]
            wb = d_l * dec * kk + d_a * dec * qk
            ew = wb * bt
            s_ke = jnp.sum(dke * k * (e * bh), axis=-1, keepdims=True)
            dbeta = jnp.sum(wb.T, axis=-1, keepdims=True) + jnp.sum(dke * k * e, axis=-1, keepdims=True)
            dgc = (jnp.sum(ew, axis=-1, keepdims=True) - jnp.sum(ew.T, axis=-1, keepdims=True)
                   + jnp.sum(dqd * q * gam, axis=-1, keepdims=True) + jnp.sum(d_gk * k * gam, axis=-1, keepdims=True) - s_ke)
            tail = jnp.sum(s_ke, axis=0, keepdims=True) + dgl_ref[0, 0:1, GDN_DIM * h:GDN_DIM * h + 1]
            dgc = dgc + jnp.where(rowi == CHUNK - 1, tail, 0.0)
            dgc4 = jnp.where(lane == h, dgc, dgc4)
            db4 = jnp.where(lane == h, dbeta, db4)
        dg_ref[...] = _exact_tri_dot(_chunk_tri(CHUNK, upper=True), dgc4)
        dbeta_ref[...] = db4

    row = pl.BlockSpec((CHUNK, GDN_WIDTH), lambda n: (n, 0))
    sq = pl.BlockSpec((CHUNK, GDN_HEADS * CHUNK), lambda n: (n, 0))
    col = pl.BlockSpec((CHUNK, 128), lambda n: (n, 0))
    return pl.pallas_call(
        body, name=name, grid=(nc,),
        in_specs=[row] * 5 + [sq, row, row, row, row, sq, row, row, pl.BlockSpec((1, 8, GDN_WIDTH), lambda n: (n, 0, 0))],
        out_specs=[row, row, row, col, col],
        out_shape=[_sds((S, GDN_WIDTH))] * 3 + [_sds((S, 128))] * 2, compiler_params=_params(("parallel",)),
    )(qn, kn, v, gcb, bb, tmat, uv, wk, du, dwk, dat, dqd, dke, dgl)


def _gdn_prep_bwd(dqn, dkn, dv, dg, dbeta, proj_a, conv_w, a_log, dt_bias, *, name, tm=256):
    S = proj_a.shape[0]
    nblk = S // tm
    W3 = 3 * GDN_WIDTH

    def body(dqn_ref, dkn_ref, dv_ref, dg_ref, dbeta_ref, cur_ref, prev_ref, ba_ref, cw_ref, al_ref, dt_ref,
             dc_ref, dba_ref, sm_ref):
        i = pl.program_id(0)
        prev = jnp.where(i > 0, prev_ref[...], 0.0)
        c = _conv_rows(prev, cur_ref[...], cw_ref[...], GDN_CONV)
        sg = _sigmoid(c)
        a = c * sg
        dsl = _dsilu(c, sg)
        ba = ba_ref[...]
        lane = lax.broadcasted_iota(jnp.int32, (tm, 128), 1)
        lane1 = lax.broadcasted_iota(jnp.int32, (1, 128), 1)
        dba = jnp.zeros((tm, 128), F32)
        sm = jnp.zeros((1, 128), F32)
        for h in range(GDN_HEADS):
            sl = slice(GDN_DIM * h, GDN_DIM * (h + 1))
            ks = slice(GDN_WIDTH + GDN_DIM * h, GDN_WIDTH + GDN_DIM * (h + 1))
            qh, kh = a[:, sl], a[:, ks]
            rq = lax.rsqrt(jnp.sum(qh * qh, axis=-1, keepdims=True) + EPS)
            rk = lax.rsqrt(jnp.sum(kh * kh, axis=-1, keepdims=True) + EPS)
            qhat, khat = qh * rq, kh * rk
            dyq = dqn_ref[:, sl] * (GDN_DIM ** -0.5)
            dyk = dkn_ref[:, sl]
            dq = rq * (dyq - qhat * jnp.sum(dyq * qhat, axis=-1, keepdims=True))
            dk = rk * (dyk - khat * jnp.sum(dyk * khat, axis=-1, keepdims=True))
            dc_ref[:, sl] = dq * dsl[:, sl]
            dc_ref[:, ks] = dk * dsl[:, ks]
            beta = _sigmoid(ba[:, h:h + 1])
            db = dbeta_ref[:, h:h + 1] * beta * (1.0 - beta)
            aneg = -jnp.exp(al_ref[0:1, h:h + 1])
            xa = ba[:, GDN_HEADS + h:GDN_HEADS + h + 1] + dt_ref[0:1, h:h + 1]
            dgh = dg_ref[:, h:h + 1]
            dxa = dgh * aneg * _sigmoid(xa)
            dba = jnp.where(lane == h, db, dba)
            dba = jnp.where(lane == GDN_HEADS + h, dxa, dba)
            d_alog = jnp.sum(dgh * _softplus(xa), axis=0, keepdims=True) * aneg
            sm = jnp.where(lane1 == h, d_alog, sm)
            sm = jnp.where(lane1 == GDN_HEADS + h, jnp.sum(dxa, axis=0, keepdims=True), sm)
        vs = slice(2 * GDN_WIDTH, W3)
        dc_ref[:, vs] = dv_ref[...] * dsl[:, vs]
        dba_ref[...] = dba

        @pl.when(i == 0)
        def _():
            sm_ref[...] = sm

        @pl.when(i > 0)
        def _():
            sm_ref[...] += sm

    prev_spec, _ = _halo_specs(tm, W3, 0, nblk)
    row = pl.BlockSpec((tm, GDN_WIDTH), lambda i: (i, 0))
    col = pl.BlockSpec((tm, 128), lambda i: (i, 0))
    small = lambda a: pl.BlockSpec(a.shape, lambda i: (0, 0))
    return pl.pallas_call(
        body, name=name, grid=(nblk,),
        in_specs=[row, row, row, col, col, pl.BlockSpec((tm, W3), lambda i: (i, 0)), prev_spec,
                  pl.BlockSpec((tm, 128), lambda i: (i, W3 // 128)), small(conv_w), small(a_log), small(dt_bias)],
        out_specs=[pl.BlockSpec((tm, W3), lambda i: (i, 0)), col, pl.BlockSpec((1, 128), lambda i: (0, 0))],
        out_shape=[_sds((S, W3)), _sds((S, 128)), _sds((1, 128))], compiler_params=_params(("arbitrary",)),
    )(dqn, dkn, dv, dg, dbeta, proj_a, proj_a, proj_a, conv_w, a_log, dt_bias)


def _gdn_conv_bwd(dc, dba, proj_a, conv_w, *, name, tm=256):
    S = proj_a.shape[0]
    nblk = S // tm
    W3 = 3 * GDN_WIDTH

    def body(dc_ref, dnext_ref, dba_ref, cur_ref, prev_ref, cw_ref, da_ref, dcw_ref):
        i = pl.program_id(0)
        prev = jnp.where(i > 0, prev_ref[...], 0.0)
        nxt = jnp.where(i < nblk - 1, dnext_ref[...], 0.0)
        dx, dw = _conv_rows_bwd(dc_ref[...], nxt, prev, cur_ref[...], cw_ref[...], GDN_CONV)
        da_ref[:, 0:W3] = dx.astype(BF16)
        da_ref[:, W3:] = dba_ref[...].astype(BF16)

        @pl.when(i == 0)
        def _():
            dcw_ref[...] = dw

        @pl.when(i > 0)
        def _():
            dcw_ref[...] += dw

    prev_spec, next_spec = _halo_specs(tm, W3, 0, nblk)
    wide = pl.BlockSpec((tm, W3), lambda i: (i, 0))
    return pl.pallas_call(
        body, name=name, grid=(nblk,),
        in_specs=[wide, next_spec, pl.BlockSpec((tm, 128), lambda i: (i, 0)), wide, prev_spec,
                  pl.BlockSpec(conv_w.shape, lambda i: (0, 0))],
        out_specs=[pl.BlockSpec((tm, A_COLS), lambda i: (i, 0)), pl.BlockSpec(conv_w.shape, lambda i: (0, 0))],
        out_shape=[_sds((S, A_COLS), BF16), _sds(conv_w.shape)], compiler_params=_params(("arbitrary",)),
    )(dc, dc, dba, proj_a, proj_a, conv_w)


def _band_mask(nk):
    i = lax.broadcasted_iota(jnp.int32, (BAND, nk), 0)
    j = lax.broadcasted_iota(jnp.int32, (BAND, nk), 1)
    if nk == BAND:
        return j <= i
    return (j >= i) & (j <= i + BAND)


def _rows(start, size, stride):
    return pl.ds(start, size) if stride == 1 else pl.ds(start, size, stride=stride)


def _attn_blocks(S, visit):
    for d in DILATIONS:
        nb = S // (d * BAND)

        def per_residue(r, carry, d=d, nb=nb):
            visit(d, r, 0, True)
            if nb > 1:
                def per_block(n, c):
                    visit(d, r, n, False)
                    return c
                lax.fori_loop(1, nb, per_block, 0)
            return carry

        if d == 1:
            per_residue(0, 0)
        else:
            lax.fori_loop(0, d, per_residue, 0)


def _attn_fwd(proj_b, oab, *, name):
    S = proj_b.shape[0]
    scale = DIL_DIM ** -0.5

    def body(q_ref, k_ref, v_ref, oab_in_ref, ob_ref, lse_ref, m_ref, l_ref, acc_ref):
        del oab_in_ref
        lane = lax.broadcasted_iota(jnp.int32, (BAND, 128), 1)
        lo = lane < DIL_DIM
        m_ref[...] = jnp.full_like(m_ref, NEG_BIG)
        l_ref[...] = jnp.zeros_like(l_ref)
        acc_ref[...] = jnp.zeros_like(acc_ref)

        def visit(d, r, n, first):
            nk = BAND if first else 2 * BAND
            qs = r + n * (BAND * d)
            ks = r if first else r + (n - 1) * (BAND * d)
            qrows = _rows(qs, BAND, d)
            krows = _rows(ks, nk, d)
            q = q_ref[qrows, :] * scale
            k = k_ref[krows, :].astype(BF16)
            v = v_ref[krows, :].astype(BF16)
            valid = _band_mask(nk)
            s0 = jnp.where(valid, _bdot_nt(jnp.where(lo, q, 0.0), k), NEG_BIG)
            s1 = jnp.where(valid, _bdot_nt(jnp.where(lo, 0.0, q), k), NEG_BIG)
            m_old = m_ref[qrows, :]
            mb = jnp.where(lo, jnp.max(s0, axis=-1, keepdims=True), jnp.max(s1, axis=-1, keepdims=True))
            m_new = jnp.maximum(m_old, mb)
            alpha = jnp.exp(m_old - m_new)
            p0 = jnp.exp(s0 - m_new[:, 0:1])
            p1 = jnp.exp(s1 - m_new[:, DIL_DIM:DIL_DIM + 1])
            psum = jnp.where(lo, jnp.sum(p0, axis=-1, keepdims=True), jnp.sum(p1, axis=-1, keepdims=True))
            pv = jnp.where(lo, _bdot(p0, v), _bdot(p1, v))
            m_ref[qrows, :] = m_new
            l_ref[qrows, :] = alpha * l_ref[qrows, :] + psum
            acc_ref[qrows, :] = alpha * acc_ref[qrows, :] + pv

        _attn_blocks(S, visit)
        ob_ref[...] = (acc_ref[...] / l_ref[...]).astype(BF16)
        lse_ref[...] = m_ref[...] + jnp.log(l_ref[...])

    part = lambda t: pl.BlockSpec((S, 128), lambda p: (0, 3 * p + t))
    return pl.pallas_call(
        body, name=name, grid=(DIL_PAIRS,),
        in_specs=[part(0), part(1), part(2), pl.BlockSpec(memory_space=pl.ANY)],
        out_specs=[pl.BlockSpec((S, 128), lambda p: (0, GDN_WIDTH // 128 + p)), pl.BlockSpec((S, 128), lambda p: (0, p))],
        out_shape=[_sds(oab.shape, BF16), _sds((S, DIL_WIDTH))],
        scratch_shapes=[pltpu.VMEM((S, 128), F32)] * 3, input_output_aliases={3: 0},
        compiler_params=_params(("parallel",)),
    )(proj_b, proj_b, proj_b, oab)


def _attn_bwd(proj_b, oab, d_oab, lse, *, name):
    S = proj_b.shape[0]
    scale = DIL_DIM ** -0.5

    def body(q_ref, k_ref, v_ref, o_ref, do_ref, lse_ref, dqkv_ref, dq_ref, dk_ref, dv_ref, delta_ref):
        lane = lax.broadcasted_iota(jnp.int32, (BAND, 128), 1)
        lo = lane < DIL_DIM
        dq_ref[...] = jnp.zeros_like(dq_ref)
        dk_ref[...] = jnp.zeros_like(dk_ref)
        dv_ref[...] = jnp.zeros_like(dv_ref)
        prod = do_ref[...] * o_ref[...].astype(F32)
        lo_all = lax.broadcasted_iota(jnp.int32, (S, 128), 1) < DIL_DIM
        delta_ref[...] = jnp.where(lo_all, jnp.sum(jnp.where(lo_all, prod, 0.0), axis=-1, keepdims=True),
                                   jnp.sum(jnp.where(lo_all, 0.0, prod), axis=-1, keepdims=True))

        def visit(d, r, n, first):
            nk = BAND if first else 2 * BAND
            qs = r + n * (BAND * d)
            ks = r if first else r + (n - 1) * (BAND * d)
            qrows = _rows(qs, BAND, d)
            krows = _rows(ks, nk, d)
            q = q_ref[qrows, :] * scale
            k = k_ref[krows, :]
            v = v_ref[krows, :]
            do = do_ref[qrows, :]
            delta_b = delta_ref[qrows, :]
            lse_b = lse_ref[qrows, :]
            valid = _band_mask(nk)
            lkl = lax.broadcasted_iota(jnp.int32, (nk, 128), 1) < DIL_DIM
            dq = jnp.zeros((BAND, 128), F32)
            dk = jnp.zeros((nk, 128), F32)
            dv = jnp.zeros((nk, 128), F32)
            for sel, ksel, lcol in ((lo, lkl, 0), (~lo, ~lkl, DIL_DIM)):
                qh = jnp.where(sel, q, 0.0)
                doh = jnp.where(sel, do, 0.0)
                s = _bdot_nt(qh, k)
                p = jnp.where(valid, jnp.exp(s - lse_b[:, lcol:lcol + 1]), 0.0)
                dp = _bdot_nt(doh, v)
                ds = p * (dp - delta_b[:, lcol:lcol + 1])
                dq = dq + _bdot(ds, jnp.where(ksel, k, 0.0))
                dk = dk + _bdot_tn(ds, qh)
                dv = dv + _bdot_tn(p, doh)
            dq_ref[qrows, :] += dq * scale
            dk_ref[krows, :] += dk
            dv_ref[krows, :] += dv

        _attn_blocks(S, visit)
        dqkv_ref[:, 0:128] = dq_ref[...].astype(BF16)
        dqkv_ref[:, 128:256] = dk_ref[...].astype(BF16)
        dqkv_ref[:, 256:384] = dv_ref[...].astype(BF16)

    half = lambda p: (0, GDN_WIDTH // 128 + p)
    part = lambda t: pl.BlockSpec((S, 128), lambda p: (0, 3 * p + t))
    return pl.pallas_call(
        body, name=name, grid=(DIL_PAIRS,),
        in_specs=[part(0), part(1), part(2), pl.BlockSpec((S, 128), half), pl.BlockSpec((S, 128), half),
                  pl.BlockSpec((S, 128), lambda p: (0, p))],
        out_specs=pl.BlockSpec((S, 384), lambda p: (0, p)), out_shape=_sds((S, 3 * DIL_WIDTH), BF16),
        scratch_shapes=[pltpu.VMEM((S, 128), F32)] * 4, compiler_params=_params(("parallel",)),
    )(proj_b, proj_b, proj_b, oab, d_oab, lse)


def _ffn_act(u_pre, conv_w, *, name, tm=256, tc=256):
    S = u_pre.shape[0]
    nblk, ncol = S // tm, D_FF // tc

    def body(g_ref, gp_ref, u_ref, up_ref, wg_ref, wu_ref, act_ref):
        i = pl.program_id(1)
        zero = lambda ref: jnp.where(i > 0, ref[...], 0.0)
        gate = _conv_rows(zero(gp_ref), g_ref[...], wg_ref[...], FFN_CONV)
        up = _conv_rows(zero(up_ref), u_ref[...], wu_ref[...], FFN_CONV)
        act_ref[...] = (gate * _sigmoid(gate) * up).astype(BF16)

    per = tm // HALO
    prev = lambda off: pl.BlockSpec((HALO, tc), lambda j, i: (jnp.maximum(i * per - 1, 0), j + off))
    cur = lambda off: pl.BlockSpec((tm, tc), lambda j, i: (i, j + off))
    wsp = lambda off: pl.BlockSpec((FFN_CONV, tc), lambda j, i: (0, j + off))
    return pl.pallas_call(
        body, name=name, grid=(ncol, nblk),
        in_specs=[cur(0), prev(0), cur(ncol), prev(ncol), wsp(0), wsp(ncol)],
        out_specs=pl.BlockSpec((tm, tc), lambda j, i: (i, j)), out_shape=_sds((S, D_FF), BF16),
        compiler_params=_params(("parallel", "parallel")),
    )(u_pre, u_pre, u_pre, u_pre, conv_w, conv_w)


def _ffn_act_bwd(d_act, u_pre, conv_w, *, name, tm=256, tc=256):
    S = u_pre.shape[0]
    nblk, ncol = S // tm, D_FF // tc

    def body(da_ref, g_ref, gp_ref, u_ref, up_ref, wg_ref, wu_ref, dg_ref, du_ref):
        i = pl.program_id(1)
        zero = lambda ref: jnp.where(i > 0, ref[...], 0.0)
        gate = _conv_rows(zero(gp_ref), g_ref[...], wg_ref[...], FFN_CONV)
        up = _conv_rows(zero(up_ref), u_ref[...], wu_ref[...], FFN_CONV)
        sg = _sigmoid(gate)
        da = da_ref[...]
        dg_ref[...] = (da * up * _dsilu(gate, sg)).astype(BF16)
        du_ref[...] = (da * gate * sg).astype(BF16)

    per = tm // HALO
    prev = lambda off: pl.BlockSpec((HALO, tc), lambda j, i: (jnp.maximum(i * per - 1, 0), j + off))
    cur = lambda off: pl.BlockSpec((tm, tc), lambda j, i: (i, j + off))
    wsp = lambda off: pl.BlockSpec((FFN_CONV, tc), lambda j, i: (0, j + off))
    outs = pl.pallas_call(
        body, name=name, grid=(ncol, nblk),
        in_specs=[cur(0), cur(0), prev(0), cur(ncol), prev(ncol), wsp(0), wsp(ncol)],
        out_specs=[cur(0), cur(0)], out_shape=[_sds((S, D_FF), BF16)] * 2,
        compiler_params=_params(("parallel", "parallel")),
    )(d_act, u_pre, u_pre, u_pre, u_pre, conv_w, conv_w)
    return outs


def _ffn_conv_bwd(dc, u_pre, conv_w, col_off, *, name, tm=256, tc=256):
    S = u_pre.shape[0]
    nblk, ncol = S // tm, D_FF // tc
    off = col_off // tc

    def body(dc_ref, dn_ref, x_ref, xp_ref, w_ref, dx_ref, dw_ref):
        i = pl.program_id(1)
        prev = jnp.where(i > 0, xp_ref[...], 0.0)
        nxt = jnp.where(i < nblk - 1, dn_ref[...].astype(F32), 0.0)
        dx, dw = _conv_rows_bwd(dc_ref[...].astype(F32), nxt[0:HALO], prev, x_ref[...], w_ref[...], FFN_CONV)
        dx_ref[...] = dx.astype(BF16)

        @pl.when(i == 0)
        def _():
            dw_ref[...] = dw

        @pl.when(i > 0)
        def _():
            dw_ref[...] += dw

    per = tm // HALO
    hb = 2 * HALO
    perb = tm // hb
    return pl.pallas_call(
        body, name=name, grid=(ncol, nblk),
        in_specs=[pl.BlockSpec((tm, tc), lambda j, i: (i, j)),
                  pl.BlockSpec((hb, tc), lambda j, i: (jnp.minimum((i + 1) * perb, nblk * perb - 1), j)),
                  pl.BlockSpec((tm, tc), lambda j, i: (i, j + off)),
                  pl.BlockSpec((HALO, tc), lambda j, i: (jnp.maximum(i * per - 1, 0), j + off)),
                  pl.BlockSpec((FFN_CONV, tc), lambda j, i: (0, j + off))],
        out_specs=[pl.BlockSpec((tm, tc), lambda j, i: (i, j)), pl.BlockSpec((FFN_CONV, tc), lambda j, i: (0, j))],
        out_shape=[_sds((S, D_FF), BF16), _sds((FFN_CONV, D_FF))],
        compiler_params=_params(("parallel", "arbitrary")),
    )(dc, dc, u_pre, u_pre, conv_w)


def _local_step(x, tgt, norm1_w, w_a, w_z, w_b, conv_a, a_log, dt_bias, gnw, w_out, norm2_w, w_up, conv_f, w_down, final_w, emit):
    wgrad = functools.partial(_mm, ta=True, out_dtype=BF16)
    h1 = _rms_fwd(x, norm1_w, name="rms1_fwd")
    proj_a = _mm(h1, w_a, name="proj_a", tn=A_COLS, tk=1024)
    proj_z = _mm(h1, w_z, name="proj_z", tk=1024)
    proj_b = _mm(h1, w_b, name="proj_b", tn=768, tk=1024)
    qn, kn, v, gcb, bb = _gdn_prep_fwd(proj_a, conv_a, a_log, dt_bias, name="gdn_prep_fwd")
    uv, wk, at, tmat = _gdn_chunk_fwd(qn, kn, v, gcb, bb, name="gdn_chunk_fwd")
    o, u, sp, oab = _gdn_scan_fwd(uv, wk, at, qn, kn, gcb, bb, proj_z, gnw, name="gdn_scan_fwd")
    oab, lse = _attn_fwd(proj_b, oab, name="attn_fwd")
    x1 = _mm(oab, w_out, res=x, name="out_proj", tk=1024)
    h2 = _rms_fwd(x1, norm2_w, name="rms2_fwd")
    u_pre = _mm(h2, w_up, name="ffn_up", tk=1024)
    act = _ffn_act(u_pre, conv_f, name="ffn_act")
    x2 = _mm(act, w_down, res=x1, name="ffn_down", tk=1408)
    dx2, d_final, loss = _loss_head(x2, final_w, tgt, name="loss_head")
    d_act = _mm(dx2, w_down, tb=True, name="ffn_down_dx", tn=1408, tk=1024)
    emit("down", w_down=wgrad(act, dx2, name="ffn_down_dw", tm=1408, tk=512))
    dgate, dup = _ffn_act_bwd(d_act, u_pre, conv_f, name="ffn_act_bwd")
    du_g, dcw_g = _ffn_conv_bwd(dgate, u_pre, conv_f, 0, name="ffn_conv_bwd_gate")
    du_u, dcw_u = _ffn_conv_bwd(dup, u_pre, conv_f, D_FF, name="ffn_conv_bwd_up")
    dh2 = _mm(du_g, w_up[:, :D_FF], tb=True, name="ffn_up_dx_gate", tk=1408)
    dh2 = _mm(du_u, w_up[:, D_FF:], tb=True, res=dh2, name="ffn_up_dx_up", tk=1408)
    g_up_g = wgrad(h2, du_g, name="ffn_up_dw_gate", tn=1408)
    g_up_u = wgrad(h2, du_u, name="ffn_up_dw_up", tn=1408)
    emit("up", w_up=jnp.concatenate([g_up_g, g_up_u], axis=1), conv_f=jnp.concatenate([dcw_g, dcw_u], axis=1))
    dx1, d_norm2 = _rms_bwd(dh2, x1, norm2_w, dx2, name="rms2_bwd")
    d_oab = _mm(dx1, w_out, tb=True, name="out_proj_dx", tk=1024)
    emit("out", w_out=wgrad(oab, dx1, name="out_proj_dw"))
    dz, d_gnw, du, dwk, dat, dqd, dke, dgl = _gdn_scan_bwd(d_oab, o, proj_z, gnw, sp, u, wk, at, qn, kn, gcb, bb, name="gdn_scan_bwd")
    dqn, dkn, dv, dg, dbeta = _gdn_chunk_bwd(qn, kn, v, gcb, bb, tmat, uv, wk, du, dwk, dat, dqd, dke, dgl, name="gdn_chunk_bwd")
    dc, dba, d_small = _gdn_prep_bwd(dqn, dkn, dv, dg, dbeta, proj_a, conv_a, a_log, dt_bias, name="gdn_prep_bwd")
    d_pa, d_conv_a = _gdn_conv_bwd(dc, dba, proj_a, conv_a, name="gdn_conv_bwd")
    d_pb = _attn_bwd(proj_b, oab, d_oab, lse, name="attn_bwd")
    g_a = wgrad(h1, d_pa, name="proj_a_dw", tn=A_COLS)
    g_z = wgrad(h1, dz, name="proj_z_dw")
    g_b = wgrad(h1, d_pb, name="proj_b_dw", tn=768)
    emit("in", w_a=g_a, w_z=g_z, w_b=g_b, conv_a=d_conv_a)
    dh1 = _mm(d_pa, w_a, tb=True, name="proj_a_dx", tk=A_COLS)
    dh1 = _mm(dz, w_z, tb=True, res=dh1, name="proj_z_dx")
    dh1 = _mm(d_pb, w_b, tb=True, res=dh1, name="proj_b_dx", tk=1536)
    grad_x, d_norm1 = _rms_bwd(dh1, x, norm1_w, dx1, name="rms1_bwd")
    small = dict(norm1=d_norm1, small=d_small, gnw=d_gnw, norm2=d_norm2, final=d_final)
    return loss, grad_x, small


_O1 = 3 * GDN_WIDTH
_O2 = _O1 + GDN_WIDTH
_O3 = _O2 + 2 * GDN_HEADS


def _split_w_in(w_in):
    d = w_in.shape[0]
    pad = jnp.zeros((d, A_COLS - _O1 - 2 * GDN_HEADS), w_in.dtype)
    w_a = jnp.concatenate([w_in[:, :_O1], w_in[:, _O2:_O3], pad], axis=1)
    w_b = w_in[:, _O3:].reshape(d, 3, DIL_PAIRS, 128).transpose(0, 2, 1, 3).reshape(d, 3 * DIL_WIDTH)
    return w_a, w_in[:, _O1:_O2], w_b


def _merge_g_in(g_a, g_z, g_b):
    d = g_a.shape[0]
    g_b = g_b.reshape(d, DIL_PAIRS, 3, 128).transpose(0, 2, 1, 3).reshape(d, 3 * DIL_WIDTH)
    return jnp.concatenate([g_a[:, :_O1], g_z, g_a[:, _O1:_O1 + 2 * GDN_HEADS], g_b], axis=1)


MESH = pl.DeviceIdType.MESH
ANY = pl.BlockSpec(memory_space=pl.ANY)


def _position():
    return lax.axis_index("x"), lax.axis_index("y"), lax.axis_index("c")


def _slot(p):
    return 4 * p[0] + 2 * p[1] + p[2]


def _all_gather(blocks, *, name):
    n = len(blocks)

    def body(*refs):
        ins, outs = refs[:n], refs[n:2 * n]
        send_sems, recv_sems, local_sems = refs[2 * n:]
        x, y, c = _position()
        me, sibling = (x, y, c), (x, y, 1 - c)
        chips = [(1 - x, y), (x, 1 - y), (1 - x, 1 - y)]

        def copy(a, k, block, to, src=None):
            dst = outs[a].at[_slot(block)]
            return pltpu.make_async_remote_copy(
                src_ref=dst if src is None else src, dst_ref=dst, send_sem=send_sems.at[a, k], recv_sem=recv_sems.at[a, k],
                device_id=to, device_id_type=MESH)

        mine = [pltpu.make_async_copy(ins[a], outs[a].at[_slot(me)], local_sems.at[a]) for a in range(n)]
        for cp in mine:
            cp.start()
        first = []
        for a in range(n):
            first.append(copy(a, 0, me, sibling, src=ins[a]))
            first += [copy(a, 1 + j, me, (*chip, c), src=ins[a]) for j, chip in enumerate(chips)]
        for cp in first:
            cp.start()
        passed = []
        for j, chip in enumerate(chips):
            for a in range(n):
                copy(a, 1 + j, (*chip, c), me).wait_recv()
                fwd = copy(a, 4 + j, (*chip, c), sibling)
                fwd.start()
                passed.append(fwd)
        for a in range(n):
            copy(a, 0, sibling, me).wait_recv()
            for j, chip in enumerate(chips):
                copy(a, 4 + j, (*chip, 1 - c), me).wait_recv()
        for cp in first + passed:
            cp.wait_send()
        for cp in mine:
            cp.wait()

    return pl.pallas_call(
        body, name=name, in_specs=[ANY] * n, out_specs=[ANY] * n,
        out_shape=[_sds((N_DEV,) + b.shape, b.dtype) for b in blocks],
        scratch_shapes=[pltpu.SemaphoreType.DMA((n, 7)), pltpu.SemaphoreType.DMA((n, 7)), pltpu.SemaphoreType.DMA((n,))],
    )(*blocks)


HBM = pl.BlockSpec(memory_space=pltpu.HBM)
SEM = pl.BlockSpec(memory_space=pltpu.SEMAPHORE)
EFFECT = pltpu.SideEffectType.DATAFLOW_SIDE_EFFECTING


def _peer_of(k, x, y, c):
    return (1 - x if k & 4 else x, 1 - y if k & 2 else y, 1 - c if k & 1 else c)


def _flight(a, k):
    return a * (N_DEV - 1) + k - 1


def _exchange_start(slabs, *, name):
    n = len(slabs)

    def body(*refs):
        ins, lands = refs[:n], refs[n:2 * n]
        send_sems, recv_sems = refs[2 * n:2 * n + 2]
        x, y, c = _position()
        me = _slot((x, y, c))
        for k in range(1, N_DEV):
            peer = _peer_of(k, x, y, c)
            for a in range(n):
                pltpu.make_async_remote_copy(
                    src_ref=ins[a].at[_slot(peer)], dst_ref=lands[a].at[me], send_sem=send_sems.at[_flight(a, k)],
                    recv_sem=recv_sems.at[_flight(a, k)], device_id=peer, device_id_type=MESH).start()

    lands = [pltpu.with_memory_space_constraint(jnp.zeros(s.shape, s.dtype), pltpu.HBM) for s in slabs]
    srcs = [pltpu.with_memory_space_constraint(s, pltpu.HBM) for s in slabs]
    outs = pl.pallas_call(
        body, name=name, in_specs=[HBM] * (2 * n), out_specs=[SEM, SEM] + [HBM] * (2 * n),
        out_shape=[pltpu.SemaphoreType.DMA((n * (N_DEV - 1),)), pltpu.SemaphoreType.DMA((n * (N_DEV - 1),))]
        + [pltpu.HBM(s.shape, s.dtype) for s in slabs] * 2,
        input_output_aliases={i: 2 + i for i in range(2 * n)},
        compiler_params=pltpu.CompilerParams(has_side_effects=EFFECT),
    )(*srcs, *lands)
    return outs[0], outs[1], outs[2:2 + n], outs[2 + n:]


def _exchange_wait(send_sems, recv_sems, srcs, lands, after, *, name):
    n = len(srcs)

    def body(*refs):
        ins, lnd = refs[:n], refs[n:2 * n]
        send_ref, recv_ref = refs[2 * n:2 * n + 2]
        x, y, c = _position()
        for k in range(1, N_DEV):
            for a in range(n):
                cp = pltpu.make_async_remote_copy(
                    src_ref=ins[a].at[0], dst_ref=lnd[a].at[0], send_sem=send_ref.at[_flight(a, k)], recv_sem=recv_ref.at[_flight(a, k)],
                    device_id=_peer_of(k, x, y, c), device_id_type=MESH)
                cp.wait_send()
                cp.wait_recv()

    outs = pl.pallas_call(
        body, name=name, in_specs=[HBM] * (2 * n) + [SEM, SEM, ANY], out_specs=[HBM] * (2 * n),
        out_shape=[pltpu.HBM(s.shape, s.dtype) for s in srcs] * 2,
        input_output_aliases={i: i for i in range(2 * n)},
        compiler_params=pltpu.CompilerParams(has_side_effects=EFFECT),
    )(*srcs, *lands, send_sems, recv_sems, after)
    return outs[n:]


def _adamw(parts, w, m, v, *, name, own=None, tr=None):
    R, C = w.shape
    tr = R if tr is None else tr
    assert R % tr == 0
    c1 = 1.0 - ADAM_B1 ** ADAM_STEP
    c2 = 1.0 - ADAM_B2 ** ADAM_STEP
    has_own = own is not None

    def body(*refs):
        if has_own:
            own_ref, p_ref, w_ref, m_ref, v_ref, g_ref, d_ref, nm_ref, nv_ref = refs
            g = own_ref[...].astype(F32) + p_ref[0].astype(F32)
        else:
            p_ref, w_ref, m_ref, v_ref, g_ref, d_ref, nm_ref, nv_ref = refs
            g = p_ref[0].astype(F32)
        for s in range(1, N_DEV):
            g = g + p_ref[s].astype(F32)
        nm = ADAM_B1 * m_ref[...] + (1.0 - ADAM_B1) * g
        nv = ADAM_B2 * v_ref[...] + (1.0 - ADAM_B2) * (g * g)
        g_ref[...] = g
        nm_ref[...] = nm
        nv_ref[...] = nv
        d_ref[...] = -ADAM_LR * ((nm / c1) / (jnp.sqrt(nv / c2) + ADAM_EPS) + ADAM_WD * w_ref[...])

    blk = pl.BlockSpec((tr, C), lambda i: (i, 0))
    return pl.pallas_call(
        body, name=name, grid=(R // tr,),
        in_specs=[blk] * has_own + [pl.BlockSpec((N_DEV, tr, C), lambda i: (0, i, 0)), blk, blk, blk],
        out_specs=[blk] * 4, out_shape=[_sds((R, C))] * 4, compiler_params=_params(("parallel",)),
    )(*((own,) if has_own else ()), parts, w, m, v)


_SMALL_ROWS = 8


def _pack_small(norm1, norm2, final, gnw, a_log, dt_bias, loss=None):
    loss = jnp.zeros((1, 128), F32) if loss is None else loss
    row3 = jnp.concatenate([gnw, a_log, dt_bias, jnp.zeros((1, 128 - 2 * GDN_HEADS), F32), loss,
                            jnp.zeros((1, D_MODEL - 3 * 128), F32)], axis=1)
    return jnp.concatenate([norm1, norm2, final, row3, jnp.zeros((_SMALL_ROWS - 4, D_MODEL), F32)], axis=0)


def _unpack_small(p):
    return (p[0:1], p[1:2], p[2], p[3:4, 0:128], p[3:4, 128:128 + GDN_HEADS], p[3:4, 128 + GDN_HEADS:128 + 2 * GDN_HEADS])


def _slabs_by_cols(g):
    r = g.shape[0]
    return g.reshape(r, N_DEV, -1).transpose(1, 0, 2)


def _cols_from_slabs(s):
    return s.transpose(1, 0, 2).reshape(s.shape[1], -1)


def kernel(x, norm1_w, w_in, conv_qkv_w, a_log, dt_bias, gdn_norm_w, w_out, norm2_w, w_up, ffn_conv_w, w_down, final_norm_w, loss_target, m_norm1_w, m_w_in, m_conv_qkv_w, m_a_log, m_dt_bias, m_gdn_norm_w, m_w_out, m_norm2_w, m_w_up, m_ffn_conv_w, m_w_down, m_final_norm_w, v_norm1_w, v_w_in, v_conv_qkv_w, v_a_log, v_dt_bias, v_gdn_norm_w, v_w_out, v_norm2_w, v_w_up, v_ffn_conv_w, v_w_down, v_final_norm_w):
    bf = lambda a: a.astype(BF16)
    gw_in, gw_out, gw_up, gw_down, g_conv_a, g_conv_f = _all_gather(
        [bf(w_in[0]), bf(w_out[0]), bf(w_up[0]), bf(w_down[0]), conv_qkv_w[0], ffn_conv_w[0]], name="gather_weights")
    w_a, w_z, w_b = _split_w_in(_cols_from_slabs(gw_in))
    me = _slot(_position())
    flights = {}

    def emit(group, **grads):
        if group == "in":
            slabs = dict(w_in=_slabs_by_cols(_merge_g_in(grads["w_a"], grads["w_z"], grads["w_b"])),
                         conv_a=_slabs_by_cols(grads["conv_a"]))
        elif group == "up":
            slabs = dict(w_up=_slabs_by_cols(grads["w_up"]), conv_f=_slabs_by_cols(grads["conv_f"]))
        else:
            slabs = {k: v.reshape(N_DEV, -1, D_MODEL) for k, v in grads.items()}
        names = list(slabs)
        own = {k: lax.dynamic_index_in_dim(slabs[k], me, 0, keepdims=False) for k in names}
        flights[group] = (names, own, _exchange_start([slabs[k] for k in names], name="grads_start_" + group))

    loss, grad_x, g = _local_step(
        x[0], loss_target[0], norm1_w, w_a, w_z, w_b, _cols_from_slabs(g_conv_a), a_log, dt_bias, gdn_norm_w,
        gw_out.reshape(D_MODEL, D_MODEL), norm2_w, _cols_from_slabs(gw_up), _cols_from_slabs(g_conv_f),
        gw_down.reshape(D_FF, D_MODEL), final_norm_w[None], emit)
    (small_all,) = _all_gather(
        [_pack_small(g["norm1"], g["norm2"], g["final"], g["gnw"], g["small"][:, 0:GDN_HEADS],
                     g["small"][:, GDN_HEADS:2 * GDN_HEADS], loss)], name="gather_small")
    got, mine = {}, {}
    for group in ("down", "up", "out", "in"):
        names, own, (send_sems, recv_sems, srcs, lands) = flights[group]
        landed = _exchange_wait(send_sems, recv_sems, srcs, lands, small_all, name="grads_wait_" + group)
        got.update(zip(names, landed))
        mine.update(own)
    o_in = _adamw(got["w_in"], w_in[0], m_w_in[0], v_w_in[0], own=mine["w_in"], name="adamw_w_in", tr=128)
    o_out = _adamw(got["w_out"], w_out[0], m_w_out[0], v_w_out[0], own=mine["w_out"], name="adamw_w_out")
    o_up = _adamw(got["w_up"], w_up[0], m_w_up[0], v_w_up[0], own=mine["w_up"], name="adamw_w_up", tr=128)
    o_down = _adamw(got["w_down"], w_down[0], m_w_down[0], v_w_down[0], own=mine["w_down"], name="adamw_w_down", tr=176)
    o_ca = _adamw(got["conv_a"], conv_qkv_w[0], m_conv_qkv_w[0], v_conv_qkv_w[0], own=mine["conv_a"], name="adamw_conv_a")
    o_cf = _adamw(got["conv_f"], ffn_conv_w[0], m_ffn_conv_w[0], v_ffn_conv_w[0], own=mine["conv_f"], name="adamw_conv_f")
    o_small = _adamw(
        small_all, _pack_small(norm1_w, norm2_w, final_norm_w[None], gdn_norm_w, a_log, dt_bias),
        _pack_small(m_norm1_w, m_norm2_w, m_final_norm_w[None], m_gdn_norm_w, m_a_log, m_dt_bias),
        _pack_small(v_norm1_w, v_norm2_w, v_final_norm_w[None], v_gdn_norm_w, v_a_log, v_dt_bias), name="adamw_small")
    total_loss = o_small[0][3, 256]
    outs = [total_loss, grad_x[None]]
    for k in range(4):
        n1, n2, fin, gn, al, dt = _unpack_small(o_small[k])
        outs += [n1, o_in[k][None], o_ca[k][None], al, dt, gn, o_out[k][None], n2, o_up[k][None], o_cf[k][None], o_down[k][None], fin]
    return tuple(outs)
```

```python
import functools

import jax
import jax.numpy as jnp
from jax import lax
from jax.experimental import pallas as pl
from jax.experimental.pallas import tpu as pltpu

F32 = jnp.float32
BF16 = jnp.bfloat16

N_DEV = 8
D_MODEL = 1024
GDN_HEADS = 4
GDN_DIM = 128
GDN_WIDTH = GDN_HEADS * GDN_DIM
GDN_CONV = 4
CHUNK = 64
DIL_HEADS = 8
DIL_DIM = 64
DIL_WIDTH = DIL_HEADS * DIL_DIM
DIL_PAIRS = DIL_HEADS // 2
DILATIONS = (1, 4, 16)
BAND = 128
D_FF = 2816
FFN_CONV = 3
EPS = 1e-6
A_COLS = 3 * GDN_WIDTH + 128
HALO = 8

ADAM_LR = 0.001
ADAM_B1 = 0.9
ADAM_B2 = 0.999
ADAM_EPS = 1e-08
ADAM_WD = 0.01
ADAM_STEP = 10

VMEM_LIMIT_BYTES = 56 * 1024 * 1024
NEG_BIG = -1e30


def _params(sem=None):
    return pltpu.CompilerParams(dimension_semantics=sem, vmem_limit_bytes=VMEM_LIMIT_BYTES)


def _sds(shape, dtype=F32):
    return jax.ShapeDtypeStruct(shape, dtype)


def _bdot(a, b):
    return jnp.dot(a.astype(BF16), b.astype(BF16), preferred_element_type=F32)


def _bdot_nt(a, b):
    return lax.dot_general(a.astype(BF16), b.astype(BF16), (((1,), (1,)), ((), ())), preferred_element_type=F32)


def _bdot_tn(a, b):
    return lax.dot_general(a.astype(BF16), b.astype(BF16), (((0,), (0,)), ((), ())), preferred_element_type=F32)


def _split(a):
    hi = a.astype(BF16)
    lo = (a - hi.astype(F32)).astype(BF16)
    return hi, lo


def _dot3(a, b, dims):
    ah, al = _split(a)
    bh, bl = _split(b)
    d = functools.partial(lax.dot_general, dimension_numbers=(dims, ((), ())), preferred_element_type=F32)
    return d(ah, bh) + (d(al, bh) + d(ah, bl))


def _exact_tri_dot(tri, g):
    g1 = g.astype(BF16)
    r1 = g - g1.astype(F32)
    g2 = r1.astype(BF16)
    g3 = (r1 - g2.astype(F32)).astype(BF16)
    t = tri.astype(BF16)
    d = functools.partial(jnp.dot, preferred_element_type=F32)
    return d(t, g1) + (d(t, g2) + d(t, g3))


def _sigmoid(x):
    return 1.0 / (1.0 + jnp.exp(-x))


def _dsilu(x, sg):
    return sg * (1.0 + x * (1.0 - sg))


def _mm(a, b, *, name, ta=False, tb=False, res=None, out_dtype=F32, tm=512, tn=512, tk=512):
    if ta:
        K, M = a.shape
    else:
        M, K = a.shape
    if tb:
        N, Kb = b.shape
    else:
        Kb, N = b.shape
    assert K == Kb, (a.shape, b.shape)
    tm, tn, tk = min(tm, M), min(tn, N), min(tk, K)
    assert M % tm == 0 and N % tn == 0 and K % tk == 0, (name, M, N, K, tm, tn, tk)
    nk = K // tk
    dims = (((0 if ta else 1,), (1 if tb else 0,)), ((), ()))
    has_res = res is not None

    def body(*refs):
        if has_res:
            a_ref, b_ref, r_ref, o_ref, acc_ref = refs
        else:
            a_ref, b_ref, o_ref, acc_ref = refs
        k = pl.program_id(2)
        part = lax.dot_general(a_ref[...].astype(BF16), b_ref[...].astype(BF16), dims, preferred_element_type=F32)

        @pl.when(k == 0)
        def _():
            acc_ref[...] = part

        @pl.when(k > 0)
        def _():
            acc_ref[...] += part

        @pl.when(k == nk - 1)
        def _():
            r = acc_ref[...]
            if has_res:
                r = r + r_ref[...]
            o_ref[...] = r.astype(out_dtype)

    a_spec = pl.BlockSpec((tk, tm), lambda i, j, k: (k, i)) if ta else pl.BlockSpec((tm, tk), lambda i, j, k: (i, k))
    b_spec = pl.BlockSpec((tn, tk), lambda i, j, k: (j, k)) if tb else pl.BlockSpec((tk, tn), lambda i, j, k: (k, j))
    o_spec = pl.BlockSpec((tm, tn), lambda i, j, k: (i, j))
    in_specs = [a_spec, b_spec] + ([o_spec] if has_res else [])
    args = (a, b) + ((res,) if has_res else ())
    return pl.pallas_call(
        body, name=name, grid=(M // tm, N // tn, nk), in_specs=in_specs, out_specs=o_spec,
        out_shape=_sds((M, N), out_dtype), scratch_shapes=[pltpu.VMEM((tm, tn), F32)],
        compiler_params=_params(("parallel", "parallel", "arbitrary")),
    )(*args)


def _rms_fwd(x, w, *, name, tm=256):
    S, D = x.shape

    def body(x_ref, w_ref, h_ref):
        xv = x_ref[...]
        r = lax.rsqrt(jnp.mean(xv * xv, axis=-1, keepdims=True) + EPS)
        h_ref[...] = (xv * r * w_ref[...]).astype(BF16)

    return pl.pallas_call(
        body, name=name, grid=(S // tm,),
        in_specs=[pl.BlockSpec((tm, D), lambda i: (i, 0)), pl.BlockSpec((1, D), lambda i: (0, 0))],
        out_specs=pl.BlockSpec((tm, D), lambda i: (i, 0)), out_shape=_sds((S, D), BF16),
        compiler_params=_params(("parallel",)),
    )(x, w)


def _rms_bwd(dh, x, w, res, *, name, tm=256):
    S, D = x.shape

    def body(dh_ref, x_ref, w_ref, res_ref, dx_ref, dw_ref):
        i = pl.program_id(0)
        xv = x_ref[...]
        g = dh_ref[...]
        r = lax.rsqrt(jnp.mean(xv * xv, axis=-1, keepdims=True) + EPS)
        xh = xv * r
        gw = g * w_ref[...]
        dx_ref[...] = res_ref[...] + r * (gw - xh * jnp.mean(gw * xh, axis=-1, keepdims=True))
        part = jnp.sum(g * xh, axis=0, keepdims=True)

        @pl.when(i == 0)
        def _():
            dw_ref[...] = part

        @pl.when(i > 0)
        def _():
            dw_ref[...] += part

    row = pl.BlockSpec((tm, D), lambda i: (i, 0))
    one = pl.BlockSpec((1, D), lambda i: (0, 0))
    return pl.pallas_call(
        body, name=name, grid=(S // tm,), in_specs=[row, row, one, row], out_specs=[row, one],
        out_shape=[_sds((S, D)), _sds((1, D))], compiler_params=_params(("arbitrary",)),
    )(dh, x, w, res)


def _loss_head(x2, w, tgt, *, name, tm=256):
    S, D = x2.shape

    def body(x_ref, w_ref, t_ref, dx_ref, dw_ref, loss_ref):
        i = pl.program_id(0)
        xv = x_ref[...]
        wv = w_ref[...]
        r = lax.rsqrt(jnp.mean(xv * xv, axis=-1, keepdims=True) + EPS)
        xh = xv * r
        err = xh * wv - t_ref[...]
        lrow = jnp.sum(err * err, axis=-1, keepdims=True)
        lsum = jnp.sum(lrow, axis=0, keepdims=True) * (0.5 / D)
        g = err * (1.0 / D)
        gw = g * wv
        dx_ref[...] = r * (gw - xh * jnp.mean(gw * xh, axis=-1, keepdims=True))
        part = jnp.sum(g * xh, axis=0, keepdims=True)
        lpart = jnp.broadcast_to(lsum, (1, 128))

        @pl.when(i == 0)
        def _():
            dw_ref[...] = part
            loss_ref[...] = lpart

        @pl.when(i > 0)
        def _():
            dw_ref[...] += part
            loss_ref[...] += lpart

    row = pl.BlockSpec((tm, D), lambda i: (i, 0))
    one = pl.BlockSpec((1, D), lambda i: (0, 0))
    return pl.pallas_call(
        body, name=name, grid=(S // tm,), in_specs=[row, one, row],
        out_specs=[row, one, pl.BlockSpec((1, 128), lambda i: (0, 0))],
        out_shape=[_sds((S, D)), _sds((1, D)), _sds((1, 128))], compiler_params=_params(("arbitrary",)),
    )(x2, w, tgt)


def _conv_rows(prev, cur, w, taps):
    n = cur.shape[0]
    xs = jnp.concatenate([prev, cur], axis=0)
    base = HALO - (taps - 1)
    out = xs[base:base + n] * w[0:1]
    for i in range(1, taps):
        out = out + xs[base + i:base + i + n] * w[i:i + 1]
    return out


def _conv_rows_bwd(cur_d, next_d, prev_x, cur_x, w, taps):
    n = cur_d.shape[0]
    ds = jnp.concatenate([cur_d, next_d], axis=0)
    dx = ds[taps - 1:taps - 1 + n] * w[0:1]
    for i in range(1, taps):
        dx = dx + ds[taps - 1 - i:taps - 1 - i + n] * w[i:i + 1]
    xs = jnp.concatenate([prev_x, cur_x], axis=0)
    base = HALO - (taps - 1)
    dws = [jnp.sum(cur_d * xs[base + i:base + i + n], axis=0, keepdims=True) for i in range(taps)]
    return dx, jnp.concatenate(dws, axis=0)


def _halo_specs(tm, width, col, nblk):
    per = tm // HALO
    prev = pl.BlockSpec((HALO, width), lambda i, *_: (jnp.maximum(i * per - 1, 0), col))
    nxt = pl.BlockSpec((HALO, width), lambda i, *_: (jnp.minimum((i + 1) * per, nblk * per - 1), col))
    return prev, nxt


def _softplus(x):
    return jnp.maximum(x, 0.0) + jnp.log1p(jnp.exp(-jnp.abs(x)))


def _chunk_tri(tm, upper=False):
    r = lax.broadcasted_iota(jnp.int32, (tm, tm), 0)
    c = lax.broadcasted_iota(jnp.int32, (tm, tm), 1)
    same = lax.div(r, CHUNK) == lax.div(c, CHUNK)
    order = (c >= r) if upper else (c <= r)
    return jnp.where(same & order, 1.0, 0.0)


def _gdn_prep_fwd(proj_a, conv_w, a_log, dt_bias, *, name, tm=256):
    S = proj_a.shape[0]
    nblk = S // tm
    W3 = 3 * GDN_WIDTH

    def body(cur_ref, prev_ref, ba_ref, cw_ref, al_ref, dt_ref, qn_ref, kn_ref, v_ref, gcb_ref, bb_ref):
        i = pl.program_id(0)
        prev = jnp.where(i > 0, prev_ref[...], 0.0)
        c = _conv_rows(prev, cur_ref[...], cw_ref[...], GDN_CONV)
        a = c * _sigmoid(c)
        ba = ba_ref[...]
        lane = lax.broadcasted_iota(jnp.int32, (tm, 128), 1)
        g4 = jnp.zeros((tm, 128), F32)
        for h in range(GDN_HEADS):
            sl = slice(GDN_DIM * h, GDN_DIM * (h + 1))
            qh = a[:, GDN_DIM * h:GDN_DIM * (h + 1)]
            kh = a[:, GDN_WIDTH + GDN_DIM * h:GDN_WIDTH + GDN_DIM * (h + 1)]
            qn_ref[:, sl] = qh * (lax.rsqrt(jnp.sum(qh * qh, axis=-1, keepdims=True) + EPS) * (GDN_DIM ** -0.5))
            kn_ref[:, sl] = kh * lax.rsqrt(jnp.sum(kh * kh, axis=-1, keepdims=True) + EPS)
            beta = _sigmoid(ba[:, h:h + 1])
            bb_ref[:, sl] = jnp.broadcast_to(beta, (tm, GDN_DIM))
            g = -jnp.exp(al_ref[0:1, h:h + 1]) * _softplus(ba[:, GDN_HEADS + h:GDN_HEADS + h + 1] + dt_ref[0:1, h:h + 1])
            g4 = jnp.where(lane == h, g, g4)
        v_ref[...] = a[:, 2 * GDN_WIDTH:]
        gc = _exact_tri_dot(_chunk_tri(tm), g4)
        for h in range(GDN_HEADS):
            gcb_ref[:, GDN_DIM * h:GDN_DIM * (h + 1)] = jnp.broadcast_to(gc[:, h:h + 1], (tm, GDN_DIM))

    prev_spec, _ = _halo_specs(tm, W3, 0, nblk)
    row = pl.BlockSpec((tm, GDN_WIDTH), lambda i: (i, 0))
    small = lambda a: pl.BlockSpec(a.shape, lambda i: (0, 0))
    return pl.pallas_call(
        body, name=name, grid=(nblk,),
        in_specs=[pl.BlockSpec((tm, W3), lambda i: (i, 0)), prev_spec,
                  pl.BlockSpec((tm, 128), lambda i: (i, W3 // 128)), small(conv_w), small(a_log), small(dt_bias)],
        out_specs=[row] * 5, out_shape=[_sds((S, GDN_WIDTH))] * 5, compiler_params=_params(("parallel",)),
    )(proj_a, proj_a, proj_a, conv_w, a_log, dt_bias)


def _chunk_masks():
    r = lax.broadcasted_iota(jnp.int32, (CHUNK, CHUNK), 0)
    c = lax.broadcasted_iota(jnp.int32, (CHUNK, CHUNK), 1)
    return r >= c, r > c, r == c


def _chunk_decay(gcb_h, bb_h, incl):
    G = gcb_h[:, 0:CHUNK]
    diff = G - G.T
    dec = jnp.where(incl, jnp.exp(jnp.where(incl, diff, 0.0)), 0.0)
    return dec, bb_h[:, 0:CHUNK].T


def _gdn_chunk_fwd(qn, kn, v, gcb, bb, *, name):
    S = qn.shape[0]
    nc = S // CHUNK

    def body(qn_ref, kn_ref, v_ref, gcb_ref, bb_ref, uv_ref, wk_ref, at_ref, t_ref):
        incl, strict, diag = _chunk_masks()
        ats, ts = [], []
        for h in range(GDN_HEADS):
            sl = slice(GDN_DIM * h, GDN_DIM * (h + 1))
            q, k, vv, gh = qn_ref[:, sl], kn_ref[:, sl], v_ref[:, sl], gcb_ref[:, sl]
            dec, bt = _chunk_decay(gh, bb_ref[:, sl], incl)
            lmat = jnp.where(strict, dec * _bdot_nt(k, k) * bt, 0.0)
            p = -lmat
            t = jnp.where(diag, 1.0, 0.0) + p
            for _ in range(5):
                p = _bdot(p, p)
                t = t + _bdot(t, p)
            rhs = jnp.concatenate([vv, jnp.exp(gh) * k], axis=1)
            sol = _dot3(t, rhs, ((1,), (0,)))
            uv_ref[:, sl] = sol[:, :GDN_DIM]
            wk_ref[:, sl] = sol[:, GDN_DIM:]
            ats.append(dec * _bdot_nt(q, k) * bt)
            ts.append(t)
        at_ref[...] = jnp.concatenate(ats, axis=1)
        t_ref[...] = jnp.concatenate(ts, axis=1)

    row = pl.BlockSpec((CHUNK, GDN_WIDTH), lambda n: (n, 0))
    sq = pl.BlockSpec((CHUNK, GDN_HEADS * CHUNK), lambda n: (n, 0))
    return pl.pallas_call(
        body, name=name, grid=(nc,), in_specs=[row] * 5, out_specs=[row, row, sq, sq],
        out_shape=[_sds((S, GDN_WIDTH)), _sds((S, GDN_WIDTH)), _sds((S, GDN_HEADS * CHUNK)), _sds((S, GDN_HEADS * CHUNK))],
        compiler_params=_params(("parallel",)),
    )(qn, kn, v, gcb, bb)


def _gdn_scan_fwd(uv, wk, at, qn, kn, gcb, bb, proj_z, gnw, *, name):
    S = uv.shape[0]
    nc = S // CHUNK

    def body(uv_ref, wk_ref, at_ref, qn_ref, kn_ref, gcb_ref, bb_ref, z_ref, gnw_ref, o_ref, u_ref, sp_ref, oa_ref, st_ref):
        n = pl.program_id(0)

        @pl.when(n == 0)
        def _():
            st_ref[...] = jnp.zeros_like(st_ref)

        oas = []
        for h in range(GDN_HEADS):
            sl = slice(GDN_DIM * h, GDN_DIM * (h + 1))
            st = st_ref[h]
            sp_ref[sl, :] = st
            gh = gcb_ref[:, sl]
            glast = gcb_ref[CHUNK - 1:CHUNK, sl]
            u = uv_ref[:, sl] - _bdot(wk_ref[:, sl], st)
            o = _bdot(qn_ref[:, sl] * jnp.exp(gh), st) + _bdot(at_ref[:, CHUNK * h:CHUNK * (h + 1)], u)
            ke = kn_ref[:, sl] * jnp.exp(glast - gh) * bb_ref[:, sl]
            st_ref[h] = jnp.exp(glast) * st + _bdot_tn(ke, u)
            u_ref[:, sl] = u
            o_ref[:, sl] = o
            z = z_ref[:, sl]
            r = lax.rsqrt(jnp.mean(o * o, axis=-1, keepdims=True) + EPS)
            oas.append(o * r * gnw_ref[...] * (z * _sigmoid(z)))
        oa_ref[...] = jnp.concatenate(oas, axis=1).astype(BF16)

    row = pl.BlockSpec((CHUNK, GDN_WIDTH), lambda n: (n, 0))
    sq = pl.BlockSpec((CHUNK, GDN_HEADS * CHUNK), lambda n: (n, 0))
    return pl.pallas_call(
        body, name=name, grid=(nc,),
        in_specs=[row, row, sq, row, row, row, row, row, pl.BlockSpec((1, GDN_DIM), lambda n: (0, 0))],
        out_specs=[row, row, pl.BlockSpec((GDN_WIDTH, GDN_DIM), lambda n: (n, 0)), row],
        out_shape=[_sds((S, GDN_WIDTH)), _sds((S, GDN_WIDTH)), _sds((nc * GDN_WIDTH, GDN_DIM)), _sds((S, 2 * GDN_WIDTH), BF16)],
        scratch_shapes=[pltpu.VMEM((GDN_HEADS, GDN_DIM, GDN_DIM), F32)],
        compiler_params=_params(("arbitrary",)),
    )(uv, wk, at, qn, kn, gcb, bb, proj_z, gnw)


def _gdn_scan_bwd(d_oab, o, proj_z, gnw, sp, u, wk, at, qn, kn, gcb, bb, *, name):
    S = o.shape[0]
    nc = S // CHUNK

    def body(do_ref, o_ref, z_ref, gnw_ref, sp_ref, u_ref, wk_ref, at_ref, qn_ref, kn_ref, gcb_ref, bb_ref,
             dz_ref, dgn_ref, du_ref, dwk_ref, dat_ref, dqd_ref, dke_ref, dgl_ref, ds_ref):
        n = pl.program_id(0)

        @pl.when(n == 0)
        def _():
            ds_ref[...] = jnp.zeros_like(ds_ref)
            dgn_ref[...] = jnp.zeros_like(dgn_ref)

        dats, dgn = [], jnp.zeros((1, GDN_DIM), F32)
        for h in range(GDN_HEADS):
            sl = slice(GDN_DIM * h, GDN_DIM * (h + 1))
            oo = o_ref[:, sl]
            z = z_ref[:, sl]
            gw = gnw_ref[...]
            sg = _sigmoid(z)
            r = lax.rsqrt(jnp.mean(oo * oo, axis=-1, keepdims=True) + EPS)
            xh = oo * r
            d_oa = do_ref[:, sl]
            dy = d_oa * (z * sg)
            dz_ref[:, sl] = (d_oa * (xh * gw) * _dsilu(z, sg)).astype(BF16)
            dgn = dgn + jnp.sum(dy * xh, axis=0, keepdims=True)
            dxh = dy * gw
            do = r * (dxh - xh * jnp.mean(dxh * xh, axis=-1, keepdims=True))

            st = sp_ref[sl, :]
            dst = ds_ref[h]
            gh = gcb_ref[:, sl]
            glast = gcb_ref[CHUNK - 1:CHUNK, sl]
            uu = u_ref[:, sl]
            ath = at_ref[:, CHUNK * h:CHUNK * (h + 1)]
            qd = qn_ref[:, sl] * jnp.exp(gh)
            ke = kn_ref[:, sl] * jnp.exp(glast - gh) * bb_ref[:, sl]
            ge = jnp.exp(glast)
            dqd_ref[:, sl] = _bdot_nt(do, st)
            dats.append(_bdot_nt(do, uu))
            du = _bdot_tn(ath, do) + _bdot(ke, dst)
            dke_ref[:, sl] = _bdot_nt(uu, dst)
            dge = jnp.sum(jnp.sum(dst * st, axis=1, keepdims=True), axis=0, keepdims=True)
            dgl_ref[0, :, sl] = jnp.broadcast_to(dge * ge, (8, GDN_DIM))
            ds_ref[h] = _bdot_tn(qd, do) + ge * dst - _bdot_tn(wk_ref[:, sl], du)
            du_ref[:, sl] = du
            dwk_ref[:, sl] = -_bdot_nt(du, st)
        dat_ref[...] = jnp.concatenate(dats, axis=1)
        dgn_ref[...] += dgn

    rev = lambda n: (nc - 1 - n, 0)
    row = pl.BlockSpec((CHUNK, GDN_WIDTH), rev)
    sq = pl.BlockSpec((CHUNK, GDN_HEADS * CHUNK), rev)
    one = pl.BlockSpec((1, GDN_DIM), lambda n: (0, 0))
    return pl.pallas_call(
        body, name=name, grid=(nc,),
        in_specs=[row, row, row, one, pl.BlockSpec((GDN_WIDTH, GDN_DIM), rev), row, row, sq, row, row, row, row],
        out_specs=[row, one, row, row, sq, row, row, pl.BlockSpec((1, 8, GDN_WIDTH), lambda n: (nc - 1 - n, 0, 0))],
        out_shape=[_sds((S, GDN_WIDTH), BF16), _sds((1, GDN_DIM)), _sds((S, GDN_WIDTH)), _sds((S, GDN_WIDTH)),
                   _sds((S, GDN_HEADS * CHUNK)), _sds((S, GDN_WIDTH)), _sds((S, GDN_WIDTH)), _sds((nc, 8, GDN_WIDTH))],
        scratch_shapes=[pltpu.VMEM((GDN_HEADS, GDN_DIM, GDN_DIM), F32)],
        compiler_params=_params(("arbitrary",)),
    )(d_oab, o, proj_z, gnw, sp, u, wk, at, qn, kn, gcb, bb)


def _gdn_chunk_bwd(qn, kn, v, gcb, bb, tmat, uv, wk, du, dwk, dat, dqd, dke, dgl, *, name):
    S = qn.shape[0]
    nc = S // CHUNK

    def body(qn_ref, kn_ref, v_ref, gcb_ref, bb_ref, t_ref, uv_ref, wk_ref, du_ref, dwk_ref, dat_ref, dqd_ref, dke_ref,
             dgl_ref, dq_ref, dk_ref, dv_ref, dg_ref, dbeta_ref):
        incl, strict, _ = _chunk_masks()
        lane = lax.broadcasted_iota(jnp.int32, (CHUNK, 128), 1)
        rowi = lax.broadcasted_iota(jnp.int32, (CHUNK, 1), 0)
        dgc4 = jnp.zeros((CHUNK, 128), F32)
        db4 = jnp.zeros((CHUNK, 128), F32)
        for h in range(GDN_HEADS):
            sl = slice(GDN_DIM * h, GDN_DIM * (h + 1))
            sq = slice(CHUNK * h, CHUNK * (h + 1))
            q, k, gh, bh = qn_ref[:, sl], kn_ref[:, sl], gcb_ref[:, sl], bb_ref[:, sl]
            dec, bt = _chunk_decay(gh, bh, incl)
            kk = _bdot_nt(k, k)
            qk = _bdot_nt(q, k)
            t = t_ref[:, sq]
            d_sol = jnp.concatenate([du_ref[:, sl], dwk_ref[:, sl]], axis=1)
            d_rhs = _dot3(t, d_sol, ((0,), (0,)))
            sol = jnp.concatenate([uv_ref[:, sl], wk_ref[:, sl]], axis=1)
            d_l = jnp.where(strict, -_dot3(d_rhs, sol, ((1,), (1,))), 0.0)
            d_a = jnp.where(incl, dat_ref[:, sq], 0.0)
            gam = jnp.exp(gh)
            glast = gcb_ref[CHUNK - 1:CHUNK, sl]
            e = jnp.exp(glast - gh)
            d_gk = d_rhs[:, GDN_DIM:]
            dqd = dqd_ref[:, sl]
            dke = dke_ref[:, sl]
            ml = d_l * dec * bt
            ma = d_a * dec * bt
            dq_ref[:, sl] = _bdot(ma, k) + dqd * gam
            dk_ref[:, sl] = (_bdot(ml, k) + _bdot_tn(ml, k) + _bdot_tn(ma, q)) + d_gk * gam + dke * (e * bh)
            dv_ref[:, sl] = d_rhs[:, :GDN_DIM]
            wb = d_l * dec * kk + d_a * dec * qk
            ew = wb * bt
            s_ke = jnp.sum(dke * k * (e * bh), axis=-1, keepdims=True)
            dbeta = jnp.sum(wb.T, axis=-1, keepdims=True) + jnp.sum(dke * k * e, axis=-1, keepdims=True)
            dgc = (jnp.sum(ew, axis=-1, keepdims=True) - jnp.sum(ew.T, axis=-1, keepdims=True)
                   + jnp.sum(dqd * q * gam, axis=-1, keepdims=True) + jnp.sum(d_gk * k * gam, axis=-1, keepdims=True) - s_ke)
            tail = jnp.sum(s_ke, axis=0, keepdims=True) + dgl_ref[0, 0:1, GDN_DIM * h:GDN_DIM * h + 1]
            dgc = dgc + jnp.where(rowi == CHUNK - 1, tail, 0.0)
            dgc4 = jnp.where(lane == h, dgc, dgc4)
            db4 = jnp.where(lane == h, dbeta, db4)
        dg_ref[...] = _exact_tri_dot(_chunk_tri(CHUNK, upper=True), dgc4)
        dbeta_ref[...] = db4

    row = pl.BlockSpec((CHUNK, GDN_WIDTH), lambda n: (n, 0))
    sq = pl.BlockSpec((CHUNK, GDN_HEADS * CHUNK), lambda n: (n, 0))
    col = pl.BlockSpec((CHUNK, 128), lambda n: (n, 0))
    return pl.pallas_call(
        body, name=name, grid=(nc,),
        in_specs=[row] * 5 + [sq, row, row, row, row, sq, row, row, pl.BlockSpec((1, 8, GDN_WIDTH), lambda n: (n, 0, 0))],
        out_specs=[row, row, row, col, col],
        out_shape=[_sds((S, GDN_WIDTH))] * 3 + [_sds((S, 128))] * 2, compiler_params=_params(("parallel",)),
    )(qn, kn, v, gcb, bb, tmat, uv, wk, du, dwk, dat, dqd, dke, dgl)


def _gdn_prep_bwd(dqn, dkn, dv, dg, dbeta, proj_a, conv_w, a_log, dt_bias, *, name, tm=256):
    S = proj_a.shape[0]
    nblk = S // tm
    W3 = 3 * GDN_WIDTH

    def body(dqn_ref, dkn_ref, dv_ref, dg_ref, dbeta_ref, cur_ref, prev_ref, ba_ref, cw_ref, al_ref, dt_ref,
             dc_ref, dba_ref, sm_ref):
        i = pl.program_id(0)
        prev = jnp.where(i > 0, prev_ref[...], 0.0)
        c = _conv_rows(prev, cur_ref[...], cw_ref[...], GDN_CONV)
        sg = _sigmoid(c)
        a = c * sg
        dsl = _dsilu(c, sg)
        ba = ba_ref[...]
        lane = lax.broadcasted_iota(jnp.int32, (tm, 128), 1)
        lane1 = lax.broadcasted_iota(jnp.int32, (1, 128), 1)
        dba = jnp.zeros((tm, 128), F32)
        sm = jnp.zeros((1, 128), F32)
        for h in range(GDN_HEADS):
            sl = slice(GDN_DIM * h, GDN_DIM * (h + 1))
            ks = slice(GDN_WIDTH + GDN_DIM * h, GDN_WIDTH + GDN_DIM * (h + 1))
            qh, kh = a[:, sl], a[:, ks]
            rq = lax.rsqrt(jnp.sum(qh * qh, axis=-1, keepdims=True) + EPS)
            rk = lax.rsqrt(jnp.sum(kh * kh, axis=-1, keepdims=True) + EPS)
            qhat, khat = qh * rq, kh * rk
            dyq = dqn_ref[:, sl] * (GDN_DIM ** -0.5)
            dyk = dkn_ref[:, sl]
            dq = rq * (dyq - qhat * jnp.sum(dyq * qhat, axis=-1, keepdims=True))
            dk = rk * (dyk - khat * jnp.sum(dyk * khat, axis=-1, keepdims=True))
            dc_ref[:, sl] = dq * dsl[:, sl]
            dc_ref[:, ks] = dk * dsl[:, ks]
            beta = _sigmoid(ba[:, h:h + 1])
            db = dbeta_ref[:, h:h + 1] * beta * (1.0 - beta)
            aneg = -jnp.exp(al_ref[0:1, h:h + 1])
            xa = ba[:, GDN_HEADS + h:GDN_HEADS + h + 1] + dt_ref[0:1, h:h + 1]
            dgh = dg_ref[:, h:h + 1]
            dxa = dgh * aneg * _sigmoid(xa)
            dba = jnp.where(lane == h, db, dba)
            dba = jnp.where(lane == GDN_HEADS + h, dxa, dba)
            d_alog = jnp.sum(dgh * _softplus(xa), axis=0, keepdims=True) * aneg
            sm = jnp.where(lane1 == h, d_alog, sm)
            sm = jnp.where(lane1 == GDN_HEADS + h, jnp.sum(dxa, axis=0, keepdims=True), sm)
        vs = slice(2 * GDN_WIDTH, W3)
        dc_ref[:, vs] = dv_ref[...] * dsl[:, vs]
        dba_ref[...] = dba

        @pl.when(i == 0)
        def _():
            sm_ref[...] = sm

        @pl.when(i > 0)
        def _():
            sm_ref[...] += sm

    prev_spec, _ = _halo_specs(tm, W3, 0, nblk)
    row = pl.BlockSpec((tm, GDN_WIDTH), lambda i: (i, 0))
    col = pl.BlockSpec((tm, 128), lambda i: (i, 0))
    small = lambda a: pl.BlockSpec(a.shape, lambda i: (0, 0))
    return pl.pallas_call(
        body, name=name, grid=(nblk,),
        in_specs=[row, row, row, col, col, pl.BlockSpec((tm, W3), lambda i: (i, 0)), prev_spec,
                  pl.BlockSpec((tm, 128), lambda i: (i, W3 // 128)), small(conv_w), small(a_log), small(dt_bias)],
        out_specs=[pl.BlockSpec((tm, W3), lambda i: (i, 0)), col, pl.BlockSpec((1, 128), lambda i: (0, 0))],
        out_shape=[_sds((S, W3)), _sds((S, 128)), _sds((1, 128))], compiler_params=_params(("arbitrary",)),
    )(dqn, dkn, dv, dg, dbeta, proj_a, proj_a, proj_a, conv_w, a_log, dt_bias)


def _gdn_conv_bwd(dc, dba, proj_a, conv_w, *, name, tm=256):
    S = proj_a.shape[0]
    nblk = S // tm
    W3 = 3 * GDN_WIDTH

    def body(dc_ref, dnext_ref, dba_ref, cur_ref, prev_ref, cw_ref, da_ref, dcw_ref):
        i = pl.program_id(0)
        prev = jnp.where(i > 0, prev_ref[...], 0.0)
        nxt = jnp.where(i < nblk - 1, dnext_ref[...], 0.0)
        dx, dw = _conv_rows_bwd(dc_ref[...], nxt, prev, cur_ref[...], cw_ref[...], GDN_CONV)
        da_ref[:, 0:W3] = dx.astype(BF16)
        da_ref[:, W3:] = dba_ref[...].astype(BF16)

        @pl.when(i == 0)
        def _():
            dcw_ref[...] = dw

        @pl.when(i > 0)
        def _():
            dcw_ref[...] += dw

    prev_spec, next_spec = _halo_specs(tm, W3, 0, nblk)
    wide = pl.BlockSpec((tm, W3), lambda i: (i, 0))
    return pl.pallas_call(
        body, name=name, grid=(nblk,),
        in_specs=[wide, next_spec, pl.BlockSpec((tm, 128), lambda i: (i, 0)), wide, prev_spec,
                  pl.BlockSpec(conv_w.shape, lambda i: (0, 0))],
        out_specs=[pl.BlockSpec((tm, A_COLS), lambda i: (i, 0)), pl.BlockSpec(conv_w.shape, lambda i: (0, 0))],
        out_shape=[_sds((S, A_COLS), BF16), _sds(conv_w.shape)], compiler_params=_params(("arbitrary",)),
    )(dc, dc, dba, proj_a, proj_a, conv_w)


def _band_mask(nk):
    i = lax.broadcasted_iota(jnp.int32, (BAND, nk), 0)
    j = lax.broadcasted_iota(jnp.int32, (BAND, nk), 1)
    if nk == BAND:
        return j <= i
    return (j >= i) & (j <= i + BAND)


def _rows(start, size, stride):
    return pl.ds(start, size) if stride == 1 else pl.ds(start, size, stride=stride)


def _attn_blocks(S, visit):
    for d in DILATIONS:
        nb = S // (d * BAND)

        def per_residue(r, carry, d=d, nb=nb):
            visit(d, r, 0, True)
            if nb > 1:
                def per_block(n, c):
                    visit(d, r, n, False)
                    return c
                lax.fori_loop(1, nb, per_block, 0)
            return carry

        if d == 1:
            per_residue(0, 0)
        else:
            lax.fori_loop(0, d, per_residue, 0)


def _attn_fwd(proj_b, oab, *, name):
    S = proj_b.shape[0]
    scale = DIL_DIM ** -0.5

    def body(q_ref, k_ref, v_ref, oab_in_ref, ob_ref, lse_ref, m_ref, l_ref, acc_ref):
        del oab_in_ref
        lane = lax.broadcasted_iota(jnp.int32, (BAND, 128), 1)
        lo = lane < DIL_DIM
        m_ref[...] = jnp.full_like(m_ref, NEG_BIG)
        l_ref[...] = jnp.zeros_like(l_ref)
        acc_ref[...] = jnp.zeros_like(acc_ref)

        def visit(d, r, n, first):
            nk = BAND if first else 2 * BAND
            qs = r + n * (BAND * d)
            ks = r if first else r + (n - 1) * (BAND * d)
            qrows = _rows(qs, BAND, d)
            krows = _rows(ks, nk, d)
            q = q_ref[qrows, :] * scale
            k = k_ref[krows, :].astype(BF16)
            v = v_ref[krows, :].astype(BF16)
            valid = _band_mask(nk)
            s0 = jnp.where(valid, _bdot_nt(jnp.where(lo, q, 0.0), k), NEG_BIG)
            s1 = jnp.where(valid, _bdot_nt(jnp.where(lo, 0.0, q), k), NEG_BIG)
            m_old = m_ref[qrows, :]
            mb = jnp.where(lo, jnp.max(s0, axis=-1, keepdims=True), jnp.max(s1, axis=-1, keepdims=True))
            m_new = jnp.maximum(m_old, mb)
            alpha = jnp.exp(m_old - m_new)
            p0 = jnp.exp(s0 - m_new[:, 0:1])
            p1 = jnp.exp(s1 - m_new[:, DIL_DIM:DIL_DIM + 1])
            psum = jnp.where(lo, jnp.sum(p0, axis=-1, keepdims=True), jnp.sum(p1, axis=-1, keepdims=True))
            pv = jnp.where(lo, _bdot(p0, v), _bdot(p1, v))
            m_ref[qrows, :] = m_new
            l_ref[qrows, :] = alpha * l_ref[qrows, :] + psum
            acc_ref[qrows, :] = alpha * acc_ref[qrows, :] + pv

        _attn_blocks(S, visit)
        ob_ref[...] = (acc_ref[...] / l_ref[...]).astype(BF16)
        lse_ref[...] = m_ref[...] + jnp.log(l_ref[...])

    part = lambda t: pl.BlockSpec((S, 128), lambda p: (0, 3 * p + t))
    return pl.pallas_call(
        body, name=name, grid=(DIL_PAIRS,),
        in_specs=[part(0), part(1), part(2), pl.BlockSpec(memory_space=pl.ANY)],
        out_specs=[pl.BlockSpec((S, 128), lambda p: (0, GDN_WIDTH // 128 + p)), pl.BlockSpec((S, 128), lambda p: (0, p))],
        out_shape=[_sds(oab.shape, BF16), _sds((S, DIL_WIDTH))],
        scratch_shapes=[pltpu.VMEM((S, 128), F32)] * 3, input_output_aliases={3: 0},
        compiler_params=_params(("parallel",)),
    )(proj_b, proj_b, proj_b, oab)


def _attn_bwd(proj_b, oab, d_oab, lse, *, name):
    S = proj_b.shape[0]
    scale = DIL_DIM ** -0.5

    def body(q_ref, k_ref, v_ref, o_ref, do_ref, lse_ref, dqkv_ref, dq_ref, dk_ref, dv_ref, delta_ref):
        lane = lax.broadcasted_iota(jnp.int32, (BAND, 128), 1)
        lo = lane < DIL_DIM
        dq_ref[...] = jnp.zeros_like(dq_ref)
        dk_ref[...] = jnp.zeros_like(dk_ref)
        dv_ref[...] = jnp.zeros_like(dv_ref)
        prod = do_ref[...] * o_ref[...].astype(F32)
        lo_all = lax.broadcasted_iota(jnp.int32, (S, 128), 1) < DIL_DIM
        delta_ref[...] = jnp.where(lo_all, jnp.sum(jnp.where(lo_all, prod, 0.0), axis=-1, keepdims=True),
                                   jnp.sum(jnp.where(lo_all, 0.0, prod), axis=-1, keepdims=True))

        def visit(d, r, n, first):
            nk = BAND if first else 2 * BAND
            qs = r + n * (BAND * d)
            ks = r if first else r + (n - 1) * (BAND * d)
            qrows = _rows(qs, BAND, d)
            krows = _rows(ks, nk, d)
            q = q_ref[qrows, :] * scale
            k = k_ref[krows, :]
            v = v_ref[krows, :]
            do = do_ref[qrows, :]
            delta_b = delta_ref[qrows, :]
            lse_b = lse_ref[qrows, :]
            valid = _band_mask(nk)
            lkl = lax.broadcasted_iota(jnp.int32, (nk, 128), 1) < DIL_DIM
            dq = jnp.zeros((BAND, 128), F32)
            dk = jnp.zeros((nk, 128), F32)
            dv = jnp.zeros((nk, 128), F32)
            for sel, ksel, lcol in ((lo, lkl, 0), (~lo, ~lkl, DIL_DIM)):
                qh = jnp.where(sel, q, 0.0)
                doh = jnp.where(sel, do, 0.0)
                s = _bdot_nt(qh, k)
                p = jnp.where(valid, jnp.exp(s - lse_b[:, lcol:lcol + 1]), 0.0)
                dp = _bdot_nt(doh, v)
                ds = p * (dp - delta_b[:, lcol:lcol + 1])
                dq = dq + _bdot(ds, jnp.where(ksel, k, 0.0))
                dk = dk + _bdot_tn(ds, qh)
                dv = dv + _bdot_tn(p, doh)
            dq_ref[qrows, :] += dq * scale
            dk_ref[krows, :] += dk
            dv_ref[krows, :] += dv

        _attn_blocks(S, visit)
        dqkv_ref[:, 0:128] = dq_ref[...].astype(BF16)
        dqkv_ref[:, 128:256] = dk_ref[...].astype(BF16)
        dqkv_ref[:, 256:384] = dv_ref[...].astype(BF16)

    half = lambda p: (0, GDN_WIDTH // 128 + p)
    part = lambda t: pl.BlockSpec((S, 128), lambda p: (0, 3 * p + t))
    return pl.pallas_call(
        body, name=name, grid=(DIL_PAIRS,),
        in_specs=[part(0), part(1), part(2), pl.BlockSpec((S, 128), half), pl.BlockSpec((S, 128), half),
                  pl.BlockSpec((S, 128), lambda p: (0, p))],
        out_specs=pl.BlockSpec((S, 384), lambda p: (0, p)), out_shape=_sds((S, 3 * DIL_WIDTH), BF16),
        scratch_shapes=[pltpu.VMEM((S, 128), F32)] * 4, compiler_params=_params(("parallel",)),
    )(proj_b, proj_b, proj_b, oab, d_oab, lse)


def _ffn_act(u_pre, conv_w, *, name, tm=512, tc=1408):
    S = u_pre.shape[0]
    nblk, ncol = S // tm, D_FF // tc

    def body(g_ref, gp_ref, u_ref, up_ref, wg_ref, wu_ref, act_ref):
        i = pl.program_id(1)
        zero = lambda ref: jnp.where(i > 0, ref[...], 0.0)
        gate = _conv_rows(zero(gp_ref), g_ref[...], wg_ref[...], FFN_CONV)
        up = _conv_rows(zero(up_ref), u_ref[...], wu_ref[...], FFN_CONV)
        act_ref[...] = (gate * _sigmoid(gate) * up).astype(BF16)

    per = tm // HALO
    prev = lambda off: pl.BlockSpec((HALO, tc), lambda j, i: (jnp.maximum(i * per - 1, 0), j + off))
    cur = lambda off: pl.BlockSpec((tm, tc), lambda j, i: (i, j + off))
    wsp = lambda off: pl.BlockSpec((FFN_CONV, tc), lambda j, i: (0, j + off))
    return pl.pallas_call(
        body, name=name, grid=(ncol, nblk),
        in_specs=[cur(0), prev(0), cur(ncol), prev(ncol), wsp(0), wsp(ncol)],
        out_specs=pl.BlockSpec((tm, tc), lambda j, i: (i, j)), out_shape=_sds((S, D_FF), BF16),
        compiler_params=_params(("parallel", "parallel")),
    )(u_pre, u_pre, u_pre, u_pre, conv_w, conv_w)


def _ffn_act_bwd(d_act, u_pre, conv_w, *, name, tm=512, tc=1408):
    S = u_pre.shape[0]
    nblk, ncol = S // tm, D_FF // tc

    def body(da_ref, g_ref, gp_ref, u_ref, up_ref, wg_ref, wu_ref, dg_ref, du_ref):
        i = pl.program_id(1)
        zero = lambda ref: jnp.where(i > 0, ref[...], 0.0)
        gate = _conv_rows(zero(gp_ref), g_ref[...], wg_ref[...], FFN_CONV)
        up = _conv_rows(zero(up_ref), u_ref[...], wu_ref[...], FFN_CONV)
        sg = _sigmoid(gate)
        da = da_ref[...]
        dg_ref[...] = (da * up * _dsilu(gate, sg)).astype(BF16)
        du_ref[...] = (da * gate * sg).astype(BF16)

    per = tm // HALO
    prev = lambda off: pl.BlockSpec((HALO, tc), lambda j, i: (jnp.maximum(i * per - 1, 0), j + off))
    cur = lambda off: pl.BlockSpec((tm, tc), lambda j, i: (i, j + off))
    wsp = lambda off: pl.BlockSpec((FFN_CONV, tc), lambda j, i: (0, j + off))
    outs = pl.pallas_call(
        body, name=name, grid=(ncol, nblk),
        in_specs=[cur(0), cur(0), prev(0), cur(ncol), prev(ncol), wsp(0), wsp(ncol)],
        out_specs=[cur(0), cur(0)], out_shape=[_sds((S, D_FF), BF16)] * 2,
        compiler_params=_params(("parallel", "parallel")),
    )(d_act, u_pre, u_pre, u_pre, u_pre, conv_w, conv_w)
    return outs


def _ffn_conv_bwd(dc, u_pre, conv_w, col_off, *, name, tm=512, tc=1408):
    S = u_pre.shape[0]
    nblk, ncol = S // tm, D_FF // tc
    off = col_off // tc

    def body(dc_ref, dn_ref, x_ref, xp_ref, w_ref, dx_ref, dw_ref):
        i = pl.program_id(1)
        prev = jnp.where(i > 0, xp_ref[...], 0.0)
        nxt = jnp.where(i < nblk - 1, dn_ref[...].astype(F32), 0.0)
        dx, dw = _conv_rows_bwd(dc_ref[...].astype(F32), nxt[0:HALO], prev, x_ref[...], w_ref[...], FFN_CONV)
        dx_ref[...] = dx.astype(BF16)

        @pl.when(i == 0)
        def _():
            dw_ref[...] = dw

        @pl.when(i > 0)
        def _():
            dw_ref[...] += dw

    per = tm // HALO
    hb = 2 * HALO
    perb = tm // hb
    return pl.pallas_call(
        body, name=name, grid=(ncol, nblk),
        in_specs=[pl.BlockSpec((tm, tc), lambda j, i: (i, j)),
                  pl.BlockSpec((hb, tc), lambda j, i: (jnp.minimum((i + 1) * perb, nblk * perb - 1), j)),
                  pl.BlockSpec((tm, tc), lambda j, i: (i, j + off)),
                  pl.BlockSpec((HALO, tc), lambda j, i: (jnp.maximum(i * per - 1, 0), j + off)),
                  pl.BlockSpec((FFN_CONV, tc), lambda j, i: (0, j + off))],
        out_specs=[pl.BlockSpec((tm, tc), lambda j, i: (i, j)), pl.BlockSpec((FFN_CONV, tc), lambda j, i: (0, j))],
        out_shape=[_sds((S, D_FF), BF16), _sds((FFN_CONV, D_FF))],
        compiler_params=_params(("parallel", "arbitrary")),
    )(dc, dc, u_pre, u_pre, conv_w)


def _local_step(x, tgt, norm1_w, w_a, w_z, w_b, conv_a, a_log, dt_bias, gnw, norm2_w, final_w, late_weights, emit):
    wgrad = functools.partial(_mm, ta=True, out_dtype=BF16)
    h1 = _rms_fwd(x, norm1_w, name="rms1_fwd")
    proj_a = _mm(h1, w_a, name="proj_a", tn=A_COLS, tk=1024)
    proj_z = _mm(h1, w_z, name="proj_z", tk=1024)
    proj_b = _mm(h1, w_b, name="proj_b", tn=768, tk=1024)
    qn, kn, v, gcb, bb = _gdn_prep_fwd(proj_a, conv_a, a_log, dt_bias, name="gdn_prep_fwd")
    uv, wk, at, tmat = _gdn_chunk_fwd(qn, kn, v, gcb, bb, name="gdn_chunk_fwd")
    o, u, sp, oab = _gdn_scan_fwd(uv, wk, at, qn, kn, gcb, bb, proj_z, gnw, name="gdn_scan_fwd")
    oab, lse = _attn_fwd(proj_b, oab, name="attn_fwd")
    w_out, w_up, conv_f, w_down = late_weights(oab)
    x1 = _mm(oab, w_out, res=x, name="out_proj", tk=1024)
    h2 = _rms_fwd(x1, norm2_w, name="rms2_fwd")
    u_pre = _mm(h2, w_up, name="ffn_up", tk=1024)
    act = _ffn_act(u_pre, conv_f, name="ffn_act")
    x2 = _mm(act, w_down, res=x1, name="ffn_down", tk=1408)
    dx2, d_final, loss = _loss_head(x2, final_w, tgt, name="loss_head")
    d_act = _mm(dx2, w_down, tb=True, name="ffn_down_dx", tn=1408, tk=1024)
    conv_f = _behind(conv_f, emit("down", w_down=wgrad(act, dx2, name="ffn_down_dw", tm=1408, tk=512)))
    dgate, dup = _ffn_act_bwd(d_act, u_pre, conv_f, name="ffn_act_bwd")
    du_g, dcw_g = _ffn_conv_bwd(dgate, u_pre, conv_f, 0, name="ffn_conv_bwd_gate")
    du_u, dcw_u = _ffn_conv_bwd(dup, u_pre, conv_f, D_FF, name="ffn_conv_bwd_up")
    dh2 = _mm(du_g, w_up[:, :D_FF], tb=True, name="ffn_up_dx_gate", tk=1408)
    dh2 = _mm(du_u, w_up[:, D_FF:], tb=True, res=dh2, name="ffn_up_dx_up", tk=1408)
    g_up_g = wgrad(h2, du_g, name="ffn_up_dw_gate", tn=1408)
    g_up_u = wgrad(h2, du_u, name="ffn_up_dw_up", tn=1408)
    token = emit("up", w_up=jnp.concatenate([g_up_g, g_up_u], axis=1), conv_f=jnp.concatenate([dcw_g, dcw_u], axis=1))
    dx1, d_norm2 = _rms_bwd(dh2, x1, _behind(norm2_w, token), dx2, name="rms2_bwd")
    d_oab = _mm(dx1, w_out, tb=True, name="out_proj_dx", tk=1024)
    gnw = _behind(gnw, emit("out", w_out=wgrad(oab, dx1, name="out_proj_dw")))
    dz, d_gnw, du, dwk, dat, dqd, dke, dgl = _gdn_scan_bwd(d_oab, o, proj_z, gnw, sp, u, wk, at, qn, kn, gcb, bb, name="gdn_scan_bwd")
    dqn, dkn, dv, dg, dbeta = _gdn_chunk_bwd(qn, kn, v, gcb, bb, tmat, uv, wk, du, dwk, dat, dqd, dke, dgl, name="gdn_chunk_bwd")
    dc, dba, d_small = _gdn_prep_bwd(dqn, dkn, dv, dg, dbeta, proj_a, conv_a, a_log, dt_bias, name="gdn_prep_bwd")
    d_pa, d_conv_a = _gdn_conv_bwd(dc, dba, proj_a, conv_a, name="gdn_conv_bwd")
    d_pb = _attn_bwd(proj_b, oab, d_oab, lse, name="attn_bwd")
    g_a = wgrad(h1, d_pa, name="proj_a_dw", tn=A_COLS)
    g_z = wgrad(h1, dz, name="proj_z_dw")
    g_b = wgrad(h1, d_pb, name="proj_b_dw", tn=768)
    w_z = _behind(w_z, emit("in", w_a=g_a, w_z=g_z, w_b=g_b, conv_a=d_conv_a))
    dh1 = _mm(dz, w_z, tb=True, name="proj_z_dx")
    dh1 = _mm(d_pa, w_a, tb=True, res=dh1, name="proj_a_dx", tk=A_COLS)
    dh1 = _mm(d_pb, w_b, tb=True, res=dh1, name="proj_b_dx", tk=1536)
    grad_x, d_norm1 = _rms_bwd(dh1, x, norm1_w, dx1, name="rms1_bwd")
    small = dict(norm1=d_norm1, small=d_small, gnw=d_gnw, norm2=d_norm2, final=d_final)
    return loss, grad_x, small


_O1 = 3 * GDN_WIDTH
_O2 = _O1 + GDN_WIDTH
_O3 = _O2 + 2 * GDN_HEADS


def _split_w_in(w_in):
    d = w_in.shape[0]
    pad = jnp.zeros((d, A_COLS - _O1 - 2 * GDN_HEADS), w_in.dtype)
    w_a = jnp.concatenate([w_in[:, :_O1], w_in[:, _O2:_O3], pad], axis=1)
    w_b = w_in[:, _O3:].reshape(d, 3, DIL_PAIRS, 128).transpose(0, 2, 1, 3).reshape(d, 3 * DIL_WIDTH)
    return w_a, w_in[:, _O1:_O2], w_b


def _merge_g_in(g_a, g_z, g_b):
    d = g_a.shape[0]
    g_b = g_b.reshape(d, DIL_PAIRS, 3, 128).transpose(0, 2, 1, 3).reshape(d, 3 * DIL_WIDTH)
    return jnp.concatenate([g_a[:, :_O1], g_z, g_a[:, _O1:_O1 + 2 * GDN_HEADS], g_b], axis=1)


MESH = pl.DeviceIdType.MESH
ANY = pl.BlockSpec(memory_space=pl.ANY)


def _position():
    return lax.axis_index("x"), lax.axis_index("y"), lax.axis_index("c")


def _slot(p):
    return 4 * p[0] + 2 * p[1] + p[2]


def _all_gather(blocks, *, name):
    n = len(blocks)

    def body(*refs):
        ins, outs = refs[:n], refs[n:2 * n]
        send_sems, recv_sems, local_sems = refs[2 * n:]
        x, y, c = _position()
        me, sibling = (x, y, c), (x, y, 1 - c)
        chips = [(1 - x, y), (x, 1 - y), (1 - x, 1 - y)]

        def copy(a, k, block, to, src=None):
            dst = outs[a].at[_slot(block)]
            return pltpu.make_async_remote_copy(
                src_ref=dst if src is None else src, dst_ref=dst, send_sem=send_sems.at[a, k], recv_sem=recv_sems.at[a, k],
                device_id=to, device_id_type=MESH)

        mine = [pltpu.make_async_copy(ins[a], outs[a].at[_slot(me)], local_sems.at[a]) for a in range(n)]
        for cp in mine:
            cp.start()
        first = []
        for a in range(n):
            first.append(copy(a, 0, me, sibling, src=ins[a]))
            first += [copy(a, 1 + j, me, (*chip, c), src=ins[a]) for j, chip in enumerate(chips)]
        for cp in first:
            cp.start()
        passed = []
        for j, chip in enumerate(chips):
            for a in range(n):
                copy(a, 1 + j, (*chip, c), me).wait_recv()
                fwd = copy(a, 4 + j, (*chip, c), sibling)
                fwd.start()
                passed.append(fwd)
        for a in range(n):
            copy(a, 0, sibling, me).wait_recv()
            for j, chip in enumerate(chips):
                copy(a, 4 + j, (*chip, 1 - c), me).wait_recv()
        for cp in first + passed:
            cp.wait_send()
        for cp in mine:
            cp.wait()

    return pl.pallas_call(
        body, name=name, in_specs=[ANY] * n, out_specs=[ANY] * n,
        out_shape=[_sds((N_DEV,) + b.shape, b.dtype) for b in blocks],
        scratch_shapes=[pltpu.SemaphoreType.DMA((n, 7)), pltpu.SemaphoreType.DMA((n, 7)), pltpu.SemaphoreType.DMA((n,))],
    )(*blocks)


HBM = pl.BlockSpec(memory_space=pltpu.HBM)
SEM = pl.BlockSpec(memory_space=pltpu.SEMAPHORE)
EFFECT = pltpu.SideEffectType.DATAFLOW_SIDE_EFFECTING


def _peer_of(k, x, y, c):
    return (1 - x if k & 4 else x, 1 - y if k & 2 else y, 1 - c if k & 1 else c)


def _flight(a, k):
    return a * (N_DEV - 1) + k - 1


def _exchange_start(arrays, *, name, broadcast=False):
    n = len(arrays)

    def body(*refs):
        ins, lands = refs[:n], refs[n:2 * n]
        send_sems, recv_sems = refs[2 * n:2 * n + 2]
        token = refs[-1]
        x, y, c = _position()
        me = _slot((x, y, c))
        for k in range(1, N_DEV):
            peer = _peer_of(k, x, y, c)
            for a in range(n):
                pltpu.make_async_remote_copy(
                    src_ref=ins[a] if broadcast else ins[a].at[_slot(peer)], dst_ref=lands[a].at[me],
                    send_sem=send_sems.at[_flight(a, k)], recv_sem=recv_sems.at[_flight(a, k)],
                    device_id=peer, device_id_type=MESH).start()
        token[...] = jnp.zeros_like(token)

    land_shapes = [((N_DEV,) + s.shape) if broadcast else s.shape for s in arrays]
    lands = [pltpu.with_memory_space_constraint(jnp.zeros(shp, s.dtype), pltpu.HBM) for shp, s in zip(land_shapes, arrays)]
    srcs = [pltpu.with_memory_space_constraint(s, pltpu.HBM) for s in arrays]
    outs = pl.pallas_call(
        body, name=name, in_specs=[HBM] * (2 * n),
        out_specs=[SEM, SEM] + [HBM] * (2 * n) + [pl.BlockSpec(memory_space=pltpu.VMEM)],
        out_shape=[pltpu.SemaphoreType.DMA((n * (N_DEV - 1),)), pltpu.SemaphoreType.DMA((n * (N_DEV - 1),))]
        + [pltpu.HBM(s.shape, s.dtype) for s in arrays] + [pltpu.HBM(shp, s.dtype) for shp, s in zip(land_shapes, arrays)]
        + [_sds((8, 128))],
        input_output_aliases={i: 2 + i for i in range(2 * n)},
        compiler_params=pltpu.CompilerParams(has_side_effects=EFFECT),
    )(*srcs, *lands)
    return outs[0], outs[1], outs[2:2 + n], outs[2 + n:2 + 2 * n], outs[-1]


def _exchange_wait(send_sems, recv_sems, srcs, lands, after, *, name, broadcast=False):
    n = len(srcs)

    def body(*refs):
        ins, lnd = refs[:n], refs[n:2 * n]
        send_ref, recv_ref = refs[2 * n:2 * n + 2]
        x, y, c = _position()
        for k in range(1, N_DEV):
            for a in range(n):
                cp = pltpu.make_async_remote_copy(
                    src_ref=ins[a] if broadcast else ins[a].at[0], dst_ref=lnd[a].at[0], send_sem=send_ref.at[_flight(a, k)],
                    recv_sem=recv_ref.at[_flight(a, k)], device_id=_peer_of(k, x, y, c), device_id_type=MESH)
                cp.wait_send()
                cp.wait_recv()

    outs = pl.pallas_call(
        body, name=name, in_specs=[HBM] * (2 * n) + [SEM, SEM, ANY], out_specs=[HBM] * (2 * n),
        out_shape=[pltpu.HBM(s.shape, s.dtype) for s in srcs] + [pltpu.HBM(s.shape, s.dtype) for s in lands],
        input_output_aliases={i: i for i in range(2 * n)},
        compiler_params=pltpu.CompilerParams(has_side_effects=EFFECT),
    )(*srcs, *lands, send_sems, recv_sems, after)
    return outs[:n], outs[n:]


def _behind(x, token):
    return x if token is None else x + token[0, 0].astype(x.dtype)


def _adamw(parts, w, m, v, *, name, own=None, tr=None):
    R, C = w.shape
    tr = R if tr is None else tr
    assert R % tr == 0
    c1 = 1.0 - ADAM_B1 ** ADAM_STEP
    c2 = 1.0 - ADAM_B2 ** ADAM_STEP
    has_own = own is not None

    def body(*refs):
        if has_own:
            own_ref, p_ref, w_ref, m_ref, v_ref, g_ref, d_ref, nm_ref, nv_ref = refs
            g = own_ref[...].astype(F32) + p_ref[0].astype(F32)
        else:
            p_ref, w_ref, m_ref, v_ref, g_ref, d_ref, nm_ref, nv_ref = refs
            g = p_ref[0].astype(F32)
        for s in range(1, N_DEV):
            g = g + p_ref[s].astype(F32)
        nm = ADAM_B1 * m_ref[...] + (1.0 - ADAM_B1) * g
        nv = ADAM_B2 * v_ref[...] + (1.0 - ADAM_B2) * (g * g)
        g_ref[...] = g
        nm_ref[...] = nm
        nv_ref[...] = nv
        d_ref[...] = -ADAM_LR * ((nm / c1) / (jnp.sqrt(nv / c2) + ADAM_EPS) + ADAM_WD * w_ref[...])

    blk = pl.BlockSpec((tr, C), lambda i: (i, 0))
    return pl.pallas_call(
        body, name=name, grid=(R // tr,),
        in_specs=[blk] * has_own + [pl.BlockSpec((N_DEV, tr, C), lambda i: (0, i, 0)), blk, blk, blk],
        out_specs=[blk] * 4, out_shape=[_sds((R, C))] * 4, compiler_params=_params(("parallel",)),
    )(*((own,) if has_own else ()), parts, w, m, v)


_SMALL_ROWS = 8


def _pack_small(norm1, norm2, final, gnw, a_log, dt_bias, loss=None):
    loss = jnp.zeros((1, 128), F32) if loss is None else loss
    row3 = jnp.concatenate([gnw, a_log, dt_bias, jnp.zeros((1, 128 - 2 * GDN_HEADS), F32), loss,
                            jnp.zeros((1, D_MODEL - 3 * 128), F32)], axis=1)
    return jnp.concatenate([norm1, norm2, final, row3, jnp.zeros((_SMALL_ROWS - 4, D_MODEL), F32)], axis=0)


def _unpack_small(p):
    return (p[0:1], p[1:2], p[2], p[3:4, 0:128], p[3:4, 128:128 + GDN_HEADS], p[3:4, 128 + GDN_HEADS:128 + 2 * GDN_HEADS])


def _slabs_by_cols(g):
    r = g.shape[0]
    return g.reshape(r, N_DEV, -1).transpose(1, 0, 2)


def _cols_from_slabs(s):
    return s.transpose(1, 0, 2).reshape(s.shape[1], -1)


def kernel(x, norm1_w, w_in, conv_qkv_w, a_log, dt_bias, gdn_norm_w, w_out, norm2_w, w_up, ffn_conv_w, w_down, final_norm_w, loss_target, m_norm1_w, m_w_in, m_conv_qkv_w, m_a_log, m_dt_bias, m_gdn_norm_w, m_w_out, m_norm2_w, m_w_up, m_ffn_conv_w, m_w_down, m_final_norm_w, v_norm1_w, v_w_in, v_conv_qkv_w, v_a_log, v_dt_bias, v_gdn_norm_w, v_w_out, v_norm2_w, v_w_up, v_ffn_conv_w, v_w_down, v_final_norm_w):
    bf = lambda a: a.astype(BF16)
    me = _slot(_position())
    gw_in, g_conv_a = _all_gather([bf(w_in[0]), conv_qkv_w[0]], name="gather_w_in")
    w_a, w_z, w_b = _split_w_in(_cols_from_slabs(gw_in))
    late_src, _ = lax.optimization_barrier(([bf(w_out[0]), bf(w_up[0]), bf(w_down[0]), ffn_conv_w[0]], gw_in))
    l_send, l_recv, l_srcs, l_lands, l_token = _exchange_start(late_src, name="weights_start", broadcast=True)

    def late_weights(after):
        srcs, landed = _exchange_wait(l_send, l_recv, l_srcs, l_lands, after, name="weights_wait", broadcast=True)
        gw_out, gw_up, gw_down, g_conv_f = [lax.dynamic_update_index_in_dim(l, s, me, 0) for l, s in zip(landed, srcs)]
        return (gw_out.reshape(D_MODEL, D_MODEL), _cols_from_slabs(gw_up), _cols_from_slabs(g_conv_f),
                gw_down.reshape(D_FF, D_MODEL))

    flights = {}

    def emit(group, **grads):
        if group == "in":
            slabs = dict(w_in=_slabs_by_cols(_merge_g_in(grads["w_a"], grads["w_z"], grads["w_b"])),
                         conv_a=_slabs_by_cols(grads["conv_a"]))
        elif group == "up":
            slabs = dict(w_up=_slabs_by_cols(grads["w_up"]), conv_f=_slabs_by_cols(grads["conv_f"]))
        else:
            slabs = {k: v.reshape(N_DEV, -1, D_MODEL) for k, v in grads.items()}
        names = list(slabs)
        own = {k: lax.dynamic_index_in_dim(slabs[k], me, 0, keepdims=False) for k in names}
        *flight, token = _exchange_start([slabs[k] for k in names], name="grads_start_" + group)
        flights[group] = (names, own, flight)
        return token

    loss, grad_x, g = _local_step(
        x[0], loss_target[0], _behind(norm1_w, l_token), w_a, w_z, w_b, _cols_from_slabs(g_conv_a), a_log, dt_bias,
        gdn_norm_w, norm2_w, final_norm_w[None], late_weights, emit)
    (small_all,) = _all_gather(
        [_pack_small(g["norm1"], g["norm2"], g["final"], g["gnw"], g["small"][:, 0:GDN_HEADS],
                     g["small"][:, GDN_HEADS:2 * GDN_HEADS], loss)], name="gather_small")
    got, mine = {}, {}
    for group in ("down", "up", "out", "in"):
        names, own, (send_sems, recv_sems, srcs, lands) = flights[group]
        _, landed = _exchange_wait(send_sems, recv_sems, srcs, lands, small_all, name="grads_wait_" + group)
        got.update(zip(names, landed))
        mine.update(own)
    o_in = _adamw(got["w_in"], w_in[0], m_w_in[0], v_w_in[0], own=mine["w_in"], name="adamw_w_in", tr=128)
    o_out = _adamw(got["w_out"], w_out[0], m_w_out[0], v_w_out[0], own=mine["w_out"], name="adamw_w_out")
    o_up = _adamw(got["w_up"], w_up[0], m_w_up[0], v_w_up[0], own=mine["w_up"], name="adamw_w_up", tr=128)
    o_down = _adamw(got["w_down"], w_down[0], m_w_down[0], v_w_down[0], own=mine["w_down"], name="adamw_w_down", tr=176)
    o_ca = _adamw(got["conv_a"], conv_qkv_w[0], m_conv_qkv_w[0], v_conv_qkv_w[0], own=mine["conv_a"], name="adamw_conv_a")
    o_cf = _adamw(got["conv_f"], ffn_conv_w[0], m_ffn_conv_w[0], v_ffn_conv_w[0], own=mine["conv_f"], name="adamw_conv_f")
    o_small = _adamw(
        small_all, _pack_small(norm1_w, norm2_w, final_norm_w[None], gdn_norm_w, a_log, dt_bias),
        _pack_small(m_norm1_w, m_norm2_w, m_final_norm_w[None], m_gdn_norm_w, m_a_log, m_dt_bias),
        _pack_small(v_norm1_w, v_norm2_w, v_final_norm_w[None], v_gdn_norm_w, v_a_log, v_dt_bias), name="adamw_small")
    total_loss = o_small[0][3, 256]
    outs = [total_loss, grad_x[None]]
    for k in range(4):
        n1, n2, fin, gn, al, dt = _unpack_small(o_small[k])
        outs += [n1, o_in[k][None], o_ca[k][None], al, dt, gn, o_out[k][None], n2, o_up[k][None], o_cf[k][None], o_down[k][None], fin]
    return tuple(outs)
```

```python
import functools

import jax
import jax.numpy as jnp
from jax import lax
from jax.experimental import pallas as pl
from jax.experimental.pallas import tpu as pltpu

F32 = jnp.float32
BF16 = jnp.bfloat16

N_DEV = 8
D_MODEL = 1024
GDN_HEADS = 4
GDN_DIM = 128
GDN_WIDTH = GDN_HEADS * GDN_DIM
GDN_CONV = 4
CHUNK = 64
DIL_HEADS = 8
DIL_DIM = 64
DIL_WIDTH = DIL_HEADS * DIL_DIM
DIL_PAIRS = DIL_HEADS // 2
DILATIONS = (1, 4, 16)
BAND = 128
D_FF = 2816
FFN_CONV = 3
EPS = 1e-6
A_COLS = 3 * GDN_WIDTH + 128
HALO = 8

ADAM_LR = 0.001
ADAM_B1 = 0.9
ADAM_B2 = 0.999
ADAM_EPS = 1e-08
ADAM_WD = 0.01
ADAM_STEP = 10

VMEM_LIMIT_BYTES = 56 * 1024 * 1024
NEG_BIG = -1e30


def _params(sem=None):
    return pltpu.CompilerParams(dimension_semantics=sem, vmem_limit_bytes=VMEM_LIMIT_BYTES)


def _sds(shape, dtype=F32):
    return jax.ShapeDtypeStruct(shape, dtype)


def _bdot(a, b):
    return jnp.dot(a.astype(BF16), b.astype(BF16), preferred_element_type=F32)


def _bdot_nt(a, b):
    return lax.dot_general(a.astype(BF16), b.astype(BF16), (((1,), (1,)), ((), ())), preferred_element_type=F32)


def _bdot_tn(a, b):
    return lax.dot_general(a.astype(BF16), b.astype(BF16), (((0,), (0,)), ((), ())), preferred_element_type=F32)


def _split(a):
    hi = a.astype(BF16)
    lo = (a - hi.astype(F32)).astype(BF16)
    return hi, lo


def _dot3(a, b, dims):
    ah, al = _split(a)
    bh, bl = _split(b)
    d = functools.partial(lax.dot_general, dimension_numbers=(dims, ((), ())), preferred_element_type=F32)
    return d(ah, bh) + (d(al, bh) + d(ah, bl))


def _exact_tri_dot(tri, g):
    g1 = g.astype(BF16)
    r1 = g - g1.astype(F32)
    g2 = r1.astype(BF16)
    g3 = (r1 - g2.astype(F32)).astype(BF16)
    t = tri.astype(BF16)
    d = functools.partial(jnp.dot, preferred_element_type=F32)
    return d(t, g1) + (d(t, g2) + d(t, g3))


def _sigmoid(x):
    return 1.0 / (1.0 + jnp.exp(-x))


def _dsilu(x, sg):
    return sg * (1.0 + x * (1.0 - sg))


def _mm(a, b, *, name, ta=False, tb=False, res=None, out_dtype=F32, tm=512, tn=512, tk=512):
    if ta:
        K, M = a.shape
    else:
        M, K = a.shape
    if tb:
        N, Kb = b.shape
    else:
        Kb, N = b.shape
    assert K == Kb, (a.shape, b.shape)
    tm, tn, tk = min(tm, M), min(tn, N), min(tk, K)
    assert M % tm == 0 and N % tn == 0 and K % tk == 0, (name, M, N, K, tm, tn, tk)
    nk = K // tk
    dims = (((0 if ta else 1,), (1 if tb else 0,)), ((), ()))
    has_res = res is not None

    def body(*refs):
        if has_res:
            a_ref, b_ref, r_ref, o_ref, acc_ref = refs
        else:
            a_ref, b_ref, o_ref, acc_ref = refs
        k = pl.program_id(2)
        part = lax.dot_general(a_ref[...].astype(BF16), b_ref[...].astype(BF16), dims, preferred_element_type=F32)

        @pl.when(k == 0)
        def _():
            acc_ref[...] = part

        @pl.when(k > 0)
        def _():
            acc_ref[...] += part

        @pl.when(k == nk - 1)
        def _():
            r = acc_ref[...]
            if has_res:
                r = r + r_ref[...]
            o_ref[...] = r.astype(out_dtype)

    a_spec = pl.BlockSpec((tk, tm), lambda i, j, k: (k, i)) if ta else pl.BlockSpec((tm, tk), lambda i, j, k: (i, k))
    b_spec = pl.BlockSpec((tn, tk), lambda i, j, k: (j, k)) if tb else pl.BlockSpec((tk, tn), lambda i, j, k: (k, j))
    o_spec = pl.BlockSpec((tm, tn), lambda i, j, k: (i, j))
    in_specs = [a_spec, b_spec] + ([o_spec] if has_res else [])
    args = (a, b) + ((res,) if has_res else ())
    return pl.pallas_call(
        body, name=name, grid=(M // tm, N // tn, nk), in_specs=in_specs, out_specs=o_spec,
        out_shape=_sds((M, N), out_dtype), scratch_shapes=[pltpu.VMEM((tm, tn), F32)],
        compiler_params=_params(("parallel", "parallel", "arbitrary")),
    )(*args)


def _rms_fwd(x, w, *, name, tm=256):
    S, D = x.shape

    def body(x_ref, w_ref, h_ref):
        xv = x_ref[...]
        r = lax.rsqrt(jnp.mean(xv * xv, axis=-1, keepdims=True) + EPS)
        h_ref[...] = (xv * r * w_ref[...]).astype(BF16)

    return pl.pallas_call(
        body, name=name, grid=(S // tm,),
        in_specs=[pl.BlockSpec((tm, D), lambda i: (i, 0)), pl.BlockSpec((1, D), lambda i: (0, 0))],
        out_specs=pl.BlockSpec((tm, D), lambda i: (i, 0)), out_shape=_sds((S, D), BF16),
        compiler_params=_params(("parallel",)),
    )(x, w)


def _rms_bwd(dh, x, w, res, *, name, tm=256):
    S, D = x.shape

    def body(dh_ref, x_ref, w_ref, res_ref, dx_ref, dw_ref):
        i = pl.program_id(0)
        xv = x_ref[...]
        g = dh_ref[...]
        r = lax.rsqrt(jnp.mean(xv * xv, axis=-1, keepdims=True) + EPS)
        xh = xv * r
        gw = g * w_ref[...]
        dx_ref[...] = res_ref[...] + r * (gw - xh * jnp.mean(gw * xh, axis=-1, keepdims=True))
        part = jnp.sum(g * xh, axis=0, keepdims=True)

        @pl.when(i == 0)
        def _():
            dw_ref[...] = part

        @pl.when(i > 0)
        def _():
            dw_ref[...] += part

    row = pl.BlockSpec((tm, D), lambda i: (i, 0))
    one = pl.BlockSpec((1, D), lambda i: (0, 0))
    return pl.pallas_call(
        body, name=name, grid=(S // tm,), in_specs=[row, row, one, row], out_specs=[row, one],
        out_shape=[_sds((S, D)), _sds((1, D))], compiler_params=_params(("arbitrary",)),
    )(dh, x, w, res)


def _loss_head(x2, w, tgt, *, name, tm=256):
    S, D = x2.shape

    def body(x_ref, w_ref, t_ref, dx_ref, dxb_ref, dw_ref, loss_ref):
        i = pl.program_id(0)
        xv = x_ref[...]
        wv = w_ref[...]
        r = lax.rsqrt(jnp.mean(xv * xv, axis=-1, keepdims=True) + EPS)
        xh = xv * r
        err = xh * wv - t_ref[...]
        lrow = jnp.sum(err * err, axis=-1, keepdims=True)
        lsum = jnp.sum(lrow, axis=0, keepdims=True) * (0.5 / D)
        g = err * (1.0 / D)
        gw = g * wv
        dx = r * (gw - xh * jnp.mean(gw * xh, axis=-1, keepdims=True))
        dx_ref[...] = dx
        dxb_ref[...] = dx.astype(BF16)
        part = jnp.sum(g * xh, axis=0, keepdims=True)
        lpart = jnp.broadcast_to(lsum, (1, 128))

        @pl.when(i == 0)
        def _():
            dw_ref[...] = part
            loss_ref[...] = lpart

        @pl.when(i > 0)
        def _():
            dw_ref[...] += part
            loss_ref[...] += lpart

    row = pl.BlockSpec((tm, D), lambda i: (i, 0))
    one = pl.BlockSpec((1, D), lambda i: (0, 0))
    return pl.pallas_call(
        body, name=name, grid=(S // tm,), in_specs=[row, one, row],
        out_specs=[row, row, one, pl.BlockSpec((1, 128), lambda i: (0, 0))],
        out_shape=[_sds((S, D)), _sds((S, D), BF16), _sds((1, D)), _sds((1, 128))], compiler_params=_params(("arbitrary",)),
    )(x2, w, tgt)


def _conv_rows(prev, cur, w, taps):
    n = cur.shape[0]
    xs = jnp.concatenate([prev, cur], axis=0)
    base = HALO - (taps - 1)
    out = xs[base:base + n] * w[0:1]
    for i in range(1, taps):
        out = out + xs[base + i:base + i + n] * w[i:i + 1]
    return out


def _conv_rows_bwd(cur_d, next_d, prev_x, cur_x, w, taps):
    n = cur_d.shape[0]
    ds = jnp.concatenate([cur_d, next_d], axis=0)
    dx = ds[taps - 1:taps - 1 + n] * w[0:1]
    for i in range(1, taps):
        dx = dx + ds[taps - 1 - i:taps - 1 - i + n] * w[i:i + 1]
    xs = jnp.concatenate([prev_x, cur_x], axis=0)
    base = HALO - (taps - 1)
    dws = [jnp.sum(cur_d * xs[base + i:base + i + n], axis=0, keepdims=True) for i in range(taps)]
    return dx, jnp.concatenate(dws, axis=0)


def _halo_specs(tm, width, col, nblk):
    per = tm // HALO
    prev = pl.BlockSpec((HALO, width), lambda i, *_: (jnp.maximum(i * per - 1, 0), col))
    nxt = pl.BlockSpec((HALO, width), lambda i, *_: (jnp.minimum((i + 1) * per, nblk * per - 1), col))
    return prev, nxt


def _softplus(x):
    return jnp.maximum(x, 0.0) + jnp.log1p(jnp.exp(-jnp.abs(x)))


def _chunk_tri(tm, upper=False):
    r = lax.broadcasted_iota(jnp.int32, (tm, tm), 0)
    c = lax.broadcasted_iota(jnp.int32, (tm, tm), 1)
    same = lax.div(r, CHUNK) == lax.div(c, CHUNK)
    order = (c >= r) if upper else (c <= r)
    return jnp.where(same & order, 1.0, 0.0)


def _gdn_prep_fwd(proj_a, conv_w, a_log, dt_bias, *, name, tm=256):
    S = proj_a.shape[0]
    nblk = S // tm
    W3 = 3 * GDN_WIDTH

    def body(cur_ref, prev_ref, ba_ref, cw_ref, al_ref, dt_ref, qn_ref, kn_ref, v_ref, gcb_ref, bb_ref):
        i = pl.program_id(0)
        prev = jnp.where(i > 0, prev_ref[...], 0.0)
        c = _conv_rows(prev, cur_ref[...], cw_ref[...], GDN_CONV)
        a = c * _sigmoid(c)
        ba = ba_ref[...]
        lane = lax.broadcasted_iota(jnp.int32, (tm, 128), 1)
        g4 = jnp.zeros((tm, 128), F32)
        for h in range(GDN_HEADS):
            sl = slice(GDN_DIM * h, GDN_DIM * (h + 1))
            qh = a[:, GDN_DIM * h:GDN_DIM * (h + 1)]
            kh = a[:, GDN_WIDTH + GDN_DIM * h:GDN_WIDTH + GDN_DIM * (h + 1)]
            qn_ref[:, sl] = qh * (lax.rsqrt(jnp.sum(qh * qh, axis=-1, keepdims=True) + EPS) * (GDN_DIM ** -0.5))
            kn_ref[:, sl] = kh * lax.rsqrt(jnp.sum(kh * kh, axis=-1, keepdims=True) + EPS)
            beta = _sigmoid(ba[:, h:h + 1])
            bb_ref[:, sl] = jnp.broadcast_to(beta, (tm, GDN_DIM))
            g = -jnp.exp(al_ref[0:1, h:h + 1]) * _softplus(ba[:, GDN_HEADS + h:GDN_HEADS + h + 1] + dt_ref[0:1, h:h + 1])
            g4 = jnp.where(lane == h, g, g4)
        v_ref[...] = a[:, 2 * GDN_WIDTH:]
        gc = _exact_tri_dot(_chunk_tri(tm), g4)
        for h in range(GDN_HEADS):
            gcb_ref[:, GDN_DIM * h:GDN_DIM * (h + 1)] = jnp.broadcast_to(gc[:, h:h + 1], (tm, GDN_DIM))

    prev_spec, _ = _halo_specs(tm, W3, 0, nblk)
    row = pl.BlockSpec((tm, GDN_WIDTH), lambda i: (i, 0))
    small = lambda a: pl.BlockSpec(a.shape, lambda i: (0, 0))
    return pl.pallas_call(
        body, name=name, grid=(nblk,),
        in_specs=[pl.BlockSpec((tm, W3), lambda i: (i, 0)), prev_spec,
                  pl.BlockSpec((tm, 128), lambda i: (i, W3 // 128)), small(conv_w), small(a_log), small(dt_bias)],
        out_specs=[row] * 5, out_shape=[_sds((S, GDN_WIDTH))] * 5, compiler_params=_params(("parallel",)),
    )(proj_a, proj_a, proj_a, conv_w, a_log, dt_bias)


def _chunk_masks():
    r = lax.broadcasted_iota(jnp.int32, (CHUNK, CHUNK), 0)
    c = lax.broadcasted_iota(jnp.int32, (CHUNK, CHUNK), 1)
    return r >= c, r > c, r == c


def _chunk_decay(gcb_h, bb_h, incl):
    G = gcb_h[:, 0:CHUNK]
    diff = G - G.T
    dec = jnp.where(incl, jnp.exp(jnp.where(incl, diff, 0.0)), 0.0)
    return dec, bb_h[:, 0:CHUNK].T


def _gdn_chunk_fwd(qn, kn, v, gcb, bb, *, name):
    S = qn.shape[0]
    nc = S // CHUNK

    def body(qn_ref, kn_ref, v_ref, gcb_ref, bb_ref, uv_ref, wk_ref, at_ref, t_ref):
        incl, strict, diag = _chunk_masks()
        ats, ts = [], []
        for h in range(GDN_HEADS):
            sl = slice(GDN_DIM * h, GDN_DIM * (h + 1))
            q, k, vv, gh = qn_ref[:, sl], kn_ref[:, sl], v_ref[:, sl], gcb_ref[:, sl]
            dec, bt = _chunk_decay(gh, bb_ref[:, sl], incl)
            lmat = jnp.where(strict, dec * _bdot_nt(k, k) * bt, 0.0)
            p = -lmat
            t = jnp.where(diag, 1.0, 0.0) + p
            for _ in range(5):
                p = _bdot(p, p)
                t = t + _bdot(t, p)
            rhs = jnp.concatenate([vv, jnp.exp(gh) * k], axis=1)
            sol = _dot3(t, rhs, ((1,), (0,)))
            uv_ref[:, sl] = sol[:, :GDN_DIM]
            wk_ref[:, sl] = sol[:, GDN_DIM:]
            ats.append(dec * _bdot_nt(q, k) * bt)
            ts.append(t)
        at_ref[...] = jnp.concatenate(ats, axis=1)
        t_ref[...] = jnp.concatenate(ts, axis=1)

    row = pl.BlockSpec((CHUNK, GDN_WIDTH), lambda n: (n, 0))
    sq = pl.BlockSpec((CHUNK, GDN_HEADS * CHUNK), lambda n: (n, 0))
    return pl.pallas_call(
        body, name=name, grid=(nc,), in_specs=[row] * 5, out_specs=[row, row, sq, sq],
        out_shape=[_sds((S, GDN_WIDTH)), _sds((S, GDN_WIDTH)), _sds((S, GDN_HEADS * CHUNK)), _sds((S, GDN_HEADS * CHUNK))],
        compiler_params=_params(("parallel",)),
    )(qn, kn, v, gcb, bb)


def _gdn_scan_fwd(uv, wk, at, qn, kn, gcb, bb, proj_z, gnw, *, name):
    S = uv.shape[0]
    nc = S // CHUNK

    def body(uv_ref, wk_ref, at_ref, qn_ref, kn_ref, gcb_ref, bb_ref, z_ref, gnw_ref, o_ref, u_ref, sp_ref, oa_ref, st_ref):
        n = pl.program_id(0)

        @pl.when(n == 0)
        def _():
            st_ref[...] = jnp.zeros_like(st_ref)

        oas = []
        for h in range(GDN_HEADS):
            sl = slice(GDN_DIM * h, GDN_DIM * (h + 1))
            st = st_ref[h]
            sp_ref[sl, :] = st
            gh = gcb_ref[:, sl]
            glast = gcb_ref[CHUNK - 1:CHUNK, sl]
            u = uv_ref[:, sl] - _bdot(wk_ref[:, sl], st)
            o = _bdot(qn_ref[:, sl] * jnp.exp(gh), st) + _bdot(at_ref[:, CHUNK * h:CHUNK * (h + 1)], u)
            ke = kn_ref[:, sl] * jnp.exp(glast - gh) * bb_ref[:, sl]
            st_ref[h] = jnp.exp(glast) * st + _bdot_tn(ke, u)
            u_ref[:, sl] = u
            o_ref[:, sl] = o
            z = z_ref[:, sl]
            r = lax.rsqrt(jnp.mean(o * o, axis=-1, keepdims=True) + EPS)
            oas.append(o * r * gnw_ref[...] * (z * _sigmoid(z)))
        oa_ref[...] = jnp.concatenate(oas, axis=1).astype(BF16)

    row = pl.BlockSpec((CHUNK, GDN_WIDTH), lambda n: (n, 0))
    sq = pl.BlockSpec((CHUNK, GDN_HEADS * CHUNK), lambda n: (n, 0))
    return pl.pallas_call(
        body, name=name, grid=(nc,),
        in_specs=[row, row, sq, row, row, row, row, row, pl.BlockSpec((1, GDN_DIM), lambda n: (0, 0))],
        out_specs=[row, row, pl.BlockSpec((GDN_WIDTH, GDN_DIM), lambda n: (n, 0)), row],
        out_shape=[_sds((S, GDN_WIDTH)), _sds((S, GDN_WIDTH)), _sds((nc * GDN_WIDTH, GDN_DIM)), _sds((S, 2 * GDN_WIDTH), BF16)],
        scratch_shapes=[pltpu.VMEM((GDN_HEADS, GDN_DIM, GDN_DIM), F32)],
        compiler_params=_params(("arbitrary",)),
    )(uv, wk, at, qn, kn, gcb, bb, proj_z, gnw)


def _gdn_scan_bwd(d_oab, o, proj_z, gnw, sp, u, wk, at, qn, kn, gcb, bb, *, name):
    S = o.shape[0]
    nc = S // CHUNK

    def body(do_ref, o_ref, z_ref, gnw_ref, sp_ref, u_ref, wk_ref, at_ref, qn_ref, kn_ref, gcb_ref, bb_ref,
             dz_ref, dgn_ref, du_ref, dwk_ref, dat_ref, dqd_ref, dke_ref, dgl_ref, ds_ref):
        n = pl.program_id(0)

        @pl.when(n == 0)
        def _():
            ds_ref[...] = jnp.zeros_like(ds_ref)
            dgn_ref[...] = jnp.zeros_like(dgn_ref)

        dats, dgn = [], jnp.zeros((1, GDN_DIM), F32)
        for h in range(GDN_HEADS):
            sl = slice(GDN_DIM * h, GDN_DIM * (h + 1))
            oo = o_ref[:, sl]
            z = z_ref[:, sl]
            gw = gnw_ref[...]
            sg = _sigmoid(z)
            r = lax.rsqrt(jnp.mean(oo * oo, axis=-1, keepdims=True) + EPS)
            xh = oo * r
            d_oa = do_ref[:, sl]
            dy = d_oa * (z * sg)
            dz_ref[:, sl] = (d_oa * (xh * gw) * _dsilu(z, sg)).astype(BF16)
            dgn = dgn + jnp.sum(dy * xh, axis=0, keepdims=True)
            dxh = dy * gw
            do = r * (dxh - xh * jnp.mean(dxh * xh, axis=-1, keepdims=True))

            st = sp_ref[sl, :]
            dst = ds_ref[h]
            gh = gcb_ref[:, sl]
            glast = gcb_ref[CHUNK - 1:CHUNK, sl]
            uu = u_ref[:, sl]
            ath = at_ref[:, CHUNK * h:CHUNK * (h + 1)]
            qd = qn_ref[:, sl] * jnp.exp(gh)
            ke = kn_ref[:, sl] * jnp.exp(glast - gh) * bb_ref[:, sl]
            ge = jnp.exp(glast)
            dqd_ref[:, sl] = _bdot_nt(do, st)
            dats.append(_bdot_nt(do, uu))
            du = _bdot_tn(ath, do) + _bdot(ke, dst)
            dke_ref[:, sl] = _bdot_nt(uu, dst)
            dge = jnp.sum(jnp.sum(dst * st, axis=1, keepdims=True), axis=0, keepdims=True)
            dgl_ref[0, :, sl] = jnp.broadcast_to(dge * ge, (8, GDN_DIM))
            ds_ref[h] = _bdot_tn(qd, do) + ge * dst - _bdot_tn(wk_ref[:, sl], du)
            du_ref[:, sl] = du
            dwk_ref[:, sl] = -_bdot_nt(du, st)
        dat_ref[...] = jnp.concatenate(dats, axis=1)
        dgn_ref[...] += dgn

    rev = lambda n: (nc - 1 - n, 0)
    row = pl.BlockSpec((CHUNK, GDN_WIDTH), rev)
    sq = pl.BlockSpec((CHUNK, GDN_HEADS * CHUNK), rev)
    one = pl.BlockSpec((1, GDN_DIM), lambda n: (0, 0))
    return pl.pallas_call(
        body, name=name, grid=(nc,),
        in_specs=[row, row, row, one, pl.BlockSpec((GDN_WIDTH, GDN_DIM), rev), row, row, sq, row, row, row, row],
        out_specs=[row, one, row, row, sq, row, row, pl.BlockSpec((1, 8, GDN_WIDTH), lambda n: (nc - 1 - n, 0, 0))],
        out_shape=[_sds((S, GDN_WIDTH), BF16), _sds((1, GDN_DIM)), _sds((S, GDN_WIDTH)), _sds((S, GDN_WIDTH)),
                   _sds((S, GDN_HEADS * CHUNK)), _sds((S, GDN_WIDTH)), _sds((S, GDN_WIDTH)), _sds((nc, 8, GDN_WIDTH))],
        scratch_shapes=[pltpu.VMEM((GDN_HEADS, GDN_DIM, GDN_DIM), F32)],
        compiler_params=_params(("arbitrary",)),
    )(d_oab, o, proj_z, gnw, sp, u, wk, at, qn, kn, gcb, bb)


def _gdn_chunk_bwd(qn, kn, v, gcb, bb, tmat, uv, wk, du, dwk, dat, dqd, dke, dgl, *, name):
    S = qn.shape[0]
    nc = S // CHUNK

    def body(qn_ref, kn_ref, v_ref, gcb_ref, bb_ref, t_ref, uv_ref, wk_ref, du_ref, dwk_ref, dat_ref, dqd_ref, dke_ref,
             dgl_ref, dq_ref, dk_ref, dv_ref, dg_ref, dbeta_ref):
        incl, strict, _ = _chunk_masks()
        lane = lax.broadcasted_iota(jnp.int32, (CHUNK, 128), 1)
        rowi = lax.broadcasted_iota(jnp.int32, (CHUNK, 1), 0)
        dgc4 = jnp.zeros((CHUNK, 128), F32)
        db4 = jnp.zeros((CHUNK, 128), F32)
        for h in range(GDN_HEADS):
            sl = slice(GDN_DIM * h, GDN_DIM * (h + 1))
            sq = slice(CHUNK * h, CHUNK * (h + 1))
            q, k, gh, bh = qn_ref[:, sl], kn_ref[:, sl], gcb_ref[:, sl], bb_ref[:, sl]
            dec, bt = _chunk_decay(gh, bh, incl)
            kk = _bdot_nt(k, k)
            qk = _bdot_nt(q, k)
            t = t_ref[:, sq]
            d_sol = jnp.concatenate([du_ref[:, sl], dwk_ref[:, sl]], axis=1)
            d_rhs = _dot3(t, d_sol, ((0,), (0,)))
            sol = jnp.concatenate([uv_ref[:, sl], wk_ref[:, sl]], axis=1)
            d_l = jnp.where(strict, -_dot3(d_rhs, sol, ((1,), (1,))), 0.0)
            d_a = jnp.where(incl, dat_ref[:, sq], 0.0)
            gam = jnp.exp(gh)
            glast = gcb_ref[CHUNK - 1:CHUNK, sl]
            e = jnp.exp(glast - gh)
            d_gk = d_rhs[:, GDN_DIM:]
            dqd = dqd_ref[:, sl]
            dke = dke_ref[:, sl]
            ml = d_l * dec * bt
            ma = d_a * dec * bt
            dq_ref[:, sl] = _bdot(ma, k) + dqd * gam
            dk_ref[:, sl] = (_bdot(ml, k) + _bdot_tn(ml, k) + _bdot_tn(ma, q)) + d_gk * gam + dke * (e * bh)
            dv_ref[:, sl] = d_rhs[:, :GDN_DIM]
            wb = d_l * dec * kk + d_a * dec * qk
            ew = wb * bt
            s_ke = jnp.sum(dke * k * (e * bh), axis=-1, keepdims=True)
            dbeta = jnp.sum(wb.T, axis=-1, keepdims=True) + jnp.sum(dke * k * e, axis=-1, keepdims=True)
            dgc = (jnp.sum(ew, axis=-1, keepdims=True) - jnp.sum(ew.T, axis=-1, keepdims=True)
                   + jnp.sum(dqd * q * gam, axis=-1, keepdims=True) + jnp.sum(d_gk * k * gam, axis=-1, keepdims=True) - s_ke)
            tail = jnp.sum(s_ke, axis=0, keepdims=True) + dgl_ref[0, 0:1, GDN_DIM * h:GDN_DIM * h + 1]
            dgc = dgc + jnp.where(rowi == CHUNK - 1, tail, 0.0)
            dgc4 = jnp.where(lane == h, dgc, dgc4)
            db4 = jnp.where(lane == h, dbeta, db4)
        dg_ref[...] = _exact_tri_dot(_chunk_tri(CHUNK, upper=True), dgc4)
        dbeta_ref[...] = db4

    row = pl.BlockSpec((CHUNK, GDN_WIDTH), lambda n: (n, 0))
    sq = pl.BlockSpec((CHUNK, GDN_HEADS * CHUNK), lambda n: (n, 0))
    col = pl.BlockSpec((CHUNK, 128), lambda n: (n, 0))
    return pl.pallas_call(
        body, name=name, grid=(nc,),
        in_specs=[row] * 5 + [sq, row, row, row, row, sq, row, row, pl.BlockSpec((1, 8, GDN_WIDTH), lambda n: (n, 0, 0))],
        out_specs=[row, row, row, col, col],
        out_shape=[_sds((S, GDN_WIDTH))] * 3 + [_sds((S, 128))] * 2, compiler_params=_params(("parallel",)),
    )(qn, kn, v, gcb, bb, tmat, uv, wk, du, dwk, dat, dqd, dke, dgl)


def _gdn_prep_bwd(dqn, dkn, dv, dg, dbeta, proj_a, conv_w, a_log, dt_bias, *, name, tm=256):
    S = proj_a.shape[0]
    nblk = S // tm
    W3 = 3 * GDN_WIDTH

    def body(dqn_ref, dkn_ref, dv_ref, dg_ref, dbeta_ref, cur_ref, prev_ref, ba_ref, cw_ref, al_ref, dt_ref,
             dc_ref, dba_ref, sm_ref):
        i = pl.program_id(0)
        prev = jnp.where(i > 0, prev_ref[...], 0.0)
        c = _conv_rows(prev, cur_ref[...], cw_ref[...], GDN_CONV)
        sg = _sigmoid(c)
        a = c * sg
        dsl = _dsilu(c, sg)
        ba = ba_ref[...]
        lane = lax.broadcasted_iota(jnp.int32, (tm, 128), 1)
        lane1 = lax.broadcasted_iota(jnp.int32, (1, 128), 1)
        dba = jnp.zeros((tm, 128), F32)
        sm = jnp.zeros((1, 128), F32)
        for h in range(GDN_HEADS):
            sl = slice(GDN_DIM * h, GDN_DIM * (h + 1))
            ks = slice(GDN_WIDTH + GDN_DIM * h, GDN_WIDTH + GDN_DIM * (h + 1))
            qh, kh = a[:, sl], a[:, ks]
            rq = lax.rsqrt(jnp.sum(qh * qh, axis=-1, keepdims=True) + EPS)
            rk = lax.rsqrt(jnp.sum(kh * kh, axis=-1, keepdims=True) + EPS)
            qhat, khat = qh * rq, kh * rk
            dyq = dqn_ref[:, sl] * (GDN_DIM ** -0.5)
            dyk = dkn_ref[:, sl]
            dq = rq * (dyq - qhat * jnp.sum(dyq * qhat, axis=-1, keepdims=True))
            dk = rk * (dyk - khat * jnp.sum(dyk * khat, axis=-1, keepdims=True))
            dc_ref[:, sl] = dq * dsl[:, sl]
            dc_ref[:, ks] = dk * dsl[:, ks]
            beta = _sigmoid(ba[:, h:h + 1])
            db = dbeta_ref[:, h:h + 1] * beta * (1.0 - beta)
            aneg = -jnp.exp(al_ref[0:1, h:h + 1])
            xa = ba[:, GDN_HEADS + h:GDN_HEADS + h + 1] + dt_ref[0:1, h:h + 1]
            dgh = dg_ref[:, h:h + 1]
            dxa = dgh * aneg * _sigmoid(xa)
            dba = jnp.where(lane == h, db, dba)
            dba = jnp.where(lane == GDN_HEADS + h, dxa, dba)
            d_alog = jnp.sum(dgh * _softplus(xa), axis=0, keepdims=True) * aneg
            sm = jnp.where(lane1 == h, d_alog, sm)
            sm = jnp.where(lane1 == GDN_HEADS + h, jnp.sum(dxa, axis=0, keepdims=True), sm)
        vs = slice(2 * GDN_WIDTH, W3)
        dc_ref[:, vs] = dv_ref[...] * dsl[:, vs]
        dba_ref[...] = dba

        @pl.when(i == 0)
        def _():
            sm_ref[...] = sm

        @pl.when(i > 0)
        def _():
            sm_ref[...] += sm

    prev_spec, _ = _halo_specs(tm, W3, 0, nblk)
    row = pl.BlockSpec((tm, GDN_WIDTH), lambda i: (i, 0))
    col = pl.BlockSpec((tm, 128), lambda i: (i, 0))
    small = lambda a: pl.BlockSpec(a.shape, lambda i: (0, 0))
    return pl.pallas_call(
        body, name=name, grid=(nblk,),
        in_specs=[row, row, row, col, col, pl.BlockSpec((tm, W3), lambda i: (i, 0)), prev_spec,
                  pl.BlockSpec((tm, 128), lambda i: (i, W3 // 128)), small(conv_w), small(a_log), small(dt_bias)],
        out_specs=[pl.BlockSpec((tm, W3), lambda i: (i, 0)), col, pl.BlockSpec((1, 128), lambda i: (0, 0))],
        out_shape=[_sds((S, W3)), _sds((S, 128)), _sds((1, 128))], compiler_params=_params(("arbitrary",)),
    )(dqn, dkn, dv, dg, dbeta, proj_a, proj_a, proj_a, conv_w, a_log, dt_bias)


def _gdn_conv_bwd(dc, dba, proj_a, conv_w, *, name, tm=256):
    S = proj_a.shape[0]
    nblk = S // tm
    W3 = 3 * GDN_WIDTH

    def body(dc_ref, dnext_ref, dba_ref, cur_ref, prev_ref, cw_ref, da_ref, dcw_ref):
        i = pl.program_id(0)
        prev = jnp.where(i > 0, prev_ref[...], 0.0)
        nxt = jnp.where(i < nblk - 1, dnext_ref[...], 0.0)
        dx, dw = _conv_rows_bwd(dc_ref[...], nxt, prev, cur_ref[...], cw_ref[...], GDN_CONV)
        da_ref[:, 0:W3] = dx.astype(BF16)
        da_ref[:, W3:] = dba_ref[...].astype(BF16)

        @pl.when(i == 0)
        def _():
            dcw_ref[...] = dw

        @pl.when(i > 0)
        def _():
            dcw_ref[...] += dw

    prev_spec, next_spec = _halo_specs(tm, W3, 0, nblk)
    wide = pl.BlockSpec((tm, W3), lambda i: (i, 0))
    return pl.pallas_call(
        body, name=name, grid=(nblk,),
        in_specs=[wide, next_spec, pl.BlockSpec((tm, 128), lambda i: (i, 0)), wide, prev_spec,
                  pl.BlockSpec(conv_w.shape, lambda i: (0, 0))],
        out_specs=[pl.BlockSpec((tm, A_COLS), lambda i: (i, 0)), pl.BlockSpec(conv_w.shape, lambda i: (0, 0))],
        out_shape=[_sds((S, A_COLS), BF16), _sds(conv_w.shape)], compiler_params=_params(("arbitrary",)),
    )(dc, dc, dba, proj_a, proj_a, conv_w)


def _band_mask(nk):
    i = lax.broadcasted_iota(jnp.int32, (BAND, nk), 0)
    j = lax.broadcasted_iota(jnp.int32, (BAND, nk), 1)
    if nk == BAND:
        return j <= i
    return (j >= i) & (j <= i + BAND)


def _rows(start, size, stride):
    return pl.ds(start, size) if stride == 1 else pl.ds(start, size, stride=stride)


def _attn_blocks(S, visit):
    for d in DILATIONS:
        nb = S // (d * BAND)

        def per_residue(r, carry, d=d, nb=nb):
            visit(d, r, 0, True)
            if nb > 1:
                def per_block(n, c):
                    visit(d, r, n, False)
                    return c
                lax.fori_loop(1, nb, per_block, 0)
            return carry

        if d == 1:
            per_residue(0, 0)
        else:
            lax.fori_loop(0, d, per_residue, 0)


def _attn_fwd(proj_b, oab, *, name):
    S = proj_b.shape[0]
    scale = DIL_DIM ** -0.5

    def body(q_ref, k_ref, v_ref, oab_in_ref, ob_ref, lse_ref, m_ref, l_ref, acc_ref):
        del oab_in_ref
        lane = lax.broadcasted_iota(jnp.int32, (BAND, 128), 1)
        lo = lane < DIL_DIM
        m_ref[...] = jnp.full_like(m_ref, NEG_BIG)
        l_ref[...] = jnp.zeros_like(l_ref)
        acc_ref[...] = jnp.zeros_like(acc_ref)

        def visit(d, r, n, first):
            nk = BAND if first else 2 * BAND
            qs = r + n * (BAND * d)
            ks = r if first else r + (n - 1) * (BAND * d)
            qrows = _rows(qs, BAND, d)
            krows = _rows(ks, nk, d)
            q = q_ref[qrows, :] * scale
            k = k_ref[krows, :].astype(BF16)
            v = v_ref[krows, :].astype(BF16)
            valid = _band_mask(nk)
            s0 = jnp.where(valid, _bdot_nt(jnp.where(lo, q, 0.0), k), NEG_BIG)
            s1 = jnp.where(valid, _bdot_nt(jnp.where(lo, 0.0, q), k), NEG_BIG)
            m_old = m_ref[qrows, :]
            mb = jnp.where(lo, jnp.max(s0, axis=-1, keepdims=True), jnp.max(s1, axis=-1, keepdims=True))
            m_new = jnp.maximum(m_old, mb)
            alpha = jnp.exp(m_old - m_new)
            p0 = jnp.exp(s0 - m_new[:, 0:1])
            p1 = jnp.exp(s1 - m_new[:, DIL_DIM:DIL_DIM + 1])
            psum = jnp.where(lo, jnp.sum(p0, axis=-1, keepdims=True), jnp.sum(p1, axis=-1, keepdims=True))
            pv = jnp.where(lo, _bdot(p0, v), _bdot(p1, v))
            m_ref[qrows, :] = m_new
            l_ref[qrows, :] = alpha * l_ref[qrows, :] + psum
            acc_ref[qrows, :] = alpha * acc_ref[qrows, :] + pv

        _attn_blocks(S, visit)
        ob_ref[...] = (acc_ref[...] / l_ref[...]).astype(BF16)
        lse_ref[...] = m_ref[...] + jnp.log(l_ref[...])

    part = lambda t: pl.BlockSpec((S, 128), lambda p: (0, 3 * p + t))
    return pl.pallas_call(
        body, name=name, grid=(DIL_PAIRS,),
        in_specs=[part(0), part(1), part(2), pl.BlockSpec(memory_space=pl.ANY)],
        out_specs=[pl.BlockSpec((S, 128), lambda p: (0, GDN_WIDTH // 128 + p)), pl.BlockSpec((S, 128), lambda p: (0, p))],
        out_shape=[_sds(oab.shape, BF16), _sds((S, DIL_WIDTH))],
        scratch_shapes=[pltpu.VMEM((S, 128), F32)] * 3, input_output_aliases={3: 0},
        compiler_params=_params(("parallel",)),
    )(proj_b, proj_b, proj_b, oab)


def _attn_bwd(proj_b, oab, d_oab, lse, *, name):
    S = proj_b.shape[0]
    scale = DIL_DIM ** -0.5

    def body(q_ref, k_ref, v_ref, o_ref, do_ref, lse_ref, dqkv_ref, dq_ref, dk_ref, dv_ref, delta_ref):
        lane = lax.broadcasted_iota(jnp.int32, (BAND, 128), 1)
        lo = lane < DIL_DIM
        dq_ref[...] = jnp.zeros_like(dq_ref)
        dk_ref[...] = jnp.zeros_like(dk_ref)
        dv_ref[...] = jnp.zeros_like(dv_ref)
        prod = do_ref[...] * o_ref[...].astype(F32)
        lo_all = lax.broadcasted_iota(jnp.int32, (S, 128), 1) < DIL_DIM
        delta_ref[...] = jnp.where(lo_all, jnp.sum(jnp.where(lo_all, prod, 0.0), axis=-1, keepdims=True),
                                   jnp.sum(jnp.where(lo_all, 0.0, prod), axis=-1, keepdims=True))

        def visit(d, r, n, first):
            nk = BAND if first else 2 * BAND
            qs = r + n * (BAND * d)
            ks = r if first else r + (n - 1) * (BAND * d)
            qrows = _rows(qs, BAND, d)
            krows = _rows(ks, nk, d)
            q = q_ref[qrows, :] * scale
            k = k_ref[krows, :]
            v = v_ref[krows, :]
            do = do_ref[qrows, :]
            delta_b = delta_ref[qrows, :]
            lse_b = lse_ref[qrows, :]
            valid = _band_mask(nk)
            lkl = lax.broadcasted_iota(jnp.int32, (nk, 128), 1) < DIL_DIM
            dq = jnp.zeros((BAND, 128), F32)
            dk = jnp.zeros((nk, 128), F32)
            dv = jnp.zeros((nk, 128), F32)
            for sel, ksel, lcol in ((lo, lkl, 0), (~lo, ~lkl, DIL_DIM)):
                qh = jnp.where(sel, q, 0.0)
                doh = jnp.where(sel, do, 0.0)
                s = _bdot_nt(qh, k)
                p = jnp.where(valid, jnp.exp(s - lse_b[:, lcol:lcol + 1]), 0.0)
                dp = _bdot_nt(doh, v)
                ds = p * (dp - delta_b[:, lcol:lcol + 1])
                dq = dq + _bdot(ds, jnp.where(ksel, k, 0.0))
                dk = dk + _bdot_tn(ds, qh)
                dv = dv + _bdot_tn(p, doh)
            dq_ref[qrows, :] += dq * scale
            dk_ref[krows, :] += dk
            dv_ref[krows, :] += dv

        _attn_blocks(S, visit)
        dqkv_ref[:, 0:128] = dq_ref[...].astype(BF16)
        dqkv_ref[:, 128:256] = dk_ref[...].astype(BF16)
        dqkv_ref[:, 256:384] = dv_ref[...].astype(BF16)

    half = lambda p: (0, GDN_WIDTH // 128 + p)
    part = lambda t: pl.BlockSpec((S, 128), lambda p: (0, 3 * p + t))
    return pl.pallas_call(
        body, name=name, grid=(DIL_PAIRS,),
        in_specs=[part(0), part(1), part(2), pl.BlockSpec((S, 128), half), pl.BlockSpec((S, 128), half),
                  pl.BlockSpec((S, 128), lambda p: (0, p))],
        out_specs=pl.BlockSpec((S, 384), lambda p: (0, p)), out_shape=_sds((S, 3 * DIL_WIDTH), BF16),
        scratch_shapes=[pltpu.VMEM((S, 128), F32)] * 4, compiler_params=_params(("parallel",)),
    )(proj_b, proj_b, proj_b, oab, d_oab, lse)


FF_SLAB = 2 * D_FF // N_DEV
FF_PAIRS = N_DEV // 2
ROWS16 = 16


def _taps(w, x, base, n):
    out = x[base:base + n] * w[0:1]
    for t in range(1, FFN_CONV):
        out = out + x[base + t:base + t + n] * w[t:t + 1]
    return out


def _ffn_fwd(h2, x1, w_up, conv_w, w_down, *, name, tm=512):
    S, D = h2.shape
    ni = S // tm
    per = tm // ROWS16

    def body(h_ref, hp_ref, x1_ref, wg_ref, wu_ref, cg_ref, cu_ref, wd_ref, x2_ref, ug_ref, uu_ref):
        i, j = pl.program_id(0), pl.program_id(1)
        hv = jnp.concatenate([hp_ref[...], h_ref[...]], axis=0)
        row = lax.broadcasted_iota(jnp.int32, (tm + ROWS16, 1), 0)
        keep = (i > 0) | (row >= ROWS16)

        def branch(w_ref, c_ref, u_ref):
            u = jnp.dot(hv, w_ref[...], preferred_element_type=F32).astype(BF16)
            u_ref[...] = u[ROWS16:]
            return _taps(c_ref[...], jnp.where(keep, u.astype(F32), 0.0), ROWS16 - (FFN_CONV - 1), tm)

        gate = branch(wg_ref, cg_ref, ug_ref)
        up = branch(wu_ref, cu_ref, uu_ref)
        act = (gate * _sigmoid(gate) * up).astype(BF16)
        part = jnp.dot(act, wd_ref[...], preferred_element_type=F32)

        @pl.when(j == 0)
        def _():
            x2_ref[...] = x1_ref[...] + part

        @pl.when(j > 0)
        def _():
            x2_ref[...] += part

    rows = pl.BlockSpec((tm, D), lambda i, j: (i, 0))
    slab = lambda off: pl.BlockSpec((None, D, FF_SLAB), lambda i, j: (j + off, 0, 0))
    cslab = lambda off: pl.BlockSpec((None, FFN_CONV, FF_SLAB), lambda i, j: (j + off, 0, 0))
    uspec = pl.BlockSpec((None, tm, FF_SLAB), lambda i, j: (j, i, 0))
    return pl.pallas_call(
        body, name=name, grid=(ni, FF_PAIRS),
        in_specs=[rows, pl.BlockSpec((ROWS16, D), lambda i, j: (jnp.maximum(i * per - 1, 0), 0)), rows,
                  slab(0), slab(FF_PAIRS), cslab(0), cslab(FF_PAIRS), pl.BlockSpec((FF_SLAB, D), lambda i, j: (j, 0))],
        out_specs=[rows, uspec, uspec],
        out_shape=[_sds((S, D)), _sds((FF_PAIRS, S, FF_SLAB), BF16), _sds((FF_PAIRS, S, FF_SLAB), BF16)],
        compiler_params=_params(("parallel", "arbitrary")),
    )(h2, h2, x1, w_up, w_up, conv_w, conv_w, w_down)


def _ffn_bwd(dx2, h2, ug, uu, conv_w, w_down, *, name, tm=512):
    S, D = h2.shape
    ni = S // tm
    per = tm // ROWS16
    ext = tm + ROWS16

    def body(dx_ref, dxn_ref, h_ref, ug_ref, ugp_ref, ugn_ref, uu_ref, uup_ref, uun_ref, cg_ref, cu_ref, wd_ref,
             dug_ref, duu_ref, gd_ref, gg_ref, gu_ref, dcg_ref, dcu_ref, acc_d, acc_g, acc_u, acc_cg, acc_cu):
        i = pl.program_id(1)

        @pl.when(i == 0)
        def _():
            acc_d[...] = jnp.zeros_like(acc_d)
            acc_g[...] = jnp.zeros_like(acc_g)
            acc_u[...] = jnp.zeros_like(acc_u)
            acc_cg[...] = jnp.zeros_like(acc_cg)
            acc_cu[...] = jnp.zeros_like(acc_cu)

        dx = dx_ref[...]
        dxe = jnp.concatenate([dx, dxn_ref[...]], axis=0)
        row = lax.broadcasted_iota(jnp.int32, (ext, 1), 0)
        live = (i < ni - 1) | (row < tm)
        d_act = jnp.where(live, lax.dot_general(dxe, wd_ref[...], (((1,), (1,)), ((), ())), preferred_element_type=F32), 0.0)
        rowp = lax.broadcasted_iota(jnp.int32, (ext + ROWS16, 1), 0)
        keep = (i > 0) | (rowp >= ROWS16)

        def pre(cur, prev, nxt):
            return jnp.where(keep, jnp.concatenate([prev[...], cur[...], nxt[...]], axis=0).astype(F32), 0.0)

        uge, uue = pre(ug_ref, ugp_ref, ugn_ref), pre(uu_ref, uup_ref, uun_ref)
        cg, cu = cg_ref[...], cu_ref[...]
        base = ROWS16 - (FFN_CONV - 1)
        gate = _taps(cg, uge, base, ext)
        up = _taps(cu, uue, base, ext)
        sg = _sigmoid(gate)
        silu = gate * sg
        dgc = d_act * up * _dsilu(gate, sg)
        duc = d_act * silu

        def conv_t(w, dc):
            out = dc[FFN_CONV - 1:FFN_CONV - 1 + tm] * w[0:1]
            for t in range(1, FFN_CONV):
                out = out + dc[FFN_CONV - 1 - t:FFN_CONV - 1 - t + tm] * w[t:t + 1]
            return out.astype(BF16)

        du_g, du_u = conv_t(cg, dgc), conv_t(cu, duc)
        dug_ref[...] = du_g
        duu_ref[...] = du_u
        dcw = lambda dc, xe: jnp.concatenate(
            [jnp.sum(dc[0:tm] * xe[base + t:base + t + tm], axis=0, keepdims=True) for t in range(FFN_CONV)], axis=0)
        acc_cg[0:FFN_CONV, :] += dcw(dgc, uge)
        acc_cu[0:FFN_CONV, :] += dcw(duc, uue)
        tn = (((0,), (0,)), ((), ()))
        act = (silu[0:tm] * up[0:tm]).astype(BF16)
        acc_d[...] += lax.dot_general(act, dx, tn, preferred_element_type=F32)
        hv = h_ref[...]
        acc_g[...] += lax.dot_general(hv, du_g, tn, preferred_element_type=F32)
        acc_u[...] += lax.dot_general(hv, du_u, tn, preferred_element_type=F32)

        @pl.when(i == ni - 1)
        def _():
            gd_ref[...] = acc_d[...].astype(BF16)
            gg_ref[...] = acc_g[...].astype(BF16)
            gu_ref[...] = acc_u[...].astype(BF16)
            dcg_ref[...] = acc_cg[0:FFN_CONV, :]
            dcu_ref[...] = acc_cu[0:FFN_CONV, :]

    last16 = S // ROWS16 - 1
    rows = pl.BlockSpec((tm, D), lambda j, i: (i, 0))
    rows_next = pl.BlockSpec((ROWS16, D), lambda j, i: (jnp.minimum((i + 1) * per, last16), 0))
    u_cur = pl.BlockSpec((None, tm, FF_SLAB), lambda j, i: (j, i, 0))
    u_prev = pl.BlockSpec((None, ROWS16, FF_SLAB), lambda j, i: (j, jnp.maximum(i * per - 1, 0), 0))
    u_next = pl.BlockSpec((None, ROWS16, FF_SLAB), lambda j, i: (j, jnp.minimum((i + 1) * per, last16), 0))
    cslab = lambda off: pl.BlockSpec((None, FFN_CONV, FF_SLAB), lambda j, i: (j + off, 0, 0))
    wslab = pl.BlockSpec((None, D, FF_SLAB), lambda j, i: (j, 0, 0))
    dslab = pl.BlockSpec((None, FFN_CONV, FF_SLAB), lambda j, i: (j, 0, 0))
    return pl.pallas_call(
        body, name=name, grid=(FF_PAIRS, ni),
        in_specs=[rows, rows_next, rows, u_cur, u_prev, u_next, u_cur, u_prev, u_next, cslab(0), cslab(FF_PAIRS),
                  pl.BlockSpec((FF_SLAB, D), lambda j, i: (j, 0))],
        out_specs=[u_cur, u_cur, pl.BlockSpec((FF_SLAB, D), lambda j, i: (j, 0)), wslab, wslab, dslab, dslab],
        out_shape=[_sds((FF_PAIRS, S, FF_SLAB), BF16), _sds((FF_PAIRS, S, FF_SLAB), BF16), _sds((D_FF, D), BF16),
                   _sds((FF_PAIRS, D, FF_SLAB), BF16), _sds((FF_PAIRS, D, FF_SLAB), BF16),
                   _sds((FF_PAIRS, FFN_CONV, FF_SLAB)), _sds((FF_PAIRS, FFN_CONV, FF_SLAB))],
        scratch_shapes=[pltpu.VMEM((FF_SLAB, D), F32), pltpu.VMEM((D, FF_SLAB), F32), pltpu.VMEM((D, FF_SLAB), F32),
                        pltpu.VMEM((8, FF_SLAB), F32), pltpu.VMEM((8, FF_SLAB), F32)],
        compiler_params=_params(("parallel", "arbitrary")),
    )(dx2, dx2, h2, ug, ug, ug, uu, uu, uu, conv_w, conv_w, w_down)


def _mm_slabs(a, w, w_off, *, name, res=None, tm=512, tn=512):
    nk, S, _ = a.shape
    D = w.shape[1]
    has_res = res is not None

    def body(*refs):
        if has_res:
            a_ref, w_ref, r_ref, o_ref, acc_ref = refs
        else:
            a_ref, w_ref, o_ref, acc_ref = refs
        k = pl.program_id(2)
        part = lax.dot_general(a_ref[...], w_ref[...], (((1,), (1,)), ((), ())), preferred_element_type=F32)

        @pl.when(k == 0)
        def _():
            acc_ref[...] = part

        @pl.when(k > 0)
        def _():
            acc_ref[...] += part

        @pl.when(k == nk - 1)
        def _():
            o_ref[...] = acc_ref[...] + r_ref[...] if has_res else acc_ref[...]

    o_spec = pl.BlockSpec((tm, tn), lambda i, j, k: (i, j))
    return pl.pallas_call(
        body, name=name, grid=(S // tm, D // tn, nk),
        in_specs=[pl.BlockSpec((None, tm, FF_SLAB), lambda i, j, k: (k, i, 0)),
                  pl.BlockSpec((None, tn, FF_SLAB), lambda i, j, k: (k + w_off, j, 0))] + ([o_spec] if has_res else []),
        out_specs=o_spec, out_shape=_sds((S, D)), scratch_shapes=[pltpu.VMEM((tm, tn), F32)],
        compiler_params=_params(("parallel", "parallel", "arbitrary")),
    )(*((a, w, res) if has_res else (a, w)))


def _local_step(x, tgt, norm1_w, w_a, w_z, w_b, conv_a, a_log, dt_bias, gnw, norm2_w, final_w, late_weights, emit):
    wgrad = functools.partial(_mm, ta=True, out_dtype=BF16)
    h1 = _rms_fwd(x, norm1_w, name="rms1_fwd")
    proj_a = _mm(h1, w_a, name="proj_a", tn=A_COLS, tk=1024)
    proj_z = _mm(h1, w_z, name="proj_z", tk=1024)
    proj_b = _mm(h1, w_b, name="proj_b", tn=768, tk=1024)
    qn, kn, v, gcb, bb = _gdn_prep_fwd(proj_a, conv_a, a_log, dt_bias, name="gdn_prep_fwd")
    uv, wk, at, tmat = _gdn_chunk_fwd(qn, kn, v, gcb, bb, name="gdn_chunk_fwd")
    o, u, sp, oab = _gdn_scan_fwd(uv, wk, at, qn, kn, gcb, bb, proj_z, gnw, name="gdn_scan_fwd")
    oab, lse = _attn_fwd(proj_b, oab, name="attn_fwd")
    w_out, w_up, conv_f, w_down = late_weights(oab)
    x1 = _mm(oab, w_out, res=x, name="out_proj", tk=1024)
    h2 = _rms_fwd(x1, norm2_w, name="rms2_fwd")
    x2, ug, uu = _ffn_fwd(h2, x1, w_up, conv_f, w_down, name="ffn_fwd")
    dx2, dx2_b, d_final, loss = _loss_head(x2, final_w, tgt, name="loss_head")
    dug, duu, g_down, g_up_g, g_up_u, dcw_g, dcw_u = _ffn_bwd(dx2_b, h2, ug, uu, conv_f, w_down, name="ffn_bwd")
    token = emit("ffn", w_down=g_down, w_up=jnp.concatenate([g_up_g, g_up_u], axis=0),
                 conv_f=jnp.concatenate([dcw_g, dcw_u], axis=0))
    dh2 = _mm_slabs(dug, w_up, 0, name="ffn_up_dx_gate")
    dh2 = _mm_slabs(duu, w_up, FF_PAIRS, res=dh2, name="ffn_up_dx_up")
    dx1, d_norm2 = _rms_bwd(dh2, x1, _behind(norm2_w, token), dx2, name="rms2_bwd")
    d_oab = _mm(dx1, w_out, tb=True, name="out_proj_dx", tk=1024)
    gnw = _behind(gnw, emit("out", w_out=wgrad(oab, dx1, name="out_proj_dw")))
    dz, d_gnw, du, dwk, dat, dqd, dke, dgl = _gdn_scan_bwd(d_oab, o, proj_z, gnw, sp, u, wk, at, qn, kn, gcb, bb, name="gdn_scan_bwd")
    dqn, dkn, dv, dg, dbeta = _gdn_chunk_bwd(qn, kn, v, gcb, bb, tmat, uv, wk, du, dwk, dat, dqd, dke, dgl, name="gdn_chunk_bwd")
    dc, dba, d_small = _gdn_prep_bwd(dqn, dkn, dv, dg, dbeta, proj_a, conv_a, a_log, dt_bias, name="gdn_prep_bwd")
    d_pa, d_conv_a = _gdn_conv_bwd(dc, dba, proj_a, conv_a, name="gdn_conv_bwd")
    d_pb = _attn_bwd(proj_b, oab, d_oab, lse, name="attn_bwd")
    g_a = wgrad(h1, d_pa, name="proj_a_dw", tn=A_COLS)
    g_z = wgrad(h1, dz, name="proj_z_dw")
    g_b = wgrad(h1, d_pb, name="proj_b_dw", tn=768)
    w_z = _behind(w_z, emit("in", w_a=g_a, w_z=g_z, w_b=g_b, conv_a=d_conv_a))
    dh1 = _mm(dz, w_z, tb=True, name="proj_z_dx")
    dh1 = _mm(d_pa, w_a, tb=True, res=dh1, name="proj_a_dx", tk=A_COLS)
    dh1 = _mm(d_pb, w_b, tb=True, res=dh1, name="proj_b_dx", tk=1536)
    grad_x, d_norm1 = _rms_bwd(dh1, x, norm1_w, dx1, name="rms1_bwd")
    small = dict(norm1=d_norm1, small=d_small, gnw=d_gnw, norm2=d_norm2, final=d_final)
    return loss, grad_x, small


_O1 = 3 * GDN_WIDTH
_O2 = _O1 + GDN_WIDTH
_O3 = _O2 + 2 * GDN_HEADS


def _split_w_in(w_in):
    d = w_in.shape[0]
    pad = jnp.zeros((d, A_COLS - _O1 - 2 * GDN_HEADS), w_in.dtype)
    w_a = jnp.concatenate([w_in[:, :_O1], w_in[:, _O2:_O3], pad], axis=1)
    w_b = w_in[:, _O3:].reshape(d, 3, DIL_PAIRS, 128).transpose(0, 2, 1, 3).reshape(d, 3 * DIL_WIDTH)
    return w_a, w_in[:, _O1:_O2], w_b


def _merge_g_in(g_a, g_z, g_b):
    d = g_a.shape[0]
    g_b = g_b.reshape(d, DIL_PAIRS, 3, 128).transpose(0, 2, 1, 3).reshape(d, 3 * DIL_WIDTH)
    return jnp.concatenate([g_a[:, :_O1], g_z, g_a[:, _O1:_O1 + 2 * GDN_HEADS], g_b], axis=1)


MESH = pl.DeviceIdType.MESH
ANY = pl.BlockSpec(memory_space=pl.ANY)


def _position():
    return lax.axis_index("x"), lax.axis_index("y"), lax.axis_index("c")


def _slot(p):
    return 4 * p[0] + 2 * p[1] + p[2]


def _all_gather(blocks, *, name):
    n = len(blocks)

    def body(*refs):
        ins, outs = refs[:n], refs[n:2 * n]
        send_sems, recv_sems, local_sems = refs[2 * n:]
        x, y, c = _position()
        me, sibling = (x, y, c), (x, y, 1 - c)
        chips = [(1 - x, y), (x, 1 - y), (1 - x, 1 - y)]

        def copy(a, k, block, to, src=None):
            dst = outs[a].at[_slot(block)]
            return pltpu.make_async_remote_copy(
                src_ref=dst if src is None else src, dst_ref=dst, send_sem=send_sems.at[a, k], recv_sem=recv_sems.at[a, k],
                device_id=to, device_id_type=MESH)

        mine = [pltpu.make_async_copy(ins[a], outs[a].at[_slot(me)], local_sems.at[a]) for a in range(n)]
        for cp in mine:
            cp.start()
        first = []
        for a in range(n):
            first.append(copy(a, 0, me, sibling, src=ins[a]))
            first += [copy(a, 1 + j, me, (*chip, c), src=ins[a]) for j, chip in enumerate(chips)]
        for cp in first:
            cp.start()
        passed = []
        for j, chip in enumerate(chips):
            for a in range(n):
                copy(a, 1 + j, (*chip, c), me).wait_recv()
                fwd = copy(a, 4 + j, (*chip, c), sibling)
                fwd.start()
                passed.append(fwd)
        for a in range(n):
            copy(a, 0, sibling, me).wait_recv()
            for j, chip in enumerate(chips):
                copy(a, 4 + j, (*chip, 1 - c), me).wait_recv()
        for cp in first + passed:
            cp.wait_send()
        for cp in mine:
            cp.wait()

    return pl.pallas_call(
        body, name=name, in_specs=[ANY] * n, out_specs=[ANY] * n,
        out_shape=[_sds((N_DEV,) + b.shape, b.dtype) for b in blocks],
        scratch_shapes=[pltpu.SemaphoreType.DMA((n, 7)), pltpu.SemaphoreType.DMA((n, 7)), pltpu.SemaphoreType.DMA((n,))],
    )(*blocks)


HBM = pl.BlockSpec(memory_space=pltpu.HBM)
SEM = pl.BlockSpec(memory_space=pltpu.SEMAPHORE)
EFFECT = pltpu.SideEffectType.DATAFLOW_SIDE_EFFECTING


def _peer_of(k, x, y, c):
    return (1 - x if k & 4 else x, 1 - y if k & 2 else y, 1 - c if k & 1 else c)


def _flight(a, k):
    return a * (N_DEV - 1) + k - 1


def _exchange_start(arrays, *, name, broadcast=False):
    n = len(arrays)

    def body(*refs):
        ins, lands = refs[:n], refs[n:2 * n]
        send_sems, recv_sems = refs[2 * n:2 * n + 2]
        token = refs[-1]
        x, y, c = _position()
        me = _slot((x, y, c))
        for k in range(1, N_DEV):
            peer = _peer_of(k, x, y, c)
            for a in range(n):
                pltpu.make_async_remote_copy(
                    src_ref=ins[a] if broadcast else ins[a].at[_slot(peer)], dst_ref=lands[a].at[me],
                    send_sem=send_sems.at[_flight(a, k)], recv_sem=recv_sems.at[_flight(a, k)],
                    device_id=peer, device_id_type=MESH).start()
        token[...] = jnp.zeros_like(token)

    land_shapes = [((N_DEV,) + s.shape) if broadcast else s.shape for s in arrays]
    lands = [pltpu.with_memory_space_constraint(jnp.zeros(shp, s.dtype), pltpu.HBM) for shp, s in zip(land_shapes, arrays)]
    srcs = [pltpu.with_memory_space_constraint(s, pltpu.HBM) for s in arrays]
    outs = pl.pallas_call(
        body, name=name, in_specs=[HBM] * (2 * n),
        out_specs=[SEM, SEM] + [HBM] * (2 * n) + [pl.BlockSpec(memory_space=pltpu.VMEM)],
        out_shape=[pltpu.SemaphoreType.DMA((n * (N_DEV - 1),)), pltpu.SemaphoreType.DMA((n * (N_DEV - 1),))]
        + [pltpu.HBM(s.shape, s.dtype) for s in arrays] + [pltpu.HBM(shp, s.dtype) for shp, s in zip(land_shapes, arrays)]
        + [_sds((8, 128))],
        input_output_aliases={i: 2 + i for i in range(2 * n)},
        compiler_params=pltpu.CompilerParams(has_side_effects=EFFECT),
    )(*srcs, *lands)
    return outs[0], outs[1], outs[2:2 + n], outs[2 + n:2 + 2 * n], outs[-1]


def _exchange_wait(send_sems, recv_sems, srcs, lands, after, *, name, broadcast=False):
    n = len(srcs)

    def body(*refs):
        ins, lnd = refs[:n], refs[n:2 * n]
        send_ref, recv_ref = refs[2 * n:2 * n + 2]
        x, y, c = _position()
        for k in range(1, N_DEV):
            for a in range(n):
                cp = pltpu.make_async_remote_copy(
                    src_ref=ins[a] if broadcast else ins[a].at[0], dst_ref=lnd[a].at[0], send_sem=send_ref.at[_flight(a, k)],
                    recv_sem=recv_ref.at[_flight(a, k)], device_id=_peer_of(k, x, y, c), device_id_type=MESH)
                cp.wait_send()
                cp.wait_recv()

    outs = pl.pallas_call(
        body, name=name, in_specs=[HBM] * (2 * n) + [SEM, SEM, ANY], out_specs=[HBM] * (2 * n),
        out_shape=[pltpu.HBM(s.shape, s.dtype) for s in srcs] + [pltpu.HBM(s.shape, s.dtype) for s in lands],
        input_output_aliases={i: i for i in range(2 * n)},
        compiler_params=pltpu.CompilerParams(has_side_effects=EFFECT),
    )(*srcs, *lands, send_sems, recv_sems, after)
    return outs[:n], outs[n:]


def _behind(x, token):
    return x if token is None else x + token[0, 0].astype(x.dtype)


def _adamw(parts, w, m, v, *, name, own=None, tr=None):
    R, C = w.shape
    tr = R if tr is None else tr
    assert R % tr == 0
    c1 = 1.0 - ADAM_B1 ** ADAM_STEP
    c2 = 1.0 - ADAM_B2 ** ADAM_STEP
    has_own = own is not None

    def body(*refs):
        if has_own:
            own_ref, p_ref, w_ref, m_ref, v_ref, g_ref, d_ref, nm_ref, nv_ref = refs
            g = own_ref[...].astype(F32) + p_ref[0].astype(F32)
        else:
            p_ref, w_ref, m_ref, v_ref, g_ref, d_ref, nm_ref, nv_ref = refs
            g = p_ref[0].astype(F32)
        for s in range(1, N_DEV):
            g = g + p_ref[s].astype(F32)
        nm = ADAM_B1 * m_ref[...] + (1.0 - ADAM_B1) * g
        nv = ADAM_B2 * v_ref[...] + (1.0 - ADAM_B2) * (g * g)
        g_ref[...] = g
        nm_ref[...] = nm
        nv_ref[...] = nv
        d_ref[...] = -ADAM_LR * ((nm / c1) / (jnp.sqrt(nv / c2) + ADAM_EPS) + ADAM_WD * w_ref[...])

    blk = pl.BlockSpec((tr, C), lambda i: (i, 0))
    return pl.pallas_call(
        body, name=name, grid=(R // tr,),
        in_specs=[blk] * has_own + [pl.BlockSpec((N_DEV, tr, C), lambda i: (0, i, 0)), blk, blk, blk],
        out_specs=[blk] * 4, out_shape=[_sds((R, C))] * 4, compiler_params=_params(("parallel",)),
    )(*((own,) if has_own else ()), parts, w, m, v)


_SMALL_ROWS = 8


def _pack_small(norm1, norm2, final, gnw, a_log, dt_bias, loss=None):
    loss = jnp.zeros((1, 128), F32) if loss is None else loss
    row3 = jnp.concatenate([gnw, a_log, dt_bias, jnp.zeros((1, 128 - 2 * GDN_HEADS), F32), loss,
                            jnp.zeros((1, D_MODEL - 3 * 128), F32)], axis=1)
    return jnp.concatenate([norm1, norm2, final, row3, jnp.zeros((_SMALL_ROWS - 4, D_MODEL), F32)], axis=0)


def _unpack_small(p):
    return (p[0:1], p[1:2], p[2], p[3:4, 0:128], p[3:4, 128:128 + GDN_HEADS], p[3:4, 128 + GDN_HEADS:128 + 2 * GDN_HEADS])


def _slabs_by_cols(g):
    r = g.shape[0]
    return g.reshape(r, N_DEV, -1).transpose(1, 0, 2)


def _cols_from_slabs(s):
    return s.transpose(1, 0, 2).reshape(s.shape[1], -1)


def kernel(x, norm1_w, w_in, conv_qkv_w, a_log, dt_bias, gdn_norm_w, w_out, norm2_w, w_up, ffn_conv_w, w_down, final_norm_w, loss_target, m_norm1_w, m_w_in, m_conv_qkv_w, m_a_log, m_dt_bias, m_gdn_norm_w, m_w_out, m_norm2_w, m_w_up, m_ffn_conv_w, m_w_down, m_final_norm_w, v_norm1_w, v_w_in, v_conv_qkv_w, v_a_log, v_dt_bias, v_gdn_norm_w, v_w_out, v_norm2_w, v_w_up, v_ffn_conv_w, v_w_down, v_final_norm_w):
    bf = lambda a: a.astype(BF16)
    me = _slot(_position())
    gw_in, g_conv_a = _all_gather([bf(w_in[0]), conv_qkv_w[0]], name="gather_w_in")
    w_a, w_z, w_b = _split_w_in(_cols_from_slabs(gw_in))
    late_src, _ = lax.optimization_barrier(([bf(w_out[0]), bf(w_up[0]), bf(w_down[0]), ffn_conv_w[0]], gw_in))
    l_send, l_recv, l_srcs, l_lands, l_token = _exchange_start(late_src, name="weights_start", broadcast=True)

    def late_weights(after):
        srcs, landed = _exchange_wait(l_send, l_recv, l_srcs, l_lands, after, name="weights_wait", broadcast=True)
        gw_out, gw_up, gw_down, g_conv_f = [lax.dynamic_update_index_in_dim(l, s, me, 0) for l, s in zip(landed, srcs)]
        return gw_out.reshape(D_MODEL, D_MODEL), gw_up, g_conv_f, gw_down.reshape(D_FF, D_MODEL)

    flights = {}

    def emit(group, **grads):
        if group == "in":
            slabs = dict(w_in=_slabs_by_cols(_merge_g_in(grads["w_a"], grads["w_z"], grads["w_b"])),
                         conv_a=_slabs_by_cols(grads["conv_a"]))
        elif group == "ffn":
            slabs = dict(w_down=grads["w_down"].reshape(N_DEV, -1, D_MODEL), w_up=grads["w_up"], conv_f=grads["conv_f"])
        else:
            slabs = {k: v.reshape(N_DEV, -1, D_MODEL) for k, v in grads.items()}
        names = list(slabs)
        own = {k: lax.dynamic_index_in_dim(slabs[k], me, 0, keepdims=False) for k in names}
        *flight, token = _exchange_start([slabs[k] for k in names], name="grads_start_" + group)
        flights[group] = (names, own, flight)
        return token

    loss, grad_x, g = _local_step(
        x[0], loss_target[0], _behind(norm1_w, l_token), w_a, w_z, w_b, _cols_from_slabs(g_conv_a), a_log, dt_bias,
        gdn_norm_w, norm2_w, final_norm_w[None], late_weights, emit)
    (small_all,) = _all_gather(
        [_pack_small(g["norm1"], g["norm2"], g["final"], g["gnw"], g["small"][:, 0:GDN_HEADS],
                     g["small"][:, GDN_HEADS:2 * GDN_HEADS], loss)], name="gather_small")
    got, mine = {}, {}
    for group in ("ffn", "out", "in"):
        names, own, (send_sems, recv_sems, srcs, lands) = flights[group]
        _, landed = _exchange_wait(send_sems, recv_sems, srcs, lands, small_all, name="grads_wait_" + group)
        got.update(zip(names, landed))
        mine.update(own)
    o_in = _adamw(got["w_in"], w_in[0], m_w_in[0], v_w_in[0], own=mine["w_in"], name="adamw_w_in", tr=128)
    o_out = _adamw(got["w_out"], w_out[0], m_w_out[0], v_w_out[0], own=mine["w_out"], name="adamw_w_out")
    o_up = _adamw(got["w_up"], w_up[0], m_w_up[0], v_w_up[0], own=mine["w_up"], name="adamw_w_up", tr=128)
    o_down = _adamw(got["w_down"], w_down[0], m_w_down[0], v_w_down[0], own=mine["w_down"], name="adamw_w_down", tr=176)
    o_ca = _adamw(got["conv_a"], conv_qkv_w[0], m_conv_qkv_w[0], v_conv_qkv_w[0], own=mine["conv_a"], name="adamw_conv_a")
    o_cf = _adamw(got["conv_f"], ffn_conv_w[0], m_ffn_conv_w[0], v_ffn_conv_w[0], own=mine["conv_f"], name="adamw_conv_f")
    o_small = _adamw(
        small_all, _pack_small(norm1_w, norm2_w, final_norm_w[None], gdn_norm_w, a_log, dt_bias),
        _pack_small(m_norm1_w, m_norm2_w, m_final_norm_w[None], m_gdn_norm_w, m_a_log, m_dt_bias),
        _pack_small(v_norm1_w, v_norm2_w, v_final_norm_w[None], v_gdn_norm_w, v_a_log, v_dt_bias), name="adamw_small")
    total_loss = o_small[0][3, 256]
    outs = [total_loss, grad_x[None]]
    for k in range(4):
        n1, n2, fin, gn, al, dt = _unpack_small(o_small[k])
        outs += [n1, o_in[k][None], o_ca[k][None], al, dt, gn, o_out[k][None], n2, o_up[k][None], o_cf[k][None], o_down[k][None], fin]
    return tuple(outs)
```

```python
import functools

import jax
import jax.numpy as jnp
from jax import lax
from jax.experimental import pallas as pl
from jax.experimental.pallas import tpu as pltpu

F32 = jnp.float32
BF16 = jnp.bfloat16

N_DEV = 8
D_MODEL = 1024
GDN_HEADS = 4
GDN_DIM = 128
GDN_WIDTH = GDN_HEADS * GDN_DIM
GDN_CONV = 4
CHUNK = 64
CHUNKS_PER_STEP = 2
DIL_HEADS = 8
DIL_DIM = 64
DIL_WIDTH = DIL_HEADS * DIL_DIM
DIL_PAIRS = DIL_HEADS // 2
DILATIONS = (1, 4, 16)
BAND = 128
D_FF = 2816
FFN_CONV = 3
EPS = 1e-6
A_COLS = 3 * GDN_WIDTH + 128
HALO = 8

ADAM_LR = 0.001
ADAM_B1 = 0.9
ADAM_B2 = 0.999
ADAM_EPS = 1e-08
ADAM_WD = 0.01
ADAM_STEP = 10

VMEM_LIMIT_BYTES = 56 * 1024 * 1024
NEG_BIG = -1e30


def _params(sem=None):
    return pltpu.CompilerParams(dimension_semantics=sem, vmem_limit_bytes=VMEM_LIMIT_BYTES)


def _sds(shape, dtype=F32):
    return jax.ShapeDtypeStruct(shape, dtype)


def _bdot(a, b):
    return jnp.dot(a.astype(BF16), b.astype(BF16), preferred_element_type=F32)


def _bdot_nt(a, b):
    return lax.dot_general(a.astype(BF16), b.astype(BF16), (((1,), (1,)), ((), ())), preferred_element_type=F32)


def _bdot_tn(a, b):
    return lax.dot_general(a.astype(BF16), b.astype(BF16), (((0,), (0,)), ((), ())), preferred_element_type=F32)


def _split(a):
    hi = a.astype(BF16)
    lo = (a - hi.astype(F32)).astype(BF16)
    return hi, lo


def _dot3(a, b, dims):
    ah, al = _split(a)
    bh, bl = _split(b)
    d = functools.partial(lax.dot_general, dimension_numbers=(dims, ((), ())), preferred_element_type=F32)
    return d(ah, bh) + (d(al, bh) + d(ah, bl))


def _exact_tri_dot(tri, g):
    g1 = g.astype(BF16)
    r1 = g - g1.astype(F32)
    g2 = r1.astype(BF16)
    g3 = (r1 - g2.astype(F32)).astype(BF16)
    t = tri.astype(BF16)
    d = functools.partial(jnp.dot, preferred_element_type=F32)
    return d(t, g1) + (d(t, g2) + d(t, g3))


def _sigmoid(x):
    return 1.0 / (1.0 + jnp.exp(-x))


def _dsilu(x, sg):
    return sg * (1.0 + x * (1.0 - sg))


def _mm(a, b, *, name, ta=False, tb=False, res=None, out_dtype=F32, tm=512, tn=512, tk=512):
    if ta:
        K, M = a.shape
    else:
        M, K = a.shape
    if tb:
        N, Kb = b.shape
    else:
        Kb, N = b.shape
    assert K == Kb, (a.shape, b.shape)
    tm, tn, tk = min(tm, M), min(tn, N), min(tk, K)
    assert M % tm == 0 and N % tn == 0 and K % tk == 0, (name, M, N, K, tm, tn, tk)
    nk = K // tk
    dims = (((0 if ta else 1,), (1 if tb else 0,)), ((), ()))
    has_res = res is not None

    def body(*refs):
        if has_res:
            a_ref, b_ref, r_ref, o_ref, acc_ref = refs
        else:
            a_ref, b_ref, o_ref, acc_ref = refs
        k = pl.program_id(2)
        part = lax.dot_general(a_ref[...].astype(BF16), b_ref[...].astype(BF16), dims, preferred_element_type=F32)

        @pl.when(k == 0)
        def _():
            acc_ref[...] = part

        @pl.when(k > 0)
        def _():
            acc_ref[...] += part

        @pl.when(k == nk - 1)
        def _():
            r = acc_ref[...]
            if has_res:
                r = r + r_ref[...]
            o_ref[...] = r.astype(out_dtype)

    a_spec = pl.BlockSpec((tk, tm), lambda i, j, k: (k, i)) if ta else pl.BlockSpec((tm, tk), lambda i, j, k: (i, k))
    b_spec = pl.BlockSpec((tn, tk), lambda i, j, k: (j, k)) if tb else pl.BlockSpec((tk, tn), lambda i, j, k: (k, j))
    o_spec = pl.BlockSpec((tm, tn), lambda i, j, k: (i, j))
    in_specs = [a_spec, b_spec] + ([o_spec] if has_res else [])
    args = (a, b) + ((res,) if has_res else ())
    return pl.pallas_call(
        body, name=name, grid=(M // tm, N // tn, nk), in_specs=in_specs, out_specs=o_spec,
        out_shape=_sds((M, N), out_dtype), scratch_shapes=[pltpu.VMEM((tm, tn), F32)],
        compiler_params=_params(("parallel", "parallel", "arbitrary")),
    )(*args)


def _rms_fwd(x, w, *, name, tm=256):
    S, D = x.shape

    def body(x_ref, w_ref, h_ref):
        xv = x_ref[...]
        r = lax.rsqrt(jnp.mean(xv * xv, axis=-1, keepdims=True) + EPS)
        h_ref[...] = (xv * r * w_ref[...]).astype(BF16)

    return pl.pallas_call(
        body, name=name, grid=(S // tm,),
        in_specs=[pl.BlockSpec((tm, D), lambda i: (i, 0)), pl.BlockSpec((1, D), lambda i: (0, 0))],
        out_specs=pl.BlockSpec((tm, D), lambda i: (i, 0)), out_shape=_sds((S, D), BF16),
        compiler_params=_params(("parallel",)),
    )(x, w)


def _rms_bwd(dh, x, w, res, *, name, tm=256):
    S, D = x.shape

    def body(dh_ref, x_ref, w_ref, res_ref, dx_ref, dw_ref):
        i = pl.program_id(0)
        xv = x_ref[...]
        g = dh_ref[...]
        r = lax.rsqrt(jnp.mean(xv * xv, axis=-1, keepdims=True) + EPS)
        xh = xv * r
        gw = g * w_ref[...]
        dx_ref[...] = res_ref[...] + r * (gw - xh * jnp.mean(gw * xh, axis=-1, keepdims=True))
        part = jnp.sum(g * xh, axis=0, keepdims=True)

        @pl.when(i == 0)
        def _():
            dw_ref[...] = part

        @pl.when(i > 0)
        def _():
            dw_ref[...] += part

    row = pl.BlockSpec((tm, D), lambda i: (i, 0))
    one = pl.BlockSpec((1, D), lambda i: (0, 0))
    return pl.pallas_call(
        body, name=name, grid=(S // tm,), in_specs=[row, row, one, row], out_specs=[row, one],
        out_shape=[_sds((S, D)), _sds((1, D))], compiler_params=_params(("arbitrary",)),
    )(dh, x, w, res)


def _loss_head(x2, w, tgt, *, name, tm=256):
    S, D = x2.shape

    def body(x_ref, w_ref, t_ref, dx_ref, dxb_ref, dw_ref, loss_ref):
        i = pl.program_id(0)
        xv = x_ref[...]
        wv = w_ref[...]
        r = lax.rsqrt(jnp.mean(xv * xv, axis=-1, keepdims=True) + EPS)
        xh = xv * r
        err = xh * wv - t_ref[...]
        lrow = jnp.sum(err * err, axis=-1, keepdims=True)
        lsum = jnp.sum(lrow, axis=0, keepdims=True) * (0.5 / D)
        g = err * (1.0 / D)
        gw = g * wv
        dx = r * (gw - xh * jnp.mean(gw * xh, axis=-1, keepdims=True))
        dx_ref[...] = dx
        dxb_ref[...] = dx.astype(BF16)
        part = jnp.sum(g * xh, axis=0, keepdims=True)
        lpart = jnp.broadcast_to(lsum, (1, 128))

        @pl.when(i == 0)
        def _():
            dw_ref[...] = part
            loss_ref[...] = lpart

        @pl.when(i > 0)
        def _():
            dw_ref[...] += part
            loss_ref[...] += lpart

    row = pl.BlockSpec((tm, D), lambda i: (i, 0))
    one = pl.BlockSpec((1, D), lambda i: (0, 0))
    return pl.pallas_call(
        body, name=name, grid=(S // tm,), in_specs=[row, one, row],
        out_specs=[row, row, one, pl.BlockSpec((1, 128), lambda i: (0, 0))],
        out_shape=[_sds((S, D)), _sds((S, D), BF16), _sds((1, D)), _sds((1, 128))], compiler_params=_params(("arbitrary",)),
    )(x2, w, tgt)


def _conv_rows(prev, cur, w, taps):
    n = cur.shape[0]
    xs = jnp.concatenate([prev, cur], axis=0)
    base = HALO - (taps - 1)
    out = xs[base:base + n] * w[0:1]
    for i in range(1, taps):
        out = out + xs[base + i:base + i + n] * w[i:i + 1]
    return out


def _conv_rows_bwd(cur_d, next_d, prev_x, cur_x, w, taps):
    n = cur_d.shape[0]
    ds = jnp.concatenate([cur_d, next_d], axis=0)
    dx = ds[taps - 1:taps - 1 + n] * w[0:1]
    for i in range(1, taps):
        dx = dx + ds[taps - 1 - i:taps - 1 - i + n] * w[i:i + 1]
    xs = jnp.concatenate([prev_x, cur_x], axis=0)
    base = HALO - (taps - 1)
    dws = [jnp.sum(cur_d * xs[base + i:base + i + n], axis=0, keepdims=True) for i in range(taps)]
    return dx, jnp.concatenate(dws, axis=0)


def _halo_specs(tm, width, col, nblk):
    per = tm // HALO
    prev = pl.BlockSpec((HALO, width), lambda i, *_: (jnp.maximum(i * per - 1, 0), col))
    nxt = pl.BlockSpec((HALO, width), lambda i, *_: (jnp.minimum((i + 1) * per, nblk * per - 1), col))
    return prev, nxt


def _softplus(x):
    return jnp.maximum(x, 0.0) + jnp.log1p(jnp.exp(-jnp.abs(x)))


def _chunk_tri(tm, upper=False):
    r = lax.broadcasted_iota(jnp.int32, (tm, tm), 0)
    c = lax.broadcasted_iota(jnp.int32, (tm, tm), 1)
    same = lax.div(r, CHUNK) == lax.div(c, CHUNK)
    order = (c >= r) if upper else (c <= r)
    return jnp.where(same & order, 1.0, 0.0)


def _gdn_prep_fwd(proj_a, conv_w, a_log, dt_bias, *, name, tm=256):
    S = proj_a.shape[0]
    nblk = S // tm
    W3 = 3 * GDN_WIDTH

    def body(cur_ref, prev_ref, ba_ref, cw_ref, al_ref, dt_ref, qn_ref, kn_ref, v_ref, gcb_ref, bb_ref):
        i = pl.program_id(0)
        prev = jnp.where(i > 0, prev_ref[...], 0.0)
        c = _conv_rows(prev, cur_ref[...], cw_ref[...], GDN_CONV)
        a = c * _sigmoid(c)
        ba = ba_ref[...]
        lane = lax.broadcasted_iota(jnp.int32, (tm, 128), 1)
        g4 = jnp.zeros((tm, 128), F32)
        for h in range(GDN_HEADS):
            sl = slice(GDN_DIM * h, GDN_DIM * (h + 1))
            qh = a[:, GDN_DIM * h:GDN_DIM * (h + 1)]
            kh = a[:, GDN_WIDTH + GDN_DIM * h:GDN_WIDTH + GDN_DIM * (h + 1)]
            qn_ref[:, sl] = qh * (lax.rsqrt(jnp.sum(qh * qh, axis=-1, keepdims=True) + EPS) * (GDN_DIM ** -0.5))
            kn_ref[:, sl] = kh * lax.rsqrt(jnp.sum(kh * kh, axis=-1, keepdims=True) + EPS)
            beta = _sigmoid(ba[:, h:h + 1])
            bb_ref[:, sl] = jnp.broadcast_to(beta, (tm, GDN_DIM))
            g = -jnp.exp(al_ref[0:1, h:h + 1]) * _softplus(ba[:, GDN_HEADS + h:GDN_HEADS + h + 1] + dt_ref[0:1, h:h + 1])
            g4 = jnp.where(lane == h, g, g4)
        v_ref[...] = a[:, 2 * GDN_WIDTH:]
        gc = _exact_tri_dot(_chunk_tri(tm), g4)
        for h in range(GDN_HEADS):
            gcb_ref[:, GDN_DIM * h:GDN_DIM * (h + 1)] = jnp.broadcast_to(gc[:, h:h + 1], (tm, GDN_DIM))

    prev_spec, _ = _halo_specs(tm, W3, 0, nblk)
    row = pl.BlockSpec((tm, GDN_WIDTH), lambda i: (i, 0))
    small = lambda a: pl.BlockSpec(a.shape, lambda i: (0, 0))
    return pl.pallas_call(
        body, name=name, grid=(nblk,),
        in_specs=[pl.BlockSpec((tm, W3), lambda i: (i, 0)), prev_spec,
                  pl.BlockSpec((tm, 128), lambda i: (i, W3 // 128)), small(conv_w), small(a_log), small(dt_bias)],
        out_specs=[row] * 5, out_shape=[_sds((S, GDN_WIDTH))] * 5, compiler_params=_params(("parallel",)),
    )(proj_a, proj_a, proj_a, conv_w, a_log, dt_bias)


def _chunk_masks():
    r = lax.broadcasted_iota(jnp.int32, (CHUNK, CHUNK), 0)
    c = lax.broadcasted_iota(jnp.int32, (CHUNK, CHUNK), 1)
    return r >= c, r > c, r == c


def _chunk_decay(gcb_h, bb_h, incl):
    G = gcb_h[:, 0:CHUNK]
    diff = G - G.T
    dec = jnp.where(incl, jnp.exp(jnp.where(incl, diff, 0.0)), 0.0)
    return dec, bb_h[:, 0:CHUNK].T


def _gdn_chunk_fwd(qn, kn, v, gcb, bb, *, name):
    S = qn.shape[0]
    nc = S // CHUNK

    def body(qn_ref, kn_ref, v_ref, gcb_ref, bb_ref, uv_ref, wk_ref, at_ref, t_ref):
        incl, strict, diag = _chunk_masks()
        for c in range(CHUNKS_PER_STEP):
            rows = slice(CHUNK * c, CHUNK * (c + 1))
            ats, ts = [], []
            for h in range(GDN_HEADS):
                sl = slice(GDN_DIM * h, GDN_DIM * (h + 1))
                q, k, vv, gh = qn_ref[rows, sl], kn_ref[rows, sl], v_ref[rows, sl], gcb_ref[rows, sl]
                dec, bt = _chunk_decay(gh, bb_ref[rows, sl], incl)
                lmat = jnp.where(strict, dec * _bdot_nt(k, k) * bt, 0.0)
                p = -lmat
                t = jnp.where(diag, 1.0, 0.0) + p
                for _ in range(5):
                    p = _bdot(p, p)
                    t = t + _bdot(t, p)
                rhs = jnp.concatenate([vv, jnp.exp(gh) * k], axis=1)
                sol = _dot3(t, rhs, ((1,), (0,)))
                uv_ref[rows, sl] = sol[:, :GDN_DIM]
                wk_ref[rows, sl] = sol[:, GDN_DIM:]
                ats.append(dec * _bdot_nt(q, k) * bt)
                ts.append(t)
            at_ref[rows, :] = jnp.concatenate(ats, axis=1)
            t_ref[rows, :] = jnp.concatenate(ts, axis=1)

    step = CHUNKS_PER_STEP * CHUNK
    row = pl.BlockSpec((step, GDN_WIDTH), lambda n: (n, 0))
    sq = pl.BlockSpec((step, GDN_HEADS * CHUNK), lambda n: (n, 0))
    return pl.pallas_call(
        body, name=name, grid=(S // step,), in_specs=[row] * 5, out_specs=[row, row, sq, sq],
        out_shape=[_sds((S, GDN_WIDTH)), _sds((S, GDN_WIDTH)), _sds((S, GDN_HEADS * CHUNK)), _sds((S, GDN_HEADS * CHUNK))],
        compiler_params=_params(("parallel",)),
    )(qn, kn, v, gcb, bb)


def _gdn_scan_fwd(uv, wk, at, qn, kn, gcb, bb, proj_z, gnw, *, name):
    S = uv.shape[0]
    nc = S // CHUNK

    def body(uv_ref, wk_ref, at_ref, qn_ref, kn_ref, gcb_ref, bb_ref, z_ref, gnw_ref, o_ref, u_ref, sp_ref, oa_ref, st_ref):
        n = pl.program_id(0)

        @pl.when(n == 0)
        def _():
            st_ref[...] = jnp.zeros_like(st_ref)

        oas = []
        for h in range(GDN_HEADS):
            sl = slice(GDN_DIM * h, GDN_DIM * (h + 1))
            st = st_ref[h]
            sp_ref[sl, :] = st
            gh = gcb_ref[:, sl]
            glast = gcb_ref[CHUNK - 1:CHUNK, sl]
            u = uv_ref[:, sl] - _bdot(wk_ref[:, sl], st)
            o = _bdot(qn_ref[:, sl] * jnp.exp(gh), st) + _bdot(at_ref[:, CHUNK * h:CHUNK * (h + 1)], u)
            ke = kn_ref[:, sl] * jnp.exp(glast - gh) * bb_ref[:, sl]
            st_ref[h] = jnp.exp(glast) * st + _bdot_tn(ke, u)
            u_ref[:, sl] = u
            o_ref[:, sl] = o
            z = z_ref[:, sl]
            r = lax.rsqrt(jnp.mean(o * o, axis=-1, keepdims=True) + EPS)
            oas.append(o * r * gnw_ref[...] * (z * _sigmoid(z)))
        oa_ref[...] = jnp.concatenate(oas, axis=1).astype(BF16)

    row = pl.BlockSpec((CHUNK, GDN_WIDTH), lambda n: (n, 0))
    sq = pl.BlockSpec((CHUNK, GDN_HEADS * CHUNK), lambda n: (n, 0))
    return pl.pallas_call(
        body, name=name, grid=(nc,),
        in_specs=[row, row, sq, row, row, row, row, row, pl.BlockSpec((1, GDN_DIM), lambda n: (0, 0))],
        out_specs=[row, row, pl.BlockSpec((GDN_WIDTH, GDN_DIM), lambda n: (n, 0)), row],
        out_shape=[_sds((S, GDN_WIDTH)), _sds((S, GDN_WIDTH)), _sds((nc * GDN_WIDTH, GDN_DIM)), _sds((S, 2 * GDN_WIDTH), BF16)],
        scratch_shapes=[pltpu.VMEM((GDN_HEADS, GDN_DIM, GDN_DIM), F32)],
        compiler_params=_params(("arbitrary",)),
    )(uv, wk, at, qn, kn, gcb, bb, proj_z, gnw)


def _gdn_scan_bwd(d_oab, o, proj_z, gnw, sp, u, wk, at, qn, kn, gcb, bb, *, name):
    S = o.shape[0]
    nc = S // CHUNK

    def body(do_ref, o_ref, z_ref, gnw_ref, sp_ref, u_ref, wk_ref, at_ref, qn_ref, kn_ref, gcb_ref, bb_ref,
             dz_ref, dgn_ref, du_ref, dwk_ref, dat_ref, dqd_ref, dke_ref, dgl_ref, ds_ref):
        n = pl.program_id(0)

        @pl.when(n == 0)
        def _():
            ds_ref[...] = jnp.zeros_like(ds_ref)
            dgn_ref[...] = jnp.zeros_like(dgn_ref)

        dats, dgn = [], jnp.zeros((1, GDN_DIM), F32)
        for h in range(GDN_HEADS):
            sl = slice(GDN_DIM * h, GDN_DIM * (h + 1))
            oo = o_ref[:, sl]
            z = z_ref[:, sl]
            gw = gnw_ref[...]
            sg = _sigmoid(z)
            r = lax.rsqrt(jnp.mean(oo * oo, axis=-1, keepdims=True) + EPS)
            xh = oo * r
            d_oa = do_ref[:, sl]
            dy = d_oa * (z * sg)
            dz_ref[:, sl] = (d_oa * (xh * gw) * _dsilu(z, sg)).astype(BF16)
            dgn = dgn + jnp.sum(dy * xh, axis=0, keepdims=True)
            dxh = dy * gw
            do = r * (dxh - xh * jnp.mean(dxh * xh, axis=-1, keepdims=True))

            st = sp_ref[sl, :]
            dst = ds_ref[h]
            gh = gcb_ref[:, sl]
            glast = gcb_ref[CHUNK - 1:CHUNK, sl]
            uu = u_ref[:, sl]
            ath = at_ref[:, CHUNK * h:CHUNK * (h + 1)]
            qd = qn_ref[:, sl] * jnp.exp(gh)
            ke = kn_ref[:, sl] * jnp.exp(glast - gh) * bb_ref[:, sl]
            ge = jnp.exp(glast)
            dqd_ref[:, sl] = _bdot_nt(do, st)
            dats.append(_bdot_nt(do, uu))
            du = _bdot_tn(ath, do) + _bdot(ke, dst)
            dke_ref[:, sl] = _bdot_nt(uu, dst)
            dge = jnp.sum(jnp.sum(dst * st, axis=1, keepdims=True), axis=0, keepdims=True)
            dgl_ref[0, :, sl] = jnp.broadcast_to(dge * ge, (8, GDN_DIM))
            ds_ref[h] = _bdot_tn(qd, do) + ge * dst - _bdot_tn(wk_ref[:, sl], du)
            du_ref[:, sl] = du
            dwk_ref[:, sl] = -_bdot_nt(du, st)
        dat_ref[...] = jnp.concatenate(dats, axis=1)
        dgn_ref[...] += dgn

    rev = lambda n: (nc - 1 - n, 0)
    row = pl.BlockSpec((CHUNK, GDN_WIDTH), rev)
    sq = pl.BlockSpec((CHUNK, GDN_HEADS * CHUNK), rev)
    one = pl.BlockSpec((1, GDN_DIM), lambda n: (0, 0))
    return pl.pallas_call(
        body, name=name, grid=(nc,),
        in_specs=[row, row, row, one, pl.BlockSpec((GDN_WIDTH, GDN_DIM), rev), row, row, sq, row, row, row, row],
        out_specs=[row, one, row, row, sq, row, row, pl.BlockSpec((1, 8, GDN_WIDTH), lambda n: (nc - 1 - n, 0, 0))],
        out_shape=[_sds((S, GDN_WIDTH), BF16), _sds((1, GDN_DIM)), _sds((S, GDN_WIDTH)), _sds((S, GDN_WIDTH)),
                   _sds((S, GDN_HEADS * CHUNK)), _sds((S, GDN_WIDTH)), _sds((S, GDN_WIDTH)), _sds((nc, 8, GDN_WIDTH))],
        scratch_shapes=[pltpu.VMEM((GDN_HEADS, GDN_DIM, GDN_DIM), F32)],
        compiler_params=_params(("arbitrary",)),
    )(d_oab, o, proj_z, gnw, sp, u, wk, at, qn, kn, gcb, bb)


def _gdn_chunk_bwd(qn, kn, v, gcb, bb, tmat, uv, wk, du, dwk, dat, dqd, dke, dgl, *, name):
    S = qn.shape[0]
    nc = S // CHUNK

    def body(qn_ref, kn_ref, v_ref, gcb_ref, bb_ref, t_ref, uv_ref, wk_ref, du_ref, dwk_ref, dat_ref, dqd_ref, dke_ref,
             dgl_ref, dq_ref, dk_ref, dv_ref, dg_ref, dbeta_ref):
        incl, strict, _ = _chunk_masks()
        lane = lax.broadcasted_iota(jnp.int32, (CHUNK, 128), 1)
        rowi = lax.broadcasted_iota(jnp.int32, (CHUNK, 1), 0)
        for c in range(CHUNKS_PER_STEP):
            rows = slice(CHUNK * c, CHUNK * (c + 1))
            last = slice(CHUNK * (c + 1) - 1, CHUNK * (c + 1))
            dgc4 = jnp.zeros((CHUNK, 128), F32)
            db4 = jnp.zeros((CHUNK, 128), F32)
            for h in range(GDN_HEADS):
                sl = slice(GDN_DIM * h, GDN_DIM * (h + 1))
                sq = slice(CHUNK * h, CHUNK * (h + 1))
                q, k, gh, bh = qn_ref[rows, sl], kn_ref[rows, sl], gcb_ref[rows, sl], bb_ref[rows, sl]
                dec, bt = _chunk_decay(gh, bh, incl)
                kk = _bdot_nt(k, k)
                qk = _bdot_nt(q, k)
                t = t_ref[rows, sq]
                d_sol = jnp.concatenate([du_ref[rows, sl], dwk_ref[rows, sl]], axis=1)
                d_rhs = _dot3(t, d_sol, ((0,), (0,)))
                sol = jnp.concatenate([uv_ref[rows, sl], wk_ref[rows, sl]], axis=1)
                d_l = jnp.where(strict, -_dot3(d_rhs, sol, ((1,), (1,))), 0.0)
                d_a = jnp.where(incl, dat_ref[rows, sq], 0.0)
                gam = jnp.exp(gh)
                glast = gcb_ref[last, sl]
                e = jnp.exp(glast - gh)
                d_gk = d_rhs[:, GDN_DIM:]
                dqd = dqd_ref[rows, sl]
                dke = dke_ref[rows, sl]
                ml = d_l * dec * bt
                ma = d_a * dec * bt
                dq_ref[rows, sl] = _bdot(ma, k) + dqd * gam
                dk_ref[rows, sl] = (_bdot(ml, k) + _bdot_tn(ml, k) + _bdot_tn(ma, q)) + d_gk * gam + dke * (e * bh)
                dv_ref[rows, sl] = d_rhs[:, :GDN_DIM]
                wb = d_l * dec * kk + d_a * dec * qk
                ew = wb * bt
                s_ke = jnp.sum(dke * k * (e * bh), axis=-1, keepdims=True)
                dbeta = jnp.sum(wb.T, axis=-1, keepdims=True) + jnp.sum(dke * k * e, axis=-1, keepdims=True)
                dgc = (jnp.sum(ew, axis=-1, keepdims=True) - jnp.sum(ew.T, axis=-1, keepdims=True)
                       + jnp.sum(dqd * q * gam, axis=-1, keepdims=True) + jnp.sum(d_gk * k * gam, axis=-1, keepdims=True) - s_ke)
                tail = jnp.sum(s_ke, axis=0, keepdims=True) + dgl_ref[c, 0:1, GDN_DIM * h:GDN_DIM * h + 1]
                dgc = dgc + jnp.where(rowi == CHUNK - 1, tail, 0.0)
                dgc4 = jnp.where(lane == h, dgc, dgc4)
                db4 = jnp.where(lane == h, dbeta, db4)
            dg_ref[rows, :] = _exact_tri_dot(_chunk_tri(CHUNK, upper=True), dgc4)
            dbeta_ref[rows, :] = db4

    step = CHUNKS_PER_STEP * CHUNK
    row = pl.BlockSpec((step, GDN_WIDTH), lambda n: (n, 0))
    sq = pl.BlockSpec((step, GDN_HEADS * CHUNK), lambda n: (n, 0))
    col = pl.BlockSpec((step, 128), lambda n: (n, 0))
    return pl.pallas_call(
        body, name=name, grid=(S // step,),
        in_specs=[row] * 5 + [sq, row, row, row, row, sq, row, row,
                              pl.BlockSpec((CHUNKS_PER_STEP, 8, GDN_WIDTH), lambda n: (n, 0, 0))],
        out_specs=[row, row, row, col, col],
        out_shape=[_sds((S, GDN_WIDTH))] * 3 + [_sds((S, 128))] * 2, compiler_params=_params(("parallel",)),
    )(qn, kn, v, gcb, bb, tmat, uv, wk, du, dwk, dat, dqd, dke, dgl)


def _gdn_prep_bwd(dqn, dkn, dv, dg, dbeta, proj_a, conv_w, a_log, dt_bias, *, name, tm=256):
    S = proj_a.shape[0]
    nblk = S // tm
    W3 = 3 * GDN_WIDTH

    def body(dqn_ref, dkn_ref, dv_ref, dg_ref, dbeta_ref, cur_ref, prev_ref, ba_ref, cw_ref, al_ref, dt_ref,
             dc_ref, dba_ref, sm_ref):
        i = pl.program_id(0)
        prev = jnp.where(i > 0, prev_ref[...], 0.0)
        c = _conv_rows(prev, cur_ref[...], cw_ref[...], GDN_CONV)
        sg = _sigmoid(c)
        a = c * sg
        dsl = _dsilu(c, sg)
        ba = ba_ref[...]
        lane = lax.broadcasted_iota(jnp.int32, (tm, 128), 1)
        lane1 = lax.broadcasted_iota(jnp.int32, (1, 128), 1)
        dba = jnp.zeros((tm, 128), F32)
        sm = jnp.zeros((1, 128), F32)
        for h in range(GDN_HEADS):
            sl = slice(GDN_DIM * h, GDN_DIM * (h + 1))
            ks = slice(GDN_WIDTH + GDN_DIM * h, GDN_WIDTH + GDN_DIM * (h + 1))
            qh, kh = a[:, sl], a[:, ks]
            rq = lax.rsqrt(jnp.sum(qh * qh, axis=-1, keepdims=True) + EPS)
            rk = lax.rsqrt(jnp.sum(kh * kh, axis=-1, keepdims=True) + EPS)
            qhat, khat = qh * rq, kh * rk
            dyq = dqn_ref[:, sl] * (GDN_DIM ** -0.5)
            dyk = dkn_ref[:, sl]
            dq = rq * (dyq - qhat * jnp.sum(dyq * qhat, axis=-1, keepdims=True))
            dk = rk * (dyk - khat * jnp.sum(dyk * khat, axis=-1, keepdims=True))
            dc_ref[:, sl] = dq * dsl[:, sl]
            dc_ref[:, ks] = dk * dsl[:, ks]
            beta = _sigmoid(ba[:, h:h + 1])
            db = dbeta_ref[:, h:h + 1] * beta * (1.0 - beta)
            aneg = -jnp.exp(al_ref[0:1, h:h + 1])
            xa = ba[:, GDN_HEADS + h:GDN_HEADS + h + 1] + dt_ref[0:1, h:h + 1]
            dgh = dg_ref[:, h:h + 1]
            dxa = dgh * aneg * _sigmoid(xa)
            dba = jnp.where(lane == h, db, dba)
            dba = jnp.where(lane == GDN_HEADS + h, dxa, dba)
            d_alog = jnp.sum(dgh * _softplus(xa), axis=0, keepdims=True) * aneg
            sm = jnp.where(lane1 == h, d_alog, sm)
            sm = jnp.where(lane1 == GDN_HEADS + h, jnp.sum(dxa, axis=0, keepdims=True), sm)
        vs = slice(2 * GDN_WIDTH, W3)
        dc_ref[:, vs] = dv_ref[...] * dsl[:, vs]
        dba_ref[...] = dba

        @pl.when(i == 0)
        def _():
            sm_ref[...] = sm

        @pl.when(i > 0)
        def _():
            sm_ref[...] += sm

    prev_spec, _ = _halo_specs(tm, W3, 0, nblk)
    row = pl.BlockSpec((tm, GDN_WIDTH), lambda i: (i, 0))
    col = pl.BlockSpec((tm, 128), lambda i: (i, 0))
    small = lambda a: pl.BlockSpec(a.shape, lambda i: (0, 0))
    return pl.pallas_call(
        body, name=name, grid=(nblk,),
        in_specs=[row, row, row, col, col, pl.BlockSpec((tm, W3), lambda i: (i, 0)), prev_spec,
                  pl.BlockSpec((tm, 128), lambda i: (i, W3 // 128)), small(conv_w), small(a_log), small(dt_bias)],
        out_specs=[pl.BlockSpec((tm, W3), lambda i: (i, 0)), col, pl.BlockSpec((1, 128), lambda i: (0, 0))],
        out_shape=[_sds((S, W3)), _sds((S, 128)), _sds((1, 128))], compiler_params=_params(("arbitrary",)),
    )(dqn, dkn, dv, dg, dbeta, proj_a, proj_a, proj_a, conv_w, a_log, dt_bias)


def _gdn_conv_bwd(dc, dba, proj_a, conv_w, *, name, tm=256):
    S = proj_a.shape[0]
    nblk = S // tm
    W3 = 3 * GDN_WIDTH

    def body(dc_ref, dnext_ref, dba_ref, cur_ref, prev_ref, cw_ref, da_ref, dcw_ref):
        i = pl.program_id(0)
        prev = jnp.where(i > 0, prev_ref[...], 0.0)
        nxt = jnp.where(i < nblk - 1, dnext_ref[...], 0.0)
        dx, dw = _conv_rows_bwd(dc_ref[...], nxt, prev, cur_ref[...], cw_ref[...], GDN_CONV)
        da_ref[:, 0:W3] = dx.astype(BF16)
        da_ref[:, W3:] = dba_ref[...].astype(BF16)

        @pl.when(i == 0)
        def _():
            dcw_ref[...] = dw

        @pl.when(i > 0)
        def _():
            dcw_ref[...] += dw

    prev_spec, next_spec = _halo_specs(tm, W3, 0, nblk)
    wide = pl.BlockSpec((tm, W3), lambda i: (i, 0))
    return pl.pallas_call(
        body, name=name, grid=(nblk,),
        in_specs=[wide, next_spec, pl.BlockSpec((tm, 128), lambda i: (i, 0)), wide, prev_spec,
                  pl.BlockSpec(conv_w.shape, lambda i: (0, 0))],
        out_specs=[pl.BlockSpec((tm, A_COLS), lambda i: (i, 0)), pl.BlockSpec(conv_w.shape, lambda i: (0, 0))],
        out_shape=[_sds((S, A_COLS), BF16), _sds(conv_w.shape)], compiler_params=_params(("arbitrary",)),
    )(dc, dc, dba, proj_a, proj_a, conv_w)


def _band_mask(nk):
    i = lax.broadcasted_iota(jnp.int32, (BAND, nk), 0)
    j = lax.broadcasted_iota(jnp.int32, (BAND, nk), 1)
    if nk == BAND:
        return j <= i
    return (j >= i) & (j <= i + BAND)


def _rows(start, size, stride):
    return pl.ds(start, size) if stride == 1 else pl.ds(start, size, stride=stride)


ATTN_LANES = 4


def _attn_blocks(S, visit_many):
    for d in DILATIONS:
        nb = S // (d * BAND)
        if d == 1:
            half = nb // 2
            visit_many(d, [(0, 0, True), (0, half, False)])

            def pair(n, c):
                visit_many(1, [(0, n, False), (0, n + half, False)])
                return c
            lax.fori_loop(1, half, pair, 0)
        elif nb > 1:
            assert d == ATTN_LANES
            visit_many(d, [(r, 0, True) for r in range(d)])

            def column(n, c, d=d):
                visit_many(d, [(r, n, False) for r in range(d)])
                return c
            lax.fori_loop(1, nb, column, 0)
        else:
            def group(g, c, d=d):
                visit_many(d, [(g * ATTN_LANES + t, 0, True) for t in range(ATTN_LANES)])
                return c
            lax.fori_loop(0, d // ATTN_LANES, group, 0)


def _attn_fwd(proj_b, oab, *, name):
    S = proj_b.shape[0]
    scale = DIL_DIM ** -0.5

    def body(q_ref, k_ref, v_ref, oab_in_ref, ob_ref, lse_ref, m_ref, l_ref, acc_ref):
        del oab_in_ref
        lane = lax.broadcasted_iota(jnp.int32, (BAND, 128), 1)
        lo = lane < DIL_DIM
        m_ref[...] = jnp.full_like(m_ref, NEG_BIG)
        l_ref[...] = jnp.zeros_like(l_ref)
        acc_ref[...] = jnp.zeros_like(acc_ref)

        def load(d, r, n, first):
            nk = BAND if first else 2 * BAND
            qrows = _rows(r + n * (BAND * d), BAND, d)
            krows = _rows(r if first else r + (n - 1) * (BAND * d), nk, d)
            return dict(nk=nk, qrows=qrows, q=q_ref[qrows, :] * scale, k=k_ref[krows, :].astype(BF16),
                        v=v_ref[krows, :].astype(BF16), m=m_ref[qrows, :], l=l_ref[qrows, :], acc=acc_ref[qrows, :])

        def compute(b):
            valid = _band_mask(b["nk"])
            q, k, v = b["q"], b["k"], b["v"]
            s0 = jnp.where(valid, _bdot_nt(jnp.where(lo, q, 0.0), k), NEG_BIG)
            s1 = jnp.where(valid, _bdot_nt(jnp.where(lo, 0.0, q), k), NEG_BIG)
            mb = jnp.where(lo, jnp.max(s0, axis=-1, keepdims=True), jnp.max(s1, axis=-1, keepdims=True))
            m_new = jnp.maximum(b["m"], mb)
            alpha = jnp.exp(b["m"] - m_new)
            p0 = jnp.exp(s0 - m_new[:, 0:1])
            p1 = jnp.exp(s1 - m_new[:, DIL_DIM:DIL_DIM + 1])
            psum = jnp.where(lo, jnp.sum(p0, axis=-1, keepdims=True), jnp.sum(p1, axis=-1, keepdims=True))
            pv = jnp.where(lo, _bdot(p0, v), _bdot(p1, v))
            return m_new, alpha * b["l"] + psum, alpha * b["acc"] + pv

        def visit_many(d, blocks):
            loaded = [load(d, *blk) for blk in blocks]
            done = [compute(b) for b in loaded]
            for b, (m_new, l_new, acc_new) in zip(loaded, done):
                m_ref[b["qrows"], :] = m_new
                l_ref[b["qrows"], :] = l_new
                acc_ref[b["qrows"], :] = acc_new

        _attn_blocks(S, visit_many)
        ob_ref[...] = (acc_ref[...] / l_ref[...]).astype(BF16)
        lse_ref[...] = m_ref[...] + jnp.log(l_ref[...])

    part = lambda t: pl.BlockSpec((S, 128), lambda p: (0, 3 * p + t))
    return pl.pallas_call(
        body, name=name, grid=(DIL_PAIRS,),
        in_specs=[part(0), part(1), part(2), pl.BlockSpec(memory_space=pl.ANY)],
        out_specs=[pl.BlockSpec((S, 128), lambda p: (0, GDN_WIDTH // 128 + p)), pl.BlockSpec((S, 128), lambda p: (0, p))],
        out_shape=[_sds(oab.shape, BF16), _sds((S, DIL_WIDTH))],
        scratch_shapes=[pltpu.VMEM((S, 128), F32)] * 3, input_output_aliases={3: 0},
        compiler_params=_params(("parallel",)),
    )(proj_b, proj_b, proj_b, oab)


def _attn_bwd(proj_b, oab, d_oab, lse, *, name):
    S = proj_b.shape[0]
    scale = DIL_DIM ** -0.5

    def body(q_ref, k_ref, v_ref, o_ref, do_ref, lse_ref, dqkv_ref, dq_ref, dk_ref, dv_ref, delta_ref):
        lane = lax.broadcasted_iota(jnp.int32, (BAND, 128), 1)
        lo = lane < DIL_DIM
        dq_ref[...] = jnp.zeros_like(dq_ref)
        dk_ref[...] = jnp.zeros_like(dk_ref)
        dv_ref[...] = jnp.zeros_like(dv_ref)
        prod = do_ref[...] * o_ref[...].astype(F32)
        lo_all = lax.broadcasted_iota(jnp.int32, (S, 128), 1) < DIL_DIM
        delta_ref[...] = jnp.where(lo_all, jnp.sum(jnp.where(lo_all, prod, 0.0), axis=-1, keepdims=True),
                                   jnp.sum(jnp.where(lo_all, 0.0, prod), axis=-1, keepdims=True))

        def load(d, r, n, first):
            nk = BAND if first else 2 * BAND
            qrows = _rows(r + n * (BAND * d), BAND, d)
            krows = _rows(r if first else r + (n - 1) * (BAND * d), nk, d)
            return dict(nk=nk, qrows=qrows, krows=krows, q=q_ref[qrows, :] * scale, k=k_ref[krows, :], v=v_ref[krows, :],
                        do=do_ref[qrows, :], delta=delta_ref[qrows, :], lse=lse_ref[qrows, :],
                        dq=dq_ref[qrows, :], dk=dk_ref[krows, :], dv=dv_ref[krows, :])

        def compute(b):
            nk, q, k, v, do = b["nk"], b["q"], b["k"], b["v"], b["do"]
            valid = _band_mask(nk)
            lkl = lax.broadcasted_iota(jnp.int32, (nk, 128), 1) < DIL_DIM
            dq, dk, dv = b["dq"], b["dk"], b["dv"]
            for sel, ksel, lcol in ((lo, lkl, 0), (~lo, ~lkl, DIL_DIM)):
                qh = jnp.where(sel, q, 0.0)
                doh = jnp.where(sel, do, 0.0)
                s = _bdot_nt(qh, k)
                p = jnp.where(valid, jnp.exp(s - b["lse"][:, lcol:lcol + 1]), 0.0)
                dp = _bdot_nt(doh, v)
                ds = p * (dp - b["delta"][:, lcol:lcol + 1])
                dq = dq + _bdot(ds, jnp.where(ksel, k, 0.0)) * scale
                dk = dk + _bdot_tn(ds, qh)
                dv = dv + _bdot_tn(p, doh)
            return dq, dk, dv

        def visit_many(d, blocks):
            loaded = [load(d, *blk) for blk in blocks]
            done = [compute(b) for b in loaded]
            for b, (dq, dk, dv) in zip(loaded, done):
                dq_ref[b["qrows"], :] = dq
                dk_ref[b["krows"], :] = dk
                dv_ref[b["krows"], :] = dv

        _attn_blocks(S, visit_many)
        dqkv_ref[:, 0:128] = dq_ref[...].astype(BF16)
        dqkv_ref[:, 128:256] = dk_ref[...].astype(BF16)
        dqkv_ref[:, 256:384] = dv_ref[...].astype(BF16)

    half = lambda p: (0, GDN_WIDTH // 128 + p)
    part = lambda t: pl.BlockSpec((S, 128), lambda p: (0, 3 * p + t))
    return pl.pallas_call(
        body, name=name, grid=(DIL_PAIRS,),
        in_specs=[part(0), part(1), part(2), pl.BlockSpec((S, 128), half), pl.BlockSpec((S, 128), half),
                  pl.BlockSpec((S, 128), lambda p: (0, p))],
        out_specs=pl.BlockSpec((S, 384), lambda p: (0, p)), out_shape=_sds((S, 3 * DIL_WIDTH), BF16),
        scratch_shapes=[pltpu.VMEM((S, 128), F32)] * 4, compiler_params=_params(("parallel",)),
    )(proj_b, proj_b, proj_b, oab, d_oab, lse)


FF_SLAB = 2 * D_FF // N_DEV
FF_PAIRS = N_DEV // 2
ROWS16 = 16


def _taps(w, x, base, n):
    out = x[base:base + n] * w[0:1]
    for t in range(1, FFN_CONV):
        out = out + x[base + t:base + t + n] * w[t:t + 1]
    return out


def _ffn_fwd(h2, x1, w_up, conv_w, w_down, *, name, tm=512):
    S, D = h2.shape
    ni = S // tm
    per = tm // ROWS16

    def body(h_ref, hp_ref, x1_ref, wg_ref, wu_ref, cg_ref, cu_ref, wd_ref, x2_ref, ug_ref, uu_ref):
        i, j = pl.program_id(0), pl.program_id(1)
        hv = jnp.concatenate([hp_ref[...], h_ref[...]], axis=0)
        row = lax.broadcasted_iota(jnp.int32, (tm + ROWS16, 1), 0)
        keep = (i > 0) | (row >= ROWS16)

        def branch(w_ref, c_ref, u_ref):
            u = jnp.dot(hv, w_ref[...], preferred_element_type=F32).astype(BF16)
            u_ref[...] = u[ROWS16:]
            return _taps(c_ref[...], jnp.where(keep, u.astype(F32), 0.0), ROWS16 - (FFN_CONV - 1), tm)

        gate = branch(wg_ref, cg_ref, ug_ref)
        up = branch(wu_ref, cu_ref, uu_ref)
        act = (gate * _sigmoid(gate) * up).astype(BF16)
        part = jnp.dot(act, wd_ref[...], preferred_element_type=F32)

        @pl.when(j == 0)
        def _():
            x2_ref[...] = x1_ref[...] + part

        @pl.when(j > 0)
        def _():
            x2_ref[...] += part

    rows = pl.BlockSpec((tm, D), lambda i, j: (i, 0))
    slab = lambda off: pl.BlockSpec((None, D, FF_SLAB), lambda i, j: (j + off, 0, 0))
    cslab = lambda off: pl.BlockSpec((None, FFN_CONV, FF_SLAB), lambda i, j: (j + off, 0, 0))
    uspec = pl.BlockSpec((None, tm, FF_SLAB), lambda i, j: (j, i, 0))
    return pl.pallas_call(
        body, name=name, grid=(ni, FF_PAIRS),
        in_specs=[rows, pl.BlockSpec((ROWS16, D), lambda i, j: (jnp.maximum(i * per - 1, 0), 0)), rows,
                  slab(0), slab(FF_PAIRS), cslab(0), cslab(FF_PAIRS), pl.BlockSpec((FF_SLAB, D), lambda i, j: (j, 0))],
        out_specs=[rows, uspec, uspec],
        out_shape=[_sds((S, D)), _sds((FF_PAIRS, S, FF_SLAB), BF16), _sds((FF_PAIRS, S, FF_SLAB), BF16)],
        compiler_params=_params(("parallel", "arbitrary")),
    )(h2, h2, x1, w_up, w_up, conv_w, conv_w, w_down)


def _ffn_bwd(dx2, h2, ug, uu, conv_w, w_down, *, name, tm=512):
    S, D = h2.shape
    ni = S // tm
    per = tm // ROWS16
    ext = tm + ROWS16

    def body(dx_ref, dxn_ref, h_ref, ug_ref, ugp_ref, ugn_ref, uu_ref, uup_ref, uun_ref, cg_ref, cu_ref, wd_ref,
             dug_ref, duu_ref, gd_ref, gg_ref, gu_ref, dcg_ref, dcu_ref, acc_d, acc_g, acc_u, acc_cg, acc_cu):
        i = pl.program_id(1)

        @pl.when(i == 0)
        def _():
            acc_d[...] = jnp.zeros_like(acc_d)
            acc_g[...] = jnp.zeros_like(acc_g)
            acc_u[...] = jnp.zeros_like(acc_u)
            acc_cg[...] = jnp.zeros_like(acc_cg)
            acc_cu[...] = jnp.zeros_like(acc_cu)

        dx = dx_ref[...]
        dxe = jnp.concatenate([dx, dxn_ref[...]], axis=0)
        row = lax.broadcasted_iota(jnp.int32, (ext, 1), 0)
        live = (i < ni - 1) | (row < tm)
        d_act = jnp.where(live, lax.dot_general(dxe, wd_ref[...], (((1,), (1,)), ((), ())), preferred_element_type=F32), 0.0)
        rowp = lax.broadcasted_iota(jnp.int32, (ext + ROWS16, 1), 0)
        keep = (i > 0) | (rowp >= ROWS16)

        def pre(cur, prev, nxt):
            return jnp.where(keep, jnp.concatenate([prev[...], cur[...], nxt[...]], axis=0).astype(F32), 0.0)

        uge, uue = pre(ug_ref, ugp_ref, ugn_ref), pre(uu_ref, uup_ref, uun_ref)
        cg, cu = cg_ref[...], cu_ref[...]
        base = ROWS16 - (FFN_CONV - 1)
        gate = _taps(cg, uge, base, ext)
        up = _taps(cu, uue, base, ext)
        sg = _sigmoid(gate)
        silu = gate * sg
        dgc = d_act * up * _dsilu(gate, sg)
        duc = d_act * silu

        def conv_t(w, dc):
            out = dc[FFN_CONV - 1:FFN_CONV - 1 + tm] * w[0:1]
            for t in range(1, FFN_CONV):
                out = out + dc[FFN_CONV - 1 - t:FFN_CONV - 1 - t + tm] * w[t:t + 1]
            return out.astype(BF16)

        du_g, du_u = conv_t(cg, dgc), conv_t(cu, duc)
        dug_ref[...] = du_g
        duu_ref[...] = du_u
        dcw = lambda dc, xe: jnp.concatenate(
            [jnp.sum(dc[0:tm] * xe[base + t:base + t + tm], axis=0, keepdims=True) for t in range(FFN_CONV)], axis=0)
        acc_cg[0:FFN_CONV, :] += dcw(dgc, uge)
        acc_cu[0:FFN_CONV, :] += dcw(duc, uue)
        tn = (((0,), (0,)), ((), ()))
        act = (silu[0:tm] * up[0:tm]).astype(BF16)
        acc_d[...] += lax.dot_general(act, dx, tn, preferred_element_type=F32)
        hv = h_ref[...]
        acc_g[...] += lax.dot_general(hv, du_g, tn, preferred_element_type=F32)
        acc_u[...] += lax.dot_general(hv, du_u, tn, preferred_element_type=F32)

        @pl.when(i == ni - 1)
        def _():
            gd_ref[...] = acc_d[...].astype(BF16)
            gg_ref[...] = acc_g[...].astype(BF16)
            gu_ref[...] = acc_u[...].astype(BF16)
            dcg_ref[...] = acc_cg[0:FFN_CONV, :]
            dcu_ref[...] = acc_cu[0:FFN_CONV, :]

    last16 = S // ROWS16 - 1
    rows = pl.BlockSpec((tm, D), lambda j, i: (i, 0))
    rows_next = pl.BlockSpec((ROWS16, D), lambda j, i: (jnp.minimum((i + 1) * per, last16), 0))
    u_cur = pl.BlockSpec((None, tm, FF_SLAB), lambda j, i: (j, i, 0))
    u_prev = pl.BlockSpec((None, ROWS16, FF_SLAB), lambda j, i: (j, jnp.maximum(i * per - 1, 0), 0))
    u_next = pl.BlockSpec((None, ROWS16, FF_SLAB), lambda j, i: (j, jnp.minimum((i + 1) * per, last16), 0))
    cslab = lambda off: pl.BlockSpec((None, FFN_CONV, FF_SLAB), lambda j, i: (j + off, 0, 0))
    wslab = pl.BlockSpec((None, D, FF_SLAB), lambda j, i: (j, 0, 0))
    dslab = pl.BlockSpec((None, FFN_CONV, FF_SLAB), lambda j, i: (j, 0, 0))
    return pl.pallas_call(
        body, name=name, grid=(FF_PAIRS, ni),
        in_specs=[rows, rows_next, rows, u_cur, u_prev, u_next, u_cur, u_prev, u_next, cslab(0), cslab(FF_PAIRS),
                  pl.BlockSpec((FF_SLAB, D), lambda j, i: (j, 0))],
        out_specs=[u_cur, u_cur, pl.BlockSpec((FF_SLAB, D), lambda j, i: (j, 0)), wslab, wslab, dslab, dslab],
        out_shape=[_sds((FF_PAIRS, S, FF_SLAB), BF16), _sds((FF_PAIRS, S, FF_SLAB), BF16), _sds((D_FF, D), BF16),
                   _sds((FF_PAIRS, D, FF_SLAB), BF16), _sds((FF_PAIRS, D, FF_SLAB), BF16),
                   _sds((FF_PAIRS, FFN_CONV, FF_SLAB)), _sds((FF_PAIRS, FFN_CONV, FF_SLAB))],
        scratch_shapes=[pltpu.VMEM((FF_SLAB, D), F32), pltpu.VMEM((D, FF_SLAB), F32), pltpu.VMEM((D, FF_SLAB), F32),
                        pltpu.VMEM((8, FF_SLAB), F32), pltpu.VMEM((8, FF_SLAB), F32)],
        compiler_params=_params(("parallel", "arbitrary")),
    )(dx2, dx2, h2, ug, ug, ug, uu, uu, uu, conv_w, conv_w, w_down)


def _mm_slabs(a, w, w_off, *, name, res=None, tm=1024, tn=1024):
    nk, S, _ = a.shape
    D = w.shape[1]
    has_res = res is not None

    def body(*refs):
        if has_res:
            a_ref, w_ref, r_ref, o_ref, acc_ref = refs
        else:
            a_ref, w_ref, o_ref, acc_ref = refs
        k = pl.program_id(2)
        part = lax.dot_general(a_ref[...], w_ref[...], (((1,), (1,)), ((), ())), preferred_element_type=F32)

        @pl.when(k == 0)
        def _():
            acc_ref[...] = part

        @pl.when(k > 0)
        def _():
            acc_ref[...] += part

        @pl.when(k == nk - 1)
        def _():
            o_ref[...] = acc_ref[...] + r_ref[...] if has_res else acc_ref[...]

    o_spec = pl.BlockSpec((tm, tn), lambda i, j, k: (i, j))
    return pl.pallas_call(
        body, name=name, grid=(S // tm, D // tn, nk),
        in_specs=[pl.BlockSpec((None, tm, FF_SLAB), lambda i, j, k: (k, i, 0)),
                  pl.BlockSpec((None, tn, FF_SLAB), lambda i, j, k: (k + w_off, j, 0))] + ([o_spec] if has_res else []),
        out_specs=o_spec, out_shape=_sds((S, D)), scratch_shapes=[pltpu.VMEM((tm, tn), F32)],
        compiler_params=_params(("parallel", "parallel", "arbitrary")),
    )(*((a, w, res) if has_res else (a, w)))


def _local_step(x, tgt, norm1_w, w_a, w_z, w_b, conv_a, a_log, dt_bias, gnw, norm2_w, final_w, late_weights, emit):
    wgrad = functools.partial(_mm, ta=True, out_dtype=BF16)
    h1 = _rms_fwd(x, norm1_w, name="rms1_fwd")
    proj_a = _mm(h1, w_a, name="proj_a", tn=A_COLS, tk=1024)
    proj_z = _mm(h1, w_z, name="proj_z", tk=1024)
    proj_b = _mm(h1, w_b, name="proj_b", tn=768, tk=1024)
    qn, kn, v, gcb, bb = _gdn_prep_fwd(proj_a, conv_a, a_log, dt_bias, name="gdn_prep_fwd")
    uv, wk, at, tmat = _gdn_chunk_fwd(qn, kn, v, gcb, bb, name="gdn_chunk_fwd")
    o, u, sp, oab = _gdn_scan_fwd(uv, wk, at, qn, kn, gcb, bb, proj_z, gnw, name="gdn_scan_fwd")
    oab, lse = _attn_fwd(proj_b, oab, name="attn_fwd")
    w_out, w_up, conv_f, w_down = late_weights(oab)
    x1 = _mm(oab, w_out, res=x, name="out_proj", tk=1024)
    h2 = _rms_fwd(x1, norm2_w, name="rms2_fwd")
    x2, ug, uu = _ffn_fwd(h2, x1, w_up, conv_f, w_down, name="ffn_fwd")
    dx2, dx2_b, d_final, loss = _loss_head(x2, final_w, tgt, name="loss_head")
    dug, duu, g_down, g_up_g, g_up_u, dcw_g, dcw_u = _ffn_bwd(dx2_b, h2, ug, uu, conv_f, w_down, name="ffn_bwd")
    token = emit("ffn", w_down=g_down, w_up=jnp.concatenate([g_up_g, g_up_u], axis=0),
                 conv_f=jnp.concatenate([dcw_g, dcw_u], axis=0))
    dh2 = _mm_slabs(dug, w_up, 0, name="ffn_up_dx_gate")
    dh2 = _mm_slabs(duu, w_up, FF_PAIRS, res=dh2, name="ffn_up_dx_up")
    dx1, d_norm2 = _rms_bwd(dh2, x1, _behind(norm2_w, token), dx2, name="rms2_bwd")
    d_oab = _mm(dx1, w_out, tb=True, name="out_proj_dx", tk=1024)
    gnw = _behind(gnw, emit("out", w_out=wgrad(oab, dx1, name="out_proj_dw")))
    dz, d_gnw, du, dwk, dat, dqd, dke, dgl = _gdn_scan_bwd(d_oab, o, proj_z, gnw, sp, u, wk, at, qn, kn, gcb, bb, name="gdn_scan_bwd")
    dqn, dkn, dv, dg, dbeta = _gdn_chunk_bwd(qn, kn, v, gcb, bb, tmat, uv, wk, du, dwk, dat, dqd, dke, dgl, name="gdn_chunk_bwd")
    dc, dba, d_small = _gdn_prep_bwd(dqn, dkn, dv, dg, dbeta, proj_a, conv_a, a_log, dt_bias, name="gdn_prep_bwd")
    d_pa, d_conv_a = _gdn_conv_bwd(dc, dba, proj_a, conv_a, name="gdn_conv_bwd")
    d_pb = _attn_bwd(proj_b, oab, d_oab, lse, name="attn_bwd")
    g_a = wgrad(h1, d_pa, name="proj_a_dw", tn=A_COLS)
    g_z = wgrad(h1, dz, name="proj_z_dw")
    g_b = wgrad(h1, d_pb, name="proj_b_dw", tn=768)
    w_z = _behind(w_z, emit("in", w_a=g_a, w_z=g_z, w_b=g_b, conv_a=d_conv_a))
    dh1 = _mm(dz, w_z, tb=True, name="proj_z_dx")
    dh1 = _mm(d_pa, w_a, tb=True, res=dh1, name="proj_a_dx", tk=A_COLS)
    dh1 = _mm(d_pb, w_b, tb=True, res=dh1, name="proj_b_dx", tk=1536)
    grad_x, d_norm1 = _rms_bwd(dh1, x, norm1_w, dx1, name="rms1_bwd")
    small = dict(norm1=d_norm1, small=d_small, gnw=d_gnw, norm2=d_norm2, final=d_final)
    return loss, grad_x, small


_O1 = 3 * GDN_WIDTH
_O2 = _O1 + GDN_WIDTH
_O3 = _O2 + 2 * GDN_HEADS


def _split_w_in(w_in):
    d = w_in.shape[0]
    pad = jnp.zeros((d, A_COLS - _O1 - 2 * GDN_HEADS), w_in.dtype)
    w_a = jnp.concatenate([w_in[:, :_O1], w_in[:, _O2:_O3], pad], axis=1)
    w_b = w_in[:, _O3:].reshape(d, 3, DIL_PAIRS, 128).transpose(0, 2, 1, 3).reshape(d, 3 * DIL_WIDTH)
    return w_a, w_in[:, _O1:_O2], w_b


def _merge_g_in(g_a, g_z, g_b):
    d = g_a.shape[0]
    g_b = g_b.reshape(d, DIL_PAIRS, 3, 128).transpose(0, 2, 1, 3).reshape(d, 3 * DIL_WIDTH)
    return jnp.concatenate([g_a[:, :_O1], g_z, g_a[:, _O1:_O1 + 2 * GDN_HEADS], g_b], axis=1)


MESH = pl.DeviceIdType.MESH
ANY = pl.BlockSpec(memory_space=pl.ANY)


def _position():
    return lax.axis_index("x"), lax.axis_index("y"), lax.axis_index("c")


def _slot(p):
    return 4 * p[0] + 2 * p[1] + p[2]


def _all_gather(blocks, *, name):
    n = len(blocks)

    def body(*refs):
        ins, outs = refs[:n], refs[n:2 * n]
        send_sems, recv_sems, local_sems = refs[2 * n:]
        x, y, c = _position()
        me, sibling = (x, y, c), (x, y, 1 - c)
        chips = [(1 - x, y), (x, 1 - y), (1 - x, 1 - y)]

        def copy(a, k, block, to, src=None):
            dst = outs[a].at[_slot(block)]
            return pltpu.make_async_remote_copy(
                src_ref=dst if src is None else src, dst_ref=dst, send_sem=send_sems.at[a, k], recv_sem=recv_sems.at[a, k],
                device_id=to, device_id_type=MESH)

        mine = [pltpu.make_async_copy(ins[a], outs[a].at[_slot(me)], local_sems.at[a]) for a in range(n)]
        for cp in mine:
            cp.start()
        first = []
        for a in range(n):
            first.append(copy(a, 0, me, sibling, src=ins[a]))
            first += [copy(a, 1 + j, me, (*chip, c), src=ins[a]) for j, chip in enumerate(chips)]
        for cp in first:
            cp.start()
        passed = []
        for j, chip in enumerate(chips):
            for a in range(n):
                copy(a, 1 + j, (*chip, c), me).wait_recv()
                fwd = copy(a, 4 + j, (*chip, c), sibling)
                fwd.start()
                passed.append(fwd)
        for a in range(n):
            copy(a, 0, sibling, me).wait_recv()
            for j, chip in enumerate(chips):
                copy(a, 4 + j, (*chip, 1 - c), me).wait_recv()
        for cp in first + passed:
            cp.wait_send()
        for cp in mine:
            cp.wait()

    return pl.pallas_call(
        body, name=name, in_specs=[ANY] * n, out_specs=[ANY] * n,
        out_shape=[_sds((N_DEV,) + b.shape, b.dtype) for b in blocks],
        scratch_shapes=[pltpu.SemaphoreType.DMA((n, 7)), pltpu.SemaphoreType.DMA((n, 7)), pltpu.SemaphoreType.DMA((n,))],
    )(*blocks)


def _gather_direct(block, *, name):
    def body(in_ref, out_ref, send_sems, recv_sems, local_sem):
        x, y, c = _position()
        me = _slot((x, y, c))
        mine = pltpu.make_async_copy(in_ref, out_ref.at[me], local_sem)
        mine.start()
        copies = [pltpu.make_async_remote_copy(
            src_ref=in_ref, dst_ref=out_ref.at[me], send_sem=send_sems.at[k - 1], recv_sem=recv_sems.at[k - 1],
            device_id=_peer_of(k, x, y, c), device_id_type=MESH) for k in range(1, N_DEV)]
        for cp in copies:
            cp.start()
        for cp in copies:
            cp.wait()
        mine.wait()

    return pl.pallas_call(
        body, name=name, in_specs=[pl.BlockSpec(memory_space=pltpu.VMEM)], out_specs=pl.BlockSpec(memory_space=pltpu.VMEM),
        out_shape=_sds((N_DEV,) + block.shape, block.dtype),
        scratch_shapes=[pltpu.SemaphoreType.DMA((N_DEV - 1,)), pltpu.SemaphoreType.DMA((N_DEV - 1,)), pltpu.SemaphoreType.DMA],
    )(block)


HBM = pl.BlockSpec(memory_space=pltpu.HBM)
SEM = pl.BlockSpec(memory_space=pltpu.SEMAPHORE)
EFFECT = pltpu.SideEffectType.DATAFLOW_SIDE_EFFECTING


def _peer_of(k, x, y, c):
    return (1 - x if k & 4 else x, 1 - y if k & 2 else y, 1 - c if k & 1 else c)


def _flight(a, k):
    return a * (N_DEV - 1) + k - 1


def _exchange_start(arrays, *, name, broadcast=False):
    n = len(arrays)

    def body(*refs):
        ins, lands = refs[:n], refs[n:2 * n]
        send_sems, recv_sems = refs[2 * n:2 * n + 2]
        token = refs[-1]
        x, y, c = _position()
        me = _slot((x, y, c))
        for k in range(1, N_DEV):
            peer = _peer_of(k, x, y, c)
            for a in range(n):
                pltpu.make_async_remote_copy(
                    src_ref=ins[a] if broadcast else ins[a].at[_slot(peer)], dst_ref=lands[a].at[me],
                    send_sem=send_sems.at[_flight(a, k)], recv_sem=recv_sems.at[_flight(a, k)],
                    device_id=peer, device_id_type=MESH).start()
        token[...] = jnp.zeros_like(token)

    land_shapes = [((N_DEV,) + s.shape) if broadcast else s.shape for s in arrays]
    lands = [pltpu.with_memory_space_constraint(jnp.zeros(shp, s.dtype), pltpu.HBM) for shp, s in zip(land_shapes, arrays)]
    srcs = [pltpu.with_memory_space_constraint(s, pltpu.HBM) for s in arrays]
    outs = pl.pallas_call(
        body, name=name, in_specs=[HBM] * (2 * n),
        out_specs=[SEM, SEM] + [HBM] * (2 * n) + [pl.BlockSpec(memory_space=pltpu.VMEM)],
        out_shape=[pltpu.SemaphoreType.DMA((n * (N_DEV - 1),)), pltpu.SemaphoreType.DMA((n * (N_DEV - 1),))]
        + [pltpu.HBM(s.shape, s.dtype) for s in arrays] + [pltpu.HBM(shp, s.dtype) for shp, s in zip(land_shapes, arrays)]
        + [_sds((8, 128))],
        input_output_aliases={i: 2 + i for i in range(2 * n)},
        compiler_params=pltpu.CompilerParams(has_side_effects=EFFECT),
    )(*srcs, *lands)
    return outs[0], outs[1], outs[2:2 + n], outs[2 + n:2 + 2 * n], outs[-1]


def _exchange_wait(send_sems, recv_sems, srcs, lands, after, *, name, broadcast=False):
    n = len(srcs)

    def body(*refs):
        ins, lnd = refs[:n], refs[n:2 * n]
        send_ref, recv_ref = refs[2 * n:2 * n + 2]
        x, y, c = _position()
        for k in range(1, N_DEV):
            for a in range(n):
                cp = pltpu.make_async_remote_copy(
                    src_ref=ins[a] if broadcast else ins[a].at[0], dst_ref=lnd[a].at[0], send_sem=send_ref.at[_flight(a, k)],
                    recv_sem=recv_ref.at[_flight(a, k)], device_id=_peer_of(k, x, y, c), device_id_type=MESH)
                cp.wait_send()
                cp.wait_recv()

    outs = pl.pallas_call(
        body, name=name, in_specs=[HBM] * (2 * n) + [SEM, SEM, ANY], out_specs=[HBM] * (2 * n),
        out_shape=[pltpu.HBM(s.shape, s.dtype) for s in srcs] + [pltpu.HBM(s.shape, s.dtype) for s in lands],
        input_output_aliases={i: i for i in range(2 * n)},
        compiler_params=pltpu.CompilerParams(has_side_effects=EFFECT),
    )(*srcs, *lands, send_sems, recv_sems, after)
    return outs[:n], outs[n:]


def _behind(x, token):
    return x if token is None else x + token[0, 0].astype(x.dtype)


def _adamw(parts, w, m, v, *, name, own=None, tr=None):
    R, C = w.shape
    tr = R if tr is None else tr
    assert R % tr == 0
    c1 = 1.0 - ADAM_B1 ** ADAM_STEP
    c2 = 1.0 - ADAM_B2 ** ADAM_STEP
    has_own = own is not None

    def body(*refs):
        if has_own:
            own_ref, p_ref, w_ref, m_ref, v_ref, g_ref, d_ref, nm_ref, nv_ref = refs
            g = own_ref[...].astype(F32) + p_ref[0].astype(F32)
        else:
            p_ref, w_ref, m_ref, v_ref, g_ref, d_ref, nm_ref, nv_ref = refs
            g = p_ref[0].astype(F32)
        for s in range(1, N_DEV):
            g = g + p_ref[s].astype(F32)
        nm = ADAM_B1 * m_ref[...] + (1.0 - ADAM_B1) * g
        nv = ADAM_B2 * v_ref[...] + (1.0 - ADAM_B2) * (g * g)
        g_ref[...] = g
        nm_ref[...] = nm
        nv_ref[...] = nv
        d_ref[...] = -ADAM_LR * ((nm / c1) / (jnp.sqrt(nv / c2) + ADAM_EPS) + ADAM_WD * w_ref[...])

    blk = pl.BlockSpec((tr, C), lambda i: (i, 0))
    return pl.pallas_call(
        body, name=name, grid=(R // tr,),
        in_specs=[blk] * has_own + [pl.BlockSpec((N_DEV, tr, C), lambda i: (0, i, 0)), blk, blk, blk],
        out_specs=[blk] * 4, out_shape=[_sds((R, C))] * 4, compiler_params=_params(("parallel",)),
    )(*((own,) if has_own else ()), parts, w, m, v)


_SMALL_ROWS = 8


def _pack_small(norm1, norm2, final, gnw, a_log, dt_bias, loss=None):
    loss = jnp.zeros((1, 128), F32) if loss is None else loss
    row3 = jnp.concatenate([gnw, a_log, dt_bias, jnp.zeros((1, 128 - 2 * GDN_HEADS), F32), loss,
                            jnp.zeros((1, D_MODEL - 3 * 128), F32)], axis=1)
    return jnp.concatenate([norm1, norm2, final, row3, jnp.zeros((_SMALL_ROWS - 4, D_MODEL), F32)], axis=0)


def _unpack_small(p):
    return (p[0:1], p[1:2], p[2], p[3:4, 0:128], p[3:4, 128:128 + GDN_HEADS], p[3:4, 128 + GDN_HEADS:128 + 2 * GDN_HEADS])


def _slabs_by_cols(g):
    r = g.shape[0]
    return g.reshape(r, N_DEV, -1).transpose(1, 0, 2)


def _cols_from_slabs(s):
    return s.transpose(1, 0, 2).reshape(s.shape[1], -1)


def kernel(x, norm1_w, w_in, conv_qkv_w, a_log, dt_bias, gdn_norm_w, w_out, norm2_w, w_up, ffn_conv_w, w_down, final_norm_w, loss_target, m_norm1_w, m_w_in, m_conv_qkv_w, m_a_log, m_dt_bias, m_gdn_norm_w, m_w_out, m_norm2_w, m_w_up, m_ffn_conv_w, m_w_down, m_final_norm_w, v_norm1_w, v_w_in, v_conv_qkv_w, v_a_log, v_dt_bias, v_gdn_norm_w, v_w_out, v_norm2_w, v_w_up, v_ffn_conv_w, v_w_down, v_final_norm_w):
    bf = lambda a: a.astype(BF16)
    me = _slot(_position())
    gw_in, g_conv_a = _all_gather([bf(w_in[0]), conv_qkv_w[0]], name="gather_w_in")
    w_a, w_z, w_b = _split_w_in(_cols_from_slabs(gw_in))
    late_src, _ = lax.optimization_barrier(([bf(w_out[0]), bf(w_up[0]), bf(w_down[0]), ffn_conv_w[0]], gw_in))
    l_send, l_recv, l_srcs, l_lands, l_token = _exchange_start(late_src, name="weights_start", broadcast=True)

    def late_weights(after):
        srcs, landed = _exchange_wait(l_send, l_recv, l_srcs, l_lands, after, name="weights_wait", broadcast=True)
        gw_out, gw_up, gw_down, g_conv_f = [lax.dynamic_update_index_in_dim(l, s, me, 0) for l, s in zip(landed, srcs)]
        return gw_out.reshape(D_MODEL, D_MODEL), gw_up, g_conv_f, gw_down.reshape(D_FF, D_MODEL)

    flights = {}

    def emit(group, **grads):
        if group == "in":
            slabs = dict(w_in=_slabs_by_cols(_merge_g_in(grads["w_a"], grads["w_z"], grads["w_b"])),
                         conv_a=_slabs_by_cols(grads["conv_a"]))
        elif group == "ffn":
            slabs = dict(w_down=grads["w_down"].reshape(N_DEV, -1, D_MODEL), w_up=grads["w_up"], conv_f=grads["conv_f"])
        else:
            slabs = {k: v.reshape(N_DEV, -1, D_MODEL) for k, v in grads.items()}
        names = list(slabs)
        own = {k: lax.dynamic_index_in_dim(slabs[k], me, 0, keepdims=False) for k in names}
        *flight, token = _exchange_start([slabs[k] for k in names], name="grads_start_" + group)
        flights[group] = (names, own, flight)
        return token

    loss, grad_x, g = _local_step(
        x[0], loss_target[0], _behind(norm1_w, l_token), w_a, w_z, w_b, _cols_from_slabs(g_conv_a), a_log, dt_bias,
        gdn_norm_w, norm2_w, final_norm_w[None], late_weights, emit)
    small_all = _gather_direct(
        _pack_small(g["norm1"], g["norm2"], g["final"], g["gnw"], g["small"][:, 0:GDN_HEADS],
                    g["small"][:, GDN_HEADS:2 * GDN_HEADS], loss), name="gather_small")
    got, mine = {}, {}
    for group in ("ffn", "out", "in"):
        names, own, (send_sems, recv_sems, srcs, lands) = flights[group]
        _, landed = _exchange_wait(send_sems, recv_sems, srcs, lands, small_all, name="grads_wait_" + group)
        got.update(zip(names, landed))
        mine.update(own)
    o_in = _adamw(got["w_in"], w_in[0], m_w_in[0], v_w_in[0], own=mine["w_in"], name="adamw_w_in", tr=128)
    o_out = _adamw(got["w_out"], w_out[0], m_w_out[0], v_w_out[0], own=mine["w_out"], name="adamw_w_out")
    o_up = _adamw(got["w_up"], w_up[0], m_w_up[0], v_w_up[0], own=mine["w_up"], name="adamw_w_up", tr=128)
    o_down = _adamw(got["w_down"], w_down[0], m_w_down[0], v_w_down[0], own=mine["w_down"], name="adamw_w_down", tr=176)
    o_ca = _adamw(got["conv_a"], conv_qkv_w[0], m_conv_qkv_w[0], v_conv_qkv_w[0], own=mine["conv_a"], name="adamw_conv_a")
    o_cf = _adamw(got["conv_f"], ffn_conv_w[0], m_ffn_conv_w[0], v_ffn_conv_w[0], own=mine["conv_f"], name="adamw_conv_f")
    o_small = _adamw(
        small_all, _pack_small(norm1_w, norm2_w, final_norm_w[None], gdn_norm_w, a_log, dt_bias),
        _pack_small(m_norm1_w, m_norm2_w, m_final_norm_w[None], m_gdn_norm_w, m_a_log, m_dt_bias),
        _pack_small(v_norm1_w, v_norm2_w, v_final_norm_w[None], v_gdn_norm_w, v_a_log, v_dt_bias), name="adamw_small")
    total_loss = o_small[0][3, 256]
    outs = [total_loss, grad_x[None]]
    for k in range(4):
        n1, n2, fin, gn, al, dt = _unpack_small(o_small[k])
        outs += [n1, o_in[k][None], o_ca[k][None], al, dt, gn, o_out[k][None], n2, o_up[k][None], o_cf[k][None], o_down[k][None], fin]
    return tuple(outs)
```

```python
import functools

import jax
import jax.numpy as jnp
from jax import lax
from jax.experimental import pallas as pl
from jax.experimental.pallas import tpu as pltpu

F32 = jnp.float32
BF16 = jnp.bfloat16

N_DEV = 8
D_MODEL = 1024
GDN_HEADS = 4
GDN_DIM = 128
GDN_WIDTH = GDN_HEADS * GDN_DIM
GDN_CONV = 4
CHUNK = 64
CHUNKS_PER_STEP = 2
DIL_HEADS = 8
DIL_DIM = 64
DIL_WIDTH = DIL_HEADS * DIL_DIM
DIL_PAIRS = DIL_HEADS // 2
DILATIONS = (1, 4, 16)
BAND = 128
D_FF = 2816
FFN_CONV = 3
EPS = 1e-6
A_COLS = 3 * GDN_WIDTH + 128
HALO = 8

ADAM_LR = 0.001
ADAM_B1 = 0.9
ADAM_B2 = 0.999
ADAM_EPS = 1e-08
ADAM_WD = 0.01
ADAM_STEP = 10

VMEM_LIMIT_BYTES = 56 * 1024 * 1024
NEG_BIG = -1e30


def _params(sem=None):
    return pltpu.CompilerParams(dimension_semantics=sem, vmem_limit_bytes=VMEM_LIMIT_BYTES)


def _sds(shape, dtype=F32):
    return jax.ShapeDtypeStruct(shape, dtype)


def _bdot(a, b):
    return jnp.dot(a.astype(BF16), b.astype(BF16), preferred_element_type=F32)


def _bdot_nt(a, b):
    return lax.dot_general(a.astype(BF16), b.astype(BF16), (((1,), (1,)), ((), ())), preferred_element_type=F32)


def _bdot_tn(a, b):
    return lax.dot_general(a.astype(BF16), b.astype(BF16), (((0,), (0,)), ((), ())), preferred_element_type=F32)


def _split(a):
    hi = a.astype(BF16)
    lo = (a - hi.astype(F32)).astype(BF16)
    return hi, lo


def _dot3(a, b, dims):
    ah, al = _split(a)
    bh, bl = _split(b)
    d = functools.partial(lax.dot_general, dimension_numbers=(dims, ((), ())), preferred_element_type=F32)
    return d(ah, bh) + (d(al, bh) + d(ah, bl))


def _exact_tri_dot(tri, g):
    g1 = g.astype(BF16)
    r1 = g - g1.astype(F32)
    g2 = r1.astype(BF16)
    g3 = (r1 - g2.astype(F32)).astype(BF16)
    t = tri.astype(BF16)
    d = functools.partial(jnp.dot, preferred_element_type=F32)
    return d(t, g1) + (d(t, g2) + d(t, g3))


def _sigmoid(x):
    return 1.0 / (1.0 + jnp.exp(-x))


def _dsilu(x, sg):
    return sg * (1.0 + x * (1.0 - sg))


def _mm(a, b, *, name, ta=False, tb=False, res=None, out_dtype=F32, tm=512, tn=512, tk=512):
    if ta:
        K, M = a.shape
    else:
        M, K = a.shape
    if tb:
        N, Kb = b.shape
    else:
        Kb, N = b.shape
    assert K == Kb, (a.shape, b.shape)
    tm, tn, tk = min(tm, M), min(tn, N), min(tk, K)
    assert M % tm == 0 and N % tn == 0 and K % tk == 0, (name, M, N, K, tm, tn, tk)
    nk = K // tk
    dims = (((0 if ta else 1,), (1 if tb else 0,)), ((), ()))
    has_res = res is not None

    def body(*refs):
        if has_res:
            a_ref, b_ref, r_ref, o_ref, acc_ref = refs
        else:
            a_ref, b_ref, o_ref, acc_ref = refs
        k = pl.program_id(2)
        part = lax.dot_general(a_ref[...].astype(BF16), b_ref[...].astype(BF16), dims, preferred_element_type=F32)

        @pl.when(k == 0)
        def _():
            acc_ref[...] = part

        @pl.when(k > 0)
        def _():
            acc_ref[...] += part

        @pl.when(k == nk - 1)
        def _():
            r = acc_ref[...]
            if has_res:
                r = r + r_ref[...]
            o_ref[...] = r.astype(out_dtype)

    a_spec = pl.BlockSpec((tk, tm), lambda i, j, k: (k, i)) if ta else pl.BlockSpec((tm, tk), lambda i, j, k: (i, k))
    b_spec = pl.BlockSpec((tn, tk), lambda i, j, k: (j, k)) if tb else pl.BlockSpec((tk, tn), lambda i, j, k: (k, j))
    o_spec = pl.BlockSpec((tm, tn), lambda i, j, k: (i, j))
    in_specs = [a_spec, b_spec] + ([o_spec] if has_res else [])
    args = (a, b) + ((res,) if has_res else ())
    return pl.pallas_call(
        body, name=name, grid=(M // tm, N // tn, nk), in_specs=in_specs, out_specs=o_spec,
        out_shape=_sds((M, N), out_dtype), scratch_shapes=[pltpu.VMEM((tm, tn), F32)],
        compiler_params=_params(("parallel", "parallel", "arbitrary")),
    )(*args)


def _rms_fwd(x, w, *, name, tm=256):
    S, D = x.shape

    def body(x_ref, w_ref, h_ref):
        xv = x_ref[...]
        r = lax.rsqrt(jnp.mean(xv * xv, axis=-1, keepdims=True) + EPS)
        h_ref[...] = (xv * r * w_ref[...]).astype(BF16)

    return pl.pallas_call(
        body, name=name, grid=(S // tm,),
        in_specs=[pl.BlockSpec((tm, D), lambda i: (i, 0)), pl.BlockSpec((1, D), lambda i: (0, 0))],
        out_specs=pl.BlockSpec((tm, D), lambda i: (i, 0)), out_shape=_sds((S, D), BF16),
        compiler_params=_params(("parallel",)),
    )(x, w)


def _rms_bwd(dh, x, w, res, *, name, tm=256):
    S, D = x.shape

    def body(dh_ref, x_ref, w_ref, res_ref, dx_ref, dw_ref):
        i = pl.program_id(0)
        xv = x_ref[...]
        g = dh_ref[...]
        r = lax.rsqrt(jnp.mean(xv * xv, axis=-1, keepdims=True) + EPS)
        xh = xv * r
        gw = g * w_ref[...]
        dx_ref[...] = res_ref[...] + r * (gw - xh * jnp.mean(gw * xh, axis=-1, keepdims=True))
        part = jnp.sum(g * xh, axis=0, keepdims=True)

        @pl.when(i == 0)
        def _():
            dw_ref[...] = part

        @pl.when(i > 0)
        def _():
            dw_ref[...] += part

    row = pl.BlockSpec((tm, D), lambda i: (i, 0))
    one = pl.BlockSpec((1, D), lambda i: (0, 0))
    return pl.pallas_call(
        body, name=name, grid=(S // tm,), in_specs=[row, row, one, row], out_specs=[row, one],
        out_shape=[_sds((S, D)), _sds((1, D))], compiler_params=_params(("arbitrary",)),
    )(dh, x, w, res)


def _loss_head(x2, w, tgt, *, name, tm=256):
    S, D = x2.shape

    def body(x_ref, w_ref, t_ref, dx_ref, dxb_ref, dw_ref, loss_ref):
        i = pl.program_id(0)
        xv = x_ref[...]
        wv = w_ref[...]
        r = lax.rsqrt(jnp.mean(xv * xv, axis=-1, keepdims=True) + EPS)
        xh = xv * r
        err = xh * wv - t_ref[...]
        lrow = jnp.sum(err * err, axis=-1, keepdims=True)
        lsum = jnp.sum(lrow, axis=0, keepdims=True) * (0.5 / D)
        g = err * (1.0 / D)
        gw = g * wv
        dx = r * (gw - xh * jnp.mean(gw * xh, axis=-1, keepdims=True))
        dx_ref[...] = dx
        dxb_ref[...] = dx.astype(BF16)
        part = jnp.sum(g * xh, axis=0, keepdims=True)
        lpart = jnp.broadcast_to(lsum, (1, 128))

        @pl.when(i == 0)
        def _():
            dw_ref[...] = part
            loss_ref[...] = lpart

        @pl.when(i > 0)
        def _():
            dw_ref[...] += part
            loss_ref[...] += lpart

    row = pl.BlockSpec((tm, D), lambda i: (i, 0))
    one = pl.BlockSpec((1, D), lambda i: (0, 0))
    return pl.pallas_call(
        body, name=name, grid=(S // tm,), in_specs=[row, one, row],
        out_specs=[row, row, one, pl.BlockSpec((1, 128), lambda i: (0, 0))],
        out_shape=[_sds((S, D)), _sds((S, D), BF16), _sds((1, D)), _sds((1, 128))], compiler_params=_params(("arbitrary",)),
    )(x2, w, tgt)


def _conv_rows(prev, cur, w, taps):
    n = cur.shape[0]
    xs = jnp.concatenate([prev, cur], axis=0)
    base = HALO - (taps - 1)
    out = xs[base:base + n] * w[0:1]
    for i in range(1, taps):
        out = out + xs[base + i:base + i + n] * w[i:i + 1]
    return out


def _conv_rows_bwd(cur_d, next_d, prev_x, cur_x, w, taps):
    n = cur_d.shape[0]
    ds = jnp.concatenate([cur_d, next_d], axis=0)
    dx = ds[taps - 1:taps - 1 + n] * w[0:1]
    for i in range(1, taps):
        dx = dx + ds[taps - 1 - i:taps - 1 - i + n] * w[i:i + 1]
    xs = jnp.concatenate([prev_x, cur_x], axis=0)
    base = HALO - (taps - 1)
    dws = [jnp.sum(cur_d * xs[base + i:base + i + n], axis=0, keepdims=True) for i in range(taps)]
    return dx, jnp.concatenate(dws, axis=0)


def _halo_specs(tm, width, col, nblk):
    per = tm // HALO
    prev = pl.BlockSpec((HALO, width), lambda i, *_: (jnp.maximum(i * per - 1, 0), col))
    nxt = pl.BlockSpec((HALO, width), lambda i, *_: (jnp.minimum((i + 1) * per, nblk * per - 1), col))
    return prev, nxt


def _softplus(x):
    return jnp.maximum(x, 0.0) + jnp.log1p(jnp.exp(-jnp.abs(x)))


def _chunk_tri(tm, upper=False):
    r = lax.broadcasted_iota(jnp.int32, (tm, tm), 0)
    c = lax.broadcasted_iota(jnp.int32, (tm, tm), 1)
    same = lax.div(r, CHUNK) == lax.div(c, CHUNK)
    order = (c >= r) if upper else (c <= r)
    return jnp.where(same & order, 1.0, 0.0)


def _gdn_prep_fwd(proj_a, conv_w, a_log, dt_bias, *, name, tm=256):
    S = proj_a.shape[0]
    nblk = S // tm
    W3 = 3 * GDN_WIDTH

    def body(cur_ref, prev_ref, ba_ref, cw_ref, al_ref, dt_ref, qn_ref, kn_ref, v_ref, gcb_ref, bb_ref):
        i = pl.program_id(0)
        prev = jnp.where(i > 0, prev_ref[...], 0.0)
        c = _conv_rows(prev, cur_ref[...], cw_ref[...], GDN_CONV)
        a = c * _sigmoid(c)
        ba = ba_ref[...]
        lane = lax.broadcasted_iota(jnp.int32, (tm, 128), 1)
        g4 = jnp.zeros((tm, 128), F32)
        for h in range(GDN_HEADS):
            sl = slice(GDN_DIM * h, GDN_DIM * (h + 1))
            qh = a[:, GDN_DIM * h:GDN_DIM * (h + 1)]
            kh = a[:, GDN_WIDTH + GDN_DIM * h:GDN_WIDTH + GDN_DIM * (h + 1)]
            qn_ref[:, sl] = qh * (lax.rsqrt(jnp.sum(qh * qh, axis=-1, keepdims=True) + EPS) * (GDN_DIM ** -0.5))
            kn_ref[:, sl] = kh * lax.rsqrt(jnp.sum(kh * kh, axis=-1, keepdims=True) + EPS)
            beta = _sigmoid(ba[:, h:h + 1])
            bb_ref[:, sl] = jnp.broadcast_to(beta, (tm, GDN_DIM))
            g = -jnp.exp(al_ref[0:1, h:h + 1]) * _softplus(ba[:, GDN_HEADS + h:GDN_HEADS + h + 1] + dt_ref[0:1, h:h + 1])
            g4 = jnp.where(lane == h, g, g4)
        v_ref[...] = a[:, 2 * GDN_WIDTH:]
        gc = _exact_tri_dot(_chunk_tri(tm), g4)
        for h in range(GDN_HEADS):
            gcb_ref[:, GDN_DIM * h:GDN_DIM * (h + 1)] = jnp.broadcast_to(gc[:, h:h + 1], (tm, GDN_DIM))

    prev_spec, _ = _halo_specs(tm, W3, 0, nblk)
    row = pl.BlockSpec((tm, GDN_WIDTH), lambda i: (i, 0))
    small = lambda a: pl.BlockSpec(a.shape, lambda i: (0, 0))
    return pl.pallas_call(
        body, name=name, grid=(nblk,),
        in_specs=[pl.BlockSpec((tm, W3), lambda i: (i, 0)), prev_spec,
                  pl.BlockSpec((tm, 128), lambda i: (i, W3 // 128)), small(conv_w), small(a_log), small(dt_bias)],
        out_specs=[row] * 5, out_shape=[_sds((S, GDN_WIDTH))] * 5, compiler_params=_params(("parallel",)),
    )(proj_a, proj_a, proj_a, conv_w, a_log, dt_bias)


def _chunk_masks():
    r = lax.broadcasted_iota(jnp.int32, (CHUNK, CHUNK), 0)
    c = lax.broadcasted_iota(jnp.int32, (CHUNK, CHUNK), 1)
    return r >= c, r > c, r == c


def _chunk_decay(gcb_h, bb_h, incl):
    G = gcb_h[:, 0:CHUNK]
    diff = G - G.T
    dec = jnp.where(incl, jnp.exp(jnp.where(incl, diff, 0.0)), 0.0)
    return dec, bb_h[:, 0:CHUNK].T


def _gdn_chunk_fwd(qn, kn, v, gcb, bb, *, name):
    S = qn.shape[0]
    nc = S // CHUNK

    def body(qn_ref, kn_ref, v_ref, gcb_ref, bb_ref, uv_ref, wk_ref, at_ref, t_ref):
        incl, strict, diag = _chunk_masks()
        for c in range(CHUNKS_PER_STEP):
            rows = slice(CHUNK * c, CHUNK * (c + 1))
            ats, ts = [], []
            for h in range(GDN_HEADS):
                sl = slice(GDN_DIM * h, GDN_DIM * (h + 1))
                q, k, vv, gh = qn_ref[rows, sl], kn_ref[rows, sl], v_ref[rows, sl], gcb_ref[rows, sl]
                dec, bt = _chunk_decay(gh, bb_ref[rows, sl], incl)
                lmat = jnp.where(strict, dec * _bdot_nt(k, k) * bt, 0.0)
                p = -lmat
                t = jnp.where(diag, 1.0, 0.0) + p
                for _ in range(5):
                    p = _bdot(p, p)
                    t = t + _bdot(t, p)
                rhs = jnp.concatenate([vv, jnp.exp(gh) * k], axis=1)
                sol = _dot3(t, rhs, ((1,), (0,)))
                uv_ref[rows, sl] = sol[:, :GDN_DIM]
                wk_ref[rows, sl] = sol[:, GDN_DIM:]
                ats.append(dec * _bdot_nt(q, k) * bt)
                ts.append(t)
            at_ref[rows, :] = jnp.concatenate(ats, axis=1)
            t_ref[rows, :] = jnp.concatenate(ts, axis=1)

    step = CHUNKS_PER_STEP * CHUNK
    row = pl.BlockSpec((step, GDN_WIDTH), lambda n: (n, 0))
    sq = pl.BlockSpec((step, GDN_HEADS * CHUNK), lambda n: (n, 0))
    return pl.pallas_call(
        body, name=name, grid=(S // step,), in_specs=[row] * 5, out_specs=[row, row, sq, sq],
        out_shape=[_sds((S, GDN_WIDTH)), _sds((S, GDN_WIDTH)), _sds((S, GDN_HEADS * CHUNK)), _sds((S, GDN_HEADS * CHUNK))],
        compiler_params=_params(("parallel",)),
    )(qn, kn, v, gcb, bb)


def _gdn_scan_fwd(uv, wk, at, qn, kn, gcb, bb, proj_z, gnw, *, name):
    S = uv.shape[0]
    nc = S // CHUNK

    def body(uv_ref, wk_ref, at_ref, qn_ref, kn_ref, gcb_ref, bb_ref, z_ref, gnw_ref, o_ref, u_ref, sp_ref, oa_ref, st_ref):
        n = pl.program_id(0)

        @pl.when(n == 0)
        def _():
            st_ref[...] = jnp.zeros_like(st_ref)

        oas = []
        for h in range(GDN_HEADS):
            sl = slice(GDN_DIM * h, GDN_DIM * (h + 1))
            st = st_ref[h]
            sp_ref[sl, :] = st
            gh = gcb_ref[:, sl]
            glast = gcb_ref[CHUNK - 1:CHUNK, sl]
            u = uv_ref[:, sl] - _bdot(wk_ref[:, sl], st)
            o = _bdot(qn_ref[:, sl] * jnp.exp(gh), st) + _bdot(at_ref[:, CHUNK * h:CHUNK * (h + 1)], u)
            ke = kn_ref[:, sl] * jnp.exp(glast - gh) * bb_ref[:, sl]
            st_ref[h] = jnp.exp(glast) * st + _bdot_tn(ke, u)
            u_ref[:, sl] = u
            o_ref[:, sl] = o
            z = z_ref[:, sl]
            r = lax.rsqrt(jnp.mean(o * o, axis=-1, keepdims=True) + EPS)
            oas.append(o * r * gnw_ref[...] * (z * _sigmoid(z)))
        oa_ref[...] = jnp.concatenate(oas, axis=1).astype(BF16)

    row = pl.BlockSpec((CHUNK, GDN_WIDTH), lambda n: (n, 0))
    sq = pl.BlockSpec((CHUNK, GDN_HEADS * CHUNK), lambda n: (n, 0))
    return pl.pallas_call(
        body, name=name, grid=(nc,),
        in_specs=[row, row, sq, row, row, row, row, row, pl.BlockSpec((1, GDN_DIM), lambda n: (0, 0))],
        out_specs=[row, row, pl.BlockSpec((GDN_WIDTH, GDN_DIM), lambda n: (n, 0)), row],
        out_shape=[_sds((S, GDN_WIDTH)), _sds((S, GDN_WIDTH)), _sds((nc * GDN_WIDTH, GDN_DIM)), _sds((S, 2 * GDN_WIDTH), BF16)],
        scratch_shapes=[pltpu.VMEM((GDN_HEADS, GDN_DIM, GDN_DIM), F32)],
        compiler_params=_params(("arbitrary",)),
    )(uv, wk, at, qn, kn, gcb, bb, proj_z, gnw)


def _gdn_scan_bwd(d_oab, o, proj_z, gnw, sp, u, wk, at, qn, kn, gcb, bb, *, name):
    S = o.shape[0]
    nc = S // CHUNK

    def body(do_ref, o_ref, z_ref, gnw_ref, sp_ref, u_ref, wk_ref, at_ref, qn_ref, kn_ref, gcb_ref, bb_ref,
             dz_ref, dgn_ref, du_ref, dwk_ref, dat_ref, dqd_ref, dke_ref, dgl_ref, ds_ref):
        n = pl.program_id(0)

        @pl.when(n == 0)
        def _():
            ds_ref[...] = jnp.zeros_like(ds_ref)
            dgn_ref[...] = jnp.zeros_like(dgn_ref)

        dats, dgn = [], jnp.zeros((1, GDN_DIM), F32)
        for h in range(GDN_HEADS):
            sl = slice(GDN_DIM * h, GDN_DIM * (h + 1))
            oo = o_ref[:, sl]
            z = z_ref[:, sl]
            gw = gnw_ref[...]
            sg = _sigmoid(z)
            r = lax.rsqrt(jnp.mean(oo * oo, axis=-1, keepdims=True) + EPS)
            xh = oo * r
            d_oa = do_ref[:, sl]
            dy = d_oa * (z * sg)
            dz_ref[:, sl] = (d_oa * (xh * gw) * _dsilu(z, sg)).astype(BF16)
            dgn = dgn + jnp.sum(dy * xh, axis=0, keepdims=True)
            dxh = dy * gw
            do = r * (dxh - xh * jnp.mean(dxh * xh, axis=-1, keepdims=True))

            st = sp_ref[sl, :]
            dst = ds_ref[h]
            gh = gcb_ref[:, sl]
            glast = gcb_ref[CHUNK - 1:CHUNK, sl]
            uu = u_ref[:, sl]
            ath = at_ref[:, CHUNK * h:CHUNK * (h + 1)]
            qd = qn_ref[:, sl] * jnp.exp(gh)
            ke = kn_ref[:, sl] * jnp.exp(glast - gh) * bb_ref[:, sl]
            ge = jnp.exp(glast)
            dqd_ref[:, sl] = _bdot_nt(do, st)
            dats.append(_bdot_nt(do, uu))
            du = _bdot_tn(ath, do) + _bdot(ke, dst)
            dke_ref[:, sl] = _bdot_nt(uu, dst)
            dge = jnp.sum(jnp.sum(dst * st, axis=1, keepdims=True), axis=0, keepdims=True)
            dgl_ref[0, :, sl] = jnp.broadcast_to(dge * ge, (8, GDN_DIM))
            ds_ref[h] = _bdot_tn(qd, do) + ge * dst - _bdot_tn(wk_ref[:, sl], du)
            du_ref[:, sl] = du
            dwk_ref[:, sl] = -_bdot_nt(du, st)
        dat_ref[...] = jnp.concatenate(dats, axis=1)
        dgn_ref[...] += dgn

    rev = lambda n: (nc - 1 - n, 0)
    row = pl.BlockSpec((CHUNK, GDN_WIDTH), rev)
    sq = pl.BlockSpec((CHUNK, GDN_HEADS * CHUNK), rev)
    one = pl.BlockSpec((1, GDN_DIM), lambda n: (0, 0))
    return pl.pallas_call(
        body, name=name, grid=(nc,),
        in_specs=[row, row, row, one, pl.BlockSpec((GDN_WIDTH, GDN_DIM), rev), row, row, sq, row, row, row, row],
        out_specs=[row, one, row, row, sq, row, row, pl.BlockSpec((1, 8, GDN_WIDTH), lambda n: (nc - 1 - n, 0, 0))],
        out_shape=[_sds((S, GDN_WIDTH), BF16), _sds((1, GDN_DIM)), _sds((S, GDN_WIDTH)), _sds((S, GDN_WIDTH)),
                   _sds((S, GDN_HEADS * CHUNK)), _sds((S, GDN_WIDTH)), _sds((S, GDN_WIDTH)), _sds((nc, 8, GDN_WIDTH))],
        scratch_shapes=[pltpu.VMEM((GDN_HEADS, GDN_DIM, GDN_DIM), F32)],
        compiler_params=_params(("arbitrary",)),
    )(d_oab, o, proj_z, gnw, sp, u, wk, at, qn, kn, gcb, bb)


def _gdn_chunk_bwd(qn, kn, v, gcb, bb, tmat, uv, wk, du, dwk, dat, dqd, dke, dgl, *, name):
    S = qn.shape[0]
    nc = S // CHUNK

    def body(qn_ref, kn_ref, v_ref, gcb_ref, bb_ref, t_ref, uv_ref, wk_ref, du_ref, dwk_ref, dat_ref, dqd_ref, dke_ref,
             dgl_ref, dq_ref, dk_ref, dv_ref, dg_ref, dbeta_ref):
        incl, strict, _ = _chunk_masks()
        lane = lax.broadcasted_iota(jnp.int32, (CHUNK, 128), 1)
        rowi = lax.broadcasted_iota(jnp.int32, (CHUNK, 1), 0)
        for c in range(CHUNKS_PER_STEP):
            rows = slice(CHUNK * c, CHUNK * (c + 1))
            last = slice(CHUNK * (c + 1) - 1, CHUNK * (c + 1))
            dgc4 = jnp.zeros((CHUNK, 128), F32)
            db4 = jnp.zeros((CHUNK, 128), F32)
            for h in range(GDN_HEADS):
                sl = slice(GDN_DIM * h, GDN_DIM * (h + 1))
                sq = slice(CHUNK * h, CHUNK * (h + 1))
                q, k, gh, bh = qn_ref[rows, sl], kn_ref[rows, sl], gcb_ref[rows, sl], bb_ref[rows, sl]
                dec, bt = _chunk_decay(gh, bh, incl)
                kk = _bdot_nt(k, k)
                qk = _bdot_nt(q, k)
                t = t_ref[rows, sq]
                d_sol = jnp.concatenate([du_ref[rows, sl], dwk_ref[rows, sl]], axis=1)
                d_rhs = _dot3(t, d_sol, ((0,), (0,)))
                sol = jnp.concatenate([uv_ref[rows, sl], wk_ref[rows, sl]], axis=1)
                d_l = jnp.where(strict, -_dot3(d_rhs, sol, ((1,), (1,))), 0.0)
                d_a = jnp.where(incl, dat_ref[rows, sq], 0.0)
                gam = jnp.exp(gh)
                glast = gcb_ref[last, sl]
                e = jnp.exp(glast - gh)
                d_gk = d_rhs[:, GDN_DIM:]
                dqd = dqd_ref[rows, sl]
                dke = dke_ref[rows, sl]
                ml = d_l * dec * bt
                ma = d_a * dec * bt
                dq_ref[rows, sl] = _bdot(ma, k) + dqd * gam
                dk_ref[rows, sl] = (_bdot(ml, k) + _bdot_tn(ml, k) + _bdot_tn(ma, q)) + d_gk * gam + dke * (e * bh)
                dv_ref[rows, sl] = d_rhs[:, :GDN_DIM]
                wb = d_l * dec * kk + d_a * dec * qk
                ew = wb * bt
                s_ke = jnp.sum(dke * k * (e * bh), axis=-1, keepdims=True)
                dbeta = jnp.sum(wb.T, axis=-1, keepdims=True) + jnp.sum(dke * k * e, axis=-1, keepdims=True)
                dgc = (jnp.sum(ew, axis=-1, keepdims=True) - jnp.sum(ew.T, axis=-1, keepdims=True)
                       + jnp.sum(dqd * q * gam, axis=-1, keepdims=True) + jnp.sum(d_gk * k * gam, axis=-1, keepdims=True) - s_ke)
                tail = jnp.sum(s_ke, axis=0, keepdims=True) + dgl_ref[c, 0:1, GDN_DIM * h:GDN_DIM * h + 1]
                dgc = dgc + jnp.where(rowi == CHUNK - 1, tail, 0.0)
                dgc4 = jnp.where(lane == h, dgc, dgc4)
                db4 = jnp.where(lane == h, dbeta, db4)
            dg_ref[rows, :] = _exact_tri_dot(_chunk_tri(CHUNK, upper=True), dgc4)
            dbeta_ref[rows, :] = db4

    step = CHUNKS_PER_STEP * CHUNK
    row = pl.BlockSpec((step, GDN_WIDTH), lambda n: (n, 0))
    sq = pl.BlockSpec((step, GDN_HEADS * CHUNK), lambda n: (n, 0))
    col = pl.BlockSpec((step, 128), lambda n: (n, 0))
    return pl.pallas_call(
        body, name=name, grid=(S // step,),
        in_specs=[row] * 5 + [sq, row, row, row, row, sq, row, row,
                              pl.BlockSpec((CHUNKS_PER_STEP, 8, GDN_WIDTH), lambda n: (n, 0, 0))],
        out_specs=[row, row, row, col, col],
        out_shape=[_sds((S, GDN_WIDTH))] * 3 + [_sds((S, 128))] * 2, compiler_params=_params(("parallel",)),
    )(qn, kn, v, gcb, bb, tmat, uv, wk, du, dwk, dat, dqd, dke, dgl)


def _gdn_prep_bwd(dqn, dkn, dv, dg, dbeta, proj_a, conv_w, a_log, dt_bias, *, name, tm=256):
    S = proj_a.shape[0]
    nblk = S // tm
    W3 = 3 * GDN_WIDTH

    def body(dqn_ref, dkn_ref, dv_ref, dg_ref, dbeta_ref, cur_ref, prev_ref, ba_ref, cw_ref, al_ref, dt_ref,
             dc_ref, dba_ref, sm_ref):
        i = pl.program_id(0)
        prev = jnp.where(i > 0, prev_ref[...], 0.0)
        c = _conv_rows(prev, cur_ref[...], cw_ref[...], GDN_CONV)
        sg = _sigmoid(c)
        a = c * sg
        dsl = _dsilu(c, sg)
        ba = ba_ref[...]
        lane = lax.broadcasted_iota(jnp.int32, (tm, 128), 1)
        lane1 = lax.broadcasted_iota(jnp.int32, (1, 128), 1)
        dba = jnp.zeros((tm, 128), F32)
        sm = jnp.zeros((1, 128), F32)
        for h in range(GDN_HEADS):
            sl = slice(GDN_DIM * h, GDN_DIM * (h + 1))
            ks = slice(GDN_WIDTH + GDN_DIM * h, GDN_WIDTH + GDN_DIM * (h + 1))
            qh, kh = a[:, sl], a[:, ks]
            rq = lax.rsqrt(jnp.sum(qh * qh, axis=-1, keepdims=True) + EPS)
            rk = lax.rsqrt(jnp.sum(kh * kh, axis=-1, keepdims=True) + EPS)
            qhat, khat = qh * rq, kh * rk
            dyq = dqn_ref[:, sl] * (GDN_DIM ** -0.5)
            dyk = dkn_ref[:, sl]
            dq = rq * (dyq - qhat * jnp.sum(dyq * qhat, axis=-1, keepdims=True))
            dk = rk * (dyk - khat * jnp.sum(dyk * khat, axis=-1, keepdims=True))
            dc_ref[:, sl] = dq * dsl[:, sl]
            dc_ref[:, ks] = dk * dsl[:, ks]
            beta = _sigmoid(ba[:, h:h + 1])
            db = dbeta_ref[:, h:h + 1] * beta * (1.0 - beta)
            aneg = -jnp.exp(al_ref[0:1, h:h + 1])
            xa = ba[:, GDN_HEADS + h:GDN_HEADS + h + 1] + dt_ref[0:1, h:h + 1]
            dgh = dg_ref[:, h:h + 1]
            dxa = dgh * aneg * _sigmoid(xa)
            dba = jnp.where(lane == h, db, dba)
            dba = jnp.where(lane == GDN_HEADS + h, dxa, dba)
            d_alog = jnp.sum(dgh * _softplus(xa), axis=0, keepdims=True) * aneg
            sm = jnp.where(lane1 == h, d_alog, sm)
            sm = jnp.where(lane1 == GDN_HEADS + h, jnp.sum(dxa, axis=0, keepdims=True), sm)
        vs = slice(2 * GDN_WIDTH, W3)
        dc_ref[:, vs] = dv_ref[...] * dsl[:, vs]
        dba_ref[...] = dba

        @pl.when(i == 0)
        def _():
            sm_ref[...] = sm

        @pl.when(i > 0)
        def _():
            sm_ref[...] += sm

    prev_spec, _ = _halo_specs(tm, W3, 0, nblk)
    row = pl.BlockSpec((tm, GDN_WIDTH), lambda i: (i, 0))
    col = pl.BlockSpec((tm, 128), lambda i: (i, 0))
    small = lambda a: pl.BlockSpec(a.shape, lambda i: (0, 0))
    return pl.pallas_call(
        body, name=name, grid=(nblk,),
        in_specs=[row, row, row, col, col, pl.BlockSpec((tm, W3), lambda i: (i, 0)), prev_spec,
                  pl.BlockSpec((tm, 128), lambda i: (i, W3 // 128)), small(conv_w), small(a_log), small(dt_bias)],
        out_specs=[pl.BlockSpec((tm, W3), lambda i: (i, 0)), col, pl.BlockSpec((1, 128), lambda i: (0, 0))],
        out_shape=[_sds((S, W3)), _sds((S, 128)), _sds((1, 128))], compiler_params=_params(("arbitrary",)),
    )(dqn, dkn, dv, dg, dbeta, proj_a, proj_a, proj_a, conv_w, a_log, dt_bias)


def _gdn_conv_bwd(dc, dba, proj_a, conv_w, *, name, tm=256):
    S = proj_a.shape[0]
    nblk = S // tm
    W3 = 3 * GDN_WIDTH

    def body(dc_ref, dnext_ref, dba_ref, cur_ref, prev_ref, cw_ref, da_ref, dcw_ref):
        i = pl.program_id(0)
        prev = jnp.where(i > 0, prev_ref[...], 0.0)
        nxt = jnp.where(i < nblk - 1, dnext_ref[...], 0.0)
        dx, dw = _conv_rows_bwd(dc_ref[...], nxt, prev, cur_ref[...], cw_ref[...], GDN_CONV)
        da_ref[:, 0:W3] = dx.astype(BF16)
        da_ref[:, W3:] = dba_ref[...].astype(BF16)

        @pl.when(i == 0)
        def _():
            dcw_ref[...] = dw

        @pl.when(i > 0)
        def _():
            dcw_ref[...] += dw

    prev_spec, next_spec = _halo_specs(tm, W3, 0, nblk)
    wide = pl.BlockSpec((tm, W3), lambda i: (i, 0))
    return pl.pallas_call(
        body, name=name, grid=(nblk,),
        in_specs=[wide, next_spec, pl.BlockSpec((tm, 128), lambda i: (i, 0)), wide, prev_spec,
                  pl.BlockSpec(conv_w.shape, lambda i: (0, 0))],
        out_specs=[pl.BlockSpec((tm, A_COLS), lambda i: (i, 0)), pl.BlockSpec(conv_w.shape, lambda i: (0, 0))],
        out_shape=[_sds((S, A_COLS), BF16), _sds(conv_w.shape)], compiler_params=_params(("arbitrary",)),
    )(dc, dc, dba, proj_a, proj_a, conv_w)


def _band_mask(nk):
    i = lax.broadcasted_iota(jnp.int32, (2 * BAND, nk), 0) & (BAND - 1)
    j = lax.broadcasted_iota(jnp.int32, (2 * BAND, nk), 1)
    if nk == BAND:
        return j <= i
    return (j >= i) & (j <= i + BAND)


def _stack_heads(x, lo):
    return jnp.concatenate([jnp.where(lo, x, 0.0), jnp.where(lo, 0.0, x)], axis=0)


def _stack_cols(x):
    return jnp.concatenate([x[:, 0:1], x[:, DIL_DIM:DIL_DIM + 1]], axis=0)


def _unstack(x, lo):
    return jnp.where(lo, x[0:BAND], x[BAND:2 * BAND])


def _rows(start, size, stride):
    return pl.ds(start, size) if stride == 1 else pl.ds(start, size, stride=stride)


ATTN_LANES = 4


def _attn_blocks(S, visit_many, lanes=ATTN_LANES):
    for d in DILATIONS:
        nb = S // (d * BAND)
        if d == 1:
            half = nb // 2
            visit_many(d, [(0, 0, True), (0, half, False)])

            def pair(n, c):
                visit_many(1, [(0, n, False), (0, n + half, False)])
                return c
            lax.fori_loop(1, half, pair, 0)
        elif nb > 1:
            for r0 in range(0, d, lanes):
                visit_many(d, [(r0 + t, 0, True) for t in range(lanes)])

                def column(n, c, d=d, r0=r0):
                    visit_many(d, [(r0 + t, n, False) for t in range(lanes)])
                    return c
                lax.fori_loop(1, nb, column, 0)
        else:
            def group(g, c, d=d):
                visit_many(d, [(g * lanes + t, 0, True) for t in range(lanes)])
                return c
            lax.fori_loop(0, d // lanes, group, 0)


def _attn_fwd(proj_b, oab, *, name):
    S = proj_b.shape[0]
    scale = DIL_DIM ** -0.5

    def body(q_ref, k_ref, v_ref, oab_in_ref, ob_ref, lse_ref, m_ref, l_ref, acc_ref):
        del oab_in_ref
        lane = lax.broadcasted_iota(jnp.int32, (BAND, 128), 1)
        lo = lane < DIL_DIM
        m_ref[...] = jnp.full_like(m_ref, NEG_BIG)
        l_ref[...] = jnp.zeros_like(l_ref)
        acc_ref[...] = jnp.zeros_like(acc_ref)

        def load(d, r, n, first):
            nk = BAND if first else 2 * BAND
            qrows = _rows(r + n * (BAND * d), BAND, d)
            krows = _rows(r if first else r + (n - 1) * (BAND * d), nk, d)
            return dict(nk=nk, qrows=qrows, q=q_ref[qrows, :] * scale, k=k_ref[krows, :].astype(BF16),
                        v=v_ref[krows, :].astype(BF16), m=m_ref[qrows, :], l=l_ref[qrows, :], acc=acc_ref[qrows, :])

        def compute(b):
            q, k, v = b["q"], b["k"], b["v"]
            s = jnp.where(_band_mask(b["nk"]), _bdot_nt(_stack_heads(q, lo), k), NEG_BIG)
            m_old = _stack_cols(b["m"])
            m_new = jnp.maximum(m_old, jnp.max(s, axis=-1, keepdims=True))
            p = jnp.exp(s - m_new)
            alpha = _unstack(jnp.exp(m_old - m_new), lo)
            l_new = alpha * b["l"] + _unstack(jnp.sum(p, axis=-1, keepdims=True), lo)
            return _unstack(m_new, lo), l_new, alpha * b["acc"] + _unstack(_bdot(p, v), lo)

        def visit_many(d, blocks):
            loaded = [load(d, *blk) for blk in blocks]
            done = [compute(b) for b in loaded]
            for b, (m_new, l_new, acc_new) in zip(loaded, done):
                m_ref[b["qrows"], :] = m_new
                l_ref[b["qrows"], :] = l_new
                acc_ref[b["qrows"], :] = acc_new

        _attn_blocks(S, visit_many)
        ob_ref[...] = (acc_ref[...] / l_ref[...]).astype(BF16)
        lse_ref[...] = m_ref[...] + jnp.log(l_ref[...])

    part = lambda t: pl.BlockSpec((S, 128), lambda p: (0, 3 * p + t))
    return pl.pallas_call(
        body, name=name, grid=(DIL_PAIRS,),
        in_specs=[part(0), part(1), part(2), pl.BlockSpec(memory_space=pl.ANY)],
        out_specs=[pl.BlockSpec((S, 128), lambda p: (0, GDN_WIDTH // 128 + p)), pl.BlockSpec((S, 128), lambda p: (0, p))],
        out_shape=[_sds(oab.shape, BF16), _sds((S, DIL_WIDTH))],
        scratch_shapes=[pltpu.VMEM((S, 128), F32)] * 3, input_output_aliases={3: 0},
        compiler_params=_params(("parallel",)),
    )(proj_b, proj_b, proj_b, oab)


def _attn_bwd(proj_b, oab, d_oab, lse, *, name):
    S = proj_b.shape[0]
    scale = DIL_DIM ** -0.5

    def body(q_ref, k_ref, v_ref, o_ref, do_ref, lse_ref, dqkv_ref, dq_ref, dk_ref, dv_ref, delta_ref):
        lane = lax.broadcasted_iota(jnp.int32, (BAND, 128), 1)
        lo = lane < DIL_DIM
        dq_ref[...] = jnp.zeros_like(dq_ref)
        dk_ref[...] = jnp.zeros_like(dk_ref)
        dv_ref[...] = jnp.zeros_like(dv_ref)
        prod = do_ref[...] * o_ref[...].astype(F32)
        lo_all = lax.broadcasted_iota(jnp.int32, (S, 128), 1) < DIL_DIM
        delta_ref[...] = jnp.where(lo_all, jnp.sum(jnp.where(lo_all, prod, 0.0), axis=-1, keepdims=True),
                                   jnp.sum(jnp.where(lo_all, 0.0, prod), axis=-1, keepdims=True))

        def load(d, r, n, first):
            nk = BAND if first else 2 * BAND
            qrows = _rows(r + n * (BAND * d), BAND, d)
            krows = _rows(r if first else r + (n - 1) * (BAND * d), nk, d)
            return dict(nk=nk, qrows=qrows, krows=krows, q=q_ref[qrows, :] * scale, k=k_ref[krows, :], v=v_ref[krows, :],
                        do=do_ref[qrows, :], delta=delta_ref[qrows, :], lse=lse_ref[qrows, :],
                        dq=dq_ref[qrows, :], dk=dk_ref[krows, :], dv=dv_ref[krows, :])

        def compute(b):
            q, k, v, do = b["q"], b["k"], b["v"], b["do"]
            qs, dos = _stack_heads(q, lo), _stack_heads(do, lo)
            p = jnp.where(_band_mask(b["nk"]), jnp.exp(_bdot_nt(qs, k) - _stack_cols(b["lse"])), 0.0)
            ds = p * (_bdot_nt(dos, v) - _stack_cols(b["delta"]))
            dq = b["dq"] + _unstack(_bdot(ds, k), lo) * scale
            return dq, b["dk"] + _bdot_tn(ds, qs), b["dv"] + _bdot_tn(p, dos)

        def visit_many(d, blocks):
            loaded = [load(d, *blk) for blk in blocks]
            done = [compute(b) for b in loaded]
            for b, (dq, dk, dv) in zip(loaded, done):
                dq_ref[b["qrows"], :] = dq
                dk_ref[b["krows"], :] = dk
                dv_ref[b["krows"], :] = dv

        _attn_blocks(S, visit_many, lanes=2)
        dqkv_ref[:, 0:128] = dq_ref[...].astype(BF16)
        dqkv_ref[:, 128:256] = dk_ref[...].astype(BF16)
        dqkv_ref[:, 256:384] = dv_ref[...].astype(BF16)

    half = lambda p: (0, GDN_WIDTH // 128 + p)
    part = lambda t: pl.BlockSpec((S, 128), lambda p: (0, 3 * p + t))
    return pl.pallas_call(
        body, name=name, grid=(DIL_PAIRS,),
        in_specs=[part(0), part(1), part(2), pl.BlockSpec((S, 128), half), pl.BlockSpec((S, 128), half),
                  pl.BlockSpec((S, 128), lambda p: (0, p))],
        out_specs=pl.BlockSpec((S, 384), lambda p: (0, p)), out_shape=_sds((S, 3 * DIL_WIDTH), BF16),
        scratch_shapes=[pltpu.VMEM((S, 128), F32)] * 4, compiler_params=_params(("parallel",)),
    )(proj_b, proj_b, proj_b, oab, d_oab, lse)


FF_SLAB = 2 * D_FF // N_DEV
FF_PAIRS = N_DEV // 2
ROWS16 = 16


def _taps(w, x, base, n):
    out = x[base:base + n] * w[0:1]
    for t in range(1, FFN_CONV):
        out = out + x[base + t:base + t + n] * w[t:t + 1]
    return out


def _ffn_fwd(h2, x1, w_up, conv_w, w_down, *, name, tm=512):
    S, D = h2.shape
    ni = S // tm
    per = tm // ROWS16

    def body(h_ref, hp_ref, x1_ref, wg_ref, wu_ref, cg_ref, cu_ref, wd_ref, x2_ref, ug_ref, uu_ref):
        i, j = pl.program_id(0), pl.program_id(1)
        hv = jnp.concatenate([hp_ref[...], h_ref[...]], axis=0)
        row = lax.broadcasted_iota(jnp.int32, (tm + ROWS16, 1), 0)
        keep = (i > 0) | (row >= ROWS16)

        def branch(w_ref, c_ref, u_ref):
            u = jnp.dot(hv, w_ref[...], preferred_element_type=F32).astype(BF16)
            u_ref[...] = u[ROWS16:]
            return _taps(c_ref[...], jnp.where(keep, u.astype(F32), 0.0), ROWS16 - (FFN_CONV - 1), tm)

        gate = branch(wg_ref, cg_ref, ug_ref)
        up = branch(wu_ref, cu_ref, uu_ref)
        act = (gate * _sigmoid(gate) * up).astype(BF16)
        part = jnp.dot(act, wd_ref[...], preferred_element_type=F32)

        @pl.when(j == 0)
        def _():
            x2_ref[...] = x1_ref[...] + part

        @pl.when(j > 0)
        def _():
            x2_ref[...] += part

    rows = pl.BlockSpec((tm, D), lambda i, j: (i, 0))
    slab = lambda off: pl.BlockSpec((None, D, FF_SLAB), lambda i, j: (j + off, 0, 0))
    cslab = lambda off: pl.BlockSpec((None, FFN_CONV, FF_SLAB), lambda i, j: (j + off, 0, 0))
    uspec = pl.BlockSpec((None, tm, FF_SLAB), lambda i, j: (j, i, 0))
    return pl.pallas_call(
        body, name=name, grid=(ni, FF_PAIRS),
        in_specs=[rows, pl.BlockSpec((ROWS16, D), lambda i, j: (jnp.maximum(i * per - 1, 0), 0)), rows,
                  slab(0), slab(FF_PAIRS), cslab(0), cslab(FF_PAIRS), pl.BlockSpec((FF_SLAB, D), lambda i, j: (j, 0))],
        out_specs=[rows, uspec, uspec],
        out_shape=[_sds((S, D)), _sds((FF_PAIRS, S, FF_SLAB), BF16), _sds((FF_PAIRS, S, FF_SLAB), BF16)],
        compiler_params=_params(("parallel", "arbitrary")),
    )(h2, h2, x1, w_up, w_up, conv_w, conv_w, w_down)


def _ffn_bwd(dx2, h2, ug, uu, conv_w, w_down, *, name, tm=512):
    S, D = h2.shape
    ni = S // tm
    per = tm // ROWS16
    ext = tm + ROWS16

    def body(dx_ref, dxn_ref, h_ref, ug_ref, ugp_ref, ugn_ref, uu_ref, uup_ref, uun_ref, cg_ref, cu_ref, wd_ref,
             dug_ref, duu_ref, gd_ref, gg_ref, gu_ref, dcg_ref, dcu_ref, acc_d, acc_g, acc_u, acc_cg, acc_cu):
        i = pl.program_id(1)

        @pl.when(i == 0)
        def _():
            acc_d[...] = jnp.zeros_like(acc_d)
            acc_g[...] = jnp.zeros_like(acc_g)
            acc_u[...] = jnp.zeros_like(acc_u)
            acc_cg[...] = jnp.zeros_like(acc_cg)
            acc_cu[...] = jnp.zeros_like(acc_cu)

        dx = dx_ref[...]
        dxe = jnp.concatenate([dx, dxn_ref[...]], axis=0)
        row = lax.broadcasted_iota(jnp.int32, (ext, 1), 0)
        live = (i < ni - 1) | (row < tm)
        d_act = jnp.where(live, lax.dot_general(dxe, wd_ref[...], (((1,), (1,)), ((), ())), preferred_element_type=F32), 0.0)
        rowp = lax.broadcasted_iota(jnp.int32, (ext + ROWS16, 1), 0)
        keep = (i > 0) | (rowp >= ROWS16)

        def pre(cur, prev, nxt):
            return jnp.where(keep, jnp.concatenate([prev[...], cur[...], nxt[...]], axis=0).astype(F32), 0.0)

        uge, uue = pre(ug_ref, ugp_ref, ugn_ref), pre(uu_ref, uup_ref, uun_ref)
        cg, cu = cg_ref[...], cu_ref[...]
        base = ROWS16 - (FFN_CONV - 1)
        gate = _taps(cg, uge, base, ext)
        up = _taps(cu, uue, base, ext)
        sg = _sigmoid(gate)
        silu = gate * sg
        dgc = d_act * up * _dsilu(gate, sg)
        duc = d_act * silu

        def conv_t(w, dc):
            out = dc[FFN_CONV - 1:FFN_CONV - 1 + tm] * w[0:1]
            for t in range(1, FFN_CONV):
                out = out + dc[FFN_CONV - 1 - t:FFN_CONV - 1 - t + tm] * w[t:t + 1]
            return out.astype(BF16)

        du_g, du_u = conv_t(cg, dgc), conv_t(cu, duc)
        dug_ref[...] = du_g
        duu_ref[...] = du_u
        dcw = lambda dc, xe: jnp.concatenate(
            [jnp.sum(dc[0:tm] * xe[base + t:base + t + tm], axis=0, keepdims=True) for t in range(FFN_CONV)], axis=0)
        acc_cg[0:FFN_CONV, :] += dcw(dgc, uge)
        acc_cu[0:FFN_CONV, :] += dcw(duc, uue)
        tn = (((0,), (0,)), ((), ()))
        act = (silu[0:tm] * up[0:tm]).astype(BF16)
        acc_d[...] += lax.dot_general(act, dx, tn, preferred_element_type=F32)
        hv = h_ref[...]
        acc_g[...] += lax.dot_general(hv, du_g, tn, preferred_element_type=F32)
        acc_u[...] += lax.dot_general(hv, du_u, tn, preferred_element_type=F32)

        @pl.when(i == ni - 1)
        def _():
            gd_ref[...] = acc_d[...].astype(BF16)
            gg_ref[...] = acc_g[...].astype(BF16)
            gu_ref[...] = acc_u[...].astype(BF16)
            dcg_ref[...] = acc_cg[0:FFN_CONV, :]
            dcu_ref[...] = acc_cu[0:FFN_CONV, :]

    last16 = S // ROWS16 - 1
    rows = pl.BlockSpec((tm, D), lambda j, i: (i, 0))
    rows_next = pl.BlockSpec((ROWS16, D), lambda j, i: (jnp.minimum((i + 1) * per, last16), 0))
    u_cur = pl.BlockSpec((None, tm, FF_SLAB), lambda j, i: (j, i, 0))
    u_prev = pl.BlockSpec((None, ROWS16, FF_SLAB), lambda j, i: (j, jnp.maximum(i * per - 1, 0), 0))
    u_next = pl.BlockSpec((None, ROWS16, FF_SLAB), lambda j, i: (j, jnp.minimum((i + 1) * per, last16), 0))
    cslab = lambda off: pl.BlockSpec((None, FFN_CONV, FF_SLAB), lambda j, i: (j + off, 0, 0))
    wslab = pl.BlockSpec((None, D, FF_SLAB), lambda j, i: (j, 0, 0))
    dslab = pl.BlockSpec((None, FFN_CONV, FF_SLAB), lambda j, i: (j, 0, 0))
    return pl.pallas_call(
        body, name=name, grid=(FF_PAIRS, ni),
        in_specs=[rows, rows_next, rows, u_cur, u_prev, u_next, u_cur, u_prev, u_next, cslab(0), cslab(FF_PAIRS),
                  pl.BlockSpec((FF_SLAB, D), lambda j, i: (j, 0))],
        out_specs=[u_cur, u_cur, pl.BlockSpec((FF_SLAB, D), lambda j, i: (j, 0)), wslab, wslab, dslab, dslab],
        out_shape=[_sds((FF_PAIRS, S, FF_SLAB), BF16), _sds((FF_PAIRS, S, FF_SLAB), BF16), _sds((D_FF, D), BF16),
                   _sds((FF_PAIRS, D, FF_SLAB), BF16), _sds((FF_PAIRS, D, FF_SLAB), BF16),
                   _sds((FF_PAIRS, FFN_CONV, FF_SLAB)), _sds((FF_PAIRS, FFN_CONV, FF_SLAB))],
        scratch_shapes=[pltpu.VMEM((FF_SLAB, D), F32), pltpu.VMEM((D, FF_SLAB), F32), pltpu.VMEM((D, FF_SLAB), F32),
                        pltpu.VMEM((8, FF_SLAB), F32), pltpu.VMEM((8, FF_SLAB), F32)],
        compiler_params=_params(("parallel", "arbitrary")),
    )(dx2, dx2, h2, ug, ug, ug, uu, uu, uu, conv_w, conv_w, w_down)


def _mm_slabs(a, w, w_off, *, name, res=None, tm=1024, tn=1024):
    nk, S, _ = a.shape
    D = w.shape[1]
    has_res = res is not None

    def body(*refs):
        if has_res:
            a_ref, w_ref, r_ref, o_ref, acc_ref = refs
        else:
            a_ref, w_ref, o_ref, acc_ref = refs
        k = pl.program_id(2)
        part = lax.dot_general(a_ref[...], w_ref[...], (((1,), (1,)), ((), ())), preferred_element_type=F32)

        @pl.when(k == 0)
        def _():
            acc_ref[...] = part

        @pl.when(k > 0)
        def _():
            acc_ref[...] += part

        @pl.when(k == nk - 1)
        def _():
            o_ref[...] = acc_ref[...] + r_ref[...] if has_res else acc_ref[...]

    o_spec = pl.BlockSpec((tm, tn), lambda i, j, k: (i, j))
    return pl.pallas_call(
        body, name=name, grid=(S // tm, D // tn, nk),
        in_specs=[pl.BlockSpec((None, tm, FF_SLAB), lambda i, j, k: (k, i, 0)),
                  pl.BlockSpec((None, tn, FF_SLAB), lambda i, j, k: (k + w_off, j, 0))] + ([o_spec] if has_res else []),
        out_specs=o_spec, out_shape=_sds((S, D)), scratch_shapes=[pltpu.VMEM((tm, tn), F32)],
        compiler_params=_params(("parallel", "parallel", "arbitrary")),
    )(*((a, w, res) if has_res else (a, w)))


def _local_step(x, tgt, norm1_w, w_a, w_z, w_b, conv_a, a_log, dt_bias, gnw, norm2_w, final_w, late_weights, emit):
    wgrad = functools.partial(_mm, ta=True, out_dtype=BF16)
    h1 = _rms_fwd(x, norm1_w, name="rms1_fwd")
    proj_a = _mm(h1, w_a, name="proj_a", tn=A_COLS, tk=1024)
    proj_z = _mm(h1, w_z, name="proj_z", tk=1024)
    proj_b = _mm(h1, w_b, name="proj_b", tn=768, tk=1024)
    qn, kn, v, gcb, bb = _gdn_prep_fwd(proj_a, conv_a, a_log, dt_bias, name="gdn_prep_fwd")
    uv, wk, at, tmat = _gdn_chunk_fwd(qn, kn, v, gcb, bb, name="gdn_chunk_fwd")
    o, u, sp, oab = _gdn_scan_fwd(uv, wk, at, qn, kn, gcb, bb, proj_z, gnw, name="gdn_scan_fwd")
    oab, lse = _attn_fwd(proj_b, oab, name="attn_fwd")
    w_out, w_up, conv_f, w_down = late_weights(oab)
    x1 = _mm(oab, w_out, res=x, name="out_proj", tk=1024)
    h2 = _rms_fwd(x1, norm2_w, name="rms2_fwd")
    x2, ug, uu = _ffn_fwd(h2, x1, w_up, conv_f, w_down, name="ffn_fwd")
    dx2, dx2_b, d_final, loss = _loss_head(x2, final_w, tgt, name="loss_head")
    dug, duu, g_down, g_up_g, g_up_u, dcw_g, dcw_u = _ffn_bwd(dx2_b, h2, ug, uu, conv_f, w_down, name="ffn_bwd")
    token = emit("ffn", w_down=g_down, w_up=jnp.concatenate([g_up_g, g_up_u], axis=0),
                 conv_f=jnp.concatenate([dcw_g, dcw_u], axis=0))
    dh2 = _mm_slabs(dug, w_up, 0, name="ffn_up_dx_gate")
    dh2 = _mm_slabs(duu, w_up, FF_PAIRS, res=dh2, name="ffn_up_dx_up")
    dx1, d_norm2 = _rms_bwd(dh2, x1, _behind(norm2_w, token), dx2, name="rms2_bwd")
    d_oab = _mm(dx1, w_out, tb=True, name="out_proj_dx", tk=1024)
    gnw = _behind(gnw, emit("out", w_out=wgrad(oab, dx1, name="out_proj_dw")))
    dz, d_gnw, du, dwk, dat, dqd, dke, dgl = _gdn_scan_bwd(d_oab, o, proj_z, gnw, sp, u, wk, at, qn, kn, gcb, bb, name="gdn_scan_bwd")
    dqn, dkn, dv, dg, dbeta = _gdn_chunk_bwd(qn, kn, v, gcb, bb, tmat, uv, wk, du, dwk, dat, dqd, dke, dgl, name="gdn_chunk_bwd")
    dc, dba, d_small = _gdn_prep_bwd(dqn, dkn, dv, dg, dbeta, proj_a, conv_a, a_log, dt_bias, name="gdn_prep_bwd")
    d_pa, d_conv_a = _gdn_conv_bwd(dc, dba, proj_a, conv_a, name="gdn_conv_bwd")
    d_pb = _attn_bwd(proj_b, oab, d_oab, lse, name="attn_bwd")
    g_a = wgrad(h1, d_pa, name="proj_a_dw", tn=A_COLS)
    g_z = wgrad(h1, dz, name="proj_z_dw")
    g_b = wgrad(h1, d_pb, name="proj_b_dw", tn=768)
    w_z = _behind(w_z, emit("in", w_a=g_a, w_z=g_z, w_b=g_b, conv_a=d_conv_a))
    dh1 = _mm(dz, w_z, tb=True, name="proj_z_dx")
    dh1 = _mm(d_pa, w_a, tb=True, res=dh1, name="proj_a_dx", tk=A_COLS)
    dh1 = _mm(d_pb, w_b, tb=True, res=dh1, name="proj_b_dx", tk=1536)
    grad_x, d_norm1 = _rms_bwd(dh1, x, norm1_w, dx1, name="rms1_bwd")
    small = dict(norm1=d_norm1, small=d_small, gnw=d_gnw, norm2=d_norm2, final=d_final)
    return loss, grad_x, small


_O1 = 3 * GDN_WIDTH
_O2 = _O1 + GDN_WIDTH
_O3 = _O2 + 2 * GDN_HEADS


def _split_w_in(w_in):
    d = w_in.shape[0]
    pad = jnp.zeros((d, A_COLS - _O1 - 2 * GDN_HEADS), w_in.dtype)
    w_a = jnp.concatenate([w_in[:, :_O1], w_in[:, _O2:_O3], pad], axis=1)
    w_b = w_in[:, _O3:].reshape(d, 3, DIL_PAIRS, 128).transpose(0, 2, 1, 3).reshape(d, 3 * DIL_WIDTH)
    return w_a, w_in[:, _O1:_O2], w_b


def _merge_g_in(g_a, g_z, g_b):
    d = g_a.shape[0]
    g_b = g_b.reshape(d, DIL_PAIRS, 3, 128).transpose(0, 2, 1, 3).reshape(d, 3 * DIL_WIDTH)
    return jnp.concatenate([g_a[:, :_O1], g_z, g_a[:, _O1:_O1 + 2 * GDN_HEADS], g_b], axis=1)


MESH = pl.DeviceIdType.MESH
ANY = pl.BlockSpec(memory_space=pl.ANY)


def _position():
    return lax.axis_index("x"), lax.axis_index("y"), lax.axis_index("c")


def _slot(p):
    return 4 * p[0] + 2 * p[1] + p[2]


def _all_gather(blocks, *, name):
    n = len(blocks)

    def body(*refs):
        ins, outs = refs[:n], refs[n:2 * n]
        send_sems, recv_sems, local_sems = refs[2 * n:]
        x, y, c = _position()
        me, sibling = (x, y, c), (x, y, 1 - c)
        chips = [(1 - x, y), (x, 1 - y), (1 - x, 1 - y)]

        def copy(a, k, block, to, src=None):
            dst = outs[a].at[_slot(block)]
            return pltpu.make_async_remote_copy(
                src_ref=dst if src is None else src, dst_ref=dst, send_sem=send_sems.at[a, k], recv_sem=recv_sems.at[a, k],
                device_id=to, device_id_type=MESH)

        mine = [pltpu.make_async_copy(ins[a], outs[a].at[_slot(me)], local_sems.at[a]) for a in range(n)]
        for cp in mine:
            cp.start()
        first = []
        for a in range(n):
            first.append(copy(a, 0, me, sibling, src=ins[a]))
            first += [copy(a, 1 + j, me, (*chip, c), src=ins[a]) for j, chip in enumerate(chips)]
        for cp in first:
            cp.start()
        passed = []
        for j, chip in enumerate(chips):
            for a in range(n):
                copy(a, 1 + j, (*chip, c), me).wait_recv()
                fwd = copy(a, 4 + j, (*chip, c), sibling)
                fwd.start()
                passed.append(fwd)
        for a in range(n):
            copy(a, 0, sibling, me).wait_recv()
            for j, chip in enumerate(chips):
                copy(a, 4 + j, (*chip, 1 - c), me).wait_recv()
        for cp in first + passed:
            cp.wait_send()
        for cp in mine:
            cp.wait()

    return pl.pallas_call(
        body, name=name, in_specs=[ANY] * n, out_specs=[ANY] * n,
        out_shape=[_sds((N_DEV,) + b.shape, b.dtype) for b in blocks],
        scratch_shapes=[pltpu.SemaphoreType.DMA((n, 7)), pltpu.SemaphoreType.DMA((n, 7)), pltpu.SemaphoreType.DMA((n,))],
    )(*blocks)


def _gather_direct(block, *, name):
    def body(in_ref, out_ref, send_sems, recv_sems, local_sem):
        x, y, c = _position()
        me = _slot((x, y, c))
        mine = pltpu.make_async_copy(in_ref, out_ref.at[me], local_sem)
        mine.start()
        copies = [pltpu.make_async_remote_copy(
            src_ref=in_ref, dst_ref=out_ref.at[me], send_sem=send_sems.at[k - 1], recv_sem=recv_sems.at[k - 1],
            device_id=_peer_of(k, x, y, c), device_id_type=MESH) for k in range(1, N_DEV)]
        for cp in copies:
            cp.start()
        for cp in copies:
            cp.wait()
        mine.wait()

    return pl.pallas_call(
        body, name=name, in_specs=[pl.BlockSpec(memory_space=pltpu.VMEM)], out_specs=pl.BlockSpec(memory_space=pltpu.VMEM),
        out_shape=_sds((N_DEV,) + block.shape, block.dtype),
        scratch_shapes=[pltpu.SemaphoreType.DMA((N_DEV - 1,)), pltpu.SemaphoreType.DMA((N_DEV - 1,)), pltpu.SemaphoreType.DMA],
    )(block)


HBM = pl.BlockSpec(memory_space=pltpu.HBM)
SEM = pl.BlockSpec(memory_space=pltpu.SEMAPHORE)
EFFECT = pltpu.SideEffectType.DATAFLOW_SIDE_EFFECTING


def _peer_of(k, x, y, c):
    return (1 - x if k & 4 else x, 1 - y if k & 2 else y, 1 - c if k & 1 else c)


def _flight(a, k):
    return a * (N_DEV - 1) + k - 1


def _exchange_start(arrays, *, name, broadcast=False):
    n = len(arrays)

    def body(*refs):
        ins, lands = refs[:n], refs[n:2 * n]
        send_sems, recv_sems = refs[2 * n:2 * n + 2]
        token = refs[-1]
        x, y, c = _position()
        me = _slot((x, y, c))
        for k in range(1, N_DEV):
            peer = _peer_of(k, x, y, c)
            for a in range(n):
                pltpu.make_async_remote_copy(
                    src_ref=ins[a] if broadcast else ins[a].at[_slot(peer)], dst_ref=lands[a].at[me],
                    send_sem=send_sems.at[_flight(a, k)], recv_sem=recv_sems.at[_flight(a, k)],
                    device_id=peer, device_id_type=MESH).start()
        token[...] = jnp.zeros_like(token)

    land_shapes = [((N_DEV,) + s.shape) if broadcast else s.shape for s in arrays]
    lands = [pltpu.with_memory_space_constraint(jnp.zeros(shp, s.dtype), pltpu.HBM) for shp, s in zip(land_shapes, arrays)]
    srcs = [pltpu.with_memory_space_constraint(s, pltpu.HBM) for s in arrays]
    outs = pl.pallas_call(
        body, name=name, in_specs=[HBM] * (2 * n),
        out_specs=[SEM, SEM] + [HBM] * (2 * n) + [pl.BlockSpec(memory_space=pltpu.VMEM)],
        out_shape=[pltpu.SemaphoreType.DMA((n * (N_DEV - 1),)), pltpu.SemaphoreType.DMA((n * (N_DEV - 1),))]
        + [pltpu.HBM(s.shape, s.dtype) for s in arrays] + [pltpu.HBM(shp, s.dtype) for shp, s in zip(land_shapes, arrays)]
        + [_sds((8, 128))],
        input_output_aliases={i: 2 + i for i in range(2 * n)},
        compiler_params=pltpu.CompilerParams(has_side_effects=EFFECT),
    )(*srcs, *lands)
    return outs[0], outs[1], outs[2:2 + n], outs[2 + n:2 + 2 * n], outs[-1]


def _exchange_wait(send_sems, recv_sems, srcs, lands, after, *, name, broadcast=False):
    n = len(srcs)

    def body(*refs):
        ins, lnd = refs[:n], refs[n:2 * n]
        send_ref, recv_ref = refs[2 * n:2 * n + 2]
        x, y, c = _position()
        for k in range(1, N_DEV):
            for a in range(n):
                cp = pltpu.make_async_remote_copy(
                    src_ref=ins[a] if broadcast else ins[a].at[0], dst_ref=lnd[a].at[0], send_sem=send_ref.at[_flight(a, k)],
                    recv_sem=recv_ref.at[_flight(a, k)], device_id=_peer_of(k, x, y, c), device_id_type=MESH)
                cp.wait_send()
                cp.wait_recv()

    outs = pl.pallas_call(
        body, name=name, in_specs=[HBM] * (2 * n) + [SEM, SEM, ANY], out_specs=[HBM] * (2 * n),
        out_shape=[pltpu.HBM(s.shape, s.dtype) for s in srcs] + [pltpu.HBM(s.shape, s.dtype) for s in lands],
        input_output_aliases={i: i for i in range(2 * n)},
        compiler_params=pltpu.CompilerParams(has_side_effects=EFFECT),
    )(*srcs, *lands, send_sems, recv_sems, after)
    return outs[:n], outs[n:]


def _behind(x, token):
    return x if token is None else x + token[0, 0].astype(x.dtype)


def _adamw(parts, w, m, v, *, name, own=None, tr=None):
    R, C = w.shape
    tr = R if tr is None else tr
    assert R % tr == 0
    c1 = 1.0 - ADAM_B1 ** ADAM_STEP
    c2 = 1.0 - ADAM_B2 ** ADAM_STEP
    has_own = own is not None

    def body(*refs):
        if has_own:
            own_ref, p_ref, w_ref, m_ref, v_ref, g_ref, d_ref, nm_ref, nv_ref = refs
            g = own_ref[...].astype(F32) + p_ref[0].astype(F32)
        else:
            p_ref, w_ref, m_ref, v_ref, g_ref, d_ref, nm_ref, nv_ref = refs
            g = p_ref[0].astype(F32)
        for s in range(1, N_DEV):
            g = g + p_ref[s].astype(F32)
        nm = ADAM_B1 * m_ref[...] + (1.0 - ADAM_B1) * g
        nv = ADAM_B2 * v_ref[...] + (1.0 - ADAM_B2) * (g * g)
        g_ref[...] = g
        nm_ref[...] = nm
        nv_ref[...] = nv
        d_ref[...] = -ADAM_LR * ((nm / c1) / (jnp.sqrt(nv / c2) + ADAM_EPS) + ADAM_WD * w_ref[...])

    blk = pl.BlockSpec((tr, C), lambda i: (i, 0))
    return pl.pallas_call(
        body, name=name, grid=(R // tr,),
        in_specs=[blk] * has_own + [pl.BlockSpec((N_DEV, tr, C), lambda i: (0, i, 0)), blk, blk, blk],
        out_specs=[blk] * 4, out_shape=[_sds((R, C))] * 4, compiler_params=_params(("parallel",)),
    )(*((own,) if has_own else ()), parts, w, m, v)


_SMALL_ROWS = 8


def _pack_small(norm1, norm2, final, gnw, a_log, dt_bias, loss=None):
    loss = jnp.zeros((1, 128), F32) if loss is None else loss
    row3 = jnp.concatenate([gnw, a_log, dt_bias, jnp.zeros((1, 128 - 2 * GDN_HEADS), F32), loss,
                            jnp.zeros((1, D_MODEL - 3 * 128), F32)], axis=1)
    return jnp.concatenate([norm1, norm2, final, row3, jnp.zeros((_SMALL_ROWS - 4, D_MODEL), F32)], axis=0)


def _unpack_small(p):
    return (p[0:1], p[1:2], p[2], p[3:4, 0:128], p[3:4, 128:128 + GDN_HEADS], p[3:4, 128 + GDN_HEADS:128 + 2 * GDN_HEADS])


def _slabs_by_cols(g):
    r = g.shape[0]
    return g.reshape(r, N_DEV, -1).transpose(1, 0, 2)


def _cols_from_slabs(s):
    return s.transpose(1, 0, 2).reshape(s.shape[1], -1)


def kernel(x, norm1_w, w_in, conv_qkv_w, a_log, dt_bias, gdn_norm_w, w_out, norm2_w, w_up, ffn_conv_w, w_down, final_norm_w, loss_target, m_norm1_w, m_w_in, m_conv_qkv_w, m_a_log, m_dt_bias, m_gdn_norm_w, m_w_out, m_norm2_w, m_w_up, m_ffn_conv_w, m_w_down, m_final_norm_w, v_norm1_w, v_w_in, v_conv_qkv_w, v_a_log, v_dt_bias, v_gdn_norm_w, v_w_out, v_norm2_w, v_w_up, v_ffn_conv_w, v_w_down, v_final_norm_w):
    bf = lambda a: a.astype(BF16)
    me = _slot(_position())
    gw_in, g_conv_a = _all_gather([bf(w_in[0]), conv_qkv_w[0]], name="gather_w_in")
    w_a, w_z, w_b = _split_w_in(_cols_from_slabs(gw_in))
    late_src, _ = lax.optimization_barrier(([bf(w_out[0]), bf(w_up[0]), bf(w_down[0]), ffn_conv_w[0]], gw_in))
    l_send, l_recv, l_srcs, l_lands, l_token = _exchange_start(late_src, name="weights_start", broadcast=True)

    def late_weights(after):
        srcs, landed = _exchange_wait(l_send, l_recv, l_srcs, l_lands, after, name="weights_wait", broadcast=True)
        gw_out, gw_up, gw_down, g_conv_f = [lax.dynamic_update_index_in_dim(l, s, me, 0) for l, s in zip(landed, srcs)]
        return gw_out.reshape(D_MODEL, D_MODEL), gw_up, g_conv_f, gw_down.reshape(D_FF, D_MODEL)

    flights = {}

    def emit(group, **grads):
        if group == "in":
            slabs = dict(w_in=_slabs_by_cols(_merge_g_in(grads["w_a"], grads["w_z"], grads["w_b"])),
                         conv_a=_slabs_by_cols(grads["conv_a"]))
        elif group == "ffn":
            slabs = dict(w_down=grads["w_down"].reshape(N_DEV, -1, D_MODEL), w_up=grads["w_up"], conv_f=grads["conv_f"])
        else:
            slabs = {k: v.reshape(N_DEV, -1, D_MODEL) for k, v in grads.items()}
        names = list(slabs)
        own = {k: lax.dynamic_index_in_dim(slabs[k], me, 0, keepdims=False) for k in names}
        *flight, token = _exchange_start([slabs[k] for k in names], name="grads_start_" + group)
        flights[group] = (names, own, flight)
        return token

    loss, grad_x, g = _local_step(
        x[0], loss_target[0], _behind(norm1_w, l_token), w_a, w_z, w_b, _cols_from_slabs(g_conv_a), a_log, dt_bias,
        gdn_norm_w, norm2_w, final_norm_w[None], late_weights, emit)
    small_all = _gather_direct(
        _pack_small(g["norm1"], g["norm2"], g["final"], g["gnw"], g["small"][:, 0:GDN_HEADS],
                    g["small"][:, GDN_HEADS:2 * GDN_HEADS], loss), name="gather_small")
    got, mine = {}, {}
    for group in ("ffn", "out", "in"):
        names, own, (send_sems, recv_sems, srcs, lands) = flights[group]
        _, landed = _exchange_wait(send_sems, recv_sems, srcs, lands, small_all, name="grads_wait_" + group)
        got.update(zip(names, landed))
        mine.update(own)
    o_in = _adamw(got["w_in"], w_in[0], m_w_in[0], v_w_in[0], own=mine["w_in"], name="adamw_w_in", tr=128)
    o_out = _adamw(got["w_out"], w_out[0], m_w_out[0], v_w_out[0], own=mine["w_out"], name="adamw_w_out")
    o_up = _adamw(got["w_up"], w_up[0], m_w_up[0], v_w_up[0], own=mine["w_up"], name="adamw_w_up", tr=128)
    o_down = _adamw(got["w_down"], w_down[0], m_w_down[0], v_w_down[0], own=mine["w_down"], name="adamw_w_down", tr=176)
    o_ca = _adamw(got["conv_a"], conv_qkv_w[0], m_conv_qkv_w[0], v_conv_qkv_w[0], own=mine["conv_a"], name="adamw_conv_a")
    o_cf = _adamw(got["conv_f"], ffn_conv_w[0], m_ffn_conv_w[0], v_ffn_conv_w[0], own=mine["conv_f"], name="adamw_conv_f")
    o_small = _adamw(
        small_all, _pack_small(norm1_w, norm2_w, final_norm_w[None], gdn_norm_w, a_log, dt_bias),
        _pack_small(m_norm1_w, m_norm2_w, m_final_norm_w[None], m_gdn_norm_w, m_a_log, m_dt_bias),
        _pack_small(v_norm1_w, v_norm2_w, v_final_norm_w[None], v_gdn_norm_w, v_a_log, v_dt_bias), name="adamw_small")
    total_loss = o_small[0][3, 256]
    outs = [total_loss, grad_x[None]]
    for k in range(4):
        n1, n2, fin, gn, al, dt = _unpack_small(o_small[k])
        outs += [n1, o_in[k][None], o_ca[k][None], al, dt, gn, o_out[k][None], n2, o_up[k][None], o_cf[k][None], o_down[k][None], fin]
    return tuple(outs)
```

```python
import functools

import jax
import jax.numpy as jnp
from jax import lax
from jax.experimental import pallas as pl
from jax.experimental.pallas import tpu as pltpu

F32 = jnp.float32
BF16 = jnp.bfloat16

N_DEV = 8
D_MODEL = 1024
GDN_HEADS = 4
GDN_DIM = 128
GDN_WIDTH = GDN_HEADS * GDN_DIM
GDN_CONV = 4
CHUNK = 64
CHUNKS_PER_STEP = 2
DIL_HEADS = 8
DIL_DIM = 64
DIL_WIDTH = DIL_HEADS * DIL_DIM
DIL_PAIRS = DIL_HEADS // 2
DILATIONS = (1, 4, 16)
BAND = 128
D_FF = 2816
FFN_CONV = 3
EPS = 1e-6
A_COLS = 3 * GDN_WIDTH + 128
HALO = 8

ADAM_LR = 0.001
ADAM_B1 = 0.9
ADAM_B2 = 0.999
ADAM_EPS = 1e-08
ADAM_WD = 0.01
ADAM_STEP = 10

VMEM_LIMIT_BYTES = 56 * 1024 * 1024
NEG_BIG = -1e30


def _params(sem=None):
    return pltpu.CompilerParams(dimension_semantics=sem, vmem_limit_bytes=VMEM_LIMIT_BYTES)


def _sds(shape, dtype=F32):
    return jax.ShapeDtypeStruct(shape, dtype)


def _bdot(a, b):
    return jnp.dot(a.astype(BF16), b.astype(BF16), preferred_element_type=F32)


def _bdot_nt(a, b):
    return lax.dot_general(a.astype(BF16), b.astype(BF16), (((1,), (1,)), ((), ())), preferred_element_type=F32)


def _bdot_tn(a, b):
    return lax.dot_general(a.astype(BF16), b.astype(BF16), (((0,), (0,)), ((), ())), preferred_element_type=F32)


def _split(a):
    hi = a.astype(BF16)
    lo = (a - hi.astype(F32)).astype(BF16)
    return hi, lo


def _dot3(a, b, dims):
    ah, al = _split(a)
    bh, bl = _split(b)
    d = functools.partial(lax.dot_general, dimension_numbers=(dims, ((), ())), preferred_element_type=F32)
    return d(ah, bh) + (d(al, bh) + d(ah, bl))


def _exact_tri_dot(tri, g):
    g1 = g.astype(BF16)
    r1 = g - g1.astype(F32)
    g2 = r1.astype(BF16)
    g3 = (r1 - g2.astype(F32)).astype(BF16)
    t = tri.astype(BF16)
    d = functools.partial(jnp.dot, preferred_element_type=F32)
    return d(t, g1) + (d(t, g2) + d(t, g3))


def _sigmoid(x):
    return 1.0 / (1.0 + jnp.exp(-x))


def _dsilu(x, sg):
    return sg * (1.0 + x * (1.0 - sg))


def _mm(a, b, *, name, ta=False, tb=False, res=None, out_dtype=F32, tm=512, tn=512, tk=512):
    if ta:
        K, M = a.shape
    else:
        M, K = a.shape
    if tb:
        N, Kb = b.shape
    else:
        Kb, N = b.shape
    assert K == Kb, (a.shape, b.shape)
    tm, tn, tk = min(tm, M), min(tn, N), min(tk, K)
    assert M % tm == 0 and N % tn == 0 and K % tk == 0, (name, M, N, K, tm, tn, tk)
    nk = K // tk
    dims = (((0 if ta else 1,), (1 if tb else 0,)), ((), ()))
    has_res = res is not None

    def body(*refs):
        if has_res:
            a_ref, b_ref, r_ref, o_ref, acc_ref = refs
        else:
            a_ref, b_ref, o_ref, acc_ref = refs
        k = pl.program_id(2)
        part = lax.dot_general(a_ref[...].astype(BF16), b_ref[...].astype(BF16), dims, preferred_element_type=F32)

        @pl.when(k == 0)
        def _():
            acc_ref[...] = part

        @pl.when(k > 0)
        def _():
            acc_ref[...] += part

        @pl.when(k == nk - 1)
        def _():
            r = acc_ref[...]
            if has_res:
                r = r + r_ref[...]
            o_ref[...] = r.astype(out_dtype)

    a_spec = pl.BlockSpec((tk, tm), lambda i, j, k: (k, i)) if ta else pl.BlockSpec((tm, tk), lambda i, j, k: (i, k))
    b_spec = pl.BlockSpec((tn, tk), lambda i, j, k: (j, k)) if tb else pl.BlockSpec((tk, tn), lambda i, j, k: (k, j))
    o_spec = pl.BlockSpec((tm, tn), lambda i, j, k: (i, j))
    in_specs = [a_spec, b_spec] + ([o_spec] if has_res else [])
    args = (a, b) + ((res,) if has_res else ())
    return pl.pallas_call(
        body, name=name, grid=(M // tm, N // tn, nk), in_specs=in_specs, out_specs=o_spec,
        out_shape=_sds((M, N), out_dtype), scratch_shapes=[pltpu.VMEM((tm, tn), F32)],
        compiler_params=_params(("parallel", "parallel", "arbitrary")),
    )(*args)


def _rms_fwd(x, w, *, name, tm=256):
    S, D = x.shape

    def body(x_ref, w_ref, h_ref):
        xv = x_ref[...]
        r = lax.rsqrt(jnp.mean(xv * xv, axis=-1, keepdims=True) + EPS)
        h_ref[...] = (xv * r * w_ref[...]).astype(BF16)

    return pl.pallas_call(
        body, name=name, grid=(S // tm,),
        in_specs=[pl.BlockSpec((tm, D), lambda i: (i, 0)), pl.BlockSpec((1, D), lambda i: (0, 0))],
        out_specs=pl.BlockSpec((tm, D), lambda i: (i, 0)), out_shape=_sds((S, D), BF16),
        compiler_params=_params(("parallel",)),
    )(x, w)


def _rms_bwd(dh, x, w, res, *, name, tm=256):
    S, D = x.shape

    def body(dh_ref, x_ref, w_ref, res_ref, dx_ref, dw_ref):
        i = pl.program_id(0)
        xv = x_ref[...]
        g = dh_ref[...]
        r = lax.rsqrt(jnp.mean(xv * xv, axis=-1, keepdims=True) + EPS)
        xh = xv * r
        gw = g * w_ref[...]
        dx_ref[...] = res_ref[...] + r * (gw - xh * jnp.mean(gw * xh, axis=-1, keepdims=True))
        part = jnp.sum(g * xh, axis=0, keepdims=True)

        @pl.when(i == 0)
        def _():
            dw_ref[...] = part

        @pl.when(i > 0)
        def _():
            dw_ref[...] += part

    row = pl.BlockSpec((tm, D), lambda i: (i, 0))
    one = pl.BlockSpec((1, D), lambda i: (0, 0))
    return pl.pallas_call(
        body, name=name, grid=(S // tm,), in_specs=[row, row, one, row], out_specs=[row, one],
        out_shape=[_sds((S, D)), _sds((1, D))], compiler_params=_params(("arbitrary",)),
    )(dh, x, w, res)


def _loss_head(x2, w, tgt, *, name, tm=256):
    S, D = x2.shape

    def body(x_ref, w_ref, t_ref, dx_ref, dxb_ref, dw_ref, loss_ref):
        i = pl.program_id(0)
        xv = x_ref[...]
        wv = w_ref[...]
        r = lax.rsqrt(jnp.mean(xv * xv, axis=-1, keepdims=True) + EPS)
        xh = xv * r
        err = xh * wv - t_ref[...]
        lrow = jnp.sum(err * err, axis=-1, keepdims=True)
        lsum = jnp.sum(lrow, axis=0, keepdims=True) * (0.5 / D)
        g = err * (1.0 / D)
        gw = g * wv
        dx = r * (gw - xh * jnp.mean(gw * xh, axis=-1, keepdims=True))
        dx_ref[...] = dx
        dxb_ref[...] = dx.astype(BF16)
        part = jnp.sum(g * xh, axis=0, keepdims=True)
        lpart = jnp.broadcast_to(lsum, (1, 128))

        @pl.when(i == 0)
        def _():
            dw_ref[...] = part
            loss_ref[...] = lpart

        @pl.when(i > 0)
        def _():
            dw_ref[...] += part
            loss_ref[...] += lpart

    row = pl.BlockSpec((tm, D), lambda i: (i, 0))
    one = pl.BlockSpec((1, D), lambda i: (0, 0))
    return pl.pallas_call(
        body, name=name, grid=(S // tm,), in_specs=[row, one, row],
        out_specs=[row, row, one, pl.BlockSpec((1, 128), lambda i: (0, 0))],
        out_shape=[_sds((S, D)), _sds((S, D), BF16), _sds((1, D)), _sds((1, 128))], compiler_params=_params(("arbitrary",)),
    )(x2, w, tgt)


def _conv_rows(prev, cur, w, taps):
    n = cur.shape[0]
    xs = jnp.concatenate([prev, cur], axis=0)
    base = HALO - (taps - 1)
    out = xs[base:base + n] * w[0:1]
    for i in range(1, taps):
        out = out + xs[base + i:base + i + n] * w[i:i + 1]
    return out


def _conv_rows_bwd(cur_d, next_d, prev_x, cur_x, w, taps):
    n = cur_d.shape[0]
    ds = jnp.concatenate([cur_d, next_d], axis=0)
    dx = ds[taps - 1:taps - 1 + n] * w[0:1]
    for i in range(1, taps):
        dx = dx + ds[taps - 1 - i:taps - 1 - i + n] * w[i:i + 1]
    xs = jnp.concatenate([prev_x, cur_x], axis=0)
    base = HALO - (taps - 1)
    dws = [jnp.sum(cur_d * xs[base + i:base + i + n], axis=0, keepdims=True) for i in range(taps)]
    return dx, jnp.concatenate(dws, axis=0)


def _halo_specs(tm, width, col, nblk):
    per = tm // HALO
    prev = pl.BlockSpec((HALO, width), lambda i, *_: (jnp.maximum(i * per - 1, 0), col))
    nxt = pl.BlockSpec((HALO, width), lambda i, *_: (jnp.minimum((i + 1) * per, nblk * per - 1), col))
    return prev, nxt


def _softplus(x):
    return jnp.maximum(x, 0.0) + jnp.log1p(jnp.exp(-jnp.abs(x)))


def _chunk_tri(tm, upper=False):
    r = lax.broadcasted_iota(jnp.int32, (tm, tm), 0)
    c = lax.broadcasted_iota(jnp.int32, (tm, tm), 1)
    same = lax.div(r, CHUNK) == lax.div(c, CHUNK)
    order = (c >= r) if upper else (c <= r)
    return jnp.where(same & order, 1.0, 0.0)


def _gdn_prep_fwd(proj_a, conv_w, a_log, dt_bias, *, name, tm=256):
    S = proj_a.shape[0]
    nblk = S // tm
    W3 = 3 * GDN_WIDTH

    def body(cur_ref, prev_ref, ba_ref, cw_ref, al_ref, dt_ref, qn_ref, kn_ref, v_ref, gcb_ref, bb_ref):
        i = pl.program_id(0)
        prev = jnp.where(i > 0, prev_ref[...], 0.0)
        c = _conv_rows(prev, cur_ref[...], cw_ref[...], GDN_CONV)
        a = c * _sigmoid(c)
        ba = ba_ref[...]
        lane = lax.broadcasted_iota(jnp.int32, (tm, 128), 1)
        g4 = jnp.zeros((tm, 128), F32)
        for h in range(GDN_HEADS):
            sl = slice(GDN_DIM * h, GDN_DIM * (h + 1))
            qh = a[:, GDN_DIM * h:GDN_DIM * (h + 1)]
            kh = a[:, GDN_WIDTH + GDN_DIM * h:GDN_WIDTH + GDN_DIM * (h + 1)]
            qn_ref[:, sl] = qh * (lax.rsqrt(jnp.sum(qh * qh, axis=-1, keepdims=True) + EPS) * (GDN_DIM ** -0.5))
            kn_ref[:, sl] = kh * lax.rsqrt(jnp.sum(kh * kh, axis=-1, keepdims=True) + EPS)
            beta = _sigmoid(ba[:, h:h + 1])
            bb_ref[:, sl] = jnp.broadcast_to(beta, (tm, GDN_DIM))
            g = -jnp.exp(al_ref[0:1, h:h + 1]) * _softplus(ba[:, GDN_HEADS + h:GDN_HEADS + h + 1] + dt_ref[0:1, h:h + 1])
            g4 = jnp.where(lane == h, g, g4)
        v_ref[...] = a[:, 2 * GDN_WIDTH:]
        gc = _exact_tri_dot(_chunk_tri(tm), g4)
        for h in range(GDN_HEADS):
            gcb_ref[:, GDN_DIM * h:GDN_DIM * (h + 1)] = jnp.broadcast_to(gc[:, h:h + 1], (tm, GDN_DIM))

    prev_spec, _ = _halo_specs(tm, W3, 0, nblk)
    row = pl.BlockSpec((tm, GDN_WIDTH), lambda i: (i, 0))
    small = lambda a: pl.BlockSpec(a.shape, lambda i: (0, 0))
    return pl.pallas_call(
        body, name=name, grid=(nblk,),
        in_specs=[pl.BlockSpec((tm, W3), lambda i: (i, 0)), prev_spec,
                  pl.BlockSpec((tm, 128), lambda i: (i, W3 // 128)), small(conv_w), small(a_log), small(dt_bias)],
        out_specs=[row] * 5, out_shape=[_sds((S, GDN_WIDTH))] * 5, compiler_params=_params(("parallel",)),
    )(proj_a, proj_a, proj_a, conv_w, a_log, dt_bias)


def _chunk_masks():
    r = lax.broadcasted_iota(jnp.int32, (CHUNK, CHUNK), 0)
    c = lax.broadcasted_iota(jnp.int32, (CHUNK, CHUNK), 1)
    return r >= c, r > c, r == c


def _chunk_decay(gcb_h, bb_h, incl):
    G = gcb_h[:, 0:CHUNK]
    diff = G - G.T
    dec = jnp.where(incl, jnp.exp(jnp.where(incl, diff, 0.0)), 0.0)
    return dec, bb_h[:, 0:CHUNK].T


def _gdn_chunk_fwd(qn, kn, v, gcb, bb, *, name):
    S = qn.shape[0]
    nc = S // CHUNK

    def body(qn_ref, kn_ref, v_ref, gcb_ref, bb_ref, uv_ref, wk_ref, at_ref, t_ref):
        incl, strict, diag = _chunk_masks()
        for c in range(CHUNKS_PER_STEP):
            rows = slice(CHUNK * c, CHUNK * (c + 1))
            ats, ts = [], []
            for h in range(GDN_HEADS):
                sl = slice(GDN_DIM * h, GDN_DIM * (h + 1))
                q, k, vv, gh = qn_ref[rows, sl], kn_ref[rows, sl], v_ref[rows, sl], gcb_ref[rows, sl]
                dec, bt = _chunk_decay(gh, bb_ref[rows, sl], incl)
                lmat = jnp.where(strict, dec * _bdot_nt(k, k) * bt, 0.0)
                p = -lmat
                t = jnp.where(diag, 1.0, 0.0) + p
                for _ in range(5):
                    p = _bdot(p, p)
                    t = t + _bdot(t, p)
                rhs = jnp.concatenate([vv, jnp.exp(gh) * k], axis=1)
                sol = _dot3(t, rhs, ((1,), (0,)))
                uv_ref[rows, sl] = sol[:, :GDN_DIM]
                wk_ref[rows, sl] = sol[:, GDN_DIM:]
                ats.append(dec * _bdot_nt(q, k) * bt)
                ts.append(t)
            at_ref[rows, :] = jnp.concatenate(ats, axis=1)
            t_ref[rows, :] = jnp.concatenate(ts, axis=1)

    step = CHUNKS_PER_STEP * CHUNK
    row = pl.BlockSpec((step, GDN_WIDTH), lambda n: (n, 0))
    sq = pl.BlockSpec((step, GDN_HEADS * CHUNK), lambda n: (n, 0))
    return pl.pallas_call(
        body, name=name, grid=(S // step,), in_specs=[row] * 5, out_specs=[row, row, sq, sq],
        out_shape=[_sds((S, GDN_WIDTH)), _sds((S, GDN_WIDTH)), _sds((S, GDN_HEADS * CHUNK)), _sds((S, GDN_HEADS * CHUNK))],
        compiler_params=_params(("parallel",)),
    )(qn, kn, v, gcb, bb)


def _gdn_scan_fwd(uv, wk, at, qn, kn, gcb, bb, proj_z, gnw, *, name):
    S = uv.shape[0]
    nc = S // CHUNK

    def body(uv_ref, wk_ref, at_ref, qn_ref, kn_ref, gcb_ref, bb_ref, z_ref, gnw_ref, o_ref, u_ref, sp_ref, oa_ref, st_ref):
        n = pl.program_id(0)

        @pl.when(n == 0)
        def _():
            st_ref[...] = jnp.zeros_like(st_ref)

        oas = []
        for h in range(GDN_HEADS):
            sl = slice(GDN_DIM * h, GDN_DIM * (h + 1))
            st = st_ref[h]
            sp_ref[sl, :] = st
            gh = gcb_ref[:, sl]
            glast = gcb_ref[CHUNK - 1:CHUNK, sl]
            u = uv_ref[:, sl] - _bdot(wk_ref[:, sl], st)
            o = _bdot(qn_ref[:, sl] * jnp.exp(gh), st) + _bdot(at_ref[:, CHUNK * h:CHUNK * (h + 1)], u)
            ke = kn_ref[:, sl] * jnp.exp(glast - gh) * bb_ref[:, sl]
            st_ref[h] = jnp.exp(glast) * st + _bdot_tn(ke, u)
            u_ref[:, sl] = u
            o_ref[:, sl] = o
            z = z_ref[:, sl]
            r = lax.rsqrt(jnp.mean(o * o, axis=-1, keepdims=True) + EPS)
            oas.append(o * r * gnw_ref[...] * (z * _sigmoid(z)))
        oa_ref[...] = jnp.concatenate(oas, axis=1).astype(BF16)

    row = pl.BlockSpec((CHUNK, GDN_WIDTH), lambda n: (n, 0))
    sq = pl.BlockSpec((CHUNK, GDN_HEADS * CHUNK), lambda n: (n, 0))
    return pl.pallas_call(
        body, name=name, grid=(nc,),
        in_specs=[row, row, sq, row, row, row, row, row, pl.BlockSpec((1, GDN_DIM), lambda n: (0, 0))],
        out_specs=[row, row, pl.BlockSpec((GDN_WIDTH, GDN_DIM), lambda n: (n, 0)), row],
        out_shape=[_sds((S, GDN_WIDTH)), _sds((S, GDN_WIDTH)), _sds((nc * GDN_WIDTH, GDN_DIM)), _sds((S, 2 * GDN_WIDTH), BF16)],
        scratch_shapes=[pltpu.VMEM((GDN_HEADS, GDN_DIM, GDN_DIM), F32)],
        compiler_params=_params(("arbitrary",)),
    )(uv, wk, at, qn, kn, gcb, bb, proj_z, gnw)


def _gdn_scan_bwd(d_oab, o, proj_z, gnw, sp, u, wk, at, qn, kn, gcb, bb, *, name):
    S = o.shape[0]
    nc = S // CHUNK

    def body(do_ref, o_ref, z_ref, gnw_ref, sp_ref, u_ref, wk_ref, at_ref, qn_ref, kn_ref, gcb_ref, bb_ref,
             dz_ref, dgn_ref, du_ref, dwk_ref, dat_ref, dqd_ref, dke_ref, dgl_ref, ds_ref):
        n = pl.program_id(0)

        @pl.when(n == 0)
        def _():
            ds_ref[...] = jnp.zeros_like(ds_ref)
            dgn_ref[...] = jnp.zeros_like(dgn_ref)

        dats, dgn = [], jnp.zeros((1, GDN_DIM), F32)
        for h in range(GDN_HEADS):
            sl = slice(GDN_DIM * h, GDN_DIM * (h + 1))
            oo = o_ref[:, sl]
            z = z_ref[:, sl]
            gw = gnw_ref[...]
            sg = _sigmoid(z)
            r = lax.rsqrt(jnp.mean(oo * oo, axis=-1, keepdims=True) + EPS)
            xh = oo * r
            d_oa = do_ref[:, sl]
            dy = d_oa * (z * sg)
            dz_ref[:, sl] = (d_oa * (xh * gw) * _dsilu(z, sg)).astype(BF16)
            dgn = dgn + jnp.sum(dy * xh, axis=0, keepdims=True)
            dxh = dy * gw
            do = r * (dxh - xh * jnp.mean(dxh * xh, axis=-1, keepdims=True))

            st = sp_ref[sl, :]
            dst = ds_ref[h]
            gh = gcb_ref[:, sl]
            glast = gcb_ref[CHUNK - 1:CHUNK, sl]
            uu = u_ref[:, sl]
            ath = at_ref[:, CHUNK * h:CHUNK * (h + 1)]
            qd = qn_ref[:, sl] * jnp.exp(gh)
            ke = kn_ref[:, sl] * jnp.exp(glast - gh) * bb_ref[:, sl]
            ge = jnp.exp(glast)
            dqd_ref[:, sl] = _bdot_nt(do, st)
            dats.append(_bdot_nt(do, uu))
            du = _bdot_tn(ath, do) + _bdot(ke, dst)
            dke_ref[:, sl] = _bdot_nt(uu, dst)
            dge = jnp.sum(jnp.sum(dst * st, axis=1, keepdims=True), axis=0, keepdims=True)
            dgl_ref[0, :, sl] = jnp.broadcast_to(dge * ge, (8, GDN_DIM))
            ds_ref[h] = _bdot_tn(qd, do) + ge * dst - _bdot_tn(wk_ref[:, sl], du)
            du_ref[:, sl] = du
            dwk_ref[:, sl] = -_bdot_nt(du, st)
        dat_ref[...] = jnp.concatenate(dats, axis=1)
        dgn_ref[...] += dgn

    rev = lambda n: (nc - 1 - n, 0)
    row = pl.BlockSpec((CHUNK, GDN_WIDTH), rev)
    sq = pl.BlockSpec((CHUNK, GDN_HEADS * CHUNK), rev)
    one = pl.BlockSpec((1, GDN_DIM), lambda n: (0, 0))
    return pl.pallas_call(
        body, name=name, grid=(nc,),
        in_specs=[row, row, row, one, pl.BlockSpec((GDN_WIDTH, GDN_DIM), rev), row, row, sq, row, row, row, row],
        out_specs=[row, one, row, row, sq, row, row, pl.BlockSpec((1, 8, GDN_WIDTH), lambda n: (nc - 1 - n, 0, 0))],
        out_shape=[_sds((S, GDN_WIDTH), BF16), _sds((1, GDN_DIM)), _sds((S, GDN_WIDTH)), _sds((S, GDN_WIDTH)),
                   _sds((S, GDN_HEADS * CHUNK)), _sds((S, GDN_WIDTH)), _sds((S, GDN_WIDTH)), _sds((nc, 8, GDN_WIDTH))],
        scratch_shapes=[pltpu.VMEM((GDN_HEADS, GDN_DIM, GDN_DIM), F32)],
        compiler_params=_params(("arbitrary",)),
    )(d_oab, o, proj_z, gnw, sp, u, wk, at, qn, kn, gcb, bb)


def _gdn_chunk_bwd(qn, kn, v, gcb, bb, tmat, uv, wk, du, dwk, dat, dqd, dke, dgl, *, name):
    S = qn.shape[0]
    nc = S // CHUNK

    def body(qn_ref, kn_ref, v_ref, gcb_ref, bb_ref, t_ref, uv_ref, wk_ref, du_ref, dwk_ref, dat_ref, dqd_ref, dke_ref,
             dgl_ref, dq_ref, dk_ref, dv_ref, dg_ref, dbeta_ref):
        incl, strict, _ = _chunk_masks()
        lane = lax.broadcasted_iota(jnp.int32, (CHUNK, 128), 1)
        rowi = lax.broadcasted_iota(jnp.int32, (CHUNK, 1), 0)
        for c in range(CHUNKS_PER_STEP):
            rows = slice(CHUNK * c, CHUNK * (c + 1))
            last = slice(CHUNK * (c + 1) - 1, CHUNK * (c + 1))
            dgc4 = jnp.zeros((CHUNK, 128), F32)
            db4 = jnp.zeros((CHUNK, 128), F32)
            for h in range(GDN_HEADS):
                sl = slice(GDN_DIM * h, GDN_DIM * (h + 1))
                sq = slice(CHUNK * h, CHUNK * (h + 1))
                q, k, gh, bh = qn_ref[rows, sl], kn_ref[rows, sl], gcb_ref[rows, sl], bb_ref[rows, sl]
                dec, bt = _chunk_decay(gh, bh, incl)
                kk = _bdot_nt(k, k)
                qk = _bdot_nt(q, k)
                t = t_ref[rows, sq]
                d_sol = jnp.concatenate([du_ref[rows, sl], dwk_ref[rows, sl]], axis=1)
                d_rhs = _dot3(t, d_sol, ((0,), (0,)))
                sol = jnp.concatenate([uv_ref[rows, sl], wk_ref[rows, sl]], axis=1)
                d_l = jnp.where(strict, -_dot3(d_rhs, sol, ((1,), (1,))), 0.0)
                d_a = jnp.where(incl, dat_ref[rows, sq], 0.0)
                gam = jnp.exp(gh)
                glast = gcb_ref[last, sl]
                e = jnp.exp(glast - gh)
                d_gk = d_rhs[:, GDN_DIM:]
                dqd = dqd_ref[rows, sl]
                dke = dke_ref[rows, sl]
                ml = d_l * dec * bt
                ma = d_a * dec * bt
                dq_ref[rows, sl] = _bdot(ma, k) + dqd * gam
                dk_ref[rows, sl] = (_bdot(ml, k) + _bdot_tn(ml, k) + _bdot_tn(ma, q)) + d_gk * gam + dke * (e * bh)
                dv_ref[rows, sl] = d_rhs[:, :GDN_DIM]
                wb = d_l * dec * kk + d_a * dec * qk
                ew = wb * bt
                s_ke = jnp.sum(dke * k * (e * bh), axis=-1, keepdims=True)
                dbeta = jnp.sum(wb.T, axis=-1, keepdims=True) + jnp.sum(dke * k * e, axis=-1, keepdims=True)
                dgc = (jnp.sum(ew, axis=-1, keepdims=True) - jnp.sum(ew.T, axis=-1, keepdims=True)
                       + jnp.sum(dqd * q * gam, axis=-1, keepdims=True) + jnp.sum(d_gk * k * gam, axis=-1, keepdims=True) - s_ke)
                tail = jnp.sum(s_ke, axis=0, keepdims=True) + dgl_ref[c, 0:1, GDN_DIM * h:GDN_DIM * h + 1]
                dgc = dgc + jnp.where(rowi == CHUNK - 1, tail, 0.0)
                dgc4 = jnp.where(lane == h, dgc, dgc4)
                db4 = jnp.where(lane == h, dbeta, db4)
            dg_ref[rows, :] = _exact_tri_dot(_chunk_tri(CHUNK, upper=True), dgc4)
            dbeta_ref[rows, :] = db4

    step = CHUNKS_PER_STEP * CHUNK
    row = pl.BlockSpec((step, GDN_WIDTH), lambda n: (n, 0))
    sq = pl.BlockSpec((step, GDN_HEADS * CHUNK), lambda n: (n, 0))
    col = pl.BlockSpec((step, 128), lambda n: (n, 0))
    return pl.pallas_call(
        body, name=name, grid=(S // step,),
        in_specs=[row] * 5 + [sq, row, row, row, row, sq, row, row,
                              pl.BlockSpec((CHUNKS_PER_STEP, 8, GDN_WIDTH), lambda n: (n, 0, 0))],
        out_specs=[row, row, row, col, col],
        out_shape=[_sds((S, GDN_WIDTH))] * 3 + [_sds((S, 128))] * 2, compiler_params=_params(("parallel",)),
    )(qn, kn, v, gcb, bb, tmat, uv, wk, du, dwk, dat, dqd, dke, dgl)


def _gdn_prep_bwd(dqn, dkn, dv, dg, dbeta, proj_a, conv_w, a_log, dt_bias, *, name, tm=256):
    S = proj_a.shape[0]
    nblk = S // tm
    W3 = 3 * GDN_WIDTH

    def body(dqn_ref, dkn_ref, dv_ref, dg_ref, dbeta_ref, cur_ref, prev_ref, ba_ref, cw_ref, al_ref, dt_ref,
             dc_ref, dba_ref, sm_ref):
        i = pl.program_id(0)
        prev = jnp.where(i > 0, prev_ref[...], 0.0)
        c = _conv_rows(prev, cur_ref[...], cw_ref[...], GDN_CONV)
        sg = _sigmoid(c)
        a = c * sg
        dsl = _dsilu(c, sg)
        ba = ba_ref[...]
        lane = lax.broadcasted_iota(jnp.int32, (tm, 128), 1)
        lane1 = lax.broadcasted_iota(jnp.int32, (1, 128), 1)
        dba = jnp.zeros((tm, 128), F32)
        sm = jnp.zeros((1, 128), F32)
        for h in range(GDN_HEADS):
            sl = slice(GDN_DIM * h, GDN_DIM * (h + 1))
            ks = slice(GDN_WIDTH + GDN_DIM * h, GDN_WIDTH + GDN_DIM * (h + 1))
            qh, kh = a[:, sl], a[:, ks]
            rq = lax.rsqrt(jnp.sum(qh * qh, axis=-1, keepdims=True) + EPS)
            rk = lax.rsqrt(jnp.sum(kh * kh, axis=-1, keepdims=True) + EPS)
            qhat, khat = qh * rq, kh * rk
            dyq = dqn_ref[:, sl] * (GDN_DIM ** -0.5)
            dyk = dkn_ref[:, sl]
            dq = rq * (dyq - qhat * jnp.sum(dyq * qhat, axis=-1, keepdims=True))
            dk = rk * (dyk - khat * jnp.sum(dyk * khat, axis=-1, keepdims=True))
            dc_ref[:, sl] = dq * dsl[:, sl]
            dc_ref[:, ks] = dk * dsl[:, ks]
            beta = _sigmoid(ba[:, h:h + 1])
            db = dbeta_ref[:, h:h + 1] * beta * (1.0 - beta)
            aneg = -jnp.exp(al_ref[0:1, h:h + 1])
            xa = ba[:, GDN_HEADS + h:GDN_HEADS + h + 1] + dt_ref[0:1, h:h + 1]
            dgh = dg_ref[:, h:h + 1]
            dxa = dgh * aneg * _sigmoid(xa)
            dba = jnp.where(lane == h, db, dba)
            dba = jnp.where(lane == GDN_HEADS + h, dxa, dba)
            d_alog = jnp.sum(dgh * _softplus(xa), axis=0, keepdims=True) * aneg
            sm = jnp.where(lane1 == h, d_alog, sm)
            sm = jnp.where(lane1 == GDN_HEADS + h, jnp.sum(dxa, axis=0, keepdims=True), sm)
        vs = slice(2 * GDN_WIDTH, W3)
        dc_ref[:, vs] = dv_ref[...] * dsl[:, vs]
        dba_ref[...] = dba

        @pl.when(i == 0)
        def _():
            sm_ref[...] = sm

        @pl.when(i > 0)
        def _():
            sm_ref[...] += sm

    prev_spec, _ = _halo_specs(tm, W3, 0, nblk)
    row = pl.BlockSpec((tm, GDN_WIDTH), lambda i: (i, 0))
    col = pl.BlockSpec((tm, 128), lambda i: (i, 0))
    small = lambda a: pl.BlockSpec(a.shape, lambda i: (0, 0))
    return pl.pallas_call(
        body, name=name, grid=(nblk,),
        in_specs=[row, row, row, col, col, pl.BlockSpec((tm, W3), lambda i: (i, 0)), prev_spec,
                  pl.BlockSpec((tm, 128), lambda i: (i, W3 // 128)), small(conv_w), small(a_log), small(dt_bias)],
        out_specs=[pl.BlockSpec((tm, W3), lambda i: (i, 0)), col, pl.BlockSpec((1, 128), lambda i: (0, 0))],
        out_shape=[_sds((S, W3)), _sds((S, 128)), _sds((1, 128))], compiler_params=_params(("arbitrary",)),
    )(dqn, dkn, dv, dg, dbeta, proj_a, proj_a, proj_a, conv_w, a_log, dt_bias)


def _gdn_conv_bwd(dc, dba, proj_a, conv_w, *, name, tm=256):
    S = proj_a.shape[0]
    nblk = S // tm
    W3 = 3 * GDN_WIDTH

    def body(dc_ref, dnext_ref, dba_ref, cur_ref, prev_ref, cw_ref, da_ref, dcw_ref):
        i = pl.program_id(0)
        prev = jnp.where(i > 0, prev_ref[...], 0.0)
        nxt = jnp.where(i < nblk - 1, dnext_ref[...], 0.0)
        dx, dw = _conv_rows_bwd(dc_ref[...], nxt, prev, cur_ref[...], cw_ref[...], GDN_CONV)
        da_ref[:, 0:W3] = dx.astype(BF16)
        da_ref[:, W3:] = dba_ref[...].astype(BF16)

        @pl.when(i == 0)
        def _():
            dcw_ref[...] = dw

        @pl.when(i > 0)
        def _():
            dcw_ref[...] += dw

    prev_spec, next_spec = _halo_specs(tm, W3, 0, nblk)
    wide = pl.BlockSpec((tm, W3), lambda i: (i, 0))
    return pl.pallas_call(
        body, name=name, grid=(nblk,),
        in_specs=[wide, next_spec, pl.BlockSpec((tm, 128), lambda i: (i, 0)), wide, prev_spec,
                  pl.BlockSpec(conv_w.shape, lambda i: (0, 0))],
        out_specs=[pl.BlockSpec((tm, A_COLS), lambda i: (i, 0)), pl.BlockSpec(conv_w.shape, lambda i: (0, 0))],
        out_shape=[_sds((S, A_COLS), BF16), _sds(conv_w.shape)], compiler_params=_params(("arbitrary",)),
    )(dc, dc, dba, proj_a, proj_a, conv_w)


def _band_mask(nk):
    i = lax.broadcasted_iota(jnp.int32, (2 * BAND, nk), 0) & (BAND - 1)
    j = lax.broadcasted_iota(jnp.int32, (2 * BAND, nk), 1)
    if nk == BAND:
        return j <= i
    return (j >= i) & (j <= i + BAND)


def _stack_heads(x, lo):
    return jnp.concatenate([jnp.where(lo, x, 0.0), jnp.where(lo, 0.0, x)], axis=0)


def _stack_cols(x):
    return jnp.concatenate([x[:, 0:1], x[:, DIL_DIM:DIL_DIM + 1]], axis=0)


def _unstack(x, lo):
    return jnp.where(lo, x[0:BAND], x[BAND:2 * BAND])


def _rows(start, size, stride):
    return pl.ds(start, size) if stride == 1 else pl.ds(start, size, stride=stride)


ATTN_LANES = 4


def _attn_blocks(S, visit_many, lanes=ATTN_LANES):
    for d in DILATIONS:
        nb = S // (d * BAND)
        if d == 1:
            half = nb // 2
            visit_many(d, [(0, 0, True), (0, half, False)])

            def pair(n, c):
                visit_many(1, [(0, n, False), (0, n + half, False)])
                return c
            lax.fori_loop(1, half, pair, 0)
        elif nb > 1:
            for r0 in range(0, d, lanes):
                visit_many(d, [(r0 + t, 0, True) for t in range(lanes)])

                def column(n, c, d=d, r0=r0):
                    visit_many(d, [(r0 + t, n, False) for t in range(lanes)])
                    return c
                lax.fori_loop(1, nb, column, 0)
        else:
            def group(g, c, d=d):
                visit_many(d, [(g * lanes + t, 0, True) for t in range(lanes)])
                return c
            lax.fori_loop(0, d // lanes, group, 0)


def _attn_fwd(proj_b, oab, *, name):
    S = proj_b.shape[0]
    scale = DIL_DIM ** -0.5

    def body(q_ref, k_ref, v_ref, oab_in_ref, ob_ref, lse_ref, m_ref, l_ref, acc_ref):
        del oab_in_ref
        lane = lax.broadcasted_iota(jnp.int32, (BAND, 128), 1)
        lo = lane < DIL_DIM
        m_ref[...] = jnp.full_like(m_ref, NEG_BIG)
        l_ref[...] = jnp.zeros_like(l_ref)
        acc_ref[...] = jnp.zeros_like(acc_ref)

        def load(d, r, n, first):
            nk = BAND if first else 2 * BAND
            qrows = _rows(r + n * (BAND * d), BAND, d)
            krows = _rows(r if first else r + (n - 1) * (BAND * d), nk, d)
            return dict(nk=nk, qrows=qrows, q=q_ref[qrows, :] * scale, k=k_ref[krows, :].astype(BF16),
                        v=v_ref[krows, :].astype(BF16), m=m_ref[qrows, :], l=l_ref[qrows, :], acc=acc_ref[qrows, :])

        def compute(b):
            q, k, v = b["q"], b["k"], b["v"]
            s = jnp.where(_band_mask(b["nk"]), _bdot_nt(_stack_heads(q, lo), k), NEG_BIG)
            m_old = _stack_cols(b["m"])
            m_new = jnp.maximum(m_old, jnp.max(s, axis=-1, keepdims=True))
            p = jnp.exp(s - m_new)
            alpha = _unstack(jnp.exp(m_old - m_new), lo)
            l_new = alpha * b["l"] + _unstack(jnp.sum(p, axis=-1, keepdims=True), lo)
            return _unstack(m_new, lo), l_new, alpha * b["acc"] + _unstack(_bdot(p, v), lo)

        def visit_many(d, blocks):
            loaded = [load(d, *blk) for blk in blocks]
            done = [compute(b) for b in loaded]
            for b, (m_new, l_new, acc_new) in zip(loaded, done):
                m_ref[b["qrows"], :] = m_new
                l_ref[b["qrows"], :] = l_new
                acc_ref[b["qrows"], :] = acc_new

        _attn_blocks(S, visit_many)
        ob_ref[...] = (acc_ref[...] / l_ref[...]).astype(BF16)
        lse_ref[...] = m_ref[...] + jnp.log(l_ref[...])

    part = lambda t: pl.BlockSpec((S, 128), lambda p: (0, 3 * p + t))
    return pl.pallas_call(
        body, name=name, grid=(DIL_PAIRS,),
        in_specs=[part(0), part(1), part(2), pl.BlockSpec(memory_space=pl.ANY)],
        out_specs=[pl.BlockSpec((S, 128), lambda p: (0, GDN_WIDTH // 128 + p)), pl.BlockSpec((S, 128), lambda p: (0, p))],
        out_shape=[_sds(oab.shape, BF16), _sds((S, DIL_WIDTH))],
        scratch_shapes=[pltpu.VMEM((S, 128), F32)] * 3, input_output_aliases={3: 0},
        compiler_params=_params(("parallel",)),
    )(proj_b, proj_b, proj_b, oab)


def _attn_bwd(proj_b, oab, d_oab, lse, *, name):
    S = proj_b.shape[0]
    scale = DIL_DIM ** -0.5

    def body(q_ref, k_ref, v_ref, o_ref, do_ref, lse_ref, dqkv_ref, dq_ref, dk_ref, dv_ref, delta_ref):
        lane = lax.broadcasted_iota(jnp.int32, (BAND, 128), 1)
        lo = lane < DIL_DIM
        dq_ref[...] = jnp.zeros_like(dq_ref)
        dk_ref[...] = jnp.zeros_like(dk_ref)
        dv_ref[...] = jnp.zeros_like(dv_ref)
        prod = do_ref[...] * o_ref[...].astype(F32)
        lo_all = lax.broadcasted_iota(jnp.int32, (S, 128), 1) < DIL_DIM
        delta_ref[...] = jnp.where(lo_all, jnp.sum(jnp.where(lo_all, prod, 0.0), axis=-1, keepdims=True),
                                   jnp.sum(jnp.where(lo_all, 0.0, prod), axis=-1, keepdims=True))

        def load(d, r, n, first):
            nk = BAND if first else 2 * BAND
            qrows = _rows(r + n * (BAND * d), BAND, d)
            krows = _rows(r if first else r + (n - 1) * (BAND * d), nk, d)
            return dict(nk=nk, qrows=qrows, krows=krows, q=q_ref[qrows, :] * scale, k=k_ref[krows, :], v=v_ref[krows, :],
                        do=do_ref[qrows, :], delta=delta_ref[qrows, :], lse=lse_ref[qrows, :],
                        dq=dq_ref[qrows, :], dk=dk_ref[krows, :], dv=dv_ref[krows, :])

        def compute(b):
            q, k, v, do = b["q"], b["k"], b["v"], b["do"]
            qs, dos = _stack_heads(q, lo), _stack_heads(do, lo)
            p = jnp.where(_band_mask(b["nk"]), jnp.exp(_bdot_nt(qs, k) - _stack_cols(b["lse"])), 0.0)
            ds = p * (_bdot_nt(dos, v) - _stack_cols(b["delta"]))
            dq = b["dq"] + _unstack(_bdot(ds, k), lo) * scale
            return dq, b["dk"] + _bdot_tn(ds, qs), b["dv"] + _bdot_tn(p, dos)

        def visit_many(d, blocks):
            loaded = [load(d, *blk) for blk in blocks]
            done = [compute(b) for b in loaded]
            for b, (dq, dk, dv) in zip(loaded, done):
                dq_ref[b["qrows"], :] = dq
                dk_ref[b["krows"], :] = dk
                dv_ref[b["krows"], :] = dv

        _attn_blocks(S, visit_many, lanes=2)
        dqkv_ref[:, 0:128] = dq_ref[...].astype(BF16)
        dqkv_ref[:, 128:256] = dk_ref[...].astype(BF16)
        dqkv_ref[:, 256:384] = dv_ref[...].astype(BF16)

    half = lambda p: (0, GDN_WIDTH // 128 + p)
    part = lambda t: pl.BlockSpec((S, 128), lambda p: (0, 3 * p + t))
    return pl.pallas_call(
        body, name=name, grid=(DIL_PAIRS,),
        in_specs=[part(0), part(1), part(2), pl.BlockSpec((S, 128), half), pl.BlockSpec((S, 128), half),
                  pl.BlockSpec((S, 128), lambda p: (0, p))],
        out_specs=pl.BlockSpec((S, 384), lambda p: (0, p)), out_shape=_sds((S, 3 * DIL_WIDTH), BF16),
        scratch_shapes=[pltpu.VMEM((S, 128), F32)] * 4, compiler_params=_params(("parallel",)),
    )(proj_b, proj_b, proj_b, oab, d_oab, lse)


FF_SLAB = 2 * D_FF // N_DEV
FF_PAIRS = N_DEV // 2
ROWS16 = 16


def _taps(w, x, base, n):
    out = x[base:base + n] * w[0:1]
    for t in range(1, FFN_CONV):
        out = out + x[base + t:base + t + n] * w[t:t + 1]
    return out


def _ffn_fwd(h2, x1, w_up, conv_w, w_down, *, name, tm=512):
    S, D = h2.shape
    ni = S // tm
    per = tm // ROWS16

    def body(h_ref, hp_ref, x1_ref, wg_ref, wu_ref, cg_ref, cu_ref, wd_ref, x2_ref, ug_ref, uu_ref):
        i, j = pl.program_id(0), pl.program_id(1)
        hv = jnp.concatenate([hp_ref[...], h_ref[...]], axis=0)
        row = lax.broadcasted_iota(jnp.int32, (tm + ROWS16, 1), 0)
        keep = (i > 0) | (row >= ROWS16)

        def branch(w_ref, c_ref, u_ref):
            u = lax.dot_general(hv, w_ref[...], (((1,), (1,)), ((), ())), preferred_element_type=F32).astype(BF16)
            u_ref[...] = u[ROWS16:]
            return _taps(c_ref[...], jnp.where(keep, u.astype(F32), 0.0), ROWS16 - (FFN_CONV - 1), tm)

        gate = branch(wg_ref, cg_ref, ug_ref)
        up = branch(wu_ref, cu_ref, uu_ref)
        act = (gate * _sigmoid(gate) * up).astype(BF16)
        part = jnp.dot(act, wd_ref[...], preferred_element_type=F32)

        @pl.when(j == 0)
        def _():
            x2_ref[...] = x1_ref[...] + part

        @pl.when(j > 0)
        def _():
            x2_ref[...] += part

    rows = pl.BlockSpec((tm, D), lambda i, j: (i, 0))
    slab = lambda off: pl.BlockSpec((None, FF_SLAB, D), lambda i, j: (j + off, 0, 0))
    cslab = lambda off: pl.BlockSpec((None, FFN_CONV, FF_SLAB), lambda i, j: (j + off, 0, 0))
    uspec = pl.BlockSpec((None, tm, FF_SLAB), lambda i, j: (j, i, 0))
    return pl.pallas_call(
        body, name=name, grid=(ni, FF_PAIRS),
        in_specs=[rows, pl.BlockSpec((ROWS16, D), lambda i, j: (jnp.maximum(i * per - 1, 0), 0)), rows,
                  slab(0), slab(FF_PAIRS), cslab(0), cslab(FF_PAIRS), pl.BlockSpec((FF_SLAB, D), lambda i, j: (j, 0))],
        out_specs=[rows, uspec, uspec],
        out_shape=[_sds((S, D)), _sds((FF_PAIRS, S, FF_SLAB), BF16), _sds((FF_PAIRS, S, FF_SLAB), BF16)],
        compiler_params=_params(("parallel", "arbitrary")),
    )(h2, h2, x1, w_up, w_up, conv_w, conv_w, w_down)


def _ffn_bwd(dx2, h2, ug, uu, conv_w, w_down, *, name, tm=512):
    S, D = h2.shape
    ni = S // tm
    per = tm // ROWS16
    ext = tm + ROWS16

    def body(dx_ref, dxn_ref, h_ref, ug_ref, ugp_ref, ugn_ref, uu_ref, uup_ref, uun_ref, cg_ref, cu_ref, wd_ref,
             dug_ref, duu_ref, gd_ref, gg_ref, gu_ref, dcg_ref, dcu_ref, acc_d, acc_g, acc_u, acc_cg, acc_cu):
        i = pl.program_id(1)

        @pl.when(i == 0)
        def _():
            acc_d[...] = jnp.zeros_like(acc_d)
            acc_g[...] = jnp.zeros_like(acc_g)
            acc_u[...] = jnp.zeros_like(acc_u)
            acc_cg[...] = jnp.zeros_like(acc_cg)
            acc_cu[...] = jnp.zeros_like(acc_cu)

        dx = dx_ref[...]
        dxe = jnp.concatenate([dx, dxn_ref[...]], axis=0)
        row = lax.broadcasted_iota(jnp.int32, (ext, 1), 0)
        live = (i < ni - 1) | (row < tm)
        d_act = jnp.where(live, lax.dot_general(dxe, wd_ref[...], (((1,), (1,)), ((), ())), preferred_element_type=F32), 0.0)
        rowp = lax.broadcasted_iota(jnp.int32, (ext + ROWS16, 1), 0)
        keep = (i > 0) | (rowp >= ROWS16)

        def pre(cur, prev, nxt):
            return jnp.where(keep, jnp.concatenate([prev[...], cur[...], nxt[...]], axis=0).astype(F32), 0.0)

        uge, uue = pre(ug_ref, ugp_ref, ugn_ref), pre(uu_ref, uup_ref, uun_ref)
        cg, cu = cg_ref[...], cu_ref[...]
        base = ROWS16 - (FFN_CONV - 1)
        gate = _taps(cg, uge, base, ext)
        up = _taps(cu, uue, base, ext)
        sg = _sigmoid(gate)
        silu = gate * sg
        dgc = d_act * up * _dsilu(gate, sg)
        duc = d_act * silu

        def conv_t(w, dc):
            out = dc[FFN_CONV - 1:FFN_CONV - 1 + tm] * w[0:1]
            for t in range(1, FFN_CONV):
                out = out + dc[FFN_CONV - 1 - t:FFN_CONV - 1 - t + tm] * w[t:t + 1]
            return out.astype(BF16)

        du_g, du_u = conv_t(cg, dgc), conv_t(cu, duc)
        dug_ref[...] = du_g
        duu_ref[...] = du_u
        dcw = lambda dc, xe: jnp.concatenate(
            [jnp.sum(dc[0:tm] * xe[base + t:base + t + tm], axis=0, keepdims=True) for t in range(FFN_CONV)], axis=0)
        acc_cg[0:FFN_CONV, :] += dcw(dgc, uge)
        acc_cu[0:FFN_CONV, :] += dcw(duc, uue)
        tn = (((0,), (0,)), ((), ()))
        act = (silu[0:tm] * up[0:tm]).astype(BF16)
        acc_d[...] += lax.dot_general(act, dx, tn, preferred_element_type=F32)
        hv = h_ref[...]
        acc_g[...] += lax.dot_general(du_g, hv, tn, preferred_element_type=F32)
        acc_u[...] += lax.dot_general(du_u, hv, tn, preferred_element_type=F32)

        @pl.when(i == ni - 1)
        def _():
            gd_ref[...] = acc_d[...].astype(BF16)
            gg_ref[...] = acc_g[...].astype(BF16)
            gu_ref[...] = acc_u[...].astype(BF16)
            dcg_ref[...] = acc_cg[0:FFN_CONV, :]
            dcu_ref[...] = acc_cu[0:FFN_CONV, :]

    last16 = S // ROWS16 - 1
    rows = pl.BlockSpec((tm, D), lambda j, i: (i, 0))
    rows_next = pl.BlockSpec((ROWS16, D), lambda j, i: (jnp.minimum((i + 1) * per, last16), 0))
    u_cur = pl.BlockSpec((None, tm, FF_SLAB), lambda j, i: (j, i, 0))
    u_prev = pl.BlockSpec((None, ROWS16, FF_SLAB), lambda j, i: (j, jnp.maximum(i * per - 1, 0), 0))
    u_next = pl.BlockSpec((None, ROWS16, FF_SLAB), lambda j, i: (j, jnp.minimum((i + 1) * per, last16), 0))
    cslab = lambda off: pl.BlockSpec((None, FFN_CONV, FF_SLAB), lambda j, i: (j + off, 0, 0))
    wslab = pl.BlockSpec((None, FF_SLAB, D), lambda j, i: (j, 0, 0))
    dslab = pl.BlockSpec((None, FFN_CONV, FF_SLAB), lambda j, i: (j, 0, 0))
    return pl.pallas_call(
        body, name=name, grid=(FF_PAIRS, ni),
        in_specs=[rows, rows_next, rows, u_cur, u_prev, u_next, u_cur, u_prev, u_next, cslab(0), cslab(FF_PAIRS),
                  pl.BlockSpec((FF_SLAB, D), lambda j, i: (j, 0))],
        out_specs=[u_cur, u_cur, pl.BlockSpec((FF_SLAB, D), lambda j, i: (j, 0)), wslab, wslab, dslab, dslab],
        out_shape=[_sds((FF_PAIRS, S, FF_SLAB), BF16), _sds((FF_PAIRS, S, FF_SLAB), BF16), _sds((D_FF, D), BF16),
                   _sds((FF_PAIRS, FF_SLAB, D), BF16), _sds((FF_PAIRS, FF_SLAB, D), BF16),
                   _sds((FF_PAIRS, FFN_CONV, FF_SLAB)), _sds((FF_PAIRS, FFN_CONV, FF_SLAB))],
        scratch_shapes=[pltpu.VMEM((FF_SLAB, D), F32), pltpu.VMEM((FF_SLAB, D), F32), pltpu.VMEM((FF_SLAB, D), F32),
                        pltpu.VMEM((8, FF_SLAB), F32), pltpu.VMEM((8, FF_SLAB), F32)],
        compiler_params=_params(("parallel", "arbitrary")),
    )(dx2, dx2, h2, ug, ug, ug, uu, uu, uu, conv_w, conv_w, w_down)


def _mm_slabs(a, w, w_off, *, name, res=None, tm=1024, tn=1024):
    nk, S, _ = a.shape
    D = w.shape[2]
    has_res = res is not None

    def body(*refs):
        if has_res:
            a_ref, w_ref, r_ref, o_ref, acc_ref = refs
        else:
            a_ref, w_ref, o_ref, acc_ref = refs
        k = pl.program_id(2)
        part = jnp.dot(a_ref[...], w_ref[...], preferred_element_type=F32)

        @pl.when(k == 0)
        def _():
            acc_ref[...] = part

        @pl.when(k > 0)
        def _():
            acc_ref[...] += part

        @pl.when(k == nk - 1)
        def _():
            o_ref[...] = acc_ref[...] + r_ref[...] if has_res else acc_ref[...]

    o_spec = pl.BlockSpec((tm, tn), lambda i, j, k: (i, j))
    return pl.pallas_call(
        body, name=name, grid=(S // tm, D // tn, nk),
        in_specs=[pl.BlockSpec((None, tm, FF_SLAB), lambda i, j, k: (k, i, 0)),
                  pl.BlockSpec((None, FF_SLAB, tn), lambda i, j, k: (k + w_off, 0, j))] + ([o_spec] if has_res else []),
        out_specs=o_spec, out_shape=_sds((S, D)), scratch_shapes=[pltpu.VMEM((tm, tn), F32)],
        compiler_params=_params(("parallel", "parallel", "arbitrary")),
    )(*((a, w, res) if has_res else (a, w)))


def _local_step(x, tgt, norm1_w, w_a, w_z, w_b, conv_a, a_log, dt_bias, gnw, norm2_w, final_w, late_weights, emit):
    wgrad = functools.partial(_mm, ta=True, out_dtype=BF16)
    h1 = _rms_fwd(x, norm1_w, name="rms1_fwd")
    proj_a = _mm(h1, w_a, tb=True, name="proj_a", tn=A_COLS, tk=1024)
    proj_z = _mm(h1, w_z, tb=True, name="proj_z", tk=1024)
    proj_b = _mm(h1, w_b, tb=True, name="proj_b", tn=768, tk=1024)
    qn, kn, v, gcb, bb = _gdn_prep_fwd(proj_a, conv_a, a_log, dt_bias, name="gdn_prep_fwd")
    uv, wk, at, tmat = _gdn_chunk_fwd(qn, kn, v, gcb, bb, name="gdn_chunk_fwd")
    o, u, sp, oab = _gdn_scan_fwd(uv, wk, at, qn, kn, gcb, bb, proj_z, gnw, name="gdn_scan_fwd")
    oab, lse = _attn_fwd(proj_b, oab, name="attn_fwd")
    w_out, w_up, conv_f, w_down = late_weights(oab)
    x1 = _mm(oab, w_out, res=x, name="out_proj", tk=1024)
    h2 = _rms_fwd(x1, norm2_w, name="rms2_fwd")
    x2, ug, uu = _ffn_fwd(h2, x1, w_up, conv_f, w_down, name="ffn_fwd")
    dx2, dx2_b, d_final, loss = _loss_head(x2, final_w, tgt, name="loss_head")
    dug, duu, g_down, g_up_g, g_up_u, dcw_g, dcw_u = _ffn_bwd(dx2_b, h2, ug, uu, conv_f, w_down, name="ffn_bwd")
    token = emit("ffn", w_down=g_down, w_up=jnp.concatenate([g_up_g, g_up_u], axis=0),
                 conv_f=jnp.concatenate([dcw_g, dcw_u], axis=0))
    dh2 = _mm_slabs(dug, w_up, 0, name="ffn_up_dx_gate")
    dh2 = _mm_slabs(duu, w_up, FF_PAIRS, res=dh2, name="ffn_up_dx_up")
    dx1, d_norm2 = _rms_bwd(dh2, x1, _behind(norm2_w, token), dx2, name="rms2_bwd")
    d_oab = _mm(dx1, w_out, tb=True, name="out_proj_dx", tk=1024)
    gnw = _behind(gnw, emit("out", w_out=wgrad(oab, dx1, name="out_proj_dw")))
    dz, d_gnw, du, dwk, dat, dqd, dke, dgl = _gdn_scan_bwd(d_oab, o, proj_z, gnw, sp, u, wk, at, qn, kn, gcb, bb, name="gdn_scan_bwd")
    dqn, dkn, dv, dg, dbeta = _gdn_chunk_bwd(qn, kn, v, gcb, bb, tmat, uv, wk, du, dwk, dat, dqd, dke, dgl, name="gdn_chunk_bwd")
    dc, dba, d_small = _gdn_prep_bwd(dqn, dkn, dv, dg, dbeta, proj_a, conv_a, a_log, dt_bias, name="gdn_prep_bwd")
    d_pa, d_conv_a = _gdn_conv_bwd(dc, dba, proj_a, conv_a, name="gdn_conv_bwd")
    d_pb = _attn_bwd(proj_b, oab, d_oab, lse, name="attn_bwd")
    g_a = wgrad(d_pa, h1, name="proj_a_dw", tm=A_COLS)
    g_z = wgrad(dz, h1, name="proj_z_dw")
    g_b = wgrad(d_pb, h1, name="proj_b_dw", tm=768)
    w_z = _behind(w_z, emit("in", w_a=g_a, w_z=g_z, w_b=g_b, conv_a=d_conv_a))
    dh1 = _mm(dz, w_z, name="proj_z_dx")
    dh1 = _mm(d_pa, w_a, res=dh1, name="proj_a_dx", tk=A_COLS)
    dh1 = _mm(d_pb, w_b, res=dh1, name="proj_b_dx", tk=1536)
    grad_x, d_norm1 = _rms_bwd(dh1, x, norm1_w, dx1, name="rms1_bwd")
    small = dict(norm1=d_norm1, small=d_small, gnw=d_gnw, norm2=d_norm2, final=d_final)
    return loss, grad_x, small


_O1 = 3 * GDN_WIDTH
_O2 = _O1 + GDN_WIDTH
_O3 = _O2 + 2 * GDN_HEADS


def _split_w_in(w_t):
    d = w_t.shape[1]
    pad = jnp.zeros((A_COLS - _O1 - 2 * GDN_HEADS, d), w_t.dtype)
    w_a = jnp.concatenate([w_t[:_O1], w_t[_O2:_O3], pad], axis=0)
    w_b = w_t[_O3:].reshape(3, DIL_PAIRS, 128, d).transpose(1, 0, 2, 3).reshape(3 * DIL_WIDTH, d)
    return w_a, w_t[_O1:_O2], w_b


def _merge_g_in(g_a, g_z, g_b):
    d = g_a.shape[1]
    g_b = g_b.reshape(DIL_PAIRS, 3, 128, d).transpose(1, 0, 2, 3).reshape(3 * DIL_WIDTH, d)
    return jnp.concatenate([g_a[:_O1], g_z, g_a[_O1:_O1 + 2 * GDN_HEADS], g_b], axis=0)


MESH = pl.DeviceIdType.MESH
ANY = pl.BlockSpec(memory_space=pl.ANY)


def _position():
    return lax.axis_index("x"), lax.axis_index("y"), lax.axis_index("c")


def _slot(p):
    return 4 * p[0] + 2 * p[1] + p[2]


def _all_gather(blocks, *, name):
    n = len(blocks)

    def body(*refs):
        ins, outs = refs[:n], refs[n:2 * n]
        send_sems, recv_sems, local_sems = refs[2 * n:]
        x, y, c = _position()
        me, sibling = (x, y, c), (x, y, 1 - c)
        chips = [(1 - x, y), (x, 1 - y), (1 - x, 1 - y)]

        def copy(a, k, block, to, src=None):
            dst = outs[a].at[_slot(block)]
            return pltpu.make_async_remote_copy(
                src_ref=dst if src is None else src, dst_ref=dst, send_sem=send_sems.at[a, k], recv_sem=recv_sems.at[a, k],
                device_id=to, device_id_type=MESH)

        mine = [pltpu.make_async_copy(ins[a], outs[a].at[_slot(me)], local_sems.at[a]) for a in range(n)]
        for cp in mine:
            cp.start()
        first = []
        for a in range(n):
            first.append(copy(a, 0, me, sibling, src=ins[a]))
            first += [copy(a, 1 + j, me, (*chip, c), src=ins[a]) for j, chip in enumerate(chips)]
        for cp in first:
            cp.start()
        passed = []
        for j, chip in enumerate(chips):
            for a in range(n):
                copy(a, 1 + j, (*chip, c), me).wait_recv()
                fwd = copy(a, 4 + j, (*chip, c), sibling)
                fwd.start()
                passed.append(fwd)
        for a in range(n):
            copy(a, 0, sibling, me).wait_recv()
            for j, chip in enumerate(chips):
                copy(a, 4 + j, (*chip, 1 - c), me).wait_recv()
        for cp in first + passed:
            cp.wait_send()
        for cp in mine:
            cp.wait()

    return pl.pallas_call(
        body, name=name, in_specs=[ANY] * n, out_specs=[ANY] * n,
        out_shape=[_sds((N_DEV,) + b.shape, b.dtype) for b in blocks],
        scratch_shapes=[pltpu.SemaphoreType.DMA((n, 7)), pltpu.SemaphoreType.DMA((n, 7)), pltpu.SemaphoreType.DMA((n,))],
    )(*blocks)


def _gather_direct(block, *, name):
    def body(in_ref, out_ref, send_sems, recv_sems, local_sem):
        x, y, c = _position()
        me = _slot((x, y, c))
        mine = pltpu.make_async_copy(in_ref, out_ref.at[me], local_sem)
        mine.start()
        copies = [pltpu.make_async_remote_copy(
            src_ref=in_ref, dst_ref=out_ref.at[me], send_sem=send_sems.at[k - 1], recv_sem=recv_sems.at[k - 1],
            device_id=_peer_of(k, x, y, c), device_id_type=MESH) for k in range(1, N_DEV)]
        for cp in copies:
            cp.start()
        for cp in copies:
            cp.wait()
        mine.wait()

    return pl.pallas_call(
        body, name=name, in_specs=[pl.BlockSpec(memory_space=pltpu.VMEM)], out_specs=pl.BlockSpec(memory_space=pltpu.VMEM),
        out_shape=_sds((N_DEV,) + block.shape, block.dtype),
        scratch_shapes=[pltpu.SemaphoreType.DMA((N_DEV - 1,)), pltpu.SemaphoreType.DMA((N_DEV - 1,)), pltpu.SemaphoreType.DMA],
    )(block)


HBM = pl.BlockSpec(memory_space=pltpu.HBM)
SEM = pl.BlockSpec(memory_space=pltpu.SEMAPHORE)
EFFECT = pltpu.SideEffectType.DATAFLOW_SIDE_EFFECTING


def _peer_of(k, x, y, c):
    return (1 - x if k & 4 else x, 1 - y if k & 2 else y, 1 - c if k & 1 else c)


def _flight(a, k):
    return a * (N_DEV - 1) + k - 1


def _exchange_start(arrays, *, name, broadcast=False):
    n = len(arrays)

    def body(*refs):
        ins, lands = refs[:n], refs[n:2 * n]
        send_sems, recv_sems = refs[2 * n:2 * n + 2]
        token = refs[-1]
        x, y, c = _position()
        me = _slot((x, y, c))
        for k in range(1, N_DEV):
            peer = _peer_of(k, x, y, c)
            for a in range(n):
                pltpu.make_async_remote_copy(
                    src_ref=ins[a] if broadcast else ins[a].at[_slot(peer)], dst_ref=lands[a].at[me],
                    send_sem=send_sems.at[_flight(a, k)], recv_sem=recv_sems.at[_flight(a, k)],
                    device_id=peer, device_id_type=MESH).start()
        token[...] = jnp.zeros_like(token)

    land_shapes = [((N_DEV,) + s.shape) if broadcast else s.shape for s in arrays]
    lands = [pltpu.with_memory_space_constraint(lax.empty(shp, s.dtype), pltpu.HBM) for shp, s in zip(land_shapes, arrays)]
    srcs = [pltpu.with_memory_space_constraint(s, pltpu.HBM) for s in arrays]
    outs = pl.pallas_call(
        body, name=name, in_specs=[HBM] * (2 * n),
        out_specs=[SEM, SEM] + [HBM] * (2 * n) + [pl.BlockSpec(memory_space=pltpu.VMEM)],
        out_shape=[pltpu.SemaphoreType.DMA((n * (N_DEV - 1),)), pltpu.SemaphoreType.DMA((n * (N_DEV - 1),))]
        + [pltpu.HBM(s.shape, s.dtype) for s in arrays] + [pltpu.HBM(shp, s.dtype) for shp, s in zip(land_shapes, arrays)]
        + [_sds((8, 128))],
        input_output_aliases={i: 2 + i for i in range(2 * n)},
        compiler_params=pltpu.CompilerParams(has_side_effects=EFFECT),
    )(*srcs, *lands)
    return outs[0], outs[1], outs[2:2 + n], outs[2 + n:2 + 2 * n], outs[-1]


def _exchange_wait(send_sems, recv_sems, srcs, lands, after, *, name, broadcast=False):
    n = len(srcs)

    def body(*refs):
        ins, lnd = refs[:n], refs[n:2 * n]
        send_ref, recv_ref = refs[2 * n:2 * n + 2]
        x, y, c = _position()
        for k in range(1, N_DEV):
            for a in range(n):
                cp = pltpu.make_async_remote_copy(
                    src_ref=ins[a] if broadcast else ins[a].at[0], dst_ref=lnd[a].at[0], send_sem=send_ref.at[_flight(a, k)],
                    recv_sem=recv_ref.at[_flight(a, k)], device_id=_peer_of(k, x, y, c), device_id_type=MESH)
                cp.wait_send()
                cp.wait_recv()

    outs = pl.pallas_call(
        body, name=name, in_specs=[HBM] * (2 * n) + [SEM, SEM, ANY], out_specs=[HBM] * (2 * n),
        out_shape=[pltpu.HBM(s.shape, s.dtype) for s in srcs] + [pltpu.HBM(s.shape, s.dtype) for s in lands],
        input_output_aliases={i: i for i in range(2 * n)},
        compiler_params=pltpu.CompilerParams(has_side_effects=EFFECT),
    )(*srcs, *lands, send_sems, recv_sems, after)
    return outs[:n], outs[n:]


def _with_own(landed, srcs, me, broadcast=False):
    own = srcs if broadcast else [lax.dynamic_index_in_dim(s, me, 0, keepdims=False) for s in srcs]
    return [lax.dynamic_update_index_in_dim(l, o, me, 0) for l, o in zip(landed, own)]


def _behind(x, token):
    return x if token is None else x + token[0, 0].astype(x.dtype)


def _adamw(parts, w, m, v, *, name, tr=None, tc=None):
    R, C = w.shape
    tr = R if tr is None else tr
    tc = C if tc is None else tc
    assert R % tr == 0 and C % tc == 0
    c1 = 1.0 - ADAM_B1 ** ADAM_STEP
    c2 = 1.0 - ADAM_B2 ** ADAM_STEP

    def body(p_ref, w_ref, m_ref, v_ref, g_ref, d_ref, nm_ref, nv_ref):
        g = p_ref[0].astype(F32)
        for s in range(1, N_DEV):
            g = g + p_ref[s].astype(F32)
        nm = ADAM_B1 * m_ref[...] + (1.0 - ADAM_B1) * g
        nv = ADAM_B2 * v_ref[...] + (1.0 - ADAM_B2) * (g * g)
        g_ref[...] = g
        nm_ref[...] = nm
        nv_ref[...] = nv
        d_ref[...] = -ADAM_LR * ((nm / c1) / (jnp.sqrt(nv / c2) + ADAM_EPS) + ADAM_WD * w_ref[...])

    blk = pl.BlockSpec((tr, tc), lambda i, j: (i, j))
    return pl.pallas_call(
        body, name=name, grid=(R // tr, C // tc),
        in_specs=[pl.BlockSpec((N_DEV, tr, tc), lambda i, j: (0, i, j)), blk, blk, blk],
        out_specs=[blk] * 4, out_shape=[_sds((R, C))] * 4, compiler_params=_params(("parallel", "parallel")),
    )(parts, w, m, v)


_SMALL_ROWS = 8


def _pack_small(norm1, norm2, final, gnw, a_log, dt_bias, loss=None):
    loss = jnp.zeros((1, 128), F32) if loss is None else loss
    row3 = jnp.concatenate([gnw, a_log, dt_bias, jnp.zeros((1, 128 - 2 * GDN_HEADS), F32), loss,
                            jnp.zeros((1, D_MODEL - 3 * 128), F32)], axis=1)
    return jnp.concatenate([norm1, norm2, final, row3, jnp.zeros((_SMALL_ROWS - 4, D_MODEL), F32)], axis=0)


def _unpack_small(p):
    return (p[0:1], p[1:2], p[2], p[3:4, 0:128], p[3:4, 128:128 + GDN_HEADS], p[3:4, 128 + GDN_HEADS:128 + 2 * GDN_HEADS])


def _slabs_by_cols(g):
    r = g.shape[0]
    return g.reshape(r, N_DEV, -1).transpose(1, 0, 2)


def _cols_from_slabs(s):
    return s.transpose(1, 0, 2).reshape(s.shape[1], -1)


def kernel(x, norm1_w, w_in, conv_qkv_w, a_log, dt_bias, gdn_norm_w, w_out, norm2_w, w_up, ffn_conv_w, w_down, final_norm_w, loss_target, m_norm1_w, m_w_in, m_conv_qkv_w, m_a_log, m_dt_bias, m_gdn_norm_w, m_w_out, m_norm2_w, m_w_up, m_ffn_conv_w, m_w_down, m_final_norm_w, v_norm1_w, v_w_in, v_conv_qkv_w, v_a_log, v_dt_bias, v_gdn_norm_w, v_w_out, v_norm2_w, v_w_up, v_ffn_conv_w, v_w_down, v_final_norm_w):
    bf = lambda a: a.astype(BF16)
    me = _slot(_position())
    t_in = lambda a: a[0].T
    gw_in, g_conv_a = _all_gather([bf(t_in(w_in)), conv_qkv_w[0]], name="gather_w_in")
    w_a, w_z, w_b = _split_w_in(gw_in.reshape(-1, D_MODEL))
    late_src, _ = lax.optimization_barrier(([bf(w_out[0]), bf(t_in(w_up)), bf(w_down[0]), ffn_conv_w[0]], gw_in))
    l_send, l_recv, l_srcs, l_lands, l_token = _exchange_start(late_src, name="weights_start", broadcast=True)

    def late_weights(after):
        srcs, landed = _exchange_wait(l_send, l_recv, l_srcs, l_lands, after, name="weights_wait", broadcast=True)
        gw_out, gw_up, gw_down, g_conv_f = _with_own(landed, srcs, me, broadcast=True)
        return gw_out.reshape(D_MODEL, D_MODEL), gw_up, g_conv_f, gw_down.reshape(D_FF, D_MODEL)

    flights = {}

    def emit(group, **grads):
        if group == "in":
            slabs = dict(w_in=_merge_g_in(grads["w_a"], grads["w_z"], grads["w_b"]).reshape(N_DEV, -1, D_MODEL),
                         conv_a=_slabs_by_cols(grads["conv_a"]))
        elif group == "ffn":
            slabs = dict(w_down=grads["w_down"].reshape(N_DEV, -1, D_MODEL), w_up=grads["w_up"], conv_f=grads["conv_f"])
        else:
            slabs = {k: v.reshape(N_DEV, -1, D_MODEL) for k, v in grads.items()}
        names = list(slabs)
        *flight, token = _exchange_start([slabs[k] for k in names], name="grads_start_" + group)
        flights[group] = (names, flight)
        return token

    loss, grad_x, g = _local_step(
        x[0], loss_target[0], _behind(norm1_w, l_token), w_a, w_z, w_b, _cols_from_slabs(g_conv_a), a_log, dt_bias,
        gdn_norm_w, norm2_w, final_norm_w[None], late_weights, emit)
    small_all = _gather_direct(
        _pack_small(g["norm1"], g["norm2"], g["final"], g["gnw"], g["small"][:, 0:GDN_HEADS],
                    g["small"][:, GDN_HEADS:2 * GDN_HEADS], loss), name="gather_small")
    got = {}
    for group in ("ffn", "out", "in"):
        names, (send_sems, recv_sems, srcs, lands) = flights[group]
        srcs, landed = _exchange_wait(send_sems, recv_sems, srcs, lands, small_all, name="grads_wait_" + group)
        got.update(zip(names, _with_own(landed, srcs, me)))
    o_in = [o.T for o in _adamw(got["w_in"], t_in(w_in), t_in(m_w_in), t_in(v_w_in), name="adamw_w_in", tc=256)]
    o_out = _adamw(got["w_out"], w_out[0], m_w_out[0], v_w_out[0], name="adamw_w_out")
    o_up = [o.T for o in _adamw(got["w_up"], t_in(w_up), t_in(m_w_up), t_in(v_w_up), name="adamw_w_up", tr=176)]
    o_down = _adamw(got["w_down"], w_down[0], m_w_down[0], v_w_down[0], name="adamw_w_down", tr=176)
    o_ca = _adamw(got["conv_a"], conv_qkv_w[0], m_conv_qkv_w[0], v_conv_qkv_w[0], name="adamw_conv_a")
    o_cf = _adamw(got["conv_f"], ffn_conv_w[0], m_ffn_conv_w[0], v_ffn_conv_w[0], name="adamw_conv_f")
    o_small = _adamw(
        small_all, _pack_small(norm1_w, norm2_w, final_norm_w[None], gdn_norm_w, a_log, dt_bias),
        _pack_small(m_norm1_w, m_norm2_w, m_final_norm_w[None], m_gdn_norm_w, m_a_log, m_dt_bias),
        _pack_small(v_norm1_w, v_norm2_w, v_final_norm_w[None], v_gdn_norm_w, v_a_log, v_dt_bias), name="adamw_small")
    total_loss = o_small[0][3, 256]
    outs = [total_loss, grad_x[None]]
    for k in range(4):
        n1, n2, fin, gn, al, dt = _unpack_small(o_small[k])
        outs += [n1, o_in[k][None], o_ca[k][None], al, dt, gn, o_out[k][None], n2, o_up[k][None], o_cf[k][None], o_down[k][None], fin]
    return tuple(outs)
```

```python
import functools

import jax
import jax.numpy as jnp
from jax import lax
from jax.experimental import pallas as pl
from jax.experimental.pallas import tpu as pltpu

F32 = jnp.float32
BF16 = jnp.bfloat16

N_DEV = 8
D_MODEL = 1024
GDN_HEADS = 4
GDN_DIM = 128
GDN_WIDTH = GDN_HEADS * GDN_DIM
GDN_CONV = 4
CHUNK = 64
CHUNKS_PER_STEP = 2
DIL_HEADS = 8
DIL_DIM = 64
DIL_WIDTH = DIL_HEADS * DIL_DIM
DIL_PAIRS = DIL_HEADS // 2
DILATIONS = (1, 4, 16)
BAND = 128
D_FF = 2816
FFN_CONV = 3
EPS = 1e-6
A_COLS = 3 * GDN_WIDTH + 128
HALO = 8

ADAM_LR = 0.001
ADAM_B1 = 0.9
ADAM_B2 = 0.999
ADAM_EPS = 1e-08
ADAM_WD = 0.01
ADAM_STEP = 10

VMEM_LIMIT_BYTES = 56 * 1024 * 1024
NEG_BIG = -1e30


def _params(sem=None):
    return pltpu.CompilerParams(dimension_semantics=sem, vmem_limit_bytes=VMEM_LIMIT_BYTES)


def _sds(shape, dtype=F32):
    return jax.ShapeDtypeStruct(shape, dtype)


def _bdot(a, b):
    return jnp.dot(a.astype(BF16), b.astype(BF16), preferred_element_type=F32)


def _bdot_nt(a, b):
    return lax.dot_general(a.astype(BF16), b.astype(BF16), (((1,), (1,)), ((), ())), preferred_element_type=F32)


def _bdot_tn(a, b):
    return lax.dot_general(a.astype(BF16), b.astype(BF16), (((0,), (0,)), ((), ())), preferred_element_type=F32)


def _split(a):
    hi = a.astype(BF16)
    lo = (a - hi.astype(F32)).astype(BF16)
    return hi, lo


def _dot3(a, b, dims):
    ah, al = _split(a)
    bh, bl = _split(b)
    d = functools.partial(lax.dot_general, dimension_numbers=(dims, ((), ())), preferred_element_type=F32)
    return d(ah, bh) + (d(al, bh) + d(ah, bl))


def _exact_tri_dot(tri, g):
    g1 = g.astype(BF16)
    r1 = g - g1.astype(F32)
    g2 = r1.astype(BF16)
    g3 = (r1 - g2.astype(F32)).astype(BF16)
    t = tri.astype(BF16)
    d = functools.partial(jnp.dot, preferred_element_type=F32)
    return d(t, g1) + (d(t, g2) + d(t, g3))


def _sigmoid(x):
    return 1.0 / (1.0 + jnp.exp(-x))


def _dsilu(x, sg):
    return sg * (1.0 + x * (1.0 - sg))


def _mm(a, b, *, name, ta=False, tb=False, res=None, out_dtype=F32, tm=512, tn=512, tk=512):
    if ta:
        K, M = a.shape
    else:
        M, K = a.shape
    if tb:
        N, Kb = b.shape
    else:
        Kb, N = b.shape
    assert K == Kb, (a.shape, b.shape)
    tm, tn, tk = min(tm, M), min(tn, N), min(tk, K)
    assert M % tm == 0 and N % tn == 0 and K % tk == 0, (name, M, N, K, tm, tn, tk)
    nk = K // tk
    dims = (((0 if ta else 1,), (1 if tb else 0,)), ((), ()))
    has_res = res is not None

    def body(*refs):
        if has_res:
            a_ref, b_ref, r_ref, o_ref, acc_ref = refs
        else:
            a_ref, b_ref, o_ref, acc_ref = refs
        k = pl.program_id(2)
        part = lax.dot_general(a_ref[...].astype(BF16), b_ref[...].astype(BF16), dims, preferred_element_type=F32)

        @pl.when(k == 0)
        def _():
            acc_ref[...] = part

        @pl.when(k > 0)
        def _():
            acc_ref[...] += part

        @pl.when(k == nk - 1)
        def _():
            r = acc_ref[...]
            if has_res:
                r = r + r_ref[...]
            o_ref[...] = r.astype(out_dtype)

    a_spec = pl.BlockSpec((tk, tm), lambda i, j, k: (k, i)) if ta else pl.BlockSpec((tm, tk), lambda i, j, k: (i, k))
    b_spec = pl.BlockSpec((tn, tk), lambda i, j, k: (j, k)) if tb else pl.BlockSpec((tk, tn), lambda i, j, k: (k, j))
    o_spec = pl.BlockSpec((tm, tn), lambda i, j, k: (i, j))
    in_specs = [a_spec, b_spec] + ([o_spec] if has_res else [])
    args = (a, b) + ((res,) if has_res else ())
    return pl.pallas_call(
        body, name=name, grid=(M // tm, N // tn, nk), in_specs=in_specs, out_specs=o_spec,
        out_shape=_sds((M, N), out_dtype), scratch_shapes=[pltpu.VMEM((tm, tn), F32)],
        compiler_params=_params(("parallel", "parallel", "arbitrary")),
    )(*args)


def _rms_fwd(x, w, *, name, tm=256):
    S, D = x.shape

    def body(x_ref, w_ref, h_ref):
        xv = x_ref[...]
        r = lax.rsqrt(jnp.mean(xv * xv, axis=-1, keepdims=True) + EPS)
        h_ref[...] = (xv * r * w_ref[...]).astype(BF16)

    return pl.pallas_call(
        body, name=name, grid=(S // tm,),
        in_specs=[pl.BlockSpec((tm, D), lambda i: (i, 0)), pl.BlockSpec((1, D), lambda i: (0, 0))],
        out_specs=pl.BlockSpec((tm, D), lambda i: (i, 0)), out_shape=_sds((S, D), BF16),
        compiler_params=_params(("parallel",)),
    )(x, w)


def _rms_bwd(dh, x, w, res, *, name, tm=256):
    S, D = x.shape

    def body(dh_ref, x_ref, w_ref, res_ref, dx_ref, dw_ref):
        i = pl.program_id(0)
        xv = x_ref[...]
        g = dh_ref[...]
        r = lax.rsqrt(jnp.mean(xv * xv, axis=-1, keepdims=True) + EPS)
        xh = xv * r
        gw = g * w_ref[...]
        dx_ref[...] = res_ref[...] + r * (gw - xh * jnp.mean(gw * xh, axis=-1, keepdims=True))
        part = jnp.sum(g * xh, axis=0, keepdims=True)

        @pl.when(i == 0)
        def _():
            dw_ref[...] = part

        @pl.when(i > 0)
        def _():
            dw_ref[...] += part

    row = pl.BlockSpec((tm, D), lambda i: (i, 0))
    one = pl.BlockSpec((1, D), lambda i: (0, 0))
    return pl.pallas_call(
        body, name=name, grid=(S // tm,), in_specs=[row, row, one, row], out_specs=[row, one],
        out_shape=[_sds((S, D)), _sds((1, D))], compiler_params=_params(("arbitrary",)),
    )(dh, x, w, res)


def _loss_head(x2, w, tgt, *, name, tm=256):
    S, D = x2.shape

    def body(x_ref, w_ref, t_ref, dx_ref, dxb_ref, dw_ref, loss_ref):
        i = pl.program_id(0)
        xv = x_ref[...]
        wv = w_ref[...]
        r = lax.rsqrt(jnp.mean(xv * xv, axis=-1, keepdims=True) + EPS)
        xh = xv * r
        err = xh * wv - t_ref[...]
        lrow = jnp.sum(err * err, axis=-1, keepdims=True)
        lsum = jnp.sum(lrow, axis=0, keepdims=True) * (0.5 / D)
        g = err * (1.0 / D)
        gw = g * wv
        dx = r * (gw - xh * jnp.mean(gw * xh, axis=-1, keepdims=True))
        dx_ref[...] = dx
        dxb_ref[...] = dx.astype(BF16)
        part = jnp.sum(g * xh, axis=0, keepdims=True)
        lpart = jnp.broadcast_to(lsum, (1, 128))

        @pl.when(i == 0)
        def _():
            dw_ref[...] = part
            loss_ref[...] = lpart

        @pl.when(i > 0)
        def _():
            dw_ref[...] += part
            loss_ref[...] += lpart

    row = pl.BlockSpec((tm, D), lambda i: (i, 0))
    one = pl.BlockSpec((1, D), lambda i: (0, 0))
    return pl.pallas_call(
        body, name=name, grid=(S // tm,), in_specs=[row, one, row],
        out_specs=[row, row, one, pl.BlockSpec((1, 128), lambda i: (0, 0))],
        out_shape=[_sds((S, D)), _sds((S, D), BF16), _sds((1, D)), _sds((1, 128))], compiler_params=_params(("arbitrary",)),
    )(x2, w, tgt)


def _conv_rows(prev, cur, w, taps):
    n = cur.shape[0]
    xs = jnp.concatenate([prev, cur], axis=0)
    base = HALO - (taps - 1)
    out = xs[base:base + n] * w[0:1]
    for i in range(1, taps):
        out = out + xs[base + i:base + i + n] * w[i:i + 1]
    return out


def _conv_rows_bwd(cur_d, next_d, prev_x, cur_x, w, taps):
    n = cur_d.shape[0]
    ds = jnp.concatenate([cur_d, next_d], axis=0)
    dx = ds[taps - 1:taps - 1 + n] * w[0:1]
    for i in range(1, taps):
        dx = dx + ds[taps - 1 - i:taps - 1 - i + n] * w[i:i + 1]
    xs = jnp.concatenate([prev_x, cur_x], axis=0)
    base = HALO - (taps - 1)
    dws = [jnp.sum(cur_d * xs[base + i:base + i + n], axis=0, keepdims=True) for i in range(taps)]
    return dx, jnp.concatenate(dws, axis=0)


def _halo_specs(tm, width, col, nblk):
    per = tm // HALO
    prev = pl.BlockSpec((HALO, width), lambda i, *_: (jnp.maximum(i * per - 1, 0), col))
    nxt = pl.BlockSpec((HALO, width), lambda i, *_: (jnp.minimum((i + 1) * per, nblk * per - 1), col))
    return prev, nxt


def _softplus(x):
    return jnp.maximum(x, 0.0) + jnp.log1p(jnp.exp(-jnp.abs(x)))


def _chunk_tri(tm, upper=False):
    r = lax.broadcasted_iota(jnp.int32, (tm, tm), 0)
    c = lax.broadcasted_iota(jnp.int32, (tm, tm), 1)
    same = lax.div(r, CHUNK) == lax.div(c, CHUNK)
    order = (c >= r) if upper else (c <= r)
    return jnp.where(same & order, 1.0, 0.0)


def _gdn_prep_fwd(proj_a, conv_w, a_log, dt_bias, *, name, tm=256):
    S = proj_a.shape[0]
    nblk = S // tm
    W3 = 3 * GDN_WIDTH

    def body(cur_ref, prev_ref, ba_ref, cw_ref, al_ref, dt_ref, qn_ref, kn_ref, v_ref, gcb_ref, bb_ref):
        i = pl.program_id(0)
        prev = jnp.where(i > 0, prev_ref[...], 0.0)
        c = _conv_rows(prev, cur_ref[...], cw_ref[...], GDN_CONV)
        a = c * _sigmoid(c)
        ba = ba_ref[...]
        lane = lax.broadcasted_iota(jnp.int32, (tm, 128), 1)
        g4 = jnp.zeros((tm, 128), F32)
        for h in range(GDN_HEADS):
            sl = slice(GDN_DIM * h, GDN_DIM * (h + 1))
            qh = a[:, GDN_DIM * h:GDN_DIM * (h + 1)]
            kh = a[:, GDN_WIDTH + GDN_DIM * h:GDN_WIDTH + GDN_DIM * (h + 1)]
            qn_ref[:, sl] = qh * (lax.rsqrt(jnp.sum(qh * qh, axis=-1, keepdims=True) + EPS) * (GDN_DIM ** -0.5))
            kn_ref[:, sl] = kh * lax.rsqrt(jnp.sum(kh * kh, axis=-1, keepdims=True) + EPS)
            beta = _sigmoid(ba[:, h:h + 1])
            bb_ref[:, sl] = jnp.broadcast_to(beta, (tm, GDN_DIM))
            g = -jnp.exp(al_ref[0:1, h:h + 1]) * _softplus(ba[:, GDN_HEADS + h:GDN_HEADS + h + 1] + dt_ref[0:1, h:h + 1])
            g4 = jnp.where(lane == h, g, g4)
        v_ref[...] = a[:, 2 * GDN_WIDTH:]
        gc = _exact_tri_dot(_chunk_tri(tm), g4)
        for h in range(GDN_HEADS):
            gcb_ref[:, GDN_DIM * h:GDN_DIM * (h + 1)] = jnp.broadcast_to(gc[:, h:h + 1], (tm, GDN_DIM))

    prev_spec, _ = _halo_specs(tm, W3, 0, nblk)
    row = pl.BlockSpec((tm, GDN_WIDTH), lambda i: (i, 0))
    small = lambda a: pl.BlockSpec(a.shape, lambda i: (0, 0))
    return pl.pallas_call(
        body, name=name, grid=(nblk,),
        in_specs=[pl.BlockSpec((tm, W3), lambda i: (i, 0)), prev_spec,
                  pl.BlockSpec((tm, 128), lambda i: (i, W3 // 128)), small(conv_w), small(a_log), small(dt_bias)],
        out_specs=[row] * 5, out_shape=[_sds((S, GDN_WIDTH))] * 5, compiler_params=_params(("parallel",)),
    )(proj_a, proj_a, proj_a, conv_w, a_log, dt_bias)


GDN_STACK = GDN_HEADS * CHUNK


def _stack(ref, rows):
    return jnp.concatenate([ref[rows, GDN_DIM * h:GDN_DIM * (h + 1)] for h in range(GDN_HEADS)], axis=0)


def _unstack_to(ref, rows, x):
    for h in range(GDN_HEADS):
        ref[rows, GDN_DIM * h:GDN_DIM * (h + 1)] = x[CHUNK * h:CHUNK * (h + 1)].astype(ref.dtype)


def _stack_masks():
    r = lax.broadcasted_iota(jnp.int32, (GDN_STACK, GDN_STACK), 0)
    c = lax.broadcasted_iota(jnp.int32, (GDN_STACK, GDN_STACK), 1)
    same = (r & -CHUNK) == (c & -CHUNK)
    return same & (r >= c), same & (r > c), r == c


def _stack_decay(gs, bs, incl):
    g2 = jnp.concatenate([gs, gs], axis=1)
    diff = g2 - g2.T
    dec = jnp.where(incl, jnp.exp(jnp.where(incl, diff, 0.0)), 0.0)
    return dec, jnp.concatenate([bs, bs], axis=1).T


def _head_mask():
    r = lax.broadcasted_iota(jnp.int32, (GDN_STACK, GDN_WIDTH), 0)
    c = lax.broadcasted_iota(jnp.int32, (GDN_STACK, GDN_WIDTH), 1)
    return (r & -CHUNK) * (GDN_DIM // CHUNK) == (c & -GDN_DIM)


def _head_spread(x):
    return jnp.where(_head_mask(), jnp.concatenate([x] * GDN_HEADS, axis=1), 0.0)


def _head_diag(x):
    xm = jnp.where(_head_mask(), x, 0.0)
    out = xm[:, 0:GDN_DIM]
    for h in range(1, GDN_HEADS):
        out = out + xm[:, GDN_DIM * h:GDN_DIM * (h + 1)]
    return out


def _last_rows(gs, n):
    return jnp.concatenate([jnp.broadcast_to(gs[CHUNK * (h + 1) - 1:CHUNK * (h + 1)], (n, GDN_DIM)) for h in range(GDN_HEADS)], axis=0)


def _gdn_chunk_fwd(qn, kn, v, gcb, bb, *, name):
    S = qn.shape[0]

    def body(qn_ref, kn_ref, v_ref, gcb_ref, bb_ref, uv_ref, wk_ref, at_ref, t_ref):
        incl, strict, diag = _stack_masks()
        for c in range(CHUNKS_PER_STEP):
            rows = slice(CHUNK * c, CHUNK * (c + 1))
            srows = slice(GDN_STACK * c, GDN_STACK * (c + 1))
            q, k, vv, gs, bs = [_stack(r, rows) for r in (qn_ref, kn_ref, v_ref, gcb_ref, bb_ref)]
            dec, bt = _stack_decay(gs, bs, incl)
            p = -jnp.where(strict, dec * _bdot_nt(k, k) * bt, 0.0)
            t = jnp.where(diag, 1.0, 0.0) + p
            for _ in range(5):
                p = _bdot(p, p)
                t = t + _bdot(t, p)
            sol = _dot3(t, jnp.concatenate([vv, jnp.exp(gs) * k], axis=1), ((1,), (0,)))
            _unstack_to(uv_ref, rows, sol[:, :GDN_DIM])
            _unstack_to(wk_ref, rows, sol[:, GDN_DIM:])
            at_ref[srows, :] = dec * _bdot_nt(q, k) * bt
            t_ref[srows, :] = t

    step = CHUNKS_PER_STEP * CHUNK
    row = pl.BlockSpec((step, GDN_WIDTH), lambda n: (n, 0))
    sq = pl.BlockSpec((CHUNKS_PER_STEP * GDN_STACK, GDN_STACK), lambda n: (n, 0))
    nsq = S // CHUNK * GDN_STACK
    return pl.pallas_call(
        body, name=name, grid=(S // step,), in_specs=[row] * 5, out_specs=[row, row, sq, sq],
        out_shape=[_sds((S, GDN_WIDTH)), _sds((S, GDN_WIDTH)), _sds((nsq, GDN_STACK)), _sds((nsq, GDN_STACK))],
        compiler_params=_params(("parallel",)),
    )(qn, kn, v, gcb, bb)


def _gdn_scan_fwd(uv, wk, at, qn, kn, gcb, bb, proj_z, gnw, *, name):
    S = uv.shape[0]
    nc = S // CHUNK
    rows = slice(None)

    def body(uv_ref, wk_ref, at_ref, qn_ref, kn_ref, gcb_ref, bb_ref, z_ref, gnw_ref, o_ref, u_ref, sp_ref, oa_ref, st_ref):
        n = pl.program_id(0)

        @pl.when(n == 0)
        def _():
            st_ref[...] = jnp.zeros_like(st_ref)

        st = st_ref[...]
        sp_ref[...] = st
        uv, wk, q, k, gs, bs, z = [_stack(r, rows) for r in (uv_ref, wk_ref, qn_ref, kn_ref, gcb_ref, bb_ref, z_ref)]
        u = uv - _bdot(_head_spread(wk), st)
        o = _bdot(_head_spread(q * jnp.exp(gs)), st) + _bdot(at_ref[...], u)
        ke = k * jnp.exp(_last_rows(gs, CHUNK) - gs) * bs
        st_ref[...] = jnp.exp(_last_rows(gs, GDN_DIM)) * st + _bdot_tn(_head_spread(ke), u)
        _unstack_to(u_ref, rows, u)
        _unstack_to(o_ref, rows, o)
        r = lax.rsqrt(jnp.mean(o * o, axis=-1, keepdims=True) + EPS)
        oa = o * r * gnw_ref[...] * (z * _sigmoid(z))
        oa_ref[...] = jnp.concatenate([oa[CHUNK * h:CHUNK * (h + 1)] for h in range(GDN_HEADS)], axis=1).astype(BF16)

    row = pl.BlockSpec((CHUNK, GDN_WIDTH), lambda n: (n, 0))
    sq = pl.BlockSpec((GDN_STACK, GDN_STACK), lambda n: (n, 0))
    return pl.pallas_call(
        body, name=name, grid=(nc,),
        in_specs=[row, row, sq, row, row, row, row, row, pl.BlockSpec((1, GDN_DIM), lambda n: (0, 0))],
        out_specs=[row, row, pl.BlockSpec((GDN_WIDTH, GDN_DIM), lambda n: (n, 0)), row],
        out_shape=[_sds((S, GDN_WIDTH)), _sds((S, GDN_WIDTH)), _sds((nc * GDN_WIDTH, GDN_DIM)), _sds((S, 2 * GDN_WIDTH), BF16)],
        scratch_shapes=[pltpu.VMEM((GDN_WIDTH, GDN_DIM), F32)],
        compiler_params=_params(("arbitrary",)),
    )(uv, wk, at, qn, kn, gcb, bb, proj_z, gnw)


def _gdn_scan_bwd(d_oab, o, proj_z, gnw, sp, u, wk, at, qn, kn, gcb, bb, *, name):
    S = o.shape[0]
    nc = S // CHUNK
    rows = slice(None)

    def body(do_ref, o_ref, z_ref, gnw_ref, sp_ref, u_ref, wk_ref, at_ref, qn_ref, kn_ref, gcb_ref, bb_ref,
             dz_ref, dgn_ref, du_ref, dwk_ref, dat_ref, dqd_ref, dke_ref, dgl_ref, ds_ref):
        n = pl.program_id(0)

        @pl.when(n == 0)
        def _():
            ds_ref[...] = jnp.zeros_like(ds_ref)
            dgn_ref[...] = jnp.zeros_like(dgn_ref)

        d_oa, oo, z, uu, wk, q, k, gs, bs = [
            _stack(r, rows) for r in (do_ref, o_ref, z_ref, u_ref, wk_ref, qn_ref, kn_ref, gcb_ref, bb_ref)]
        gw = gnw_ref[...]
        sg = _sigmoid(z)
        r = lax.rsqrt(jnp.mean(oo * oo, axis=-1, keepdims=True) + EPS)
        xh = oo * r
        dy = d_oa * (z * sg)
        _unstack_to(dz_ref, rows, d_oa * (xh * gw) * _dsilu(z, sg))
        dgn_ref[...] += jnp.sum(dy * xh, axis=0, keepdims=True)
        dxh = dy * gw
        do = r * (dxh - xh * jnp.mean(dxh * xh, axis=-1, keepdims=True))

        st = sp_ref[...]
        dst = ds_ref[...]
        glast = _last_rows(gs, CHUNK)
        ge = jnp.exp(_last_rows(gs, GDN_DIM))
        qd = q * jnp.exp(gs)
        ke = k * jnp.exp(glast - gs) * bs
        _unstack_to(dqd_ref, rows, _head_diag(_bdot_nt(do, st)))
        dat_ref[...] = _bdot_nt(do, uu)
        du = _bdot_tn(at_ref[...], do) + _bdot(_head_spread(ke), dst)
        _unstack_to(dke_ref, rows, _head_diag(_bdot_nt(uu, dst)))
        prod = dst * st
        for h in range(GDN_HEADS):
            blk = prod[GDN_DIM * h:GDN_DIM * (h + 1)]
            dge = jnp.sum(jnp.sum(blk, axis=1, keepdims=True), axis=0, keepdims=True)
            dgl_ref[0, :, GDN_DIM * h:GDN_DIM * (h + 1)] = jnp.broadcast_to(dge * ge[GDN_DIM * h:GDN_DIM * h + 1], (8, GDN_DIM))
        ds_ref[...] = _bdot_tn(_head_spread(qd), do) + ge * dst - _bdot_tn(_head_spread(wk), du)
        _unstack_to(du_ref, rows, du)
        _unstack_to(dwk_ref, rows, -_head_diag(_bdot_nt(du, st)))

    rev = lambda n: (nc - 1 - n, 0)
    row = pl.BlockSpec((CHUNK, GDN_WIDTH), rev)
    sq = pl.BlockSpec((GDN_STACK, GDN_STACK), rev)
    one = pl.BlockSpec((1, GDN_DIM), lambda n: (0, 0))
    return pl.pallas_call(
        body, name=name, grid=(nc,),
        in_specs=[row, row, row, one, pl.BlockSpec((GDN_WIDTH, GDN_DIM), rev), row, row, sq, row, row, row, row],
        out_specs=[row, one, row, row, sq, row, row, pl.BlockSpec((1, 8, GDN_WIDTH), lambda n: (nc - 1 - n, 0, 0))],
        out_shape=[_sds((S, GDN_WIDTH), BF16), _sds((1, GDN_DIM)), _sds((S, GDN_WIDTH)), _sds((S, GDN_WIDTH)),
                   _sds((nc * GDN_STACK, GDN_STACK)), _sds((S, GDN_WIDTH)), _sds((S, GDN_WIDTH)), _sds((nc, 8, GDN_WIDTH))],
        scratch_shapes=[pltpu.VMEM((GDN_WIDTH, GDN_DIM), F32)],
        compiler_params=_params(("arbitrary",)),
    )(d_oab, o, proj_z, gnw, sp, u, wk, at, qn, kn, gcb, bb)


def _gdn_chunk_bwd(qn, kn, gcb, bb, tmat, uv, wk, du, dwk, dat, dqd, dke, dgl, *, name):
    S = qn.shape[0]

    def body(qn_ref, kn_ref, gcb_ref, bb_ref, t_ref, uv_ref, wk_ref, du_ref, dwk_ref, dat_ref, dqd_ref, dke_ref,
             dgl_ref, dq_ref, dk_ref, dv_ref, dg_ref, dbeta_ref):
        incl, strict, _ = _stack_masks()
        lane = lax.broadcasted_iota(jnp.int32, (CHUNK, 128), 1)
        rowi = lax.broadcasted_iota(jnp.int32, (CHUNK, 1), 0)
        rsum = lambda x: jnp.sum(x, axis=-1, keepdims=True)
        for c in range(CHUNKS_PER_STEP):
            rows = slice(CHUNK * c, CHUNK * (c + 1))
            srows = slice(GDN_STACK * c, GDN_STACK * (c + 1))
            q, k, gs, bs, uv, wk, du, dwk, dqd, dke = [
                _stack(r, rows) for r in (qn_ref, kn_ref, gcb_ref, bb_ref, uv_ref, wk_ref, du_ref, dwk_ref, dqd_ref, dke_ref)]
            dec, bt = _stack_decay(gs, bs, incl)
            kk = _bdot_nt(k, k)
            qk = _bdot_nt(q, k)
            d_rhs = _dot3(t_ref[srows, :], jnp.concatenate([du, dwk], axis=1), ((0,), (0,)))
            sol = jnp.concatenate([uv, wk], axis=1)
            d_l = jnp.where(strict, -_dot3(d_rhs, sol, ((1,), (1,))), 0.0)
            d_a = jnp.where(incl, dat_ref[srows, :], 0.0)
            gam = jnp.exp(gs)
            e = jnp.exp(_last_rows(gs, CHUNK) - gs)
            d_gk = d_rhs[:, GDN_DIM:]
            ml = d_l * dec * bt
            ma = d_a * dec * bt
            _unstack_to(dq_ref, rows, _bdot(ma, k) + dqd * gam)
            _unstack_to(dk_ref, rows, _bdot(ml + ml.T, k) + _bdot_tn(ma, q) + d_gk * gam + dke * (e * bs))
            _unstack_to(dv_ref, rows, d_rhs[:, :GDN_DIM])
            wb = d_l * dec * kk + d_a * dec * qk
            ew = wb * bt
            s_ke = rsum(dke * k * (e * bs))
            dbeta = rsum(wb.T) + rsum(dke * k * e)
            dgc = rsum(ew) - rsum(ew.T) + rsum(dqd * q * gam) + rsum(d_gk * k * gam) - s_ke
            dgc4 = jnp.zeros((CHUNK, 128), F32)
            db4 = jnp.zeros((CHUNK, 128), F32)
            for h in range(GDN_HEADS):
                hr = slice(CHUNK * h, CHUNK * (h + 1))
                tail = jnp.sum(s_ke[hr], axis=0, keepdims=True) + dgl_ref[c, 0:1, GDN_DIM * h:GDN_DIM * h + 1]
                dgc4 = jnp.where(lane == h, dgc[hr] + jnp.where(rowi == CHUNK - 1, tail, 0.0), dgc4)
                db4 = jnp.where(lane == h, dbeta[hr], db4)
            dg_ref[rows, :] = _exact_tri_dot(_chunk_tri(CHUNK, upper=True), dgc4)
            dbeta_ref[rows, :] = db4

    step = CHUNKS_PER_STEP * CHUNK
    row = pl.BlockSpec((step, GDN_WIDTH), lambda n: (n, 0))
    sq = pl.BlockSpec((CHUNKS_PER_STEP * GDN_STACK, GDN_STACK), lambda n: (n, 0))
    col = pl.BlockSpec((step, 128), lambda n: (n, 0))
    return pl.pallas_call(
        body, name=name, grid=(S // step,),
        in_specs=[row] * 4 + [sq, row, row, row, row, sq, row, row,
                              pl.BlockSpec((CHUNKS_PER_STEP, 8, GDN_WIDTH), lambda n: (n, 0, 0))],
        out_specs=[row, row, row, col, col],
        out_shape=[_sds((S, GDN_WIDTH))] * 3 + [_sds((S, 128))] * 2, compiler_params=_params(("parallel",)),
    )(qn, kn, gcb, bb, tmat, uv, wk, du, dwk, dat, dqd, dke, dgl)


def _gdn_prep_bwd(dqn, dkn, dv, dg, dbeta, proj_a, conv_w, a_log, dt_bias, *, name, tm=256):
    S = proj_a.shape[0]
    nblk = S // tm
    W3 = 3 * GDN_WIDTH

    def body(dqn_ref, dkn_ref, dv_ref, dg_ref, dbeta_ref, cur_ref, prev_ref, ba_ref, cw_ref, al_ref, dt_ref,
             dc_ref, dba_ref, sm_ref):
        i = pl.program_id(0)
        prev = jnp.where(i > 0, prev_ref[...], 0.0)
        c = _conv_rows(prev, cur_ref[...], cw_ref[...], GDN_CONV)
        sg = _sigmoid(c)
        a = c * sg
        dsl = _dsilu(c, sg)
        ba = ba_ref[...]
        lane = lax.broadcasted_iota(jnp.int32, (tm, 128), 1)
        lane1 = lax.broadcasted_iota(jnp.int32, (1, 128), 1)
        dba = jnp.zeros((tm, 128), F32)
        sm = jnp.zeros((1, 128), F32)
        for h in range(GDN_HEADS):
            sl = slice(GDN_DIM * h, GDN_DIM * (h + 1))
            ks = slice(GDN_WIDTH + GDN_DIM * h, GDN_WIDTH + GDN_DIM * (h + 1))
            qh, kh = a[:, sl], a[:, ks]
            rq = lax.rsqrt(jnp.sum(qh * qh, axis=-1, keepdims=True) + EPS)
            rk = lax.rsqrt(jnp.sum(kh * kh, axis=-1, keepdims=True) + EPS)
            qhat, khat = qh * rq, kh * rk
            dyq = dqn_ref[:, sl] * (GDN_DIM ** -0.5)
            dyk = dkn_ref[:, sl]
            dq = rq * (dyq - qhat * jnp.sum(dyq * qhat, axis=-1, keepdims=True))
            dk = rk * (dyk - khat * jnp.sum(dyk * khat, axis=-1, keepdims=True))
            dc_ref[:, sl] = dq * dsl[:, sl]
            dc_ref[:, ks] = dk * dsl[:, ks]
            beta = _sigmoid(ba[:, h:h + 1])
            db = dbeta_ref[:, h:h + 1] * beta * (1.0 - beta)
            aneg = -jnp.exp(al_ref[0:1, h:h + 1])
            xa = ba[:, GDN_HEADS + h:GDN_HEADS + h + 1] + dt_ref[0:1, h:h + 1]
            dgh = dg_ref[:, h:h + 1]
            dxa = dgh * aneg * _sigmoid(xa)
            dba = jnp.where(lane == h, db, dba)
            dba = jnp.where(lane == GDN_HEADS + h, dxa, dba)
            d_alog = jnp.sum(dgh * _softplus(xa), axis=0, keepdims=True) * aneg
            sm = jnp.where(lane1 == h, d_alog, sm)
            sm = jnp.where(lane1 == GDN_HEADS + h, jnp.sum(dxa, axis=0, keepdims=True), sm)
        vs = slice(2 * GDN_WIDTH, W3)
        dc_ref[:, vs] = dv_ref[...] * dsl[:, vs]
        dba_ref[...] = dba

        @pl.when(i == 0)
        def _():
            sm_ref[...] = sm

        @pl.when(i > 0)
        def _():
            sm_ref[...] += sm

    prev_spec, _ = _halo_specs(tm, W3, 0, nblk)
    row = pl.BlockSpec((tm, GDN_WIDTH), lambda i: (i, 0))
    col = pl.BlockSpec((tm, 128), lambda i: (i, 0))
    small = lambda a: pl.BlockSpec(a.shape, lambda i: (0, 0))
    return pl.pallas_call(
        body, name=name, grid=(nblk,),
        in_specs=[row, row, row, col, col, pl.BlockSpec((tm, W3), lambda i: (i, 0)), prev_spec,
                  pl.BlockSpec((tm, 128), lambda i: (i, W3 // 128)), small(conv_w), small(a_log), small(dt_bias)],
        out_specs=[pl.BlockSpec((tm, W3), lambda i: (i, 0)), col, pl.BlockSpec((1, 128), lambda i: (0, 0))],
        out_shape=[_sds((S, W3)), _sds((S, 128)), _sds((1, 128))], compiler_params=_params(("arbitrary",)),
    )(dqn, dkn, dv, dg, dbeta, proj_a, proj_a, proj_a, conv_w, a_log, dt_bias)


def _gdn_conv_bwd(dc, dba, proj_a, conv_w, *, name, tm=256):
    S = proj_a.shape[0]
    nblk = S // tm
    W3 = 3 * GDN_WIDTH

    def body(dc_ref, dnext_ref, dba_ref, cur_ref, prev_ref, cw_ref, da_ref, dcw_ref):
        i = pl.program_id(0)
        prev = jnp.where(i > 0, prev_ref[...], 0.0)
        nxt = jnp.where(i < nblk - 1, dnext_ref[...], 0.0)
        dx, dw = _conv_rows_bwd(dc_ref[...], nxt, prev, cur_ref[...], cw_ref[...], GDN_CONV)
        da_ref[:, 0:W3] = dx.astype(BF16)
        da_ref[:, W3:] = dba_ref[...].astype(BF16)

        @pl.when(i == 0)
        def _():
            dcw_ref[...] = dw

        @pl.when(i > 0)
        def _():
            dcw_ref[...] += dw

    prev_spec, next_spec = _halo_specs(tm, W3, 0, nblk)
    wide = pl.BlockSpec((tm, W3), lambda i: (i, 0))
    return pl.pallas_call(
        body, name=name, grid=(nblk,),
        in_specs=[wide, next_spec, pl.BlockSpec((tm, 128), lambda i: (i, 0)), wide, prev_spec,
                  pl.BlockSpec(conv_w.shape, lambda i: (0, 0))],
        out_specs=[pl.BlockSpec((tm, A_COLS), lambda i: (i, 0)), pl.BlockSpec(conv_w.shape, lambda i: (0, 0))],
        out_shape=[_sds((S, A_COLS), BF16), _sds(conv_w.shape)], compiler_params=_params(("arbitrary",)),
    )(dc, dc, dba, proj_a, proj_a, conv_w)


def _band_mask(nk):
    i = lax.broadcasted_iota(jnp.int32, (2 * BAND, nk), 0) & (BAND - 1)
    j = lax.broadcasted_iota(jnp.int32, (2 * BAND, nk), 1)
    if nk == BAND:
        return j <= i
    return (j >= i) & (j <= i + BAND)


def _stack_heads(x, lo):
    return jnp.concatenate([jnp.where(lo, x, 0.0), jnp.where(lo, 0.0, x)], axis=0)


def _stack_cols(x):
    return jnp.concatenate([x[:, 0:1], x[:, DIL_DIM:DIL_DIM + 1]], axis=0)


def _unstack(x, lo):
    return jnp.where(lo, x[0:BAND], x[BAND:2 * BAND])


def _rows(start, size, stride):
    return pl.ds(start, size) if stride == 1 else pl.ds(start, size, stride=stride)


ATTN_LANES = 4


def _attn_blocks(S, visit_many, lanes=ATTN_LANES):
    for d in DILATIONS:
        nb = S // (d * BAND)
        if d == 1:
            half = nb // 2
            visit_many(d, [(0, 0, True), (0, half, False)])

            def pair(n, c):
                visit_many(1, [(0, n, False), (0, n + half, False)])
                return c
            lax.fori_loop(1, half, pair, 0)
        elif nb > 1:
            for r0 in range(0, d, lanes):
                visit_many(d, [(r0 + t, 0, True) for t in range(lanes)])

                def column(n, c, d=d, r0=r0):
                    visit_many(d, [(r0 + t, n, False) for t in range(lanes)])
                    return c
                lax.fori_loop(1, nb, column, 0)
        else:
            def group(g, c, d=d):
                visit_many(d, [(g * lanes + t, 0, True) for t in range(lanes)])
                return c
            lax.fori_loop(0, d // lanes, group, 0)


def _attn_fwd(proj_b, oab, *, name):
    S = proj_b.shape[0]
    scale = DIL_DIM ** -0.5

    def body(q_ref, k_ref, v_ref, oab_in_ref, ob_ref, lse_ref, m_ref, l_ref, acc_ref):
        del oab_in_ref
        lane = lax.broadcasted_iota(jnp.int32, (BAND, 128), 1)
        lo = lane < DIL_DIM
        m_ref[...] = jnp.full_like(m_ref, NEG_BIG)
        l_ref[...] = jnp.zeros_like(l_ref)
        acc_ref[...] = jnp.zeros_like(acc_ref)

        def load(d, r, n, first):
            nk = BAND if first else 2 * BAND
            qrows = _rows(r + n * (BAND * d), BAND, d)
            krows = _rows(r if first else r + (n - 1) * (BAND * d), nk, d)
            return dict(nk=nk, qrows=qrows, q=q_ref[qrows, :] * scale, k=k_ref[krows, :].astype(BF16),
                        v=v_ref[krows, :].astype(BF16), m=m_ref[qrows, :], l=l_ref[qrows, :], acc=acc_ref[qrows, :])

        def compute(b):
            q, k, v = b["q"], b["k"], b["v"]
            s = jnp.where(_band_mask(b["nk"]), _bdot_nt(_stack_heads(q, lo), k), NEG_BIG)
            m_old = _stack_cols(b["m"])
            m_new = jnp.maximum(m_old, jnp.max(s, axis=-1, keepdims=True))
            p = jnp.exp(s - m_new)
            alpha = _unstack(jnp.exp(m_old - m_new), lo)
            l_new = alpha * b["l"] + _unstack(jnp.sum(p, axis=-1, keepdims=True), lo)
            return _unstack(m_new, lo), l_new, alpha * b["acc"] + _unstack(_bdot(p, v), lo)

        def visit_many(d, blocks):
            loaded = [load(d, *blk) for blk in blocks]
            done = [compute(b) for b in loaded]
            for b, (m_new, l_new, acc_new) in zip(loaded, done):
                m_ref[b["qrows"], :] = m_new
                l_ref[b["qrows"], :] = l_new
                acc_ref[b["qrows"], :] = acc_new

        _attn_blocks(S, visit_many)
        ob_ref[...] = (acc_ref[...] / l_ref[...]).astype(BF16)
        lse_ref[...] = m_ref[...] + jnp.log(l_ref[...])

    part = lambda t: pl.BlockSpec((S, 128), lambda p: (0, 3 * p + t))
    return pl.pallas_call(
        body, name=name, grid=(DIL_PAIRS,),
        in_specs=[part(0), part(1), part(2), pl.BlockSpec(memory_space=pl.ANY)],
        out_specs=[pl.BlockSpec((S, 128), lambda p: (0, GDN_WIDTH // 128 + p)), pl.BlockSpec((S, 128), lambda p: (0, p))],
        out_shape=[_sds(oab.shape, BF16), _sds((S, DIL_WIDTH))],
        scratch_shapes=[pltpu.VMEM((S, 128), F32)] * 3, input_output_aliases={3: 0},
        compiler_params=_params(("parallel",)),
    )(proj_b, proj_b, proj_b, oab)


def _attn_bwd(proj_b, oab, d_oab, lse, *, name):
    S = proj_b.shape[0]
    scale = DIL_DIM ** -0.5

    def body(q_ref, k_ref, v_ref, o_ref, do_ref, lse_ref, dqkv_ref, dq_ref, dk_ref, dv_ref, delta_ref):
        lane = lax.broadcasted_iota(jnp.int32, (BAND, 128), 1)
        lo = lane < DIL_DIM
        dq_ref[...] = jnp.zeros_like(dq_ref)
        dk_ref[...] = jnp.zeros_like(dk_ref)
        dv_ref[...] = jnp.zeros_like(dv_ref)
        prod = do_ref[...] * o_ref[...].astype(F32)
        lo_all = lax.broadcasted_iota(jnp.int32, (S, 128), 1) < DIL_DIM
        delta_ref[...] = jnp.where(lo_all, jnp.sum(jnp.where(lo_all, prod, 0.0), axis=-1, keepdims=True),
                                   jnp.sum(jnp.where(lo_all, 0.0, prod), axis=-1, keepdims=True))

        def load(d, r, n, first):
            nk = BAND if first else 2 * BAND
            qrows = _rows(r + n * (BAND * d), BAND, d)
            krows = _rows(r if first else r + (n - 1) * (BAND * d), nk, d)
            return dict(nk=nk, qrows=qrows, krows=krows, q=q_ref[qrows, :] * scale, k=k_ref[krows, :], v=v_ref[krows, :],
                        do=do_ref[qrows, :], delta=delta_ref[qrows, :], lse=lse_ref[qrows, :],
                        dq=dq_ref[qrows, :], dk=dk_ref[krows, :], dv=dv_ref[krows, :])

        def compute(b):
            q, k, v, do = b["q"], b["k"], b["v"], b["do"]
            qs, dos = _stack_heads(q, lo), _stack_heads(do, lo)
            p = jnp.where(_band_mask(b["nk"]), jnp.exp(_bdot_nt(qs, k) - _stack_cols(b["lse"])), 0.0)
            ds = p * (_bdot_nt(dos, v) - _stack_cols(b["delta"]))
            dq = b["dq"] + _unstack(_bdot(ds, k), lo) * scale
            return dq, b["dk"] + _bdot_tn(ds, qs), b["dv"] + _bdot_tn(p, dos)

        def visit_many(d, blocks):
            loaded = [load(d, *blk) for blk in blocks]
            done = [compute(b) for b in loaded]
            for b, (dq, dk, dv) in zip(loaded, done):
                dq_ref[b["qrows"], :] = dq
                dk_ref[b["krows"], :] = dk
                dv_ref[b["krows"], :] = dv

        _attn_blocks(S, visit_many, lanes=2)
        dqkv_ref[:, 0:128] = dq_ref[...].astype(BF16)
        dqkv_ref[:, 128:256] = dk_ref[...].astype(BF16)
        dqkv_ref[:, 256:384] = dv_ref[...].astype(BF16)

    half = lambda p: (0, GDN_WIDTH // 128 + p)
    part = lambda t: pl.BlockSpec((S, 128), lambda p: (0, 3 * p + t))
    return pl.pallas_call(
        body, name=name, grid=(DIL_PAIRS,),
        in_specs=[part(0), part(1), part(2), pl.BlockSpec((S, 128), half), pl.BlockSpec((S, 128), half),
                  pl.BlockSpec((S, 128), lambda p: (0, p))],
        out_specs=pl.BlockSpec((S, 384), lambda p: (0, p)), out_shape=_sds((S, 3 * DIL_WIDTH), BF16),
        scratch_shapes=[pltpu.VMEM((S, 128), F32)] * 4, compiler_params=_params(("parallel",)),
    )(proj_b, proj_b, proj_b, oab, d_oab, lse)


FF_SLAB = 2 * D_FF // N_DEV
FF_PAIRS = N_DEV // 2
ROWS16 = 16


def _taps(w, x, base, n):
    out = x[base:base + n] * w[0:1]
    for t in range(1, FFN_CONV):
        out = out + x[base + t:base + t + n] * w[t:t + 1]
    return out


def _ffn_fwd(h2, x1, w_up, conv_w, w_down, *, name, tm=512):
    S, D = h2.shape
    ni = S // tm
    per = tm // ROWS16

    def body(h_ref, hp_ref, x1_ref, wg_ref, wu_ref, cg_ref, cu_ref, wd_ref, x2_ref, ug_ref, uu_ref):
        i, j = pl.program_id(0), pl.program_id(1)
        hv = jnp.concatenate([hp_ref[...], h_ref[...]], axis=0)
        row = lax.broadcasted_iota(jnp.int32, (tm + ROWS16, 1), 0)
        keep = (i > 0) | (row >= ROWS16)

        def branch(w_ref, c_ref, u_ref):
            u = lax.dot_general(hv, w_ref[...], (((1,), (1,)), ((), ())), preferred_element_type=F32).astype(BF16)
            u_ref[...] = u[ROWS16:]
            return _taps(c_ref[...], jnp.where(keep, u.astype(F32), 0.0), ROWS16 - (FFN_CONV - 1), tm)

        gate = branch(wg_ref, cg_ref, ug_ref)
        up = branch(wu_ref, cu_ref, uu_ref)
        act = (gate * _sigmoid(gate) * up).astype(BF16)
        part = jnp.dot(act, wd_ref[...], preferred_element_type=F32)

        @pl.when(j == 0)
        def _():
            x2_ref[...] = x1_ref[...] + part

        @pl.when(j > 0)
        def _():
            x2_ref[...] += part

    rows = pl.BlockSpec((tm, D), lambda i, j: (i, 0))
    slab = lambda off: pl.BlockSpec((None, FF_SLAB, D), lambda i, j: (j + off, 0, 0))
    cslab = lambda off: pl.BlockSpec((None, FFN_CONV, FF_SLAB), lambda i, j: (j + off, 0, 0))
    uspec = pl.BlockSpec((None, tm, FF_SLAB), lambda i, j: (j, i, 0))
    return pl.pallas_call(
        body, name=name, grid=(ni, FF_PAIRS),
        in_specs=[rows, pl.BlockSpec((ROWS16, D), lambda i, j: (jnp.maximum(i * per - 1, 0), 0)), rows,
                  slab(0), slab(FF_PAIRS), cslab(0), cslab(FF_PAIRS), pl.BlockSpec((FF_SLAB, D), lambda i, j: (j, 0))],
        out_specs=[rows, uspec, uspec],
        out_shape=[_sds((S, D)), _sds((FF_PAIRS, S, FF_SLAB), BF16), _sds((FF_PAIRS, S, FF_SLAB), BF16)],
        compiler_params=_params(("parallel", "arbitrary")),
    )(h2, h2, x1, w_up, w_up, conv_w, conv_w, w_down)


def _ffn_bwd(dx2, h2, ug, uu, conv_w, w_down, *, name, tm=512):
    S, D = h2.shape
    ni = S // tm
    per = tm // ROWS16
    ext = tm + ROWS16

    def body(dx_ref, dxn_ref, h_ref, ug_ref, ugp_ref, ugn_ref, uu_ref, uup_ref, uun_ref, cg_ref, cu_ref, wd_ref,
             dug_ref, duu_ref, gd_ref, gg_ref, gu_ref, dcg_ref, dcu_ref, acc_d, acc_g, acc_u, acc_cg, acc_cu):
        i = pl.program_id(1)

        @pl.when(i == 0)
        def _():
            acc_d[...] = jnp.zeros_like(acc_d)
            acc_g[...] = jnp.zeros_like(acc_g)
            acc_u[...] = jnp.zeros_like(acc_u)
            acc_cg[...] = jnp.zeros_like(acc_cg)
            acc_cu[...] = jnp.zeros_like(acc_cu)

        dx = dx_ref[...]
        dxe = jnp.concatenate([dx, dxn_ref[...]], axis=0)
        row = lax.broadcasted_iota(jnp.int32, (ext, 1), 0)
        live = (i < ni - 1) | (row < tm)
        d_act = jnp.where(live, lax.dot_general(dxe, wd_ref[...], (((1,), (1,)), ((), ())), preferred_element_type=F32), 0.0)
        rowp = lax.broadcasted_iota(jnp.int32, (ext + ROWS16, 1), 0)
        keep = (i > 0) | (rowp >= ROWS16)

        def pre(cur, prev, nxt):
            return jnp.where(keep, jnp.concatenate([prev[...], cur[...], nxt[...]], axis=0).astype(F32), 0.0)

        uge, uue = pre(ug_ref, ugp_ref, ugn_ref), pre(uu_ref, uup_ref, uun_ref)
        cg, cu = cg_ref[...], cu_ref[...]
        base = ROWS16 - (FFN_CONV - 1)
        gate = _taps(cg, uge, base, ext)
        up = _taps(cu, uue, base, ext)
        sg = _sigmoid(gate)
        silu = gate * sg
        dgc = d_act * up * _dsilu(gate, sg)
        duc = d_act * silu

        def conv_t(w, dc):
            out = dc[FFN_CONV - 1:FFN_CONV - 1 + tm] * w[0:1]
            for t in range(1, FFN_CONV):
                out = out + dc[FFN_CONV - 1 - t:FFN_CONV - 1 - t + tm] * w[t:t + 1]
            return out.astype(BF16)

        du_g, du_u = conv_t(cg, dgc), conv_t(cu, duc)
        dug_ref[...] = du_g
        duu_ref[...] = du_u
        dcw = lambda dc, xe: jnp.concatenate(
            [jnp.sum(dc[0:tm] * xe[base + t:base + t + tm], axis=0, keepdims=True) for t in range(FFN_CONV)], axis=0)
        acc_cg[0:FFN_CONV, :] += dcw(dgc, uge)
        acc_cu[0:FFN_CONV, :] += dcw(duc, uue)
        tn = (((0,), (0,)), ((), ()))
        act = (silu[0:tm] * up[0:tm]).astype(BF16)
        acc_d[...] += lax.dot_general(act, dx, tn, preferred_element_type=F32)
        hv = h_ref[...]
        acc_g[...] += lax.dot_general(du_g, hv, tn, preferred_element_type=F32)
        acc_u[...] += lax.dot_general(du_u, hv, tn, preferred_element_type=F32)

        @pl.when(i == ni - 1)
        def _():
            gd_ref[...] = acc_d[...].astype(BF16)
            gg_ref[...] = acc_g[...].astype(BF16)
            gu_ref[...] = acc_u[...].astype(BF16)
            dcg_ref[...] = acc_cg[0:FFN_CONV, :]
            dcu_ref[...] = acc_cu[0:FFN_CONV, :]

    last16 = S // ROWS16 - 1
    rows = pl.BlockSpec((tm, D), lambda j, i: (i, 0))
    rows_next = pl.BlockSpec((ROWS16, D), lambda j, i: (jnp.minimum((i + 1) * per, last16), 0))
    u_cur = pl.BlockSpec((None, tm, FF_SLAB), lambda j, i: (j, i, 0))
    u_prev = pl.BlockSpec((None, ROWS16, FF_SLAB), lambda j, i: (j, jnp.maximum(i * per - 1, 0), 0))
    u_next = pl.BlockSpec((None, ROWS16, FF_SLAB), lambda j, i: (j, jnp.minimum((i + 1) * per, last16), 0))
    cslab = lambda off: pl.BlockSpec((None, FFN_CONV, FF_SLAB), lambda j, i: (j + off, 0, 0))
    wslab = pl.BlockSpec((None, FF_SLAB, D), lambda j, i: (j, 0, 0))
    dslab = pl.BlockSpec((None, FFN_CONV, FF_SLAB), lambda j, i: (j, 0, 0))
    return pl.pallas_call(
        body, name=name, grid=(FF_PAIRS, ni),
        in_specs=[rows, rows_next, rows, u_cur, u_prev, u_next, u_cur, u_prev, u_next, cslab(0), cslab(FF_PAIRS),
                  pl.BlockSpec((FF_SLAB, D), lambda j, i: (j, 0))],
        out_specs=[u_cur, u_cur, pl.BlockSpec((FF_SLAB, D), lambda j, i: (j, 0)), wslab, wslab, dslab, dslab],
        out_shape=[_sds((FF_PAIRS, S, FF_SLAB), BF16), _sds((FF_PAIRS, S, FF_SLAB), BF16), _sds((D_FF, D), BF16),
                   _sds((FF_PAIRS, FF_SLAB, D), BF16), _sds((FF_PAIRS, FF_SLAB, D), BF16),
                   _sds((FF_PAIRS, FFN_CONV, FF_SLAB)), _sds((FF_PAIRS, FFN_CONV, FF_SLAB))],
        scratch_shapes=[pltpu.VMEM((FF_SLAB, D), F32), pltpu.VMEM((FF_SLAB, D), F32), pltpu.VMEM((FF_SLAB, D), F32),
                        pltpu.VMEM((8, FF_SLAB), F32), pltpu.VMEM((8, FF_SLAB), F32)],
        compiler_params=_params(("parallel", "arbitrary")),
    )(dx2, dx2, h2, ug, ug, ug, uu, uu, uu, conv_w, conv_w, w_down)


def _mm_slabs(a, w, w_off, *, name, res=None, tm=1024, tn=1024):
    nk, S, _ = a.shape
    D = w.shape[2]
    has_res = res is not None

    def body(*refs):
        if has_res:
            a_ref, w_ref, r_ref, o_ref, acc_ref = refs
        else:
            a_ref, w_ref, o_ref, acc_ref = refs
        k = pl.program_id(2)
        part = jnp.dot(a_ref[...], w_ref[...], preferred_element_type=F32)

        @pl.when(k == 0)
        def _():
            acc_ref[...] = part

        @pl.when(k > 0)
        def _():
            acc_ref[...] += part

        @pl.when(k == nk - 1)
        def _():
            o_ref[...] = acc_ref[...] + r_ref[...] if has_res else acc_ref[...]

    o_spec = pl.BlockSpec((tm, tn), lambda i, j, k: (i, j))
    return pl.pallas_call(
        body, name=name, grid=(S // tm, D // tn, nk),
        in_specs=[pl.BlockSpec((None, tm, FF_SLAB), lambda i, j, k: (k, i, 0)),
                  pl.BlockSpec((None, FF_SLAB, tn), lambda i, j, k: (k + w_off, 0, j))] + ([o_spec] if has_res else []),
        out_specs=o_spec, out_shape=_sds((S, D)), scratch_shapes=[pltpu.VMEM((tm, tn), F32)],
        compiler_params=_params(("parallel", "parallel", "arbitrary")),
    )(*((a, w, res) if has_res else (a, w)))


def _local_step(x, tgt, norm1_w, w_a, w_z, w_b, conv_a, a_log, dt_bias, gnw, norm2_w, final_w, late_weights, emit):
    wgrad = functools.partial(_mm, ta=True, out_dtype=BF16)
    h1 = _rms_fwd(x, norm1_w, name="rms1_fwd")
    proj_a = _mm(h1, w_a, tb=True, name="proj_a", tn=A_COLS, tk=1024)
    proj_z = _mm(h1, w_z, tb=True, name="proj_z", tk=1024)
    proj_b = _mm(h1, w_b, tb=True, name="proj_b", tn=768, tk=1024)
    qn, kn, v, gcb, bb = _gdn_prep_fwd(proj_a, conv_a, a_log, dt_bias, name="gdn_prep_fwd")
    uv, wk, at, tmat = _gdn_chunk_fwd(qn, kn, v, gcb, bb, name="gdn_chunk_fwd")
    o, u, sp, oab = _gdn_scan_fwd(uv, wk, at, qn, kn, gcb, bb, proj_z, gnw, name="gdn_scan_fwd")
    oab, lse = _attn_fwd(proj_b, oab, name="attn_fwd")
    w_out, w_up, conv_f, w_down = late_weights(oab)
    x1 = _mm(oab, w_out, res=x, name="out_proj", tk=1024)
    h2 = _rms_fwd(x1, norm2_w, name="rms2_fwd")
    x2, ug, uu = _ffn_fwd(h2, x1, w_up, conv_f, w_down, name="ffn_fwd")
    dx2, dx2_b, d_final, loss = _loss_head(x2, final_w, tgt, name="loss_head")
    dug, duu, g_down, g_up_g, g_up_u, dcw_g, dcw_u = _ffn_bwd(dx2_b, h2, ug, uu, conv_f, w_down, name="ffn_bwd")
    token = emit("ffn", w_down=g_down, w_up=jnp.concatenate([g_up_g, g_up_u], axis=0),
                 conv_f=jnp.concatenate([dcw_g, dcw_u], axis=0))
    dh2 = _mm_slabs(dug, w_up, 0, name="ffn_up_dx_gate")
    dh2 = _mm_slabs(duu, w_up, FF_PAIRS, res=dh2, name="ffn_up_dx_up")
    dx1, d_norm2 = _rms_bwd(dh2, x1, _behind(norm2_w, token), dx2, name="rms2_bwd")
    d_oab = _mm(dx1, w_out, tb=True, name="out_proj_dx", tk=1024)
    gnw = _behind(gnw, emit("out", w_out=wgrad(oab, dx1, name="out_proj_dw")))
    dz, d_gnw, du, dwk, dat, dqd, dke, dgl = _gdn_scan_bwd(d_oab, o, proj_z, gnw, sp, u, wk, at, qn, kn, gcb, bb, name="gdn_scan_bwd")
    dqn, dkn, dv, dg, dbeta = _gdn_chunk_bwd(qn, kn, gcb, bb, tmat, uv, wk, du, dwk, dat, dqd, dke, dgl, name="gdn_chunk_bwd")
    dc, dba, d_small = _gdn_prep_bwd(dqn, dkn, dv, dg, dbeta, proj_a, conv_a, a_log, dt_bias, name="gdn_prep_bwd")
    d_pa, d_conv_a = _gdn_conv_bwd(dc, dba, proj_a, conv_a, name="gdn_conv_bwd")
    d_pb = _attn_bwd(proj_b, oab, d_oab, lse, name="attn_bwd")
    g_a = wgrad(d_pa, h1, name="proj_a_dw", tm=A_COLS)
    g_z = wgrad(dz, h1, name="proj_z_dw")
    g_b = wgrad(d_pb, h1, name="proj_b_dw", tm=768)
    w_z = _behind(w_z, emit("in", w_a=g_a, w_z=g_z, w_b=g_b, conv_a=d_conv_a))
    dh1 = _mm(dz, w_z, name="proj_z_dx")
    dh1 = _mm(d_pa, w_a, res=dh1, name="proj_a_dx", tk=A_COLS)
    dh1 = _mm(d_pb, w_b, res=dh1, name="proj_b_dx", tk=1536)
    grad_x, d_norm1 = _rms_bwd(dh1, x, norm1_w, dx1, name="rms1_bwd")
    small = dict(norm1=d_norm1, small=d_small, gnw=d_gnw, norm2=d_norm2, final=d_final)
    return loss, grad_x, small


_O1 = 3 * GDN_WIDTH
_O2 = _O1 + GDN_WIDTH
_O3 = _O2 + 2 * GDN_HEADS


def _split_w_in(w_t):
    d = w_t.shape[1]
    pad = jnp.zeros((A_COLS - _O1 - 2 * GDN_HEADS, d), w_t.dtype)
    w_a = jnp.concatenate([w_t[:_O1], w_t[_O2:_O3], pad], axis=0)
    w_b = w_t[_O3:].reshape(3, DIL_PAIRS, 128, d).transpose(1, 0, 2, 3).reshape(3 * DIL_WIDTH, d)
    return w_a, w_t[_O1:_O2], w_b


def _merge_g_in(g_a, g_z, g_b):
    d = g_a.shape[1]
    g_b = g_b.reshape(DIL_PAIRS, 3, 128, d).transpose(1, 0, 2, 3).reshape(3 * DIL_WIDTH, d)
    return jnp.concatenate([g_a[:_O1], g_z, g_a[_O1:_O1 + 2 * GDN_HEADS], g_b], axis=0)


MESH = pl.DeviceIdType.MESH
ANY = pl.BlockSpec(memory_space=pl.ANY)


def _position():
    return lax.axis_index("x"), lax.axis_index("y"), lax.axis_index("c")


def _slot(p):
    return 4 * p[0] + 2 * p[1] + p[2]


def _all_gather(blocks, *, name):
    n = len(blocks)

    def body(*refs):
        ins, outs = refs[:n], refs[n:2 * n]
        send_sems, recv_sems, local_sems = refs[2 * n:]
        x, y, c = _position()
        me, sibling = (x, y, c), (x, y, 1 - c)
        chips = [(1 - x, y), (x, 1 - y), (1 - x, 1 - y)]

        def copy(a, k, block, to, src=None):
            dst = outs[a].at[_slot(block)]
            return pltpu.make_async_remote_copy(
                src_ref=dst if src is None else src, dst_ref=dst, send_sem=send_sems.at[a, k], recv_sem=recv_sems.at[a, k],
                device_id=to, device_id_type=MESH)

        mine = [pltpu.make_async_copy(ins[a], outs[a].at[_slot(me)], local_sems.at[a]) for a in range(n)]
        for cp in mine:
            cp.start()
        first = []
        for a in range(n):
            first.append(copy(a, 0, me, sibling, src=ins[a]))
            first += [copy(a, 1 + j, me, (*chip, c), src=ins[a]) for j, chip in enumerate(chips)]
        for cp in first:
            cp.start()
        passed = []
        for j, chip in enumerate(chips):
            for a in range(n):
                copy(a, 1 + j, (*chip, c), me).wait_recv()
                fwd = copy(a, 4 + j, (*chip, c), sibling)
                fwd.start()
                passed.append(fwd)
        for a in range(n):
            copy(a, 0, sibling, me).wait_recv()
            for j, chip in enumerate(chips):
                copy(a, 4 + j, (*chip, 1 - c), me).wait_recv()
        for cp in first + passed:
            cp.wait_send()
        for cp in mine:
            cp.wait()

    return pl.pallas_call(
        body, name=name, in_specs=[ANY] * n, out_specs=[ANY] * n,
        out_shape=[_sds((N_DEV,) + b.shape, b.dtype) for b in blocks],
        scratch_shapes=[pltpu.SemaphoreType.DMA((n, 7)), pltpu.SemaphoreType.DMA((n, 7)), pltpu.SemaphoreType.DMA((n,))],
    )(*blocks)


def _gather_direct(block, *, name):
    def body(in_ref, out_ref, send_sems, recv_sems, local_sem):
        x, y, c = _position()
        me = _slot((x, y, c))
        mine = pltpu.make_async_copy(in_ref, out_ref.at[me], local_sem)
        mine.start()
        copies = [pltpu.make_async_remote_copy(
            src_ref=in_ref, dst_ref=out_ref.at[me], send_sem=send_sems.at[k - 1], recv_sem=recv_sems.at[k - 1],
            device_id=_peer_of(k, x, y, c), device_id_type=MESH) for k in range(1, N_DEV)]
        for cp in copies:
            cp.start()
        for cp in copies:
            cp.wait()
        mine.wait()

    return pl.pallas_call(
        body, name=name, in_specs=[pl.BlockSpec(memory_space=pltpu.VMEM)], out_specs=pl.BlockSpec(memory_space=pltpu.VMEM),
        out_shape=_sds((N_DEV,) + block.shape, block.dtype),
        scratch_shapes=[pltpu.SemaphoreType.DMA((N_DEV - 1,)), pltpu.SemaphoreType.DMA((N_DEV - 1,)), pltpu.SemaphoreType.DMA],
    )(block)


HBM = pl.BlockSpec(memory_space=pltpu.HBM)
SEM = pl.BlockSpec(memory_space=pltpu.SEMAPHORE)
EFFECT = pltpu.SideEffectType.DATAFLOW_SIDE_EFFECTING


def _peer_of(k, x, y, c):
    return (1 - x if k & 4 else x, 1 - y if k & 2 else y, 1 - c if k & 1 else c)


def _flight(a, k):
    return a * (N_DEV - 1) + k - 1


def _exchange_start(arrays, *, name, broadcast=False):
    n = len(arrays)

    def body(*refs):
        ins, lands = refs[:n], refs[n:2 * n]
        send_sems, recv_sems = refs[2 * n:2 * n + 2]
        token = refs[-1]
        x, y, c = _position()
        me = _slot((x, y, c))
        for k in range(1, N_DEV):
            peer = _peer_of(k, x, y, c)
            for a in range(n):
                pltpu.make_async_remote_copy(
                    src_ref=ins[a] if broadcast else ins[a].at[_slot(peer)], dst_ref=lands[a].at[me],
                    send_sem=send_sems.at[_flight(a, k)], recv_sem=recv_sems.at[_flight(a, k)],
                    device_id=peer, device_id_type=MESH).start()
        token[...] = jnp.zeros_like(token)

    land_shapes = [((N_DEV,) + s.shape) if broadcast else s.shape for s in arrays]
    lands = [pltpu.with_memory_space_constraint(lax.empty(shp, s.dtype), pltpu.HBM) for shp, s in zip(land_shapes, arrays)]
    srcs = [pltpu.with_memory_space_constraint(s, pltpu.HBM) for s in arrays]
    outs = pl.pallas_call(
        body, name=name, in_specs=[HBM] * (2 * n),
        out_specs=[SEM, SEM] + [HBM] * (2 * n) + [pl.BlockSpec(memory_space=pltpu.VMEM)],
        out_shape=[pltpu.SemaphoreType.DMA((n * (N_DEV - 1),)), pltpu.SemaphoreType.DMA((n * (N_DEV - 1),))]
        + [pltpu.HBM(s.shape, s.dtype) for s in arrays] + [pltpu.HBM(shp, s.dtype) for shp, s in zip(land_shapes, arrays)]
        + [_sds((8, 128))],
        input_output_aliases={i: 2 + i for i in range(2 * n)},
        compiler_params=pltpu.CompilerParams(has_side_effects=EFFECT),
    )(*srcs, *lands)
    return outs[0], outs[1], outs[2:2 + n], outs[2 + n:2 + 2 * n], outs[-1]


def _exchange_wait(send_sems, recv_sems, srcs, lands, after, *, name, broadcast=False):
    n = len(srcs)

    def body(*refs):
        ins, lnd = refs[:n], refs[n:2 * n]
        send_ref, recv_ref = refs[2 * n:2 * n + 2]
        x, y, c = _position()
        for k in range(1, N_DEV):
            for a in range(n):
                cp = pltpu.make_async_remote_copy(
                    src_ref=ins[a] if broadcast else ins[a].at[0], dst_ref=lnd[a].at[0], send_sem=send_ref.at[_flight(a, k)],
                    recv_sem=recv_ref.at[_flight(a, k)], device_id=_peer_of(k, x, y, c), device_id_type=MESH)
                cp.wait_send()
                cp.wait_recv()

    outs = pl.pallas_call(
        body, name=name, in_specs=[HBM] * (2 * n) + [SEM, SEM, ANY], out_specs=[HBM] * (2 * n),
        out_shape=[pltpu.HBM(s.shape, s.dtype) for s in srcs] + [pltpu.HBM(s.shape, s.dtype) for s in lands],
        input_output_aliases={i: i for i in range(2 * n)},
        compiler_params=pltpu.CompilerParams(has_side_effects=EFFECT),
    )(*srcs, *lands, send_sems, recv_sems, after)
    return outs[:n], outs[n:]


def _with_own(landed, srcs, me, broadcast=False):
    own = srcs if broadcast else [lax.dynamic_index_in_dim(s, me, 0, keepdims=False) for s in srcs]
    return [lax.dynamic_update_index_in_dim(l, o, me, 0) for l, o in zip(landed, own)]


def _behind(x, token):
    return x if token is None else x + token[0, 0].astype(x.dtype)


def _adamw(parts, w, m, v, *, name, tr=None, tc=None):
    R, C = w.shape
    tr = R if tr is None else tr
    tc = C if tc is None else tc
    assert R % tr == 0 and C % tc == 0
    c1 = 1.0 - ADAM_B1 ** ADAM_STEP
    c2 = 1.0 - ADAM_B2 ** ADAM_STEP

    def body(p_ref, w_ref, m_ref, v_ref, g_ref, d_ref, nm_ref, nv_ref):
        g = p_ref[0].astype(F32)
        for s in range(1, N_DEV):
            g = g + p_ref[s].astype(F32)
        nm = ADAM_B1 * m_ref[...] + (1.0 - ADAM_B1) * g
        nv = ADAM_B2 * v_ref[...] + (1.0 - ADAM_B2) * (g * g)
        g_ref[...] = g
        nm_ref[...] = nm
        nv_ref[...] = nv
        d_ref[...] = -ADAM_LR * ((nm / c1) / (jnp.sqrt(nv / c2) + ADAM_EPS) + ADAM_WD * w_ref[...])

    blk = pl.BlockSpec((tr, tc), lambda i, j: (i, j))
    return pl.pallas_call(
        body, name=name, grid=(R // tr, C // tc),
        in_specs=[pl.BlockSpec((N_DEV, tr, tc), lambda i, j: (0, i, j)), blk, blk, blk],
        out_specs=[blk] * 4, out_shape=[_sds((R, C))] * 4, compiler_params=_params(("parallel", "parallel")),
    )(parts, w, m, v)


_SMALL_ROWS = 8


def _pack_small(norm1, norm2, final, gnw, a_log, dt_bias, loss=None):
    loss = jnp.zeros((1, 128), F32) if loss is None else loss
    row3 = jnp.concatenate([gnw, a_log, dt_bias, jnp.zeros((1, 128 - 2 * GDN_HEADS), F32), loss,
                            jnp.zeros((1, D_MODEL - 3 * 128), F32)], axis=1)
    return jnp.concatenate([norm1, norm2, final, row3, jnp.zeros((_SMALL_ROWS - 4, D_MODEL), F32)], axis=0)


def _unpack_small(p):
    return (p[0:1], p[1:2], p[2], p[3:4, 0:128], p[3:4, 128:128 + GDN_HEADS], p[3:4, 128 + GDN_HEADS:128 + 2 * GDN_HEADS])


def _slabs_by_cols(g):
    r = g.shape[0]
    return g.reshape(r, N_DEV, -1).transpose(1, 0, 2)


def _cols_from_slabs(s):
    return s.transpose(1, 0, 2).reshape(s.shape[1], -1)


def kernel(x, norm1_w, w_in, conv_qkv_w, a_log, dt_bias, gdn_norm_w, w_out, norm2_w, w_up, ffn_conv_w, w_down, final_norm_w, loss_target, m_norm1_w, m_w_in, m_conv_qkv_w, m_a_log, m_dt_bias, m_gdn_norm_w, m_w_out, m_norm2_w, m_w_up, m_ffn_conv_w, m_w_down, m_final_norm_w, v_norm1_w, v_w_in, v_conv_qkv_w, v_a_log, v_dt_bias, v_gdn_norm_w, v_w_out, v_norm2_w, v_w_up, v_ffn_conv_w, v_w_down, v_final_norm_w):
    bf = lambda a: a.astype(BF16)
    me = _slot(_position())
    t_in = lambda a: a[0].T
    gw_in, g_conv_a = _all_gather([bf(t_in(w_in)), conv_qkv_w[0]], name="gather_w_in")
    w_a, w_z, w_b = _split_w_in(gw_in.reshape(-1, D_MODEL))
    late_src, _ = lax.optimization_barrier(([bf(w_out[0]), bf(t_in(w_up)), bf(w_down[0]), ffn_conv_w[0]], gw_in))
    l_send, l_recv, l_srcs, l_lands, l_token = _exchange_start(late_src, name="weights_start", broadcast=True)

    def late_weights(after):
        srcs, landed = _exchange_wait(l_send, l_recv, l_srcs, l_lands, after, name="weights_wait", broadcast=True)
        gw_out, gw_up, gw_down, g_conv_f = _with_own(landed, srcs, me, broadcast=True)
        return gw_out.reshape(D_MODEL, D_MODEL), gw_up, g_conv_f, gw_down.reshape(D_FF, D_MODEL)

    flights = {}

    def emit(group, **grads):
        if group == "in":
            slabs = dict(w_in=_merge_g_in(grads["w_a"], grads["w_z"], grads["w_b"]).reshape(N_DEV, -1, D_MODEL),
                         conv_a=_slabs_by_cols(grads["conv_a"]))
        elif group == "ffn":
            slabs = dict(w_down=grads["w_down"].reshape(N_DEV, -1, D_MODEL), w_up=grads["w_up"], conv_f=grads["conv_f"])
        else:
            slabs = {k: v.reshape(N_DEV, -1, D_MODEL) for k, v in grads.items()}
        names = list(slabs)
        *flight, token = _exchange_start([slabs[k] for k in names], name="grads_start_" + group)
        flights[group] = (names, flight)
        return token

    loss, grad_x, g = _local_step(
        x[0], loss_target[0], _behind(norm1_w, l_token), w_a, w_z, w_b, _cols_from_slabs(g_conv_a), a_log, dt_bias,
        gdn_norm_w, norm2_w, final_norm_w[None], late_weights, emit)
    small_all = _gather_direct(
        _pack_small(g["norm1"], g["norm2"], g["final"], g["gnw"], g["small"][:, 0:GDN_HEADS],
                    g["small"][:, GDN_HEADS:2 * GDN_HEADS], loss), name="gather_small")
    got = {}
    for group in ("ffn", "out", "in"):
        names, (send_sems, recv_sems, srcs, lands) = flights[group]
        srcs, landed = _exchange_wait(send_sems, recv_sems, srcs, lands, small_all, name="grads_wait_" + group)
        got.update(zip(names, _with_own(landed, srcs, me)))
    o_in = [o.T for o in _adamw(got["w_in"], t_in(w_in), t_in(m_w_in), t_in(v_w_in), name="adamw_w_in", tc=256)]
    o_out = _adamw(got["w_out"], w_out[0], m_w_out[0], v_w_out[0], name="adamw_w_out")
    o_up = [o.T for o in _adamw(got["w_up"], t_in(w_up), t_in(m_w_up), t_in(v_w_up), name="adamw_w_up", tr=176)]
    o_down = _adamw(got["w_down"], w_down[0], m_w_down[0], v_w_down[0], name="adamw_w_down", tr=176)
    o_ca = _adamw(got["conv_a"], conv_qkv_w[0], m_conv_qkv_w[0], v_conv_qkv_w[0], name="adamw_conv_a")
    o_cf = _adamw(got["conv_f"], ffn_conv_w[0], m_ffn_conv_w[0], v_ffn_conv_w[0], name="adamw_conv_f")
    o_small = _adamw(
        small_all, _pack_small(norm1_w, norm2_w, final_norm_w[None], gdn_norm_w, a_log, dt_bias),
        _pack_small(m_norm1_w, m_norm2_w, m_final_norm_w[None], m_gdn_norm_w, m_a_log, m_dt_bias),
        _pack_small(v_norm1_w, v_norm2_w, v_final_norm_w[None], v_gdn_norm_w, v_a_log, v_dt_bias), name="adamw_small")
    total_loss = o_small[0][3, 256]
    outs = [total_loss, grad_x[None]]
    for k in range(4):
        n1, n2, fin, gn, al, dt = _unpack_small(o_small[k])
        outs += [n1, o_in[k][None], o_ca[k][None], al, dt, gn, o_out[k][None], n2, o_up[k][None], o_cf[k][None], o_down[k][None], fin]
    return tuple(outs)
```

```python
import functools

import jax
import jax.numpy as jnp
from jax import lax
from jax.experimental import pallas as pl
from jax.experimental.pallas import tpu as pltpu

F32 = jnp.float32
BF16 = jnp.bfloat16

N_DEV = 8
D_MODEL = 1024
GDN_HEADS = 4
GDN_DIM = 128
GDN_WIDTH = GDN_HEADS * GDN_DIM
GDN_CONV = 4
CHUNK = 64
CHUNKS_PER_STEP = 2
DIL_HEADS = 8
DIL_DIM = 64
DIL_WIDTH = DIL_HEADS * DIL_DIM
DIL_PAIRS = DIL_HEADS // 2
DILATIONS = (1, 4, 16)
BAND = 128
D_FF = 2816
FFN_CONV = 3
EPS = 1e-6
A_COLS = 3 * GDN_WIDTH + 128
HALO = 8

ADAM_LR = 0.001
ADAM_B1 = 0.9
ADAM_B2 = 0.999
ADAM_EPS = 1e-08
ADAM_WD = 0.01
ADAM_STEP = 10

VMEM_LIMIT_BYTES = 56 * 1024 * 1024
NEG_BIG = -1e30


def _params(sem=None):
    return pltpu.CompilerParams(dimension_semantics=sem, vmem_limit_bytes=VMEM_LIMIT_BYTES)


def _sds(shape, dtype=F32):
    return jax.ShapeDtypeStruct(shape, dtype)


def _bdot(a, b):
    return jnp.dot(a.astype(BF16), b.astype(BF16), preferred_element_type=F32)


def _bdot_nt(a, b):
    return lax.dot_general(a.astype(BF16), b.astype(BF16), (((1,), (1,)), ((), ())), preferred_element_type=F32)


def _bdot_tn(a, b):
    return lax.dot_general(a.astype(BF16), b.astype(BF16), (((0,), (0,)), ((), ())), preferred_element_type=F32)


def _split(a):
    hi = a.astype(BF16)
    lo = (a - hi.astype(F32)).astype(BF16)
    return hi, lo


def _dot3(a, b, dims):
    ah, al = _split(a)
    bh, bl = _split(b)
    d = functools.partial(lax.dot_general, dimension_numbers=(dims, ((), ())), preferred_element_type=F32)
    return d(ah, bh) + (d(al, bh) + d(ah, bl))


def _exact_tri_dot(tri, g):
    g1 = g.astype(BF16)
    r1 = g - g1.astype(F32)
    g2 = r1.astype(BF16)
    g3 = (r1 - g2.astype(F32)).astype(BF16)
    t = tri.astype(BF16)
    d = functools.partial(jnp.dot, preferred_element_type=F32)
    return d(t, g1) + (d(t, g2) + d(t, g3))


def _sigmoid(x):
    return 1.0 / (1.0 + jnp.exp(-x))


def _dsilu(x, sg):
    return sg * (1.0 + x * (1.0 - sg))


def _mm(a, b, *, name, ta=False, tb=False, res=None, out_dtype=F32, tm=512, tn=512, tk=512):
    if ta:
        K, M = a.shape
    else:
        M, K = a.shape
    if tb:
        N, Kb = b.shape
    else:
        Kb, N = b.shape
    assert K == Kb, (a.shape, b.shape)
    tm, tn, tk = min(tm, M), min(tn, N), min(tk, K)
    assert M % tm == 0 and N % tn == 0 and K % tk == 0, (name, M, N, K, tm, tn, tk)
    nk = K // tk
    dims = (((0 if ta else 1,), (1 if tb else 0,)), ((), ()))
    has_res = res is not None

    def body(*refs):
        if has_res:
            a_ref, b_ref, r_ref, o_ref, acc_ref = refs
        else:
            a_ref, b_ref, o_ref, acc_ref = refs
        k = pl.program_id(2)
        part = lax.dot_general(a_ref[...].astype(BF16), b_ref[...].astype(BF16), dims, preferred_element_type=F32)

        @pl.when(k == 0)
        def _():
            acc_ref[...] = part

        @pl.when(k > 0)
        def _():
            acc_ref[...] += part

        @pl.when(k == nk - 1)
        def _():
            r = acc_ref[...]
            if has_res:
                r = r + r_ref[...]
            o_ref[...] = r.astype(out_dtype)

    a_spec = pl.BlockSpec((tk, tm), lambda i, j, k: (k, i)) if ta else pl.BlockSpec((tm, tk), lambda i, j, k: (i, k))
    b_spec = pl.BlockSpec((tn, tk), lambda i, j, k: (j, k)) if tb else pl.BlockSpec((tk, tn), lambda i, j, k: (k, j))
    o_spec = pl.BlockSpec((tm, tn), lambda i, j, k: (i, j))
    in_specs = [a_spec, b_spec] + ([o_spec] if has_res else [])
    args = (a, b) + ((res,) if has_res else ())
    return pl.pallas_call(
        body, name=name, grid=(M // tm, N // tn, nk), in_specs=in_specs, out_specs=o_spec,
        out_shape=_sds((M, N), out_dtype), scratch_shapes=[pltpu.VMEM((tm, tn), F32)],
        compiler_params=_params(("parallel", "parallel", "arbitrary")),
    )(*args)


def _rms_fwd(x, w, *, name, tm=256):
    S, D = x.shape

    def body(x_ref, w_ref, h_ref):
        xv = x_ref[...]
        r = lax.rsqrt(jnp.mean(xv * xv, axis=-1, keepdims=True) + EPS)
        h_ref[...] = (xv * r * w_ref[...]).astype(BF16)

    return pl.pallas_call(
        body, name=name, grid=(S // tm,),
        in_specs=[pl.BlockSpec((tm, D), lambda i: (i, 0)), pl.BlockSpec((1, D), lambda i: (0, 0))],
        out_specs=pl.BlockSpec((tm, D), lambda i: (i, 0)), out_shape=_sds((S, D), BF16),
        compiler_params=_params(("parallel",)),
    )(x, w)


def _rms_bwd(dh, x, w, res, *, name, tm=256):
    S, D = x.shape

    def body(dh_ref, x_ref, w_ref, res_ref, dx_ref, dw_ref):
        i = pl.program_id(0)
        xv = x_ref[...]
        g = dh_ref[...]
        r = lax.rsqrt(jnp.mean(xv * xv, axis=-1, keepdims=True) + EPS)
        xh = xv * r
        gw = g * w_ref[...]
        dx_ref[...] = res_ref[...] + r * (gw - xh * jnp.mean(gw * xh, axis=-1, keepdims=True))
        part = jnp.sum(g * xh, axis=0, keepdims=True)

        @pl.when(i == 0)
        def _():
            dw_ref[...] = part

        @pl.when(i > 0)
        def _():
            dw_ref[...] += part

    row = pl.BlockSpec((tm, D), lambda i: (i, 0))
    one = pl.BlockSpec((1, D), lambda i: (0, 0))
    return pl.pallas_call(
        body, name=name, grid=(S // tm,), in_specs=[row, row, one, row], out_specs=[row, one],
        out_shape=[_sds((S, D)), _sds((1, D))], compiler_params=_params(("arbitrary",)),
    )(dh, x, w, res)


def _loss_head(x2, w, tgt, *, name, tm=256):
    S, D = x2.shape

    def body(x_ref, w_ref, t_ref, dx_ref, dxb_ref, dw_ref, loss_ref):
        i = pl.program_id(0)
        xv = x_ref[...]
        wv = w_ref[...]
        r = lax.rsqrt(jnp.mean(xv * xv, axis=-1, keepdims=True) + EPS)
        xh = xv * r
        err = xh * wv - t_ref[...]
        lrow = jnp.sum(err * err, axis=-1, keepdims=True)
        lsum = jnp.sum(lrow, axis=0, keepdims=True) * (0.5 / D)
        g = err * (1.0 / D)
        gw = g * wv
        dx = r * (gw - xh * jnp.mean(gw * xh, axis=-1, keepdims=True))
        dx_ref[...] = dx
        dxb_ref[...] = dx.astype(BF16)
        part = jnp.sum(g * xh, axis=0, keepdims=True)
        lpart = jnp.broadcast_to(lsum, (1, 128))

        @pl.when(i == 0)
        def _():
            dw_ref[...] = part
            loss_ref[...] = lpart

        @pl.when(i > 0)
        def _():
            dw_ref[...] += part
            loss_ref[...] += lpart

    row = pl.BlockSpec((tm, D), lambda i: (i, 0))
    one = pl.BlockSpec((1, D), lambda i: (0, 0))
    return pl.pallas_call(
        body, name=name, grid=(S // tm,), in_specs=[row, one, row],
        out_specs=[row, row, one, pl.BlockSpec((1, 128), lambda i: (0, 0))],
        out_shape=[_sds((S, D)), _sds((S, D), BF16), _sds((1, D)), _sds((1, 128))], compiler_params=_params(("arbitrary",)),
    )(x2, w, tgt)


def _shifted(x, start, n):
    aligned = -(-start // HALO) * HALO
    assert aligned + n <= x.shape[0], (start, n, x.shape)
    return (x if aligned == start else pltpu.roll(x, aligned - start, axis=0))[aligned:aligned + n]


def _conv_rows(prev, cur, w, taps):
    n = cur.shape[0]
    xs = jnp.concatenate([prev, cur], axis=0)
    base = HALO - (taps - 1)
    out = _shifted(xs, base, n) * w[0:1]
    for i in range(1, taps):
        out = out + _shifted(xs, base + i, n) * w[i:i + 1]
    return out


def _conv_rows_bwd(cur_d, next_d, prev_x, cur_x, w, taps):
    n = cur_d.shape[0]
    ds = jnp.concatenate([cur_d, next_d], axis=0)
    dx = _shifted(ds, taps - 1, n) * w[0:1]
    for i in range(1, taps):
        dx = dx + _shifted(ds, taps - 1 - i, n) * w[i:i + 1]
    xs = jnp.concatenate([prev_x, cur_x], axis=0)
    base = HALO - (taps - 1)
    dws = [jnp.sum(cur_d * _shifted(xs, base + i, n), axis=0, keepdims=True) for i in range(taps)]
    return dx, jnp.concatenate(dws, axis=0)


def _halo_specs(tm, width, col, nblk):
    per = tm // HALO
    prev = pl.BlockSpec((HALO, width), lambda i, *_: (jnp.maximum(i * per - 1, 0), col))
    nxt = pl.BlockSpec((HALO, width), lambda i, *_: (jnp.minimum((i + 1) * per, nblk * per - 1), col))
    return prev, nxt


def _softplus(x):
    return jnp.maximum(x, 0.0) + jnp.log1p(jnp.exp(-jnp.abs(x)))


def _chunk_tri(tm, upper=False):
    r = lax.broadcasted_iota(jnp.int32, (tm, tm), 0)
    c = lax.broadcasted_iota(jnp.int32, (tm, tm), 1)
    same = lax.div(r, CHUNK) == lax.div(c, CHUNK)
    order = (c >= r) if upper else (c <= r)
    return jnp.where(same & order, 1.0, 0.0)


def _gdn_prep_fwd(proj_a, conv_w, a_log, dt_bias, *, name, tm=256):
    S = proj_a.shape[0]
    nblk = S // tm
    W3 = 3 * GDN_WIDTH

    def body(cur_ref, prev_ref, ba_ref, cw_ref, al_ref, dt_ref, qn_ref, kn_ref, v_ref, gcb_ref, bb_ref):
        i = pl.program_id(0)
        prev = jnp.where(i > 0, prev_ref[...], 0.0)
        c = _conv_rows(prev, cur_ref[...], cw_ref[...], GDN_CONV)
        a = c * _sigmoid(c)
        ba = ba_ref[...]
        lane = lax.broadcasted_iota(jnp.int32, (tm, 128), 1)
        g4 = jnp.zeros((tm, 128), F32)
        for h in range(GDN_HEADS):
            sl = slice(GDN_DIM * h, GDN_DIM * (h + 1))
            qh = a[:, GDN_DIM * h:GDN_DIM * (h + 1)]
            kh = a[:, GDN_WIDTH + GDN_DIM * h:GDN_WIDTH + GDN_DIM * (h + 1)]
            qn_ref[:, sl] = qh * (lax.rsqrt(jnp.sum(qh * qh, axis=-1, keepdims=True) + EPS) * (GDN_DIM ** -0.5))
            kn_ref[:, sl] = kh * lax.rsqrt(jnp.sum(kh * kh, axis=-1, keepdims=True) + EPS)
            beta = _sigmoid(ba[:, h:h + 1])
            bb_ref[:, sl] = jnp.broadcast_to(beta, (tm, GDN_DIM))
            g = -jnp.exp(al_ref[0:1, h:h + 1]) * _softplus(ba[:, GDN_HEADS + h:GDN_HEADS + h + 1] + dt_ref[0:1, h:h + 1])
            g4 = jnp.where(lane == h, g, g4)
        v_ref[...] = a[:, 2 * GDN_WIDTH:]
        gc = _exact_tri_dot(_chunk_tri(tm), g4)
        for h in range(GDN_HEADS):
            gcb_ref[:, GDN_DIM * h:GDN_DIM * (h + 1)] = jnp.broadcast_to(gc[:, h:h + 1], (tm, GDN_DIM))

    prev_spec, _ = _halo_specs(tm, W3, 0, nblk)
    row = pl.BlockSpec((tm, GDN_WIDTH), lambda i: (i, 0))
    small = lambda a: pl.BlockSpec(a.shape, lambda i: (0, 0))
    return pl.pallas_call(
        body, name=name, grid=(nblk,),
        in_specs=[pl.BlockSpec((tm, W3), lambda i: (i, 0)), prev_spec,
                  pl.BlockSpec((tm, 128), lambda i: (i, W3 // 128)), small(conv_w), small(a_log), small(dt_bias)],
        out_specs=[row] * 5, out_shape=[_sds((S, GDN_WIDTH))] * 5, compiler_params=_params(("parallel",)),
    )(proj_a, proj_a, proj_a, conv_w, a_log, dt_bias)


GDN_STACK = GDN_HEADS * CHUNK


def _stack(ref, rows):
    return jnp.concatenate([ref[rows, GDN_DIM * h:GDN_DIM * (h + 1)] for h in range(GDN_HEADS)], axis=0)


def _unstack_to(ref, rows, x):
    for h in range(GDN_HEADS):
        ref[rows, GDN_DIM * h:GDN_DIM * (h + 1)] = x[CHUNK * h:CHUNK * (h + 1)].astype(ref.dtype)


def _stack_masks():
    r = lax.broadcasted_iota(jnp.int32, (GDN_STACK, GDN_STACK), 0)
    c = lax.broadcasted_iota(jnp.int32, (GDN_STACK, GDN_STACK), 1)
    same = (r & -CHUNK) == (c & -CHUNK)
    return same & (r >= c), same & (r > c), r == c


def _stack_decay(gs, bs, incl):
    g2 = jnp.concatenate([gs, gs], axis=1)
    diff = g2 - g2.T
    dec = jnp.where(incl, jnp.exp(jnp.where(incl, diff, 0.0)), 0.0)
    return dec, jnp.concatenate([bs, bs], axis=1).T


def _head_mask():
    r = lax.broadcasted_iota(jnp.int32, (GDN_STACK, GDN_WIDTH), 0)
    c = lax.broadcasted_iota(jnp.int32, (GDN_STACK, GDN_WIDTH), 1)
    return (r & -CHUNK) * (GDN_DIM // CHUNK) == (c & -GDN_DIM)


def _head_spread(x):
    return jnp.where(_head_mask(), jnp.concatenate([x] * GDN_HEADS, axis=1), 0.0)


def _head_diag(x):
    xm = jnp.where(_head_mask(), x, 0.0)
    out = xm[:, 0:GDN_DIM]
    for h in range(1, GDN_HEADS):
        out = out + xm[:, GDN_DIM * h:GDN_DIM * (h + 1)]
    return out


def _last_rows(gs, n):
    return jnp.concatenate([jnp.broadcast_to(gs[CHUNK * (h + 1) - 1:CHUNK * (h + 1)], (n, GDN_DIM)) for h in range(GDN_HEADS)], axis=0)


def _gdn_chunk_fwd(qn, kn, v, gcb, bb, *, name):
    S = qn.shape[0]

    def body(qn_ref, kn_ref, v_ref, gcb_ref, bb_ref, uv_ref, wk_ref, at_ref, t_ref):
        incl, strict, diag = _stack_masks()
        for c in range(CHUNKS_PER_STEP):
            rows = slice(CHUNK * c, CHUNK * (c + 1))
            srows = slice(GDN_STACK * c, GDN_STACK * (c + 1))
            q, k, vv, gs, bs = [_stack(r, rows) for r in (qn_ref, kn_ref, v_ref, gcb_ref, bb_ref)]
            dec, bt = _stack_decay(gs, bs, incl)
            p = -jnp.where(strict, dec * _bdot_nt(k, k) * bt, 0.0)
            t = jnp.where(diag, 1.0, 0.0) + p
            for _ in range(5):
                p = _bdot(p, p)
                t = t + _bdot(t, p)
            sol = _dot3(t, jnp.concatenate([vv, jnp.exp(gs) * k], axis=1), ((1,), (0,)))
            _unstack_to(uv_ref, rows, sol[:, :GDN_DIM])
            _unstack_to(wk_ref, rows, sol[:, GDN_DIM:])
            at_ref[srows, :] = dec * _bdot_nt(q, k) * bt
            t_ref[srows, :] = t

    step = CHUNKS_PER_STEP * CHUNK
    row = pl.BlockSpec((step, GDN_WIDTH), lambda n: (n, 0))
    sq = pl.BlockSpec((CHUNKS_PER_STEP * GDN_STACK, GDN_STACK), lambda n: (n, 0))
    nsq = S // CHUNK * GDN_STACK
    return pl.pallas_call(
        body, name=name, grid=(S // step,), in_specs=[row] * 5, out_specs=[row, row, sq, sq],
        out_shape=[_sds((S, GDN_WIDTH)), _sds((S, GDN_WIDTH)), _sds((nsq, GDN_STACK)), _sds((nsq, GDN_STACK))],
        compiler_params=_params(("parallel",)),
    )(qn, kn, v, gcb, bb)


def _gdn_scan_fwd(uv, wk, at, qn, kn, gcb, bb, proj_z, gnw, *, name):
    S = uv.shape[0]
    nc = S // CHUNK
    rows = slice(None)

    def body(uv_ref, wk_ref, at_ref, qn_ref, kn_ref, gcb_ref, bb_ref, z_ref, gnw_ref, o_ref, u_ref, sp_ref, oa_ref, st_ref):
        n = pl.program_id(0)

        @pl.when(n == 0)
        def _():
            st_ref[...] = jnp.zeros_like(st_ref)

        st = st_ref[...]
        sp_ref[...] = st
        uv, wk, q, k, gs, bs, z = [_stack(r, rows) for r in (uv_ref, wk_ref, qn_ref, kn_ref, gcb_ref, bb_ref, z_ref)]
        u = uv - _bdot(_head_spread(wk), st)
        o = _bdot(_head_spread(q * jnp.exp(gs)), st) + _bdot(at_ref[...], u)
        ke = k * jnp.exp(_last_rows(gs, CHUNK) - gs) * bs
        st_ref[...] = jnp.exp(_last_rows(gs, GDN_DIM)) * st + _bdot_tn(_head_spread(ke), u)
        _unstack_to(u_ref, rows, u)
        _unstack_to(o_ref, rows, o)
        r = lax.rsqrt(jnp.mean(o * o, axis=-1, keepdims=True) + EPS)
        oa = o * r * gnw_ref[...] * (z * _sigmoid(z))
        oa_ref[...] = jnp.concatenate([oa[CHUNK * h:CHUNK * (h + 1)] for h in range(GDN_HEADS)], axis=1).astype(BF16)

    row = pl.BlockSpec((CHUNK, GDN_WIDTH), lambda n: (n, 0))
    sq = pl.BlockSpec((GDN_STACK, GDN_STACK), lambda n: (n, 0))
    return pl.pallas_call(
        body, name=name, grid=(nc,),
        in_specs=[row, row, sq, row, row, row, row, row, pl.BlockSpec((1, GDN_DIM), lambda n: (0, 0))],
        out_specs=[row, row, pl.BlockSpec((GDN_WIDTH, GDN_DIM), lambda n: (n, 0)), row],
        out_shape=[_sds((S, GDN_WIDTH)), _sds((S, GDN_WIDTH)), _sds((nc * GDN_WIDTH, GDN_DIM)), _sds((S, 2 * GDN_WIDTH), BF16)],
        scratch_shapes=[pltpu.VMEM((GDN_WIDTH, GDN_DIM), F32)],
        compiler_params=_params(("arbitrary",)),
    )(uv, wk, at, qn, kn, gcb, bb, proj_z, gnw)


def _gdn_scan_bwd(d_oab, o, proj_z, gnw, sp, u, wk, at, qn, kn, gcb, bb, *, name):
    S = o.shape[0]
    nc = S // CHUNK
    rows = slice(None)

    def body(do_ref, o_ref, z_ref, gnw_ref, sp_ref, u_ref, wk_ref, at_ref, qn_ref, kn_ref, gcb_ref, bb_ref,
             dz_ref, dgn_ref, du_ref, dwk_ref, dat_ref, dqd_ref, dke_ref, dgl_ref, ds_ref):
        n = pl.program_id(0)

        @pl.when(n == 0)
        def _():
            ds_ref[...] = jnp.zeros_like(ds_ref)
            dgn_ref[...] = jnp.zeros_like(dgn_ref)

        d_oa, oo, z, uu, wk, q, k, gs, bs = [
            _stack(r, rows) for r in (do_ref, o_ref, z_ref, u_ref, wk_ref, qn_ref, kn_ref, gcb_ref, bb_ref)]
        gw = gnw_ref[...]
        sg = _sigmoid(z)
        r = lax.rsqrt(jnp.mean(oo * oo, axis=-1, keepdims=True) + EPS)
        xh = oo * r
        dy = d_oa * (z * sg)
        _unstack_to(dz_ref, rows, d_oa * (xh * gw) * _dsilu(z, sg))
        dgn_ref[...] += jnp.sum(dy * xh, axis=0, keepdims=True)
        dxh = dy * gw
        do = r * (dxh - xh * jnp.mean(dxh * xh, axis=-1, keepdims=True))

        st = sp_ref[...]
        dst = ds_ref[...]
        glast = _last_rows(gs, CHUNK)
        ge = jnp.exp(_last_rows(gs, GDN_DIM))
        qd = q * jnp.exp(gs)
        ke = k * jnp.exp(glast - gs) * bs
        _unstack_to(dqd_ref, rows, _head_diag(_bdot_nt(do, st)))
        dat_ref[...] = _bdot_nt(do, uu)
        du = _bdot_tn(at_ref[...], do) + _bdot(_head_spread(ke), dst)
        _unstack_to(dke_ref, rows, _head_diag(_bdot_nt(uu, dst)))
        prod = dst * st
        for h in range(GDN_HEADS):
            blk = prod[GDN_DIM * h:GDN_DIM * (h + 1)]
            dge = jnp.sum(jnp.sum(blk, axis=1, keepdims=True), axis=0, keepdims=True)
            dgl_ref[0, :, GDN_DIM * h:GDN_DIM * (h + 1)] = jnp.broadcast_to(dge * ge[GDN_DIM * h:GDN_DIM * h + 1], (8, GDN_DIM))
        ds_ref[...] = _bdot_tn(_head_spread(qd), do) + ge * dst - _bdot_tn(_head_spread(wk), du)
        _unstack_to(du_ref, rows, du)
        _unstack_to(dwk_ref, rows, -_head_diag(_bdot_nt(du, st)))

    rev = lambda n: (nc - 1 - n, 0)
    row = pl.BlockSpec((CHUNK, GDN_WIDTH), rev)
    sq = pl.BlockSpec((GDN_STACK, GDN_STACK), rev)
    one = pl.BlockSpec((1, GDN_DIM), lambda n: (0, 0))
    return pl.pallas_call(
        body, name=name, grid=(nc,),
        in_specs=[row, row, row, one, pl.BlockSpec((GDN_WIDTH, GDN_DIM), rev), row, row, sq, row, row, row, row],
        out_specs=[row, one, row, row, sq, row, row, pl.BlockSpec((1, 8, GDN_WIDTH), lambda n: (nc - 1 - n, 0, 0))],
        out_shape=[_sds((S, GDN_WIDTH), BF16), _sds((1, GDN_DIM)), _sds((S, GDN_WIDTH)), _sds((S, GDN_WIDTH)),
                   _sds((nc * GDN_STACK, GDN_STACK)), _sds((S, GDN_WIDTH)), _sds((S, GDN_WIDTH)), _sds((nc, 8, GDN_WIDTH))],
        scratch_shapes=[pltpu.VMEM((GDN_WIDTH, GDN_DIM), F32)],
        compiler_params=_params(("arbitrary",)),
    )(d_oab, o, proj_z, gnw, sp, u, wk, at, qn, kn, gcb, bb)


def _gdn_chunk_bwd(qn, kn, gcb, bb, tmat, uv, wk, du, dwk, dat, dqd, dke, dgl, *, name):
    S = qn.shape[0]

    def body(qn_ref, kn_ref, gcb_ref, bb_ref, t_ref, uv_ref, wk_ref, du_ref, dwk_ref, dat_ref, dqd_ref, dke_ref,
             dgl_ref, dq_ref, dk_ref, dv_ref, dg_ref, dbeta_ref):
        incl, strict, _ = _stack_masks()
        lane = lax.broadcasted_iota(jnp.int32, (CHUNK, 128), 1)
        rowi = lax.broadcasted_iota(jnp.int32, (CHUNK, 1), 0)
        rsum = lambda x: jnp.sum(x, axis=-1, keepdims=True)
        for c in range(CHUNKS_PER_STEP):
            rows = slice(CHUNK * c, CHUNK * (c + 1))
            srows = slice(GDN_STACK * c, GDN_STACK * (c + 1))
            q, k, gs, bs, uv, wk, du, dwk, dqd, dke = [
                _stack(r, rows) for r in (qn_ref, kn_ref, gcb_ref, bb_ref, uv_ref, wk_ref, du_ref, dwk_ref, dqd_ref, dke_ref)]
            dec, bt = _stack_decay(gs, bs, incl)
            kk = _bdot_nt(k, k)
            qk = _bdot_nt(q, k)
            d_rhs = _dot3(t_ref[srows, :], jnp.concatenate([du, dwk], axis=1), ((0,), (0,)))
            sol = jnp.concatenate([uv, wk], axis=1)
            d_l = jnp.where(strict, -_dot3(d_rhs, sol, ((1,), (1,))), 0.0)
            d_a = jnp.where(incl, dat_ref[srows, :], 0.0)
            gam = jnp.exp(gs)
            e = jnp.exp(_last_rows(gs, CHUNK) - gs)
            d_gk = d_rhs[:, GDN_DIM:]
            ml = d_l * dec * bt
            ma = d_a * dec * bt
            _unstack_to(dq_ref, rows, _bdot(ma, k) + dqd * gam)
            _unstack_to(dk_ref, rows, _bdot(ml + ml.T, k) + _bdot_tn(ma, q) + d_gk * gam + dke * (e * bs))
            _unstack_to(dv_ref, rows, d_rhs[:, :GDN_DIM])
            wb = d_l * dec * kk + d_a * dec * qk
            ew = wb * bt
            s_ke = rsum(dke * k * (e * bs))
            dbeta = rsum(wb.T) + rsum(dke * k * e)
            dgc = rsum(ew) - rsum(ew.T) + rsum(dqd * q * gam) + rsum(d_gk * k * gam) - s_ke
            dgc4 = jnp.zeros((CHUNK, 128), F32)
            db4 = jnp.zeros((CHUNK, 128), F32)
            for h in range(GDN_HEADS):
                hr = slice(CHUNK * h, CHUNK * (h + 1))
                tail = jnp.sum(s_ke[hr], axis=0, keepdims=True) + dgl_ref[c, 0:1, GDN_DIM * h:GDN_DIM * h + 1]
                dgc4 = jnp.where(lane == h, dgc[hr] + jnp.where(rowi == CHUNK - 1, tail, 0.0), dgc4)
                db4 = jnp.where(lane == h, dbeta[hr], db4)
            dg_ref[rows, :] = _exact_tri_dot(_chunk_tri(CHUNK, upper=True), dgc4)
            dbeta_ref[rows, :] = db4

    step = CHUNKS_PER_STEP * CHUNK
    row = pl.BlockSpec((step, GDN_WIDTH), lambda n: (n, 0))
    sq = pl.BlockSpec((CHUNKS_PER_STEP * GDN_STACK, GDN_STACK), lambda n: (n, 0))
    col = pl.BlockSpec((step, 128), lambda n: (n, 0))
    return pl.pallas_call(
        body, name=name, grid=(S // step,),
        in_specs=[row] * 4 + [sq, row, row, row, row, sq, row, row,
                              pl.BlockSpec((CHUNKS_PER_STEP, 8, GDN_WIDTH), lambda n: (n, 0, 0))],
        out_specs=[row, row, row, col, col],
        out_shape=[_sds((S, GDN_WIDTH))] * 3 + [_sds((S, 128))] * 2, compiler_params=_params(("parallel",)),
    )(qn, kn, gcb, bb, tmat, uv, wk, du, dwk, dat, dqd, dke, dgl)


def _gdn_prep_bwd(dqn, dkn, dv, dg, dbeta, proj_a, conv_w, a_log, dt_bias, *, name, tm=256):
    S = proj_a.shape[0]
    nblk = S // tm
    W3 = 3 * GDN_WIDTH

    def body(dqn_ref, dkn_ref, dv_ref, dg_ref, dbeta_ref, cur_ref, prev_ref, ba_ref, cw_ref, al_ref, dt_ref,
             dc_ref, dba_ref, sm_ref):
        i = pl.program_id(0)
        prev = jnp.where(i > 0, prev_ref[...], 0.0)
        c = _conv_rows(prev, cur_ref[...], cw_ref[...], GDN_CONV)
        sg = _sigmoid(c)
        a = c * sg
        dsl = _dsilu(c, sg)
        ba = ba_ref[...]
        lane = lax.broadcasted_iota(jnp.int32, (tm, 128), 1)
        lane1 = lax.broadcasted_iota(jnp.int32, (1, 128), 1)
        dba = jnp.zeros((tm, 128), F32)
        sm = jnp.zeros((1, 128), F32)
        for h in range(GDN_HEADS):
            sl = slice(GDN_DIM * h, GDN_DIM * (h + 1))
            ks = slice(GDN_WIDTH + GDN_DIM * h, GDN_WIDTH + GDN_DIM * (h + 1))
            qh, kh = a[:, sl], a[:, ks]
            rq = lax.rsqrt(jnp.sum(qh * qh, axis=-1, keepdims=True) + EPS)
            rk = lax.rsqrt(jnp.sum(kh * kh, axis=-1, keepdims=True) + EPS)
            qhat, khat = qh * rq, kh * rk
            dyq = dqn_ref[:, sl] * (GDN_DIM ** -0.5)
            dyk = dkn_ref[:, sl]
            dq = rq * (dyq - qhat * jnp.sum(dyq * qhat, axis=-1, keepdims=True))
            dk = rk * (dyk - khat * jnp.sum(dyk * khat, axis=-1, keepdims=True))
            dc_ref[:, sl] = dq * dsl[:, sl]
            dc_ref[:, ks] = dk * dsl[:, ks]
            beta = _sigmoid(ba[:, h:h + 1])
            db = dbeta_ref[:, h:h + 1] * beta * (1.0 - beta)
            aneg = -jnp.exp(al_ref[0:1, h:h + 1])
            xa = ba[:, GDN_HEADS + h:GDN_HEADS + h + 1] + dt_ref[0:1, h:h + 1]
            dgh = dg_ref[:, h:h + 1]
            dxa = dgh * aneg * _sigmoid(xa)
            dba = jnp.where(lane == h, db, dba)
            dba = jnp.where(lane == GDN_HEADS + h, dxa, dba)
            d_alog = jnp.sum(dgh * _softplus(xa), axis=0, keepdims=True) * aneg
            sm = jnp.where(lane1 == h, d_alog, sm)
            sm = jnp.where(lane1 == GDN_HEADS + h, jnp.sum(dxa, axis=0, keepdims=True), sm)
        vs = slice(2 * GDN_WIDTH, W3)
        dc_ref[:, vs] = dv_ref[...] * dsl[:, vs]
        dba_ref[...] = dba

        @pl.when(i == 0)
        def _():
            sm_ref[...] = sm

        @pl.when(i > 0)
        def _():
            sm_ref[...] += sm

    prev_spec, _ = _halo_specs(tm, W3, 0, nblk)
    row = pl.BlockSpec((tm, GDN_WIDTH), lambda i: (i, 0))
    col = pl.BlockSpec((tm, 128), lambda i: (i, 0))
    small = lambda a: pl.BlockSpec(a.shape, lambda i: (0, 0))
    return pl.pallas_call(
        body, name=name, grid=(nblk,),
        in_specs=[row, row, row, col, col, pl.BlockSpec((tm, W3), lambda i: (i, 0)), prev_spec,
                  pl.BlockSpec((tm, 128), lambda i: (i, W3 // 128)), small(conv_w), small(a_log), small(dt_bias)],
        out_specs=[pl.BlockSpec((tm, W3), lambda i: (i, 0)), col, pl.BlockSpec((1, 128), lambda i: (0, 0))],
        out_shape=[_sds((S, W3)), _sds((S, 128)), _sds((1, 128))], compiler_params=_params(("arbitrary",)),
    )(dqn, dkn, dv, dg, dbeta, proj_a, proj_a, proj_a, conv_w, a_log, dt_bias)


def _gdn_conv_bwd(dc, dba, proj_a, conv_w, *, name, tm=256):
    S = proj_a.shape[0]
    nblk = S // tm
    W3 = 3 * GDN_WIDTH

    def body(dc_ref, dnext_ref, dba_ref, cur_ref, prev_ref, cw_ref, da_ref, dcw_ref):
        i = pl.program_id(0)
        prev = jnp.where(i > 0, prev_ref[...], 0.0)
        nxt = jnp.where(i < nblk - 1, dnext_ref[...], 0.0)
        dx, dw = _conv_rows_bwd(dc_ref[...], nxt, prev, cur_ref[...], cw_ref[...], GDN_CONV)
        da_ref[:, 0:W3] = dx.astype(BF16)
        da_ref[:, W3:] = dba_ref[...].astype(BF16)

        @pl.when(i == 0)
        def _():
            dcw_ref[...] = dw

        @pl.when(i > 0)
        def _():
            dcw_ref[...] += dw

    prev_spec, next_spec = _halo_specs(tm, W3, 0, nblk)
    wide = pl.BlockSpec((tm, W3), lambda i: (i, 0))
    return pl.pallas_call(
        body, name=name, grid=(nblk,),
        in_specs=[wide, next_spec, pl.BlockSpec((tm, 128), lambda i: (i, 0)), wide, prev_spec,
                  pl.BlockSpec(conv_w.shape, lambda i: (0, 0))],
        out_specs=[pl.BlockSpec((tm, A_COLS), lambda i: (i, 0)), pl.BlockSpec(conv_w.shape, lambda i: (0, 0))],
        out_shape=[_sds((S, A_COLS), BF16), _sds(conv_w.shape)], compiler_params=_params(("arbitrary",)),
    )(dc, dc, dba, proj_a, proj_a, conv_w)


def _band_mask(nk):
    i = lax.broadcasted_iota(jnp.int32, (2 * BAND, nk), 0) & (BAND - 1)
    j = lax.broadcasted_iota(jnp.int32, (2 * BAND, nk), 1)
    if nk == BAND:
        return j <= i
    return (j >= i) & (j <= i + BAND)


def _stack_heads(x, lo):
    return jnp.concatenate([jnp.where(lo, x, 0.0), jnp.where(lo, 0.0, x)], axis=0)


def _stack_cols(x):
    return jnp.concatenate([x[:, 0:1], x[:, DIL_DIM:DIL_DIM + 1]], axis=0)


def _unstack(x, lo):
    return jnp.where(lo, x[0:BAND], x[BAND:2 * BAND])


def _rows(start, size, stride):
    return pl.ds(start, size) if stride == 1 else pl.ds(start, size, stride=stride)


ATTN_LANES = 4


def _attn_blocks(S, visit_many, lanes=ATTN_LANES):
    for d in DILATIONS:
        nb = S // (d * BAND)
        if d == 1:
            half = nb // 2
            visit_many(d, [(0, 0, True), (0, half, False)])

            def pair(n, c):
                visit_many(1, [(0, n, False), (0, n + half, False)])
                return c
            lax.fori_loop(1, half, pair, 0)
        elif nb > 1:
            for r0 in range(0, d, lanes):
                visit_many(d, [(r0 + t, 0, True) for t in range(lanes)])

                def column(n, c, d=d, r0=r0):
                    visit_many(d, [(r0 + t, n, False) for t in range(lanes)])
                    return c
                lax.fori_loop(1, nb, column, 0)
        else:
            def group(g, c, d=d):
                visit_many(d, [(g * lanes + t, 0, True) for t in range(lanes)])
                return c
            lax.fori_loop(0, d // lanes, group, 0)


def _attn_fwd(proj_b, oab, *, name):
    S = proj_b.shape[0]
    scale = DIL_DIM ** -0.5

    def body(q_ref, k_ref, v_ref, oab_in_ref, ob_ref, lse_ref, m_ref, l_ref, acc_ref):
        del oab_in_ref
        lane = lax.broadcasted_iota(jnp.int32, (BAND, 128), 1)
        lo = lane < DIL_DIM
        m_ref[...] = jnp.full_like(m_ref, NEG_BIG)
        l_ref[...] = jnp.zeros_like(l_ref)
        acc_ref[...] = jnp.zeros_like(acc_ref)

        def load(d, r, n, first):
            nk = BAND if first else 2 * BAND
            qrows = _rows(r + n * (BAND * d), BAND, d)
            krows = _rows(r if first else r + (n - 1) * (BAND * d), nk, d)
            return dict(nk=nk, qrows=qrows, q=q_ref[qrows, :] * scale, k=k_ref[krows, :].astype(BF16),
                        v=v_ref[krows, :].astype(BF16), m=m_ref[qrows, :], l=l_ref[qrows, :], acc=acc_ref[qrows, :])

        def compute(b):
            q, k, v = b["q"], b["k"], b["v"]
            s = jnp.where(_band_mask(b["nk"]), _bdot_nt(_stack_heads(q, lo), k), NEG_BIG)
            m_old = _stack_cols(b["m"])
            m_new = jnp.maximum(m_old, jnp.max(s, axis=-1, keepdims=True))
            p = jnp.exp(s - m_new)
            alpha = _unstack(jnp.exp(m_old - m_new), lo)
            l_new = alpha * b["l"] + _unstack(jnp.sum(p, axis=-1, keepdims=True), lo)
            return _unstack(m_new, lo), l_new, alpha * b["acc"] + _unstack(_bdot(p, v), lo)

        def visit_many(d, blocks):
            loaded = [load(d, *blk) for blk in blocks]
            done = [compute(b) for b in loaded]
            for b, (m_new, l_new, acc_new) in zip(loaded, done):
                m_ref[b["qrows"], :] = m_new
                l_ref[b["qrows"], :] = l_new
                acc_ref[b["qrows"], :] = acc_new

        _attn_blocks(S, visit_many)
        ob_ref[...] = (acc_ref[...] / l_ref[...]).astype(BF16)
        lse_ref[...] = m_ref[...] + jnp.log(l_ref[...])

    part = lambda t: pl.BlockSpec((S, 128), lambda p: (0, 3 * p + t))
    return pl.pallas_call(
        body, name=name, grid=(DIL_PAIRS,),
        in_specs=[part(0), part(1), part(2), pl.BlockSpec(memory_space=pl.ANY)],
        out_specs=[pl.BlockSpec((S, 128), lambda p: (0, GDN_WIDTH // 128 + p)), pl.BlockSpec((S, 128), lambda p: (0, p))],
        out_shape=[_sds(oab.shape, BF16), _sds((S, DIL_WIDTH))],
        scratch_shapes=[pltpu.VMEM((S, 128), F32)] * 3, input_output_aliases={3: 0},
        compiler_params=_params(("parallel",)),
    )(proj_b, proj_b, proj_b, oab)


def _attn_bwd(proj_b, oab, d_oab, lse, *, name):
    S = proj_b.shape[0]
    scale = DIL_DIM ** -0.5

    def body(q_ref, k_ref, v_ref, o_ref, do_ref, lse_ref, dqkv_ref, dq_ref, dk_ref, dv_ref, delta_ref):
        lane = lax.broadcasted_iota(jnp.int32, (BAND, 128), 1)
        lo = lane < DIL_DIM
        dq_ref[...] = jnp.zeros_like(dq_ref)
        dk_ref[...] = jnp.zeros_like(dk_ref)
        dv_ref[...] = jnp.zeros_like(dv_ref)
        prod = do_ref[...] * o_ref[...].astype(F32)
        lo_all = lax.broadcasted_iota(jnp.int32, (S, 128), 1) < DIL_DIM
        delta_ref[...] = jnp.where(lo_all, jnp.sum(jnp.where(lo_all, prod, 0.0), axis=-1, keepdims=True),
                                   jnp.sum(jnp.where(lo_all, 0.0, prod), axis=-1, keepdims=True))

        def load(d, r, n, first):
            nk = BAND if first else 2 * BAND
            qrows = _rows(r + n * (BAND * d), BAND, d)
            krows = _rows(r if first else r + (n - 1) * (BAND * d), nk, d)
            return dict(nk=nk, qrows=qrows, krows=krows, q=q_ref[qrows, :] * scale, k=k_ref[krows, :], v=v_ref[krows, :],
                        do=do_ref[qrows, :], delta=delta_ref[qrows, :], lse=lse_ref[qrows, :],
                        dq=dq_ref[qrows, :], dk=dk_ref[krows, :], dv=dv_ref[krows, :])

        def compute(b):
            q, k, v, do = b["q"], b["k"], b["v"], b["do"]
            qs, dos = _stack_heads(q, lo), _stack_heads(do, lo)
            p = jnp.where(_band_mask(b["nk"]), jnp.exp(_bdot_nt(qs, k) - _stack_cols(b["lse"])), 0.0)
            ds = p * (_bdot_nt(dos, v) - _stack_cols(b["delta"]))
            dq = b["dq"] + _unstack(_bdot(ds, k), lo) * scale
            return dq, b["dk"] + _bdot_tn(ds, qs), b["dv"] + _bdot_tn(p, dos)

        def visit_many(d, blocks):
            loaded = [load(d, *blk) for blk in blocks]
            done = [compute(b) for b in loaded]
            for b, (dq, dk, dv) in zip(loaded, done):
                dq_ref[b["qrows"], :] = dq
                dk_ref[b["krows"], :] = dk
                dv_ref[b["krows"], :] = dv

        _attn_blocks(S, visit_many, lanes=2)
        dqkv_ref[:, 0:128] = dq_ref[...].astype(BF16)
        dqkv_ref[:, 128:256] = dk_ref[...].astype(BF16)
        dqkv_ref[:, 256:384] = dv_ref[...].astype(BF16)

    half = lambda p: (0, GDN_WIDTH // 128 + p)
    part = lambda t: pl.BlockSpec((S, 128), lambda p: (0, 3 * p + t))
    return pl.pallas_call(
        body, name=name, grid=(DIL_PAIRS,),
        in_specs=[part(0), part(1), part(2), pl.BlockSpec((S, 128), half), pl.BlockSpec((S, 128), half),
                  pl.BlockSpec((S, 128), lambda p: (0, p))],
        out_specs=pl.BlockSpec((S, 384), lambda p: (0, p)), out_shape=_sds((S, 3 * DIL_WIDTH), BF16),
        scratch_shapes=[pltpu.VMEM((S, 128), F32)] * 4, compiler_params=_params(("parallel",)),
    )(proj_b, proj_b, proj_b, oab, d_oab, lse)


FF_SLAB = 2 * D_FF // N_DEV
FF_PAIRS = N_DEV // 2
ROWS16 = 16


def _taps(w, x, base, n):
    out = _shifted(x, base, n) * w[0:1]
    for t in range(1, FFN_CONV):
        out = out + _shifted(x, base + t, n) * w[t:t + 1]
    return out


def _ffn_fwd(h2, x1, w_up, conv_w, w_down, *, name, tm=512):
    S, D = h2.shape
    ni = S // tm
    per = tm // ROWS16

    def body(h_ref, hp_ref, x1_ref, wg_ref, wu_ref, cg_ref, cu_ref, wd_ref, x2_ref, ug_ref, uu_ref):
        i, j = pl.program_id(0), pl.program_id(1)
        hv = jnp.concatenate([hp_ref[...], h_ref[...]], axis=0)
        row = lax.broadcasted_iota(jnp.int32, (tm + ROWS16, 1), 0)
        keep = (i > 0) | (row >= ROWS16)

        def branch(w_ref, c_ref, u_ref):
            u = lax.dot_general(hv, w_ref[...], (((1,), (1,)), ((), ())), preferred_element_type=F32).astype(BF16)
            u_ref[...] = u[ROWS16:]
            return _taps(c_ref[...], jnp.where(keep, u.astype(F32), 0.0), ROWS16 - (FFN_CONV - 1), tm)

        gate = branch(wg_ref, cg_ref, ug_ref)
        up = branch(wu_ref, cu_ref, uu_ref)
        act = (gate * _sigmoid(gate) * up).astype(BF16)
        part = jnp.dot(act, wd_ref[...], preferred_element_type=F32)

        @pl.when(j == 0)
        def _():
            x2_ref[...] = x1_ref[...] + part

        @pl.when(j > 0)
        def _():
            x2_ref[...] += part

    rows = pl.BlockSpec((tm, D), lambda i, j: (i, 0))
    slab = lambda off: pl.BlockSpec((None, FF_SLAB, D), lambda i, j: (j + off, 0, 0))
    cslab = lambda off: pl.BlockSpec((None, FFN_CONV, FF_SLAB), lambda i, j: (j + off, 0, 0))
    uspec = pl.BlockSpec((None, tm, FF_SLAB), lambda i, j: (j, i, 0))
    return pl.pallas_call(
        body, name=name, grid=(ni, FF_PAIRS),
        in_specs=[rows, pl.BlockSpec((ROWS16, D), lambda i, j: (jnp.maximum(i * per - 1, 0), 0)), rows,
                  slab(0), slab(FF_PAIRS), cslab(0), cslab(FF_PAIRS), pl.BlockSpec((FF_SLAB, D), lambda i, j: (j, 0))],
        out_specs=[rows, uspec, uspec],
        out_shape=[_sds((S, D)), _sds((FF_PAIRS, S, FF_SLAB), BF16), _sds((FF_PAIRS, S, FF_SLAB), BF16)],
        compiler_params=_params(("parallel", "arbitrary")),
    )(h2, h2, x1, w_up, w_up, conv_w, conv_w, w_down)


def _ffn_bwd(dx2, h2, ug, uu, conv_w, w_down, *, name, tm=512):
    S, D = h2.shape
    ni = S // tm
    per = tm // ROWS16
    ext = tm + ROWS16

    def body(dx_ref, dxn_ref, h_ref, ug_ref, ugp_ref, ugn_ref, uu_ref, uup_ref, uun_ref, cg_ref, cu_ref, wd_ref,
             dug_ref, duu_ref, gd_ref, gg_ref, gu_ref, dcg_ref, dcu_ref, acc_d, acc_g, acc_u, acc_cg, acc_cu):
        i = pl.program_id(1)

        @pl.when(i == 0)
        def _():
            acc_d[...] = jnp.zeros_like(acc_d)
            acc_g[...] = jnp.zeros_like(acc_g)
            acc_u[...] = jnp.zeros_like(acc_u)
            acc_cg[...] = jnp.zeros_like(acc_cg)
            acc_cu[...] = jnp.zeros_like(acc_cu)

        dx = dx_ref[...]
        dxe = jnp.concatenate([dx, dxn_ref[...]], axis=0)
        row = lax.broadcasted_iota(jnp.int32, (ext, 1), 0)
        live = (i < ni - 1) | (row < tm)
        d_act = jnp.where(live, lax.dot_general(dxe, wd_ref[...], (((1,), (1,)), ((), ())), preferred_element_type=F32), 0.0)
        rowp = lax.broadcasted_iota(jnp.int32, (ext + ROWS16, 1), 0)
        keep = (i > 0) | (rowp >= ROWS16)

        def pre(cur, prev, nxt):
            return jnp.where(keep, jnp.concatenate([prev[...], cur[...], nxt[...]], axis=0).astype(F32), 0.0)

        uge, uue = pre(ug_ref, ugp_ref, ugn_ref), pre(uu_ref, uup_ref, uun_ref)
        cg, cu = cg_ref[...], cu_ref[...]
        base = ROWS16 - (FFN_CONV - 1)
        gate = _taps(cg, uge, base, ext)
        up = _taps(cu, uue, base, ext)
        sg = _sigmoid(gate)
        silu = gate * sg
        dgc = d_act * up * _dsilu(gate, sg)
        duc = d_act * silu

        def conv_t(w, dc):
            out = _shifted(dc, FFN_CONV - 1, tm) * w[0:1]
            for t in range(1, FFN_CONV):
                out = out + _shifted(dc, FFN_CONV - 1 - t, tm) * w[t:t + 1]
            return out.astype(BF16)

        du_g, du_u = conv_t(cg, dgc), conv_t(cu, duc)
        dug_ref[...] = du_g
        duu_ref[...] = du_u
        dcw = lambda dc, xe: jnp.concatenate(
            [jnp.sum(dc[0:tm] * _shifted(xe, base + t, tm), axis=0, keepdims=True) for t in range(FFN_CONV)], axis=0)
        acc_cg[0:FFN_CONV, :] += dcw(dgc, uge)
        acc_cu[0:FFN_CONV, :] += dcw(duc, uue)
        tn = (((0,), (0,)), ((), ()))
        act = (silu[0:tm] * up[0:tm]).astype(BF16)
        acc_d[...] += lax.dot_general(act, dx, tn, preferred_element_type=F32)
        hv = h_ref[...]
        acc_g[...] += lax.dot_general(du_g, hv, tn, preferred_element_type=F32)
        acc_u[...] += lax.dot_general(du_u, hv, tn, preferred_element_type=F32)

        @pl.when(i == ni - 1)
        def _():
            gd_ref[...] = acc_d[...].astype(BF16)
            gg_ref[...] = acc_g[...].astype(BF16)
            gu_ref[...] = acc_u[...].astype(BF16)
            dcg_ref[...] = acc_cg[0:FFN_CONV, :]
            dcu_ref[...] = acc_cu[0:FFN_CONV, :]

    last16 = S // ROWS16 - 1
    rows = pl.BlockSpec((tm, D), lambda j, i: (i, 0))
    rows_next = pl.BlockSpec((ROWS16, D), lambda j, i: (jnp.minimum((i + 1) * per, last16), 0))
    u_cur = pl.BlockSpec((None, tm, FF_SLAB), lambda j, i: (j, i, 0))
    u_prev = pl.BlockSpec((None, ROWS16, FF_SLAB), lambda j, i: (j, jnp.maximum(i * per - 1, 0), 0))
    u_next = pl.BlockSpec((None, ROWS16, FF_SLAB), lambda j, i: (j, jnp.minimum((i + 1) * per, last16), 0))
    cslab = lambda off: pl.BlockSpec((None, FFN_CONV, FF_SLAB), lambda j, i: (j + off, 0, 0))
    wslab = pl.BlockSpec((None, FF_SLAB, D), lambda j, i: (j, 0, 0))
    dslab = pl.BlockSpec((None, FFN_CONV, FF_SLAB), lambda j, i: (j, 0, 0))
    return pl.pallas_call(
        body, name=name, grid=(FF_PAIRS, ni),
        in_specs=[rows, rows_next, rows, u_cur, u_prev, u_next, u_cur, u_prev, u_next, cslab(0), cslab(FF_PAIRS),
                  pl.BlockSpec((FF_SLAB, D), lambda j, i: (j, 0))],
        out_specs=[u_cur, u_cur, pl.BlockSpec((FF_SLAB, D), lambda j, i: (j, 0)), wslab, wslab, dslab, dslab],
        out_shape=[_sds((FF_PAIRS, S, FF_SLAB), BF16), _sds((FF_PAIRS, S, FF_SLAB), BF16), _sds((D_FF, D), BF16),
                   _sds((FF_PAIRS, FF_SLAB, D), BF16), _sds((FF_PAIRS, FF_SLAB, D), BF16),
                   _sds((FF_PAIRS, FFN_CONV, FF_SLAB)), _sds((FF_PAIRS, FFN_CONV, FF_SLAB))],
        scratch_shapes=[pltpu.VMEM((FF_SLAB, D), F32), pltpu.VMEM((FF_SLAB, D), F32), pltpu.VMEM((FF_SLAB, D), F32),
                        pltpu.VMEM((8, FF_SLAB), F32), pltpu.VMEM((8, FF_SLAB), F32)],
        compiler_params=_params(("parallel", "arbitrary")),
    )(dx2, dx2, h2, ug, ug, ug, uu, uu, uu, conv_w, conv_w, w_down)


def _mm_slabs(a, w, w_off, *, name, res=None, tm=1024, tn=1024):
    nk, S, _ = a.shape
    D = w.shape[2]
    has_res = res is not None

    def body(*refs):
        if has_res:
            a_ref, w_ref, r_ref, o_ref, acc_ref = refs
        else:
            a_ref, w_ref, o_ref, acc_ref = refs
        k = pl.program_id(2)
        part = jnp.dot(a_ref[...], w_ref[...], preferred_element_type=F32)

        @pl.when(k == 0)
        def _():
            acc_ref[...] = part

        @pl.when(k > 0)
        def _():
            acc_ref[...] += part

        @pl.when(k == nk - 1)
        def _():
            o_ref[...] = acc_ref[...] + r_ref[...] if has_res else acc_ref[...]

    o_spec = pl.BlockSpec((tm, tn), lambda i, j, k: (i, j))
    return pl.pallas_call(
        body, name=name, grid=(S // tm, D // tn, nk),
        in_specs=[pl.BlockSpec((None, tm, FF_SLAB), lambda i, j, k: (k, i, 0)),
                  pl.BlockSpec((None, FF_SLAB, tn), lambda i, j, k: (k + w_off, 0, j))] + ([o_spec] if has_res else []),
        out_specs=o_spec, out_shape=_sds((S, D)), scratch_shapes=[pltpu.VMEM((tm, tn), F32)],
        compiler_params=_params(("parallel", "parallel", "arbitrary")),
    )(*((a, w, res) if has_res else (a, w)))


def _local_step(x, tgt, norm1_w, w_a, w_z, w_b, conv_a, a_log, dt_bias, gnw, norm2_w, final_w, late_weights, emit):
    wgrad = functools.partial(_mm, ta=True, out_dtype=BF16)
    h1 = _rms_fwd(x, norm1_w, name="rms1_fwd")
    proj_a = _mm(h1, w_a, tb=True, name="proj_a", tn=A_COLS, tk=1024)
    proj_z = _mm(h1, w_z, tb=True, name="proj_z", tk=1024)
    proj_b = _mm(h1, w_b, tb=True, name="proj_b", tn=768, tk=1024)
    qn, kn, v, gcb, bb = _gdn_prep_fwd(proj_a, conv_a, a_log, dt_bias, name="gdn_prep_fwd")
    uv, wk, at, tmat = _gdn_chunk_fwd(qn, kn, v, gcb, bb, name="gdn_chunk_fwd")
    o, u, sp, oab = _gdn_scan_fwd(uv, wk, at, qn, kn, gcb, bb, proj_z, gnw, name="gdn_scan_fwd")
    oab, lse = _attn_fwd(proj_b, oab, name="attn_fwd")
    w_out, w_up, conv_f, w_down = late_weights(oab)
    x1 = _mm(oab, w_out, res=x, name="out_proj", tk=1024)
    h2 = _rms_fwd(x1, norm2_w, name="rms2_fwd")
    x2, ug, uu = _ffn_fwd(h2, x1, w_up, conv_f, w_down, name="ffn_fwd")
    dx2, dx2_b, d_final, loss = _loss_head(x2, final_w, tgt, name="loss_head")
    dug, duu, g_down, g_up_g, g_up_u, dcw_g, dcw_u = _ffn_bwd(dx2_b, h2, ug, uu, conv_f, w_down, name="ffn_bwd")
    token = emit("ffn", w_down=g_down, w_up=jnp.concatenate([g_up_g, g_up_u], axis=0),
                 conv_f=jnp.concatenate([dcw_g, dcw_u], axis=0))
    dh2 = _mm_slabs(dug, w_up, 0, name="ffn_up_dx_gate")
    dh2 = _mm_slabs(duu, w_up, FF_PAIRS, res=dh2, name="ffn_up_dx_up")
    dx1, d_norm2 = _rms_bwd(dh2, x1, _behind(norm2_w, token), dx2, name="rms2_bwd")
    d_oab = _mm(dx1, w_out, tb=True, name="out_proj_dx", tk=1024)
    gnw = _behind(gnw, emit("out", w_out=wgrad(oab, dx1, name="out_proj_dw")))
    dz, d_gnw, du, dwk, dat, dqd, dke, dgl = _gdn_scan_bwd(d_oab, o, proj_z, gnw, sp, u, wk, at, qn, kn, gcb, bb, name="gdn_scan_bwd")
    dqn, dkn, dv, dg, dbeta = _gdn_chunk_bwd(qn, kn, gcb, bb, tmat, uv, wk, du, dwk, dat, dqd, dke, dgl, name="gdn_chunk_bwd")
    dc, dba, d_small = _gdn_prep_bwd(dqn, dkn, dv, dg, dbeta, proj_a, conv_a, a_log, dt_bias, name="gdn_prep_bwd")
    d_pa, d_conv_a = _gdn_conv_bwd(dc, dba, proj_a, conv_a, name="gdn_conv_bwd")
    d_pb = _attn_bwd(proj_b, oab, d_oab, lse, name="attn_bwd")
    g_a = wgrad(d_pa, h1, name="proj_a_dw", tm=A_COLS)
    g_z = wgrad(dz, h1, name="proj_z_dw")
    g_b = wgrad(d_pb, h1, name="proj_b_dw", tm=768)
    w_z = _behind(w_z, emit("in", w_a=g_a, w_z=g_z, w_b=g_b, conv_a=d_conv_a))
    dh1 = _mm(dz, w_z, name="proj_z_dx")
    dh1 = _mm(d_pa, w_a, res=dh1, name="proj_a_dx", tk=A_COLS)
    dh1 = _mm(d_pb, w_b, res=dh1, name="proj_b_dx", tk=1536)
    grad_x, d_norm1 = _rms_bwd(dh1, x, norm1_w, dx1, name="rms1_bwd")
    small = dict(norm1=d_norm1, small=d_small, gnw=d_gnw, norm2=d_norm2, final=d_final)
    return loss, grad_x, small


_O1 = 3 * GDN_WIDTH
_O2 = _O1 + GDN_WIDTH
_O3 = _O2 + 2 * GDN_HEADS


def _split_w_in(w_t):
    d = w_t.shape[1]
    pad = jnp.zeros((A_COLS - _O1 - 2 * GDN_HEADS, d), w_t.dtype)
    w_a = jnp.concatenate([w_t[:_O1], w_t[_O2:_O3], pad], axis=0)
    w_b = w_t[_O3:].reshape(3, DIL_PAIRS, 128, d).transpose(1, 0, 2, 3).reshape(3 * DIL_WIDTH, d)
    return w_a, w_t[_O1:_O2], w_b


def _merge_g_in(g_a, g_z, g_b):
    d = g_a.shape[1]
    g_b = g_b.reshape(DIL_PAIRS, 3, 128, d).transpose(1, 0, 2, 3).reshape(3 * DIL_WIDTH, d)
    return jnp.concatenate([g_a[:_O1], g_z, g_a[_O1:_O1 + 2 * GDN_HEADS], g_b], axis=0)


MESH = pl.DeviceIdType.MESH
ANY = pl.BlockSpec(memory_space=pl.ANY)


def _position():
    return lax.axis_index("x"), lax.axis_index("y"), lax.axis_index("c")


def _slot(p):
    return 4 * p[0] + 2 * p[1] + p[2]


def _all_gather(blocks, *, name):
    n = len(blocks)

    def body(*refs):
        ins, outs = refs[:n], refs[n:2 * n]
        send_sems, recv_sems, local_sems = refs[2 * n:]
        x, y, c = _position()
        me, sibling = (x, y, c), (x, y, 1 - c)
        chips = [(1 - x, y), (x, 1 - y), (1 - x, 1 - y)]

        def copy(a, k, block, to, src=None):
            dst = outs[a].at[_slot(block)]
            return pltpu.make_async_remote_copy(
                src_ref=dst if src is None else src, dst_ref=dst, send_sem=send_sems.at[a, k], recv_sem=recv_sems.at[a, k],
                device_id=to, device_id_type=MESH)

        mine = [pltpu.make_async_copy(ins[a], outs[a].at[_slot(me)], local_sems.at[a]) for a in range(n)]
        for cp in mine:
            cp.start()
        first = []
        for a in range(n):
            first.append(copy(a, 0, me, sibling, src=ins[a]))
            first += [copy(a, 1 + j, me, (*chip, c), src=ins[a]) for j, chip in enumerate(chips)]
        for cp in first:
            cp.start()
        passed = []
        for j, chip in enumerate(chips):
            for a in range(n):
                copy(a, 1 + j, (*chip, c), me).wait_recv()
                fwd = copy(a, 4 + j, (*chip, c), sibling)
                fwd.start()
                passed.append(fwd)
        for a in range(n):
            copy(a, 0, sibling, me).wait_recv()
            for j, chip in enumerate(chips):
                copy(a, 4 + j, (*chip, 1 - c), me).wait_recv()
        for cp in first + passed:
            cp.wait_send()
        for cp in mine:
            cp.wait()

    return pl.pallas_call(
        body, name=name, in_specs=[ANY] * n, out_specs=[ANY] * n,
        out_shape=[_sds((N_DEV,) + b.shape, b.dtype) for b in blocks],
        scratch_shapes=[pltpu.SemaphoreType.DMA((n, 7)), pltpu.SemaphoreType.DMA((n, 7)), pltpu.SemaphoreType.DMA((n,))],
    )(*blocks)


def _gather_direct(block, *, name):
    def body(in_ref, out_ref, send_sems, recv_sems, local_sem):
        x, y, c = _position()
        me = _slot((x, y, c))
        mine = pltpu.make_async_copy(in_ref, out_ref.at[me], local_sem)
        mine.start()
        copies = [pltpu.make_async_remote_copy(
            src_ref=in_ref, dst_ref=out_ref.at[me], send_sem=send_sems.at[k - 1], recv_sem=recv_sems.at[k - 1],
            device_id=_peer_of(k, x, y, c), device_id_type=MESH) for k in range(1, N_DEV)]
        for cp in copies:
            cp.start()
        for cp in copies:
            cp.wait()
        mine.wait()

    return pl.pallas_call(
        body, name=name, in_specs=[pl.BlockSpec(memory_space=pltpu.VMEM)], out_specs=pl.BlockSpec(memory_space=pltpu.VMEM),
        out_shape=_sds((N_DEV,) + block.shape, block.dtype),
        scratch_shapes=[pltpu.SemaphoreType.DMA((N_DEV - 1,)), pltpu.SemaphoreType.DMA((N_DEV - 1,)), pltpu.SemaphoreType.DMA],
    )(block)


HBM = pl.BlockSpec(memory_space=pltpu.HBM)
SEM = pl.BlockSpec(memory_space=pltpu.SEMAPHORE)
EFFECT = pltpu.SideEffectType.DATAFLOW_SIDE_EFFECTING


def _peer_of(k, x, y, c):
    return (1 - x if k & 4 else x, 1 - y if k & 2 else y, 1 - c if k & 1 else c)


def _flight(a, k):
    return a * (N_DEV - 1) + k - 1


def _exchange_start(arrays, *, name, broadcast=False):
    n = len(arrays)

    def body(*refs):
        ins, lands = refs[:n], refs[n:2 * n]
        send_sems, recv_sems = refs[2 * n:2 * n + 2]
        token = refs[-1]
        x, y, c = _position()
        me = _slot((x, y, c))
        for k in range(1, N_DEV):
            peer = _peer_of(k, x, y, c)
            for a in range(n):
                pltpu.make_async_remote_copy(
                    src_ref=ins[a] if broadcast else ins[a].at[_slot(peer)], dst_ref=lands[a].at[me],
                    send_sem=send_sems.at[_flight(a, k)], recv_sem=recv_sems.at[_flight(a, k)],
                    device_id=peer, device_id_type=MESH).start()
        token[...] = jnp.zeros_like(token)

    land_shapes = [((N_DEV,) + s.shape) if broadcast else s.shape for s in arrays]
    lands = [pltpu.with_memory_space_constraint(lax.empty(shp, s.dtype), pltpu.HBM) for shp, s in zip(land_shapes, arrays)]
    srcs = [pltpu.with_memory_space_constraint(s, pltpu.HBM) for s in arrays]
    outs = pl.pallas_call(
        body, name=name, in_specs=[HBM] * (2 * n),
        out_specs=[SEM, SEM] + [HBM] * (2 * n) + [pl.BlockSpec(memory_space=pltpu.VMEM)],
        out_shape=[pltpu.SemaphoreType.DMA((n * (N_DEV - 1),)), pltpu.SemaphoreType.DMA((n * (N_DEV - 1),))]
        + [pltpu.HBM(s.shape, s.dtype) for s in arrays] + [pltpu.HBM(shp, s.dtype) for shp, s in zip(land_shapes, arrays)]
        + [_sds((8, 128))],
        input_output_aliases={i: 2 + i for i in range(2 * n)},
        compiler_params=pltpu.CompilerParams(has_side_effects=EFFECT),
    )(*srcs, *lands)
    return outs[0], outs[1], outs[2:2 + n], outs[2 + n:2 + 2 * n], outs[-1]


def _exchange_wait(send_sems, recv_sems, srcs, lands, after, *, name, broadcast=False):
    n = len(srcs)

    def body(*refs):
        ins, lnd = refs[:n], refs[n:2 * n]
        send_ref, recv_ref = refs[2 * n:2 * n + 2]
        x, y, c = _position()
        for k in range(1, N_DEV):
            for a in range(n):
                cp = pltpu.make_async_remote_copy(
                    src_ref=ins[a] if broadcast else ins[a].at[0], dst_ref=lnd[a].at[0], send_sem=send_ref.at[_flight(a, k)],
                    recv_sem=recv_ref.at[_flight(a, k)], device_id=_peer_of(k, x, y, c), device_id_type=MESH)
                cp.wait_send()
                cp.wait_recv()

    outs = pl.pallas_call(
        body, name=name, in_specs=[HBM] * (2 * n) + [SEM, SEM, ANY], out_specs=[HBM] * (2 * n),
        out_shape=[pltpu.HBM(s.shape, s.dtype) for s in srcs] + [pltpu.HBM(s.shape, s.dtype) for s in lands],
        input_output_aliases={i: i for i in range(2 * n)},
        compiler_params=pltpu.CompilerParams(has_side_effects=EFFECT),
    )(*srcs, *lands, send_sems, recv_sems, after)
    return outs[:n], outs[n:]


def _with_own(landed, srcs, me, broadcast=False):
    own = srcs if broadcast else [lax.dynamic_index_in_dim(s, me, 0, keepdims=False) for s in srcs]
    return [lax.dynamic_update_index_in_dim(l, o, me, 0) for l, o in zip(landed, own)]


def _behind(x, token):
    return x if token is None else x + token[0, 0].astype(x.dtype)


def _adamw(parts, w, m, v, *, name, tr=None, tc=None):
    R, C = w.shape
    tr = R if tr is None else tr
    tc = C if tc is None else tc
    assert R % tr == 0 and C % tc == 0
    c1 = 1.0 - ADAM_B1 ** ADAM_STEP
    c2 = 1.0 - ADAM_B2 ** ADAM_STEP

    def body(p_ref, w_ref, m_ref, v_ref, g_ref, d_ref, nm_ref, nv_ref):
        g = p_ref[0].astype(F32)
        for s in range(1, N_DEV):
            g = g + p_ref[s].astype(F32)
        nm = ADAM_B1 * m_ref[...] + (1.0 - ADAM_B1) * g
        nv = ADAM_B2 * v_ref[...] + (1.0 - ADAM_B2) * (g * g)
        g_ref[...] = g
        nm_ref[...] = nm
        nv_ref[...] = nv
        d_ref[...] = -ADAM_LR * ((nm / c1) / (jnp.sqrt(nv / c2) + ADAM_EPS) + ADAM_WD * w_ref[...])

    blk = pl.BlockSpec((tr, tc), lambda i, j: (i, j))
    return pl.pallas_call(
        body, name=name, grid=(R // tr, C // tc),
        in_specs=[pl.BlockSpec((N_DEV, tr, tc), lambda i, j: (0, i, j)), blk, blk, blk],
        out_specs=[blk] * 4, out_shape=[_sds((R, C))] * 4, compiler_params=_params(("parallel", "parallel")),
    )(parts, w, m, v)


_SMALL_ROWS = 8


def _pack_small(norm1, norm2, final, gnw, a_log, dt_bias, loss=None):
    loss = jnp.zeros((1, 128), F32) if loss is None else loss
    row3 = jnp.concatenate([gnw, a_log, dt_bias, jnp.zeros((1, 128 - 2 * GDN_HEADS), F32), loss,
                            jnp.zeros((1, D_MODEL - 3 * 128), F32)], axis=1)
    return jnp.concatenate([norm1, norm2, final, row3, jnp.zeros((_SMALL_ROWS - 4, D_MODEL), F32)], axis=0)


def _unpack_small(p):
    return (p[0:1], p[1:2], p[2], p[3:4, 0:128], p[3:4, 128:128 + GDN_HEADS], p[3:4, 128 + GDN_HEADS:128 + 2 * GDN_HEADS])


def _slabs_by_cols(g):
    r = g.shape[0]
    return g.reshape(r, N_DEV, -1).transpose(1, 0, 2)


def _cols_from_slabs(s):
    return s.transpose(1, 0, 2).reshape(s.shape[1], -1)


def kernel(x, norm1_w, w_in, conv_qkv_w, a_log, dt_bias, gdn_norm_w, w_out, norm2_w, w_up, ffn_conv_w, w_down, final_norm_w, loss_target, m_norm1_w, m_w_in, m_conv_qkv_w, m_a_log, m_dt_bias, m_gdn_norm_w, m_w_out, m_norm2_w, m_w_up, m_ffn_conv_w, m_w_down, m_final_norm_w, v_norm1_w, v_w_in, v_conv_qkv_w, v_a_log, v_dt_bias, v_gdn_norm_w, v_w_out, v_norm2_w, v_w_up, v_ffn_conv_w, v_w_down, v_final_norm_w):
    bf = lambda a: a.astype(BF16)
    me = _slot(_position())
    t_in = lambda a: a[0].T
    gw_in, g_conv_a = _all_gather([bf(t_in(w_in)), conv_qkv_w[0]], name="gather_w_in")
    w_a, w_z, w_b = _split_w_in(gw_in.reshape(-1, D_MODEL))
    late_src, _ = lax.optimization_barrier(([bf(w_out[0]), bf(t_in(w_up)), bf(w_down[0]), ffn_conv_w[0]], gw_in))
    l_send, l_recv, l_srcs, l_lands, l_token = _exchange_start(late_src, name="weights_start", broadcast=True)

    def late_weights(after):
        srcs, landed = _exchange_wait(l_send, l_recv, l_srcs, l_lands, after, name="weights_wait", broadcast=True)
        gw_out, gw_up, gw_down, g_conv_f = _with_own(landed, srcs, me, broadcast=True)
        return gw_out.reshape(D_MODEL, D_MODEL), gw_up, g_conv_f, gw_down.reshape(D_FF, D_MODEL)

    flights = {}

    def emit(group, **grads):
        if group == "in":
            slabs = dict(w_in=_merge_g_in(grads["w_a"], grads["w_z"], grads["w_b"]).reshape(N_DEV, -1, D_MODEL),
                         conv_a=_slabs_by_cols(grads["conv_a"]))
        elif group == "ffn":
            slabs = dict(w_down=grads["w_down"].reshape(N_DEV, -1, D_MODEL), w_up=grads["w_up"], conv_f=grads["conv_f"])
        else:
            slabs = {k: v.reshape(N_DEV, -1, D_MODEL) for k, v in grads.items()}
        names = list(slabs)
        *flight, token = _exchange_start([slabs[k] for k in names], name="grads_start_" + group)
        flights[group] = (names, flight)
        return token

    loss, grad_x, g = _local_step(
        x[0], loss_target[0], _behind(norm1_w, l_token), w_a, w_z, w_b, _cols_from_slabs(g_conv_a), a_log, dt_bias,
        gdn_norm_w, norm2_w, final_norm_w[None], late_weights, emit)
    small_all = _gather_direct(
        _pack_small(g["norm1"], g["norm2"], g["final"], g["gnw"], g["small"][:, 0:GDN_HEADS],
                    g["small"][:, GDN_HEADS:2 * GDN_HEADS], loss), name="gather_small")
    got = {}
    for group in ("ffn", "out", "in"):
        names, (send_sems, recv_sems, srcs, lands) = flights[group]
        srcs, landed = _exchange_wait(send_sems, recv_sems, srcs, lands, small_all, name="grads_wait_" + group)
        got.update(zip(names, _with_own(landed, srcs, me)))
    o_in = [o.T for o in _adamw(got["w_in"], t_in(w_in), t_in(m_w_in), t_in(v_w_in), name="adamw_w_in", tc=256)]
    o_out = _adamw(got["w_out"], w_out[0], m_w_out[0], v_w_out[0], name="adamw_w_out")
    o_up = [o.T for o in _adamw(got["w_up"], t_in(w_up), t_in(m_w_up), t_in(v_w_up), name="adamw_w_up", tr=176)]
    o_down = _adamw(got["w_down"], w_down[0], m_w_down[0], v_w_down[0], name="adamw_w_down", tr=176)
    o_ca = _adamw(got["conv_a"], conv_qkv_w[0], m_conv_qkv_w[0], v_conv_qkv_w[0], name="adamw_conv_a")
    o_cf = _adamw(got["conv_f"], ffn_conv_w[0], m_ffn_conv_w[0], v_ffn_conv_w[0], name="adamw_conv_f")
    o_small = _adamw(
        small_all, _pack_small(norm1_w, norm2_w, final_norm_w[None], gdn_norm_w, a_log, dt_bias),
        _pack_small(m_norm1_w, m_norm2_w, m_final_norm_w[None], m_gdn_norm_w, m_a_log, m_dt_bias),
        _pack_small(v_norm1_w, v_norm2_w, v_final_norm_w[None], v_gdn_norm_w, v_a_log, v_dt_bias), name="adamw_small")
    total_loss = o_small[0][3, 256]
    outs = [total_loss, grad_x[None]]
    for k in range(4):
        n1, n2, fin, gn, al, dt = _unpack_small(o_small[k])
        outs += [n1, o_in[k][None], o_ca[k][None], al, dt, gn, o_out[k][None], n2, o_up[k][None], o_cf[k][None], o_down[k][None], fin]
    return tuple(outs)
```

```python
import functools

import jax
import jax.numpy as jnp
from jax import lax
from jax.experimental import pallas as pl
from jax.experimental.pallas import tpu as pltpu

F32 = jnp.float32
BF16 = jnp.bfloat16

N_DEV = 8
D_MODEL = 1024
GDN_HEADS = 4
GDN_DIM = 128
GDN_WIDTH = GDN_HEADS * GDN_DIM
GDN_CONV = 4
CHUNK = 64
CHUNKS_PER_STEP = 2
DIL_HEADS = 8
DIL_DIM = 64
DIL_WIDTH = DIL_HEADS * DIL_DIM
DIL_PAIRS = DIL_HEADS // 2
DILATIONS = (1, 4, 16)
BAND = 128
D_FF = 2816
FFN_CONV = 3
EPS = 1e-6
A_COLS = 3 * GDN_WIDTH + 128
HALO = 8

ADAM_LR = 0.001
ADAM_B1 = 0.9
ADAM_B2 = 0.999
ADAM_EPS = 1e-08
ADAM_WD = 0.01
ADAM_STEP = 10

VMEM_LIMIT_BYTES = 56 * 1024 * 1024
NEG_BIG = -1e30


def _params(sem=None):
    return pltpu.CompilerParams(dimension_semantics=sem, vmem_limit_bytes=VMEM_LIMIT_BYTES)


def _sds(shape, dtype=F32):
    return jax.ShapeDtypeStruct(shape, dtype)


def _bdot(a, b):
    return jnp.dot(a.astype(BF16), b.astype(BF16), preferred_element_type=F32)


def _bdot_nt(a, b):
    return lax.dot_general(a.astype(BF16), b.astype(BF16), (((1,), (1,)), ((), ())), preferred_element_type=F32)


def _bdot_tn(a, b):
    return lax.dot_general(a.astype(BF16), b.astype(BF16), (((0,), (0,)), ((), ())), preferred_element_type=F32)


def _split(a):
    hi = a.astype(BF16)
    lo = (a - hi.astype(F32)).astype(BF16)
    return hi, lo


def _dot3(a, b, dims):
    ah, al = _split(a)
    bh, bl = _split(b)
    d = functools.partial(lax.dot_general, dimension_numbers=(dims, ((), ())), preferred_element_type=F32)
    return d(ah, bh) + (d(al, bh) + d(ah, bl))


def _exact_tri_dot(tri, g):
    g1 = g.astype(BF16)
    r1 = g - g1.astype(F32)
    g2 = r1.astype(BF16)
    g3 = (r1 - g2.astype(F32)).astype(BF16)
    t = tri.astype(BF16)
    d = functools.partial(jnp.dot, preferred_element_type=F32)
    return d(t, g1) + (d(t, g2) + d(t, g3))


def _sigmoid(x):
    return 1.0 / (1.0 + jnp.exp(-x))


def _dsilu(x, sg):
    return sg * (1.0 + x * (1.0 - sg))


def _mm(a, b, *, name, ta=False, tb=False, res=None, out_dtype=F32, tm=512, tn=512, tk=512):
    if ta:
        K, M = a.shape
    else:
        M, K = a.shape
    if tb:
        N, Kb = b.shape
    else:
        Kb, N = b.shape
    assert K == Kb, (a.shape, b.shape)
    tm, tn, tk = min(tm, M), min(tn, N), min(tk, K)
    assert M % tm == 0 and N % tn == 0 and K % tk == 0, (name, M, N, K, tm, tn, tk)
    nk = K // tk
    dims = (((0 if ta else 1,), (1 if tb else 0,)), ((), ()))
    has_res = res is not None

    def body(*refs):
        if has_res:
            a_ref, b_ref, r_ref, o_ref, acc_ref = refs
        else:
            a_ref, b_ref, o_ref, acc_ref = refs
        k = pl.program_id(2)
        part = lax.dot_general(a_ref[...].astype(BF16), b_ref[...].astype(BF16), dims, preferred_element_type=F32)

        @pl.when(k == 0)
        def _():
            acc_ref[...] = part

        @pl.when(k > 0)
        def _():
            acc_ref[...] += part

        @pl.when(k == nk - 1)
        def _():
            r = acc_ref[...]
            if has_res:
                r = r + r_ref[...]
            o_ref[...] = r.astype(out_dtype)

    a_spec = pl.BlockSpec((tk, tm), lambda i, j, k: (k, i)) if ta else pl.BlockSpec((tm, tk), lambda i, j, k: (i, k))
    b_spec = pl.BlockSpec((tn, tk), lambda i, j, k: (j, k)) if tb else pl.BlockSpec((tk, tn), lambda i, j, k: (k, j))
    o_spec = pl.BlockSpec((tm, tn), lambda i, j, k: (i, j))
    in_specs = [a_spec, b_spec] + ([o_spec] if has_res else [])
    args = (a, b) + ((res,) if has_res else ())
    return pl.pallas_call(
        body, name=name, grid=(M // tm, N // tn, nk), in_specs=in_specs, out_specs=o_spec,
        out_shape=_sds((M, N), out_dtype), scratch_shapes=[pltpu.VMEM((tm, tn), F32)],
        compiler_params=_params(("parallel", "parallel", "arbitrary")),
    )(*args)


def _rms_fwd(x, w, *, name, tm=256):
    S, D = x.shape

    def body(x_ref, w_ref, h_ref):
        xv = x_ref[...]
        r = lax.rsqrt(jnp.mean(xv * xv, axis=-1, keepdims=True) + EPS)
        h_ref[...] = (xv * r * w_ref[...]).astype(BF16)

    return pl.pallas_call(
        body, name=name, grid=(S // tm,),
        in_specs=[pl.BlockSpec((tm, D), lambda i: (i, 0)), pl.BlockSpec((1, D), lambda i: (0, 0))],
        out_specs=pl.BlockSpec((tm, D), lambda i: (i, 0)), out_shape=_sds((S, D), BF16),
        compiler_params=_params(("parallel",)),
    )(x, w)


def _rms_bwd(dh, x, w, res, *, name, tm=256):
    S, D = x.shape

    def body(dh_ref, x_ref, w_ref, res_ref, dx_ref, dw_ref):
        i = pl.program_id(0)
        xv = x_ref[...]
        g = dh_ref[...]
        r = lax.rsqrt(jnp.mean(xv * xv, axis=-1, keepdims=True) + EPS)
        xh = xv * r
        gw = g * w_ref[...]
        dx_ref[...] = res_ref[...] + r * (gw - xh * jnp.mean(gw * xh, axis=-1, keepdims=True))
        part = jnp.sum(g * xh, axis=0, keepdims=True)

        @pl.when(i == 0)
        def _():
            dw_ref[...] = part

        @pl.when(i > 0)
        def _():
            dw_ref[...] += part

    row = pl.BlockSpec((tm, D), lambda i: (i, 0))
    one = pl.BlockSpec((1, D), lambda i: (0, 0))
    return pl.pallas_call(
        body, name=name, grid=(S // tm,), in_specs=[row, row, one, row], out_specs=[row, one],
        out_shape=[_sds((S, D)), _sds((1, D))], compiler_params=_params(("arbitrary",)),
    )(dh, x, w, res)


def _loss_head(x2, w, tgt, *, name, tm=256):
    S, D = x2.shape

    def body(x_ref, w_ref, t_ref, dx_ref, dxb_ref, dw_ref, loss_ref):
        i = pl.program_id(0)
        xv = x_ref[...]
        wv = w_ref[...]
        r = lax.rsqrt(jnp.mean(xv * xv, axis=-1, keepdims=True) + EPS)
        xh = xv * r
        err = xh * wv - t_ref[...]
        lrow = jnp.sum(err * err, axis=-1, keepdims=True)
        lsum = jnp.sum(lrow, axis=0, keepdims=True) * (0.5 / D)
        g = err * (1.0 / D)
        gw = g * wv
        dx = r * (gw - xh * jnp.mean(gw * xh, axis=-1, keepdims=True))
        dx_ref[...] = dx
        dxb_ref[...] = dx.astype(BF16)
        part = jnp.sum(g * xh, axis=0, keepdims=True)
        lpart = jnp.broadcast_to(lsum, (1, 128))

        @pl.when(i == 0)
        def _():
            dw_ref[...] = part
            loss_ref[...] = lpart

        @pl.when(i > 0)
        def _():
            dw_ref[...] += part
            loss_ref[...] += lpart

    row = pl.BlockSpec((tm, D), lambda i: (i, 0))
    one = pl.BlockSpec((1, D), lambda i: (0, 0))
    return pl.pallas_call(
        body, name=name, grid=(S // tm,), in_specs=[row, one, row],
        out_specs=[row, row, one, pl.BlockSpec((1, 128), lambda i: (0, 0))],
        out_shape=[_sds((S, D)), _sds((S, D), BF16), _sds((1, D)), _sds((1, 128))], compiler_params=_params(("arbitrary",)),
    )(x2, w, tgt)


def _shifted(x, start, n):
    aligned = -(-start // HALO) * HALO
    assert aligned + n <= x.shape[0], (start, n, x.shape)
    return (x if aligned == start else pltpu.roll(x, aligned - start, axis=0))[aligned:aligned + n]


def _conv_rows(prev, cur, w, taps):
    n = cur.shape[0]
    xs = jnp.concatenate([prev, cur], axis=0)
    base = HALO - (taps - 1)
    out = _shifted(xs, base, n) * w[0:1]
    for i in range(1, taps):
        out = out + _shifted(xs, base + i, n) * w[i:i + 1]
    return out


def _conv_rows_bwd(cur_d, next_d, prev_x, cur_x, w, taps):
    n = cur_d.shape[0]
    ds = jnp.concatenate([cur_d, next_d], axis=0)
    dx = _shifted(ds, taps - 1, n) * w[0:1]
    for i in range(1, taps):
        dx = dx + _shifted(ds, taps - 1 - i, n) * w[i:i + 1]
    xs = jnp.concatenate([prev_x, cur_x], axis=0)
    base = HALO - (taps - 1)
    dws = [jnp.sum(cur_d * _shifted(xs, base + i, n), axis=0, keepdims=True) for i in range(taps)]
    return dx, jnp.concatenate(dws, axis=0)


def _halo_specs(tm, width, col, nblk):
    per = tm // HALO
    prev = pl.BlockSpec((HALO, width), lambda i, *_: (jnp.maximum(i * per - 1, 0), col))
    nxt = pl.BlockSpec((HALO, width), lambda i, *_: (jnp.minimum((i + 1) * per, nblk * per - 1), col))
    return prev, nxt


def _softplus(x):
    return jnp.maximum(x, 0.0) + jnp.log1p(jnp.exp(-jnp.abs(x)))


def _chunk_tri(tm, upper=False):
    r = lax.broadcasted_iota(jnp.int32, (tm, tm), 0)
    c = lax.broadcasted_iota(jnp.int32, (tm, tm), 1)
    same = lax.div(r, CHUNK) == lax.div(c, CHUNK)
    order = (c >= r) if upper else (c <= r)
    return jnp.where(same & order, 1.0, 0.0)


def _gdn_prep_fwd(proj_a, conv_w, a_log, dt_bias, *, name, tm=256):
    S = proj_a.shape[0]
    nblk = S // tm
    W3 = 3 * GDN_WIDTH

    def body(cur_ref, prev_ref, ba_ref, cw_ref, al_ref, dt_ref, qn_ref, kn_ref, v_ref, gcb_ref, bb_ref):
        i = pl.program_id(0)
        prev = jnp.where(i > 0, prev_ref[...], 0.0)
        c = _conv_rows(prev, cur_ref[...], cw_ref[...], GDN_CONV)
        a = c * _sigmoid(c)
        ba = ba_ref[...]
        lane = lax.broadcasted_iota(jnp.int32, (tm, 128), 1)
        g4 = jnp.zeros((tm, 128), F32)
        for h in range(GDN_HEADS):
            sl = slice(GDN_DIM * h, GDN_DIM * (h + 1))
            qh = a[:, GDN_DIM * h:GDN_DIM * (h + 1)]
            kh = a[:, GDN_WIDTH + GDN_DIM * h:GDN_WIDTH + GDN_DIM * (h + 1)]
            qn_ref[:, sl] = qh * (lax.rsqrt(jnp.sum(qh * qh, axis=-1, keepdims=True) + EPS) * (GDN_DIM ** -0.5))
            kn_ref[:, sl] = kh * lax.rsqrt(jnp.sum(kh * kh, axis=-1, keepdims=True) + EPS)
            beta = _sigmoid(ba[:, h:h + 1])
            bb_ref[:, sl] = jnp.broadcast_to(beta, (tm, GDN_DIM))
            g = -jnp.exp(al_ref[0:1, h:h + 1]) * _softplus(ba[:, GDN_HEADS + h:GDN_HEADS + h + 1] + dt_ref[0:1, h:h + 1])
            g4 = jnp.where(lane == h, g, g4)
        v_ref[...] = a[:, 2 * GDN_WIDTH:]
        gc = _exact_tri_dot(_chunk_tri(tm), g4)
        for h in range(GDN_HEADS):
            gcb_ref[:, GDN_DIM * h:GDN_DIM * (h + 1)] = jnp.broadcast_to(gc[:, h:h + 1], (tm, GDN_DIM))

    prev_spec, _ = _halo_specs(tm, W3, 0, nblk)
    row = pl.BlockSpec((tm, GDN_WIDTH), lambda i: (i, 0))
    small = lambda a: pl.BlockSpec(a.shape, lambda i: (0, 0))
    return pl.pallas_call(
        body, name=name, grid=(nblk,),
        in_specs=[pl.BlockSpec((tm, W3), lambda i: (i, 0)), prev_spec,
                  pl.BlockSpec((tm, 128), lambda i: (i, W3 // 128)), small(conv_w), small(a_log), small(dt_bias)],
        out_specs=[row] * 5, out_shape=[_sds((S, GDN_WIDTH))] * 5, compiler_params=_params(("parallel",)),
    )(proj_a, proj_a, proj_a, conv_w, a_log, dt_bias)


GDN_STACK = GDN_HEADS * CHUNK


def _stack(ref, rows):
    return jnp.concatenate([ref[rows, GDN_DIM * h:GDN_DIM * (h + 1)] for h in range(GDN_HEADS)], axis=0)


def _unstack_to(ref, rows, x):
    for h in range(GDN_HEADS):
        ref[rows, GDN_DIM * h:GDN_DIM * (h + 1)] = x[CHUNK * h:CHUNK * (h + 1)].astype(ref.dtype)


def _stack_masks():
    r = lax.broadcasted_iota(jnp.int32, (GDN_STACK, GDN_STACK), 0)
    c = lax.broadcasted_iota(jnp.int32, (GDN_STACK, GDN_STACK), 1)
    same = (r & -CHUNK) == (c & -CHUNK)
    return same & (r >= c), same & (r > c), r == c


def _stack_decay(gs, bs, incl):
    g2 = jnp.concatenate([gs, gs], axis=1)
    diff = g2 - g2.T
    dec = jnp.where(incl, jnp.exp(jnp.where(incl, diff, 0.0)), 0.0)
    return dec, jnp.concatenate([bs, bs], axis=1).T


def _head_mask():
    r = lax.broadcasted_iota(jnp.int32, (GDN_STACK, GDN_WIDTH), 0)
    c = lax.broadcasted_iota(jnp.int32, (GDN_STACK, GDN_WIDTH), 1)
    return (r & -CHUNK) * (GDN_DIM // CHUNK) == (c & -GDN_DIM)


def _head_spread(x):
    return jnp.where(_head_mask(), jnp.concatenate([x] * GDN_HEADS, axis=1), 0.0)


def _head_diag(x):
    xm = jnp.where(_head_mask(), x, 0.0)
    out = xm[:, 0:GDN_DIM]
    for h in range(1, GDN_HEADS):
        out = out + xm[:, GDN_DIM * h:GDN_DIM * (h + 1)]
    return out


def _last_rows(gs, n):
    return jnp.concatenate([jnp.broadcast_to(gs[CHUNK * (h + 1) - 1:CHUNK * (h + 1)], (n, GDN_DIM)) for h in range(GDN_HEADS)], axis=0)


def _gdn_chunk_fwd(qn, kn, v, gcb, bb, *, name):
    S = qn.shape[0]

    def body(qn_ref, kn_ref, v_ref, gcb_ref, bb_ref, uv_ref, wk_ref, at_ref, t_ref, wkb_ref, qdb_ref, keb_ref):
        incl, strict, diag = _stack_masks()
        for c in range(CHUNKS_PER_STEP):
            rows = slice(CHUNK * c, CHUNK * (c + 1))
            srows = slice(GDN_STACK * c, GDN_STACK * (c + 1))
            q, k, vv, gs, bs = [_stack(r, rows) for r in (qn_ref, kn_ref, v_ref, gcb_ref, bb_ref)]
            dec, bt = _stack_decay(gs, bs, incl)
            p = -jnp.where(strict, dec * _bdot_nt(k, k) * bt, 0.0)
            t = jnp.where(diag, 1.0, 0.0) + p
            for _ in range(5):
                p = _bdot(p, p)
                t = t + _bdot(t, p)
            sol = _dot3(t, jnp.concatenate([vv, jnp.exp(gs) * k], axis=1), ((1,), (0,)))
            _unstack_to(uv_ref, rows, sol[:, :GDN_DIM])
            _unstack_to(wk_ref, rows, sol[:, GDN_DIM:])
            at_ref[srows, :] = dec * _bdot_nt(q, k) * bt
            t_ref[srows, :] = t
            wkb_ref[srows, :] = _head_spread(sol[:, GDN_DIM:]).astype(BF16)
            qdb_ref[srows, :] = _head_spread(q * jnp.exp(gs)).astype(BF16)
            keb_ref[srows, :] = _head_spread(k * jnp.exp(_last_rows(gs, CHUNK) - gs) * bs).astype(BF16)

    step = CHUNKS_PER_STEP * CHUNK
    row = pl.BlockSpec((step, GDN_WIDTH), lambda n: (n, 0))
    sq = pl.BlockSpec((CHUNKS_PER_STEP * GDN_STACK, GDN_STACK), lambda n: (n, 0))
    wide = pl.BlockSpec((CHUNKS_PER_STEP * GDN_STACK, GDN_WIDTH), lambda n: (n, 0))
    nsq = S // CHUNK * GDN_STACK
    return pl.pallas_call(
        body, name=name, grid=(S // step,), in_specs=[row] * 5, out_specs=[row, row, sq, sq, wide, wide, wide],
        out_shape=[_sds((S, GDN_WIDTH)), _sds((S, GDN_WIDTH)), _sds((nsq, GDN_STACK)), _sds((nsq, GDN_STACK))]
        + [_sds((nsq, GDN_WIDTH), BF16)] * 3,
        compiler_params=_params(("parallel",)),
    )(qn, kn, v, gcb, bb)


def _gdn_scan_fwd(uv, at, wkb, qdb, keb, gcb, proj_z, gnw, *, name):
    S = uv.shape[0]
    nc = S // CHUNK
    rows = slice(None)

    def body(uv_ref, at_ref, wkb_ref, qdb_ref, keb_ref, gcb_ref, z_ref, gnw_ref, o_ref, u_ref, sp_ref, oa_ref, st_ref):
        n = pl.program_id(0)

        @pl.when(n == 0)
        def _():
            st_ref[...] = jnp.zeros_like(st_ref)

        st = st_ref[...]
        sp_ref[...] = st
        uv, gs, z = [_stack(r, rows) for r in (uv_ref, gcb_ref, z_ref)]
        u = uv - _bdot(wkb_ref[...], st)
        o = _bdot(qdb_ref[...], st) + _bdot(at_ref[...], u)
        st_ref[...] = jnp.exp(_last_rows(gs, GDN_DIM)) * st + _bdot_tn(keb_ref[...], u)
        _unstack_to(u_ref, rows, u)
        _unstack_to(o_ref, rows, o)
        r = lax.rsqrt(jnp.mean(o * o, axis=-1, keepdims=True) + EPS)
        oa = o * r * gnw_ref[...] * (z * _sigmoid(z))
        oa_ref[...] = jnp.concatenate([oa[CHUNK * h:CHUNK * (h + 1)] for h in range(GDN_HEADS)], axis=1).astype(BF16)

    row = pl.BlockSpec((CHUNK, GDN_WIDTH), lambda n: (n, 0))
    sq = pl.BlockSpec((GDN_STACK, GDN_STACK), lambda n: (n, 0))
    wide = pl.BlockSpec((GDN_STACK, GDN_WIDTH), lambda n: (n, 0))
    return pl.pallas_call(
        body, name=name, grid=(nc,),
        in_specs=[row, sq, wide, wide, wide, row, row, pl.BlockSpec((1, GDN_DIM), lambda n: (0, 0))],
        out_specs=[row, row, pl.BlockSpec((GDN_WIDTH, GDN_DIM), lambda n: (n, 0)), row],
        out_shape=[_sds((S, GDN_WIDTH)), _sds((S, GDN_WIDTH)), _sds((nc * GDN_WIDTH, GDN_DIM)), _sds((S, 2 * GDN_WIDTH), BF16)],
        scratch_shapes=[pltpu.VMEM((GDN_WIDTH, GDN_DIM), F32)],
        compiler_params=_params(("arbitrary",)),
    )(uv, at, wkb, qdb, keb, gcb, proj_z, gnw)


def _gdn_scan_bwd(d_oab, o, proj_z, gnw, sp, u, at, wkb, qdb, keb, gcb, *, name):
    S = o.shape[0]
    nc = S // CHUNK
    rows = slice(None)

    def body(do_ref, o_ref, z_ref, gnw_ref, sp_ref, u_ref, at_ref, wkb_ref, qdb_ref, keb_ref, gcb_ref,
             dz_ref, dgn_ref, du_ref, dwk_ref, dat_ref, dqd_ref, dke_ref, dgl_ref, ds_ref):
        n = pl.program_id(0)

        @pl.when(n == 0)
        def _():
            ds_ref[...] = jnp.zeros_like(ds_ref)
            dgn_ref[...] = jnp.zeros_like(dgn_ref)

        d_oa, oo, z, uu, gs = [_stack(r, rows) for r in (do_ref, o_ref, z_ref, u_ref, gcb_ref)]
        gw = gnw_ref[...]
        sg = _sigmoid(z)
        r = lax.rsqrt(jnp.mean(oo * oo, axis=-1, keepdims=True) + EPS)
        xh = oo * r
        dy = d_oa * (z * sg)
        _unstack_to(dz_ref, rows, d_oa * (xh * gw) * _dsilu(z, sg))
        dgn_ref[...] += jnp.sum(dy * xh, axis=0, keepdims=True)
        dxh = dy * gw
        do = r * (dxh - xh * jnp.mean(dxh * xh, axis=-1, keepdims=True))

        st = sp_ref[...]
        dst = ds_ref[...]
        ge = jnp.exp(_last_rows(gs, GDN_DIM))
        _unstack_to(dqd_ref, rows, _head_diag(_bdot_nt(do, st)))
        dat_ref[...] = _bdot_nt(do, uu)
        du = _bdot_tn(at_ref[...], do) + _bdot(keb_ref[...], dst)
        _unstack_to(dke_ref, rows, _head_diag(_bdot_nt(uu, dst)))
        prod = dst * st
        for h in range(GDN_HEADS):
            blk = prod[GDN_DIM * h:GDN_DIM * (h + 1)]
            dge = jnp.sum(jnp.sum(blk, axis=1, keepdims=True), axis=0, keepdims=True)
            dgl_ref[0, :, GDN_DIM * h:GDN_DIM * (h + 1)] = jnp.broadcast_to(dge * ge[GDN_DIM * h:GDN_DIM * h + 1], (8, GDN_DIM))
        ds_ref[...] = _bdot_tn(qdb_ref[...], do) + ge * dst - _bdot_tn(wkb_ref[...], du)
        _unstack_to(du_ref, rows, du)
        _unstack_to(dwk_ref, rows, -_head_diag(_bdot_nt(du, st)))

    rev = lambda n: (nc - 1 - n, 0)
    row = pl.BlockSpec((CHUNK, GDN_WIDTH), rev)
    sq = pl.BlockSpec((GDN_STACK, GDN_STACK), rev)
    wide = pl.BlockSpec((GDN_STACK, GDN_WIDTH), rev)
    one = pl.BlockSpec((1, GDN_DIM), lambda n: (0, 0))
    return pl.pallas_call(
        body, name=name, grid=(nc,),
        in_specs=[row, row, row, one, pl.BlockSpec((GDN_WIDTH, GDN_DIM), rev), row, sq, wide, wide, wide, row],
        out_specs=[row, one, row, row, sq, row, row, pl.BlockSpec((1, 8, GDN_WIDTH), lambda n: (nc - 1 - n, 0, 0))],
        out_shape=[_sds((S, GDN_WIDTH), BF16), _sds((1, GDN_DIM)), _sds((S, GDN_WIDTH)), _sds((S, GDN_WIDTH)),
                   _sds((nc * GDN_STACK, GDN_STACK)), _sds((S, GDN_WIDTH)), _sds((S, GDN_WIDTH)), _sds((nc, 8, GDN_WIDTH))],
        scratch_shapes=[pltpu.VMEM((GDN_WIDTH, GDN_DIM), F32)],
        compiler_params=_params(("arbitrary",)),
    )(d_oab, o, proj_z, gnw, sp, u, at, wkb, qdb, keb, gcb)


def _gdn_chunk_bwd(qn, kn, gcb, bb, tmat, uv, wk, du, dwk, dat, dqd, dke, dgl, *, name):
    S = qn.shape[0]

    def body(qn_ref, kn_ref, gcb_ref, bb_ref, t_ref, uv_ref, wk_ref, du_ref, dwk_ref, dat_ref, dqd_ref, dke_ref,
             dgl_ref, dq_ref, dk_ref, dv_ref, dg_ref, dbeta_ref):
        incl, strict, _ = _stack_masks()
        lane = lax.broadcasted_iota(jnp.int32, (CHUNK, 128), 1)
        rowi = lax.broadcasted_iota(jnp.int32, (CHUNK, 1), 0)
        rsum = lambda x: jnp.sum(x, axis=-1, keepdims=True)
        for c in range(CHUNKS_PER_STEP):
            rows = slice(CHUNK * c, CHUNK * (c + 1))
            srows = slice(GDN_STACK * c, GDN_STACK * (c + 1))
            q, k, gs, bs, uv, wk, du, dwk, dqd, dke = [
                _stack(r, rows) for r in (qn_ref, kn_ref, gcb_ref, bb_ref, uv_ref, wk_ref, du_ref, dwk_ref, dqd_ref, dke_ref)]
            dec, bt = _stack_decay(gs, bs, incl)
            kk = _bdot_nt(k, k)
            qk = _bdot_nt(q, k)
            d_rhs = _dot3(t_ref[srows, :], jnp.concatenate([du, dwk], axis=1), ((0,), (0,)))
            sol = jnp.concatenate([uv, wk], axis=1)
            d_l = jnp.where(strict, -_dot3(d_rhs, sol, ((1,), (1,))), 0.0)
            d_a = jnp.where(incl, dat_ref[srows, :], 0.0)
            gam = jnp.exp(gs)
            e = jnp.exp(_last_rows(gs, CHUNK) - gs)
            d_gk = d_rhs[:, GDN_DIM:]
            ml = d_l * dec * bt
            ma = d_a * dec * bt
            _unstack_to(dq_ref, rows, _bdot(ma, k) + dqd * gam)
            _unstack_to(dk_ref, rows, _bdot(ml + ml.T, k) + _bdot_tn(ma, q) + d_gk * gam + dke * (e * bs))
            _unstack_to(dv_ref, rows, d_rhs[:, :GDN_DIM])
            wb = d_l * dec * kk + d_a * dec * qk
            ew = wb * bt
            s_ke = rsum(dke * k * (e * bs))
            dbeta = rsum(wb.T) + rsum(dke * k * e)
            dgc = rsum(ew) - rsum(ew.T) + rsum(dqd * q * gam) + rsum(d_gk * k * gam) - s_ke
            dgc4 = jnp.zeros((CHUNK, 128), F32)
            db4 = jnp.zeros((CHUNK, 128), F32)
            for h in range(GDN_HEADS):
                hr = slice(CHUNK * h, CHUNK * (h + 1))
                tail = jnp.sum(s_ke[hr], axis=0, keepdims=True) + dgl_ref[c, 0:1, GDN_DIM * h:GDN_DIM * h + 1]
                dgc4 = jnp.where(lane == h, dgc[hr] + jnp.where(rowi == CHUNK - 1, tail, 0.0), dgc4)
                db4 = jnp.where(lane == h, dbeta[hr], db4)
            dg_ref[rows, :] = _exact_tri_dot(_chunk_tri(CHUNK, upper=True), dgc4)
            dbeta_ref[rows, :] = db4

    step = CHUNKS_PER_STEP * CHUNK
    row = pl.BlockSpec((step, GDN_WIDTH), lambda n: (n, 0))
    sq = pl.BlockSpec((CHUNKS_PER_STEP * GDN_STACK, GDN_STACK), lambda n: (n, 0))
    col = pl.BlockSpec((step, 128), lambda n: (n, 0))
    return pl.pallas_call(
        body, name=name, grid=(S // step,),
        in_specs=[row] * 4 + [sq, row, row, row, row, sq, row, row,
                              pl.BlockSpec((CHUNKS_PER_STEP, 8, GDN_WIDTH), lambda n: (n, 0, 0))],
        out_specs=[row, row, row, col, col],
        out_shape=[_sds((S, GDN_WIDTH))] * 3 + [_sds((S, 128))] * 2, compiler_params=_params(("parallel",)),
    )(qn, kn, gcb, bb, tmat, uv, wk, du, dwk, dat, dqd, dke, dgl)


def _gdn_prep_bwd(dqn, dkn, dv, dg, dbeta, proj_a, conv_w, a_log, dt_bias, *, name, tm=256):
    S = proj_a.shape[0]
    nblk = S // tm
    W3 = 3 * GDN_WIDTH

    def body(dqn_ref, dkn_ref, dv_ref, dg_ref, dbeta_ref, cur_ref, prev_ref, ba_ref, cw_ref, al_ref, dt_ref,
             dc_ref, dba_ref, sm_ref):
        i = pl.program_id(0)
        prev = jnp.where(i > 0, prev_ref[...], 0.0)
        c = _conv_rows(prev, cur_ref[...], cw_ref[...], GDN_CONV)
        sg = _sigmoid(c)
        a = c * sg
        dsl = _dsilu(c, sg)
        ba = ba_ref[...]
        lane = lax.broadcasted_iota(jnp.int32, (tm, 128), 1)
        lane1 = lax.broadcasted_iota(jnp.int32, (1, 128), 1)
        dba = jnp.zeros((tm, 128), F32)
        sm = jnp.zeros((1, 128), F32)
        for h in range(GDN_HEADS):
            sl = slice(GDN_DIM * h, GDN_DIM * (h + 1))
            ks = slice(GDN_WIDTH + GDN_DIM * h, GDN_WIDTH + GDN_DIM * (h + 1))
            qh, kh = a[:, sl], a[:, ks]
            rq = lax.rsqrt(jnp.sum(qh * qh, axis=-1, keepdims=True) + EPS)
            rk = lax.rsqrt(jnp.sum(kh * kh, axis=-1, keepdims=True) + EPS)
            qhat, khat = qh * rq, kh * rk
            dyq = dqn_ref[:, sl] * (GDN_DIM ** -0.5)
            dyk = dkn_ref[:, sl]
            dq = rq * (dyq - qhat * jnp.sum(dyq * qhat, axis=-1, keepdims=True))
            dk = rk * (dyk - khat * jnp.sum(dyk * khat, axis=-1, keepdims=True))
            dc_ref[:, sl] = dq * dsl[:, sl]
            dc_ref[:, ks] = dk * dsl[:, ks]
            beta = _sigmoid(ba[:, h:h + 1])
            db = dbeta_ref[:, h:h + 1] * beta * (1.0 - beta)
            aneg = -jnp.exp(al_ref[0:1, h:h + 1])
            xa = ba[:, GDN_HEADS + h:GDN_HEADS + h + 1] + dt_ref[0:1, h:h + 1]
            dgh = dg_ref[:, h:h + 1]
            dxa = dgh * aneg * _sigmoid(xa)
            dba = jnp.where(lane == h, db, dba)
            dba = jnp.where(lane == GDN_HEADS + h, dxa, dba)
            d_alog = jnp.sum(dgh * _softplus(xa), axis=0, keepdims=True) * aneg
            sm = jnp.where(lane1 == h, d_alog, sm)
            sm = jnp.where(lane1 == GDN_HEADS + h, jnp.sum(dxa, axis=0, keepdims=True), sm)
        vs = slice(2 * GDN_WIDTH, W3)
        dc_ref[:, vs] = dv_ref[...] * dsl[:, vs]
        dba_ref[...] = dba

        @pl.when(i == 0)
        def _():
            sm_ref[...] = sm

        @pl.when(i > 0)
        def _():
            sm_ref[...] += sm

    prev_spec, _ = _halo_specs(tm, W3, 0, nblk)
    row = pl.BlockSpec((tm, GDN_WIDTH), lambda i: (i, 0))
    col = pl.BlockSpec((tm, 128), lambda i: (i, 0))
    small = lambda a: pl.BlockSpec(a.shape, lambda i: (0, 0))
    return pl.pallas_call(
        body, name=name, grid=(nblk,),
        in_specs=[row, row, row, col, col, pl.BlockSpec((tm, W3), lambda i: (i, 0)), prev_spec,
                  pl.BlockSpec((tm, 128), lambda i: (i, W3 // 128)), small(conv_w), small(a_log), small(dt_bias)],
        out_specs=[pl.BlockSpec((tm, W3), lambda i: (i, 0)), col, pl.BlockSpec((1, 128), lambda i: (0, 0))],
        out_shape=[_sds((S, W3)), _sds((S, 128)), _sds((1, 128))], compiler_params=_params(("arbitrary",)),
    )(dqn, dkn, dv, dg, dbeta, proj_a, proj_a, proj_a, conv_w, a_log, dt_bias)


def _gdn_conv_bwd(dc, dba, proj_a, conv_w, *, name, tm=256):
    S = proj_a.shape[0]
    nblk = S // tm
    W3 = 3 * GDN_WIDTH

    def body(dc_ref, dnext_ref, dba_ref, cur_ref, prev_ref, cw_ref, da_ref, dcw_ref):
        i = pl.program_id(0)
        prev = jnp.where(i > 0, prev_ref[...], 0.0)
        nxt = jnp.where(i < nblk - 1, dnext_ref[...], 0.0)
        dx, dw = _conv_rows_bwd(dc_ref[...], nxt, prev, cur_ref[...], cw_ref[...], GDN_CONV)
        da_ref[:, 0:W3] = dx.astype(BF16)
        da_ref[:, W3:] = dba_ref[...].astype(BF16)

        @pl.when(i == 0)
        def _():
            dcw_ref[...] = dw

        @pl.when(i > 0)
        def _():
            dcw_ref[...] += dw

    prev_spec, next_spec = _halo_specs(tm, W3, 0, nblk)
    wide = pl.BlockSpec((tm, W3), lambda i: (i, 0))
    return pl.pallas_call(
        body, name=name, grid=(nblk,),
        in_specs=[wide, next_spec, pl.BlockSpec((tm, 128), lambda i: (i, 0)), wide, prev_spec,
                  pl.BlockSpec(conv_w.shape, lambda i: (0, 0))],
        out_specs=[pl.BlockSpec((tm, A_COLS), lambda i: (i, 0)), pl.BlockSpec(conv_w.shape, lambda i: (0, 0))],
        out_shape=[_sds((S, A_COLS), BF16), _sds(conv_w.shape)], compiler_params=_params(("arbitrary",)),
    )(dc, dc, dba, proj_a, proj_a, conv_w)


def _band_mask(nk):
    i = lax.broadcasted_iota(jnp.int32, (2 * BAND, nk), 0) & (BAND - 1)
    j = lax.broadcasted_iota(jnp.int32, (2 * BAND, nk), 1)
    if nk == BAND:
        return j <= i
    return (j >= i) & (j <= i + BAND)


def _stack_heads(x, lo):
    return jnp.concatenate([jnp.where(lo, x, 0.0), jnp.where(lo, 0.0, x)], axis=0)


def _stack_cols(x):
    return jnp.concatenate([x[:, 0:1], x[:, DIL_DIM:DIL_DIM + 1]], axis=0)


def _unstack(x, lo):
    return jnp.where(lo, x[0:BAND], x[BAND:2 * BAND])


def _rows(start, size, stride):
    return pl.ds(start, size) if stride == 1 else pl.ds(start, size, stride=stride)


ATTN_LANES = 4


def _attn_blocks(S, visit_many, lanes=ATTN_LANES):
    for d in DILATIONS:
        nb = S // (d * BAND)
        if d == 1:
            half = nb // 2
            visit_many(d, [(0, 0, True), (0, half, False)])

            def pair(n, c):
                visit_many(1, [(0, n, False), (0, n + half, False)])
                return c
            lax.fori_loop(1, half, pair, 0)
        elif nb > 1:
            for r0 in range(0, d, lanes):
                visit_many(d, [(r0 + t, 0, True) for t in range(lanes)])

                def column(n, c, d=d, r0=r0):
                    visit_many(d, [(r0 + t, n, False) for t in range(lanes)])
                    return c
                lax.fori_loop(1, nb, column, 0)
        else:
            def group(g, c, d=d):
                visit_many(d, [(g * lanes + t, 0, True) for t in range(lanes)])
                return c
            lax.fori_loop(0, d // lanes, group, 0)


def _attn_fwd(proj_b, oab, *, name):
    S = proj_b.shape[0]
    scale = DIL_DIM ** -0.5

    def body(q_ref, k_ref, v_ref, oab_in_ref, ob_ref, lse_ref, m_ref, l_ref, acc_ref):
        del oab_in_ref
        lane = lax.broadcasted_iota(jnp.int32, (BAND, 128), 1)
        lo = lane < DIL_DIM
        m_ref[...] = jnp.full_like(m_ref, NEG_BIG)
        l_ref[...] = jnp.zeros_like(l_ref)
        acc_ref[...] = jnp.zeros_like(acc_ref)

        def load(d, r, n, first):
            nk = BAND if first else 2 * BAND
            qrows = _rows(r + n * (BAND * d), BAND, d)
            krows = _rows(r if first else r + (n - 1) * (BAND * d), nk, d)
            return dict(nk=nk, qrows=qrows, q=q_ref[qrows, :] * scale, k=k_ref[krows, :].astype(BF16),
                        v=v_ref[krows, :].astype(BF16), m=m_ref[qrows, :], l=l_ref[qrows, :], acc=acc_ref[qrows, :])

        def compute(b):
            q, k, v = b["q"], b["k"], b["v"]
            s = jnp.where(_band_mask(b["nk"]), _bdot_nt(_stack_heads(q, lo), k), NEG_BIG)
            m_old = _stack_cols(b["m"])
            m_new = jnp.maximum(m_old, jnp.max(s, axis=-1, keepdims=True))
            p = jnp.exp(s - m_new)
            alpha = _unstack(jnp.exp(m_old - m_new), lo)
            l_new = alpha * b["l"] + _unstack(jnp.sum(p, axis=-1, keepdims=True), lo)
            return _unstack(m_new, lo), l_new, alpha * b["acc"] + _unstack(_bdot(p, v), lo)

        def visit_many(d, blocks):
            loaded = [load(d, *blk) for blk in blocks]
            done = [compute(b) for b in loaded]
            for b, (m_new, l_new, acc_new) in zip(loaded, done):
                m_ref[b["qrows"], :] = m_new
                l_ref[b["qrows"], :] = l_new
                acc_ref[b["qrows"], :] = acc_new

        _attn_blocks(S, visit_many)
        ob_ref[...] = (acc_ref[...] / l_ref[...]).astype(BF16)
        lse_ref[...] = m_ref[...] + jnp.log(l_ref[...])

    part = lambda t: pl.BlockSpec((S, 128), lambda p: (0, 3 * p + t))
    return pl.pallas_call(
        body, name=name, grid=(DIL_PAIRS,),
        in_specs=[part(0), part(1), part(2), pl.BlockSpec(memory_space=pl.ANY)],
        out_specs=[pl.BlockSpec((S, 128), lambda p: (0, GDN_WIDTH // 128 + p)), pl.BlockSpec((S, 128), lambda p: (0, p))],
        out_shape=[_sds(oab.shape, BF16), _sds((S, DIL_WIDTH))],
        scratch_shapes=[pltpu.VMEM((S, 128), F32)] * 3, input_output_aliases={3: 0},
        compiler_params=_params(("parallel",)),
    )(proj_b, proj_b, proj_b, oab)


def _attn_bwd(proj_b, oab, d_oab, lse, *, name):
    S = proj_b.shape[0]
    scale = DIL_DIM ** -0.5

    def body(q_ref, k_ref, v_ref, o_ref, do_ref, lse_ref, dqkv_ref, dq_ref, dk_ref, dv_ref, delta_ref):
        lane = lax.broadcasted_iota(jnp.int32, (BAND, 128), 1)
        lo = lane < DIL_DIM
        dq_ref[...] = jnp.zeros_like(dq_ref)
        dk_ref[...] = jnp.zeros_like(dk_ref)
        dv_ref[...] = jnp.zeros_like(dv_ref)
        prod = do_ref[...] * o_ref[...].astype(F32)
        lo_all = lax.broadcasted_iota(jnp.int32, (S, 128), 1) < DIL_DIM
        delta_ref[...] = jnp.where(lo_all, jnp.sum(jnp.where(lo_all, prod, 0.0), axis=-1, keepdims=True),
                                   jnp.sum(jnp.where(lo_all, 0.0, prod), axis=-1, keepdims=True))

        def load(d, r, n, first):
            nk = BAND if first else 2 * BAND
            qrows = _rows(r + n * (BAND * d), BAND, d)
            krows = _rows(r if first else r + (n - 1) * (BAND * d), nk, d)
            return dict(nk=nk, qrows=qrows, krows=krows, q=q_ref[qrows, :] * scale, k=k_ref[krows, :], v=v_ref[krows, :],
                        do=do_ref[qrows, :], delta=delta_ref[qrows, :], lse=lse_ref[qrows, :],
                        dq=dq_ref[qrows, :], dk=dk_ref[krows, :], dv=dv_ref[krows, :])

        def compute(b):
            q, k, v, do = b["q"], b["k"], b["v"], b["do"]
            qs, dos = _stack_heads(q, lo), _stack_heads(do, lo)
            p = jnp.where(_band_mask(b["nk"]), jnp.exp(_bdot_nt(qs, k) - _stack_cols(b["lse"])), 0.0)
            ds = p * (_bdot_nt(dos, v) - _stack_cols(b["delta"]))
            dq = b["dq"] + _unstack(_bdot(ds, k), lo) * scale
            return dq, b["dk"] + _bdot_tn(ds, qs), b["dv"] + _bdot_tn(p, dos)

        def visit_many(d, blocks):
            loaded = [load(d, *blk) for blk in blocks]
            done = [compute(b) for b in loaded]
            for b, (dq, dk, dv) in zip(loaded, done):
                dq_ref[b["qrows"], :] = dq
                dk_ref[b["krows"], :] = dk
                dv_ref[b["krows"], :] = dv

        _attn_blocks(S, visit_many, lanes=2)
        dqkv_ref[:, 0:128] = dq_ref[...].astype(BF16)
        dqkv_ref[:, 128:256] = dk_ref[...].astype(BF16)
        dqkv_ref[:, 256:384] = dv_ref[...].astype(BF16)

    half = lambda p: (0, GDN_WIDTH // 128 + p)
    part = lambda t: pl.BlockSpec((S, 128), lambda p: (0, 3 * p + t))
    return pl.pallas_call(
        body, name=name, grid=(DIL_PAIRS,),
        in_specs=[part(0), part(1), part(2), pl.BlockSpec((S, 128), half), pl.BlockSpec((S, 128), half),
                  pl.BlockSpec((S, 128), lambda p: (0, p))],
        out_specs=pl.BlockSpec((S, 384), lambda p: (0, p)), out_shape=_sds((S, 3 * DIL_WIDTH), BF16),
        scratch_shapes=[pltpu.VMEM((S, 128), F32)] * 4, compiler_params=_params(("parallel",)),
    )(proj_b, proj_b, proj_b, oab, d_oab, lse)


FF_SLAB = 2 * D_FF // N_DEV
FF_PAIRS = N_DEV // 2
ROWS16 = 16


def _taps(w, x, base, n):
    out = _shifted(x, base, n) * w[0:1]
    for t in range(1, FFN_CONV):
        out = out + _shifted(x, base + t, n) * w[t:t + 1]
    return out


def _ffn_fwd(h2, x1, w_up, conv_w, w_down, *, name, tm=512):
    S, D = h2.shape
    ni = S // tm
    per = tm // ROWS16

    def body(h_ref, hp_ref, x1_ref, wg_ref, wu_ref, cg_ref, cu_ref, wd_ref, x2_ref, ug_ref, uu_ref):
        i, j = pl.program_id(0), pl.program_id(1)
        hv = jnp.concatenate([hp_ref[...], h_ref[...]], axis=0)
        row = lax.broadcasted_iota(jnp.int32, (tm + ROWS16, 1), 0)
        keep = (i > 0) | (row >= ROWS16)

        def branch(w_ref, c_ref, u_ref):
            u = lax.dot_general(hv, w_ref[...], (((1,), (1,)), ((), ())), preferred_element_type=F32).astype(BF16)
            u_ref[...] = u[ROWS16:]
            return _taps(c_ref[...], jnp.where(keep, u.astype(F32), 0.0), ROWS16 - (FFN_CONV - 1), tm)

        gate = branch(wg_ref, cg_ref, ug_ref)
        up = branch(wu_ref, cu_ref, uu_ref)
        act = (gate * _sigmoid(gate) * up).astype(BF16)
        part = jnp.dot(act, wd_ref[...], preferred_element_type=F32)

        @pl.when(j == 0)
        def _():
            x2_ref[...] = x1_ref[...] + part

        @pl.when(j > 0)
        def _():
            x2_ref[...] += part

    rows = pl.BlockSpec((tm, D), lambda i, j: (i, 0))
    slab = lambda off: pl.BlockSpec((None, FF_SLAB, D), lambda i, j: (j + off, 0, 0))
    cslab = lambda off: pl.BlockSpec((None, FFN_CONV, FF_SLAB), lambda i, j: (j + off, 0, 0))
    uspec = pl.BlockSpec((None, tm, FF_SLAB), lambda i, j: (j, i, 0))
    return pl.pallas_call(
        body, name=name, grid=(ni, FF_PAIRS),
        in_specs=[rows, pl.BlockSpec((ROWS16, D), lambda i, j: (jnp.maximum(i * per - 1, 0), 0)), rows,
                  slab(0), slab(FF_PAIRS), cslab(0), cslab(FF_PAIRS), pl.BlockSpec((FF_SLAB, D), lambda i, j: (j, 0))],
        out_specs=[rows, uspec, uspec],
        out_shape=[_sds((S, D)), _sds((FF_PAIRS, S, FF_SLAB), BF16), _sds((FF_PAIRS, S, FF_SLAB), BF16)],
        compiler_params=_params(("parallel", "arbitrary")),
    )(h2, h2, x1, w_up, w_up, conv_w, conv_w, w_down)


def _ffn_bwd(dx2, h2, ug, uu, conv_w, w_down, *, name, tm=512):
    S, D = h2.shape
    ni = S // tm
    per = tm // ROWS16
    ext = tm + ROWS16

    def body(dx_ref, dxn_ref, h_ref, ug_ref, ugp_ref, ugn_ref, uu_ref, uup_ref, uun_ref, cg_ref, cu_ref, wd_ref,
             dug_ref, duu_ref, gd_ref, gg_ref, gu_ref, dcg_ref, dcu_ref, acc_d, acc_g, acc_u, acc_cg, acc_cu):
        i = pl.program_id(1)

        @pl.when(i == 0)
        def _():
            acc_d[...] = jnp.zeros_like(acc_d)
            acc_g[...] = jnp.zeros_like(acc_g)
            acc_u[...] = jnp.zeros_like(acc_u)
            acc_cg[...] = jnp.zeros_like(acc_cg)
            acc_cu[...] = jnp.zeros_like(acc_cu)

        dx = dx_ref[...]
        dxe = jnp.concatenate([dx, dxn_ref[...]], axis=0)
        row = lax.broadcasted_iota(jnp.int32, (ext, 1), 0)
        live = (i < ni - 1) | (row < tm)
        d_act = jnp.where(live, lax.dot_general(dxe, wd_ref[...], (((1,), (1,)), ((), ())), preferred_element_type=F32), 0.0)
        rowp = lax.broadcasted_iota(jnp.int32, (ext + ROWS16, 1), 0)
        keep = (i > 0) | (rowp >= ROWS16)

        def pre(cur, prev, nxt):
            return jnp.where(keep, jnp.concatenate([prev[...], cur[...], nxt[...]], axis=0).astype(F32), 0.0)

        uge, uue = pre(ug_ref, ugp_ref, ugn_ref), pre(uu_ref, uup_ref, uun_ref)
        cg, cu = cg_ref[...], cu_ref[...]
        base = ROWS16 - (FFN_CONV - 1)
        gate = _taps(cg, uge, base, ext)
        up = _taps(cu, uue, base, ext)
        sg = _sigmoid(gate)
        silu = gate * sg
        dgc = d_act * up * _dsilu(gate, sg)
        duc = d_act * silu

        def conv_t(w, dc):
            out = _shifted(dc, FFN_CONV - 1, tm) * w[0:1]
            for t in range(1, FFN_CONV):
                out = out + _shifted(dc, FFN_CONV - 1 - t, tm) * w[t:t + 1]
            return out.astype(BF16)

        du_g, du_u = conv_t(cg, dgc), conv_t(cu, duc)
        dug_ref[...] = du_g
        duu_ref[...] = du_u
        dcw = lambda dc, xe: jnp.concatenate(
            [jnp.sum(dc[0:tm] * _shifted(xe, base + t, tm), axis=0, keepdims=True) for t in range(FFN_CONV)], axis=0)
        acc_cg[0:FFN_CONV, :] += dcw(dgc, uge)
        acc_cu[0:FFN_CONV, :] += dcw(duc, uue)
        tn = (((0,), (0,)), ((), ()))
        act = (silu[0:tm] * up[0:tm]).astype(BF16)
        acc_d[...] += lax.dot_general(act, dx, tn, preferred_element_type=F32)
        hv = h_ref[...]
        acc_g[...] += lax.dot_general(du_g, hv, tn, preferred_element_type=F32)
        acc_u[...] += lax.dot_general(du_u, hv, tn, preferred_element_type=F32)

        @pl.when(i == ni - 1)
        def _():
            gd_ref[...] = acc_d[...].astype(BF16)
            gg_ref[...] = acc_g[...].astype(BF16)
            gu_ref[...] = acc_u[...].astype(BF16)
            dcg_ref[...] = acc_cg[0:FFN_CONV, :]
            dcu_ref[...] = acc_cu[0:FFN_CONV, :]

    last16 = S // ROWS16 - 1
    rows = pl.BlockSpec((tm, D), lambda j, i: (i, 0))
    rows_next = pl.BlockSpec((ROWS16, D), lambda j, i: (jnp.minimum((i + 1) * per, last16), 0))
    u_cur = pl.BlockSpec((None, tm, FF_SLAB), lambda j, i: (j, i, 0))
    u_prev = pl.BlockSpec((None, ROWS16, FF_SLAB), lambda j, i: (j, jnp.maximum(i * per - 1, 0), 0))
    u_next = pl.BlockSpec((None, ROWS16, FF_SLAB), lambda j, i: (j, jnp.minimum((i + 1) * per, last16), 0))
    cslab = lambda off: pl.BlockSpec((None, FFN_CONV, FF_SLAB), lambda j, i: (j + off, 0, 0))
    wslab = pl.BlockSpec((None, FF_SLAB, D), lambda j, i: (j, 0, 0))
    dslab = pl.BlockSpec((None, FFN_CONV, FF_SLAB), lambda j, i: (j, 0, 0))
    return pl.pallas_call(
        body, name=name, grid=(FF_PAIRS, ni),
        in_specs=[rows, rows_next, rows, u_cur, u_prev, u_next, u_cur, u_prev, u_next, cslab(0), cslab(FF_PAIRS),
                  pl.BlockSpec((FF_SLAB, D), lambda j, i: (j, 0))],
        out_specs=[u_cur, u_cur, pl.BlockSpec((FF_SLAB, D), lambda j, i: (j, 0)), wslab, wslab, dslab, dslab],
        out_shape=[_sds((FF_PAIRS, S, FF_SLAB), BF16), _sds((FF_PAIRS, S, FF_SLAB), BF16), _sds((D_FF, D), BF16),
                   _sds((FF_PAIRS, FF_SLAB, D), BF16), _sds((FF_PAIRS, FF_SLAB, D), BF16),
                   _sds((FF_PAIRS, FFN_CONV, FF_SLAB)), _sds((FF_PAIRS, FFN_CONV, FF_SLAB))],
        scratch_shapes=[pltpu.VMEM((FF_SLAB, D), F32), pltpu.VMEM((FF_SLAB, D), F32), pltpu.VMEM((FF_SLAB, D), F32),
                        pltpu.VMEM((8, FF_SLAB), F32), pltpu.VMEM((8, FF_SLAB), F32)],
        compiler_params=_params(("parallel", "arbitrary")),
    )(dx2, dx2, h2, ug, ug, ug, uu, uu, uu, conv_w, conv_w, w_down)


def _mm_slabs(a, w, w_off, *, name, res=None, tm=1024, tn=1024):
    nk, S, _ = a.shape
    D = w.shape[2]
    has_res = res is not None

    def body(*refs):
        if has_res:
            a_ref, w_ref, r_ref, o_ref, acc_ref = refs
        else:
            a_ref, w_ref, o_ref, acc_ref = refs
        k = pl.program_id(2)
        part = jnp.dot(a_ref[...], w_ref[...], preferred_element_type=F32)

        @pl.when(k == 0)
        def _():
            acc_ref[...] = part

        @pl.when(k > 0)
        def _():
            acc_ref[...] += part

        @pl.when(k == nk - 1)
        def _():
            o_ref[...] = acc_ref[...] + r_ref[...] if has_res else acc_ref[...]

    o_spec = pl.BlockSpec((tm, tn), lambda i, j, k: (i, j))
    return pl.pallas_call(
        body, name=name, grid=(S // tm, D // tn, nk),
        in_specs=[pl.BlockSpec((None, tm, FF_SLAB), lambda i, j, k: (k, i, 0)),
                  pl.BlockSpec((None, FF_SLAB, tn), lambda i, j, k: (k + w_off, 0, j))] + ([o_spec] if has_res else []),
        out_specs=o_spec, out_shape=_sds((S, D)), scratch_shapes=[pltpu.VMEM((tm, tn), F32)],
        compiler_params=_params(("parallel", "parallel", "arbitrary")),
    )(*((a, w, res) if has_res else (a, w)))


def _local_step(x, tgt, norm1_w, w_a, w_z, w_b, conv_a, a_log, dt_bias, gnw, norm2_w, final_w, late_weights, emit):
    wgrad = functools.partial(_mm, ta=True, out_dtype=BF16)
    h1 = _rms_fwd(x, norm1_w, name="rms1_fwd")
    proj_a = _mm(h1, w_a, tb=True, name="proj_a", tn=A_COLS, tk=1024)
    proj_z = _mm(h1, w_z, tb=True, name="proj_z", tk=1024)
    proj_b = _mm(h1, w_b, tb=True, name="proj_b", tn=768, tk=1024)
    qn, kn, v, gcb, bb = _gdn_prep_fwd(proj_a, conv_a, a_log, dt_bias, name="gdn_prep_fwd")
    uv, wk, at, tmat, wkb, qdb, keb = _gdn_chunk_fwd(qn, kn, v, gcb, bb, name="gdn_chunk_fwd")
    o, u, sp, oab = _gdn_scan_fwd(uv, at, wkb, qdb, keb, gcb, proj_z, gnw, name="gdn_scan_fwd")
    oab, lse = _attn_fwd(proj_b, oab, name="attn_fwd")
    w_out, w_up, conv_f, w_down = late_weights(oab)
    x1 = _mm(oab, w_out, res=x, name="out_proj", tk=1024)
    h2 = _rms_fwd(x1, norm2_w, name="rms2_fwd")
    x2, ug, uu = _ffn_fwd(h2, x1, w_up, conv_f, w_down, name="ffn_fwd")
    dx2, dx2_b, d_final, loss = _loss_head(x2, final_w, tgt, name="loss_head")
    dug, duu, g_down, g_up_g, g_up_u, dcw_g, dcw_u = _ffn_bwd(dx2_b, h2, ug, uu, conv_f, w_down, name="ffn_bwd")
    token = emit("ffn", w_down=g_down, w_up=jnp.concatenate([g_up_g, g_up_u], axis=0),
                 conv_f=jnp.concatenate([dcw_g, dcw_u], axis=0))
    dh2 = _mm_slabs(dug, w_up, 0, name="ffn_up_dx_gate")
    dh2 = _mm_slabs(duu, w_up, FF_PAIRS, res=dh2, name="ffn_up_dx_up")
    dx1, d_norm2 = _rms_bwd(dh2, x1, _behind(norm2_w, token), dx2, name="rms2_bwd")
    d_oab = _mm(dx1, w_out, tb=True, name="out_proj_dx", tk=1024)
    gnw = _behind(gnw, emit("out", w_out=wgrad(oab, dx1, name="out_proj_dw")))
    dz, d_gnw, du, dwk, dat, dqd, dke, dgl = _gdn_scan_bwd(d_oab, o, proj_z, gnw, sp, u, at, wkb, qdb, keb, gcb, name="gdn_scan_bwd")
    dqn, dkn, dv, dg, dbeta = _gdn_chunk_bwd(qn, kn, gcb, bb, tmat, uv, wk, du, dwk, dat, dqd, dke, dgl, name="gdn_chunk_bwd")
    dc, dba, d_small = _gdn_prep_bwd(dqn, dkn, dv, dg, dbeta, proj_a, conv_a, a_log, dt_bias, name="gdn_prep_bwd")
    d_pa, d_conv_a = _gdn_conv_bwd(dc, dba, proj_a, conv_a, name="gdn_conv_bwd")
    d_pb = _attn_bwd(proj_b, oab, d_oab, lse, name="attn_bwd")
    g_a = wgrad(d_pa, h1, name="proj_a_dw", tm=A_COLS)
    g_z = wgrad(dz, h1, name="proj_z_dw")
    g_b = wgrad(d_pb, h1, name="proj_b_dw", tm=768)
    w_z = _behind(w_z, emit("in", w_a=g_a, w_z=g_z, w_b=g_b, conv_a=d_conv_a))
    dh1 = _mm(dz, w_z, name="proj_z_dx")
    dh1 = _mm(d_pa, w_a, res=dh1, name="proj_a_dx", tk=A_COLS)
    dh1 = _mm(d_pb, w_b, res=dh1, name="proj_b_dx", tk=1536)
    grad_x, d_norm1 = _rms_bwd(dh1, x, norm1_w, dx1, name="rms1_bwd")
    small = dict(norm1=d_norm1, small=d_small, gnw=d_gnw, norm2=d_norm2, final=d_final)
    return loss, grad_x, small


_O1 = 3 * GDN_WIDTH
_O2 = _O1 + GDN_WIDTH
_O3 = _O2 + 2 * GDN_HEADS


def _split_w_in(w_t):
    d = w_t.shape[1]
    pad = jnp.zeros((A_COLS - _O1 - 2 * GDN_HEADS, d), w_t.dtype)
    w_a = jnp.concatenate([w_t[:_O1], w_t[_O2:_O3], pad], axis=0)
    w_b = w_t[_O3:].reshape(3, DIL_PAIRS, 128, d).transpose(1, 0, 2, 3).reshape(3 * DIL_WIDTH, d)
    return w_a, w_t[_O1:_O2], w_b


def _merge_g_in(g_a, g_z, g_b):
    d = g_a.shape[1]
    g_b = g_b.reshape(DIL_PAIRS, 3, 128, d).transpose(1, 0, 2, 3).reshape(3 * DIL_WIDTH, d)
    return jnp.concatenate([g_a[:_O1], g_z, g_a[_O1:_O1 + 2 * GDN_HEADS], g_b], axis=0)


MESH = pl.DeviceIdType.MESH
ANY = pl.BlockSpec(memory_space=pl.ANY)


def _position():
    return lax.axis_index("x"), lax.axis_index("y"), lax.axis_index("c")


def _slot(p):
    return 4 * p[0] + 2 * p[1] + p[2]


def _all_gather(blocks, *, name):
    n = len(blocks)

    def body(*refs):
        ins, outs = refs[:n], refs[n:2 * n]
        send_sems, recv_sems, local_sems = refs[2 * n:]
        x, y, c = _position()
        me, sibling = (x, y, c), (x, y, 1 - c)
        chips = [(1 - x, y), (x, 1 - y), (1 - x, 1 - y)]

        def copy(a, k, block, to, src=None):
            dst = outs[a].at[_slot(block)]
            return pltpu.make_async_remote_copy(
                src_ref=dst if src is None else src, dst_ref=dst, send_sem=send_sems.at[a, k], recv_sem=recv_sems.at[a, k],
                device_id=to, device_id_type=MESH)

        mine = [pltpu.make_async_copy(ins[a], outs[a].at[_slot(me)], local_sems.at[a]) for a in range(n)]
        for cp in mine:
            cp.start()
        first = []
        for a in range(n):
            first.append(copy(a, 0, me, sibling, src=ins[a]))
            first += [copy(a, 1 + j, me, (*chip, c), src=ins[a]) for j, chip in enumerate(chips)]
        for cp in first:
            cp.start()
        passed = []
        for j, chip in enumerate(chips):
            for a in range(n):
                copy(a, 1 + j, (*chip, c), me).wait_recv()
                fwd = copy(a, 4 + j, (*chip, c), sibling)
                fwd.start()
                passed.append(fwd)
        for a in range(n):
            copy(a, 0, sibling, me).wait_recv()
            for j, chip in enumerate(chips):
                copy(a, 4 + j, (*chip, 1 - c), me).wait_recv()
        for cp in first + passed:
            cp.wait_send()
        for cp in mine:
            cp.wait()

    return pl.pallas_call(
        body, name=name, in_specs=[ANY] * n, out_specs=[ANY] * n,
        out_shape=[_sds((N_DEV,) + b.shape, b.dtype) for b in blocks],
        scratch_shapes=[pltpu.SemaphoreType.DMA((n, 7)), pltpu.SemaphoreType.DMA((n, 7)), pltpu.SemaphoreType.DMA((n,))],
    )(*blocks)


def _gather_direct(block, *, name):
    def body(in_ref, out_ref, send_sems, recv_sems, local_sem):
        x, y, c = _position()
        me = _slot((x, y, c))
        mine = pltpu.make_async_copy(in_ref, out_ref.at[me], local_sem)
        mine.start()
        copies = [pltpu.make_async_remote_copy(
            src_ref=in_ref, dst_ref=out_ref.at[me], send_sem=send_sems.at[k - 1], recv_sem=recv_sems.at[k - 1],
            device_id=_peer_of(k, x, y, c), device_id_type=MESH) for k in range(1, N_DEV)]
        for cp in copies:
            cp.start()
        for cp in copies:
            cp.wait()
        mine.wait()

    return pl.pallas_call(
        body, name=name, in_specs=[pl.BlockSpec(memory_space=pltpu.VMEM)], out_specs=pl.BlockSpec(memory_space=pltpu.VMEM),
        out_shape=_sds((N_DEV,) + block.shape, block.dtype),
        scratch_shapes=[pltpu.SemaphoreType.DMA((N_DEV - 1,)), pltpu.SemaphoreType.DMA((N_DEV - 1,)), pltpu.SemaphoreType.DMA],
    )(block)


HBM = pl.BlockSpec(memory_space=pltpu.HBM)
SEM = pl.BlockSpec(memory_space=pltpu.SEMAPHORE)
EFFECT = pltpu.SideEffectType.DATAFLOW_SIDE_EFFECTING


def _peer_of(k, x, y, c):
    return (1 - x if k & 4 else x, 1 - y if k & 2 else y, 1 - c if k & 1 else c)


def _flight(a, k):
    return a * (N_DEV - 1) + k - 1


def _exchange_start(arrays, *, name, broadcast=False):
    n = len(arrays)

    def body(*refs):
        ins, lands = refs[:n], refs[n:2 * n]
        send_sems, recv_sems = refs[2 * n:2 * n + 2]
        token = refs[-1]
        x, y, c = _position()
        me = _slot((x, y, c))
        for k in range(1, N_DEV):
            peer = _peer_of(k, x, y, c)
            for a in range(n):
                pltpu.make_async_remote_copy(
                    src_ref=ins[a] if broadcast else ins[a].at[_slot(peer)], dst_ref=lands[a].at[me],
                    send_sem=send_sems.at[_flight(a, k)], recv_sem=recv_sems.at[_flight(a, k)],
                    device_id=peer, device_id_type=MESH).start()
        token[...] = jnp.zeros_like(token)

    land_shapes = [((N_DEV,) + s.shape) if broadcast else s.shape for s in arrays]
    lands = [pltpu.with_memory_space_constraint(lax.empty(shp, s.dtype), pltpu.HBM) for shp, s in zip(land_shapes, arrays)]
    srcs = [pltpu.with_memory_space_constraint(s, pltpu.HBM) for s in arrays]
    outs = pl.pallas_call(
        body, name=name, in_specs=[HBM] * (2 * n),
        out_specs=[SEM, SEM] + [HBM] * (2 * n) + [pl.BlockSpec(memory_space=pltpu.VMEM)],
        out_shape=[pltpu.SemaphoreType.DMA((n * (N_DEV - 1),)), pltpu.SemaphoreType.DMA((n * (N_DEV - 1),))]
        + [pltpu.HBM(s.shape, s.dtype) for s in arrays] + [pltpu.HBM(shp, s.dtype) for shp, s in zip(land_shapes, arrays)]
        + [_sds((8, 128))],
        input_output_aliases={i: 2 + i for i in range(2 * n)},
        compiler_params=pltpu.CompilerParams(has_side_effects=EFFECT),
    )(*srcs, *lands)
    return outs[0], outs[1], outs[2:2 + n], outs[2 + n:2 + 2 * n], outs[-1]


def _exchange_wait(send_sems, recv_sems, srcs, lands, after, *, name, broadcast=False):
    n = len(srcs)

    def body(*refs):
        ins, lnd = refs[:n], refs[n:2 * n]
        send_ref, recv_ref = refs[2 * n:2 * n + 2]
        x, y, c = _position()
        for k in range(1, N_DEV):
            for a in range(n):
                cp = pltpu.make_async_remote_copy(
                    src_ref=ins[a] if broadcast else ins[a].at[0], dst_ref=lnd[a].at[0], send_sem=send_ref.at[_flight(a, k)],
                    recv_sem=recv_ref.at[_flight(a, k)], device_id=_peer_of(k, x, y, c), device_id_type=MESH)
                cp.wait_send()
                cp.wait_recv()

    outs = pl.pallas_call(
        body, name=name, in_specs=[HBM] * (2 * n) + [SEM, SEM, ANY], out_specs=[HBM] * (2 * n),
        out_shape=[pltpu.HBM(s.shape, s.dtype) for s in srcs] + [pltpu.HBM(s.shape, s.dtype) for s in lands],
        input_output_aliases={i: i for i in range(2 * n)},
        compiler_params=pltpu.CompilerParams(has_side_effects=EFFECT),
    )(*srcs, *lands, send_sems, recv_sems, after)
    return outs[:n], outs[n:]


def _with_own(landed, srcs, me, broadcast=False):
    own = srcs if broadcast else [lax.dynamic_index_in_dim(s, me, 0, keepdims=False) for s in srcs]
    return [lax.dynamic_update_index_in_dim(l, o, me, 0) for l, o in zip(landed, own)]


def _behind(x, token):
    return x if token is None else x + token[0, 0].astype(x.dtype)


def _adamw(parts, w, m, v, *, name, tr=None, tc=None):
    R, C = w.shape
    tr = R if tr is None else tr
    tc = C if tc is None else tc
    assert R % tr == 0 and C % tc == 0
    c1 = 1.0 - ADAM_B1 ** ADAM_STEP
    c2 = 1.0 - ADAM_B2 ** ADAM_STEP

    def body(p_ref, w_ref, m_ref, v_ref, g_ref, d_ref, nm_ref, nv_ref):
        g = p_ref[0].astype(F32)
        for s in range(1, N_DEV):
            g = g + p_ref[s].astype(F32)
        nm = ADAM_B1 * m_ref[...] + (1.0 - ADAM_B1) * g
        nv = ADAM_B2 * v_ref[...] + (1.0 - ADAM_B2) * (g * g)
        g_ref[...] = g
        nm_ref[...] = nm
        nv_ref[...] = nv
        d_ref[...] = -ADAM_LR * ((nm / c1) / (jnp.sqrt(nv / c2) + ADAM_EPS) + ADAM_WD * w_ref[...])

    blk = pl.BlockSpec((tr, tc), lambda i, j: (i, j))
    return pl.pallas_call(
        body, name=name, grid=(R // tr, C // tc),
        in_specs=[pl.BlockSpec((N_DEV, tr, tc), lambda i, j: (0, i, j)), blk, blk, blk],
        out_specs=[blk] * 4, out_shape=[_sds((R, C))] * 4, compiler_params=_params(("parallel", "parallel")),
    )(parts, w, m, v)


_SMALL_ROWS = 8


def _pack_small(norm1, norm2, final, gnw, a_log, dt_bias, loss=None):
    loss = jnp.zeros((1, 128), F32) if loss is None else loss
    row3 = jnp.concatenate([gnw, a_log, dt_bias, jnp.zeros((1, 128 - 2 * GDN_HEADS), F32), loss,
                            jnp.zeros((1, D_MODEL - 3 * 128), F32)], axis=1)
    return jnp.concatenate([norm1, norm2, final, row3, jnp.zeros((_SMALL_ROWS - 4, D_MODEL), F32)], axis=0)


def _unpack_small(p):
    return (p[0:1], p[1:2], p[2], p[3:4, 0:128], p[3:4, 128:128 + GDN_HEADS], p[3:4, 128 + GDN_HEADS:128 + 2 * GDN_HEADS])


def _slabs_by_cols(g):
    r = g.shape[0]
    return g.reshape(r, N_DEV, -1).transpose(1, 0, 2)


def _cols_from_slabs(s):
    return s.transpose(1, 0, 2).reshape(s.shape[1], -1)


def kernel(x, norm1_w, w_in, conv_qkv_w, a_log, dt_bias, gdn_norm_w, w_out, norm2_w, w_up, ffn_conv_w, w_down, final_norm_w, loss_target, m_norm1_w, m_w_in, m_conv_qkv_w, m_a_log, m_dt_bias, m_gdn_norm_w, m_w_out, m_norm2_w, m_w_up, m_ffn_conv_w, m_w_down, m_final_norm_w, v_norm1_w, v_w_in, v_conv_qkv_w, v_a_log, v_dt_bias, v_gdn_norm_w, v_w_out, v_norm2_w, v_w_up, v_ffn_conv_w, v_w_down, v_final_norm_w):
    bf = lambda a: a.astype(BF16)
    me = _slot(_position())
    t_in = lambda a: a[0].T
    gw_in, g_conv_a = _all_gather([bf(t_in(w_in)), conv_qkv_w[0]], name="gather_w_in")
    w_a, w_z, w_b = _split_w_in(gw_in.reshape(-1, D_MODEL))
    late_src, _ = lax.optimization_barrier(([bf(w_out[0]), bf(t_in(w_up)), bf(w_down[0]), ffn_conv_w[0]], gw_in))
    l_send, l_recv, l_srcs, l_lands, l_token = _exchange_start(late_src, name="weights_start", broadcast=True)

    def late_weights(after):
        srcs, landed = _exchange_wait(l_send, l_recv, l_srcs, l_lands, after, name="weights_wait", broadcast=True)
        gw_out, gw_up, gw_down, g_conv_f = _with_own(landed, srcs, me, broadcast=True)
        return gw_out.reshape(D_MODEL, D_MODEL), gw_up, g_conv_f, gw_down.reshape(D_FF, D_MODEL)

    flights = {}

    def emit(group, **grads):
        if group == "in":
            slabs = dict(w_in=_merge_g_in(grads["w_a"], grads["w_z"], grads["w_b"]).reshape(N_DEV, -1, D_MODEL),
                         conv_a=_slabs_by_cols(grads["conv_a"]))
        elif group == "ffn":
            slabs = dict(w_down=grads["w_down"].reshape(N_DEV, -1, D_MODEL), w_up=grads["w_up"], conv_f=grads["conv_f"])
        else:
            slabs = {k: v.reshape(N_DEV, -1, D_MODEL) for k, v in grads.items()}
        names = list(slabs)
        *flight, token = _exchange_start([slabs[k] for k in names], name="grads_start_" + group)
        flights[group] = (names, flight)
        return token

    loss, grad_x, g = _local_step(
        x[0], loss_target[0], _behind(norm1_w, l_token), w_a, w_z, w_b, _cols_from_slabs(g_conv_a), a_log, dt_bias,
        gdn_norm_w, norm2_w, final_norm_w[None], late_weights, emit)
    small_all = _gather_direct(
        _pack_small(g["norm1"], g["norm2"], g["final"], g["gnw"], g["small"][:, 0:GDN_HEADS],
                    g["small"][:, GDN_HEADS:2 * GDN_HEADS], loss), name="gather_small")
    got = {}
    for group in ("ffn", "out", "in"):
        names, (send_sems, recv_sems, srcs, lands) = flights[group]
        srcs, landed = _exchange_wait(send_sems, recv_sems, srcs, lands, small_all, name="grads_wait_" + group)
        got.update(zip(names, _with_own(landed, srcs, me)))
    o_in = [o.T for o in _adamw(got["w_in"], t_in(w_in), t_in(m_w_in), t_in(v_w_in), name="adamw_w_in", tc=256)]
    o_out = _adamw(got["w_out"], w_out[0], m_w_out[0], v_w_out[0], name="adamw_w_out")
    o_up = [o.T for o in _adamw(got["w_up"], t_in(w_up), t_in(m_w_up), t_in(v_w_up), name="adamw_w_up", tr=176)]
    o_down = _adamw(got["w_down"], w_down[0], m_w_down[0], v_w_down[0], name="adamw_w_down", tr=176)
    o_ca = _adamw(got["conv_a"], conv_qkv_w[0], m_conv_qkv_w[0], v_conv_qkv_w[0], name="adamw_conv_a")
    o_cf = _adamw(got["conv_f"], ffn_conv_w[0], m_ffn_conv_w[0], v_ffn_conv_w[0], name="adamw_conv_f")
    o_small = _adamw(
        small_all, _pack_small(norm1_w, norm2_w, final_norm_w[None], gdn_norm_w, a_log, dt_bias),
        _pack_small(m_norm1_w, m_norm2_w, m_final_norm_w[None], m_gdn_norm_w, m_a_log, m_dt_bias),
        _pack_small(v_norm1_w, v_norm2_w, v_final_norm_w[None], v_gdn_norm_w, v_a_log, v_dt_bias), name="adamw_small")
    total_loss = o_small[0][3, 256]
    outs = [total_loss, grad_x[None]]
    for k in range(4):
        n1, n2, fin, gn, al, dt = _unpack_small(o_small[k])
        outs += [n1, o_in[k][None], o_ca[k][None], al, dt, gn, o_out[k][None], n2, o_up[k][None], o_cf[k][None], o_down[k][None], fin]
    return tuple(outs)
```

```python
import functools

import jax
import jax.numpy as jnp
from jax import lax
from jax.experimental import pallas as pl
from jax.experimental.pallas import tpu as pltpu

F32 = jnp.float32
BF16 = jnp.bfloat16

N_DEV = 8
D_MODEL = 1024
GDN_HEADS = 4
GDN_DIM = 128
GDN_WIDTH = GDN_HEADS * GDN_DIM
GDN_CONV = 4
CHUNK = 64
CHUNKS_PER_STEP = 2
DIL_HEADS = 8
DIL_DIM = 64
DIL_WIDTH = DIL_HEADS * DIL_DIM
DIL_PAIRS = DIL_HEADS // 2
DILATIONS = (1, 4, 16)
BAND = 128
D_FF = 2816
FFN_CONV = 3
EPS = 1e-6
A_COLS = 3 * GDN_WIDTH + 128
HALO = 8

ADAM_LR = 0.001
ADAM_B1 = 0.9
ADAM_B2 = 0.999
ADAM_EPS = 1e-08
ADAM_WD = 0.01
ADAM_STEP = 10

VMEM_LIMIT_BYTES = 56 * 1024 * 1024
NEG_BIG = -1e30


def _params(sem=None):
    return pltpu.CompilerParams(dimension_semantics=sem, vmem_limit_bytes=VMEM_LIMIT_BYTES)


def _sds(shape, dtype=F32):
    return jax.ShapeDtypeStruct(shape, dtype)


def _bdot(a, b):
    return jnp.dot(a.astype(BF16), b.astype(BF16), preferred_element_type=F32)


def _bdot_nt(a, b):
    return lax.dot_general(a.astype(BF16), b.astype(BF16), (((1,), (1,)), ((), ())), preferred_element_type=F32)


def _bdot_tn(a, b):
    return lax.dot_general(a.astype(BF16), b.astype(BF16), (((0,), (0,)), ((), ())), preferred_element_type=F32)


def _split(a):
    hi = a.astype(BF16)
    lo = (a - hi.astype(F32)).astype(BF16)
    return hi, lo


def _dot3(a, b, dims):
    ah, al = _split(a)
    bh, bl = _split(b)
    d = functools.partial(lax.dot_general, dimension_numbers=(dims, ((), ())), preferred_element_type=F32)
    return d(ah, bh) + (d(al, bh) + d(ah, bl))


def _exact_tri_dot(tri, g):
    g1 = g.astype(BF16)
    r1 = g - g1.astype(F32)
    g2 = r1.astype(BF16)
    g3 = (r1 - g2.astype(F32)).astype(BF16)
    t = tri.astype(BF16)
    d = functools.partial(jnp.dot, preferred_element_type=F32)
    return d(t, g1) + (d(t, g2) + d(t, g3))


def _sigmoid(x):
    return 1.0 / (1.0 + jnp.exp(-x))


def _dsilu(x, sg):
    return sg * (1.0 + x * (1.0 - sg))


def _mm(a, b, *, name, ta=False, tb=False, res=None, out_dtype=F32, tm=512, tn=512, tk=512):
    if ta:
        K, M = a.shape
    else:
        M, K = a.shape
    if tb:
        N, Kb = b.shape
    else:
        Kb, N = b.shape
    assert K == Kb, (a.shape, b.shape)
    tm, tn, tk = min(tm, M), min(tn, N), min(tk, K)
    assert M % tm == 0 and N % tn == 0 and K % tk == 0, (name, M, N, K, tm, tn, tk)
    nk = K // tk
    dims = (((0 if ta else 1,), (1 if tb else 0,)), ((), ()))
    has_res = res is not None

    def body(*refs):
        if has_res:
            a_ref, b_ref, r_ref, o_ref, acc_ref = refs
        else:
            a_ref, b_ref, o_ref, acc_ref = refs
        k = pl.program_id(2)
        part = lax.dot_general(a_ref[...].astype(BF16), b_ref[...].astype(BF16), dims, preferred_element_type=F32)

        @pl.when(k == 0)
        def _():
            acc_ref[...] = part

        @pl.when(k > 0)
        def _():
            acc_ref[...] += part

        @pl.when(k == nk - 1)
        def _():
            r = acc_ref[...]
            if has_res:
                r = r + r_ref[...]
            o_ref[...] = r.astype(out_dtype)

    a_spec = pl.BlockSpec((tk, tm), lambda i, j, k: (k, i)) if ta else pl.BlockSpec((tm, tk), lambda i, j, k: (i, k))
    b_spec = pl.BlockSpec((tn, tk), lambda i, j, k: (j, k)) if tb else pl.BlockSpec((tk, tn), lambda i, j, k: (k, j))
    o_spec = pl.BlockSpec((tm, tn), lambda i, j, k: (i, j))
    in_specs = [a_spec, b_spec] + ([o_spec] if has_res else [])
    args = (a, b) + ((res,) if has_res else ())
    return pl.pallas_call(
        body, name=name, grid=(M // tm, N // tn, nk), in_specs=in_specs, out_specs=o_spec,
        out_shape=_sds((M, N), out_dtype), scratch_shapes=[pltpu.VMEM((tm, tn), F32)],
        compiler_params=_params(("parallel", "parallel", "arbitrary")),
    )(*args)


def _rms_fwd(x, w, *, name, tm=256):
    S, D = x.shape

    def body(x_ref, w_ref, h_ref):
        xv = x_ref[...]
        r = lax.rsqrt(jnp.mean(xv * xv, axis=-1, keepdims=True) + EPS)
        h_ref[...] = (xv * r * w_ref[...]).astype(BF16)

    return pl.pallas_call(
        body, name=name, grid=(S // tm,),
        in_specs=[pl.BlockSpec((tm, D), lambda i: (i, 0)), pl.BlockSpec((1, D), lambda i: (0, 0))],
        out_specs=pl.BlockSpec((tm, D), lambda i: (i, 0)), out_shape=_sds((S, D), BF16),
        compiler_params=_params(("parallel",)),
    )(x, w)


def _rms_bwd(dh, x, w, res, *, name, tm=256):
    S, D = x.shape

    def body(dh_ref, x_ref, w_ref, res_ref, dx_ref, dw_ref):
        i = pl.program_id(0)
        xv = x_ref[...]
        g = dh_ref[...]
        r = lax.rsqrt(jnp.mean(xv * xv, axis=-1, keepdims=True) + EPS)
        xh = xv * r
        gw = g * w_ref[...]
        dx_ref[...] = res_ref[...] + r * (gw - xh * jnp.mean(gw * xh, axis=-1, keepdims=True))
        part = jnp.sum(g * xh, axis=0, keepdims=True)

        @pl.when(i == 0)
        def _():
            dw_ref[...] = part

        @pl.when(i > 0)
        def _():
            dw_ref[...] += part

    row = pl.BlockSpec((tm, D), lambda i: (i, 0))
    one = pl.BlockSpec((1, D), lambda i: (0, 0))
    return pl.pallas_call(
        body, name=name, grid=(S // tm,), in_specs=[row, row, one, row], out_specs=[row, one],
        out_shape=[_sds((S, D)), _sds((1, D))], compiler_params=_params(("arbitrary",)),
    )(dh, x, w, res)


def _loss_head(x2, w, tgt, *, name, tm=256):
    S, D = x2.shape

    def body(x_ref, w_ref, t_ref, dx_ref, dxb_ref, dw_ref, loss_ref):
        i = pl.program_id(0)
        xv = x_ref[...]
        wv = w_ref[...]
        r = lax.rsqrt(jnp.mean(xv * xv, axis=-1, keepdims=True) + EPS)
        xh = xv * r
        err = xh * wv - t_ref[...]
        lrow = jnp.sum(err * err, axis=-1, keepdims=True)
        lsum = jnp.sum(lrow, axis=0, keepdims=True) * (0.5 / D)
        g = err * (1.0 / D)
        gw = g * wv
        dx = r * (gw - xh * jnp.mean(gw * xh, axis=-1, keepdims=True))
        dx_ref[...] = dx
        dxb_ref[...] = dx.astype(BF16)
        part = jnp.sum(g * xh, axis=0, keepdims=True)
        lpart = jnp.broadcast_to(lsum, (1, 128))

        @pl.when(i == 0)
        def _():
            dw_ref[...] = part
            loss_ref[...] = lpart

        @pl.when(i > 0)
        def _():
            dw_ref[...] += part
            loss_ref[...] += lpart

    row = pl.BlockSpec((tm, D), lambda i: (i, 0))
    one = pl.BlockSpec((1, D), lambda i: (0, 0))
    return pl.pallas_call(
        body, name=name, grid=(S // tm,), in_specs=[row, one, row],
        out_specs=[row, row, one, pl.BlockSpec((1, 128), lambda i: (0, 0))],
        out_shape=[_sds((S, D)), _sds((S, D), BF16), _sds((1, D)), _sds((1, 128))], compiler_params=_params(("arbitrary",)),
    )(x2, w, tgt)


def _shifted(x, start, n):
    aligned = -(-start // HALO) * HALO
    assert aligned + n <= x.shape[0], (start, n, x.shape)
    return (x if aligned == start else pltpu.roll(x, aligned - start, axis=0))[aligned:aligned + n]


def _conv_rows(prev, cur, w, taps):
    n = cur.shape[0]
    xs = jnp.concatenate([prev, cur], axis=0)
    base = HALO - (taps - 1)
    out = _shifted(xs, base, n) * w[0:1]
    for i in range(1, taps):
        out = out + _shifted(xs, base + i, n) * w[i:i + 1]
    return out


def _conv_rows_bwd(cur_d, next_d, prev_x, cur_x, w, taps):
    n = cur_d.shape[0]
    ds = jnp.concatenate([cur_d, next_d], axis=0)
    dx = _shifted(ds, taps - 1, n) * w[0:1]
    for i in range(1, taps):
        dx = dx + _shifted(ds, taps - 1 - i, n) * w[i:i + 1]
    xs = jnp.concatenate([prev_x, cur_x], axis=0)
    base = HALO - (taps - 1)
    dws = [jnp.sum(cur_d * _shifted(xs, base + i, n), axis=0, keepdims=True) for i in range(taps)]
    return dx, jnp.concatenate(dws, axis=0)


def _halo_specs(tm, width, col, nblk):
    per = tm // HALO
    prev = pl.BlockSpec((HALO, width), lambda i, *_: (jnp.maximum(i * per - 1, 0), col))
    nxt = pl.BlockSpec((HALO, width), lambda i, *_: (jnp.minimum((i + 1) * per, nblk * per - 1), col))
    return prev, nxt


def _softplus(x):
    return jnp.maximum(x, 0.0) + jnp.log1p(jnp.exp(-jnp.abs(x)))


def _chunk_tri(tm, upper=False):
    r = lax.broadcasted_iota(jnp.int32, (tm, tm), 0)
    c = lax.broadcasted_iota(jnp.int32, (tm, tm), 1)
    same = lax.div(r, CHUNK) == lax.div(c, CHUNK)
    order = (c >= r) if upper else (c <= r)
    return jnp.where(same & order, 1.0, 0.0)


def _gdn_prep_fwd(proj_a, conv_w, a_log, dt_bias, *, name, tm=256):
    S = proj_a.shape[0]
    nblk = S // tm
    W3 = 3 * GDN_WIDTH

    def body(cur_ref, prev_ref, ba_ref, cw_ref, al_ref, dt_ref, qn_ref, kn_ref, v_ref, gcb_ref, bb_ref):
        i = pl.program_id(0)
        prev = jnp.where(i > 0, prev_ref[...], 0.0)
        c = _conv_rows(prev, cur_ref[...], cw_ref[...], GDN_CONV)
        a = c * _sigmoid(c)
        ba = ba_ref[...]
        lane = lax.broadcasted_iota(jnp.int32, (tm, 128), 1)
        g4 = jnp.zeros((tm, 128), F32)
        for h in range(GDN_HEADS):
            sl = slice(GDN_DIM * h, GDN_DIM * (h + 1))
            qh = a[:, GDN_DIM * h:GDN_DIM * (h + 1)]
            kh = a[:, GDN_WIDTH + GDN_DIM * h:GDN_WIDTH + GDN_DIM * (h + 1)]
            qn_ref[:, sl] = qh * (lax.rsqrt(jnp.sum(qh * qh, axis=-1, keepdims=True) + EPS) * (GDN_DIM ** -0.5))
            kn_ref[:, sl] = kh * lax.rsqrt(jnp.sum(kh * kh, axis=-1, keepdims=True) + EPS)
            beta = _sigmoid(ba[:, h:h + 1])
            bb_ref[:, sl] = jnp.broadcast_to(beta, (tm, GDN_DIM))
            g = -jnp.exp(al_ref[0:1, h:h + 1]) * _softplus(ba[:, GDN_HEADS + h:GDN_HEADS + h + 1] + dt_ref[0:1, h:h + 1])
            g4 = jnp.where(lane == h, g, g4)
        v_ref[...] = a[:, 2 * GDN_WIDTH:]
        gc = _exact_tri_dot(_chunk_tri(tm), g4)
        for h in range(GDN_HEADS):
            gcb_ref[:, GDN_DIM * h:GDN_DIM * (h + 1)] = jnp.broadcast_to(gc[:, h:h + 1], (tm, GDN_DIM))

    prev_spec, _ = _halo_specs(tm, W3, 0, nblk)
    row = pl.BlockSpec((tm, GDN_WIDTH), lambda i: (i, 0))
    small = lambda a: pl.BlockSpec(a.shape, lambda i: (0, 0))
    return pl.pallas_call(
        body, name=name, grid=(nblk,),
        in_specs=[pl.BlockSpec((tm, W3), lambda i: (i, 0)), prev_spec,
                  pl.BlockSpec((tm, 128), lambda i: (i, W3 // 128)), small(conv_w), small(a_log), small(dt_bias)],
        out_specs=[row] * 5, out_shape=[_sds((S, GDN_WIDTH))] * 5, compiler_params=_params(("parallel",)),
    )(proj_a, proj_a, proj_a, conv_w, a_log, dt_bias)


GDN_STACK = GDN_HEADS * CHUNK


def _stack(ref, rows):
    return jnp.concatenate([ref[rows, GDN_DIM * h:GDN_DIM * (h + 1)] for h in range(GDN_HEADS)], axis=0)


def _unstack_to(ref, rows, x):
    for h in range(GDN_HEADS):
        ref[rows, GDN_DIM * h:GDN_DIM * (h + 1)] = x[CHUNK * h:CHUNK * (h + 1)].astype(ref.dtype)


def _stack_masks():
    r = lax.broadcasted_iota(jnp.int32, (GDN_STACK, GDN_STACK), 0)
    c = lax.broadcasted_iota(jnp.int32, (GDN_STACK, GDN_STACK), 1)
    same = (r & -CHUNK) == (c & -CHUNK)
    return same & (r >= c), same & (r > c), r == c


def _stack_decay(gs, bs, incl):
    g2 = jnp.concatenate([gs, gs], axis=1)
    diff = g2 - g2.T
    dec = jnp.where(incl, jnp.exp(jnp.where(incl, diff, 0.0)), 0.0)
    return dec, jnp.concatenate([bs, bs], axis=1).T


def _head_mask():
    r = lax.broadcasted_iota(jnp.int32, (GDN_STACK, GDN_WIDTH), 0)
    c = lax.broadcasted_iota(jnp.int32, (GDN_STACK, GDN_WIDTH), 1)
    return (r & -CHUNK) * (GDN_DIM // CHUNK) == (c & -GDN_DIM)


def _head_spread(x):
    return jnp.where(_head_mask(), jnp.concatenate([x] * GDN_HEADS, axis=1), 0.0)


def _head_diag(x):
    xm = jnp.where(_head_mask(), x, 0.0)
    out = xm[:, 0:GDN_DIM]
    for h in range(1, GDN_HEADS):
        out = out + xm[:, GDN_DIM * h:GDN_DIM * (h + 1)]
    return out


def _last_rows(gs, n):
    return jnp.concatenate([jnp.broadcast_to(gs[CHUNK * (h + 1) - 1:CHUNK * (h + 1)], (n, GDN_DIM)) for h in range(GDN_HEADS)], axis=0)


def _gdn_chunk_fwd(qn, kn, v, gcb, bb, *, name):
    S = qn.shape[0]

    def body(qn_ref, kn_ref, v_ref, gcb_ref, bb_ref, uv_ref, wk_ref, at_ref, t_ref, wkb_ref, qdb_ref, keb_ref):
        incl, strict, diag = _stack_masks()
        for c in range(CHUNKS_PER_STEP):
            rows = slice(CHUNK * c, CHUNK * (c + 1))
            srows = slice(GDN_STACK * c, GDN_STACK * (c + 1))
            q, k, vv, gs, bs = [_stack(r, rows) for r in (qn_ref, kn_ref, v_ref, gcb_ref, bb_ref)]
            dec, bt = _stack_decay(gs, bs, incl)
            p = -jnp.where(strict, dec * _bdot_nt(k, k) * bt, 0.0)
            t = jnp.where(diag, 1.0, 0.0) + p
            for _ in range(5):
                p = _bdot(p, p)
                t = t + _bdot(t, p)
            sol = _dot3(t, jnp.concatenate([vv, jnp.exp(gs) * k], axis=1), ((1,), (0,)))
            _unstack_to(uv_ref, rows, sol[:, :GDN_DIM])
            _unstack_to(wk_ref, rows, sol[:, GDN_DIM:])
            at_ref[srows, :] = dec * _bdot_nt(q, k) * bt
            t_ref[srows, :] = t
            wkb_ref[srows, :] = _head_spread(sol[:, GDN_DIM:]).astype(BF16)
            qdb_ref[srows, :] = _head_spread(q * jnp.exp(gs)).astype(BF16)
            keb_ref[srows, :] = _head_spread(k * jnp.exp(_last_rows(gs, CHUNK) - gs) * bs).astype(BF16)

    step = CHUNKS_PER_STEP * CHUNK
    row = pl.BlockSpec((step, GDN_WIDTH), lambda n: (n, 0))
    sq = pl.BlockSpec((CHUNKS_PER_STEP * GDN_STACK, GDN_STACK), lambda n: (n, 0))
    wide = pl.BlockSpec((CHUNKS_PER_STEP * GDN_STACK, GDN_WIDTH), lambda n: (n, 0))
    nsq = S // CHUNK * GDN_STACK
    return pl.pallas_call(
        body, name=name, grid=(S // step,), in_specs=[row] * 5, out_specs=[row, row, sq, sq, wide, wide, wide],
        out_shape=[_sds((S, GDN_WIDTH)), _sds((S, GDN_WIDTH)), _sds((nsq, GDN_STACK)), _sds((nsq, GDN_STACK))]
        + [_sds((nsq, GDN_WIDTH), BF16)] * 3,
        compiler_params=_params(("parallel",)),
    )(qn, kn, v, gcb, bb)


SCAN_CHUNKS = 4


def _gdn_scan_fwd(uv, at, wkb, qdb, keb, gcb, proj_z, gnw, *, name):
    S = uv.shape[0]
    nc = S // CHUNK

    def body(uv_ref, at_ref, wkb_ref, qdb_ref, keb_ref, gcb_ref, z_ref, gnw_ref, o_ref, u_ref, sp_ref, oa_ref, st_ref):
        n = pl.program_id(0)

        @pl.when(n == 0)
        def _():
            st_ref[...] = jnp.zeros_like(st_ref)

        for c in range(SCAN_CHUNKS):
            rows = slice(CHUNK * c, CHUNK * (c + 1))
            srows = slice(GDN_STACK * c, GDN_STACK * (c + 1))
            st = st_ref[...]
            sp_ref[GDN_WIDTH * c:GDN_WIDTH * (c + 1), :] = st
            uv, gs, z = [_stack(r, rows) for r in (uv_ref, gcb_ref, z_ref)]
            u = uv - _bdot(wkb_ref[srows, :], st)
            o = _bdot(qdb_ref[srows, :], st) + _bdot(at_ref[srows, :], u)
            st_ref[...] = jnp.exp(_last_rows(gs, GDN_DIM)) * st + _bdot_tn(keb_ref[srows, :], u)
            _unstack_to(u_ref, rows, u)
            _unstack_to(o_ref, rows, o)
            r = lax.rsqrt(jnp.mean(o * o, axis=-1, keepdims=True) + EPS)
            oa = o * r * gnw_ref[...] * (z * _sigmoid(z))
            oa_ref[rows, :] = jnp.concatenate([oa[CHUNK * h:CHUNK * (h + 1)] for h in range(GDN_HEADS)], axis=1).astype(BF16)

    row = pl.BlockSpec((SCAN_CHUNKS * CHUNK, GDN_WIDTH), lambda n: (n, 0))
    sq = pl.BlockSpec((SCAN_CHUNKS * GDN_STACK, GDN_STACK), lambda n: (n, 0))
    wide = pl.BlockSpec((SCAN_CHUNKS * GDN_STACK, GDN_WIDTH), lambda n: (n, 0))
    return pl.pallas_call(
        body, name=name, grid=(nc // SCAN_CHUNKS,),
        in_specs=[row, sq, wide, wide, wide, row, row, pl.BlockSpec((1, GDN_DIM), lambda n: (0, 0))],
        out_specs=[row, row, pl.BlockSpec((SCAN_CHUNKS * GDN_WIDTH, GDN_DIM), lambda n: (n, 0)), row],
        out_shape=[_sds((S, GDN_WIDTH)), _sds((S, GDN_WIDTH)), _sds((nc * GDN_WIDTH, GDN_DIM)), _sds((S, 2 * GDN_WIDTH), BF16)],
        scratch_shapes=[pltpu.VMEM((GDN_WIDTH, GDN_DIM), F32)],
        compiler_params=_params(("arbitrary",)),
    )(uv, at, wkb, qdb, keb, gcb, proj_z, gnw)


def _gdn_scan_bwd(d_oab, o, proj_z, gnw, sp, u, at, wkb, qdb, keb, gcb, *, name):
    S = o.shape[0]
    nc = S // CHUNK
    ns = nc // SCAN_CHUNKS

    def body(do_ref, o_ref, z_ref, gnw_ref, sp_ref, u_ref, at_ref, wkb_ref, qdb_ref, keb_ref, gcb_ref,
             dz_ref, dgn_ref, du_ref, dwk_ref, dat_ref, dqd_ref, dke_ref, dgl_ref, ds_ref):
        n = pl.program_id(0)

        @pl.when(n == 0)
        def _():
            ds_ref[...] = jnp.zeros_like(ds_ref)
            dgn_ref[...] = jnp.zeros_like(dgn_ref)

        gw = gnw_ref[...]
        for c in reversed(range(SCAN_CHUNKS)):
            rows = slice(CHUNK * c, CHUNK * (c + 1))
            srows = slice(GDN_STACK * c, GDN_STACK * (c + 1))
            d_oa, oo, z, uu, gs = [_stack(r, rows) for r in (do_ref, o_ref, z_ref, u_ref, gcb_ref)]
            sg = _sigmoid(z)
            r = lax.rsqrt(jnp.mean(oo * oo, axis=-1, keepdims=True) + EPS)
            xh = oo * r
            dy = d_oa * (z * sg)
            _unstack_to(dz_ref, rows, d_oa * (xh * gw) * _dsilu(z, sg))
            dgn_ref[...] += jnp.sum(dy * xh, axis=0, keepdims=True)
            dxh = dy * gw
            do = r * (dxh - xh * jnp.mean(dxh * xh, axis=-1, keepdims=True))

            st = sp_ref[GDN_WIDTH * c:GDN_WIDTH * (c + 1), :]
            dst = ds_ref[...]
            ge = jnp.exp(_last_rows(gs, GDN_DIM))
            _unstack_to(dqd_ref, rows, _head_diag(_bdot_nt(do, st)))
            dat_ref[srows, :] = _bdot_nt(do, uu)
            du = _bdot_tn(at_ref[srows, :], do) + _bdot(keb_ref[srows, :], dst)
            _unstack_to(dke_ref, rows, _head_diag(_bdot_nt(uu, dst)))
            prod = dst * st
            for h in range(GDN_HEADS):
                blk = prod[GDN_DIM * h:GDN_DIM * (h + 1)]
                dge = jnp.sum(jnp.sum(blk, axis=1, keepdims=True), axis=0, keepdims=True)
                dgl_ref[c, :, GDN_DIM * h:GDN_DIM * (h + 1)] = jnp.broadcast_to(dge * ge[GDN_DIM * h:GDN_DIM * h + 1], (8, GDN_DIM))
            ds_ref[...] = _bdot_tn(qdb_ref[srows, :], do) + ge * dst - _bdot_tn(wkb_ref[srows, :], du)
            _unstack_to(du_ref, rows, du)
            _unstack_to(dwk_ref, rows, -_head_diag(_bdot_nt(du, st)))

    rev = lambda n: (ns - 1 - n, 0)
    row = pl.BlockSpec((SCAN_CHUNKS * CHUNK, GDN_WIDTH), rev)
    sq = pl.BlockSpec((SCAN_CHUNKS * GDN_STACK, GDN_STACK), rev)
    wide = pl.BlockSpec((SCAN_CHUNKS * GDN_STACK, GDN_WIDTH), rev)
    one = pl.BlockSpec((1, GDN_DIM), lambda n: (0, 0))
    return pl.pallas_call(
        body, name=name, grid=(ns,),
        in_specs=[row, row, row, one, pl.BlockSpec((SCAN_CHUNKS * GDN_WIDTH, GDN_DIM), rev), row, sq, wide, wide, wide, row],
        out_specs=[row, one, row, row, sq, row, row, pl.BlockSpec((SCAN_CHUNKS, 8, GDN_WIDTH), lambda n: (ns - 1 - n, 0, 0))],
        out_shape=[_sds((S, GDN_WIDTH), BF16), _sds((1, GDN_DIM)), _sds((S, GDN_WIDTH)), _sds((S, GDN_WIDTH)),
                   _sds((nc * GDN_STACK, GDN_STACK)), _sds((S, GDN_WIDTH)), _sds((S, GDN_WIDTH)), _sds((nc, 8, GDN_WIDTH))],
        scratch_shapes=[pltpu.VMEM((GDN_WIDTH, GDN_DIM), F32)],
        compiler_params=_params(("arbitrary",)),
    )(d_oab, o, proj_z, gnw, sp, u, at, wkb, qdb, keb, gcb)


def _gdn_chunk_bwd(qn, kn, gcb, bb, tmat, uv, wk, du, dwk, dat, dqd, dke, dgl, *, name):
    S = qn.shape[0]

    def body(qn_ref, kn_ref, gcb_ref, bb_ref, t_ref, uv_ref, wk_ref, du_ref, dwk_ref, dat_ref, dqd_ref, dke_ref,
             dgl_ref, dq_ref, dk_ref, dv_ref, dg_ref, dbeta_ref):
        incl, strict, _ = _stack_masks()
        lane = lax.broadcasted_iota(jnp.int32, (CHUNK, 128), 1)
        rowi = lax.broadcasted_iota(jnp.int32, (CHUNK, 1), 0)
        rsum = lambda x: jnp.sum(x, axis=-1, keepdims=True)
        for c in range(CHUNKS_PER_STEP):
            rows = slice(CHUNK * c, CHUNK * (c + 1))
            srows = slice(GDN_STACK * c, GDN_STACK * (c + 1))
            q, k, gs, bs, uv, wk, du, dwk, dqd, dke = [
                _stack(r, rows) for r in (qn_ref, kn_ref, gcb_ref, bb_ref, uv_ref, wk_ref, du_ref, dwk_ref, dqd_ref, dke_ref)]
            dec, bt = _stack_decay(gs, bs, incl)
            kk = _bdot_nt(k, k)
            qk = _bdot_nt(q, k)
            d_rhs = _dot3(t_ref[srows, :], jnp.concatenate([du, dwk], axis=1), ((0,), (0,)))
            sol = jnp.concatenate([uv, wk], axis=1)
            d_l = jnp.where(strict, -_dot3(d_rhs, sol, ((1,), (1,))), 0.0)
            d_a = jnp.where(incl, dat_ref[srows, :], 0.0)
            gam = jnp.exp(gs)
            e = jnp.exp(_last_rows(gs, CHUNK) - gs)
            d_gk = d_rhs[:, GDN_DIM:]
            ml = d_l * dec * bt
            ma = d_a * dec * bt
            _unstack_to(dq_ref, rows, _bdot(ma, k) + dqd * gam)
            _unstack_to(dk_ref, rows, _bdot(ml + ml.T, k) + _bdot_tn(ma, q) + d_gk * gam + dke * (e * bs))
            _unstack_to(dv_ref, rows, d_rhs[:, :GDN_DIM])
            wb = d_l * dec * kk + d_a * dec * qk
            ew = wb * bt
            s_ke = rsum(dke * k * (e * bs))
            dbeta = rsum(wb.T) + rsum(dke * k * e)
            dgc = rsum(ew) - rsum(ew.T) + rsum(dqd * q * gam) + rsum(d_gk * k * gam) - s_ke
            dgc4 = jnp.zeros((CHUNK, 128), F32)
            db4 = jnp.zeros((CHUNK, 128), F32)
            for h in range(GDN_HEADS):
                hr = slice(CHUNK * h, CHUNK * (h + 1))
                tail = jnp.sum(s_ke[hr], axis=0, keepdims=True) + dgl_ref[c, 0:1, GDN_DIM * h:GDN_DIM * h + 1]
                dgc4 = jnp.where(lane == h, dgc[hr] + jnp.where(rowi == CHUNK - 1, tail, 0.0), dgc4)
                db4 = jnp.where(lane == h, dbeta[hr], db4)
            dg_ref[rows, :] = _exact_tri_dot(_chunk_tri(CHUNK, upper=True), dgc4)
            dbeta_ref[rows, :] = db4

    step = CHUNKS_PER_STEP * CHUNK
    row = pl.BlockSpec((step, GDN_WIDTH), lambda n: (n, 0))
    sq = pl.BlockSpec((CHUNKS_PER_STEP * GDN_STACK, GDN_STACK), lambda n: (n, 0))
    col = pl.BlockSpec((step, 128), lambda n: (n, 0))
    return pl.pallas_call(
        body, name=name, grid=(S // step,),
        in_specs=[row] * 4 + [sq, row, row, row, row, sq, row, row,
                              pl.BlockSpec((CHUNKS_PER_STEP, 8, GDN_WIDTH), lambda n: (n, 0, 0))],
        out_specs=[row, row, row, col, col],
        out_shape=[_sds((S, GDN_WIDTH))] * 3 + [_sds((S, 128))] * 2, compiler_params=_params(("parallel",)),
    )(qn, kn, gcb, bb, tmat, uv, wk, du, dwk, dat, dqd, dke, dgl)


def _gdn_prep_bwd(dqn, dkn, dv, dg, dbeta, proj_a, conv_w, a_log, dt_bias, *, name, tm=256):
    S = proj_a.shape[0]
    nblk = S // tm
    W3 = 3 * GDN_WIDTH

    def body(dqn_ref, dkn_ref, dv_ref, dg_ref, dbeta_ref, cur_ref, prev_ref, ba_ref, cw_ref, al_ref, dt_ref,
             dc_ref, dba_ref, sm_ref):
        i = pl.program_id(0)
        prev = jnp.where(i > 0, prev_ref[...], 0.0)
        c = _conv_rows(prev, cur_ref[...], cw_ref[...], GDN_CONV)
        sg = _sigmoid(c)
        a = c * sg
        dsl = _dsilu(c, sg)
        ba = ba_ref[...]
        lane = lax.broadcasted_iota(jnp.int32, (tm, 128), 1)
        lane1 = lax.broadcasted_iota(jnp.int32, (1, 128), 1)
        dba = jnp.zeros((tm, 128), F32)
        sm = jnp.zeros((1, 128), F32)
        for h in range(GDN_HEADS):
            sl = slice(GDN_DIM * h, GDN_DIM * (h + 1))
            ks = slice(GDN_WIDTH + GDN_DIM * h, GDN_WIDTH + GDN_DIM * (h + 1))
            qh, kh = a[:, sl], a[:, ks]
            rq = lax.rsqrt(jnp.sum(qh * qh, axis=-1, keepdims=True) + EPS)
            rk = lax.rsqrt(jnp.sum(kh * kh, axis=-1, keepdims=True) + EPS)
            qhat, khat = qh * rq, kh * rk
            dyq = dqn_ref[:, sl] * (GDN_DIM ** -0.5)
            dyk = dkn_ref[:, sl]
            dq = rq * (dyq - qhat * jnp.sum(dyq * qhat, axis=-1, keepdims=True))
            dk = rk * (dyk - khat * jnp.sum(dyk * khat, axis=-1, keepdims=True))
            dc_ref[:, sl] = dq * dsl[:, sl]
            dc_ref[:, ks] = dk * dsl[:, ks]
            beta = _sigmoid(ba[:, h:h + 1])
            db = dbeta_ref[:, h:h + 1] * beta * (1.0 - beta)
            aneg = -jnp.exp(al_ref[0:1, h:h + 1])
            xa = ba[:, GDN_HEADS + h:GDN_HEADS + h + 1] + dt_ref[0:1, h:h + 1]
            dgh = dg_ref[:, h:h + 1]
            dxa = dgh * aneg * _sigmoid(xa)
            dba = jnp.where(lane == h, db, dba)
            dba = jnp.where(lane == GDN_HEADS + h, dxa, dba)
            d_alog = jnp.sum(dgh * _softplus(xa), axis=0, keepdims=True) * aneg
            sm = jnp.where(lane1 == h, d_alog, sm)
            sm = jnp.where(lane1 == GDN_HEADS + h, jnp.sum(dxa, axis=0, keepdims=True), sm)
        vs = slice(2 * GDN_WIDTH, W3)
        dc_ref[:, vs] = dv_ref[...] * dsl[:, vs]
        dba_ref[...] = dba

        @pl.when(i == 0)
        def _():
            sm_ref[...] = sm

        @pl.when(i > 0)
        def _():
            sm_ref[...] += sm

    prev_spec, _ = _halo_specs(tm, W3, 0, nblk)
    row = pl.BlockSpec((tm, GDN_WIDTH), lambda i: (i, 0))
    col = pl.BlockSpec((tm, 128), lambda i: (i, 0))
    small = lambda a: pl.BlockSpec(a.shape, lambda i: (0, 0))
    return pl.pallas_call(
        body, name=name, grid=(nblk,),
        in_specs=[row, row, row, col, col, pl.BlockSpec((tm, W3), lambda i: (i, 0)), prev_spec,
                  pl.BlockSpec((tm, 128), lambda i: (i, W3 // 128)), small(conv_w), small(a_log), small(dt_bias)],
        out_specs=[pl.BlockSpec((tm, W3), lambda i: (i, 0)), col, pl.BlockSpec((1, 128), lambda i: (0, 0))],
        out_shape=[_sds((S, W3)), _sds((S, 128)), _sds((1, 128))], compiler_params=_params(("arbitrary",)),
    )(dqn, dkn, dv, dg, dbeta, proj_a, proj_a, proj_a, conv_w, a_log, dt_bias)


def _gdn_conv_bwd(dc, dba, proj_a, conv_w, *, name, tm=256):
    S = proj_a.shape[0]
    nblk = S // tm
    W3 = 3 * GDN_WIDTH

    def body(dc_ref, dnext_ref, dba_ref, cur_ref, prev_ref, cw_ref, da_ref, dcw_ref):
        i = pl.program_id(0)
        prev = jnp.where(i > 0, prev_ref[...], 0.0)
        nxt = jnp.where(i < nblk - 1, dnext_ref[...], 0.0)
        dx, dw = _conv_rows_bwd(dc_ref[...], nxt, prev, cur_ref[...], cw_ref[...], GDN_CONV)
        da_ref[:, 0:W3] = dx.astype(BF16)
        da_ref[:, W3:] = dba_ref[...].astype(BF16)

        @pl.when(i == 0)
        def _():
            dcw_ref[...] = dw

        @pl.when(i > 0)
        def _():
            dcw_ref[...] += dw

    prev_spec, next_spec = _halo_specs(tm, W3, 0, nblk)
    wide = pl.BlockSpec((tm, W3), lambda i: (i, 0))
    return pl.pallas_call(
        body, name=name, grid=(nblk,),
        in_specs=[wide, next_spec, pl.BlockSpec((tm, 128), lambda i: (i, 0)), wide, prev_spec,
                  pl.BlockSpec(conv_w.shape, lambda i: (0, 0))],
        out_specs=[pl.BlockSpec((tm, A_COLS), lambda i: (i, 0)), pl.BlockSpec(conv_w.shape, lambda i: (0, 0))],
        out_shape=[_sds((S, A_COLS), BF16), _sds(conv_w.shape)], compiler_params=_params(("arbitrary",)),
    )(dc, dc, dba, proj_a, proj_a, conv_w)


def _band_mask(nk):
    i = lax.broadcasted_iota(jnp.int32, (2 * BAND, nk), 0) & (BAND - 1)
    j = lax.broadcasted_iota(jnp.int32, (2 * BAND, nk), 1)
    if nk == BAND:
        return j <= i
    return (j >= i) & (j <= i + BAND)


def _stack_heads(x, lo):
    return jnp.concatenate([jnp.where(lo, x, 0.0), jnp.where(lo, 0.0, x)], axis=0)


def _stack_cols(x):
    return jnp.concatenate([x[:, 0:1], x[:, DIL_DIM:DIL_DIM + 1]], axis=0)


def _unstack(x, lo):
    return jnp.where(lo, x[0:BAND], x[BAND:2 * BAND])


def _rows(start, size, stride):
    return pl.ds(start, size) if stride == 1 else pl.ds(start, size, stride=stride)


ATTN_LANES = 4


def _attn_blocks(S, visit_many, lanes=ATTN_LANES):
    for d in DILATIONS:
        nb = S // (d * BAND)
        if d == 1:
            half = nb // 2
            visit_many(d, [(0, 0, True), (0, half, False)])

            def pair(n, c):
                visit_many(1, [(0, n, False), (0, n + half, False)])
                return c
            lax.fori_loop(1, half, pair, 0)
        elif nb > 1:
            for r0 in range(0, d, lanes):
                visit_many(d, [(r0 + t, 0, True) for t in range(lanes)])

                def column(n, c, d=d, r0=r0):
                    visit_many(d, [(r0 + t, n, False) for t in range(lanes)])
                    return c
                lax.fori_loop(1, nb, column, 0)
        else:
            def group(g, c, d=d):
                visit_many(d, [(g * lanes + t, 0, True) for t in range(lanes)])
                return c
            lax.fori_loop(0, d // lanes, group, 0)


def _attn_fwd(proj_b, oab, *, name):
    S = proj_b.shape[0]
    scale = DIL_DIM ** -0.5

    def body(q_ref, k_ref, v_ref, oab_in_ref, ob_ref, lse_ref, m_ref, l_ref, acc_ref):
        del oab_in_ref
        lane = lax.broadcasted_iota(jnp.int32, (BAND, 128), 1)
        lo = lane < DIL_DIM
        m_ref[...] = jnp.full_like(m_ref, NEG_BIG)
        l_ref[...] = jnp.zeros_like(l_ref)
        acc_ref[...] = jnp.zeros_like(acc_ref)

        def load(d, r, n, first):
            nk = BAND if first else 2 * BAND
            qrows = _rows(r + n * (BAND * d), BAND, d)
            krows = _rows(r if first else r + (n - 1) * (BAND * d), nk, d)
            return dict(nk=nk, qrows=qrows, q=q_ref[qrows, :] * scale, k=k_ref[krows, :].astype(BF16),
                        v=v_ref[krows, :].astype(BF16), m=m_ref[qrows, :], l=l_ref[qrows, :], acc=acc_ref[qrows, :])

        def compute(b):
            q, k, v = b["q"], b["k"], b["v"]
            s = jnp.where(_band_mask(b["nk"]), _bdot_nt(_stack_heads(q, lo), k), NEG_BIG)
            m_old = _stack_cols(b["m"])
            m_new = jnp.maximum(m_old, jnp.max(s, axis=-1, keepdims=True))
            p = jnp.exp(s - m_new)
            alpha = _unstack(jnp.exp(m_old - m_new), lo)
            l_new = alpha * b["l"] + _unstack(jnp.sum(p, axis=-1, keepdims=True), lo)
            return _unstack(m_new, lo), l_new, alpha * b["acc"] + _unstack(_bdot(p, v), lo)

        def visit_many(d, blocks):
            loaded = [load(d, *blk) for blk in blocks]
            done = [compute(b) for b in loaded]
            for b, (m_new, l_new, acc_new) in zip(loaded, done):
                m_ref[b["qrows"], :] = m_new
                l_ref[b["qrows"], :] = l_new
                acc_ref[b["qrows"], :] = acc_new

        _attn_blocks(S, visit_many)
        ob_ref[...] = (acc_ref[...] / l_ref[...]).astype(BF16)
        lse_ref[...] = m_ref[...] + jnp.log(l_ref[...])

    part = lambda t: pl.BlockSpec((S, 128), lambda p: (0, 3 * p + t))
    return pl.pallas_call(
        body, name=name, grid=(DIL_PAIRS,),
        in_specs=[part(0), part(1), part(2), pl.BlockSpec(memory_space=pl.ANY)],
        out_specs=[pl.BlockSpec((S, 128), lambda p: (0, GDN_WIDTH // 128 + p)), pl.BlockSpec((S, 128), lambda p: (0, p))],
        out_shape=[_sds(oab.shape, BF16), _sds((S, DIL_WIDTH))],
        scratch_shapes=[pltpu.VMEM((S, 128), F32)] * 3, input_output_aliases={3: 0},
        compiler_params=_params(("parallel",)),
    )(proj_b, proj_b, proj_b, oab)


def _attn_bwd(proj_b, oab, d_oab, lse, *, name):
    S = proj_b.shape[0]
    scale = DIL_DIM ** -0.5

    def body(q_ref, k_ref, v_ref, o_ref, do_ref, lse_ref, dqkv_ref, dq_ref, dk_ref, dv_ref, delta_ref):
        lane = lax.broadcasted_iota(jnp.int32, (BAND, 128), 1)
        lo = lane < DIL_DIM
        dq_ref[...] = jnp.zeros_like(dq_ref)
        dk_ref[...] = jnp.zeros_like(dk_ref)
        dv_ref[...] = jnp.zeros_like(dv_ref)
        prod = do_ref[...] * o_ref[...].astype(F32)
        lo_all = lax.broadcasted_iota(jnp.int32, (S, 128), 1) < DIL_DIM
        delta_ref[...] = jnp.where(lo_all, jnp.sum(jnp.where(lo_all, prod, 0.0), axis=-1, keepdims=True),
                                   jnp.sum(jnp.where(lo_all, 0.0, prod), axis=-1, keepdims=True))

        def load(d, r, n, first):
            nk = BAND if first else 2 * BAND
            qrows = _rows(r + n * (BAND * d), BAND, d)
            krows = _rows(r if first else r + (n - 1) * (BAND * d), nk, d)
            return dict(nk=nk, qrows=qrows, krows=krows, q=q_ref[qrows, :] * scale, k=k_ref[krows, :], v=v_ref[krows, :],
                        do=do_ref[qrows, :], delta=delta_ref[qrows, :], lse=lse_ref[qrows, :],
                        dq=dq_ref[qrows, :], dk=dk_ref[krows, :], dv=dv_ref[krows, :])

        def compute(b):
            q, k, v, do = b["q"], b["k"], b["v"], b["do"]
            qs, dos = _stack_heads(q, lo), _stack_heads(do, lo)
            p = jnp.where(_band_mask(b["nk"]), jnp.exp(_bdot_nt(qs, k) - _stack_cols(b["lse"])), 0.0)
            ds = p * (_bdot_nt(dos, v) - _stack_cols(b["delta"]))
            dq = b["dq"] + _unstack(_bdot(ds, k), lo) * scale
            return dq, b["dk"] + _bdot_tn(ds, qs), b["dv"] + _bdot_tn(p, dos)

        def visit_many(d, blocks):
            loaded = [load(d, *blk) for blk in blocks]
            done = [compute(b) for b in loaded]
            for b, (dq, dk, dv) in zip(loaded, done):
                dq_ref[b["qrows"], :] = dq
                dk_ref[b["krows"], :] = dk
                dv_ref[b["krows"], :] = dv

        _attn_blocks(S, visit_many, lanes=2)
        dqkv_ref[:, 0:128] = dq_ref[...].astype(BF16)
        dqkv_ref[:, 128:256] = dk_ref[...].astype(BF16)
        dqkv_ref[:, 256:384] = dv_ref[...].astype(BF16)

    half = lambda p: (0, GDN_WIDTH // 128 + p)
    part = lambda t: pl.BlockSpec((S, 128), lambda p: (0, 3 * p + t))
    return pl.pallas_call(
        body, name=name, grid=(DIL_PAIRS,),
        in_specs=[part(0), part(1), part(2), pl.BlockSpec((S, 128), half), pl.BlockSpec((S, 128), half),
                  pl.BlockSpec((S, 128), lambda p: (0, p))],
        out_specs=pl.BlockSpec((S, 384), lambda p: (0, p)), out_shape=_sds((S, 3 * DIL_WIDTH), BF16),
        scratch_shapes=[pltpu.VMEM((S, 128), F32)] * 4, compiler_params=_params(("parallel",)),
    )(proj_b, proj_b, proj_b, oab, d_oab, lse)


FF_SLAB = 2 * D_FF // N_DEV
FF_PAIRS = N_DEV // 2
ROWS16 = 16


def _taps(w, x, base, n):
    out = _shifted(x, base, n) * w[0:1]
    for t in range(1, FFN_CONV):
        out = out + _shifted(x, base + t, n) * w[t:t + 1]
    return out


def _ffn_fwd(h2, x1, w_up, conv_w, w_down, *, name, tm=512):
    S, D = h2.shape
    ni = S // tm
    per = tm // ROWS16

    def body(h_ref, hp_ref, x1_ref, wg_ref, wu_ref, cg_ref, cu_ref, wd_ref, x2_ref, ug_ref, uu_ref):
        i, j = pl.program_id(0), pl.program_id(1)
        hv = jnp.concatenate([hp_ref[...], h_ref[...]], axis=0)
        row = lax.broadcasted_iota(jnp.int32, (tm + ROWS16, 1), 0)
        keep = (i > 0) | (row >= ROWS16)

        def branch(w_ref, c_ref, u_ref):
            u = lax.dot_general(hv, w_ref[...], (((1,), (1,)), ((), ())), preferred_element_type=F32).astype(BF16)
            u_ref[...] = u[ROWS16:]
            return _taps(c_ref[...], jnp.where(keep, u.astype(F32), 0.0), ROWS16 - (FFN_CONV - 1), tm)

        gate = branch(wg_ref, cg_ref, ug_ref)
        up = branch(wu_ref, cu_ref, uu_ref)
        act = (gate * _sigmoid(gate) * up).astype(BF16)
        part = jnp.dot(act, wd_ref[...], preferred_element_type=F32)

        @pl.when(j == 0)
        def _():
            x2_ref[...] = x1_ref[...] + part

        @pl.when(j > 0)
        def _():
            x2_ref[...] += part

    rows = pl.BlockSpec((tm, D), lambda i, j: (i, 0))
    slab = lambda off: pl.BlockSpec((None, FF_SLAB, D), lambda i, j: (j + off, 0, 0))
    cslab = lambda off: pl.BlockSpec((None, FFN_CONV, FF_SLAB), lambda i, j: (j + off, 0, 0))
    uspec = pl.BlockSpec((None, tm, FF_SLAB), lambda i, j: (j, i, 0))
    return pl.pallas_call(
        body, name=name, grid=(ni, FF_PAIRS),
        in_specs=[rows, pl.BlockSpec((ROWS16, D), lambda i, j: (jnp.maximum(i * per - 1, 0), 0)), rows,
                  slab(0), slab(FF_PAIRS), cslab(0), cslab(FF_PAIRS), pl.BlockSpec((FF_SLAB, D), lambda i, j: (j, 0))],
        out_specs=[rows, uspec, uspec],
        out_shape=[_sds((S, D)), _sds((FF_PAIRS, S, FF_SLAB), BF16), _sds((FF_PAIRS, S, FF_SLAB), BF16)],
        compiler_params=_params(("parallel", "arbitrary")),
    )(h2, h2, x1, w_up, w_up, conv_w, conv_w, w_down)


def _ffn_bwd(dx2, h2, ug, uu, conv_w, w_down, *, name, tm=512):
    S, D = h2.shape
    ni = S // tm
    per = tm // ROWS16
    ext = tm + ROWS16

    def body(dx_ref, dxn_ref, h_ref, ug_ref, ugp_ref, ugn_ref, uu_ref, uup_ref, uun_ref, cg_ref, cu_ref, wd_ref,
             dug_ref, duu_ref, gd_ref, gg_ref, gu_ref, dcg_ref, dcu_ref, acc_d, acc_g, acc_u, acc_cg, acc_cu):
        i = pl.program_id(1)

        @pl.when(i == 0)
        def _():
            acc_d[...] = jnp.zeros_like(acc_d)
            acc_g[...] = jnp.zeros_like(acc_g)
            acc_u[...] = jnp.zeros_like(acc_u)
            acc_cg[...] = jnp.zeros_like(acc_cg)
            acc_cu[...] = jnp.zeros_like(acc_cu)

        dx = dx_ref[...]
        dxe = jnp.concatenate([dx, dxn_ref[...]], axis=0)
        row = lax.broadcasted_iota(jnp.int32, (ext, 1), 0)
        live = (i < ni - 1) | (row < tm)
        d_act = jnp.where(live, lax.dot_general(dxe, wd_ref[...], (((1,), (1,)), ((), ())), preferred_element_type=F32), 0.0)
        rowp = lax.broadcasted_iota(jnp.int32, (ext + ROWS16, 1), 0)
        keep = (i > 0) | (rowp >= ROWS16)

        def pre(cur, prev, nxt):
            return jnp.where(keep, jnp.concatenate([prev[...], cur[...], nxt[...]], axis=0).astype(F32), 0.0)

        uge, uue = pre(ug_ref, ugp_ref, ugn_ref), pre(uu_ref, uup_ref, uun_ref)
        cg, cu = cg_ref[...], cu_ref[...]
        base = ROWS16 - (FFN_CONV - 1)
        gate = _taps(cg, uge, base, ext)
        up = _taps(cu, uue, base, ext)
        sg = _sigmoid(gate)
        silu = gate * sg
        dgc = d_act * up * _dsilu(gate, sg)
        duc = d_act * silu

        def conv_t(w, dc):
            out = _shifted(dc, FFN_CONV - 1, tm) * w[0:1]
            for t in range(1, FFN_CONV):
                out = out + _shifted(dc, FFN_CONV - 1 - t, tm) * w[t:t + 1]
            return out.astype(BF16)

        du_g, du_u = conv_t(cg, dgc), conv_t(cu, duc)
        dug_ref[...] = du_g
        duu_ref[...] = du_u
        dcw = lambda dc, xe: jnp.concatenate(
            [jnp.sum(dc[0:tm] * _shifted(xe, base + t, tm), axis=0, keepdims=True) for t in range(FFN_CONV)], axis=0)
        acc_cg[0:FFN_CONV, :] += dcw(dgc, uge)
        acc_cu[0:FFN_CONV, :] += dcw(duc, uue)
        tn = (((0,), (0,)), ((), ()))
        act = (silu[0:tm] * up[0:tm]).astype(BF16)
        acc_d[...] += lax.dot_general(act, dx, tn, preferred_element_type=F32)
        hv = h_ref[...]
        acc_g[...] += lax.dot_general(du_g, hv, tn, preferred_element_type=F32)
        acc_u[...] += lax.dot_general(du_u, hv, tn, preferred_element_type=F32)

        @pl.when(i == ni - 1)
        def _():
            gd_ref[...] = acc_d[...].astype(BF16)
            gg_ref[...] = acc_g[...].astype(BF16)
            gu_ref[...] = acc_u[...].astype(BF16)
            dcg_ref[...] = acc_cg[0:FFN_CONV, :]
            dcu_ref[...] = acc_cu[0:FFN_CONV, :]

    last16 = S // ROWS16 - 1
    rows = pl.BlockSpec((tm, D), lambda j, i: (i, 0))
    rows_next = pl.BlockSpec((ROWS16, D), lambda j, i: (jnp.minimum((i + 1) * per, last16), 0))
    u_cur = pl.BlockSpec((None, tm, FF_SLAB), lambda j, i: (j, i, 0))
    u_prev = pl.BlockSpec((None, ROWS16, FF_SLAB), lambda j, i: (j, jnp.maximum(i * per - 1, 0), 0))
    u_next = pl.BlockSpec((None, ROWS16, FF_SLAB), lambda j, i: (j, jnp.minimum((i + 1) * per, last16), 0))
    cslab = lambda off: pl.BlockSpec((None, FFN_CONV, FF_SLAB), lambda j, i: (j + off, 0, 0))
    wslab = pl.BlockSpec((None, FF_SLAB, D), lambda j, i: (j, 0, 0))
    dslab = pl.BlockSpec((None, FFN_CONV, FF_SLAB), lambda j, i: (j, 0, 0))
    return pl.pallas_call(
        body, name=name, grid=(FF_PAIRS, ni),
        in_specs=[rows, rows_next, rows, u_cur, u_prev, u_next, u_cur, u_prev, u_next, cslab(0), cslab(FF_PAIRS),
                  pl.BlockSpec((FF_SLAB, D), lambda j, i: (j, 0))],
        out_specs=[u_cur, u_cur, pl.BlockSpec((FF_SLAB, D), lambda j, i: (j, 0)), wslab, wslab, dslab, dslab],
        out_shape=[_sds((FF_PAIRS, S, FF_SLAB), BF16), _sds((FF_PAIRS, S, FF_SLAB), BF16), _sds((D_FF, D), BF16),
                   _sds((FF_PAIRS, FF_SLAB, D), BF16), _sds((FF_PAIRS, FF_SLAB, D), BF16),
                   _sds((FF_PAIRS, FFN_CONV, FF_SLAB)), _sds((FF_PAIRS, FFN_CONV, FF_SLAB))],
        scratch_shapes=[pltpu.VMEM((FF_SLAB, D), F32), pltpu.VMEM((FF_SLAB, D), F32), pltpu.VMEM((FF_SLAB, D), F32),
                        pltpu.VMEM((8, FF_SLAB), F32), pltpu.VMEM((8, FF_SLAB), F32)],
        compiler_params=_params(("parallel", "arbitrary")),
    )(dx2, dx2, h2, ug, ug, ug, uu, uu, uu, conv_w, conv_w, w_down)


def _mm_slabs(a, w, w_off, *, name, res=None, tm=1024, tn=1024):
    nk, S, _ = a.shape
    D = w.shape[2]
    has_res = res is not None

    def body(*refs):
        if has_res:
            a_ref, w_ref, r_ref, o_ref, acc_ref = refs
        else:
            a_ref, w_ref, o_ref, acc_ref = refs
        k = pl.program_id(2)
        part = jnp.dot(a_ref[...], w_ref[...], preferred_element_type=F32)

        @pl.when(k == 0)
        def _():
            acc_ref[...] = part

        @pl.when(k > 0)
        def _():
            acc_ref[...] += part

        @pl.when(k == nk - 1)
        def _():
            o_ref[...] = acc_ref[...] + r_ref[...] if has_res else acc_ref[...]

    o_spec = pl.BlockSpec((tm, tn), lambda i, j, k: (i, j))
    return pl.pallas_call(
        body, name=name, grid=(S // tm, D // tn, nk),
        in_specs=[pl.BlockSpec((None, tm, FF_SLAB), lambda i, j, k: (k, i, 0)),
                  pl.BlockSpec((None, FF_SLAB, tn), lambda i, j, k: (k + w_off, 0, j))] + ([o_spec] if has_res else []),
        out_specs=o_spec, out_shape=_sds((S, D)), scratch_shapes=[pltpu.VMEM((tm, tn), F32)],
        compiler_params=_params(("parallel", "parallel", "arbitrary")),
    )(*((a, w, res) if has_res else (a, w)))


def _local_step(x, tgt, norm1_w, w_a, w_z, w_b, conv_a, a_log, dt_bias, gnw, norm2_w, final_w, late_weights, emit):
    wgrad = functools.partial(_mm, ta=True, out_dtype=BF16)
    h1 = _rms_fwd(x, norm1_w, name="rms1_fwd")
    proj_a = _mm(h1, w_a, tb=True, name="proj_a", tn=A_COLS, tk=1024)
    proj_z = _mm(h1, w_z, tb=True, name="proj_z", tk=1024)
    proj_b = _mm(h1, w_b, tb=True, name="proj_b", tn=768, tk=1024)
    qn, kn, v, gcb, bb = _gdn_prep_fwd(proj_a, conv_a, a_log, dt_bias, name="gdn_prep_fwd")
    uv, wk, at, tmat, wkb, qdb, keb = _gdn_chunk_fwd(qn, kn, v, gcb, bb, name="gdn_chunk_fwd")
    o, u, sp, oab = _gdn_scan_fwd(uv, at, wkb, qdb, keb, gcb, proj_z, gnw, name="gdn_scan_fwd")
    oab, lse = _attn_fwd(proj_b, oab, name="attn_fwd")
    w_out, w_up, conv_f, w_down = late_weights(oab)
    x1 = _mm(oab, w_out, res=x, name="out_proj", tk=1024)
    h2 = _rms_fwd(x1, norm2_w, name="rms2_fwd")
    x2, ug, uu = _ffn_fwd(h2, x1, w_up, conv_f, w_down, name="ffn_fwd")
    dx2, dx2_b, d_final, loss = _loss_head(x2, final_w, tgt, name="loss_head")
    dug, duu, g_down, g_up_g, g_up_u, dcw_g, dcw_u = _ffn_bwd(dx2_b, h2, ug, uu, conv_f, w_down, name="ffn_bwd")
    token = emit("ffn", w_down=g_down, w_up=jnp.concatenate([g_up_g, g_up_u], axis=0),
                 conv_f=jnp.concatenate([dcw_g, dcw_u], axis=0))
    dh2 = _mm_slabs(dug, w_up, 0, name="ffn_up_dx_gate")
    dh2 = _mm_slabs(duu, w_up, FF_PAIRS, res=dh2, name="ffn_up_dx_up")
    dx1, d_norm2 = _rms_bwd(dh2, x1, _behind(norm2_w, token), dx2, name="rms2_bwd")
    d_oab = _mm(dx1, w_out, tb=True, name="out_proj_dx", tk=1024)
    gnw = _behind(gnw, emit("out", w_out=wgrad(oab, dx1, name="out_proj_dw")))
    dz, d_gnw, du, dwk, dat, dqd, dke, dgl = _gdn_scan_bwd(d_oab, o, proj_z, gnw, sp, u, at, wkb, qdb, keb, gcb, name="gdn_scan_bwd")
    dqn, dkn, dv, dg, dbeta = _gdn_chunk_bwd(qn, kn, gcb, bb, tmat, uv, wk, du, dwk, dat, dqd, dke, dgl, name="gdn_chunk_bwd")
    dc, dba, d_small = _gdn_prep_bwd(dqn, dkn, dv, dg, dbeta, proj_a, conv_a, a_log, dt_bias, name="gdn_prep_bwd")
    d_pa, d_conv_a = _gdn_conv_bwd(dc, dba, proj_a, conv_a, name="gdn_conv_bwd")
    d_pb = _attn_bwd(proj_b, oab, d_oab, lse, name="attn_bwd")
    g_a = wgrad(d_pa, h1, name="proj_a_dw", tm=A_COLS)
    g_z = wgrad(dz, h1, name="proj_z_dw")
    g_b = wgrad(d_pb, h1, name="proj_b_dw", tm=768)
    w_z = _behind(w_z, emit("in", w_a=g_a, w_z=g_z, w_b=g_b, conv_a=d_conv_a))
    dh1 = _mm(dz, w_z, name="proj_z_dx")
    dh1 = _mm(d_pa, w_a, res=dh1, name="proj_a_dx", tk=A_COLS)
    dh1 = _mm(d_pb, w_b, res=dh1, name="proj_b_dx", tk=1536)
    grad_x, d_norm1 = _rms_bwd(dh1, x, norm1_w, dx1, name="rms1_bwd")
    small = dict(norm1=d_norm1, small=d_small, gnw=d_gnw, norm2=d_norm2, final=d_final)
    return loss, grad_x, small


_O1 = 3 * GDN_WIDTH
_O2 = _O1 + GDN_WIDTH
_O3 = _O2 + 2 * GDN_HEADS


def _split_w_in(w_t):
    d = w_t.shape[1]
    pad = jnp.zeros((A_COLS - _O1 - 2 * GDN_HEADS, d), w_t.dtype)
    w_a = jnp.concatenate([w_t[:_O1], w_t[_O2:_O3], pad], axis=0)
    w_b = w_t[_O3:].reshape(3, DIL_PAIRS, 128, d).transpose(1, 0, 2, 3).reshape(3 * DIL_WIDTH, d)
    return w_a, w_t[_O1:_O2], w_b


def _merge_g_in(g_a, g_z, g_b):
    d = g_a.shape[1]
    g_b = g_b.reshape(DIL_PAIRS, 3, 128, d).transpose(1, 0, 2, 3).reshape(3 * DIL_WIDTH, d)
    return jnp.concatenate([g_a[:_O1], g_z, g_a[_O1:_O1 + 2 * GDN_HEADS], g_b], axis=0)


MESH = pl.DeviceIdType.MESH
ANY = pl.BlockSpec(memory_space=pl.ANY)


def _position():
    return lax.axis_index("x"), lax.axis_index("y"), lax.axis_index("c")


def _slot(p):
    return 4 * p[0] + 2 * p[1] + p[2]


def _all_gather(blocks, *, name):
    n = len(blocks)

    def body(*refs):
        ins, outs = refs[:n], refs[n:2 * n]
        send_sems, recv_sems, local_sems = refs[2 * n:]
        x, y, c = _position()
        me, sibling = (x, y, c), (x, y, 1 - c)
        chips = [(1 - x, y), (x, 1 - y), (1 - x, 1 - y)]

        def copy(a, k, block, to, src=None):
            dst = outs[a].at[_slot(block)]
            return pltpu.make_async_remote_copy(
                src_ref=dst if src is None else src, dst_ref=dst, send_sem=send_sems.at[a, k], recv_sem=recv_sems.at[a, k],
                device_id=to, device_id_type=MESH)

        mine = [pltpu.make_async_copy(ins[a], outs[a].at[_slot(me)], local_sems.at[a]) for a in range(n)]
        for cp in mine:
            cp.start()
        first = []
        for a in range(n):
            first.append(copy(a, 0, me, sibling, src=ins[a]))
            first += [copy(a, 1 + j, me, (*chip, c), src=ins[a]) for j, chip in enumerate(chips)]
        for cp in first:
            cp.start()
        passed = []
        for j, chip in enumerate(chips):
            for a in range(n):
                copy(a, 1 + j, (*chip, c), me).wait_recv()
                fwd = copy(a, 4 + j, (*chip, c), sibling)
                fwd.start()
                passed.append(fwd)
        for a in range(n):
            copy(a, 0, sibling, me).wait_recv()
            for j, chip in enumerate(chips):
                copy(a, 4 + j, (*chip, 1 - c), me).wait_recv()
        for cp in first + passed:
            cp.wait_send()
        for cp in mine:
            cp.wait()

    return pl.pallas_call(
        body, name=name, in_specs=[ANY] * n, out_specs=[ANY] * n,
        out_shape=[_sds((N_DEV,) + b.shape, b.dtype) for b in blocks],
        scratch_shapes=[pltpu.SemaphoreType.DMA((n, 7)), pltpu.SemaphoreType.DMA((n, 7)), pltpu.SemaphoreType.DMA((n,))],
    )(*blocks)


def _gather_direct(block, *, name):
    def body(in_ref, out_ref, send_sems, recv_sems, local_sem):
        x, y, c = _position()
        me = _slot((x, y, c))
        mine = pltpu.make_async_copy(in_ref, out_ref.at[me], local_sem)
        mine.start()
        copies = [pltpu.make_async_remote_copy(
            src_ref=in_ref, dst_ref=out_ref.at[me], send_sem=send_sems.at[k - 1], recv_sem=recv_sems.at[k - 1],
            device_id=_peer_of(k, x, y, c), device_id_type=MESH) for k in range(1, N_DEV)]
        for cp in copies:
            cp.start()
        for cp in copies:
            cp.wait()
        mine.wait()

    return pl.pallas_call(
        body, name=name, in_specs=[pl.BlockSpec(memory_space=pltpu.VMEM)], out_specs=pl.BlockSpec(memory_space=pltpu.VMEM),
        out_shape=_sds((N_DEV,) + block.shape, block.dtype),
        scratch_shapes=[pltpu.SemaphoreType.DMA((N_DEV - 1,)), pltpu.SemaphoreType.DMA((N_DEV - 1,)), pltpu.SemaphoreType.DMA],
    )(block)


HBM = pl.BlockSpec(memory_space=pltpu.HBM)
SEM = pl.BlockSpec(memory_space=pltpu.SEMAPHORE)
EFFECT = pltpu.SideEffectType.DATAFLOW_SIDE_EFFECTING


def _peer_of(k, x, y, c):
    return (1 - x if k & 4 else x, 1 - y if k & 2 else y, 1 - c if k & 1 else c)


def _flight(a, k):
    return a * (N_DEV - 1) + k - 1


def _exchange_start(arrays, *, name, broadcast=False):
    n = len(arrays)

    def body(*refs):
        ins, lands = refs[:n], refs[n:2 * n]
        send_sems, recv_sems = refs[2 * n:2 * n + 2]
        token = refs[-1]
        x, y, c = _position()
        me = _slot((x, y, c))
        for k in range(1, N_DEV):
            peer = _peer_of(k, x, y, c)
            for a in range(n):
                pltpu.make_async_remote_copy(
                    src_ref=ins[a] if broadcast else ins[a].at[_slot(peer)], dst_ref=lands[a].at[me],
                    send_sem=send_sems.at[_flight(a, k)], recv_sem=recv_sems.at[_flight(a, k)],
                    device_id=peer, device_id_type=MESH).start()
        token[...] = jnp.zeros_like(token)

    land_shapes = [((N_DEV,) + s.shape) if broadcast else s.shape for s in arrays]
    lands = [pltpu.with_memory_space_constraint(lax.empty(shp, s.dtype), pltpu.HBM) for shp, s in zip(land_shapes, arrays)]
    srcs = [pltpu.with_memory_space_constraint(s, pltpu.HBM) for s in arrays]
    outs = pl.pallas_call(
        body, name=name, in_specs=[HBM] * (2 * n),
        out_specs=[SEM, SEM] + [HBM] * (2 * n) + [pl.BlockSpec(memory_space=pltpu.VMEM)],
        out_shape=[pltpu.SemaphoreType.DMA((n * (N_DEV - 1),)), pltpu.SemaphoreType.DMA((n * (N_DEV - 1),))]
        + [pltpu.HBM(s.shape, s.dtype) for s in arrays] + [pltpu.HBM(shp, s.dtype) for shp, s in zip(land_shapes, arrays)]
        + [_sds((8, 128))],
        input_output_aliases={i: 2 + i for i in range(2 * n)},
        compiler_params=pltpu.CompilerParams(has_side_effects=EFFECT),
    )(*srcs, *lands)
    return outs[0], outs[1], outs[2:2 + n], outs[2 + n:2 + 2 * n], outs[-1]


def _exchange_wait(send_sems, recv_sems, srcs, lands, after, *, name, broadcast=False):
    n = len(srcs)

    def body(*refs):
        ins, lnd = refs[:n], refs[n:2 * n]
        send_ref, recv_ref = refs[2 * n:2 * n + 2]
        x, y, c = _position()
        for k in range(1, N_DEV):
            for a in range(n):
                cp = pltpu.make_async_remote_copy(
                    src_ref=ins[a] if broadcast else ins[a].at[0], dst_ref=lnd[a].at[0], send_sem=send_ref.at[_flight(a, k)],
                    recv_sem=recv_ref.at[_flight(a, k)], device_id=_peer_of(k, x, y, c), device_id_type=MESH)
                cp.wait_send()
                cp.wait_recv()

    outs = pl.pallas_call(
        body, name=name, in_specs=[HBM] * (2 * n) + [SEM, SEM, ANY], out_specs=[HBM] * (2 * n),
        out_shape=[pltpu.HBM(s.shape, s.dtype) for s in srcs] + [pltpu.HBM(s.shape, s.dtype) for s in lands],
        input_output_aliases={i: i for i in range(2 * n)},
        compiler_params=pltpu.CompilerParams(has_side_effects=EFFECT),
    )(*srcs, *lands, send_sems, recv_sems, after)
    return outs[:n], outs[n:]


def _with_own(landed, srcs, me, broadcast=False):
    own = srcs if broadcast else [lax.dynamic_index_in_dim(s, me, 0, keepdims=False) for s in srcs]
    return [lax.dynamic_update_index_in_dim(l, o, me, 0) for l, o in zip(landed, own)]


def _behind(x, token):
    return x if token is None else x + token[0, 0].astype(x.dtype)


def _adamw(parts, w, m, v, *, name, tr=None, tc=None):
    R, C = w.shape
    tr = R if tr is None else tr
    tc = C if tc is None else tc
    assert R % tr == 0 and C % tc == 0
    c1 = 1.0 - ADAM_B1 ** ADAM_STEP
    c2 = 1.0 - ADAM_B2 ** ADAM_STEP

    def body(p_ref, w_ref, m_ref, v_ref, g_ref, d_ref, nm_ref, nv_ref):
        g = p_ref[0].astype(F32)
        for s in range(1, N_DEV):
            g = g + p_ref[s].astype(F32)
        nm = ADAM_B1 * m_ref[...] + (1.0 - ADAM_B1) * g
        nv = ADAM_B2 * v_ref[...] + (1.0 - ADAM_B2) * (g * g)
        g_ref[...] = g
        nm_ref[...] = nm
        nv_ref[...] = nv
        d_ref[...] = -ADAM_LR * ((nm / c1) / (jnp.sqrt(nv / c2) + ADAM_EPS) + ADAM_WD * w_ref[...])

    blk = pl.BlockSpec((tr, tc), lambda i, j: (i, j))
    return pl.pallas_call(
        body, name=name, grid=(R // tr, C // tc),
        in_specs=[pl.BlockSpec((N_DEV, tr, tc), lambda i, j: (0, i, j)), blk, blk, blk],
        out_specs=[blk] * 4, out_shape=[_sds((R, C))] * 4, compiler_params=_params(("parallel", "parallel")),
    )(parts, w, m, v)


_SMALL_ROWS = 8


def _pack_small(norm1, norm2, final, gnw, a_log, dt_bias, loss=None):
    loss = jnp.zeros((1, 128), F32) if loss is None else loss
    row3 = jnp.concatenate([gnw, a_log, dt_bias, jnp.zeros((1, 128 - 2 * GDN_HEADS), F32), loss,
                            jnp.zeros((1, D_MODEL - 3 * 128), F32)], axis=1)
    return jnp.concatenate([norm1, norm2, final, row3, jnp.zeros((_SMALL_ROWS - 4, D_MODEL), F32)], axis=0)


def _unpack_small(p):
    return (p[0:1], p[1:2], p[2], p[3:4, 0:128], p[3:4, 128:128 + GDN_HEADS], p[3:4, 128 + GDN_HEADS:128 + 2 * GDN_HEADS])


def _slabs_by_cols(g):
    r = g.shape[0]
    return g.reshape(r, N_DEV, -1).transpose(1, 0, 2)


def _cols_from_slabs(s):
    return s.transpose(1, 0, 2).reshape(s.shape[1], -1)


def kernel(x, norm1_w, w_in, conv_qkv_w, a_log, dt_bias, gdn_norm_w, w_out, norm2_w, w_up, ffn_conv_w, w_down, final_norm_w, loss_target, m_norm1_w, m_w_in, m_conv_qkv_w, m_a_log, m_dt_bias, m_gdn_norm_w, m_w_out, m_norm2_w, m_w_up, m_ffn_conv_w, m_w_down, m_final_norm_w, v_norm1_w, v_w_in, v_conv_qkv_w, v_a_log, v_dt_bias, v_gdn_norm_w, v_w_out, v_norm2_w, v_w_up, v_ffn_conv_w, v_w_down, v_final_norm_w):
    bf = lambda a: a.astype(BF16)
    me = _slot(_position())
    t_in = lambda a: a[0].T
    gw_in, g_conv_a = _all_gather([bf(t_in(w_in)), conv_qkv_w[0]], name="gather_w_in")
    w_a, w_z, w_b = _split_w_in(gw_in.reshape(-1, D_MODEL))
    late_src, _ = lax.optimization_barrier(([bf(w_out[0]), bf(t_in(w_up)), bf(w_down[0]), ffn_conv_w[0]], gw_in))
    l_send, l_recv, l_srcs, l_lands, l_token = _exchange_start(late_src, name="weights_start", broadcast=True)

    def late_weights(after):
        srcs, landed = _exchange_wait(l_send, l_recv, l_srcs, l_lands, after, name="weights_wait", broadcast=True)
        gw_out, gw_up, gw_down, g_conv_f = _with_own(landed, srcs, me, broadcast=True)
        return gw_out.reshape(D_MODEL, D_MODEL), gw_up, g_conv_f, gw_down.reshape(D_FF, D_MODEL)

    flights = {}

    def emit(group, **grads):
        if group == "in":
            slabs = dict(w_in=_merge_g_in(grads["w_a"], grads["w_z"], grads["w_b"]).reshape(N_DEV, -1, D_MODEL),
                         conv_a=_slabs_by_cols(grads["conv_a"]))
        elif group == "ffn":
            slabs = dict(w_down=grads["w_down"].reshape(N_DEV, -1, D_MODEL), w_up=grads["w_up"], conv_f=grads["conv_f"])
        else:
            slabs = {k: v.reshape(N_DEV, -1, D_MODEL) for k, v in grads.items()}
        names = list(slabs)
        *flight, token = _exchange_start([slabs[k] for k in names], name="grads_start_" + group)
        flights[group] = (names, flight)
        return token

    loss, grad_x, g = _local_step(
        x[0], loss_target[0], _behind(norm1_w, l_token), w_a, w_z, w_b, _cols_from_slabs(g_conv_a), a_log, dt_bias,
        gdn_norm_w, norm2_w, final_norm_w[None], late_weights, emit)
    small_all = _gather_direct(
        _pack_small(g["norm1"], g["norm2"], g["final"], g["gnw"], g["small"][:, 0:GDN_HEADS],
                    g["small"][:, GDN_HEADS:2 * GDN_HEADS], loss), name="gather_small")
    got = {}
    for group in ("ffn", "out", "in"):
        names, (send_sems, recv_sems, srcs, lands) = flights[group]
        srcs, landed = _exchange_wait(send_sems, recv_sems, srcs, lands, small_all, name="grads_wait_" + group)
        got.update(zip(names, _with_own(landed, srcs, me)))
    o_in = [o.T for o in _adamw(got["w_in"], t_in(w_in), t_in(m_w_in), t_in(v_w_in), name="adamw_w_in", tc=256)]
    o_out = _adamw(got["w_out"], w_out[0], m_w_out[0], v_w_out[0], name="adamw_w_out")
    o_up = [o.T for o in _adamw(got["w_up"], t_in(w_up), t_in(m_w_up), t_in(v_w_up), name="adamw_w_up", tr=176)]
    o_down = _adamw(got["w_down"], w_down[0], m_w_down[0], v_w_down[0], name="adamw_w_down", tr=176)
    o_ca = _adamw(got["conv_a"], conv_qkv_w[0], m_conv_qkv_w[0], v_conv_qkv_w[0], name="adamw_conv_a")
    o_cf = _adamw(got["conv_f"], ffn_conv_w[0], m_ffn_conv_w[0], v_ffn_conv_w[0], name="adamw_conv_f")
    o_small = _adamw(
        small_all, _pack_small(norm1_w, norm2_w, final_norm_w[None], gdn_norm_w, a_log, dt_bias),
        _pack_small(m_norm1_w, m_norm2_w, m_final_norm_w[None], m_gdn_norm_w, m_a_log, m_dt_bias),
        _pack_small(v_norm1_w, v_norm2_w, v_final_norm_w[None], v_gdn_norm_w, v_a_log, v_dt_bias), name="adamw_small")
    total_loss = o_small[0][3, 256]
    outs = [total_loss, grad_x[None]]
    for k in range(4):
        n1, n2, fin, gn, al, dt = _unpack_small(o_small[k])
        outs += [n1, o_in[k][None], o_ca[k][None], al, dt, gn, o_out[k][None], n2, o_up[k][None], o_cf[k][None], o_down[k][None], fin]
    return tuple(outs)
```

```python
import functools

import jax
import jax.numpy as jnp
from jax import lax
from jax.experimental import pallas as pl
from jax.experimental.pallas import tpu as pltpu

F32 = jnp.float32
BF16 = jnp.bfloat16

N_DEV = 8
D_MODEL = 1024
GDN_HEADS = 4
GDN_DIM = 128
GDN_WIDTH = GDN_HEADS * GDN_DIM
GDN_CONV = 4
CHUNK = 64
CHUNKS_PER_STEP = 2
DIL_HEADS = 8
DIL_DIM = 64
DIL_WIDTH = DIL_HEADS * DIL_DIM
DIL_PAIRS = DIL_HEADS // 2
DILATIONS = (1, 4, 16)
BAND = 128
D_FF = 2816
FFN_CONV = 3
EPS = 1e-6
A_COLS = 3 * GDN_WIDTH + 128
HALO = 8

ADAM_LR = 0.001
ADAM_B1 = 0.9
ADAM_B2 = 0.999
ADAM_EPS = 1e-08
ADAM_WD = 0.01
ADAM_STEP = 10

VMEM_LIMIT_BYTES = 56 * 1024 * 1024
NEG_BIG = -1e30


def _params(sem=None):
    return pltpu.CompilerParams(dimension_semantics=sem, vmem_limit_bytes=VMEM_LIMIT_BYTES)


def _sds(shape, dtype=F32):
    return jax.ShapeDtypeStruct(shape, dtype)


def _bdot(a, b):
    return jnp.dot(a.astype(BF16), b.astype(BF16), preferred_element_type=F32)


def _bdot_nt(a, b):
    return lax.dot_general(a.astype(BF16), b.astype(BF16), (((1,), (1,)), ((), ())), preferred_element_type=F32)


def _bdot_tn(a, b):
    return lax.dot_general(a.astype(BF16), b.astype(BF16), (((0,), (0,)), ((), ())), preferred_element_type=F32)


def _split(a):
    hi = a.astype(BF16)
    lo = (a - hi.astype(F32)).astype(BF16)
    return hi, lo


def _dot3(a, b, dims):
    ah, al = _split(a)
    bh, bl = _split(b)
    d = functools.partial(lax.dot_general, dimension_numbers=(dims, ((), ())), preferred_element_type=F32)
    return d(ah, bh) + (d(al, bh) + d(ah, bl))


def _exact_tri_dot(tri, g):
    g1 = g.astype(BF16)
    r1 = g - g1.astype(F32)
    g2 = r1.astype(BF16)
    g3 = (r1 - g2.astype(F32)).astype(BF16)
    t = tri.astype(BF16)
    d = functools.partial(jnp.dot, preferred_element_type=F32)
    return d(t, g1) + (d(t, g2) + d(t, g3))


def _sigmoid(x):
    return 1.0 / (1.0 + jnp.exp(-x))


def _dsilu(x, sg):
    return sg * (1.0 + x * (1.0 - sg))


def _rms_bwd_rows(dh, x, w):
    r = lax.rsqrt(jnp.mean(x * x, axis=-1, keepdims=True) + EPS)
    xh = x * r
    gw = dh * w
    return r * (gw - xh * jnp.mean(gw * xh, axis=-1, keepdims=True)), jnp.sum(dh * xh, axis=0, keepdims=True)


def _mm(a, b, *, name, ta=False, tb=False, res=None, norm_bwd=None, out_dtype=F32, tm=512, tn=512, tk=512):
    if ta:
        K, M = a.shape
    else:
        M, K = a.shape
    if tb:
        N, Kb = b.shape
    else:
        Kb, N = b.shape
    assert K == Kb, (a.shape, b.shape)
    tm, tn, tk = min(tm, M), min(tn, N), min(tk, K)
    assert M % tm == 0 and N % tn == 0 and K % tk == 0, (name, M, N, K, tm, tn, tk)
    nk = K // tk
    dims = (((0 if ta else 1,), (1 if tb else 0,)), ((), ()))
    has_res = res is not None
    has_norm = norm_bwd is not None
    assert not has_norm or tn == N

    def body(*refs):
        a_ref, b_ref = refs[:2]
        r_ref = refs[2] if has_res else None
        if has_norm:
            x_ref, w_ref, skip_ref, o_ref, dw_ref, acc_ref = refs[2 + has_res:]
        else:
            o_ref, acc_ref = refs[2 + has_res:]
        i, k = pl.program_id(0), pl.program_id(2)
        part = lax.dot_general(a_ref[...].astype(BF16), b_ref[...].astype(BF16), dims, preferred_element_type=F32)

        @pl.when(k == 0)
        def _():
            acc_ref[...] = part

        @pl.when(k > 0)
        def _():
            acc_ref[...] += part

        @pl.when(k == nk - 1)
        def _():
            r = acc_ref[...]
            if has_res:
                r = r + r_ref[...]
            if has_norm:
                dx, dw = _rms_bwd_rows(r, x_ref[...], w_ref[...])
                o_ref[...] = skip_ref[...] + dx

                @pl.when(i == 0)
                def _():
                    dw_ref[...] = dw

                @pl.when(i > 0)
                def _():
                    dw_ref[...] += dw
            else:
                o_ref[...] = r.astype(out_dtype)

    a_spec = pl.BlockSpec((tk, tm), lambda i, j, k: (k, i)) if ta else pl.BlockSpec((tm, tk), lambda i, j, k: (i, k))
    b_spec = pl.BlockSpec((tn, tk), lambda i, j, k: (j, k)) if tb else pl.BlockSpec((tk, tn), lambda i, j, k: (k, j))
    o_spec = pl.BlockSpec((tm, tn), lambda i, j, k: (i, j))
    one = pl.BlockSpec((1, tn), lambda i, j, k: (0, 0))
    in_specs = [a_spec, b_spec] + [o_spec] * has_res + ([o_spec, one, o_spec] if has_norm else [])
    args = (a, b) + ((res,) if has_res else ()) + (tuple(norm_bwd) if has_norm else ())
    return pl.pallas_call(
        body, name=name, grid=(M // tm, N // tn, nk), in_specs=in_specs,
        out_specs=[o_spec, one] if has_norm else o_spec,
        out_shape=[_sds((M, N)), _sds((1, N))] if has_norm else _sds((M, N), out_dtype),
        scratch_shapes=[pltpu.VMEM((tm, tn), F32)],
        compiler_params=_params(("arbitrary" if has_norm else "parallel", "parallel", "arbitrary")),
    )(*args)


def _in_proj(x, norm_w, w_a, w_z, w_b, *, name, tm=512):
    S, D = x.shape
    ws = (w_a, w_z, w_b)

    def body(x_ref, nw_ref, wa_ref, wz_ref, wb_ref, h_ref, pa_ref, pz_ref, pb_ref):
        xv = x_ref[...]
        r = lax.rsqrt(jnp.mean(xv * xv, axis=-1, keepdims=True) + EPS)
        h = (xv * r * nw_ref[...]).astype(BF16)
        h_ref[...] = h
        for w_ref, p_ref in ((wa_ref, pa_ref), (wz_ref, pz_ref), (wb_ref, pb_ref)):
            p_ref[...] = lax.dot_general(h, w_ref[...], (((1,), (1,)), ((), ())), preferred_element_type=F32)

    row = lambda n: pl.BlockSpec((tm, n), lambda i: (i, 0))
    full = lambda a: pl.BlockSpec(a.shape, lambda i: (0, 0))
    return pl.pallas_call(
        body, name=name, grid=(S // tm,), in_specs=[row(D), full(norm_w)] + [full(w) for w in ws],
        out_specs=[row(D)] + [row(w.shape[0]) for w in ws],
        out_shape=[_sds((S, D), BF16)] + [_sds((S, w.shape[0])) for w in ws], compiler_params=_params(("parallel",)),
    )(x, norm_w, *ws)


def _out_proj_norm(a, w, x, norm_w, *, name, tm=512):
    S, D = x.shape

    def body(a_ref, w_ref, x_ref, nw_ref, x1_ref, h_ref):
        x1 = x_ref[...] + jnp.dot(a_ref[...], w_ref[...], preferred_element_type=F32)
        x1_ref[...] = x1
        r = lax.rsqrt(jnp.mean(x1 * x1, axis=-1, keepdims=True) + EPS)
        h_ref[...] = (x1 * r * nw_ref[...]).astype(BF16)

    row = pl.BlockSpec((tm, D), lambda i: (i, 0))
    return pl.pallas_call(
        body, name=name, grid=(S // tm,),
        in_specs=[pl.BlockSpec((tm, a.shape[1]), lambda i: (i, 0)), pl.BlockSpec(w.shape, lambda i: (0, 0)), row,
                  pl.BlockSpec((1, D), lambda i: (0, 0))],
        out_specs=[row, row], out_shape=[_sds((S, D)), _sds((S, D), BF16)], compiler_params=_params(("parallel",)),
    )(a, w, x, norm_w)


def _shifted(x, start, n):
    aligned = -(-start // HALO) * HALO
    assert aligned + n <= x.shape[0], (start, n, x.shape)
    return (x if aligned == start else pltpu.roll(x, aligned - start, axis=0))[aligned:aligned + n]


def _conv_rows(prev, cur, w, taps):
    n = cur.shape[0]
    xs = jnp.concatenate([prev, cur], axis=0)
    base = HALO - (taps - 1)
    out = _shifted(xs, base, n) * w[0:1]
    for i in range(1, taps):
        out = out + _shifted(xs, base + i, n) * w[i:i + 1]
    return out


def _conv_rows_bwd(cur_d, next_d, prev_x, cur_x, w, taps):
    n = cur_d.shape[0]
    ds = jnp.concatenate([cur_d, next_d], axis=0)
    dx = _shifted(ds, taps - 1, n) * w[0:1]
    for i in range(1, taps):
        dx = dx + _shifted(ds, taps - 1 - i, n) * w[i:i + 1]
    xs = jnp.concatenate([prev_x, cur_x], axis=0)
    base = HALO - (taps - 1)
    dws = [jnp.sum(cur_d * _shifted(xs, base + i, n), axis=0, keepdims=True) for i in range(taps)]
    return dx, jnp.concatenate(dws, axis=0)


def _halo_specs(tm, width, col, nblk):
    per = tm // HALO
    prev = pl.BlockSpec((HALO, width), lambda i, *_: (jnp.maximum(i * per - 1, 0), col))
    nxt = pl.BlockSpec((HALO, width), lambda i, *_: (jnp.minimum((i + 1) * per, nblk * per - 1), col))
    return prev, nxt


def _softplus(x):
    return jnp.maximum(x, 0.0) + jnp.log1p(jnp.exp(-jnp.abs(x)))


def _chunk_tri(tm, upper=False):
    r = lax.broadcasted_iota(jnp.int32, (tm, tm), 0)
    c = lax.broadcasted_iota(jnp.int32, (tm, tm), 1)
    same = lax.div(r, CHUNK) == lax.div(c, CHUNK)
    order = (c >= r) if upper else (c <= r)
    return jnp.where(same & order, 1.0, 0.0)


def _gdn_prep_fwd(proj_a, conv_w, a_log, dt_bias, *, name, tm=256):
    S = proj_a.shape[0]
    nblk = S // tm
    W3 = 3 * GDN_WIDTH

    def body(cur_ref, prev_ref, ba_ref, cw_ref, al_ref, dt_ref, qn_ref, kn_ref, v_ref, gcb_ref, bb_ref):
        i = pl.program_id(0)
        prev = jnp.where(i > 0, prev_ref[...], 0.0)
        c = _conv_rows(prev, cur_ref[...], cw_ref[...], GDN_CONV)
        a = c * _sigmoid(c)
        ba = ba_ref[...]
        lane = lax.broadcasted_iota(jnp.int32, (tm, 128), 1)
        g4 = jnp.zeros((tm, 128), F32)
        for h in range(GDN_HEADS):
            sl = slice(GDN_DIM * h, GDN_DIM * (h + 1))
            qh = a[:, GDN_DIM * h:GDN_DIM * (h + 1)]
            kh = a[:, GDN_WIDTH + GDN_DIM * h:GDN_WIDTH + GDN_DIM * (h + 1)]
            qn_ref[:, sl] = qh * (lax.rsqrt(jnp.sum(qh * qh, axis=-1, keepdims=True) + EPS) * (GDN_DIM ** -0.5))
            kn_ref[:, sl] = kh * lax.rsqrt(jnp.sum(kh * kh, axis=-1, keepdims=True) + EPS)
            beta = _sigmoid(ba[:, h:h + 1])
            bb_ref[:, sl] = jnp.broadcast_to(beta, (tm, GDN_DIM))
            g = -jnp.exp(al_ref[0:1, h:h + 1]) * _softplus(ba[:, GDN_HEADS + h:GDN_HEADS + h + 1] + dt_ref[0:1, h:h + 1])
            g4 = jnp.where(lane == h, g, g4)
        v_ref[...] = a[:, 2 * GDN_WIDTH:]
        gc = _exact_tri_dot(_chunk_tri(tm), g4)
        for h in range(GDN_HEADS):
            gcb_ref[:, GDN_DIM * h:GDN_DIM * (h + 1)] = jnp.broadcast_to(gc[:, h:h + 1], (tm, GDN_DIM))

    prev_spec, _ = _halo_specs(tm, W3, 0, nblk)
    row = pl.BlockSpec((tm, GDN_WIDTH), lambda i: (i, 0))
    small = lambda a: pl.BlockSpec(a.shape, lambda i: (0, 0))
    return pl.pallas_call(
        body, name=name, grid=(nblk,),
        in_specs=[pl.BlockSpec((tm, W3), lambda i: (i, 0)), prev_spec,
                  pl.BlockSpec((tm, 128), lambda i: (i, W3 // 128)), small(conv_w), small(a_log), small(dt_bias)],
        out_specs=[row] * 5, out_shape=[_sds((S, GDN_WIDTH))] * 5, compiler_params=_params(("parallel",)),
    )(proj_a, proj_a, proj_a, conv_w, a_log, dt_bias)


GDN_STACK = GDN_HEADS * CHUNK


def _stack(ref, rows):
    return jnp.concatenate([ref[rows, GDN_DIM * h:GDN_DIM * (h + 1)] for h in range(GDN_HEADS)], axis=0)


def _unstack_to(ref, rows, x):
    for h in range(GDN_HEADS):
        ref[rows, GDN_DIM * h:GDN_DIM * (h + 1)] = x[CHUNK * h:CHUNK * (h + 1)].astype(ref.dtype)


def _stack_masks():
    r = lax.broadcasted_iota(jnp.int32, (GDN_STACK, GDN_STACK), 0)
    c = lax.broadcasted_iota(jnp.int32, (GDN_STACK, GDN_STACK), 1)
    same = (r & -CHUNK) == (c & -CHUNK)
    return same & (r >= c), same & (r > c), r == c


def _stack_decay(gs, bs, incl):
    g2 = jnp.concatenate([gs, gs], axis=1)
    diff = g2 - g2.T
    dec = jnp.where(incl, jnp.exp(jnp.where(incl, diff, 0.0)), 0.0)
    return dec, jnp.concatenate([bs, bs], axis=1).T


def _head_mask():
    r = lax.broadcasted_iota(jnp.int32, (GDN_STACK, GDN_WIDTH), 0)
    c = lax.broadcasted_iota(jnp.int32, (GDN_STACK, GDN_WIDTH), 1)
    return (r & -CHUNK) * (GDN_DIM // CHUNK) == (c & -GDN_DIM)


def _head_spread(x):
    return jnp.where(_head_mask(), jnp.concatenate([x] * GDN_HEADS, axis=1), 0.0)


def _head_diag(x):
    xm = jnp.where(_head_mask(), x, 0.0)
    out = xm[:, 0:GDN_DIM]
    for h in range(1, GDN_HEADS):
        out = out + xm[:, GDN_DIM * h:GDN_DIM * (h + 1)]
    return out


def _last_rows(gs, n):
    return jnp.concatenate([jnp.broadcast_to(gs[CHUNK * (h + 1) - 1:CHUNK * (h + 1)], (n, GDN_DIM)) for h in range(GDN_HEADS)], axis=0)


def _gdn_chunk_fwd(qn, kn, v, gcb, bb, *, name):
    S = qn.shape[0]

    def body(qn_ref, kn_ref, v_ref, gcb_ref, bb_ref, uv_ref, wk_ref, at_ref, t_ref, wkb_ref, qdb_ref, keb_ref):
        incl, strict, diag = _stack_masks()
        for c in range(CHUNKS_PER_STEP):
            rows = slice(CHUNK * c, CHUNK * (c + 1))
            srows = slice(GDN_STACK * c, GDN_STACK * (c + 1))
            q, k, vv, gs, bs = [_stack(r, rows) for r in (qn_ref, kn_ref, v_ref, gcb_ref, bb_ref)]
            dec, bt = _stack_decay(gs, bs, incl)
            p = -jnp.where(strict, dec * _bdot_nt(k, k) * bt, 0.0)
            t = jnp.where(diag, 1.0, 0.0) + p
            for _ in range(5):
                p = _bdot(p, p)
                t = t + _bdot(t, p)
            sol = _dot3(t, jnp.concatenate([vv, jnp.exp(gs) * k], axis=1), ((1,), (0,)))
            _unstack_to(uv_ref, rows, sol[:, :GDN_DIM])
            _unstack_to(wk_ref, rows, sol[:, GDN_DIM:])
            at_ref[srows, :] = dec * _bdot_nt(q, k) * bt
            t_ref[srows, :] = t
            wkb_ref[srows, :] = _head_spread(sol[:, GDN_DIM:]).astype(BF16)
            qdb_ref[srows, :] = _head_spread(q * jnp.exp(gs)).astype(BF16)
            keb_ref[srows, :] = _head_spread(k * jnp.exp(_last_rows(gs, CHUNK) - gs) * bs).astype(BF16)

    step = CHUNKS_PER_STEP * CHUNK
    row = pl.BlockSpec((step, GDN_WIDTH), lambda n: (n, 0))
    sq = pl.BlockSpec((CHUNKS_PER_STEP * GDN_STACK, GDN_STACK), lambda n: (n, 0))
    wide = pl.BlockSpec((CHUNKS_PER_STEP * GDN_STACK, GDN_WIDTH), lambda n: (n, 0))
    nsq = S // CHUNK * GDN_STACK
    return pl.pallas_call(
        body, name=name, grid=(S // step,), in_specs=[row] * 5, out_specs=[row, row, sq, sq, wide, wide, wide],
        out_shape=[_sds((S, GDN_WIDTH)), _sds((S, GDN_WIDTH)), _sds((nsq, GDN_STACK)), _sds((nsq, GDN_STACK))]
        + [_sds((nsq, GDN_WIDTH), BF16)] * 3,
        compiler_params=_params(("parallel",)),
    )(qn, kn, v, gcb, bb)


SCAN_CHUNKS = 4


def _gdn_scan_fwd(uv, at, wkb, qdb, keb, gcb, proj_z, gnw, *, name):
    S = uv.shape[0]
    nc = S // CHUNK

    def body(uv_ref, at_ref, wkb_ref, qdb_ref, keb_ref, gcb_ref, z_ref, gnw_ref, o_ref, u_ref, sp_ref, oa_ref, st_ref):
        n = pl.program_id(0)

        @pl.when(n == 0)
        def _():
            st_ref[...] = jnp.zeros_like(st_ref)

        for c in range(SCAN_CHUNKS):
            rows = slice(CHUNK * c, CHUNK * (c + 1))
            srows = slice(GDN_STACK * c, GDN_STACK * (c + 1))
            st = st_ref[...]
            sp_ref[GDN_WIDTH * c:GDN_WIDTH * (c + 1), :] = st
            uv, gs, z = [_stack(r, rows) for r in (uv_ref, gcb_ref, z_ref)]
            u = uv - _bdot(wkb_ref[srows, :], st)
            o = _bdot(qdb_ref[srows, :], st) + _bdot(at_ref[srows, :], u)
            st_ref[...] = jnp.exp(_last_rows(gs, GDN_DIM)) * st + _bdot_tn(keb_ref[srows, :], u)
            _unstack_to(u_ref, rows, u)
            _unstack_to(o_ref, rows, o)
            r = lax.rsqrt(jnp.mean(o * o, axis=-1, keepdims=True) + EPS)
            oa = o * r * gnw_ref[...] * (z * _sigmoid(z))
            oa_ref[rows, :] = jnp.concatenate([oa[CHUNK * h:CHUNK * (h + 1)] for h in range(GDN_HEADS)], axis=1).astype(BF16)

    row = pl.BlockSpec((SCAN_CHUNKS * CHUNK, GDN_WIDTH), lambda n: (n, 0))
    sq = pl.BlockSpec((SCAN_CHUNKS * GDN_STACK, GDN_STACK), lambda n: (n, 0))
    wide = pl.BlockSpec((SCAN_CHUNKS * GDN_STACK, GDN_WIDTH), lambda n: (n, 0))
    return pl.pallas_call(
        body, name=name, grid=(nc // SCAN_CHUNKS,),
        in_specs=[row, sq, wide, wide, wide, row, row, pl.BlockSpec((1, GDN_DIM), lambda n: (0, 0))],
        out_specs=[row, row, pl.BlockSpec((SCAN_CHUNKS * GDN_WIDTH, GDN_DIM), lambda n: (n, 0)), row],
        out_shape=[_sds((S, GDN_WIDTH)), _sds((S, GDN_WIDTH)), _sds((nc * GDN_WIDTH, GDN_DIM)), _sds((S, 2 * GDN_WIDTH), BF16)],
        scratch_shapes=[pltpu.VMEM((GDN_WIDTH, GDN_DIM), F32)],
        compiler_params=_params(("arbitrary",)),
    )(uv, at, wkb, qdb, keb, gcb, proj_z, gnw)


def _gdn_scan_bwd(d_oab, o, proj_z, gnw, sp, u, at, wkb, qdb, keb, gcb, *, name):
    S = o.shape[0]
    nc = S // CHUNK
    ns = nc // SCAN_CHUNKS

    def body(do_ref, o_ref, z_ref, gnw_ref, sp_ref, u_ref, at_ref, wkb_ref, qdb_ref, keb_ref, gcb_ref,
             dz_ref, dgn_ref, du_ref, dwk_ref, dat_ref, dqd_ref, dke_ref, dgl_ref, ds_ref):
        n = pl.program_id(0)

        @pl.when(n == 0)
        def _():
            ds_ref[...] = jnp.zeros_like(ds_ref)
            dgn_ref[...] = jnp.zeros_like(dgn_ref)

        gw = gnw_ref[...]
        for c in reversed(range(SCAN_CHUNKS)):
            rows = slice(CHUNK * c, CHUNK * (c + 1))
            srows = slice(GDN_STACK * c, GDN_STACK * (c + 1))
            d_oa, oo, z, uu, gs = [_stack(r, rows) for r in (do_ref, o_ref, z_ref, u_ref, gcb_ref)]
            sg = _sigmoid(z)
            r = lax.rsqrt(jnp.mean(oo * oo, axis=-1, keepdims=True) + EPS)
            xh = oo * r
            dy = d_oa * (z * sg)
            _unstack_to(dz_ref, rows, d_oa * (xh * gw) * _dsilu(z, sg))
            dgn_ref[...] += jnp.sum(dy * xh, axis=0, keepdims=True)
            dxh = dy * gw
            do = r * (dxh - xh * jnp.mean(dxh * xh, axis=-1, keepdims=True))

            st = sp_ref[GDN_WIDTH * c:GDN_WIDTH * (c + 1), :]
            dst = ds_ref[...]
            ge = jnp.exp(_last_rows(gs, GDN_DIM))
            _unstack_to(dqd_ref, rows, _head_diag(_bdot_nt(do, st)))
            dat_ref[srows, :] = _bdot_nt(do, uu)
            du = _bdot_tn(at_ref[srows, :], do) + _bdot(keb_ref[srows, :], dst)
            _unstack_to(dke_ref, rows, _head_diag(_bdot_nt(uu, dst)))
            prod = dst * st
            for h in range(GDN_HEADS):
                blk = prod[GDN_DIM * h:GDN_DIM * (h + 1)]
                dge = jnp.sum(jnp.sum(blk, axis=1, keepdims=True), axis=0, keepdims=True)
                dgl_ref[c, :, GDN_DIM * h:GDN_DIM * (h + 1)] = jnp.broadcast_to(dge * ge[GDN_DIM * h:GDN_DIM * h + 1], (8, GDN_DIM))
            ds_ref[...] = _bdot_tn(qdb_ref[srows, :], do) + ge * dst - _bdot_tn(wkb_ref[srows, :], du)
            _unstack_to(du_ref, rows, du)
            _unstack_to(dwk_ref, rows, -_head_diag(_bdot_nt(du, st)))

    rev = lambda n: (ns - 1 - n, 0)
    row = pl.BlockSpec((SCAN_CHUNKS * CHUNK, GDN_WIDTH), rev)
    sq = pl.BlockSpec((SCAN_CHUNKS * GDN_STACK, GDN_STACK), rev)
    wide = pl.BlockSpec((SCAN_CHUNKS * GDN_STACK, GDN_WIDTH), rev)
    one = pl.BlockSpec((1, GDN_DIM), lambda n: (0, 0))
    return pl.pallas_call(
        body, name=name, grid=(ns,),
        in_specs=[row, row, row, one, pl.BlockSpec((SCAN_CHUNKS * GDN_WIDTH, GDN_DIM), rev), row, sq, wide, wide, wide, row],
        out_specs=[row, one, row, row, sq, row, row, pl.BlockSpec((SCAN_CHUNKS, 8, GDN_WIDTH), lambda n: (ns - 1 - n, 0, 0))],
        out_shape=[_sds((S, GDN_WIDTH), BF16), _sds((1, GDN_DIM)), _sds((S, GDN_WIDTH)), _sds((S, GDN_WIDTH)),
                   _sds((nc * GDN_STACK, GDN_STACK)), _sds((S, GDN_WIDTH)), _sds((S, GDN_WIDTH)), _sds((nc, 8, GDN_WIDTH))],
        scratch_shapes=[pltpu.VMEM((GDN_WIDTH, GDN_DIM), F32)],
        compiler_params=_params(("arbitrary",)),
    )(d_oab, o, proj_z, gnw, sp, u, at, wkb, qdb, keb, gcb)


def _gdn_chunk_bwd(qn, kn, gcb, bb, tmat, uv, wk, du, dwk, dat, dqd, dke, dgl, *, name):
    S = qn.shape[0]

    def body(qn_ref, kn_ref, gcb_ref, bb_ref, t_ref, uv_ref, wk_ref, du_ref, dwk_ref, dat_ref, dqd_ref, dke_ref,
             dgl_ref, dq_ref, dk_ref, dv_ref, dg_ref, dbeta_ref):
        incl, strict, _ = _stack_masks()
        lane = lax.broadcasted_iota(jnp.int32, (CHUNK, 128), 1)
        rowi = lax.broadcasted_iota(jnp.int32, (CHUNK, 1), 0)
        rsum = lambda x: jnp.sum(x, axis=-1, keepdims=True)
        for c in range(CHUNKS_PER_STEP):
            rows = slice(CHUNK * c, CHUNK * (c + 1))
            srows = slice(GDN_STACK * c, GDN_STACK * (c + 1))
            q, k, gs, bs, uv, wk, du, dwk, dqd, dke = [
                _stack(r, rows) for r in (qn_ref, kn_ref, gcb_ref, bb_ref, uv_ref, wk_ref, du_ref, dwk_ref, dqd_ref, dke_ref)]
            dec, bt = _stack_decay(gs, bs, incl)
            kk = _bdot_nt(k, k)
            qk = _bdot_nt(q, k)
            d_rhs = _dot3(t_ref[srows, :], jnp.concatenate([du, dwk], axis=1), ((0,), (0,)))
            sol = jnp.concatenate([uv, wk], axis=1)
            d_l = jnp.where(strict, -_dot3(d_rhs, sol, ((1,), (1,))), 0.0)
            d_a = jnp.where(incl, dat_ref[srows, :], 0.0)
            gam = jnp.exp(gs)
            e = jnp.exp(_last_rows(gs, CHUNK) - gs)
            d_gk = d_rhs[:, GDN_DIM:]
            ml = d_l * dec * bt
            ma = d_a * dec * bt
            _unstack_to(dq_ref, rows, _bdot(ma, k) + dqd * gam)
            _unstack_to(dk_ref, rows, _bdot(ml + ml.T, k) + _bdot_tn(ma, q) + d_gk * gam + dke * (e * bs))
            _unstack_to(dv_ref, rows, d_rhs[:, :GDN_DIM])
            wb = d_l * dec * kk + d_a * dec * qk
            ew = wb * bt
            s_ke = rsum(dke * k * (e * bs))
            dbeta = rsum(wb.T) + rsum(dke * k * e)
            dgc = rsum(ew) - rsum(ew.T) + rsum(dqd * q * gam) + rsum(d_gk * k * gam) - s_ke
            dgc4 = jnp.zeros((CHUNK, 128), F32)
            db4 = jnp.zeros((CHUNK, 128), F32)
            for h in range(GDN_HEADS):
                hr = slice(CHUNK * h, CHUNK * (h + 1))
                tail = jnp.sum(s_ke[hr], axis=0, keepdims=True) + dgl_ref[c, 0:1, GDN_DIM * h:GDN_DIM * h + 1]
                dgc4 = jnp.where(lane == h, dgc[hr] + jnp.where(rowi == CHUNK - 1, tail, 0.0), dgc4)
                db4 = jnp.where(lane == h, dbeta[hr], db4)
            dg_ref[rows, :] = _exact_tri_dot(_chunk_tri(CHUNK, upper=True), dgc4)
            dbeta_ref[rows, :] = db4

    step = CHUNKS_PER_STEP * CHUNK
    row = pl.BlockSpec((step, GDN_WIDTH), lambda n: (n, 0))
    sq = pl.BlockSpec((CHUNKS_PER_STEP * GDN_STACK, GDN_STACK), lambda n: (n, 0))
    col = pl.BlockSpec((step, 128), lambda n: (n, 0))
    return pl.pallas_call(
        body, name=name, grid=(S // step,),
        in_specs=[row] * 4 + [sq, row, row, row, row, sq, row, row,
                              pl.BlockSpec((CHUNKS_PER_STEP, 8, GDN_WIDTH), lambda n: (n, 0, 0))],
        out_specs=[row, row, row, col, col],
        out_shape=[_sds((S, GDN_WIDTH))] * 3 + [_sds((S, 128))] * 2, compiler_params=_params(("parallel",)),
    )(qn, kn, gcb, bb, tmat, uv, wk, du, dwk, dat, dqd, dke, dgl)


def _gdn_prep_bwd(dqn, dkn, dv, dg, dbeta, proj_a, conv_w, a_log, dt_bias, *, name, tm=256):
    S = proj_a.shape[0]
    nblk = S // tm
    W3 = 3 * GDN_WIDTH

    def body(dqn_ref, dkn_ref, dv_ref, dg_ref, dbeta_ref, cur_ref, prev_ref, ba_ref, cw_ref, al_ref, dt_ref,
             dc_ref, dba_ref, sm_ref):
        i = pl.program_id(0)
        prev = jnp.where(i > 0, prev_ref[...], 0.0)
        c = _conv_rows(prev, cur_ref[...], cw_ref[...], GDN_CONV)
        sg = _sigmoid(c)
        a = c * sg
        dsl = _dsilu(c, sg)
        ba = ba_ref[...]
        lane = lax.broadcasted_iota(jnp.int32, (tm, 128), 1)
        lane1 = lax.broadcasted_iota(jnp.int32, (1, 128), 1)
        dba = jnp.zeros((tm, 128), F32)
        sm = jnp.zeros((1, 128), F32)
        for h in range(GDN_HEADS):
            sl = slice(GDN_DIM * h, GDN_DIM * (h + 1))
            ks = slice(GDN_WIDTH + GDN_DIM * h, GDN_WIDTH + GDN_DIM * (h + 1))
            qh, kh = a[:, sl], a[:, ks]
            rq = lax.rsqrt(jnp.sum(qh * qh, axis=-1, keepdims=True) + EPS)
            rk = lax.rsqrt(jnp.sum(kh * kh, axis=-1, keepdims=True) + EPS)
            qhat, khat = qh * rq, kh * rk
            dyq = dqn_ref[:, sl] * (GDN_DIM ** -0.5)
            dyk = dkn_ref[:, sl]
            dq = rq * (dyq - qhat * jnp.sum(dyq * qhat, axis=-1, keepdims=True))
            dk = rk * (dyk - khat * jnp.sum(dyk * khat, axis=-1, keepdims=True))
            dc_ref[:, sl] = dq * dsl[:, sl]
            dc_ref[:, ks] = dk * dsl[:, ks]
            beta = _sigmoid(ba[:, h:h + 1])
            db = dbeta_ref[:, h:h + 1] * beta * (1.0 - beta)
            aneg = -jnp.exp(al_ref[0:1, h:h + 1])
            xa = ba[:, GDN_HEADS + h:GDN_HEADS + h + 1] + dt_ref[0:1, h:h + 1]
            dgh = dg_ref[:, h:h + 1]
            dxa = dgh * aneg * _sigmoid(xa)
            dba = jnp.where(lane == h, db, dba)
            dba = jnp.where(lane == GDN_HEADS + h, dxa, dba)
            d_alog = jnp.sum(dgh * _softplus(xa), axis=0, keepdims=True) * aneg
            sm = jnp.where(lane1 == h, d_alog, sm)
            sm = jnp.where(lane1 == GDN_HEADS + h, jnp.sum(dxa, axis=0, keepdims=True), sm)
        vs = slice(2 * GDN_WIDTH, W3)
        dc_ref[:, vs] = dv_ref[...] * dsl[:, vs]
        dba_ref[...] = dba

        @pl.when(i == 0)
        def _():
            sm_ref[...] = sm

        @pl.when(i > 0)
        def _():
            sm_ref[...] += sm

    prev_spec, _ = _halo_specs(tm, W3, 0, nblk)
    row = pl.BlockSpec((tm, GDN_WIDTH), lambda i: (i, 0))
    col = pl.BlockSpec((tm, 128), lambda i: (i, 0))
    small = lambda a: pl.BlockSpec(a.shape, lambda i: (0, 0))
    return pl.pallas_call(
        body, name=name, grid=(nblk,),
        in_specs=[row, row, row, col, col, pl.BlockSpec((tm, W3), lambda i: (i, 0)), prev_spec,
                  pl.BlockSpec((tm, 128), lambda i: (i, W3 // 128)), small(conv_w), small(a_log), small(dt_bias)],
        out_specs=[pl.BlockSpec((tm, W3), lambda i: (i, 0)), col, pl.BlockSpec((1, 128), lambda i: (0, 0))],
        out_shape=[_sds((S, W3)), _sds((S, 128)), _sds((1, 128))], compiler_params=_params(("arbitrary",)),
    )(dqn, dkn, dv, dg, dbeta, proj_a, proj_a, proj_a, conv_w, a_log, dt_bias)


def _gdn_conv_bwd(dc, dba, proj_a, conv_w, *, name, tm=256):
    S = proj_a.shape[0]
    nblk = S // tm
    W3 = 3 * GDN_WIDTH

    def body(dc_ref, dnext_ref, dba_ref, cur_ref, prev_ref, cw_ref, da_ref, dcw_ref):
        i = pl.program_id(0)
        prev = jnp.where(i > 0, prev_ref[...], 0.0)
        nxt = jnp.where(i < nblk - 1, dnext_ref[...], 0.0)
        dx, dw = _conv_rows_bwd(dc_ref[...], nxt, prev, cur_ref[...], cw_ref[...], GDN_CONV)
        da_ref[:, 0:W3] = dx.astype(BF16)
        da_ref[:, W3:] = dba_ref[...].astype(BF16)

        @pl.when(i == 0)
        def _():
            dcw_ref[...] = dw

        @pl.when(i > 0)
        def _():
            dcw_ref[...] += dw

    prev_spec, next_spec = _halo_specs(tm, W3, 0, nblk)
    wide = pl.BlockSpec((tm, W3), lambda i: (i, 0))
    return pl.pallas_call(
        body, name=name, grid=(nblk,),
        in_specs=[wide, next_spec, pl.BlockSpec((tm, 128), lambda i: (i, 0)), wide, prev_spec,
                  pl.BlockSpec(conv_w.shape, lambda i: (0, 0))],
        out_specs=[pl.BlockSpec((tm, A_COLS), lambda i: (i, 0)), pl.BlockSpec(conv_w.shape, lambda i: (0, 0))],
        out_shape=[_sds((S, A_COLS), BF16), _sds(conv_w.shape)], compiler_params=_params(("arbitrary",)),
    )(dc, dc, dba, proj_a, proj_a, conv_w)


def _band_mask(nk):
    i = lax.broadcasted_iota(jnp.int32, (2 * BAND, nk), 0) & (BAND - 1)
    j = lax.broadcasted_iota(jnp.int32, (2 * BAND, nk), 1)
    if nk == BAND:
        return j <= i
    return (j >= i) & (j <= i + BAND)


def _stack_heads(x, lo):
    return jnp.concatenate([jnp.where(lo, x, 0.0), jnp.where(lo, 0.0, x)], axis=0)


def _stack_cols(x):
    return jnp.concatenate([x[:, 0:1], x[:, DIL_DIM:DIL_DIM + 1]], axis=0)


def _unstack(x, lo):
    return jnp.where(lo, x[0:BAND], x[BAND:2 * BAND])


def _rows(start, size, stride):
    return pl.ds(start, size) if stride == 1 else pl.ds(start, size, stride=stride)


ATTN_LANES = 4


def _attn_blocks(S, visit_many, lanes=ATTN_LANES):
    for d in DILATIONS:
        nb = S // (d * BAND)
        if d == 1:
            half = nb // 2
            visit_many(d, [(0, 0, True), (0, half, False)])

            def pair(n, c):
                visit_many(1, [(0, n, False), (0, n + half, False)])
                return c
            lax.fori_loop(1, half, pair, 0)
        elif nb > 1:
            for r0 in range(0, d, lanes):
                visit_many(d, [(r0 + t, 0, True) for t in range(lanes)])

                def column(n, c, d=d, r0=r0):
                    visit_many(d, [(r0 + t, n, False) for t in range(lanes)])
                    return c
                lax.fori_loop(1, nb, column, 0)
        else:
            def group(g, c, d=d):
                visit_many(d, [(g * lanes + t, 0, True) for t in range(lanes)])
                return c
            lax.fori_loop(0, d // lanes, group, 0)


def _attn_fwd(proj_b, oab, *, name):
    S = proj_b.shape[0]
    scale = DIL_DIM ** -0.5

    def body(q_ref, k_ref, v_ref, oab_in_ref, ob_ref, lse_ref, m_ref, l_ref, acc_ref):
        del oab_in_ref
        lane = lax.broadcasted_iota(jnp.int32, (BAND, 128), 1)
        lo = lane < DIL_DIM
        m_ref[...] = jnp.full_like(m_ref, NEG_BIG)
        l_ref[...] = jnp.zeros_like(l_ref)
        acc_ref[...] = jnp.zeros_like(acc_ref)

        def load(d, r, n, first):
            nk = BAND if first else 2 * BAND
            qrows = _rows(r + n * (BAND * d), BAND, d)
            krows = _rows(r if first else r + (n - 1) * (BAND * d), nk, d)
            return dict(nk=nk, qrows=qrows, q=q_ref[qrows, :] * scale, k=k_ref[krows, :].astype(BF16),
                        v=v_ref[krows, :].astype(BF16), m=m_ref[qrows, :], l=l_ref[qrows, :], acc=acc_ref[qrows, :])

        def compute(b):
            q, k, v = b["q"], b["k"], b["v"]
            s = jnp.where(_band_mask(b["nk"]), _bdot_nt(_stack_heads(q, lo), k), NEG_BIG)
            m_old = _stack_cols(b["m"])
            m_new = jnp.maximum(m_old, jnp.max(s, axis=-1, keepdims=True))
            p = jnp.exp(s - m_new)
            alpha = _unstack(jnp.exp(m_old - m_new), lo)
            l_new = alpha * b["l"] + _unstack(jnp.sum(p, axis=-1, keepdims=True), lo)
            return _unstack(m_new, lo), l_new, alpha * b["acc"] + _unstack(_bdot(p, v), lo)

        def visit_many(d, blocks):
            loaded = [load(d, *blk) for blk in blocks]
            done = [compute(b) for b in loaded]
            for b, (m_new, l_new, acc_new) in zip(loaded, done):
                m_ref[b["qrows"], :] = m_new
                l_ref[b["qrows"], :] = l_new
                acc_ref[b["qrows"], :] = acc_new

        _attn_blocks(S, visit_many)
        ob_ref[...] = (acc_ref[...] / l_ref[...]).astype(BF16)
        lse_ref[...] = m_ref[...] + jnp.log(l_ref[...])

    part = lambda t: pl.BlockSpec((S, 128), lambda p: (0, 3 * p + t))
    return pl.pallas_call(
        body, name=name, grid=(DIL_PAIRS,),
        in_specs=[part(0), part(1), part(2), pl.BlockSpec(memory_space=pl.ANY)],
        out_specs=[pl.BlockSpec((S, 128), lambda p: (0, GDN_WIDTH // 128 + p)), pl.BlockSpec((S, 128), lambda p: (0, p))],
        out_shape=[_sds(oab.shape, BF16), _sds((S, DIL_WIDTH))],
        scratch_shapes=[pltpu.VMEM((S, 128), F32)] * 3, input_output_aliases={3: 0},
        compiler_params=_params(("parallel",)),
    )(proj_b, proj_b, proj_b, oab)


def _attn_bwd(proj_b, oab, d_oab, lse, *, name):
    S = proj_b.shape[0]
    scale = DIL_DIM ** -0.5

    def body(q_ref, k_ref, v_ref, o_ref, do_ref, lse_ref, dqkv_ref, dq_ref, dk_ref, dv_ref, delta_ref):
        lane = lax.broadcasted_iota(jnp.int32, (BAND, 128), 1)
        lo = lane < DIL_DIM
        dq_ref[...] = jnp.zeros_like(dq_ref)
        dk_ref[...] = jnp.zeros_like(dk_ref)
        dv_ref[...] = jnp.zeros_like(dv_ref)
        prod = do_ref[...] * o_ref[...].astype(F32)
        lo_all = lax.broadcasted_iota(jnp.int32, (S, 128), 1) < DIL_DIM
        delta_ref[...] = jnp.where(lo_all, jnp.sum(jnp.where(lo_all, prod, 0.0), axis=-1, keepdims=True),
                                   jnp.sum(jnp.where(lo_all, 0.0, prod), axis=-1, keepdims=True))

        def load(d, r, n, first):
            nk = BAND if first else 2 * BAND
            qrows = _rows(r + n * (BAND * d), BAND, d)
            krows = _rows(r if first else r + (n - 1) * (BAND * d), nk, d)
            return dict(nk=nk, qrows=qrows, krows=krows, q=q_ref[qrows, :] * scale, k=k_ref[krows, :], v=v_ref[krows, :],
                        do=do_ref[qrows, :], delta=delta_ref[qrows, :], lse=lse_ref[qrows, :],
                        dq=dq_ref[qrows, :], dk=dk_ref[krows, :], dv=dv_ref[krows, :])

        def compute(b):
            q, k, v, do = b["q"], b["k"], b["v"], b["do"]
            qs, dos = _stack_heads(q, lo), _stack_heads(do, lo)
            p = jnp.where(_band_mask(b["nk"]), jnp.exp(_bdot_nt(qs, k) - _stack_cols(b["lse"])), 0.0)
            ds = p * (_bdot_nt(dos, v) - _stack_cols(b["delta"]))
            dq = b["dq"] + _unstack(_bdot(ds, k), lo) * scale
            return dq, b["dk"] + _bdot_tn(ds, qs), b["dv"] + _bdot_tn(p, dos)

        def visit_many(d, blocks):
            loaded = [load(d, *blk) for blk in blocks]
            done = [compute(b) for b in loaded]
            for b, (dq, dk, dv) in zip(loaded, done):
                dq_ref[b["qrows"], :] = dq
                dk_ref[b["krows"], :] = dk
                dv_ref[b["krows"], :] = dv

        _attn_blocks(S, visit_many, lanes=2)
        dqkv_ref[:, 0:128] = dq_ref[...].astype(BF16)
        dqkv_ref[:, 128:256] = dk_ref[...].astype(BF16)
        dqkv_ref[:, 256:384] = dv_ref[...].astype(BF16)

    half = lambda p: (0, GDN_WIDTH // 128 + p)
    part = lambda t: pl.BlockSpec((S, 128), lambda p: (0, 3 * p + t))
    return pl.pallas_call(
        body, name=name, grid=(DIL_PAIRS,),
        in_specs=[part(0), part(1), part(2), pl.BlockSpec((S, 128), half), pl.BlockSpec((S, 128), half),
                  pl.BlockSpec((S, 128), lambda p: (0, p))],
        out_specs=pl.BlockSpec((S, 384), lambda p: (0, p)), out_shape=_sds((S, 3 * DIL_WIDTH), BF16),
        scratch_shapes=[pltpu.VMEM((S, 128), F32)] * 4, compiler_params=_params(("parallel",)),
    )(proj_b, proj_b, proj_b, oab, d_oab, lse)


FF_SLAB = 2 * D_FF // N_DEV
FF_PAIRS = N_DEV // 2
ROWS16 = 16


def _taps(w, x, base, n):
    out = _shifted(x, base, n) * w[0:1]
    for t in range(1, FFN_CONV):
        out = out + _shifted(x, base + t, n) * w[t:t + 1]
    return out


def _ffn_fwd(h2, x1, w_up, conv_w, w_down, final_w, tgt, *, name, tm=512):
    S, D = h2.shape
    ni = S // tm
    per = tm // ROWS16

    def body(h_ref, hp_ref, x1_ref, wg_ref, wu_ref, cg_ref, cu_ref, wd_ref, fw_ref, t_ref,
             dx_ref, dxb_ref, dfw_ref, loss_ref, ug_ref, uu_ref, x2_ref):
        i, j = pl.program_id(0), pl.program_id(1)
        hv = jnp.concatenate([hp_ref[...], h_ref[...]], axis=0)
        row = lax.broadcasted_iota(jnp.int32, (tm + ROWS16, 1), 0)
        keep = (i > 0) | (row >= ROWS16)

        def branch(w_ref, c_ref, u_ref):
            u = lax.dot_general(hv, w_ref[...], (((1,), (1,)), ((), ())), preferred_element_type=F32).astype(BF16)
            u_ref[...] = u[ROWS16:]
            return _taps(c_ref[...], jnp.where(keep, u.astype(F32), 0.0), ROWS16 - (FFN_CONV - 1), tm)

        gate = branch(wg_ref, cg_ref, ug_ref)
        up = branch(wu_ref, cu_ref, uu_ref)
        act = (gate * _sigmoid(gate) * up).astype(BF16)
        part = jnp.dot(act, wd_ref[...], preferred_element_type=F32)

        @pl.when(j == 0)
        def _():
            x2_ref[...] = x1_ref[...] + part

        @pl.when((j > 0) & (j < FF_PAIRS - 1))
        def _():
            x2_ref[...] += part

        @pl.when(j == FF_PAIRS - 1)
        def _():
            xv = x2_ref[...] + part
            wv = fw_ref[...]
            r = lax.rsqrt(jnp.mean(xv * xv, axis=-1, keepdims=True) + EPS)
            err = xv * r * wv - t_ref[...]
            lsum = jnp.sum(jnp.sum(err * err, axis=-1, keepdims=True), axis=0, keepdims=True) * (0.5 / D)
            g = err * (1.0 / D)
            xh = xv * r
            gw = g * wv
            dx = r * (gw - xh * jnp.mean(gw * xh, axis=-1, keepdims=True))
            dx_ref[...] = dx
            dxb_ref[...] = dx.astype(BF16)
            dfw = jnp.sum(g * xh, axis=0, keepdims=True)
            lpart = jnp.broadcast_to(lsum, (1, 128))

            @pl.when(i == 0)
            def _():
                dfw_ref[...] = dfw
                loss_ref[...] = lpart

            @pl.when(i > 0)
            def _():
                dfw_ref[...] += dfw
                loss_ref[...] += lpart

    rows = pl.BlockSpec((tm, D), lambda i, j: (i, 0))
    slab = lambda off: pl.BlockSpec((None, FF_SLAB, D), lambda i, j: (j + off, 0, 0))
    cslab = lambda off: pl.BlockSpec((None, FFN_CONV, FF_SLAB), lambda i, j: (j + off, 0, 0))
    uspec = pl.BlockSpec((None, tm, FF_SLAB), lambda i, j: (j, i, 0))
    return pl.pallas_call(
        body, name=name, grid=(ni, FF_PAIRS),
        in_specs=[rows, pl.BlockSpec((ROWS16, D), lambda i, j: (jnp.maximum(i * per - 1, 0), 0)), rows,
                  slab(0), slab(FF_PAIRS), cslab(0), cslab(FF_PAIRS), pl.BlockSpec((FF_SLAB, D), lambda i, j: (j, 0)),
                  pl.BlockSpec((1, D), lambda i, j: (0, 0)), rows],
        out_specs=[rows, rows, pl.BlockSpec((1, D), lambda i, j: (0, 0)), pl.BlockSpec((1, 128), lambda i, j: (0, 0)), uspec, uspec],
        out_shape=[_sds((S, D)), _sds((S, D), BF16), _sds((1, D)), _sds((1, 128)),
                   _sds((FF_PAIRS, S, FF_SLAB), BF16), _sds((FF_PAIRS, S, FF_SLAB), BF16)],
        scratch_shapes=[pltpu.VMEM((tm, D), F32)],
        compiler_params=_params(("arbitrary", "arbitrary")),
    )(h2, h2, x1, w_up, w_up, conv_w, conv_w, w_down, final_w, tgt)


def _ffn_bwd(dx2, h2, ug, uu, conv_w, w_down, *, name, tm=512):
    S, D = h2.shape
    ni = S // tm
    per = tm // ROWS16
    ext = tm + ROWS16

    def body(dx_ref, dxn_ref, h_ref, ug_ref, ugp_ref, ugn_ref, uu_ref, uup_ref, uun_ref, cg_ref, cu_ref, wd_ref,
             dug_ref, duu_ref, gd_ref, gg_ref, gu_ref, dcg_ref, dcu_ref, acc_d, acc_g, acc_u, acc_cg, acc_cu):
        i = pl.program_id(1)

        @pl.when(i == 0)
        def _():
            acc_d[...] = jnp.zeros_like(acc_d)
            acc_g[...] = jnp.zeros_like(acc_g)
            acc_u[...] = jnp.zeros_like(acc_u)
            acc_cg[...] = jnp.zeros_like(acc_cg)
            acc_cu[...] = jnp.zeros_like(acc_cu)

        dx = dx_ref[...]
        dxe = jnp.concatenate([dx, dxn_ref[...]], axis=0)
        row = lax.broadcasted_iota(jnp.int32, (ext, 1), 0)
        live = (i < ni - 1) | (row < tm)
        d_act = jnp.where(live, lax.dot_general(dxe, wd_ref[...], (((1,), (1,)), ((), ())), preferred_element_type=F32), 0.0)
        rowp = lax.broadcasted_iota(jnp.int32, (ext + ROWS16, 1), 0)
        keep = (i > 0) | (rowp >= ROWS16)

        def pre(cur, prev, nxt):
            return jnp.where(keep, jnp.concatenate([prev[...], cur[...], nxt[...]], axis=0).astype(F32), 0.0)

        uge, uue = pre(ug_ref, ugp_ref, ugn_ref), pre(uu_ref, uup_ref, uun_ref)
        cg, cu = cg_ref[...], cu_ref[...]
        base = ROWS16 - (FFN_CONV - 1)
        gate = _taps(cg, uge, base, ext)
        up = _taps(cu, uue, base, ext)
        sg = _sigmoid(gate)
        silu = gate * sg
        dgc = d_act * up * _dsilu(gate, sg)
        duc = d_act * silu

        def conv_t(w, dc):
            out = _shifted(dc, FFN_CONV - 1, tm) * w[0:1]
            for t in range(1, FFN_CONV):
                out = out + _shifted(dc, FFN_CONV - 1 - t, tm) * w[t:t + 1]
            return out.astype(BF16)

        du_g, du_u = conv_t(cg, dgc), conv_t(cu, duc)
        dug_ref[...] = du_g
        duu_ref[...] = du_u
        dcw = lambda dc, xe: jnp.concatenate(
            [jnp.sum(dc[0:tm] * _shifted(xe, base + t, tm), axis=0, keepdims=True) for t in range(FFN_CONV)], axis=0)
        acc_cg[0:FFN_CONV, :] += dcw(dgc, uge)
        acc_cu[0:FFN_CONV, :] += dcw(duc, uue)
        tn = (((0,), (0,)), ((), ()))
        act = (silu[0:tm] * up[0:tm]).astype(BF16)
        acc_d[...] += lax.dot_general(act, dx, tn, preferred_element_type=F32)
        hv = h_ref[...]
        acc_g[...] += lax.dot_general(du_g, hv, tn, preferred_element_type=F32)
        acc_u[...] += lax.dot_general(du_u, hv, tn, preferred_element_type=F32)

        @pl.when(i == ni - 1)
        def _():
            gd_ref[...] = acc_d[...].astype(BF16)
            gg_ref[...] = acc_g[...].astype(BF16)
            gu_ref[...] = acc_u[...].astype(BF16)
            dcg_ref[...] = acc_cg[0:FFN_CONV, :]
            dcu_ref[...] = acc_cu[0:FFN_CONV, :]

    last16 = S // ROWS16 - 1
    rows = pl.BlockSpec((tm, D), lambda j, i: (i, 0))
    rows_next = pl.BlockSpec((ROWS16, D), lambda j, i: (jnp.minimum((i + 1) * per, last16), 0))
    u_cur = pl.BlockSpec((None, tm, FF_SLAB), lambda j, i: (j, i, 0))
    u_prev = pl.BlockSpec((None, ROWS16, FF_SLAB), lambda j, i: (j, jnp.maximum(i * per - 1, 0), 0))
    u_next = pl.BlockSpec((None, ROWS16, FF_SLAB), lambda j, i: (j, jnp.minimum((i + 1) * per, last16), 0))
    cslab = lambda off: pl.BlockSpec((None, FFN_CONV, FF_SLAB), lambda j, i: (j + off, 0, 0))
    wslab = pl.BlockSpec((None, FF_SLAB, D), lambda j, i: (j, 0, 0))
    dslab = pl.BlockSpec((None, FFN_CONV, FF_SLAB), lambda j, i: (j, 0, 0))
    return pl.pallas_call(
        body, name=name, grid=(FF_PAIRS, ni),
        in_specs=[rows, rows_next, rows, u_cur, u_prev, u_next, u_cur, u_prev, u_next, cslab(0), cslab(FF_PAIRS),
                  pl.BlockSpec((FF_SLAB, D), lambda j, i: (j, 0))],
        out_specs=[u_cur, u_cur, pl.BlockSpec((FF_SLAB, D), lambda j, i: (j, 0)), wslab, wslab, dslab, dslab],
        out_shape=[_sds((FF_PAIRS, S, FF_SLAB), BF16), _sds((FF_PAIRS, S, FF_SLAB), BF16), _sds((D_FF, D), BF16),
                   _sds((FF_PAIRS, FF_SLAB, D), BF16), _sds((FF_PAIRS, FF_SLAB, D), BF16),
                   _sds((FF_PAIRS, FFN_CONV, FF_SLAB)), _sds((FF_PAIRS, FFN_CONV, FF_SLAB))],
        scratch_shapes=[pltpu.VMEM((FF_SLAB, D), F32), pltpu.VMEM((FF_SLAB, D), F32), pltpu.VMEM((FF_SLAB, D), F32),
                        pltpu.VMEM((8, FF_SLAB), F32), pltpu.VMEM((8, FF_SLAB), F32)],
        compiler_params=_params(("parallel", "arbitrary")),
    )(dx2, dx2, h2, ug, ug, ug, uu, uu, uu, conv_w, conv_w, w_down)


def _mm_slabs(a, w, w_off, *, name, res=None, norm_bwd=None, tm=1024, tn=1024):
    nk, S, _ = a.shape
    D = w.shape[2]
    has_res = res is not None
    has_norm = norm_bwd is not None
    assert not has_norm or tn == D

    def body(*refs):
        a_ref, w_ref = refs[:2]
        r_ref = refs[2] if has_res else None
        if has_norm:
            x_ref, nw_ref, skip_ref, o_ref, dw_ref, acc_ref = refs[2 + has_res:]
        else:
            o_ref, acc_ref = refs[2 + has_res:]
        i, k = pl.program_id(0), pl.program_id(2)
        part = jnp.dot(a_ref[...], w_ref[...], preferred_element_type=F32)

        @pl.when(k == 0)
        def _():
            acc_ref[...] = part

        @pl.when(k > 0)
        def _():
            acc_ref[...] += part

        @pl.when(k == nk - 1)
        def _():
            r = acc_ref[...] + r_ref[...] if has_res else acc_ref[...]
            if has_norm:
                dx, dw = _rms_bwd_rows(r, x_ref[...], nw_ref[...])
                o_ref[...] = skip_ref[...] + dx

                @pl.when(i == 0)
                def _():
                    dw_ref[...] = dw

                @pl.when(i > 0)
                def _():
                    dw_ref[...] += dw
            else:
                o_ref[...] = r

    o_spec = pl.BlockSpec((tm, tn), lambda i, j, k: (i, j))
    one = pl.BlockSpec((1, tn), lambda i, j, k: (0, 0))
    return pl.pallas_call(
        body, name=name, grid=(S // tm, D // tn, nk),
        in_specs=[pl.BlockSpec((None, tm, FF_SLAB), lambda i, j, k: (k, i, 0)),
                  pl.BlockSpec((None, FF_SLAB, tn), lambda i, j, k: (k + w_off, 0, j))] + [o_spec] * has_res
        + ([o_spec, one, o_spec] if has_norm else []),
        out_specs=[o_spec, one] if has_norm else o_spec, out_shape=[_sds((S, D)), _sds((1, D))] if has_norm else _sds((S, D)),
        scratch_shapes=[pltpu.VMEM((tm, tn), F32)],
        compiler_params=_params(("arbitrary" if has_norm else "parallel", "parallel", "arbitrary")),
    )(*((a, w) + ((res,) if has_res else ()) + (tuple(norm_bwd) if has_norm else ())))


def _local_step(x, tgt, norm1_w, w_a, w_z, w_b, conv_a, a_log, dt_bias, gnw, norm2_w, final_w, late_weights, emit):
    wgrad = functools.partial(_mm, ta=True, out_dtype=BF16)
    h1, proj_a, proj_z, proj_b = _in_proj(x, norm1_w, w_a, w_z, w_b, name="in_proj")
    qn, kn, v, gcb, bb = _gdn_prep_fwd(proj_a, conv_a, a_log, dt_bias, name="gdn_prep_fwd")
    uv, wk, at, tmat, wkb, qdb, keb = _gdn_chunk_fwd(qn, kn, v, gcb, bb, name="gdn_chunk_fwd")
    o, u, sp, oab = _gdn_scan_fwd(uv, at, wkb, qdb, keb, gcb, proj_z, gnw, name="gdn_scan_fwd")
    oab, lse = _attn_fwd(proj_b, oab, name="attn_fwd")
    w_out, w_up, conv_f, w_down = late_weights(oab)
    x1, h2 = _out_proj_norm(oab, w_out, x, norm2_w, name="out_proj")
    dx2, dx2_b, d_final, loss, ug, uu = _ffn_fwd(h2, x1, w_up, conv_f, w_down, final_w, tgt, name="ffn_fwd")
    dug, duu, g_down, g_up_g, g_up_u, dcw_g, dcw_u = _ffn_bwd(dx2_b, h2, ug, uu, conv_f, w_down, name="ffn_bwd")
    token = emit("ffn", w_down=g_down, w_up=jnp.concatenate([g_up_g, g_up_u], axis=0),
                 conv_f=jnp.concatenate([dcw_g, dcw_u], axis=0))
    dh2 = _mm_slabs(dug, w_up, 0, name="ffn_up_dx_gate")
    dx1, d_norm2 = _mm_slabs(duu, w_up, FF_PAIRS, res=dh2, norm_bwd=(x1, _behind(norm2_w, token), dx2), name="ffn_up_dx_up")
    d_oab = _mm(dx1, w_out, tb=True, name="out_proj_dx", tk=1024)
    gnw = _behind(gnw, emit("out", w_out=wgrad(oab, dx1, name="out_proj_dw")))
    dz, d_gnw, du, dwk, dat, dqd, dke, dgl = _gdn_scan_bwd(d_oab, o, proj_z, gnw, sp, u, at, wkb, qdb, keb, gcb, name="gdn_scan_bwd")
    dqn, dkn, dv, dg, dbeta = _gdn_chunk_bwd(qn, kn, gcb, bb, tmat, uv, wk, du, dwk, dat, dqd, dke, dgl, name="gdn_chunk_bwd")
    dc, dba, d_small = _gdn_prep_bwd(dqn, dkn, dv, dg, dbeta, proj_a, conv_a, a_log, dt_bias, name="gdn_prep_bwd")
    d_pa, d_conv_a = _gdn_conv_bwd(dc, dba, proj_a, conv_a, name="gdn_conv_bwd")
    d_pb = _attn_bwd(proj_b, oab, d_oab, lse, name="attn_bwd")
    g_a = wgrad(d_pa, h1, name="proj_a_dw", tm=A_COLS)
    g_z = wgrad(dz, h1, name="proj_z_dw")
    g_b = wgrad(d_pb, h1, name="proj_b_dw", tm=768)
    w_z = _behind(w_z, emit("in", w_a=g_a, w_z=g_z, w_b=g_b, conv_a=d_conv_a))
    dh1 = _mm(dz, w_z, name="proj_z_dx")
    dh1 = _mm(d_pa, w_a, res=dh1, name="proj_a_dx", tk=A_COLS)
    grad_x, d_norm1 = _mm(d_pb, w_b, res=dh1, norm_bwd=(x, norm1_w, dx1), name="proj_b_dx", tn=D_MODEL, tk=1536)
    small = dict(norm1=d_norm1, small=d_small, gnw=d_gnw, norm2=d_norm2, final=d_final)
    return loss, grad_x, small


_O1 = 3 * GDN_WIDTH
_O2 = _O1 + GDN_WIDTH
_O3 = _O2 + 2 * GDN_HEADS


def _split_w_in(w_t):
    d = w_t.shape[1]
    pad = jnp.zeros((A_COLS - _O1 - 2 * GDN_HEADS, d), w_t.dtype)
    w_a = jnp.concatenate([w_t[:_O1], w_t[_O2:_O3], pad], axis=0)
    w_b = w_t[_O3:].reshape(3, DIL_PAIRS, 128, d).transpose(1, 0, 2, 3).reshape(3 * DIL_WIDTH, d)
    return w_a, w_t[_O1:_O2], w_b


def _merge_g_in(g_a, g_z, g_b):
    d = g_a.shape[1]
    g_b = g_b.reshape(DIL_PAIRS, 3, 128, d).transpose(1, 0, 2, 3).reshape(3 * DIL_WIDTH, d)
    return jnp.concatenate([g_a[:_O1], g_z, g_a[_O1:_O1 + 2 * GDN_HEADS], g_b], axis=0)


MESH = pl.DeviceIdType.MESH
ANY = pl.BlockSpec(memory_space=pl.ANY)


def _position():
    return lax.axis_index("x"), lax.axis_index("y"), lax.axis_index("c")


def _slot(p):
    return 4 * p[0] + 2 * p[1] + p[2]


def _all_gather(blocks, *, name):
    n = len(blocks)

    def body(*refs):
        ins, outs = refs[:n], refs[n:2 * n]
        send_sems, recv_sems, local_sems = refs[2 * n:]
        x, y, c = _position()
        me, sibling = (x, y, c), (x, y, 1 - c)
        chips = [(1 - x, y), (x, 1 - y), (1 - x, 1 - y)]

        def copy(a, k, block, to, src=None):
            dst = outs[a].at[_slot(block)]
            return pltpu.make_async_remote_copy(
                src_ref=dst if src is None else src, dst_ref=dst, send_sem=send_sems.at[a, k], recv_sem=recv_sems.at[a, k],
                device_id=to, device_id_type=MESH)

        mine = [pltpu.make_async_copy(ins[a], outs[a].at[_slot(me)], local_sems.at[a]) for a in range(n)]
        for cp in mine:
            cp.start()
        first = []
        for a in range(n):
            first.append(copy(a, 0, me, sibling, src=ins[a]))
            first += [copy(a, 1 + j, me, (*chip, c), src=ins[a]) for j, chip in enumerate(chips)]
        for cp in first:
            cp.start()
        passed = []
        for j, chip in enumerate(chips):
            for a in range(n):
                copy(a, 1 + j, (*chip, c), me).wait_recv()
                fwd = copy(a, 4 + j, (*chip, c), sibling)
                fwd.start()
                passed.append(fwd)
        for a in range(n):
            copy(a, 0, sibling, me).wait_recv()
            for j, chip in enumerate(chips):
                copy(a, 4 + j, (*chip, 1 - c), me).wait_recv()
        for cp in first + passed:
            cp.wait_send()
        for cp in mine:
            cp.wait()

    return pl.pallas_call(
        body, name=name, in_specs=[ANY] * n, out_specs=[ANY] * n,
        out_shape=[_sds((N_DEV,) + b.shape, b.dtype) for b in blocks],
        scratch_shapes=[pltpu.SemaphoreType.DMA((n, 7)), pltpu.SemaphoreType.DMA((n, 7)), pltpu.SemaphoreType.DMA((n,))],
    )(*blocks)


def _gather_direct(block, *, name):
    def body(in_ref, out_ref, send_sems, recv_sems, local_sem):
        x, y, c = _position()
        me = _slot((x, y, c))
        mine = pltpu.make_async_copy(in_ref, out_ref.at[me], local_sem)
        mine.start()
        copies = [pltpu.make_async_remote_copy(
            src_ref=in_ref, dst_ref=out_ref.at[me], send_sem=send_sems.at[k - 1], recv_sem=recv_sems.at[k - 1],
            device_id=_peer_of(k, x, y, c), device_id_type=MESH) for k in range(1, N_DEV)]
        for cp in copies:
            cp.start()
        for cp in copies:
            cp.wait()
        mine.wait()

    return pl.pallas_call(
        body, name=name, in_specs=[pl.BlockSpec(memory_space=pltpu.VMEM)], out_specs=pl.BlockSpec(memory_space=pltpu.VMEM),
        out_shape=_sds((N_DEV,) + block.shape, block.dtype),
        scratch_shapes=[pltpu.SemaphoreType.DMA((N_DEV - 1,)), pltpu.SemaphoreType.DMA((N_DEV - 1,)), pltpu.SemaphoreType.DMA],
    )(block)


HBM = pl.BlockSpec(memory_space=pltpu.HBM)
SEM = pl.BlockSpec(memory_space=pltpu.SEMAPHORE)
EFFECT = pltpu.SideEffectType.DATAFLOW_SIDE_EFFECTING


def _peer_of(k, x, y, c):
    return (1 - x if k & 4 else x, 1 - y if k & 2 else y, 1 - c if k & 1 else c)


def _flight(a, k):
    return a * (N_DEV - 1) + k - 1


def _exchange_start(arrays, *, name, broadcast=False):
    n = len(arrays)

    def body(*refs):
        ins, lands = refs[:n], refs[n:2 * n]
        send_sems, recv_sems = refs[2 * n:2 * n + 2]
        token = refs[-1]
        x, y, c = _position()
        me = _slot((x, y, c))
        for k in range(1, N_DEV):
            peer = _peer_of(k, x, y, c)
            for a in range(n):
                pltpu.make_async_remote_copy(
                    src_ref=ins[a] if broadcast else ins[a].at[_slot(peer)], dst_ref=lands[a].at[me],
                    send_sem=send_sems.at[_flight(a, k)], recv_sem=recv_sems.at[_flight(a, k)],
                    device_id=peer, device_id_type=MESH).start()
        token[...] = jnp.zeros_like(token)

    land_shapes = [((N_DEV,) + s.shape) if broadcast else s.shape for s in arrays]
    lands = [pltpu.with_memory_space_constraint(lax.empty(shp, s.dtype), pltpu.HBM) for shp, s in zip(land_shapes, arrays)]
    srcs = [pltpu.with_memory_space_constraint(s, pltpu.HBM) for s in arrays]
    outs = pl.pallas_call(
        body, name=name, in_specs=[HBM] * (2 * n),
        out_specs=[SEM, SEM] + [HBM] * (2 * n) + [pl.BlockSpec(memory_space=pltpu.VMEM)],
        out_shape=[pltpu.SemaphoreType.DMA((n * (N_DEV - 1),)), pltpu.SemaphoreType.DMA((n * (N_DEV - 1),))]
        + [pltpu.HBM(s.shape, s.dtype) for s in arrays] + [pltpu.HBM(shp, s.dtype) for shp, s in zip(land_shapes, arrays)]
        + [_sds((8, 128))],
        input_output_aliases={i: 2 + i for i in range(2 * n)},
        compiler_params=pltpu.CompilerParams(has_side_effects=EFFECT),
    )(*srcs, *lands)
    return outs[0], outs[1], outs[2:2 + n], outs[2 + n:2 + 2 * n], outs[-1]


def _exchange_wait(send_sems, recv_sems, srcs, lands, after, *, name, broadcast=False):
    n = len(srcs)

    def body(*refs):
        ins, lnd = refs[:n], refs[n:2 * n]
        send_ref, recv_ref = refs[2 * n:2 * n + 2]
        x, y, c = _position()
        for k in range(1, N_DEV):
            for a in range(n):
                cp = pltpu.make_async_remote_copy(
                    src_ref=ins[a] if broadcast else ins[a].at[0], dst_ref=lnd[a].at[0], send_sem=send_ref.at[_flight(a, k)],
                    recv_sem=recv_ref.at[_flight(a, k)], device_id=_peer_of(k, x, y, c), device_id_type=MESH)
                cp.wait_send()
                cp.wait_recv()

    outs = pl.pallas_call(
        body, name=name, in_specs=[HBM] * (2 * n) + [SEM, SEM, ANY], out_specs=[HBM] * (2 * n),
        out_shape=[pltpu.HBM(s.shape, s.dtype) for s in srcs] + [pltpu.HBM(s.shape, s.dtype) for s in lands],
        input_output_aliases={i: i for i in range(2 * n)},
        compiler_params=pltpu.CompilerParams(has_side_effects=EFFECT),
    )(*srcs, *lands, send_sems, recv_sems, after)
    return outs[:n], outs[n:]


def _with_own(landed, srcs, me, broadcast=False):
    own = srcs if broadcast else [lax.dynamic_index_in_dim(s, me, 0, keepdims=False) for s in srcs]
    return [lax.dynamic_update_index_in_dim(l, o, me, 0) for l, o in zip(landed, own)]


def _behind(x, token):
    return x if token is None else x + token[0, 0].astype(x.dtype)


def _adamw(parts, w, m, v, *, name, tr=None, tc=None):
    R, C = w.shape
    tr = R if tr is None else tr
    tc = C if tc is None else tc
    assert R % tr == 0 and C % tc == 0
    c1 = 1.0 - ADAM_B1 ** ADAM_STEP
    c2 = 1.0 - ADAM_B2 ** ADAM_STEP

    def body(p_ref, w_ref, m_ref, v_ref, g_ref, d_ref, nm_ref, nv_ref):
        g = p_ref[0].astype(F32)
        for s in range(1, N_DEV):
            g = g + p_ref[s].astype(F32)
        nm = ADAM_B1 * m_ref[...] + (1.0 - ADAM_B1) * g
        nv = ADAM_B2 * v_ref[...] + (1.0 - ADAM_B2) * (g * g)
        g_ref[...] = g
        nm_ref[...] = nm
        nv_ref[...] = nv
        d_ref[...] = -ADAM_LR * ((nm / c1) / (jnp.sqrt(nv / c2) + ADAM_EPS) + ADAM_WD * w_ref[...])

    blk = pl.BlockSpec((tr, tc), lambda i, j: (i, j))
    return pl.pallas_call(
        body, name=name, grid=(R // tr, C // tc),
        in_specs=[pl.BlockSpec((N_DEV, tr, tc), lambda i, j: (0, i, j)), blk, blk, blk],
        out_specs=[blk] * 4, out_shape=[_sds((R, C))] * 4, compiler_params=_params(("parallel", "parallel")),
    )(parts, w, m, v)


_SMALL_ROWS = 8


def _pack_small(norm1, norm2, final, gnw, a_log, dt_bias, loss=None):
    loss = jnp.zeros((1, 128), F32) if loss is None else loss
    row3 = jnp.concatenate([gnw, a_log, dt_bias, jnp.zeros((1, 128 - 2 * GDN_HEADS), F32), loss,
                            jnp.zeros((1, D_MODEL - 3 * 128), F32)], axis=1)
    return jnp.concatenate([norm1, norm2, final, row3, jnp.zeros((_SMALL_ROWS - 4, D_MODEL), F32)], axis=0)


def _unpack_small(p):
    return (p[0:1], p[1:2], p[2], p[3:4, 0:128], p[3:4, 128:128 + GDN_HEADS], p[3:4, 128 + GDN_HEADS:128 + 2 * GDN_HEADS])


def _slabs_by_cols(g):
    r = g.shape[0]
    return g.reshape(r, N_DEV, -1).transpose(1, 0, 2)


def _cols_from_slabs(s):
    return s.transpose(1, 0, 2).reshape(s.shape[1], -1)


def kernel(x, norm1_w, w_in, conv_qkv_w, a_log, dt_bias, gdn_norm_w, w_out, norm2_w, w_up, ffn_conv_w, w_down, final_norm_w, loss_target, m_norm1_w, m_w_in, m_conv_qkv_w, m_a_log, m_dt_bias, m_gdn_norm_w, m_w_out, m_norm2_w, m_w_up, m_ffn_conv_w, m_w_down, m_final_norm_w, v_norm1_w, v_w_in, v_conv_qkv_w, v_a_log, v_dt_bias, v_gdn_norm_w, v_w_out, v_norm2_w, v_w_up, v_ffn_conv_w, v_w_down, v_final_norm_w):
    bf = lambda a: a.astype(BF16)
    me = _slot(_position())
    t_in = lambda a: a[0].T
    gw_in, g_conv_a = _all_gather([bf(t_in(w_in)), conv_qkv_w[0]], name="gather_w_in")
    w_a, w_z, w_b = _split_w_in(gw_in.reshape(-1, D_MODEL))
    late_src, _ = lax.optimization_barrier(([bf(w_out[0]), bf(t_in(w_up)), bf(w_down[0]), ffn_conv_w[0]], gw_in))
    l_send, l_recv, l_srcs, l_lands, l_token = _exchange_start(late_src, name="weights_start", broadcast=True)

    def late_weights(after):
        srcs, landed = _exchange_wait(l_send, l_recv, l_srcs, l_lands, after, name="weights_wait", broadcast=True)
        gw_out, gw_up, gw_down, g_conv_f = _with_own(landed, srcs, me, broadcast=True)
        return gw_out.reshape(D_MODEL, D_MODEL), gw_up, g_conv_f, gw_down.reshape(D_FF, D_MODEL)

    flights = {}

    def emit(group, **grads):
        if group == "in":
            slabs = dict(w_in=_merge_g_in(grads["w_a"], grads["w_z"], grads["w_b"]).reshape(N_DEV, -1, D_MODEL),
                         conv_a=_slabs_by_cols(grads["conv_a"]))
        elif group == "ffn":
            slabs = dict(w_down=grads["w_down"].reshape(N_DEV, -1, D_MODEL), w_up=grads["w_up"], conv_f=grads["conv_f"])
        else:
            slabs = {k: v.reshape(N_DEV, -1, D_MODEL) for k, v in grads.items()}
        names = list(slabs)
        *flight, token = _exchange_start([slabs[k] for k in names], name="grads_start_" + group)
        flights[group] = (names, flight)
        return token

    loss, grad_x, g = _local_step(
        x[0], loss_target[0], _behind(norm1_w, l_token), w_a, w_z, w_b, _cols_from_slabs(g_conv_a), a_log, dt_bias,
        gdn_norm_w, norm2_w, final_norm_w[None], late_weights, emit)
    small_all = _gather_direct(
        _pack_small(g["norm1"], g["norm2"], g["final"], g["gnw"], g["small"][:, 0:GDN_HEADS],
                    g["small"][:, GDN_HEADS:2 * GDN_HEADS], loss), name="gather_small")
    got = {}
    for group in ("ffn", "out", "in"):
        names, (send_sems, recv_sems, srcs, lands) = flights[group]
        srcs, landed = _exchange_wait(send_sems, recv_sems, srcs, lands, small_all, name="grads_wait_" + group)
        got.update(zip(names, _with_own(landed, srcs, me)))
    o_in = [o.T for o in _adamw(got["w_in"], t_in(w_in), t_in(m_w_in), t_in(v_w_in), name="adamw_w_in", tc=256)]
    o_out = _adamw(got["w_out"], w_out[0], m_w_out[0], v_w_out[0], name="adamw_w_out")
    o_up = [o.T for o in _adamw(got["w_up"], t_in(w_up), t_in(m_w_up), t_in(v_w_up), name="adamw_w_up", tr=176)]
    o_down = _adamw(got["w_down"], w_down[0], m_w_down[0], v_w_down[0], name="adamw_w_down", tr=176)
    o_ca = _adamw(got["conv_a"], conv_qkv_w[0], m_conv_qkv_w[0], v_conv_qkv_w[0], name="adamw_conv_a")
    o_cf = _adamw(got["conv_f"], ffn_conv_w[0], m_ffn_conv_w[0], v_ffn_conv_w[0], name="adamw_conv_f")
    o_small = _adamw(
        small_all, _pack_small(norm1_w, norm2_w, final_norm_w[None], gdn_norm_w, a_log, dt_bias),
        _pack_small(m_norm1_w, m_norm2_w, m_final_norm_w[None], m_gdn_norm_w, m_a_log, m_dt_bias),
        _pack_small(v_norm1_w, v_norm2_w, v_final_norm_w[None], v_gdn_norm_w, v_a_log, v_dt_bias), name="adamw_small")
    total_loss = o_small[0][3, 256]
    outs = [total_loss, grad_x[None]]
    for k in range(4):
        n1, n2, fin, gn, al, dt = _unpack_small(o_small[k])
        outs += [n1, o_in[k][None], o_ca[k][None], al, dt, gn, o_out[k][None], n2, o_up[k][None], o_cf[k][None], o_down[k][None], fin]
    return tuple(outs)
```

```python
import functools

import jax
import jax.numpy as jnp
from jax import lax
from jax.experimental import pallas as pl
from jax.experimental.pallas import tpu as pltpu

F32 = jnp.float32
BF16 = jnp.bfloat16

N_DEV = 8
D_MODEL = 1024
GDN_HEADS = 4
GDN_DIM = 128
GDN_WIDTH = GDN_HEADS * GDN_DIM
GDN_CONV = 4
CHUNK = 64
CHUNKS_PER_STEP = 4
DIL_HEADS = 8
DIL_DIM = 64
DIL_WIDTH = DIL_HEADS * DIL_DIM
DIL_PAIRS = DIL_HEADS // 2
DILATIONS = (1, 4, 16)
BAND = 128
D_FF = 2816
FFN_CONV = 3
EPS = 1e-6
A_COLS = 3 * GDN_WIDTH + 128
HALO = 8

ADAM_LR = 0.001
ADAM_B1 = 0.9
ADAM_B2 = 0.999
ADAM_EPS = 1e-08
ADAM_WD = 0.01
ADAM_STEP = 10

VMEM_LIMIT_BYTES = 56 * 1024 * 1024
NEG_BIG = -1e30


def _params(sem=None):
    return pltpu.CompilerParams(dimension_semantics=sem, vmem_limit_bytes=VMEM_LIMIT_BYTES)


def _sds(shape, dtype=F32):
    return jax.ShapeDtypeStruct(shape, dtype)


def _bdot(a, b):
    return jnp.dot(a.astype(BF16), b.astype(BF16), preferred_element_type=F32)


def _bdot_nt(a, b):
    return lax.dot_general(a.astype(BF16), b.astype(BF16), (((1,), (1,)), ((), ())), preferred_element_type=F32)


def _bdot_tn(a, b):
    return lax.dot_general(a.astype(BF16), b.astype(BF16), (((0,), (0,)), ((), ())), preferred_element_type=F32)


def _split(a):
    hi = a.astype(BF16)
    lo = (a - hi.astype(F32)).astype(BF16)
    return hi, lo


def _dot3(a, b, dims):
    ah, al = _split(a)
    bh, bl = _split(b)
    d = functools.partial(lax.dot_general, dimension_numbers=(dims, ((), ())), preferred_element_type=F32)
    return d(ah, bh) + (d(al, bh) + d(ah, bl))


def _exact_tri_dot(tri, g):
    g1 = g.astype(BF16)
    r1 = g - g1.astype(F32)
    g2 = r1.astype(BF16)
    g3 = (r1 - g2.astype(F32)).astype(BF16)
    t = tri.astype(BF16)
    d = functools.partial(jnp.dot, preferred_element_type=F32)
    return d(t, g1) + (d(t, g2) + d(t, g3))


def _sigmoid(x):
    return 1.0 / (1.0 + jnp.exp(-x))


def _dsilu(x, sg):
    return sg * (1.0 + x * (1.0 - sg))


def _rms_bwd_rows(dh, x, w):
    r = lax.rsqrt(jnp.mean(x * x, axis=-1, keepdims=True) + EPS)
    xh = x * r
    gw = dh * w
    return r * (gw - xh * jnp.mean(gw * xh, axis=-1, keepdims=True)), jnp.sum(dh * xh, axis=0, keepdims=True)


def _mm(a, b, *, name, ta=False, tb=False, res=None, norm_bwd=None, out_dtype=F32, tm=512, tn=512, tk=512):
    if ta:
        K, M = a.shape
    else:
        M, K = a.shape
    if tb:
        N, Kb = b.shape
    else:
        Kb, N = b.shape
    assert K == Kb, (a.shape, b.shape)
    tm, tn, tk = min(tm, M), min(tn, N), min(tk, K)
    assert M % tm == 0 and N % tn == 0 and K % tk == 0, (name, M, N, K, tm, tn, tk)
    nk = K // tk
    dims = (((0 if ta else 1,), (1 if tb else 0,)), ((), ()))
    has_res = res is not None
    has_norm = norm_bwd is not None
    assert not has_norm or tn == N

    def body(*refs):
        a_ref, b_ref = refs[:2]
        r_ref = refs[2] if has_res else None
        if has_norm:
            x_ref, w_ref, skip_ref, o_ref, dw_ref, acc_ref = refs[2 + has_res:]
        else:
            o_ref, acc_ref = refs[2 + has_res:]
        i, k = pl.program_id(0), pl.program_id(2)
        part = lax.dot_general(a_ref[...].astype(BF16), b_ref[...].astype(BF16), dims, preferred_element_type=F32)

        @pl.when(k == 0)
        def _():
            acc_ref[...] = part

        @pl.when(k > 0)
        def _():
            acc_ref[...] += part

        @pl.when(k == nk - 1)
        def _():
            r = acc_ref[...]
            if has_res:
                r = r + r_ref[...]
            if has_norm:
                dx, dw = _rms_bwd_rows(r, x_ref[...], w_ref[...])
                o_ref[...] = skip_ref[...] + dx

                @pl.when(i == 0)
                def _():
                    dw_ref[...] = dw

                @pl.when(i > 0)
                def _():
                    dw_ref[...] += dw
            else:
                o_ref[...] = r.astype(out_dtype)

    a_spec = pl.BlockSpec((tk, tm), lambda i, j, k: (k, i)) if ta else pl.BlockSpec((tm, tk), lambda i, j, k: (i, k))
    b_spec = pl.BlockSpec((tn, tk), lambda i, j, k: (j, k)) if tb else pl.BlockSpec((tk, tn), lambda i, j, k: (k, j))
    o_spec = pl.BlockSpec((tm, tn), lambda i, j, k: (i, j))
    one = pl.BlockSpec((1, tn), lambda i, j, k: (0, 0))
    in_specs = [a_spec, b_spec] + [o_spec] * has_res + ([o_spec, one, o_spec] if has_norm else [])
    args = (a, b) + ((res,) if has_res else ()) + (tuple(norm_bwd) if has_norm else ())
    return pl.pallas_call(
        body, name=name, grid=(M // tm, N // tn, nk), in_specs=in_specs,
        out_specs=[o_spec, one] if has_norm else o_spec,
        out_shape=[_sds((M, N)), _sds((1, N))] if has_norm else _sds((M, N), out_dtype),
        scratch_shapes=[pltpu.VMEM((tm, tn), F32)],
        compiler_params=_params(("arbitrary" if has_norm else "parallel", "parallel", "arbitrary")),
    )(*args)


def _in_proj(x, norm_w, w_a, w_z, w_b, *, name, tm=512):
    S, D = x.shape
    ws = (w_a, w_z, w_b)

    def body(x_ref, nw_ref, wa_ref, wz_ref, wb_ref, h_ref, pa_ref, pz_ref, pb_ref):
        xv = x_ref[...]
        r = lax.rsqrt(jnp.mean(xv * xv, axis=-1, keepdims=True) + EPS)
        h = (xv * r * nw_ref[...]).astype(BF16)
        h_ref[...] = h
        for w_ref, p_ref in ((wa_ref, pa_ref), (wz_ref, pz_ref), (wb_ref, pb_ref)):
            p_ref[...] = lax.dot_general(h, w_ref[...], (((1,), (1,)), ((), ())), preferred_element_type=F32)

    row = lambda n: pl.BlockSpec((tm, n), lambda i: (i, 0))
    full = lambda a: pl.BlockSpec(a.shape, lambda i: (0, 0))
    return pl.pallas_call(
        body, name=name, grid=(S // tm,), in_specs=[row(D), full(norm_w)] + [full(w) for w in ws],
        out_specs=[row(D)] + [row(w.shape[0]) for w in ws],
        out_shape=[_sds((S, D), BF16)] + [_sds((S, w.shape[0])) for w in ws], compiler_params=_params(("parallel",)),
    )(x, norm_w, *ws)


def _out_proj_norm(a, w, x, norm_w, *, name, tm=512):
    S, D = x.shape

    def body(a_ref, w_ref, x_ref, nw_ref, x1_ref, h_ref):
        x1 = x_ref[...] + jnp.dot(a_ref[...], w_ref[...], preferred_element_type=F32)
        x1_ref[...] = x1
        r = lax.rsqrt(jnp.mean(x1 * x1, axis=-1, keepdims=True) + EPS)
        h_ref[...] = (x1 * r * nw_ref[...]).astype(BF16)

    row = pl.BlockSpec((tm, D), lambda i: (i, 0))
    return pl.pallas_call(
        body, name=name, grid=(S // tm,),
        in_specs=[pl.BlockSpec((tm, a.shape[1]), lambda i: (i, 0)), pl.BlockSpec(w.shape, lambda i: (0, 0)), row,
                  pl.BlockSpec((1, D), lambda i: (0, 0))],
        out_specs=[row, row], out_shape=[_sds((S, D)), _sds((S, D), BF16)], compiler_params=_params(("parallel",)),
    )(a, w, x, norm_w)


def _shifted(x, start, n):
    aligned = -(-start // HALO) * HALO
    assert aligned + n <= x.shape[0], (start, n, x.shape)
    return (x if aligned == start else pltpu.roll(x, aligned - start, axis=0))[aligned:aligned + n]


def _conv_rows(prev, cur, w, taps):
    n = cur.shape[0]
    xs = jnp.concatenate([prev, cur], axis=0)
    base = HALO - (taps - 1)
    out = _shifted(xs, base, n) * w[0:1]
    for i in range(1, taps):
        out = out + _shifted(xs, base + i, n) * w[i:i + 1]
    return out


def _conv_rows_bwd(cur_d, next_d, prev_x, cur_x, w, taps):
    n = cur_d.shape[0]
    ds = jnp.concatenate([cur_d, next_d], axis=0)
    dx = _shifted(ds, taps - 1, n) * w[0:1]
    for i in range(1, taps):
        dx = dx + _shifted(ds, taps - 1 - i, n) * w[i:i + 1]
    xs = jnp.concatenate([prev_x, cur_x], axis=0)
    base = HALO - (taps - 1)
    dws = [jnp.sum(cur_d * _shifted(xs, base + i, n), axis=0, keepdims=True) for i in range(taps)]
    return dx, jnp.concatenate(dws, axis=0)


def _halo_specs(tm, width, col, nblk):
    per = tm // HALO
    prev = pl.BlockSpec((HALO, width), lambda i, *_: (jnp.maximum(i * per - 1, 0), col))
    nxt = pl.BlockSpec((HALO, width), lambda i, *_: (jnp.minimum((i + 1) * per, nblk * per - 1), col))
    return prev, nxt


def _softplus(x):
    return jnp.maximum(x, 0.0) + jnp.log1p(jnp.exp(-jnp.abs(x)))


def _chunk_tri(tm, upper=False):
    r = lax.broadcasted_iota(jnp.int32, (tm, tm), 0)
    c = lax.broadcasted_iota(jnp.int32, (tm, tm), 1)
    same = lax.div(r, CHUNK) == lax.div(c, CHUNK)
    order = (c >= r) if upper else (c <= r)
    return jnp.where(same & order, 1.0, 0.0)


def _gdn_prep_fwd(proj_a, conv_w, a_log, dt_bias, *, name, tm=256):
    S = proj_a.shape[0]
    nblk = S // tm
    W3 = 3 * GDN_WIDTH

    def body(cur_ref, prev_ref, ba_ref, cw_ref, al_ref, dt_ref, qn_ref, kn_ref, v_ref, gcb_ref, bb_ref):
        i = pl.program_id(0)
        prev = jnp.where(i > 0, prev_ref[...], 0.0)
        c = _conv_rows(prev, cur_ref[...], cw_ref[...], GDN_CONV)
        a = c * _sigmoid(c)
        ba = ba_ref[...]
        lane = lax.broadcasted_iota(jnp.int32, (tm, 128), 1)
        g4 = jnp.zeros((tm, 128), F32)
        for h in range(GDN_HEADS):
            sl = slice(GDN_DIM * h, GDN_DIM * (h + 1))
            qh = a[:, GDN_DIM * h:GDN_DIM * (h + 1)]
            kh = a[:, GDN_WIDTH + GDN_DIM * h:GDN_WIDTH + GDN_DIM * (h + 1)]
            qn_ref[:, sl] = qh * (lax.rsqrt(jnp.sum(qh * qh, axis=-1, keepdims=True) + EPS) * (GDN_DIM ** -0.5))
            kn_ref[:, sl] = kh * lax.rsqrt(jnp.sum(kh * kh, axis=-1, keepdims=True) + EPS)
            beta = _sigmoid(ba[:, h:h + 1])
            bb_ref[:, sl] = jnp.broadcast_to(beta, (tm, GDN_DIM))
            g = -jnp.exp(al_ref[0:1, h:h + 1]) * _softplus(ba[:, GDN_HEADS + h:GDN_HEADS + h + 1] + dt_ref[0:1, h:h + 1])
            g4 = jnp.where(lane == h, g, g4)
        v_ref[...] = a[:, 2 * GDN_WIDTH:]
        gc = _exact_tri_dot(_chunk_tri(tm), g4)
        for h in range(GDN_HEADS):
            gcb_ref[:, GDN_DIM * h:GDN_DIM * (h + 1)] = jnp.broadcast_to(gc[:, h:h + 1], (tm, GDN_DIM))

    prev_spec, _ = _halo_specs(tm, W3, 0, nblk)
    row = pl.BlockSpec((tm, GDN_WIDTH), lambda i: (i, 0))
    small = lambda a: pl.BlockSpec(a.shape, lambda i: (0, 0))
    return pl.pallas_call(
        body, name=name, grid=(nblk,),
        in_specs=[pl.BlockSpec((tm, W3), lambda i: (i, 0)), prev_spec,
                  pl.BlockSpec((tm, 128), lambda i: (i, W3 // 128)), small(conv_w), small(a_log), small(dt_bias)],
        out_specs=[row] * 5, out_shape=[_sds((S, GDN_WIDTH))] * 5, compiler_params=_params(("parallel",)),
    )(proj_a, proj_a, proj_a, conv_w, a_log, dt_bias)


GDN_STACK = GDN_HEADS * CHUNK


def _stack(ref, rows):
    return jnp.concatenate([ref[rows, GDN_DIM * h:GDN_DIM * (h + 1)] for h in range(GDN_HEADS)], axis=0)


def _unstack_to(ref, rows, x):
    for h in range(GDN_HEADS):
        ref[rows, GDN_DIM * h:GDN_DIM * (h + 1)] = x[CHUNK * h:CHUNK * (h + 1)].astype(ref.dtype)


def _stack_masks():
    r = lax.broadcasted_iota(jnp.int32, (GDN_STACK, GDN_STACK), 0)
    c = lax.broadcasted_iota(jnp.int32, (GDN_STACK, GDN_STACK), 1)
    same = (r & -CHUNK) == (c & -CHUNK)
    return same & (r >= c), same & (r > c), r == c


def _stack_decay(gs, bs, incl):
    g2 = jnp.concatenate([gs, gs], axis=1)
    diff = g2 - g2.T
    dec = jnp.where(incl, jnp.exp(jnp.where(incl, diff, 0.0)), 0.0)
    return dec, jnp.concatenate([bs, bs], axis=1).T


def _head_mask():
    r = lax.broadcasted_iota(jnp.int32, (GDN_STACK, GDN_WIDTH), 0)
    c = lax.broadcasted_iota(jnp.int32, (GDN_STACK, GDN_WIDTH), 1)
    return (r & -CHUNK) * (GDN_DIM // CHUNK) == (c & -GDN_DIM)


def _head_spread(x):
    return jnp.where(_head_mask(), jnp.concatenate([x] * GDN_HEADS, axis=1), 0.0)


def _head_diag(x):
    xm = jnp.where(_head_mask(), x, 0.0)
    out = xm[:, 0:GDN_DIM]
    for h in range(1, GDN_HEADS):
        out = out + xm[:, GDN_DIM * h:GDN_DIM * (h + 1)]
    return out


def _last_rows(gs, n):
    return jnp.concatenate([jnp.broadcast_to(gs[CHUNK * (h + 1) - 1:CHUNK * (h + 1)], (n, GDN_DIM)) for h in range(GDN_HEADS)], axis=0)


def _gdn_chunk_fwd(qn, kn, v, gcb, bb, *, name):
    S = qn.shape[0]

    def body(qn_ref, kn_ref, v_ref, gcb_ref, bb_ref, uv_ref, wk_ref, at_ref, t_ref, wkb_ref, qdb_ref, keb_ref):
        incl, strict, diag = _stack_masks()
        for c in range(CHUNKS_PER_STEP):
            rows = slice(CHUNK * c, CHUNK * (c + 1))
            srows = slice(GDN_STACK * c, GDN_STACK * (c + 1))
            q, k, vv, gs, bs = [_stack(r, rows) for r in (qn_ref, kn_ref, v_ref, gcb_ref, bb_ref)]
            dec, bt = _stack_decay(gs, bs, incl)
            p = -jnp.where(strict, dec * _bdot_nt(k, k) * bt, 0.0)
            t = jnp.where(diag, 1.0, 0.0) + p
            for _ in range(5):
                p = _bdot(p, p)
                t = t + _bdot(t, p)
            sol = _dot3(t, jnp.concatenate([vv, jnp.exp(gs) * k], axis=1), ((1,), (0,)))
            _unstack_to(uv_ref, rows, sol[:, :GDN_DIM])
            _unstack_to(wk_ref, rows, sol[:, GDN_DIM:])
            at_ref[srows, :] = dec * _bdot_nt(q, k) * bt
            t_ref[srows, :] = t
            wkb_ref[srows, :] = _head_spread(sol[:, GDN_DIM:]).astype(BF16)
            qdb_ref[srows, :] = _head_spread(q * jnp.exp(gs)).astype(BF16)
            keb_ref[srows, :] = _head_spread(k * jnp.exp(_last_rows(gs, CHUNK) - gs) * bs).astype(BF16)

    step = CHUNKS_PER_STEP * CHUNK
    row = pl.BlockSpec((step, GDN_WIDTH), lambda n: (n, 0))
    sq = pl.BlockSpec((CHUNKS_PER_STEP * GDN_STACK, GDN_STACK), lambda n: (n, 0))
    wide = pl.BlockSpec((CHUNKS_PER_STEP * GDN_STACK, GDN_WIDTH), lambda n: (n, 0))
    nsq = S // CHUNK * GDN_STACK
    return pl.pallas_call(
        body, name=name, grid=(S // step,), in_specs=[row] * 5, out_specs=[row, row, sq, sq, wide, wide, wide],
        out_shape=[_sds((S, GDN_WIDTH)), _sds((S, GDN_WIDTH)), _sds((nsq, GDN_STACK)), _sds((nsq, GDN_STACK))]
        + [_sds((nsq, GDN_WIDTH), BF16)] * 3,
        compiler_params=_params(("parallel",)),
    )(qn, kn, v, gcb, bb)


SCAN_CHUNKS = 8


def _gdn_scan_fwd(uv, at, wkb, qdb, keb, gcb, proj_z, gnw, *, name):
    S = uv.shape[0]
    nc = S // CHUNK

    def body(uv_ref, at_ref, wkb_ref, qdb_ref, keb_ref, gcb_ref, z_ref, gnw_ref, o_ref, u_ref, sp_ref, oa_ref, st_ref):
        n = pl.program_id(0)

        @pl.when(n == 0)
        def _():
            st_ref[...] = jnp.zeros_like(st_ref)

        for c in range(SCAN_CHUNKS):
            rows = slice(CHUNK * c, CHUNK * (c + 1))
            srows = slice(GDN_STACK * c, GDN_STACK * (c + 1))
            st = st_ref[...]
            sp_ref[GDN_WIDTH * c:GDN_WIDTH * (c + 1), :] = st
            uv, gs, z = [_stack(r, rows) for r in (uv_ref, gcb_ref, z_ref)]
            u = uv - _bdot(wkb_ref[srows, :], st)
            o = _bdot(qdb_ref[srows, :], st) + _bdot(at_ref[srows, :], u)
            st_ref[...] = jnp.exp(_last_rows(gs, GDN_DIM)) * st + _bdot_tn(keb_ref[srows, :], u)
            _unstack_to(u_ref, rows, u)
            _unstack_to(o_ref, rows, o)
            r = lax.rsqrt(jnp.mean(o * o, axis=-1, keepdims=True) + EPS)
            oa = o * r * gnw_ref[...] * (z * _sigmoid(z))
            oa_ref[rows, :] = jnp.concatenate([oa[CHUNK * h:CHUNK * (h + 1)] for h in range(GDN_HEADS)], axis=1).astype(BF16)

    row = pl.BlockSpec((SCAN_CHUNKS * CHUNK, GDN_WIDTH), lambda n: (n, 0))
    sq = pl.BlockSpec((SCAN_CHUNKS * GDN_STACK, GDN_STACK), lambda n: (n, 0))
    wide = pl.BlockSpec((SCAN_CHUNKS * GDN_STACK, GDN_WIDTH), lambda n: (n, 0))
    return pl.pallas_call(
        body, name=name, grid=(nc // SCAN_CHUNKS,),
        in_specs=[row, sq, wide, wide, wide, row, row, pl.BlockSpec((1, GDN_DIM), lambda n: (0, 0))],
        out_specs=[row, row, pl.BlockSpec((SCAN_CHUNKS * GDN_WIDTH, GDN_DIM), lambda n: (n, 0)), row],
        out_shape=[_sds((S, GDN_WIDTH)), _sds((S, GDN_WIDTH)), _sds((nc * GDN_WIDTH, GDN_DIM)), _sds((S, 2 * GDN_WIDTH), BF16)],
        scratch_shapes=[pltpu.VMEM((GDN_WIDTH, GDN_DIM), F32)],
        compiler_params=_params(("arbitrary",)),
    )(uv, at, wkb, qdb, keb, gcb, proj_z, gnw)


def _gdn_scan_bwd(d_oab, o, proj_z, gnw, sp, u, at, wkb, qdb, keb, gcb, *, name):
    S = o.shape[0]
    nc = S // CHUNK
    ns = nc // SCAN_CHUNKS

    def body(do_ref, o_ref, z_ref, gnw_ref, sp_ref, u_ref, at_ref, wkb_ref, qdb_ref, keb_ref, gcb_ref,
             dz_ref, dgn_ref, du_ref, dwk_ref, dat_ref, dqd_ref, dke_ref, dgl_ref, ds_ref):
        n = pl.program_id(0)

        @pl.when(n == 0)
        def _():
            ds_ref[...] = jnp.zeros_like(ds_ref)
            dgn_ref[...] = jnp.zeros_like(dgn_ref)

        gw = gnw_ref[...]
        for c in reversed(range(SCAN_CHUNKS)):
            rows = slice(CHUNK * c, CHUNK * (c + 1))
            srows = slice(GDN_STACK * c, GDN_STACK * (c + 1))
            d_oa, oo, z, uu, gs = [_stack(r, rows) for r in (do_ref, o_ref, z_ref, u_ref, gcb_ref)]
            sg = _sigmoid(z)
            r = lax.rsqrt(jnp.mean(oo * oo, axis=-1, keepdims=True) + EPS)
            xh = oo * r
            dy = d_oa * (z * sg)
            _unstack_to(dz_ref, rows, d_oa * (xh * gw) * _dsilu(z, sg))
            dgn_ref[...] += jnp.sum(dy * xh, axis=0, keepdims=True)
            dxh = dy * gw
            do = r * (dxh - xh * jnp.mean(dxh * xh, axis=-1, keepdims=True))

            st = sp_ref[GDN_WIDTH * c:GDN_WIDTH * (c + 1), :]
            dst = ds_ref[...]
            ge = jnp.exp(_last_rows(gs, GDN_DIM))
            _unstack_to(dqd_ref, rows, _head_diag(_bdot_nt(do, st)))
            dat_ref[srows, :] = _bdot_nt(do, uu)
            du = _bdot_tn(at_ref[srows, :], do) + _bdot(keb_ref[srows, :], dst)
            _unstack_to(dke_ref, rows, _head_diag(_bdot_nt(uu, dst)))
            prod = dst * st
            for h in range(GDN_HEADS):
                blk = prod[GDN_DIM * h:GDN_DIM * (h + 1)]
                dge = jnp.sum(jnp.sum(blk, axis=1, keepdims=True), axis=0, keepdims=True)
                dgl_ref[c, :, GDN_DIM * h:GDN_DIM * (h + 1)] = jnp.broadcast_to(dge * ge[GDN_DIM * h:GDN_DIM * h + 1], (8, GDN_DIM))
            ds_ref[...] = _bdot_tn(qdb_ref[srows, :], do) + ge * dst - _bdot_tn(wkb_ref[srows, :], du)
            _unstack_to(du_ref, rows, du)
            _unstack_to(dwk_ref, rows, -_head_diag(_bdot_nt(du, st)))

    rev = lambda n: (ns - 1 - n, 0)
    row = pl.BlockSpec((SCAN_CHUNKS * CHUNK, GDN_WIDTH), rev)
    sq = pl.BlockSpec((SCAN_CHUNKS * GDN_STACK, GDN_STACK), rev)
    wide = pl.BlockSpec((SCAN_CHUNKS * GDN_STACK, GDN_WIDTH), rev)
    one = pl.BlockSpec((1, GDN_DIM), lambda n: (0, 0))
    return pl.pallas_call(
        body, name=name, grid=(ns,),
        in_specs=[row, row, row, one, pl.BlockSpec((SCAN_CHUNKS * GDN_WIDTH, GDN_DIM), rev), row, sq, wide, wide, wide, row],
        out_specs=[row, one, row, row, sq, row, row, pl.BlockSpec((SCAN_CHUNKS, 8, GDN_WIDTH), lambda n: (ns - 1 - n, 0, 0))],
        out_shape=[_sds((S, GDN_WIDTH), BF16), _sds((1, GDN_DIM)), _sds((S, GDN_WIDTH)), _sds((S, GDN_WIDTH)),
                   _sds((nc * GDN_STACK, GDN_STACK)), _sds((S, GDN_WIDTH)), _sds((S, GDN_WIDTH)), _sds((nc, 8, GDN_WIDTH))],
        scratch_shapes=[pltpu.VMEM((GDN_WIDTH, GDN_DIM), F32)],
        compiler_params=_params(("arbitrary",)),
    )(d_oab, o, proj_z, gnw, sp, u, at, wkb, qdb, keb, gcb)


def _gdn_chunk_bwd(qn, kn, gcb, bb, tmat, uv, wk, du, dwk, dat, dqd, dke, dgl, *, name):
    S = qn.shape[0]

    def body(qn_ref, kn_ref, gcb_ref, bb_ref, t_ref, uv_ref, wk_ref, du_ref, dwk_ref, dat_ref, dqd_ref, dke_ref,
             dgl_ref, dq_ref, dk_ref, dv_ref, dg_ref, dbeta_ref):
        incl, strict, _ = _stack_masks()
        lane = lax.broadcasted_iota(jnp.int32, (CHUNK, 128), 1)
        rowi = lax.broadcasted_iota(jnp.int32, (CHUNK, 1), 0)
        rsum = lambda x: jnp.sum(x, axis=-1, keepdims=True)
        for c in range(CHUNKS_PER_STEP):
            rows = slice(CHUNK * c, CHUNK * (c + 1))
            srows = slice(GDN_STACK * c, GDN_STACK * (c + 1))
            q, k, gs, bs, uv, wk, du, dwk, dqd, dke = [
                _stack(r, rows) for r in (qn_ref, kn_ref, gcb_ref, bb_ref, uv_ref, wk_ref, du_ref, dwk_ref, dqd_ref, dke_ref)]
            dec, bt = _stack_decay(gs, bs, incl)
            kk = _bdot_nt(k, k)
            qk = _bdot_nt(q, k)
            d_rhs = _dot3(t_ref[srows, :], jnp.concatenate([du, dwk], axis=1), ((0,), (0,)))
            sol = jnp.concatenate([uv, wk], axis=1)
            d_l = jnp.where(strict, -_dot3(d_rhs, sol, ((1,), (1,))), 0.0)
            d_a = jnp.where(incl, dat_ref[srows, :], 0.0)
            gam = jnp.exp(gs)
            e = jnp.exp(_last_rows(gs, CHUNK) - gs)
            d_gk = d_rhs[:, GDN_DIM:]
            ml = d_l * dec * bt
            ma = d_a * dec * bt
            _unstack_to(dq_ref, rows, _bdot(ma, k) + dqd * gam)
            _unstack_to(dk_ref, rows, _bdot(ml + ml.T, k) + _bdot_tn(ma, q) + d_gk * gam + dke * (e * bs))
            _unstack_to(dv_ref, rows, d_rhs[:, :GDN_DIM])
            wb = d_l * dec * kk + d_a * dec * qk
            ew = wb * bt
            s_ke = rsum(dke * k * (e * bs))
            dbeta = rsum(wb.T) + rsum(dke * k * e)
            dgc = rsum(ew) - rsum(ew.T) + rsum(dqd * q * gam) + rsum(d_gk * k * gam) - s_ke
            dgc4 = jnp.zeros((CHUNK, 128), F32)
            db4 = jnp.zeros((CHUNK, 128), F32)
            for h in range(GDN_HEADS):
                hr = slice(CHUNK * h, CHUNK * (h + 1))
                tail = jnp.sum(s_ke[hr], axis=0, keepdims=True) + dgl_ref[c, 0:1, GDN_DIM * h:GDN_DIM * h + 1]
                dgc4 = jnp.where(lane == h, dgc[hr] + jnp.where(rowi == CHUNK - 1, tail, 0.0), dgc4)
                db4 = jnp.where(lane == h, dbeta[hr], db4)
            dg_ref[rows, :] = _exact_tri_dot(_chunk_tri(CHUNK, upper=True), dgc4)
            dbeta_ref[rows, :] = db4

    step = CHUNKS_PER_STEP * CHUNK
    row = pl.BlockSpec((step, GDN_WIDTH), lambda n: (n, 0))
    sq = pl.BlockSpec((CHUNKS_PER_STEP * GDN_STACK, GDN_STACK), lambda n: (n, 0))
    col = pl.BlockSpec((step, 128), lambda n: (n, 0))
    return pl.pallas_call(
        body, name=name, grid=(S // step,),
        in_specs=[row] * 4 + [sq, row, row, row, row, sq, row, row,
                              pl.BlockSpec((CHUNKS_PER_STEP, 8, GDN_WIDTH), lambda n: (n, 0, 0))],
        out_specs=[row, row, row, col, col],
        out_shape=[_sds((S, GDN_WIDTH))] * 3 + [_sds((S, 128))] * 2, compiler_params=_params(("parallel",)),
    )(qn, kn, gcb, bb, tmat, uv, wk, du, dwk, dat, dqd, dke, dgl)


def _gdn_prep_bwd(dqn, dkn, dv, dg, dbeta, proj_a, conv_w, a_log, dt_bias, *, name, tm=256):
    S = proj_a.shape[0]
    nblk = S // tm
    W3 = 3 * GDN_WIDTH

    def body(dqn_ref, dkn_ref, dv_ref, dg_ref, dbeta_ref, cur_ref, prev_ref, ba_ref, cw_ref, al_ref, dt_ref,
             dc_ref, dba_ref, sm_ref):
        i = pl.program_id(0)
        prev = jnp.where(i > 0, prev_ref[...], 0.0)
        c = _conv_rows(prev, cur_ref[...], cw_ref[...], GDN_CONV)
        sg = _sigmoid(c)
        a = c * sg
        dsl = _dsilu(c, sg)
        ba = ba_ref[...]
        lane = lax.broadcasted_iota(jnp.int32, (tm, 128), 1)
        lane1 = lax.broadcasted_iota(jnp.int32, (1, 128), 1)
        dba = jnp.zeros((tm, 128), F32)
        sm = jnp.zeros((1, 128), F32)
        for h in range(GDN_HEADS):
            sl = slice(GDN_DIM * h, GDN_DIM * (h + 1))
            ks = slice(GDN_WIDTH + GDN_DIM * h, GDN_WIDTH + GDN_DIM * (h + 1))
            qh, kh = a[:, sl], a[:, ks]
            rq = lax.rsqrt(jnp.sum(qh * qh, axis=-1, keepdims=True) + EPS)
            rk = lax.rsqrt(jnp.sum(kh * kh, axis=-1, keepdims=True) + EPS)
            qhat, khat = qh * rq, kh * rk
            dyq = dqn_ref[:, sl] * (GDN_DIM ** -0.5)
            dyk = dkn_ref[:, sl]
            dq = rq * (dyq - qhat * jnp.sum(dyq * qhat, axis=-1, keepdims=True))
            dk = rk * (dyk - khat * jnp.sum(dyk * khat, axis=-1, keepdims=True))
            dc_ref[:, sl] = dq * dsl[:, sl]
            dc_ref[:, ks] = dk * dsl[:, ks]
            beta = _sigmoid(ba[:, h:h + 1])
            db = dbeta_ref[:, h:h + 1] * beta * (1.0 - beta)
            aneg = -jnp.exp(al_ref[0:1, h:h + 1])
            xa = ba[:, GDN_HEADS + h:GDN_HEADS + h + 1] + dt_ref[0:1, h:h + 1]
            dgh = dg_ref[:, h:h + 1]
            dxa = dgh * aneg * _sigmoid(xa)
            dba = jnp.where(lane == h, db, dba)
            dba = jnp.where(lane == GDN_HEADS + h, dxa, dba)
            d_alog = jnp.sum(dgh * _softplus(xa), axis=0, keepdims=True) * aneg
            sm = jnp.where(lane1 == h, d_alog, sm)
            sm = jnp.where(lane1 == GDN_HEADS + h, jnp.sum(dxa, axis=0, keepdims=True), sm)
        vs = slice(2 * GDN_WIDTH, W3)
        dc_ref[:, vs] = dv_ref[...] * dsl[:, vs]
        dba_ref[...] = dba

        @pl.when(i == 0)
        def _():
            sm_ref[...] = sm

        @pl.when(i > 0)
        def _():
            sm_ref[...] += sm

    prev_spec, _ = _halo_specs(tm, W3, 0, nblk)
    row = pl.BlockSpec((tm, GDN_WIDTH), lambda i: (i, 0))
    col = pl.BlockSpec((tm, 128), lambda i: (i, 0))
    small = lambda a: pl.BlockSpec(a.shape, lambda i: (0, 0))
    return pl.pallas_call(
        body, name=name, grid=(nblk,),
        in_specs=[row, row, row, col, col, pl.BlockSpec((tm, W3), lambda i: (i, 0)), prev_spec,
                  pl.BlockSpec((tm, 128), lambda i: (i, W3 // 128)), small(conv_w), small(a_log), small(dt_bias)],
        out_specs=[pl.BlockSpec((tm, W3), lambda i: (i, 0)), col, pl.BlockSpec((1, 128), lambda i: (0, 0))],
        out_shape=[_sds((S, W3)), _sds((S, 128)), _sds((1, 128))], compiler_params=_params(("arbitrary",)),
    )(dqn, dkn, dv, dg, dbeta, proj_a, proj_a, proj_a, conv_w, a_log, dt_bias)


def _gdn_conv_bwd(dc, dba, proj_a, conv_w, *, name, tm=256):
    S = proj_a.shape[0]
    nblk = S // tm
    W3 = 3 * GDN_WIDTH

    def body(dc_ref, dnext_ref, dba_ref, cur_ref, prev_ref, cw_ref, da_ref, dcw_ref):
        i = pl.program_id(0)
        prev = jnp.where(i > 0, prev_ref[...], 0.0)
        nxt = jnp.where(i < nblk - 1, dnext_ref[...], 0.0)
        dx, dw = _conv_rows_bwd(dc_ref[...], nxt, prev, cur_ref[...], cw_ref[...], GDN_CONV)
        da_ref[:, 0:W3] = dx.astype(BF16)
        da_ref[:, W3:] = dba_ref[...].astype(BF16)

        @pl.when(i == 0)
        def _():
            dcw_ref[...] = dw

        @pl.when(i > 0)
        def _():
            dcw_ref[...] += dw

    prev_spec, next_spec = _halo_specs(tm, W3, 0, nblk)
    wide = pl.BlockSpec((tm, W3), lambda i: (i, 0))
    return pl.pallas_call(
        body, name=name, grid=(nblk,),
        in_specs=[wide, next_spec, pl.BlockSpec((tm, 128), lambda i: (i, 0)), wide, prev_spec,
                  pl.BlockSpec(conv_w.shape, lambda i: (0, 0))],
        out_specs=[pl.BlockSpec((tm, A_COLS), lambda i: (i, 0)), pl.BlockSpec(conv_w.shape, lambda i: (0, 0))],
        out_shape=[_sds((S, A_COLS), BF16), _sds(conv_w.shape)], compiler_params=_params(("arbitrary",)),
    )(dc, dc, dba, proj_a, proj_a, conv_w)


def _band_mask(nk):
    i = lax.broadcasted_iota(jnp.int32, (2 * BAND, nk), 0) & (BAND - 1)
    j = lax.broadcasted_iota(jnp.int32, (2 * BAND, nk), 1)
    if nk == BAND:
        return j <= i
    return (j >= i) & (j <= i + BAND)


def _stack_heads(x, lo):
    return jnp.concatenate([jnp.where(lo, x, 0.0), jnp.where(lo, 0.0, x)], axis=0)


def _stack_cols(x):
    return jnp.concatenate([x[:, 0:1], x[:, DIL_DIM:DIL_DIM + 1]], axis=0)


def _unstack(x, lo):
    return jnp.where(lo, x[0:BAND], x[BAND:2 * BAND])


def _rows(start, size, stride):
    return pl.ds(start, size) if stride == 1 else pl.ds(start, size, stride=stride)


ATTN_LANES = 4


def _attn_blocks(S, visit_many, lanes=ATTN_LANES):
    for d in DILATIONS:
        nb = S // (d * BAND)
        if d == 1:
            half = nb // 2
            visit_many(d, [(0, 0, True), (0, half, False)])

            def pair(n, c):
                visit_many(1, [(0, n, False), (0, n + half, False)])
                return c
            lax.fori_loop(1, half, pair, 0)
        elif nb > 1:
            for r0 in range(0, d, lanes):
                visit_many(d, [(r0 + t, 0, True) for t in range(lanes)])

                def column(n, c, d=d, r0=r0):
                    visit_many(d, [(r0 + t, n, False) for t in range(lanes)])
                    return c
                lax.fori_loop(1, nb, column, 0)
        else:
            def group(g, c, d=d):
                visit_many(d, [(g * lanes + t, 0, True) for t in range(lanes)])
                return c
            lax.fori_loop(0, d // lanes, group, 0)


def _attn_fwd(proj_b, oab, *, name):
    S = proj_b.shape[0]
    scale = DIL_DIM ** -0.5

    def body(q_ref, k_ref, v_ref, oab_in_ref, ob_ref, lse_ref, m_ref, l_ref, acc_ref):
        del oab_in_ref
        lane = lax.broadcasted_iota(jnp.int32, (BAND, 128), 1)
        lo = lane < DIL_DIM
        m_ref[...] = jnp.full_like(m_ref, NEG_BIG)
        l_ref[...] = jnp.zeros_like(l_ref)
        acc_ref[...] = jnp.zeros_like(acc_ref)

        def load(d, r, n, first):
            nk = BAND if first else 2 * BAND
            qrows = _rows(r + n * (BAND * d), BAND, d)
            krows = _rows(r if first else r + (n - 1) * (BAND * d), nk, d)
            return dict(nk=nk, qrows=qrows, q=q_ref[qrows, :] * scale, k=k_ref[krows, :].astype(BF16),
                        v=v_ref[krows, :].astype(BF16), m=m_ref[qrows, :], l=l_ref[qrows, :], acc=acc_ref[qrows, :])

        def compute(b):
            q, k, v = b["q"], b["k"], b["v"]
            s = jnp.where(_band_mask(b["nk"]), _bdot_nt(_stack_heads(q, lo), k), NEG_BIG)
            m_old = _stack_cols(b["m"])
            m_new = jnp.maximum(m_old, jnp.max(s, axis=-1, keepdims=True))
            p = jnp.exp(s - m_new)
            alpha = _unstack(jnp.exp(m_old - m_new), lo)
            l_new = alpha * b["l"] + _unstack(jnp.sum(p, axis=-1, keepdims=True), lo)
            return _unstack(m_new, lo), l_new, alpha * b["acc"] + _unstack(_bdot(p, v), lo)

        def visit_many(d, blocks):
            loaded = [load(d, *blk) for blk in blocks]
            done = [compute(b) for b in loaded]
            for b, (m_new, l_new, acc_new) in zip(loaded, done):
                m_ref[b["qrows"], :] = m_new
                l_ref[b["qrows"], :] = l_new
                acc_ref[b["qrows"], :] = acc_new

        _attn_blocks(S, visit_many)
        ob_ref[...] = (acc_ref[...] / l_ref[...]).astype(BF16)
        lse_ref[...] = m_ref[...] + jnp.log(l_ref[...])

    part = lambda t: pl.BlockSpec((S, 128), lambda p: (0, 3 * p + t))
    return pl.pallas_call(
        body, name=name, grid=(DIL_PAIRS,),
        in_specs=[part(0), part(1), part(2), pl.BlockSpec(memory_space=pl.ANY)],
        out_specs=[pl.BlockSpec((S, 128), lambda p: (0, GDN_WIDTH // 128 + p)), pl.BlockSpec((S, 128), lambda p: (0, p))],
        out_shape=[_sds(oab.shape, BF16), _sds((S, DIL_WIDTH))],
        scratch_shapes=[pltpu.VMEM((S, 128), F32)] * 3, input_output_aliases={3: 0},
        compiler_params=_params(("parallel",)),
    )(proj_b, proj_b, proj_b, oab)


def _attn_bwd(proj_b, oab, d_oab, lse, *, name):
    S = proj_b.shape[0]
    scale = DIL_DIM ** -0.5

    def body(q_ref, k_ref, v_ref, o_ref, do_ref, lse_ref, dqkv_ref, dq_ref, dk_ref, dv_ref, delta_ref):
        lane = lax.broadcasted_iota(jnp.int32, (BAND, 128), 1)
        lo = lane < DIL_DIM
        dq_ref[...] = jnp.zeros_like(dq_ref)
        dk_ref[...] = jnp.zeros_like(dk_ref)
        dv_ref[...] = jnp.zeros_like(dv_ref)
        prod = do_ref[...] * o_ref[...].astype(F32)
        lo_all = lax.broadcasted_iota(jnp.int32, (S, 128), 1) < DIL_DIM
        delta_ref[...] = jnp.where(lo_all, jnp.sum(jnp.where(lo_all, prod, 0.0), axis=-1, keepdims=True),
                                   jnp.sum(jnp.where(lo_all, 0.0, prod), axis=-1, keepdims=True))

        def load(d, r, n, first):
            nk = BAND if first else 2 * BAND
            qrows = _rows(r + n * (BAND * d), BAND, d)
            krows = _rows(r if first else r + (n - 1) * (BAND * d), nk, d)
            return dict(nk=nk, qrows=qrows, krows=krows, q=q_ref[qrows, :] * scale, k=k_ref[krows, :], v=v_ref[krows, :],
                        do=do_ref[qrows, :], delta=delta_ref[qrows, :], lse=lse_ref[qrows, :],
                        dq=dq_ref[qrows, :], dk=dk_ref[krows, :], dv=dv_ref[krows, :])

        def compute(b):
            q, k, v, do = b["q"], b["k"], b["v"], b["do"]
            qs, dos = _stack_heads(q, lo), _stack_heads(do, lo)
            p = jnp.where(_band_mask(b["nk"]), jnp.exp(_bdot_nt(qs, k) - _stack_cols(b["lse"])), 0.0)
            ds = p * (_bdot_nt(dos, v) - _stack_cols(b["delta"]))
            dq = b["dq"] + _unstack(_bdot(ds, k), lo) * scale
            return dq, b["dk"] + _bdot_tn(ds, qs), b["dv"] + _bdot_tn(p, dos)

        def visit_many(d, blocks):
            loaded = [load(d, *blk) for blk in blocks]
            done = [compute(b) for b in loaded]
            for b, (dq, dk, dv) in zip(loaded, done):
                dq_ref[b["qrows"], :] = dq
                dk_ref[b["krows"], :] = dk
                dv_ref[b["krows"], :] = dv

        _attn_blocks(S, visit_many, lanes=2)
        dqkv_ref[:, 0:128] = dq_ref[...].astype(BF16)
        dqkv_ref[:, 128:256] = dk_ref[...].astype(BF16)
        dqkv_ref[:, 256:384] = dv_ref[...].astype(BF16)

    half = lambda p: (0, GDN_WIDTH // 128 + p)
    part = lambda t: pl.BlockSpec((S, 128), lambda p: (0, 3 * p + t))
    return pl.pallas_call(
        body, name=name, grid=(DIL_PAIRS,),
        in_specs=[part(0), part(1), part(2), pl.BlockSpec((S, 128), half), pl.BlockSpec((S, 128), half),
                  pl.BlockSpec((S, 128), lambda p: (0, p))],
        out_specs=pl.BlockSpec((S, 384), lambda p: (0, p)), out_shape=_sds((S, 3 * DIL_WIDTH), BF16),
        scratch_shapes=[pltpu.VMEM((S, 128), F32)] * 4, compiler_params=_params(("parallel",)),
    )(proj_b, proj_b, proj_b, oab, d_oab, lse)


FF_SLAB = 2 * D_FF // N_DEV
FF_PAIRS = N_DEV // 2
ROWS16 = 16


def _taps(w, x, base, n):
    out = _shifted(x, base, n) * w[0:1]
    for t in range(1, FFN_CONV):
        out = out + _shifted(x, base + t, n) * w[t:t + 1]
    return out


def _ffn_fwd(h2, x1, w_up, conv_w, w_down, final_w, tgt, *, name, tm=512):
    S, D = h2.shape
    ni = S // tm
    per = tm // ROWS16

    def body(h_ref, hp_ref, x1_ref, wg_ref, wu_ref, cg_ref, cu_ref, wd_ref, fw_ref, t_ref,
             dx_ref, dxb_ref, dfw_ref, loss_ref, ug_ref, uu_ref, x2_ref):
        i, j = pl.program_id(0), pl.program_id(1)
        hv = jnp.concatenate([hp_ref[...], h_ref[...]], axis=0)
        row = lax.broadcasted_iota(jnp.int32, (tm + ROWS16, 1), 0)
        keep = (i > 0) | (row >= ROWS16)

        def branch(w_ref, c_ref, u_ref):
            u = lax.dot_general(hv, w_ref[...], (((1,), (1,)), ((), ())), preferred_element_type=F32).astype(BF16)
            u_ref[...] = u[ROWS16:]
            return _taps(c_ref[...], jnp.where(keep, u.astype(F32), 0.0), ROWS16 - (FFN_CONV - 1), tm)

        gate = branch(wg_ref, cg_ref, ug_ref)
        up = branch(wu_ref, cu_ref, uu_ref)
        act = (gate * _sigmoid(gate) * up).astype(BF16)
        part = jnp.dot(act, wd_ref[...], preferred_element_type=F32)

        @pl.when(j == 0)
        def _():
            x2_ref[...] = x1_ref[...] + part

        @pl.when((j > 0) & (j < FF_PAIRS - 1))
        def _():
            x2_ref[...] += part

        @pl.when(j == FF_PAIRS - 1)
        def _():
            xv = x2_ref[...] + part
            wv = fw_ref[...]
            r = lax.rsqrt(jnp.mean(xv * xv, axis=-1, keepdims=True) + EPS)
            err = xv * r * wv - t_ref[...]
            lsum = jnp.sum(jnp.sum(err * err, axis=-1, keepdims=True), axis=0, keepdims=True) * (0.5 / D)
            g = err * (1.0 / D)
            xh = xv * r
            gw = g * wv
            dx = r * (gw - xh * jnp.mean(gw * xh, axis=-1, keepdims=True))
            dx_ref[...] = dx
            dxb_ref[...] = dx.astype(BF16)
            dfw = jnp.sum(g * xh, axis=0, keepdims=True)
            lpart = jnp.broadcast_to(lsum, (1, 128))

            @pl.when(i == 0)
            def _():
                dfw_ref[...] = dfw
                loss_ref[...] = lpart

            @pl.when(i > 0)
            def _():
                dfw_ref[...] += dfw
                loss_ref[...] += lpart

    rows = pl.BlockSpec((tm, D), lambda i, j: (i, 0))
    slab = lambda off: pl.BlockSpec((None, FF_SLAB, D), lambda i, j: (j + off, 0, 0))
    cslab = lambda off: pl.BlockSpec((None, FFN_CONV, FF_SLAB), lambda i, j: (j + off, 0, 0))
    uspec = pl.BlockSpec((None, tm, FF_SLAB), lambda i, j: (j, i, 0))
    return pl.pallas_call(
        body, name=name, grid=(ni, FF_PAIRS),
        in_specs=[rows, pl.BlockSpec((ROWS16, D), lambda i, j: (jnp.maximum(i * per - 1, 0), 0)), rows,
                  slab(0), slab(FF_PAIRS), cslab(0), cslab(FF_PAIRS), pl.BlockSpec((FF_SLAB, D), lambda i, j: (j, 0)),
                  pl.BlockSpec((1, D), lambda i, j: (0, 0)), rows],
        out_specs=[rows, rows, pl.BlockSpec((1, D), lambda i, j: (0, 0)), pl.BlockSpec((1, 128), lambda i, j: (0, 0)), uspec, uspec],
        out_shape=[_sds((S, D)), _sds((S, D), BF16), _sds((1, D)), _sds((1, 128)),
                   _sds((FF_PAIRS, S, FF_SLAB), BF16), _sds((FF_PAIRS, S, FF_SLAB), BF16)],
        scratch_shapes=[pltpu.VMEM((tm, D), F32)],
        compiler_params=_params(("arbitrary", "arbitrary")),
    )(h2, h2, x1, w_up, w_up, conv_w, conv_w, w_down, final_w, tgt)


def _ffn_bwd(dx2, h2, ug, uu, conv_w, w_down, *, name, tm=512):
    S, D = h2.shape
    ni = S // tm
    per = tm // ROWS16
    ext = tm + ROWS16

    def body(dx_ref, dxn_ref, h_ref, ug_ref, ugp_ref, ugn_ref, uu_ref, uup_ref, uun_ref, cg_ref, cu_ref, wd_ref,
             dug_ref, duu_ref, gd_ref, gg_ref, gu_ref, dcg_ref, dcu_ref, acc_d, acc_g, acc_u, acc_cg, acc_cu):
        i = pl.program_id(1)

        @pl.when(i == 0)
        def _():
            acc_d[...] = jnp.zeros_like(acc_d)
            acc_g[...] = jnp.zeros_like(acc_g)
            acc_u[...] = jnp.zeros_like(acc_u)
            acc_cg[...] = jnp.zeros_like(acc_cg)
            acc_cu[...] = jnp.zeros_like(acc_cu)

        dx = dx_ref[...]
        dxe = jnp.concatenate([dx, dxn_ref[...]], axis=0)
        row = lax.broadcasted_iota(jnp.int32, (ext, 1), 0)
        live = (i < ni - 1) | (row < tm)
        d_act = jnp.where(live, lax.dot_general(dxe, wd_ref[...], (((1,), (1,)), ((), ())), preferred_element_type=F32), 0.0)
        rowp = lax.broadcasted_iota(jnp.int32, (ext + ROWS16, 1), 0)
        keep = (i > 0) | (rowp >= ROWS16)

        def pre(cur, prev, nxt):
            return jnp.where(keep, jnp.concatenate([prev[...], cur[...], nxt[...]], axis=0).astype(F32), 0.0)

        uge, uue = pre(ug_ref, ugp_ref, ugn_ref), pre(uu_ref, uup_ref, uun_ref)
        cg, cu = cg_ref[...], cu_ref[...]
        base = ROWS16 - (FFN_CONV - 1)
        gate = _taps(cg, uge, base, ext)
        up = _taps(cu, uue, base, ext)
        sg = _sigmoid(gate)
        silu = gate * sg
        dgc = d_act * up * _dsilu(gate, sg)
        duc = d_act * silu

        def conv_t(w, dc):
            out = _shifted(dc, FFN_CONV - 1, tm) * w[0:1]
            for t in range(1, FFN_CONV):
                out = out + _shifted(dc, FFN_CONV - 1 - t, tm) * w[t:t + 1]
            return out.astype(BF16)

        du_g, du_u = conv_t(cg, dgc), conv_t(cu, duc)
        dug_ref[...] = du_g
        duu_ref[...] = du_u
        dcw = lambda dc, xe: jnp.concatenate(
            [jnp.sum(dc[0:tm] * _shifted(xe, base + t, tm), axis=0, keepdims=True) for t in range(FFN_CONV)], axis=0)
        acc_cg[0:FFN_CONV, :] += dcw(dgc, uge)
        acc_cu[0:FFN_CONV, :] += dcw(duc, uue)
        tn = (((0,), (0,)), ((), ()))
        act = (silu[0:tm] * up[0:tm]).astype(BF16)
        acc_d[...] += lax.dot_general(act, dx, tn, preferred_element_type=F32)
        hv = h_ref[...]
        acc_g[...] += lax.dot_general(du_g, hv, tn, preferred_element_type=F32)
        acc_u[...] += lax.dot_general(du_u, hv, tn, preferred_element_type=F32)

        @pl.when(i == ni - 1)
        def _():
            gd_ref[...] = acc_d[...].astype(BF16)
            gg_ref[...] = acc_g[...].astype(BF16)
            gu_ref[...] = acc_u[...].astype(BF16)
            dcg_ref[...] = acc_cg[0:FFN_CONV, :]
            dcu_ref[...] = acc_cu[0:FFN_CONV, :]

    last16 = S // ROWS16 - 1
    rows = pl.BlockSpec((tm, D), lambda j, i: (i, 0))
    rows_next = pl.BlockSpec((ROWS16, D), lambda j, i: (jnp.minimum((i + 1) * per, last16), 0))
    u_cur = pl.BlockSpec((None, tm, FF_SLAB), lambda j, i: (j, i, 0))
    u_prev = pl.BlockSpec((None, ROWS16, FF_SLAB), lambda j, i: (j, jnp.maximum(i * per - 1, 0), 0))
    u_next = pl.BlockSpec((None, ROWS16, FF_SLAB), lambda j, i: (j, jnp.minimum((i + 1) * per, last16), 0))
    cslab = lambda off: pl.BlockSpec((None, FFN_CONV, FF_SLAB), lambda j, i: (j + off, 0, 0))
    wslab = pl.BlockSpec((None, FF_SLAB, D), lambda j, i: (j, 0, 0))
    dslab = pl.BlockSpec((None, FFN_CONV, FF_SLAB), lambda j, i: (j, 0, 0))
    return pl.pallas_call(
        body, name=name, grid=(FF_PAIRS, ni),
        in_specs=[rows, rows_next, rows, u_cur, u_prev, u_next, u_cur, u_prev, u_next, cslab(0), cslab(FF_PAIRS),
                  pl.BlockSpec((FF_SLAB, D), lambda j, i: (j, 0))],
        out_specs=[u_cur, u_cur, pl.BlockSpec((FF_SLAB, D), lambda j, i: (j, 0)), wslab, wslab, dslab, dslab],
        out_shape=[_sds((FF_PAIRS, S, FF_SLAB), BF16), _sds((FF_PAIRS, S, FF_SLAB), BF16), _sds((D_FF, D), BF16),
                   _sds((FF_PAIRS, FF_SLAB, D), BF16), _sds((FF_PAIRS, FF_SLAB, D), BF16),
                   _sds((FF_PAIRS, FFN_CONV, FF_SLAB)), _sds((FF_PAIRS, FFN_CONV, FF_SLAB))],
        scratch_shapes=[pltpu.VMEM((FF_SLAB, D), F32), pltpu.VMEM((FF_SLAB, D), F32), pltpu.VMEM((FF_SLAB, D), F32),
                        pltpu.VMEM((8, FF_SLAB), F32), pltpu.VMEM((8, FF_SLAB), F32)],
        compiler_params=_params(("parallel", "arbitrary")),
    )(dx2, dx2, h2, ug, ug, ug, uu, uu, uu, conv_w, conv_w, w_down)


def _mm_slabs(a, w, w_off, *, name, res=None, norm_bwd=None, tm=1024, tn=1024):
    nk, S, _ = a.shape
    D = w.shape[2]
    has_res = res is not None
    has_norm = norm_bwd is not None
    assert not has_norm or tn == D

    def body(*refs):
        a_ref, w_ref = refs[:2]
        r_ref = refs[2] if has_res else None
        if has_norm:
            x_ref, nw_ref, skip_ref, o_ref, dw_ref, acc_ref = refs[2 + has_res:]
        else:
            o_ref, acc_ref = refs[2 + has_res:]
        i, k = pl.program_id(0), pl.program_id(2)
        part = jnp.dot(a_ref[...], w_ref[...], preferred_element_type=F32)

        @pl.when(k == 0)
        def _():
            acc_ref[...] = part

        @pl.when(k > 0)
        def _():
            acc_ref[...] += part

        @pl.when(k == nk - 1)
        def _():
            r = acc_ref[...] + r_ref[...] if has_res else acc_ref[...]
            if has_norm:
                dx, dw = _rms_bwd_rows(r, x_ref[...], nw_ref[...])
                o_ref[...] = skip_ref[...] + dx

                @pl.when(i == 0)
                def _():
                    dw_ref[...] = dw

                @pl.when(i > 0)
                def _():
                    dw_ref[...] += dw
            else:
                o_ref[...] = r

    o_spec = pl.BlockSpec((tm, tn), lambda i, j, k: (i, j))
    one = pl.BlockSpec((1, tn), lambda i, j, k: (0, 0))
    return pl.pallas_call(
        body, name=name, grid=(S // tm, D // tn, nk),
        in_specs=[pl.BlockSpec((None, tm, FF_SLAB), lambda i, j, k: (k, i, 0)),
                  pl.BlockSpec((None, FF_SLAB, tn), lambda i, j, k: (k + w_off, 0, j))] + [o_spec] * has_res
        + ([o_spec, one, o_spec] if has_norm else []),
        out_specs=[o_spec, one] if has_norm else o_spec, out_shape=[_sds((S, D)), _sds((1, D))] if has_norm else _sds((S, D)),
        scratch_shapes=[pltpu.VMEM((tm, tn), F32)],
        compiler_params=_params(("arbitrary" if has_norm else "parallel", "parallel", "arbitrary")),
    )(*((a, w) + ((res,) if has_res else ()) + (tuple(norm_bwd) if has_norm else ())))


def _local_step(x, tgt, norm1_w, w_a, w_z, w_b, conv_a, a_log, dt_bias, gnw, norm2_w, final_w, late_weights, emit):
    wgrad = functools.partial(_mm, ta=True, out_dtype=BF16)
    h1, proj_a, proj_z, proj_b = _in_proj(x, norm1_w, w_a, w_z, w_b, name="in_proj")
    qn, kn, v, gcb, bb = _gdn_prep_fwd(proj_a, conv_a, a_log, dt_bias, name="gdn_prep_fwd")
    uv, wk, at, tmat, wkb, qdb, keb = _gdn_chunk_fwd(qn, kn, v, gcb, bb, name="gdn_chunk_fwd")
    o, u, sp, oab = _gdn_scan_fwd(uv, at, wkb, qdb, keb, gcb, proj_z, gnw, name="gdn_scan_fwd")
    oab, lse = _attn_fwd(proj_b, oab, name="attn_fwd")
    w_out, w_up, conv_f, w_down = late_weights(oab)
    x1, h2 = _out_proj_norm(oab, w_out, x, norm2_w, name="out_proj")
    dx2, dx2_b, d_final, loss, ug, uu = _ffn_fwd(h2, x1, w_up, conv_f, w_down, final_w, tgt, name="ffn_fwd")
    dug, duu, g_down, g_up_g, g_up_u, dcw_g, dcw_u = _ffn_bwd(dx2_b, h2, ug, uu, conv_f, w_down, name="ffn_bwd")
    token = emit("ffn", w_down=g_down, w_up=jnp.concatenate([g_up_g, g_up_u], axis=0),
                 conv_f=jnp.concatenate([dcw_g, dcw_u], axis=0))
    dh2 = _mm_slabs(dug, w_up, 0, name="ffn_up_dx_gate")
    dx1, d_norm2 = _mm_slabs(duu, w_up, FF_PAIRS, res=dh2, norm_bwd=(x1, _behind(norm2_w, token), dx2), name="ffn_up_dx_up")
    d_oab = _mm(dx1, w_out, tb=True, name="out_proj_dx", tk=1024)
    gnw = _behind(gnw, emit("out", w_out=wgrad(oab, dx1, name="out_proj_dw")))
    dz, d_gnw, du, dwk, dat, dqd, dke, dgl = _gdn_scan_bwd(d_oab, o, proj_z, gnw, sp, u, at, wkb, qdb, keb, gcb, name="gdn_scan_bwd")
    dqn, dkn, dv, dg, dbeta = _gdn_chunk_bwd(qn, kn, gcb, bb, tmat, uv, wk, du, dwk, dat, dqd, dke, dgl, name="gdn_chunk_bwd")
    dc, dba, d_small = _gdn_prep_bwd(dqn, dkn, dv, dg, dbeta, proj_a, conv_a, a_log, dt_bias, name="gdn_prep_bwd")
    d_pa, d_conv_a = _gdn_conv_bwd(dc, dba, proj_a, conv_a, name="gdn_conv_bwd")
    d_pb = _attn_bwd(proj_b, oab, d_oab, lse, name="attn_bwd")
    g_a = wgrad(d_pa, h1, name="proj_a_dw", tm=A_COLS)
    g_z = wgrad(dz, h1, name="proj_z_dw")
    g_b = wgrad(d_pb, h1, name="proj_b_dw", tm=768)
    w_z = _behind(w_z, emit("in", w_a=g_a, w_z=g_z, w_b=g_b, conv_a=d_conv_a))
    dh1 = _mm(dz, w_z, name="proj_z_dx")
    dh1 = _mm(d_pa, w_a, res=dh1, name="proj_a_dx", tk=A_COLS)
    grad_x, d_norm1 = _mm(d_pb, w_b, res=dh1, norm_bwd=(x, norm1_w, dx1), name="proj_b_dx", tn=D_MODEL, tk=1536)
    small = dict(norm1=d_norm1, small=d_small, gnw=d_gnw, norm2=d_norm2, final=d_final)
    return loss, grad_x, small


_O1 = 3 * GDN_WIDTH
_O2 = _O1 + GDN_WIDTH
_O3 = _O2 + 2 * GDN_HEADS


def _split_w_in(w_t):
    d = w_t.shape[1]
    pad = jnp.zeros((A_COLS - _O1 - 2 * GDN_HEADS, d), w_t.dtype)
    w_a = jnp.concatenate([w_t[:_O1], w_t[_O2:_O3], pad], axis=0)
    w_b = w_t[_O3:].reshape(3, DIL_PAIRS, 128, d).transpose(1, 0, 2, 3).reshape(3 * DIL_WIDTH, d)
    return w_a, w_t[_O1:_O2], w_b


def _merge_g_in(g_a, g_z, g_b):
    d = g_a.shape[1]
    g_b = g_b.reshape(DIL_PAIRS, 3, 128, d).transpose(1, 0, 2, 3).reshape(3 * DIL_WIDTH, d)
    return jnp.concatenate([g_a[:_O1], g_z, g_a[_O1:_O1 + 2 * GDN_HEADS], g_b], axis=0)


MESH = pl.DeviceIdType.MESH
ANY = pl.BlockSpec(memory_space=pl.ANY)


def _position():
    return lax.axis_index("x"), lax.axis_index("y"), lax.axis_index("c")


def _slot(p):
    return 4 * p[0] + 2 * p[1] + p[2]


def _all_gather(blocks, *, name):
    n = len(blocks)

    def body(*refs):
        ins, outs = refs[:n], refs[n:2 * n]
        send_sems, recv_sems, local_sems = refs[2 * n:]
        x, y, c = _position()
        me, sibling = (x, y, c), (x, y, 1 - c)
        chips = [(1 - x, y), (x, 1 - y), (1 - x, 1 - y)]

        def copy(a, k, block, to, src=None):
            dst = outs[a].at[_slot(block)]
            return pltpu.make_async_remote_copy(
                src_ref=dst if src is None else src, dst_ref=dst, send_sem=send_sems.at[a, k], recv_sem=recv_sems.at[a, k],
                device_id=to, device_id_type=MESH)

        mine = [pltpu.make_async_copy(ins[a], outs[a].at[_slot(me)], local_sems.at[a]) for a in range(n)]
        for cp in mine:
            cp.start()
        first = []
        for a in range(n):
            first.append(copy(a, 0, me, sibling, src=ins[a]))
            first += [copy(a, 1 + j, me, (*chip, c), src=ins[a]) for j, chip in enumerate(chips)]
        for cp in first:
            cp.start()
        passed = []
        for j, chip in enumerate(chips):
            for a in range(n):
                copy(a, 1 + j, (*chip, c), me).wait_recv()
                fwd = copy(a, 4 + j, (*chip, c), sibling)
                fwd.start()
                passed.append(fwd)
        for a in range(n):
            copy(a, 0, sibling, me).wait_recv()
            for j, chip in enumerate(chips):
                copy(a, 4 + j, (*chip, 1 - c), me).wait_recv()
        for cp in first + passed:
            cp.wait_send()
        for cp in mine:
            cp.wait()

    return pl.pallas_call(
        body, name=name, in_specs=[ANY] * n, out_specs=[ANY] * n,
        out_shape=[_sds((N_DEV,) + b.shape, b.dtype) for b in blocks],
        scratch_shapes=[pltpu.SemaphoreType.DMA((n, 7)), pltpu.SemaphoreType.DMA((n, 7)), pltpu.SemaphoreType.DMA((n,))],
    )(*blocks)


def _gather_direct(block, *, name):
    def body(in_ref, out_ref, send_sems, recv_sems, local_sem):
        x, y, c = _position()
        me = _slot((x, y, c))
        mine = pltpu.make_async_copy(in_ref, out_ref.at[me], local_sem)
        mine.start()
        copies = [pltpu.make_async_remote_copy(
            src_ref=in_ref, dst_ref=out_ref.at[me], send_sem=send_sems.at[k - 1], recv_sem=recv_sems.at[k - 1],
            device_id=_peer_of(k, x, y, c), device_id_type=MESH) for k in range(1, N_DEV)]
        for cp in copies:
            cp.start()
        for cp in copies:
            cp.wait()
        mine.wait()

    return pl.pallas_call(
        body, name=name, in_specs=[pl.BlockSpec(memory_space=pltpu.VMEM)], out_specs=pl.BlockSpec(memory_space=pltpu.VMEM),
        out_shape=_sds((N_DEV,) + block.shape, block.dtype),
        scratch_shapes=[pltpu.SemaphoreType.DMA((N_DEV - 1,)), pltpu.SemaphoreType.DMA((N_DEV - 1,)), pltpu.SemaphoreType.DMA],
    )(block)


HBM = pl.BlockSpec(memory_space=pltpu.HBM)
SEM = pl.BlockSpec(memory_space=pltpu.SEMAPHORE)
EFFECT = pltpu.SideEffectType.DATAFLOW_SIDE_EFFECTING


def _peer_of(k, x, y, c):
    return (1 - x if k & 4 else x, 1 - y if k & 2 else y, 1 - c if k & 1 else c)


def _flight(a, k):
    return a * (N_DEV - 1) + k - 1


def _exchange_start(arrays, *, name, broadcast=False):
    n = len(arrays)

    def body(*refs):
        ins, lands = refs[:n], refs[n:2 * n]
        send_sems, recv_sems = refs[2 * n:2 * n + 2]
        token = refs[-1]
        x, y, c = _position()
        me = _slot((x, y, c))
        for k in range(1, N_DEV):
            peer = _peer_of(k, x, y, c)
            for a in range(n):
                pltpu.make_async_remote_copy(
                    src_ref=ins[a] if broadcast else ins[a].at[_slot(peer)], dst_ref=lands[a].at[me],
                    send_sem=send_sems.at[_flight(a, k)], recv_sem=recv_sems.at[_flight(a, k)],
                    device_id=peer, device_id_type=MESH).start()
        token[...] = jnp.zeros_like(token)

    land_shapes = [((N_DEV,) + s.shape) if broadcast else s.shape for s in arrays]
    lands = [pltpu.with_memory_space_constraint(lax.empty(shp, s.dtype), pltpu.HBM) for shp, s in zip(land_shapes, arrays)]
    srcs = [pltpu.with_memory_space_constraint(s, pltpu.HBM) for s in arrays]
    outs = pl.pallas_call(
        body, name=name, in_specs=[HBM] * (2 * n),
        out_specs=[SEM, SEM] + [HBM] * (2 * n) + [pl.BlockSpec(memory_space=pltpu.VMEM)],
        out_shape=[pltpu.SemaphoreType.DMA((n * (N_DEV - 1),)), pltpu.SemaphoreType.DMA((n * (N_DEV - 1),))]
        + [pltpu.HBM(s.shape, s.dtype) for s in arrays] + [pltpu.HBM(shp, s.dtype) for shp, s in zip(land_shapes, arrays)]
        + [_sds((8, 128))],
        input_output_aliases={i: 2 + i for i in range(2 * n)},
        compiler_params=pltpu.CompilerParams(has_side_effects=EFFECT),
    )(*srcs, *lands)
    return outs[0], outs[1], outs[2:2 + n], outs[2 + n:2 + 2 * n], outs[-1]


def _exchange_wait(send_sems, recv_sems, srcs, lands, after, *, name, broadcast=False):
    n = len(srcs)

    def body(*refs):
        ins, lnd = refs[:n], refs[n:2 * n]
        send_ref, recv_ref = refs[2 * n:2 * n + 2]
        x, y, c = _position()
        for k in range(1, N_DEV):
            for a in range(n):
                cp = pltpu.make_async_remote_copy(
                    src_ref=ins[a] if broadcast else ins[a].at[0], dst_ref=lnd[a].at[0], send_sem=send_ref.at[_flight(a, k)],
                    recv_sem=recv_ref.at[_flight(a, k)], device_id=_peer_of(k, x, y, c), device_id_type=MESH)
                cp.wait_send()
                cp.wait_recv()

    outs = pl.pallas_call(
        body, name=name, in_specs=[HBM] * (2 * n) + [SEM, SEM, ANY], out_specs=[HBM] * (2 * n),
        out_shape=[pltpu.HBM(s.shape, s.dtype) for s in srcs] + [pltpu.HBM(s.shape, s.dtype) for s in lands],
        input_output_aliases={i: i for i in range(2 * n)},
        compiler_params=pltpu.CompilerParams(has_side_effects=EFFECT),
    )(*srcs, *lands, send_sems, recv_sems, after)
    return outs[:n], outs[n:]


def _with_own(landed, srcs, me, broadcast=False):
    own = srcs if broadcast else [lax.dynamic_index_in_dim(s, me, 0, keepdims=False) for s in srcs]
    return [lax.dynamic_update_index_in_dim(l, o, me, 0) for l, o in zip(landed, own)]


def _behind(x, token):
    return x if token is None else x + token[0, 0].astype(x.dtype)


def _adamw(parts, w, m, v, *, name, tr=None, tc=None):
    R, C = w.shape
    tr = R if tr is None else tr
    tc = C if tc is None else tc
    assert R % tr == 0 and C % tc == 0
    c1 = 1.0 - ADAM_B1 ** ADAM_STEP
    c2 = 1.0 - ADAM_B2 ** ADAM_STEP

    def body(p_ref, w_ref, m_ref, v_ref, g_ref, d_ref, nm_ref, nv_ref):
        g = p_ref[0].astype(F32)
        for s in range(1, N_DEV):
            g = g + p_ref[s].astype(F32)
        nm = ADAM_B1 * m_ref[...] + (1.0 - ADAM_B1) * g
        nv = ADAM_B2 * v_ref[...] + (1.0 - ADAM_B2) * (g * g)
        g_ref[...] = g
        nm_ref[...] = nm
        nv_ref[...] = nv
        d_ref[...] = -ADAM_LR * ((nm / c1) / (jnp.sqrt(nv / c2) + ADAM_EPS) + ADAM_WD * w_ref[...])

    blk = pl.BlockSpec((tr, tc), lambda i, j: (i, j))
    return pl.pallas_call(
        body, name=name, grid=(R // tr, C // tc),
        in_specs=[pl.BlockSpec((N_DEV, tr, tc), lambda i, j: (0, i, j)), blk, blk, blk],
        out_specs=[blk] * 4, out_shape=[_sds((R, C))] * 4, compiler_params=_params(("parallel", "parallel")),
    )(parts, w, m, v)


_SMALL_ROWS = 8


def _pack_small(norm1, norm2, final, gnw, a_log, dt_bias, loss=None):
    loss = jnp.zeros((1, 128), F32) if loss is None else loss
    row3 = jnp.concatenate([gnw, a_log, dt_bias, jnp.zeros((1, 128 - 2 * GDN_HEADS), F32), loss,
                            jnp.zeros((1, D_MODEL - 3 * 128), F32)], axis=1)
    return jnp.concatenate([norm1, norm2, final, row3, jnp.zeros((_SMALL_ROWS - 4, D_MODEL), F32)], axis=0)


def _unpack_small(p):
    return (p[0:1], p[1:2], p[2], p[3:4, 0:128], p[3:4, 128:128 + GDN_HEADS], p[3:4, 128 + GDN_HEADS:128 + 2 * GDN_HEADS])


def _slabs_by_cols(g):
    r = g.shape[0]
    return g.reshape(r, N_DEV, -1).transpose(1, 0, 2)


def _cols_from_slabs(s):
    return s.transpose(1, 0, 2).reshape(s.shape[1], -1)


def kernel(x, norm1_w, w_in, conv_qkv_w, a_log, dt_bias, gdn_norm_w, w_out, norm2_w, w_up, ffn_conv_w, w_down, final_norm_w, loss_target, m_norm1_w, m_w_in, m_conv_qkv_w, m_a_log, m_dt_bias, m_gdn_norm_w, m_w_out, m_norm2_w, m_w_up, m_ffn_conv_w, m_w_down, m_final_norm_w, v_norm1_w, v_w_in, v_conv_qkv_w, v_a_log, v_dt_bias, v_gdn_norm_w, v_w_out, v_norm2_w, v_w_up, v_ffn_conv_w, v_w_down, v_final_norm_w):
    bf = lambda a: a.astype(BF16)
    me = _slot(_position())
    t_in = lambda a: a[0].T
    gw_in, g_conv_a = _all_gather([bf(t_in(w_in)), conv_qkv_w[0]], name="gather_w_in")
    w_a, w_z, w_b = _split_w_in(gw_in.reshape(-1, D_MODEL))
    late_src, _ = lax.optimization_barrier(([bf(w_out[0]), bf(t_in(w_up)), bf(w_down[0]), ffn_conv_w[0]], gw_in))
    l_send, l_recv, l_srcs, l_lands, l_token = _exchange_start(late_src, name="weights_start", broadcast=True)

    def late_weights(after):
        srcs, landed = _exchange_wait(l_send, l_recv, l_srcs, l_lands, after, name="weights_wait", broadcast=True)
        gw_out, gw_up, gw_down, g_conv_f = _with_own(landed, srcs, me, broadcast=True)
        return gw_out.reshape(D_MODEL, D_MODEL), gw_up, g_conv_f, gw_down.reshape(D_FF, D_MODEL)

    flights = {}

    def emit(group, **grads):
        if group == "in":
            slabs = dict(w_in=_merge_g_in(grads["w_a"], grads["w_z"], grads["w_b"]).reshape(N_DEV, -1, D_MODEL),
                         conv_a=_slabs_by_cols(grads["conv_a"]))
        elif group == "ffn":
            slabs = dict(w_down=grads["w_down"].reshape(N_DEV, -1, D_MODEL), w_up=grads["w_up"], conv_f=grads["conv_f"])
        else:
            slabs = {k: v.reshape(N_DEV, -1, D_MODEL) for k, v in grads.items()}
        names = list(slabs)
        *flight, token = _exchange_start([slabs[k] for k in names], name="grads_start_" + group)
        flights[group] = (names, flight)
        return token

    loss, grad_x, g = _local_step(
        x[0], loss_target[0], _behind(norm1_w, l_token), w_a, w_z, w_b, _cols_from_slabs(g_conv_a), a_log, dt_bias,
        gdn_norm_w, norm2_w, final_norm_w[None], late_weights, emit)
    small_all = _gather_direct(
        _pack_small(g["norm1"], g["norm2"], g["final"], g["gnw"], g["small"][:, 0:GDN_HEADS],
                    g["small"][:, GDN_HEADS:2 * GDN_HEADS], loss), name="gather_small")
    got = {}
    for group in ("ffn", "out", "in"):
        names, (send_sems, recv_sems, srcs, lands) = flights[group]
        srcs, landed = _exchange_wait(send_sems, recv_sems, srcs, lands, small_all, name="grads_wait_" + group)
        got.update(zip(names, _with_own(landed, srcs, me)))
    o_in = [o.T for o in _adamw(got["w_in"], t_in(w_in), t_in(m_w_in), t_in(v_w_in), name="adamw_w_in", tc=256)]
    o_out = _adamw(got["w_out"], w_out[0], m_w_out[0], v_w_out[0], name="adamw_w_out")
    o_up = [o.T for o in _adamw(got["w_up"], t_in(w_up), t_in(m_w_up), t_in(v_w_up), name="adamw_w_up", tr=176)]
    o_down = _adamw(got["w_down"], w_down[0], m_w_down[0], v_w_down[0], name="adamw_w_down", tr=176)
    o_ca = _adamw(got["conv_a"], conv_qkv_w[0], m_conv_qkv_w[0], v_conv_qkv_w[0], name="adamw_conv_a")
    o_cf = _adamw(got["conv_f"], ffn_conv_w[0], m_ffn_conv_w[0], v_ffn_conv_w[0], name="adamw_conv_f")
    o_small = _adamw(
        small_all, _pack_small(norm1_w, norm2_w, final_norm_w[None], gdn_norm_w, a_log, dt_bias),
        _pack_small(m_norm1_w, m_norm2_w, m_final_norm_w[None], m_gdn_norm_w, m_a_log, m_dt_bias),
        _pack_small(v_norm1_w, v_norm2_w, v_final_norm_w[None], v_gdn_norm_w, v_a_log, v_dt_bias), name="adamw_small")
    total_loss = o_small[0][3, 256]
    outs = [total_loss, grad_x[None]]
    for k in range(4):
        n1, n2, fin, gn, al, dt = _unpack_small(o_small[k])
        outs += [n1, o_in[k][None], o_ca[k][None], al, dt, gn, o_out[k][None], n2, o_up[k][None], o_cf[k][None], o_down[k][None], fin]
    return tuple(outs)
```

```python
import functools

import jax
import jax.numpy as jnp
from jax import lax
from jax.experimental import pallas as pl
from jax.experimental.pallas import tpu as pltpu

F32 = jnp.float32
BF16 = jnp.bfloat16

N_DEV = 8
D_MODEL = 1024
GDN_HEADS = 4
GDN_DIM = 128
GDN_WIDTH = GDN_HEADS * GDN_DIM
GDN_CONV = 4
CHUNK = 64
CHUNKS_PER_STEP = 4
DIL_HEADS = 8
DIL_DIM = 64
DIL_WIDTH = DIL_HEADS * DIL_DIM
DIL_PAIRS = DIL_HEADS // 2
DILATIONS = (1, 4, 16)
BAND = 128
D_FF = 2816
FFN_CONV = 3
EPS = 1e-6
A_COLS = 3 * GDN_WIDTH + 128
HALO = 8

ADAM_LR = 0.001
ADAM_B1 = 0.9
ADAM_B2 = 0.999
ADAM_EPS = 1e-08
ADAM_WD = 0.01
ADAM_STEP = 10

VMEM_LIMIT_BYTES = 56 * 1024 * 1024
NEG_BIG = -1e30


def _params(sem=None):
    return pltpu.CompilerParams(dimension_semantics=sem, vmem_limit_bytes=VMEM_LIMIT_BYTES)


def _sds(shape, dtype=F32):
    return jax.ShapeDtypeStruct(shape, dtype)


def _bdot(a, b):
    return jnp.dot(a.astype(BF16), b.astype(BF16), preferred_element_type=F32)


def _bdot_nt(a, b):
    return lax.dot_general(a.astype(BF16), b.astype(BF16), (((1,), (1,)), ((), ())), preferred_element_type=F32)


def _bdot_tn(a, b):
    return lax.dot_general(a.astype(BF16), b.astype(BF16), (((0,), (0,)), ((), ())), preferred_element_type=F32)


def _split(a):
    hi = a.astype(BF16)
    lo = (a - hi.astype(F32)).astype(BF16)
    return hi, lo


def _dot3(a, b, dims):
    ah, al = _split(a)
    bh, bl = _split(b)
    d = functools.partial(lax.dot_general, dimension_numbers=(dims, ((), ())), preferred_element_type=F32)
    return d(ah, bh) + (d(al, bh) + d(ah, bl))


def _exact_tri_dot(tri, g):
    g1 = g.astype(BF16)
    r1 = g - g1.astype(F32)
    g2 = r1.astype(BF16)
    g3 = (r1 - g2.astype(F32)).astype(BF16)
    t = tri.astype(BF16)
    d = functools.partial(jnp.dot, preferred_element_type=F32)
    return d(t, g1) + (d(t, g2) + d(t, g3))


def _sigmoid(x):
    return 1.0 / (1.0 + jnp.exp(-x))


def _dsilu(x, sg):
    return sg * (1.0 + x * (1.0 - sg))


def _rms_bwd_rows(dh, x, w):
    r = lax.rsqrt(jnp.mean(x * x, axis=-1, keepdims=True) + EPS)
    xh = x * r
    gw = dh * w
    return r * (gw - xh * jnp.mean(gw * xh, axis=-1, keepdims=True)), jnp.sum(dh * xh, axis=0, keepdims=True)


def _mm(a, b, *, name, ta=False, tb=False, res=None, norm_bwd=None, out_dtype=F32, tm=512, tn=512, tk=512):
    if ta:
        K, M = a.shape
    else:
        M, K = a.shape
    if tb:
        N, Kb = b.shape
    else:
        Kb, N = b.shape
    assert K == Kb, (a.shape, b.shape)
    tm, tn, tk = min(tm, M), min(tn, N), min(tk, K)
    assert M % tm == 0 and N % tn == 0 and K % tk == 0, (name, M, N, K, tm, tn, tk)
    nk = K // tk
    dims = (((0 if ta else 1,), (1 if tb else 0,)), ((), ()))
    has_res = res is not None
    has_norm = norm_bwd is not None
    assert not has_norm or tn == N

    def body(*refs):
        a_ref, b_ref = refs[:2]
        r_ref = refs[2] if has_res else None
        if has_norm:
            x_ref, w_ref, skip_ref, o_ref, dw_ref, acc_ref = refs[2 + has_res:]
        else:
            o_ref, acc_ref = refs[2 + has_res:]
        i, k = pl.program_id(0), pl.program_id(2)
        part = lax.dot_general(a_ref[...].astype(BF16), b_ref[...].astype(BF16), dims, preferred_element_type=F32)

        @pl.when(k == 0)
        def _():
            acc_ref[...] = part

        @pl.when(k > 0)
        def _():
            acc_ref[...] += part

        @pl.when(k == nk - 1)
        def _():
            r = acc_ref[...]
            if has_res:
                r = r + r_ref[...]
            if has_norm:
                dx, dw = _rms_bwd_rows(r, x_ref[...], w_ref[...])
                o_ref[...] = skip_ref[...] + dx

                @pl.when(i == 0)
                def _():
                    dw_ref[...] = dw

                @pl.when(i > 0)
                def _():
                    dw_ref[...] += dw
            else:
                o_ref[...] = r.astype(out_dtype)

    a_spec = pl.BlockSpec((tk, tm), lambda i, j, k: (k, i)) if ta else pl.BlockSpec((tm, tk), lambda i, j, k: (i, k))
    b_spec = pl.BlockSpec((tn, tk), lambda i, j, k: (j, k)) if tb else pl.BlockSpec((tk, tn), lambda i, j, k: (k, j))
    o_spec = pl.BlockSpec((tm, tn), lambda i, j, k: (i, j))
    one = pl.BlockSpec((1, tn), lambda i, j, k: (0, 0))
    in_specs = [a_spec, b_spec] + [o_spec] * has_res + ([o_spec, one, o_spec] if has_norm else [])
    args = (a, b) + ((res,) if has_res else ()) + (tuple(norm_bwd) if has_norm else ())
    return pl.pallas_call(
        body, name=name, grid=(M // tm, N // tn, nk), in_specs=in_specs,
        out_specs=[o_spec, one] if has_norm else o_spec,
        out_shape=[_sds((M, N)), _sds((1, N))] if has_norm else _sds((M, N), out_dtype),
        scratch_shapes=[pltpu.VMEM((tm, tn), F32)],
        compiler_params=_params(("arbitrary" if has_norm else "parallel", "parallel", "arbitrary")),
    )(*args)


def _in_proj(x, norm_w, w_a, w_z, w_b, *, name, tm=512):
    S, D = x.shape
    ws = (w_a, w_z, w_b)

    def body(x_ref, nw_ref, wa_ref, wz_ref, wb_ref, h_ref, pa_ref, pz_ref, pb_ref):
        xv = x_ref[...]
        r = lax.rsqrt(jnp.mean(xv * xv, axis=-1, keepdims=True) + EPS)
        h = (xv * r * nw_ref[...]).astype(BF16)
        h_ref[...] = h
        for w_ref, p_ref in ((wa_ref, pa_ref), (wz_ref, pz_ref), (wb_ref, pb_ref)):
            p_ref[...] = lax.dot_general(h, w_ref[...], (((1,), (1,)), ((), ())), preferred_element_type=F32)

    row = lambda n: pl.BlockSpec((tm, n), lambda i: (i, 0))
    full = lambda a: pl.BlockSpec(a.shape, lambda i: (0, 0))
    return pl.pallas_call(
        body, name=name, grid=(S // tm,), in_specs=[row(D), full(norm_w)] + [full(w) for w in ws],
        out_specs=[row(D)] + [row(w.shape[0]) for w in ws],
        out_shape=[_sds((S, D), BF16)] + [_sds((S, w.shape[0])) for w in ws], compiler_params=_params(("parallel",)),
    )(x, norm_w, *ws)


def _out_proj_norm(a, w, x, norm_w, *, name, tm=512):
    S, D = x.shape

    def body(a_ref, w_ref, x_ref, nw_ref, x1_ref, h_ref):
        x1 = x_ref[...] + jnp.dot(a_ref[...], w_ref[...], preferred_element_type=F32)
        x1_ref[...] = x1
        r = lax.rsqrt(jnp.mean(x1 * x1, axis=-1, keepdims=True) + EPS)
        h_ref[...] = (x1 * r * nw_ref[...]).astype(BF16)

    row = pl.BlockSpec((tm, D), lambda i: (i, 0))
    return pl.pallas_call(
        body, name=name, grid=(S // tm,),
        in_specs=[pl.BlockSpec((tm, a.shape[1]), lambda i: (i, 0)), pl.BlockSpec(w.shape, lambda i: (0, 0)), row,
                  pl.BlockSpec((1, D), lambda i: (0, 0))],
        out_specs=[row, row], out_shape=[_sds((S, D)), _sds((S, D), BF16)], compiler_params=_params(("parallel",)),
    )(a, w, x, norm_w)


def _shifted(x, start, n):
    aligned = -(-start // HALO) * HALO
    assert aligned + n <= x.shape[0], (start, n, x.shape)
    return (x if aligned == start else pltpu.roll(x, aligned - start, axis=0))[aligned:aligned + n]


def _conv_rows(prev, cur, w, taps):
    n = cur.shape[0]
    xs = jnp.concatenate([prev, cur], axis=0)
    base = HALO - (taps - 1)
    out = _shifted(xs, base, n) * w[0:1]
    for i in range(1, taps):
        out = out + _shifted(xs, base + i, n) * w[i:i + 1]
    return out


def _conv_rows_bwd(cur_d, next_d, prev_x, cur_x, w, taps):
    n = cur_d.shape[0]
    ds = jnp.concatenate([cur_d, next_d], axis=0)
    dx = _shifted(ds, taps - 1, n) * w[0:1]
    for i in range(1, taps):
        dx = dx + _shifted(ds, taps - 1 - i, n) * w[i:i + 1]
    xs = jnp.concatenate([prev_x, cur_x], axis=0)
    base = HALO - (taps - 1)
    dws = [jnp.sum(cur_d * _shifted(xs, base + i, n), axis=0, keepdims=True) for i in range(taps)]
    return dx, jnp.concatenate(dws, axis=0)


def _halo_specs(tm, width, col, nblk):
    per = tm // HALO
    prev = pl.BlockSpec((HALO, width), lambda i, *_: (jnp.maximum(i * per - 1, 0), col))
    nxt = pl.BlockSpec((HALO, width), lambda i, *_: (jnp.minimum((i + 1) * per, nblk * per - 1), col))
    return prev, nxt


def _softplus(x):
    return jnp.maximum(x, 0.0) + jnp.log1p(jnp.exp(-jnp.abs(x)))


def _chunk_tri(tm, upper=False):
    r = lax.broadcasted_iota(jnp.int32, (tm, tm), 0)
    c = lax.broadcasted_iota(jnp.int32, (tm, tm), 1)
    same = lax.div(r, CHUNK) == lax.div(c, CHUNK)
    order = (c >= r) if upper else (c <= r)
    return jnp.where(same & order, 1.0, 0.0)


def _gdn_prep_fwd(proj_a, conv_w, a_log, dt_bias, *, name, tm=256):
    S = proj_a.shape[0]
    nblk = S // tm
    W3 = 3 * GDN_WIDTH

    def body(cur_ref, prev_ref, ba_ref, cw_ref, al_ref, dt_ref, qn_ref, kn_ref, v_ref, gcb_ref, bb_ref):
        i = pl.program_id(0)
        prev = jnp.where(i > 0, prev_ref[...], 0.0)
        c = _conv_rows(prev, cur_ref[...], cw_ref[...], GDN_CONV)
        a = c * _sigmoid(c)
        ba = ba_ref[...]
        lane = lax.broadcasted_iota(jnp.int32, (tm, 128), 1)
        g4 = jnp.zeros((tm, 128), F32)
        for h in range(GDN_HEADS):
            sl = slice(GDN_DIM * h, GDN_DIM * (h + 1))
            qh = a[:, GDN_DIM * h:GDN_DIM * (h + 1)]
            kh = a[:, GDN_WIDTH + GDN_DIM * h:GDN_WIDTH + GDN_DIM * (h + 1)]
            qn_ref[:, sl] = qh * (lax.rsqrt(jnp.sum(qh * qh, axis=-1, keepdims=True) + EPS) * (GDN_DIM ** -0.5))
            kn_ref[:, sl] = kh * lax.rsqrt(jnp.sum(kh * kh, axis=-1, keepdims=True) + EPS)
            beta = _sigmoid(ba[:, h:h + 1])
            bb_ref[:, sl] = jnp.broadcast_to(beta, (tm, GDN_DIM))
            g = -jnp.exp(al_ref[0:1, h:h + 1]) * _softplus(ba[:, GDN_HEADS + h:GDN_HEADS + h + 1] + dt_ref[0:1, h:h + 1])
            g4 = jnp.where(lane == h, g, g4)
        v_ref[...] = a[:, 2 * GDN_WIDTH:]
        gc = _exact_tri_dot(_chunk_tri(tm), g4)
        for h in range(GDN_HEADS):
            gcb_ref[:, GDN_DIM * h:GDN_DIM * (h + 1)] = jnp.broadcast_to(gc[:, h:h + 1], (tm, GDN_DIM))

    prev_spec, _ = _halo_specs(tm, W3, 0, nblk)
    row = pl.BlockSpec((tm, GDN_WIDTH), lambda i: (i, 0))
    small = lambda a: pl.BlockSpec(a.shape, lambda i: (0, 0))
    return pl.pallas_call(
        body, name=name, grid=(nblk,),
        in_specs=[pl.BlockSpec((tm, W3), lambda i: (i, 0)), prev_spec,
                  pl.BlockSpec((tm, 128), lambda i: (i, W3 // 128)), small(conv_w), small(a_log), small(dt_bias)],
        out_specs=[row] * 5, out_shape=[_sds((S, GDN_WIDTH))] * 5, compiler_params=_params(("parallel",)),
    )(proj_a, proj_a, proj_a, conv_w, a_log, dt_bias)


GDN_STACK = GDN_HEADS * CHUNK


def _stack(ref, rows):
    return jnp.concatenate([ref[rows, GDN_DIM * h:GDN_DIM * (h + 1)] for h in range(GDN_HEADS)], axis=0)


def _unstack_to(ref, rows, x):
    for h in range(GDN_HEADS):
        ref[rows, GDN_DIM * h:GDN_DIM * (h + 1)] = x[CHUNK * h:CHUNK * (h + 1)].astype(ref.dtype)


def _stack_masks():
    r = lax.broadcasted_iota(jnp.int32, (GDN_STACK, GDN_STACK), 0)
    c = lax.broadcasted_iota(jnp.int32, (GDN_STACK, GDN_STACK), 1)
    same = (r & -CHUNK) == (c & -CHUNK)
    return same & (r >= c), same & (r > c), r == c


def _stack_decay(gs, bs, incl):
    g2 = jnp.concatenate([gs, gs], axis=1)
    diff = g2 - g2.T
    dec = jnp.where(incl, jnp.exp(jnp.where(incl, diff, 0.0)), 0.0)
    return dec, jnp.concatenate([bs, bs], axis=1).T


def _head_mask():
    r = lax.broadcasted_iota(jnp.int32, (GDN_STACK, GDN_WIDTH), 0)
    c = lax.broadcasted_iota(jnp.int32, (GDN_STACK, GDN_WIDTH), 1)
    return (r & -CHUNK) * (GDN_DIM // CHUNK) == (c & -GDN_DIM)


def _head_spread(x):
    return jnp.where(_head_mask(), jnp.concatenate([x] * GDN_HEADS, axis=1), 0.0)


def _head_diag(x):
    xm = jnp.where(_head_mask(), x, 0.0)
    out = xm[:, 0:GDN_DIM]
    for h in range(1, GDN_HEADS):
        out = out + xm[:, GDN_DIM * h:GDN_DIM * (h + 1)]
    return out


def _last_rows(gs, n):
    return jnp.concatenate([jnp.broadcast_to(gs[CHUNK * (h + 1) - 1:CHUNK * (h + 1)], (n, GDN_DIM)) for h in range(GDN_HEADS)], axis=0)


def _gdn_chunk_fwd(qn, kn, v, gcb, bb, *, name):
    S = qn.shape[0]

    def body(qn_ref, kn_ref, v_ref, gcb_ref, bb_ref, uv_ref, wk_ref, at_ref, t_ref, wkb_ref, qdb_ref, keb_ref):
        incl, strict, diag = _stack_masks()
        for c in range(CHUNKS_PER_STEP):
            rows = slice(CHUNK * c, CHUNK * (c + 1))
            srows = slice(GDN_STACK * c, GDN_STACK * (c + 1))
            q, k, vv, gs, bs = [_stack(r, rows) for r in (qn_ref, kn_ref, v_ref, gcb_ref, bb_ref)]
            dec, bt = _stack_decay(gs, bs, incl)
            p = -jnp.where(strict, dec * _bdot_nt(k, k) * bt, 0.0)
            t = jnp.where(diag, 1.0, 0.0) + p
            for _ in range(5):
                p = _bdot(p, p)
                t = t + _bdot(t, p)
            sol = _dot3(t, jnp.concatenate([vv, jnp.exp(gs) * k], axis=1), ((1,), (0,)))
            _unstack_to(uv_ref, rows, sol[:, :GDN_DIM])
            _unstack_to(wk_ref, rows, sol[:, GDN_DIM:])
            at_ref[srows, :] = dec * _bdot_nt(q, k) * bt
            t_ref[srows, :] = t
            wkb_ref[srows, :] = _head_spread(sol[:, GDN_DIM:]).astype(BF16)
            qdb_ref[srows, :] = _head_spread(q * jnp.exp(gs)).astype(BF16)
            keb_ref[srows, :] = _head_spread(k * jnp.exp(_last_rows(gs, CHUNK) - gs) * bs).astype(BF16)

    step = CHUNKS_PER_STEP * CHUNK
    row = pl.BlockSpec((step, GDN_WIDTH), lambda n: (n, 0))
    sq = pl.BlockSpec((CHUNKS_PER_STEP * GDN_STACK, GDN_STACK), lambda n: (n, 0))
    wide = pl.BlockSpec((CHUNKS_PER_STEP * GDN_STACK, GDN_WIDTH), lambda n: (n, 0))
    nsq = S // CHUNK * GDN_STACK
    return pl.pallas_call(
        body, name=name, grid=(S // step,), in_specs=[row] * 5, out_specs=[row, row, sq, sq, wide, wide, wide],
        out_shape=[_sds((S, GDN_WIDTH)), _sds((S, GDN_WIDTH)), _sds((nsq, GDN_STACK)), _sds((nsq, GDN_STACK))]
        + [_sds((nsq, GDN_WIDTH), BF16)] * 3,
        compiler_params=_params(("parallel",)),
    )(qn, kn, v, gcb, bb)


SCAN_CHUNKS = 8


def _gdn_scan_fwd(uv, at, wkb, qdb, keb, gcb, proj_z, gnw, *, name):
    S = uv.shape[0]
    nc = S // CHUNK

    def body(uv_ref, at_ref, wkb_ref, qdb_ref, keb_ref, gcb_ref, z_ref, gnw_ref, o_ref, u_ref, sp_ref, oa_ref, st_ref):
        n = pl.program_id(0)

        @pl.when(n == 0)
        def _():
            st_ref[...] = jnp.zeros_like(st_ref)

        for c in range(SCAN_CHUNKS):
            rows = slice(CHUNK * c, CHUNK * (c + 1))
            srows = slice(GDN_STACK * c, GDN_STACK * (c + 1))
            st = st_ref[...]
            sp_ref[GDN_WIDTH * c:GDN_WIDTH * (c + 1), :] = st
            uv, gs, z = [_stack(r, rows) for r in (uv_ref, gcb_ref, z_ref)]
            u = uv - _bdot(wkb_ref[srows, :], st)
            o = _bdot(qdb_ref[srows, :], st) + _bdot(at_ref[srows, :], u)
            st_ref[...] = jnp.exp(_last_rows(gs, GDN_DIM)) * st + _bdot_tn(keb_ref[srows, :], u)
            _unstack_to(u_ref, rows, u)
            _unstack_to(o_ref, rows, o)
            r = lax.rsqrt(jnp.mean(o * o, axis=-1, keepdims=True) + EPS)
            oa = o * r * gnw_ref[...] * (z * _sigmoid(z))
            oa_ref[rows, :] = jnp.concatenate([oa[CHUNK * h:CHUNK * (h + 1)] for h in range(GDN_HEADS)], axis=1).astype(BF16)

    row = pl.BlockSpec((SCAN_CHUNKS * CHUNK, GDN_WIDTH), lambda n: (n, 0))
    sq = pl.BlockSpec((SCAN_CHUNKS * GDN_STACK, GDN_STACK), lambda n: (n, 0))
    wide = pl.BlockSpec((SCAN_CHUNKS * GDN_STACK, GDN_WIDTH), lambda n: (n, 0))
    return pl.pallas_call(
        body, name=name, grid=(nc // SCAN_CHUNKS,),
        in_specs=[row, sq, wide, wide, wide, row, row, pl.BlockSpec((1, GDN_DIM), lambda n: (0, 0))],
        out_specs=[row, row, pl.BlockSpec((SCAN_CHUNKS * GDN_WIDTH, GDN_DIM), lambda n: (n, 0)), row],
        out_shape=[_sds((S, GDN_WIDTH)), _sds((S, GDN_WIDTH)), _sds((nc * GDN_WIDTH, GDN_DIM)), _sds((S, 2 * GDN_WIDTH), BF16)],
        scratch_shapes=[pltpu.VMEM((GDN_WIDTH, GDN_DIM), F32)],
        compiler_params=_params(("arbitrary",)),
    )(uv, at, wkb, qdb, keb, gcb, proj_z, gnw)


def _gdn_scan_bwd(d_oab, o, proj_z, gnw, sp, u, at, wkb, qdb, keb, gcb, *, name):
    S = o.shape[0]
    nc = S // CHUNK
    ns = nc // SCAN_CHUNKS

    def body(do_ref, o_ref, z_ref, gnw_ref, sp_ref, u_ref, at_ref, wkb_ref, qdb_ref, keb_ref, gcb_ref,
             dz_ref, dgn_ref, du_ref, dwk_ref, dat_ref, dqd_ref, dke_ref, dgl_ref, ds_ref):
        n = pl.program_id(0)

        @pl.when(n == 0)
        def _():
            ds_ref[...] = jnp.zeros_like(ds_ref)
            dgn_ref[...] = jnp.zeros_like(dgn_ref)

        gw = gnw_ref[...]
        for c in reversed(range(SCAN_CHUNKS)):
            rows = slice(CHUNK * c, CHUNK * (c + 1))
            srows = slice(GDN_STACK * c, GDN_STACK * (c + 1))
            d_oa, oo, z, uu, gs = [_stack(r, rows) for r in (do_ref, o_ref, z_ref, u_ref, gcb_ref)]
            sg = _sigmoid(z)
            r = lax.rsqrt(jnp.mean(oo * oo, axis=-1, keepdims=True) + EPS)
            xh = oo * r
            dy = d_oa * (z * sg)
            _unstack_to(dz_ref, rows, d_oa * (xh * gw) * _dsilu(z, sg))
            dgn_ref[...] += jnp.sum(dy * xh, axis=0, keepdims=True)
            dxh = dy * gw
            do = r * (dxh - xh * jnp.mean(dxh * xh, axis=-1, keepdims=True))

            st = sp_ref[GDN_WIDTH * c:GDN_WIDTH * (c + 1), :]
            dst = ds_ref[...]
            ge = jnp.exp(_last_rows(gs, GDN_DIM))
            _unstack_to(dqd_ref, rows, _head_diag(_bdot_nt(do, st)))
            dat_ref[srows, :] = _bdot_nt(do, uu)
            du = _bdot_tn(at_ref[srows, :], do) + _bdot(keb_ref[srows, :], dst)
            _unstack_to(dke_ref, rows, _head_diag(_bdot_nt(uu, dst)))
            prod = dst * st
            for h in range(GDN_HEADS):
                blk = prod[GDN_DIM * h:GDN_DIM * (h + 1)]
                dge = jnp.sum(jnp.sum(blk, axis=1, keepdims=True), axis=0, keepdims=True)
                dgl_ref[c, :, GDN_DIM * h:GDN_DIM * (h + 1)] = jnp.broadcast_to(dge * ge[GDN_DIM * h:GDN_DIM * h + 1], (8, GDN_DIM))
            ds_ref[...] = _bdot_tn(qdb_ref[srows, :], do) + ge * dst - _bdot_tn(wkb_ref[srows, :], du)
            _unstack_to(du_ref, rows, du)
            _unstack_to(dwk_ref, rows, -_head_diag(_bdot_nt(du, st)))

    rev = lambda n: (ns - 1 - n, 0)
    row = pl.BlockSpec((SCAN_CHUNKS * CHUNK, GDN_WIDTH), rev)
    sq = pl.BlockSpec((SCAN_CHUNKS * GDN_STACK, GDN_STACK), rev)
    wide = pl.BlockSpec((SCAN_CHUNKS * GDN_STACK, GDN_WIDTH), rev)
    one = pl.BlockSpec((1, GDN_DIM), lambda n: (0, 0))
    return pl.pallas_call(
        body, name=name, grid=(ns,),
        in_specs=[row, row, row, one, pl.BlockSpec((SCAN_CHUNKS * GDN_WIDTH, GDN_DIM), rev), row, sq, wide, wide, wide, row],
        out_specs=[row, one, row, row, sq, row, row, pl.BlockSpec((SCAN_CHUNKS, 8, GDN_WIDTH), lambda n: (ns - 1 - n, 0, 0))],
        out_shape=[_sds((S, GDN_WIDTH), BF16), _sds((1, GDN_DIM)), _sds((S, GDN_WIDTH)), _sds((S, GDN_WIDTH)),
                   _sds((nc * GDN_STACK, GDN_STACK)), _sds((S, GDN_WIDTH)), _sds((S, GDN_WIDTH)), _sds((nc, 8, GDN_WIDTH))],
        scratch_shapes=[pltpu.VMEM((GDN_WIDTH, GDN_DIM), F32)],
        compiler_params=_params(("arbitrary",)),
    )(d_oab, o, proj_z, gnw, sp, u, at, wkb, qdb, keb, gcb)


def _gdn_chunk_bwd(qn, kn, gcb, bb, tmat, uv, wk, du, dwk, dat, dqd, dke, dgl, *, name):
    S = qn.shape[0]

    def body(qn_ref, kn_ref, gcb_ref, bb_ref, t_ref, uv_ref, wk_ref, du_ref, dwk_ref, dat_ref, dqd_ref, dke_ref,
             dgl_ref, dq_ref, dk_ref, dv_ref, dg_ref, dbeta_ref):
        incl, strict, _ = _stack_masks()
        lane = lax.broadcasted_iota(jnp.int32, (CHUNK, 128), 1)
        rowi = lax.broadcasted_iota(jnp.int32, (CHUNK, 1), 0)
        rsum = lambda x: jnp.sum(x, axis=-1, keepdims=True)
        for c in range(CHUNKS_PER_STEP):
            rows = slice(CHUNK * c, CHUNK * (c + 1))
            srows = slice(GDN_STACK * c, GDN_STACK * (c + 1))
            q, k, gs, bs, uv, wk, du, dwk, dqd, dke = [
                _stack(r, rows) for r in (qn_ref, kn_ref, gcb_ref, bb_ref, uv_ref, wk_ref, du_ref, dwk_ref, dqd_ref, dke_ref)]
            dec, bt = _stack_decay(gs, bs, incl)
            kk = _bdot_nt(k, k)
            qk = _bdot_nt(q, k)
            d_rhs = _dot3(t_ref[srows, :], jnp.concatenate([du, dwk], axis=1), ((0,), (0,)))
            sol = jnp.concatenate([uv, wk], axis=1)
            d_l = jnp.where(strict, -_dot3(d_rhs, sol, ((1,), (1,))), 0.0)
            d_a = jnp.where(incl, dat_ref[srows, :], 0.0)
            gam = jnp.exp(gs)
            e = jnp.exp(_last_rows(gs, CHUNK) - gs)
            d_gk = d_rhs[:, GDN_DIM:]
            ml = d_l * dec * bt
            ma = d_a * dec * bt
            _unstack_to(dq_ref, rows, _bdot(ma, k) + dqd * gam)
            _unstack_to(dk_ref, rows, _bdot(ml + ml.T, k) + _bdot_tn(ma, q) + d_gk * gam + dke * (e * bs))
            _unstack_to(dv_ref, rows, d_rhs[:, :GDN_DIM])
            wb = d_l * dec * kk + d_a * dec * qk
            ew = wb * bt
            s_ke = rsum(dke * k * (e * bs))
            dbeta = rsum(wb.T) + rsum(dke * k * e)
            dgc = rsum(ew) - rsum(ew.T) + rsum(dqd * q * gam) + rsum(d_gk * k * gam) - s_ke
            dgc4 = jnp.zeros((CHUNK, 128), F32)
            db4 = jnp.zeros((CHUNK, 128), F32)
            for h in range(GDN_HEADS):
                hr = slice(CHUNK * h, CHUNK * (h + 1))
                tail = jnp.sum(s_ke[hr], axis=0, keepdims=True) + dgl_ref[c, 0:1, GDN_DIM * h:GDN_DIM * h + 1]
                dgc4 = jnp.where(lane == h, dgc[hr] + jnp.where(rowi == CHUNK - 1, tail, 0.0), dgc4)
                db4 = jnp.where(lane == h, dbeta[hr], db4)
            dg_ref[rows, :] = _exact_tri_dot(_chunk_tri(CHUNK, upper=True), dgc4)
            dbeta_ref[rows, :] = db4

    step = CHUNKS_PER_STEP * CHUNK
    row = pl.BlockSpec((step, GDN_WIDTH), lambda n: (n, 0))
    sq = pl.BlockSpec((CHUNKS_PER_STEP * GDN_STACK, GDN_STACK), lambda n: (n, 0))
    col = pl.BlockSpec((step, 128), lambda n: (n, 0))
    return pl.pallas_call(
        body, name=name, grid=(S // step,),
        in_specs=[row] * 4 + [sq, row, row, row, row, sq, row, row,
                              pl.BlockSpec((CHUNKS_PER_STEP, 8, GDN_WIDTH), lambda n: (n, 0, 0))],
        out_specs=[row, row, row, col, col],
        out_shape=[_sds((S, GDN_WIDTH))] * 3 + [_sds((S, 128))] * 2, compiler_params=_params(("parallel",)),
    )(qn, kn, gcb, bb, tmat, uv, wk, du, dwk, dat, dqd, dke, dgl)


def _gdn_prep_bwd(dqn, dkn, dv, dg, dbeta, proj_a, conv_w, a_log, dt_bias, *, name, tm=256):
    S = proj_a.shape[0]
    nblk = S // tm
    W3 = 3 * GDN_WIDTH

    def body(dqn_ref, dkn_ref, dv_ref, dg_ref, dbeta_ref, cur_ref, prev_ref, ba_ref, cw_ref, al_ref, dt_ref,
             dc_ref, dba_ref, sm_ref):
        i = pl.program_id(0)
        prev = jnp.where(i > 0, prev_ref[...], 0.0)
        c = _conv_rows(prev, cur_ref[...], cw_ref[...], GDN_CONV)
        sg = _sigmoid(c)
        a = c * sg
        dsl = _dsilu(c, sg)
        ba = ba_ref[...]
        lane = lax.broadcasted_iota(jnp.int32, (tm, 128), 1)
        lane1 = lax.broadcasted_iota(jnp.int32, (1, 128), 1)
        dba = jnp.zeros((tm, 128), F32)
        sm = jnp.zeros((1, 128), F32)
        for h in range(GDN_HEADS):
            sl = slice(GDN_DIM * h, GDN_DIM * (h + 1))
            ks = slice(GDN_WIDTH + GDN_DIM * h, GDN_WIDTH + GDN_DIM * (h + 1))
            qh, kh = a[:, sl], a[:, ks]
            rq = lax.rsqrt(jnp.sum(qh * qh, axis=-1, keepdims=True) + EPS)
            rk = lax.rsqrt(jnp.sum(kh * kh, axis=-1, keepdims=True) + EPS)
            qhat, khat = qh * rq, kh * rk
            dyq = dqn_ref[:, sl] * (GDN_DIM ** -0.5)
            dyk = dkn_ref[:, sl]
            dq = rq * (dyq - qhat * jnp.sum(dyq * qhat, axis=-1, keepdims=True))
            dk = rk * (dyk - khat * jnp.sum(dyk * khat, axis=-1, keepdims=True))
            dc_ref[:, sl] = dq * dsl[:, sl]
            dc_ref[:, ks] = dk * dsl[:, ks]
            beta = _sigmoid(ba[:, h:h + 1])
            db = dbeta_ref[:, h:h + 1] * beta * (1.0 - beta)
            aneg = -jnp.exp(al_ref[0:1, h:h + 1])
            xa = ba[:, GDN_HEADS + h:GDN_HEADS + h + 1] + dt_ref[0:1, h:h + 1]
            dgh = dg_ref[:, h:h + 1]
            dxa = dgh * aneg * _sigmoid(xa)
            dba = jnp.where(lane == h, db, dba)
            dba = jnp.where(lane == GDN_HEADS + h, dxa, dba)
            d_alog = jnp.sum(dgh * _softplus(xa), axis=0, keepdims=True) * aneg
            sm = jnp.where(lane1 == h, d_alog, sm)
            sm = jnp.where(lane1 == GDN_HEADS + h, jnp.sum(dxa, axis=0, keepdims=True), sm)
        vs = slice(2 * GDN_WIDTH, W3)
        dc_ref[:, vs] = dv_ref[...] * dsl[:, vs]
        dba_ref[...] = dba

        @pl.when(i == 0)
        def _():
            sm_ref[...] = sm

        @pl.when(i > 0)
        def _():
            sm_ref[...] += sm

    prev_spec, _ = _halo_specs(tm, W3, 0, nblk)
    row = pl.BlockSpec((tm, GDN_WIDTH), lambda i: (i, 0))
    col = pl.BlockSpec((tm, 128), lambda i: (i, 0))
    small = lambda a: pl.BlockSpec(a.shape, lambda i: (0, 0))
    return pl.pallas_call(
        body, name=name, grid=(nblk,),
        in_specs=[row, row, row, col, col, pl.BlockSpec((tm, W3), lambda i: (i, 0)), prev_spec,
                  pl.BlockSpec((tm, 128), lambda i: (i, W3 // 128)), small(conv_w), small(a_log), small(dt_bias)],
        out_specs=[pl.BlockSpec((tm, W3), lambda i: (i, 0)), col, pl.BlockSpec((1, 128), lambda i: (0, 0))],
        out_shape=[_sds((S, W3)), _sds((S, 128)), _sds((1, 128))], compiler_params=_params(("arbitrary",)),
    )(dqn, dkn, dv, dg, dbeta, proj_a, proj_a, proj_a, conv_w, a_log, dt_bias)


def _gdn_conv_bwd(dc, dba, proj_a, conv_w, *, name, tm=256):
    S = proj_a.shape[0]
    nblk = S // tm
    W3 = 3 * GDN_WIDTH

    def body(dc_ref, dnext_ref, dba_ref, cur_ref, prev_ref, cw_ref, da_ref, dcw_ref):
        i = pl.program_id(0)
        prev = jnp.where(i > 0, prev_ref[...], 0.0)
        nxt = jnp.where(i < nblk - 1, dnext_ref[...], 0.0)
        dx, dw = _conv_rows_bwd(dc_ref[...], nxt, prev, cur_ref[...], cw_ref[...], GDN_CONV)
        da_ref[:, 0:W3] = dx.astype(BF16)
        da_ref[:, W3:] = dba_ref[...].astype(BF16)

        @pl.when(i == 0)
        def _():
            dcw_ref[...] = dw

        @pl.when(i > 0)
        def _():
            dcw_ref[...] += dw

    prev_spec, next_spec = _halo_specs(tm, W3, 0, nblk)
    wide = pl.BlockSpec((tm, W3), lambda i: (i, 0))
    return pl.pallas_call(
        body, name=name, grid=(nblk,),
        in_specs=[wide, next_spec, pl.BlockSpec((tm, 128), lambda i: (i, 0)), wide, prev_spec,
                  pl.BlockSpec(conv_w.shape, lambda i: (0, 0))],
        out_specs=[pl.BlockSpec((tm, A_COLS), lambda i: (i, 0)), pl.BlockSpec(conv_w.shape, lambda i: (0, 0))],
        out_shape=[_sds((S, A_COLS), BF16), _sds(conv_w.shape)], compiler_params=_params(("arbitrary",)),
    )(dc, dc, dba, proj_a, proj_a, conv_w)


def _band_mask(nk):
    i = lax.broadcasted_iota(jnp.int32, (2 * BAND, nk), 0) & (BAND - 1)
    j = lax.broadcasted_iota(jnp.int32, (2 * BAND, nk), 1)
    if nk == BAND:
        return j <= i
    return (j >= i) & (j <= i + BAND)


def _stack_heads(x, lo):
    return jnp.concatenate([jnp.where(lo, x, 0.0), jnp.where(lo, 0.0, x)], axis=0)


def _stack_cols(x):
    return jnp.concatenate([x[:, 0:1], x[:, DIL_DIM:DIL_DIM + 1]], axis=0)


def _unstack(x, lo):
    return jnp.where(lo, x[0:BAND], x[BAND:2 * BAND])


def _rows(start, size, stride):
    return pl.ds(start, size) if stride == 1 else pl.ds(start, size, stride=stride)


ATTN_LANES = 4


def _attn_blocks(S, visit_many, lanes=ATTN_LANES):
    for d in DILATIONS:
        nb = S // (d * BAND)
        if d == 1:
            half = nb // 2
            visit_many(d, [(0, 0, True), (0, half, False)])

            def pair(n, c):
                visit_many(1, [(0, n, False), (0, n + half, False)])
                return c
            lax.fori_loop(1, half, pair, 0)
        elif nb > 1:
            for r0 in range(0, d, lanes):
                visit_many(d, [(r0 + t, 0, True) for t in range(lanes)])

                def column(n, c, d=d, r0=r0):
                    visit_many(d, [(r0 + t, n, False) for t in range(lanes)])
                    return c
                lax.fori_loop(1, nb, column, 0)
        else:
            def group(g, c, d=d):
                visit_many(d, [(g * lanes + t, 0, True) for t in range(lanes)])
                return c
            lax.fori_loop(0, d // lanes, group, 0)


def _attn_fwd(proj_b, oab, *, name):
    S = proj_b.shape[0]
    scale = DIL_DIM ** -0.5

    def body(q_ref, k_ref, v_ref, oab_in_ref, ob_ref, lse_ref, m_ref, l_ref, acc_ref):
        del oab_in_ref
        lane = lax.broadcasted_iota(jnp.int32, (BAND, 128), 1)
        lo = lane < DIL_DIM
        m_ref[...] = jnp.full_like(m_ref, NEG_BIG)
        l_ref[...] = jnp.zeros_like(l_ref)
        acc_ref[...] = jnp.zeros_like(acc_ref)

        def load(d, r, n, first):
            nk = BAND if first else 2 * BAND
            qrows = _rows(r + n * (BAND * d), BAND, d)
            krows = _rows(r if first else r + (n - 1) * (BAND * d), nk, d)
            return dict(nk=nk, qrows=qrows, q=q_ref[qrows, :] * scale, k=k_ref[krows, :].astype(BF16),
                        v=v_ref[krows, :].astype(BF16), m=m_ref[qrows, :], l=l_ref[qrows, :], acc=acc_ref[qrows, :])

        def compute(b):
            q, k, v = b["q"], b["k"], b["v"]
            s = jnp.where(_band_mask(b["nk"]), _bdot_nt(_stack_heads(q, lo), k), NEG_BIG)
            m_old = _stack_cols(b["m"])
            m_new = jnp.maximum(m_old, jnp.max(s, axis=-1, keepdims=True))
            p = jnp.exp(s - m_new)
            alpha = _unstack(jnp.exp(m_old - m_new), lo)
            l_new = alpha * b["l"] + _unstack(jnp.sum(p, axis=-1, keepdims=True), lo)
            return _unstack(m_new, lo), l_new, alpha * b["acc"] + _unstack(_bdot(p, v), lo)

        def visit_many(d, blocks):
            loaded = [load(d, *blk) for blk in blocks]
            done = [compute(b) for b in loaded]
            for b, (m_new, l_new, acc_new) in zip(loaded, done):
                m_ref[b["qrows"], :] = m_new
                l_ref[b["qrows"], :] = l_new
                acc_ref[b["qrows"], :] = acc_new

        _attn_blocks(S, visit_many)
        ob_ref[...] = (acc_ref[...] / l_ref[...]).astype(BF16)
        lse_ref[...] = m_ref[...] + jnp.log(l_ref[...])

    part = lambda t: pl.BlockSpec((S, 128), lambda p: (0, 3 * p + t))
    return pl.pallas_call(
        body, name=name, grid=(DIL_PAIRS,),
        in_specs=[part(0), part(1), part(2), pl.BlockSpec(memory_space=pl.ANY)],
        out_specs=[pl.BlockSpec((S, 128), lambda p: (0, GDN_WIDTH // 128 + p)), pl.BlockSpec((S, 128), lambda p: (0, p))],
        out_shape=[_sds(oab.shape, BF16), _sds((S, DIL_WIDTH))],
        scratch_shapes=[pltpu.VMEM((S, 128), F32)] * 3, input_output_aliases={3: 0},
        compiler_params=_params(("parallel",)),
    )(proj_b, proj_b, proj_b, oab)


def _attn_bwd(proj_b, oab, d_oab, lse, *, name):
    S = proj_b.shape[0]
    scale = DIL_DIM ** -0.5

    def body(q_ref, k_ref, v_ref, o_ref, do_ref, lse_ref, dqkv_ref, dq_ref, dk_ref, dv_ref, delta_ref):
        lane = lax.broadcasted_iota(jnp.int32, (BAND, 128), 1)
        lo = lane < DIL_DIM
        dq_ref[...] = jnp.zeros_like(dq_ref)
        dk_ref[...] = jnp.zeros_like(dk_ref)
        dv_ref[...] = jnp.zeros_like(dv_ref)
        prod = do_ref[...] * o_ref[...].astype(F32)
        lo_all = lax.broadcasted_iota(jnp.int32, (S, 128), 1) < DIL_DIM
        delta_ref[...] = jnp.where(lo_all, jnp.sum(jnp.where(lo_all, prod, 0.0), axis=-1, keepdims=True),
                                   jnp.sum(jnp.where(lo_all, 0.0, prod), axis=-1, keepdims=True))

        def load(d, r, n, first):
            nk = BAND if first else 2 * BAND
            qrows = _rows(r + n * (BAND * d), BAND, d)
            krows = _rows(r if first else r + (n - 1) * (BAND * d), nk, d)
            return dict(nk=nk, qrows=qrows, krows=krows, q=q_ref[qrows, :] * scale, k=k_ref[krows, :], v=v_ref[krows, :],
                        do=do_ref[qrows, :], delta=delta_ref[qrows, :], lse=lse_ref[qrows, :],
                        dq=dq_ref[qrows, :], dk=dk_ref[krows, :], dv=dv_ref[krows, :])

        def compute(b):
            q, k, v, do = b["q"], b["k"], b["v"], b["do"]
            qs, dos = _stack_heads(q, lo), _stack_heads(do, lo)
            p = jnp.where(_band_mask(b["nk"]), jnp.exp(_bdot_nt(qs, k) - _stack_cols(b["lse"])), 0.0)
            ds = p * (_bdot_nt(dos, v) - _stack_cols(b["delta"]))
            dq = b["dq"] + _unstack(_bdot(ds, k), lo) * scale
            return dq, b["dk"] + _bdot_tn(ds, qs), b["dv"] + _bdot_tn(p, dos)

        def visit_many(d, blocks):
            loaded = [load(d, *blk) for blk in blocks]
            done = [compute(b) for b in loaded]
            for b, (dq, dk, dv) in zip(loaded, done):
                dq_ref[b["qrows"], :] = dq
                dk_ref[b["krows"], :] = dk
                dv_ref[b["krows"], :] = dv

        _attn_blocks(S, visit_many, lanes=2)
        dqkv_ref[:, 0:128] = dq_ref[...].astype(BF16)
        dqkv_ref[:, 128:256] = dk_ref[...].astype(BF16)
        dqkv_ref[:, 256:384] = dv_ref[...].astype(BF16)

    half = lambda p: (0, GDN_WIDTH // 128 + p)
    part = lambda t: pl.BlockSpec((S, 128), lambda p: (0, 3 * p + t))
    return pl.pallas_call(
        body, name=name, grid=(DIL_PAIRS,),
        in_specs=[part(0), part(1), part(2), pl.BlockSpec((S, 128), half), pl.BlockSpec((S, 128), half),
                  pl.BlockSpec((S, 128), lambda p: (0, p))],
        out_specs=pl.BlockSpec((S, 384), lambda p: (0, p)), out_shape=_sds((S, 3 * DIL_WIDTH), BF16),
        scratch_shapes=[pltpu.VMEM((S, 128), F32)] * 4, compiler_params=_params(("parallel",)),
    )(proj_b, proj_b, proj_b, oab, d_oab, lse)


FF_SLAB = 2 * D_FF // N_DEV
FF_PAIRS = N_DEV // 2
ROWS16 = 16


def _taps(w, x, base, n):
    out = _shifted(x, base, n) * w[0:1]
    for t in range(1, FFN_CONV):
        out = out + _shifted(x, base + t, n) * w[t:t + 1]
    return out


def _ffn_fwd(h2, x1, w_up, conv_w, w_down, final_w, tgt, *, name, tm=512):
    S, D = h2.shape
    ni = S // tm
    per = tm // ROWS16

    def body(h_ref, hp_ref, x1_ref, wg_ref, wu_ref, cg_ref, cu_ref, wd_ref, fw_ref, t_ref,
             dx_ref, dxb_ref, dfw_ref, loss_ref, ug_ref, uu_ref, x2_ref):
        i, j = pl.program_id(0), pl.program_id(1)
        hv = jnp.concatenate([hp_ref[...], h_ref[...]], axis=0)
        row = lax.broadcasted_iota(jnp.int32, (tm + ROWS16, 1), 0)
        keep = (i > 0) | (row >= ROWS16)

        def branch(w_ref, c_ref, u_ref):
            u = lax.dot_general(hv, w_ref[...], (((1,), (1,)), ((), ())), preferred_element_type=F32).astype(BF16)
            u_ref[...] = u[ROWS16:]
            return _taps(c_ref[...], jnp.where(keep, u.astype(F32), 0.0), ROWS16 - (FFN_CONV - 1), tm)

        gate = branch(wg_ref, cg_ref, ug_ref)
        up = branch(wu_ref, cu_ref, uu_ref)
        act = (gate * _sigmoid(gate) * up).astype(BF16)
        part = jnp.dot(act, wd_ref[...], preferred_element_type=F32)

        @pl.when(j == 0)
        def _():
            x2_ref[...] = x1_ref[...] + part

        @pl.when((j > 0) & (j < FF_PAIRS - 1))
        def _():
            x2_ref[...] += part

        @pl.when(j == FF_PAIRS - 1)
        def _():
            xv = x2_ref[...] + part
            wv = fw_ref[...]
            r = lax.rsqrt(jnp.mean(xv * xv, axis=-1, keepdims=True) + EPS)
            err = xv * r * wv - t_ref[...]
            lsum = jnp.sum(jnp.sum(err * err, axis=-1, keepdims=True), axis=0, keepdims=True) * (0.5 / D)
            g = err * (1.0 / D)
            xh = xv * r
            gw = g * wv
            dx = r * (gw - xh * jnp.mean(gw * xh, axis=-1, keepdims=True))
            dx_ref[...] = dx
            dxb_ref[...] = dx.astype(BF16)
            dfw = jnp.sum(g * xh, axis=0, keepdims=True)
            lpart = jnp.broadcast_to(lsum, (1, 128))

            @pl.when(i == 0)
            def _():
                dfw_ref[...] = dfw
                loss_ref[...] = lpart

            @pl.when(i > 0)
            def _():
                dfw_ref[...] += dfw
                loss_ref[...] += lpart

    rows = pl.BlockSpec((tm, D), lambda i, j: (i, 0))
    slab = lambda off: pl.BlockSpec((None, FF_SLAB, D), lambda i, j: (j + off, 0, 0))
    cslab = lambda off: pl.BlockSpec((None, FFN_CONV, FF_SLAB), lambda i, j: (j + off, 0, 0))
    uspec = pl.BlockSpec((None, tm, FF_SLAB), lambda i, j: (j, i, 0))
    return pl.pallas_call(
        body, name=name, grid=(ni, FF_PAIRS),
        in_specs=[rows, pl.BlockSpec((ROWS16, D), lambda i, j: (jnp.maximum(i * per - 1, 0), 0)), rows,
                  slab(0), slab(FF_PAIRS), cslab(0), cslab(FF_PAIRS), pl.BlockSpec((FF_SLAB, D), lambda i, j: (j, 0)),
                  pl.BlockSpec((1, D), lambda i, j: (0, 0)), rows],
        out_specs=[rows, rows, pl.BlockSpec((1, D), lambda i, j: (0, 0)), pl.BlockSpec((1, 128), lambda i, j: (0, 0)), uspec, uspec],
        out_shape=[_sds((S, D)), _sds((S, D), BF16), _sds((1, D)), _sds((1, 128)),
                   _sds((FF_PAIRS, S, FF_SLAB), BF16), _sds((FF_PAIRS, S, FF_SLAB), BF16)],
        scratch_shapes=[pltpu.VMEM((tm, D), F32)],
        compiler_params=_params(("arbitrary", "arbitrary")),
    )(h2, h2, x1, w_up, w_up, conv_w, conv_w, w_down, final_w, tgt)


def _ffn_bwd(dx2, h2, ug, uu, conv_w, w_down, *, name, tm=512):
    S, D = h2.shape
    ni = S // tm
    per = tm // ROWS16
    ext = tm + ROWS16

    def body(dx_ref, dxn_ref, h_ref, ug_ref, ugp_ref, ugn_ref, uu_ref, uup_ref, uun_ref, cg_ref, cu_ref, wd_ref,
             dug_ref, duu_ref, gd_ref, gg_ref, gu_ref, dcg_ref, dcu_ref, acc_d, acc_g, acc_u, acc_cg, acc_cu):
        i = pl.program_id(1)

        @pl.when(i == 0)
        def _():
            acc_d[...] = jnp.zeros_like(acc_d)
            acc_g[...] = jnp.zeros_like(acc_g)
            acc_u[...] = jnp.zeros_like(acc_u)
            acc_cg[...] = jnp.zeros_like(acc_cg)
            acc_cu[...] = jnp.zeros_like(acc_cu)

        dx = dx_ref[...]
        dxe = jnp.concatenate([dx, dxn_ref[...]], axis=0)
        row = lax.broadcasted_iota(jnp.int32, (ext, 1), 0)
        live = (i < ni - 1) | (row < tm)
        d_act = jnp.where(live, lax.dot_general(dxe, wd_ref[...], (((1,), (1,)), ((), ())), preferred_element_type=F32), 0.0)
        rowp = lax.broadcasted_iota(jnp.int32, (ext + ROWS16, 1), 0)
        keep = (i > 0) | (rowp >= ROWS16)

        def pre(cur, prev, nxt):
            return jnp.where(keep, jnp.concatenate([prev[...], cur[...], nxt[...]], axis=0).astype(F32), 0.0)

        uge, uue = pre(ug_ref, ugp_ref, ugn_ref), pre(uu_ref, uup_ref, uun_ref)
        cg, cu = cg_ref[...], cu_ref[...]
        base = ROWS16 - (FFN_CONV - 1)
        gate = _taps(cg, uge, base, ext)
        up = _taps(cu, uue, base, ext)
        sg = _sigmoid(gate)
        silu = gate * sg
        dgc = d_act * up * _dsilu(gate, sg)
        duc = d_act * silu

        def conv_t(w, dc):
            out = _shifted(dc, FFN_CONV - 1, tm) * w[0:1]
            for t in range(1, FFN_CONV):
                out = out + _shifted(dc, FFN_CONV - 1 - t, tm) * w[t:t + 1]
            return out.astype(BF16)

        du_g, du_u = conv_t(cg, dgc), conv_t(cu, duc)
        dug_ref[...] = du_g
        duu_ref[...] = du_u
        dcw = lambda dc, xe: jnp.concatenate(
            [jnp.sum(dc[0:tm] * _shifted(xe, base + t, tm), axis=0, keepdims=True) for t in range(FFN_CONV)], axis=0)
        acc_cg[0:FFN_CONV, :] += dcw(dgc, uge)
        acc_cu[0:FFN_CONV, :] += dcw(duc, uue)
        tn = (((0,), (0,)), ((), ()))
        act = (silu[0:tm] * up[0:tm]).astype(BF16)
        acc_d[...] += lax.dot_general(act, dx, tn, preferred_element_type=F32)
        hv = h_ref[...]
        acc_g[...] += lax.dot_general(du_g, hv, tn, preferred_element_type=F32)
        acc_u[...] += lax.dot_general(du_u, hv, tn, preferred_element_type=F32)

        @pl.when(i == ni - 1)
        def _():
            gd_ref[...] = acc_d[...].astype(BF16)
            gg_ref[...] = acc_g[...].astype(BF16)
            gu_ref[...] = acc_u[...].astype(BF16)
            dcg_ref[...] = acc_cg[0:FFN_CONV, :]
            dcu_ref[...] = acc_cu[0:FFN_CONV, :]

    last16 = S // ROWS16 - 1
    rows = pl.BlockSpec((tm, D), lambda j, i: (i, 0))
    rows_next = pl.BlockSpec((ROWS16, D), lambda j, i: (jnp.minimum((i + 1) * per, last16), 0))
    u_cur = pl.BlockSpec((None, tm, FF_SLAB), lambda j, i: (j, i, 0))
    u_prev = pl.BlockSpec((None, ROWS16, FF_SLAB), lambda j, i: (j, jnp.maximum(i * per - 1, 0), 0))
    u_next = pl.BlockSpec((None, ROWS16, FF_SLAB), lambda j, i: (j, jnp.minimum((i + 1) * per, last16), 0))
    cslab = lambda off: pl.BlockSpec((None, FFN_CONV, FF_SLAB), lambda j, i: (j + off, 0, 0))
    wslab = pl.BlockSpec((None, FF_SLAB, D), lambda j, i: (j, 0, 0))
    dslab = pl.BlockSpec((None, FFN_CONV, FF_SLAB), lambda j, i: (j, 0, 0))
    return pl.pallas_call(
        body, name=name, grid=(FF_PAIRS, ni),
        in_specs=[rows, rows_next, rows, u_cur, u_prev, u_next, u_cur, u_prev, u_next, cslab(0), cslab(FF_PAIRS),
                  pl.BlockSpec((FF_SLAB, D), lambda j, i: (j, 0))],
        out_specs=[u_cur, u_cur, pl.BlockSpec((FF_SLAB, D), lambda j, i: (j, 0)), wslab, wslab, dslab, dslab],
        out_shape=[_sds((FF_PAIRS, S, FF_SLAB), BF16), _sds((FF_PAIRS, S, FF_SLAB), BF16), _sds((D_FF, D), BF16),
                   _sds((FF_PAIRS, FF_SLAB, D), BF16), _sds((FF_PAIRS, FF_SLAB, D), BF16),
                   _sds((FF_PAIRS, FFN_CONV, FF_SLAB)), _sds((FF_PAIRS, FFN_CONV, FF_SLAB))],
        scratch_shapes=[pltpu.VMEM((FF_SLAB, D), F32), pltpu.VMEM((FF_SLAB, D), F32), pltpu.VMEM((FF_SLAB, D), F32),
                        pltpu.VMEM((8, FF_SLAB), F32), pltpu.VMEM((8, FF_SLAB), F32)],
        compiler_params=_params(("parallel", "arbitrary")),
    )(dx2, dx2, h2, ug, ug, ug, uu, uu, uu, conv_w, conv_w, w_down)


def _mm_slabs(a, w, w_off, *, name, res=None, norm_bwd=None, tm=1024, tn=1024):
    nk, S, _ = a.shape
    D = w.shape[2]
    has_res = res is not None
    has_norm = norm_bwd is not None
    assert not has_norm or tn == D

    def body(*refs):
        a_ref, w_ref = refs[:2]
        r_ref = refs[2] if has_res else None
        if has_norm:
            x_ref, nw_ref, skip_ref, o_ref, dw_ref, acc_ref = refs[2 + has_res:]
        else:
            o_ref, acc_ref = refs[2 + has_res:]
        i, k = pl.program_id(0), pl.program_id(2)
        part = jnp.dot(a_ref[...], w_ref[...], preferred_element_type=F32)

        @pl.when(k == 0)
        def _():
            acc_ref[...] = part

        @pl.when(k > 0)
        def _():
            acc_ref[...] += part

        @pl.when(k == nk - 1)
        def _():
            r = acc_ref[...] + r_ref[...] if has_res else acc_ref[...]
            if has_norm:
                dx, dw = _rms_bwd_rows(r, x_ref[...], nw_ref[...])
                o_ref[...] = skip_ref[...] + dx

                @pl.when(i == 0)
                def _():
                    dw_ref[...] = dw

                @pl.when(i > 0)
                def _():
                    dw_ref[...] += dw
            else:
                o_ref[...] = r

    o_spec = pl.BlockSpec((tm, tn), lambda i, j, k: (i, j))
    one = pl.BlockSpec((1, tn), lambda i, j, k: (0, 0))
    return pl.pallas_call(
        body, name=name, grid=(S // tm, D // tn, nk),
        in_specs=[pl.BlockSpec((None, tm, FF_SLAB), lambda i, j, k: (k, i, 0)),
                  pl.BlockSpec((None, FF_SLAB, tn), lambda i, j, k: (k + w_off, 0, j))] + [o_spec] * has_res
        + ([o_spec, one, o_spec] if has_norm else []),
        out_specs=[o_spec, one] if has_norm else o_spec, out_shape=[_sds((S, D)), _sds((1, D))] if has_norm else _sds((S, D)),
        scratch_shapes=[pltpu.VMEM((tm, tn), F32)],
        compiler_params=_params(("arbitrary" if has_norm else "parallel", "parallel", "arbitrary")),
    )(*((a, w) + ((res,) if has_res else ()) + (tuple(norm_bwd) if has_norm else ())))


def _local_step(x, tgt, norm1_w, w_a, w_z, w_b, conv_a, a_log, dt_bias, gnw, norm2_w, final_w, late_weights, emit):
    wgrad = functools.partial(_mm, ta=True, out_dtype=BF16)
    h1, proj_a, proj_z, proj_b = _in_proj(x, norm1_w, w_a, w_z, w_b, name="in_proj")
    qn, kn, v, gcb, bb = _gdn_prep_fwd(proj_a, conv_a, a_log, dt_bias, name="gdn_prep_fwd")
    uv, wk, at, tmat, wkb, qdb, keb = _gdn_chunk_fwd(qn, kn, v, gcb, bb, name="gdn_chunk_fwd")
    o, u, sp, oab = _gdn_scan_fwd(uv, at, wkb, qdb, keb, gcb, proj_z, gnw, name="gdn_scan_fwd")
    oab, lse = _attn_fwd(proj_b, oab, name="attn_fwd")
    w_out, w_up, conv_f, w_down = late_weights(oab)
    x1, h2 = _out_proj_norm(oab, w_out, x, norm2_w, name="out_proj")
    dx2, dx2_b, d_final, loss, ug, uu = _ffn_fwd(h2, x1, w_up, conv_f, w_down, final_w, tgt, name="ffn_fwd")
    dug, duu, g_down, g_up_g, g_up_u, dcw_g, dcw_u = _ffn_bwd(dx2_b, h2, ug, uu, conv_f, w_down, name="ffn_bwd")
    token = emit("ffn", w_down=g_down, w_up=jnp.concatenate([g_up_g, g_up_u], axis=0),
                 conv_f=jnp.concatenate([dcw_g, dcw_u], axis=0))
    dh2 = _mm_slabs(dug, w_up, 0, name="ffn_up_dx_gate")
    dx1, d_norm2 = _mm_slabs(duu, w_up, FF_PAIRS, res=dh2, norm_bwd=(x1, _behind(norm2_w, token), dx2), name="ffn_up_dx_up")
    d_oab = _mm(dx1, w_out, tb=True, name="out_proj_dx", tk=1024)
    gnw = _behind(gnw, emit("out", w_out=wgrad(oab, dx1, name="out_proj_dw")))
    dz, d_gnw, du, dwk, dat, dqd, dke, dgl = _gdn_scan_bwd(d_oab, o, proj_z, gnw, sp, u, at, wkb, qdb, keb, gcb, name="gdn_scan_bwd")
    dqn, dkn, dv, dg, dbeta = _gdn_chunk_bwd(qn, kn, gcb, bb, tmat, uv, wk, du, dwk, dat, dqd, dke, dgl, name="gdn_chunk_bwd")
    dc, dba, d_small = _gdn_prep_bwd(dqn, dkn, dv, dg, dbeta, proj_a, conv_a, a_log, dt_bias, name="gdn_prep_bwd")
    d_pa, d_conv_a = _gdn_conv_bwd(dc, dba, proj_a, conv_a, name="gdn_conv_bwd")
    d_pb = _attn_bwd(proj_b, oab, d_oab, lse, name="attn_bwd")
    g_a = wgrad(d_pa, h1, name="proj_a_dw", tm=A_COLS)
    g_z = wgrad(dz, h1, name="proj_z_dw")
    g_b = wgrad(d_pb, h1, name="proj_b_dw", tm=768)
    w_z = _behind(w_z, emit("in", w_a=g_a, w_z=g_z, w_b=g_b, conv_a=d_conv_a))
    dh1 = _mm(dz, w_z, name="proj_z_dx")
    dh1 = _mm(d_pa, w_a, res=dh1, name="proj_a_dx", tk=A_COLS)
    grad_x, d_norm1 = _mm(d_pb, w_b, res=dh1, norm_bwd=(x, norm1_w, dx1), name="proj_b_dx", tn=D_MODEL, tk=1536)
    small = dict(norm1=d_norm1, small=d_small, gnw=d_gnw, norm2=d_norm2, final=d_final)
    return loss, grad_x, small


_O1 = 3 * GDN_WIDTH
_O2 = _O1 + GDN_WIDTH
_O3 = _O2 + 2 * GDN_HEADS


def _split_w_in(w_t):
    d = w_t.shape[1]
    pad = jnp.zeros((A_COLS - _O1 - 2 * GDN_HEADS, d), w_t.dtype)
    w_a = jnp.concatenate([w_t[:_O1], w_t[_O2:_O3], pad], axis=0)
    w_b = w_t[_O3:].reshape(3, DIL_PAIRS, 128, d).transpose(1, 0, 2, 3).reshape(3 * DIL_WIDTH, d)
    return w_a, w_t[_O1:_O2], w_b


def _merge_g_in(g_a, g_z, g_b):
    d = g_a.shape[1]
    g_b = g_b.reshape(DIL_PAIRS, 3, 128, d).transpose(1, 0, 2, 3).reshape(3 * DIL_WIDTH, d)
    return jnp.concatenate([g_a[:_O1], g_z, g_a[_O1:_O1 + 2 * GDN_HEADS], g_b], axis=0)


MESH = pl.DeviceIdType.MESH
ANY = pl.BlockSpec(memory_space=pl.ANY)


def _position():
    return lax.axis_index("x"), lax.axis_index("y"), lax.axis_index("c")


def _slot(p):
    return 4 * p[0] + 2 * p[1] + p[2]


def _all_gather(blocks, *, name):
    n = len(blocks)

    def body(*refs):
        ins, outs = refs[:n], refs[n:2 * n]
        send_sems, recv_sems, local_sems = refs[2 * n:]
        x, y, c = _position()
        me, sibling = (x, y, c), (x, y, 1 - c)
        chips = [(1 - x, y), (x, 1 - y), (1 - x, 1 - y)]

        def copy(a, k, block, to, src=None):
            dst = outs[a].at[_slot(block)]
            return pltpu.make_async_remote_copy(
                src_ref=dst if src is None else src, dst_ref=dst, send_sem=send_sems.at[a, k], recv_sem=recv_sems.at[a, k],
                device_id=to, device_id_type=MESH)

        mine = [pltpu.make_async_copy(ins[a], outs[a].at[_slot(me)], local_sems.at[a]) for a in range(n)]
        for cp in mine:
            cp.start()
        first = []
        for a in range(n):
            first.append(copy(a, 0, me, sibling, src=ins[a]))
            first += [copy(a, 1 + j, me, (*chip, c), src=ins[a]) for j, chip in enumerate(chips)]
        for cp in first:
            cp.start()
        passed = []
        for j, chip in enumerate(chips):
            for a in range(n):
                copy(a, 1 + j, (*chip, c), me).wait_recv()
                fwd = copy(a, 4 + j, (*chip, c), sibling)
                fwd.start()
                passed.append(fwd)
        for a in range(n):
            copy(a, 0, sibling, me).wait_recv()
            for j, chip in enumerate(chips):
                copy(a, 4 + j, (*chip, 1 - c), me).wait_recv()
        for cp in first + passed:
            cp.wait_send()
        for cp in mine:
            cp.wait()

    return pl.pallas_call(
        body, name=name, in_specs=[ANY] * n, out_specs=[ANY] * n,
        out_shape=[_sds((N_DEV,) + b.shape, b.dtype) for b in blocks],
        scratch_shapes=[pltpu.SemaphoreType.DMA((n, 7)), pltpu.SemaphoreType.DMA((n, 7)), pltpu.SemaphoreType.DMA((n,))],
    )(*blocks)


def _gather_direct(block, *, name):
    def body(in_ref, out_ref, send_sems, recv_sems, local_sem):
        x, y, c = _position()
        me = _slot((x, y, c))
        mine = pltpu.make_async_copy(in_ref, out_ref.at[me], local_sem)
        mine.start()
        copies = [pltpu.make_async_remote_copy(
            src_ref=in_ref, dst_ref=out_ref.at[me], send_sem=send_sems.at[k - 1], recv_sem=recv_sems.at[k - 1],
            device_id=_peer_of(k, x, y, c), device_id_type=MESH) for k in range(1, N_DEV)]
        for cp in copies:
            cp.start()
        for cp in copies:
            cp.wait()
        mine.wait()

    return pl.pallas_call(
        body, name=name, in_specs=[pl.BlockSpec(memory_space=pltpu.VMEM)], out_specs=pl.BlockSpec(memory_space=pltpu.VMEM),
        out_shape=_sds((N_DEV,) + block.shape, block.dtype),
        scratch_shapes=[pltpu.SemaphoreType.DMA((N_DEV - 1,)), pltpu.SemaphoreType.DMA((N_DEV - 1,)), pltpu.SemaphoreType.DMA],
    )(block)


HBM = pl.BlockSpec(memory_space=pltpu.HBM)
SEM = pl.BlockSpec(memory_space=pltpu.SEMAPHORE)
EFFECT = pltpu.SideEffectType.DATAFLOW_SIDE_EFFECTING


def _peer_of(k, x, y, c):
    return (1 - x if k & 4 else x, 1 - y if k & 2 else y, 1 - c if k & 1 else c)


def _flight(a, k):
    return a * (N_DEV - 1) + k - 1


def _exchange_start(arrays, *, name, broadcast=False):
    n = len(arrays)

    def body(*refs):
        ins, lands = refs[:n], refs[n:2 * n]
        send_sems, recv_sems = refs[2 * n:2 * n + 2]
        token = refs[-1]
        x, y, c = _position()
        me = _slot((x, y, c))
        for k in range(1, N_DEV):
            peer = _peer_of(k, x, y, c)
            for a in range(n):
                pltpu.make_async_remote_copy(
                    src_ref=ins[a] if broadcast else ins[a].at[_slot(peer)], dst_ref=lands[a].at[me],
                    send_sem=send_sems.at[_flight(a, k)], recv_sem=recv_sems.at[_flight(a, k)],
                    device_id=peer, device_id_type=MESH).start()
        token[...] = jnp.zeros_like(token)

    land_shapes = [((N_DEV,) + s.shape) if broadcast else s.shape for s in arrays]
    lands = [pltpu.with_memory_space_constraint(lax.empty(shp, s.dtype), pltpu.HBM) for shp, s in zip(land_shapes, arrays)]
    srcs = [pltpu.with_memory_space_constraint(s, pltpu.HBM) for s in arrays]
    outs = pl.pallas_call(
        body, name=name, in_specs=[HBM] * (2 * n),
        out_specs=[SEM, SEM] + [HBM] * (2 * n) + [pl.BlockSpec(memory_space=pltpu.VMEM)],
        out_shape=[pltpu.SemaphoreType.DMA((n * (N_DEV - 1),)), pltpu.SemaphoreType.DMA((n * (N_DEV - 1),))]
        + [pltpu.HBM(s.shape, s.dtype) for s in arrays] + [pltpu.HBM(shp, s.dtype) for shp, s in zip(land_shapes, arrays)]
        + [_sds((8, 128))],
        input_output_aliases={i: 2 + i for i in range(2 * n)},
        compiler_params=pltpu.CompilerParams(has_side_effects=EFFECT),
    )(*srcs, *lands)
    return outs[0], outs[1], outs[2:2 + n], outs[2 + n:2 + 2 * n], outs[-1]


def _exchange_wait(send_sems, recv_sems, srcs, lands, after, *, name, broadcast=False):
    n = len(srcs)

    def body(*refs):
        ins, lnd = refs[:n], refs[n:2 * n]
        send_ref, recv_ref = refs[2 * n:2 * n + 2]
        x, y, c = _position()
        for k in range(1, N_DEV):
            for a in range(n):
                cp = pltpu.make_async_remote_copy(
                    src_ref=ins[a] if broadcast else ins[a].at[0], dst_ref=lnd[a].at[0], send_sem=send_ref.at[_flight(a, k)],
                    recv_sem=recv_ref.at[_flight(a, k)], device_id=_peer_of(k, x, y, c), device_id_type=MESH)
                cp.wait_send()
                cp.wait_recv()

    outs = pl.pallas_call(
        body, name=name, in_specs=[HBM] * (2 * n) + [SEM, SEM, ANY], out_specs=[HBM] * (2 * n),
        out_shape=[pltpu.HBM(s.shape, s.dtype) for s in srcs] + [pltpu.HBM(s.shape, s.dtype) for s in lands],
        input_output_aliases={i: i for i in range(2 * n)},
        compiler_params=pltpu.CompilerParams(has_side_effects=EFFECT),
    )(*srcs, *lands, send_sems, recv_sems, after)
    return outs[:n], outs[n:]


def _with_own(landed, srcs, me, broadcast=False):
    own = srcs if broadcast else [lax.dynamic_index_in_dim(s, me, 0, keepdims=False) for s in srcs]
    return [lax.dynamic_update_index_in_dim(l, o, me, 0) for l, o in zip(landed, own)]


def _behind(x, token):
    return x if token is None else x + token[0, 0].astype(x.dtype)


def _adamw(parts, w, m, v, *, name, tr=None, tc=None):
    R, C = w.shape
    tr = R if tr is None else tr
    tc = C if tc is None else tc
    assert R % tr == 0 and C % tc == 0
    c1 = 1.0 - ADAM_B1 ** ADAM_STEP
    c2 = 1.0 - ADAM_B2 ** ADAM_STEP

    def body(p_ref, w_ref, m_ref, v_ref, g_ref, d_ref, nm_ref, nv_ref):
        g = p_ref[0].astype(F32)
        for s in range(1, N_DEV):
            g = g + p_ref[s].astype(F32)
        nm = ADAM_B1 * m_ref[...] + (1.0 - ADAM_B1) * g
        nv = ADAM_B2 * v_ref[...] + (1.0 - ADAM_B2) * (g * g)
        g_ref[...] = g
        nm_ref[...] = nm
        nv_ref[...] = nv
        d_ref[...] = -ADAM_LR * ((nm / c1) / (jnp.sqrt(nv / c2) + ADAM_EPS) + ADAM_WD * w_ref[...])

    blk = pl.BlockSpec((tr, tc), lambda i, j: (i, j))
    return pl.pallas_call(
        body, name=name, grid=(R // tr, C // tc),
        in_specs=[pl.BlockSpec((N_DEV, tr, tc), lambda i, j: (0, i, j)), blk, blk, blk],
        out_specs=[blk] * 4, out_shape=[_sds((R, C))] * 4, compiler_params=_params(("parallel", "parallel")),
    )(parts, w, m, v)


_SMALL_ROWS = 8


def _pack_small(norm1, norm2, final, gnw, a_log, dt_bias, loss=None):
    loss = jnp.zeros((1, 128), F32) if loss is None else loss
    row3 = jnp.concatenate([gnw, a_log, dt_bias, jnp.zeros((1, 128 - 2 * GDN_HEADS), F32), loss,
                            jnp.zeros((1, D_MODEL - 3 * 128), F32)], axis=1)
    return jnp.concatenate([norm1, norm2, final, row3, jnp.zeros((_SMALL_ROWS - 4, D_MODEL), F32)], axis=0)


def _unpack_small(p):
    return (p[0:1], p[1:2], p[2], p[3:4, 0:128], p[3:4, 128:128 + GDN_HEADS], p[3:4, 128 + GDN_HEADS:128 + 2 * GDN_HEADS])


def _slabs_by_cols(g):
    r = g.shape[0]
    return g.reshape(r, N_DEV, -1).transpose(1, 0, 2)


def _cols_from_slabs(s):
    return s.transpose(1, 0, 2).reshape(s.shape[1], -1)


def kernel(x, norm1_w, w_in, conv_qkv_w, a_log, dt_bias, gdn_norm_w, w_out, norm2_w, w_up, ffn_conv_w, w_down, final_norm_w, loss_target, m_norm1_w, m_w_in, m_conv_qkv_w, m_a_log, m_dt_bias, m_gdn_norm_w, m_w_out, m_norm2_w, m_w_up, m_ffn_conv_w, m_w_down, m_final_norm_w, v_norm1_w, v_w_in, v_conv_qkv_w, v_a_log, v_dt_bias, v_gdn_norm_w, v_w_out, v_norm2_w, v_w_up, v_ffn_conv_w, v_w_down, v_final_norm_w):
    bf = lambda a: a.astype(BF16)
    me = _slot(_position())
    t_in = lambda a: a[0].T
    gw_in, g_conv_a = _all_gather([bf(t_in(w_in)), conv_qkv_w[0]], name="gather_w_in")
    w_a, w_z, w_b = _split_w_in(gw_in.reshape(-1, D_MODEL))
    late_src, _ = lax.optimization_barrier(([bf(w_out[0]), bf(t_in(w_up)), bf(w_down[0]), ffn_conv_w[0]], gw_in))
    l_send, l_recv, l_srcs, l_lands, l_token = _exchange_start(late_src, name="weights_start", broadcast=True)

    def late_weights(after):
        srcs, landed = _exchange_wait(l_send, l_recv, l_srcs, l_lands, after, name="weights_wait", broadcast=True)
        gw_out, gw_up, gw_down, g_conv_f = _with_own(landed, srcs, me, broadcast=True)
        return gw_out.reshape(D_MODEL, D_MODEL), gw_up, g_conv_f, gw_down.reshape(D_FF, D_MODEL)

    flights = {}

    def emit(group, **grads):
        if group == "in":
            slabs = dict(w_in=_merge_g_in(grads["w_a"], grads["w_z"], grads["w_b"]).reshape(N_DEV, -1, D_MODEL),
                         conv_a=_slabs_by_cols(grads["conv_a"]))
        elif group == "ffn":
            slabs = dict(w_down=grads["w_down"].reshape(N_DEV, -1, D_MODEL), w_up=grads["w_up"], conv_f=grads["conv_f"])
        else:
            slabs = {k: v.reshape(N_DEV, -1, D_MODEL) for k, v in grads.items()}
        names = list(slabs)
        *flight, token = _exchange_start([slabs[k] for k in names], name="grads_start_" + group)
        flights[group] = (names, flight)
        return token

    loss, grad_x, g = _local_step(
        x[0], loss_target[0], _behind(norm1_w, l_token), w_a, w_z, w_b, _cols_from_slabs(g_conv_a), a_log, dt_bias,
        gdn_norm_w, norm2_w, final_norm_w[None], late_weights, emit)
    got = {}

    def collect(group, after):
        names, (send_sems, recv_sems, srcs, lands) = flights[group]
        srcs, landed = _exchange_wait(send_sems, recv_sems, srcs, lands, after, name="grads_wait_" + group)
        got.update(zip(names, _with_own(landed, srcs, me)))

    collect("ffn", grad_x)
    collect("out", grad_x)
    o_out = _adamw(got["w_out"], w_out[0], m_w_out[0], v_w_out[0], name="adamw_w_out")
    o_up = [o.T for o in _adamw(got["w_up"], t_in(w_up), t_in(m_w_up), t_in(v_w_up), name="adamw_w_up", tr=176)]
    o_down = _adamw(got["w_down"], w_down[0], m_w_down[0], v_w_down[0], name="adamw_w_down", tr=176)
    o_cf = _adamw(got["conv_f"], ffn_conv_w[0], m_ffn_conv_w[0], v_ffn_conv_w[0], name="adamw_conv_f")
    pack = _pack_small(g["norm1"], g["norm2"], g["final"], g["gnw"], g["small"][:, 0:GDN_HEADS],
                       g["small"][:, GDN_HEADS:2 * GDN_HEADS], loss)
    pack, _ = lax.optimization_barrier((pack, (o_out[0], o_up[0], o_down[0], o_cf[0])))
    small_all = _gather_direct(pack, name="gather_small")
    collect("in", small_all)
    o_in = [o.T for o in _adamw(got["w_in"], t_in(w_in), t_in(m_w_in), t_in(v_w_in), name="adamw_w_in", tc=256)]
    o_ca = _adamw(got["conv_a"], conv_qkv_w[0], m_conv_qkv_w[0], v_conv_qkv_w[0], name="adamw_conv_a")
    o_small = _adamw(
        small_all, _pack_small(norm1_w, norm2_w, final_norm_w[None], gdn_norm_w, a_log, dt_bias),
        _pack_small(m_norm1_w, m_norm2_w, m_final_norm_w[None], m_gdn_norm_w, m_a_log, m_dt_bias),
        _pack_small(v_norm1_w, v_norm2_w, v_final_norm_w[None], v_gdn_norm_w, v_a_log, v_dt_bias), name="adamw_small")
    total_loss = o_small[0][3, 256]
    outs = [total_loss, grad_x[None]]
    for k in range(4):
        n1, n2, fin, gn, al, dt = _unpack_small(o_small[k])
        outs += [n1, o_in[k][None], o_ca[k][None], al, dt, gn, o_out[k][None], n2, o_up[k][None], o_cf[k][None], o_down[k][None], fin]
    return tuple(outs)
```

```python
import functools

import jax
import jax.numpy as jnp
from jax import lax
from jax.experimental import pallas as pl
from jax.experimental.pallas import tpu as pltpu

F32 = jnp.float32
BF16 = jnp.bfloat16

N_DEV = 8
D_MODEL = 1024
GDN_HEADS = 4
GDN_DIM = 128
GDN_WIDTH = GDN_HEADS * GDN_DIM
GDN_CONV = 4
CHUNK = 64
CHUNKS_PER_STEP = 4
DIL_HEADS = 8
DIL_DIM = 64
DIL_WIDTH = DIL_HEADS * DIL_DIM
DIL_PAIRS = DIL_HEADS // 2
DILATIONS = (1, 4, 16)
BAND = 128
D_FF = 2816
FFN_CONV = 3
EPS = 1e-6
A_COLS = 3 * GDN_WIDTH + 128
HALO = 8

ADAM_LR = 0.001
ADAM_B1 = 0.9
ADAM_B2 = 0.999
ADAM_EPS = 1e-08
ADAM_WD = 0.01
ADAM_STEP = 10

VMEM_LIMIT_BYTES = 56 * 1024 * 1024
NEG_BIG = -1e30


def _params(sem=None):
    return pltpu.CompilerParams(dimension_semantics=sem, vmem_limit_bytes=VMEM_LIMIT_BYTES)


def _sds(shape, dtype=F32):
    return jax.ShapeDtypeStruct(shape, dtype)


def _bdot(a, b):
    return jnp.dot(a.astype(BF16), b.astype(BF16), preferred_element_type=F32)


def _bdot_nt(a, b):
    return lax.dot_general(a.astype(BF16), b.astype(BF16), (((1,), (1,)), ((), ())), preferred_element_type=F32)


def _bdot_tn(a, b):
    return lax.dot_general(a.astype(BF16), b.astype(BF16), (((0,), (0,)), ((), ())), preferred_element_type=F32)


def _split(a):
    hi = a.astype(BF16)
    lo = (a - hi.astype(F32)).astype(BF16)
    return hi, lo


def _dot3(a, b, dims):
    ah, al = _split(a)
    bh, bl = _split(b)
    d = functools.partial(lax.dot_general, dimension_numbers=(dims, ((), ())), preferred_element_type=F32)
    return d(ah, bh) + (d(al, bh) + d(ah, bl))


def _exact_tri_dot(tri, g):
    g1 = g.astype(BF16)
    r1 = g - g1.astype(F32)
    g2 = r1.astype(BF16)
    g3 = (r1 - g2.astype(F32)).astype(BF16)
    t = tri.astype(BF16)
    d = functools.partial(jnp.dot, preferred_element_type=F32)
    return d(t, g1) + (d(t, g2) + d(t, g3))


def _sigmoid(x):
    return 1.0 / (1.0 + jnp.exp(-x))


def _dsilu(x, sg):
    return sg * (1.0 + x * (1.0 - sg))


def _rms_bwd_rows(dh, x, w):
    r = lax.rsqrt(jnp.mean(x * x, axis=-1, keepdims=True) + EPS)
    xh = x * r
    gw = dh * w
    return r * (gw - xh * jnp.mean(gw * xh, axis=-1, keepdims=True)), jnp.sum(dh * xh, axis=0, keepdims=True)


def _mm(a, b, *, name, ta=False, tb=False, res=None, norm_bwd=None, after=(), out_dtype=F32, tm=512, tn=512, tk=512):
    if ta:
        K, M = a.shape
    else:
        M, K = a.shape
    if tb:
        N, Kb = b.shape
    else:
        Kb, N = b.shape
    assert K == Kb, (a.shape, b.shape)
    tm, tn, tk = min(tm, M), min(tn, N), min(tk, K)
    assert M % tm == 0 and N % tn == 0 and K % tk == 0, (name, M, N, K, tm, tn, tk)
    nk = K // tk
    dims = (((0 if ta else 1,), (1 if tb else 0,)), ((), ()))
    has_res = res is not None
    has_norm = norm_bwd is not None
    assert not has_norm or tn == N

    def body(*refs):
        a_ref, b_ref = refs[:2]
        r_ref = refs[2] if has_res else None
        if has_norm:
            x_ref, w_ref, skip_ref = refs[2 + has_res:5 + has_res]
            o_ref, dw_ref, acc_ref = refs[-3:]
        else:
            o_ref, acc_ref = refs[-2:]
        i, k = pl.program_id(0), pl.program_id(2)
        part = lax.dot_general(a_ref[...].astype(BF16), b_ref[...].astype(BF16), dims, preferred_element_type=F32)

        @pl.when(k == 0)
        def _():
            acc_ref[...] = part

        @pl.when(k > 0)
        def _():
            acc_ref[...] += part

        @pl.when(k == nk - 1)
        def _():
            r = acc_ref[...]
            if has_res:
                r = r + r_ref[...]
            if has_norm:
                dx, dw = _rms_bwd_rows(r, x_ref[...], w_ref[...])
                o_ref[...] = skip_ref[...] + dx

                @pl.when(i == 0)
                def _():
                    dw_ref[...] = dw

                @pl.when(i > 0)
                def _():
                    dw_ref[...] += dw
            else:
                o_ref[...] = r.astype(out_dtype)

    a_spec = pl.BlockSpec((tk, tm), lambda i, j, k: (k, i)) if ta else pl.BlockSpec((tm, tk), lambda i, j, k: (i, k))
    b_spec = pl.BlockSpec((tn, tk), lambda i, j, k: (j, k)) if tb else pl.BlockSpec((tk, tn), lambda i, j, k: (k, j))
    o_spec = pl.BlockSpec((tm, tn), lambda i, j, k: (i, j))
    one = pl.BlockSpec((1, tn), lambda i, j, k: (0, 0))
    in_specs = [a_spec, b_spec] + [o_spec] * has_res + ([o_spec, one, o_spec] if has_norm else []) + [ANY] * len(after)
    args = (a, b) + ((res,) if has_res else ()) + (tuple(norm_bwd) if has_norm else ()) + tuple(after)
    return pl.pallas_call(
        body, name=name, grid=(M // tm, N // tn, nk), in_specs=in_specs,
        out_specs=[o_spec, one] if has_norm else o_spec,
        out_shape=[_sds((M, N)), _sds((1, N))] if has_norm else _sds((M, N), out_dtype),
        scratch_shapes=[pltpu.VMEM((tm, tn), F32)],
        compiler_params=_params(("arbitrary" if has_norm else "parallel", "parallel", "arbitrary")),
    )(*args)


def _in_proj(x, norm_w, w_a, w_z, w_b, *, name, after=(), tm=512):
    S, D = x.shape
    ws = (w_a, w_z, w_b)

    def body(x_ref, nw_ref, wa_ref, wz_ref, wb_ref, *rest):
        h_ref, pa_ref, pz_ref, pb_ref = rest[len(after):]
        xv = x_ref[...]
        r = lax.rsqrt(jnp.mean(xv * xv, axis=-1, keepdims=True) + EPS)
        h = (xv * r * nw_ref[...]).astype(BF16)
        h_ref[...] = h
        for w_ref, p_ref in ((wa_ref, pa_ref), (wz_ref, pz_ref), (wb_ref, pb_ref)):
            p_ref[...] = lax.dot_general(h, w_ref[...], (((1,), (1,)), ((), ())), preferred_element_type=F32)

    row = lambda n: pl.BlockSpec((tm, n), lambda i: (i, 0))
    full = lambda a: pl.BlockSpec(a.shape, lambda i: (0, 0))
    return pl.pallas_call(
        body, name=name, grid=(S // tm,), in_specs=[row(D), full(norm_w)] + [full(w) for w in ws] + [ANY] * len(after),
        out_specs=[row(D)] + [row(w.shape[0]) for w in ws],
        out_shape=[_sds((S, D), BF16)] + [_sds((S, w.shape[0])) for w in ws], compiler_params=_params(("parallel",)),
    )(x, norm_w, *ws, *after)


def _out_proj_norm(a, w, x, norm_w, *, name, tm=512):
    S, D = x.shape

    def body(a_ref, w_ref, x_ref, nw_ref, x1_ref, h_ref):
        x1 = x_ref[...] + jnp.dot(a_ref[...], w_ref[...], preferred_element_type=F32)
        x1_ref[...] = x1
        r = lax.rsqrt(jnp.mean(x1 * x1, axis=-1, keepdims=True) + EPS)
        h_ref[...] = (x1 * r * nw_ref[...]).astype(BF16)

    row = pl.BlockSpec((tm, D), lambda i: (i, 0))
    return pl.pallas_call(
        body, name=name, grid=(S // tm,),
        in_specs=[pl.BlockSpec((tm, a.shape[1]), lambda i: (i, 0)), pl.BlockSpec(w.shape, lambda i: (0, 0)), row,
                  pl.BlockSpec((1, D), lambda i: (0, 0))],
        out_specs=[row, row], out_shape=[_sds((S, D)), _sds((S, D), BF16)], compiler_params=_params(("parallel",)),
    )(a, w, x, norm_w)


def _shifted(x, start, n):
    aligned = -(-start // HALO) * HALO
    assert aligned + n <= x.shape[0], (start, n, x.shape)
    return (x if aligned == start else pltpu.roll(x, aligned - start, axis=0))[aligned:aligned + n]


def _conv_rows(prev, cur, w, taps):
    n = cur.shape[0]
    xs = jnp.concatenate([prev, cur], axis=0)
    base = HALO - (taps - 1)
    out = _shifted(xs, base, n) * w[0:1]
    for i in range(1, taps):
        out = out + _shifted(xs, base + i, n) * w[i:i + 1]
    return out


def _conv_rows_bwd(cur_d, next_d, prev_x, cur_x, w, taps):
    n = cur_d.shape[0]
    ds = jnp.concatenate([cur_d, next_d], axis=0)
    dx = _shifted(ds, taps - 1, n) * w[0:1]
    for i in range(1, taps):
        dx = dx + _shifted(ds, taps - 1 - i, n) * w[i:i + 1]
    xs = jnp.concatenate([prev_x, cur_x], axis=0)
    base = HALO - (taps - 1)
    dws = [jnp.sum(cur_d * _shifted(xs, base + i, n), axis=0, keepdims=True) for i in range(taps)]
    return dx, jnp.concatenate(dws, axis=0)


def _halo_specs(tm, width, col, nblk):
    per = tm // HALO
    prev = pl.BlockSpec((HALO, width), lambda i, *_: (jnp.maximum(i * per - 1, 0), col))
    nxt = pl.BlockSpec((HALO, width), lambda i, *_: (jnp.minimum((i + 1) * per, nblk * per - 1), col))
    return prev, nxt


def _softplus(x):
    return jnp.maximum(x, 0.0) + jnp.log1p(jnp.exp(-jnp.abs(x)))


def _chunk_tri(tm, upper=False):
    r = lax.broadcasted_iota(jnp.int32, (tm, tm), 0)
    c = lax.broadcasted_iota(jnp.int32, (tm, tm), 1)
    same = lax.div(r, CHUNK) == lax.div(c, CHUNK)
    order = (c >= r) if upper else (c <= r)
    return jnp.where(same & order, 1.0, 0.0)


def _gdn_prep_fwd(proj_a, conv_w, a_log, dt_bias, *, name, tm=256):
    S = proj_a.shape[0]
    nblk = S // tm
    W3 = 3 * GDN_WIDTH

    def body(cur_ref, prev_ref, ba_ref, cw_ref, al_ref, dt_ref, qn_ref, kn_ref, v_ref, gcb_ref, bb_ref):
        i = pl.program_id(0)
        prev = jnp.where(i > 0, prev_ref[...], 0.0)
        c = _conv_rows(prev, cur_ref[...], cw_ref[...], GDN_CONV)
        a = c * _sigmoid(c)
        ba = ba_ref[...]
        lane = lax.broadcasted_iota(jnp.int32, (tm, 128), 1)
        g4 = jnp.zeros((tm, 128), F32)
        for h in range(GDN_HEADS):
            sl = slice(GDN_DIM * h, GDN_DIM * (h + 1))
            qh = a[:, GDN_DIM * h:GDN_DIM * (h + 1)]
            kh = a[:, GDN_WIDTH + GDN_DIM * h:GDN_WIDTH + GDN_DIM * (h + 1)]
            qn_ref[:, sl] = qh * (lax.rsqrt(jnp.sum(qh * qh, axis=-1, keepdims=True) + EPS) * (GDN_DIM ** -0.5))
            kn_ref[:, sl] = kh * lax.rsqrt(jnp.sum(kh * kh, axis=-1, keepdims=True) + EPS)
            beta = _sigmoid(ba[:, h:h + 1])
            bb_ref[:, sl] = jnp.broadcast_to(beta, (tm, GDN_DIM))
            g = -jnp.exp(al_ref[0:1, h:h + 1]) * _softplus(ba[:, GDN_HEADS + h:GDN_HEADS + h + 1] + dt_ref[0:1, h:h + 1])
            g4 = jnp.where(lane == h, g, g4)
        v_ref[...] = a[:, 2 * GDN_WIDTH:]
        gc = _exact_tri_dot(_chunk_tri(tm), g4)
        for h in range(GDN_HEADS):
            gcb_ref[:, GDN_DIM * h:GDN_DIM * (h + 1)] = jnp.broadcast_to(gc[:, h:h + 1], (tm, GDN_DIM))

    prev_spec, _ = _halo_specs(tm, W3, 0, nblk)
    row = pl.BlockSpec((tm, GDN_WIDTH), lambda i: (i, 0))
    small = lambda a: pl.BlockSpec(a.shape, lambda i: (0, 0))
    return pl.pallas_call(
        body, name=name, grid=(nblk,),
        in_specs=[pl.BlockSpec((tm, W3), lambda i: (i, 0)), prev_spec,
                  pl.BlockSpec((tm, 128), lambda i: (i, W3 // 128)), small(conv_w), small(a_log), small(dt_bias)],
        out_specs=[row] * 5, out_shape=[_sds((S, GDN_WIDTH))] * 5, compiler_params=_params(("parallel",)),
    )(proj_a, proj_a, proj_a, conv_w, a_log, dt_bias)


GDN_STACK = GDN_HEADS * CHUNK


def _stack(ref, rows):
    return jnp.concatenate([ref[rows, GDN_DIM * h:GDN_DIM * (h + 1)] for h in range(GDN_HEADS)], axis=0)


def _unstack_to(ref, rows, x):
    for h in range(GDN_HEADS):
        ref[rows, GDN_DIM * h:GDN_DIM * (h + 1)] = x[CHUNK * h:CHUNK * (h + 1)].astype(ref.dtype)


def _stack_masks():
    r = lax.broadcasted_iota(jnp.int32, (GDN_STACK, GDN_STACK), 0)
    c = lax.broadcasted_iota(jnp.int32, (GDN_STACK, GDN_STACK), 1)
    same = (r & -CHUNK) == (c & -CHUNK)
    return same & (r >= c), same & (r > c), r == c


def _stack_decay(gs, bs, incl):
    g2 = jnp.concatenate([gs, gs], axis=1)
    diff = g2 - g2.T
    dec = jnp.where(incl, jnp.exp(jnp.where(incl, diff, 0.0)), 0.0)
    return dec, jnp.concatenate([bs, bs], axis=1).T


def _head_mask():
    r = lax.broadcasted_iota(jnp.int32, (GDN_STACK, GDN_WIDTH), 0)
    c = lax.broadcasted_iota(jnp.int32, (GDN_STACK, GDN_WIDTH), 1)
    return (r & -CHUNK) * (GDN_DIM // CHUNK) == (c & -GDN_DIM)


def _head_spread(x):
    return jnp.where(_head_mask(), jnp.concatenate([x] * GDN_HEADS, axis=1), 0.0)


def _head_diag(x):
    xm = jnp.where(_head_mask(), x, 0.0)
    out = xm[:, 0:GDN_DIM]
    for h in range(1, GDN_HEADS):
        out = out + xm[:, GDN_DIM * h:GDN_DIM * (h + 1)]
    return out


def _last_rows(gs, n):
    return jnp.concatenate([jnp.broadcast_to(gs[CHUNK * (h + 1) - 1:CHUNK * (h + 1)], (n, GDN_DIM)) for h in range(GDN_HEADS)], axis=0)


def _gdn_chunk_fwd(qn, kn, v, gcb, bb, *, name):
    S = qn.shape[0]

    def body(qn_ref, kn_ref, v_ref, gcb_ref, bb_ref, uv_ref, wk_ref, at_ref, t_ref, wkb_ref, qdb_ref, keb_ref):
        incl, strict, diag = _stack_masks()
        for c in range(CHUNKS_PER_STEP):
            rows = slice(CHUNK * c, CHUNK * (c + 1))
            srows = slice(GDN_STACK * c, GDN_STACK * (c + 1))
            q, k, vv, gs, bs = [_stack(r, rows) for r in (qn_ref, kn_ref, v_ref, gcb_ref, bb_ref)]
            dec, bt = _stack_decay(gs, bs, incl)
            p = -jnp.where(strict, dec * _bdot_nt(k, k) * bt, 0.0)
            t = jnp.where(diag, 1.0, 0.0) + p
            for _ in range(5):
                p = _bdot(p, p)
                t = t + _bdot(t, p)
            sol = _dot3(t, jnp.concatenate([vv, jnp.exp(gs) * k], axis=1), ((1,), (0,)))
            _unstack_to(uv_ref, rows, sol[:, :GDN_DIM])
            _unstack_to(wk_ref, rows, sol[:, GDN_DIM:])
            at_ref[srows, :] = dec * _bdot_nt(q, k) * bt
            t_ref[srows, :] = t
            wkb_ref[srows, :] = _head_spread(sol[:, GDN_DIM:]).astype(BF16)
            qdb_ref[srows, :] = _head_spread(q * jnp.exp(gs)).astype(BF16)
            keb_ref[srows, :] = _head_spread(k * jnp.exp(_last_rows(gs, CHUNK) - gs) * bs).astype(BF16)

    step = CHUNKS_PER_STEP * CHUNK
    row = pl.BlockSpec((step, GDN_WIDTH), lambda n: (n, 0))
    sq = pl.BlockSpec((CHUNKS_PER_STEP * GDN_STACK, GDN_STACK), lambda n: (n, 0))
    wide = pl.BlockSpec((CHUNKS_PER_STEP * GDN_STACK, GDN_WIDTH), lambda n: (n, 0))
    nsq = S // CHUNK * GDN_STACK
    return pl.pallas_call(
        body, name=name, grid=(S // step,), in_specs=[row] * 5, out_specs=[row, row, sq, sq, wide, wide, wide],
        out_shape=[_sds((S, GDN_WIDTH)), _sds((S, GDN_WIDTH)), _sds((nsq, GDN_STACK)), _sds((nsq, GDN_STACK))]
        + [_sds((nsq, GDN_WIDTH), BF16)] * 3,
        compiler_params=_params(("parallel",)),
    )(qn, kn, v, gcb, bb)


SCAN_CHUNKS = 8


def _gdn_scan_fwd(uv, at, wkb, qdb, keb, gcb, proj_z, gnw, *, name):
    S = uv.shape[0]
    nc = S // CHUNK

    def body(uv_ref, at_ref, wkb_ref, qdb_ref, keb_ref, gcb_ref, z_ref, gnw_ref, o_ref, u_ref, sp_ref, oa_ref, st_ref):
        n = pl.program_id(0)

        @pl.when(n == 0)
        def _():
            st_ref[...] = jnp.zeros_like(st_ref)

        for c in range(SCAN_CHUNKS):
            rows = slice(CHUNK * c, CHUNK * (c + 1))
            srows = slice(GDN_STACK * c, GDN_STACK * (c + 1))
            st = st_ref[...]
            sp_ref[GDN_WIDTH * c:GDN_WIDTH * (c + 1), :] = st
            uv, gs, z = [_stack(r, rows) for r in (uv_ref, gcb_ref, z_ref)]
            u = uv - _bdot(wkb_ref[srows, :], st)
            o = _bdot(qdb_ref[srows, :], st) + _bdot(at_ref[srows, :], u)
            st_ref[...] = jnp.exp(_last_rows(gs, GDN_DIM)) * st + _bdot_tn(keb_ref[srows, :], u)
            _unstack_to(u_ref, rows, u)
            _unstack_to(o_ref, rows, o)
            r = lax.rsqrt(jnp.mean(o * o, axis=-1, keepdims=True) + EPS)
            oa = o * r * gnw_ref[...] * (z * _sigmoid(z))
            oa_ref[rows, :] = jnp.concatenate([oa[CHUNK * h:CHUNK * (h + 1)] for h in range(GDN_HEADS)], axis=1).astype(BF16)

    row = pl.BlockSpec((SCAN_CHUNKS * CHUNK, GDN_WIDTH), lambda n: (n, 0))
    sq = pl.BlockSpec((SCAN_CHUNKS * GDN_STACK, GDN_STACK), lambda n: (n, 0))
    wide = pl.BlockSpec((SCAN_CHUNKS * GDN_STACK, GDN_WIDTH), lambda n: (n, 0))
    return pl.pallas_call(
        body, name=name, grid=(nc // SCAN_CHUNKS,),
        in_specs=[row, sq, wide, wide, wide, row, row, pl.BlockSpec((1, GDN_DIM), lambda n: (0, 0))],
        out_specs=[row, row, pl.BlockSpec((SCAN_CHUNKS * GDN_WIDTH, GDN_DIM), lambda n: (n, 0)), row],
        out_shape=[_sds((S, GDN_WIDTH)), _sds((S, GDN_WIDTH)), _sds((nc * GDN_WIDTH, GDN_DIM)), _sds((S, 2 * GDN_WIDTH), BF16)],
        scratch_shapes=[pltpu.VMEM((GDN_WIDTH, GDN_DIM), F32)],
        compiler_params=_params(("arbitrary",)),
    )(uv, at, wkb, qdb, keb, gcb, proj_z, gnw)


def _gdn_scan_bwd(d_oab, o, proj_z, gnw, sp, u, at, wkb, qdb, keb, gcb, *, name, after=()):
    S = o.shape[0]
    nc = S // CHUNK
    ns = nc // SCAN_CHUNKS

    def body(do_ref, o_ref, z_ref, gnw_ref, sp_ref, u_ref, at_ref, wkb_ref, qdb_ref, keb_ref, gcb_ref, *rest):
        dz_ref, dgn_ref, du_ref, dwk_ref, dat_ref, dqd_ref, dke_ref, dgl_ref, ds_ref = rest[len(after):]
        n = pl.program_id(0)

        @pl.when(n == 0)
        def _():
            ds_ref[...] = jnp.zeros_like(ds_ref)
            dgn_ref[...] = jnp.zeros_like(dgn_ref)

        gw = gnw_ref[...]
        for c in reversed(range(SCAN_CHUNKS)):
            rows = slice(CHUNK * c, CHUNK * (c + 1))
            srows = slice(GDN_STACK * c, GDN_STACK * (c + 1))
            d_oa, oo, z, uu, gs = [_stack(r, rows) for r in (do_ref, o_ref, z_ref, u_ref, gcb_ref)]
            sg = _sigmoid(z)
            r = lax.rsqrt(jnp.mean(oo * oo, axis=-1, keepdims=True) + EPS)
            xh = oo * r
            dy = d_oa * (z * sg)
            _unstack_to(dz_ref, rows, d_oa * (xh * gw) * _dsilu(z, sg))
            dgn_ref[...] += jnp.sum(dy * xh, axis=0, keepdims=True)
            dxh = dy * gw
            do = r * (dxh - xh * jnp.mean(dxh * xh, axis=-1, keepdims=True))

            st = sp_ref[GDN_WIDTH * c:GDN_WIDTH * (c + 1), :]
            dst = ds_ref[...]
            ge = jnp.exp(_last_rows(gs, GDN_DIM))
            _unstack_to(dqd_ref, rows, _head_diag(_bdot_nt(do, st)))
            dat_ref[srows, :] = _bdot_nt(do, uu)
            du = _bdot_tn(at_ref[srows, :], do) + _bdot(keb_ref[srows, :], dst)
            _unstack_to(dke_ref, rows, _head_diag(_bdot_nt(uu, dst)))
            prod = dst * st
            for h in range(GDN_HEADS):
                blk = prod[GDN_DIM * h:GDN_DIM * (h + 1)]
                dge = jnp.sum(jnp.sum(blk, axis=1, keepdims=True), axis=0, keepdims=True)
                dgl_ref[c, :, GDN_DIM * h:GDN_DIM * (h + 1)] = jnp.broadcast_to(dge * ge[GDN_DIM * h:GDN_DIM * h + 1], (8, GDN_DIM))
            ds_ref[...] = _bdot_tn(qdb_ref[srows, :], do) + ge * dst - _bdot_tn(wkb_ref[srows, :], du)
            _unstack_to(du_ref, rows, du)
            _unstack_to(dwk_ref, rows, -_head_diag(_bdot_nt(du, st)))

    rev = lambda n: (ns - 1 - n, 0)
    row = pl.BlockSpec((SCAN_CHUNKS * CHUNK, GDN_WIDTH), rev)
    sq = pl.BlockSpec((SCAN_CHUNKS * GDN_STACK, GDN_STACK), rev)
    wide = pl.BlockSpec((SCAN_CHUNKS * GDN_STACK, GDN_WIDTH), rev)
    one = pl.BlockSpec((1, GDN_DIM), lambda n: (0, 0))
    return pl.pallas_call(
        body, name=name, grid=(ns,),
        in_specs=[row, row, row, one, pl.BlockSpec((SCAN_CHUNKS * GDN_WIDTH, GDN_DIM), rev), row, sq, wide, wide, wide, row]
        + [ANY] * len(after),
        out_specs=[row, one, row, row, sq, row, row, pl.BlockSpec((SCAN_CHUNKS, 8, GDN_WIDTH), lambda n: (ns - 1 - n, 0, 0))],
        out_shape=[_sds((S, GDN_WIDTH), BF16), _sds((1, GDN_DIM)), _sds((S, GDN_WIDTH)), _sds((S, GDN_WIDTH)),
                   _sds((nc * GDN_STACK, GDN_STACK)), _sds((S, GDN_WIDTH)), _sds((S, GDN_WIDTH)), _sds((nc, 8, GDN_WIDTH))],
        scratch_shapes=[pltpu.VMEM((GDN_WIDTH, GDN_DIM), F32)],
        compiler_params=_params(("arbitrary",)),
    )(d_oab, o, proj_z, gnw, sp, u, at, wkb, qdb, keb, gcb, *after)


def _gdn_chunk_bwd(qn, kn, gcb, bb, tmat, uv, wk, du, dwk, dat, dqd, dke, dgl, *, name):
    S = qn.shape[0]

    def body(qn_ref, kn_ref, gcb_ref, bb_ref, t_ref, uv_ref, wk_ref, du_ref, dwk_ref, dat_ref, dqd_ref, dke_ref,
             dgl_ref, dq_ref, dk_ref, dv_ref, dg_ref, dbeta_ref):
        incl, strict, _ = _stack_masks()
        lane = lax.broadcasted_iota(jnp.int32, (CHUNK, 128), 1)
        rowi = lax.broadcasted_iota(jnp.int32, (CHUNK, 1), 0)
        rsum = lambda x: jnp.sum(x, axis=-1, keepdims=True)
        for c in range(CHUNKS_PER_STEP):
            rows = slice(CHUNK * c, CHUNK * (c + 1))
            srows = slice(GDN_STACK * c, GDN_STACK * (c + 1))
            q, k, gs, bs, uv, wk, du, dwk, dqd, dke = [
                _stack(r, rows) for r in (qn_ref, kn_ref, gcb_ref, bb_ref, uv_ref, wk_ref, du_ref, dwk_ref, dqd_ref, dke_ref)]
            dec, bt = _stack_decay(gs, bs, incl)
            kk = _bdot_nt(k, k)
            qk = _bdot_nt(q, k)
            d_rhs = _dot3(t_ref[srows, :], jnp.concatenate([du, dwk], axis=1), ((0,), (0,)))
            sol = jnp.concatenate([uv, wk], axis=1)
            d_l = jnp.where(strict, -_dot3(d_rhs, sol, ((1,), (1,))), 0.0)
            d_a = jnp.where(incl, dat_ref[srows, :], 0.0)
            gam = jnp.exp(gs)
            e = jnp.exp(_last_rows(gs, CHUNK) - gs)
            d_gk = d_rhs[:, GDN_DIM:]
            ml = d_l * dec * bt
            ma = d_a * dec * bt
            _unstack_to(dq_ref, rows, _bdot(ma, k) + dqd * gam)
            _unstack_to(dk_ref, rows, _bdot(ml + ml.T, k) + _bdot_tn(ma, q) + d_gk * gam + dke * (e * bs))
            _unstack_to(dv_ref, rows, d_rhs[:, :GDN_DIM])
            wb = d_l * dec * kk + d_a * dec * qk
            ew = wb * bt
            s_ke = rsum(dke * k * (e * bs))
            dbeta = rsum(wb.T) + rsum(dke * k * e)
            dgc = rsum(ew) - rsum(ew.T) + rsum(dqd * q * gam) + rsum(d_gk * k * gam) - s_ke
            dgc4 = jnp.zeros((CHUNK, 128), F32)
            db4 = jnp.zeros((CHUNK, 128), F32)
            for h in range(GDN_HEADS):
                hr = slice(CHUNK * h, CHUNK * (h + 1))
                tail = jnp.sum(s_ke[hr], axis=0, keepdims=True) + dgl_ref[c, 0:1, GDN_DIM * h:GDN_DIM * h + 1]
                dgc4 = jnp.where(lane == h, dgc[hr] + jnp.where(rowi == CHUNK - 1, tail, 0.0), dgc4)
                db4 = jnp.where(lane == h, dbeta[hr], db4)
            dg_ref[rows, :] = _exact_tri_dot(_chunk_tri(CHUNK, upper=True), dgc4)
            dbeta_ref[rows, :] = db4

    step = CHUNKS_PER_STEP * CHUNK
    row = pl.BlockSpec((step, GDN_WIDTH), lambda n: (n, 0))
    sq = pl.BlockSpec((CHUNKS_PER_STEP * GDN_STACK, GDN_STACK), lambda n: (n, 0))
    col = pl.BlockSpec((step, 128), lambda n: (n, 0))
    return pl.pallas_call(
        body, name=name, grid=(S // step,),
        in_specs=[row] * 4 + [sq, row, row, row, row, sq, row, row,
                              pl.BlockSpec((CHUNKS_PER_STEP, 8, GDN_WIDTH), lambda n: (n, 0, 0))],
        out_specs=[row, row, row, col, col],
        out_shape=[_sds((S, GDN_WIDTH))] * 3 + [_sds((S, 128))] * 2, compiler_params=_params(("parallel",)),
    )(qn, kn, gcb, bb, tmat, uv, wk, du, dwk, dat, dqd, dke, dgl)


def _gdn_prep_bwd(dqn, dkn, dv, dg, dbeta, proj_a, conv_w, a_log, dt_bias, *, name, tm=256):
    S = proj_a.shape[0]
    nblk = S // tm
    W3 = 3 * GDN_WIDTH

    def body(dqn_ref, dkn_ref, dv_ref, dg_ref, dbeta_ref, cur_ref, prev_ref, ba_ref, cw_ref, al_ref, dt_ref,
             dc_ref, dba_ref, sm_ref):
        i = pl.program_id(0)
        prev = jnp.where(i > 0, prev_ref[...], 0.0)
        c = _conv_rows(prev, cur_ref[...], cw_ref[...], GDN_CONV)
        sg = _sigmoid(c)
        a = c * sg
        dsl = _dsilu(c, sg)
        ba = ba_ref[...]
        lane = lax.broadcasted_iota(jnp.int32, (tm, 128), 1)
        lane1 = lax.broadcasted_iota(jnp.int32, (1, 128), 1)
        dba = jnp.zeros((tm, 128), F32)
        sm = jnp.zeros((1, 128), F32)
        for h in range(GDN_HEADS):
            sl = slice(GDN_DIM * h, GDN_DIM * (h + 1))
            ks = slice(GDN_WIDTH + GDN_DIM * h, GDN_WIDTH + GDN_DIM * (h + 1))
            qh, kh = a[:, sl], a[:, ks]
            rq = lax.rsqrt(jnp.sum(qh * qh, axis=-1, keepdims=True) + EPS)
            rk = lax.rsqrt(jnp.sum(kh * kh, axis=-1, keepdims=True) + EPS)
            qhat, khat = qh * rq, kh * rk
            dyq = dqn_ref[:, sl] * (GDN_DIM ** -0.5)
            dyk = dkn_ref[:, sl]
            dq = rq * (dyq - qhat * jnp.sum(dyq * qhat, axis=-1, keepdims=True))
            dk = rk * (dyk - khat * jnp.sum(dyk * khat, axis=-1, keepdims=True))
            dc_ref[:, sl] = dq * dsl[:, sl]
            dc_ref[:, ks] = dk * dsl[:, ks]
            beta = _sigmoid(ba[:, h:h + 1])
            db = dbeta_ref[:, h:h + 1] * beta * (1.0 - beta)
            aneg = -jnp.exp(al_ref[0:1, h:h + 1])
            xa = ba[:, GDN_HEADS + h:GDN_HEADS + h + 1] + dt_ref[0:1, h:h + 1]
            dgh = dg_ref[:, h:h + 1]
            dxa = dgh * aneg * _sigmoid(xa)
            dba = jnp.where(lane == h, db, dba)
            dba = jnp.where(lane == GDN_HEADS + h, dxa, dba)
            d_alog = jnp.sum(dgh * _softplus(xa), axis=0, keepdims=True) * aneg
            sm = jnp.where(lane1 == h, d_alog, sm)
            sm = jnp.where(lane1 == GDN_HEADS + h, jnp.sum(dxa, axis=0, keepdims=True), sm)
        vs = slice(2 * GDN_WIDTH, W3)
        dc_ref[:, vs] = dv_ref[...] * dsl[:, vs]
        dba_ref[...] = dba

        @pl.when(i == 0)
        def _():
            sm_ref[...] = sm

        @pl.when(i > 0)
        def _():
            sm_ref[...] += sm

    prev_spec, _ = _halo_specs(tm, W3, 0, nblk)
    row = pl.BlockSpec((tm, GDN_WIDTH), lambda i: (i, 0))
    col = pl.BlockSpec((tm, 128), lambda i: (i, 0))
    small = lambda a: pl.BlockSpec(a.shape, lambda i: (0, 0))
    return pl.pallas_call(
        body, name=name, grid=(nblk,),
        in_specs=[row, row, row, col, col, pl.BlockSpec((tm, W3), lambda i: (i, 0)), prev_spec,
                  pl.BlockSpec((tm, 128), lambda i: (i, W3 // 128)), small(conv_w), small(a_log), small(dt_bias)],
        out_specs=[pl.BlockSpec((tm, W3), lambda i: (i, 0)), col, pl.BlockSpec((1, 128), lambda i: (0, 0))],
        out_shape=[_sds((S, W3)), _sds((S, 128)), _sds((1, 128))], compiler_params=_params(("arbitrary",)),
    )(dqn, dkn, dv, dg, dbeta, proj_a, proj_a, proj_a, conv_w, a_log, dt_bias)


def _gdn_conv_bwd(dc, dba, proj_a, conv_w, *, name, tm=256):
    S = proj_a.shape[0]
    nblk = S // tm
    W3 = 3 * GDN_WIDTH

    def body(dc_ref, dnext_ref, dba_ref, cur_ref, prev_ref, cw_ref, da_ref, dcw_ref):
        i = pl.program_id(0)
        prev = jnp.where(i > 0, prev_ref[...], 0.0)
        nxt = jnp.where(i < nblk - 1, dnext_ref[...], 0.0)
        dx, dw = _conv_rows_bwd(dc_ref[...], nxt, prev, cur_ref[...], cw_ref[...], GDN_CONV)
        da_ref[:, 0:W3] = dx.astype(BF16)
        da_ref[:, W3:] = dba_ref[...].astype(BF16)

        @pl.when(i == 0)
        def _():
            dcw_ref[...] = dw

        @pl.when(i > 0)
        def _():
            dcw_ref[...] += dw

    prev_spec, next_spec = _halo_specs(tm, W3, 0, nblk)
    wide = pl.BlockSpec((tm, W3), lambda i: (i, 0))
    return pl.pallas_call(
        body, name=name, grid=(nblk,),
        in_specs=[wide, next_spec, pl.BlockSpec((tm, 128), lambda i: (i, 0)), wide, prev_spec,
                  pl.BlockSpec(conv_w.shape, lambda i: (0, 0))],
        out_specs=[pl.BlockSpec((tm, A_COLS), lambda i: (i, 0)), pl.BlockSpec(conv_w.shape, lambda i: (0, 0))],
        out_shape=[_sds((S, A_COLS), BF16), _sds(conv_w.shape)], compiler_params=_params(("arbitrary",)),
    )(dc, dc, dba, proj_a, proj_a, conv_w)


def _band_mask(nk):
    i = lax.broadcasted_iota(jnp.int32, (2 * BAND, nk), 0) & (BAND - 1)
    j = lax.broadcasted_iota(jnp.int32, (2 * BAND, nk), 1)
    if nk == BAND:
        return j <= i
    return (j >= i) & (j <= i + BAND)


def _stack_heads(x, lo):
    return jnp.concatenate([jnp.where(lo, x, 0.0), jnp.where(lo, 0.0, x)], axis=0)


def _stack_cols(x):
    return jnp.concatenate([x[:, 0:1], x[:, DIL_DIM:DIL_DIM + 1]], axis=0)


def _unstack(x, lo):
    return jnp.where(lo, x[0:BAND], x[BAND:2 * BAND])


def _rows(start, size, stride):
    return pl.ds(start, size) if stride == 1 else pl.ds(start, size, stride=stride)


ATTN_LANES = 4


def _attn_blocks(S, visit_many, lanes=ATTN_LANES):
    for d in DILATIONS:
        nb = S // (d * BAND)
        if d == 1:
            half = nb // 2
            visit_many(d, [(0, 0, True), (0, half, False)])

            def pair(n, c):
                visit_many(1, [(0, n, False), (0, n + half, False)])
                return c
            lax.fori_loop(1, half, pair, 0)
        elif nb > 1:
            for r0 in range(0, d, lanes):
                visit_many(d, [(r0 + t, 0, True) for t in range(lanes)])

                def column(n, c, d=d, r0=r0):
                    visit_many(d, [(r0 + t, n, False) for t in range(lanes)])
                    return c
                lax.fori_loop(1, nb, column, 0)
        else:
            def group(g, c, d=d):
                visit_many(d, [(g * lanes + t, 0, True) for t in range(lanes)])
                return c
            lax.fori_loop(0, d // lanes, group, 0)


def _attn_fwd(proj_b, oab, *, name):
    S = proj_b.shape[0]
    scale = DIL_DIM ** -0.5

    def body(q_ref, k_ref, v_ref, oab_in_ref, ob_ref, lse_ref, m_ref, l_ref, acc_ref):
        del oab_in_ref
        lane = lax.broadcasted_iota(jnp.int32, (BAND, 128), 1)
        lo = lane < DIL_DIM
        m_ref[...] = jnp.full_like(m_ref, NEG_BIG)
        l_ref[...] = jnp.zeros_like(l_ref)
        acc_ref[...] = jnp.zeros_like(acc_ref)

        def load(d, r, n, first):
            nk = BAND if first else 2 * BAND
            qrows = _rows(r + n * (BAND * d), BAND, d)
            krows = _rows(r if first else r + (n - 1) * (BAND * d), nk, d)
            return dict(nk=nk, qrows=qrows, q=q_ref[qrows, :] * scale, k=k_ref[krows, :].astype(BF16),
                        v=v_ref[krows, :].astype(BF16), m=m_ref[qrows, :], l=l_ref[qrows, :], acc=acc_ref[qrows, :])

        def compute(b):
            q, k, v = b["q"], b["k"], b["v"]
            s = jnp.where(_band_mask(b["nk"]), _bdot_nt(_stack_heads(q, lo), k), NEG_BIG)
            m_old = _stack_cols(b["m"])
            m_new = jnp.maximum(m_old, jnp.max(s, axis=-1, keepdims=True))
            p = jnp.exp(s - m_new)
            alpha = _unstack(jnp.exp(m_old - m_new), lo)
            l_new = alpha * b["l"] + _unstack(jnp.sum(p, axis=-1, keepdims=True), lo)
            return _unstack(m_new, lo), l_new, alpha * b["acc"] + _unstack(_bdot(p, v), lo)

        def visit_many(d, blocks):
            loaded = [load(d, *blk) for blk in blocks]
            done = [compute(b) for b in loaded]
            for b, (m_new, l_new, acc_new) in zip(loaded, done):
                m_ref[b["qrows"], :] = m_new
                l_ref[b["qrows"], :] = l_new
                acc_ref[b["qrows"], :] = acc_new

        _attn_blocks(S, visit_many)
        ob_ref[...] = (acc_ref[...] / l_ref[...]).astype(BF16)
        lse_ref[...] = m_ref[...] + jnp.log(l_ref[...])

    part = lambda t: pl.BlockSpec((S, 128), lambda p: (0, 3 * p + t))
    return pl.pallas_call(
        body, name=name, grid=(DIL_PAIRS,),
        in_specs=[part(0), part(1), part(2), pl.BlockSpec(memory_space=pl.ANY)],
        out_specs=[pl.BlockSpec((S, 128), lambda p: (0, GDN_WIDTH // 128 + p)), pl.BlockSpec((S, 128), lambda p: (0, p))],
        out_shape=[_sds(oab.shape, BF16), _sds((S, DIL_WIDTH))],
        scratch_shapes=[pltpu.VMEM((S, 128), F32)] * 3, input_output_aliases={3: 0},
        compiler_params=_params(("parallel",)),
    )(proj_b, proj_b, proj_b, oab)


def _attn_bwd(proj_b, oab, d_oab, lse, *, name):
    S = proj_b.shape[0]
    scale = DIL_DIM ** -0.5

    def body(q_ref, k_ref, v_ref, o_ref, do_ref, lse_ref, dqkv_ref, dq_ref, dk_ref, dv_ref, delta_ref):
        lane = lax.broadcasted_iota(jnp.int32, (BAND, 128), 1)
        lo = lane < DIL_DIM
        dq_ref[...] = jnp.zeros_like(dq_ref)
        dk_ref[...] = jnp.zeros_like(dk_ref)
        dv_ref[...] = jnp.zeros_like(dv_ref)
        prod = do_ref[...] * o_ref[...].astype(F32)
        lo_all = lax.broadcasted_iota(jnp.int32, (S, 128), 1) < DIL_DIM
        delta_ref[...] = jnp.where(lo_all, jnp.sum(jnp.where(lo_all, prod, 0.0), axis=-1, keepdims=True),
                                   jnp.sum(jnp.where(lo_all, 0.0, prod), axis=-1, keepdims=True))

        def load(d, r, n, first):
            nk = BAND if first else 2 * BAND
            qrows = _rows(r + n * (BAND * d), BAND, d)
            krows = _rows(r if first else r + (n - 1) * (BAND * d), nk, d)
            return dict(nk=nk, qrows=qrows, krows=krows, q=q_ref[qrows, :] * scale, k=k_ref[krows, :], v=v_ref[krows, :],
                        do=do_ref[qrows, :], delta=delta_ref[qrows, :], lse=lse_ref[qrows, :],
                        dq=dq_ref[qrows, :], dk=dk_ref[krows, :], dv=dv_ref[krows, :])

        def compute(b):
            q, k, v, do = b["q"], b["k"], b["v"], b["do"]
            qs, dos = _stack_heads(q, lo), _stack_heads(do, lo)
            p = jnp.where(_band_mask(b["nk"]), jnp.exp(_bdot_nt(qs, k) - _stack_cols(b["lse"])), 0.0)
            ds = p * (_bdot_nt(dos, v) - _stack_cols(b["delta"]))
            dq = b["dq"] + _unstack(_bdot(ds, k), lo) * scale
            return dq, b["dk"] + _bdot_tn(ds, qs), b["dv"] + _bdot_tn(p, dos)

        def visit_many(d, blocks):
            loaded = [load(d, *blk) for blk in blocks]
            done = [compute(b) for b in loaded]
            for b, (dq, dk, dv) in zip(loaded, done):
                dq_ref[b["qrows"], :] = dq
                dk_ref[b["krows"], :] = dk
                dv_ref[b["krows"], :] = dv

        _attn_blocks(S, visit_many, lanes=2)
        dqkv_ref[:, 0:128] = dq_ref[...].astype(BF16)
        dqkv_ref[:, 128:256] = dk_ref[...].astype(BF16)
        dqkv_ref[:, 256:384] = dv_ref[...].astype(BF16)

    half = lambda p: (0, GDN_WIDTH // 128 + p)
    part = lambda t: pl.BlockSpec((S, 128), lambda p: (0, 3 * p + t))
    return pl.pallas_call(
        body, name=name, grid=(DIL_PAIRS,),
        in_specs=[part(0), part(1), part(2), pl.BlockSpec((S, 128), half), pl.BlockSpec((S, 128), half),
                  pl.BlockSpec((S, 128), lambda p: (0, p))],
        out_specs=pl.BlockSpec((S, 384), lambda p: (0, p)), out_shape=_sds((S, 3 * DIL_WIDTH), BF16),
        scratch_shapes=[pltpu.VMEM((S, 128), F32)] * 4, compiler_params=_params(("parallel",)),
    )(proj_b, proj_b, proj_b, oab, d_oab, lse)


FF_SLAB = 2 * D_FF // N_DEV
FF_PAIRS = N_DEV // 2
ROWS16 = 16


def _taps(w, x, base, n):
    out = _shifted(x, base, n) * w[0:1]
    for t in range(1, FFN_CONV):
        out = out + _shifted(x, base + t, n) * w[t:t + 1]
    return out


def _ffn_fwd(h2, x1, w_up, conv_w, w_down, final_w, tgt, *, name, tm=512):
    S, D = h2.shape
    ni = S // tm
    per = tm // ROWS16

    def body(h_ref, hp_ref, x1_ref, wg_ref, wu_ref, cg_ref, cu_ref, wd_ref, fw_ref, t_ref,
             dx_ref, dxb_ref, dfw_ref, loss_ref, ug_ref, uu_ref, x2_ref):
        i, j = pl.program_id(0), pl.program_id(1)
        hv = jnp.concatenate([hp_ref[...], h_ref[...]], axis=0)
        row = lax.broadcasted_iota(jnp.int32, (tm + ROWS16, 1), 0)
        keep = (i > 0) | (row >= ROWS16)

        def branch(w_ref, c_ref, u_ref):
            u = lax.dot_general(hv, w_ref[...], (((1,), (1,)), ((), ())), preferred_element_type=F32).astype(BF16)
            u_ref[...] = u[ROWS16:]
            return _taps(c_ref[...], jnp.where(keep, u.astype(F32), 0.0), ROWS16 - (FFN_CONV - 1), tm)

        gate = branch(wg_ref, cg_ref, ug_ref)
        up = branch(wu_ref, cu_ref, uu_ref)
        act = (gate * _sigmoid(gate) * up).astype(BF16)
        part = jnp.dot(act, wd_ref[...], preferred_element_type=F32)

        @pl.when(j == 0)
        def _():
            x2_ref[...] = x1_ref[...] + part

        @pl.when((j > 0) & (j < FF_PAIRS - 1))
        def _():
            x2_ref[...] += part

        @pl.when(j == FF_PAIRS - 1)
        def _():
            xv = x2_ref[...] + part
            wv = fw_ref[...]
            r = lax.rsqrt(jnp.mean(xv * xv, axis=-1, keepdims=True) + EPS)
            err = xv * r * wv - t_ref[...]
            lsum = jnp.sum(jnp.sum(err * err, axis=-1, keepdims=True), axis=0, keepdims=True) * (0.5 / D)
            g = err * (1.0 / D)
            xh = xv * r
            gw = g * wv
            dx = r * (gw - xh * jnp.mean(gw * xh, axis=-1, keepdims=True))
            dx_ref[...] = dx
            dxb_ref[...] = dx.astype(BF16)
            dfw = jnp.sum(g * xh, axis=0, keepdims=True)
            lpart = jnp.broadcast_to(lsum, (1, 128))

            @pl.when(i == 0)
            def _():
                dfw_ref[...] = dfw
                loss_ref[...] = lpart

            @pl.when(i > 0)
            def _():
                dfw_ref[...] += dfw
                loss_ref[...] += lpart

    rows = pl.BlockSpec((tm, D), lambda i, j: (i, 0))
    slab = lambda off: pl.BlockSpec((None, FF_SLAB, D), lambda i, j: (j + off, 0, 0))
    cslab = lambda off: pl.BlockSpec((None, FFN_CONV, FF_SLAB), lambda i, j: (j + off, 0, 0))
    uspec = pl.BlockSpec((None, tm, FF_SLAB), lambda i, j: (j, i, 0))
    return pl.pallas_call(
        body, name=name, grid=(ni, FF_PAIRS),
        in_specs=[rows, pl.BlockSpec((ROWS16, D), lambda i, j: (jnp.maximum(i * per - 1, 0), 0)), rows,
                  slab(0), slab(FF_PAIRS), cslab(0), cslab(FF_PAIRS), pl.BlockSpec((FF_SLAB, D), lambda i, j: (j, 0)),
                  pl.BlockSpec((1, D), lambda i, j: (0, 0)), rows],
        out_specs=[rows, rows, pl.BlockSpec((1, D), lambda i, j: (0, 0)), pl.BlockSpec((1, 128), lambda i, j: (0, 0)), uspec, uspec],
        out_shape=[_sds((S, D)), _sds((S, D), BF16), _sds((1, D)), _sds((1, 128)),
                   _sds((FF_PAIRS, S, FF_SLAB), BF16), _sds((FF_PAIRS, S, FF_SLAB), BF16)],
        scratch_shapes=[pltpu.VMEM((tm, D), F32)],
        compiler_params=_params(("arbitrary", "arbitrary")),
    )(h2, h2, x1, w_up, w_up, conv_w, conv_w, w_down, final_w, tgt)


def _ffn_bwd(dx2, h2, ug, uu, conv_w, w_down, *, name, tm=512):
    S, D = h2.shape
    ni = S // tm
    per = tm // ROWS16
    ext = tm + ROWS16

    def body(dx_ref, dxn_ref, h_ref, ug_ref, ugp_ref, ugn_ref, uu_ref, uup_ref, uun_ref, cg_ref, cu_ref, wd_ref,
             dug_ref, duu_ref, gd_ref, gg_ref, gu_ref, dcg_ref, dcu_ref, acc_d, acc_g, acc_u, acc_cg, acc_cu):
        i = pl.program_id(1)

        @pl.when(i == 0)
        def _():
            acc_d[...] = jnp.zeros_like(acc_d)
            acc_g[...] = jnp.zeros_like(acc_g)
            acc_u[...] = jnp.zeros_like(acc_u)
            acc_cg[...] = jnp.zeros_like(acc_cg)
            acc_cu[...] = jnp.zeros_like(acc_cu)

        dx = dx_ref[...]
        dxe = jnp.concatenate([dx, dxn_ref[...]], axis=0)
        row = lax.broadcasted_iota(jnp.int32, (ext, 1), 0)
        live = (i < ni - 1) | (row < tm)
        d_act = jnp.where(live, lax.dot_general(dxe, wd_ref[...], (((1,), (1,)), ((), ())), preferred_element_type=F32), 0.0)
        rowp = lax.broadcasted_iota(jnp.int32, (ext + ROWS16, 1), 0)
        keep = (i > 0) | (rowp >= ROWS16)

        def pre(cur, prev, nxt):
            return jnp.where(keep, jnp.concatenate([prev[...], cur[...], nxt[...]], axis=0).astype(F32), 0.0)

        uge, uue = pre(ug_ref, ugp_ref, ugn_ref), pre(uu_ref, uup_ref, uun_ref)
        cg, cu = cg_ref[...], cu_ref[...]
        base = ROWS16 - (FFN_CONV - 1)
        gate = _taps(cg, uge, base, ext)
        up = _taps(cu, uue, base, ext)
        sg = _sigmoid(gate)
        silu = gate * sg
        dgc = d_act * up * _dsilu(gate, sg)
        duc = d_act * silu

        def conv_t(w, dc):
            out = _shifted(dc, FFN_CONV - 1, tm) * w[0:1]
            for t in range(1, FFN_CONV):
                out = out + _shifted(dc, FFN_CONV - 1 - t, tm) * w[t:t + 1]
            return out.astype(BF16)

        du_g, du_u = conv_t(cg, dgc), conv_t(cu, duc)
        dug_ref[...] = du_g
        duu_ref[...] = du_u
        dcw = lambda dc, xe: jnp.concatenate(
            [jnp.sum(dc[0:tm] * _shifted(xe, base + t, tm), axis=0, keepdims=True) for t in range(FFN_CONV)], axis=0)
        acc_cg[0:FFN_CONV, :] += dcw(dgc, uge)
        acc_cu[0:FFN_CONV, :] += dcw(duc, uue)
        tn = (((0,), (0,)), ((), ()))
        act = (silu[0:tm] * up[0:tm]).astype(BF16)
        acc_d[...] += lax.dot_general(act, dx, tn, preferred_element_type=F32)
        hv = h_ref[...]
        acc_g[...] += lax.dot_general(du_g, hv, tn, preferred_element_type=F32)
        acc_u[...] += lax.dot_general(du_u, hv, tn, preferred_element_type=F32)

        @pl.when(i == ni - 1)
        def _():
            gd_ref[...] = acc_d[...].astype(BF16)
            gg_ref[...] = acc_g[...].astype(BF16)
            gu_ref[...] = acc_u[...].astype(BF16)
            dcg_ref[...] = acc_cg[0:FFN_CONV, :]
            dcu_ref[...] = acc_cu[0:FFN_CONV, :]

    last16 = S // ROWS16 - 1
    rows = pl.BlockSpec((tm, D), lambda j, i: (i, 0))
    rows_next = pl.BlockSpec((ROWS16, D), lambda j, i: (jnp.minimum((i + 1) * per, last16), 0))
    u_cur = pl.BlockSpec((None, tm, FF_SLAB), lambda j, i: (j, i, 0))
    u_prev = pl.BlockSpec((None, ROWS16, FF_SLAB), lambda j, i: (j, jnp.maximum(i * per - 1, 0), 0))
    u_next = pl.BlockSpec((None, ROWS16, FF_SLAB), lambda j, i: (j, jnp.minimum((i + 1) * per, last16), 0))
    cslab = lambda off: pl.BlockSpec((None, FFN_CONV, FF_SLAB), lambda j, i: (j + off, 0, 0))
    wslab = pl.BlockSpec((None, FF_SLAB, D), lambda j, i: (j, 0, 0))
    dslab = pl.BlockSpec((None, FFN_CONV, FF_SLAB), lambda j, i: (j, 0, 0))
    return pl.pallas_call(
        body, name=name, grid=(FF_PAIRS, ni),
        in_specs=[rows, rows_next, rows, u_cur, u_prev, u_next, u_cur, u_prev, u_next, cslab(0), cslab(FF_PAIRS),
                  pl.BlockSpec((FF_SLAB, D), lambda j, i: (j, 0))],
        out_specs=[u_cur, u_cur, pl.BlockSpec((FF_SLAB, D), lambda j, i: (j, 0)), wslab, wslab, dslab, dslab],
        out_shape=[_sds((FF_PAIRS, S, FF_SLAB), BF16), _sds((FF_PAIRS, S, FF_SLAB), BF16), _sds((D_FF, D), BF16),
                   _sds((FF_PAIRS, FF_SLAB, D), BF16), _sds((FF_PAIRS, FF_SLAB, D), BF16),
                   _sds((FF_PAIRS, FFN_CONV, FF_SLAB)), _sds((FF_PAIRS, FFN_CONV, FF_SLAB))],
        scratch_shapes=[pltpu.VMEM((FF_SLAB, D), F32), pltpu.VMEM((FF_SLAB, D), F32), pltpu.VMEM((FF_SLAB, D), F32),
                        pltpu.VMEM((8, FF_SLAB), F32), pltpu.VMEM((8, FF_SLAB), F32)],
        compiler_params=_params(("parallel", "arbitrary")),
    )(dx2, dx2, h2, ug, ug, ug, uu, uu, uu, conv_w, conv_w, w_down)


def _mm_slabs(a, w, w_off, *, name, res=None, norm_bwd=None, after=(), tm=1024, tn=1024):
    nk, S, _ = a.shape
    D = w.shape[2]
    has_res = res is not None
    has_norm = norm_bwd is not None
    assert not has_norm or tn == D

    def body(*refs):
        a_ref, w_ref = refs[:2]
        r_ref = refs[2] if has_res else None
        if has_norm:
            x_ref, nw_ref, skip_ref = refs[2 + has_res:5 + has_res]
            o_ref, dw_ref, acc_ref = refs[-3:]
        else:
            o_ref, acc_ref = refs[-2:]
        i, k = pl.program_id(0), pl.program_id(2)
        part = jnp.dot(a_ref[...], w_ref[...], preferred_element_type=F32)

        @pl.when(k == 0)
        def _():
            acc_ref[...] = part

        @pl.when(k > 0)
        def _():
            acc_ref[...] += part

        @pl.when(k == nk - 1)
        def _():
            r = acc_ref[...] + r_ref[...] if has_res else acc_ref[...]
            if has_norm:
                dx, dw = _rms_bwd_rows(r, x_ref[...], nw_ref[...])
                o_ref[...] = skip_ref[...] + dx

                @pl.when(i == 0)
                def _():
                    dw_ref[...] = dw

                @pl.when(i > 0)
                def _():
                    dw_ref[...] += dw
            else:
                o_ref[...] = r

    o_spec = pl.BlockSpec((tm, tn), lambda i, j, k: (i, j))
    one = pl.BlockSpec((1, tn), lambda i, j, k: (0, 0))
    return pl.pallas_call(
        body, name=name, grid=(S // tm, D // tn, nk),
        in_specs=[pl.BlockSpec((None, tm, FF_SLAB), lambda i, j, k: (k, i, 0)),
                  pl.BlockSpec((None, FF_SLAB, tn), lambda i, j, k: (k + w_off, 0, j))] + [o_spec] * has_res
        + ([o_spec, one, o_spec] if has_norm else []) + [ANY] * len(after),
        out_specs=[o_spec, one] if has_norm else o_spec, out_shape=[_sds((S, D)), _sds((1, D))] if has_norm else _sds((S, D)),
        scratch_shapes=[pltpu.VMEM((tm, tn), F32)],
        compiler_params=_params(("arbitrary" if has_norm else "parallel", "parallel", "arbitrary")),
    )(*((a, w) + ((res,) if has_res else ()) + (tuple(norm_bwd) if has_norm else ()) + tuple(after)))


def _local_step(x, tgt, norm1_w, w_a, w_z, w_b, conv_a, a_log, dt_bias, gnw, norm2_w, final_w, late_weights, emit, start_after=()):
    wgrad = functools.partial(_mm, ta=True, out_dtype=BF16)
    h1, proj_a, proj_z, proj_b = _in_proj(x, norm1_w, w_a, w_z, w_b, after=start_after, name="in_proj")
    qn, kn, v, gcb, bb = _gdn_prep_fwd(proj_a, conv_a, a_log, dt_bias, name="gdn_prep_fwd")
    uv, wk, at, tmat, wkb, qdb, keb = _gdn_chunk_fwd(qn, kn, v, gcb, bb, name="gdn_chunk_fwd")
    o, u, sp, oab = _gdn_scan_fwd(uv, at, wkb, qdb, keb, gcb, proj_z, gnw, name="gdn_scan_fwd")
    oab, lse = _attn_fwd(proj_b, oab, name="attn_fwd")
    w_out, w_up, conv_f, w_down = late_weights(oab)
    x1, h2 = _out_proj_norm(oab, w_out, x, norm2_w, name="out_proj")
    dx2, dx2_b, d_final, loss, ug, uu = _ffn_fwd(h2, x1, w_up, conv_f, w_down, final_w, tgt, name="ffn_fwd")
    dug, duu, g_down, g_up_g, g_up_u, dcw_g, dcw_u = _ffn_bwd(dx2_b, h2, ug, uu, conv_f, w_down, name="ffn_bwd")
    token = emit("ffn", w_down=g_down, w_up=jnp.concatenate([g_up_g, g_up_u], axis=0),
                 conv_f=jnp.concatenate([dcw_g, dcw_u], axis=0))
    dh2 = _mm_slabs(dug, w_up, 0, name="ffn_up_dx_gate")
    dx1, d_norm2 = _mm_slabs(duu, w_up, FF_PAIRS, res=dh2, norm_bwd=(x1, norm2_w, dx2), after=token, name="ffn_up_dx_up")
    d_oab = _mm(dx1, w_out, tb=True, name="out_proj_dx", tk=1024)
    token = emit("out", w_out=wgrad(oab, dx1, name="out_proj_dw"))
    dz, d_gnw, du, dwk, dat, dqd, dke, dgl = _gdn_scan_bwd(d_oab, o, proj_z, gnw, sp, u, at, wkb, qdb, keb, gcb, after=token, name="gdn_scan_bwd")
    dqn, dkn, dv, dg, dbeta = _gdn_chunk_bwd(qn, kn, gcb, bb, tmat, uv, wk, du, dwk, dat, dqd, dke, dgl, name="gdn_chunk_bwd")
    dc, dba, d_small = _gdn_prep_bwd(dqn, dkn, dv, dg, dbeta, proj_a, conv_a, a_log, dt_bias, name="gdn_prep_bwd")
    d_pa, d_conv_a = _gdn_conv_bwd(dc, dba, proj_a, conv_a, name="gdn_conv_bwd")
    d_pb = _attn_bwd(proj_b, oab, d_oab, lse, name="attn_bwd")
    g_a = wgrad(d_pa, h1, name="proj_a_dw", tm=A_COLS)
    g_z = wgrad(dz, h1, name="proj_z_dw")
    g_b = wgrad(d_pb, h1, name="proj_b_dw", tm=768)
    token = emit("in", w_a=g_a, w_z=g_z, w_b=g_b, conv_a=d_conv_a)
    dh1 = _mm(dz, w_z, after=token, name="proj_z_dx")
    dh1 = _mm(d_pa, w_a, res=dh1, name="proj_a_dx", tk=A_COLS)
    grad_x, d_norm1 = _mm(d_pb, w_b, res=dh1, norm_bwd=(x, norm1_w, dx1), name="proj_b_dx", tn=D_MODEL, tk=1536)
    small = dict(norm1=d_norm1, small=d_small, gnw=d_gnw, norm2=d_norm2, final=d_final)
    return loss, grad_x, small


_O1 = 3 * GDN_WIDTH
_O2 = _O1 + GDN_WIDTH
_O3 = _O2 + 2 * GDN_HEADS


def _split_w_in(w_t):
    d = w_t.shape[1]
    pad = jnp.zeros((A_COLS - _O1 - 2 * GDN_HEADS, d), w_t.dtype)
    w_a = jnp.concatenate([w_t[:_O1], w_t[_O2:_O3], pad], axis=0)
    w_b = w_t[_O3:].reshape(3, DIL_PAIRS, 128, d).transpose(1, 0, 2, 3).reshape(3 * DIL_WIDTH, d)
    return w_a, w_t[_O1:_O2], w_b


def _merge_g_in(g_a, g_z, g_b):
    d = g_a.shape[1]
    g_b = g_b.reshape(DIL_PAIRS, 3, 128, d).transpose(1, 0, 2, 3).reshape(3 * DIL_WIDTH, d)
    return jnp.concatenate([g_a[:_O1], g_z, g_a[_O1:_O1 + 2 * GDN_HEADS], g_b], axis=0)


MESH = pl.DeviceIdType.MESH
ANY = pl.BlockSpec(memory_space=pl.ANY)


def _position():
    return lax.axis_index("x"), lax.axis_index("y"), lax.axis_index("c")


def _slot(p):
    return 4 * p[0] + 2 * p[1] + p[2]


def _all_gather(blocks, *, name):
    n = len(blocks)

    def body(*refs):
        ins, outs = refs[:n], refs[n:2 * n]
        send_sems, recv_sems, local_sems = refs[2 * n:]
        x, y, c = _position()
        me, sibling = (x, y, c), (x, y, 1 - c)
        chips = [(1 - x, y), (x, 1 - y), (1 - x, 1 - y)]

        def copy(a, k, block, to, src=None):
            dst = outs[a].at[_slot(block)]
            return pltpu.make_async_remote_copy(
                src_ref=dst if src is None else src, dst_ref=dst, send_sem=send_sems.at[a, k], recv_sem=recv_sems.at[a, k],
                device_id=to, device_id_type=MESH)

        mine = [pltpu.make_async_copy(ins[a], outs[a].at[_slot(me)], local_sems.at[a]) for a in range(n)]
        for cp in mine:
            cp.start()
        first = []
        for a in range(n):
            first.append(copy(a, 0, me, sibling, src=ins[a]))
            first += [copy(a, 1 + j, me, (*chip, c), src=ins[a]) for j, chip in enumerate(chips)]
        for cp in first:
            cp.start()
        passed = []
        for j, chip in enumerate(chips):
            for a in range(n):
                copy(a, 1 + j, (*chip, c), me).wait_recv()
                fwd = copy(a, 4 + j, (*chip, c), sibling)
                fwd.start()
                passed.append(fwd)
        for a in range(n):
            copy(a, 0, sibling, me).wait_recv()
            for j, chip in enumerate(chips):
                copy(a, 4 + j, (*chip, 1 - c), me).wait_recv()
        for cp in first + passed:
            cp.wait_send()
        for cp in mine:
            cp.wait()

    return pl.pallas_call(
        body, name=name, in_specs=[ANY] * n, out_specs=[ANY] * n,
        out_shape=[_sds((N_DEV,) + b.shape, b.dtype) for b in blocks],
        scratch_shapes=[pltpu.SemaphoreType.DMA((n, 7)), pltpu.SemaphoreType.DMA((n, 7)), pltpu.SemaphoreType.DMA((n,))],
    )(*blocks)


def _gather_direct(block, *, name, after=()):
    def body(in_ref, *rest):
        out_ref, send_sems, recv_sems, local_sem = rest[len(after):]
        x, y, c = _position()
        me = _slot((x, y, c))
        mine = pltpu.make_async_copy(in_ref, out_ref.at[me], local_sem)
        mine.start()
        copies = [pltpu.make_async_remote_copy(
            src_ref=in_ref, dst_ref=out_ref.at[me], send_sem=send_sems.at[k - 1], recv_sem=recv_sems.at[k - 1],
            device_id=_peer_of(k, x, y, c), device_id_type=MESH) for k in range(1, N_DEV)]
        for cp in copies:
            cp.start()
        for cp in copies:
            cp.wait()
        mine.wait()

    return pl.pallas_call(
        body, name=name, in_specs=[pl.BlockSpec(memory_space=pltpu.VMEM)] + [ANY] * len(after),
        out_specs=pl.BlockSpec(memory_space=pltpu.VMEM),
        out_shape=_sds((N_DEV,) + block.shape, block.dtype),
        scratch_shapes=[pltpu.SemaphoreType.DMA((N_DEV - 1,)), pltpu.SemaphoreType.DMA((N_DEV - 1,)), pltpu.SemaphoreType.DMA],
    )(block, *after)


HBM = pl.BlockSpec(memory_space=pltpu.HBM)
SEM = pl.BlockSpec(memory_space=pltpu.SEMAPHORE)
EFFECT = pltpu.SideEffectType.DATAFLOW_SIDE_EFFECTING


def _peer_of(k, x, y, c):
    return (1 - x if k & 4 else x, 1 - y if k & 2 else y, 1 - c if k & 1 else c)


def _flight(a, k):
    return a * (N_DEV - 1) + k - 1


def _exchange_start(arrays, *, name, broadcast=False):
    n = len(arrays)

    def body(*refs):
        ins, lands = refs[:n], refs[n:2 * n]
        send_sems, recv_sems = refs[2 * n:2 * n + 2]
        token = refs[-1]
        x, y, c = _position()
        me = _slot((x, y, c))
        for k in range(1, N_DEV):
            peer = _peer_of(k, x, y, c)
            for a in range(n):
                pltpu.make_async_remote_copy(
                    src_ref=ins[a] if broadcast else ins[a].at[_slot(peer)], dst_ref=lands[a].at[me],
                    send_sem=send_sems.at[_flight(a, k)], recv_sem=recv_sems.at[_flight(a, k)],
                    device_id=peer, device_id_type=MESH).start()
        token[...] = jnp.zeros_like(token)

    land_shapes = [((N_DEV,) + s.shape) if broadcast else s.shape for s in arrays]
    lands = [pltpu.with_memory_space_constraint(lax.empty(shp, s.dtype), pltpu.HBM) for shp, s in zip(land_shapes, arrays)]
    srcs = [pltpu.with_memory_space_constraint(s, pltpu.HBM) for s in arrays]
    outs = pl.pallas_call(
        body, name=name, in_specs=[HBM] * (2 * n),
        out_specs=[SEM, SEM] + [HBM] * (2 * n) + [pl.BlockSpec(memory_space=pltpu.VMEM)],
        out_shape=[pltpu.SemaphoreType.DMA((n * (N_DEV - 1),)), pltpu.SemaphoreType.DMA((n * (N_DEV - 1),))]
        + [pltpu.HBM(s.shape, s.dtype) for s in arrays] + [pltpu.HBM(shp, s.dtype) for shp, s in zip(land_shapes, arrays)]
        + [_sds((8, 128))],
        input_output_aliases={i: 2 + i for i in range(2 * n)},
        compiler_params=pltpu.CompilerParams(has_side_effects=EFFECT),
    )(*srcs, *lands)
    return outs[0], outs[1], outs[2:2 + n], outs[2 + n:2 + 2 * n], outs[-1]


def _exchange_wait(send_sems, recv_sems, srcs, lands, after, *, name, broadcast=False):
    n = len(srcs)

    def body(*refs):
        ins, lnd = refs[:n], refs[n:2 * n]
        send_ref, recv_ref = refs[2 * n:2 * n + 2]
        x, y, c = _position()
        for k in range(1, N_DEV):
            for a in range(n):
                cp = pltpu.make_async_remote_copy(
                    src_ref=ins[a] if broadcast else ins[a].at[0], dst_ref=lnd[a].at[0], send_sem=send_ref.at[_flight(a, k)],
                    recv_sem=recv_ref.at[_flight(a, k)], device_id=_peer_of(k, x, y, c), device_id_type=MESH)
                cp.wait_send()
                cp.wait_recv()

    outs = pl.pallas_call(
        body, name=name, in_specs=[HBM] * (2 * n) + [SEM, SEM, ANY], out_specs=[HBM] * (2 * n),
        out_shape=[pltpu.HBM(s.shape, s.dtype) for s in srcs] + [pltpu.HBM(s.shape, s.dtype) for s in lands],
        input_output_aliases={i: i for i in range(2 * n)},
        compiler_params=pltpu.CompilerParams(has_side_effects=EFFECT),
    )(*srcs, *lands, send_sems, recv_sems, after)
    return outs[:n], outs[n:]


def _with_own(landed, srcs, me, broadcast=False):
    own = srcs if broadcast else [lax.dynamic_index_in_dim(s, me, 0, keepdims=False) for s in srcs]
    return [lax.dynamic_update_index_in_dim(l, o, me, 0) for l, o in zip(landed, own)]


def _adamw(parts, w, m, v, *, name, tr=None, tc=None):
    R, C = w.shape
    tr = R if tr is None else tr
    tc = C if tc is None else tc
    assert R % tr == 0 and C % tc == 0
    c1 = 1.0 - ADAM_B1 ** ADAM_STEP
    c2 = 1.0 - ADAM_B2 ** ADAM_STEP

    def body(p_ref, w_ref, m_ref, v_ref, g_ref, d_ref, nm_ref, nv_ref, token_ref):
        token_ref[...] = jnp.zeros_like(token_ref)
        g = p_ref[0].astype(F32)
        for s in range(1, N_DEV):
            g = g + p_ref[s].astype(F32)
        nm = ADAM_B1 * m_ref[...] + (1.0 - ADAM_B1) * g
        nv = ADAM_B2 * v_ref[...] + (1.0 - ADAM_B2) * (g * g)
        g_ref[...] = g
        nm_ref[...] = nm
        nv_ref[...] = nv
        d_ref[...] = -ADAM_LR * ((nm / c1) / (jnp.sqrt(nv / c2) + ADAM_EPS) + ADAM_WD * w_ref[...])

    blk = pl.BlockSpec((tr, tc), lambda i, j: (i, j))
    return pl.pallas_call(
        body, name=name, grid=(R // tr, C // tc),
        in_specs=[pl.BlockSpec((N_DEV, tr, tc), lambda i, j: (0, i, j)), blk, blk, blk],
        out_specs=[blk] * 4 + [pl.BlockSpec((8, 128), lambda i, j: (0, 0))], out_shape=[_sds((R, C))] * 4 + [_sds((8, 128))],
        compiler_params=_params(("arbitrary", "arbitrary")),
    )(parts, w, m, v)


_SMALL_ROWS = 8


def _pack_small(norm1, norm2, final, gnw, a_log, dt_bias, loss=None):
    loss = jnp.zeros((1, 128), F32) if loss is None else loss
    row3 = jnp.concatenate([gnw, a_log, dt_bias, jnp.zeros((1, 128 - 2 * GDN_HEADS), F32), loss,
                            jnp.zeros((1, D_MODEL - 3 * 128), F32)], axis=1)
    return jnp.concatenate([norm1, norm2, final, row3, jnp.zeros((_SMALL_ROWS - 4, D_MODEL), F32)], axis=0)


def _unpack_small(p):
    return (p[0:1], p[1:2], p[2], p[3:4, 0:128], p[3:4, 128:128 + GDN_HEADS], p[3:4, 128 + GDN_HEADS:128 + 2 * GDN_HEADS])


def _slabs_by_cols(g):
    r = g.shape[0]
    return g.reshape(r, N_DEV, -1).transpose(1, 0, 2)


def _cols_from_slabs(s):
    return s.transpose(1, 0, 2).reshape(s.shape[1], -1)


def kernel(x, norm1_w, w_in, conv_qkv_w, a_log, dt_bias, gdn_norm_w, w_out, norm2_w, w_up, ffn_conv_w, w_down, final_norm_w, loss_target, m_norm1_w, m_w_in, m_conv_qkv_w, m_a_log, m_dt_bias, m_gdn_norm_w, m_w_out, m_norm2_w, m_w_up, m_ffn_conv_w, m_w_down, m_final_norm_w, v_norm1_w, v_w_in, v_conv_qkv_w, v_a_log, v_dt_bias, v_gdn_norm_w, v_w_out, v_norm2_w, v_w_up, v_ffn_conv_w, v_w_down, v_final_norm_w):
    bf = lambda a: a.astype(BF16)
    me = _slot(_position())
    t_in = lambda a: a[0].T
    gw_in, g_conv_a = _all_gather([bf(t_in(w_in)), conv_qkv_w[0]], name="gather_w_in")
    w_a, w_z, w_b = _split_w_in(gw_in.reshape(-1, D_MODEL))
    late_src, _ = lax.optimization_barrier(([bf(w_out[0]), bf(t_in(w_up)), bf(w_down[0]), ffn_conv_w[0]], gw_in))
    l_send, l_recv, l_srcs, l_lands, l_token = _exchange_start(late_src, name="weights_start", broadcast=True)

    def late_weights(after):
        srcs, landed = _exchange_wait(l_send, l_recv, l_srcs, l_lands, after, name="weights_wait", broadcast=True)
        gw_out, gw_up, gw_down, g_conv_f = _with_own(landed, srcs, me, broadcast=True)
        return gw_out.reshape(D_MODEL, D_MODEL), gw_up, g_conv_f, gw_down.reshape(D_FF, D_MODEL)

    flights = {}

    def emit(group, **grads):
        if group == "in":
            slabs = dict(w_in=_merge_g_in(grads["w_a"], grads["w_z"], grads["w_b"]).reshape(N_DEV, -1, D_MODEL),
                         conv_a=_slabs_by_cols(grads["conv_a"]))
        elif group == "ffn":
            slabs = dict(w_down=grads["w_down"].reshape(N_DEV, -1, D_MODEL), w_up=grads["w_up"], conv_f=grads["conv_f"])
        else:
            slabs = {k: v.reshape(N_DEV, -1, D_MODEL) for k, v in grads.items()}
        names = list(slabs)
        *flight, token = _exchange_start([slabs[k] for k in names], name="grads_start_" + group)
        flights[group] = (names, flight)
        return (token,)

    loss, grad_x, g = _local_step(
        x[0], loss_target[0], norm1_w, w_a, w_z, w_b, _cols_from_slabs(g_conv_a), a_log, dt_bias,
        gdn_norm_w, norm2_w, final_norm_w[None], late_weights, emit, start_after=(l_token,))
    got = {}

    def collect(group, after):
        names, (send_sems, recv_sems, srcs, lands) = flights[group]
        srcs, landed = _exchange_wait(send_sems, recv_sems, srcs, lands, after, name="grads_wait_" + group)
        got.update(zip(names, _with_own(landed, srcs, me)))

    collect("ffn", grad_x)
    collect("out", grad_x)
    *o_out, t1 = _adamw(got["w_out"], w_out[0], m_w_out[0], v_w_out[0], name="adamw_w_out")
    *o_up, t2 = _adamw(got["w_up"], t_in(w_up), t_in(m_w_up), t_in(v_w_up), name="adamw_w_up", tr=176)
    o_up = [o.T for o in o_up]
    *o_down, t3 = _adamw(got["w_down"], w_down[0], m_w_down[0], v_w_down[0], name="adamw_w_down", tr=176)
    *o_cf, t4 = _adamw(got["conv_f"], ffn_conv_w[0], m_ffn_conv_w[0], v_ffn_conv_w[0], name="adamw_conv_f")
    pack = _pack_small(g["norm1"], g["norm2"], g["final"], g["gnw"], g["small"][:, 0:GDN_HEADS],
                       g["small"][:, GDN_HEADS:2 * GDN_HEADS], loss)
    small_all = _gather_direct(pack, after=(t1, t2, t3, t4), name="gather_small")
    collect("in", small_all)
    o_in = [o.T for o in _adamw(got["w_in"], t_in(w_in), t_in(m_w_in), t_in(v_w_in), name="adamw_w_in", tc=256)[:4]]
    o_ca = _adamw(got["conv_a"], conv_qkv_w[0], m_conv_qkv_w[0], v_conv_qkv_w[0], name="adamw_conv_a")
    o_small = _adamw(
        small_all, _pack_small(norm1_w, norm2_w, final_norm_w[None], gdn_norm_w, a_log, dt_bias),
        _pack_small(m_norm1_w, m_norm2_w, m_final_norm_w[None], m_gdn_norm_w, m_a_log, m_dt_bias),
        _pack_small(v_norm1_w, v_norm2_w, v_final_norm_w[None], v_gdn_norm_w, v_a_log, v_dt_bias), name="adamw_small")
    total_loss = o_small[0][3, 256]
    outs = [total_loss, grad_x[None]]
    for k in range(4):
        n1, n2, fin, gn, al, dt = _unpack_small(o_small[k])
        outs += [n1, o_in[k][None], o_ca[k][None], al, dt, gn, o_out[k][None], n2, o_up[k][None], o_cf[k][None], o_down[k][None], fin]
    return tuple(outs)
```

```python
import functools

import jax
import jax.numpy as jnp
from jax import lax
from jax.experimental import pallas as pl
from jax.experimental.pallas import tpu as pltpu

F32 = jnp.float32
BF16 = jnp.bfloat16

N_DEV = 8
D_MODEL = 1024
GDN_HEADS = 4
GDN_DIM = 128
GDN_WIDTH = GDN_HEADS * GDN_DIM
GDN_CONV = 4
CHUNK = 64
CHUNKS_PER_STEP = 4
DIL_HEADS = 8
DIL_DIM = 64
DIL_WIDTH = DIL_HEADS * DIL_DIM
DIL_PAIRS = DIL_HEADS // 2
DILATIONS = (1, 4, 16)
BAND = 128
D_FF = 2816
FFN_CONV = 3
EPS = 1e-6
A_COLS = 3 * GDN_WIDTH + 128
HALO = 8

ADAM_LR = 0.001
ADAM_B1 = 0.9
ADAM_B2 = 0.999
ADAM_EPS = 1e-08
ADAM_WD = 0.01
ADAM_STEP = 10

VMEM_LIMIT_BYTES = 56 * 1024 * 1024
NEG_BIG = -1e30


def _params(sem=None):
    return pltpu.CompilerParams(dimension_semantics=sem, vmem_limit_bytes=VMEM_LIMIT_BYTES)


def _sds(shape, dtype=F32):
    return jax.ShapeDtypeStruct(shape, dtype)


def _bdot(a, b):
    return jnp.dot(a.astype(BF16), b.astype(BF16), preferred_element_type=F32)


def _bdot_nt(a, b):
    return lax.dot_general(a.astype(BF16), b.astype(BF16), (((1,), (1,)), ((), ())), preferred_element_type=F32)


def _bdot_tn(a, b):
    return lax.dot_general(a.astype(BF16), b.astype(BF16), (((0,), (0,)), ((), ())), preferred_element_type=F32)


def _split(a):
    hi = a.astype(BF16)
    lo = (a - hi.astype(F32)).astype(BF16)
    return hi, lo


def _dot3(a, b, dims):
    ah, al = _split(a)
    bh, bl = _split(b)
    d = functools.partial(lax.dot_general, dimension_numbers=(dims, ((), ())), preferred_element_type=F32)
    return d(ah, bh) + (d(al, bh) + d(ah, bl))


def _exact_tri_dot(tri, g):
    g1 = g.astype(BF16)
    r1 = g - g1.astype(F32)
    g2 = r1.astype(BF16)
    g3 = (r1 - g2.astype(F32)).astype(BF16)
    t = tri.astype(BF16)
    d = functools.partial(jnp.dot, preferred_element_type=F32)
    return d(t, g1) + (d(t, g2) + d(t, g3))


def _sigmoid(x):
    return 1.0 / (1.0 + jnp.exp(-x))


def _dsilu(x, sg):
    return sg * (1.0 + x * (1.0 - sg))


def _rms_bwd_rows(dh, x, w):
    r = lax.rsqrt(jnp.mean(x * x, axis=-1, keepdims=True) + EPS)
    xh = x * r
    gw = dh * w
    return r * (gw - xh * jnp.mean(gw * xh, axis=-1, keepdims=True)), jnp.sum(dh * xh, axis=0, keepdims=True)


def _mm(a, b, *, name, ta=False, tb=False, res=None, norm_bwd=None, after=(), out_dtype=F32, tm=512, tn=512, tk=512):
    if ta:
        K, M = a.shape
    else:
        M, K = a.shape
    if tb:
        N, Kb = b.shape
    else:
        Kb, N = b.shape
    assert K == Kb, (a.shape, b.shape)
    tm, tn, tk = min(tm, M), min(tn, N), min(tk, K)
    assert M % tm == 0 and N % tn == 0 and K % tk == 0, (name, M, N, K, tm, tn, tk)
    nk = K // tk
    dims = (((0 if ta else 1,), (1 if tb else 0,)), ((), ()))
    has_res = res is not None
    has_norm = norm_bwd is not None
    assert not has_norm or tn == N

    def body(*refs):
        a_ref, b_ref = refs[:2]
        r_ref = refs[2] if has_res else None
        if has_norm:
            x_ref, w_ref, skip_ref = refs[2 + has_res:5 + has_res]
            o_ref, dw_ref, acc_ref = refs[-3:]
        else:
            o_ref, acc_ref = refs[-2:]
        i, k = pl.program_id(0), pl.program_id(2)
        part = lax.dot_general(a_ref[...].astype(BF16), b_ref[...].astype(BF16), dims, preferred_element_type=F32)

        @pl.when(k == 0)
        def _():
            acc_ref[...] = part

        @pl.when(k > 0)
        def _():
            acc_ref[...] += part

        @pl.when(k == nk - 1)
        def _():
            r = acc_ref[...]
            if has_res:
                r = r + r_ref[...]
            if has_norm:
                dx, dw = _rms_bwd_rows(r, x_ref[...], w_ref[...])
                o_ref[...] = skip_ref[...] + dx

                @pl.when(i == 0)
                def _():
                    dw_ref[...] = dw

                @pl.when(i > 0)
                def _():
                    dw_ref[...] += dw
            else:
                o_ref[...] = r.astype(out_dtype)

    a_spec = pl.BlockSpec((tk, tm), lambda i, j, k: (k, i)) if ta else pl.BlockSpec((tm, tk), lambda i, j, k: (i, k))
    b_spec = pl.BlockSpec((tn, tk), lambda i, j, k: (j, k)) if tb else pl.BlockSpec((tk, tn), lambda i, j, k: (k, j))
    o_spec = pl.BlockSpec((tm, tn), lambda i, j, k: (i, j))
    one = pl.BlockSpec((1, tn), lambda i, j, k: (0, 0))
    in_specs = [a_spec, b_spec] + [o_spec] * has_res + ([o_spec, one, o_spec] if has_norm else []) + [ANY] * len(after)
    args = (a, b) + ((res,) if has_res else ()) + (tuple(norm_bwd) if has_norm else ()) + tuple(after)
    return pl.pallas_call(
        body, name=name, grid=(M // tm, N // tn, nk), in_specs=in_specs,
        out_specs=[o_spec, one] if has_norm else o_spec,
        out_shape=[_sds((M, N)), _sds((1, N))] if has_norm else _sds((M, N), out_dtype),
        scratch_shapes=[pltpu.VMEM((tm, tn), F32)],
        compiler_params=_params(("arbitrary" if has_norm else "parallel", "parallel", "arbitrary")),
    )(*args)


def _in_proj(x, norm_w, w_a, w_z, w_b, *, name, after=(), tm=512):
    S, D = x.shape
    ws = (w_a, w_z, w_b)

    def body(x_ref, nw_ref, wa_ref, wz_ref, wb_ref, *rest):
        h_ref, pa_ref, pz_ref, pb_ref = rest[len(after):]
        xv = x_ref[...]
        r = lax.rsqrt(jnp.mean(xv * xv, axis=-1, keepdims=True) + EPS)
        h = (xv * r * nw_ref[...]).astype(BF16)
        h_ref[...] = h
        for w_ref, p_ref in ((wa_ref, pa_ref), (wz_ref, pz_ref), (wb_ref, pb_ref)):
            p_ref[...] = lax.dot_general(h, w_ref[...], (((1,), (1,)), ((), ())), preferred_element_type=F32)

    row = lambda n: pl.BlockSpec((tm, n), lambda i: (i, 0))
    full = lambda a: pl.BlockSpec(a.shape, lambda i: (0, 0))
    return pl.pallas_call(
        body, name=name, grid=(S // tm,), in_specs=[row(D), full(norm_w)] + [full(w) for w in ws] + [ANY] * len(after),
        out_specs=[row(D)] + [row(w.shape[0]) for w in ws],
        out_shape=[_sds((S, D), BF16)] + [_sds((S, w.shape[0])) for w in ws], compiler_params=_params(("parallel",)),
    )(x, norm_w, *ws, *after)


def _out_proj_norm(a, w, x, norm_w, *, name, tm=512):
    S, D = x.shape

    def body(a_ref, w_ref, x_ref, nw_ref, x1_ref, h_ref):
        x1 = x_ref[...] + jnp.dot(a_ref[...], w_ref[...], preferred_element_type=F32)
        x1_ref[...] = x1
        r = lax.rsqrt(jnp.mean(x1 * x1, axis=-1, keepdims=True) + EPS)
        h_ref[...] = (x1 * r * nw_ref[...]).astype(BF16)

    row = pl.BlockSpec((tm, D), lambda i: (i, 0))
    return pl.pallas_call(
        body, name=name, grid=(S // tm,),
        in_specs=[pl.BlockSpec((tm, a.shape[1]), lambda i: (i, 0)), pl.BlockSpec(w.shape, lambda i: (0, 0)), row,
                  pl.BlockSpec((1, D), lambda i: (0, 0))],
        out_specs=[row, row], out_shape=[_sds((S, D)), _sds((S, D), BF16)], compiler_params=_params(("parallel",)),
    )(a, w, x, norm_w)


def _shifted(x, start, n):
    aligned = -(-start // HALO) * HALO
    assert aligned + n <= x.shape[0], (start, n, x.shape)
    return (x if aligned == start else pltpu.roll(x, aligned - start, axis=0))[aligned:aligned + n]


def _conv_rows(prev, cur, w, taps):
    n = cur.shape[0]
    xs = jnp.concatenate([prev, cur], axis=0)
    base = HALO - (taps - 1)
    out = _shifted(xs, base, n) * w[0:1]
    for i in range(1, taps):
        out = out + _shifted(xs, base + i, n) * w[i:i + 1]
    return out


def _conv_rows_bwd(cur_d, next_d, prev_x, cur_x, w, taps):
    n = cur_d.shape[0]
    ds = jnp.concatenate([cur_d, next_d], axis=0)
    dx = _shifted(ds, taps - 1, n) * w[0:1]
    for i in range(1, taps):
        dx = dx + _shifted(ds, taps - 1 - i, n) * w[i:i + 1]
    xs = jnp.concatenate([prev_x, cur_x], axis=0)
    base = HALO - (taps - 1)
    dws = [jnp.sum(cur_d * _shifted(xs, base + i, n), axis=0, keepdims=True) for i in range(taps)]
    return dx, jnp.concatenate(dws, axis=0)


def _halo_specs(tm, width, col, nblk):
    per = tm // HALO
    prev = pl.BlockSpec((HALO, width), lambda i, *_: (jnp.maximum(i * per - 1, 0), col))
    nxt = pl.BlockSpec((HALO, width), lambda i, *_: (jnp.minimum((i + 1) * per, nblk * per - 1), col))
    return prev, nxt


def _softplus(x):
    return jnp.maximum(x, 0.0) + jnp.log1p(jnp.exp(-jnp.abs(x)))


def _chunk_tri(tm, upper=False):
    r = lax.broadcasted_iota(jnp.int32, (tm, tm), 0)
    c = lax.broadcasted_iota(jnp.int32, (tm, tm), 1)
    same = lax.div(r, CHUNK) == lax.div(c, CHUNK)
    order = (c >= r) if upper else (c <= r)
    return jnp.where(same & order, 1.0, 0.0)


def _gdn_prep_fwd(proj_a, conv_w, a_log, dt_bias, *, name, tm=256):
    S = proj_a.shape[0]
    nblk = S // tm
    W3 = 3 * GDN_WIDTH

    def body(cur_ref, prev_ref, ba_ref, cw_ref, al_ref, dt_ref, qn_ref, kn_ref, v_ref, gcb_ref, bb_ref):
        i = pl.program_id(0)
        prev = jnp.where(i > 0, prev_ref[...], 0.0)
        c = _conv_rows(prev, cur_ref[...], cw_ref[...], GDN_CONV)
        a = c * _sigmoid(c)
        ba = ba_ref[...]
        lane = lax.broadcasted_iota(jnp.int32, (tm, 128), 1)
        g4 = jnp.zeros((tm, 128), F32)
        for h in range(GDN_HEADS):
            sl = slice(GDN_DIM * h, GDN_DIM * (h + 1))
            qh = a[:, GDN_DIM * h:GDN_DIM * (h + 1)]
            kh = a[:, GDN_WIDTH + GDN_DIM * h:GDN_WIDTH + GDN_DIM * (h + 1)]
            qn_ref[:, sl] = qh * (lax.rsqrt(jnp.sum(qh * qh, axis=-1, keepdims=True) + EPS) * (GDN_DIM ** -0.5))
            kn_ref[:, sl] = kh * lax.rsqrt(jnp.sum(kh * kh, axis=-1, keepdims=True) + EPS)
            beta = _sigmoid(ba[:, h:h + 1])
            bb_ref[:, sl] = jnp.broadcast_to(beta, (tm, GDN_DIM))
            g = -jnp.exp(al_ref[0:1, h:h + 1]) * _softplus(ba[:, GDN_HEADS + h:GDN_HEADS + h + 1] + dt_ref[0:1, h:h + 1])
            g4 = jnp.where(lane == h, g, g4)
        v_ref[...] = a[:, 2 * GDN_WIDTH:]
        gc = _exact_tri_dot(_chunk_tri(tm), g4)
        for h in range(GDN_HEADS):
            gcb_ref[:, GDN_DIM * h:GDN_DIM * (h + 1)] = jnp.broadcast_to(gc[:, h:h + 1], (tm, GDN_DIM))

    prev_spec, _ = _halo_specs(tm, W3, 0, nblk)
    row = pl.BlockSpec((tm, GDN_WIDTH), lambda i: (i, 0))
    small = lambda a: pl.BlockSpec(a.shape, lambda i: (0, 0))
    return pl.pallas_call(
        body, name=name, grid=(nblk,),
        in_specs=[pl.BlockSpec((tm, W3), lambda i: (i, 0)), prev_spec,
                  pl.BlockSpec((tm, 128), lambda i: (i, W3 // 128)), small(conv_w), small(a_log), small(dt_bias)],
        out_specs=[row] * 5, out_shape=[_sds((S, GDN_WIDTH))] * 5, compiler_params=_params(("parallel",)),
    )(proj_a, proj_a, proj_a, conv_w, a_log, dt_bias)


GDN_STACK = GDN_HEADS * CHUNK


def _stack(ref, rows):
    return jnp.concatenate([ref[rows, GDN_DIM * h:GDN_DIM * (h + 1)] for h in range(GDN_HEADS)], axis=0)


def _unstack_to(ref, rows, x):
    for h in range(GDN_HEADS):
        ref[rows, GDN_DIM * h:GDN_DIM * (h + 1)] = x[CHUNK * h:CHUNK * (h + 1)].astype(ref.dtype)


def _stack_masks():
    r = lax.broadcasted_iota(jnp.int32, (GDN_STACK, GDN_STACK), 0)
    c = lax.broadcasted_iota(jnp.int32, (GDN_STACK, GDN_STACK), 1)
    same = (r & -CHUNK) == (c & -CHUNK)
    return same & (r >= c), same & (r > c), r == c


def _stack_decay(gs, bs, incl):
    g2 = jnp.concatenate([gs, gs], axis=1)
    diff = g2 - g2.T
    dec = jnp.where(incl, jnp.exp(jnp.where(incl, diff, 0.0)), 0.0)
    return dec, jnp.concatenate([bs, bs], axis=1).T


def _head_mask():
    r = lax.broadcasted_iota(jnp.int32, (GDN_STACK, GDN_WIDTH), 0)
    c = lax.broadcasted_iota(jnp.int32, (GDN_STACK, GDN_WIDTH), 1)
    return (r & -CHUNK) * (GDN_DIM // CHUNK) == (c & -GDN_DIM)


def _head_spread(x):
    return jnp.where(_head_mask(), jnp.concatenate([x] * GDN_HEADS, axis=1), 0.0)


def _head_diag(x):
    xm = jnp.where(_head_mask(), x, 0.0)
    out = xm[:, 0:GDN_DIM]
    for h in range(1, GDN_HEADS):
        out = out + xm[:, GDN_DIM * h:GDN_DIM * (h + 1)]
    return out


def _last_rows(gs, n):
    return jnp.concatenate([jnp.broadcast_to(gs[CHUNK * (h + 1) - 1:CHUNK * (h + 1)], (n, GDN_DIM)) for h in range(GDN_HEADS)], axis=0)


def _gdn_chunk_fwd(qn, kn, v, gcb, bb, *, name):
    S = qn.shape[0]

    def body(qn_ref, kn_ref, v_ref, gcb_ref, bb_ref, uv_ref, wk_ref, at_ref, t_ref, wkb_ref, qdb_ref, keb_ref):
        incl, strict, diag = _stack_masks()
        for c in range(CHUNKS_PER_STEP):
            rows = slice(CHUNK * c, CHUNK * (c + 1))
            srows = slice(GDN_STACK * c, GDN_STACK * (c + 1))
            q, k, vv, gs, bs = [_stack(r, rows) for r in (qn_ref, kn_ref, v_ref, gcb_ref, bb_ref)]
            dec, bt = _stack_decay(gs, bs, incl)
            p = -jnp.where(strict, dec * _bdot_nt(k, k) * bt, 0.0)
            t = jnp.where(diag, 1.0, 0.0) + p
            for _ in range(5):
                p = _bdot(p, p)
                t = t + _bdot(t, p)
            sol = _dot3(t, jnp.concatenate([vv, jnp.exp(gs) * k], axis=1), ((1,), (0,)))
            _unstack_to(uv_ref, rows, sol[:, :GDN_DIM])
            _unstack_to(wk_ref, rows, sol[:, GDN_DIM:])
            at_ref[srows, :] = dec * _bdot_nt(q, k) * bt
            t_ref[srows, :] = t
            wkb_ref[srows, :] = _head_spread(sol[:, GDN_DIM:]).astype(BF16)
            qdb_ref[srows, :] = _head_spread(q * jnp.exp(gs)).astype(BF16)
            keb_ref[srows, :] = _head_spread(k * jnp.exp(_last_rows(gs, CHUNK) - gs) * bs).astype(BF16)

    step = CHUNKS_PER_STEP * CHUNK
    row = pl.BlockSpec((step, GDN_WIDTH), lambda n: (n, 0))
    sq = pl.BlockSpec((CHUNKS_PER_STEP * GDN_STACK, GDN_STACK), lambda n: (n, 0))
    wide = pl.BlockSpec((CHUNKS_PER_STEP * GDN_STACK, GDN_WIDTH), lambda n: (n, 0))
    nsq = S // CHUNK * GDN_STACK
    return pl.pallas_call(
        body, name=name, grid=(S // step,), in_specs=[row] * 5, out_specs=[row, row, sq, sq, wide, wide, wide],
        out_shape=[_sds((S, GDN_WIDTH)), _sds((S, GDN_WIDTH)), _sds((nsq, GDN_STACK)), _sds((nsq, GDN_STACK))]
        + [_sds((nsq, GDN_WIDTH), BF16)] * 3,
        compiler_params=_params(("parallel",)),
    )(qn, kn, v, gcb, bb)


SCAN_CHUNKS = 8


def _gdn_scan_fwd(uv, at, wkb, qdb, keb, gcb, proj_z, gnw, *, name):
    S = uv.shape[0]
    nc = S // CHUNK

    def body(uv_ref, at_ref, wkb_ref, qdb_ref, keb_ref, gcb_ref, z_ref, gnw_ref, o_ref, u_ref, sp_ref, oa_ref, st_ref):
        n = pl.program_id(0)

        @pl.when(n == 0)
        def _():
            st_ref[...] = jnp.zeros_like(st_ref)

        for c in range(SCAN_CHUNKS):
            rows = slice(CHUNK * c, CHUNK * (c + 1))
            srows = slice(GDN_STACK * c, GDN_STACK * (c + 1))
            st = st_ref[...]
            sp_ref[GDN_WIDTH * c:GDN_WIDTH * (c + 1), :] = st
            uv, gs, z = [_stack(r, rows) for r in (uv_ref, gcb_ref, z_ref)]
            u = uv - _bdot(wkb_ref[srows, :], st)
            o = _bdot(qdb_ref[srows, :], st) + _bdot(at_ref[srows, :], u)
            st_ref[...] = jnp.exp(_last_rows(gs, GDN_DIM)) * st + _bdot_tn(keb_ref[srows, :], u)
            _unstack_to(u_ref, rows, u)
            _unstack_to(o_ref, rows, o)
            r = lax.rsqrt(jnp.mean(o * o, axis=-1, keepdims=True) + EPS)
            oa = o * r * gnw_ref[...] * (z * _sigmoid(z))
            oa_ref[rows, :] = jnp.concatenate([oa[CHUNK * h:CHUNK * (h + 1)] for h in range(GDN_HEADS)], axis=1).astype(BF16)

    row = pl.BlockSpec((SCAN_CHUNKS * CHUNK, GDN_WIDTH), lambda n: (n, 0))
    sq = pl.BlockSpec((SCAN_CHUNKS * GDN_STACK, GDN_STACK), lambda n: (n, 0))
    wide = pl.BlockSpec((SCAN_CHUNKS * GDN_STACK, GDN_WIDTH), lambda n: (n, 0))
    return pl.pallas_call(
        body, name=name, grid=(nc // SCAN_CHUNKS,),
        in_specs=[row, sq, wide, wide, wide, row, row, pl.BlockSpec((1, GDN_DIM), lambda n: (0, 0))],
        out_specs=[row, row, pl.BlockSpec((SCAN_CHUNKS * GDN_WIDTH, GDN_DIM), lambda n: (n, 0)), row],
        out_shape=[_sds((S, GDN_WIDTH)), _sds((S, GDN_WIDTH)), _sds((nc * GDN_WIDTH, GDN_DIM)), _sds((S, 2 * GDN_WIDTH), BF16)],
        scratch_shapes=[pltpu.VMEM((GDN_WIDTH, GDN_DIM), F32)],
        compiler_params=_params(("arbitrary",)),
    )(uv, at, wkb, qdb, keb, gcb, proj_z, gnw)


def _gdn_scan_bwd(d_oab, o, proj_z, gnw, sp, u, at, wkb, qdb, keb, gcb, *, name, after=()):
    S = o.shape[0]
    nc = S // CHUNK
    ns = nc // SCAN_CHUNKS

    def body(do_ref, o_ref, z_ref, gnw_ref, sp_ref, u_ref, at_ref, wkb_ref, qdb_ref, keb_ref, gcb_ref, *rest):
        dz_ref, dgn_ref, du_ref, dwk_ref, dat_ref, dqd_ref, dke_ref, dgl_ref, ds_ref = rest[len(after):]
        n = pl.program_id(0)

        @pl.when(n == 0)
        def _():
            ds_ref[...] = jnp.zeros_like(ds_ref)
            dgn_ref[...] = jnp.zeros_like(dgn_ref)

        gw = gnw_ref[...]
        for c in reversed(range(SCAN_CHUNKS)):
            rows = slice(CHUNK * c, CHUNK * (c + 1))
            srows = slice(GDN_STACK * c, GDN_STACK * (c + 1))
            d_oa, oo, z, uu, gs = [_stack(r, rows) for r in (do_ref, o_ref, z_ref, u_ref, gcb_ref)]
            sg = _sigmoid(z)
            r = lax.rsqrt(jnp.mean(oo * oo, axis=-1, keepdims=True) + EPS)
            xh = oo * r
            dy = d_oa * (z * sg)
            _unstack_to(dz_ref, rows, d_oa * (xh * gw) * _dsilu(z, sg))
            dgn_ref[...] += jnp.sum(dy * xh, axis=0, keepdims=True)
            dxh = dy * gw
            do = r * (dxh - xh * jnp.mean(dxh * xh, axis=-1, keepdims=True))

            st = sp_ref[GDN_WIDTH * c:GDN_WIDTH * (c + 1), :]
            dst = ds_ref[...]
            ge = jnp.exp(_last_rows(gs, GDN_DIM))
            _unstack_to(dqd_ref, rows, _head_diag(_bdot_nt(do, st)))
            dat_ref[srows, :] = _bdot_nt(do, uu)
            du = _bdot_tn(at_ref[srows, :], do) + _bdot(keb_ref[srows, :], dst)
            _unstack_to(dke_ref, rows, _head_diag(_bdot_nt(uu, dst)))
            prod = dst * st
            for h in range(GDN_HEADS):
                blk = prod[GDN_DIM * h:GDN_DIM * (h + 1)]
                dge = jnp.sum(jnp.sum(blk, axis=1, keepdims=True), axis=0, keepdims=True)
                dgl_ref[c, :, GDN_DIM * h:GDN_DIM * (h + 1)] = jnp.broadcast_to(dge * ge[GDN_DIM * h:GDN_DIM * h + 1], (8, GDN_DIM))
            ds_ref[...] = _bdot_tn(qdb_ref[srows, :], do) + ge * dst - _bdot_tn(wkb_ref[srows, :], du)
            _unstack_to(du_ref, rows, du)
            _unstack_to(dwk_ref, rows, -_head_diag(_bdot_nt(du, st)))

    rev = lambda n: (ns - 1 - n, 0)
    row = pl.BlockSpec((SCAN_CHUNKS * CHUNK, GDN_WIDTH), rev)
    sq = pl.BlockSpec((SCAN_CHUNKS * GDN_STACK, GDN_STACK), rev)
    wide = pl.BlockSpec((SCAN_CHUNKS * GDN_STACK, GDN_WIDTH), rev)
    one = pl.BlockSpec((1, GDN_DIM), lambda n: (0, 0))
    return pl.pallas_call(
        body, name=name, grid=(ns,),
        in_specs=[row, row, row, one, pl.BlockSpec((SCAN_CHUNKS * GDN_WIDTH, GDN_DIM), rev), row, sq, wide, wide, wide, row]
        + [ANY] * len(after),
        out_specs=[row, one, row, row, sq, row, row, pl.BlockSpec((SCAN_CHUNKS, 8, GDN_WIDTH), lambda n: (ns - 1 - n, 0, 0))],
        out_shape=[_sds((S, GDN_WIDTH), BF16), _sds((1, GDN_DIM)), _sds((S, GDN_WIDTH)), _sds((S, GDN_WIDTH)),
                   _sds((nc * GDN_STACK, GDN_STACK)), _sds((S, GDN_WIDTH)), _sds((S, GDN_WIDTH)), _sds((nc, 8, GDN_WIDTH))],
        scratch_shapes=[pltpu.VMEM((GDN_WIDTH, GDN_DIM), F32)],
        compiler_params=_params(("arbitrary",)),
    )(d_oab, o, proj_z, gnw, sp, u, at, wkb, qdb, keb, gcb, *after)


def _gdn_chunk_bwd(qn, kn, gcb, bb, tmat, uv, wk, du, dwk, dat, dqd, dke, dgl, *, name):
    S = qn.shape[0]

    def body(qn_ref, kn_ref, gcb_ref, bb_ref, t_ref, uv_ref, wk_ref, du_ref, dwk_ref, dat_ref, dqd_ref, dke_ref,
             dgl_ref, dq_ref, dk_ref, dv_ref, dg_ref, dbeta_ref):
        incl, strict, _ = _stack_masks()
        lane = lax.broadcasted_iota(jnp.int32, (CHUNK, 128), 1)
        rowi = lax.broadcasted_iota(jnp.int32, (CHUNK, 1), 0)
        rsum = lambda x: jnp.sum(x, axis=-1, keepdims=True)
        for c in range(CHUNKS_PER_STEP):
            rows = slice(CHUNK * c, CHUNK * (c + 1))
            srows = slice(GDN_STACK * c, GDN_STACK * (c + 1))
            q, k, gs, bs, uv, wk, du, dwk, dqd, dke = [
                _stack(r, rows) for r in (qn_ref, kn_ref, gcb_ref, bb_ref, uv_ref, wk_ref, du_ref, dwk_ref, dqd_ref, dke_ref)]
            dec, bt = _stack_decay(gs, bs, incl)
            kk = _bdot_nt(k, k)
            qk = _bdot_nt(q, k)
            d_rhs = _dot3(t_ref[srows, :], jnp.concatenate([du, dwk], axis=1), ((0,), (0,)))
            sol = jnp.concatenate([uv, wk], axis=1)
            d_l = jnp.where(strict, -_dot3(d_rhs, sol, ((1,), (1,))), 0.0)
            d_a = jnp.where(incl, dat_ref[srows, :], 0.0)
            gam = jnp.exp(gs)
            e = jnp.exp(_last_rows(gs, CHUNK) - gs)
            d_gk = d_rhs[:, GDN_DIM:]
            ml = d_l * dec * bt
            ma = d_a * dec * bt
            _unstack_to(dq_ref, rows, _bdot(ma, k) + dqd * gam)
            _unstack_to(dk_ref, rows, _bdot(ml + ml.T, k) + _bdot_tn(ma, q) + d_gk * gam + dke * (e * bs))
            _unstack_to(dv_ref, rows, d_rhs[:, :GDN_DIM])
            wb = d_l * dec * kk + d_a * dec * qk
            ew = wb * bt
            s_ke = rsum(dke * k * (e * bs))
            dbeta = rsum(wb.T) + rsum(dke * k * e)
            dgc = rsum(ew) - rsum(ew.T) + rsum(dqd * q * gam) + rsum(d_gk * k * gam) - s_ke
            dgc4 = jnp.zeros((CHUNK, 128), F32)
            db4 = jnp.zeros((CHUNK, 128), F32)
            for h in range(GDN_HEADS):
                hr = slice(CHUNK * h, CHUNK * (h + 1))
                tail = jnp.sum(s_ke[hr], axis=0, keepdims=True) + dgl_ref[c, 0:1, GDN_DIM * h:GDN_DIM * h + 1]
                dgc4 = jnp.where(lane == h, dgc[hr] + jnp.where(rowi == CHUNK - 1, tail, 0.0), dgc4)
                db4 = jnp.where(lane == h, dbeta[hr], db4)
            dg_ref[rows, :] = _exact_tri_dot(_chunk_tri(CHUNK, upper=True), dgc4)
            dbeta_ref[rows, :] = db4

    step = CHUNKS_PER_STEP * CHUNK
    row = pl.BlockSpec((step, GDN_WIDTH), lambda n: (n, 0))
    sq = pl.BlockSpec((CHUNKS_PER_STEP * GDN_STACK, GDN_STACK), lambda n: (n, 0))
    col = pl.BlockSpec((step, 128), lambda n: (n, 0))
    return pl.pallas_call(
        body, name=name, grid=(S // step,),
        in_specs=[row] * 4 + [sq, row, row, row, row, sq, row, row,
                              pl.BlockSpec((CHUNKS_PER_STEP, 8, GDN_WIDTH), lambda n: (n, 0, 0))],
        out_specs=[row, row, row, col, col],
        out_shape=[_sds((S, GDN_WIDTH))] * 3 + [_sds((S, 128))] * 2, compiler_params=_params(("parallel",)),
    )(qn, kn, gcb, bb, tmat, uv, wk, du, dwk, dat, dqd, dke, dgl)


def _gdn_prep_bwd(dqn, dkn, dv, dg, dbeta, proj_a, conv_w, a_log, dt_bias, *, name, tm=256):
    S = proj_a.shape[0]
    nblk = S // tm
    W3 = 3 * GDN_WIDTH

    def body(dqn_ref, dkn_ref, dv_ref, dg_ref, dbeta_ref, cur_ref, prev_ref, ba_ref, cw_ref, al_ref, dt_ref,
             dc_ref, dba_ref, sm_ref):
        i = pl.program_id(0)
        prev = jnp.where(i > 0, prev_ref[...], 0.0)
        c = _conv_rows(prev, cur_ref[...], cw_ref[...], GDN_CONV)
        sg = _sigmoid(c)
        a = c * sg
        dsl = _dsilu(c, sg)
        ba = ba_ref[...]
        lane = lax.broadcasted_iota(jnp.int32, (tm, 128), 1)
        lane1 = lax.broadcasted_iota(jnp.int32, (1, 128), 1)
        dba = jnp.zeros((tm, 128), F32)
        sm = jnp.zeros((1, 128), F32)
        for h in range(GDN_HEADS):
            sl = slice(GDN_DIM * h, GDN_DIM * (h + 1))
            ks = slice(GDN_WIDTH + GDN_DIM * h, GDN_WIDTH + GDN_DIM * (h + 1))
            qh, kh = a[:, sl], a[:, ks]
            rq = lax.rsqrt(jnp.sum(qh * qh, axis=-1, keepdims=True) + EPS)
            rk = lax.rsqrt(jnp.sum(kh * kh, axis=-1, keepdims=True) + EPS)
            qhat, khat = qh * rq, kh * rk
            dyq = dqn_ref[:, sl] * (GDN_DIM ** -0.5)
            dyk = dkn_ref[:, sl]
            dq = rq * (dyq - qhat * jnp.sum(dyq * qhat, axis=-1, keepdims=True))
            dk = rk * (dyk - khat * jnp.sum(dyk * khat, axis=-1, keepdims=True))
            dc_ref[:, sl] = dq * dsl[:, sl]
            dc_ref[:, ks] = dk * dsl[:, ks]
            beta = _sigmoid(ba[:, h:h + 1])
            db = dbeta_ref[:, h:h + 1] * beta * (1.0 - beta)
            aneg = -jnp.exp(al_ref[0:1, h:h + 1])
            xa = ba[:, GDN_HEADS + h:GDN_HEADS + h + 1] + dt_ref[0:1, h:h + 1]
            dgh = dg_ref[:, h:h + 1]
            dxa = dgh * aneg * _sigmoid(xa)
            dba = jnp.where(lane == h, db, dba)
            dba = jnp.where(lane == GDN_HEADS + h, dxa, dba)
            d_alog = jnp.sum(dgh * _softplus(xa), axis=0, keepdims=True) * aneg
            sm = jnp.where(lane1 == h, d_alog, sm)
            sm = jnp.where(lane1 == GDN_HEADS + h, jnp.sum(dxa, axis=0, keepdims=True), sm)
        vs = slice(2 * GDN_WIDTH, W3)
        dc_ref[:, vs] = dv_ref[...] * dsl[:, vs]
        dba_ref[...] = dba

        @pl.when(i == 0)
        def _():
            sm_ref[...] = sm

        @pl.when(i > 0)
        def _():
            sm_ref[...] += sm

    prev_spec, _ = _halo_specs(tm, W3, 0, nblk)
    row = pl.BlockSpec((tm, GDN_WIDTH), lambda i: (i, 0))
    col = pl.BlockSpec((tm, 128), lambda i: (i, 0))
    small = lambda a: pl.BlockSpec(a.shape, lambda i: (0, 0))
    return pl.pallas_call(
        body, name=name, grid=(nblk,),
        in_specs=[row, row, row, col, col, pl.BlockSpec((tm, W3), lambda i: (i, 0)), prev_spec,
                  pl.BlockSpec((tm, 128), lambda i: (i, W3 // 128)), small(conv_w), small(a_log), small(dt_bias)],
        out_specs=[pl.BlockSpec((tm, W3), lambda i: (i, 0)), col, pl.BlockSpec((1, 128), lambda i: (0, 0))],
        out_shape=[_sds((S, W3)), _sds((S, 128)), _sds((1, 128))], compiler_params=_params(("arbitrary",)),
    )(dqn, dkn, dv, dg, dbeta, proj_a, proj_a, proj_a, conv_w, a_log, dt_bias)


def _gdn_conv_bwd(dc, dba, proj_a, conv_w, *, name, tm=256):
    S = proj_a.shape[0]
    nblk = S // tm
    W3 = 3 * GDN_WIDTH

    def body(dc_ref, dnext_ref, dba_ref, cur_ref, prev_ref, cw_ref, da_ref, dcw_ref):
        i = pl.program_id(0)
        prev = jnp.where(i > 0, prev_ref[...], 0.0)
        nxt = jnp.where(i < nblk - 1, dnext_ref[...], 0.0)
        dx, dw = _conv_rows_bwd(dc_ref[...], nxt, prev, cur_ref[...], cw_ref[...], GDN_CONV)
        da_ref[:, 0:W3] = dx.astype(BF16)
        da_ref[:, W3:] = dba_ref[...].astype(BF16)

        @pl.when(i == 0)
        def _():
            dcw_ref[...] = dw

        @pl.when(i > 0)
        def _():
            dcw_ref[...] += dw

    prev_spec, next_spec = _halo_specs(tm, W3, 0, nblk)
    wide = pl.BlockSpec((tm, W3), lambda i: (i, 0))
    return pl.pallas_call(
        body, name=name, grid=(nblk,),
        in_specs=[wide, next_spec, pl.BlockSpec((tm, 128), lambda i: (i, 0)), wide, prev_spec,
                  pl.BlockSpec(conv_w.shape, lambda i: (0, 0))],
        out_specs=[pl.BlockSpec((tm, A_COLS), lambda i: (i, 0)), pl.BlockSpec(conv_w.shape, lambda i: (0, 0))],
        out_shape=[_sds((S, A_COLS), BF16), _sds(conv_w.shape)], compiler_params=_params(("arbitrary",)),
    )(dc, dc, dba, proj_a, proj_a, conv_w)


def _band_mask(nk):
    i = lax.broadcasted_iota(jnp.int32, (2 * BAND, nk), 0) & (BAND - 1)
    j = lax.broadcasted_iota(jnp.int32, (2 * BAND, nk), 1)
    if nk == BAND:
        return j <= i
    return (j >= i) & (j <= i + BAND)


def _stack_heads(x, lo):
    return jnp.concatenate([jnp.where(lo, x, 0.0), jnp.where(lo, 0.0, x)], axis=0)


def _stack_cols(x):
    return jnp.concatenate([x[:, 0:1], x[:, DIL_DIM:DIL_DIM + 1]], axis=0)


def _unstack(x, lo):
    return jnp.where(lo, x[0:BAND], x[BAND:2 * BAND])


def _rows(start, size, stride):
    return pl.ds(start, size) if stride == 1 else pl.ds(start, size, stride=stride)


ATTN_LANES = 4


def _attn_blocks(S, visit_many, lanes=ATTN_LANES):
    for d in DILATIONS:
        nb = S // (d * BAND)
        if d == 1:
            half = nb // 2
            visit_many(d, [(0, 0, True), (0, half, False)])

            def pair(n, c):
                visit_many(1, [(0, n, False), (0, n + half, False)])
                return c
            lax.fori_loop(1, half, pair, 0)
        elif nb > 1:
            for r0 in range(0, d, lanes):
                visit_many(d, [(r0 + t, 0, True) for t in range(lanes)])

                def column(n, c, d=d, r0=r0):
                    visit_many(d, [(r0 + t, n, False) for t in range(lanes)])
                    return c
                lax.fori_loop(1, nb, column, 0)
        else:
            def group(g, c, d=d):
                visit_many(d, [(g * lanes + t, 0, True) for t in range(lanes)])
                return c
            lax.fori_loop(0, d // lanes, group, 0)


def _attn_fwd(proj_b, oab, *, name):
    S = proj_b.shape[0]
    scale = DIL_DIM ** -0.5

    def body(q_ref, k_ref, v_ref, oab_in_ref, ob_ref, lse_ref, m_ref, l_ref, acc_ref):
        del oab_in_ref
        lane = lax.broadcasted_iota(jnp.int32, (BAND, 128), 1)
        lo = lane < DIL_DIM
        m_ref[...] = jnp.full_like(m_ref, NEG_BIG)
        l_ref[...] = jnp.zeros_like(l_ref)
        acc_ref[...] = jnp.zeros_like(acc_ref)

        def load(d, r, n, first):
            nk = BAND if first else 2 * BAND
            qrows = _rows(r + n * (BAND * d), BAND, d)
            krows = _rows(r if first else r + (n - 1) * (BAND * d), nk, d)
            return dict(nk=nk, qrows=qrows, q=q_ref[qrows, :] * scale, k=k_ref[krows, :].astype(BF16),
                        v=v_ref[krows, :].astype(BF16), m=m_ref[qrows, :], l=l_ref[qrows, :], acc=acc_ref[qrows, :])

        def compute(b):
            q, k, v = b["q"], b["k"], b["v"]
            s = jnp.where(_band_mask(b["nk"]), _bdot_nt(_stack_heads(q, lo), k), NEG_BIG)
            m_old = _stack_cols(b["m"])
            m_new = jnp.maximum(m_old, jnp.max(s, axis=-1, keepdims=True))
            p = jnp.exp(s - m_new)
            alpha = _unstack(jnp.exp(m_old - m_new), lo)
            l_new = alpha * b["l"] + _unstack(jnp.sum(p, axis=-1, keepdims=True), lo)
            return _unstack(m_new, lo), l_new, alpha * b["acc"] + _unstack(_bdot(p, v), lo)

        def visit_many(d, blocks):
            loaded = [load(d, *blk) for blk in blocks]
            done = [compute(b) for b in loaded]
            for b, (m_new, l_new, acc_new) in zip(loaded, done):
                m_ref[b["qrows"], :] = m_new
                l_ref[b["qrows"], :] = l_new
                acc_ref[b["qrows"], :] = acc_new

        _attn_blocks(S, visit_many)
        ob_ref[...] = (acc_ref[...] / l_ref[...]).astype(BF16)
        lse_ref[...] = m_ref[...] + jnp.log(l_ref[...])

    part = lambda t: pl.BlockSpec((S, 128), lambda p: (0, 3 * p + t))
    return pl.pallas_call(
        body, name=name, grid=(DIL_PAIRS,),
        in_specs=[part(0), part(1), part(2), pl.BlockSpec(memory_space=pl.ANY)],
        out_specs=[pl.BlockSpec((S, 128), lambda p: (0, GDN_WIDTH // 128 + p)), pl.BlockSpec((S, 128), lambda p: (0, p))],
        out_shape=[_sds(oab.shape, BF16), _sds((S, DIL_WIDTH))],
        scratch_shapes=[pltpu.VMEM((S, 128), F32)] * 3, input_output_aliases={3: 0},
        compiler_params=_params(("parallel",)),
    )(proj_b, proj_b, proj_b, oab)


def _attn_bwd(proj_b, oab, d_oab, lse, *, name):
    S = proj_b.shape[0]
    scale = DIL_DIM ** -0.5

    def body(q_ref, k_ref, v_ref, o_ref, do_ref, lse_ref, dqkv_ref, dq_ref, dk_ref, dv_ref, delta_ref):
        lane = lax.broadcasted_iota(jnp.int32, (BAND, 128), 1)
        lo = lane < DIL_DIM
        dq_ref[...] = jnp.zeros_like(dq_ref)
        dk_ref[...] = jnp.zeros_like(dk_ref)
        dv_ref[...] = jnp.zeros_like(dv_ref)
        prod = do_ref[...] * o_ref[...].astype(F32)
        lo_all = lax.broadcasted_iota(jnp.int32, (S, 128), 1) < DIL_DIM
        delta_ref[...] = jnp.where(lo_all, jnp.sum(jnp.where(lo_all, prod, 0.0), axis=-1, keepdims=True),
                                   jnp.sum(jnp.where(lo_all, 0.0, prod), axis=-1, keepdims=True))

        def load(d, r, n, first):
            nk = BAND if first else 2 * BAND
            qrows = _rows(r + n * (BAND * d), BAND, d)
            krows = _rows(r if first else r + (n - 1) * (BAND * d), nk, d)
            return dict(nk=nk, qrows=qrows, krows=krows, q=q_ref[qrows, :] * scale, k=k_ref[krows, :], v=v_ref[krows, :],
                        do=do_ref[qrows, :], delta=delta_ref[qrows, :], lse=lse_ref[qrows, :],
                        dq=dq_ref[qrows, :], dk=dk_ref[krows, :], dv=dv_ref[krows, :])

        def compute(b):
            q, k, v, do = b["q"], b["k"], b["v"], b["do"]
            qs, dos = _stack_heads(q, lo), _stack_heads(do, lo)
            p = jnp.where(_band_mask(b["nk"]), jnp.exp(_bdot_nt(qs, k) - _stack_cols(b["lse"])), 0.0)
            ds = p * (_bdot_nt(dos, v) - _stack_cols(b["delta"]))
            dq = b["dq"] + _unstack(_bdot(ds, k), lo) * scale
            return dq, b["dk"] + _bdot_tn(ds, qs), b["dv"] + _bdot_tn(p, dos)

        def visit_many(d, blocks):
            loaded = [load(d, *blk) for blk in blocks]
            done = [compute(b) for b in loaded]
            for b, (dq, dk, dv) in zip(loaded, done):
                dq_ref[b["qrows"], :] = dq
                dk_ref[b["krows"], :] = dk
                dv_ref[b["krows"], :] = dv

        _attn_blocks(S, visit_many, lanes=2)
        dqkv_ref[:, 0:128] = dq_ref[...].astype(BF16)
        dqkv_ref[:, 128:256] = dk_ref[...].astype(BF16)
        dqkv_ref[:, 256:384] = dv_ref[...].astype(BF16)

    half = lambda p: (0, GDN_WIDTH // 128 + p)
    part = lambda t: pl.BlockSpec((S, 128), lambda p: (0, 3 * p + t))
    return pl.pallas_call(
        body, name=name, grid=(DIL_PAIRS,),
        in_specs=[part(0), part(1), part(2), pl.BlockSpec((S, 128), half), pl.BlockSpec((S, 128), half),
                  pl.BlockSpec((S, 128), lambda p: (0, p))],
        out_specs=pl.BlockSpec((S, 384), lambda p: (0, p)), out_shape=_sds((S, 3 * DIL_WIDTH), BF16),
        scratch_shapes=[pltpu.VMEM((S, 128), F32)] * 4, compiler_params=_params(("parallel",)),
    )(proj_b, proj_b, proj_b, oab, d_oab, lse)


FF_SLAB = 2 * D_FF // N_DEV
FF_PAIRS = N_DEV // 2
ROWS16 = 16


def _taps(w, x, base, n):
    out = _shifted(x, base, n) * w[0:1]
    for t in range(1, FFN_CONV):
        out = out + _shifted(x, base + t, n) * w[t:t + 1]
    return out


def _ffn_fwd(h2, x1, w_up, conv_w, w_down, final_w, tgt, *, name, tm=512):
    S, D = h2.shape
    ni = S // tm
    per = tm // ROWS16

    def body(h_ref, hp_ref, x1_ref, wg_ref, wu_ref, cg_ref, cu_ref, wd_ref, fw_ref, t_ref,
             dx_ref, dxb_ref, dfw_ref, loss_ref, ug_ref, uu_ref, x2_ref):
        i, j = pl.program_id(0), pl.program_id(1)
        hv = jnp.concatenate([hp_ref[...], h_ref[...]], axis=0)
        row = lax.broadcasted_iota(jnp.int32, (tm + ROWS16, 1), 0)
        keep = (i > 0) | (row >= ROWS16)

        def branch(w_ref, c_ref, u_ref):
            u = lax.dot_general(hv, w_ref[...], (((1,), (1,)), ((), ())), preferred_element_type=F32).astype(BF16)
            u_ref[...] = u[ROWS16:]
            return _taps(c_ref[...], jnp.where(keep, u.astype(F32), 0.0), ROWS16 - (FFN_CONV - 1), tm)

        gate = branch(wg_ref, cg_ref, ug_ref)
        up = branch(wu_ref, cu_ref, uu_ref)
        act = (gate * _sigmoid(gate) * up).astype(BF16)
        part = jnp.dot(act, wd_ref[...], preferred_element_type=F32)

        @pl.when(j == 0)
        def _():
            x2_ref[...] = x1_ref[...] + part

        @pl.when((j > 0) & (j < FF_PAIRS - 1))
        def _():
            x2_ref[...] += part

        @pl.when(j == FF_PAIRS - 1)
        def _():
            xv = x2_ref[...] + part
            wv = fw_ref[...]
            r = lax.rsqrt(jnp.mean(xv * xv, axis=-1, keepdims=True) + EPS)
            err = xv * r * wv - t_ref[...]
            lsum = jnp.sum(jnp.sum(err * err, axis=-1, keepdims=True), axis=0, keepdims=True) * (0.5 / D)
            g = err * (1.0 / D)
            xh = xv * r
            gw = g * wv
            dx = r * (gw - xh * jnp.mean(gw * xh, axis=-1, keepdims=True))
            dx_ref[...] = dx
            dxb_ref[...] = dx.astype(BF16)
            dfw = jnp.sum(g * xh, axis=0, keepdims=True)
            lpart = jnp.broadcast_to(lsum, (1, 128))

            @pl.when(i == 0)
            def _():
                dfw_ref[...] = dfw
                loss_ref[...] = lpart

            @pl.when(i > 0)
            def _():
                dfw_ref[...] += dfw
                loss_ref[...] += lpart

    rows = pl.BlockSpec((tm, D), lambda i, j: (i, 0))
    slab = lambda off: pl.BlockSpec((None, FF_SLAB, D), lambda i, j: (j + off, 0, 0))
    cslab = lambda off: pl.BlockSpec((None, FFN_CONV, FF_SLAB), lambda i, j: (j + off, 0, 0))
    uspec = pl.BlockSpec((None, tm, FF_SLAB), lambda i, j: (j, i, 0))
    return pl.pallas_call(
        body, name=name, grid=(ni, FF_PAIRS),
        in_specs=[rows, pl.BlockSpec((ROWS16, D), lambda i, j: (jnp.maximum(i * per - 1, 0), 0)), rows,
                  slab(0), slab(FF_PAIRS), cslab(0), cslab(FF_PAIRS), pl.BlockSpec((FF_SLAB, D), lambda i, j: (j, 0)),
                  pl.BlockSpec((1, D), lambda i, j: (0, 0)), rows],
        out_specs=[rows, rows, pl.BlockSpec((1, D), lambda i, j: (0, 0)), pl.BlockSpec((1, 128), lambda i, j: (0, 0)), uspec, uspec],
        out_shape=[_sds((S, D)), _sds((S, D), BF16), _sds((1, D)), _sds((1, 128)),
                   _sds((FF_PAIRS, S, FF_SLAB), BF16), _sds((FF_PAIRS, S, FF_SLAB), BF16)],
        scratch_shapes=[pltpu.VMEM((tm, D), F32)],
        compiler_params=_params(("arbitrary", "arbitrary")),
    )(h2, h2, x1, w_up, w_up, conv_w, conv_w, w_down, final_w, tgt)


def _ffn_bwd(dx2, h2, ug, uu, conv_w, w_down, *, name, tm=512):
    S, D = h2.shape
    ni = S // tm
    per = tm // ROWS16
    ext = tm + ROWS16

    def body(dx_ref, dxn_ref, h_ref, ug_ref, ugp_ref, ugn_ref, uu_ref, uup_ref, uun_ref, cg_ref, cu_ref, wd_ref,
             dug_ref, duu_ref, gd_ref, gg_ref, gu_ref, dcg_ref, dcu_ref, acc_d, acc_g, acc_u, acc_cg, acc_cu):
        i = pl.program_id(1)

        @pl.when(i == 0)
        def _():
            acc_d[...] = jnp.zeros_like(acc_d)
            acc_g[...] = jnp.zeros_like(acc_g)
            acc_u[...] = jnp.zeros_like(acc_u)
            acc_cg[...] = jnp.zeros_like(acc_cg)
            acc_cu[...] = jnp.zeros_like(acc_cu)

        dx = dx_ref[...]
        dxe = jnp.concatenate([dx, dxn_ref[...]], axis=0)
        row = lax.broadcasted_iota(jnp.int32, (ext, 1), 0)
        live = (i < ni - 1) | (row < tm)
        d_act = jnp.where(live, lax.dot_general(dxe, wd_ref[...], (((1,), (1,)), ((), ())), preferred_element_type=F32), 0.0)
        rowp = lax.broadcasted_iota(jnp.int32, (ext + ROWS16, 1), 0)
        keep = (i > 0) | (rowp >= ROWS16)

        def pre(cur, prev, nxt):
            return jnp.where(keep, jnp.concatenate([prev[...], cur[...], nxt[...]], axis=0).astype(F32), 0.0)

        uge, uue = pre(ug_ref, ugp_ref, ugn_ref), pre(uu_ref, uup_ref, uun_ref)
        cg, cu = cg_ref[...], cu_ref[...]
        base = ROWS16 - (FFN_CONV - 1)
        gate = _taps(cg, uge, base, ext)
        up = _taps(cu, uue, base, ext)
        sg = _sigmoid(gate)
        silu = gate * sg
        dgc = d_act * up * _dsilu(gate, sg)
        duc = d_act * silu

        def conv_t(w, dc):
            out = _shifted(dc, FFN_CONV - 1, tm) * w[0:1]
            for t in range(1, FFN_CONV):
                out = out + _shifted(dc, FFN_CONV - 1 - t, tm) * w[t:t + 1]
            return out.astype(BF16)

        du_g, du_u = conv_t(cg, dgc), conv_t(cu, duc)
        dug_ref[...] = du_g
        duu_ref[...] = du_u
        dcw = lambda dc, xe: jnp.concatenate(
            [jnp.sum(dc[0:tm] * _shifted(xe, base + t, tm), axis=0, keepdims=True) for t in range(FFN_CONV)], axis=0)
        acc_cg[0:FFN_CONV, :] += dcw(dgc, uge)
        acc_cu[0:FFN_CONV, :] += dcw(duc, uue)
        tn = (((0,), (0,)), ((), ()))
        act = (silu[0:tm] * up[0:tm]).astype(BF16)
        acc_d[...] += lax.dot_general(act, dx, tn, preferred_element_type=F32)
        hv = h_ref[...]
        acc_g[...] += lax.dot_general(du_g, hv, tn, preferred_element_type=F32)
        acc_u[...] += lax.dot_general(du_u, hv, tn, preferred_element_type=F32)

        @pl.when(i == ni - 1)
        def _():
            gd_ref[...] = acc_d[...].astype(BF16)
            gg_ref[...] = acc_g[...].astype(BF16)
            gu_ref[...] = acc_u[...].astype(BF16)
            dcg_ref[...] = acc_cg[0:FFN_CONV, :]
            dcu_ref[...] = acc_cu[0:FFN_CONV, :]

    last16 = S // ROWS16 - 1
    rows = pl.BlockSpec((tm, D), lambda j, i: (i, 0))
    rows_next = pl.BlockSpec((ROWS16, D), lambda j, i: (jnp.minimum((i + 1) * per, last16), 0))
    u_cur = pl.BlockSpec((None, tm, FF_SLAB), lambda j, i: (j, i, 0))
    u_prev = pl.BlockSpec((None, ROWS16, FF_SLAB), lambda j, i: (j, jnp.maximum(i * per - 1, 0), 0))
    u_next = pl.BlockSpec((None, ROWS16, FF_SLAB), lambda j, i: (j, jnp.minimum((i + 1) * per, last16), 0))
    cslab = lambda off: pl.BlockSpec((None, FFN_CONV, FF_SLAB), lambda j, i: (j + off, 0, 0))
    wslab = pl.BlockSpec((None, FF_SLAB, D), lambda j, i: (j, 0, 0))
    dslab = pl.BlockSpec((None, FFN_CONV, FF_SLAB), lambda j, i: (j, 0, 0))
    return pl.pallas_call(
        body, name=name, grid=(FF_PAIRS, ni),
        in_specs=[rows, rows_next, rows, u_cur, u_prev, u_next, u_cur, u_prev, u_next, cslab(0), cslab(FF_PAIRS),
                  pl.BlockSpec((FF_SLAB, D), lambda j, i: (j, 0))],
        out_specs=[u_cur, u_cur, pl.BlockSpec((FF_SLAB, D), lambda j, i: (j, 0)), wslab, wslab, dslab, dslab],
        out_shape=[_sds((FF_PAIRS, S, FF_SLAB), BF16), _sds((FF_PAIRS, S, FF_SLAB), BF16), _sds((D_FF, D), BF16),
                   _sds((FF_PAIRS, FF_SLAB, D), BF16), _sds((FF_PAIRS, FF_SLAB, D), BF16),
                   _sds((FF_PAIRS, FFN_CONV, FF_SLAB)), _sds((FF_PAIRS, FFN_CONV, FF_SLAB))],
        scratch_shapes=[pltpu.VMEM((FF_SLAB, D), F32), pltpu.VMEM((FF_SLAB, D), F32), pltpu.VMEM((FF_SLAB, D), F32),
                        pltpu.VMEM((8, FF_SLAB), F32), pltpu.VMEM((8, FF_SLAB), F32)],
        compiler_params=_params(("parallel", "arbitrary")),
    )(dx2, dx2, h2, ug, ug, ug, uu, uu, uu, conv_w, conv_w, w_down)


def _mm_slabs(a, w, w_off, *, name, res=None, norm_bwd=None, after=(), tm=1024, tn=1024):
    nk, S, _ = a.shape
    D = w.shape[2]
    has_res = res is not None
    has_norm = norm_bwd is not None
    assert not has_norm or tn == D

    def body(*refs):
        a_ref, w_ref = refs[:2]
        r_ref = refs[2] if has_res else None
        if has_norm:
            x_ref, nw_ref, skip_ref = refs[2 + has_res:5 + has_res]
            o_ref, dw_ref, acc_ref = refs[-3:]
        else:
            o_ref, acc_ref = refs[-2:]
        i, k = pl.program_id(0), pl.program_id(2)
        part = jnp.dot(a_ref[...], w_ref[...], preferred_element_type=F32)

        @pl.when(k == 0)
        def _():
            acc_ref[...] = part

        @pl.when(k > 0)
        def _():
            acc_ref[...] += part

        @pl.when(k == nk - 1)
        def _():
            r = acc_ref[...] + r_ref[...] if has_res else acc_ref[...]
            if has_norm:
                dx, dw = _rms_bwd_rows(r, x_ref[...], nw_ref[...])
                o_ref[...] = skip_ref[...] + dx

                @pl.when(i == 0)
                def _():
                    dw_ref[...] = dw

                @pl.when(i > 0)
                def _():
                    dw_ref[...] += dw
            else:
                o_ref[...] = r

    o_spec = pl.BlockSpec((tm, tn), lambda i, j, k: (i, j))
    one = pl.BlockSpec((1, tn), lambda i, j, k: (0, 0))
    return pl.pallas_call(
        body, name=name, grid=(S // tm, D // tn, nk),
        in_specs=[pl.BlockSpec((None, tm, FF_SLAB), lambda i, j, k: (k, i, 0)),
                  pl.BlockSpec((None, FF_SLAB, tn), lambda i, j, k: (k + w_off, 0, j))] + [o_spec] * has_res
        + ([o_spec, one, o_spec] if has_norm else []) + [ANY] * len(after),
        out_specs=[o_spec, one] if has_norm else o_spec, out_shape=[_sds((S, D)), _sds((1, D))] if has_norm else _sds((S, D)),
        scratch_shapes=[pltpu.VMEM((tm, tn), F32)],
        compiler_params=_params(("arbitrary" if has_norm else "parallel", "parallel", "arbitrary")),
    )(*((a, w) + ((res,) if has_res else ()) + (tuple(norm_bwd) if has_norm else ()) + tuple(after)))


def _local_step(x, tgt, norm1_w, w_a, w_z, w_b, conv_a, a_log, dt_bias, gnw, norm2_w, final_w, late_weights, emit, start_after=()):
    wgrad = functools.partial(_mm, ta=True, out_dtype=BF16)
    h1, proj_a, proj_z, proj_b = _in_proj(x, norm1_w, w_a, w_z, w_b, after=start_after, name="in_proj")
    qn, kn, v, gcb, bb = _gdn_prep_fwd(proj_a, conv_a, a_log, dt_bias, name="gdn_prep_fwd")
    uv, wk, at, tmat, wkb, qdb, keb = _gdn_chunk_fwd(qn, kn, v, gcb, bb, name="gdn_chunk_fwd")
    o, u, sp, oab = _gdn_scan_fwd(uv, at, wkb, qdb, keb, gcb, proj_z, gnw, name="gdn_scan_fwd")
    oab, lse = _attn_fwd(proj_b, oab, name="attn_fwd")
    w_out, w_up, conv_f, w_down = late_weights(oab)
    x1, h2 = _out_proj_norm(oab, w_out, x, norm2_w, name="out_proj")
    dx2, dx2_b, d_final, loss, ug, uu = _ffn_fwd(h2, x1, w_up, conv_f, w_down, final_w, tgt, name="ffn_fwd")
    dug, duu, g_down, g_up_g, g_up_u, dcw_g, dcw_u = _ffn_bwd(dx2_b, h2, ug, uu, conv_f, w_down, name="ffn_bwd")
    token = emit("ffn", w_down=g_down, w_up=jnp.concatenate([g_up_g, g_up_u], axis=0),
                 conv_f=jnp.concatenate([dcw_g, dcw_u], axis=0))
    dh2 = _mm_slabs(dug, w_up, 0, name="ffn_up_dx_gate")
    dx1, d_norm2 = _mm_slabs(duu, w_up, FF_PAIRS, res=dh2, norm_bwd=(x1, norm2_w, dx2), after=token, name="ffn_up_dx_up")
    d_oab = _mm(dx1, w_out, tb=True, name="out_proj_dx", tk=1024)
    token = emit("out", w_out=wgrad(oab, dx1, name="out_proj_dw"))
    dz, d_gnw, du, dwk, dat, dqd, dke, dgl = _gdn_scan_bwd(d_oab, o, proj_z, gnw, sp, u, at, wkb, qdb, keb, gcb, after=token, name="gdn_scan_bwd")
    dqn, dkn, dv, dg, dbeta = _gdn_chunk_bwd(qn, kn, gcb, bb, tmat, uv, wk, du, dwk, dat, dqd, dke, dgl, name="gdn_chunk_bwd")
    dc, dba, d_small = _gdn_prep_bwd(dqn, dkn, dv, dg, dbeta, proj_a, conv_a, a_log, dt_bias, name="gdn_prep_bwd")
    d_pa, d_conv_a = _gdn_conv_bwd(dc, dba, proj_a, conv_a, name="gdn_conv_bwd")
    d_pb = _attn_bwd(proj_b, oab, d_oab, lse, name="attn_bwd")
    g_a = wgrad(d_pa, h1, name="proj_a_dw", tm=A_COLS)
    g_z = wgrad(dz, h1, name="proj_z_dw")
    g_b = wgrad(d_pb, h1, name="proj_b_dw", tm=768)
    token = emit("in", w_a=g_a, w_z=g_z, w_b=g_b, conv_a=d_conv_a)
    dh1 = _mm(dz, w_z, after=token, name="proj_z_dx")
    dh1 = _mm(d_pa, w_a, res=dh1, name="proj_a_dx", tk=A_COLS)
    grad_x, d_norm1 = _mm(d_pb, w_b, res=dh1, norm_bwd=(x, norm1_w, dx1), name="proj_b_dx", tn=D_MODEL, tk=1536)
    small = dict(norm1=d_norm1, small=d_small, gnw=d_gnw, norm2=d_norm2, final=d_final)
    return loss, grad_x, small


_O1 = 3 * GDN_WIDTH
_O2 = _O1 + GDN_WIDTH
_O3 = _O2 + 2 * GDN_HEADS


def _split_w_in(w_t):
    d = w_t.shape[1]
    pad = jnp.zeros((A_COLS - _O1 - 2 * GDN_HEADS, d), w_t.dtype)
    w_a = jnp.concatenate([w_t[:_O1], w_t[_O2:_O3], pad], axis=0)
    w_b = w_t[_O3:].reshape(3, DIL_PAIRS, 128, d).transpose(1, 0, 2, 3).reshape(3 * DIL_WIDTH, d)
    return w_a, w_t[_O1:_O2], w_b


def _merge_g_in(g_a, g_z, g_b):
    d = g_a.shape[1]
    g_b = g_b.reshape(DIL_PAIRS, 3, 128, d).transpose(1, 0, 2, 3).reshape(3 * DIL_WIDTH, d)
    return jnp.concatenate([g_a[:_O1], g_z, g_a[_O1:_O1 + 2 * GDN_HEADS], g_b], axis=0)


MESH = pl.DeviceIdType.MESH
ANY = pl.BlockSpec(memory_space=pl.ANY)


def _position():
    return lax.axis_index("x"), lax.axis_index("y"), lax.axis_index("c")


def _slot(p):
    return 4 * p[0] + 2 * p[1] + p[2]


def _all_gather(blocks, *, name):
    n = len(blocks)

    def body(*refs):
        ins, outs = refs[:n], refs[n:2 * n]
        send_sems, recv_sems, local_sems = refs[2 * n:]
        x, y, c = _position()
        me, sibling = (x, y, c), (x, y, 1 - c)
        chips = [(1 - x, y), (x, 1 - y), (1 - x, 1 - y)]

        def copy(a, k, block, to, src=None):
            dst = outs[a].at[_slot(block)]
            return pltpu.make_async_remote_copy(
                src_ref=dst if src is None else src, dst_ref=dst, send_sem=send_sems.at[a, k], recv_sem=recv_sems.at[a, k],
                device_id=to, device_id_type=MESH)

        mine = [pltpu.make_async_copy(ins[a], outs[a].at[_slot(me)], local_sems.at[a]) for a in range(n)]
        for cp in mine:
            cp.start()
        first = []
        for a in range(n):
            first.append(copy(a, 0, me, sibling, src=ins[a]))
            first += [copy(a, 1 + j, me, (*chip, c), src=ins[a]) for j, chip in enumerate(chips)]
        for cp in first:
            cp.start()
        passed = []
        for j, chip in enumerate(chips):
            for a in range(n):
                copy(a, 1 + j, (*chip, c), me).wait_recv()
                fwd = copy(a, 4 + j, (*chip, c), sibling)
                fwd.start()
                passed.append(fwd)
        for a in range(n):
            copy(a, 0, sibling, me).wait_recv()
            for j, chip in enumerate(chips):
                copy(a, 4 + j, (*chip, 1 - c), me).wait_recv()
        for cp in first + passed:
            cp.wait_send()
        for cp in mine:
            cp.wait()

    return pl.pallas_call(
        body, name=name, in_specs=[ANY] * n, out_specs=[ANY] * n,
        out_shape=[_sds((N_DEV,) + b.shape, b.dtype) for b in blocks],
        scratch_shapes=[pltpu.SemaphoreType.DMA((n, 7)), pltpu.SemaphoreType.DMA((n, 7)), pltpu.SemaphoreType.DMA((n,))],
    )(*blocks)


def _gather_direct(block, *, name, after=()):
    def body(in_ref, *rest):
        out_ref, send_sems, recv_sems, local_sem = rest[len(after):]
        x, y, c = _position()
        me = _slot((x, y, c))
        mine = pltpu.make_async_copy(in_ref, out_ref.at[me], local_sem)
        mine.start()
        copies = [pltpu.make_async_remote_copy(
            src_ref=in_ref, dst_ref=out_ref.at[me], send_sem=send_sems.at[k - 1], recv_sem=recv_sems.at[k - 1],
            device_id=_peer_of(k, x, y, c), device_id_type=MESH) for k in range(1, N_DEV)]
        for cp in copies:
            cp.start()
        for cp in copies:
            cp.wait()
        mine.wait()

    return pl.pallas_call(
        body, name=name, in_specs=[pl.BlockSpec(memory_space=pltpu.VMEM)] + [ANY] * len(after),
        out_specs=pl.BlockSpec(memory_space=pltpu.VMEM),
        out_shape=_sds((N_DEV,) + block.shape, block.dtype),
        scratch_shapes=[pltpu.SemaphoreType.DMA((N_DEV - 1,)), pltpu.SemaphoreType.DMA((N_DEV - 1,)), pltpu.SemaphoreType.DMA],
    )(block, *after)


HBM = pl.BlockSpec(memory_space=pltpu.HBM)
SEM = pl.BlockSpec(memory_space=pltpu.SEMAPHORE)
EFFECT = pltpu.SideEffectType.DATAFLOW_SIDE_EFFECTING


def _peer_of(k, x, y, c):
    return (1 - x if k & 4 else x, 1 - y if k & 2 else y, 1 - c if k & 1 else c)


def _flight(a, k):
    return a * (N_DEV - 1) + k - 1


def _exchange_start(arrays, *, name, broadcast=False):
    n = len(arrays)

    def body(*refs):
        ins, lands = refs[:n], refs[n:2 * n]
        send_sems, recv_sems = refs[2 * n:2 * n + 2]
        token = refs[-1]
        x, y, c = _position()
        me = _slot((x, y, c))
        for k in range(1, N_DEV):
            peer = _peer_of(k, x, y, c)
            for a in range(n):
                pltpu.make_async_remote_copy(
                    src_ref=ins[a] if broadcast else ins[a].at[_slot(peer)], dst_ref=lands[a].at[me],
                    send_sem=send_sems.at[_flight(a, k)], recv_sem=recv_sems.at[_flight(a, k)],
                    device_id=peer, device_id_type=MESH).start()
        token[...] = jnp.zeros_like(token)

    land_shapes = [((N_DEV,) + s.shape) if broadcast else s.shape for s in arrays]
    lands = [pltpu.with_memory_space_constraint(lax.empty(shp, s.dtype), pltpu.HBM) for shp, s in zip(land_shapes, arrays)]
    srcs = [pltpu.with_memory_space_constraint(s, pltpu.HBM) for s in arrays]
    outs = pl.pallas_call(
        body, name=name, in_specs=[HBM] * (2 * n),
        out_specs=[SEM, SEM] + [HBM] * (2 * n) + [pl.BlockSpec(memory_space=pltpu.VMEM)],
        out_shape=[pltpu.SemaphoreType.DMA((n * (N_DEV - 1),)), pltpu.SemaphoreType.DMA((n * (N_DEV - 1),))]
        + [pltpu.HBM(s.shape, s.dtype) for s in arrays] + [pltpu.HBM(shp, s.dtype) for shp, s in zip(land_shapes, arrays)]
        + [_sds((8, 128))],
        input_output_aliases={i: 2 + i for i in range(2 * n)},
        compiler_params=pltpu.CompilerParams(has_side_effects=EFFECT),
    )(*srcs, *lands)
    return outs[0], outs[1], outs[2:2 + n], outs[2 + n:2 + 2 * n], outs[-1]


def _exchange_wait(send_sems, recv_sems, srcs, lands, after, *, name, broadcast=False):
    n = len(srcs)

    def body(*refs):
        ins, lnd = refs[:n], refs[n:2 * n]
        send_ref, recv_ref = refs[2 * n:2 * n + 2]
        x, y, c = _position()
        for k in range(1, N_DEV):
            for a in range(n):
                cp = pltpu.make_async_remote_copy(
                    src_ref=ins[a] if broadcast else ins[a].at[0], dst_ref=lnd[a].at[0], send_sem=send_ref.at[_flight(a, k)],
                    recv_sem=recv_ref.at[_flight(a, k)], device_id=_peer_of(k, x, y, c), device_id_type=MESH)
                cp.wait_send()
                cp.wait_recv()

    outs = pl.pallas_call(
        body, name=name, in_specs=[HBM] * (2 * n) + [SEM, SEM, ANY], out_specs=[HBM] * (2 * n),
        out_shape=[pltpu.HBM(s.shape, s.dtype) for s in srcs] + [pltpu.HBM(s.shape, s.dtype) for s in lands],
        input_output_aliases={i: i for i in range(2 * n)},
        compiler_params=pltpu.CompilerParams(has_side_effects=EFFECT),
    )(*srcs, *lands, send_sems, recv_sems, after)
    return outs[:n], outs[n:]


def _with_own(landed, srcs, me):
    return [lax.dynamic_update_index_in_dim(l, o, me, 0) for l, o in zip(landed, srcs)]


def _adam_update(g, w, m, v):
    c1 = 1.0 - ADAM_B1 ** ADAM_STEP
    c2 = 1.0 - ADAM_B2 ** ADAM_STEP
    nm = ADAM_B1 * m + (1.0 - ADAM_B1) * g
    nv = ADAM_B2 * v + (1.0 - ADAM_B2) * (g * g)
    return -ADAM_LR * ((nm / c1) / (jnp.sqrt(nv / c2) + ADAM_EPS) + ADAM_WD * w), nm, nv


def _adamw(landed, sent, me, w, m, v, *, name, tr=None, tc=None):
    R, C = w.shape
    tr = R if tr is None else tr
    tc = C if tc is None else tc
    assert R % tr == 0 and C % tc == 0

    def body(me_ref, own_ref, p_ref, w_ref, m_ref, v_ref, g_ref, d_ref, nm_ref, nv_ref, token_ref):
        token_ref[...] = jnp.zeros_like(token_ref)
        g = own_ref[...].astype(F32)
        for s in range(N_DEV):
            g = g + jnp.where(me_ref[0] == s, 0.0, p_ref[s].astype(F32))
        delta, nm, nv = _adam_update(g, w_ref[...], m_ref[...], v_ref[...])
        g_ref[...] = g
        nm_ref[...] = nm
        nv_ref[...] = nv
        d_ref[...] = delta

    blk = pl.BlockSpec((tr, tc), lambda i, j, me_ref: (i, j))
    return pl.pallas_call(
        body, name=name,
        grid_spec=pltpu.PrefetchScalarGridSpec(
            num_scalar_prefetch=1, grid=(R // tr, C // tc),
            in_specs=[pl.BlockSpec((None, tr, tc), lambda i, j, me_ref: (me_ref[0], i, j)),
                      pl.BlockSpec((N_DEV, tr, tc), lambda i, j, me_ref: (0, i, j)), blk, blk, blk],
            out_specs=[blk] * 4 + [pl.BlockSpec((8, 128), lambda i, j, me_ref: (0, 0))]),
        out_shape=[_sds((R, C))] * 4 + [_sds((8, 128))],
        compiler_params=_params(("arbitrary", "arbitrary")),
    )(me, sent, landed, w, m, v)


_SMALL_ROWS = 8
_SMALL_SLOTS = ((0, 0, D_MODEL), (1, 0, D_MODEL), (2, 0, D_MODEL), (3, 0, GDN_DIM), (3, GDN_DIM, GDN_HEADS),
                (3, GDN_DIM + GDN_HEADS, GDN_HEADS))
_LOSS_LANE = 2 * GDN_DIM


def _pack_small(norm1, norm2, final, gnw, a_log, dt_bias, loss):
    row3 = jnp.concatenate([gnw, a_log, dt_bias, jnp.zeros((1, 128 - 2 * GDN_HEADS), F32), loss,
                            jnp.zeros((1, D_MODEL - 3 * 128), F32)], axis=1)
    return jnp.concatenate([norm1, norm2, final, row3, jnp.zeros((_SMALL_ROWS - 4, D_MODEL), F32)], axis=0)


def _adamw_small(packs, ws, ms, vs, *, name):
    n = len(ws)

    def body(p_ref, *refs):
        w_refs, m_refs, v_refs = refs[:n], refs[n:2 * n], refs[2 * n:3 * n]
        outs = refs[3 * n:]
        g_all = p_ref[0]
        for s in range(1, N_DEV):
            g_all = g_all + p_ref[s]
        for i, (row, lane, width) in enumerate(_SMALL_SLOTS):
            g = g_all[row:row + 1, lane:lane + width]
            delta, nm, nv = _adam_update(g, w_refs[i][...], m_refs[i][...], v_refs[i][...])
            for o_ref, val in zip(outs[4 * i:4 * i + 4], (g, delta, nm, nv)):
                o_ref[...] = val
        outs[-1][...] = g_all[3:4, _LOSS_LANE:_LOSS_LANE + 128]

    vm = pl.BlockSpec(memory_space=pltpu.VMEM)
    outs = pl.pallas_call(
        body, name=name, in_specs=[vm] * (1 + 3 * n), out_specs=[vm] * (4 * n + 1),
        out_shape=[_sds(w.shape) for w in ws for _ in range(4)] + [_sds((1, 128))],
    )(packs, *ws, *ms, *vs)
    return [outs[4 * i:4 * i + 4] for i in range(n)], outs[-1]


def _slabs_by_cols(g):
    r = g.shape[0]
    return g.reshape(r, N_DEV, -1).transpose(1, 0, 2)


def _cols_from_slabs(s):
    return s.transpose(1, 0, 2).reshape(s.shape[1], -1)


def kernel(x, norm1_w, w_in, conv_qkv_w, a_log, dt_bias, gdn_norm_w, w_out, norm2_w, w_up, ffn_conv_w, w_down, final_norm_w, loss_target, m_norm1_w, m_w_in, m_conv_qkv_w, m_a_log, m_dt_bias, m_gdn_norm_w, m_w_out, m_norm2_w, m_w_up, m_ffn_conv_w, m_w_down, m_final_norm_w, v_norm1_w, v_w_in, v_conv_qkv_w, v_a_log, v_dt_bias, v_gdn_norm_w, v_w_out, v_norm2_w, v_w_up, v_ffn_conv_w, v_w_down, v_final_norm_w):
    bf = lambda a: a.astype(BF16)
    me = _slot(_position())
    t_in = lambda a: a[0].T
    gw_in, g_conv_a = _all_gather([bf(t_in(w_in)), conv_qkv_w[0]], name="gather_w_in")
    w_a, w_z, w_b = _split_w_in(gw_in.reshape(-1, D_MODEL))
    late_src, _ = lax.optimization_barrier(([bf(w_out[0]), bf(t_in(w_up)), bf(w_down[0]), ffn_conv_w[0]], gw_in))
    l_send, l_recv, l_srcs, l_lands, l_token = _exchange_start(late_src, name="weights_start", broadcast=True)

    def late_weights(after):
        srcs, landed = _exchange_wait(l_send, l_recv, l_srcs, l_lands, after, name="weights_wait", broadcast=True)
        gw_out, gw_up, gw_down, g_conv_f = _with_own(landed, srcs, me)
        return gw_out.reshape(D_MODEL, D_MODEL), gw_up, g_conv_f, gw_down.reshape(D_FF, D_MODEL)

    flights = {}

    def emit(group, **grads):
        if group == "in":
            slabs = dict(w_in=_merge_g_in(grads["w_a"], grads["w_z"], grads["w_b"]).reshape(N_DEV, -1, D_MODEL),
                         conv_a=_slabs_by_cols(grads["conv_a"]))
        elif group == "ffn":
            slabs = dict(w_down=grads["w_down"].reshape(N_DEV, -1, D_MODEL), w_up=grads["w_up"], conv_f=grads["conv_f"])
        else:
            slabs = {k: v.reshape(N_DEV, -1, D_MODEL) for k, v in grads.items()}
        names = list(slabs)
        *flight, token = _exchange_start([slabs[k] for k in names], name="grads_start_" + group)
        flights[group] = (names, flight)
        return (token,)

    loss, grad_x, g = _local_step(
        x[0], loss_target[0], norm1_w, w_a, w_z, w_b, _cols_from_slabs(g_conv_a), a_log, dt_bias,
        gdn_norm_w, norm2_w, final_norm_w[None], late_weights, emit, start_after=(l_token,))
    got = {}
    me1 = jnp.reshape(me, (1,)).astype(jnp.int32)

    def collect(group, after):
        names, (send_sems, recv_sems, srcs, lands) = flights[group]
        srcs, landed = _exchange_wait(send_sems, recv_sems, srcs, lands, after, name="grads_wait_" + group)
        got.update(zip(names, zip(landed, srcs)))

    def update(key, w, m, v, **tiles):
        return _adamw(*got[key], me1, w, m, v, name="adamw_" + key, **tiles)

    collect("ffn", grad_x)
    collect("out", grad_x)
    *o_out, t1 = update("w_out", w_out[0], m_w_out[0], v_w_out[0])
    *o_up, t2 = update("w_up", t_in(w_up), t_in(m_w_up), t_in(v_w_up), tr=176)
    o_up = [o.T for o in o_up]
    *o_down, t3 = update("w_down", w_down[0], m_w_down[0], v_w_down[0], tr=176)
    *o_cf, t4 = update("conv_f", ffn_conv_w[0], m_ffn_conv_w[0], v_ffn_conv_w[0])
    pack = _pack_small(g["norm1"], g["norm2"], g["final"], g["gnw"], g["small"][:, 0:GDN_HEADS],
                       g["small"][:, GDN_HEADS:2 * GDN_HEADS], loss)
    small_all = _gather_direct(pack, after=(t1, t2, t3, t4), name="gather_small")
    collect("in", small_all)
    o_in = [o.T for o in update("w_in", t_in(w_in), t_in(m_w_in), t_in(v_w_in), tc=256)[:4]]
    o_ca = update("conv_a", conv_qkv_w[0], m_conv_qkv_w[0], v_conv_qkv_w[0])
    (o_n1, o_n2, o_fin, o_gn, o_al, o_dt), total = _adamw_small(
        small_all, (norm1_w, norm2_w, final_norm_w[None], gdn_norm_w, a_log, dt_bias),
        (m_norm1_w, m_norm2_w, m_final_norm_w[None], m_gdn_norm_w, m_a_log, m_dt_bias),
        (v_norm1_w, v_norm2_w, v_final_norm_w[None], v_gdn_norm_w, v_a_log, v_dt_bias), name="adamw_small")
    outs = [total[0, 0], grad_x[None]]
    for k in range(4):
        outs += [o_n1[k], o_in[k][None], o_ca[k][None], o_al[k], o_dt[k], o_gn[k], o_out[k][None], o_n2[k], o_up[k][None],
                 o_cf[k][None], o_down[k][None], o_fin[k][0]]
    return tuple(outs)
```

```python
import functools

import jax
import jax.numpy as jnp
from jax import lax
from jax.experimental import pallas as pl
from jax.experimental.pallas import tpu as pltpu

F32 = jnp.float32
BF16 = jnp.bfloat16

N_DEV = 8
D_MODEL = 1024
GDN_HEADS = 4
GDN_DIM = 128
GDN_WIDTH = GDN_HEADS * GDN_DIM
GDN_CONV = 4
CHUNK = 64
CHUNKS_PER_STEP = 4
DIL_HEADS = 8
DIL_DIM = 64
DIL_WIDTH = DIL_HEADS * DIL_DIM
DIL_PAIRS = DIL_HEADS // 2
DILATIONS = (1, 4, 16)
BAND = 128
D_FF = 2816
FFN_CONV = 3
EPS = 1e-6
A_COLS = 3 * GDN_WIDTH + 128
HALO = 8

ADAM_LR = 0.001
ADAM_B1 = 0.9
ADAM_B2 = 0.999
ADAM_EPS = 1e-08
ADAM_WD = 0.01
ADAM_STEP = 10

VMEM_LIMIT_BYTES = 56 * 1024 * 1024
NEG_BIG = -1e30


def _params(sem=None):
    return pltpu.CompilerParams(dimension_semantics=sem, vmem_limit_bytes=VMEM_LIMIT_BYTES)


def _sds(shape, dtype=F32):
    return jax.ShapeDtypeStruct(shape, dtype)


def _bdot(a, b):
    return jnp.dot(a.astype(BF16), b.astype(BF16), preferred_element_type=F32)


def _bdot_nt(a, b):
    return lax.dot_general(a.astype(BF16), b.astype(BF16), (((1,), (1,)), ((), ())), preferred_element_type=F32)


def _bdot_tn(a, b):
    return lax.dot_general(a.astype(BF16), b.astype(BF16), (((0,), (0,)), ((), ())), preferred_element_type=F32)


def _split(a):
    hi = a.astype(BF16)
    lo = (a - hi.astype(F32)).astype(BF16)
    return hi, lo


def _dot3(a, b, dims):
    ah, al = _split(a)
    bh, bl = _split(b)
    d = functools.partial(lax.dot_general, dimension_numbers=(dims, ((), ())), preferred_element_type=F32)
    return d(ah, bh) + (d(al, bh) + d(ah, bl))


def _exact_tri_dot(tri, g):
    g1 = g.astype(BF16)
    r1 = g - g1.astype(F32)
    g2 = r1.astype(BF16)
    g3 = (r1 - g2.astype(F32)).astype(BF16)
    t = tri.astype(BF16)
    d = functools.partial(jnp.dot, preferred_element_type=F32)
    return d(t, g1) + (d(t, g2) + d(t, g3))


def _sigmoid(x):
    return 1.0 / (1.0 + jnp.exp(-x))


def _dsilu(x, sg):
    return sg * (1.0 + x * (1.0 - sg))


def _rms_bwd_rows(dh, x, w):
    r = lax.rsqrt(jnp.mean(x * x, axis=-1, keepdims=True) + EPS)
    xh = x * r
    gw = dh * w
    return r * (gw - xh * jnp.mean(gw * xh, axis=-1, keepdims=True)), jnp.sum(dh * xh, axis=0, keepdims=True)


def _mm(a, b, *, name, ta=False, tb=False, res=None, norm_bwd=None, after=(), out_dtype=F32, tm=512, tn=512, tk=512):
    if ta:
        K, M = a.shape
    else:
        M, K = a.shape
    if tb:
        N, Kb = b.shape
    else:
        Kb, N = b.shape
    assert K == Kb, (a.shape, b.shape)
    tm, tn, tk = min(tm, M), min(tn, N), min(tk, K)
    assert M % tm == 0 and N % tn == 0 and K % tk == 0, (name, M, N, K, tm, tn, tk)
    nk = K // tk
    dims = (((0 if ta else 1,), (1 if tb else 0,)), ((), ()))
    has_res = res is not None
    has_norm = norm_bwd is not None
    assert not has_norm or tn == N

    def body(*refs):
        a_ref, b_ref = refs[:2]
        r_ref = refs[2] if has_res else None
        if has_norm:
            x_ref, w_ref, skip_ref = refs[2 + has_res:5 + has_res]
            o_ref, dw_ref, acc_ref = refs[-3:]
        else:
            o_ref, acc_ref = refs[-2:]
        i, k = pl.program_id(0), pl.program_id(2)
        part = lax.dot_general(a_ref[...].astype(BF16), b_ref[...].astype(BF16), dims, preferred_element_type=F32)

        @pl.when(k == 0)
        def _():
            acc_ref[...] = part

        @pl.when(k > 0)
        def _():
            acc_ref[...] += part

        @pl.when(k == nk - 1)
        def _():
            r = acc_ref[...]
            if has_res:
                r = r + r_ref[...]
            if has_norm:
                dx, dw = _rms_bwd_rows(r, x_ref[...], w_ref[...])
                o_ref[...] = skip_ref[...] + dx

                @pl.when(i == 0)
                def _():
                    dw_ref[...] = dw

                @pl.when(i > 0)
                def _():
                    dw_ref[...] += dw
            else:
                o_ref[...] = r.astype(out_dtype)

    a_spec = pl.BlockSpec((tk, tm), lambda i, j, k: (k, i)) if ta else pl.BlockSpec((tm, tk), lambda i, j, k: (i, k))
    b_spec = pl.BlockSpec((tn, tk), lambda i, j, k: (j, k)) if tb else pl.BlockSpec((tk, tn), lambda i, j, k: (k, j))
    o_spec = pl.BlockSpec((tm, tn), lambda i, j, k: (i, j))
    one = pl.BlockSpec((1, tn), lambda i, j, k: (0, 0))
    in_specs = [a_spec, b_spec] + [o_spec] * has_res + ([o_spec, one, o_spec] if has_norm else []) + [ANY] * len(after)
    args = (a, b) + ((res,) if has_res else ()) + (tuple(norm_bwd) if has_norm else ()) + tuple(after)
    return pl.pallas_call(
        body, name=name, grid=(M // tm, N // tn, nk), in_specs=in_specs,
        out_specs=[o_spec, one] if has_norm else o_spec,
        out_shape=[_sds((M, N)), _sds((1, N))] if has_norm else _sds((M, N), out_dtype),
        scratch_shapes=[pltpu.VMEM((tm, tn), F32)],
        compiler_params=_params(("arbitrary" if has_norm else "parallel", "parallel", "arbitrary")),
    )(*args)


def _in_proj(x, norm_w, w_a, w_z, w_b, *, name, after=(), tm=512):
    S, D = x.shape
    ws = (w_a, w_z, w_b)

    def body(x_ref, nw_ref, wa_ref, wz_ref, wb_ref, *rest):
        h_ref, pa_ref, pz_ref, pb_ref = rest[len(after):]
        xv = x_ref[...]
        r = lax.rsqrt(jnp.mean(xv * xv, axis=-1, keepdims=True) + EPS)
        h = (xv * r * nw_ref[...]).astype(BF16)
        h_ref[...] = h
        for w_ref, p_ref in ((wa_ref, pa_ref), (wz_ref, pz_ref), (wb_ref, pb_ref)):
            p_ref[...] = lax.dot_general(h, w_ref[...], (((1,), (1,)), ((), ())), preferred_element_type=F32)

    row = lambda n: pl.BlockSpec((tm, n), lambda i: (i, 0))
    full = lambda a: pl.BlockSpec(a.shape, lambda i: (0, 0))
    return pl.pallas_call(
        body, name=name, grid=(S // tm,), in_specs=[row(D), full(norm_w)] + [full(w) for w in ws] + [ANY] * len(after),
        out_specs=[row(D)] + [row(w.shape[0]) for w in ws],
        out_shape=[_sds((S, D), BF16)] + [_sds((S, w.shape[0])) for w in ws], compiler_params=_params(("parallel",)),
    )(x, norm_w, *ws, *after)


def _out_proj_norm(a, w, x, norm_w, *, name, tm=512):
    S, D = x.shape

    def body(a_ref, w_ref, x_ref, nw_ref, x1_ref, h_ref):
        x1 = x_ref[...] + jnp.dot(a_ref[...], w_ref[...], preferred_element_type=F32)
        x1_ref[...] = x1
        r = lax.rsqrt(jnp.mean(x1 * x1, axis=-1, keepdims=True) + EPS)
        h_ref[...] = (x1 * r * nw_ref[...]).astype(BF16)

    row = pl.BlockSpec((tm, D), lambda i: (i, 0))
    return pl.pallas_call(
        body, name=name, grid=(S // tm,),
        in_specs=[pl.BlockSpec((tm, a.shape[1]), lambda i: (i, 0)), pl.BlockSpec(w.shape, lambda i: (0, 0)), row,
                  pl.BlockSpec((1, D), lambda i: (0, 0))],
        out_specs=[row, row], out_shape=[_sds((S, D)), _sds((S, D), BF16)], compiler_params=_params(("parallel",)),
    )(a, w, x, norm_w)


def _shifted(x, start, n):
    aligned = -(-start // HALO) * HALO
    assert aligned + n <= x.shape[0], (start, n, x.shape)
    return (x if aligned == start else pltpu.roll(x, aligned - start, axis=0))[aligned:aligned + n]


def _conv_rows(prev, cur, w, taps):
    n = cur.shape[0]
    xs = jnp.concatenate([prev, cur], axis=0)
    base = HALO - (taps - 1)
    out = _shifted(xs, base, n) * w[0:1]
    for i in range(1, taps):
        out = out + _shifted(xs, base + i, n) * w[i:i + 1]
    return out


def _conv_rows_bwd(cur_d, next_d, prev_x, cur_x, w, taps):
    n = cur_d.shape[0]
    ds = jnp.concatenate([cur_d, next_d], axis=0)
    dx = _shifted(ds, taps - 1, n) * w[0:1]
    for i in range(1, taps):
        dx = dx + _shifted(ds, taps - 1 - i, n) * w[i:i + 1]
    xs = jnp.concatenate([prev_x, cur_x], axis=0)
    base = HALO - (taps - 1)
    dws = [jnp.sum(cur_d * _shifted(xs, base + i, n), axis=0, keepdims=True) for i in range(taps)]
    return dx, jnp.concatenate(dws, axis=0)


def _halo_specs(tm, width, col, nblk):
    per = tm // HALO
    prev = pl.BlockSpec((HALO, width), lambda i, *_: (jnp.maximum(i * per - 1, 0), col))
    nxt = pl.BlockSpec((HALO, width), lambda i, *_: (jnp.minimum((i + 1) * per, nblk * per - 1), col))
    return prev, nxt


def _softplus(x):
    return jnp.maximum(x, 0.0) + jnp.log1p(jnp.exp(-jnp.abs(x)))


def _chunk_tri(tm, upper=False):
    r = lax.broadcasted_iota(jnp.int32, (tm, tm), 0)
    c = lax.broadcasted_iota(jnp.int32, (tm, tm), 1)
    same = lax.div(r, CHUNK) == lax.div(c, CHUNK)
    order = (c >= r) if upper else (c <= r)
    return jnp.where(same & order, 1.0, 0.0)


def _gdn_prep_fwd(proj_a, conv_w, a_log, dt_bias, *, name, tm=256):
    S = proj_a.shape[0]
    nblk = S // tm
    W3 = 3 * GDN_WIDTH

    def body(cur_ref, prev_ref, ba_ref, cw_ref, al_ref, dt_ref, qn_ref, kn_ref, v_ref, gcb_ref, bb_ref):
        i = pl.program_id(0)
        prev = jnp.where(i > 0, prev_ref[...], 0.0)
        c = _conv_rows(prev, cur_ref[...], cw_ref[...], GDN_CONV)
        a = c * _sigmoid(c)
        ba = ba_ref[...]
        lane = lax.broadcasted_iota(jnp.int32, (tm, 128), 1)
        g4 = jnp.zeros((tm, 128), F32)
        for h in range(GDN_HEADS):
            sl = slice(GDN_DIM * h, GDN_DIM * (h + 1))
            qh = a[:, GDN_DIM * h:GDN_DIM * (h + 1)]
            kh = a[:, GDN_WIDTH + GDN_DIM * h:GDN_WIDTH + GDN_DIM * (h + 1)]
            qn_ref[:, sl] = qh * (lax.rsqrt(jnp.sum(qh * qh, axis=-1, keepdims=True) + EPS) * (GDN_DIM ** -0.5))
            kn_ref[:, sl] = kh * lax.rsqrt(jnp.sum(kh * kh, axis=-1, keepdims=True) + EPS)
            beta = _sigmoid(ba[:, h:h + 1])
            bb_ref[:, sl] = jnp.broadcast_to(beta, (tm, GDN_DIM))
            g = -jnp.exp(al_ref[0:1, h:h + 1]) * _softplus(ba[:, GDN_HEADS + h:GDN_HEADS + h + 1] + dt_ref[0:1, h:h + 1])
            g4 = jnp.where(lane == h, g, g4)
        v_ref[...] = a[:, 2 * GDN_WIDTH:]
        gc = _exact_tri_dot(_chunk_tri(tm), g4)
        for h in range(GDN_HEADS):
            gcb_ref[:, GDN_DIM * h:GDN_DIM * (h + 1)] = jnp.broadcast_to(gc[:, h:h + 1], (tm, GDN_DIM))

    prev_spec, _ = _halo_specs(tm, W3, 0, nblk)
    row = pl.BlockSpec((tm, GDN_WIDTH), lambda i: (i, 0))
    small = lambda a: pl.BlockSpec(a.shape, lambda i: (0, 0))
    return pl.pallas_call(
        body, name=name, grid=(nblk,),
        in_specs=[pl.BlockSpec((tm, W3), lambda i: (i, 0)), prev_spec,
                  pl.BlockSpec((tm, 128), lambda i: (i, W3 // 128)), small(conv_w), small(a_log), small(dt_bias)],
        out_specs=[row] * 5, out_shape=[_sds((S, GDN_WIDTH))] * 5, compiler_params=_params(("parallel",)),
    )(proj_a, proj_a, proj_a, conv_w, a_log, dt_bias)


GDN_STACK = GDN_HEADS * CHUNK


def _stack(ref, rows):
    return jnp.concatenate([ref[rows, GDN_DIM * h:GDN_DIM * (h + 1)] for h in range(GDN_HEADS)], axis=0)


def _unstack_to(ref, rows, x):
    for h in range(GDN_HEADS):
        ref[rows, GDN_DIM * h:GDN_DIM * (h + 1)] = x[CHUNK * h:CHUNK * (h + 1)].astype(ref.dtype)


def _stack_masks():
    r = lax.broadcasted_iota(jnp.int32, (GDN_STACK, GDN_STACK), 0)
    c = lax.broadcasted_iota(jnp.int32, (GDN_STACK, GDN_STACK), 1)
    same = (r & -CHUNK) == (c & -CHUNK)
    return same & (r >= c), same & (r > c), r == c


def _stack_decay(gs, bs, incl):
    g2 = jnp.concatenate([gs, gs], axis=1)
    diff = g2 - g2.T
    dec = jnp.where(incl, jnp.exp(jnp.where(incl, diff, 0.0)), 0.0)
    return dec, jnp.concatenate([bs, bs], axis=1).T


def _head_mask():
    r = lax.broadcasted_iota(jnp.int32, (GDN_STACK, GDN_WIDTH), 0)
    c = lax.broadcasted_iota(jnp.int32, (GDN_STACK, GDN_WIDTH), 1)
    return (r & -CHUNK) * (GDN_DIM // CHUNK) == (c & -GDN_DIM)


def _head_spread(x):
    return jnp.where(_head_mask(), jnp.concatenate([x] * GDN_HEADS, axis=1), 0.0)


def _head_diag(x):
    xm = jnp.where(_head_mask(), x, 0.0)
    out = xm[:, 0:GDN_DIM]
    for h in range(1, GDN_HEADS):
        out = out + xm[:, GDN_DIM * h:GDN_DIM * (h + 1)]
    return out


def _last_rows(gs, n):
    return jnp.concatenate([jnp.broadcast_to(gs[CHUNK * (h + 1) - 1:CHUNK * (h + 1)], (n, GDN_DIM)) for h in range(GDN_HEADS)], axis=0)


def _gdn_chunk_fwd(qn, kn, v, gcb, bb, *, name):
    S = qn.shape[0]

    def body(qn_ref, kn_ref, v_ref, gcb_ref, bb_ref, uv_ref, wk_ref, at_ref, t_ref, wkb_ref, qdb_ref, keb_ref):
        incl, strict, diag = _stack_masks()
        for c in range(CHUNKS_PER_STEP):
            rows = slice(CHUNK * c, CHUNK * (c + 1))
            srows = slice(GDN_STACK * c, GDN_STACK * (c + 1))
            q, k, vv, gs, bs = [_stack(r, rows) for r in (qn_ref, kn_ref, v_ref, gcb_ref, bb_ref)]
            dec, bt = _stack_decay(gs, bs, incl)
            p = -jnp.where(strict, dec * _bdot_nt(k, k) * bt, 0.0)
            t = jnp.where(diag, 1.0, 0.0) + p
            for _ in range(5):
                p = _bdot(p, p)
                t = t + _bdot(t, p)
            sol = _dot3(t, jnp.concatenate([vv, jnp.exp(gs) * k], axis=1), ((1,), (0,)))
            _unstack_to(uv_ref, rows, sol[:, :GDN_DIM])
            _unstack_to(wk_ref, rows, sol[:, GDN_DIM:])
            at_ref[srows, :] = dec * _bdot_nt(q, k) * bt
            t_ref[srows, :] = t
            wkb_ref[srows, :] = _head_spread(sol[:, GDN_DIM:]).astype(BF16)
            qdb_ref[srows, :] = _head_spread(q * jnp.exp(gs)).astype(BF16)
            keb_ref[srows, :] = _head_spread(k * jnp.exp(_last_rows(gs, CHUNK) - gs) * bs).astype(BF16)

    step = CHUNKS_PER_STEP * CHUNK
    row = pl.BlockSpec((step, GDN_WIDTH), lambda n: (n, 0))
    sq = pl.BlockSpec((CHUNKS_PER_STEP * GDN_STACK, GDN_STACK), lambda n: (n, 0))
    wide = pl.BlockSpec((CHUNKS_PER_STEP * GDN_STACK, GDN_WIDTH), lambda n: (n, 0))
    nsq = S // CHUNK * GDN_STACK
    return pl.pallas_call(
        body, name=name, grid=(S // step,), in_specs=[row] * 5, out_specs=[row, row, sq, sq, wide, wide, wide],
        out_shape=[_sds((S, GDN_WIDTH)), _sds((S, GDN_WIDTH)), _sds((nsq, GDN_STACK)), _sds((nsq, GDN_STACK))]
        + [_sds((nsq, GDN_WIDTH), BF16)] * 3,
        compiler_params=_params(("parallel",)),
    )(qn, kn, v, gcb, bb)


SCAN_CHUNKS = 8


def _gdn_scan_fwd(uv, at, wkb, qdb, keb, gcb, proj_z, gnw, *, name):
    S = uv.shape[0]
    nc = S // CHUNK

    def body(uv_ref, at_ref, wkb_ref, qdb_ref, keb_ref, gcb_ref, z_ref, gnw_ref, o_ref, u_ref, sp_ref, oa_ref, st_ref):
        n = pl.program_id(0)

        @pl.when(n == 0)
        def _():
            st_ref[...] = jnp.zeros_like(st_ref)

        for c in range(SCAN_CHUNKS):
            rows = slice(CHUNK * c, CHUNK * (c + 1))
            srows = slice(GDN_STACK * c, GDN_STACK * (c + 1))
            st = st_ref[...]
            sp_ref[GDN_WIDTH * c:GDN_WIDTH * (c + 1), :] = st
            uv, gs, z = [_stack(r, rows) for r in (uv_ref, gcb_ref, z_ref)]
            u = uv - _bdot(wkb_ref[srows, :], st)
            o = _bdot(qdb_ref[srows, :], st) + _bdot(at_ref[srows, :], u)
            st_ref[...] = jnp.exp(_last_rows(gs, GDN_DIM)) * st + _bdot_tn(keb_ref[srows, :], u)
            _unstack_to(u_ref, rows, u)
            _unstack_to(o_ref, rows, o)
            r = lax.rsqrt(jnp.mean(o * o, axis=-1, keepdims=True) + EPS)
            oa = o * r * gnw_ref[...] * (z * _sigmoid(z))
            oa_ref[rows, :] = jnp.concatenate([oa[CHUNK * h:CHUNK * (h + 1)] for h in range(GDN_HEADS)], axis=1).astype(BF16)

    row = pl.BlockSpec((SCAN_CHUNKS * CHUNK, GDN_WIDTH), lambda n: (n, 0))
    sq = pl.BlockSpec((SCAN_CHUNKS * GDN_STACK, GDN_STACK), lambda n: (n, 0))
    wide = pl.BlockSpec((SCAN_CHUNKS * GDN_STACK, GDN_WIDTH), lambda n: (n, 0))
    return pl.pallas_call(
        body, name=name, grid=(nc // SCAN_CHUNKS,),
        in_specs=[row, sq, wide, wide, wide, row, row, pl.BlockSpec((1, GDN_DIM), lambda n: (0, 0))],
        out_specs=[row, row, pl.BlockSpec((SCAN_CHUNKS * GDN_WIDTH, GDN_DIM), lambda n: (n, 0)), row],
        out_shape=[_sds((S, GDN_WIDTH)), _sds((S, GDN_WIDTH)), _sds((nc * GDN_WIDTH, GDN_DIM)), _sds((S, 2 * GDN_WIDTH), BF16)],
        scratch_shapes=[pltpu.VMEM((GDN_WIDTH, GDN_DIM), F32)],
        compiler_params=_params(("arbitrary",)),
    )(uv, at, wkb, qdb, keb, gcb, proj_z, gnw)


def _gdn_scan_bwd(d_oab, o, proj_z, gnw, sp, u, at, wkb, qdb, keb, gcb, *, name, after=()):
    S = o.shape[0]
    nc = S // CHUNK
    ns = nc // SCAN_CHUNKS

    def body(do_ref, o_ref, z_ref, gnw_ref, sp_ref, u_ref, at_ref, wkb_ref, qdb_ref, keb_ref, gcb_ref, *rest):
        dz_ref, dgn_ref, du_ref, dwk_ref, dat_ref, dqd_ref, dke_ref, dgl_ref, ds_ref = rest[len(after):]
        n = pl.program_id(0)

        @pl.when(n == 0)
        def _():
            ds_ref[...] = jnp.zeros_like(ds_ref)
            dgn_ref[...] = jnp.zeros_like(dgn_ref)

        gw = gnw_ref[...]
        for c in reversed(range(SCAN_CHUNKS)):
            rows = slice(CHUNK * c, CHUNK * (c + 1))
            srows = slice(GDN_STACK * c, GDN_STACK * (c + 1))
            d_oa, oo, z, uu, gs = [_stack(r, rows) for r in (do_ref, o_ref, z_ref, u_ref, gcb_ref)]
            sg = _sigmoid(z)
            r = lax.rsqrt(jnp.mean(oo * oo, axis=-1, keepdims=True) + EPS)
            xh = oo * r
            dy = d_oa * (z * sg)
            _unstack_to(dz_ref, rows, d_oa * (xh * gw) * _dsilu(z, sg))
            dgn_ref[...] += jnp.sum(dy * xh, axis=0, keepdims=True)
            dxh = dy * gw
            do = r * (dxh - xh * jnp.mean(dxh * xh, axis=-1, keepdims=True))

            st = sp_ref[GDN_WIDTH * c:GDN_WIDTH * (c + 1), :]
            dst = ds_ref[...]
            ge = jnp.exp(_last_rows(gs, GDN_DIM))
            _unstack_to(dqd_ref, rows, _head_diag(_bdot_nt(do, st)))
            dat_ref[srows, :] = _bdot_nt(do, uu)
            du = _bdot_tn(at_ref[srows, :], do) + _bdot(keb_ref[srows, :], dst)
            _unstack_to(dke_ref, rows, _head_diag(_bdot_nt(uu, dst)))
            prod = dst * st
            for h in range(GDN_HEADS):
                blk = prod[GDN_DIM * h:GDN_DIM * (h + 1)]
                dge = jnp.sum(jnp.sum(blk, axis=1, keepdims=True), axis=0, keepdims=True)
                dgl_ref[c, :, GDN_DIM * h:GDN_DIM * (h + 1)] = jnp.broadcast_to(dge * ge[GDN_DIM * h:GDN_DIM * h + 1], (8, GDN_DIM))
            ds_ref[...] = _bdot_tn(qdb_ref[srows, :], do) + ge * dst - _bdot_tn(wkb_ref[srows, :], du)
            _unstack_to(du_ref, rows, du)
            _unstack_to(dwk_ref, rows, -_head_diag(_bdot_nt(du, st)))

    rev = lambda n: (ns - 1 - n, 0)
    row = pl.BlockSpec((SCAN_CHUNKS * CHUNK, GDN_WIDTH), rev)
    sq = pl.BlockSpec((SCAN_CHUNKS * GDN_STACK, GDN_STACK), rev)
    wide = pl.BlockSpec((SCAN_CHUNKS * GDN_STACK, GDN_WIDTH), rev)
    one = pl.BlockSpec((1, GDN_DIM), lambda n: (0, 0))
    return pl.pallas_call(
        body, name=name, grid=(ns,),
        in_specs=[row, row, row, one, pl.BlockSpec((SCAN_CHUNKS * GDN_WIDTH, GDN_DIM), rev), row, sq, wide, wide, wide, row]
        + [ANY] * len(after),
        out_specs=[row, one, row, row, sq, row, row, pl.BlockSpec((SCAN_CHUNKS, 8, GDN_WIDTH), lambda n: (ns - 1 - n, 0, 0))],
        out_shape=[_sds((S, GDN_WIDTH), BF16), _sds((1, GDN_DIM)), _sds((S, GDN_WIDTH)), _sds((S, GDN_WIDTH)),
                   _sds((nc * GDN_STACK, GDN_STACK)), _sds((S, GDN_WIDTH)), _sds((S, GDN_WIDTH)), _sds((nc, 8, GDN_WIDTH))],
        scratch_shapes=[pltpu.VMEM((GDN_WIDTH, GDN_DIM), F32)],
        compiler_params=_params(("arbitrary",)),
    )(d_oab, o, proj_z, gnw, sp, u, at, wkb, qdb, keb, gcb, *after)


def _gdn_chunk_bwd(qn, kn, gcb, bb, tmat, uv, wk, du, dwk, dat, dqd, dke, dgl, *, name):
    S = qn.shape[0]

    def body(qn_ref, kn_ref, gcb_ref, bb_ref, t_ref, uv_ref, wk_ref, du_ref, dwk_ref, dat_ref, dqd_ref, dke_ref,
             dgl_ref, dq_ref, dk_ref, dv_ref, dg_ref, dbeta_ref):
        incl, strict, _ = _stack_masks()
        lane = lax.broadcasted_iota(jnp.int32, (CHUNK, 128), 1)
        rowi = lax.broadcasted_iota(jnp.int32, (CHUNK, 1), 0)
        rsum = lambda x: jnp.sum(x, axis=-1, keepdims=True)
        for c in range(CHUNKS_PER_STEP):
            rows = slice(CHUNK * c, CHUNK * (c + 1))
            srows = slice(GDN_STACK * c, GDN_STACK * (c + 1))
            q, k, gs, bs, uv, wk, du, dwk, dqd, dke = [
                _stack(r, rows) for r in (qn_ref, kn_ref, gcb_ref, bb_ref, uv_ref, wk_ref, du_ref, dwk_ref, dqd_ref, dke_ref)]
            dec, bt = _stack_decay(gs, bs, incl)
            kk = _bdot_nt(k, k)
            qk = _bdot_nt(q, k)
            d_rhs = _dot3(t_ref[srows, :], jnp.concatenate([du, dwk], axis=1), ((0,), (0,)))
            sol = jnp.concatenate([uv, wk], axis=1)
            d_l = jnp.where(strict, -_dot3(d_rhs, sol, ((1,), (1,))), 0.0)
            d_a = jnp.where(incl, dat_ref[srows, :], 0.0)
            gam = jnp.exp(gs)
            e = jnp.exp(_last_rows(gs, CHUNK) - gs)
            d_gk = d_rhs[:, GDN_DIM:]
            ml = d_l * dec * bt
            ma = d_a * dec * bt
            _unstack_to(dq_ref, rows, _bdot(ma, k) + dqd * gam)
            _unstack_to(dk_ref, rows, _bdot(ml + ml.T, k) + _bdot_tn(ma, q) + d_gk * gam + dke * (e * bs))
            _unstack_to(dv_ref, rows, d_rhs[:, :GDN_DIM])
            wb = d_l * dec * kk + d_a * dec * qk
            ew = wb * bt
            s_ke = rsum(dke * k * (e * bs))
            dbeta = rsum(wb.T) + rsum(dke * k * e)
            dgc = rsum(ew) - rsum(ew.T) + rsum(dqd * q * gam) + rsum(d_gk * k * gam) - s_ke
            dgc4 = jnp.zeros((CHUNK, 128), F32)
            db4 = jnp.zeros((CHUNK, 128), F32)
            for h in range(GDN_HEADS):
                hr = slice(CHUNK * h, CHUNK * (h + 1))
                tail = jnp.sum(s_ke[hr], axis=0, keepdims=True) + dgl_ref[c, 0:1, GDN_DIM * h:GDN_DIM * h + 1]
                dgc4 = jnp.where(lane == h, dgc[hr] + jnp.where(rowi == CHUNK - 1, tail, 0.0), dgc4)
                db4 = jnp.where(lane == h, dbeta[hr], db4)
            dg_ref[rows, :] = _exact_tri_dot(_chunk_tri(CHUNK, upper=True), dgc4)
            dbeta_ref[rows, :] = db4

    step = CHUNKS_PER_STEP * CHUNK
    row = pl.BlockSpec((step, GDN_WIDTH), lambda n: (n, 0))
    sq = pl.BlockSpec((CHUNKS_PER_STEP * GDN_STACK, GDN_STACK), lambda n: (n, 0))
    col = pl.BlockSpec((step, 128), lambda n: (n, 0))
    return pl.pallas_call(
        body, name=name, grid=(S // step,),
        in_specs=[row] * 4 + [sq, row, row, row, row, sq, row, row,
                              pl.BlockSpec((CHUNKS_PER_STEP, 8, GDN_WIDTH), lambda n: (n, 0, 0))],
        out_specs=[row, row, row, col, col],
        out_shape=[_sds((S, GDN_WIDTH))] * 3 + [_sds((S, 128))] * 2, compiler_params=_params(("parallel",)),
    )(qn, kn, gcb, bb, tmat, uv, wk, du, dwk, dat, dqd, dke, dgl)


def _gdn_prep_bwd(dqn, dkn, dv, dg, dbeta, proj_a, conv_w, a_log, dt_bias, *, name, tm=256):
    S = proj_a.shape[0]
    nblk = S // tm
    W3 = 3 * GDN_WIDTH

    def body(dqn_ref, dkn_ref, dv_ref, dg_ref, dbeta_ref, cur_ref, prev_ref, ba_ref, cw_ref, al_ref, dt_ref,
             dc_ref, dba_ref, sm_ref):
        i = pl.program_id(0)
        prev = jnp.where(i > 0, prev_ref[...], 0.0)
        c = _conv_rows(prev, cur_ref[...], cw_ref[...], GDN_CONV)
        sg = _sigmoid(c)
        a = c * sg
        dsl = _dsilu(c, sg)
        ba = ba_ref[...]
        lane = lax.broadcasted_iota(jnp.int32, (tm, 128), 1)
        lane1 = lax.broadcasted_iota(jnp.int32, (1, 128), 1)
        dba = jnp.zeros((tm, 128), F32)
        sm = jnp.zeros((1, 128), F32)
        for h in range(GDN_HEADS):
            sl = slice(GDN_DIM * h, GDN_DIM * (h + 1))
            ks = slice(GDN_WIDTH + GDN_DIM * h, GDN_WIDTH + GDN_DIM * (h + 1))
            qh, kh = a[:, sl], a[:, ks]
            rq = lax.rsqrt(jnp.sum(qh * qh, axis=-1, keepdims=True) + EPS)
            rk = lax.rsqrt(jnp.sum(kh * kh, axis=-1, keepdims=True) + EPS)
            qhat, khat = qh * rq, kh * rk
            dyq = dqn_ref[:, sl] * (GDN_DIM ** -0.5)
            dyk = dkn_ref[:, sl]
            dq = rq * (dyq - qhat * jnp.sum(dyq * qhat, axis=-1, keepdims=True))
            dk = rk * (dyk - khat * jnp.sum(dyk * khat, axis=-1, keepdims=True))
            dc_ref[:, sl] = dq * dsl[:, sl]
            dc_ref[:, ks] = dk * dsl[:, ks]
            beta = _sigmoid(ba[:, h:h + 1])
            db = dbeta_ref[:, h:h + 1] * beta * (1.0 - beta)
            aneg = -jnp.exp(al_ref[0:1, h:h + 1])
            xa = ba[:, GDN_HEADS + h:GDN_HEADS + h + 1] + dt_ref[0:1, h:h + 1]
            dgh = dg_ref[:, h:h + 1]
            dxa = dgh * aneg * _sigmoid(xa)
            dba = jnp.where(lane == h, db, dba)
            dba = jnp.where(lane == GDN_HEADS + h, dxa, dba)
            d_alog = jnp.sum(dgh * _softplus(xa), axis=0, keepdims=True) * aneg
            sm = jnp.where(lane1 == h, d_alog, sm)
            sm = jnp.where(lane1 == GDN_HEADS + h, jnp.sum(dxa, axis=0, keepdims=True), sm)
        vs = slice(2 * GDN_WIDTH, W3)
        dc_ref[:, vs] = dv_ref[...] * dsl[:, vs]
        dba_ref[...] = dba

        @pl.when(i == 0)
        def _():
            sm_ref[...] = sm

        @pl.when(i > 0)
        def _():
            sm_ref[...] += sm

    prev_spec, _ = _halo_specs(tm, W3, 0, nblk)
    row = pl.BlockSpec((tm, GDN_WIDTH), lambda i: (i, 0))
    col = pl.BlockSpec((tm, 128), lambda i: (i, 0))
    small = lambda a: pl.BlockSpec(a.shape, lambda i: (0, 0))
    return pl.pallas_call(
        body, name=name, grid=(nblk,),
        in_specs=[row, row, row, col, col, pl.BlockSpec((tm, W3), lambda i: (i, 0)), prev_spec,
                  pl.BlockSpec((tm, 128), lambda i: (i, W3 // 128)), small(conv_w), small(a_log), small(dt_bias)],
        out_specs=[pl.BlockSpec((tm, W3), lambda i: (i, 0)), col, pl.BlockSpec((1, 128), lambda i: (0, 0))],
        out_shape=[_sds((S, W3)), _sds((S, 128)), _sds((1, 128))], compiler_params=_params(("arbitrary",)),
    )(dqn, dkn, dv, dg, dbeta, proj_a, proj_a, proj_a, conv_w, a_log, dt_bias)


def _gdn_conv_bwd(dc, dba, proj_a, conv_w, *, name, tm=256):
    S = proj_a.shape[0]
    nblk = S // tm
    W3 = 3 * GDN_WIDTH

    def body(dc_ref, dnext_ref, dba_ref, cur_ref, prev_ref, cw_ref, da_ref, dcw_ref):
        i = pl.program_id(0)
        prev = jnp.where(i > 0, prev_ref[...], 0.0)
        nxt = jnp.where(i < nblk - 1, dnext_ref[...], 0.0)
        dx, dw = _conv_rows_bwd(dc_ref[...], nxt, prev, cur_ref[...], cw_ref[...], GDN_CONV)
        da_ref[:, 0:W3] = dx.astype(BF16)
        da_ref[:, W3:] = dba_ref[...].astype(BF16)

        @pl.when(i == 0)
        def _():
            dcw_ref[...] = dw

        @pl.when(i > 0)
        def _():
            dcw_ref[...] += dw

    prev_spec, next_spec = _halo_specs(tm, W3, 0, nblk)
    wide = pl.BlockSpec((tm, W3), lambda i: (i, 0))
    return pl.pallas_call(
        body, name=name, grid=(nblk,),
        in_specs=[wide, next_spec, pl.BlockSpec((tm, 128), lambda i: (i, 0)), wide, prev_spec,
                  pl.BlockSpec(conv_w.shape, lambda i: (0, 0))],
        out_specs=[pl.BlockSpec((tm, A_COLS), lambda i: (i, 0)), pl.BlockSpec(conv_w.shape, lambda i: (0, 0))],
        out_shape=[_sds((S, A_COLS), BF16), _sds(conv_w.shape)], compiler_params=_params(("arbitrary",)),
    )(dc, dc, dba, proj_a, proj_a, conv_w)


def _band_mask(nk):
    i = lax.broadcasted_iota(jnp.int32, (2 * BAND, nk), 0) & (BAND - 1)
    j = lax.broadcasted_iota(jnp.int32, (2 * BAND, nk), 1)
    if nk == BAND:
        return j <= i
    return (j >= i) & (j <= i + BAND)


def _stack_heads(x, lo):
    return jnp.concatenate([jnp.where(lo, x, 0.0), jnp.where(lo, 0.0, x)], axis=0)


def _stack_cols(x):
    return jnp.concatenate([x[:, 0:1], x[:, DIL_DIM:DIL_DIM + 1]], axis=0)


def _unstack(x, lo):
    return jnp.where(lo, x[0:BAND], x[BAND:2 * BAND])


def _rows(start, size, stride):
    return pl.ds(start, size) if stride == 1 else pl.ds(start, size, stride=stride)


ATTN_LANES = 4


def _attn_blocks(S, visit_many, lanes=ATTN_LANES):
    for d in DILATIONS:
        nb = S // (d * BAND)
        if d == 1:
            half = nb // 2
            visit_many(d, [(0, 0, True), (0, half, False)])

            def pair(n, c):
                visit_many(1, [(0, n, False), (0, n + half, False)])
                return c
            lax.fori_loop(1, half, pair, 0)
        elif nb > 1:
            for r0 in range(0, d, lanes):
                visit_many(d, [(r0 + t, 0, True) for t in range(lanes)])

                def column(n, c, d=d, r0=r0):
                    visit_many(d, [(r0 + t, n, False) for t in range(lanes)])
                    return c
                lax.fori_loop(1, nb, column, 0)
        else:
            def group(g, c, d=d):
                visit_many(d, [(g * lanes + t, 0, True) for t in range(lanes)])
                return c
            lax.fori_loop(0, d // lanes, group, 0)


def _attn_fwd(proj_b, oab, *, name):
    S = proj_b.shape[0]
    scale = DIL_DIM ** -0.5

    def body(q_ref, k_ref, v_ref, oab_in_ref, ob_ref, lse_ref, m_ref, l_ref, acc_ref):
        del oab_in_ref
        lane = lax.broadcasted_iota(jnp.int32, (BAND, 128), 1)
        lo = lane < DIL_DIM
        m_ref[...] = jnp.full_like(m_ref, NEG_BIG)
        l_ref[...] = jnp.zeros_like(l_ref)
        acc_ref[...] = jnp.zeros_like(acc_ref)

        def load(d, r, n, first):
            nk = BAND if first else 2 * BAND
            qrows = _rows(r + n * (BAND * d), BAND, d)
            krows = _rows(r if first else r + (n - 1) * (BAND * d), nk, d)
            return dict(nk=nk, qrows=qrows, q=q_ref[qrows, :] * scale, k=k_ref[krows, :].astype(BF16),
                        v=v_ref[krows, :].astype(BF16), m=m_ref[qrows, :], l=l_ref[qrows, :], acc=acc_ref[qrows, :])

        def compute(b):
            q, k, v = b["q"], b["k"], b["v"]
            s = jnp.where(_band_mask(b["nk"]), _bdot_nt(_stack_heads(q, lo), k), NEG_BIG)
            m_old = _stack_cols(b["m"])
            m_new = jnp.maximum(m_old, jnp.max(s, axis=-1, keepdims=True))
            p = jnp.exp(s - m_new)
            alpha = _unstack(jnp.exp(m_old - m_new), lo)
            l_new = alpha * b["l"] + _unstack(jnp.sum(p, axis=-1, keepdims=True), lo)
            return _unstack(m_new, lo), l_new, alpha * b["acc"] + _unstack(_bdot(p, v), lo)

        def visit_many(d, blocks):
            loaded = [load(d, *blk) for blk in blocks]
            done = [compute(b) for b in loaded]
            for b, (m_new, l_new, acc_new) in zip(loaded, done):
                m_ref[b["qrows"], :] = m_new
                l_ref[b["qrows"], :] = l_new
                acc_ref[b["qrows"], :] = acc_new

        _attn_blocks(S, visit_many)
        ob_ref[...] = (acc_ref[...] / l_ref[...]).astype(BF16)
        lse_ref[...] = m_ref[...] + jnp.log(l_ref[...])

    part = lambda t: pl.BlockSpec((S, 128), lambda p: (0, 3 * p + t))
    return pl.pallas_call(
        body, name=name, grid=(DIL_PAIRS,),
        in_specs=[part(0), part(1), part(2), pl.BlockSpec(memory_space=pl.ANY)],
        out_specs=[pl.BlockSpec((S, 128), lambda p: (0, GDN_WIDTH // 128 + p)), pl.BlockSpec((S, 128), lambda p: (0, p))],
        out_shape=[_sds(oab.shape, BF16), _sds((S, DIL_WIDTH))],
        scratch_shapes=[pltpu.VMEM((S, 128), F32)] * 3, input_output_aliases={3: 0},
        compiler_params=_params(("parallel",)),
    )(proj_b, proj_b, proj_b, oab)


def _attn_bwd(proj_b, oab, d_oab, lse, *, name):
    S = proj_b.shape[0]
    scale = DIL_DIM ** -0.5

    def body(q_ref, k_ref, v_ref, o_ref, do_ref, lse_ref, dqkv_ref, dq_ref, dk_ref, dv_ref, delta_ref):
        lane = lax.broadcasted_iota(jnp.int32, (BAND, 128), 1)
        lo = lane < DIL_DIM
        dq_ref[...] = jnp.zeros_like(dq_ref)
        dk_ref[...] = jnp.zeros_like(dk_ref)
        dv_ref[...] = jnp.zeros_like(dv_ref)
        prod = do_ref[...] * o_ref[...].astype(F32)
        lo_all = lax.broadcasted_iota(jnp.int32, (S, 128), 1) < DIL_DIM
        delta_ref[...] = jnp.where(lo_all, jnp.sum(jnp.where(lo_all, prod, 0.0), axis=-1, keepdims=True),
                                   jnp.sum(jnp.where(lo_all, 0.0, prod), axis=-1, keepdims=True))

        def load(d, r, n, first):
            nk = BAND if first else 2 * BAND
            qrows = _rows(r + n * (BAND * d), BAND, d)
            krows = _rows(r if first else r + (n - 1) * (BAND * d), nk, d)
            return dict(nk=nk, qrows=qrows, krows=krows, q=q_ref[qrows, :] * scale, k=k_ref[krows, :], v=v_ref[krows, :],
                        do=do_ref[qrows, :], delta=delta_ref[qrows, :], lse=lse_ref[qrows, :],
                        dq=dq_ref[qrows, :], dk=dk_ref[krows, :], dv=dv_ref[krows, :])

        def compute(b):
            q, k, v, do = b["q"], b["k"], b["v"], b["do"]
            qs, dos = _stack_heads(q, lo), _stack_heads(do, lo)
            p = jnp.where(_band_mask(b["nk"]), jnp.exp(_bdot_nt(qs, k) - _stack_cols(b["lse"])), 0.0)
            ds = p * (_bdot_nt(dos, v) - _stack_cols(b["delta"]))
            dq = b["dq"] + _unstack(_bdot(ds, k), lo) * scale
            return dq, b["dk"] + _bdot_tn(ds, qs), b["dv"] + _bdot_tn(p, dos)

        def visit_many(d, blocks):
            loaded = [load(d, *blk) for blk in blocks]
            done = [compute(b) for b in loaded]
            for b, (dq, dk, dv) in zip(loaded, done):
                dq_ref[b["qrows"], :] = dq
                dk_ref[b["krows"], :] = dk
                dv_ref[b["krows"], :] = dv

        _attn_blocks(S, visit_many, lanes=2)
        dqkv_ref[:, 0:128] = dq_ref[...].astype(BF16)
        dqkv_ref[:, 128:256] = dk_ref[...].astype(BF16)
        dqkv_ref[:, 256:384] = dv_ref[...].astype(BF16)

    half = lambda p: (0, GDN_WIDTH // 128 + p)
    part = lambda t: pl.BlockSpec((S, 128), lambda p: (0, 3 * p + t))
    return pl.pallas_call(
        body, name=name, grid=(DIL_PAIRS,),
        in_specs=[part(0), part(1), part(2), pl.BlockSpec((S, 128), half), pl.BlockSpec((S, 128), half),
                  pl.BlockSpec((S, 128), lambda p: (0, p))],
        out_specs=pl.BlockSpec((S, 384), lambda p: (0, p)), out_shape=_sds((S, 3 * DIL_WIDTH), BF16),
        scratch_shapes=[pltpu.VMEM((S, 128), F32)] * 4, compiler_params=_params(("parallel",)),
    )(proj_b, proj_b, proj_b, oab, d_oab, lse)


FF_SLAB = 2 * D_FF // N_DEV
FF_PAIRS = N_DEV // 2
ROWS16 = 16


def _taps(w, x, base, n):
    out = _shifted(x, base, n) * w[0:1]
    for t in range(1, FFN_CONV):
        out = out + _shifted(x, base + t, n) * w[t:t + 1]
    return out


def _ffn_fwd(h2, x1, w_up, conv_w, w_down, final_w, tgt, *, name, tm=512):
    S, D = h2.shape
    ni = S // tm
    per = tm // ROWS16

    def body(h_ref, hp_ref, x1_ref, wg_ref, wu_ref, cg_ref, cu_ref, wd_ref, fw_ref, t_ref,
             dx_ref, dxb_ref, dfw_ref, loss_ref, ug_ref, uu_ref, x2_ref):
        i, j = pl.program_id(0), pl.program_id(1)
        hv = jnp.concatenate([hp_ref[...], h_ref[...]], axis=0)
        row = lax.broadcasted_iota(jnp.int32, (tm + ROWS16, 1), 0)
        keep = (i > 0) | (row >= ROWS16)

        def branch(w_ref, c_ref, u_ref):
            u = lax.dot_general(hv, w_ref[...], (((1,), (1,)), ((), ())), preferred_element_type=F32).astype(BF16)
            u_ref[...] = u[ROWS16:]
            return _taps(c_ref[...], jnp.where(keep, u.astype(F32), 0.0), ROWS16 - (FFN_CONV - 1), tm)

        gate = branch(wg_ref, cg_ref, ug_ref)
        up = branch(wu_ref, cu_ref, uu_ref)
        act = (gate * _sigmoid(gate) * up).astype(BF16)
        part = jnp.dot(act, wd_ref[...], preferred_element_type=F32)

        @pl.when(j == 0)
        def _():
            x2_ref[...] = x1_ref[...] + part

        @pl.when((j > 0) & (j < FF_PAIRS - 1))
        def _():
            x2_ref[...] += part

        @pl.when(j == FF_PAIRS - 1)
        def _():
            xv = x2_ref[...] + part
            wv = fw_ref[...]
            r = lax.rsqrt(jnp.mean(xv * xv, axis=-1, keepdims=True) + EPS)
            err = xv * r * wv - t_ref[...]
            lsum = jnp.sum(jnp.sum(err * err, axis=-1, keepdims=True), axis=0, keepdims=True) * (0.5 / D)
            g = err * (1.0 / D)
            xh = xv * r
            gw = g * wv
            dx = r * (gw - xh * jnp.mean(gw * xh, axis=-1, keepdims=True))
            dx_ref[...] = dx
            dxb_ref[...] = dx.astype(BF16)
            dfw = jnp.sum(g * xh, axis=0, keepdims=True)
            lpart = jnp.broadcast_to(lsum, (1, 128))

            @pl.when(i == 0)
            def _():
                dfw_ref[...] = dfw
                loss_ref[...] = lpart

            @pl.when(i > 0)
            def _():
                dfw_ref[...] += dfw
                loss_ref[...] += lpart

    rows = pl.BlockSpec((tm, D), lambda i, j: (i, 0))
    slab = lambda off: pl.BlockSpec((None, FF_SLAB, D), lambda i, j: (j + off, 0, 0))
    cslab = lambda off: pl.BlockSpec((None, FFN_CONV, FF_SLAB), lambda i, j: (j + off, 0, 0))
    uspec = pl.BlockSpec((None, tm, FF_SLAB), lambda i, j: (j, i, 0))
    return pl.pallas_call(
        body, name=name, grid=(ni, FF_PAIRS),
        in_specs=[rows, pl.BlockSpec((ROWS16, D), lambda i, j: (jnp.maximum(i * per - 1, 0), 0)), rows,
                  slab(0), slab(FF_PAIRS), cslab(0), cslab(FF_PAIRS), pl.BlockSpec((FF_SLAB, D), lambda i, j: (j, 0)),
                  pl.BlockSpec((1, D), lambda i, j: (0, 0)), rows],
        out_specs=[rows, rows, pl.BlockSpec((1, D), lambda i, j: (0, 0)), pl.BlockSpec((1, 128), lambda i, j: (0, 0)), uspec, uspec],
        out_shape=[_sds((S, D)), _sds((S, D), BF16), _sds((1, D)), _sds((1, 128)),
                   _sds((FF_PAIRS, S, FF_SLAB), BF16), _sds((FF_PAIRS, S, FF_SLAB), BF16)],
        scratch_shapes=[pltpu.VMEM((tm, D), F32)],
        compiler_params=_params(("arbitrary", "arbitrary")),
    )(h2, h2, x1, w_up, w_up, conv_w, conv_w, w_down, final_w, tgt)


def _ffn_bwd(dx2, h2, ug, uu, conv_w, w_down, *, name, tm=512):
    S, D = h2.shape
    ni = S // tm
    per = tm // ROWS16
    ext = tm + ROWS16

    def body(dx_ref, dxn_ref, h_ref, ug_ref, ugp_ref, ugn_ref, uu_ref, uup_ref, uun_ref, cg_ref, cu_ref, wd_ref,
             du_ref, gd_ref, gup_ref, dcw_ref, acc_d, acc_g, acc_u, acc_cg, acc_cu):
        i = pl.program_id(1)

        @pl.when(i == 0)
        def _():
            acc_d[...] = jnp.zeros_like(acc_d)
            acc_g[...] = jnp.zeros_like(acc_g)
            acc_u[...] = jnp.zeros_like(acc_u)
            acc_cg[...] = jnp.zeros_like(acc_cg)
            acc_cu[...] = jnp.zeros_like(acc_cu)

        dx = dx_ref[...]
        dxe = jnp.concatenate([dx, dxn_ref[...]], axis=0)
        row = lax.broadcasted_iota(jnp.int32, (ext, 1), 0)
        live = (i < ni - 1) | (row < tm)
        d_act = jnp.where(live, lax.dot_general(dxe, wd_ref[...], (((1,), (1,)), ((), ())), preferred_element_type=F32), 0.0)
        rowp = lax.broadcasted_iota(jnp.int32, (ext + ROWS16, 1), 0)
        keep = (i > 0) | (rowp >= ROWS16)

        def pre(cur, prev, nxt):
            return jnp.where(keep, jnp.concatenate([prev[...], cur[...], nxt[...]], axis=0).astype(F32), 0.0)

        uge, uue = pre(ug_ref, ugp_ref, ugn_ref), pre(uu_ref, uup_ref, uun_ref)
        cg, cu = cg_ref[...], cu_ref[...]
        base = ROWS16 - (FFN_CONV - 1)
        gate = _taps(cg, uge, base, ext)
        up = _taps(cu, uue, base, ext)
        sg = _sigmoid(gate)
        silu = gate * sg
        dgc = d_act * up * _dsilu(gate, sg)
        duc = d_act * silu

        def conv_t(w, dc):
            out = _shifted(dc, FFN_CONV - 1, tm) * w[0:1]
            for t in range(1, FFN_CONV):
                out = out + _shifted(dc, FFN_CONV - 1 - t, tm) * w[t:t + 1]
            return out.astype(BF16)

        du_g, du_u = conv_t(cg, dgc), conv_t(cu, duc)
        du_ref[0] = du_g
        du_ref[1] = du_u
        dcw = lambda dc, xe: jnp.concatenate(
            [jnp.sum(dc[0:tm] * _shifted(xe, base + t, tm), axis=0, keepdims=True) for t in range(FFN_CONV)], axis=0)
        acc_cg[0:FFN_CONV, :] += dcw(dgc, uge)
        acc_cu[0:FFN_CONV, :] += dcw(duc, uue)
        tn = (((0,), (0,)), ((), ()))
        act = (silu[0:tm] * up[0:tm]).astype(BF16)
        acc_d[...] += lax.dot_general(act, dx, tn, preferred_element_type=F32)
        hv = h_ref[...]
        acc_g[...] += lax.dot_general(du_g, hv, tn, preferred_element_type=F32)
        acc_u[...] += lax.dot_general(du_u, hv, tn, preferred_element_type=F32)

        @pl.when(i == ni - 1)
        def _():
            gd_ref[...] = acc_d[...].astype(BF16)
            gup_ref[0] = acc_g[...].astype(BF16)
            gup_ref[1] = acc_u[...].astype(BF16)
            dcw_ref[0] = acc_cg[0:FFN_CONV, :]
            dcw_ref[1] = acc_cu[0:FFN_CONV, :]

    last16 = S // ROWS16 - 1
    rows = pl.BlockSpec((tm, D), lambda j, i: (i, 0))
    rows_next = pl.BlockSpec((ROWS16, D), lambda j, i: (jnp.minimum((i + 1) * per, last16), 0))
    u_cur = pl.BlockSpec((None, tm, FF_SLAB), lambda j, i: (j, i, 0))
    u_prev = pl.BlockSpec((None, ROWS16, FF_SLAB), lambda j, i: (j, jnp.maximum(i * per - 1, 0), 0))
    u_next = pl.BlockSpec((None, ROWS16, FF_SLAB), lambda j, i: (j, jnp.minimum((i + 1) * per, last16), 0))
    cslab = lambda off: pl.BlockSpec((None, FFN_CONV, FF_SLAB), lambda j, i: (j + off, 0, 0))
    return pl.pallas_call(
        body, name=name, grid=(FF_PAIRS, ni),
        in_specs=[rows, rows_next, rows, u_cur, u_prev, u_next, u_cur, u_prev, u_next, cslab(0), cslab(FF_PAIRS),
                  pl.BlockSpec((FF_SLAB, D), lambda j, i: (j, 0))],
        out_specs=[pl.BlockSpec((None, 2, tm, FF_SLAB), lambda j, i: (j, 0, i, 0)), pl.BlockSpec((FF_SLAB, D), lambda j, i: (j, 0)),
                   pl.BlockSpec((None, 2, FF_SLAB, D), lambda j, i: (j, 0, 0, 0)),
                   pl.BlockSpec((None, 2, FFN_CONV, FF_SLAB), lambda j, i: (j, 0, 0, 0))],
        out_shape=[_sds((FF_PAIRS, 2, S, FF_SLAB), BF16), _sds((D_FF, D), BF16), _sds((FF_PAIRS, 2, FF_SLAB, D), BF16),
                   _sds((FF_PAIRS, 2, FFN_CONV, FF_SLAB))],
        scratch_shapes=[pltpu.VMEM((FF_SLAB, D), F32), pltpu.VMEM((FF_SLAB, D), F32), pltpu.VMEM((FF_SLAB, D), F32),
                        pltpu.VMEM((8, FF_SLAB), F32), pltpu.VMEM((8, FF_SLAB), F32)],
        compiler_params=_params(("parallel", "arbitrary")),
    )(dx2, dx2, h2, ug, ug, ug, uu, uu, uu, conv_w, conv_w, w_down)


def _pair_slot(p):
    return 2 * (p & (FF_PAIRS - 1)) + (p >> 2)


def _mm_slabs(a, w, *, name, res=None, norm_bwd=None, after=(), tm=1024, tn=1024):
    nk, S, _ = a.shape
    D = w.shape[2]
    has_res = res is not None
    has_norm = norm_bwd is not None
    assert not has_norm or tn == D

    def body(*refs):
        a_ref, w_ref = refs[:2]
        r_ref = refs[2] if has_res else None
        if has_norm:
            x_ref, nw_ref, skip_ref = refs[2 + has_res:5 + has_res]
            o_ref, dw_ref, acc_ref = refs[-3:]
        else:
            o_ref, acc_ref = refs[-2:]
        i, k = pl.program_id(0), pl.program_id(2)
        part = jnp.dot(a_ref[...], w_ref[...], preferred_element_type=F32)

        @pl.when(k == 0)
        def _():
            acc_ref[...] = part

        @pl.when(k > 0)
        def _():
            acc_ref[...] += part

        @pl.when(k == nk - 1)
        def _():
            r = acc_ref[...] + r_ref[...] if has_res else acc_ref[...]
            if has_norm:
                dx, dw = _rms_bwd_rows(r, x_ref[...], nw_ref[...])
                o_ref[...] = skip_ref[...] + dx

                @pl.when(i == 0)
                def _():
                    dw_ref[...] = dw

                @pl.when(i > 0)
                def _():
                    dw_ref[...] += dw
            else:
                o_ref[...] = r

    o_spec = pl.BlockSpec((tm, tn), lambda i, j, k: (i, j))
    one = pl.BlockSpec((1, tn), lambda i, j, k: (0, 0))
    return pl.pallas_call(
        body, name=name, grid=(S // tm, D // tn, nk),
        in_specs=[pl.BlockSpec((None, tm, FF_SLAB), lambda i, j, k: (k, i, 0)),
                  pl.BlockSpec((None, FF_SLAB, tn), lambda i, j, k: (FF_PAIRS * (k & 1) + (k >> 1), 0, j))] + [o_spec] * has_res
        + ([o_spec, one, o_spec] if has_norm else []) + [ANY] * len(after),
        out_specs=[o_spec, one] if has_norm else o_spec, out_shape=[_sds((S, D)), _sds((1, D))] if has_norm else _sds((S, D)),
        scratch_shapes=[pltpu.VMEM((tm, tn), F32)],
        compiler_params=_params(("arbitrary" if has_norm else "parallel", "parallel", "arbitrary")),
    )(*((a, w) + ((res,) if has_res else ()) + (tuple(norm_bwd) if has_norm else ()) + tuple(after)))


def _local_step(x, tgt, norm1_w, w_a, w_z, w_b, conv_a, a_log, dt_bias, gnw, norm2_w, final_w, late_weights, emit, start_after=()):
    wgrad = functools.partial(_mm, ta=True, out_dtype=BF16)
    h1, proj_a, proj_z, proj_b = _in_proj(x, norm1_w, w_a, w_z, w_b, after=start_after, name="in_proj")
    qn, kn, v, gcb, bb = _gdn_prep_fwd(proj_a, conv_a, a_log, dt_bias, name="gdn_prep_fwd")
    uv, wk, at, tmat, wkb, qdb, keb = _gdn_chunk_fwd(qn, kn, v, gcb, bb, name="gdn_chunk_fwd")
    o, u, sp, oab = _gdn_scan_fwd(uv, at, wkb, qdb, keb, gcb, proj_z, gnw, name="gdn_scan_fwd")
    oab, lse = _attn_fwd(proj_b, oab, name="attn_fwd")
    w_out, w_up, conv_f, w_down = late_weights(oab)
    x1, h2 = _out_proj_norm(oab, w_out, x, norm2_w, name="out_proj")
    dx2, dx2_b, d_final, loss, ug, uu = _ffn_fwd(h2, x1, w_up, conv_f, w_down, final_w, tgt, name="ffn_fwd")
    du, g_down, g_up, dcw = _ffn_bwd(dx2_b, h2, ug, uu, conv_f, w_down, name="ffn_bwd")
    token = emit("ffn", w_down=g_down, w_up=g_up.reshape(N_DEV, FF_SLAB, -1), conv_f=dcw.reshape(N_DEV, FFN_CONV, -1))
    dx1, d_norm2 = _mm_slabs(du.reshape(N_DEV, -1, FF_SLAB), w_up, norm_bwd=(x1, norm2_w, dx2), after=token, name="ffn_up_dx")
    d_oab = _mm(dx1, w_out, tb=True, name="out_proj_dx", tk=1024)
    token = emit("out", w_out=wgrad(oab, dx1, name="out_proj_dw"))
    dz, d_gnw, du, dwk, dat, dqd, dke, dgl = _gdn_scan_bwd(d_oab, o, proj_z, gnw, sp, u, at, wkb, qdb, keb, gcb, after=token, name="gdn_scan_bwd")
    dqn, dkn, dv, dg, dbeta = _gdn_chunk_bwd(qn, kn, gcb, bb, tmat, uv, wk, du, dwk, dat, dqd, dke, dgl, name="gdn_chunk_bwd")
    dc, dba, d_small = _gdn_prep_bwd(dqn, dkn, dv, dg, dbeta, proj_a, conv_a, a_log, dt_bias, name="gdn_prep_bwd")
    d_pa, d_conv_a = _gdn_conv_bwd(dc, dba, proj_a, conv_a, name="gdn_conv_bwd")
    d_pb = _attn_bwd(proj_b, oab, d_oab, lse, name="attn_bwd")
    g_a = wgrad(d_pa, h1, name="proj_a_dw", tm=A_COLS)
    g_z = wgrad(dz, h1, name="proj_z_dw")
    g_b = wgrad(d_pb, h1, name="proj_b_dw", tm=768)
    token = emit("in", w_a=g_a, w_z=g_z, w_b=g_b, conv_a=d_conv_a)
    dh1 = _mm(dz, w_z, after=token, name="proj_z_dx")
    dh1 = _mm(d_pa, w_a, res=dh1, name="proj_a_dx", tk=A_COLS)
    grad_x, d_norm1 = _mm(d_pb, w_b, res=dh1, norm_bwd=(x, norm1_w, dx1), name="proj_b_dx", tn=D_MODEL, tk=1536)
    small = dict(norm1=d_norm1, small=d_small, gnw=d_gnw, norm2=d_norm2, final=d_final)
    return loss, grad_x, small


_O1 = 3 * GDN_WIDTH
_O2 = _O1 + GDN_WIDTH
_O3 = _O2 + 2 * GDN_HEADS


def _split_w_in(w_t):
    d = w_t.shape[1]
    pad = jnp.zeros((A_COLS - _O1 - 2 * GDN_HEADS, d), w_t.dtype)
    w_a = jnp.concatenate([w_t[:_O1], w_t[_O2:_O3], pad], axis=0)
    w_b = w_t[_O3:].reshape(3, DIL_PAIRS, 128, d).transpose(1, 0, 2, 3).reshape(3 * DIL_WIDTH, d)
    return w_a, w_t[_O1:_O2], w_b


def _merge_g_in(g_a, g_z, g_b):
    d = g_a.shape[1]
    g_b = g_b.reshape(DIL_PAIRS, 3, 128, d).transpose(1, 0, 2, 3).reshape(3 * DIL_WIDTH, d)
    return jnp.concatenate([g_a[:_O1], g_z, g_a[_O1:_O1 + 2 * GDN_HEADS], g_b], axis=0)


MESH = pl.DeviceIdType.MESH
ANY = pl.BlockSpec(memory_space=pl.ANY)


def _position():
    return lax.axis_index("x"), lax.axis_index("y"), lax.axis_index("c")


def _slot(p):
    return 4 * p[0] + 2 * p[1] + p[2]


def _all_gather(blocks, *, name):
    n = len(blocks)

    def body(*refs):
        ins, outs = refs[:n], refs[n:2 * n]
        send_sems, recv_sems, local_sems = refs[2 * n:]
        x, y, c = _position()
        me, sibling = (x, y, c), (x, y, 1 - c)
        chips = [(1 - x, y), (x, 1 - y), (1 - x, 1 - y)]

        def copy(a, k, block, to, src=None):
            dst = outs[a].at[_slot(block)]
            return pltpu.make_async_remote_copy(
                src_ref=dst if src is None else src, dst_ref=dst, send_sem=send_sems.at[a, k], recv_sem=recv_sems.at[a, k],
                device_id=to, device_id_type=MESH)

        mine = [pltpu.make_async_copy(ins[a], outs[a].at[_slot(me)], local_sems.at[a]) for a in range(n)]
        for cp in mine:
            cp.start()
        first = []
        for a in range(n):
            first.append(copy(a, 0, me, sibling, src=ins[a]))
            first += [copy(a, 1 + j, me, (*chip, c), src=ins[a]) for j, chip in enumerate(chips)]
        for cp in first:
            cp.start()
        passed = []
        for j, chip in enumerate(chips):
            for a in range(n):
                copy(a, 1 + j, (*chip, c), me).wait_recv()
                fwd = copy(a, 4 + j, (*chip, c), sibling)
                fwd.start()
                passed.append(fwd)
        for a in range(n):
            copy(a, 0, sibling, me).wait_recv()
            for j, chip in enumerate(chips):
                copy(a, 4 + j, (*chip, 1 - c), me).wait_recv()
        for cp in first + passed:
            cp.wait_send()
        for cp in mine:
            cp.wait()

    return pl.pallas_call(
        body, name=name, in_specs=[ANY] * n, out_specs=[ANY] * n,
        out_shape=[_sds((N_DEV,) + b.shape, b.dtype) for b in blocks],
        scratch_shapes=[pltpu.SemaphoreType.DMA((n, 7)), pltpu.SemaphoreType.DMA((n, 7)), pltpu.SemaphoreType.DMA((n,))],
    )(*blocks)


def _gather_direct(block, *, name, after=()):
    def body(in_ref, *rest):
        out_ref, send_sems, recv_sems, local_sem = rest[len(after):]
        x, y, c = _position()
        me = _slot((x, y, c))
        mine = pltpu.make_async_copy(in_ref, out_ref.at[me], local_sem)
        mine.start()
        copies = [pltpu.make_async_remote_copy(
            src_ref=in_ref, dst_ref=out_ref.at[me], send_sem=send_sems.at[k - 1], recv_sem=recv_sems.at[k - 1],
            device_id=_peer_of(k, x, y, c), device_id_type=MESH) for k in range(1, N_DEV)]
        for cp in copies:
            cp.start()
        for cp in copies:
            cp.wait()
        mine.wait()

    return pl.pallas_call(
        body, name=name, in_specs=[pl.BlockSpec(memory_space=pltpu.VMEM)] + [ANY] * len(after),
        out_specs=pl.BlockSpec(memory_space=pltpu.VMEM),
        out_shape=_sds((N_DEV,) + block.shape, block.dtype),
        scratch_shapes=[pltpu.SemaphoreType.DMA((N_DEV - 1,)), pltpu.SemaphoreType.DMA((N_DEV - 1,)), pltpu.SemaphoreType.DMA],
    )(block, *after)


HBM = pl.BlockSpec(memory_space=pltpu.HBM)
SEM = pl.BlockSpec(memory_space=pltpu.SEMAPHORE)
EFFECT = pltpu.SideEffectType.DATAFLOW_SIDE_EFFECTING


def _peer_of(k, x, y, c):
    return (1 - x if k & 4 else x, 1 - y if k & 2 else y, 1 - c if k & 1 else c)


def _flight(a, k):
    return a * (N_DEV - 1) + k - 1


def _exchange_start(arrays, *, name, broadcast=False, paired=()):
    n = len(arrays)

    def body(*refs):
        ins, lands = refs[:n], refs[n:2 * n]
        send_sems, recv_sems = refs[2 * n:2 * n + 2]
        token = refs[-1]
        x, y, c = _position()
        me = _slot((x, y, c))
        for k in range(1, N_DEV):
            peer = _peer_of(k, x, y, c)
            for a in range(n):
                at = _pair_slot(_slot(peer)) if a in paired else _slot(peer)
                pltpu.make_async_remote_copy(
                    src_ref=ins[a] if broadcast else ins[a].at[at], dst_ref=lands[a].at[me],
                    send_sem=send_sems.at[_flight(a, k)], recv_sem=recv_sems.at[_flight(a, k)],
                    device_id=peer, device_id_type=MESH).start()
        token[...] = jnp.zeros_like(token)

    land_shapes = [((N_DEV,) + s.shape) if broadcast else s.shape for s in arrays]
    lands = [pltpu.with_memory_space_constraint(lax.empty(shp, s.dtype), pltpu.HBM) for shp, s in zip(land_shapes, arrays)]
    srcs = [pltpu.with_memory_space_constraint(s, pltpu.HBM) for s in arrays]
    outs = pl.pallas_call(
        body, name=name, in_specs=[HBM] * (2 * n),
        out_specs=[SEM, SEM] + [HBM] * (2 * n) + [pl.BlockSpec(memory_space=pltpu.VMEM)],
        out_shape=[pltpu.SemaphoreType.DMA((n * (N_DEV - 1),)), pltpu.SemaphoreType.DMA((n * (N_DEV - 1),))]
        + [pltpu.HBM(s.shape, s.dtype) for s in arrays] + [pltpu.HBM(shp, s.dtype) for shp, s in zip(land_shapes, arrays)]
        + [_sds((8, 128))],
        input_output_aliases={i: 2 + i for i in range(2 * n)},
        compiler_params=pltpu.CompilerParams(has_side_effects=EFFECT),
    )(*srcs, *lands)
    return outs[0], outs[1], outs[2:2 + n], outs[2 + n:2 + 2 * n], outs[-1]


def _exchange_wait(send_sems, recv_sems, srcs, lands, after, *, name, broadcast=False):
    n = len(srcs)

    def body(*refs):
        ins, lnd = refs[:n], refs[n:2 * n]
        send_ref, recv_ref = refs[2 * n:2 * n + 2]
        x, y, c = _position()
        for k in range(1, N_DEV):
            for a in range(n):
                cp = pltpu.make_async_remote_copy(
                    src_ref=ins[a] if broadcast else ins[a].at[0], dst_ref=lnd[a].at[0], send_sem=send_ref.at[_flight(a, k)],
                    recv_sem=recv_ref.at[_flight(a, k)], device_id=_peer_of(k, x, y, c), device_id_type=MESH)
                cp.wait_send()
                cp.wait_recv()

    outs = pl.pallas_call(
        body, name=name, in_specs=[HBM] * (2 * n) + [SEM, SEM, ANY], out_specs=[HBM] * (2 * n),
        out_shape=[pltpu.HBM(s.shape, s.dtype) for s in srcs] + [pltpu.HBM(s.shape, s.dtype) for s in lands],
        input_output_aliases={i: i for i in range(2 * n)},
        compiler_params=pltpu.CompilerParams(has_side_effects=EFFECT),
    )(*srcs, *lands, send_sems, recv_sems, after)
    return outs[:n], outs[n:]


def _with_own(landed, srcs, me):
    return [lax.dynamic_update_index_in_dim(l, o, me, 0) for l, o in zip(landed, srcs)]


def _adam_update(g, w, m, v):
    c1 = 1.0 - ADAM_B1 ** ADAM_STEP
    c2 = 1.0 - ADAM_B2 ** ADAM_STEP
    nm = ADAM_B1 * m + (1.0 - ADAM_B1) * g
    nv = ADAM_B2 * v + (1.0 - ADAM_B2) * (g * g)
    return -ADAM_LR * ((nm / c1) / (jnp.sqrt(nv / c2) + ADAM_EPS) + ADAM_WD * w), nm, nv


def _adamw(landed, sent, me, w, m, v, *, name, tr=None, tc=None):
    R, C = w.shape
    tr = R if tr is None else tr
    tc = C if tc is None else tc
    assert R % tr == 0 and C % tc == 0

    def body(me_ref, own_ref, p_ref, w_ref, m_ref, v_ref, g_ref, d_ref, nm_ref, nv_ref, token_ref):
        token_ref[...] = jnp.zeros_like(token_ref)
        g = own_ref[...].astype(F32)
        for s in range(N_DEV):
            g = g + jnp.where(me_ref[1] == s, 0.0, p_ref[s].astype(F32))
        delta, nm, nv = _adam_update(g, w_ref[...], m_ref[...], v_ref[...])
        g_ref[...] = g
        nm_ref[...] = nm
        nv_ref[...] = nv
        d_ref[...] = delta

    blk = pl.BlockSpec((tr, tc), lambda i, j, me_ref: (i, j))
    return pl.pallas_call(
        body, name=name,
        grid_spec=pltpu.PrefetchScalarGridSpec(
            num_scalar_prefetch=1, grid=(R // tr, C // tc),
            in_specs=[pl.BlockSpec((None, tr, tc), lambda i, j, me_ref: (me_ref[0], i, j)),
                      pl.BlockSpec((N_DEV, tr, tc), lambda i, j, me_ref: (0, i, j)), blk, blk, blk],
            out_specs=[blk] * 4 + [pl.BlockSpec((8, 128), lambda i, j, me_ref: (0, 0))]),
        out_shape=[_sds((R, C))] * 4 + [_sds((8, 128))],
        compiler_params=_params(("arbitrary", "arbitrary")),
    )(me, sent, landed, w, m, v)


_SMALL_ROWS = 8
_SMALL_SLOTS = ((0, 0, D_MODEL), (1, 0, D_MODEL), (2, 0, D_MODEL), (3, 0, GDN_DIM), (3, GDN_DIM, GDN_HEADS),
                (3, GDN_DIM + GDN_HEADS, GDN_HEADS))
_LOSS_LANE = 2 * GDN_DIM


def _pack_small(norm1, norm2, final, gnw, a_log, dt_bias, loss):
    row3 = jnp.concatenate([gnw, a_log, dt_bias, jnp.zeros((1, 128 - 2 * GDN_HEADS), F32), loss,
                            jnp.zeros((1, D_MODEL - 3 * 128), F32)], axis=1)
    return jnp.concatenate([norm1, norm2, final, row3, jnp.zeros((_SMALL_ROWS - 4, D_MODEL), F32)], axis=0)


def _adamw_small(packs, ws, ms, vs, *, name):
    n = len(ws)

    def body(p_ref, *refs):
        w_refs, m_refs, v_refs = refs[:n], refs[n:2 * n], refs[2 * n:3 * n]
        outs = refs[3 * n:]
        g_all = p_ref[0]
        for s in range(1, N_DEV):
            g_all = g_all + p_ref[s]
        for i, (row, lane, width) in enumerate(_SMALL_SLOTS):
            g = g_all[row:row + 1, lane:lane + width]
            delta, nm, nv = _adam_update(g, w_refs[i][...], m_refs[i][...], v_refs[i][...])
            for o_ref, val in zip(outs[4 * i:4 * i + 4], (g, delta, nm, nv)):
                o_ref[...] = val
        outs[-1][...] = g_all[3:4, _LOSS_LANE:_LOSS_LANE + 128]

    vm = pl.BlockSpec(memory_space=pltpu.VMEM)
    outs = pl.pallas_call(
        body, name=name, in_specs=[vm] * (1 + 3 * n), out_specs=[vm] * (4 * n + 1),
        out_shape=[_sds(w.shape) for w in ws for _ in range(4)] + [_sds((1, 128))],
    )(packs, *ws, *ms, *vs)
    return [outs[4 * i:4 * i + 4] for i in range(n)], outs[-1]


def _slabs_by_cols(g):
    r = g.shape[0]
    return g.reshape(r, N_DEV, -1).transpose(1, 0, 2)


def _cols_from_slabs(s):
    return s.transpose(1, 0, 2).reshape(s.shape[1], -1)


def kernel(x, norm1_w, w_in, conv_qkv_w, a_log, dt_bias, gdn_norm_w, w_out, norm2_w, w_up, ffn_conv_w, w_down, final_norm_w, loss_target, m_norm1_w, m_w_in, m_conv_qkv_w, m_a_log, m_dt_bias, m_gdn_norm_w, m_w_out, m_norm2_w, m_w_up, m_ffn_conv_w, m_w_down, m_final_norm_w, v_norm1_w, v_w_in, v_conv_qkv_w, v_a_log, v_dt_bias, v_gdn_norm_w, v_w_out, v_norm2_w, v_w_up, v_ffn_conv_w, v_w_down, v_final_norm_w):
    bf = lambda a: a.astype(BF16)
    me = _slot(_position())
    t_in = lambda a: a[0].T
    gw_in, g_conv_a = _all_gather([bf(t_in(w_in)), conv_qkv_w[0]], name="gather_w_in")
    w_a, w_z, w_b = _split_w_in(gw_in.reshape(-1, D_MODEL))
    late_src, _ = lax.optimization_barrier(([bf(w_out[0]), bf(t_in(w_up)), bf(w_down[0]), ffn_conv_w[0]], gw_in))
    l_send, l_recv, l_srcs, l_lands, l_token = _exchange_start(late_src, name="weights_start", broadcast=True)

    def late_weights(after):
        srcs, landed = _exchange_wait(l_send, l_recv, l_srcs, l_lands, after, name="weights_wait", broadcast=True)
        gw_out, gw_up, gw_down, g_conv_f = _with_own(landed, srcs, me)
        return gw_out.reshape(D_MODEL, D_MODEL), gw_up, g_conv_f, gw_down.reshape(D_FF, D_MODEL)

    flights = {}

    def emit(group, **grads):
        paired = ()
        if group == "in":
            slabs = dict(w_in=_merge_g_in(grads["w_a"], grads["w_z"], grads["w_b"]).reshape(N_DEV, -1, D_MODEL),
                         conv_a=_slabs_by_cols(grads["conv_a"]))
        elif group == "ffn":
            slabs = dict(w_down=grads["w_down"].reshape(N_DEV, -1, D_MODEL), w_up=grads["w_up"], conv_f=grads["conv_f"])
            paired = (1, 2)
        else:
            slabs = {k: v.reshape(N_DEV, -1, D_MODEL) for k, v in grads.items()}
        names = list(slabs)
        *flight, token = _exchange_start([slabs[k] for k in names], paired=paired, name="grads_start_" + group)
        flights[group] = (names, flight)
        return (token,)

    loss, grad_x, g = _local_step(
        x[0], loss_target[0], norm1_w, w_a, w_z, w_b, _cols_from_slabs(g_conv_a), a_log, dt_bias,
        gdn_norm_w, norm2_w, final_norm_w[None], late_weights, emit, start_after=(l_token,))
    got = {}
    me1 = jnp.reshape(me, (1,)).astype(jnp.int32)

    def collect(group, after):
        names, (send_sems, recv_sems, srcs, lands) = flights[group]
        srcs, landed = _exchange_wait(send_sems, recv_sems, srcs, lands, after, name="grads_wait_" + group)
        got.update(zip(names, zip(landed, srcs)))

    def update(key, w, m, v, paired=False, **tiles):
        where = jnp.concatenate([_pair_slot(me1) if paired else me1, me1])
        return _adamw(*got[key], where, w, m, v, name="adamw_" + key, **tiles)

    collect("ffn", grad_x)
    collect("out", grad_x)
    *o_out, t1 = update("w_out", w_out[0], m_w_out[0], v_w_out[0])
    *o_up, t2 = update("w_up", t_in(w_up), t_in(m_w_up), t_in(v_w_up), paired=True, tr=176)
    o_up = [o.T for o in o_up]
    *o_down, t3 = update("w_down", w_down[0], m_w_down[0], v_w_down[0], tr=176)
    *o_cf, t4 = update("conv_f", ffn_conv_w[0], m_ffn_conv_w[0], v_ffn_conv_w[0], paired=True)
    pack = _pack_small(g["norm1"], g["norm2"], g["final"], g["gnw"], g["small"][:, 0:GDN_HEADS],
                       g["small"][:, GDN_HEADS:2 * GDN_HEADS], loss)
    small_all = _gather_direct(pack, after=(t1, t2, t3, t4), name="gather_small")
    collect("in", small_all)
    o_in = [o.T for o in update("w_in", t_in(w_in), t_in(m_w_in), t_in(v_w_in), tc=256)[:4]]
    o_ca = update("conv_a", conv_qkv_w[0], m_conv_qkv_w[0], v_conv_qkv_w[0])
    (o_n1, o_n2, o_fin, o_gn, o_al, o_dt), total = _adamw_small(
        small_all, (norm1_w, norm2_w, final_norm_w[None], gdn_norm_w, a_log, dt_bias),
        (m_norm1_w, m_norm2_w, m_final_norm_w[None], m_gdn_norm_w, m_a_log, m_dt_bias),
        (v_norm1_w, v_norm2_w, v_final_norm_w[None], v_gdn_norm_w, v_a_log, v_dt_bias), name="adamw_small")
    outs = [total[0, 0], grad_x[None]]
    for k in range(4):
        outs += [o_n1[k], o_in[k][None], o_ca[k][None], o_al[k], o_dt[k], o_gn[k], o_out[k][None], o_n2[k], o_up[k][None],
                 o_cf[k][None], o_down[k][None], o_fin[k][0]]
    return tuple(outs)
```

```python
import functools

import jax
import jax.numpy as jnp
from jax import lax
from jax.experimental import pallas as pl
from jax.experimental.pallas import tpu as pltpu

F32 = jnp.float32
BF16 = jnp.bfloat16

N_DEV = 8
D_MODEL = 1024
GDN_HEADS = 4
GDN_DIM = 128
GDN_WIDTH = GDN_HEADS * GDN_DIM
GDN_CONV = 4
CHUNK = 64
CHUNKS_PER_STEP = 4
DIL_HEADS = 8
DIL_DIM = 64
DIL_WIDTH = DIL_HEADS * DIL_DIM
DIL_PAIRS = DIL_HEADS // 2
DILATIONS = (1, 4, 16)
BAND = 128
D_FF = 2816
FFN_CONV = 3
EPS = 1e-6
A_COLS = 3 * GDN_WIDTH + 128
HALO = 8

ADAM_LR = 0.001
ADAM_B1 = 0.9
ADAM_B2 = 0.999
ADAM_EPS = 1e-08
ADAM_WD = 0.01
ADAM_STEP = 10

VMEM_LIMIT_BYTES = 56 * 1024 * 1024
NEG_BIG = -1e30


def _params(sem=None):
    return pltpu.CompilerParams(dimension_semantics=sem, vmem_limit_bytes=VMEM_LIMIT_BYTES)


def _sds(shape, dtype=F32):
    return jax.ShapeDtypeStruct(shape, dtype)


def _bdot(a, b):
    return jnp.dot(a.astype(BF16), b.astype(BF16), preferred_element_type=F32)


def _bdot_nt(a, b):
    return lax.dot_general(a.astype(BF16), b.astype(BF16), (((1,), (1,)), ((), ())), preferred_element_type=F32)


def _bdot_tn(a, b):
    return lax.dot_general(a.astype(BF16), b.astype(BF16), (((0,), (0,)), ((), ())), preferred_element_type=F32)


def _split(a):
    hi = a.astype(BF16)
    lo = (a - hi.astype(F32)).astype(BF16)
    return hi, lo


def _dot3(a, b, dims):
    ah, al = _split(a)
    bh, bl = _split(b)
    d = functools.partial(lax.dot_general, dimension_numbers=(dims, ((), ())), preferred_element_type=F32)
    return d(ah, bh) + (d(al, bh) + d(ah, bl))


def _exact_tri_dot(tri, g):
    g1 = g.astype(BF16)
    r1 = g - g1.astype(F32)
    g2 = r1.astype(BF16)
    g3 = (r1 - g2.astype(F32)).astype(BF16)
    t = tri.astype(BF16)
    d = functools.partial(jnp.dot, preferred_element_type=F32)
    return d(t, g1) + (d(t, g2) + d(t, g3))


def _sigmoid(x):
    return 1.0 / (1.0 + jnp.exp(-x))


def _dsilu(x, sg):
    return sg * (1.0 + x * (1.0 - sg))


def _rms_bwd_rows(dh, x, w):
    r = lax.rsqrt(jnp.mean(x * x, axis=-1, keepdims=True) + EPS)
    xh = x * r
    gw = dh * w
    return r * (gw - xh * jnp.mean(gw * xh, axis=-1, keepdims=True)), jnp.sum(dh * xh, axis=0, keepdims=True)


def _mm(a, b, *, name, ta=False, tb=False, res=None, norm_bwd=None, after=(), out_dtype=F32, tm=512, tn=512, tk=512):
    if ta:
        K, M = a.shape
    else:
        M, K = a.shape
    if tb:
        N, Kb = b.shape
    else:
        Kb, N = b.shape
    assert K == Kb, (a.shape, b.shape)
    tm, tn, tk = min(tm, M), min(tn, N), min(tk, K)
    assert M % tm == 0 and N % tn == 0 and K % tk == 0, (name, M, N, K, tm, tn, tk)
    nk = K // tk
    dims = (((0 if ta else 1,), (1 if tb else 0,)), ((), ()))
    has_res = res is not None
    has_norm = norm_bwd is not None
    assert not has_norm or tn == N

    def body(*refs):
        a_ref, b_ref = refs[:2]
        r_ref = refs[2] if has_res else None
        if has_norm:
            x_ref, w_ref, skip_ref = refs[2 + has_res:5 + has_res]
            o_ref, dw_ref, acc_ref = refs[-3:]
        else:
            o_ref, acc_ref = refs[-2:]
        i, k = pl.program_id(0), pl.program_id(2)
        part = lax.dot_general(a_ref[...].astype(BF16), b_ref[...].astype(BF16), dims, preferred_element_type=F32)

        @pl.when(k == 0)
        def _():
            acc_ref[...] = part

        @pl.when(k > 0)
        def _():
            acc_ref[...] += part

        @pl.when(k == nk - 1)
        def _():
            r = acc_ref[...]
            if has_res:
                r = r + r_ref[...]
            if has_norm:
                dx, dw = _rms_bwd_rows(r, x_ref[...], w_ref[...])
                o_ref[...] = skip_ref[...] + dx

                @pl.when(i == 0)
                def _():
                    dw_ref[...] = dw

                @pl.when(i > 0)
                def _():
                    dw_ref[...] += dw
            else:
                o_ref[...] = r.astype(out_dtype)

    a_spec = pl.BlockSpec((tk, tm), lambda i, j, k: (k, i)) if ta else pl.BlockSpec((tm, tk), lambda i, j, k: (i, k))
    b_spec = pl.BlockSpec((tn, tk), lambda i, j, k: (j, k)) if tb else pl.BlockSpec((tk, tn), lambda i, j, k: (k, j))
    o_spec = pl.BlockSpec((tm, tn), lambda i, j, k: (i, j))
    one = pl.BlockSpec((1, tn), lambda i, j, k: (0, 0))
    in_specs = [a_spec, b_spec] + [o_spec] * has_res + ([o_spec, one, o_spec] if has_norm else []) + [ANY] * len(after)
    args = (a, b) + ((res,) if has_res else ()) + (tuple(norm_bwd) if has_norm else ()) + tuple(after)
    return pl.pallas_call(
        body, name=name, grid=(M // tm, N // tn, nk), in_specs=in_specs,
        out_specs=[o_spec, one] if has_norm else o_spec,
        out_shape=[_sds((M, N)), _sds((1, N))] if has_norm else _sds((M, N), out_dtype),
        scratch_shapes=[pltpu.VMEM((tm, tn), F32)],
        compiler_params=_params(("arbitrary" if has_norm else "parallel", "parallel", "arbitrary")),
    )(*args)


def _in_proj(x, norm_w, w_a, w_z, w_b, *, name, after=(), tm=512):
    S, D = x.shape
    ws = (w_a, w_z, w_b)

    def body(x_ref, nw_ref, wa_ref, wz_ref, wb_ref, *rest):
        h_ref, pa_ref, pz_ref, pb_ref = rest[len(after):]
        xv = x_ref[...]
        r = lax.rsqrt(jnp.mean(xv * xv, axis=-1, keepdims=True) + EPS)
        h = (xv * r * nw_ref[...]).astype(BF16)
        h_ref[...] = h
        for w_ref, p_ref in ((wa_ref, pa_ref), (wz_ref, pz_ref), (wb_ref, pb_ref)):
            p_ref[...] = lax.dot_general(h, w_ref[...], (((1,), (1,)), ((), ())), preferred_element_type=F32)

    row = lambda n: pl.BlockSpec((tm, n), lambda i: (i, 0))
    full = lambda a: pl.BlockSpec(a.shape, lambda i: (0, 0))
    return pl.pallas_call(
        body, name=name, grid=(S // tm,), in_specs=[row(D), full(norm_w)] + [full(w) for w in ws] + [ANY] * len(after),
        out_specs=[row(D)] + [row(w.shape[0]) for w in ws],
        out_shape=[_sds((S, D), BF16)] + [_sds((S, w.shape[0])) for w in ws], compiler_params=_params(("parallel",)),
    )(x, norm_w, *ws, *after)


def _in_proj_dx(ds, ws, x, norm_w, skip, *, name, after=(), tm=512):
    S, D = x.shape
    n = len(ds)

    def body(*refs):
        d_refs, w_refs = refs[:n], refs[n:2 * n]
        x_ref, nw_ref, skip_ref = refs[2 * n:2 * n + 3]
        o_ref, dw_ref = refs[-2:]
        i = pl.program_id(0)
        dh = jnp.dot(d_refs[0][...], w_refs[0][...], preferred_element_type=F32)
        for d_ref, w_ref in zip(d_refs[1:], w_refs[1:]):
            dh = dh + jnp.dot(d_ref[...], w_ref[...], preferred_element_type=F32)
        dx, dw = _rms_bwd_rows(dh, x_ref[...], nw_ref[...])
        o_ref[...] = skip_ref[...] + dx

        @pl.when(i == 0)
        def _():
            dw_ref[...] = dw

        @pl.when(i > 0)
        def _():
            dw_ref[...] += dw

    row = lambda c: pl.BlockSpec((tm, c), lambda i: (i, 0))
    full = lambda a: pl.BlockSpec(a.shape, lambda i: (0, 0))
    return pl.pallas_call(
        body, name=name, grid=(S // tm,),
        in_specs=[row(d.shape[1]) for d in ds] + [full(w) for w in ws] + [row(D), full(norm_w), row(D)] + [ANY] * len(after),
        out_specs=[row(D), pl.BlockSpec((1, D), lambda i: (0, 0))], out_shape=[_sds((S, D)), _sds((1, D))],
        compiler_params=_params(("arbitrary",)),
    )(*ds, *ws, x, norm_w, skip, *after)


def _out_proj_norm(a, w, x, norm_w, *, name, tm=512):
    S, D = x.shape

    def body(a_ref, w_ref, x_ref, nw_ref, x1_ref, h_ref):
        x1 = x_ref[...] + jnp.dot(a_ref[...], w_ref[...], preferred_element_type=F32)
        x1_ref[...] = x1
        r = lax.rsqrt(jnp.mean(x1 * x1, axis=-1, keepdims=True) + EPS)
        h_ref[...] = (x1 * r * nw_ref[...]).astype(BF16)

    row = pl.BlockSpec((tm, D), lambda i: (i, 0))
    return pl.pallas_call(
        body, name=name, grid=(S // tm,),
        in_specs=[pl.BlockSpec((tm, a.shape[1]), lambda i: (i, 0)), pl.BlockSpec(w.shape, lambda i: (0, 0)), row,
                  pl.BlockSpec((1, D), lambda i: (0, 0))],
        out_specs=[row, row], out_shape=[_sds((S, D)), _sds((S, D), BF16)], compiler_params=_params(("parallel",)),
    )(a, w, x, norm_w)


def _shifted(x, start, n):
    aligned = -(-start // HALO) * HALO
    assert aligned + n <= x.shape[0], (start, n, x.shape)
    return (x if aligned == start else pltpu.roll(x, aligned - start, axis=0))[aligned:aligned + n]


def _conv_rows(prev, cur, w, taps):
    n = cur.shape[0]
    xs = jnp.concatenate([prev, cur], axis=0)
    base = HALO - (taps - 1)
    out = _shifted(xs, base, n) * w[0:1]
    for i in range(1, taps):
        out = out + _shifted(xs, base + i, n) * w[i:i + 1]
    return out


def _conv_rows_bwd(cur_d, next_d, prev_x, cur_x, w, taps):
    n = cur_d.shape[0]
    ds = jnp.concatenate([cur_d, next_d], axis=0)
    dx = _shifted(ds, taps - 1, n) * w[0:1]
    for i in range(1, taps):
        dx = dx + _shifted(ds, taps - 1 - i, n) * w[i:i + 1]
    xs = jnp.concatenate([prev_x, cur_x], axis=0)
    base = HALO - (taps - 1)
    dws = [jnp.sum(cur_d * _shifted(xs, base + i, n), axis=0, keepdims=True) for i in range(taps)]
    return dx, jnp.concatenate(dws, axis=0)


def _halo_specs(tm, width, col, nblk):
    per = tm // HALO
    prev = pl.BlockSpec((HALO, width), lambda i, *_: (jnp.maximum(i * per - 1, 0), col))
    nxt = pl.BlockSpec((HALO, width), lambda i, *_: (jnp.minimum((i + 1) * per, nblk * per - 1), col))
    return prev, nxt


def _softplus(x):
    return jnp.maximum(x, 0.0) + jnp.log1p(jnp.exp(-jnp.abs(x)))


def _chunk_tri(tm, upper=False):
    r = lax.broadcasted_iota(jnp.int32, (tm, tm), 0)
    c = lax.broadcasted_iota(jnp.int32, (tm, tm), 1)
    same = lax.div(r, CHUNK) == lax.div(c, CHUNK)
    order = (c >= r) if upper else (c <= r)
    return jnp.where(same & order, 1.0, 0.0)


def _gdn_prep_fwd(proj_a, conv_w, a_log, dt_bias, *, name, tm=256):
    S = proj_a.shape[0]
    nblk = S // tm
    W3 = 3 * GDN_WIDTH

    def body(cur_ref, prev_ref, ba_ref, cw_ref, al_ref, dt_ref, qn_ref, kn_ref, v_ref, gcb_ref, bb_ref):
        i = pl.program_id(0)
        prev = jnp.where(i > 0, prev_ref[...], 0.0)
        c = _conv_rows(prev, cur_ref[...], cw_ref[...], GDN_CONV)
        a = c * _sigmoid(c)
        ba = ba_ref[...]
        lane = lax.broadcasted_iota(jnp.int32, (tm, 128), 1)
        g4 = jnp.zeros((tm, 128), F32)
        for h in range(GDN_HEADS):
            sl = slice(GDN_DIM * h, GDN_DIM * (h + 1))
            qh = a[:, GDN_DIM * h:GDN_DIM * (h + 1)]
            kh = a[:, GDN_WIDTH + GDN_DIM * h:GDN_WIDTH + GDN_DIM * (h + 1)]
            qn_ref[:, sl] = qh * (lax.rsqrt(jnp.sum(qh * qh, axis=-1, keepdims=True) + EPS) * (GDN_DIM ** -0.5))
            kn_ref[:, sl] = kh * lax.rsqrt(jnp.sum(kh * kh, axis=-1, keepdims=True) + EPS)
            beta = _sigmoid(ba[:, h:h + 1])
            bb_ref[:, sl] = jnp.broadcast_to(beta, (tm, GDN_DIM))
            g = -jnp.exp(al_ref[0:1, h:h + 1]) * _softplus(ba[:, GDN_HEADS + h:GDN_HEADS + h + 1] + dt_ref[0:1, h:h + 1])
            g4 = jnp.where(lane == h, g, g4)
        v_ref[...] = a[:, 2 * GDN_WIDTH:]
        gc = _exact_tri_dot(_chunk_tri(tm), g4)
        for h in range(GDN_HEADS):
            gcb_ref[:, GDN_DIM * h:GDN_DIM * (h + 1)] = jnp.broadcast_to(gc[:, h:h + 1], (tm, GDN_DIM))

    prev_spec, _ = _halo_specs(tm, W3, 0, nblk)
    row = pl.BlockSpec((tm, GDN_WIDTH), lambda i: (i, 0))
    small = lambda a: pl.BlockSpec(a.shape, lambda i: (0, 0))
    return pl.pallas_call(
        body, name=name, grid=(nblk,),
        in_specs=[pl.BlockSpec((tm, W3), lambda i: (i, 0)), prev_spec,
                  pl.BlockSpec((tm, 128), lambda i: (i, W3 // 128)), small(conv_w), small(a_log), small(dt_bias)],
        out_specs=[row] * 5, out_shape=[_sds((S, GDN_WIDTH))] * 5, compiler_params=_params(("parallel",)),
    )(proj_a, proj_a, proj_a, conv_w, a_log, dt_bias)


GDN_STACK = GDN_HEADS * CHUNK


def _stack(ref, rows):
    return jnp.concatenate([ref[rows, GDN_DIM * h:GDN_DIM * (h + 1)] for h in range(GDN_HEADS)], axis=0)


def _unstack_to(ref, rows, x):
    for h in range(GDN_HEADS):
        ref[rows, GDN_DIM * h:GDN_DIM * (h + 1)] = x[CHUNK * h:CHUNK * (h + 1)].astype(ref.dtype)


def _stack_masks():
    r = lax.broadcasted_iota(jnp.int32, (GDN_STACK, GDN_STACK), 0)
    c = lax.broadcasted_iota(jnp.int32, (GDN_STACK, GDN_STACK), 1)
    same = (r & -CHUNK) == (c & -CHUNK)
    return same & (r >= c), same & (r > c), r == c


def _stack_decay(gs, bs, incl):
    g2 = jnp.concatenate([gs, gs], axis=1)
    diff = g2 - g2.T
    dec = jnp.where(incl, jnp.exp(jnp.where(incl, diff, 0.0)), 0.0)
    return dec, jnp.concatenate([bs, bs], axis=1).T


def _head_mask():
    r = lax.broadcasted_iota(jnp.int32, (GDN_STACK, GDN_WIDTH), 0)
    c = lax.broadcasted_iota(jnp.int32, (GDN_STACK, GDN_WIDTH), 1)
    return (r & -CHUNK) * (GDN_DIM // CHUNK) == (c & -GDN_DIM)


def _head_spread(x):
    return jnp.where(_head_mask(), jnp.concatenate([x] * GDN_HEADS, axis=1), 0.0)


def _head_diag(x):
    xm = jnp.where(_head_mask(), x, 0.0)
    out = xm[:, 0:GDN_DIM]
    for h in range(1, GDN_HEADS):
        out = out + xm[:, GDN_DIM * h:GDN_DIM * (h + 1)]
    return out


def _last_rows(gs, n):
    return jnp.concatenate([jnp.broadcast_to(gs[CHUNK * (h + 1) - 1:CHUNK * (h + 1)], (n, GDN_DIM)) for h in range(GDN_HEADS)], axis=0)


def _gdn_chunk_fwd(qn, kn, v, gcb, bb, *, name):
    S = qn.shape[0]

    def body(qn_ref, kn_ref, v_ref, gcb_ref, bb_ref, uv_ref, wk_ref, at_ref, t_ref, wkb_ref, qdb_ref, keb_ref):
        incl, strict, diag = _stack_masks()
        for c in range(CHUNKS_PER_STEP):
            rows = slice(CHUNK * c, CHUNK * (c + 1))
            srows = slice(GDN_STACK * c, GDN_STACK * (c + 1))
            q, k, vv, gs, bs = [_stack(r, rows) for r in (qn_ref, kn_ref, v_ref, gcb_ref, bb_ref)]
            dec, bt = _stack_decay(gs, bs, incl)
            p = -jnp.where(strict, dec * _bdot_nt(k, k) * bt, 0.0)
            t = jnp.where(diag, 1.0, 0.0) + p
            for _ in range(5):
                p = _bdot(p, p)
                t = t + _bdot(t, p)
            sol = _dot3(t, jnp.concatenate([vv, jnp.exp(gs) * k], axis=1), ((1,), (0,)))
            _unstack_to(uv_ref, rows, sol[:, :GDN_DIM])
            _unstack_to(wk_ref, rows, sol[:, GDN_DIM:])
            at_ref[srows, :] = dec * _bdot_nt(q, k) * bt
            t_ref[srows, :] = t
            wkb_ref[srows, :] = _head_spread(sol[:, GDN_DIM:]).astype(BF16)
            qdb_ref[srows, :] = _head_spread(q * jnp.exp(gs)).astype(BF16)
            keb_ref[srows, :] = _head_spread(k * jnp.exp(_last_rows(gs, CHUNK) - gs) * bs).astype(BF16)

    step = CHUNKS_PER_STEP * CHUNK
    row = pl.BlockSpec((step, GDN_WIDTH), lambda n: (n, 0))
    sq = pl.BlockSpec((CHUNKS_PER_STEP * GDN_STACK, GDN_STACK), lambda n: (n, 0))
    wide = pl.BlockSpec((CHUNKS_PER_STEP * GDN_STACK, GDN_WIDTH), lambda n: (n, 0))
    nsq = S // CHUNK * GDN_STACK
    return pl.pallas_call(
        body, name=name, grid=(S // step,), in_specs=[row] * 5, out_specs=[row, row, sq, sq, wide, wide, wide],
        out_shape=[_sds((S, GDN_WIDTH)), _sds((S, GDN_WIDTH)), _sds((nsq, GDN_STACK)), _sds((nsq, GDN_STACK))]
        + [_sds((nsq, GDN_WIDTH), BF16)] * 3,
        compiler_params=_params(("parallel",)),
    )(qn, kn, v, gcb, bb)


SCAN_CHUNKS = 8


def _gdn_scan_fwd(uv, at, wkb, qdb, keb, gcb, proj_z, gnw, *, name):
    S = uv.shape[0]
    nc = S // CHUNK

    def body(uv_ref, at_ref, wkb_ref, qdb_ref, keb_ref, gcb_ref, z_ref, gnw_ref, o_ref, u_ref, sp_ref, oa_ref, st_ref):
        n = pl.program_id(0)

        @pl.when(n == 0)
        def _():
            st_ref[...] = jnp.zeros_like(st_ref)

        for c in range(SCAN_CHUNKS):
            rows = slice(CHUNK * c, CHUNK * (c + 1))
            srows = slice(GDN_STACK * c, GDN_STACK * (c + 1))
            st = st_ref[...]
            sp_ref[GDN_WIDTH * c:GDN_WIDTH * (c + 1), :] = st
            uv, gs, z = [_stack(r, rows) for r in (uv_ref, gcb_ref, z_ref)]
            u = uv - _bdot(wkb_ref[srows, :], st)
            o = _bdot(qdb_ref[srows, :], st) + _bdot(at_ref[srows, :], u)
            st_ref[...] = jnp.exp(_last_rows(gs, GDN_DIM)) * st + _bdot_tn(keb_ref[srows, :], u)
            _unstack_to(u_ref, rows, u)
            _unstack_to(o_ref, rows, o)
            r = lax.rsqrt(jnp.mean(o * o, axis=-1, keepdims=True) + EPS)
            oa = o * r * gnw_ref[...] * (z * _sigmoid(z))
            oa_ref[rows, :] = jnp.concatenate([oa[CHUNK * h:CHUNK * (h + 1)] for h in range(GDN_HEADS)], axis=1).astype(BF16)

    row = pl.BlockSpec((SCAN_CHUNKS * CHUNK, GDN_WIDTH), lambda n: (n, 0))
    sq = pl.BlockSpec((SCAN_CHUNKS * GDN_STACK, GDN_STACK), lambda n: (n, 0))
    wide = pl.BlockSpec((SCAN_CHUNKS * GDN_STACK, GDN_WIDTH), lambda n: (n, 0))
    return pl.pallas_call(
        body, name=name, grid=(nc // SCAN_CHUNKS,),
        in_specs=[row, sq, wide, wide, wide, row, row, pl.BlockSpec((1, GDN_DIM), lambda n: (0, 0))],
        out_specs=[row, row, pl.BlockSpec((SCAN_CHUNKS * GDN_WIDTH, GDN_DIM), lambda n: (n, 0)), row],
        out_shape=[_sds((S, GDN_WIDTH)), _sds((S, GDN_WIDTH)), _sds((nc * GDN_WIDTH, GDN_DIM)), _sds((S, 2 * GDN_WIDTH), BF16)],
        scratch_shapes=[pltpu.VMEM((GDN_WIDTH, GDN_DIM), F32)],
        compiler_params=_params(("arbitrary",)),
    )(uv, at, wkb, qdb, keb, gcb, proj_z, gnw)


def _gdn_scan_bwd(d_oab, o, proj_z, gnw, sp, u, at, wkb, qdb, keb, gcb, *, name, after=()):
    S = o.shape[0]
    nc = S // CHUNK
    ns = nc // SCAN_CHUNKS

    def body(do_ref, o_ref, z_ref, gnw_ref, sp_ref, u_ref, at_ref, wkb_ref, qdb_ref, keb_ref, gcb_ref, *rest):
        dz_ref, dgn_ref, du_ref, dwk_ref, dat_ref, dqd_ref, dke_ref, dgl_ref, ds_ref = rest[len(after):]
        n = pl.program_id(0)

        @pl.when(n == 0)
        def _():
            ds_ref[...] = jnp.zeros_like(ds_ref)
            dgn_ref[...] = jnp.zeros_like(dgn_ref)

        gw = gnw_ref[...]
        for c in reversed(range(SCAN_CHUNKS)):
            rows = slice(CHUNK * c, CHUNK * (c + 1))
            srows = slice(GDN_STACK * c, GDN_STACK * (c + 1))
            d_oa, oo, z, uu, gs = [_stack(r, rows) for r in (do_ref, o_ref, z_ref, u_ref, gcb_ref)]
            sg = _sigmoid(z)
            r = lax.rsqrt(jnp.mean(oo * oo, axis=-1, keepdims=True) + EPS)
            xh = oo * r
            dy = d_oa * (z * sg)
            _unstack_to(dz_ref, rows, d_oa * (xh * gw) * _dsilu(z, sg))
            dgn_ref[...] += jnp.sum(dy * xh, axis=0, keepdims=True)
            dxh = dy * gw
            do = r * (dxh - xh * jnp.mean(dxh * xh, axis=-1, keepdims=True))

            st = sp_ref[GDN_WIDTH * c:GDN_WIDTH * (c + 1), :]
            dst = ds_ref[...]
            ge = jnp.exp(_last_rows(gs, GDN_DIM))
            _unstack_to(dqd_ref, rows, _head_diag(_bdot_nt(do, st)))
            dat_ref[srows, :] = _bdot_nt(do, uu)
            du = _bdot_tn(at_ref[srows, :], do) + _bdot(keb_ref[srows, :], dst)
            _unstack_to(dke_ref, rows, _head_diag(_bdot_nt(uu, dst)))
            prod = dst * st
            for h in range(GDN_HEADS):
                blk = prod[GDN_DIM * h:GDN_DIM * (h + 1)]
                dge = jnp.sum(jnp.sum(blk, axis=1, keepdims=True), axis=0, keepdims=True)
                dgl_ref[c, :, GDN_DIM * h:GDN_DIM * (h + 1)] = jnp.broadcast_to(dge * ge[GDN_DIM * h:GDN_DIM * h + 1], (8, GDN_DIM))
            ds_ref[...] = _bdot_tn(qdb_ref[srows, :], do) + ge * dst - _bdot_tn(wkb_ref[srows, :], du)
            _unstack_to(du_ref, rows, du)
            _unstack_to(dwk_ref, rows, -_head_diag(_bdot_nt(du, st)))

    rev = lambda n: (ns - 1 - n, 0)
    row = pl.BlockSpec((SCAN_CHUNKS * CHUNK, GDN_WIDTH), rev)
    sq = pl.BlockSpec((SCAN_CHUNKS * GDN_STACK, GDN_STACK), rev)
    wide = pl.BlockSpec((SCAN_CHUNKS * GDN_STACK, GDN_WIDTH), rev)
    one = pl.BlockSpec((1, GDN_DIM), lambda n: (0, 0))
    return pl.pallas_call(
        body, name=name, grid=(ns,),
        in_specs=[row, row, row, one, pl.BlockSpec((SCAN_CHUNKS * GDN_WIDTH, GDN_DIM), rev), row, sq, wide, wide, wide, row]
        + [ANY] * len(after),
        out_specs=[row, one, row, row, sq, row, row, pl.BlockSpec((SCAN_CHUNKS, 8, GDN_WIDTH), lambda n: (ns - 1 - n, 0, 0))],
        out_shape=[_sds((S, GDN_WIDTH), BF16), _sds((1, GDN_DIM)), _sds((S, GDN_WIDTH)), _sds((S, GDN_WIDTH)),
                   _sds((nc * GDN_STACK, GDN_STACK)), _sds((S, GDN_WIDTH)), _sds((S, GDN_WIDTH)), _sds((nc, 8, GDN_WIDTH))],
        scratch_shapes=[pltpu.VMEM((GDN_WIDTH, GDN_DIM), F32)],
        compiler_params=_params(("arbitrary",)),
    )(d_oab, o, proj_z, gnw, sp, u, at, wkb, qdb, keb, gcb, *after)


def _gdn_chunk_bwd(qn, kn, gcb, bb, tmat, uv, wk, du, dwk, dat, dqd, dke, dgl, *, name):
    S = qn.shape[0]

    def body(qn_ref, kn_ref, gcb_ref, bb_ref, t_ref, uv_ref, wk_ref, du_ref, dwk_ref, dat_ref, dqd_ref, dke_ref,
             dgl_ref, dq_ref, dk_ref, dv_ref, dg_ref, dbeta_ref):
        incl, strict, _ = _stack_masks()
        lane = lax.broadcasted_iota(jnp.int32, (CHUNK, 128), 1)
        rowi = lax.broadcasted_iota(jnp.int32, (CHUNK, 1), 0)
        rsum = lambda x: jnp.sum(x, axis=-1, keepdims=True)
        for c in range(CHUNKS_PER_STEP):
            rows = slice(CHUNK * c, CHUNK * (c + 1))
            srows = slice(GDN_STACK * c, GDN_STACK * (c + 1))
            q, k, gs, bs, uv, wk, du, dwk, dqd, dke = [
                _stack(r, rows) for r in (qn_ref, kn_ref, gcb_ref, bb_ref, uv_ref, wk_ref, du_ref, dwk_ref, dqd_ref, dke_ref)]
            dec, bt = _stack_decay(gs, bs, incl)
            kk = _bdot_nt(k, k)
            qk = _bdot_nt(q, k)
            d_rhs = _dot3(t_ref[srows, :], jnp.concatenate([du, dwk], axis=1), ((0,), (0,)))
            sol = jnp.concatenate([uv, wk], axis=1)
            d_l = jnp.where(strict, -_dot3(d_rhs, sol, ((1,), (1,))), 0.0)
            d_a = jnp.where(incl, dat_ref[srows, :], 0.0)
            gam = jnp.exp(gs)
            e = jnp.exp(_last_rows(gs, CHUNK) - gs)
            d_gk = d_rhs[:, GDN_DIM:]
            ml = d_l * dec * bt
            ma = d_a * dec * bt
            _unstack_to(dq_ref, rows, _bdot(ma, k) + dqd * gam)
            _unstack_to(dk_ref, rows, _bdot(ml + ml.T, k) + _bdot_tn(ma, q) + d_gk * gam + dke * (e * bs))
            _unstack_to(dv_ref, rows, d_rhs[:, :GDN_DIM])
            wb = d_l * dec * kk + d_a * dec * qk
            ew = wb * bt
            s_ke = rsum(dke * k * (e * bs))
            dbeta = rsum(wb.T) + rsum(dke * k * e)
            dgc = rsum(ew) - rsum(ew.T) + rsum(dqd * q * gam) + rsum(d_gk * k * gam) - s_ke
            dgc4 = jnp.zeros((CHUNK, 128), F32)
            db4 = jnp.zeros((CHUNK, 128), F32)
            for h in range(GDN_HEADS):
                hr = slice(CHUNK * h, CHUNK * (h + 1))
                tail = jnp.sum(s_ke[hr], axis=0, keepdims=True) + dgl_ref[c, 0:1, GDN_DIM * h:GDN_DIM * h + 1]
                dgc4 = jnp.where(lane == h, dgc[hr] + jnp.where(rowi == CHUNK - 1, tail, 0.0), dgc4)
                db4 = jnp.where(lane == h, dbeta[hr], db4)
            dg_ref[rows, :] = _exact_tri_dot(_chunk_tri(CHUNK, upper=True), dgc4)
            dbeta_ref[rows, :] = db4

    step = CHUNKS_PER_STEP * CHUNK
    row = pl.BlockSpec((step, GDN_WIDTH), lambda n: (n, 0))
    sq = pl.BlockSpec((CHUNKS_PER_STEP * GDN_STACK, GDN_STACK), lambda n: (n, 0))
    col = pl.BlockSpec((step, 128), lambda n: (n, 0))
    return pl.pallas_call(
        body, name=name, grid=(S // step,),
        in_specs=[row] * 4 + [sq, row, row, row, row, sq, row, row,
                              pl.BlockSpec((CHUNKS_PER_STEP, 8, GDN_WIDTH), lambda n: (n, 0, 0))],
        out_specs=[row, row, row, col, col],
        out_shape=[_sds((S, GDN_WIDTH))] * 3 + [_sds((S, 128))] * 2, compiler_params=_params(("parallel",)),
    )(qn, kn, gcb, bb, tmat, uv, wk, du, dwk, dat, dqd, dke, dgl)


def _gdn_prep_bwd(dqn, dkn, dv, dg, dbeta, proj_a, conv_w, a_log, dt_bias, *, name, tm=256):
    S = proj_a.shape[0]
    nblk = S // tm
    W3 = 3 * GDN_WIDTH

    def body(dqn_ref, dkn_ref, dv_ref, dg_ref, dbeta_ref, cur_ref, prev_ref, ba_ref, cw_ref, al_ref, dt_ref,
             dc_ref, dba_ref, sm_ref):
        i = pl.program_id(0)
        prev = jnp.where(i > 0, prev_ref[...], 0.0)
        c = _conv_rows(prev, cur_ref[...], cw_ref[...], GDN_CONV)
        sg = _sigmoid(c)
        a = c * sg
        dsl = _dsilu(c, sg)
        ba = ba_ref[...]
        lane = lax.broadcasted_iota(jnp.int32, (tm, 128), 1)
        lane1 = lax.broadcasted_iota(jnp.int32, (1, 128), 1)
        dba = jnp.zeros((tm, 128), F32)
        sm = jnp.zeros((1, 128), F32)
        for h in range(GDN_HEADS):
            sl = slice(GDN_DIM * h, GDN_DIM * (h + 1))
            ks = slice(GDN_WIDTH + GDN_DIM * h, GDN_WIDTH + GDN_DIM * (h + 1))
            qh, kh = a[:, sl], a[:, ks]
            rq = lax.rsqrt(jnp.sum(qh * qh, axis=-1, keepdims=True) + EPS)
            rk = lax.rsqrt(jnp.sum(kh * kh, axis=-1, keepdims=True) + EPS)
            qhat, khat = qh * rq, kh * rk
            dyq = dqn_ref[:, sl] * (GDN_DIM ** -0.5)
            dyk = dkn_ref[:, sl]
            dq = rq * (dyq - qhat * jnp.sum(dyq * qhat, axis=-1, keepdims=True))
            dk = rk * (dyk - khat * jnp.sum(dyk * khat, axis=-1, keepdims=True))
            dc_ref[:, sl] = dq * dsl[:, sl]
            dc_ref[:, ks] = dk * dsl[:, ks]
            beta = _sigmoid(ba[:, h:h + 1])
            db = dbeta_ref[:, h:h + 1] * beta * (1.0 - beta)
            aneg = -jnp.exp(al_ref[0:1, h:h + 1])
            xa = ba[:, GDN_HEADS + h:GDN_HEADS + h + 1] + dt_ref[0:1, h:h + 1]
            dgh = dg_ref[:, h:h + 1]
            dxa = dgh * aneg * _sigmoid(xa)
            dba = jnp.where(lane == h, db, dba)
            dba = jnp.where(lane == GDN_HEADS + h, dxa, dba)
            d_alog = jnp.sum(dgh * _softplus(xa), axis=0, keepdims=True) * aneg
            sm = jnp.where(lane1 == h, d_alog, sm)
            sm = jnp.where(lane1 == GDN_HEADS + h, jnp.sum(dxa, axis=0, keepdims=True), sm)
        vs = slice(2 * GDN_WIDTH, W3)
        dc_ref[:, vs] = dv_ref[...] * dsl[:, vs]
        dba_ref[...] = dba

        @pl.when(i == 0)
        def _():
            sm_ref[...] = sm

        @pl.when(i > 0)
        def _():
            sm_ref[...] += sm

    prev_spec, _ = _halo_specs(tm, W3, 0, nblk)
    row = pl.BlockSpec((tm, GDN_WIDTH), lambda i: (i, 0))
    col = pl.BlockSpec((tm, 128), lambda i: (i, 0))
    small = lambda a: pl.BlockSpec(a.shape, lambda i: (0, 0))
    return pl.pallas_call(
        body, name=name, grid=(nblk,),
        in_specs=[row, row, row, col, col, pl.BlockSpec((tm, W3), lambda i: (i, 0)), prev_spec,
                  pl.BlockSpec((tm, 128), lambda i: (i, W3 // 128)), small(conv_w), small(a_log), small(dt_bias)],
        out_specs=[pl.BlockSpec((tm, W3), lambda i: (i, 0)), col, pl.BlockSpec((1, 128), lambda i: (0, 0))],
        out_shape=[_sds((S, W3)), _sds((S, 128)), _sds((1, 128))], compiler_params=_params(("arbitrary",)),
    )(dqn, dkn, dv, dg, dbeta, proj_a, proj_a, proj_a, conv_w, a_log, dt_bias)


def _gdn_conv_bwd(dc, dba, proj_a, conv_w, *, name, tm=256):
    S = proj_a.shape[0]
    nblk = S // tm
    W3 = 3 * GDN_WIDTH

    def body(dc_ref, dnext_ref, dba_ref, cur_ref, prev_ref, cw_ref, da_ref, dcw_ref):
        i = pl.program_id(0)
        prev = jnp.where(i > 0, prev_ref[...], 0.0)
        nxt = jnp.where(i < nblk - 1, dnext_ref[...], 0.0)
        dx, dw = _conv_rows_bwd(dc_ref[...], nxt, prev, cur_ref[...], cw_ref[...], GDN_CONV)
        da_ref[:, 0:W3] = dx.astype(BF16)
        da_ref[:, W3:] = dba_ref[...].astype(BF16)

        @pl.when(i == 0)
        def _():
            dcw_ref[...] = dw

        @pl.when(i > 0)
        def _():
            dcw_ref[...] += dw

    prev_spec, next_spec = _halo_specs(tm, W3, 0, nblk)
    wide = pl.BlockSpec((tm, W3), lambda i: (i, 0))
    return pl.pallas_call(
        body, name=name, grid=(nblk,),
        in_specs=[wide, next_spec, pl.BlockSpec((tm, 128), lambda i: (i, 0)), wide, prev_spec,
                  pl.BlockSpec(conv_w.shape, lambda i: (0, 0))],
        out_specs=[pl.BlockSpec((tm, A_COLS), lambda i: (i, 0)), pl.BlockSpec(conv_w.shape, lambda i: (0, 0))],
        out_shape=[_sds((S, A_COLS), BF16), _sds(conv_w.shape)], compiler_params=_params(("arbitrary",)),
    )(dc, dc, dba, proj_a, proj_a, conv_w)


def _band_mask(nk):
    i = lax.broadcasted_iota(jnp.int32, (2 * BAND, nk), 0) & (BAND - 1)
    j = lax.broadcasted_iota(jnp.int32, (2 * BAND, nk), 1)
    if nk == BAND:
        return j <= i
    return (j >= i) & (j <= i + BAND)


def _stack_heads(x, lo):
    return jnp.concatenate([jnp.where(lo, x, 0.0), jnp.where(lo, 0.0, x)], axis=0)


def _stack_cols(x):
    return jnp.concatenate([x[:, 0:1], x[:, DIL_DIM:DIL_DIM + 1]], axis=0)


def _unstack(x, lo):
    return jnp.where(lo, x[0:BAND], x[BAND:2 * BAND])


def _rows(start, size, stride):
    return pl.ds(start, size) if stride == 1 else pl.ds(start, size, stride=stride)


ATTN_LANES = 4


def _attn_blocks(S, visit_many, lanes=ATTN_LANES):
    for d in DILATIONS:
        nb = S // (d * BAND)
        if d == 1:
            half = nb // 2
            visit_many(d, [(0, 0, True), (0, half, False)])

            def pair(n, c):
                visit_many(1, [(0, n, False), (0, n + half, False)])
                return c
            lax.fori_loop(1, half, pair, 0)
        elif nb > 1:
            for r0 in range(0, d, lanes):
                visit_many(d, [(r0 + t, 0, True) for t in range(lanes)])

                def column(n, c, d=d, r0=r0):
                    visit_many(d, [(r0 + t, n, False) for t in range(lanes)])
                    return c
                lax.fori_loop(1, nb, column, 0)
        else:
            def group(g, c, d=d):
                visit_many(d, [(g * lanes + t, 0, True) for t in range(lanes)])
                return c
            lax.fori_loop(0, d // lanes, group, 0)


def _attn_fwd(proj_b, oab, *, name):
    S = proj_b.shape[0]
    scale = DIL_DIM ** -0.5

    def body(q_ref, k_ref, v_ref, oab_in_ref, ob_ref, lse_ref, m_ref, l_ref, acc_ref):
        del oab_in_ref
        lane = lax.broadcasted_iota(jnp.int32, (BAND, 128), 1)
        lo = lane < DIL_DIM
        m_ref[...] = jnp.full_like(m_ref, NEG_BIG)
        l_ref[...] = jnp.zeros_like(l_ref)
        acc_ref[...] = jnp.zeros_like(acc_ref)

        def load(d, r, n, first):
            nk = BAND if first else 2 * BAND
            qrows = _rows(r + n * (BAND * d), BAND, d)
            krows = _rows(r if first else r + (n - 1) * (BAND * d), nk, d)
            return dict(nk=nk, qrows=qrows, q=q_ref[qrows, :] * scale, k=k_ref[krows, :].astype(BF16),
                        v=v_ref[krows, :].astype(BF16), m=m_ref[qrows, :], l=l_ref[qrows, :], acc=acc_ref[qrows, :])

        def compute(b):
            q, k, v = b["q"], b["k"], b["v"]
            s = jnp.where(_band_mask(b["nk"]), _bdot_nt(_stack_heads(q, lo), k), NEG_BIG)
            m_old = _stack_cols(b["m"])
            m_new = jnp.maximum(m_old, jnp.max(s, axis=-1, keepdims=True))
            p = jnp.exp(s - m_new)
            alpha = _unstack(jnp.exp(m_old - m_new), lo)
            l_new = alpha * b["l"] + _unstack(jnp.sum(p, axis=-1, keepdims=True), lo)
            return _unstack(m_new, lo), l_new, alpha * b["acc"] + _unstack(_bdot(p, v), lo)

        def visit_many(d, blocks):
            loaded = [load(d, *blk) for blk in blocks]
            done = [compute(b) for b in loaded]
            for b, (m_new, l_new, acc_new) in zip(loaded, done):
                m_ref[b["qrows"], :] = m_new
                l_ref[b["qrows"], :] = l_new
                acc_ref[b["qrows"], :] = acc_new

        _attn_blocks(S, visit_many)
        ob_ref[...] = (acc_ref[...] / l_ref[...]).astype(BF16)
        lse_ref[...] = m_ref[...] + jnp.log(l_ref[...])

    part = lambda t: pl.BlockSpec((S, 128), lambda p: (0, 3 * p + t))
    return pl.pallas_call(
        body, name=name, grid=(DIL_PAIRS,),
        in_specs=[part(0), part(1), part(2), pl.BlockSpec(memory_space=pl.ANY)],
        out_specs=[pl.BlockSpec((S, 128), lambda p: (0, GDN_WIDTH // 128 + p)), pl.BlockSpec((S, 128), lambda p: (0, p))],
        out_shape=[_sds(oab.shape, BF16), _sds((S, DIL_WIDTH))],
        scratch_shapes=[pltpu.VMEM((S, 128), F32)] * 3, input_output_aliases={3: 0},
        compiler_params=_params(("parallel",)),
    )(proj_b, proj_b, proj_b, oab)


def _attn_bwd(proj_b, oab, d_oab, lse, *, name):
    S = proj_b.shape[0]
    scale = DIL_DIM ** -0.5

    def body(q_ref, k_ref, v_ref, o_ref, do_ref, lse_ref, dqkv_ref, dq_ref, dk_ref, dv_ref, delta_ref):
        lane = lax.broadcasted_iota(jnp.int32, (BAND, 128), 1)
        lo = lane < DIL_DIM
        dq_ref[...] = jnp.zeros_like(dq_ref)
        dk_ref[...] = jnp.zeros_like(dk_ref)
        dv_ref[...] = jnp.zeros_like(dv_ref)
        prod = do_ref[...] * o_ref[...].astype(F32)
        lo_all = lax.broadcasted_iota(jnp.int32, (S, 128), 1) < DIL_DIM
        delta_ref[...] = jnp.where(lo_all, jnp.sum(jnp.where(lo_all, prod, 0.0), axis=-1, keepdims=True),
                                   jnp.sum(jnp.where(lo_all, 0.0, prod), axis=-1, keepdims=True))

        def load(d, r, n, first):
            nk = BAND if first else 2 * BAND
            qrows = _rows(r + n * (BAND * d), BAND, d)
            krows = _rows(r if first else r + (n - 1) * (BAND * d), nk, d)
            return dict(nk=nk, qrows=qrows, krows=krows, q=q_ref[qrows, :] * scale, k=k_ref[krows, :], v=v_ref[krows, :],
                        do=do_ref[qrows, :], delta=delta_ref[qrows, :], lse=lse_ref[qrows, :],
                        dq=dq_ref[qrows, :], dk=dk_ref[krows, :], dv=dv_ref[krows, :])

        def compute(b):
            q, k, v, do = b["q"], b["k"], b["v"], b["do"]
            qs, dos = _stack_heads(q, lo), _stack_heads(do, lo)
            p = jnp.where(_band_mask(b["nk"]), jnp.exp(_bdot_nt(qs, k) - _stack_cols(b["lse"])), 0.0)
            ds = p * (_bdot_nt(dos, v) - _stack_cols(b["delta"]))
            dq = b["dq"] + _unstack(_bdot(ds, k), lo) * scale
            return dq, b["dk"] + _bdot_tn(ds, qs), b["dv"] + _bdot_tn(p, dos)

        def visit_many(d, blocks):
            loaded = [load(d, *blk) for blk in blocks]
            done = [compute(b) for b in loaded]
            for b, (dq, dk, dv) in zip(loaded, done):
                dq_ref[b["qrows"], :] = dq
                dk_ref[b["krows"], :] = dk
                dv_ref[b["krows"], :] = dv

        _attn_blocks(S, visit_many, lanes=2)
        dqkv_ref[:, 0:128] = dq_ref[...].astype(BF16)
        dqkv_ref[:, 128:256] = dk_ref[...].astype(BF16)
        dqkv_ref[:, 256:384] = dv_ref[...].astype(BF16)

    half = lambda p: (0, GDN_WIDTH // 128 + p)
    part = lambda t: pl.BlockSpec((S, 128), lambda p: (0, 3 * p + t))
    return pl.pallas_call(
        body, name=name, grid=(DIL_PAIRS,),
        in_specs=[part(0), part(1), part(2), pl.BlockSpec((S, 128), half), pl.BlockSpec((S, 128), half),
                  pl.BlockSpec((S, 128), lambda p: (0, p))],
        out_specs=pl.BlockSpec((S, 384), lambda p: (0, p)), out_shape=_sds((S, 3 * DIL_WIDTH), BF16),
        scratch_shapes=[pltpu.VMEM((S, 128), F32)] * 4, compiler_params=_params(("parallel",)),
    )(proj_b, proj_b, proj_b, oab, d_oab, lse)


FF_SLAB = 2 * D_FF // N_DEV
FF_PAIRS = N_DEV // 2
ROWS16 = 16


def _taps(w, x, base, n):
    out = _shifted(x, base, n) * w[0:1]
    for t in range(1, FFN_CONV):
        out = out + _shifted(x, base + t, n) * w[t:t + 1]
    return out


def _ffn_fwd(h2, x1, w_up, conv_w, w_down, final_w, tgt, *, name, tm=512):
    S, D = h2.shape
    ni = S // tm
    per = tm // ROWS16

    def body(h_ref, hp_ref, x1_ref, wg_ref, wu_ref, cg_ref, cu_ref, wd_ref, fw_ref, t_ref,
             dx_ref, dxb_ref, dfw_ref, loss_ref, ug_ref, uu_ref, x2_ref):
        i, j = pl.program_id(0), pl.program_id(1)
        hv = jnp.concatenate([hp_ref[...], h_ref[...]], axis=0)
        row = lax.broadcasted_iota(jnp.int32, (tm + ROWS16, 1), 0)
        keep = (i > 0) | (row >= ROWS16)

        def branch(w_ref, c_ref, u_ref):
            u = lax.dot_general(hv, w_ref[...], (((1,), (1,)), ((), ())), preferred_element_type=F32).astype(BF16)
            u_ref[...] = u[ROWS16:]
            return _taps(c_ref[...], jnp.where(keep, u.astype(F32), 0.0), ROWS16 - (FFN_CONV - 1), tm)

        gate = branch(wg_ref, cg_ref, ug_ref)
        up = branch(wu_ref, cu_ref, uu_ref)
        act = (gate * _sigmoid(gate) * up).astype(BF16)
        part = jnp.dot(act, wd_ref[...], preferred_element_type=F32)

        @pl.when(j == 0)
        def _():
            x2_ref[...] = x1_ref[...] + part

        @pl.when((j > 0) & (j < FF_PAIRS - 1))
        def _():
            x2_ref[...] += part

        @pl.when(j == FF_PAIRS - 1)
        def _():
            xv = x2_ref[...] + part
            wv = fw_ref[...]
            r = lax.rsqrt(jnp.mean(xv * xv, axis=-1, keepdims=True) + EPS)
            err = xv * r * wv - t_ref[...]
            lsum = jnp.sum(jnp.sum(err * err, axis=-1, keepdims=True), axis=0, keepdims=True) * (0.5 / D)
            g = err * (1.0 / D)
            xh = xv * r
            gw = g * wv
            dx = r * (gw - xh * jnp.mean(gw * xh, axis=-1, keepdims=True))
            dx_ref[...] = dx
            dxb_ref[...] = dx.astype(BF16)
            dfw = jnp.sum(g * xh, axis=0, keepdims=True)
            lpart = jnp.broadcast_to(lsum, (1, 128))

            @pl.when(i == 0)
            def _():
                dfw_ref[...] = dfw
                loss_ref[...] = lpart

            @pl.when(i > 0)
            def _():
                dfw_ref[...] += dfw
                loss_ref[...] += lpart

    rows = pl.BlockSpec((tm, D), lambda i, j: (i, 0))
    slab = lambda off: pl.BlockSpec((None, FF_SLAB, D), lambda i, j: (j + off, 0, 0))
    cslab = lambda off: pl.BlockSpec((None, FFN_CONV, FF_SLAB), lambda i, j: (j + off, 0, 0))
    uspec = pl.BlockSpec((None, tm, FF_SLAB), lambda i, j: (j, i, 0))
    return pl.pallas_call(
        body, name=name, grid=(ni, FF_PAIRS),
        in_specs=[rows, pl.BlockSpec((ROWS16, D), lambda i, j: (jnp.maximum(i * per - 1, 0), 0)), rows,
                  slab(0), slab(FF_PAIRS), cslab(0), cslab(FF_PAIRS), pl.BlockSpec((FF_SLAB, D), lambda i, j: (j, 0)),
                  pl.BlockSpec((1, D), lambda i, j: (0, 0)), rows],
        out_specs=[rows, rows, pl.BlockSpec((1, D), lambda i, j: (0, 0)), pl.BlockSpec((1, 128), lambda i, j: (0, 0)), uspec, uspec],
        out_shape=[_sds((S, D)), _sds((S, D), BF16), _sds((1, D)), _sds((1, 128)),
                   _sds((FF_PAIRS, S, FF_SLAB), BF16), _sds((FF_PAIRS, S, FF_SLAB), BF16)],
        scratch_shapes=[pltpu.VMEM((tm, D), F32)],
        compiler_params=_params(("arbitrary", "arbitrary")),
    )(h2, h2, x1, w_up, w_up, conv_w, conv_w, w_down, final_w, tgt)


def _ffn_bwd(dx2, h2, ug, uu, conv_w, w_down, *, name, tm=512):
    S, D = h2.shape
    ni = S // tm
    per = tm // ROWS16
    ext = tm + ROWS16

    def body(dx_ref, dxn_ref, h_ref, ug_ref, ugp_ref, ugn_ref, uu_ref, uup_ref, uun_ref, cg_ref, cu_ref, wd_ref,
             du_ref, gd_ref, gup_ref, dcw_ref, acc_d, acc_g, acc_u, acc_cg, acc_cu):
        i = pl.program_id(1)

        @pl.when(i == 0)
        def _():
            acc_d[...] = jnp.zeros_like(acc_d)
            acc_g[...] = jnp.zeros_like(acc_g)
            acc_u[...] = jnp.zeros_like(acc_u)
            acc_cg[...] = jnp.zeros_like(acc_cg)
            acc_cu[...] = jnp.zeros_like(acc_cu)

        dx = dx_ref[...]
        dxe = jnp.concatenate([dx, dxn_ref[...]], axis=0)
        row = lax.broadcasted_iota(jnp.int32, (ext, 1), 0)
        live = (i < ni - 1) | (row < tm)
        d_act = jnp.where(live, lax.dot_general(dxe, wd_ref[...], (((1,), (1,)), ((), ())), preferred_element_type=F32), 0.0)
        rowp = lax.broadcasted_iota(jnp.int32, (ext + ROWS16, 1), 0)
        keep = (i > 0) | (rowp >= ROWS16)

        def pre(cur, prev, nxt):
            return jnp.where(keep, jnp.concatenate([prev[...], cur[...], nxt[...]], axis=0).astype(F32), 0.0)

        uge, uue = pre(ug_ref, ugp_ref, ugn_ref), pre(uu_ref, uup_ref, uun_ref)
        cg, cu = cg_ref[...], cu_ref[...]
        base = ROWS16 - (FFN_CONV - 1)
        gate = _taps(cg, uge, base, ext)
        up = _taps(cu, uue, base, ext)
        sg = _sigmoid(gate)
        silu = gate * sg
        dgc = d_act * up * _dsilu(gate, sg)
        duc = d_act * silu

        def conv_t(w, dc):
            out = _shifted(dc, FFN_CONV - 1, tm) * w[0:1]
            for t in range(1, FFN_CONV):
                out = out + _shifted(dc, FFN_CONV - 1 - t, tm) * w[t:t + 1]
            return out.astype(BF16)

        du_g, du_u = conv_t(cg, dgc), conv_t(cu, duc)
        du_ref[0] = du_g
        du_ref[1] = du_u
        dcw = lambda dc, xe: jnp.concatenate(
            [jnp.sum(dc[0:tm] * _shifted(xe, base + t, tm), axis=0, keepdims=True) for t in range(FFN_CONV)], axis=0)
        acc_cg[0:FFN_CONV, :] += dcw(dgc, uge)
        acc_cu[0:FFN_CONV, :] += dcw(duc, uue)
        tn = (((0,), (0,)), ((), ()))
        act = (silu[0:tm] * up[0:tm]).astype(BF16)
        acc_d[...] += lax.dot_general(act, dx, tn, preferred_element_type=F32)
        hv = h_ref[...]
        acc_g[...] += lax.dot_general(du_g, hv, tn, preferred_element_type=F32)
        acc_u[...] += lax.dot_general(du_u, hv, tn, preferred_element_type=F32)

        @pl.when(i == ni - 1)
        def _():
            gd_ref[...] = acc_d[...].astype(BF16)
            gup_ref[0] = acc_g[...].astype(BF16)
            gup_ref[1] = acc_u[...].astype(BF16)
            dcw_ref[0] = acc_cg[0:FFN_CONV, :]
            dcw_ref[1] = acc_cu[0:FFN_CONV, :]

    last16 = S // ROWS16 - 1
    rows = pl.BlockSpec((tm, D), lambda j, i: (i, 0))
    rows_next = pl.BlockSpec((ROWS16, D), lambda j, i: (jnp.minimum((i + 1) * per, last16), 0))
    u_cur = pl.BlockSpec((None, tm, FF_SLAB), lambda j, i: (j, i, 0))
    u_prev = pl.BlockSpec((None, ROWS16, FF_SLAB), lambda j, i: (j, jnp.maximum(i * per - 1, 0), 0))
    u_next = pl.BlockSpec((None, ROWS16, FF_SLAB), lambda j, i: (j, jnp.minimum((i + 1) * per, last16), 0))
    cslab = lambda off: pl.BlockSpec((None, FFN_CONV, FF_SLAB), lambda j, i: (j + off, 0, 0))
    return pl.pallas_call(
        body, name=name, grid=(FF_PAIRS, ni),
        in_specs=[rows, rows_next, rows, u_cur, u_prev, u_next, u_cur, u_prev, u_next, cslab(0), cslab(FF_PAIRS),
                  pl.BlockSpec((FF_SLAB, D), lambda j, i: (j, 0))],
        out_specs=[pl.BlockSpec((None, 2, tm, FF_SLAB), lambda j, i: (j, 0, i, 0)), pl.BlockSpec((FF_SLAB, D), lambda j, i: (j, 0)),
                   pl.BlockSpec((None, 2, FF_SLAB, D), lambda j, i: (j, 0, 0, 0)),
                   pl.BlockSpec((None, 2, FFN_CONV, FF_SLAB), lambda j, i: (j, 0, 0, 0))],
        out_shape=[_sds((FF_PAIRS, 2, S, FF_SLAB), BF16), _sds((D_FF, D), BF16), _sds((FF_PAIRS, 2, FF_SLAB, D), BF16),
                   _sds((FF_PAIRS, 2, FFN_CONV, FF_SLAB))],
        scratch_shapes=[pltpu.VMEM((FF_SLAB, D), F32), pltpu.VMEM((FF_SLAB, D), F32), pltpu.VMEM((FF_SLAB, D), F32),
                        pltpu.VMEM((8, FF_SLAB), F32), pltpu.VMEM((8, FF_SLAB), F32)],
        compiler_params=_params(("parallel", "arbitrary")),
    )(dx2, dx2, h2, ug, ug, ug, uu, uu, uu, conv_w, conv_w, w_down)


def _pair_slot(p):
    return 2 * (p & (FF_PAIRS - 1)) + (p >> 2)


def _mm_slabs(a, w, *, name, res=None, norm_bwd=None, after=(), tm=1024, tn=1024):
    nk, S, _ = a.shape
    D = w.shape[2]
    has_res = res is not None
    has_norm = norm_bwd is not None
    assert not has_norm or tn == D

    def body(*refs):
        a_ref, w_ref = refs[:2]
        r_ref = refs[2] if has_res else None
        if has_norm:
            x_ref, nw_ref, skip_ref = refs[2 + has_res:5 + has_res]
            o_ref, dw_ref, acc_ref = refs[-3:]
        else:
            o_ref, acc_ref = refs[-2:]
        i, k = pl.program_id(0), pl.program_id(2)
        part = jnp.dot(a_ref[...], w_ref[...], preferred_element_type=F32)

        @pl.when(k == 0)
        def _():
            acc_ref[...] = part

        @pl.when(k > 0)
        def _():
            acc_ref[...] += part

        @pl.when(k == nk - 1)
        def _():
            r = acc_ref[...] + r_ref[...] if has_res else acc_ref[...]
            if has_norm:
                dx, dw = _rms_bwd_rows(r, x_ref[...], nw_ref[...])
                o_ref[...] = skip_ref[...] + dx

                @pl.when(i == 0)
                def _():
                    dw_ref[...] = dw

                @pl.when(i > 0)
                def _():
                    dw_ref[...] += dw
            else:
                o_ref[...] = r

    o_spec = pl.BlockSpec((tm, tn), lambda i, j, k: (i, j))
    one = pl.BlockSpec((1, tn), lambda i, j, k: (0, 0))
    return pl.pallas_call(
        body, name=name, grid=(S // tm, D // tn, nk),
        in_specs=[pl.BlockSpec((None, tm, FF_SLAB), lambda i, j, k: (k, i, 0)),
                  pl.BlockSpec((None, FF_SLAB, tn), lambda i, j, k: (FF_PAIRS * (k & 1) + (k >> 1), 0, j))] + [o_spec] * has_res
        + ([o_spec, one, o_spec] if has_norm else []) + [ANY] * len(after),
        out_specs=[o_spec, one] if has_norm else o_spec, out_shape=[_sds((S, D)), _sds((1, D))] if has_norm else _sds((S, D)),
        scratch_shapes=[pltpu.VMEM((tm, tn), F32)],
        compiler_params=_params(("arbitrary" if has_norm else "parallel", "parallel", "arbitrary")),
    )(*((a, w) + ((res,) if has_res else ()) + (tuple(norm_bwd) if has_norm else ()) + tuple(after)))


def _local_step(x, tgt, norm1_w, w_a, w_z, w_b, conv_a, a_log, dt_bias, gnw, norm2_w, final_w, late_weights, emit, start_after=()):
    wgrad = functools.partial(_mm, ta=True, out_dtype=BF16)
    h1, proj_a, proj_z, proj_b = _in_proj(x, norm1_w, w_a, w_z, w_b, after=start_after, name="in_proj")
    qn, kn, v, gcb, bb = _gdn_prep_fwd(proj_a, conv_a, a_log, dt_bias, name="gdn_prep_fwd")
    uv, wk, at, tmat, wkb, qdb, keb = _gdn_chunk_fwd(qn, kn, v, gcb, bb, name="gdn_chunk_fwd")
    o, u, sp, oab = _gdn_scan_fwd(uv, at, wkb, qdb, keb, gcb, proj_z, gnw, name="gdn_scan_fwd")
    oab, lse = _attn_fwd(proj_b, oab, name="attn_fwd")
    w_out, w_up, conv_f, w_down = late_weights(oab)
    x1, h2 = _out_proj_norm(oab, w_out, x, norm2_w, name="out_proj")
    dx2, dx2_b, d_final, loss, ug, uu = _ffn_fwd(h2, x1, w_up, conv_f, w_down, final_w, tgt, name="ffn_fwd")
    du, g_down, g_up, dcw = _ffn_bwd(dx2_b, h2, ug, uu, conv_f, w_down, name="ffn_bwd")
    token = emit("ffn", w_down=g_down, w_up=g_up.reshape(N_DEV, FF_SLAB, -1), conv_f=dcw.reshape(N_DEV, FFN_CONV, -1))
    dx1, d_norm2 = _mm_slabs(du.reshape(N_DEV, -1, FF_SLAB), w_up, norm_bwd=(x1, norm2_w, dx2), after=token, name="ffn_up_dx")
    d_oab = _mm(dx1, w_out, tb=True, name="out_proj_dx", tn=D_MODEL, tk=1024)
    token = emit("out", w_out=wgrad(oab, dx1, name="out_proj_dw", tm=D_MODEL, tn=D_MODEL))
    dz, d_gnw, du, dwk, dat, dqd, dke, dgl = _gdn_scan_bwd(d_oab, o, proj_z, gnw, sp, u, at, wkb, qdb, keb, gcb, after=token, name="gdn_scan_bwd")
    dqn, dkn, dv, dg, dbeta = _gdn_chunk_bwd(qn, kn, gcb, bb, tmat, uv, wk, du, dwk, dat, dqd, dke, dgl, name="gdn_chunk_bwd")
    dc, dba, d_small = _gdn_prep_bwd(dqn, dkn, dv, dg, dbeta, proj_a, conv_a, a_log, dt_bias, name="gdn_prep_bwd")
    d_pa, d_conv_a = _gdn_conv_bwd(dc, dba, proj_a, conv_a, name="gdn_conv_bwd")
    d_pb = _attn_bwd(proj_b, oab, d_oab, lse, name="attn_bwd")
    g_a = wgrad(d_pa, h1, name="proj_a_dw", tm=A_COLS, tn=D_MODEL)
    g_z = wgrad(dz, h1, name="proj_z_dw", tn=D_MODEL)
    g_b = wgrad(d_pb, h1, name="proj_b_dw", tm=768, tn=D_MODEL)
    token = emit("in", w_a=g_a, w_z=g_z, w_b=g_b, conv_a=d_conv_a)
    grad_x, d_norm1 = _in_proj_dx((d_pa, dz, d_pb), (w_a, w_z, w_b), x, norm1_w, dx1, after=token, name="in_proj_dx")
    small = dict(norm1=d_norm1, small=d_small, gnw=d_gnw, norm2=d_norm2, final=d_final)
    return loss, grad_x, small


_O1 = 3 * GDN_WIDTH
_O2 = _O1 + GDN_WIDTH
_O3 = _O2 + 2 * GDN_HEADS


def _split_w_in(w_t):
    d = w_t.shape[1]
    pad = jnp.zeros((A_COLS - _O1 - 2 * GDN_HEADS, d), w_t.dtype)
    w_a = jnp.concatenate([w_t[:_O1], w_t[_O2:_O3], pad], axis=0)
    w_b = w_t[_O3:].reshape(3, DIL_PAIRS, 128, d).transpose(1, 0, 2, 3).reshape(3 * DIL_WIDTH, d)
    return w_a, w_t[_O1:_O2], w_b


def _merge_g_in(g_a, g_z, g_b):
    d = g_a.shape[1]
    g_b = g_b.reshape(DIL_PAIRS, 3, 128, d).transpose(1, 0, 2, 3).reshape(3 * DIL_WIDTH, d)
    return jnp.concatenate([g_a[:_O1], g_z, g_a[_O1:_O1 + 2 * GDN_HEADS], g_b], axis=0)


MESH = pl.DeviceIdType.MESH
ANY = pl.BlockSpec(memory_space=pl.ANY)


def _position():
    return lax.axis_index("x"), lax.axis_index("y"), lax.axis_index("c")


def _slot(p):
    return 4 * p[0] + 2 * p[1] + p[2]


def _all_gather(blocks, *, name):
    n = len(blocks)

    def body(*refs):
        ins, outs = refs[:n], refs[n:2 * n]
        send_sems, recv_sems, local_sems = refs[2 * n:]
        x, y, c = _position()
        me, sibling = (x, y, c), (x, y, 1 - c)
        chips = [(1 - x, y), (x, 1 - y), (1 - x, 1 - y)]

        def copy(a, k, block, to, src=None):
            dst = outs[a].at[_slot(block)]
            return pltpu.make_async_remote_copy(
                src_ref=dst if src is None else src, dst_ref=dst, send_sem=send_sems.at[a, k], recv_sem=recv_sems.at[a, k],
                device_id=to, device_id_type=MESH)

        mine = [pltpu.make_async_copy(ins[a], outs[a].at[_slot(me)], local_sems.at[a]) for a in range(n)]
        for cp in mine:
            cp.start()
        first = []
        for a in range(n):
            first.append(copy(a, 0, me, sibling, src=ins[a]))
            first += [copy(a, 1 + j, me, (*chip, c), src=ins[a]) for j, chip in enumerate(chips)]
        for cp in first:
            cp.start()
        passed = []
        for j, chip in enumerate(chips):
            for a in range(n):
                copy(a, 1 + j, (*chip, c), me).wait_recv()
                fwd = copy(a, 4 + j, (*chip, c), sibling)
                fwd.start()
                passed.append(fwd)
        for a in range(n):
            copy(a, 0, sibling, me).wait_recv()
            for j, chip in enumerate(chips):
                copy(a, 4 + j, (*chip, 1 - c), me).wait_recv()
        for cp in first + passed:
            cp.wait_send()
        for cp in mine:
            cp.wait()

    return pl.pallas_call(
        body, name=name, in_specs=[ANY] * n, out_specs=[ANY] * n,
        out_shape=[_sds((N_DEV,) + b.shape, b.dtype) for b in blocks],
        scratch_shapes=[pltpu.SemaphoreType.DMA((n, 7)), pltpu.SemaphoreType.DMA((n, 7)), pltpu.SemaphoreType.DMA((n,))],
    )(*blocks)


def _gather_direct(block, *, name, after=()):
    def body(in_ref, *rest):
        out_ref, send_sems, recv_sems, local_sem = rest[len(after):]
        x, y, c = _position()
        me = _slot((x, y, c))
        mine = pltpu.make_async_copy(in_ref, out_ref.at[me], local_sem)
        mine.start()
        copies = [pltpu.make_async_remote_copy(
            src_ref=in_ref, dst_ref=out_ref.at[me], send_sem=send_sems.at[k - 1], recv_sem=recv_sems.at[k - 1],
            device_id=_peer_of(k, x, y, c), device_id_type=MESH) for k in range(1, N_DEV)]
        for cp in copies:
            cp.start()
        for cp in copies:
            cp.wait()
        mine.wait()

    return pl.pallas_call(
        body, name=name, in_specs=[pl.BlockSpec(memory_space=pltpu.VMEM)] + [ANY] * len(after),
        out_specs=pl.BlockSpec(memory_space=pltpu.VMEM),
        out_shape=_sds((N_DEV,) + block.shape, block.dtype),
        scratch_shapes=[pltpu.SemaphoreType.DMA((N_DEV - 1,)), pltpu.SemaphoreType.DMA((N_DEV - 1,)), pltpu.SemaphoreType.DMA],
    )(block, *after)


HBM = pl.BlockSpec(memory_space=pltpu.HBM)
SEM = pl.BlockSpec(memory_space=pltpu.SEMAPHORE)
EFFECT = pltpu.SideEffectType.DATAFLOW_SIDE_EFFECTING


def _peer_of(k, x, y, c):
    return (1 - x if k & 4 else x, 1 - y if k & 2 else y, 1 - c if k & 1 else c)


def _flight(a, k):
    return a * (N_DEV - 1) + k - 1


def _exchange_start(arrays, *, name, broadcast=False, paired=()):
    n = len(arrays)

    def body(*refs):
        ins, lands = refs[:n], refs[n:2 * n]
        send_sems, recv_sems = refs[2 * n:2 * n + 2]
        token = refs[-1]
        x, y, c = _position()
        me = _slot((x, y, c))
        for k in range(1, N_DEV):
            peer = _peer_of(k, x, y, c)
            for a in range(n):
                at = _pair_slot(_slot(peer)) if a in paired else _slot(peer)
                pltpu.make_async_remote_copy(
                    src_ref=ins[a] if broadcast else ins[a].at[at], dst_ref=lands[a].at[me],
                    send_sem=send_sems.at[_flight(a, k)], recv_sem=recv_sems.at[_flight(a, k)],
                    device_id=peer, device_id_type=MESH).start()
        token[...] = jnp.zeros_like(token)

    land_shapes = [((N_DEV,) + s.shape) if broadcast else s.shape for s in arrays]
    lands = [pltpu.with_memory_space_constraint(lax.empty(shp, s.dtype), pltpu.HBM) for shp, s in zip(land_shapes, arrays)]
    srcs = [pltpu.with_memory_space_constraint(s, pltpu.HBM) for s in arrays]
    outs = pl.pallas_call(
        body, name=name, in_specs=[HBM] * (2 * n),
        out_specs=[SEM, SEM] + [HBM] * (2 * n) + [pl.BlockSpec(memory_space=pltpu.VMEM)],
        out_shape=[pltpu.SemaphoreType.DMA((n * (N_DEV - 1),)), pltpu.SemaphoreType.DMA((n * (N_DEV - 1),))]
        + [pltpu.HBM(s.shape, s.dtype) for s in arrays] + [pltpu.HBM(shp, s.dtype) for shp, s in zip(land_shapes, arrays)]
        + [_sds((8, 128))],
        input_output_aliases={i: 2 + i for i in range(2 * n)},
        compiler_params=pltpu.CompilerParams(has_side_effects=EFFECT),
    )(*srcs, *lands)
    return outs[0], outs[1], outs[2:2 + n], outs[2 + n:2 + 2 * n], outs[-1]


def _exchange_wait(send_sems, recv_sems, srcs, lands, after, *, name, broadcast=False):
    n = len(srcs)

    def body(*refs):
        ins, lnd = refs[:n], refs[n:2 * n]
        send_ref, recv_ref = refs[2 * n:2 * n + 2]
        x, y, c = _position()
        for k in range(1, N_DEV):
            for a in range(n):
                cp = pltpu.make_async_remote_copy(
                    src_ref=ins[a] if broadcast else ins[a].at[0], dst_ref=lnd[a].at[0], send_sem=send_ref.at[_flight(a, k)],
                    recv_sem=recv_ref.at[_flight(a, k)], device_id=_peer_of(k, x, y, c), device_id_type=MESH)
                cp.wait_send()
                cp.wait_recv()

    outs = pl.pallas_call(
        body, name=name, in_specs=[HBM] * (2 * n) + [SEM, SEM, ANY], out_specs=[HBM] * (2 * n),
        out_shape=[pltpu.HBM(s.shape, s.dtype) for s in srcs] + [pltpu.HBM(s.shape, s.dtype) for s in lands],
        input_output_aliases={i: i for i in range(2 * n)},
        compiler_params=pltpu.CompilerParams(has_side_effects=EFFECT),
    )(*srcs, *lands, send_sems, recv_sems, after)
    return outs[:n], outs[n:]


def _with_own(landed, srcs, me):
    return [lax.dynamic_update_index_in_dim(l, o, me, 0) for l, o in zip(landed, srcs)]


def _adam_update(g, w, m, v):
    c1 = 1.0 - ADAM_B1 ** ADAM_STEP
    c2 = 1.0 - ADAM_B2 ** ADAM_STEP
    nm = ADAM_B1 * m + (1.0 - ADAM_B1) * g
    nv = ADAM_B2 * v + (1.0 - ADAM_B2) * (g * g)
    return -ADAM_LR * ((nm / c1) / (jnp.sqrt(nv / c2) + ADAM_EPS) + ADAM_WD * w), nm, nv


def _adamw(landed, sent, me, w, m, v, *, name, tr=None, tc=None):
    R, C = w.shape
    tr = R if tr is None else tr
    tc = C if tc is None else tc
    assert R % tr == 0 and C % tc == 0

    def body(me_ref, own_ref, p_ref, w_ref, m_ref, v_ref, g_ref, d_ref, nm_ref, nv_ref, token_ref):
        token_ref[...] = jnp.zeros_like(token_ref)
        g = own_ref[...].astype(F32)
        for s in range(N_DEV):
            g = g + jnp.where(me_ref[1] == s, 0.0, p_ref[s].astype(F32))
        delta, nm, nv = _adam_update(g, w_ref[...], m_ref[...], v_ref[...])
        g_ref[...] = g
        nm_ref[...] = nm
        nv_ref[...] = nv
        d_ref[...] = delta

    blk = pl.BlockSpec((tr, tc), lambda i, j, me_ref: (i, j))
    return pl.pallas_call(
        body, name=name,
        grid_spec=pltpu.PrefetchScalarGridSpec(
            num_scalar_prefetch=1, grid=(R // tr, C // tc),
            in_specs=[pl.BlockSpec((None, tr, tc), lambda i, j, me_ref: (me_ref[0], i, j)),
                      pl.BlockSpec((N_DEV, tr, tc), lambda i, j, me_ref: (0, i, j)), blk, blk, blk],
            out_specs=[blk] * 4 + [pl.BlockSpec((8, 128), lambda i, j, me_ref: (0, 0))]),
        out_shape=[_sds((R, C))] * 4 + [_sds((8, 128))],
        compiler_params=_params(("arbitrary", "arbitrary")),
    )(me, sent, landed, w, m, v)


_SMALL_ROWS = 8
_SMALL_SLOTS = ((0, 0, D_MODEL), (1, 0, D_MODEL), (2, 0, D_MODEL), (3, 0, GDN_DIM), (3, GDN_DIM, GDN_HEADS),
                (3, GDN_DIM + GDN_HEADS, GDN_HEADS))
_LOSS_LANE = 2 * GDN_DIM


def _pack_small(norm1, norm2, final, gnw, a_log, dt_bias, loss):
    row3 = jnp.concatenate([gnw, a_log, dt_bias, jnp.zeros((1, 128 - 2 * GDN_HEADS), F32), loss,
                            jnp.zeros((1, D_MODEL - 3 * 128), F32)], axis=1)
    return jnp.concatenate([norm1, norm2, final, row3, jnp.zeros((_SMALL_ROWS - 4, D_MODEL), F32)], axis=0)


def _adamw_small(packs, ws, ms, vs, *, name):
    n = len(ws)

    def body(p_ref, *refs):
        w_refs, m_refs, v_refs = refs[:n], refs[n:2 * n], refs[2 * n:3 * n]
        outs = refs[3 * n:]
        g_all = p_ref[0]
        for s in range(1, N_DEV):
            g_all = g_all + p_ref[s]
        for i, (row, lane, width) in enumerate(_SMALL_SLOTS):
            g = g_all[row:row + 1, lane:lane + width]
            delta, nm, nv = _adam_update(g, w_refs[i][...], m_refs[i][...], v_refs[i][...])
            for o_ref, val in zip(outs[4 * i:4 * i + 4], (g, delta, nm, nv)):
                o_ref[...] = val
        outs[-1][...] = g_all[3:4, _LOSS_LANE:_LOSS_LANE + 128]

    vm = pl.BlockSpec(memory_space=pltpu.VMEM)
    outs = pl.pallas_call(
        body, name=name, in_specs=[vm] * (1 + 3 * n), out_specs=[vm] * (4 * n + 1),
        out_shape=[_sds(w.shape) for w in ws for _ in range(4)] + [_sds((1, 128))],
    )(packs, *ws, *ms, *vs)
    return [outs[4 * i:4 * i + 4] for i in range(n)], outs[-1]


def _slabs_by_cols(g):
    r = g.shape[0]
    return g.reshape(r, N_DEV, -1).transpose(1, 0, 2)


def _cols_from_slabs(s):
    return s.transpose(1, 0, 2).reshape(s.shape[1], -1)


def kernel(x, norm1_w, w_in, conv_qkv_w, a_log, dt_bias, gdn_norm_w, w_out, norm2_w, w_up, ffn_conv_w, w_down, final_norm_w, loss_target, m_norm1_w, m_w_in, m_conv_qkv_w, m_a_log, m_dt_bias, m_gdn_norm_w, m_w_out, m_norm2_w, m_w_up, m_ffn_conv_w, m_w_down, m_final_norm_w, v_norm1_w, v_w_in, v_conv_qkv_w, v_a_log, v_dt_bias, v_gdn_norm_w, v_w_out, v_norm2_w, v_w_up, v_ffn_conv_w, v_w_down, v_final_norm_w):
    bf = lambda a: a.astype(BF16)
    me = _slot(_position())
    t_in = lambda a: a[0].T
    gw_in, g_conv_a = _all_gather([bf(t_in(w_in)), conv_qkv_w[0]], name="gather_w_in")
    w_a, w_z, w_b = _split_w_in(gw_in.reshape(-1, D_MODEL))
    late_src, _ = lax.optimization_barrier(([bf(w_out[0]), bf(t_in(w_up)), bf(w_down[0]), ffn_conv_w[0]], gw_in))
    l_send, l_recv, l_srcs, l_lands, l_token = _exchange_start(late_src, name="weights_start", broadcast=True)

    def late_weights(after):
        srcs, landed = _exchange_wait(l_send, l_recv, l_srcs, l_lands, after, name="weights_wait", broadcast=True)
        gw_out, gw_up, gw_down, g_conv_f = _with_own(landed, srcs, me)
        return gw_out.reshape(D_MODEL, D_MODEL), gw_up, g_conv_f, gw_down.reshape(D_FF, D_MODEL)

    flights = {}

    def emit(group, **grads):
        paired = ()
        if group == "in":
            slabs = dict(w_in=_merge_g_in(grads["w_a"], grads["w_z"], grads["w_b"]).reshape(N_DEV, -1, D_MODEL),
                         conv_a=_slabs_by_cols(grads["conv_a"]))
        elif group == "ffn":
            slabs = dict(w_down=grads["w_down"].reshape(N_DEV, -1, D_MODEL), w_up=grads["w_up"], conv_f=grads["conv_f"])
            paired = (1, 2)
        else:
            slabs = {k: v.reshape(N_DEV, -1, D_MODEL) for k, v in grads.items()}
        names = list(slabs)
        *flight, token = _exchange_start([slabs[k] for k in names], paired=paired, name="grads_start_" + group)
        flights[group] = (names, flight)
        return (token,)

    loss, grad_x, g = _local_step(
        x[0], loss_target[0], norm1_w, w_a, w_z, w_b, _cols_from_slabs(g_conv_a), a_log, dt_bias,
        gdn_norm_w, norm2_w, final_norm_w[None], late_weights, emit, start_after=(l_token,))
    got = {}
    me1 = jnp.reshape(me, (1,)).astype(jnp.int32)

    def collect(group, after):
        names, (send_sems, recv_sems, srcs, lands) = flights[group]
        srcs, landed = _exchange_wait(send_sems, recv_sems, srcs, lands, after, name="grads_wait_" + group)
        got.update(zip(names, zip(landed, srcs)))

    def update(key, w, m, v, paired=False, **tiles):
        where = jnp.concatenate([_pair_slot(me1) if paired else me1, me1])
        return _adamw(*got[key], where, w, m, v, name="adamw_" + key, **tiles)

    collect("ffn", grad_x)
    collect("out", grad_x)
    *o_out, t1 = update("w_out", w_out[0], m_w_out[0], v_w_out[0])
    *o_up, t2 = update("w_up", t_in(w_up), t_in(m_w_up), t_in(v_w_up), paired=True, tr=176)
    o_up = [o.T for o in o_up]
    *o_down, t3 = update("w_down", w_down[0], m_w_down[0], v_w_down[0], tr=176)
    *o_cf, t4 = update("conv_f", ffn_conv_w[0], m_ffn_conv_w[0], v_ffn_conv_w[0], paired=True)
    pack = _pack_small(g["norm1"], g["norm2"], g["final"], g["gnw"], g["small"][:, 0:GDN_HEADS],
                       g["small"][:, GDN_HEADS:2 * GDN_HEADS], loss)
    small_all = _gather_direct(pack, after=(t1, t2, t3, t4), name="gather_small")
    collect("in", small_all)
    o_in = [o.T for o in update("w_in", t_in(w_in), t_in(m_w_in), t_in(v_w_in), tc=256)[:4]]
    o_ca = update("conv_a", conv_qkv_w[0], m_conv_qkv_w[0], v_conv_qkv_w[0])
    (o_n1, o_n2, o_fin, o_gn, o_al, o_dt), total = _adamw_small(
        small_all, (norm1_w, norm2_w, final_norm_w[None], gdn_norm_w, a_log, dt_bias),
        (m_norm1_w, m_norm2_w, m_final_norm_w[None], m_gdn_norm_w, m_a_log, m_dt_bias),
        (v_norm1_w, v_norm2_w, v_final_norm_w[None], v_gdn_norm_w, v_a_log, v_dt_bias), name="adamw_small")
    outs = [total[0, 0], grad_x[None]]
    for k in range(4):
        outs += [o_n1[k], o_in[k][None], o_ca[k][None], o_al[k], o_dt[k], o_gn[k], o_out[k][None], o_n2[k], o_up[k][None],
                 o_cf[k][None], o_down[k][None], o_fin[k][0]]
    return tuple(outs)
```

```python
import functools

import jax
import jax.numpy as jnp
from jax import lax
from jax.experimental import pallas as pl
from jax.experimental.pallas import tpu as pltpu

F32 = jnp.float32
BF16 = jnp.bfloat16

N_DEV = 8
D_MODEL = 1024
GDN_HEADS = 4
GDN_DIM = 128
GDN_WIDTH = GDN_HEADS * GDN_DIM
GDN_CONV = 4
CHUNK = 64
CHUNKS_PER_STEP = 4
DIL_HEADS = 8
DIL_DIM = 64
DIL_WIDTH = DIL_HEADS * DIL_DIM
DIL_PAIRS = DIL_HEADS // 2
DILATIONS = (1, 4, 16)
BAND = 128
D_FF = 2816
FFN_CONV = 3
EPS = 1e-6
A_COLS = 3 * GDN_WIDTH + 128
HALO = 8

ADAM_LR = 0.001
ADAM_B1 = 0.9
ADAM_B2 = 0.999
ADAM_EPS = 1e-08
ADAM_WD = 0.01
ADAM_STEP = 10

VMEM_LIMIT_BYTES = 56 * 1024 * 1024
NEG_BIG = -1e30


def _params(sem=None):
    return pltpu.CompilerParams(dimension_semantics=sem, vmem_limit_bytes=VMEM_LIMIT_BYTES)


def _sds(shape, dtype=F32):
    return jax.ShapeDtypeStruct(shape, dtype)


def _bdot(a, b):
    return jnp.dot(a.astype(BF16), b.astype(BF16), preferred_element_type=F32)


def _bdot_nt(a, b):
    return lax.dot_general(a.astype(BF16), b.astype(BF16), (((1,), (1,)), ((), ())), preferred_element_type=F32)


def _bdot_tn(a, b):
    return lax.dot_general(a.astype(BF16), b.astype(BF16), (((0,), (0,)), ((), ())), preferred_element_type=F32)


def _split(a):
    hi = a.astype(BF16)
    lo = (a - hi.astype(F32)).astype(BF16)
    return hi, lo


def _dot3(a, b, dims):
    ah, al = _split(a)
    bh, bl = _split(b)
    d = functools.partial(lax.dot_general, dimension_numbers=(dims, ((), ())), preferred_element_type=F32)
    return d(ah, bh) + (d(al, bh) + d(ah, bl))


def _exact_tri_dot(tri, g):
    g1 = g.astype(BF16)
    r1 = g - g1.astype(F32)
    g2 = r1.astype(BF16)
    g3 = (r1 - g2.astype(F32)).astype(BF16)
    t = tri.astype(BF16)
    d = functools.partial(jnp.dot, preferred_element_type=F32)
    return d(t, g1) + (d(t, g2) + d(t, g3))


def _sigmoid(x):
    return 1.0 / (1.0 + jnp.exp(-x))


def _dsilu(x, sg):
    return sg * (1.0 + x * (1.0 - sg))


def _rms_bwd_rows(dh, x, w):
    r = lax.rsqrt(jnp.mean(x * x, axis=-1, keepdims=True) + EPS)
    xh = x * r
    gw = dh * w
    return r * (gw - xh * jnp.mean(gw * xh, axis=-1, keepdims=True)), jnp.sum(dh * xh, axis=0, keepdims=True)


def _mm(a, b, *, name, ta=False, tb=False, res=None, norm_bwd=None, after=(), out_dtype=F32, tm=512, tn=512, tk=512):
    if ta:
        K, M = a.shape
    else:
        M, K = a.shape
    if tb:
        N, Kb = b.shape
    else:
        Kb, N = b.shape
    assert K == Kb, (a.shape, b.shape)
    tm, tn, tk = min(tm, M), min(tn, N), min(tk, K)
    assert M % tm == 0 and N % tn == 0 and K % tk == 0, (name, M, N, K, tm, tn, tk)
    nk = K // tk
    dims = (((0 if ta else 1,), (1 if tb else 0,)), ((), ()))
    has_res = res is not None
    has_norm = norm_bwd is not None
    assert not has_norm or tn == N

    def body(*refs):
        a_ref, b_ref = refs[:2]
        r_ref = refs[2] if has_res else None
        if has_norm:
            x_ref, w_ref, skip_ref = refs[2 + has_res:5 + has_res]
            o_ref, dw_ref, acc_ref = refs[-3:]
        else:
            o_ref, acc_ref = refs[-2:]
        i, k = pl.program_id(0), pl.program_id(2)
        part = lax.dot_general(a_ref[...].astype(BF16), b_ref[...].astype(BF16), dims, preferred_element_type=F32)

        @pl.when(k == 0)
        def _():
            acc_ref[...] = part

        @pl.when(k > 0)
        def _():
            acc_ref[...] += part

        @pl.when(k == nk - 1)
        def _():
            r = acc_ref[...]
            if has_res:
                r = r + r_ref[...]
            if has_norm:
                dx, dw = _rms_bwd_rows(r, x_ref[...], w_ref[...])
                o_ref[...] = skip_ref[...] + dx

                @pl.when(i == 0)
                def _():
                    dw_ref[...] = dw

                @pl.when(i > 0)
                def _():
                    dw_ref[...] += dw
            else:
                o_ref[...] = r.astype(out_dtype)

    a_spec = pl.BlockSpec((tk, tm), lambda i, j, k: (k, i)) if ta else pl.BlockSpec((tm, tk), lambda i, j, k: (i, k))
    b_spec = pl.BlockSpec((tn, tk), lambda i, j, k: (j, k)) if tb else pl.BlockSpec((tk, tn), lambda i, j, k: (k, j))
    o_spec = pl.BlockSpec((tm, tn), lambda i, j, k: (i, j))
    one = pl.BlockSpec((1, tn), lambda i, j, k: (0, 0))
    in_specs = [a_spec, b_spec] + [o_spec] * has_res + ([o_spec, one, o_spec] if has_norm else []) + [ANY] * len(after)
    args = (a, b) + ((res,) if has_res else ()) + (tuple(norm_bwd) if has_norm else ()) + tuple(after)
    return pl.pallas_call(
        body, name=name, grid=(M // tm, N // tn, nk), in_specs=in_specs,
        out_specs=[o_spec, one] if has_norm else o_spec,
        out_shape=[_sds((M, N)), _sds((1, N))] if has_norm else _sds((M, N), out_dtype),
        scratch_shapes=[pltpu.VMEM((tm, tn), F32)],
        compiler_params=_params(("arbitrary" if has_norm else "parallel", "parallel", "arbitrary")),
    )(*args)


def _in_proj(x, norm_w, w_a, w_z, w_b, *, name, after=(), tm=512):
    S, D = x.shape
    ws = (w_a, w_z, w_b)

    def body(x_ref, nw_ref, wa_ref, wz_ref, wb_ref, *rest):
        h_ref, pa_ref, pz_ref, pb_ref = rest[len(after):]
        xv = x_ref[...]
        r = lax.rsqrt(jnp.mean(xv * xv, axis=-1, keepdims=True) + EPS)
        h = (xv * r * nw_ref[...]).astype(BF16)
        h_ref[...] = h
        for w_ref, p_ref in ((wa_ref, pa_ref), (wz_ref, pz_ref), (wb_ref, pb_ref)):
            p_ref[...] = lax.dot_general(h, w_ref[...], (((1,), (1,)), ((), ())), preferred_element_type=F32)

    row = lambda n: pl.BlockSpec((tm, n), lambda i: (i, 0))
    full = lambda a: pl.BlockSpec(a.shape, lambda i: (0, 0))
    return pl.pallas_call(
        body, name=name, grid=(S // tm,), in_specs=[row(D), full(norm_w)] + [full(w) for w in ws] + [ANY] * len(after),
        out_specs=[row(D)] + [row(w.shape[0]) for w in ws],
        out_shape=[_sds((S, D), BF16)] + [_sds((S, w.shape[0])) for w in ws], compiler_params=_params(("parallel",)),
    )(x, norm_w, *ws, *after)


def _in_proj_dx(ds, ws, x, norm_w, skip, *, name, after=(), tm=512):
    S, D = x.shape
    n = len(ds)

    def body(*refs):
        d_refs, w_refs = refs[:n], refs[n:2 * n]
        x_ref, nw_ref, skip_ref = refs[2 * n:2 * n + 3]
        o_ref, dw_ref = refs[-2:]
        i = pl.program_id(0)
        dh = jnp.dot(d_refs[0][...], w_refs[0][...], preferred_element_type=F32)
        for d_ref, w_ref in zip(d_refs[1:], w_refs[1:]):
            dh = dh + jnp.dot(d_ref[...], w_ref[...], preferred_element_type=F32)
        dx, dw = _rms_bwd_rows(dh, x_ref[...], nw_ref[...])
        o_ref[...] = skip_ref[...] + dx

        @pl.when(i == 0)
        def _():
            dw_ref[...] = dw

        @pl.when(i > 0)
        def _():
            dw_ref[...] += dw

    row = lambda c: pl.BlockSpec((tm, c), lambda i: (i, 0))
    full = lambda a: pl.BlockSpec(a.shape, lambda i: (0, 0))
    return pl.pallas_call(
        body, name=name, grid=(S // tm,),
        in_specs=[row(d.shape[1]) for d in ds] + [full(w) for w in ws] + [row(D), full(norm_w), row(D)] + [ANY] * len(after),
        out_specs=[row(D), pl.BlockSpec((1, D), lambda i: (0, 0))], out_shape=[_sds((S, D)), _sds((1, D))],
        compiler_params=_params(("arbitrary",)),
    )(*ds, *ws, x, norm_w, skip, *after)


def _out_proj_norm(a, w, x, norm_w, *, name, tm=512):
    S, D = x.shape

    def body(a_ref, w_ref, x_ref, nw_ref, x1_ref, h_ref):
        x1 = x_ref[...] + jnp.dot(a_ref[...], w_ref[...], preferred_element_type=F32)
        x1_ref[...] = x1
        r = lax.rsqrt(jnp.mean(x1 * x1, axis=-1, keepdims=True) + EPS)
        h_ref[...] = (x1 * r * nw_ref[...]).astype(BF16)

    row = pl.BlockSpec((tm, D), lambda i: (i, 0))
    return pl.pallas_call(
        body, name=name, grid=(S // tm,),
        in_specs=[pl.BlockSpec((tm, a.shape[1]), lambda i: (i, 0)), pl.BlockSpec(w.shape, lambda i: (0, 0)), row,
                  pl.BlockSpec((1, D), lambda i: (0, 0))],
        out_specs=[row, row], out_shape=[_sds((S, D)), _sds((S, D), BF16)], compiler_params=_params(("parallel",)),
    )(a, w, x, norm_w)


def _shifted(x, start, n):
    aligned = -(-start // HALO) * HALO
    assert aligned + n <= x.shape[0], (start, n, x.shape)
    return (x if aligned == start else pltpu.roll(x, aligned - start, axis=0))[aligned:aligned + n]


def _conv_rows(prev, cur, w, taps):
    n = cur.shape[0]
    xs = jnp.concatenate([prev, cur], axis=0)
    base = HALO - (taps - 1)
    out = _shifted(xs, base, n) * w[0:1]
    for i in range(1, taps):
        out = out + _shifted(xs, base + i, n) * w[i:i + 1]
    return out


def _conv_rows_bwd(cur_d, next_d, prev_x, cur_x, w, taps):
    n = cur_d.shape[0]
    ds = jnp.concatenate([cur_d, next_d], axis=0)
    dx = _shifted(ds, taps - 1, n) * w[0:1]
    for i in range(1, taps):
        dx = dx + _shifted(ds, taps - 1 - i, n) * w[i:i + 1]
    xs = jnp.concatenate([prev_x, cur_x], axis=0)
    base = HALO - (taps - 1)
    dws = [jnp.sum(cur_d * _shifted(xs, base + i, n), axis=0, keepdims=True) for i in range(taps)]
    return dx, jnp.concatenate(dws, axis=0)


def _halo_specs(tm, width, col, nblk):
    per = tm // HALO
    prev = pl.BlockSpec((HALO, width), lambda i, *_: (jnp.maximum(i * per - 1, 0), col))
    nxt = pl.BlockSpec((HALO, width), lambda i, *_: (jnp.minimum((i + 1) * per, nblk * per - 1), col))
    return prev, nxt


def _softplus(x):
    return jnp.maximum(x, 0.0) + jnp.log1p(jnp.exp(-jnp.abs(x)))


def _chunk_tri(tm, upper=False):
    r = lax.broadcasted_iota(jnp.int32, (tm, tm), 0)
    c = lax.broadcasted_iota(jnp.int32, (tm, tm), 1)
    same = lax.div(r, CHUNK) == lax.div(c, CHUNK)
    order = (c >= r) if upper else (c <= r)
    return jnp.where(same & order, 1.0, 0.0)


def _gdn_prep_fwd(proj_a, conv_w, a_log, dt_bias, *, name, tm=256):
    S = proj_a.shape[0]
    nblk = S // tm
    W3 = 3 * GDN_WIDTH

    def body(cur_ref, prev_ref, ba_ref, cw_ref, al_ref, dt_ref, qn_ref, kn_ref, v_ref, gcb_ref, bb_ref):
        i = pl.program_id(0)
        prev = jnp.where(i > 0, prev_ref[...], 0.0)
        c = _conv_rows(prev, cur_ref[...], cw_ref[...], GDN_CONV)
        a = c * _sigmoid(c)
        ba = ba_ref[...]
        lane = lax.broadcasted_iota(jnp.int32, (tm, 128), 1)
        g4 = jnp.zeros((tm, 128), F32)
        for h in range(GDN_HEADS):
            sl = slice(GDN_DIM * h, GDN_DIM * (h + 1))
            qh = a[:, GDN_DIM * h:GDN_DIM * (h + 1)]
            kh = a[:, GDN_WIDTH + GDN_DIM * h:GDN_WIDTH + GDN_DIM * (h + 1)]
            qn_ref[:, sl] = qh * (lax.rsqrt(jnp.sum(qh * qh, axis=-1, keepdims=True) + EPS) * (GDN_DIM ** -0.5))
            kn_ref[:, sl] = kh * lax.rsqrt(jnp.sum(kh * kh, axis=-1, keepdims=True) + EPS)
            beta = _sigmoid(ba[:, h:h + 1])
            bb_ref[:, sl] = jnp.broadcast_to(beta, (tm, GDN_DIM))
            g = -jnp.exp(al_ref[0:1, h:h + 1]) * _softplus(ba[:, GDN_HEADS + h:GDN_HEADS + h + 1] + dt_ref[0:1, h:h + 1])
            g4 = jnp.where(lane == h, g, g4)
        v_ref[...] = a[:, 2 * GDN_WIDTH:]
        gc = _exact_tri_dot(_chunk_tri(tm), g4)
        for h in range(GDN_HEADS):
            gcb_ref[:, GDN_DIM * h:GDN_DIM * (h + 1)] = jnp.broadcast_to(gc[:, h:h + 1], (tm, GDN_DIM))

    prev_spec, _ = _halo_specs(tm, W3, 0, nblk)
    row = pl.BlockSpec((tm, GDN_WIDTH), lambda i: (i, 0))
    small = lambda a: pl.BlockSpec(a.shape, lambda i: (0, 0))
    return pl.pallas_call(
        body, name=name, grid=(nblk,),
        in_specs=[pl.BlockSpec((tm, W3), lambda i: (i, 0)), prev_spec,
                  pl.BlockSpec((tm, 128), lambda i: (i, W3 // 128)), small(conv_w), small(a_log), small(dt_bias)],
        out_specs=[row] * 5, out_shape=[_sds((S, GDN_WIDTH))] * 5, compiler_params=_params(("parallel",)),
    )(proj_a, proj_a, proj_a, conv_w, a_log, dt_bias)


GDN_STACK = GDN_HEADS * CHUNK


def _stack(ref, rows):
    return jnp.concatenate([ref[rows, GDN_DIM * h:GDN_DIM * (h + 1)] for h in range(GDN_HEADS)], axis=0)


def _unstack_to(ref, rows, x):
    for h in range(GDN_HEADS):
        ref[rows, GDN_DIM * h:GDN_DIM * (h + 1)] = x[CHUNK * h:CHUNK * (h + 1)].astype(ref.dtype)


def _stack_masks():
    r = lax.broadcasted_iota(jnp.int32, (GDN_STACK, GDN_STACK), 0)
    c = lax.broadcasted_iota(jnp.int32, (GDN_STACK, GDN_STACK), 1)
    same = (r & -CHUNK) == (c & -CHUNK)
    return same & (r >= c), same & (r > c), r == c


def _stack_decay(gs, bs, incl):
    g2 = jnp.concatenate([gs, gs], axis=1)
    diff = g2 - g2.T
    dec = jnp.where(incl, jnp.exp(jnp.where(incl, diff, 0.0)), 0.0)
    return dec, jnp.concatenate([bs, bs], axis=1).T


def _head_mask():
    r = lax.broadcasted_iota(jnp.int32, (GDN_STACK, GDN_WIDTH), 0)
    c = lax.broadcasted_iota(jnp.int32, (GDN_STACK, GDN_WIDTH), 1)
    return (r & -CHUNK) * (GDN_DIM // CHUNK) == (c & -GDN_DIM)


def _head_spread(x):
    return jnp.where(_head_mask(), jnp.concatenate([x] * GDN_HEADS, axis=1), 0.0)


def _head_diag(x):
    xm = jnp.where(_head_mask(), x, 0.0)
    out = xm[:, 0:GDN_DIM]
    for h in range(1, GDN_HEADS):
        out = out + xm[:, GDN_DIM * h:GDN_DIM * (h + 1)]
    return out


def _last_rows(gs, n):
    return jnp.concatenate([jnp.broadcast_to(gs[CHUNK * (h + 1) - 1:CHUNK * (h + 1)], (n, GDN_DIM)) for h in range(GDN_HEADS)], axis=0)


def _gdn_chunk_fwd(qn, kn, v, gcb, bb, *, name):
    S = qn.shape[0]

    def body(qn_ref, kn_ref, v_ref, gcb_ref, bb_ref, uv_ref, wk_ref, at_ref, t_ref, wkb_ref, qdb_ref, keb_ref):
        incl, strict, diag = _stack_masks()
        for c in range(CHUNKS_PER_STEP):
            rows = slice(CHUNK * c, CHUNK * (c + 1))
            srows = slice(GDN_STACK * c, GDN_STACK * (c + 1))
            q, k, vv, gs, bs = [_stack(r, rows) for r in (qn_ref, kn_ref, v_ref, gcb_ref, bb_ref)]
            dec, bt = _stack_decay(gs, bs, incl)
            p = -jnp.where(strict, dec * _bdot_nt(k, k) * bt, 0.0)
            t = jnp.where(diag, 1.0, 0.0) + p
            for _ in range(5):
                p = _bdot(p, p)
                t = t + _bdot(t, p)
            sol = _dot3(t, jnp.concatenate([vv, jnp.exp(gs) * k], axis=1), ((1,), (0,)))
            _unstack_to(uv_ref, rows, sol[:, :GDN_DIM])
            _unstack_to(wk_ref, rows, sol[:, GDN_DIM:])
            at_ref[srows, :] = dec * _bdot_nt(q, k) * bt
            t_ref[srows, :] = t
            wkb_ref[srows, :] = _head_spread(sol[:, GDN_DIM:]).astype(BF16)
            qdb_ref[srows, :] = _head_spread(q * jnp.exp(gs)).astype(BF16)
            keb_ref[srows, :] = _head_spread(k * jnp.exp(_last_rows(gs, CHUNK) - gs) * bs).astype(BF16)

    step = CHUNKS_PER_STEP * CHUNK
    row = pl.BlockSpec((step, GDN_WIDTH), lambda n: (n, 0))
    sq = pl.BlockSpec((CHUNKS_PER_STEP * GDN_STACK, GDN_STACK), lambda n: (n, 0))
    wide = pl.BlockSpec((CHUNKS_PER_STEP * GDN_STACK, GDN_WIDTH), lambda n: (n, 0))
    nsq = S // CHUNK * GDN_STACK
    return pl.pallas_call(
        body, name=name, grid=(S // step,), in_specs=[row] * 5, out_specs=[row, row, sq, sq, wide, wide, wide],
        out_shape=[_sds((S, GDN_WIDTH)), _sds((S, GDN_WIDTH)), _sds((nsq, GDN_STACK)), _sds((nsq, GDN_STACK))]
        + [_sds((nsq, GDN_WIDTH), BF16)] * 3,
        compiler_params=_params(("parallel",)),
    )(qn, kn, v, gcb, bb)


SCAN_CHUNKS = 8


def _gdn_scan_fwd(uv, at, wkb, qdb, keb, gcb, proj_z, gnw, *, name):
    S = uv.shape[0]
    nc = S // CHUNK

    def body(uv_ref, at_ref, wkb_ref, qdb_ref, keb_ref, gcb_ref, z_ref, gnw_ref, o_ref, u_ref, sp_ref, oa_ref, st_ref):
        n = pl.program_id(0)

        @pl.when(n == 0)
        def _():
            st_ref[...] = jnp.zeros_like(st_ref)

        for c in range(SCAN_CHUNKS):
            rows = slice(CHUNK * c, CHUNK * (c + 1))
            srows = slice(GDN_STACK * c, GDN_STACK * (c + 1))
            st = st_ref[...]
            sp_ref[GDN_WIDTH * c:GDN_WIDTH * (c + 1), :] = st
            uv, gs, z = [_stack(r, rows) for r in (uv_ref, gcb_ref, z_ref)]
            u = uv - _bdot(wkb_ref[srows, :], st)
            o = _bdot(qdb_ref[srows, :], st) + _bdot(at_ref[srows, :], u)
            st_ref[...] = jnp.exp(_last_rows(gs, GDN_DIM)) * st + _bdot_tn(keb_ref[srows, :], u)
            _unstack_to(u_ref, rows, u)
            _unstack_to(o_ref, rows, o)
            r = lax.rsqrt(jnp.mean(o * o, axis=-1, keepdims=True) + EPS)
            oa = o * r * gnw_ref[...] * (z * _sigmoid(z))
            oa_ref[rows, :] = jnp.concatenate([oa[CHUNK * h:CHUNK * (h + 1)] for h in range(GDN_HEADS)], axis=1).astype(BF16)

    row = pl.BlockSpec((SCAN_CHUNKS * CHUNK, GDN_WIDTH), lambda n: (n, 0))
    sq = pl.BlockSpec((SCAN_CHUNKS * GDN_STACK, GDN_STACK), lambda n: (n, 0))
    wide = pl.BlockSpec((SCAN_CHUNKS * GDN_STACK, GDN_WIDTH), lambda n: (n, 0))
    return pl.pallas_call(
        body, name=name, grid=(nc // SCAN_CHUNKS,),
        in_specs=[row, sq, wide, wide, wide, row, row, pl.BlockSpec((1, GDN_DIM), lambda n: (0, 0))],
        out_specs=[row, row, pl.BlockSpec((SCAN_CHUNKS * GDN_WIDTH, GDN_DIM), lambda n: (n, 0)), row],
        out_shape=[_sds((S, GDN_WIDTH)), _sds((S, GDN_WIDTH)), _sds((nc * GDN_WIDTH, GDN_DIM)), _sds((S, 2 * GDN_WIDTH), BF16)],
        scratch_shapes=[pltpu.VMEM((GDN_WIDTH, GDN_DIM), F32)],
        compiler_params=_params(("arbitrary",)),
    )(uv, at, wkb, qdb, keb, gcb, proj_z, gnw)


def _gdn_scan_bwd(d_oab, o, proj_z, gnw, sp, u, at, wkb, qdb, keb, gcb, *, name, after=()):
    S = o.shape[0]
    nc = S // CHUNK
    ns = nc // SCAN_CHUNKS

    def body(do_ref, o_ref, z_ref, gnw_ref, sp_ref, u_ref, at_ref, wkb_ref, qdb_ref, keb_ref, gcb_ref, *rest):
        dz_ref, dgn_ref, du_ref, dwk_ref, dat_ref, dqd_ref, dke_ref, dgl_ref, ds_ref = rest[len(after):]
        n = pl.program_id(0)

        @pl.when(n == 0)
        def _():
            ds_ref[...] = jnp.zeros_like(ds_ref)
            dgn_ref[...] = jnp.zeros_like(dgn_ref)

        gw = gnw_ref[...]
        for c in reversed(range(SCAN_CHUNKS)):
            rows = slice(CHUNK * c, CHUNK * (c + 1))
            srows = slice(GDN_STACK * c, GDN_STACK * (c + 1))
            d_oa, oo, z, uu, gs = [_stack(r, rows) for r in (do_ref, o_ref, z_ref, u_ref, gcb_ref)]
            sg = _sigmoid(z)
            r = lax.rsqrt(jnp.mean(oo * oo, axis=-1, keepdims=True) + EPS)
            xh = oo * r
            dy = d_oa * (z * sg)
            _unstack_to(dz_ref, rows, d_oa * (xh * gw) * _dsilu(z, sg))
            dgn_ref[...] += jnp.sum(dy * xh, axis=0, keepdims=True)
            dxh = dy * gw
            do = r * (dxh - xh * jnp.mean(dxh * xh, axis=-1, keepdims=True))

            st = sp_ref[GDN_WIDTH * c:GDN_WIDTH * (c + 1), :]
            dst = ds_ref[...]
            ge = jnp.exp(_last_rows(gs, GDN_DIM))
            _unstack_to(dqd_ref, rows, _head_diag(_bdot_nt(do, st)))
            dat_ref[srows, :] = _bdot_nt(do, uu)
            du = _bdot_tn(at_ref[srows, :], do) + _bdot(keb_ref[srows, :], dst)
            _unstack_to(dke_ref, rows, _head_diag(_bdot_nt(uu, dst)))
            prod = dst * st
            for h in range(GDN_HEADS):
                blk = prod[GDN_DIM * h:GDN_DIM * (h + 1)]
                dge = jnp.sum(jnp.sum(blk, axis=1, keepdims=True), axis=0, keepdims=True)
                dgl_ref[c, :, GDN_DIM * h:GDN_DIM * (h + 1)] = jnp.broadcast_to(dge * ge[GDN_DIM * h:GDN_DIM * h + 1], (8, GDN_DIM))
            ds_ref[...] = _bdot_tn(qdb_ref[srows, :], do) + ge * dst - _bdot_tn(wkb_ref[srows, :], du)
            _unstack_to(du_ref, rows, du)
            _unstack_to(dwk_ref, rows, -_head_diag(_bdot_nt(du, st)))

    rev = lambda n: (ns - 1 - n, 0)
    row = pl.BlockSpec((SCAN_CHUNKS * CHUNK, GDN_WIDTH), rev)
    sq = pl.BlockSpec((SCAN_CHUNKS * GDN_STACK, GDN_STACK), rev)
    wide = pl.BlockSpec((SCAN_CHUNKS * GDN_STACK, GDN_WIDTH), rev)
    one = pl.BlockSpec((1, GDN_DIM), lambda n: (0, 0))
    return pl.pallas_call(
        body, name=name, grid=(ns,),
        in_specs=[row, row, row, one, pl.BlockSpec((SCAN_CHUNKS * GDN_WIDTH, GDN_DIM), rev), row, sq, wide, wide, wide, row]
        + [ANY] * len(after),
        out_specs=[row, one, row, row, sq, row, row, pl.BlockSpec((SCAN_CHUNKS, 8, GDN_WIDTH), lambda n: (ns - 1 - n, 0, 0))],
        out_shape=[_sds((S, GDN_WIDTH), BF16), _sds((1, GDN_DIM)), _sds((S, GDN_WIDTH)), _sds((S, GDN_WIDTH)),
                   _sds((nc * GDN_STACK, GDN_STACK)), _sds((S, GDN_WIDTH)), _sds((S, GDN_WIDTH)), _sds((nc, 8, GDN_WIDTH))],
        scratch_shapes=[pltpu.VMEM((GDN_WIDTH, GDN_DIM), F32)],
        compiler_params=_params(("arbitrary",)),
    )(d_oab, o, proj_z, gnw, sp, u, at, wkb, qdb, keb, gcb, *after)


def _gdn_chunk_bwd(qn, kn, gcb, bb, tmat, uv, wk, du, dwk, dat, dqd, dke, dgl, *, name):
    S = qn.shape[0]

    def body(qn_ref, kn_ref, gcb_ref, bb_ref, t_ref, uv_ref, wk_ref, du_ref, dwk_ref, dat_ref, dqd_ref, dke_ref,
             dgl_ref, dq_ref, dk_ref, dv_ref, dg_ref, dbeta_ref):
        incl, strict, _ = _stack_masks()
        lane = lax.broadcasted_iota(jnp.int32, (CHUNK, 128), 1)
        rowi = lax.broadcasted_iota(jnp.int32, (CHUNK, 1), 0)
        rsum = lambda x: jnp.sum(x, axis=-1, keepdims=True)
        for c in range(CHUNKS_PER_STEP):
            rows = slice(CHUNK * c, CHUNK * (c + 1))
            srows = slice(GDN_STACK * c, GDN_STACK * (c + 1))
            q, k, gs, bs, uv, wk, du, dwk, dqd, dke = [
                _stack(r, rows) for r in (qn_ref, kn_ref, gcb_ref, bb_ref, uv_ref, wk_ref, du_ref, dwk_ref, dqd_ref, dke_ref)]
            dec, bt = _stack_decay(gs, bs, incl)
            kk = _bdot_nt(k, k)
            qk = _bdot_nt(q, k)
            d_rhs = _dot3(t_ref[srows, :], jnp.concatenate([du, dwk], axis=1), ((0,), (0,)))
            sol = jnp.concatenate([uv, wk], axis=1)
            d_l = jnp.where(strict, -_dot3(d_rhs, sol, ((1,), (1,))), 0.0)
            d_a = jnp.where(incl, dat_ref[srows, :], 0.0)
            gam = jnp.exp(gs)
            e = jnp.exp(_last_rows(gs, CHUNK) - gs)
            d_gk = d_rhs[:, GDN_DIM:]
            ml = d_l * dec * bt
            ma = d_a * dec * bt
            _unstack_to(dq_ref, rows, _bdot(ma, k) + dqd * gam)
            _unstack_to(dk_ref, rows, _bdot(ml + ml.T, k) + _bdot_tn(ma, q) + d_gk * gam + dke * (e * bs))
            _unstack_to(dv_ref, rows, d_rhs[:, :GDN_DIM])
            wb = d_l * dec * kk + d_a * dec * qk
            ew = wb * bt
            s_ke = rsum(dke * k * (e * bs))
            dbeta = rsum(wb.T) + rsum(dke * k * e)
            dgc = rsum(ew) - rsum(ew.T) + rsum(dqd * q * gam) + rsum(d_gk * k * gam) - s_ke
            dgc4 = jnp.zeros((CHUNK, 128), F32)
            db4 = jnp.zeros((CHUNK, 128), F32)
            for h in range(GDN_HEADS):
                hr = slice(CHUNK * h, CHUNK * (h + 1))
                tail = jnp.sum(s_ke[hr], axis=0, keepdims=True) + dgl_ref[c, 0:1, GDN_DIM * h:GDN_DIM * h + 1]
                dgc4 = jnp.where(lane == h, dgc[hr] + jnp.where(rowi == CHUNK - 1, tail, 0.0), dgc4)
                db4 = jnp.where(lane == h, dbeta[hr], db4)
            dg_ref[rows, :] = _exact_tri_dot(_chunk_tri(CHUNK, upper=True), dgc4)
            dbeta_ref[rows, :] = db4

    step = CHUNKS_PER_STEP * CHUNK
    row = pl.BlockSpec((step, GDN_WIDTH), lambda n: (n, 0))
    sq = pl.BlockSpec((CHUNKS_PER_STEP * GDN_STACK, GDN_STACK), lambda n: (n, 0))
    col = pl.BlockSpec((step, 128), lambda n: (n, 0))
    return pl.pallas_call(
        body, name=name, grid=(S // step,),
        in_specs=[row] * 4 + [sq, row, row, row, row, sq, row, row,
                              pl.BlockSpec((CHUNKS_PER_STEP, 8, GDN_WIDTH), lambda n: (n, 0, 0))],
        out_specs=[row, row, row, col, col],
        out_shape=[_sds((S, GDN_WIDTH))] * 3 + [_sds((S, 128))] * 2, compiler_params=_params(("parallel",)),
    )(qn, kn, gcb, bb, tmat, uv, wk, du, dwk, dat, dqd, dke, dgl)


def _gdn_prep_bwd(dqn, dkn, dv, dg, dbeta, proj_a, conv_w, a_log, dt_bias, *, name, tm=256):
    S = proj_a.shape[0]
    nblk = S // tm
    W3 = 3 * GDN_WIDTH

    def body(dqn_ref, dkn_ref, dv_ref, dg_ref, dbeta_ref, cur_ref, prev_ref, ba_ref, cw_ref, al_ref, dt_ref,
             dc_ref, dba_ref, sm_ref):
        i = pl.program_id(0)
        prev = jnp.where(i > 0, prev_ref[...], 0.0)
        c = _conv_rows(prev, cur_ref[...], cw_ref[...], GDN_CONV)
        sg = _sigmoid(c)
        a = c * sg
        dsl = _dsilu(c, sg)
        ba = ba_ref[...]
        lane = lax.broadcasted_iota(jnp.int32, (tm, 128), 1)
        lane1 = lax.broadcasted_iota(jnp.int32, (1, 128), 1)
        dba = jnp.zeros((tm, 128), F32)
        sm = jnp.zeros((1, 128), F32)
        for h in range(GDN_HEADS):
            sl = slice(GDN_DIM * h, GDN_DIM * (h + 1))
            ks = slice(GDN_WIDTH + GDN_DIM * h, GDN_WIDTH + GDN_DIM * (h + 1))
            qh, kh = a[:, sl], a[:, ks]
            rq = lax.rsqrt(jnp.sum(qh * qh, axis=-1, keepdims=True) + EPS)
            rk = lax.rsqrt(jnp.sum(kh * kh, axis=-1, keepdims=True) + EPS)
            qhat, khat = qh * rq, kh * rk
            dyq = dqn_ref[:, sl] * (GDN_DIM ** -0.5)
            dyk = dkn_ref[:, sl]
            dq = rq * (dyq - qhat * jnp.sum(dyq * qhat, axis=-1, keepdims=True))
            dk = rk * (dyk - khat * jnp.sum(dyk * khat, axis=-1, keepdims=True))
            dc_ref[:, sl] = dq * dsl[:, sl]
            dc_ref[:, ks] = dk * dsl[:, ks]
            beta = _sigmoid(ba[:, h:h + 1])
            db = dbeta_ref[:, h:h + 1] * beta * (1.0 - beta)
            aneg = -jnp.exp(al_ref[0:1, h:h + 1])
            xa = ba[:, GDN_HEADS + h:GDN_HEADS + h + 1] + dt_ref[0:1, h:h + 1]
            dgh = dg_ref[:, h:h + 1]
            dxa = dgh * aneg * _sigmoid(xa)
            dba = jnp.where(lane == h, db, dba)
            dba = jnp.where(lane == GDN_HEADS + h, dxa, dba)
            d_alog = jnp.sum(dgh * _softplus(xa), axis=0, keepdims=True) * aneg
            sm = jnp.where(lane1 == h, d_alog, sm)
            sm = jnp.where(lane1 == GDN_HEADS + h, jnp.sum(dxa, axis=0, keepdims=True), sm)
        vs = slice(2 * GDN_WIDTH, W3)
        dc_ref[:, vs] = dv_ref[...] * dsl[:, vs]
        dba_ref[...] = dba

        @pl.when(i == 0)
        def _():
            sm_ref[...] = sm

        @pl.when(i > 0)
        def _():
            sm_ref[...] += sm

    prev_spec, _ = _halo_specs(tm, W3, 0, nblk)
    row = pl.BlockSpec((tm, GDN_WIDTH), lambda i: (i, 0))
    col = pl.BlockSpec((tm, 128), lambda i: (i, 0))
    small = lambda a: pl.BlockSpec(a.shape, lambda i: (0, 0))
    return pl.pallas_call(
        body, name=name, grid=(nblk,),
        in_specs=[row, row, row, col, col, pl.BlockSpec((tm, W3), lambda i: (i, 0)), prev_spec,
                  pl.BlockSpec((tm, 128), lambda i: (i, W3 // 128)), small(conv_w), small(a_log), small(dt_bias)],
        out_specs=[pl.BlockSpec((tm, W3), lambda i: (i, 0)), col, pl.BlockSpec((1, 128), lambda i: (0, 0))],
        out_shape=[_sds((S, W3)), _sds((S, 128)), _sds((1, 128))], compiler_params=_params(("arbitrary",)),
    )(dqn, dkn, dv, dg, dbeta, proj_a, proj_a, proj_a, conv_w, a_log, dt_bias)


def _gdn_conv_bwd(dc, dba, proj_a, conv_w, *, name, tm=256):
    S = proj_a.shape[0]
    nblk = S // tm
    W3 = 3 * GDN_WIDTH

    def body(dc_ref, dnext_ref, dba_ref, cur_ref, prev_ref, cw_ref, da_ref, dcw_ref):
        i = pl.program_id(0)
        prev = jnp.where(i > 0, prev_ref[...], 0.0)
        nxt = jnp.where(i < nblk - 1, dnext_ref[...], 0.0)
        dx, dw = _conv_rows_bwd(dc_ref[...], nxt, prev, cur_ref[...], cw_ref[...], GDN_CONV)
        da_ref[:, 0:W3] = dx.astype(BF16)
        da_ref[:, W3:] = dba_ref[...].astype(BF16)

        @pl.when(i == 0)
        def _():
            dcw_ref[...] = dw

        @pl.when(i > 0)
        def _():
            dcw_ref[...] += dw

    prev_spec, next_spec = _halo_specs(tm, W3, 0, nblk)
    wide = pl.BlockSpec((tm, W3), lambda i: (i, 0))
    return pl.pallas_call(
        body, name=name, grid=(nblk,),
        in_specs=[wide, next_spec, pl.BlockSpec((tm, 128), lambda i: (i, 0)), wide, prev_spec,
                  pl.BlockSpec(conv_w.shape, lambda i: (0, 0))],
        out_specs=[pl.BlockSpec((tm, A_COLS), lambda i: (i, 0)), pl.BlockSpec(conv_w.shape, lambda i: (0, 0))],
        out_shape=[_sds((S, A_COLS), BF16), _sds(conv_w.shape)], compiler_params=_params(("arbitrary",)),
    )(dc, dc, dba, proj_a, proj_a, conv_w)


def _band_mask(nk):
    i = lax.broadcasted_iota(jnp.int32, (2 * BAND, nk), 0) & (BAND - 1)
    j = lax.broadcasted_iota(jnp.int32, (2 * BAND, nk), 1)
    if nk == BAND:
        return j <= i
    return (j >= i) & (j <= i + BAND)


def _stack_heads(x, lo):
    return jnp.concatenate([jnp.where(lo, x, 0.0), jnp.where(lo, 0.0, x)], axis=0)


def _stack_cols(x):
    return jnp.concatenate([x[:, 0:1], x[:, DIL_DIM:DIL_DIM + 1]], axis=0)


def _unstack(x, lo):
    return jnp.where(lo, x[0:BAND], x[BAND:2 * BAND])


def _rows(start, size, stride):
    return pl.ds(start, size) if stride == 1 else pl.ds(start, size, stride=stride)


ATTN_LANES = 4


def _attn_blocks(S, visit_many, lanes=ATTN_LANES):
    for d in DILATIONS:
        nb = S // (d * BAND)
        if d == 1:
            half = nb // 2
            visit_many(d, [(0, 0, True), (0, half, False)])

            def pair(n, c):
                visit_many(1, [(0, n, False), (0, n + half, False)])
                return c
            lax.fori_loop(1, half, pair, 0)
        elif nb > 1:
            for r0 in range(0, d, lanes):
                visit_many(d, [(r0 + t, 0, True) for t in range(lanes)])

                def column(n, c, d=d, r0=r0):
                    visit_many(d, [(r0 + t, n, False) for t in range(lanes)])
                    return c
                lax.fori_loop(1, nb, column, 0)
        else:
            def group(g, c, d=d):
                visit_many(d, [(g * lanes + t, 0, True) for t in range(lanes)])
                return c
            lax.fori_loop(0, d // lanes, group, 0)


def _attn_fwd(proj_b, oab, *, name):
    S = proj_b.shape[0]
    scale = DIL_DIM ** -0.5

    def body(q_ref, k_ref, v_ref, oab_in_ref, ob_ref, lse_ref, m_ref, l_ref, acc_ref):
        del oab_in_ref
        lane = lax.broadcasted_iota(jnp.int32, (BAND, 128), 1)
        lo = lane < DIL_DIM
        m_ref[...] = jnp.full_like(m_ref, NEG_BIG)
        l_ref[...] = jnp.zeros_like(l_ref)
        acc_ref[...] = jnp.zeros_like(acc_ref)

        def load(d, r, n, first):
            nk = BAND if first else 2 * BAND
            qrows = _rows(r + n * (BAND * d), BAND, d)
            krows = _rows(r if first else r + (n - 1) * (BAND * d), nk, d)
            return dict(nk=nk, qrows=qrows, q=q_ref[qrows, :] * scale, k=k_ref[krows, :].astype(BF16),
                        v=v_ref[krows, :].astype(BF16), m=m_ref[qrows, :], l=l_ref[qrows, :], acc=acc_ref[qrows, :])

        def compute(b):
            q, k, v = b["q"], b["k"], b["v"]
            s = jnp.where(_band_mask(b["nk"]), _bdot_nt(_stack_heads(q, lo), k), NEG_BIG)
            m_old = _stack_cols(b["m"])
            m_new = jnp.maximum(m_old, jnp.max(s, axis=-1, keepdims=True))
            p = jnp.exp(s - m_new)
            alpha = _unstack(jnp.exp(m_old - m_new), lo)
            l_new = alpha * b["l"] + _unstack(jnp.sum(p, axis=-1, keepdims=True), lo)
            return _unstack(m_new, lo), l_new, alpha * b["acc"] + _unstack(_bdot(p, v), lo)

        def visit_many(d, blocks):
            loaded = [load(d, *blk) for blk in blocks]
            done = [compute(b) for b in loaded]
            for b, (m_new, l_new, acc_new) in zip(loaded, done):
                m_ref[b["qrows"], :] = m_new
                l_ref[b["qrows"], :] = l_new
                acc_ref[b["qrows"], :] = acc_new

        _attn_blocks(S, visit_many)
        ob_ref[...] = (acc_ref[...] / l_ref[...]).astype(BF16)
        lse_ref[...] = m_ref[...] + jnp.log(l_ref[...])

    part = lambda t: pl.BlockSpec((S, 128), lambda p: (0, 3 * p + t))
    return pl.pallas_call(
        body, name=name, grid=(DIL_PAIRS,),
        in_specs=[part(0), part(1), part(2), pl.BlockSpec(memory_space=pl.ANY)],
        out_specs=[pl.BlockSpec((S, 128), lambda p: (0, GDN_WIDTH // 128 + p)), pl.BlockSpec((S, 128), lambda p: (0, p))],
        out_shape=[_sds(oab.shape, BF16), _sds((S, DIL_WIDTH))],
        scratch_shapes=[pltpu.VMEM((S, 128), F32)] * 3, input_output_aliases={3: 0},
        compiler_params=_params(("parallel",)),
    )(proj_b, proj_b, proj_b, oab)


def _attn_bwd(proj_b, oab, d_oab, lse, *, name, after=()):
    S = proj_b.shape[0]
    scale = DIL_DIM ** -0.5

    def body(q_ref, k_ref, v_ref, o_ref, do_ref, lse_ref, *rest):
        dqkv_ref, dq_ref, dk_ref, dv_ref, delta_ref = rest[len(after):]
        lane = lax.broadcasted_iota(jnp.int32, (BAND, 128), 1)
        lo = lane < DIL_DIM
        dq_ref[...] = jnp.zeros_like(dq_ref)
        dk_ref[...] = jnp.zeros_like(dk_ref)
        dv_ref[...] = jnp.zeros_like(dv_ref)
        prod = do_ref[...] * o_ref[...].astype(F32)
        lo_all = lax.broadcasted_iota(jnp.int32, (S, 128), 1) < DIL_DIM
        delta_ref[...] = jnp.where(lo_all, jnp.sum(jnp.where(lo_all, prod, 0.0), axis=-1, keepdims=True),
                                   jnp.sum(jnp.where(lo_all, 0.0, prod), axis=-1, keepdims=True))

        def load(d, r, n, first):
            nk = BAND if first else 2 * BAND
            qrows = _rows(r + n * (BAND * d), BAND, d)
            krows = _rows(r if first else r + (n - 1) * (BAND * d), nk, d)
            return dict(nk=nk, qrows=qrows, krows=krows, q=q_ref[qrows, :] * scale, k=k_ref[krows, :], v=v_ref[krows, :],
                        do=do_ref[qrows, :], delta=delta_ref[qrows, :], lse=lse_ref[qrows, :],
                        dq=dq_ref[qrows, :], dk=dk_ref[krows, :], dv=dv_ref[krows, :])

        def compute(b):
            q, k, v, do = b["q"], b["k"], b["v"], b["do"]
            qs, dos = _stack_heads(q, lo), _stack_heads(do, lo)
            p = jnp.where(_band_mask(b["nk"]), jnp.exp(_bdot_nt(qs, k) - _stack_cols(b["lse"])), 0.0)
            ds = p * (_bdot_nt(dos, v) - _stack_cols(b["delta"]))
            dq = b["dq"] + _unstack(_bdot(ds, k), lo) * scale
            return dq, b["dk"] + _bdot_tn(ds, qs), b["dv"] + _bdot_tn(p, dos)

        def visit_many(d, blocks):
            loaded = [load(d, *blk) for blk in blocks]
            done = [compute(b) for b in loaded]
            for b, (dq, dk, dv) in zip(loaded, done):
                dq_ref[b["qrows"], :] = dq
                dk_ref[b["krows"], :] = dk
                dv_ref[b["krows"], :] = dv

        _attn_blocks(S, visit_many, lanes=2)
        dqkv_ref[:, 0:128] = dq_ref[...].astype(BF16)
        dqkv_ref[:, 128:256] = dk_ref[...].astype(BF16)
        dqkv_ref[:, 256:384] = dv_ref[...].astype(BF16)

    half = lambda p: (0, GDN_WIDTH // 128 + p)
    part = lambda t: pl.BlockSpec((S, 128), lambda p: (0, 3 * p + t))
    return pl.pallas_call(
        body, name=name, grid=(DIL_PAIRS,),
        in_specs=[part(0), part(1), part(2), pl.BlockSpec((S, 128), half), pl.BlockSpec((S, 128), half),
                  pl.BlockSpec((S, 128), lambda p: (0, p))] + [ANY] * len(after),
        out_specs=pl.BlockSpec((S, 384), lambda p: (0, p)), out_shape=_sds((S, 3 * DIL_WIDTH), BF16),
        scratch_shapes=[pltpu.VMEM((S, 128), F32)] * 4, compiler_params=_params(("parallel",)),
    )(proj_b, proj_b, proj_b, oab, d_oab, lse, *after)


FF_SLAB = 2 * D_FF // N_DEV
FF_PAIRS = N_DEV // 2
ROWS16 = 16


def _taps(w, x, base, n):
    out = _shifted(x, base, n) * w[0:1]
    for t in range(1, FFN_CONV):
        out = out + _shifted(x, base + t, n) * w[t:t + 1]
    return out


def _ffn_fwd(h2, x1, w_up, conv_w, w_down, final_w, tgt, *, name, tm=512):
    S, D = h2.shape
    ni = S // tm
    per = tm // ROWS16

    def body(h_ref, hp_ref, x1_ref, wg_ref, wu_ref, cg_ref, cu_ref, wd_ref, fw_ref, t_ref,
             dx_ref, dxb_ref, dfw_ref, loss_ref, ug_ref, uu_ref, x2_ref):
        i, j = pl.program_id(0), pl.program_id(1)
        hv = jnp.concatenate([hp_ref[...], h_ref[...]], axis=0)
        row = lax.broadcasted_iota(jnp.int32, (tm + ROWS16, 1), 0)
        keep = (i > 0) | (row >= ROWS16)

        def branch(w_ref, c_ref, u_ref):
            u = lax.dot_general(hv, w_ref[...], (((1,), (1,)), ((), ())), preferred_element_type=F32).astype(BF16)
            u_ref[...] = u[ROWS16:]
            return _taps(c_ref[...], jnp.where(keep, u.astype(F32), 0.0), ROWS16 - (FFN_CONV - 1), tm)

        gate = branch(wg_ref, cg_ref, ug_ref)
        up = branch(wu_ref, cu_ref, uu_ref)
        act = (gate * _sigmoid(gate) * up).astype(BF16)
        part = jnp.dot(act, wd_ref[...], preferred_element_type=F32)

        @pl.when(j == 0)
        def _():
            x2_ref[...] = x1_ref[...] + part

        @pl.when((j > 0) & (j < FF_PAIRS - 1))
        def _():
            x2_ref[...] += part

        @pl.when(j == FF_PAIRS - 1)
        def _():
            xv = x2_ref[...] + part
            wv = fw_ref[...]
            r = lax.rsqrt(jnp.mean(xv * xv, axis=-1, keepdims=True) + EPS)
            err = xv * r * wv - t_ref[...]
            lsum = jnp.sum(jnp.sum(err * err, axis=-1, keepdims=True), axis=0, keepdims=True) * (0.5 / D)
            g = err * (1.0 / D)
            xh = xv * r
            gw = g * wv
            dx = r * (gw - xh * jnp.mean(gw * xh, axis=-1, keepdims=True))
            dx_ref[...] = dx
            dxb_ref[...] = dx.astype(BF16)
            dfw = jnp.sum(g * xh, axis=0, keepdims=True)
            lpart = jnp.broadcast_to(lsum, (1, 128))

            @pl.when(i == 0)
            def _():
                dfw_ref[...] = dfw
                loss_ref[...] = lpart

            @pl.when(i > 0)
            def _():
                dfw_ref[...] += dfw
                loss_ref[...] += lpart

    rows = pl.BlockSpec((tm, D), lambda i, j: (i, 0))
    slab = lambda off: pl.BlockSpec((None, FF_SLAB, D), lambda i, j: (j + off, 0, 0))
    cslab = lambda off: pl.BlockSpec((None, FFN_CONV, FF_SLAB), lambda i, j: (j + off, 0, 0))
    uspec = pl.BlockSpec((None, tm, FF_SLAB), lambda i, j: (j, i, 0))
    return pl.pallas_call(
        body, name=name, grid=(ni, FF_PAIRS),
        in_specs=[rows, pl.BlockSpec((ROWS16, D), lambda i, j: (jnp.maximum(i * per - 1, 0), 0)), rows,
                  slab(0), slab(FF_PAIRS), cslab(0), cslab(FF_PAIRS), pl.BlockSpec((FF_SLAB, D), lambda i, j: (j, 0)),
                  pl.BlockSpec((1, D), lambda i, j: (0, 0)), rows],
        out_specs=[rows, rows, pl.BlockSpec((1, D), lambda i, j: (0, 0)), pl.BlockSpec((1, 128), lambda i, j: (0, 0)), uspec, uspec],
        out_shape=[_sds((S, D)), _sds((S, D), BF16), _sds((1, D)), _sds((1, 128)),
                   _sds((FF_PAIRS, S, FF_SLAB), BF16), _sds((FF_PAIRS, S, FF_SLAB), BF16)],
        scratch_shapes=[pltpu.VMEM((tm, D), F32)],
        compiler_params=_params(("arbitrary", "arbitrary")),
    )(h2, h2, x1, w_up, w_up, conv_w, conv_w, w_down, final_w, tgt)


def _ffn_bwd(dx2, h2, ug, uu, conv_w, w_down, *, name, tm=512):
    S, D = h2.shape
    ni = S // tm
    per = tm // ROWS16
    ext = tm + ROWS16

    def body(dx_ref, dxn_ref, h_ref, ug_ref, ugp_ref, ugn_ref, uu_ref, uup_ref, uun_ref, cg_ref, cu_ref, wd_ref,
             du_ref, gd_ref, gup_ref, dcw_ref, acc_d, acc_g, acc_u, acc_cg, acc_cu):
        i = pl.program_id(1)

        @pl.when(i == 0)
        def _():
            acc_d[...] = jnp.zeros_like(acc_d)
            acc_g[...] = jnp.zeros_like(acc_g)
            acc_u[...] = jnp.zeros_like(acc_u)
            acc_cg[...] = jnp.zeros_like(acc_cg)
            acc_cu[...] = jnp.zeros_like(acc_cu)

        dx = dx_ref[...]
        dxe = jnp.concatenate([dx, dxn_ref[...]], axis=0)
        row = lax.broadcasted_iota(jnp.int32, (ext, 1), 0)
        live = (i < ni - 1) | (row < tm)
        d_act = jnp.where(live, lax.dot_general(dxe, wd_ref[...], (((1,), (1,)), ((), ())), preferred_element_type=F32), 0.0)
        rowp = lax.broadcasted_iota(jnp.int32, (ext + ROWS16, 1), 0)
        keep = (i > 0) | (rowp >= ROWS16)

        def pre(cur, prev, nxt):
            return jnp.where(keep, jnp.concatenate([prev[...], cur[...], nxt[...]], axis=0).astype(F32), 0.0)

        uge, uue = pre(ug_ref, ugp_ref, ugn_ref), pre(uu_ref, uup_ref, uun_ref)
        cg, cu = cg_ref[...], cu_ref[...]
        base = ROWS16 - (FFN_CONV - 1)
        gate = _taps(cg, uge, base, ext)
        up = _taps(cu, uue, base, ext)
        sg = _sigmoid(gate)
        silu = gate * sg
        dgc = d_act * up * _dsilu(gate, sg)
        duc = d_act * silu

        def conv_t(w, dc):
            out = _shifted(dc, FFN_CONV - 1, tm) * w[0:1]
            for t in range(1, FFN_CONV):
                out = out + _shifted(dc, FFN_CONV - 1 - t, tm) * w[t:t + 1]
            return out.astype(BF16)

        du_g, du_u = conv_t(cg, dgc), conv_t(cu, duc)
        du_ref[0] = du_g
        du_ref[1] = du_u
        dcw = lambda dc, xe: jnp.concatenate(
            [jnp.sum(dc[0:tm] * _shifted(xe, base + t, tm), axis=0, keepdims=True) for t in range(FFN_CONV)], axis=0)
        acc_cg[0:FFN_CONV, :] += dcw(dgc, uge)
        acc_cu[0:FFN_CONV, :] += dcw(duc, uue)
        tn = (((0,), (0,)), ((), ()))
        act = (silu[0:tm] * up[0:tm]).astype(BF16)
        acc_d[...] += lax.dot_general(act, dx, tn, preferred_element_type=F32)
        hv = h_ref[...]
        acc_g[...] += lax.dot_general(du_g, hv, tn, preferred_element_type=F32)
        acc_u[...] += lax.dot_general(du_u, hv, tn, preferred_element_type=F32)

        @pl.when(i == ni - 1)
        def _():
            gd_ref[...] = acc_d[...].astype(BF16)
            gup_ref[0] = acc_g[...].astype(BF16)
            gup_ref[1] = acc_u[...].astype(BF16)
            dcw_ref[0] = acc_cg[0:FFN_CONV, :]
            dcw_ref[1] = acc_cu[0:FFN_CONV, :]

    last16 = S // ROWS16 - 1
    rows = pl.BlockSpec((tm, D), lambda j, i: (i, 0))
    rows_next = pl.BlockSpec((ROWS16, D), lambda j, i: (jnp.minimum((i + 1) * per, last16), 0))
    u_cur = pl.BlockSpec((None, tm, FF_SLAB), lambda j, i: (j, i, 0))
    u_prev = pl.BlockSpec((None, ROWS16, FF_SLAB), lambda j, i: (j, jnp.maximum(i * per - 1, 0), 0))
    u_next = pl.BlockSpec((None, ROWS16, FF_SLAB), lambda j, i: (j, jnp.minimum((i + 1) * per, last16), 0))
    cslab = lambda off: pl.BlockSpec((None, FFN_CONV, FF_SLAB), lambda j, i: (j + off, 0, 0))
    return pl.pallas_call(
        body, name=name, grid=(FF_PAIRS, ni),
        in_specs=[rows, rows_next, rows, u_cur, u_prev, u_next, u_cur, u_prev, u_next, cslab(0), cslab(FF_PAIRS),
                  pl.BlockSpec((FF_SLAB, D), lambda j, i: (j, 0))],
        out_specs=[pl.BlockSpec((None, 2, tm, FF_SLAB), lambda j, i: (j, 0, i, 0)), pl.BlockSpec((FF_SLAB, D), lambda j, i: (j, 0)),
                   pl.BlockSpec((None, 2, FF_SLAB, D), lambda j, i: (j, 0, 0, 0)),
                   pl.BlockSpec((None, 2, FFN_CONV, FF_SLAB), lambda j, i: (j, 0, 0, 0))],
        out_shape=[_sds((FF_PAIRS, 2, S, FF_SLAB), BF16), _sds((D_FF, D), BF16), _sds((FF_PAIRS, 2, FF_SLAB, D), BF16),
                   _sds((FF_PAIRS, 2, FFN_CONV, FF_SLAB))],
        scratch_shapes=[pltpu.VMEM((FF_SLAB, D), F32), pltpu.VMEM((FF_SLAB, D), F32), pltpu.VMEM((FF_SLAB, D), F32),
                        pltpu.VMEM((8, FF_SLAB), F32), pltpu.VMEM((8, FF_SLAB), F32)],
        compiler_params=_params(("parallel", "arbitrary")),
    )(dx2, dx2, h2, ug, ug, ug, uu, uu, uu, conv_w, conv_w, w_down)


def _pair_slot(p):
    return 2 * (p & (FF_PAIRS - 1)) + (p >> 2)


def _mm_slabs(a, w, *, name, res=None, norm_bwd=None, after=(), tm=1024, tn=1024):
    nk, S, _ = a.shape
    D = w.shape[2]
    has_res = res is not None
    has_norm = norm_bwd is not None
    assert not has_norm or tn == D

    def body(*refs):
        a_ref, w_ref = refs[:2]
        r_ref = refs[2] if has_res else None
        if has_norm:
            x_ref, nw_ref, skip_ref = refs[2 + has_res:5 + has_res]
            o_ref, dw_ref, acc_ref = refs[-3:]
        else:
            o_ref, acc_ref = refs[-2:]
        i, k = pl.program_id(0), pl.program_id(2)
        part = jnp.dot(a_ref[...], w_ref[...], preferred_element_type=F32)

        @pl.when(k == 0)
        def _():
            acc_ref[...] = part

        @pl.when(k > 0)
        def _():
            acc_ref[...] += part

        @pl.when(k == nk - 1)
        def _():
            r = acc_ref[...] + r_ref[...] if has_res else acc_ref[...]
            if has_norm:
                dx, dw = _rms_bwd_rows(r, x_ref[...], nw_ref[...])
                o_ref[...] = skip_ref[...] + dx

                @pl.when(i == 0)
                def _():
                    dw_ref[...] = dw

                @pl.when(i > 0)
                def _():
                    dw_ref[...] += dw
            else:
                o_ref[...] = r

    o_spec = pl.BlockSpec((tm, tn), lambda i, j, k: (i, j))
    one = pl.BlockSpec((1, tn), lambda i, j, k: (0, 0))
    return pl.pallas_call(
        body, name=name, grid=(S // tm, D // tn, nk),
        in_specs=[pl.BlockSpec((None, tm, FF_SLAB), lambda i, j, k: (k, i, 0)),
                  pl.BlockSpec((None, FF_SLAB, tn), lambda i, j, k: (FF_PAIRS * (k & 1) + (k >> 1), 0, j))] + [o_spec] * has_res
        + ([o_spec, one, o_spec] if has_norm else []) + [ANY] * len(after),
        out_specs=[o_spec, one] if has_norm else o_spec, out_shape=[_sds((S, D)), _sds((1, D))] if has_norm else _sds((S, D)),
        scratch_shapes=[pltpu.VMEM((tm, tn), F32)],
        compiler_params=_params(("arbitrary" if has_norm else "parallel", "parallel", "arbitrary")),
    )(*((a, w) + ((res,) if has_res else ()) + (tuple(norm_bwd) if has_norm else ()) + tuple(after)))


def _local_step(x, tgt, norm1_w, w_a, w_z, w_b, conv_a, a_log, dt_bias, gnw, norm2_w, final_w, late_weights, emit, start_after=()):
    wgrad = functools.partial(_mm, ta=True, out_dtype=BF16)
    h1, proj_a, proj_z, proj_b = _in_proj(x, norm1_w, w_a, w_z, w_b, after=start_after, name="in_proj")
    qn, kn, v, gcb, bb = _gdn_prep_fwd(proj_a, conv_a, a_log, dt_bias, name="gdn_prep_fwd")
    uv, wk, at, tmat, wkb, qdb, keb = _gdn_chunk_fwd(qn, kn, v, gcb, bb, name="gdn_chunk_fwd")
    o, u, sp, oab = _gdn_scan_fwd(uv, at, wkb, qdb, keb, gcb, proj_z, gnw, name="gdn_scan_fwd")
    oab, lse = _attn_fwd(proj_b, oab, name="attn_fwd")
    w_out, w_up, conv_f, w_down = late_weights(oab)
    x1, h2 = _out_proj_norm(oab, w_out, x, norm2_w, name="out_proj")
    dx2, dx2_b, d_final, loss, ug, uu = _ffn_fwd(h2, x1, w_up, conv_f, w_down, final_w, tgt, name="ffn_fwd")
    du, g_down, g_up, dcw = _ffn_bwd(dx2_b, h2, ug, uu, conv_f, w_down, name="ffn_bwd")
    token = emit("ffn", w_down=g_down, w_up=g_up.reshape(N_DEV, FF_SLAB, -1), conv_f=dcw.reshape(N_DEV, FFN_CONV, -1))
    dx1, d_norm2 = _mm_slabs(du.reshape(N_DEV, -1, FF_SLAB), w_up, norm_bwd=(x1, norm2_w, dx2), after=token, name="ffn_up_dx")
    d_oab = _mm(dx1, w_out, tb=True, name="out_proj_dx", tn=D_MODEL, tk=1024)
    token = emit("out", w_out=wgrad(oab, dx1, name="out_proj_dw", tm=D_MODEL, tn=D_MODEL))
    dz, d_gnw, du, dwk, dat, dqd, dke, dgl = _gdn_scan_bwd(d_oab, o, proj_z, gnw, sp, u, at, wkb, qdb, keb, gcb, after=token, name="gdn_scan_bwd")
    dqn, dkn, dv, dg, dbeta = _gdn_chunk_bwd(qn, kn, gcb, bb, tmat, uv, wk, du, dwk, dat, dqd, dke, dgl, name="gdn_chunk_bwd")
    dc, dba, d_small = _gdn_prep_bwd(dqn, dkn, dv, dg, dbeta, proj_a, conv_a, a_log, dt_bias, name="gdn_prep_bwd")
    d_pa, d_conv_a = _gdn_conv_bwd(dc, dba, proj_a, conv_a, name="gdn_conv_bwd")
    g_a = wgrad(d_pa, h1, name="proj_a_dw", tm=A_COLS, tn=D_MODEL)
    g_z = wgrad(dz, h1, name="proj_z_dw", tn=D_MODEL)
    token = emit("in_lo", w_a=g_a, w_z=g_z)
    d_pb = _attn_bwd(proj_b, oab, d_oab, lse, after=token, name="attn_bwd")
    g_b = wgrad(d_pb, h1, name="proj_b_dw", tm=768, tn=D_MODEL)
    token = emit("in", w_a=g_a, w_z=g_z, w_b=g_b, conv_a=d_conv_a)
    grad_x, d_norm1 = _in_proj_dx((d_pa, dz, d_pb), (w_a, w_z, w_b), x, norm1_w, dx1, after=token, name="in_proj_dx")
    small = dict(norm1=d_norm1, small=d_small, gnw=d_gnw, norm2=d_norm2, final=d_final)
    return loss, grad_x, small


_O1 = 3 * GDN_WIDTH
_O2 = _O1 + GDN_WIDTH
_O3 = _O2 + 2 * GDN_HEADS
_IN_ROWS = (_O3 + 3 * DIL_WIDTH) // N_DEV
_LO_DEVS = 4
assert _LO_DEVS * _IN_ROWS <= _O2


def _split_w_in(w_t):
    d = w_t.shape[1]
    pad = jnp.zeros((A_COLS - _O1 - 2 * GDN_HEADS, d), w_t.dtype)
    w_a = jnp.concatenate([w_t[:_O1], w_t[_O2:_O3], pad], axis=0)
    w_b = w_t[_O3:].reshape(3, DIL_PAIRS, 128, d).transpose(1, 0, 2, 3).reshape(3 * DIL_WIDTH, d)
    return w_a, w_t[_O1:_O2], w_b


def _merge_g_in(g_a, g_z, g_b):
    d = g_a.shape[1]
    g_b = g_b.reshape(DIL_PAIRS, 3, 128, d).transpose(1, 0, 2, 3).reshape(3 * DIL_WIDTH, d)
    return jnp.concatenate([g_a[:_O1], g_z, g_a[_O1:_O1 + 2 * GDN_HEADS], g_b], axis=0)


MESH = pl.DeviceIdType.MESH
ANY = pl.BlockSpec(memory_space=pl.ANY)


def _position():
    return lax.axis_index("x"), lax.axis_index("y"), lax.axis_index("c")


def _slot(p):
    return 4 * p[0] + 2 * p[1] + p[2]


def _all_gather(blocks, *, name):
    n = len(blocks)

    def body(*refs):
        ins, outs = refs[:n], refs[n:2 * n]
        send_sems, recv_sems, local_sems = refs[2 * n:]
        x, y, c = _position()
        me, sibling = (x, y, c), (x, y, 1 - c)
        chips = [(1 - x, y), (x, 1 - y), (1 - x, 1 - y)]

        def copy(a, k, block, to, src=None):
            dst = outs[a].at[_slot(block)]
            return pltpu.make_async_remote_copy(
                src_ref=dst if src is None else src, dst_ref=dst, send_sem=send_sems.at[a, k], recv_sem=recv_sems.at[a, k],
                device_id=to, device_id_type=MESH)

        mine = [pltpu.make_async_copy(ins[a], outs[a].at[_slot(me)], local_sems.at[a]) for a in range(n)]
        for cp in mine:
            cp.start()
        first = []
        for a in range(n):
            first.append(copy(a, 0, me, sibling, src=ins[a]))
            first += [copy(a, 1 + j, me, (*chip, c), src=ins[a]) for j, chip in enumerate(chips)]
        for cp in first:
            cp.start()
        passed = []
        for j, chip in enumerate(chips):
            for a in range(n):
                copy(a, 1 + j, (*chip, c), me).wait_recv()
                fwd = copy(a, 4 + j, (*chip, c), sibling)
                fwd.start()
                passed.append(fwd)
        for a in range(n):
            copy(a, 0, sibling, me).wait_recv()
            for j, chip in enumerate(chips):
                copy(a, 4 + j, (*chip, 1 - c), me).wait_recv()
        for cp in first + passed:
            cp.wait_send()
        for cp in mine:
            cp.wait()

    return pl.pallas_call(
        body, name=name, in_specs=[ANY] * n, out_specs=[ANY] * n,
        out_shape=[_sds((N_DEV,) + b.shape, b.dtype) for b in blocks],
        scratch_shapes=[pltpu.SemaphoreType.DMA((n, 7)), pltpu.SemaphoreType.DMA((n, 7)), pltpu.SemaphoreType.DMA((n,))],
    )(*blocks)


def _gather_direct(block, *, name, after=()):
    def body(in_ref, *rest):
        out_ref, send_sems, recv_sems, local_sem = rest[len(after):]
        x, y, c = _position()
        me = _slot((x, y, c))
        mine = pltpu.make_async_copy(in_ref, out_ref.at[me], local_sem)
        mine.start()
        copies = [pltpu.make_async_remote_copy(
            src_ref=in_ref, dst_ref=out_ref.at[me], send_sem=send_sems.at[k - 1], recv_sem=recv_sems.at[k - 1],
            device_id=_peer_of(k, x, y, c), device_id_type=MESH) for k in range(1, N_DEV)]
        for cp in copies:
            cp.start()
        for cp in copies:
            cp.wait()
        mine.wait()

    return pl.pallas_call(
        body, name=name, in_specs=[pl.BlockSpec(memory_space=pltpu.VMEM)] + [ANY] * len(after),
        out_specs=pl.BlockSpec(memory_space=pltpu.VMEM),
        out_shape=_sds((N_DEV,) + block.shape, block.dtype),
        scratch_shapes=[pltpu.SemaphoreType.DMA((N_DEV - 1,)), pltpu.SemaphoreType.DMA((N_DEV - 1,)), pltpu.SemaphoreType.DMA],
    )(block, *after)


HBM = pl.BlockSpec(memory_space=pltpu.HBM)
SEM = pl.BlockSpec(memory_space=pltpu.SEMAPHORE)
EFFECT = pltpu.SideEffectType.DATAFLOW_SIDE_EFFECTING


def _peer_of(k, x, y, c):
    return (1 - x if k & 4 else x, 1 - y if k & 2 else y, 1 - c if k & 1 else c)


def _flight(a, k):
    return a * (N_DEV - 1) + k - 1


def _serves(dests, slot):
    return None if dests is None else (slot >= dests[0]) & (slot < dests[1])


def _maybe(cond, action):
    if cond is None:
        action()
    else:
        pl.when(cond)(action)


def _exchange_start(arrays, *, name, broadcast=False, paired=(), dests=None, lands=None):
    n = len(arrays)
    dests = [None] * n if dests is None else dests
    lands = [None] * n if lands is None else lands

    def body(*refs):
        ins, lnd = refs[:n], refs[n:2 * n]
        send_sems, recv_sems = refs[2 * n:2 * n + 2]
        token = refs[-1]
        x, y, c = _position()
        me = _slot((x, y, c))
        for k in range(1, N_DEV):
            peer = _peer_of(k, x, y, c)
            to = _slot(peer)
            for a in range(n):
                at = _pair_slot(to) if a in paired else to
                if dests[a] is not None:
                    at = jnp.clip(to - dests[a][2], 0, arrays[a].shape[0] - 1)
                cp = pltpu.make_async_remote_copy(
                    src_ref=ins[a] if broadcast else ins[a].at[at], dst_ref=lnd[a].at[me],
                    send_sem=send_sems.at[_flight(a, k)], recv_sem=recv_sems.at[_flight(a, k)],
                    device_id=peer, device_id_type=MESH)
                _maybe(_serves(dests[a], to), cp.start)
        token[...] = jnp.zeros_like(token)

    land_shapes = [((N_DEV,) + s.shape) if broadcast else (N_DEV,) + s.shape[1:] for s in arrays]
    lands = [pltpu.with_memory_space_constraint(lax.empty(shp, s.dtype) if l is None else l, pltpu.HBM)
             for shp, s, l in zip(land_shapes, arrays, lands)]
    srcs = [pltpu.with_memory_space_constraint(s, pltpu.HBM) for s in arrays]
    outs = pl.pallas_call(
        body, name=name, in_specs=[HBM] * (2 * n),
        out_specs=[SEM, SEM] + [HBM] * (2 * n) + [pl.BlockSpec(memory_space=pltpu.VMEM)],
        out_shape=[pltpu.SemaphoreType.DMA((n * (N_DEV - 1),)), pltpu.SemaphoreType.DMA((n * (N_DEV - 1),))]
        + [pltpu.HBM(s.shape, s.dtype) for s in arrays] + [pltpu.HBM(shp, s.dtype) for shp, s in zip(land_shapes, arrays)]
        + [_sds((8, 128))],
        input_output_aliases={i: 2 + i for i in range(2 * n)},
        compiler_params=pltpu.CompilerParams(has_side_effects=EFFECT),
    )(*srcs, *lands)
    return outs[0], outs[1], outs[2:2 + n], outs[2 + n:2 + 2 * n], outs[-1]


def _exchange_wait(send_sems, recv_sems, srcs, lands, after, *, name, broadcast=False, dests=None):
    n = len(srcs)
    dests = [None] * n if dests is None else dests

    def body(*refs):
        ins, lnd = refs[:n], refs[n:2 * n]
        send_ref, recv_ref = refs[2 * n:2 * n + 2]
        x, y, c = _position()
        me = _slot((x, y, c))
        for k in range(1, N_DEV):
            peer = _peer_of(k, x, y, c)
            for a in range(n):
                cp = pltpu.make_async_remote_copy(
                    src_ref=ins[a] if broadcast else ins[a].at[0], dst_ref=lnd[a].at[0], send_sem=send_ref.at[_flight(a, k)],
                    recv_sem=recv_ref.at[_flight(a, k)], device_id=peer, device_id_type=MESH)
                _maybe(_serves(dests[a], _slot(peer)), cp.wait_send)
                _maybe(_serves(dests[a], me), cp.wait_recv)

    outs = pl.pallas_call(
        body, name=name, in_specs=[HBM] * (2 * n) + [SEM, SEM, ANY], out_specs=[HBM] * (2 * n),
        out_shape=[pltpu.HBM(s.shape, s.dtype) for s in srcs] + [pltpu.HBM(s.shape, s.dtype) for s in lands],
        input_output_aliases={i: i for i in range(2 * n)},
        compiler_params=pltpu.CompilerParams(has_side_effects=EFFECT),
    )(*srcs, *lands, send_sems, recv_sems, after)
    return outs[:n], outs[n:]


def _with_own(landed, srcs, me):
    return [lax.dynamic_update_index_in_dim(l, o, me, 0) for l, o in zip(landed, srcs)]


def _adam_update(g, w, m, v):
    c1 = 1.0 - ADAM_B1 ** ADAM_STEP
    c2 = 1.0 - ADAM_B2 ** ADAM_STEP
    nm = ADAM_B1 * m + (1.0 - ADAM_B1) * g
    nv = ADAM_B2 * v + (1.0 - ADAM_B2) * (g * g)
    return -ADAM_LR * ((nm / c1) / (jnp.sqrt(nv / c2) + ADAM_EPS) + ADAM_WD * w), nm, nv


def _adamw(landed, sent, me, w, m, v, *, name, tr=None, tc=None):
    R, C = w.shape
    tr = R if tr is None else tr
    tc = C if tc is None else tc
    assert R % tr == 0 and C % tc == 0

    def body(me_ref, own_ref, p_ref, w_ref, m_ref, v_ref, g_ref, d_ref, nm_ref, nv_ref, token_ref):
        token_ref[...] = jnp.zeros_like(token_ref)
        g = own_ref[...].astype(F32)
        for s in range(N_DEV):
            g = g + jnp.where(me_ref[1] == s, 0.0, p_ref[s].astype(F32))
        delta, nm, nv = _adam_update(g, w_ref[...], m_ref[...], v_ref[...])
        g_ref[...] = g
        nm_ref[...] = nm
        nv_ref[...] = nv
        d_ref[...] = delta

    blk = pl.BlockSpec((tr, tc), lambda i, j, me_ref: (i, j))
    return pl.pallas_call(
        body, name=name,
        grid_spec=pltpu.PrefetchScalarGridSpec(
            num_scalar_prefetch=1, grid=(R // tr, C // tc),
            in_specs=[pl.BlockSpec((None, tr, tc), lambda i, j, me_ref: (me_ref[0], i, j)),
                      pl.BlockSpec((N_DEV, tr, tc), lambda i, j, me_ref: (0, i, j)), blk, blk, blk],
            out_specs=[blk] * 4 + [pl.BlockSpec((8, 128), lambda i, j, me_ref: (0, 0))]),
        out_shape=[_sds((R, C))] * 4 + [_sds((8, 128))],
        compiler_params=_params(("arbitrary", "arbitrary")),
    )(me, sent, landed, w, m, v)


_SMALL_ROWS = 8
_SMALL_SLOTS = ((0, 0, D_MODEL), (1, 0, D_MODEL), (2, 0, D_MODEL), (3, 0, GDN_DIM), (3, GDN_DIM, GDN_HEADS),
                (3, GDN_DIM + GDN_HEADS, GDN_HEADS))
_LOSS_LANE = 2 * GDN_DIM


def _pack_small(norm1, norm2, final, gnw, a_log, dt_bias, loss):
    row3 = jnp.concatenate([gnw, a_log, dt_bias, jnp.zeros((1, 128 - 2 * GDN_HEADS), F32), loss,
                            jnp.zeros((1, D_MODEL - 3 * 128), F32)], axis=1)
    return jnp.concatenate([norm1, norm2, final, row3, jnp.zeros((_SMALL_ROWS - 4, D_MODEL), F32)], axis=0)


def _adamw_small(packs, ws, ms, vs, *, name):
    n = len(ws)

    def body(p_ref, *refs):
        w_refs, m_refs, v_refs = refs[:n], refs[n:2 * n], refs[2 * n:3 * n]
        outs = refs[3 * n:]
        g_all = p_ref[0]
        for s in range(1, N_DEV):
            g_all = g_all + p_ref[s]
        for i, (row, lane, width) in enumerate(_SMALL_SLOTS):
            g = g_all[row:row + 1, lane:lane + width]
            delta, nm, nv = _adam_update(g, w_refs[i][...], m_refs[i][...], v_refs[i][...])
            for o_ref, val in zip(outs[4 * i:4 * i + 4], (g, delta, nm, nv)):
                o_ref[...] = val
        outs[-1][...] = g_all[3:4, _LOSS_LANE:_LOSS_LANE + 128]

    vm = pl.BlockSpec(memory_space=pltpu.VMEM)
    outs = pl.pallas_call(
        body, name=name, in_specs=[vm] * (1 + 3 * n), out_specs=[vm] * (4 * n + 1),
        out_shape=[_sds(w.shape) for w in ws for _ in range(4)] + [_sds((1, 128))],
    )(packs, *ws, *ms, *vs)
    return [outs[4 * i:4 * i + 4] for i in range(n)], outs[-1]


def _slabs_by_cols(g):
    r = g.shape[0]
    return g.reshape(r, N_DEV, -1).transpose(1, 0, 2)


def _cols_from_slabs(s):
    return s.transpose(1, 0, 2).reshape(s.shape[1], -1)


def kernel(x, norm1_w, w_in, conv_qkv_w, a_log, dt_bias, gdn_norm_w, w_out, norm2_w, w_up, ffn_conv_w, w_down, final_norm_w, loss_target, m_norm1_w, m_w_in, m_conv_qkv_w, m_a_log, m_dt_bias, m_gdn_norm_w, m_w_out, m_norm2_w, m_w_up, m_ffn_conv_w, m_w_down, m_final_norm_w, v_norm1_w, v_w_in, v_conv_qkv_w, v_a_log, v_dt_bias, v_gdn_norm_w, v_w_out, v_norm2_w, v_w_up, v_ffn_conv_w, v_w_down, v_final_norm_w):
    bf = lambda a: a.astype(BF16)
    me = _slot(_position())
    t_in = lambda a: a[0].T
    gw_in, g_conv_a = _all_gather([bf(t_in(w_in)), conv_qkv_w[0]], name="gather_w_in")
    w_a, w_z, w_b = _split_w_in(gw_in.reshape(-1, D_MODEL))
    late_src, _ = lax.optimization_barrier(([bf(w_out[0]), bf(t_in(w_up)), bf(w_down[0]), ffn_conv_w[0]], gw_in))
    l_send, l_recv, l_srcs, l_lands, l_token = _exchange_start(late_src, name="weights_start", broadcast=True)

    def late_weights(after):
        srcs, landed = _exchange_wait(l_send, l_recv, l_srcs, l_lands, after, name="weights_wait", broadcast=True)
        gw_out, gw_up, gw_down, g_conv_f = _with_own(landed, srcs, me)
        return gw_out.reshape(D_MODEL, D_MODEL), gw_up, g_conv_f, gw_down.reshape(D_FF, D_MODEL)

    flights = {}

    def emit(group, **grads):
        paired, dests, lands = (), None, None
        if group == "in_lo":
            rows = jnp.concatenate([grads["w_a"][:_O1], grads["w_z"][:_LO_DEVS * _IN_ROWS - _O1]], axis=0)
            slabs = dict(w_in_lo=rows.reshape(_LO_DEVS, _IN_ROWS, D_MODEL))
            dests = [(0, _LO_DEVS, 0)]
        elif group == "in":
            slabs = dict(w_in=_merge_g_in(grads["w_a"], grads["w_z"], grads["w_b"]).reshape(N_DEV, -1, D_MODEL),
                         conv_a=_slabs_by_cols(grads["conv_a"]))
            dests = [(_LO_DEVS, N_DEV, 0), None]
            _, (send_sems, recv_sems, srcs, lo_lands, _) = flights.pop("in_lo")
            _, landed_lo = _exchange_wait(send_sems, recv_sems, srcs, lo_lands, grads["w_b"], dests=[(0, _LO_DEVS, 0)],
                                          name="grads_wait_in_lo")
            lands = [landed_lo[0], None]
        elif group == "ffn":
            slabs = dict(w_down=grads["w_down"].reshape(N_DEV, -1, D_MODEL), w_up=grads["w_up"], conv_f=grads["conv_f"])
            paired = (1, 2)
        else:
            slabs = {k: v.reshape(N_DEV, -1, D_MODEL) for k, v in grads.items()}
        names = list(slabs)
        *flight, token = _exchange_start([slabs[k] for k in names], paired=paired, dests=dests, lands=lands,
                                         name="grads_start_" + group)
        flights[group] = (names, flight + [dests])
        return (token,)

    loss, grad_x, g = _local_step(
        x[0], loss_target[0], norm1_w, w_a, w_z, w_b, _cols_from_slabs(g_conv_a), a_log, dt_bias,
        gdn_norm_w, norm2_w, final_norm_w[None], late_weights, emit, start_after=(l_token,))
    got = {}
    me1 = jnp.reshape(me, (1,)).astype(jnp.int32)

    def collect(group, after):
        names, (send_sems, recv_sems, srcs, lands, dests) = flights[group]
        srcs, landed = _exchange_wait(send_sems, recv_sems, srcs, lands, after, dests=dests, name="grads_wait_" + group)
        got.update(zip(names, zip(landed, srcs)))

    def update(key, w, m, v, paired=False, **tiles):
        where = jnp.concatenate([_pair_slot(me1) if paired else me1, me1])
        return _adamw(*got[key], where, w, m, v, name="adamw_" + key, **tiles)

    collect("ffn", grad_x)
    collect("out", grad_x)
    *o_out, t1 = update("w_out", w_out[0], m_w_out[0], v_w_out[0])
    *o_up, t2 = update("w_up", t_in(w_up), t_in(m_w_up), t_in(v_w_up), paired=True, tr=176)
    o_up = [o.T for o in o_up]
    *o_down, t3 = update("w_down", w_down[0], m_w_down[0], v_w_down[0], tr=176)
    *o_cf, t4 = update("conv_f", ffn_conv_w[0], m_ffn_conv_w[0], v_ffn_conv_w[0], paired=True)
    pack = _pack_small(g["norm1"], g["norm2"], g["final"], g["gnw"], g["small"][:, 0:GDN_HEADS],
                       g["small"][:, GDN_HEADS:2 * GDN_HEADS], loss)
    small_all = _gather_direct(pack, after=(t1, t2, t3, t4), name="gather_small")
    collect("in", small_all)
    o_in = [o.T for o in update("w_in", t_in(w_in), t_in(m_w_in), t_in(v_w_in), tc=256)[:4]]
    o_ca = update("conv_a", conv_qkv_w[0], m_conv_qkv_w[0], v_conv_qkv_w[0])
    (o_n1, o_n2, o_fin, o_gn, o_al, o_dt), total = _adamw_small(
        small_all, (norm1_w, norm2_w, final_norm_w[None], gdn_norm_w, a_log, dt_bias),
        (m_norm1_w, m_norm2_w, m_final_norm_w[None], m_gdn_norm_w, m_a_log, m_dt_bias),
        (v_norm1_w, v_norm2_w, v_final_norm_w[None], v_gdn_norm_w, v_a_log, v_dt_bias), name="adamw_small")
    outs = [total[0, 0], grad_x[None]]
    for k in range(4):
        outs += [o_n1[k], o_in[k][None], o_ca[k][None], o_al[k], o_dt[k], o_gn[k], o_out[k][None], o_n2[k], o_up[k][None],
                 o_cf[k][None], o_down[k][None], o_fin[k][0]]
    return tuple(outs)
```

```python
import functools

import jax
import jax.numpy as jnp
from jax import lax
from jax.experimental import pallas as pl
from jax.experimental.pallas import tpu as pltpu

F32 = jnp.float32
BF16 = jnp.bfloat16

N_DEV = 8
D_MODEL = 1024
GDN_HEADS = 4
GDN_DIM = 128
GDN_WIDTH = GDN_HEADS * GDN_DIM
GDN_CONV = 4
CHUNK = 64
CHUNKS_PER_STEP = 4
DIL_HEADS = 8
DIL_DIM = 64
DIL_WIDTH = DIL_HEADS * DIL_DIM
DIL_PAIRS = DIL_HEADS // 2
DILATIONS = (1, 4, 16)
BAND = 128
D_FF = 2816
FFN_CONV = 3
EPS = 1e-6
A_COLS = 3 * GDN_WIDTH + 128
HALO = 8

ADAM_LR = 0.001
ADAM_B1 = 0.9
ADAM_B2 = 0.999
ADAM_EPS = 1e-08
ADAM_WD = 0.01
ADAM_STEP = 10

VMEM_LIMIT_BYTES = 56 * 1024 * 1024
NEG_BIG = -1e30


def _params(sem=None):
    return pltpu.CompilerParams(dimension_semantics=sem, vmem_limit_bytes=VMEM_LIMIT_BYTES)


def _sds(shape, dtype=F32):
    return jax.ShapeDtypeStruct(shape, dtype)


def _bdot(a, b):
    return jnp.dot(a.astype(BF16), b.astype(BF16), preferred_element_type=F32)


def _bdot_nt(a, b):
    return lax.dot_general(a.astype(BF16), b.astype(BF16), (((1,), (1,)), ((), ())), preferred_element_type=F32)


def _bdot_tn(a, b):
    return lax.dot_general(a.astype(BF16), b.astype(BF16), (((0,), (0,)), ((), ())), preferred_element_type=F32)


def _split(a):
    hi = a.astype(BF16)
    lo = (a - hi.astype(F32)).astype(BF16)
    return hi, lo


def _dot3(a, b, dims):
    ah, al = _split(a)
    bh, bl = _split(b)
    d = functools.partial(lax.dot_general, dimension_numbers=(dims, ((), ())), preferred_element_type=F32)
    return d(ah, bh) + (d(al, bh) + d(ah, bl))


def _exact_tri_dot(tri, g):
    g1 = g.astype(BF16)
    r1 = g - g1.astype(F32)
    g2 = r1.astype(BF16)
    g3 = (r1 - g2.astype(F32)).astype(BF16)
    t = tri.astype(BF16)
    d = functools.partial(jnp.dot, preferred_element_type=F32)
    return d(t, g1) + (d(t, g2) + d(t, g3))


def _sigmoid(x):
    return 1.0 / (1.0 + jnp.exp(-x))


def _dsilu(x, sg):
    return sg * (1.0 + x * (1.0 - sg))


def _rms_bwd_rows(dh, x, w):
    r = lax.rsqrt(jnp.mean(x * x, axis=-1, keepdims=True) + EPS)
    xh = x * r
    gw = dh * w
    return r * (gw - xh * jnp.mean(gw * xh, axis=-1, keepdims=True)), jnp.sum(dh * xh, axis=0, keepdims=True)


def _mm(a, b, *, name, ta=False, tb=False, res=None, norm_bwd=None, after=(), out_dtype=F32, tm=512, tn=512, tk=512):
    if ta:
        K, M = a.shape
    else:
        M, K = a.shape
    if tb:
        N, Kb = b.shape
    else:
        Kb, N = b.shape
    assert K == Kb, (a.shape, b.shape)
    tm, tn, tk = min(tm, M), min(tn, N), min(tk, K)
    assert M % tm == 0 and N % tn == 0 and K % tk == 0, (name, M, N, K, tm, tn, tk)
    nk = K // tk
    dims = (((0 if ta else 1,), (1 if tb else 0,)), ((), ()))
    has_res = res is not None
    has_norm = norm_bwd is not None
    assert not has_norm or tn == N

    def body(*refs):
        a_ref, b_ref = refs[:2]
        r_ref = refs[2] if has_res else None
        if has_norm:
            x_ref, w_ref, skip_ref = refs[2 + has_res:5 + has_res]
            o_ref, dw_ref, acc_ref = refs[-3:]
        else:
            o_ref, acc_ref = refs[-2:]
        i, k = pl.program_id(0), pl.program_id(2)
        part = lax.dot_general(a_ref[...].astype(BF16), b_ref[...].astype(BF16), dims, preferred_element_type=F32)

        @pl.when(k == 0)
        def _():
            acc_ref[...] = part

        @pl.when(k > 0)
        def _():
            acc_ref[...] += part

        @pl.when(k == nk - 1)
        def _():
            r = acc_ref[...]
            if has_res:
                r = r + r_ref[...]
            if has_norm:
                dx, dw = _rms_bwd_rows(r, x_ref[...], w_ref[...])
                o_ref[...] = skip_ref[...] + dx

                @pl.when(i == 0)
                def _():
                    dw_ref[...] = dw

                @pl.when(i > 0)
                def _():
                    dw_ref[...] += dw
            else:
                o_ref[...] = r.astype(out_dtype)

    a_spec = pl.BlockSpec((tk, tm), lambda i, j, k: (k, i)) if ta else pl.BlockSpec((tm, tk), lambda i, j, k: (i, k))
    b_spec = pl.BlockSpec((tn, tk), lambda i, j, k: (j, k)) if tb else pl.BlockSpec((tk, tn), lambda i, j, k: (k, j))
    o_spec = pl.BlockSpec((tm, tn), lambda i, j, k: (i, j))
    one = pl.BlockSpec((1, tn), lambda i, j, k: (0, 0))
    in_specs = [a_spec, b_spec] + [o_spec] * has_res + ([o_spec, one, o_spec] if has_norm else []) + [ANY] * len(after)
    args = (a, b) + ((res,) if has_res else ()) + (tuple(norm_bwd) if has_norm else ()) + tuple(after)
    return pl.pallas_call(
        body, name=name, grid=(M // tm, N // tn, nk), in_specs=in_specs,
        out_specs=[o_spec, one] if has_norm else o_spec,
        out_shape=[_sds((M, N)), _sds((1, N))] if has_norm else _sds((M, N), out_dtype),
        scratch_shapes=[pltpu.VMEM((tm, tn), F32)],
        compiler_params=_params(("arbitrary" if has_norm else "parallel", "parallel", "arbitrary")),
    )(*args)


def _in_proj(x, norm_w, w_a, w_z, w_b, *, name, after=(), tm=512):
    S, D = x.shape
    ws = (w_a, w_z, w_b)

    def body(x_ref, nw_ref, wa_ref, wz_ref, wb_ref, *rest):
        h_ref, pa_ref, pz_ref, pb_ref = rest[len(after):]
        xv = x_ref[...]
        r = lax.rsqrt(jnp.mean(xv * xv, axis=-1, keepdims=True) + EPS)
        h = (xv * r * nw_ref[...]).astype(BF16)
        h_ref[...] = h
        for w_ref, p_ref in ((wa_ref, pa_ref), (wz_ref, pz_ref), (wb_ref, pb_ref)):
            p_ref[...] = lax.dot_general(h, w_ref[...], (((1,), (1,)), ((), ())), preferred_element_type=F32)

    row = lambda n: pl.BlockSpec((tm, n), lambda i: (i, 0))
    full = lambda a: pl.BlockSpec(a.shape, lambda i: (0, 0))
    return pl.pallas_call(
        body, name=name, grid=(S // tm,), in_specs=[row(D), full(norm_w)] + [full(w) for w in ws] + [ANY] * len(after),
        out_specs=[row(D)] + [row(w.shape[0]) for w in ws],
        out_shape=[_sds((S, D), BF16)] + [_sds((S, w.shape[0])) for w in ws], compiler_params=_params(("parallel",)),
    )(x, norm_w, *ws, *after)


def _in_proj_dx(ds, ws, x, norm_w, skip, *, name, after=(), tm=512):
    S, D = x.shape
    n = len(ds)

    def body(*refs):
        d_refs, w_refs = refs[:n], refs[n:2 * n]
        x_ref, nw_ref, skip_ref = refs[2 * n:2 * n + 3]
        o_ref, dw_ref = refs[-2:]
        i = pl.program_id(0)
        dh = jnp.dot(d_refs[0][...], w_refs[0][...], preferred_element_type=F32)
        for d_ref, w_ref in zip(d_refs[1:], w_refs[1:]):
            dh = dh + jnp.dot(d_ref[...], w_ref[...], preferred_element_type=F32)
        dx, dw = _rms_bwd_rows(dh, x_ref[...], nw_ref[...])
        o_ref[...] = skip_ref[...] + dx

        @pl.when(i == 0)
        def _():
            dw_ref[...] = dw

        @pl.when(i > 0)
        def _():
            dw_ref[...] += dw

    row = lambda c: pl.BlockSpec((tm, c), lambda i: (i, 0))
    full = lambda a: pl.BlockSpec(a.shape, lambda i: (0, 0))
    return pl.pallas_call(
        body, name=name, grid=(S // tm,),
        in_specs=[row(d.shape[1]) for d in ds] + [full(w) for w in ws] + [row(D), full(norm_w), row(D)] + [ANY] * len(after),
        out_specs=[row(D), pl.BlockSpec((1, D), lambda i: (0, 0))], out_shape=[_sds((S, D)), _sds((1, D))],
        compiler_params=_params(("arbitrary",)),
    )(*ds, *ws, x, norm_w, skip, *after)


def _out_proj_norm(a, w, x, norm_w, *, name, tm=512):
    S, D = x.shape

    def body(a_ref, w_ref, x_ref, nw_ref, x1_ref, h_ref):
        x1 = x_ref[...] + jnp.dot(a_ref[...], w_ref[...], preferred_element_type=F32)
        x1_ref[...] = x1
        r = lax.rsqrt(jnp.mean(x1 * x1, axis=-1, keepdims=True) + EPS)
        h_ref[...] = (x1 * r * nw_ref[...]).astype(BF16)

    row = pl.BlockSpec((tm, D), lambda i: (i, 0))
    return pl.pallas_call(
        body, name=name, grid=(S // tm,),
        in_specs=[pl.BlockSpec((tm, a.shape[1]), lambda i: (i, 0)), pl.BlockSpec(w.shape, lambda i: (0, 0)), row,
                  pl.BlockSpec((1, D), lambda i: (0, 0))],
        out_specs=[row, row], out_shape=[_sds((S, D)), _sds((S, D), BF16)], compiler_params=_params(("parallel",)),
    )(a, w, x, norm_w)


def _shifted(x, start, n):
    aligned = -(-start // HALO) * HALO
    assert aligned + n <= x.shape[0], (start, n, x.shape)
    return (x if aligned == start else pltpu.roll(x, aligned - start, axis=0))[aligned:aligned + n]


def _conv_rows(prev, cur, w, taps):
    n = cur.shape[0]
    xs = jnp.concatenate([prev, cur], axis=0)
    base = HALO - (taps - 1)
    out = _shifted(xs, base, n) * w[0:1]
    for i in range(1, taps):
        out = out + _shifted(xs, base + i, n) * w[i:i + 1]
    return out


def _conv_rows_bwd(cur_d, next_d, prev_x, cur_x, w, taps):
    n = cur_d.shape[0]
    ds = jnp.concatenate([cur_d, next_d], axis=0)
    dx = _shifted(ds, taps - 1, n) * w[0:1]
    for i in range(1, taps):
        dx = dx + _shifted(ds, taps - 1 - i, n) * w[i:i + 1]
    xs = jnp.concatenate([prev_x, cur_x], axis=0)
    base = HALO - (taps - 1)
    dws = [jnp.sum(cur_d * _shifted(xs, base + i, n), axis=0, keepdims=True) for i in range(taps)]
    return dx, jnp.concatenate(dws, axis=0)


def _halo_specs(tm, width, col, nblk):
    per = tm // HALO
    prev = pl.BlockSpec((HALO, width), lambda i, *_: (jnp.maximum(i * per - 1, 0), col))
    nxt = pl.BlockSpec((HALO, width), lambda i, *_: (jnp.minimum((i + 1) * per, nblk * per - 1), col))
    return prev, nxt


def _softplus(x):
    return jnp.maximum(x, 0.0) + jnp.log1p(jnp.exp(-jnp.abs(x)))


def _chunk_tri(tm, upper=False):
    r = lax.broadcasted_iota(jnp.int32, (tm, tm), 0)
    c = lax.broadcasted_iota(jnp.int32, (tm, tm), 1)
    same = lax.div(r, CHUNK) == lax.div(c, CHUNK)
    order = (c >= r) if upper else (c <= r)
    return jnp.where(same & order, 1.0, 0.0)


def _gdn_prep_fwd(proj_a, conv_w, a_log, dt_bias, *, name, tm=256):
    S = proj_a.shape[0]
    nblk = S // tm
    W3 = 3 * GDN_WIDTH

    def body(cur_ref, prev_ref, ba_ref, cw_ref, al_ref, dt_ref, qn_ref, kn_ref, v_ref, gcb_ref, bb_ref):
        i = pl.program_id(0)
        prev = jnp.where(i > 0, prev_ref[...], 0.0)
        c = _conv_rows(prev, cur_ref[...], cw_ref[...], GDN_CONV)
        a = c * _sigmoid(c)
        ba = ba_ref[...]
        lane = lax.broadcasted_iota(jnp.int32, (tm, 128), 1)
        g4 = jnp.zeros((tm, 128), F32)
        for h in range(GDN_HEADS):
            sl = slice(GDN_DIM * h, GDN_DIM * (h + 1))
            qh = a[:, GDN_DIM * h:GDN_DIM * (h + 1)]
            kh = a[:, GDN_WIDTH + GDN_DIM * h:GDN_WIDTH + GDN_DIM * (h + 1)]
            qn_ref[:, sl] = qh * (lax.rsqrt(jnp.sum(qh * qh, axis=-1, keepdims=True) + EPS) * (GDN_DIM ** -0.5))
            kn_ref[:, sl] = kh * lax.rsqrt(jnp.sum(kh * kh, axis=-1, keepdims=True) + EPS)
            beta = _sigmoid(ba[:, h:h + 1])
            bb_ref[:, sl] = jnp.broadcast_to(beta, (tm, GDN_DIM))
            g = -jnp.exp(al_ref[0:1, h:h + 1]) * _softplus(ba[:, GDN_HEADS + h:GDN_HEADS + h + 1] + dt_ref[0:1, h:h + 1])
            g4 = jnp.where(lane == h, g, g4)
        v_ref[...] = a[:, 2 * GDN_WIDTH:]
        gc = _exact_tri_dot(_chunk_tri(tm), g4)
        for h in range(GDN_HEADS):
            gcb_ref[:, GDN_DIM * h:GDN_DIM * (h + 1)] = jnp.broadcast_to(gc[:, h:h + 1], (tm, GDN_DIM))

    prev_spec, _ = _halo_specs(tm, W3, 0, nblk)
    row = pl.BlockSpec((tm, GDN_WIDTH), lambda i: (i, 0))
    small = lambda a: pl.BlockSpec(a.shape, lambda i: (0, 0))
    return pl.pallas_call(
        body, name=name, grid=(nblk,),
        in_specs=[pl.BlockSpec((tm, W3), lambda i: (i, 0)), prev_spec,
                  pl.BlockSpec((tm, 128), lambda i: (i, W3 // 128)), small(conv_w), small(a_log), small(dt_bias)],
        out_specs=[row] * 5, out_shape=[_sds((S, GDN_WIDTH))] * 5, compiler_params=_params(("parallel",)),
    )(proj_a, proj_a, proj_a, conv_w, a_log, dt_bias)


GDN_STACK = GDN_HEADS * CHUNK


def _stack(ref, rows):
    return jnp.concatenate([ref[rows, GDN_DIM * h:GDN_DIM * (h + 1)] for h in range(GDN_HEADS)], axis=0)


def _unstack_to(ref, rows, x):
    for h in range(GDN_HEADS):
        ref[rows, GDN_DIM * h:GDN_DIM * (h + 1)] = x[CHUNK * h:CHUNK * (h + 1)].astype(ref.dtype)


def _stack_masks():
    r = lax.broadcasted_iota(jnp.int32, (GDN_STACK, GDN_STACK), 0)
    c = lax.broadcasted_iota(jnp.int32, (GDN_STACK, GDN_STACK), 1)
    same = (r & -CHUNK) == (c & -CHUNK)
    return same & (r >= c), same & (r > c), r == c


def _stack_decay(gs, bs, incl):
    g2 = jnp.concatenate([gs, gs], axis=1)
    diff = g2 - g2.T
    dec = jnp.where(incl, jnp.exp(jnp.where(incl, diff, 0.0)), 0.0)
    return dec, jnp.concatenate([bs, bs], axis=1).T


def _head_mask():
    r = lax.broadcasted_iota(jnp.int32, (GDN_STACK, GDN_WIDTH), 0)
    c = lax.broadcasted_iota(jnp.int32, (GDN_STACK, GDN_WIDTH), 1)
    return (r & -CHUNK) * (GDN_DIM // CHUNK) == (c & -GDN_DIM)


def _head_spread(x):
    return jnp.where(_head_mask(), jnp.concatenate([x] * GDN_HEADS, axis=1), 0.0)


def _head_diag(x):
    xm = jnp.where(_head_mask(), x, 0.0)
    out = xm[:, 0:GDN_DIM]
    for h in range(1, GDN_HEADS):
        out = out + xm[:, GDN_DIM * h:GDN_DIM * (h + 1)]
    return out


def _last_rows(gs, n):
    return jnp.concatenate([jnp.broadcast_to(gs[CHUNK * (h + 1) - 1:CHUNK * (h + 1)], (n, GDN_DIM)) for h in range(GDN_HEADS)], axis=0)


def _gdn_chunk_fwd(qn, kn, v, gcb, bb, *, name):
    S = qn.shape[0]

    def body(qn_ref, kn_ref, v_ref, gcb_ref, bb_ref, uv_ref, wk_ref, at_ref, t_ref, wkb_ref, qdb_ref, keb_ref):
        incl, strict, diag = _stack_masks()
        for c in range(CHUNKS_PER_STEP):
            rows = slice(CHUNK * c, CHUNK * (c + 1))
            srows = slice(GDN_STACK * c, GDN_STACK * (c + 1))
            q, k, vv, gs, bs = [_stack(r, rows) for r in (qn_ref, kn_ref, v_ref, gcb_ref, bb_ref)]
            dec, bt = _stack_decay(gs, bs, incl)
            p = -jnp.where(strict, dec * _bdot_nt(k, k) * bt, 0.0)
            t = jnp.where(diag, 1.0, 0.0) + p
            for _ in range(5):
                p = _bdot(p, p)
                t = t + _bdot(t, p)
            sol = _dot3(t, jnp.concatenate([vv, jnp.exp(gs) * k], axis=1), ((1,), (0,)))
            _unstack_to(uv_ref, rows, sol[:, :GDN_DIM])
            _unstack_to(wk_ref, rows, sol[:, GDN_DIM:])
            at_ref[srows, :] = dec * _bdot_nt(q, k) * bt
            t_ref[srows, :] = t
            wkb_ref[srows, :] = _head_spread(sol[:, GDN_DIM:]).astype(BF16)
            qdb_ref[srows, :] = _head_spread(q * jnp.exp(gs)).astype(BF16)
            keb_ref[srows, :] = _head_spread(k * jnp.exp(_last_rows(gs, CHUNK) - gs) * bs).astype(BF16)

    step = CHUNKS_PER_STEP * CHUNK
    row = pl.BlockSpec((step, GDN_WIDTH), lambda n: (n, 0))
    sq = pl.BlockSpec((CHUNKS_PER_STEP * GDN_STACK, GDN_STACK), lambda n: (n, 0))
    wide = pl.BlockSpec((CHUNKS_PER_STEP * GDN_STACK, GDN_WIDTH), lambda n: (n, 0))
    nsq = S // CHUNK * GDN_STACK
    return pl.pallas_call(
        body, name=name, grid=(S // step,), in_specs=[row] * 5, out_specs=[row, row, sq, sq, wide, wide, wide],
        out_shape=[_sds((S, GDN_WIDTH)), _sds((S, GDN_WIDTH)), _sds((nsq, GDN_STACK)), _sds((nsq, GDN_STACK))]
        + [_sds((nsq, GDN_WIDTH), BF16)] * 3,
        compiler_params=_params(("parallel",)),
    )(qn, kn, v, gcb, bb)


SCAN_CHUNKS = 8


def _gdn_scan_fwd(uv, at, wkb, qdb, keb, gcb, proj_z, gnw, *, name):
    S = uv.shape[0]
    nc = S // CHUNK

    def body(uv_ref, at_ref, wkb_ref, qdb_ref, keb_ref, gcb_ref, z_ref, gnw_ref, o_ref, u_ref, sp_ref, oa_ref, st_ref):
        n = pl.program_id(0)

        @pl.when(n == 0)
        def _():
            st_ref[...] = jnp.zeros_like(st_ref)

        for c in range(SCAN_CHUNKS):
            rows = slice(CHUNK * c, CHUNK * (c + 1))
            srows = slice(GDN_STACK * c, GDN_STACK * (c + 1))
            st = st_ref[...]
            sp_ref[GDN_WIDTH * c:GDN_WIDTH * (c + 1), :] = st
            uv, gs, z = [_stack(r, rows) for r in (uv_ref, gcb_ref, z_ref)]
            u = uv - _bdot(wkb_ref[srows, :], st)
            o = _bdot(qdb_ref[srows, :], st) + _bdot(at_ref[srows, :], u)
            st_ref[...] = jnp.exp(_last_rows(gs, GDN_DIM)) * st + _bdot_tn(keb_ref[srows, :], u)
            _unstack_to(u_ref, rows, u)
            _unstack_to(o_ref, rows, o)
            r = lax.rsqrt(jnp.mean(o * o, axis=-1, keepdims=True) + EPS)
            oa = o * r * gnw_ref[...] * (z * _sigmoid(z))
            oa_ref[rows, :] = jnp.concatenate([oa[CHUNK * h:CHUNK * (h + 1)] for h in range(GDN_HEADS)], axis=1).astype(BF16)

    row = pl.BlockSpec((SCAN_CHUNKS * CHUNK, GDN_WIDTH), lambda n: (n, 0))
    sq = pl.BlockSpec((SCAN_CHUNKS * GDN_STACK, GDN_STACK), lambda n: (n, 0))
    wide = pl.BlockSpec((SCAN_CHUNKS * GDN_STACK, GDN_WIDTH), lambda n: (n, 0))
    return pl.pallas_call(
        body, name=name, grid=(nc // SCAN_CHUNKS,),
        in_specs=[row, sq, wide, wide, wide, row, row, pl.BlockSpec((1, GDN_DIM), lambda n: (0, 0))],
        out_specs=[row, row, pl.BlockSpec((SCAN_CHUNKS * GDN_WIDTH, GDN_DIM), lambda n: (n, 0)), row],
        out_shape=[_sds((S, GDN_WIDTH)), _sds((S, GDN_WIDTH)), _sds((nc * GDN_WIDTH, GDN_DIM)), _sds((S, 2 * GDN_WIDTH), BF16)],
        scratch_shapes=[pltpu.VMEM((GDN_WIDTH, GDN_DIM), F32)],
        compiler_params=_params(("arbitrary",)),
    )(uv, at, wkb, qdb, keb, gcb, proj_z, gnw)


def _gdn_scan_bwd(d_oab, o, proj_z, gnw, sp, u, at, wkb, qdb, keb, gcb, *, name, after=()):
    S = o.shape[0]
    nc = S // CHUNK
    ns = nc // SCAN_CHUNKS

    def body(do_ref, o_ref, z_ref, gnw_ref, sp_ref, u_ref, at_ref, wkb_ref, qdb_ref, keb_ref, gcb_ref, *rest):
        dz_ref, dgn_ref, du_ref, dwk_ref, dat_ref, dqd_ref, dke_ref, dgl_ref, ds_ref = rest[len(after):]
        n = pl.program_id(0)

        @pl.when(n == 0)
        def _():
            ds_ref[...] = jnp.zeros_like(ds_ref)
            dgn_ref[...] = jnp.zeros_like(dgn_ref)

        gw = gnw_ref[...]
        for c in reversed(range(SCAN_CHUNKS)):
            rows = slice(CHUNK * c, CHUNK * (c + 1))
            srows = slice(GDN_STACK * c, GDN_STACK * (c + 1))
            d_oa, oo, z, uu, gs = [_stack(r, rows) for r in (do_ref, o_ref, z_ref, u_ref, gcb_ref)]
            sg = _sigmoid(z)
            r = lax.rsqrt(jnp.mean(oo * oo, axis=-1, keepdims=True) + EPS)
            xh = oo * r
            dy = d_oa * (z * sg)
            _unstack_to(dz_ref, rows, d_oa * (xh * gw) * _dsilu(z, sg))
            dgn_ref[...] += jnp.sum(dy * xh, axis=0, keepdims=True)
            dxh = dy * gw
            do = r * (dxh - xh * jnp.mean(dxh * xh, axis=-1, keepdims=True))

            st = sp_ref[GDN_WIDTH * c:GDN_WIDTH * (c + 1), :]
            dst = ds_ref[...]
            ge = jnp.exp(_last_rows(gs, GDN_DIM))
            _unstack_to(dqd_ref, rows, _head_diag(_bdot_nt(do, st)))
            dat_ref[srows, :] = _bdot_nt(do, uu)
            du = _bdot_tn(at_ref[srows, :], do) + _bdot(keb_ref[srows, :], dst)
            _unstack_to(dke_ref, rows, _head_diag(_bdot_nt(uu, dst)))
            prod = dst * st
            for h in range(GDN_HEADS):
                blk = prod[GDN_DIM * h:GDN_DIM * (h + 1)]
                dge = jnp.sum(jnp.sum(blk, axis=1, keepdims=True), axis=0, keepdims=True)
                dgl_ref[c, :, GDN_DIM * h:GDN_DIM * (h + 1)] = jnp.broadcast_to(dge * ge[GDN_DIM * h:GDN_DIM * h + 1], (8, GDN_DIM))
            ds_ref[...] = _bdot_tn(qdb_ref[srows, :], do) + ge * dst - _bdot_tn(wkb_ref[srows, :], du)
            _unstack_to(du_ref, rows, du)
            _unstack_to(dwk_ref, rows, -_head_diag(_bdot_nt(du, st)))

    rev = lambda n: (ns - 1 - n, 0)
    row = pl.BlockSpec((SCAN_CHUNKS * CHUNK, GDN_WIDTH), rev)
    sq = pl.BlockSpec((SCAN_CHUNKS * GDN_STACK, GDN_STACK), rev)
    wide = pl.BlockSpec((SCAN_CHUNKS * GDN_STACK, GDN_WIDTH), rev)
    one = pl.BlockSpec((1, GDN_DIM), lambda n: (0, 0))
    return pl.pallas_call(
        body, name=name, grid=(ns,),
        in_specs=[row, row, row, one, pl.BlockSpec((SCAN_CHUNKS * GDN_WIDTH, GDN_DIM), rev), row, sq, wide, wide, wide, row]
        + [ANY] * len(after),
        out_specs=[row, one, row, row, sq, row, row, pl.BlockSpec((SCAN_CHUNKS, 8, GDN_WIDTH), lambda n: (ns - 1 - n, 0, 0))],
        out_shape=[_sds((S, GDN_WIDTH), BF16), _sds((1, GDN_DIM)), _sds((S, GDN_WIDTH)), _sds((S, GDN_WIDTH)),
                   _sds((nc * GDN_STACK, GDN_STACK)), _sds((S, GDN_WIDTH)), _sds((S, GDN_WIDTH)), _sds((nc, 8, GDN_WIDTH))],
        scratch_shapes=[pltpu.VMEM((GDN_WIDTH, GDN_DIM), F32)],
        compiler_params=_params(("arbitrary",)),
    )(d_oab, o, proj_z, gnw, sp, u, at, wkb, qdb, keb, gcb, *after)


def _gdn_chunk_bwd(qn, kn, gcb, bb, tmat, uv, wk, du, dwk, dat, dqd, dke, dgl, *, name):
    S = qn.shape[0]

    def body(qn_ref, kn_ref, gcb_ref, bb_ref, t_ref, uv_ref, wk_ref, du_ref, dwk_ref, dat_ref, dqd_ref, dke_ref,
             dgl_ref, dq_ref, dk_ref, dv_ref, dg_ref, dbeta_ref):
        incl, strict, _ = _stack_masks()
        lane = lax.broadcasted_iota(jnp.int32, (CHUNK, 128), 1)
        rowi = lax.broadcasted_iota(jnp.int32, (CHUNK, 1), 0)
        rsum = lambda x: jnp.sum(x, axis=-1, keepdims=True)
        for c in range(CHUNKS_PER_STEP):
            rows = slice(CHUNK * c, CHUNK * (c + 1))
            srows = slice(GDN_STACK * c, GDN_STACK * (c + 1))
            q, k, gs, bs, uv, wk, du, dwk, dqd, dke = [
                _stack(r, rows) for r in (qn_ref, kn_ref, gcb_ref, bb_ref, uv_ref, wk_ref, du_ref, dwk_ref, dqd_ref, dke_ref)]
            dec, bt = _stack_decay(gs, bs, incl)
            kk = _bdot_nt(k, k)
            qk = _bdot_nt(q, k)
            d_rhs = _dot3(t_ref[srows, :], jnp.concatenate([du, dwk], axis=1), ((0,), (0,)))
            sol = jnp.concatenate([uv, wk], axis=1)
            d_l = jnp.where(strict, -_dot3(d_rhs, sol, ((1,), (1,))), 0.0)
            d_a = jnp.where(incl, dat_ref[srows, :], 0.0)
            gam = jnp.exp(gs)
            e = jnp.exp(_last_rows(gs, CHUNK) - gs)
            d_gk = d_rhs[:, GDN_DIM:]
            ml = d_l * dec * bt
            ma = d_a * dec * bt
            _unstack_to(dq_ref, rows, _bdot(ma, k) + dqd * gam)
            _unstack_to(dk_ref, rows, _bdot(ml + ml.T, k) + _bdot_tn(ma, q) + d_gk * gam + dke * (e * bs))
            _unstack_to(dv_ref, rows, d_rhs[:, :GDN_DIM])
            wb = d_l * dec * kk + d_a * dec * qk
            ew = wb * bt
            s_ke = rsum(dke * k * (e * bs))
            dbeta = rsum(wb.T) + rsum(dke * k * e)
            dgc = rsum(ew) - rsum(ew.T) + rsum(dqd * q * gam) + rsum(d_gk * k * gam) - s_ke
            dgc4 = jnp.zeros((CHUNK, 128), F32)
            db4 = jnp.zeros((CHUNK, 128), F32)
            for h in range(GDN_HEADS):
                hr = slice(CHUNK * h, CHUNK * (h + 1))
                tail = jnp.sum(s_ke[hr], axis=0, keepdims=True) + dgl_ref[c, 0:1, GDN_DIM * h:GDN_DIM * h + 1]
                dgc4 = jnp.where(lane == h, dgc[hr] + jnp.where(rowi == CHUNK - 1, tail, 0.0), dgc4)
                db4 = jnp.where(lane == h, dbeta[hr], db4)
            dg_ref[rows, :] = _exact_tri_dot(_chunk_tri(CHUNK, upper=True), dgc4)
            dbeta_ref[rows, :] = db4

    step = CHUNKS_PER_STEP * CHUNK
    row = pl.BlockSpec((step, GDN_WIDTH), lambda n: (n, 0))
    sq = pl.BlockSpec((CHUNKS_PER_STEP * GDN_STACK, GDN_STACK), lambda n: (n, 0))
    col = pl.BlockSpec((step, 128), lambda n: (n, 0))
    return pl.pallas_call(
        body, name=name, grid=(S // step,),
        in_specs=[row] * 4 + [sq, row, row, row, row, sq, row, row,
                              pl.BlockSpec((CHUNKS_PER_STEP, 8, GDN_WIDTH), lambda n: (n, 0, 0))],
        out_specs=[row, row, row, col, col],
        out_shape=[_sds((S, GDN_WIDTH))] * 3 + [_sds((S, 128))] * 2, compiler_params=_params(("parallel",)),
    )(qn, kn, gcb, bb, tmat, uv, wk, du, dwk, dat, dqd, dke, dgl)


def _gdn_prep_bwd(dqn, dkn, dv, dg, dbeta, proj_a, conv_w, a_log, dt_bias, *, name, tm=256):
    S = proj_a.shape[0]
    nblk = S // tm
    W3 = 3 * GDN_WIDTH

    def body(dqn_ref, dkn_ref, dv_ref, dg_ref, dbeta_ref, cur_ref, prev_ref, ba_ref, cw_ref, al_ref, dt_ref,
             dc_ref, dba_ref, sm_ref):
        i = pl.program_id(0)
        prev = jnp.where(i > 0, prev_ref[...], 0.0)
        c = _conv_rows(prev, cur_ref[...], cw_ref[...], GDN_CONV)
        sg = _sigmoid(c)
        a = c * sg
        dsl = _dsilu(c, sg)
        ba = ba_ref[...]
        lane = lax.broadcasted_iota(jnp.int32, (tm, 128), 1)
        lane1 = lax.broadcasted_iota(jnp.int32, (1, 128), 1)
        dba = jnp.zeros((tm, 128), F32)
        sm = jnp.zeros((1, 128), F32)
        for h in range(GDN_HEADS):
            sl = slice(GDN_DIM * h, GDN_DIM * (h + 1))
            ks = slice(GDN_WIDTH + GDN_DIM * h, GDN_WIDTH + GDN_DIM * (h + 1))
            qh, kh = a[:, sl], a[:, ks]
            rq = lax.rsqrt(jnp.sum(qh * qh, axis=-1, keepdims=True) + EPS)
            rk = lax.rsqrt(jnp.sum(kh * kh, axis=-1, keepdims=True) + EPS)
            qhat, khat = qh * rq, kh * rk
            dyq = dqn_ref[:, sl] * (GDN_DIM ** -0.5)
            dyk = dkn_ref[:, sl]
            dq = rq * (dyq - qhat * jnp.sum(dyq * qhat, axis=-1, keepdims=True))
            dk = rk * (dyk - khat * jnp.sum(dyk * khat, axis=-1, keepdims=True))
            dc_ref[:, sl] = dq * dsl[:, sl]
            dc_ref[:, ks] = dk * dsl[:, ks]
            beta = _sigmoid(ba[:, h:h + 1])
            db = dbeta_ref[:, h:h + 1] * beta * (1.0 - beta)
            aneg = -jnp.exp(al_ref[0:1, h:h + 1])
            xa = ba[:, GDN_HEADS + h:GDN_HEADS + h + 1] + dt_ref[0:1, h:h + 1]
            dgh = dg_ref[:, h:h + 1]
            dxa = dgh * aneg * _sigmoid(xa)
            dba = jnp.where(lane == h, db, dba)
            dba = jnp.where(lane == GDN_HEADS + h, dxa, dba)
            d_alog = jnp.sum(dgh * _softplus(xa), axis=0, keepdims=True) * aneg
            sm = jnp.where(lane1 == h, d_alog, sm)
            sm = jnp.where(lane1 == GDN_HEADS + h, jnp.sum(dxa, axis=0, keepdims=True), sm)
        vs = slice(2 * GDN_WIDTH, W3)
        dc_ref[:, vs] = dv_ref[...] * dsl[:, vs]
        dba_ref[...] = dba

        @pl.when(i == 0)
        def _():
            sm_ref[...] = sm

        @pl.when(i > 0)
        def _():
            sm_ref[...] += sm

    prev_spec, _ = _halo_specs(tm, W3, 0, nblk)
    row = pl.BlockSpec((tm, GDN_WIDTH), lambda i: (i, 0))
    col = pl.BlockSpec((tm, 128), lambda i: (i, 0))
    small = lambda a: pl.BlockSpec(a.shape, lambda i: (0, 0))
    return pl.pallas_call(
        body, name=name, grid=(nblk,),
        in_specs=[row, row, row, col, col, pl.BlockSpec((tm, W3), lambda i: (i, 0)), prev_spec,
                  pl.BlockSpec((tm, 128), lambda i: (i, W3 // 128)), small(conv_w), small(a_log), small(dt_bias)],
        out_specs=[pl.BlockSpec((tm, W3), lambda i: (i, 0)), col, pl.BlockSpec((1, 128), lambda i: (0, 0))],
        out_shape=[_sds((S, W3)), _sds((S, 128)), _sds((1, 128))], compiler_params=_params(("arbitrary",)),
    )(dqn, dkn, dv, dg, dbeta, proj_a, proj_a, proj_a, conv_w, a_log, dt_bias)


def _gdn_conv_bwd(dc, dba, proj_a, conv_w, *, name, tm=256):
    S = proj_a.shape[0]
    nblk = S // tm
    W3 = 3 * GDN_WIDTH

    def body(dc_ref, dnext_ref, dba_ref, cur_ref, prev_ref, cw_ref, da_ref, dcw_ref):
        i = pl.program_id(0)
        prev = jnp.where(i > 0, prev_ref[...], 0.0)
        nxt = jnp.where(i < nblk - 1, dnext_ref[...], 0.0)
        dx, dw = _conv_rows_bwd(dc_ref[...], nxt, prev, cur_ref[...], cw_ref[...], GDN_CONV)
        da_ref[:, 0:W3] = dx.astype(BF16)
        da_ref[:, W3:] = dba_ref[...].astype(BF16)

        @pl.when(i == 0)
        def _():
            dcw_ref[...] = dw

        @pl.when(i > 0)
        def _():
            dcw_ref[...] += dw

    prev_spec, next_spec = _halo_specs(tm, W3, 0, nblk)
    wide = pl.BlockSpec((tm, W3), lambda i: (i, 0))
    return pl.pallas_call(
        body, name=name, grid=(nblk,),
        in_specs=[wide, next_spec, pl.BlockSpec((tm, 128), lambda i: (i, 0)), wide, prev_spec,
                  pl.BlockSpec(conv_w.shape, lambda i: (0, 0))],
        out_specs=[pl.BlockSpec((tm, A_COLS), lambda i: (i, 0)), pl.BlockSpec(conv_w.shape, lambda i: (0, 0))],
        out_shape=[_sds((S, A_COLS), BF16), _sds(conv_w.shape)], compiler_params=_params(("arbitrary",)),
    )(dc, dc, dba, proj_a, proj_a, conv_w)


def _band_mask(nk):
    i = lax.broadcasted_iota(jnp.int32, (2 * BAND, nk), 0) & (BAND - 1)
    j = lax.broadcasted_iota(jnp.int32, (2 * BAND, nk), 1)
    if nk == BAND:
        return j <= i
    return (j >= i) & (j <= i + BAND)


def _stack_heads(x, lo):
    return jnp.concatenate([jnp.where(lo, x, 0.0), jnp.where(lo, 0.0, x)], axis=0)


def _stack_cols(x):
    return jnp.concatenate([x[:, 0:1], x[:, DIL_DIM:DIL_DIM + 1]], axis=0)


def _unstack(x, lo):
    return jnp.where(lo, x[0:BAND], x[BAND:2 * BAND])


def _rows(start, size, stride):
    return pl.ds(start, size) if stride == 1 else pl.ds(start, size, stride=stride)


ATTN_LANES = 4


def _attn_blocks(S, visit_many, lanes=ATTN_LANES):
    for d in DILATIONS:
        nb = S // (d * BAND)
        if d == 1:
            half = nb // 2
            visit_many(d, [(0, 0, True), (0, half, False)])

            def pair(n, c):
                visit_many(1, [(0, n, False), (0, n + half, False)])
                return c
            lax.fori_loop(1, half, pair, 0)
        elif nb > 1:
            for r0 in range(0, d, lanes):
                visit_many(d, [(r0 + t, 0, True) for t in range(lanes)])

                def column(n, c, d=d, r0=r0):
                    visit_many(d, [(r0 + t, n, False) for t in range(lanes)])
                    return c
                lax.fori_loop(1, nb, column, 0)
        else:
            def group(g, c, d=d):
                visit_many(d, [(g * lanes + t, 0, True) for t in range(lanes)])
                return c
            lax.fori_loop(0, d // lanes, group, 0)


def _attn_fwd(proj_b, oab, *, name):
    S = proj_b.shape[0]
    scale = DIL_DIM ** -0.5

    def body(q_ref, k_ref, v_ref, oab_in_ref, ob_ref, lse_ref, m_ref, l_ref, acc_ref):
        del oab_in_ref
        lane = lax.broadcasted_iota(jnp.int32, (BAND, 128), 1)
        lo = lane < DIL_DIM
        m_ref[...] = jnp.full_like(m_ref, NEG_BIG)
        l_ref[...] = jnp.zeros_like(l_ref)
        acc_ref[...] = jnp.zeros_like(acc_ref)

        def load(d, r, n, first):
            nk = BAND if first else 2 * BAND
            qrows = _rows(r + n * (BAND * d), BAND, d)
            krows = _rows(r if first else r + (n - 1) * (BAND * d), nk, d)
            return dict(nk=nk, qrows=qrows, q=q_ref[qrows, :] * scale, k=k_ref[krows, :].astype(BF16),
                        v=v_ref[krows, :].astype(BF16), m=m_ref[qrows, :], l=l_ref[qrows, :], acc=acc_ref[qrows, :])

        def compute(b):
            q, k, v = b["q"], b["k"], b["v"]
            s = jnp.where(_band_mask(b["nk"]), _bdot_nt(_stack_heads(q, lo), k), NEG_BIG)
            m_old = _stack_cols(b["m"])
            m_new = jnp.maximum(m_old, jnp.max(s, axis=-1, keepdims=True))
            p = jnp.exp(s - m_new)
            alpha = _unstack(jnp.exp(m_old - m_new), lo)
            l_new = alpha * b["l"] + _unstack(jnp.sum(p, axis=-1, keepdims=True), lo)
            return _unstack(m_new, lo), l_new, alpha * b["acc"] + _unstack(_bdot(p, v), lo)

        def visit_many(d, blocks):
            loaded = [load(d, *blk) for blk in blocks]
            done = [compute(b) for b in loaded]
            for b, (m_new, l_new, acc_new) in zip(loaded, done):
                m_ref[b["qrows"], :] = m_new
                l_ref[b["qrows"], :] = l_new
                acc_ref[b["qrows"], :] = acc_new

        _attn_blocks(S, visit_many)
        ob_ref[...] = (acc_ref[...] / l_ref[...]).astype(BF16)
        lse_ref[...] = m_ref[...] + jnp.log(l_ref[...])

    part = lambda t: pl.BlockSpec((S, 128), lambda p: (0, 3 * p + t))
    return pl.pallas_call(
        body, name=name, grid=(DIL_PAIRS,),
        in_specs=[part(0), part(1), part(2), pl.BlockSpec(memory_space=pl.ANY)],
        out_specs=[pl.BlockSpec((S, 128), lambda p: (0, GDN_WIDTH // 128 + p)), pl.BlockSpec((S, 128), lambda p: (0, p))],
        out_shape=[_sds(oab.shape, BF16), _sds((S, DIL_WIDTH))],
        scratch_shapes=[pltpu.VMEM((S, 128), F32)] * 3, input_output_aliases={3: 0},
        compiler_params=_params(("parallel",)),
    )(proj_b, proj_b, proj_b, oab)


def _attn_bwd(proj_b, oab, d_oab, lse, *, name):
    S = proj_b.shape[0]
    scale = DIL_DIM ** -0.5

    def body(q_ref, k_ref, v_ref, o_ref, do_ref, lse_ref, dqkv_ref, dq_ref, dk_ref, dv_ref, delta_ref):
        lane = lax.broadcasted_iota(jnp.int32, (BAND, 128), 1)
        lo = lane < DIL_DIM
        dq_ref[...] = jnp.zeros_like(dq_ref)
        dk_ref[...] = jnp.zeros_like(dk_ref)
        dv_ref[...] = jnp.zeros_like(dv_ref)
        prod = do_ref[...] * o_ref[...].astype(F32)
        lo_all = lax.broadcasted_iota(jnp.int32, (S, 128), 1) < DIL_DIM
        delta_ref[...] = jnp.where(lo_all, jnp.sum(jnp.where(lo_all, prod, 0.0), axis=-1, keepdims=True),
                                   jnp.sum(jnp.where(lo_all, 0.0, prod), axis=-1, keepdims=True))

        def load(d, r, n, first):
            nk = BAND if first else 2 * BAND
            qrows = _rows(r + n * (BAND * d), BAND, d)
            krows = _rows(r if first else r + (n - 1) * (BAND * d), nk, d)
            return dict(nk=nk, qrows=qrows, krows=krows, q=q_ref[qrows, :] * scale, k=k_ref[krows, :], v=v_ref[krows, :],
                        do=do_ref[qrows, :], delta=delta_ref[qrows, :], lse=lse_ref[qrows, :],
                        dq=dq_ref[qrows, :], dk=dk_ref[krows, :], dv=dv_ref[krows, :])

        def compute(b):
            q, k, v, do = b["q"], b["k"], b["v"], b["do"]
            qs, dos = _stack_heads(q, lo), _stack_heads(do, lo)
            p = jnp.where(_band_mask(b["nk"]), jnp.exp(_bdot_nt(qs, k) - _stack_cols(b["lse"])), 0.0)
            ds = p * (_bdot_nt(dos, v) - _stack_cols(b["delta"]))
            dq = b["dq"] + _unstack(_bdot(ds, k), lo) * scale
            return dq, b["dk"] + _bdot_tn(ds, qs), b["dv"] + _bdot_tn(p, dos)

        def visit_many(d, blocks):
            loaded = [load(d, *blk) for blk in blocks]
            done = [compute(b) for b in loaded]
            for b, (dq, dk, dv) in zip(loaded, done):
                dq_ref[b["qrows"], :] = dq
                dk_ref[b["krows"], :] = dk
                dv_ref[b["krows"], :] = dv

        _attn_blocks(S, visit_many, lanes=2)
        dqkv_ref[:, 0:128] = dq_ref[...].astype(BF16)
        dqkv_ref[:, 128:256] = dk_ref[...].astype(BF16)
        dqkv_ref[:, 256:384] = dv_ref[...].astype(BF16)

    half = lambda p: (0, GDN_WIDTH // 128 + p)
    part = lambda t: pl.BlockSpec((S, 128), lambda p: (0, 3 * p + t))
    return pl.pallas_call(
        body, name=name, grid=(DIL_PAIRS,),
        in_specs=[part(0), part(1), part(2), pl.BlockSpec((S, 128), half), pl.BlockSpec((S, 128), half),
                  pl.BlockSpec((S, 128), lambda p: (0, p))],
        out_specs=pl.BlockSpec((S, 384), lambda p: (0, p)), out_shape=_sds((S, 3 * DIL_WIDTH), BF16),
        scratch_shapes=[pltpu.VMEM((S, 128), F32)] * 4, compiler_params=_params(("parallel",)),
    )(proj_b, proj_b, proj_b, oab, d_oab, lse)


FF_SLAB = 2 * D_FF // N_DEV
FF_PAIRS = N_DEV // 2
ROWS16 = 16


def _taps(w, x, base, n):
    out = _shifted(x, base, n) * w[0:1]
    for t in range(1, FFN_CONV):
        out = out + _shifted(x, base + t, n) * w[t:t + 1]
    return out


def _ffn_fwd(h2, x1, w_up, conv_w, w_down, final_w, tgt, *, name, tm=512):
    S, D = h2.shape
    ni = S // tm
    per = tm // ROWS16

    def body(h_ref, hp_ref, x1_ref, wg_ref, wu_ref, cg_ref, cu_ref, wd_ref, fw_ref, t_ref,
             dx_ref, dxb_ref, dfw_ref, loss_ref, ug_ref, uu_ref, x2_ref):
        i, j = pl.program_id(0), pl.program_id(1)
        hv = jnp.concatenate([hp_ref[...], h_ref[...]], axis=0)
        row = lax.broadcasted_iota(jnp.int32, (tm + ROWS16, 1), 0)
        keep = (i > 0) | (row >= ROWS16)

        def branch(w_ref, c_ref, u_ref):
            u = lax.dot_general(hv, w_ref[...], (((1,), (1,)), ((), ())), preferred_element_type=F32).astype(BF16)
            u_ref[...] = u[ROWS16:]
            return _taps(c_ref[...], jnp.where(keep, u.astype(F32), 0.0), ROWS16 - (FFN_CONV - 1), tm)

        gate = branch(wg_ref, cg_ref, ug_ref)
        up = branch(wu_ref, cu_ref, uu_ref)
        act = (gate * _sigmoid(gate) * up).astype(BF16)
        part = jnp.dot(act, wd_ref[...], preferred_element_type=F32)

        @pl.when(j == 0)
        def _():
            x2_ref[...] = x1_ref[...] + part

        @pl.when((j > 0) & (j < FF_PAIRS - 1))
        def _():
            x2_ref[...] += part

        @pl.when(j == FF_PAIRS - 1)
        def _():
            xv = x2_ref[...] + part
            wv = fw_ref[...]
            r = lax.rsqrt(jnp.mean(xv * xv, axis=-1, keepdims=True) + EPS)
            err = xv * r * wv - t_ref[...]
            lsum = jnp.sum(jnp.sum(err * err, axis=-1, keepdims=True), axis=0, keepdims=True) * (0.5 / D)
            g = err * (1.0 / D)
            xh = xv * r
            gw = g * wv
            dx = r * (gw - xh * jnp.mean(gw * xh, axis=-1, keepdims=True))
            dx_ref[...] = dx
            dxb_ref[...] = dx.astype(BF16)
            dfw = jnp.sum(g * xh, axis=0, keepdims=True)
            lpart = jnp.broadcast_to(lsum, (1, 128))

            @pl.when(i == 0)
            def _():
                dfw_ref[...] = dfw
                loss_ref[...] = lpart

            @pl.when(i > 0)
            def _():
                dfw_ref[...] += dfw
                loss_ref[...] += lpart

    rows = pl.BlockSpec((tm, D), lambda i, j: (i, 0))
    slab = lambda off: pl.BlockSpec((None, FF_SLAB, D), lambda i, j: (j + off, 0, 0))
    cslab = lambda off: pl.BlockSpec((None, FFN_CONV, FF_SLAB), lambda i, j: (j + off, 0, 0))
    uspec = pl.BlockSpec((None, tm, FF_SLAB), lambda i, j: (j, i, 0))
    return pl.pallas_call(
        body, name=name, grid=(ni, FF_PAIRS),
        in_specs=[rows, pl.BlockSpec((ROWS16, D), lambda i, j: (jnp.maximum(i * per - 1, 0), 0)), rows,
                  slab(0), slab(FF_PAIRS), cslab(0), cslab(FF_PAIRS), pl.BlockSpec((FF_SLAB, D), lambda i, j: (j, 0)),
                  pl.BlockSpec((1, D), lambda i, j: (0, 0)), rows],
        out_specs=[rows, rows, pl.BlockSpec((1, D), lambda i, j: (0, 0)), pl.BlockSpec((1, 128), lambda i, j: (0, 0)), uspec, uspec],
        out_shape=[_sds((S, D)), _sds((S, D), BF16), _sds((1, D)), _sds((1, 128)),
                   _sds((FF_PAIRS, S, FF_SLAB), BF16), _sds((FF_PAIRS, S, FF_SLAB), BF16)],
        scratch_shapes=[pltpu.VMEM((tm, D), F32)],
        compiler_params=_params(("arbitrary", "arbitrary")),
    )(h2, h2, x1, w_up, w_up, conv_w, conv_w, w_down, final_w, tgt)


def _ffn_bwd(dx2, h2, ug, uu, conv_w, w_down, *, name, tm=512):
    S, D = h2.shape
    ni = S // tm
    per = tm // ROWS16
    ext = tm + ROWS16

    def body(dx_ref, dxn_ref, h_ref, ug_ref, ugp_ref, ugn_ref, uu_ref, uup_ref, uun_ref, cg_ref, cu_ref, wd_ref,
             du_ref, gd_ref, gup_ref, dcw_ref, acc_d, acc_g, acc_u, acc_cg, acc_cu):
        i = pl.program_id(1)

        @pl.when(i == 0)
        def _():
            acc_d[...] = jnp.zeros_like(acc_d)
            acc_g[...] = jnp.zeros_like(acc_g)
            acc_u[...] = jnp.zeros_like(acc_u)
            acc_cg[...] = jnp.zeros_like(acc_cg)
            acc_cu[...] = jnp.zeros_like(acc_cu)

        dx = dx_ref[...]
        dxe = jnp.concatenate([dx, dxn_ref[...]], axis=0)
        row = lax.broadcasted_iota(jnp.int32, (ext, 1), 0)
        live = (i < ni - 1) | (row < tm)
        d_act = jnp.where(live, lax.dot_general(dxe, wd_ref[...], (((1,), (1,)), ((), ())), preferred_element_type=F32), 0.0)
        rowp = lax.broadcasted_iota(jnp.int32, (ext + ROWS16, 1), 0)
        keep = (i > 0) | (rowp >= ROWS16)

        def pre(cur, prev, nxt):
            return jnp.where(keep, jnp.concatenate([prev[...], cur[...], nxt[...]], axis=0).astype(F32), 0.0)

        uge, uue = pre(ug_ref, ugp_ref, ugn_ref), pre(uu_ref, uup_ref, uun_ref)
        cg, cu = cg_ref[...], cu_ref[...]
        base = ROWS16 - (FFN_CONV - 1)
        gate = _taps(cg, uge, base, ext)
        up = _taps(cu, uue, base, ext)
        sg = _sigmoid(gate)
        silu = gate * sg
        dgc = d_act * up * _dsilu(gate, sg)
        duc = d_act * silu

        def conv_t(w, dc):
            out = _shifted(dc, FFN_CONV - 1, tm) * w[0:1]
            for t in range(1, FFN_CONV):
                out = out + _shifted(dc, FFN_CONV - 1 - t, tm) * w[t:t + 1]
            return out.astype(BF16)

        du_g, du_u = conv_t(cg, dgc), conv_t(cu, duc)
        du_ref[0] = du_g
        du_ref[1] = du_u
        dcw = lambda dc, xe: jnp.concatenate(
            [jnp.sum(dc[0:tm] * _shifted(xe, base + t, tm), axis=0, keepdims=True) for t in range(FFN_CONV)], axis=0)
        acc_cg[0:FFN_CONV, :] += dcw(dgc, uge)
        acc_cu[0:FFN_CONV, :] += dcw(duc, uue)
        tn = (((0,), (0,)), ((), ()))
        act = (silu[0:tm] * up[0:tm]).astype(BF16)
        acc_d[...] += lax.dot_general(act, dx, tn, preferred_element_type=F32)
        hv = h_ref[...]
        acc_g[...] += lax.dot_general(du_g, hv, tn, preferred_element_type=F32)
        acc_u[...] += lax.dot_general(du_u, hv, tn, preferred_element_type=F32)

        @pl.when(i == ni - 1)
        def _():
            gd_ref[...] = acc_d[...].astype(BF16)
            gup_ref[0] = acc_g[...].astype(BF16)
            gup_ref[1] = acc_u[...].astype(BF16)
            dcw_ref[0] = acc_cg[0:FFN_CONV, :]
            dcw_ref[1] = acc_cu[0:FFN_CONV, :]

    last16 = S // ROWS16 - 1
    rows = pl.BlockSpec((tm, D), lambda j, i: (i, 0))
    rows_next = pl.BlockSpec((ROWS16, D), lambda j, i: (jnp.minimum((i + 1) * per, last16), 0))
    u_cur = pl.BlockSpec((None, tm, FF_SLAB), lambda j, i: (j, i, 0))
    u_prev = pl.BlockSpec((None, ROWS16, FF_SLAB), lambda j, i: (j, jnp.maximum(i * per - 1, 0), 0))
    u_next = pl.BlockSpec((None, ROWS16, FF_SLAB), lambda j, i: (j, jnp.minimum((i + 1) * per, last16), 0))
    cslab = lambda off: pl.BlockSpec((None, FFN_CONV, FF_SLAB), lambda j, i: (j + off, 0, 0))
    return pl.pallas_call(
        body, name=name, grid=(FF_PAIRS, ni),
        in_specs=[rows, rows_next, rows, u_cur, u_prev, u_next, u_cur, u_prev, u_next, cslab(0), cslab(FF_PAIRS),
                  pl.BlockSpec((FF_SLAB, D), lambda j, i: (j, 0))],
        out_specs=[pl.BlockSpec((None, 2, tm, FF_SLAB), lambda j, i: (j, 0, i, 0)), pl.BlockSpec((FF_SLAB, D), lambda j, i: (j, 0)),
                   pl.BlockSpec((None, 2, FF_SLAB, D), lambda j, i: (j, 0, 0, 0)),
                   pl.BlockSpec((None, 2, FFN_CONV, FF_SLAB), lambda j, i: (j, 0, 0, 0))],
        out_shape=[_sds((FF_PAIRS, 2, S, FF_SLAB), BF16), _sds((D_FF, D), BF16), _sds((FF_PAIRS, 2, FF_SLAB, D), BF16),
                   _sds((FF_PAIRS, 2, FFN_CONV, FF_SLAB))],
        scratch_shapes=[pltpu.VMEM((FF_SLAB, D), F32), pltpu.VMEM((FF_SLAB, D), F32), pltpu.VMEM((FF_SLAB, D), F32),
                        pltpu.VMEM((8, FF_SLAB), F32), pltpu.VMEM((8, FF_SLAB), F32)],
        compiler_params=_params(("parallel", "arbitrary")),
    )(dx2, dx2, h2, ug, ug, ug, uu, uu, uu, conv_w, conv_w, w_down)


def _pair_slot(p):
    return 2 * (p & (FF_PAIRS - 1)) + (p >> 2)


def _mm_slabs(a, w, *, name, res=None, norm_bwd=None, after=(), tm=1024, tn=1024):
    nk, S, _ = a.shape
    D = w.shape[2]
    has_res = res is not None
    has_norm = norm_bwd is not None
    assert not has_norm or tn == D

    def body(*refs):
        a_ref, w_ref = refs[:2]
        r_ref = refs[2] if has_res else None
        if has_norm:
            x_ref, nw_ref, skip_ref = refs[2 + has_res:5 + has_res]
            o_ref, dw_ref, acc_ref = refs[-3:]
        else:
            o_ref, acc_ref = refs[-2:]
        i, k = pl.program_id(0), pl.program_id(2)
        part = jnp.dot(a_ref[...], w_ref[...], preferred_element_type=F32)

        @pl.when(k == 0)
        def _():
            acc_ref[...] = part

        @pl.when(k > 0)
        def _():
            acc_ref[...] += part

        @pl.when(k == nk - 1)
        def _():
            r = acc_ref[...] + r_ref[...] if has_res else acc_ref[...]
            if has_norm:
                dx, dw = _rms_bwd_rows(r, x_ref[...], nw_ref[...])
                o_ref[...] = skip_ref[...] + dx

                @pl.when(i == 0)
                def _():
                    dw_ref[...] = dw

                @pl.when(i > 0)
                def _():
                    dw_ref[...] += dw
            else:
                o_ref[...] = r

    o_spec = pl.BlockSpec((tm, tn), lambda i, j, k: (i, j))
    one = pl.BlockSpec((1, tn), lambda i, j, k: (0, 0))
    return pl.pallas_call(
        body, name=name, grid=(S // tm, D // tn, nk),
        in_specs=[pl.BlockSpec((None, tm, FF_SLAB), lambda i, j, k: (k, i, 0)),
                  pl.BlockSpec((None, FF_SLAB, tn), lambda i, j, k: (FF_PAIRS * (k & 1) + (k >> 1), 0, j))] + [o_spec] * has_res
        + ([o_spec, one, o_spec] if has_norm else []) + [ANY] * len(after),
        out_specs=[o_spec, one] if has_norm else o_spec, out_shape=[_sds((S, D)), _sds((1, D))] if has_norm else _sds((S, D)),
        scratch_shapes=[pltpu.VMEM((tm, tn), F32)],
        compiler_params=_params(("arbitrary" if has_norm else "parallel", "parallel", "arbitrary")),
    )(*((a, w) + ((res,) if has_res else ()) + (tuple(norm_bwd) if has_norm else ()) + tuple(after)))


def _local_step(x, tgt, norm1_w, w_a, w_z, w_b, conv_a, a_log, dt_bias, gnw, norm2_w, final_w, late_weights, emit, start_after=()):
    wgrad = functools.partial(_mm, ta=True, out_dtype=BF16)
    h1, proj_a, proj_z, proj_b = _in_proj(x, norm1_w, w_a, w_z, w_b, after=start_after, name="in_proj")
    qn, kn, v, gcb, bb = _gdn_prep_fwd(proj_a, conv_a, a_log, dt_bias, name="gdn_prep_fwd")
    uv, wk, at, tmat, wkb, qdb, keb = _gdn_chunk_fwd(qn, kn, v, gcb, bb, name="gdn_chunk_fwd")
    o, u, sp, oab = _gdn_scan_fwd(uv, at, wkb, qdb, keb, gcb, proj_z, gnw, name="gdn_scan_fwd")
    oab, lse = _attn_fwd(proj_b, oab, name="attn_fwd")
    w_out, w_up, conv_f, w_down = late_weights(oab)
    x1, h2 = _out_proj_norm(oab, w_out, x, norm2_w, name="out_proj")
    dx2, dx2_b, d_final, loss, ug, uu = _ffn_fwd(h2, x1, w_up, conv_f, w_down, final_w, tgt, name="ffn_fwd")
    du, g_down, g_up, dcw = _ffn_bwd(dx2_b, h2, ug, uu, conv_f, w_down, name="ffn_bwd")
    token = emit("ffn", w_down=g_down, w_up=g_up.reshape(N_DEV, FF_SLAB, -1), conv_f=dcw.reshape(N_DEV, FFN_CONV, -1))
    dx1, d_norm2 = _mm_slabs(du.reshape(N_DEV, -1, FF_SLAB), w_up, norm_bwd=(x1, norm2_w, dx2), after=token, name="ffn_up_dx")
    d_oab = _mm(dx1, w_out, tb=True, name="out_proj_dx", tn=D_MODEL, tk=1024)
    token = emit("out", w_out=wgrad(oab, dx1, name="out_proj_dw", tm=D_MODEL, tn=D_MODEL))
    dz, d_gnw, du, dwk, dat, dqd, dke, dgl = _gdn_scan_bwd(d_oab, o, proj_z, gnw, sp, u, at, wkb, qdb, keb, gcb, after=token, name="gdn_scan_bwd")
    dqn, dkn, dv, dg, dbeta = _gdn_chunk_bwd(qn, kn, gcb, bb, tmat, uv, wk, du, dwk, dat, dqd, dke, dgl, name="gdn_chunk_bwd")
    dc, dba, d_small = _gdn_prep_bwd(dqn, dkn, dv, dg, dbeta, proj_a, conv_a, a_log, dt_bias, name="gdn_prep_bwd")
    d_pa, d_conv_a = _gdn_conv_bwd(dc, dba, proj_a, conv_a, name="gdn_conv_bwd")
    d_pb = _attn_bwd(proj_b, oab, d_oab, lse, name="attn_bwd")
    g_a = wgrad(d_pa, h1, name="proj_a_dw", tm=A_COLS, tn=D_MODEL)
    g_z = wgrad(dz, h1, name="proj_z_dw", tn=D_MODEL)
    g_b = wgrad(d_pb, h1, name="proj_b_dw", tm=768, tn=D_MODEL)
    token = emit("in", w_a=g_a, w_z=g_z, w_b=g_b, conv_a=d_conv_a)
    grad_x, d_norm1 = _in_proj_dx((d_pa, dz, d_pb), (w_a, w_z, w_b), x, norm1_w, dx1, after=token, name="in_proj_dx")
    small = dict(norm1=d_norm1, small=d_small, gnw=d_gnw, norm2=d_norm2, final=d_final)
    return loss, grad_x, small


_O1 = 3 * GDN_WIDTH
_O2 = _O1 + GDN_WIDTH
_O3 = _O2 + 2 * GDN_HEADS


def _split_w_in(w_t):
    d = w_t.shape[1]
    pad = jnp.zeros((A_COLS - _O1 - 2 * GDN_HEADS, d), w_t.dtype)
    w_a = jnp.concatenate([w_t[:_O1], w_t[_O2:_O3], pad], axis=0)
    w_b = w_t[_O3:].reshape(3, DIL_PAIRS, 128, d).transpose(1, 0, 2, 3).reshape(3 * DIL_WIDTH, d)
    return w_a, w_t[_O1:_O2], w_b


def _merge_g_in(g_a, g_z, g_b):
    d = g_a.shape[1]
    g_b = g_b.reshape(DIL_PAIRS, 3, 128, d).transpose(1, 0, 2, 3).reshape(3 * DIL_WIDTH, d)
    return jnp.concatenate([g_a[:_O1], g_z, g_a[_O1:_O1 + 2 * GDN_HEADS], g_b], axis=0)


MESH = pl.DeviceIdType.MESH
ANY = pl.BlockSpec(memory_space=pl.ANY)


def _position():
    return lax.axis_index("x"), lax.axis_index("y"), lax.axis_index("c")


def _slot(p):
    return 4 * p[0] + 2 * p[1] + p[2]


def _all_gather(blocks, *, name):
    n = len(blocks)

    def body(*refs):
        ins, outs = refs[:n], refs[n:2 * n]
        send_sems, recv_sems, local_sems = refs[2 * n:]
        x, y, c = _position()
        me, sibling = (x, y, c), (x, y, 1 - c)
        chips = [(1 - x, y), (x, 1 - y), (1 - x, 1 - y)]

        def copy(a, k, block, to, src=None):
            dst = outs[a].at[_slot(block)]
            return pltpu.make_async_remote_copy(
                src_ref=dst if src is None else src, dst_ref=dst, send_sem=send_sems.at[a, k], recv_sem=recv_sems.at[a, k],
                device_id=to, device_id_type=MESH)

        mine = [pltpu.make_async_copy(ins[a], outs[a].at[_slot(me)], local_sems.at[a]) for a in range(n)]
        for cp in mine:
            cp.start()
        first = []
        for a in range(n):
            first.append(copy(a, 0, me, sibling, src=ins[a]))
            first += [copy(a, 1 + j, me, (*chip, c), src=ins[a]) for j, chip in enumerate(chips)]
        for cp in first:
            cp.start()
        passed = []
        for j, chip in enumerate(chips):
            for a in range(n):
                copy(a, 1 + j, (*chip, c), me).wait_recv()
                fwd = copy(a, 4 + j, (*chip, c), sibling)
                fwd.start()
                passed.append(fwd)
        for a in range(n):
            copy(a, 0, sibling, me).wait_recv()
            for j, chip in enumerate(chips):
                copy(a, 4 + j, (*chip, 1 - c), me).wait_recv()
        for cp in first + passed:
            cp.wait_send()
        for cp in mine:
            cp.wait()

    return pl.pallas_call(
        body, name=name, in_specs=[ANY] * n, out_specs=[ANY] * n,
        out_shape=[_sds((N_DEV,) + b.shape, b.dtype) for b in blocks],
        scratch_shapes=[pltpu.SemaphoreType.DMA((n, 7)), pltpu.SemaphoreType.DMA((n, 7)), pltpu.SemaphoreType.DMA((n,))],
    )(*blocks)


def _gather_direct(block, *, name, after=()):
    def body(in_ref, *rest):
        out_ref, send_sems, recv_sems, local_sem = rest[len(after):]
        x, y, c = _position()
        me = _slot((x, y, c))
        mine = pltpu.make_async_copy(in_ref, out_ref.at[me], local_sem)
        mine.start()
        copies = [pltpu.make_async_remote_copy(
            src_ref=in_ref, dst_ref=out_ref.at[me], send_sem=send_sems.at[k - 1], recv_sem=recv_sems.at[k - 1],
            device_id=_peer_of(k, x, y, c), device_id_type=MESH) for k in range(1, N_DEV)]
        for cp in copies:
            cp.start()
        for cp in copies:
            cp.wait()
        mine.wait()

    return pl.pallas_call(
        body, name=name, in_specs=[pl.BlockSpec(memory_space=pltpu.VMEM)] + [ANY] * len(after),
        out_specs=pl.BlockSpec(memory_space=pltpu.VMEM),
        out_shape=_sds((N_DEV,) + block.shape, block.dtype),
        scratch_shapes=[pltpu.SemaphoreType.DMA((N_DEV - 1,)), pltpu.SemaphoreType.DMA((N_DEV - 1,)), pltpu.SemaphoreType.DMA],
    )(block, *after)


HBM = pl.BlockSpec(memory_space=pltpu.HBM)
SEM = pl.BlockSpec(memory_space=pltpu.SEMAPHORE)
EFFECT = pltpu.SideEffectType.DATAFLOW_SIDE_EFFECTING


def _peer_of(k, x, y, c):
    return (1 - x if k & 4 else x, 1 - y if k & 2 else y, 1 - c if k & 1 else c)


def _flight(a, k):
    return a * (N_DEV - 1) + k - 1


def _exchange_start(arrays, *, name, broadcast=False, paired=()):
    n = len(arrays)

    def body(*refs):
        ins, lands = refs[:n], refs[n:2 * n]
        send_sems, recv_sems = refs[2 * n:2 * n + 2]
        token = refs[-1]
        x, y, c = _position()
        me = _slot((x, y, c))
        for k in range(1, N_DEV):
            peer = _peer_of(k, x, y, c)
            for a in range(n):
                at = _pair_slot(_slot(peer)) if a in paired else _slot(peer)
                pltpu.make_async_remote_copy(
                    src_ref=ins[a] if broadcast else ins[a].at[at], dst_ref=lands[a].at[me],
                    send_sem=send_sems.at[_flight(a, k)], recv_sem=recv_sems.at[_flight(a, k)],
                    device_id=peer, device_id_type=MESH).start()
        token[...] = jnp.zeros_like(token)

    land_shapes = [((N_DEV,) + s.shape) if broadcast else s.shape for s in arrays]
    lands = [pltpu.with_memory_space_constraint(lax.empty(shp, s.dtype), pltpu.HBM) for shp, s in zip(land_shapes, arrays)]
    srcs = [pltpu.with_memory_space_constraint(s, pltpu.HBM) for s in arrays]
    outs = pl.pallas_call(
        body, name=name, in_specs=[HBM] * (2 * n),
        out_specs=[SEM, SEM] + [HBM] * (2 * n) + [pl.BlockSpec(memory_space=pltpu.VMEM)],
        out_shape=[pltpu.SemaphoreType.DMA((n * (N_DEV - 1),)), pltpu.SemaphoreType.DMA((n * (N_DEV - 1),))]
        + [pltpu.HBM(s.shape, s.dtype) for s in arrays] + [pltpu.HBM(shp, s.dtype) for shp, s in zip(land_shapes, arrays)]
        + [_sds((8, 128))],
        input_output_aliases={i: 2 + i for i in range(2 * n)},
        compiler_params=pltpu.CompilerParams(has_side_effects=EFFECT),
    )(*srcs, *lands)
    return outs[0], outs[1], outs[2:2 + n], outs[2 + n:2 + 2 * n], outs[-1]


def _exchange_wait(send_sems, recv_sems, srcs, lands, after, *, name, broadcast=False):
    n = len(srcs)

    def body(*refs):
        ins, lnd = refs[:n], refs[n:2 * n]
        send_ref, recv_ref = refs[2 * n:2 * n + 2]
        x, y, c = _position()
        for k in range(1, N_DEV):
            for a in range(n):
                cp = pltpu.make_async_remote_copy(
                    src_ref=ins[a] if broadcast else ins[a].at[0], dst_ref=lnd[a].at[0], send_sem=send_ref.at[_flight(a, k)],
                    recv_sem=recv_ref.at[_flight(a, k)], device_id=_peer_of(k, x, y, c), device_id_type=MESH)
                cp.wait_send()
                cp.wait_recv()

    outs = pl.pallas_call(
        body, name=name, in_specs=[HBM] * (2 * n) + [SEM, SEM, ANY], out_specs=[HBM] * (2 * n),
        out_shape=[pltpu.HBM(s.shape, s.dtype) for s in srcs] + [pltpu.HBM(s.shape, s.dtype) for s in lands],
        input_output_aliases={i: i for i in range(2 * n)},
        compiler_params=pltpu.CompilerParams(has_side_effects=EFFECT),
    )(*srcs, *lands, send_sems, recv_sems, after)
    return outs[:n], outs[n:]


def _with_own(landed, srcs, me):
    return [lax.dynamic_update_index_in_dim(l, o, me, 0) for l, o in zip(landed, srcs)]


def _adam_update(g, w, m, v):
    c1 = 1.0 - ADAM_B1 ** ADAM_STEP
    c2 = 1.0 - ADAM_B2 ** ADAM_STEP
    nm = ADAM_B1 * m + (1.0 - ADAM_B1) * g
    nv = ADAM_B2 * v + (1.0 - ADAM_B2) * (g * g)
    return -ADAM_LR * ((nm / c1) / (jnp.sqrt(nv / c2) + ADAM_EPS) + ADAM_WD * w), nm, nv


def _adamw(landed, sent, me, w, m, v, *, name, tr=None, tc=None):
    R, C = w.shape
    tr = R if tr is None else tr
    tc = C if tc is None else tc
    assert R % tr == 0 and C % tc == 0

    def body(me_ref, own_ref, p_ref, w_ref, m_ref, v_ref, g_ref, d_ref, nm_ref, nv_ref, token_ref):
        token_ref[...] = jnp.zeros_like(token_ref)
        g = own_ref[...].astype(F32)
        for s in range(N_DEV):
            g = g + jnp.where(me_ref[1] == s, 0.0, p_ref[s].astype(F32))
        delta, nm, nv = _adam_update(g, w_ref[...], m_ref[...], v_ref[...])
        g_ref[...] = g
        nm_ref[...] = nm
        nv_ref[...] = nv
        d_ref[...] = delta

    blk = pl.BlockSpec((tr, tc), lambda i, j, me_ref: (i, j))
    return pl.pallas_call(
        body, name=name,
        grid_spec=pltpu.PrefetchScalarGridSpec(
            num_scalar_prefetch=1, grid=(R // tr, C // tc),
            in_specs=[pl.BlockSpec((None, tr, tc), lambda i, j, me_ref: (me_ref[0], i, j)),
                      pl.BlockSpec((N_DEV, tr, tc), lambda i, j, me_ref: (0, i, j)), blk, blk, blk],
            out_specs=[blk] * 4 + [pl.BlockSpec((8, 128), lambda i, j, me_ref: (0, 0))]),
        out_shape=[_sds((R, C))] * 4 + [_sds((8, 128))],
        compiler_params=_params(("arbitrary", "arbitrary")),
    )(me, sent, landed, w, m, v)


def _adamw_rowwise(landed, sent, me, w, m, v, *, name, tr=128):
    _, R, C = landed.shape
    q = C // 128

    def body(me_ref, own_ref, p_ref, w_ref, m_ref, v_ref, g_ref, d_ref, nm_ref, nv_ref):
        g = own_ref[...].astype(F32)
        for s in range(N_DEV):
            g = g + jnp.where(me_ref[1] == s, 0.0, p_ref[s].astype(F32))
        for s in range(q):
            rows = pl.ds(s, tr, stride=q)
            gs = g[:, 128 * s:128 * (s + 1)]
            delta, nm, nv = _adam_update(gs, w_ref[rows, :], m_ref[rows, :], v_ref[rows, :])
            g_ref[rows, :] = gs
            nm_ref[rows, :] = nm
            nv_ref[rows, :] = nv
            d_ref[rows, :] = delta

    blk = pl.BlockSpec((tr * q, 128), lambda i, me_ref: (i, 0))
    return pl.pallas_call(
        body, name=name,
        grid_spec=pltpu.PrefetchScalarGridSpec(
            num_scalar_prefetch=1, grid=(pl.cdiv(R, tr),),
            in_specs=[pl.BlockSpec((None, tr, C), lambda i, me_ref: (me_ref[0], i, 0)),
                      pl.BlockSpec((N_DEV, tr, C), lambda i, me_ref: (0, i, 0)), blk, blk, blk],
            out_specs=[blk] * 4),
        out_shape=[_sds((R * q, 128))] * 4,
        compiler_params=_params(("arbitrary",)),
    )(me, sent, landed, w, m, v)


_SMALL_ROWS = 8
_SMALL_SLOTS =((0, 0, D_MODEL), (1, 0, D_MODEL), (2, 0, D_MODEL), (3, 0, GDN_DIM), (3, GDN_DIM, GDN_HEADS),
                (3, GDN_DIM + GDN_HEADS, GDN_HEADS))
_LOSS_LANE = 2 * GDN_DIM


def _pack_small(norm1, norm2, final, gnw, a_log, dt_bias, loss):
    row3 = jnp.concatenate([gnw, a_log, dt_bias, jnp.zeros((1, 128 - 2 * GDN_HEADS), F32), loss,
                            jnp.zeros((1, D_MODEL - 3 * 128), F32)], axis=1)
    return jnp.concatenate([norm1, norm2, final, row3, jnp.zeros((_SMALL_ROWS - 4, D_MODEL), F32)], axis=0)


def _adamw_small(packs, ws, ms, vs, *, name):
    n = len(ws)

    def body(p_ref, *refs):
        w_refs, m_refs, v_refs = refs[:n], refs[n:2 * n], refs[2 * n:3 * n]
        outs = refs[3 * n:]
        g_all = p_ref[0]
        for s in range(1, N_DEV):
            g_all = g_all + p_ref[s]
        for i, (row, lane, width) in enumerate(_SMALL_SLOTS):
            g = g_all[row:row + 1, lane:lane + width]
            delta, nm, nv = _adam_update(g, w_refs[i][...], m_refs[i][...], v_refs[i][...])
            for o_ref, val in zip(outs[4 * i:4 * i + 4], (g, delta, nm, nv)):
                o_ref[...] = val
        outs[-1][...] = g_all[3:4, _LOSS_LANE:_LOSS_LANE + 128]

    vm = pl.BlockSpec(memory_space=pltpu.VMEM)
    outs = pl.pallas_call(
        body, name=name, in_specs=[vm] * (1 + 3 * n), out_specs=[vm] * (4 * n + 1),
        out_shape=[_sds(w.shape) for w in ws for _ in range(4)] + [_sds((1, 128))],
    )(packs, *ws, *ms, *vs)
    return [outs[4 * i:4 * i + 4] for i in range(n)], outs[-1]


def _slabs_by_cols(g):
    r = g.shape[0]
    return g.reshape(r, N_DEV, -1).transpose(1, 0, 2)


def _cols_from_slabs(s):
    return s.transpose(1, 0, 2).reshape(s.shape[1], -1)


def kernel(x, norm1_w, w_in, conv_qkv_w, a_log, dt_bias, gdn_norm_w, w_out, norm2_w, w_up, ffn_conv_w, w_down, final_norm_w, loss_target, m_norm1_w, m_w_in, m_conv_qkv_w, m_a_log, m_dt_bias, m_gdn_norm_w, m_w_out, m_norm2_w, m_w_up, m_ffn_conv_w, m_w_down, m_final_norm_w, v_norm1_w, v_w_in, v_conv_qkv_w, v_a_log, v_dt_bias, v_gdn_norm_w, v_w_out, v_norm2_w, v_w_up, v_ffn_conv_w, v_w_down, v_final_norm_w):
    bf = lambda a: a.astype(BF16)
    me = _slot(_position())
    t_in = lambda a: a[0].T
    gw_in, g_conv_a = _all_gather([bf(t_in(w_in)), conv_qkv_w[0]], name="gather_w_in")
    w_a, w_z, w_b = _split_w_in(gw_in.reshape(-1, D_MODEL))
    late_src, _ = lax.optimization_barrier(([bf(w_out[0]), bf(t_in(w_up)), bf(w_down[0]), ffn_conv_w[0]], gw_in))
    l_send, l_recv, l_srcs, l_lands, l_token = _exchange_start(late_src, name="weights_start", broadcast=True)

    def late_weights(after):
        srcs, landed = _exchange_wait(l_send, l_recv, l_srcs, l_lands, after, name="weights_wait", broadcast=True)
        gw_out, gw_up, gw_down, g_conv_f = _with_own(landed, srcs, me)
        return gw_out.reshape(D_MODEL, D_MODEL), gw_up, g_conv_f, gw_down.reshape(D_FF, D_MODEL)

    flights = {}

    def emit(group, **grads):
        paired = ()
        if group == "in":
            slabs = dict(w_in=_merge_g_in(grads["w_a"], grads["w_z"], grads["w_b"]).reshape(N_DEV, -1, D_MODEL),
                         conv_a=_slabs_by_cols(grads["conv_a"]))
        elif group == "ffn":
            slabs = dict(w_down=grads["w_down"].reshape(N_DEV, -1, D_MODEL), w_up=grads["w_up"], conv_f=grads["conv_f"])
            paired = (1, 2)
        else:
            slabs = {k: v.reshape(N_DEV, -1, D_MODEL) for k, v in grads.items()}
        names = list(slabs)
        *flight, token = _exchange_start([slabs[k] for k in names], paired=paired, name="grads_start_" + group)
        flights[group] = (names, flight)
        return (token,)

    loss, grad_x, g = _local_step(
        x[0], loss_target[0], norm1_w, w_a, w_z, w_b, _cols_from_slabs(g_conv_a), a_log, dt_bias,
        gdn_norm_w, norm2_w, final_norm_w[None], late_weights, emit, start_after=(l_token,))
    got = {}
    me1 = jnp.reshape(me, (1,)).astype(jnp.int32)

    def collect(group, after):
        names, (send_sems, recv_sems, srcs, lands) = flights[group]
        srcs, landed = _exchange_wait(send_sems, recv_sems, srcs, lands, after, name="grads_wait_" + group)
        got.update(zip(names, zip(landed, srcs)))

    def update(key, w, m, v, paired=False, **tiles):
        where = jnp.concatenate([_pair_slot(me1) if paired else me1, me1])
        return _adamw(*got[key], where, w, m, v, name="adamw_" + key, **tiles)

    collect("ffn", grad_x)
    collect("out", grad_x)
    *o_out, t1 = update("w_out", w_out[0], m_w_out[0], v_w_out[0])
    *o_up, t2 = update("w_up", t_in(w_up), t_in(m_w_up), t_in(v_w_up), paired=True, tr=176)
    o_up = [o.T for o in o_up]
    *o_down, t3 = update("w_down", w_down[0], m_w_down[0], v_w_down[0], tr=176)
    *o_cf, t4 = update("conv_f", ffn_conv_w[0], m_ffn_conv_w[0], v_ffn_conv_w[0], paired=True)
    pack = _pack_small(g["norm1"], g["norm2"], g["final"], g["gnw"], g["small"][:, 0:GDN_HEADS],
                       g["small"][:, GDN_HEADS:2 * GDN_HEADS], loss)
    small_all = _gather_direct(pack, after=(t1, t2, t3, t4), name="gather_small")
    collect("in", small_all)
    rows = lambda a: a.reshape(D_MODEL // 128, 128, -1).transpose(2, 0, 1).reshape(-1, 128)
    o_in = [o.reshape(-1, D_MODEL // 128, 128).transpose(1, 2, 0).reshape(D_MODEL, -1) for o in _adamw_rowwise(
        *got["w_in"], jnp.concatenate([me1, me1]), rows(w_in), rows(m_w_in), rows(v_w_in), name="adamw_w_in")]
    o_ca = update("conv_a", conv_qkv_w[0], m_conv_qkv_w[0], v_conv_qkv_w[0])
    (o_n1, o_n2, o_fin, o_gn, o_al, o_dt), total = _adamw_small(
        small_all, (norm1_w, norm2_w, final_norm_w[None], gdn_norm_w, a_log, dt_bias),
        (m_norm1_w, m_norm2_w, m_final_norm_w[None], m_gdn_norm_w, m_a_log, m_dt_bias),
        (v_norm1_w, v_norm2_w, v_final_norm_w[None], v_gdn_norm_w, v_a_log, v_dt_bias), name="adamw_small")
    outs = [total[0, 0], grad_x[None]]
    for k in range(4):
        outs += [o_n1[k], o_in[k][None], o_ca[k][None], o_al[k], o_dt[k], o_gn[k], o_out[k][None], o_n2[k], o_up[k][None],
                 o_cf[k][None], o_down[k][None], o_fin[k][0]]
    return tuple(outs)
```

```python
import functools

import jax
import jax.numpy as jnp
from jax import lax
from jax.experimental import pallas as pl
from jax.experimental.pallas import tpu as pltpu

F32 = jnp.float32
BF16 = jnp.bfloat16

N_DEV = 8
D_MODEL = 1024
GDN_HEADS = 4
GDN_DIM = 128
GDN_WIDTH = GDN_HEADS * GDN_DIM
GDN_CONV = 4
CHUNK = 64
CHUNKS_PER_STEP = 4
DIL_HEADS = 8
DIL_DIM = 64
DIL_WIDTH = DIL_HEADS * DIL_DIM
DIL_PAIRS = DIL_HEADS // 2
DILATIONS = (1, 4, 16)
BAND = 128
D_FF = 2816
FFN_CONV = 3
EPS = 1e-6
A_COLS = 3 * GDN_WIDTH + 128
HALO = 8

ADAM_LR = 0.001
ADAM_B1 = 0.9
ADAM_B2 = 0.999
ADAM_EPS = 1e-08
ADAM_WD = 0.01
ADAM_STEP = 10

VMEM_LIMIT_BYTES = 56 * 1024 * 1024
NEG_BIG = -1e30


def _params(sem=None):
    return pltpu.CompilerParams(dimension_semantics=sem, vmem_limit_bytes=VMEM_LIMIT_BYTES)


def _sds(shape, dtype=F32):
    return jax.ShapeDtypeStruct(shape, dtype)


def _bdot(a, b):
    return jnp.dot(a.astype(BF16), b.astype(BF16), preferred_element_type=F32)


def _bdot_nt(a, b):
    return lax.dot_general(a.astype(BF16), b.astype(BF16), (((1,), (1,)), ((), ())), preferred_element_type=F32)


def _bdot_tn(a, b):
    return lax.dot_general(a.astype(BF16), b.astype(BF16), (((0,), (0,)), ((), ())), preferred_element_type=F32)


def _split(a):
    hi = a.astype(BF16)
    lo = (a - hi.astype(F32)).astype(BF16)
    return hi, lo


def _dot3(a, b, dims):
    ah, al = _split(a)
    bh, bl = _split(b)
    d = functools.partial(lax.dot_general, dimension_numbers=(dims, ((), ())), preferred_element_type=F32)
    return d(ah, bh) + (d(al, bh) + d(ah, bl))


def _exact_tri_dot(tri, g):
    g1 = g.astype(BF16)
    r1 = g - g1.astype(F32)
    g2 = r1.astype(BF16)
    g3 = (r1 - g2.astype(F32)).astype(BF16)
    t = tri.astype(BF16)
    d = functools.partial(jnp.dot, preferred_element_type=F32)
    return d(t, g1) + (d(t, g2) + d(t, g3))


def _sigmoid(x):
    return 1.0 / (1.0 + jnp.exp(-x))


def _dsilu(x, sg):
    return sg * (1.0 + x * (1.0 - sg))


def _rms_bwd_rows(dh, x, w):
    r = lax.rsqrt(jnp.mean(x * x, axis=-1, keepdims=True) + EPS)
    xh = x * r
    gw = dh * w
    return r * (gw - xh * jnp.mean(gw * xh, axis=-1, keepdims=True)), jnp.sum(dh * xh, axis=0, keepdims=True)


def _mm(a, b, *, name, ta=False, tb=False, res=None, norm_bwd=None, after=(), out_dtype=F32, tm=512, tn=512, tk=512):
    if ta:
        K, M = a.shape
    else:
        M, K = a.shape
    if tb:
        N, Kb = b.shape
    else:
        Kb, N = b.shape
    assert K == Kb, (a.shape, b.shape)
    tm, tn, tk = min(tm, M), min(tn, N), min(tk, K)
    assert M % tm == 0 and N % tn == 0 and K % tk == 0, (name, M, N, K, tm, tn, tk)
    nk = K // tk
    dims = (((0 if ta else 1,), (1 if tb else 0,)), ((), ()))
    has_res = res is not None
    has_norm = norm_bwd is not None
    assert not has_norm or tn == N

    def body(*refs):
        a_ref, b_ref = refs[:2]
        r_ref = refs[2] if has_res else None
        if has_norm:
            x_ref, w_ref, skip_ref = refs[2 + has_res:5 + has_res]
            o_ref, dw_ref, acc_ref = refs[-3:]
        else:
            o_ref, acc_ref = refs[-2:]
        i, k = pl.program_id(0), pl.program_id(2)
        part = lax.dot_general(a_ref[...].astype(BF16), b_ref[...].astype(BF16), dims, preferred_element_type=F32)

        @pl.when(k == 0)
        def _():
            acc_ref[...] = part

        @pl.when(k > 0)
        def _():
            acc_ref[...] += part

        @pl.when(k == nk - 1)
        def _():
            r = acc_ref[...]
            if has_res:
                r = r + r_ref[...]
            if has_norm:
                dx, dw = _rms_bwd_rows(r, x_ref[...], w_ref[...])
                o_ref[...] = skip_ref[...] + dx

                @pl.when(i == 0)
                def _():
                    dw_ref[...] = dw

                @pl.when(i > 0)
                def _():
                    dw_ref[...] += dw
            else:
                o_ref[...] = r.astype(out_dtype)

    a_spec = pl.BlockSpec((tk, tm), lambda i, j, k: (k, i)) if ta else pl.BlockSpec((tm, tk), lambda i, j, k: (i, k))
    b_spec = pl.BlockSpec((tn, tk), lambda i, j, k: (j, k)) if tb else pl.BlockSpec((tk, tn), lambda i, j, k: (k, j))
    o_spec = pl.BlockSpec((tm, tn), lambda i, j, k: (i, j))
    one = pl.BlockSpec((1, tn), lambda i, j, k: (0, 0))
    in_specs = [a_spec, b_spec] + [o_spec] * has_res + ([o_spec, one, o_spec] if has_norm else []) + [ANY] * len(after)
    args = (a, b) + ((res,) if has_res else ()) + (tuple(norm_bwd) if has_norm else ()) + tuple(after)
    return pl.pallas_call(
        body, name=name, grid=(M // tm, N // tn, nk), in_specs=in_specs,
        out_specs=[o_spec, one] if has_norm else o_spec,
        out_shape=[_sds((M, N)), _sds((1, N))] if has_norm else _sds((M, N), out_dtype),
        scratch_shapes=[pltpu.VMEM((tm, tn), F32)],
        compiler_params=_params(("arbitrary" if has_norm else "parallel", "parallel", "arbitrary")),
    )(*args)


def _in_proj(x, norm_w, w_land, *, name, after=(), tm=512):
    S, D = x.shape

    def body(x_ref, nw_ref, land_ref, *rest):
        h_ref, pa_ref, pz_ref, pb_ref, *scratch = rest[len(after):]

        @pl.when(pl.program_id(0) == 0)
        def _():
            _fetch_w_in(land_ref, *scratch)

        xv = x_ref[...]
        r = lax.rsqrt(jnp.mean(xv * xv, axis=-1, keepdims=True) + EPS)
        h = (xv * r * nw_ref[...]).astype(BF16)
        h_ref[...] = h
        for w_ref, p_ref in zip(scratch[:3], (pa_ref, pz_ref, pb_ref)):
            p_ref[...] = lax.dot_general(h, w_ref[...], (((1,), (1,)), ((), ())), preferred_element_type=F32)

    row = lambda n: pl.BlockSpec((tm, n), lambda i: (i, 0))
    full = lambda a: pl.BlockSpec(a.shape, lambda i: (0, 0))
    return pl.pallas_call(
        body, name=name, grid=(S // tm,), in_specs=[row(D), full(norm_w), ANY] + [ANY] * len(after),
        out_specs=[row(D)] + [row(n) for n in _W_IN_ROWS],
        out_shape=[_sds((S, D), BF16)] + [_sds((S, n)) for n in _W_IN_ROWS],
        scratch_shapes=_w_in_scratch(D), compiler_params=_params(("arbitrary",)),
    )(x, norm_w, w_land, *after)


def _in_proj_dx(ds, w_land, x, norm_w, skip, *, name, after=(), tm=512):
    S, D = x.shape
    n = len(ds)

    def body(*refs):
        d_refs, land_ref = refs[:n], refs[n]
        x_ref, nw_ref, skip_ref = refs[n + 1:n + 4]
        o_ref, dw_ref, *scratch = refs[n + 4 + len(after):]
        w_refs = scratch[:n]
        i = pl.program_id(0)

        @pl.when(i == 0)
        def _():
            _fetch_w_in(land_ref, *scratch)

        dh = jnp.dot(d_refs[0][...], w_refs[0][...], preferred_element_type=F32)
        for d_ref, w_ref in zip(d_refs[1:], w_refs[1:]):
            dh = dh + jnp.dot(d_ref[...], w_ref[...], preferred_element_type=F32)
        dx, dw = _rms_bwd_rows(dh, x_ref[...], nw_ref[...])
        o_ref[...] = skip_ref[...] + dx

        @pl.when(i == 0)
        def _():
            dw_ref[...] = dw

        @pl.when(i > 0)
        def _():
            dw_ref[...] += dw

    row = lambda c: pl.BlockSpec((tm, c), lambda i: (i, 0))
    full = lambda a: pl.BlockSpec(a.shape, lambda i: (0, 0))
    return pl.pallas_call(
        body, name=name, grid=(S // tm,),
        in_specs=[row(d.shape[1]) for d in ds] + [ANY, row(D), full(norm_w), row(D)] + [ANY] * len(after),
        out_specs=[row(D), pl.BlockSpec((1, D), lambda i: (0, 0))], out_shape=[_sds((S, D)), _sds((1, D))],
        scratch_shapes=_w_in_scratch(D), compiler_params=_params(("arbitrary",)),
    )(*ds, w_land, x, norm_w, skip, *after)


def _out_proj_norm(a, w, x, norm_w, *, name, tm=512):
    S, D = x.shape

    def body(a_ref, w_ref, x_ref, nw_ref, x1_ref, h_ref):
        x1 = x_ref[...] + jnp.dot(a_ref[...], w_ref[...], preferred_element_type=F32)
        x1_ref[...] = x1
        r = lax.rsqrt(jnp.mean(x1 * x1, axis=-1, keepdims=True) + EPS)
        h_ref[...] = (x1 * r * nw_ref[...]).astype(BF16)

    row = pl.BlockSpec((tm, D), lambda i: (i, 0))
    return pl.pallas_call(
        body, name=name, grid=(S // tm,),
        in_specs=[pl.BlockSpec((tm, a.shape[1]), lambda i: (i, 0)), pl.BlockSpec(w.shape, lambda i: (0, 0)), row,
                  pl.BlockSpec((1, D), lambda i: (0, 0))],
        out_specs=[row, row], out_shape=[_sds((S, D)), _sds((S, D), BF16)], compiler_params=_params(("parallel",)),
    )(a, w, x, norm_w)


def _shifted(x, start, n):
    aligned = -(-start // HALO) * HALO
    assert aligned + n <= x.shape[0], (start, n, x.shape)
    return (x if aligned == start else pltpu.roll(x, aligned - start, axis=0))[aligned:aligned + n]


def _conv_rows(prev, cur, w, taps):
    n = cur.shape[0]
    xs = jnp.concatenate([prev, cur], axis=0)
    base = HALO - (taps - 1)
    out = _shifted(xs, base, n) * w[0:1]
    for i in range(1, taps):
        out = out + _shifted(xs, base + i, n) * w[i:i + 1]
    return out


def _conv_rows_bwd(cur_d, next_d, prev_x, cur_x, w, taps):
    n = cur_d.shape[0]
    ds = jnp.concatenate([cur_d, next_d], axis=0)
    dx = _shifted(ds, taps - 1, n) * w[0:1]
    for i in range(1, taps):
        dx = dx + _shifted(ds, taps - 1 - i, n) * w[i:i + 1]
    xs = jnp.concatenate([prev_x, cur_x], axis=0)
    base = HALO - (taps - 1)
    dws = [jnp.sum(cur_d * _shifted(xs, base + i, n), axis=0, keepdims=True) for i in range(taps)]
    return dx, jnp.concatenate(dws, axis=0)


def _halo_specs(tm, width, col, nblk):
    per = tm // HALO
    prev = pl.BlockSpec((HALO, width), lambda i, *_: (jnp.maximum(i * per - 1, 0), col))
    nxt = pl.BlockSpec((HALO, width), lambda i, *_: (jnp.minimum((i + 1) * per, nblk * per - 1), col))
    return prev, nxt


def _softplus(x):
    return jnp.maximum(x, 0.0) + jnp.log1p(jnp.exp(-jnp.abs(x)))


def _chunk_tri(tm, upper=False):
    r = lax.broadcasted_iota(jnp.int32, (tm, tm), 0)
    c = lax.broadcasted_iota(jnp.int32, (tm, tm), 1)
    same = lax.div(r, CHUNK) == lax.div(c, CHUNK)
    order = (c >= r) if upper else (c <= r)
    return jnp.where(same & order, 1.0, 0.0)


def _gdn_prep_fwd(proj_a, conv_w, a_log, dt_bias, *, name, tm=256):
    S = proj_a.shape[0]
    nblk = S // tm
    W3 = 3 * GDN_WIDTH

    def body(cur_ref, prev_ref, ba_ref, cw_ref, al_ref, dt_ref, qn_ref, kn_ref, v_ref, gcb_ref, bb_ref):
        i = pl.program_id(0)
        prev = jnp.where(i > 0, prev_ref[...], 0.0)
        c = _conv_rows(prev, cur_ref[...], cw_ref[...], GDN_CONV)
        a = c * _sigmoid(c)
        ba = ba_ref[...]
        lane = lax.broadcasted_iota(jnp.int32, (tm, 128), 1)
        g4 = jnp.zeros((tm, 128), F32)
        for h in range(GDN_HEADS):
            sl = slice(GDN_DIM * h, GDN_DIM * (h + 1))
            qh = a[:, GDN_DIM * h:GDN_DIM * (h + 1)]
            kh = a[:, GDN_WIDTH + GDN_DIM * h:GDN_WIDTH + GDN_DIM * (h + 1)]
            qn_ref[:, sl] = qh * (lax.rsqrt(jnp.sum(qh * qh, axis=-1, keepdims=True) + EPS) * (GDN_DIM ** -0.5))
            kn_ref[:, sl] = kh * lax.rsqrt(jnp.sum(kh * kh, axis=-1, keepdims=True) + EPS)
            beta = _sigmoid(ba[:, h:h + 1])
            bb_ref[:, sl] = jnp.broadcast_to(beta, (tm, GDN_DIM))
            g = -jnp.exp(al_ref[0:1, h:h + 1]) * _softplus(ba[:, GDN_HEADS + h:GDN_HEADS + h + 1] + dt_ref[0:1, h:h + 1])
            g4 = jnp.where(lane == h, g, g4)
        v_ref[...] = a[:, 2 * GDN_WIDTH:]
        gc = _exact_tri_dot(_chunk_tri(tm), g4)
        for h in range(GDN_HEADS):
            gcb_ref[:, GDN_DIM * h:GDN_DIM * (h + 1)] = jnp.broadcast_to(gc[:, h:h + 1], (tm, GDN_DIM))

    prev_spec, _ = _halo_specs(tm, W3, 0, nblk)
    row = pl.BlockSpec((tm, GDN_WIDTH), lambda i: (i, 0))
    small = lambda a: pl.BlockSpec(a.shape, lambda i: (0, 0))
    return pl.pallas_call(
        body, name=name, grid=(nblk,),
        in_specs=[pl.BlockSpec((tm, W3), lambda i: (i, 0)), prev_spec,
                  pl.BlockSpec((tm, 128), lambda i: (i, W3 // 128)), small(conv_w), small(a_log), small(dt_bias)],
        out_specs=[row] * 5, out_shape=[_sds((S, GDN_WIDTH))] * 5, compiler_params=_params(("parallel",)),
    )(proj_a, proj_a, proj_a, conv_w, a_log, dt_bias)


GDN_STACK = GDN_HEADS * CHUNK


def _stack(ref, rows):
    return jnp.concatenate([ref[rows, GDN_DIM * h:GDN_DIM * (h + 1)] for h in range(GDN_HEADS)], axis=0)


def _unstack_to(ref, rows, x):
    for h in range(GDN_HEADS):
        ref[rows, GDN_DIM * h:GDN_DIM * (h + 1)] = x[CHUNK * h:CHUNK * (h + 1)].astype(ref.dtype)


def _stack_masks():
    r = lax.broadcasted_iota(jnp.int32, (GDN_STACK, GDN_STACK), 0)
    c = lax.broadcasted_iota(jnp.int32, (GDN_STACK, GDN_STACK), 1)
    same = (r & -CHUNK) == (c & -CHUNK)
    return same & (r >= c), same & (r > c), r == c


def _stack_decay(gs, bs, incl):
    g2 = jnp.concatenate([gs, gs], axis=1)
    diff = g2 - g2.T
    dec = jnp.where(incl, jnp.exp(jnp.where(incl, diff, 0.0)), 0.0)
    return dec, jnp.concatenate([bs, bs], axis=1).T


def _head_mask():
    r = lax.broadcasted_iota(jnp.int32, (GDN_STACK, GDN_WIDTH), 0)
    c = lax.broadcasted_iota(jnp.int32, (GDN_STACK, GDN_WIDTH), 1)
    return (r & -CHUNK) * (GDN_DIM // CHUNK) == (c & -GDN_DIM)


def _head_spread(x):
    return jnp.where(_head_mask(), jnp.concatenate([x] * GDN_HEADS, axis=1), 0.0)


def _head_diag(x):
    xm = jnp.where(_head_mask(), x, 0.0)
    out = xm[:, 0:GDN_DIM]
    for h in range(1, GDN_HEADS):
        out = out + xm[:, GDN_DIM * h:GDN_DIM * (h + 1)]
    return out


def _last_rows(gs, n):
    return jnp.concatenate([jnp.broadcast_to(gs[CHUNK * (h + 1) - 1:CHUNK * (h + 1)], (n, GDN_DIM)) for h in range(GDN_HEADS)], axis=0)


def _gdn_chunk_fwd(qn, kn, v, gcb, bb, *, name):
    S = qn.shape[0]

    def body(qn_ref, kn_ref, v_ref, gcb_ref, bb_ref, uv_ref, wk_ref, at_ref, t_ref, wkb_ref, qdb_ref, keb_ref):
        incl, strict, diag = _stack_masks()
        for c in range(CHUNKS_PER_STEP):
            rows = slice(CHUNK * c, CHUNK * (c + 1))
            srows = slice(GDN_STACK * c, GDN_STACK * (c + 1))
            q, k, vv, gs, bs = [_stack(r, rows) for r in (qn_ref, kn_ref, v_ref, gcb_ref, bb_ref)]
            dec, bt = _stack_decay(gs, bs, incl)
            p = -jnp.where(strict, dec * _bdot_nt(k, k) * bt, 0.0)
            t = jnp.where(diag, 1.0, 0.0) + p
            for _ in range(5):
                p = _bdot(p, p)
                t = t + _bdot(t, p)
            sol = _dot3(t, jnp.concatenate([vv, jnp.exp(gs) * k], axis=1), ((1,), (0,)))
            _unstack_to(uv_ref, rows, sol[:, :GDN_DIM])
            _unstack_to(wk_ref, rows, sol[:, GDN_DIM:])
            at_ref[srows, :] = dec * _bdot_nt(q, k) * bt
            t_ref[srows, :] = t
            wkb_ref[srows, :] = _head_spread(sol[:, GDN_DIM:]).astype(BF16)
            qdb_ref[srows, :] = _head_spread(q * jnp.exp(gs)).astype(BF16)
            keb_ref[srows, :] = _head_spread(k * jnp.exp(_last_rows(gs, CHUNK) - gs) * bs).astype(BF16)

    step = CHUNKS_PER_STEP * CHUNK
    row = pl.BlockSpec((step, GDN_WIDTH), lambda n: (n, 0))
    sq = pl.BlockSpec((CHUNKS_PER_STEP * GDN_STACK, GDN_STACK), lambda n: (n, 0))
    wide = pl.BlockSpec((CHUNKS_PER_STEP * GDN_STACK, GDN_WIDTH), lambda n: (n, 0))
    nsq = S // CHUNK * GDN_STACK
    return pl.pallas_call(
        body, name=name, grid=(S // step,), in_specs=[row] * 5, out_specs=[row, row, sq, sq, wide, wide, wide],
        out_shape=[_sds((S, GDN_WIDTH)), _sds((S, GDN_WIDTH)), _sds((nsq, GDN_STACK)), _sds((nsq, GDN_STACK))]
        + [_sds((nsq, GDN_WIDTH), BF16)] * 3,
        compiler_params=_params(("parallel",)),
    )(qn, kn, v, gcb, bb)


SCAN_CHUNKS = 8


def _gdn_scan_fwd(uv, at, wkb, qdb, keb, gcb, proj_z, gnw, *, name):
    S = uv.shape[0]
    nc = S // CHUNK

    def body(uv_ref, at_ref, wkb_ref, qdb_ref, keb_ref, gcb_ref, z_ref, gnw_ref, o_ref, u_ref, sp_ref, oa_ref, st_ref):
        n = pl.program_id(0)

        @pl.when(n == 0)
        def _():
            st_ref[...] = jnp.zeros_like(st_ref)

        for c in range(SCAN_CHUNKS):
            rows = slice(CHUNK * c, CHUNK * (c + 1))
            srows = slice(GDN_STACK * c, GDN_STACK * (c + 1))
            st = st_ref[...]
            sp_ref[GDN_WIDTH * c:GDN_WIDTH * (c + 1), :] = st
            uv, gs, z = [_stack(r, rows) for r in (uv_ref, gcb_ref, z_ref)]
            u = uv - _bdot(wkb_ref[srows, :], st)
            o = _bdot(qdb_ref[srows, :], st) + _bdot(at_ref[srows, :], u)
            st_ref[...] = jnp.exp(_last_rows(gs, GDN_DIM)) * st + _bdot_tn(keb_ref[srows, :], u)
            _unstack_to(u_ref, rows, u)
            _unstack_to(o_ref, rows, o)
            r = lax.rsqrt(jnp.mean(o * o, axis=-1, keepdims=True) + EPS)
            oa = o * r * gnw_ref[...] * (z * _sigmoid(z))
            oa_ref[rows, :] = jnp.concatenate([oa[CHUNK * h:CHUNK * (h + 1)] for h in range(GDN_HEADS)], axis=1).astype(BF16)

    row = pl.BlockSpec((SCAN_CHUNKS * CHUNK, GDN_WIDTH), lambda n: (n, 0))
    sq = pl.BlockSpec((SCAN_CHUNKS * GDN_STACK, GDN_STACK), lambda n: (n, 0))
    wide = pl.BlockSpec((SCAN_CHUNKS * GDN_STACK, GDN_WIDTH), lambda n: (n, 0))
    return pl.pallas_call(
        body, name=name, grid=(nc // SCAN_CHUNKS,),
        in_specs=[row, sq, wide, wide, wide, row, row, pl.BlockSpec((1, GDN_DIM), lambda n: (0, 0))],
        out_specs=[row, row, pl.BlockSpec((SCAN_CHUNKS * GDN_WIDTH, GDN_DIM), lambda n: (n, 0)), row],
        out_shape=[_sds((S, GDN_WIDTH)), _sds((S, GDN_WIDTH)), _sds((nc * GDN_WIDTH, GDN_DIM)), _sds((S, 2 * GDN_WIDTH), BF16)],
        scratch_shapes=[pltpu.VMEM((GDN_WIDTH, GDN_DIM), F32)],
        compiler_params=_params(("arbitrary",)),
    )(uv, at, wkb, qdb, keb, gcb, proj_z, gnw)


def _gdn_scan_bwd(d_oab, o, proj_z, gnw, sp, u, at, wkb, qdb, keb, gcb, *, name, after=()):
    S = o.shape[0]
    nc = S // CHUNK
    ns = nc // SCAN_CHUNKS

    def body(do_ref, o_ref, z_ref, gnw_ref, sp_ref, u_ref, at_ref, wkb_ref, qdb_ref, keb_ref, gcb_ref, *rest):
        dz_ref, dgn_ref, du_ref, dwk_ref, dat_ref, dqd_ref, dke_ref, dgl_ref, ds_ref = rest[len(after):]
        n = pl.program_id(0)

        @pl.when(n == 0)
        def _():
            ds_ref[...] = jnp.zeros_like(ds_ref)
            dgn_ref[...] = jnp.zeros_like(dgn_ref)

        gw = gnw_ref[...]
        for c in reversed(range(SCAN_CHUNKS)):
            rows = slice(CHUNK * c, CHUNK * (c + 1))
            srows = slice(GDN_STACK * c, GDN_STACK * (c + 1))
            d_oa, oo, z, uu, gs = [_stack(r, rows) for r in (do_ref, o_ref, z_ref, u_ref, gcb_ref)]
            sg = _sigmoid(z)
            r = lax.rsqrt(jnp.mean(oo * oo, axis=-1, keepdims=True) + EPS)
            xh = oo * r
            dy = d_oa * (z * sg)
            _unstack_to(dz_ref, rows, d_oa * (xh * gw) * _dsilu(z, sg))
            dgn_ref[...] += jnp.sum(dy * xh, axis=0, keepdims=True)
            dxh = dy * gw
            do = r * (dxh - xh * jnp.mean(dxh * xh, axis=-1, keepdims=True))

            st = sp_ref[GDN_WIDTH * c:GDN_WIDTH * (c + 1), :]
            dst = ds_ref[...]
            ge = jnp.exp(_last_rows(gs, GDN_DIM))
            _unstack_to(dqd_ref, rows, _head_diag(_bdot_nt(do, st)))
            dat_ref[srows, :] = _bdot_nt(do, uu)
            du = _bdot_tn(at_ref[srows, :], do) + _bdot(keb_ref[srows, :], dst)
            _unstack_to(dke_ref, rows, _head_diag(_bdot_nt(uu, dst)))
            prod = dst * st
            for h in range(GDN_HEADS):
                blk = prod[GDN_DIM * h:GDN_DIM * (h + 1)]
                dge = jnp.sum(jnp.sum(blk, axis=1, keepdims=True), axis=0, keepdims=True)
                dgl_ref[c, :, GDN_DIM * h:GDN_DIM * (h + 1)] = jnp.broadcast_to(dge * ge[GDN_DIM * h:GDN_DIM * h + 1], (8, GDN_DIM))
            ds_ref[...] = _bdot_tn(qdb_ref[srows, :], do) + ge * dst - _bdot_tn(wkb_ref[srows, :], du)
            _unstack_to(du_ref, rows, du)
            _unstack_to(dwk_ref, rows, -_head_diag(_bdot_nt(du, st)))

    rev = lambda n: (ns - 1 - n, 0)
    row = pl.BlockSpec((SCAN_CHUNKS * CHUNK, GDN_WIDTH), rev)
    sq = pl.BlockSpec((SCAN_CHUNKS * GDN_STACK, GDN_STACK), rev)
    wide = pl.BlockSpec((SCAN_CHUNKS * GDN_STACK, GDN_WIDTH), rev)
    one = pl.BlockSpec((1, GDN_DIM), lambda n: (0, 0))
    return pl.pallas_call(
        body, name=name, grid=(ns,),
        in_specs=[row, row, row, one, pl.BlockSpec((SCAN_CHUNKS * GDN_WIDTH, GDN_DIM), rev), row, sq, wide, wide, wide, row]
        + [ANY] * len(after),
        out_specs=[row, one, row, row, sq, row, row, pl.BlockSpec((SCAN_CHUNKS, 8, GDN_WIDTH), lambda n: (ns - 1 - n, 0, 0))],
        out_shape=[_sds((S, GDN_WIDTH), BF16), _sds((1, GDN_DIM)), _sds((S, GDN_WIDTH)), _sds((S, GDN_WIDTH)),
                   _sds((nc * GDN_STACK, GDN_STACK)), _sds((S, GDN_WIDTH)), _sds((S, GDN_WIDTH)), _sds((nc, 8, GDN_WIDTH))],
        scratch_shapes=[pltpu.VMEM((GDN_WIDTH, GDN_DIM), F32)],
        compiler_params=_params(("arbitrary",)),
    )(d_oab, o, proj_z, gnw, sp, u, at, wkb, qdb, keb, gcb, *after)


def _gdn_chunk_bwd(qn, kn, gcb, bb, tmat, uv, wk, du, dwk, dat, dqd, dke, dgl, *, name):
    S = qn.shape[0]

    def body(qn_ref, kn_ref, gcb_ref, bb_ref, t_ref, uv_ref, wk_ref, du_ref, dwk_ref, dat_ref, dqd_ref, dke_ref,
             dgl_ref, dq_ref, dk_ref, dv_ref, dg_ref, dbeta_ref):
        incl, strict, _ = _stack_masks()
        lane = lax.broadcasted_iota(jnp.int32, (CHUNK, 128), 1)
        rowi = lax.broadcasted_iota(jnp.int32, (CHUNK, 1), 0)
        rsum = lambda x: jnp.sum(x, axis=-1, keepdims=True)
        for c in range(CHUNKS_PER_STEP):
            rows = slice(CHUNK * c, CHUNK * (c + 1))
            srows = slice(GDN_STACK * c, GDN_STACK * (c + 1))
            q, k, gs, bs, uv, wk, du, dwk, dqd, dke = [
                _stack(r, rows) for r in (qn_ref, kn_ref, gcb_ref, bb_ref, uv_ref, wk_ref, du_ref, dwk_ref, dqd_ref, dke_ref)]
            dec, bt = _stack_decay(gs, bs, incl)
            kk = _bdot_nt(k, k)
            qk = _bdot_nt(q, k)
            d_rhs = _dot3(t_ref[srows, :], jnp.concatenate([du, dwk], axis=1), ((0,), (0,)))
            sol = jnp.concatenate([uv, wk], axis=1)
            d_l = jnp.where(strict, -_dot3(d_rhs, sol, ((1,), (1,))), 0.0)
            d_a = jnp.where(incl, dat_ref[srows, :], 0.0)
            gam = jnp.exp(gs)
            e = jnp.exp(_last_rows(gs, CHUNK) - gs)
            d_gk = d_rhs[:, GDN_DIM:]
            ml = d_l * dec * bt
            ma = d_a * dec * bt
            _unstack_to(dq_ref, rows, _bdot(ma, k) + dqd * gam)
            _unstack_to(dk_ref, rows, _bdot(ml + ml.T, k) + _bdot_tn(ma, q) + d_gk * gam + dke * (e * bs))
            _unstack_to(dv_ref, rows, d_rhs[:, :GDN_DIM])
            wb = d_l * dec * kk + d_a * dec * qk
            ew = wb * bt
            s_ke = rsum(dke * k * (e * bs))
            dbeta = rsum(wb.T) + rsum(dke * k * e)
            dgc = rsum(ew) - rsum(ew.T) + rsum(dqd * q * gam) + rsum(d_gk * k * gam) - s_ke
            dgc4 = jnp.zeros((CHUNK, 128), F32)
            db4 = jnp.zeros((CHUNK, 128), F32)
            for h in range(GDN_HEADS):
                hr = slice(CHUNK * h, CHUNK * (h + 1))
                tail = jnp.sum(s_ke[hr], axis=0, keepdims=True) + dgl_ref[c, 0:1, GDN_DIM * h:GDN_DIM * h + 1]
                dgc4 = jnp.where(lane == h, dgc[hr] + jnp.where(rowi == CHUNK - 1, tail, 0.0), dgc4)
                db4 = jnp.where(lane == h, dbeta[hr], db4)
            dg_ref[rows, :] = _exact_tri_dot(_chunk_tri(CHUNK, upper=True), dgc4)
            dbeta_ref[rows, :] = db4

    step = CHUNKS_PER_STEP * CHUNK
    row = pl.BlockSpec((step, GDN_WIDTH), lambda n: (n, 0))
    sq = pl.BlockSpec((CHUNKS_PER_STEP * GDN_STACK, GDN_STACK), lambda n: (n, 0))
    col = pl.BlockSpec((step, 128), lambda n: (n, 0))
    return pl.pallas_call(
        body, name=name, grid=(S // step,),
        in_specs=[row] * 4 + [sq, row, row, row, row, sq, row, row,
                              pl.BlockSpec((CHUNKS_PER_STEP, 8, GDN_WIDTH), lambda n: (n, 0, 0))],
        out_specs=[row, row, row, col, col],
        out_shape=[_sds((S, GDN_WIDTH))] * 3 + [_sds((S, 128))] * 2, compiler_params=_params(("parallel",)),
    )(qn, kn, gcb, bb, tmat, uv, wk, du, dwk, dat, dqd, dke, dgl)


def _gdn_prep_bwd(dqn, dkn, dv, dg, dbeta, proj_a, conv_w, a_log, dt_bias, *, name, tm=256):
    S = proj_a.shape[0]
    nblk = S // tm
    W3 = 3 * GDN_WIDTH

    def body(dqn_ref, dkn_ref, dv_ref, dg_ref, dbeta_ref, cur_ref, prev_ref, ba_ref, cw_ref, al_ref, dt_ref,
             dc_ref, dba_ref, sm_ref):
        i = pl.program_id(0)
        prev = jnp.where(i > 0, prev_ref[...], 0.0)
        c = _conv_rows(prev, cur_ref[...], cw_ref[...], GDN_CONV)
        sg = _sigmoid(c)
        a = c * sg
        dsl = _dsilu(c, sg)
        ba = ba_ref[...]
        lane = lax.broadcasted_iota(jnp.int32, (tm, 128), 1)
        lane1 = lax.broadcasted_iota(jnp.int32, (1, 128), 1)
        dba = jnp.zeros((tm, 128), F32)
        sm = jnp.zeros((1, 128), F32)
        for h in range(GDN_HEADS):
            sl = slice(GDN_DIM * h, GDN_DIM * (h + 1))
            ks = slice(GDN_WIDTH + GDN_DIM * h, GDN_WIDTH + GDN_DIM * (h + 1))
            qh, kh = a[:, sl], a[:, ks]
            rq = lax.rsqrt(jnp.sum(qh * qh, axis=-1, keepdims=True) + EPS)
            rk = lax.rsqrt(jnp.sum(kh * kh, axis=-1, keepdims=True) + EPS)
            qhat, khat = qh * rq, kh * rk
            dyq = dqn_ref[:, sl] * (GDN_DIM ** -0.5)
            dyk = dkn_ref[:, sl]
            dq = rq * (dyq - qhat * jnp.sum(dyq * qhat, axis=-1, keepdims=True))
            dk = rk * (dyk - khat * jnp.sum(dyk * khat, axis=-1, keepdims=True))
            dc_ref[:, sl] = dq * dsl[:, sl]
            dc_ref[:, ks] = dk * dsl[:, ks]
            beta = _sigmoid(ba[:, h:h + 1])
            db = dbeta_ref[:, h:h + 1] * beta * (1.0 - beta)
            aneg = -jnp.exp(al_ref[0:1, h:h + 1])
            xa = ba[:, GDN_HEADS + h:GDN_HEADS + h + 1] + dt_ref[0:1, h:h + 1]
            dgh = dg_ref[:, h:h + 1]
            dxa = dgh * aneg * _sigmoid(xa)
            dba = jnp.where(lane == h, db, dba)
            dba = jnp.where(lane == GDN_HEADS + h, dxa, dba)
            d_alog = jnp.sum(dgh * _softplus(xa), axis=0, keepdims=True) * aneg
            sm = jnp.where(lane1 == h, d_alog, sm)
            sm = jnp.where(lane1 == GDN_HEADS + h, jnp.sum(dxa, axis=0, keepdims=True), sm)
        vs = slice(2 * GDN_WIDTH, W3)
        dc_ref[:, vs] = dv_ref[...] * dsl[:, vs]
        dba_ref[...] = dba

        @pl.when(i == 0)
        def _():
            sm_ref[...] = sm

        @pl.when(i > 0)
        def _():
            sm_ref[...] += sm

    prev_spec, _ = _halo_specs(tm, W3, 0, nblk)
    row = pl.BlockSpec((tm, GDN_WIDTH), lambda i: (i, 0))
    col = pl.BlockSpec((tm, 128), lambda i: (i, 0))
    small = lambda a: pl.BlockSpec(a.shape, lambda i: (0, 0))
    return pl.pallas_call(
        body, name=name, grid=(nblk,),
        in_specs=[row, row, row, col, col, pl.BlockSpec((tm, W3), lambda i: (i, 0)), prev_spec,
                  pl.BlockSpec((tm, 128), lambda i: (i, W3 // 128)), small(conv_w), small(a_log), small(dt_bias)],
        out_specs=[pl.BlockSpec((tm, W3), lambda i: (i, 0)), col, pl.BlockSpec((1, 128), lambda i: (0, 0))],
        out_shape=[_sds((S, W3)), _sds((S, 128)), _sds((1, 128))], compiler_params=_params(("arbitrary",)),
    )(dqn, dkn, dv, dg, dbeta, proj_a, proj_a, proj_a, conv_w, a_log, dt_bias)


def _gdn_conv_bwd(dc, dba, proj_a, conv_w, *, name, tm=256):
    S = proj_a.shape[0]
    nblk = S // tm
    W3 = 3 * GDN_WIDTH

    def body(dc_ref, dnext_ref, dba_ref, cur_ref, prev_ref, cw_ref, da_ref, dcw_ref):
        i = pl.program_id(0)
        prev = jnp.where(i > 0, prev_ref[...], 0.0)
        nxt = jnp.where(i < nblk - 1, dnext_ref[...], 0.0)
        dx, dw = _conv_rows_bwd(dc_ref[...], nxt, prev, cur_ref[...], cw_ref[...], GDN_CONV)
        da_ref[:, 0:W3] = dx.astype(BF16)
        da_ref[:, W3:] = dba_ref[...].astype(BF16)

        @pl.when(i == 0)
        def _():
            dcw_ref[...] = dw

        @pl.when(i > 0)
        def _():
            dcw_ref[...] += dw

    prev_spec, next_spec = _halo_specs(tm, W3, 0, nblk)
    wide = pl.BlockSpec((tm, W3), lambda i: (i, 0))
    return pl.pallas_call(
        body, name=name, grid=(nblk,),
        in_specs=[wide, next_spec, pl.BlockSpec((tm, 128), lambda i: (i, 0)), wide, prev_spec,
                  pl.BlockSpec(conv_w.shape, lambda i: (0, 0))],
        out_specs=[pl.BlockSpec((tm, A_COLS), lambda i: (i, 0)), pl.BlockSpec(conv_w.shape, lambda i: (0, 0))],
        out_shape=[_sds((S, A_COLS), BF16), _sds(conv_w.shape)], compiler_params=_params(("arbitrary",)),
    )(dc, dc, dba, proj_a, proj_a, conv_w)


def _band_mask(nk):
    i = lax.broadcasted_iota(jnp.int32, (2 * BAND, nk), 0) & (BAND - 1)
    j = lax.broadcasted_iota(jnp.int32, (2 * BAND, nk), 1)
    if nk == BAND:
        return j <= i
    return (j >= i) & (j <= i + BAND)


def _stack_heads(x, lo):
    return jnp.concatenate([jnp.where(lo, x, 0.0), jnp.where(lo, 0.0, x)], axis=0)


def _stack_cols(x):
    return jnp.concatenate([x[:, 0:1], x[:, DIL_DIM:DIL_DIM + 1]], axis=0)


def _unstack(x, lo):
    return jnp.where(lo, x[0:BAND], x[BAND:2 * BAND])


def _rows(start, size, stride):
    return pl.ds(start, size) if stride == 1 else pl.ds(start, size, stride=stride)


ATTN_LANES = 4


def _attn_blocks(S, visit_many, lanes=ATTN_LANES):
    for d in DILATIONS:
        nb = S // (d * BAND)
        if d == 1:
            half = nb // 2
            visit_many(d, [(0, 0, True), (0, half, False)])

            def pair(n, c):
                visit_many(1, [(0, n, False), (0, n + half, False)])
                return c
            lax.fori_loop(1, half, pair, 0)
        elif nb > 1:
            for r0 in range(0, d, lanes):
                visit_many(d, [(r0 + t, 0, True) for t in range(lanes)])

                def column(n, c, d=d, r0=r0):
                    visit_many(d, [(r0 + t, n, False) for t in range(lanes)])
                    return c
                lax.fori_loop(1, nb, column, 0)
        else:
            def group(g, c, d=d):
                visit_many(d, [(g * lanes + t, 0, True) for t in range(lanes)])
                return c
            lax.fori_loop(0, d // lanes, group, 0)


def _attn_fwd(proj_b, oab, *, name):
    S = proj_b.shape[0]
    scale = DIL_DIM ** -0.5

    def body(q_ref, k_ref, v_ref, oab_in_ref, ob_ref, lse_ref, m_ref, l_ref, acc_ref):
        del oab_in_ref
        lane = lax.broadcasted_iota(jnp.int32, (BAND, 128), 1)
        lo = lane < DIL_DIM
        m_ref[...] = jnp.full_like(m_ref, NEG_BIG)
        l_ref[...] = jnp.zeros_like(l_ref)
        acc_ref[...] = jnp.zeros_like(acc_ref)

        def load(d, r, n, first):
            nk = BAND if first else 2 * BAND
            qrows = _rows(r + n * (BAND * d), BAND, d)
            krows = _rows(r if first else r + (n - 1) * (BAND * d), nk, d)
            return dict(nk=nk, qrows=qrows, q=q_ref[qrows, :] * scale, k=k_ref[krows, :].astype(BF16),
                        v=v_ref[krows, :].astype(BF16), m=m_ref[qrows, :], l=l_ref[qrows, :], acc=acc_ref[qrows, :])

        def compute(b):
            q, k, v = b["q"], b["k"], b["v"]
            s = jnp.where(_band_mask(b["nk"]), _bdot_nt(_stack_heads(q, lo), k), NEG_BIG)
            m_old = _stack_cols(b["m"])
            m_new = jnp.maximum(m_old, jnp.max(s, axis=-1, keepdims=True))
            p = jnp.exp(s - m_new)
            alpha = _unstack(jnp.exp(m_old - m_new), lo)
            l_new = alpha * b["l"] + _unstack(jnp.sum(p, axis=-1, keepdims=True), lo)
            return _unstack(m_new, lo), l_new, alpha * b["acc"] + _unstack(_bdot(p, v), lo)

        def visit_many(d, blocks):
            loaded = [load(d, *blk) for blk in blocks]
            done = [compute(b) for b in loaded]
            for b, (m_new, l_new, acc_new) in zip(loaded, done):
                m_ref[b["qrows"], :] = m_new
                l_ref[b["qrows"], :] = l_new
                acc_ref[b["qrows"], :] = acc_new

        _attn_blocks(S, visit_many)
        ob_ref[...] = (acc_ref[...] / l_ref[...]).astype(BF16)
        lse_ref[...] = m_ref[...] + jnp.log(l_ref[...])

    part = lambda t: pl.BlockSpec((S, 128), lambda p: (0, 3 * p + t))
    return pl.pallas_call(
        body, name=name, grid=(DIL_PAIRS,),
        in_specs=[part(0), part(1), part(2), pl.BlockSpec(memory_space=pl.ANY)],
        out_specs=[pl.BlockSpec((S, 128), lambda p: (0, GDN_WIDTH // 128 + p)), pl.BlockSpec((S, 128), lambda p: (0, p))],
        out_shape=[_sds(oab.shape, BF16), _sds((S, DIL_WIDTH))],
        scratch_shapes=[pltpu.VMEM((S, 128), F32)] * 3, input_output_aliases={3: 0},
        compiler_params=_params(("parallel",)),
    )(proj_b, proj_b, proj_b, oab)


def _attn_bwd(proj_b, oab, d_oab, lse, *, name):
    S = proj_b.shape[0]
    scale = DIL_DIM ** -0.5

    def body(q_ref, k_ref, v_ref, o_ref, do_ref, lse_ref, dqkv_ref, dq_ref, dk_ref, dv_ref, delta_ref):
        lane = lax.broadcasted_iota(jnp.int32, (BAND, 128), 1)
        lo = lane < DIL_DIM
        dq_ref[...] = jnp.zeros_like(dq_ref)
        dk_ref[...] = jnp.zeros_like(dk_ref)
        dv_ref[...] = jnp.zeros_like(dv_ref)
        prod = do_ref[...] * o_ref[...].astype(F32)
        lo_all = lax.broadcasted_iota(jnp.int32, (S, 128), 1) < DIL_DIM
        delta_ref[...] = jnp.where(lo_all, jnp.sum(jnp.where(lo_all, prod, 0.0), axis=-1, keepdims=True),
                                   jnp.sum(jnp.where(lo_all, 0.0, prod), axis=-1, keepdims=True))

        def load(d, r, n, first):
            nk = BAND if first else 2 * BAND
            qrows = _rows(r + n * (BAND * d), BAND, d)
            krows = _rows(r if first else r + (n - 1) * (BAND * d), nk, d)
            return dict(nk=nk, qrows=qrows, krows=krows, q=q_ref[qrows, :] * scale, k=k_ref[krows, :], v=v_ref[krows, :],
                        do=do_ref[qrows, :], delta=delta_ref[qrows, :], lse=lse_ref[qrows, :],
                        dq=dq_ref[qrows, :], dk=dk_ref[krows, :], dv=dv_ref[krows, :])

        def compute(b):
            q, k, v, do = b["q"], b["k"], b["v"], b["do"]
            qs, dos = _stack_heads(q, lo), _stack_heads(do, lo)
            p = jnp.where(_band_mask(b["nk"]), jnp.exp(_bdot_nt(qs, k) - _stack_cols(b["lse"])), 0.0)
            ds = p * (_bdot_nt(dos, v) - _stack_cols(b["delta"]))
            dq = b["dq"] + _unstack(_bdot(ds, k), lo) * scale
            return dq, b["dk"] + _bdot_tn(ds, qs), b["dv"] + _bdot_tn(p, dos)

        def visit_many(d, blocks):
            loaded = [load(d, *blk) for blk in blocks]
            done = [compute(b) for b in loaded]
            for b, (dq, dk, dv) in zip(loaded, done):
                dq_ref[b["qrows"], :] = dq
                dk_ref[b["krows"], :] = dk
                dv_ref[b["krows"], :] = dv

        _attn_blocks(S, visit_many, lanes=2)
        dqkv_ref[:, 0:128] = dq_ref[...].astype(BF16)
        dqkv_ref[:, 128:256] = dk_ref[...].astype(BF16)
        dqkv_ref[:, 256:384] = dv_ref[...].astype(BF16)

    half = lambda p: (0, GDN_WIDTH // 128 + p)
    part = lambda t: pl.BlockSpec((S, 128), lambda p: (0, 3 * p + t))
    return pl.pallas_call(
        body, name=name, grid=(DIL_PAIRS,),
        in_specs=[part(0), part(1), part(2), pl.BlockSpec((S, 128), half), pl.BlockSpec((S, 128), half),
                  pl.BlockSpec((S, 128), lambda p: (0, p))],
        out_specs=pl.BlockSpec((S, 384), lambda p: (0, p)), out_shape=_sds((S, 3 * DIL_WIDTH), BF16),
        scratch_shapes=[pltpu.VMEM((S, 128), F32)] * 4, compiler_params=_params(("parallel",)),
    )(proj_b, proj_b, proj_b, oab, d_oab, lse)


FF_SLAB = 2 * D_FF // N_DEV
FF_PAIRS = N_DEV // 2
ROWS16 = 16


def _taps(w, x, base, n):
    out = _shifted(x, base, n) * w[0:1]
    for t in range(1, FFN_CONV):
        out = out + _shifted(x, base + t, n) * w[t:t + 1]
    return out


def _ffn_fwd(h2, x1, w_up, conv_w, w_down, final_w, tgt, *, name, tm=512):
    S, D = h2.shape
    ni = S // tm
    per = tm // ROWS16

    def body(h_ref, hp_ref, x1_ref, wg_ref, wu_ref, cg_ref, cu_ref, wd_ref, fw_ref, t_ref,
             dx_ref, dxb_ref, dfw_ref, loss_ref, ug_ref, uu_ref, x2_ref):
        i, j = pl.program_id(0), pl.program_id(1)
        hv = jnp.concatenate([hp_ref[...], h_ref[...]], axis=0)
        row = lax.broadcasted_iota(jnp.int32, (tm + ROWS16, 1), 0)
        keep = (i > 0) | (row >= ROWS16)

        def branch(w_ref, c_ref, u_ref):
            u = lax.dot_general(hv, w_ref[...], (((1,), (1,)), ((), ())), preferred_element_type=F32).astype(BF16)
            u_ref[...] = u[ROWS16:]
            return _taps(c_ref[...], jnp.where(keep, u.astype(F32), 0.0), ROWS16 - (FFN_CONV - 1), tm)

        gate = branch(wg_ref, cg_ref, ug_ref)
        up = branch(wu_ref, cu_ref, uu_ref)
        act = (gate * _sigmoid(gate) * up).astype(BF16)
        part = jnp.dot(act, wd_ref[...], preferred_element_type=F32)

        @pl.when(j == 0)
        def _():
            x2_ref[...] = x1_ref[...] + part

        @pl.when((j > 0) & (j < FF_PAIRS - 1))
        def _():
            x2_ref[...] += part

        @pl.when(j == FF_PAIRS - 1)
        def _():
            xv = x2_ref[...] + part
            wv = fw_ref[...]
            r = lax.rsqrt(jnp.mean(xv * xv, axis=-1, keepdims=True) + EPS)
            err = xv * r * wv - t_ref[...]
            lsum = jnp.sum(jnp.sum(err * err, axis=-1, keepdims=True), axis=0, keepdims=True) * (0.5 / D)
            g = err * (1.0 / D)
            xh = xv * r
            gw = g * wv
            dx = r * (gw - xh * jnp.mean(gw * xh, axis=-1, keepdims=True))
            dx_ref[...] = dx
            dxb_ref[...] = dx.astype(BF16)
            dfw = jnp.sum(g * xh, axis=0, keepdims=True)
            lpart = jnp.broadcast_to(lsum, (1, 128))

            @pl.when(i == 0)
            def _():
                dfw_ref[...] = dfw
                loss_ref[...] = lpart

            @pl.when(i > 0)
            def _():
                dfw_ref[...] += dfw
                loss_ref[...] += lpart

    rows = pl.BlockSpec((tm, D), lambda i, j: (i, 0))
    slab = lambda off: pl.BlockSpec((None, FF_SLAB, D), lambda i, j: (j + off, 0, 0))
    cslab = lambda off: pl.BlockSpec((None, FFN_CONV, FF_SLAB), lambda i, j: (j + off, 0, 0))
    uspec = pl.BlockSpec((None, tm, FF_SLAB), lambda i, j: (j, i, 0))
    return pl.pallas_call(
        body, name=name, grid=(ni, FF_PAIRS),
        in_specs=[rows, pl.BlockSpec((ROWS16, D), lambda i, j: (jnp.maximum(i * per - 1, 0), 0)), rows,
                  slab(0), slab(FF_PAIRS), cslab(0), cslab(FF_PAIRS), pl.BlockSpec((FF_SLAB, D), lambda i, j: (j, 0)),
                  pl.BlockSpec((1, D), lambda i, j: (0, 0)), rows],
        out_specs=[rows, rows, pl.BlockSpec((1, D), lambda i, j: (0, 0)), pl.BlockSpec((1, 128), lambda i, j: (0, 0)), uspec, uspec],
        out_shape=[_sds((S, D)), _sds((S, D), BF16), _sds((1, D)), _sds((1, 128)),
                   _sds((FF_PAIRS, S, FF_SLAB), BF16), _sds((FF_PAIRS, S, FF_SLAB), BF16)],
        scratch_shapes=[pltpu.VMEM((tm, D), F32)],
        compiler_params=_params(("arbitrary", "arbitrary")),
    )(h2, h2, x1, w_up, w_up, conv_w, conv_w, w_down, final_w, tgt)


def _ffn_bwd(dx2, h2, ug, uu, conv_w, w_down, *, name, tm=512):
    S, D = h2.shape
    ni = S // tm
    per = tm // ROWS16
    ext = tm + ROWS16

    def body(dx_ref, dxn_ref, h_ref, ug_ref, ugp_ref, ugn_ref, uu_ref, uup_ref, uun_ref, cg_ref, cu_ref, wd_ref,
             du_ref, gd_ref, gup_ref, dcw_ref, acc_d, acc_g, acc_u, acc_cg, acc_cu):
        i = pl.program_id(1)

        @pl.when(i == 0)
        def _():
            acc_d[...] = jnp.zeros_like(acc_d)
            acc_g[...] = jnp.zeros_like(acc_g)
            acc_u[...] = jnp.zeros_like(acc_u)
            acc_cg[...] = jnp.zeros_like(acc_cg)
            acc_cu[...] = jnp.zeros_like(acc_cu)

        dx = dx_ref[...]
        dxe = jnp.concatenate([dx, dxn_ref[...]], axis=0)
        row = lax.broadcasted_iota(jnp.int32, (ext, 1), 0)
        live = (i < ni - 1) | (row < tm)
        d_act = jnp.where(live, lax.dot_general(dxe, wd_ref[...], (((1,), (1,)), ((), ())), preferred_element_type=F32), 0.0)
        rowp = lax.broadcasted_iota(jnp.int32, (ext + ROWS16, 1), 0)
        keep = (i > 0) | (rowp >= ROWS16)

        def pre(cur, prev, nxt):
            return jnp.where(keep, jnp.concatenate([prev[...], cur[...], nxt[...]], axis=0).astype(F32), 0.0)

        uge, uue = pre(ug_ref, ugp_ref, ugn_ref), pre(uu_ref, uup_ref, uun_ref)
        cg, cu = cg_ref[...], cu_ref[...]
        base = ROWS16 - (FFN_CONV - 1)
        gate = _taps(cg, uge, base, ext)
        up = _taps(cu, uue, base, ext)
        sg = _sigmoid(gate)
        silu = gate * sg
        dgc = d_act * up * _dsilu(gate, sg)
        duc = d_act * silu

        def conv_t(w, dc):
            out = _shifted(dc, FFN_CONV - 1, tm) * w[0:1]
            for t in range(1, FFN_CONV):
                out = out + _shifted(dc, FFN_CONV - 1 - t, tm) * w[t:t + 1]
            return out.astype(BF16)

        du_g, du_u = conv_t(cg, dgc), conv_t(cu, duc)
        du_ref[0] = du_g
        du_ref[1] = du_u
        dcw = lambda dc, xe: jnp.concatenate(
            [jnp.sum(dc[0:tm] * _shifted(xe, base + t, tm), axis=0, keepdims=True) for t in range(FFN_CONV)], axis=0)
        acc_cg[0:FFN_CONV, :] += dcw(dgc, uge)
        acc_cu[0:FFN_CONV, :] += dcw(duc, uue)
        tn = (((0,), (0,)), ((), ()))
        act = (silu[0:tm] * up[0:tm]).astype(BF16)
        acc_d[...] += lax.dot_general(act, dx, tn, preferred_element_type=F32)
        hv = h_ref[...]
        acc_g[...] += lax.dot_general(du_g, hv, tn, preferred_element_type=F32)
        acc_u[...] += lax.dot_general(du_u, hv, tn, preferred_element_type=F32)

        @pl.when(i == ni - 1)
        def _():
            gd_ref[...] = acc_d[...].astype(BF16)
            gup_ref[0] = acc_g[...].astype(BF16)
            gup_ref[1] = acc_u[...].astype(BF16)
            dcw_ref[0] = acc_cg[0:FFN_CONV, :]
            dcw_ref[1] = acc_cu[0:FFN_CONV, :]

    last16 = S // ROWS16 - 1
    rows = pl.BlockSpec((tm, D), lambda j, i: (i, 0))
    rows_next = pl.BlockSpec((ROWS16, D), lambda j, i: (jnp.minimum((i + 1) * per, last16), 0))
    u_cur = pl.BlockSpec((None, tm, FF_SLAB), lambda j, i: (j, i, 0))
    u_prev = pl.BlockSpec((None, ROWS16, FF_SLAB), lambda j, i: (j, jnp.maximum(i * per - 1, 0), 0))
    u_next = pl.BlockSpec((None, ROWS16, FF_SLAB), lambda j, i: (j, jnp.minimum((i + 1) * per, last16), 0))
    cslab = lambda off: pl.BlockSpec((None, FFN_CONV, FF_SLAB), lambda j, i: (j + off, 0, 0))
    return pl.pallas_call(
        body, name=name, grid=(FF_PAIRS, ni),
        in_specs=[rows, rows_next, rows, u_cur, u_prev, u_next, u_cur, u_prev, u_next, cslab(0), cslab(FF_PAIRS),
                  pl.BlockSpec((FF_SLAB, D), lambda j, i: (j, 0))],
        out_specs=[pl.BlockSpec((None, 2, tm, FF_SLAB), lambda j, i: (j, 0, i, 0)), pl.BlockSpec((FF_SLAB, D), lambda j, i: (j, 0)),
                   pl.BlockSpec((None, 2, FF_SLAB, D), lambda j, i: (j, 0, 0, 0)),
                   pl.BlockSpec((None, 2, FFN_CONV, FF_SLAB), lambda j, i: (j, 0, 0, 0))],
        out_shape=[_sds((FF_PAIRS, 2, S, FF_SLAB), BF16), _sds((D_FF, D), BF16), _sds((FF_PAIRS, 2, FF_SLAB, D), BF16),
                   _sds((FF_PAIRS, 2, FFN_CONV, FF_SLAB))],
        scratch_shapes=[pltpu.VMEM((FF_SLAB, D), F32), pltpu.VMEM((FF_SLAB, D), F32), pltpu.VMEM((FF_SLAB, D), F32),
                        pltpu.VMEM((8, FF_SLAB), F32), pltpu.VMEM((8, FF_SLAB), F32)],
        compiler_params=_params(("parallel", "arbitrary")),
    )(dx2, dx2, h2, ug, ug, ug, uu, uu, uu, conv_w, conv_w, w_down)


def _pair_slot(p):
    return 2 * (p & (FF_PAIRS - 1)) + (p >> 2)


def _mm_slabs(a, w, *, name, res=None, norm_bwd=None, after=(), tm=1024, tn=1024):
    nk, S, _ = a.shape
    D = w.shape[2]
    has_res = res is not None
    has_norm = norm_bwd is not None
    assert not has_norm or tn == D

    def body(*refs):
        a_ref, w_ref = refs[:2]
        r_ref = refs[2] if has_res else None
        if has_norm:
            x_ref, nw_ref, skip_ref = refs[2 + has_res:5 + has_res]
            o_ref, dw_ref, acc_ref = refs[-3:]
        else:
            o_ref, acc_ref = refs[-2:]
        i, k = pl.program_id(0), pl.program_id(2)
        part = jnp.dot(a_ref[...], w_ref[...], preferred_element_type=F32)

        @pl.when(k == 0)
        def _():
            acc_ref[...] = part

        @pl.when(k > 0)
        def _():
            acc_ref[...] += part

        @pl.when(k == nk - 1)
        def _():
            r = acc_ref[...] + r_ref[...] if has_res else acc_ref[...]
            if has_norm:
                dx, dw = _rms_bwd_rows(r, x_ref[...], nw_ref[...])
                o_ref[...] = skip_ref[...] + dx

                @pl.when(i == 0)
                def _():
                    dw_ref[...] = dw

                @pl.when(i > 0)
                def _():
                    dw_ref[...] += dw
            else:
                o_ref[...] = r

    o_spec = pl.BlockSpec((tm, tn), lambda i, j, k: (i, j))
    one = pl.BlockSpec((1, tn), lambda i, j, k: (0, 0))
    return pl.pallas_call(
        body, name=name, grid=(S // tm, D // tn, nk),
        in_specs=[pl.BlockSpec((None, tm, FF_SLAB), lambda i, j, k: (k, i, 0)),
                  pl.BlockSpec((None, FF_SLAB, tn), lambda i, j, k: (FF_PAIRS * (k & 1) + (k >> 1), 0, j))] + [o_spec] * has_res
        + ([o_spec, one, o_spec] if has_norm else []) + [ANY] * len(after),
        out_specs=[o_spec, one] if has_norm else o_spec, out_shape=[_sds((S, D)), _sds((1, D))] if has_norm else _sds((S, D)),
        scratch_shapes=[pltpu.VMEM((tm, tn), F32)],
        compiler_params=_params(("arbitrary" if has_norm else "parallel", "parallel", "arbitrary")),
    )(*((a, w) + ((res,) if has_res else ()) + (tuple(norm_bwd) if has_norm else ()) + tuple(after)))


def _local_step(x, tgt, norm1_w, w_land, conv_a, a_log, dt_bias, gnw, norm2_w, final_w, late_weights, emit, start_after=()):
    wgrad = functools.partial(_mm, ta=True, out_dtype=BF16)
    h1, proj_a, proj_z, proj_b = _in_proj(x, norm1_w, w_land, after=start_after, name="in_proj")
    qn, kn, v, gcb, bb = _gdn_prep_fwd(proj_a, conv_a, a_log, dt_bias, name="gdn_prep_fwd")
    uv, wk, at, tmat, wkb, qdb, keb = _gdn_chunk_fwd(qn, kn, v, gcb, bb, name="gdn_chunk_fwd")
    o, u, sp, oab = _gdn_scan_fwd(uv, at, wkb, qdb, keb, gcb, proj_z, gnw, name="gdn_scan_fwd")
    oab, lse = _attn_fwd(proj_b, oab, name="attn_fwd")
    w_out, w_up, conv_f, w_down = late_weights(oab)
    x1, h2 = _out_proj_norm(oab, w_out, x, norm2_w, name="out_proj")
    dx2, dx2_b, d_final, loss, ug, uu = _ffn_fwd(h2, x1, w_up, conv_f, w_down, final_w, tgt, name="ffn_fwd")
    du, g_down, g_up, dcw = _ffn_bwd(dx2_b, h2, ug, uu, conv_f, w_down, name="ffn_bwd")
    token = emit("ffn", w_down=g_down, w_up=g_up.reshape(N_DEV, FF_SLAB, -1), conv_f=dcw.reshape(N_DEV, FFN_CONV, -1))
    dx1, d_norm2 = _mm_slabs(du.reshape(N_DEV, -1, FF_SLAB), w_up, norm_bwd=(x1, norm2_w, dx2), after=token, name="ffn_up_dx")
    d_oab = _mm(dx1, w_out, tb=True, name="out_proj_dx", tn=D_MODEL, tk=1024)
    token = emit("out", w_out=wgrad(oab, dx1, name="out_proj_dw", tm=D_MODEL, tn=D_MODEL))
    dz, d_gnw, du, dwk, dat, dqd, dke, dgl = _gdn_scan_bwd(d_oab, o, proj_z, gnw, sp, u, at, wkb, qdb, keb, gcb, after=token, name="gdn_scan_bwd")
    dqn, dkn, dv, dg, dbeta = _gdn_chunk_bwd(qn, kn, gcb, bb, tmat, uv, wk, du, dwk, dat, dqd, dke, dgl, name="gdn_chunk_bwd")
    dc, dba, d_small = _gdn_prep_bwd(dqn, dkn, dv, dg, dbeta, proj_a, conv_a, a_log, dt_bias, name="gdn_prep_bwd")
    d_pa, d_conv_a = _gdn_conv_bwd(dc, dba, proj_a, conv_a, name="gdn_conv_bwd")
    d_pb = _attn_bwd(proj_b, oab, d_oab, lse, name="attn_bwd")
    g_a = wgrad(d_pa, h1, name="proj_a_dw", tm=A_COLS, tn=D_MODEL)
    g_z = wgrad(dz, h1, name="proj_z_dw", tn=D_MODEL)
    g_b = wgrad(d_pb, h1, name="proj_b_dw", tm=768, tn=D_MODEL)
    token = emit("in", w_a=g_a, w_z=g_z, w_b=g_b, conv_a=d_conv_a)
    grad_x, d_norm1 = _in_proj_dx((d_pa, dz, d_pb), w_land, x, norm1_w, dx1, after=token, name="in_proj_dx")
    small = dict(norm1=d_norm1, small=d_small, gnw=d_gnw, norm2=d_norm2, final=d_final)
    return loss, grad_x, small


_O1 = 3 * GDN_WIDTH
_O2 = _O1 + GDN_WIDTH
_O3 = _O2 + 2 * GDN_HEADS


_W_IN_ROWS = (A_COLS, GDN_WIDTH, 3 * DIL_WIDTH)
_IN_ROWS = (_O3 + 3 * DIL_WIDTH) // N_DEV
_ROW_TILE = 16
_IN_STEP = _IN_ROWS - _IN_ROWS % _ROW_TILE
_GAP = 8
_LAND_ROWS = 464
assert _O3 % _ROW_TILE == _ROW_TILE - _GAP and N_DEV - 1 + _GAP + _IN_ROWS <= _LAND_ROWS and _LAND_ROWS % _ROW_TILE == 0


def _padded_row(r):
    return r + (_GAP if r >= _O3 else 0)


def _shifted_slab(slab, j):
    z = jnp.zeros((_LAND_ROWS, slab.shape[1]), slab.dtype)
    a = lax.dynamic_update_slice(z, slab, (j, 0))
    b = lax.dynamic_update_slice(z, slab, (j + _GAP, 0))
    p = lax.broadcasted_iota(jnp.int32, (_LAND_ROWS, 1), 0) + _IN_STEP * j
    return jnp.where(p < _O3, a, jnp.where(p >= _O3 + _GAP, b, jnp.zeros_like(z)))


def _w_in_plan():
    def dest(p):
        if p < _O1:
            return 0, p
        if p < _O2:
            return 1, p - _O1
        if p < _O2 + _ROW_TILE:
            return 0, _O1
        q = p - _O3 - _GAP
        t, pair = divmod(q // 128, DIL_PAIRS)
        return 2, (3 * pair + t) * 128 + q % 128

    spans = [(_padded_row(_IN_ROWS * j), _padded_row(_IN_ROWS * (j + 1) - 1) + 1) for j in range(N_DEV)]
    runs, seams = [], []
    for p in range(0, spans[-1][1], _ROW_TILE):
        owners = [j for j, (lo, hi) in enumerate(spans) if lo < p + _ROW_TILE and hi > p]
        w, r = dest(p)
        if len(owners) == 2:
            seams.append((w, r, owners[0], p - _IN_STEP * owners[0], owners[1], p - _IN_STEP * owners[1]))
            continue
        (j,) = owners
        last = runs[-1] if runs else None
        if last and last[0] == j and last[2] == w and last[3] + last[4] == r and last[1] + last[4] == p - _IN_STEP * j:
            runs[-1] = last[:4] + (last[4] + _ROW_TILE,)
        else:
            runs.append((j, p - _IN_STEP * j, w, r, _ROW_TILE))
    return runs, seams


def _w_in_scratch(d):
    return [pltpu.VMEM((n, d), BF16) for n in _W_IN_ROWS] + [pltpu.VMEM((N_DEV - 1, _ROW_TILE, d), BF16),
                                                              pltpu.SemaphoreType.DMA(())]


def _fetch_w_in(land_ref, wa_ref, wz_ref, wb_ref, seam_ref, sem):
    w_refs = (wa_ref, wz_ref, wb_ref)
    runs, seams = _w_in_plan()
    copies = [pltpu.make_async_copy(land_ref.at[j, pl.ds(s, n)], w_refs[w].at[pl.ds(r, n)], sem) for j, s, w, r, n in runs]
    for k, (w, r, j0, s0, j1, s1) in enumerate(seams):
        copies.append(pltpu.make_async_copy(land_ref.at[j0, pl.ds(s0, _ROW_TILE)], w_refs[w].at[pl.ds(r, _ROW_TILE)], sem))
        copies.append(pltpu.make_async_copy(land_ref.at[j1, pl.ds(s1, _ROW_TILE)], seam_ref.at[k], sem))
    for cp in copies:
        cp.start()
    tail = _O1 + _ROW_TILE
    wa_ref[tail:, :] = jnp.zeros((A_COLS - tail, wa_ref.shape[1]), BF16)
    for cp in copies:
        cp.wait()
    for k, (w, r, *_) in enumerate(seams):
        both = w_refs[w][r:r + _ROW_TILE, :].astype(F32) + seam_ref[k].astype(F32)
        w_refs[w][r:r + _ROW_TILE, :] = both.astype(BF16)


def _merge_g_in(g_a, g_z, g_b):
    d = g_a.shape[1]
    g_b = g_b.reshape(DIL_PAIRS, 3, 128, d).transpose(1, 0, 2, 3).reshape(3 * DIL_WIDTH, d)
    return jnp.concatenate([g_a[:_O1], g_z, g_a[_O1:_O1 + 2 * GDN_HEADS], g_b], axis=0)


MESH = pl.DeviceIdType.MESH
ANY = pl.BlockSpec(memory_space=pl.ANY)


def _position():
    return lax.axis_index("x"), lax.axis_index("y"), lax.axis_index("c")


def _slot(p):
    return 4 * p[0] + 2 * p[1] + p[2]


def _all_gather(blocks, *, name):
    n = len(blocks)

    def body(*refs):
        ins, outs = refs[:n], refs[n:2 * n]
        send_sems, recv_sems, local_sems = refs[2 * n:]
        x, y, c = _position()
        me, sibling = (x, y, c), (x, y, 1 - c)
        chips = [(1 - x, y), (x, 1 - y), (1 - x, 1 - y)]

        def copy(a, k, block, to, src=None):
            dst = outs[a].at[_slot(block)]
            return pltpu.make_async_remote_copy(
                src_ref=dst if src is None else src, dst_ref=dst, send_sem=send_sems.at[a, k], recv_sem=recv_sems.at[a, k],
                device_id=to, device_id_type=MESH)

        mine = [pltpu.make_async_copy(ins[a], outs[a].at[_slot(me)], local_sems.at[a]) for a in range(n)]
        for cp in mine:
            cp.start()
        first = []
        for a in range(n):
            first.append(copy(a, 0, me, sibling, src=ins[a]))
            first += [copy(a, 1 + j, me, (*chip, c), src=ins[a]) for j, chip in enumerate(chips)]
        for cp in first:
            cp.start()
        passed = []
        for j, chip in enumerate(chips):
            for a in range(n):
                copy(a, 1 + j, (*chip, c), me).wait_recv()
                fwd = copy(a, 4 + j, (*chip, c), sibling)
                fwd.start()
                passed.append(fwd)
        for a in range(n):
            copy(a, 0, sibling, me).wait_recv()
            for j, chip in enumerate(chips):
                copy(a, 4 + j, (*chip, 1 - c), me).wait_recv()
        for cp in first + passed:
            cp.wait_send()
        for cp in mine:
            cp.wait()

    return pl.pallas_call(
        body, name=name, in_specs=[ANY] * n, out_specs=[ANY] * n,
        out_shape=[_sds((N_DEV,) + b.shape, b.dtype) for b in blocks],
        scratch_shapes=[pltpu.SemaphoreType.DMA((n, 7)), pltpu.SemaphoreType.DMA((n, 7)), pltpu.SemaphoreType.DMA((n,))],
    )(*blocks)


def _gather_direct(block, *, name, after=()):
    def body(in_ref, *rest):
        out_ref, send_sems, recv_sems, local_sem = rest[len(after):]
        x, y, c = _position()
        me = _slot((x, y, c))
        mine = pltpu.make_async_copy(in_ref, out_ref.at[me], local_sem)
        mine.start()
        copies = [pltpu.make_async_remote_copy(
            src_ref=in_ref, dst_ref=out_ref.at[me], send_sem=send_sems.at[k - 1], recv_sem=recv_sems.at[k - 1],
            device_id=_peer_of(k, x, y, c), device_id_type=MESH) for k in range(1, N_DEV)]
        for cp in copies:
            cp.start()
        for cp in copies:
            cp.wait()
        mine.wait()

    return pl.pallas_call(
        body, name=name, in_specs=[pl.BlockSpec(memory_space=pltpu.VMEM)] + [ANY] * len(after),
        out_specs=pl.BlockSpec(memory_space=pltpu.VMEM),
        out_shape=_sds((N_DEV,) + block.shape, block.dtype),
        scratch_shapes=[pltpu.SemaphoreType.DMA((N_DEV - 1,)), pltpu.SemaphoreType.DMA((N_DEV - 1,)), pltpu.SemaphoreType.DMA],
    )(block, *after)


HBM = pl.BlockSpec(memory_space=pltpu.HBM)
SEM = pl.BlockSpec(memory_space=pltpu.SEMAPHORE)
EFFECT = pltpu.SideEffectType.DATAFLOW_SIDE_EFFECTING


def _peer_of(k, x, y, c):
    return (1 - x if k & 4 else x, 1 - y if k & 2 else y, 1 - c if k & 1 else c)


def _flight(a, k):
    return a * (N_DEV - 1) + k - 1


def _exchange_start(arrays, *, name, broadcast=False, paired=()):
    n = len(arrays)

    def body(*refs):
        ins, lands = refs[:n], refs[n:2 * n]
        send_sems, recv_sems = refs[2 * n:2 * n + 2]
        token = refs[-1]
        x, y, c = _position()
        me = _slot((x, y, c))
        for k in range(1, N_DEV):
            peer = _peer_of(k, x, y, c)
            for a in range(n):
                at = _pair_slot(_slot(peer)) if a in paired else _slot(peer)
                pltpu.make_async_remote_copy(
                    src_ref=ins[a] if broadcast else ins[a].at[at], dst_ref=lands[a].at[me],
                    send_sem=send_sems.at[_flight(a, k)], recv_sem=recv_sems.at[_flight(a, k)],
                    device_id=peer, device_id_type=MESH).start()
        token[...] = jnp.zeros_like(token)

    land_shapes = [((N_DEV,) + s.shape) if broadcast else s.shape for s in arrays]
    lands = [pltpu.with_memory_space_constraint(lax.empty(shp, s.dtype), pltpu.HBM) for shp, s in zip(land_shapes, arrays)]
    srcs = [pltpu.with_memory_space_constraint(s, pltpu.HBM) for s in arrays]
    outs = pl.pallas_call(
        body, name=name, in_specs=[HBM] * (2 * n),
        out_specs=[SEM, SEM] + [HBM] * (2 * n) + [pl.BlockSpec(memory_space=pltpu.VMEM)],
        out_shape=[pltpu.SemaphoreType.DMA((n * (N_DEV - 1),)), pltpu.SemaphoreType.DMA((n * (N_DEV - 1),))]
        + [pltpu.HBM(s.shape, s.dtype) for s in arrays] + [pltpu.HBM(shp, s.dtype) for shp, s in zip(land_shapes, arrays)]
        + [_sds((8, 128))],
        input_output_aliases={i: 2 + i for i in range(2 * n)},
        compiler_params=pltpu.CompilerParams(has_side_effects=EFFECT),
    )(*srcs, *lands)
    return outs[0], outs[1], outs[2:2 + n], outs[2 + n:2 + 2 * n], outs[-1]


def _exchange_wait(send_sems, recv_sems, srcs, lands, after, *, name, broadcast=False):
    n = len(srcs)

    def body(*refs):
        ins, lnd = refs[:n], refs[n:2 * n]
        send_ref, recv_ref = refs[2 * n:2 * n + 2]
        x, y, c = _position()
        for k in range(1, N_DEV):
            for a in range(n):
                cp = pltpu.make_async_remote_copy(
                    src_ref=ins[a] if broadcast else ins[a].at[0], dst_ref=lnd[a].at[0], send_sem=send_ref.at[_flight(a, k)],
                    recv_sem=recv_ref.at[_flight(a, k)], device_id=_peer_of(k, x, y, c), device_id_type=MESH)
                cp.wait_send()
                cp.wait_recv()

    outs = pl.pallas_call(
        body, name=name, in_specs=[HBM] * (2 * n) + [SEM, SEM, ANY], out_specs=[HBM] * (2 * n),
        out_shape=[pltpu.HBM(s.shape, s.dtype) for s in srcs] + [pltpu.HBM(s.shape, s.dtype) for s in lands],
        input_output_aliases={i: i for i in range(2 * n)},
        compiler_params=pltpu.CompilerParams(has_side_effects=EFFECT),
    )(*srcs, *lands, send_sems, recv_sems, after)
    return outs[:n], outs[n:]


def _with_own(landed, srcs, me):
    return [lax.dynamic_update_index_in_dim(l, o, me, 0) for l, o in zip(landed, srcs)]


def _adam_update(g, w, m, v):
    c1 = 1.0 - ADAM_B1 ** ADAM_STEP
    c2 = 1.0 - ADAM_B2 ** ADAM_STEP
    nm = ADAM_B1 * m + (1.0 - ADAM_B1) * g
    nv = ADAM_B2 * v + (1.0 - ADAM_B2) * (g * g)
    return -ADAM_LR * ((nm / c1) / (jnp.sqrt(nv / c2) + ADAM_EPS) + ADAM_WD * w), nm, nv


def _adamw(landed, sent, me, w, m, v, *, name, tr=None, tc=None):
    R, C = w.shape
    tr = R if tr is None else tr
    tc = C if tc is None else tc
    assert R % tr == 0 and C % tc == 0

    def body(me_ref, own_ref, p_ref, w_ref, m_ref, v_ref, g_ref, d_ref, nm_ref, nv_ref, token_ref):
        token_ref[...] = jnp.zeros_like(token_ref)
        g = own_ref[...].astype(F32)
        for s in range(N_DEV):
            g = g + jnp.where(me_ref[1] == s, 0.0, p_ref[s].astype(F32))
        delta, nm, nv = _adam_update(g, w_ref[...], m_ref[...], v_ref[...])
        g_ref[...] = g
        nm_ref[...] = nm
        nv_ref[...] = nv
        d_ref[...] = delta

    blk = pl.BlockSpec((tr, tc), lambda i, j, me_ref: (i, j))
    return pl.pallas_call(
        body, name=name,
        grid_spec=pltpu.PrefetchScalarGridSpec(
            num_scalar_prefetch=1, grid=(R // tr, C // tc),
            in_specs=[pl.BlockSpec((None, tr, tc), lambda i, j, me_ref: (me_ref[0], i, j)),
                      pl.BlockSpec((N_DEV, tr, tc), lambda i, j, me_ref: (0, i, j)), blk, blk, blk],
            out_specs=[blk] * 4 + [pl.BlockSpec((8, 128), lambda i, j, me_ref: (0, 0))]),
        out_shape=[_sds((R, C))] * 4 + [_sds((8, 128))],
        compiler_params=_params(("arbitrary", "arbitrary")),
    )(me, sent, landed, w, m, v)


def _adamw_rowwise(landed, sent, me, w, m, v, *, name, tr=128):
    _, R, C = landed.shape
    q = C // 128

    def body(me_ref, own_ref, p_ref, w_ref, m_ref, v_ref, g_ref, d_ref, nm_ref, nv_ref):
        g = own_ref[...].astype(F32)
        for s in range(N_DEV):
            g = g + jnp.where(me_ref[1] == s, 0.0, p_ref[s].astype(F32))
        for s in range(q):
            rows = pl.ds(s, tr, stride=q)
            gs = g[:, 128 * s:128 * (s + 1)]
            delta, nm, nv = _adam_update(gs, w_ref[rows, :], m_ref[rows, :], v_ref[rows, :])
            g_ref[rows, :] = gs
            nm_ref[rows, :] = nm
            nv_ref[rows, :] = nv
            d_ref[rows, :] = delta

    blk = pl.BlockSpec((tr * q, 128), lambda i, me_ref: (i, 0))
    return pl.pallas_call(
        body, name=name,
        grid_spec=pltpu.PrefetchScalarGridSpec(
            num_scalar_prefetch=1, grid=(pl.cdiv(R, tr),),
            in_specs=[pl.BlockSpec((None, tr, C), lambda i, me_ref: (me_ref[0], i, 0)),
                      pl.BlockSpec((N_DEV, tr, C), lambda i, me_ref: (0, i, 0)), blk, blk, blk],
            out_specs=[blk] * 4),
        out_shape=[_sds((R * q, 128))] * 4,
        compiler_params=_params(("arbitrary",)),
    )(me, sent, landed, w, m, v)


_SMALL_ROWS = 8
_SMALL_SLOTS =((0, 0, D_MODEL), (1, 0, D_MODEL), (2, 0, D_MODEL), (3, 0, GDN_DIM), (3, GDN_DIM, GDN_HEADS),
                (3, GDN_DIM + GDN_HEADS, GDN_HEADS))
_LOSS_LANE = 2 * GDN_DIM


def _pack_small(norm1, norm2, final, gnw, a_log, dt_bias, loss):
    row3 = jnp.concatenate([gnw, a_log, dt_bias, jnp.zeros((1, 128 - 2 * GDN_HEADS), F32), loss,
                            jnp.zeros((1, D_MODEL - 3 * 128), F32)], axis=1)
    return jnp.concatenate([norm1, norm2, final, row3, jnp.zeros((_SMALL_ROWS - 4, D_MODEL), F32)], axis=0)


def _adamw_small(packs, ws, ms, vs, *, name):
    n = len(ws)

    def body(p_ref, *refs):
        w_refs, m_refs, v_refs = refs[:n], refs[n:2 * n], refs[2 * n:3 * n]
        outs = refs[3 * n:]
        g_all = p_ref[0]
        for s in range(1, N_DEV):
            g_all = g_all + p_ref[s]
        for i, (row, lane, width) in enumerate(_SMALL_SLOTS):
            g = g_all[row:row + 1, lane:lane + width]
            delta, nm, nv = _adam_update(g, w_refs[i][...], m_refs[i][...], v_refs[i][...])
            for o_ref, val in zip(outs[4 * i:4 * i + 4], (g, delta, nm, nv)):
                o_ref[...] = val
        outs[-1][...] = g_all[3:4, _LOSS_LANE:_LOSS_LANE + 128]

    vm = pl.BlockSpec(memory_space=pltpu.VMEM)
    outs = pl.pallas_call(
        body, name=name, in_specs=[vm] * (1 + 3 * n), out_specs=[vm] * (4 * n + 1),
        out_shape=[_sds(w.shape) for w in ws for _ in range(4)] + [_sds((1, 128))],
    )(packs, *ws, *ms, *vs)
    return [outs[4 * i:4 * i + 4] for i in range(n)], outs[-1]


def _slabs_by_cols(g):
    r = g.shape[0]
    return g.reshape(r, N_DEV, -1).transpose(1, 0, 2)


def _cols_from_slabs(s):
    return s.transpose(1, 0, 2).reshape(s.shape[1], -1)


def kernel(x, norm1_w, w_in, conv_qkv_w, a_log, dt_bias, gdn_norm_w, w_out, norm2_w, w_up, ffn_conv_w, w_down, final_norm_w, loss_target, m_norm1_w, m_w_in, m_conv_qkv_w, m_a_log, m_dt_bias, m_gdn_norm_w, m_w_out, m_norm2_w, m_w_up, m_ffn_conv_w, m_w_down, m_final_norm_w, v_norm1_w, v_w_in, v_conv_qkv_w, v_a_log, v_dt_bias, v_gdn_norm_w, v_w_out, v_norm2_w, v_w_up, v_ffn_conv_w, v_w_down, v_final_norm_w):
    bf = lambda a: a.astype(BF16)
    me = _slot(_position())
    t_in = lambda a: a[0].T
    gw_in, g_conv_a = _all_gather([_shifted_slab(bf(t_in(w_in)), me), conv_qkv_w[0]], name="gather_w_in")
    late_src, _ = lax.optimization_barrier(([bf(w_out[0]), bf(t_in(w_up)), bf(w_down[0]), ffn_conv_w[0]], gw_in))
    l_send, l_recv, l_srcs, l_lands, l_token = _exchange_start(late_src, name="weights_start", broadcast=True)

    def late_weights(after):
        srcs, landed = _exchange_wait(l_send, l_recv, l_srcs, l_lands, after, name="weights_wait", broadcast=True)
        gw_out, gw_up, gw_down, g_conv_f = _with_own(landed, srcs, me)
        return gw_out.reshape(D_MODEL, D_MODEL), gw_up, g_conv_f, gw_down.reshape(D_FF, D_MODEL)

    flights = {}

    def emit(group, **grads):
        paired = ()
        if group == "in":
            slabs = dict(w_in=_merge_g_in(grads["w_a"], grads["w_z"], grads["w_b"]).reshape(N_DEV, -1, D_MODEL),
                         conv_a=_slabs_by_cols(grads["conv_a"]))
        elif group == "ffn":
            slabs = dict(w_down=grads["w_down"].reshape(N_DEV, -1, D_MODEL), w_up=grads["w_up"], conv_f=grads["conv_f"])
            paired = (1, 2)
        else:
            slabs = {k: v.reshape(N_DEV, -1, D_MODEL) for k, v in grads.items()}
        names = list(slabs)
        *flight, token = _exchange_start([slabs[k] for k in names], paired=paired, name="grads_start_" + group)
        flights[group] = (names, flight)
        return (token,)

    loss, grad_x, g = _local_step(
        x[0], loss_target[0], norm1_w, gw_in, _cols_from_slabs(g_conv_a), a_log, dt_bias,
        gdn_norm_w, norm2_w, final_norm_w[None], late_weights, emit, start_after=(l_token,))
    got = {}
    me1 = jnp.reshape(me, (1,)).astype(jnp.int32)

    def collect(group, after):
        names, (send_sems, recv_sems, srcs, lands) = flights[group]
        srcs, landed = _exchange_wait(send_sems, recv_sems, srcs, lands, after, name="grads_wait_" + group)
        got.update(zip(names, zip(landed, srcs)))

    def update(key, w, m, v, paired=False, **tiles):
        where = jnp.concatenate([_pair_slot(me1) if paired else me1, me1])
        return _adamw(*got[key], where, w, m, v, name="adamw_" + key, **tiles)

    collect("ffn", grad_x)
    collect("out", grad_x)
    *o_out, t1 = update("w_out", w_out[0], m_w_out[0], v_w_out[0])
    *o_up, t2 = update("w_up", t_in(w_up), t_in(m_w_up), t_in(v_w_up), paired=True, tr=176)
    o_up = [o.T for o in o_up]
    *o_down, t3 = update("w_down", w_down[0], m_w_down[0], v_w_down[0], tr=176)
    *o_cf, t4 = update("conv_f", ffn_conv_w[0], m_ffn_conv_w[0], v_ffn_conv_w[0], paired=True)
    pack = _pack_small(g["norm1"], g["norm2"], g["final"], g["gnw"], g["small"][:, 0:GDN_HEADS],
                       g["small"][:, GDN_HEADS:2 * GDN_HEADS], loss)
    small_all = _gather_direct(pack, after=(t1, t2, t3, t4), name="gather_small")
    collect("in", small_all)
    rows = lambda a: a.reshape(D_MODEL // 128, 128, -1).transpose(2, 0, 1).reshape(-1, 128)
    o_in = [o.reshape(-1, D_MODEL // 128, 128).transpose(1, 2, 0).reshape(D_MODEL, -1) for o in _adamw_rowwise(
        *got["w_in"], jnp.concatenate([me1, me1]), rows(w_in), rows(m_w_in), rows(v_w_in), name="adamw_w_in")]
    o_ca = update("conv_a", conv_qkv_w[0], m_conv_qkv_w[0], v_conv_qkv_w[0])
    (o_n1, o_n2, o_fin, o_gn, o_al, o_dt), total = _adamw_small(
        small_all, (norm1_w, norm2_w, final_norm_w[None], gdn_norm_w, a_log, dt_bias),
        (m_norm1_w, m_norm2_w, m_final_norm_w[None], m_gdn_norm_w, m_a_log, m_dt_bias),
        (v_norm1_w, v_norm2_w, v_final_norm_w[None], v_gdn_norm_w, v_a_log, v_dt_bias), name="adamw_small")
    outs = [total[0, 0], grad_x[None]]
    for k in range(4):
        outs += [o_n1[k], o_in[k][None], o_ca[k][None], o_al[k], o_dt[k], o_gn[k], o_out[k][None], o_n2[k], o_up[k][None],
                 o_cf[k][None], o_down[k][None], o_fin[k][0]]
    return tuple(outs)
```

```python
import functools

import jax
import jax.numpy as jnp
from jax import lax
from jax.experimental import pallas as pl
from jax.experimental.pallas import tpu as pltpu

F32 = jnp.float32
BF16 = jnp.bfloat16

N_DEV = 8
D_MODEL = 1024
GDN_HEADS = 4
GDN_DIM = 128
GDN_WIDTH = GDN_HEADS * GDN_DIM
GDN_CONV = 4
CHUNK = 64
CHUNKS_PER_STEP = 4
DIL_HEADS = 8
DIL_DIM = 64
DIL_WIDTH = DIL_HEADS * DIL_DIM
DIL_PAIRS = DIL_HEADS // 2
DILATIONS = (1, 4, 16)
BAND = 128
D_FF = 2816
FFN_CONV = 3
EPS = 1e-6
A_COLS = 3 * GDN_WIDTH + 128
HALO = 8

ADAM_LR = 0.001
ADAM_B1 = 0.9
ADAM_B2 = 0.999
ADAM_EPS = 1e-08
ADAM_WD = 0.01
ADAM_STEP = 10

VMEM_LIMIT_BYTES = 56 * 1024 * 1024
NEG_BIG = -1e30


def _params(sem=None):
    return pltpu.CompilerParams(dimension_semantics=sem, vmem_limit_bytes=VMEM_LIMIT_BYTES)


def _sds(shape, dtype=F32):
    return jax.ShapeDtypeStruct(shape, dtype)


def _bdot(a, b):
    return jnp.dot(a.astype(BF16), b.astype(BF16), preferred_element_type=F32)


def _bdot_nt(a, b):
    return lax.dot_general(a.astype(BF16), b.astype(BF16), (((1,), (1,)), ((), ())), preferred_element_type=F32)


def _bdot_tn(a, b):
    return lax.dot_general(a.astype(BF16), b.astype(BF16), (((0,), (0,)), ((), ())), preferred_element_type=F32)


def _split(a):
    hi = a.astype(BF16)
    lo = (a - hi.astype(F32)).astype(BF16)
    return hi, lo


def _dot3(a, b, dims):
    ah, al = _split(a)
    bh, bl = _split(b)
    d = functools.partial(lax.dot_general, dimension_numbers=(dims, ((), ())), preferred_element_type=F32)
    return d(ah, bh) + (d(al, bh) + d(ah, bl))


def _exact_tri_dot(tri, g):
    g1 = g.astype(BF16)
    r1 = g - g1.astype(F32)
    g2 = r1.astype(BF16)
    g3 = (r1 - g2.astype(F32)).astype(BF16)
    t = tri.astype(BF16)
    d = functools.partial(jnp.dot, preferred_element_type=F32)
    return d(t, g1) + (d(t, g2) + d(t, g3))


def _sigmoid(x):
    return 1.0 / (1.0 + jnp.exp(-x))


def _dsilu(x, sg):
    return sg * (1.0 + x * (1.0 - sg))


def _rms_bwd_rows(dh, x, w):
    r = lax.rsqrt(jnp.mean(x * x, axis=-1, keepdims=True) + EPS)
    xh = x * r
    gw = dh * w
    return r * (gw - xh * jnp.mean(gw * xh, axis=-1, keepdims=True)), jnp.sum(dh * xh, axis=0, keepdims=True)


def _mm(a, b, *, name, ta=False, tb=False, res=None, norm_bwd=None, after=(), out_dtype=F32, tm=512, tn=512, tk=512):
    if ta:
        K, M = a.shape
    else:
        M, K = a.shape
    if tb:
        N, Kb = b.shape
    else:
        Kb, N = b.shape
    assert K == Kb, (a.shape, b.shape)
    tm, tn, tk = min(tm, M), min(tn, N), min(tk, K)
    assert M % tm == 0 and N % tn == 0 and K % tk == 0, (name, M, N, K, tm, tn, tk)
    nk = K // tk
    dims = (((0 if ta else 1,), (1 if tb else 0,)), ((), ()))
    has_res = res is not None
    has_norm = norm_bwd is not None
    assert not has_norm or tn == N

    def body(*refs):
        a_ref, b_ref = refs[:2]
        r_ref = refs[2] if has_res else None
        if has_norm:
            x_ref, w_ref, skip_ref = refs[2 + has_res:5 + has_res]
            o_ref, dw_ref, acc_ref = refs[-3:]
        else:
            o_ref, acc_ref = refs[-2:]
        i, k = pl.program_id(0), pl.program_id(2)
        part = lax.dot_general(a_ref[...].astype(BF16), b_ref[...].astype(BF16), dims, preferred_element_type=F32)

        @pl.when(k == 0)
        def _():
            acc_ref[...] = part

        @pl.when(k > 0)
        def _():
            acc_ref[...] += part

        @pl.when(k == nk - 1)
        def _():
            r = acc_ref[...]
            if has_res:
                r = r + r_ref[...]
            if has_norm:
                dx, dw = _rms_bwd_rows(r, x_ref[...], w_ref[...])
                o_ref[...] = skip_ref[...] + dx

                @pl.when(i == 0)
                def _():
                    dw_ref[...] = dw

                @pl.when(i > 0)
                def _():
                    dw_ref[...] += dw
            else:
                o_ref[...] = r.astype(out_dtype)

    a_spec = pl.BlockSpec((tk, tm), lambda i, j, k: (k, i)) if ta else pl.BlockSpec((tm, tk), lambda i, j, k: (i, k))
    b_spec = pl.BlockSpec((tn, tk), lambda i, j, k: (j, k)) if tb else pl.BlockSpec((tk, tn), lambda i, j, k: (k, j))
    o_spec = pl.BlockSpec((tm, tn), lambda i, j, k: (i, j))
    one = pl.BlockSpec((1, tn), lambda i, j, k: (0, 0))
    in_specs = [a_spec, b_spec] + [o_spec] * has_res + ([o_spec, one, o_spec] if has_norm else []) + [ANY] * len(after)
    args = (a, b) + ((res,) if has_res else ()) + (tuple(norm_bwd) if has_norm else ()) + tuple(after)
    return pl.pallas_call(
        body, name=name, grid=(M // tm, N // tn, nk), in_specs=in_specs,
        out_specs=[o_spec, one] if has_norm else o_spec,
        out_shape=[_sds((M, N)), _sds((1, N))] if has_norm else _sds((M, N), out_dtype),
        scratch_shapes=[pltpu.VMEM((tm, tn), F32)],
        compiler_params=_params(("arbitrary" if has_norm else "parallel", "parallel", "arbitrary")),
    )(*args)


def _in_proj(x, norm_w, w_land, *, name, after=(), tm=512):
    S, D = x.shape

    def body(x_ref, nw_ref, land_ref, *rest):
        h_ref, pa_ref, pz_ref, pb_ref, *scratch = rest[len(after):]

        @pl.when(pl.program_id(0) == 0)
        def _():
            _fetch_w_in(land_ref, *scratch)

        xv = x_ref[...]
        r = lax.rsqrt(jnp.mean(xv * xv, axis=-1, keepdims=True) + EPS)
        h = (xv * r * nw_ref[...]).astype(BF16)
        h_ref[...] = h
        for w_ref, p_ref in zip(scratch[:3], (pa_ref, pz_ref, pb_ref)):
            p_ref[...] = lax.dot_general(h, w_ref[...], (((1,), (1,)), ((), ())), preferred_element_type=F32)

    row = lambda n: pl.BlockSpec((tm, n), lambda i: (i, 0))
    full = lambda a: pl.BlockSpec(a.shape, lambda i: (0, 0))
    return pl.pallas_call(
        body, name=name, grid=(S // tm,), in_specs=[row(D), full(norm_w), ANY] + [ANY] * len(after),
        out_specs=[row(D)] + [row(n) for n in _W_IN_ROWS],
        out_shape=[_sds((S, D), BF16)] + [_sds((S, n)) for n in _W_IN_ROWS],
        scratch_shapes=_w_in_scratch(D), compiler_params=_params(("arbitrary",)),
    )(x, norm_w, w_land, *after)


def _in_proj_dx(ds, w_land, x, norm_w, skip, *, name, after=(), tm=512):
    S, D = x.shape
    n = len(ds)

    def body(*refs):
        d_refs, land_ref = refs[:n], refs[n]
        x_ref, nw_ref, skip_ref = refs[n + 1:n + 4]
        o_ref, dw_ref, *scratch = refs[n + 4 + len(after):]
        w_refs = scratch[:n]
        i = pl.program_id(0)

        @pl.when(i == 0)
        def _():
            _fetch_w_in(land_ref, *scratch)

        dh = jnp.dot(d_refs[0][...], w_refs[0][...], preferred_element_type=F32)
        for d_ref, w_ref in zip(d_refs[1:], w_refs[1:]):
            dh = dh + jnp.dot(d_ref[...], w_ref[...], preferred_element_type=F32)
        dx, dw = _rms_bwd_rows(dh, x_ref[...], nw_ref[...])
        o_ref[...] = skip_ref[...] + dx

        @pl.when(i == 0)
        def _():
            dw_ref[...] = dw

        @pl.when(i > 0)
        def _():
            dw_ref[...] += dw

    row = lambda c: pl.BlockSpec((tm, c), lambda i: (i, 0))
    full = lambda a: pl.BlockSpec(a.shape, lambda i: (0, 0))
    return pl.pallas_call(
        body, name=name, grid=(S // tm,),
        in_specs=[row(d.shape[1]) for d in ds] + [ANY, row(D), full(norm_w), row(D)] + [ANY] * len(after),
        out_specs=[row(D), pl.BlockSpec((1, D), lambda i: (0, 0))], out_shape=[_sds((S, D)), _sds((1, D))],
        scratch_shapes=_w_in_scratch(D), compiler_params=_params(("arbitrary",)),
    )(*ds, w_land, x, norm_w, skip, *after)


def _out_proj_norm(a, w, x, norm_w, *, name, tm=512):
    S, D = x.shape

    def body(a_ref, w_ref, x_ref, nw_ref, x1_ref, h_ref):
        x1 = x_ref[...] + jnp.dot(a_ref[...], w_ref[...], preferred_element_type=F32)
        x1_ref[...] = x1
        r = lax.rsqrt(jnp.mean(x1 * x1, axis=-1, keepdims=True) + EPS)
        h_ref[...] = (x1 * r * nw_ref[...]).astype(BF16)

    row = pl.BlockSpec((tm, D), lambda i: (i, 0))
    return pl.pallas_call(
        body, name=name, grid=(S // tm,),
        in_specs=[pl.BlockSpec((tm, a.shape[1]), lambda i: (i, 0)), pl.BlockSpec(w.shape, lambda i: (0, 0)), row,
                  pl.BlockSpec((1, D), lambda i: (0, 0))],
        out_specs=[row, row], out_shape=[_sds((S, D)), _sds((S, D), BF16)], compiler_params=_params(("parallel",)),
    )(a, w, x, norm_w)


def _shifted(x, start, n):
    aligned = -(-start // HALO) * HALO
    assert aligned + n <= x.shape[0], (start, n, x.shape)
    return (x if aligned == start else pltpu.roll(x, aligned - start, axis=0))[aligned:aligned + n]


def _conv_rows(prev, cur, w, taps):
    n = cur.shape[0]
    xs = jnp.concatenate([prev, cur], axis=0)
    base = HALO - (taps - 1)
    out = _shifted(xs, base, n) * w[0:1]
    for i in range(1, taps):
        out = out + _shifted(xs, base + i, n) * w[i:i + 1]
    return out


def _conv_rows_bwd(cur_d, next_d, prev_x, cur_x, w, taps):
    n = cur_d.shape[0]
    ds = jnp.concatenate([cur_d, next_d], axis=0)
    dx = _shifted(ds, taps - 1, n) * w[0:1]
    for i in range(1, taps):
        dx = dx + _shifted(ds, taps - 1 - i, n) * w[i:i + 1]
    xs = jnp.concatenate([prev_x, cur_x], axis=0)
    base = HALO - (taps - 1)
    dws = [jnp.sum(cur_d * _shifted(xs, base + i, n), axis=0, keepdims=True) for i in range(taps)]
    return dx, jnp.concatenate(dws, axis=0)


def _halo_specs(tm, width, col, nblk):
    per = tm // HALO
    prev = pl.BlockSpec((HALO, width), lambda i, *_: (jnp.maximum(i * per - 1, 0), col))
    nxt = pl.BlockSpec((HALO, width), lambda i, *_: (jnp.minimum((i + 1) * per, nblk * per - 1), col))
    return prev, nxt


def _softplus(x):
    return jnp.maximum(x, 0.0) + jnp.log1p(jnp.exp(-jnp.abs(x)))


def _chunk_tri(tm, upper=False):
    r = lax.broadcasted_iota(jnp.int32, (tm, tm), 0)
    c = lax.broadcasted_iota(jnp.int32, (tm, tm), 1)
    same = lax.div(r, CHUNK) == lax.div(c, CHUNK)
    order = (c >= r) if upper else (c <= r)
    return jnp.where(same & order, 1.0, 0.0)


def _gdn_prep_fwd(proj_a, conv_w, a_log, dt_bias, *, name, tm=256):
    S = proj_a.shape[0]
    nblk = S // tm
    W3 = 3 * GDN_WIDTH

    def body(cur_ref, prev_ref, ba_ref, cw_ref, al_ref, dt_ref, qn_ref, kn_ref, v_ref, gcb_ref, bb_ref):
        i = pl.program_id(0)
        prev = jnp.where(i > 0, prev_ref[...], 0.0)
        c = _conv_rows(prev, cur_ref[...], cw_ref[...], GDN_CONV)
        a = c * _sigmoid(c)
        ba = ba_ref[...]
        lane = lax.broadcasted_iota(jnp.int32, (tm, 128), 1)
        g4 = jnp.zeros((tm, 128), F32)
        for h in range(GDN_HEADS):
            sl = slice(GDN_DIM * h, GDN_DIM * (h + 1))
            qh = a[:, GDN_DIM * h:GDN_DIM * (h + 1)]
            kh = a[:, GDN_WIDTH + GDN_DIM * h:GDN_WIDTH + GDN_DIM * (h + 1)]
            qn_ref[:, sl] = qh * (lax.rsqrt(jnp.sum(qh * qh, axis=-1, keepdims=True) + EPS) * (GDN_DIM ** -0.5))
            kn_ref[:, sl] = kh * lax.rsqrt(jnp.sum(kh * kh, axis=-1, keepdims=True) + EPS)
            beta = _sigmoid(ba[:, h:h + 1])
            bb_ref[:, sl] = jnp.broadcast_to(beta, (tm, GDN_DIM))
            g = -jnp.exp(al_ref[0:1, h:h + 1]) * _softplus(ba[:, GDN_HEADS + h:GDN_HEADS + h + 1] + dt_ref[0:1, h:h + 1])
            g4 = jnp.where(lane == h, g, g4)
        v_ref[...] = a[:, 2 * GDN_WIDTH:]
        gc = _exact_tri_dot(_chunk_tri(tm), g4)
        for h in range(GDN_HEADS):
            gcb_ref[:, GDN_DIM * h:GDN_DIM * (h + 1)] = jnp.broadcast_to(gc[:, h:h + 1], (tm, GDN_DIM))

    prev_spec, _ = _halo_specs(tm, W3, 0, nblk)
    row = pl.BlockSpec((tm, GDN_WIDTH), lambda i: (i, 0))
    small = lambda a: pl.BlockSpec(a.shape, lambda i: (0, 0))
    return pl.pallas_call(
        body, name=name, grid=(nblk,),
        in_specs=[pl.BlockSpec((tm, W3), lambda i: (i, 0)), prev_spec,
                  pl.BlockSpec((tm, 128), lambda i: (i, W3 // 128)), small(conv_w), small(a_log), small(dt_bias)],
        out_specs=[row] * 5, out_shape=[_sds((S, GDN_WIDTH))] * 5, compiler_params=_params(("parallel",)),
    )(proj_a, proj_a, proj_a, conv_w, a_log, dt_bias)


GDN_STACK = GDN_HEADS * CHUNK


def _stack(ref, rows):
    return jnp.concatenate([ref[rows, GDN_DIM * h:GDN_DIM * (h + 1)] for h in range(GDN_HEADS)], axis=0)


def _unstack_to(ref, rows, x):
    for h in range(GDN_HEADS):
        ref[rows, GDN_DIM * h:GDN_DIM * (h + 1)] = x[CHUNK * h:CHUNK * (h + 1)].astype(ref.dtype)


def _stack_masks():
    r = lax.broadcasted_iota(jnp.int32, (GDN_STACK, GDN_STACK), 0)
    c = lax.broadcasted_iota(jnp.int32, (GDN_STACK, GDN_STACK), 1)
    same = (r & -CHUNK) == (c & -CHUNK)
    return same & (r >= c), same & (r > c), r == c


def _stack_decay(gs, bs, incl):
    g2 = jnp.concatenate([gs, gs], axis=1)
    diff = g2 - g2.T
    dec = jnp.where(incl, jnp.exp(jnp.where(incl, diff, 0.0)), 0.0)
    return dec, jnp.concatenate([bs, bs], axis=1).T


def _head_mask():
    r = lax.broadcasted_iota(jnp.int32, (GDN_STACK, GDN_WIDTH), 0)
    c = lax.broadcasted_iota(jnp.int32, (GDN_STACK, GDN_WIDTH), 1)
    return (r & -CHUNK) * (GDN_DIM // CHUNK) == (c & -GDN_DIM)


def _head_spread(x):
    return jnp.where(_head_mask(), jnp.concatenate([x] * GDN_HEADS, axis=1), 0.0)


def _head_diag(x):
    xm = jnp.where(_head_mask(), x, 0.0)
    out = xm[:, 0:GDN_DIM]
    for h in range(1, GDN_HEADS):
        out = out + xm[:, GDN_DIM * h:GDN_DIM * (h + 1)]
    return out


def _last_rows(gs, n):
    return jnp.concatenate([jnp.broadcast_to(gs[CHUNK * (h + 1) - 1:CHUNK * (h + 1)], (n, GDN_DIM)) for h in range(GDN_HEADS)], axis=0)


def _gdn_chunk_fwd(qn, kn, v, gcb, bb, *, name):
    S = qn.shape[0]

    def body(qn_ref, kn_ref, v_ref, gcb_ref, bb_ref, uv_ref, wk_ref, at_ref, t_ref, wkb_ref, qdb_ref, keb_ref):
        incl, strict, diag = _stack_masks()
        for c in range(CHUNKS_PER_STEP):
            rows = slice(CHUNK * c, CHUNK * (c + 1))
            srows = slice(GDN_STACK * c, GDN_STACK * (c + 1))
            q, k, vv, gs, bs = [_stack(r, rows) for r in (qn_ref, kn_ref, v_ref, gcb_ref, bb_ref)]
            dec, bt = _stack_decay(gs, bs, incl)
            p = -jnp.where(strict, dec * _bdot_nt(k, k) * bt, 0.0)
            t = jnp.where(diag, 1.0, 0.0) + p
            for _ in range(5):
                p = _bdot(p, p)
                t = t + _bdot(t, p)
            sol = _dot3(t, jnp.concatenate([vv, jnp.exp(gs) * k], axis=1), ((1,), (0,)))
            _unstack_to(uv_ref, rows, sol[:, :GDN_DIM])
            _unstack_to(wk_ref, rows, sol[:, GDN_DIM:])
            at_ref[srows, :] = dec * _bdot_nt(q, k) * bt
            t_ref[srows, :] = t
            wkb_ref[srows, :] = _head_spread(sol[:, GDN_DIM:]).astype(BF16)
            qdb_ref[srows, :] = _head_spread(q * jnp.exp(gs)).astype(BF16)
            keb_ref[srows, :] = _head_spread(k * jnp.exp(_last_rows(gs, CHUNK) - gs) * bs).astype(BF16)

    step = CHUNKS_PER_STEP * CHUNK
    row = pl.BlockSpec((step, GDN_WIDTH), lambda n: (n, 0))
    sq = pl.BlockSpec((CHUNKS_PER_STEP * GDN_STACK, GDN_STACK), lambda n: (n, 0))
    wide = pl.BlockSpec((CHUNKS_PER_STEP * GDN_STACK, GDN_WIDTH), lambda n: (n, 0))
    nsq = S // CHUNK * GDN_STACK
    return pl.pallas_call(
        body, name=name, grid=(S // step,), in_specs=[row] * 5, out_specs=[row, row, sq, sq, wide, wide, wide],
        out_shape=[_sds((S, GDN_WIDTH)), _sds((S, GDN_WIDTH)), _sds((nsq, GDN_STACK)), _sds((nsq, GDN_STACK))]
        + [_sds((nsq, GDN_WIDTH), BF16)] * 3,
        compiler_params=_params(("parallel",)),
    )(qn, kn, v, gcb, bb)


SCAN_CHUNKS = 8


def _gdn_scan_fwd(uv, at, wkb, qdb, keb, gcb, proj_z, gnw, *, name):
    S = uv.shape[0]
    nc = S // CHUNK

    def body(uv_ref, at_ref, wkb_ref, qdb_ref, keb_ref, gcb_ref, z_ref, gnw_ref, o_ref, u_ref, sp_ref, oa_ref, st_ref):
        n = pl.program_id(0)

        @pl.when(n == 0)
        def _():
            st_ref[...] = jnp.zeros_like(st_ref)

        for c in range(SCAN_CHUNKS):
            rows = slice(CHUNK * c, CHUNK * (c + 1))
            srows = slice(GDN_STACK * c, GDN_STACK * (c + 1))
            st = st_ref[...]
            sp_ref[GDN_WIDTH * c:GDN_WIDTH * (c + 1), :] = st
            uv, gs, z = [_stack(r, rows) for r in (uv_ref, gcb_ref, z_ref)]
            u = uv - _bdot(wkb_ref[srows, :], st)
            o = _bdot(qdb_ref[srows, :], st) + _bdot(at_ref[srows, :], u)
            st_ref[...] = jnp.exp(_last_rows(gs, GDN_DIM)) * st + _bdot_tn(keb_ref[srows, :], u)
            _unstack_to(u_ref, rows, u)
            _unstack_to(o_ref, rows, o)
            r = lax.rsqrt(jnp.mean(o * o, axis=-1, keepdims=True) + EPS)
            oa = o * r * gnw_ref[...] * (z * _sigmoid(z))
            oa_ref[rows, :] = jnp.concatenate([oa[CHUNK * h:CHUNK * (h + 1)] for h in range(GDN_HEADS)], axis=1).astype(BF16)

    row = pl.BlockSpec((SCAN_CHUNKS * CHUNK, GDN_WIDTH), lambda n: (n, 0))
    sq = pl.BlockSpec((SCAN_CHUNKS * GDN_STACK, GDN_STACK), lambda n: (n, 0))
    wide = pl.BlockSpec((SCAN_CHUNKS * GDN_STACK, GDN_WIDTH), lambda n: (n, 0))
    return pl.pallas_call(
        body, name=name, grid=(nc // SCAN_CHUNKS,),
        in_specs=[row, sq, wide, wide, wide, row, row, pl.BlockSpec((1, GDN_DIM), lambda n: (0, 0))],
        out_specs=[row, row, pl.BlockSpec((SCAN_CHUNKS * GDN_WIDTH, GDN_DIM), lambda n: (n, 0)), row],
        out_shape=[_sds((S, GDN_WIDTH)), _sds((S, GDN_WIDTH)), _sds((nc * GDN_WIDTH, GDN_DIM)), _sds((S, 2 * GDN_WIDTH), BF16)],
        scratch_shapes=[pltpu.VMEM((GDN_WIDTH, GDN_DIM), F32)],
        compiler_params=_params(("arbitrary",)),
    )(uv, at, wkb, qdb, keb, gcb, proj_z, gnw)


def _gdn_scan_bwd(d_oab, o, proj_z, gnw, sp, u, at, wkb, qdb, keb, gcb, *, name, after=()):
    S = o.shape[0]
    nc = S // CHUNK
    ns = nc // SCAN_CHUNKS

    def body(do_ref, o_ref, z_ref, gnw_ref, sp_ref, u_ref, at_ref, wkb_ref, qdb_ref, keb_ref, gcb_ref, *rest):
        dz_ref, dgn_ref, du_ref, dwk_ref, dat_ref, dqd_ref, dke_ref, dgl_ref, ds_ref = rest[len(after):]
        n = pl.program_id(0)

        @pl.when(n == 0)
        def _():
            ds_ref[...] = jnp.zeros_like(ds_ref)
            dgn_ref[...] = jnp.zeros_like(dgn_ref)

        gw = gnw_ref[...]
        for c in reversed(range(SCAN_CHUNKS)):
            rows = slice(CHUNK * c, CHUNK * (c + 1))
            srows = slice(GDN_STACK * c, GDN_STACK * (c + 1))
            d_oa, oo, z, uu, gs = [_stack(r, rows) for r in (do_ref, o_ref, z_ref, u_ref, gcb_ref)]
            sg = _sigmoid(z)
            r = lax.rsqrt(jnp.mean(oo * oo, axis=-1, keepdims=True) + EPS)
            xh = oo * r
            dy = d_oa * (z * sg)
            _unstack_to(dz_ref, rows, d_oa * (xh * gw) * _dsilu(z, sg))
            dgn_ref[...] += jnp.sum(dy * xh, axis=0, keepdims=True)
            dxh = dy * gw
            do = r * (dxh - xh * jnp.mean(dxh * xh, axis=-1, keepdims=True))

            st = sp_ref[GDN_WIDTH * c:GDN_WIDTH * (c + 1), :]
            dst = ds_ref[...]
            ge = jnp.exp(_last_rows(gs, GDN_DIM))
            _unstack_to(dqd_ref, rows, _head_diag(_bdot_nt(do, st)))
            dat_ref[srows, :] = _bdot_nt(do, uu)
            du = _bdot_tn(at_ref[srows, :], do) + _bdot(keb_ref[srows, :], dst)
            _unstack_to(dke_ref, rows, _head_diag(_bdot_nt(uu, dst)))
            prod = dst * st
            for h in range(GDN_HEADS):
                blk = prod[GDN_DIM * h:GDN_DIM * (h + 1)]
                dge = jnp.sum(jnp.sum(blk, axis=1, keepdims=True), axis=0, keepdims=True)
                dgl_ref[c, :, GDN_DIM * h:GDN_DIM * (h + 1)] = jnp.broadcast_to(dge * ge[GDN_DIM * h:GDN_DIM * h + 1], (8, GDN_DIM))
            ds_ref[...] = _bdot_tn(qdb_ref[srows, :], do) + ge * dst - _bdot_tn(wkb_ref[srows, :], du)
            _unstack_to(du_ref, rows, du)
            _unstack_to(dwk_ref, rows, -_head_diag(_bdot_nt(du, st)))

    rev = lambda n: (ns - 1 - n, 0)
    row = pl.BlockSpec((SCAN_CHUNKS * CHUNK, GDN_WIDTH), rev)
    sq = pl.BlockSpec((SCAN_CHUNKS * GDN_STACK, GDN_STACK), rev)
    wide = pl.BlockSpec((SCAN_CHUNKS * GDN_STACK, GDN_WIDTH), rev)
    one = pl.BlockSpec((1, GDN_DIM), lambda n: (0, 0))
    return pl.pallas_call(
        body, name=name, grid=(ns,),
        in_specs=[row, row, row, one, pl.BlockSpec((SCAN_CHUNKS * GDN_WIDTH, GDN_DIM), rev), row, sq, wide, wide, wide, row]
        + [ANY] * len(after),
        out_specs=[row, one, row, row, sq, row, row, pl.BlockSpec((SCAN_CHUNKS, 8, GDN_WIDTH), lambda n: (ns - 1 - n, 0, 0))],
        out_shape=[_sds((S, GDN_WIDTH), BF16), _sds((1, GDN_DIM)), _sds((S, GDN_WIDTH)), _sds((S, GDN_WIDTH)),
                   _sds((nc * GDN_STACK, GDN_STACK)), _sds((S, GDN_WIDTH)), _sds((S, GDN_WIDTH)), _sds((nc, 8, GDN_WIDTH))],
        scratch_shapes=[pltpu.VMEM((GDN_WIDTH, GDN_DIM), F32)],
        compiler_params=_params(("arbitrary",)),
    )(d_oab, o, proj_z, gnw, sp, u, at, wkb, qdb, keb, gcb, *after)


def _gdn_chunk_bwd(qn, kn, gcb, bb, tmat, uv, wk, du, dwk, dat, dqd, dke, dgl, *, name):
    S = qn.shape[0]

    def body(qn_ref, kn_ref, gcb_ref, bb_ref, t_ref, uv_ref, wk_ref, du_ref, dwk_ref, dat_ref, dqd_ref, dke_ref,
             dgl_ref, dq_ref, dk_ref, dv_ref, dg_ref, dbeta_ref):
        incl, strict, _ = _stack_masks()
        lane = lax.broadcasted_iota(jnp.int32, (CHUNK, 128), 1)
        rowi = lax.broadcasted_iota(jnp.int32, (CHUNK, 1), 0)
        rsum = lambda x: jnp.sum(x, axis=-1, keepdims=True)
        for c in range(CHUNKS_PER_STEP):
            rows = slice(CHUNK * c, CHUNK * (c + 1))
            srows = slice(GDN_STACK * c, GDN_STACK * (c + 1))
            q, k, gs, bs, uv, wk, du, dwk, dqd, dke = [
                _stack(r, rows) for r in (qn_ref, kn_ref, gcb_ref, bb_ref, uv_ref, wk_ref, du_ref, dwk_ref, dqd_ref, dke_ref)]
            dec, bt = _stack_decay(gs, bs, incl)
            kk = _bdot_nt(k, k)
            qk = _bdot_nt(q, k)
            d_rhs = _dot3(t_ref[srows, :], jnp.concatenate([du, dwk], axis=1), ((0,), (0,)))
            sol = jnp.concatenate([uv, wk], axis=1)
            d_l = jnp.where(strict, -_dot3(d_rhs, sol, ((1,), (1,))), 0.0)
            d_a = jnp.where(incl, dat_ref[srows, :], 0.0)
            gam = jnp.exp(gs)
            e = jnp.exp(_last_rows(gs, CHUNK) - gs)
            d_gk = d_rhs[:, GDN_DIM:]
            ml = d_l * dec * bt
            ma = d_a * dec * bt
            _unstack_to(dq_ref, rows, _bdot(ma, k) + dqd * gam)
            _unstack_to(dk_ref, rows, _bdot(ml + ml.T, k) + _bdot_tn(ma, q) + d_gk * gam + dke * (e * bs))
            _unstack_to(dv_ref, rows, d_rhs[:, :GDN_DIM])
            wb = d_l * dec * kk + d_a * dec * qk
            ew = wb * bt
            s_ke = rsum(dke * k * (e * bs))
            dbeta = rsum(wb.T) + rsum(dke * k * e)
            dgc = rsum(ew) - rsum(ew.T) + rsum(dqd * q * gam) + rsum(d_gk * k * gam) - s_ke
            dgc4 = jnp.zeros((CHUNK, 128), F32)
            db4 = jnp.zeros((CHUNK, 128), F32)
            for h in range(GDN_HEADS):
                hr = slice(CHUNK * h, CHUNK * (h + 1))
                tail = jnp.sum(s_ke[hr], axis=0, keepdims=True) + dgl_ref[c, 0:1, GDN_DIM * h:GDN_DIM * h + 1]
                dgc4 = jnp.where(lane == h, dgc[hr] + jnp.where(rowi == CHUNK - 1, tail, 0.0), dgc4)
                db4 = jnp.where(lane == h, dbeta[hr], db4)
            dg_ref[rows, :] = _exact_tri_dot(_chunk_tri(CHUNK, upper=True), dgc4)
            dbeta_ref[rows, :] = db4

    step = CHUNKS_PER_STEP * CHUNK
    row = pl.BlockSpec((step, GDN_WIDTH), lambda n: (n, 0))
    sq = pl.BlockSpec((CHUNKS_PER_STEP * GDN_STACK, GDN_STACK), lambda n: (n, 0))
    col = pl.BlockSpec((step, 128), lambda n: (n, 0))
    return pl.pallas_call(
        body, name=name, grid=(S // step,),
        in_specs=[row] * 4 + [sq, row, row, row, row, sq, row, row,
                              pl.BlockSpec((CHUNKS_PER_STEP, 8, GDN_WIDTH), lambda n: (n, 0, 0))],
        out_specs=[row, row, row, col, col],
        out_shape=[_sds((S, GDN_WIDTH))] * 3 + [_sds((S, 128))] * 2, compiler_params=_params(("parallel",)),
    )(qn, kn, gcb, bb, tmat, uv, wk, du, dwk, dat, dqd, dke, dgl)


def _gdn_prep_bwd(dqn, dkn, dv, dg, dbeta, proj_a, conv_w, a_log, dt_bias, *, name, tm=256):
    S = proj_a.shape[0]
    nblk = S // tm
    W3 = 3 * GDN_WIDTH

    def body(dqn_ref, dkn_ref, dv_ref, dg_ref, dbeta_ref, cur_ref, prev_ref, ba_ref, cw_ref, al_ref, dt_ref,
             dc_ref, dba_ref, sm_ref):
        i = pl.program_id(0)
        prev = jnp.where(i > 0, prev_ref[...], 0.0)
        c = _conv_rows(prev, cur_ref[...], cw_ref[...], GDN_CONV)
        sg = _sigmoid(c)
        a = c * sg
        dsl = _dsilu(c, sg)
        ba = ba_ref[...]
        lane = lax.broadcasted_iota(jnp.int32, (tm, 128), 1)
        lane1 = lax.broadcasted_iota(jnp.int32, (1, 128), 1)
        dba = jnp.zeros((tm, 128), F32)
        sm = jnp.zeros((1, 128), F32)
        for h in range(GDN_HEADS):
            sl = slice(GDN_DIM * h, GDN_DIM * (h + 1))
            ks = slice(GDN_WIDTH + GDN_DIM * h, GDN_WIDTH + GDN_DIM * (h + 1))
            qh, kh = a[:, sl], a[:, ks]
            rq = lax.rsqrt(jnp.sum(qh * qh, axis=-1, keepdims=True) + EPS)
            rk = lax.rsqrt(jnp.sum(kh * kh, axis=-1, keepdims=True) + EPS)
            qhat, khat = qh * rq, kh * rk
            dyq = dqn_ref[:, sl] * (GDN_DIM ** -0.5)
            dyk = dkn_ref[:, sl]
            dq = rq * (dyq - qhat * jnp.sum(dyq * qhat, axis=-1, keepdims=True))
            dk = rk * (dyk - khat * jnp.sum(dyk * khat, axis=-1, keepdims=True))
            dc_ref[:, sl] = dq * dsl[:, sl]
            dc_ref[:, ks] = dk * dsl[:, ks]
            beta = _sigmoid(ba[:, h:h + 1])
            db = dbeta_ref[:, h:h + 1] * beta * (1.0 - beta)
            aneg = -jnp.exp(al_ref[0:1, h:h + 1])
            xa = ba[:, GDN_HEADS + h:GDN_HEADS + h + 1] + dt_ref[0:1, h:h + 1]
            dgh = dg_ref[:, h:h + 1]
            dxa = dgh * aneg * _sigmoid(xa)
            dba = jnp.where(lane == h, db, dba)
            dba = jnp.where(lane == GDN_HEADS + h, dxa, dba)
            d_alog = jnp.sum(dgh * _softplus(xa), axis=0, keepdims=True) * aneg
            sm = jnp.where(lane1 == h, d_alog, sm)
            sm = jnp.where(lane1 == GDN_HEADS + h, jnp.sum(dxa, axis=0, keepdims=True), sm)
        vs = slice(2 * GDN_WIDTH, W3)
        dc_ref[:, vs] = dv_ref[...] * dsl[:, vs]
        dba_ref[...] = dba

        @pl.when(i == 0)
        def _():
            sm_ref[...] = sm

        @pl.when(i > 0)
        def _():
            sm_ref[...] += sm

    prev_spec, _ = _halo_specs(tm, W3, 0, nblk)
    row = pl.BlockSpec((tm, GDN_WIDTH), lambda i: (i, 0))
    col = pl.BlockSpec((tm, 128), lambda i: (i, 0))
    small = lambda a: pl.BlockSpec(a.shape, lambda i: (0, 0))
    return pl.pallas_call(
        body, name=name, grid=(nblk,),
        in_specs=[row, row, row, col, col, pl.BlockSpec((tm, W3), lambda i: (i, 0)), prev_spec,
                  pl.BlockSpec((tm, 128), lambda i: (i, W3 // 128)), small(conv_w), small(a_log), small(dt_bias)],
        out_specs=[pl.BlockSpec((tm, W3), lambda i: (i, 0)), col, pl.BlockSpec((1, 128), lambda i: (0, 0))],
        out_shape=[_sds((S, W3)), _sds((S, 128)), _sds((1, 128))], compiler_params=_params(("arbitrary",)),
    )(dqn, dkn, dv, dg, dbeta, proj_a, proj_a, proj_a, conv_w, a_log, dt_bias)


def _gdn_conv_bwd(dc, dba, proj_a, conv_w, *, name, tm=256):
    S = proj_a.shape[0]
    nblk = S // tm
    W3 = 3 * GDN_WIDTH

    def body(dc_ref, dnext_ref, dba_ref, cur_ref, prev_ref, cw_ref, da_ref, dcw_ref):
        i = pl.program_id(0)
        prev = jnp.where(i > 0, prev_ref[...], 0.0)
        nxt = jnp.where(i < nblk - 1, dnext_ref[...], 0.0)
        dx, dw = _conv_rows_bwd(dc_ref[...], nxt, prev, cur_ref[...], cw_ref[...], GDN_CONV)
        da_ref[:, 0:W3] = dx.astype(BF16)
        da_ref[:, W3:] = dba_ref[...].astype(BF16)

        @pl.when(i == 0)
        def _():
            dcw_ref[...] = dw

        @pl.when(i > 0)
        def _():
            dcw_ref[...] += dw

    prev_spec, next_spec = _halo_specs(tm, W3, 0, nblk)
    wide = pl.BlockSpec((tm, W3), lambda i: (i, 0))
    return pl.pallas_call(
        body, name=name, grid=(nblk,),
        in_specs=[wide, next_spec, pl.BlockSpec((tm, 128), lambda i: (i, 0)), wide, prev_spec,
                  pl.BlockSpec(conv_w.shape, lambda i: (0, 0))],
        out_specs=[pl.BlockSpec((tm, A_COLS), lambda i: (i, 0)), pl.BlockSpec(conv_w.shape, lambda i: (0, 0))],
        out_shape=[_sds((S, A_COLS), BF16), _sds(conv_w.shape)], compiler_params=_params(("arbitrary",)),
    )(dc, dc, dba, proj_a, proj_a, conv_w)


def _band_mask(nk):
    i = lax.broadcasted_iota(jnp.int32, (2 * BAND, nk), 0) & (BAND - 1)
    j = lax.broadcasted_iota(jnp.int32, (2 * BAND, nk), 1)
    if nk == BAND:
        return j <= i
    return (j >= i) & (j <= i + BAND)


def _stack_heads(x, lo):
    return jnp.concatenate([jnp.where(lo, x, 0.0), jnp.where(lo, 0.0, x)], axis=0)


def _stack_cols(x):
    return jnp.concatenate([x[:, 0:1], x[:, DIL_DIM:DIL_DIM + 1]], axis=0)


def _unstack(x, lo):
    return jnp.where(lo, x[0:BAND], x[BAND:2 * BAND])


def _rows(start, size, stride):
    return pl.ds(start, size) if stride == 1 else pl.ds(start, size, stride=stride)


ATTN_LANES = 4


def _attn_blocks(S, visit_many, lanes=ATTN_LANES):
    for d in DILATIONS:
        nb = S // (d * BAND)
        if d == 1:
            half = nb // 2
            visit_many(d, [(0, 0, True), (0, half, False)])

            def pair(n, c):
                visit_many(1, [(0, n, False), (0, n + half, False)])
                return c
            lax.fori_loop(1, half, pair, 0)
        elif nb > 1:
            for r0 in range(0, d, lanes):
                visit_many(d, [(r0 + t, 0, True) for t in range(lanes)])

                def column(n, c, d=d, r0=r0):
                    visit_many(d, [(r0 + t, n, False) for t in range(lanes)])
                    return c
                lax.fori_loop(1, nb, column, 0)
        else:
            def group(g, c, d=d):
                visit_many(d, [(g * lanes + t, 0, True) for t in range(lanes)])
                return c
            lax.fori_loop(0, d // lanes, group, 0)


def _attn_fwd(proj_b, oab, *, name):
    S = proj_b.shape[0]
    scale = DIL_DIM ** -0.5

    def body(q_ref, k_ref, v_ref, oab_in_ref, ob_ref, lse_ref, m_ref, l_ref, acc_ref):
        del oab_in_ref
        lane = lax.broadcasted_iota(jnp.int32, (BAND, 128), 1)
        lo = lane < DIL_DIM
        m_ref[...] = jnp.full_like(m_ref, NEG_BIG)
        l_ref[...] = jnp.zeros_like(l_ref)
        acc_ref[...] = jnp.zeros_like(acc_ref)

        def load(d, r, n, first):
            nk = BAND if first else 2 * BAND
            qrows = _rows(r + n * (BAND * d), BAND, d)
            krows = _rows(r if first else r + (n - 1) * (BAND * d), nk, d)
            return dict(nk=nk, qrows=qrows, q=q_ref[qrows, :] * scale, k=k_ref[krows, :].astype(BF16),
                        v=v_ref[krows, :].astype(BF16), m=m_ref[qrows, :], l=l_ref[qrows, :], acc=acc_ref[qrows, :])

        def compute(b):
            q, k, v = b["q"], b["k"], b["v"]
            s = jnp.where(_band_mask(b["nk"]), _bdot_nt(_stack_heads(q, lo), k), NEG_BIG)
            m_old = _stack_cols(b["m"])
            m_new = jnp.maximum(m_old, jnp.max(s, axis=-1, keepdims=True))
            p = jnp.exp(s - m_new)
            alpha = _unstack(jnp.exp(m_old - m_new), lo)
            l_new = alpha * b["l"] + _unstack(jnp.sum(p, axis=-1, keepdims=True), lo)
            return _unstack(m_new, lo), l_new, alpha * b["acc"] + _unstack(_bdot(p, v), lo)

        def visit_many(d, blocks):
            loaded = [load(d, *blk) for blk in blocks]
            done = [compute(b) for b in loaded]
            for b, (m_new, l_new, acc_new) in zip(loaded, done):
                m_ref[b["qrows"], :] = m_new
                l_ref[b["qrows"], :] = l_new
                acc_ref[b["qrows"], :] = acc_new

        _attn_blocks(S, visit_many)
        ob_ref[...] = (acc_ref[...] / l_ref[...]).astype(BF16)
        lse_ref[...] = m_ref[...] + jnp.log(l_ref[...])

    part = lambda t: pl.BlockSpec((S, 128), lambda p: (0, 3 * p + t))
    return pl.pallas_call(
        body, name=name, grid=(DIL_PAIRS,),
        in_specs=[part(0), part(1), part(2), pl.BlockSpec(memory_space=pl.ANY)],
        out_specs=[pl.BlockSpec((S, 128), lambda p: (0, GDN_WIDTH // 128 + p)), pl.BlockSpec((S, 128), lambda p: (0, p))],
        out_shape=[_sds(oab.shape, BF16), _sds((S, DIL_WIDTH))],
        scratch_shapes=[pltpu.VMEM((S, 128), F32)] * 3, input_output_aliases={3: 0},
        compiler_params=_params(("parallel",)),
    )(proj_b, proj_b, proj_b, oab)


def _attn_bwd(proj_b, oab, d_oab, lse, *, name):
    S = proj_b.shape[0]
    scale = DIL_DIM ** -0.5

    def body(q_ref, k_ref, v_ref, o_ref, do_ref, lse_ref, dqkv_ref, dq_ref, dk_ref, dv_ref, delta_ref):
        lane = lax.broadcasted_iota(jnp.int32, (BAND, 128), 1)
        lo = lane < DIL_DIM
        dq_ref[...] = jnp.zeros_like(dq_ref)
        dk_ref[...] = jnp.zeros_like(dk_ref)
        dv_ref[...] = jnp.zeros_like(dv_ref)
        prod = do_ref[...] * o_ref[...].astype(F32)
        lo_all = lax.broadcasted_iota(jnp.int32, (S, 128), 1) < DIL_DIM
        delta_ref[...] = jnp.where(lo_all, jnp.sum(jnp.where(lo_all, prod, 0.0), axis=-1, keepdims=True),
                                   jnp.sum(jnp.where(lo_all, 0.0, prod), axis=-1, keepdims=True))

        def load(d, r, n, first):
            nk = BAND if first else 2 * BAND
            qrows = _rows(r + n * (BAND * d), BAND, d)
            krows = _rows(r if first else r + (n - 1) * (BAND * d), nk, d)
            return dict(nk=nk, qrows=qrows, krows=krows, q=q_ref[qrows, :] * scale, k=k_ref[krows, :], v=v_ref[krows, :],
                        do=do_ref[qrows, :], delta=delta_ref[qrows, :], lse=lse_ref[qrows, :],
                        dq=dq_ref[qrows, :], dk=dk_ref[krows, :], dv=dv_ref[krows, :])

        def compute(b):
            q, k, v, do = b["q"], b["k"], b["v"], b["do"]
            qs, dos = _stack_heads(q, lo), _stack_heads(do, lo)
            p = jnp.where(_band_mask(b["nk"]), jnp.exp(_bdot_nt(qs, k) - _stack_cols(b["lse"])), 0.0)
            ds = p * (_bdot_nt(dos, v) - _stack_cols(b["delta"]))
            dq = b["dq"] + _unstack(_bdot(ds, k), lo) * scale
            return dq, b["dk"] + _bdot_tn(ds, qs), b["dv"] + _bdot_tn(p, dos)

        def visit_many(d, blocks):
            loaded = [load(d, *blk) for blk in blocks]
            done = [compute(b) for b in loaded]
            for b, (dq, dk, dv) in zip(loaded, done):
                dq_ref[b["qrows"], :] = dq
                dk_ref[b["krows"], :] = dk
                dv_ref[b["krows"], :] = dv

        _attn_blocks(S, visit_many, lanes=2)
        dqkv_ref[:, 0:128] = dq_ref[...].astype(BF16)
        dqkv_ref[:, 128:256] = dk_ref[...].astype(BF16)
        dqkv_ref[:, 256:384] = dv_ref[...].astype(BF16)

    half = lambda p: (0, GDN_WIDTH // 128 + p)
    part = lambda t: pl.BlockSpec((S, 128), lambda p: (0, 3 * p + t))
    return pl.pallas_call(
        body, name=name, grid=(DIL_PAIRS,),
        in_specs=[part(0), part(1), part(2), pl.BlockSpec((S, 128), half), pl.BlockSpec((S, 128), half),
                  pl.BlockSpec((S, 128), lambda p: (0, p))],
        out_specs=pl.BlockSpec((S, 384), lambda p: (0, p)), out_shape=_sds((S, 3 * DIL_WIDTH), BF16),
        scratch_shapes=[pltpu.VMEM((S, 128), F32)] * 4, compiler_params=_params(("parallel",)),
    )(proj_b, proj_b, proj_b, oab, d_oab, lse)


FF_SLAB = 2 * D_FF // N_DEV
FF_PAIRS = N_DEV // 2
ROWS16 = 16


def _taps(w, x, base, n):
    out = _shifted(x, base, n) * w[0:1]
    for t in range(1, FFN_CONV):
        out = out + _shifted(x, base + t, n) * w[t:t + 1]
    return out


def _ffn_fwd(h2, x1, w_up, conv_w, w_down, final_w, tgt, *, name, tm=512):
    S, D = h2.shape
    ni = S // tm
    per = tm // ROWS16

    def body(h_ref, hp_ref, x1_ref, wg_ref, wu_ref, cg_ref, cu_ref, wd_ref, fw_ref, t_ref,
             dx_ref, dxb_ref, dfw_ref, loss_ref, ug_ref, uu_ref, x2_ref):
        i, j = pl.program_id(0), pl.program_id(1)
        hv = jnp.concatenate([hp_ref[...], h_ref[...]], axis=0)
        row = lax.broadcasted_iota(jnp.int32, (tm + ROWS16, 1), 0)
        keep = (i > 0) | (row >= ROWS16)

        def branch(w_ref, c_ref, u_ref):
            u = lax.dot_general(hv, w_ref[...], (((1,), (1,)), ((), ())), preferred_element_type=F32).astype(BF16)
            u_ref[...] = u[ROWS16:]
            return _taps(c_ref[...], jnp.where(keep, u.astype(F32), 0.0), ROWS16 - (FFN_CONV - 1), tm)

        gate = branch(wg_ref, cg_ref, ug_ref)
        up = branch(wu_ref, cu_ref, uu_ref)
        act = (gate * _sigmoid(gate) * up).astype(BF16)
        part = jnp.dot(act, wd_ref[...], preferred_element_type=F32)

        @pl.when(j == 0)
        def _():
            x2_ref[...] = x1_ref[...] + part

        @pl.when((j > 0) & (j < FF_PAIRS - 1))
        def _():
            x2_ref[...] += part

        @pl.when(j == FF_PAIRS - 1)
        def _():
            xv = x2_ref[...] + part
            wv = fw_ref[...]
            r = lax.rsqrt(jnp.mean(xv * xv, axis=-1, keepdims=True) + EPS)
            err = xv * r * wv - t_ref[...]
            lsum = jnp.sum(jnp.sum(err * err, axis=-1, keepdims=True), axis=0, keepdims=True) * (0.5 / D)
            g = err * (1.0 / D)
            xh = xv * r
            gw = g * wv
            dx = r * (gw - xh * jnp.mean(gw * xh, axis=-1, keepdims=True))
            dx_ref[...] = dx
            dxb_ref[...] = dx.astype(BF16)
            dfw = jnp.sum(g * xh, axis=0, keepdims=True)
            lpart = jnp.broadcast_to(lsum, (1, 128))

            @pl.when(i == 0)
            def _():
                dfw_ref[...] = dfw
                loss_ref[...] = lpart

            @pl.when(i > 0)
            def _():
                dfw_ref[...] += dfw
                loss_ref[...] += lpart

    rows = pl.BlockSpec((tm, D), lambda i, j: (i, 0))
    slab = lambda off: pl.BlockSpec((None, FF_SLAB, D), lambda i, j: (j + off, 0, 0))
    cslab = lambda off: pl.BlockSpec((None, FFN_CONV, FF_SLAB), lambda i, j: (j + off, 0, 0))
    uspec = pl.BlockSpec((None, tm, FF_SLAB), lambda i, j: (j, i, 0))
    return pl.pallas_call(
        body, name=name, grid=(ni, FF_PAIRS),
        in_specs=[rows, pl.BlockSpec((ROWS16, D), lambda i, j: (jnp.maximum(i * per - 1, 0), 0)), rows,
                  slab(0), slab(FF_PAIRS), cslab(0), cslab(FF_PAIRS), pl.BlockSpec((FF_SLAB, D), lambda i, j: (j, 0)),
                  pl.BlockSpec((1, D), lambda i, j: (0, 0)), rows],
        out_specs=[rows, rows, pl.BlockSpec((1, D), lambda i, j: (0, 0)), pl.BlockSpec((1, 128), lambda i, j: (0, 0)), uspec, uspec],
        out_shape=[_sds((S, D)), _sds((S, D), BF16), _sds((1, D)), _sds((1, 128)),
                   _sds((FF_PAIRS, S, FF_SLAB), BF16), _sds((FF_PAIRS, S, FF_SLAB), BF16)],
        scratch_shapes=[pltpu.VMEM((tm, D), F32)],
        compiler_params=_params(("arbitrary", "arbitrary")),
    )(h2, h2, x1, w_up, w_up, conv_w, conv_w, w_down, final_w, tgt)


def _ffn_bwd(dx2, h2, ug, uu, conv_w, w_down, *, name, tm=512):
    S, D = h2.shape
    ni = S // tm
    per = tm // ROWS16
    ext = tm + ROWS16

    def body(dx_ref, dxn_ref, h_ref, ug_ref, ugp_ref, ugn_ref, uu_ref, uup_ref, uun_ref, cg_ref, cu_ref, wd_ref,
             du_ref, gd_ref, gup_ref, dcw_ref, acc_d, acc_g, acc_u, acc_cg, acc_cu):
        i = pl.program_id(1)

        @pl.when(i == 0)
        def _():
            acc_d[...] = jnp.zeros_like(acc_d)
            acc_g[...] = jnp.zeros_like(acc_g)
            acc_u[...] = jnp.zeros_like(acc_u)
            acc_cg[...] = jnp.zeros_like(acc_cg)
            acc_cu[...] = jnp.zeros_like(acc_cu)

        dx = dx_ref[...]
        dxe = jnp.concatenate([dx, dxn_ref[...]], axis=0)
        row = lax.broadcasted_iota(jnp.int32, (ext, 1), 0)
        live = (i < ni - 1) | (row < tm)
        d_act = jnp.where(live, lax.dot_general(dxe, wd_ref[...], (((1,), (1,)), ((), ())), preferred_element_type=F32), 0.0)
        rowp = lax.broadcasted_iota(jnp.int32, (ext + ROWS16, 1), 0)
        keep = (i > 0) | (rowp >= ROWS16)

        def pre(cur, prev, nxt):
            return jnp.where(keep, jnp.concatenate([prev[...], cur[...], nxt[...]], axis=0).astype(F32), 0.0)

        uge, uue = pre(ug_ref, ugp_ref, ugn_ref), pre(uu_ref, uup_ref, uun_ref)
        cg, cu = cg_ref[...], cu_ref[...]
        base = ROWS16 - (FFN_CONV - 1)
        gate = _taps(cg, uge, base, ext)
        up = _taps(cu, uue, base, ext)
        sg = _sigmoid(gate)
        silu = gate * sg
        dgc = d_act * up * _dsilu(gate, sg)
        duc = d_act * silu

        def conv_t(w, dc):
            out = _shifted(dc, FFN_CONV - 1, tm) * w[0:1]
            for t in range(1, FFN_CONV):
                out = out + _shifted(dc, FFN_CONV - 1 - t, tm) * w[t:t + 1]
            return out.astype(BF16)

        du_g, du_u = conv_t(cg, dgc), conv_t(cu, duc)
        du_ref[0] = du_g
        du_ref[1] = du_u
        dcw = lambda dc, xe: jnp.concatenate(
            [jnp.sum(dc[0:tm] * _shifted(xe, base + t, tm), axis=0, keepdims=True) for t in range(FFN_CONV)], axis=0)
        acc_cg[0:FFN_CONV, :] += dcw(dgc, uge)
        acc_cu[0:FFN_CONV, :] += dcw(duc, uue)
        tn = (((0,), (0,)), ((), ()))
        act = (silu[0:tm] * up[0:tm]).astype(BF16)
        acc_d[...] += lax.dot_general(act, dx, tn, preferred_element_type=F32)
        hv = h_ref[...]
        acc_g[...] += lax.dot_general(du_g, hv, tn, preferred_element_type=F32)
        acc_u[...] += lax.dot_general(du_u, hv, tn, preferred_element_type=F32)

        @pl.when(i == ni - 1)
        def _():
            gd_ref[...] = acc_d[...].astype(BF16)
            gup_ref[0] = acc_g[...].astype(BF16)
            gup_ref[1] = acc_u[...].astype(BF16)
            dcw_ref[0] = acc_cg[0:FFN_CONV, :]
            dcw_ref[1] = acc_cu[0:FFN_CONV, :]

    last16 = S // ROWS16 - 1
    rows = pl.BlockSpec((tm, D), lambda j, i: (i, 0))
    rows_next = pl.BlockSpec((ROWS16, D), lambda j, i: (jnp.minimum((i + 1) * per, last16), 0))
    u_cur = pl.BlockSpec((None, tm, FF_SLAB), lambda j, i: (j, i, 0))
    u_prev = pl.BlockSpec((None, ROWS16, FF_SLAB), lambda j, i: (j, jnp.maximum(i * per - 1, 0), 0))
    u_next = pl.BlockSpec((None, ROWS16, FF_SLAB), lambda j, i: (j, jnp.minimum((i + 1) * per, last16), 0))
    cslab = lambda off: pl.BlockSpec((None, FFN_CONV, FF_SLAB), lambda j, i: (j + off, 0, 0))
    return pl.pallas_call(
        body, name=name, grid=(FF_PAIRS, ni),
        in_specs=[rows, rows_next, rows, u_cur, u_prev, u_next, u_cur, u_prev, u_next, cslab(0), cslab(FF_PAIRS),
                  pl.BlockSpec((FF_SLAB, D), lambda j, i: (j, 0))],
        out_specs=[pl.BlockSpec((None, 2, tm, FF_SLAB), lambda j, i: (j, 0, i, 0)), pl.BlockSpec((FF_SLAB, D), lambda j, i: (j, 0)),
                   pl.BlockSpec((None, 2, FF_SLAB, D), lambda j, i: (j, 0, 0, 0)),
                   pl.BlockSpec((None, 2, FFN_CONV, FF_SLAB), lambda j, i: (j, 0, 0, 0))],
        out_shape=[_sds((FF_PAIRS, 2, S, FF_SLAB), BF16), _sds((D_FF, D), BF16), _sds((FF_PAIRS, 2, FF_SLAB, D), BF16),
                   _sds((FF_PAIRS, 2, FFN_CONV, FF_SLAB))],
        scratch_shapes=[pltpu.VMEM((FF_SLAB, D), F32), pltpu.VMEM((FF_SLAB, D), F32), pltpu.VMEM((FF_SLAB, D), F32),
                        pltpu.VMEM((8, FF_SLAB), F32), pltpu.VMEM((8, FF_SLAB), F32)],
        compiler_params=_params(("parallel", "arbitrary")),
    )(dx2, dx2, h2, ug, ug, ug, uu, uu, uu, conv_w, conv_w, w_down)


def _pair_slot(p):
    return 2 * (p & (FF_PAIRS - 1)) + (p >> 2)


def _mm_slabs(a, w, *, name, res=None, norm_bwd=None, after=(), tm=1024, tn=1024):
    nk, S, _ = a.shape
    D = w.shape[2]
    has_res = res is not None
    has_norm = norm_bwd is not None
    assert not has_norm or tn == D

    def body(*refs):
        a_ref, w_ref = refs[:2]
        r_ref = refs[2] if has_res else None
        if has_norm:
            x_ref, nw_ref, skip_ref = refs[2 + has_res:5 + has_res]
            o_ref, dw_ref, acc_ref = refs[-3:]
        else:
            o_ref, acc_ref = refs[-2:]
        i, k = pl.program_id(0), pl.program_id(2)
        part = jnp.dot(a_ref[...], w_ref[...], preferred_element_type=F32)

        @pl.when(k == 0)
        def _():
            acc_ref[...] = part

        @pl.when(k > 0)
        def _():
            acc_ref[...] += part

        @pl.when(k == nk - 1)
        def _():
            r = acc_ref[...] + r_ref[...] if has_res else acc_ref[...]
            if has_norm:
                dx, dw = _rms_bwd_rows(r, x_ref[...], nw_ref[...])
                o_ref[...] = skip_ref[...] + dx

                @pl.when(i == 0)
                def _():
                    dw_ref[...] = dw

                @pl.when(i > 0)
                def _():
                    dw_ref[...] += dw
            else:
                o_ref[...] = r

    o_spec = pl.BlockSpec((tm, tn), lambda i, j, k: (i, j))
    one = pl.BlockSpec((1, tn), lambda i, j, k: (0, 0))
    return pl.pallas_call(
        body, name=name, grid=(S // tm, D // tn, nk),
        in_specs=[pl.BlockSpec((None, tm, FF_SLAB), lambda i, j, k: (k, i, 0)),
                  pl.BlockSpec((None, FF_SLAB, tn), lambda i, j, k: (FF_PAIRS * (k & 1) + (k >> 1), 0, j))] + [o_spec] * has_res
        + ([o_spec, one, o_spec] if has_norm else []) + [ANY] * len(after),
        out_specs=[o_spec, one] if has_norm else o_spec, out_shape=[_sds((S, D)), _sds((1, D))] if has_norm else _sds((S, D)),
        scratch_shapes=[pltpu.VMEM((tm, tn), F32)],
        compiler_params=_params(("arbitrary" if has_norm else "parallel", "parallel", "arbitrary")),
    )(*((a, w) + ((res,) if has_res else ()) + (tuple(norm_bwd) if has_norm else ()) + tuple(after)))


def _local_step(x, tgt, norm1_w, w_land, conv_a, a_log, dt_bias, gnw, norm2_w, final_w, late_weights, emit, start_after=()):
    wgrad = functools.partial(_mm, ta=True, out_dtype=BF16)
    h1, proj_a, proj_z, proj_b = _in_proj(x, norm1_w, w_land, after=start_after, name="in_proj")
    qn, kn, v, gcb, bb = _gdn_prep_fwd(proj_a, conv_a, a_log, dt_bias, name="gdn_prep_fwd")
    uv, wk, at, tmat, wkb, qdb, keb = _gdn_chunk_fwd(qn, kn, v, gcb, bb, name="gdn_chunk_fwd")
    o, u, sp, oab = _gdn_scan_fwd(uv, at, wkb, qdb, keb, gcb, proj_z, gnw, name="gdn_scan_fwd")
    oab, lse = _attn_fwd(proj_b, oab, name="attn_fwd")
    w_out, w_up, conv_f, w_down = late_weights(oab)
    x1, h2 = _out_proj_norm(oab, w_out, x, norm2_w, name="out_proj")
    dx2, dx2_b, d_final, loss, ug, uu = _ffn_fwd(h2, x1, w_up, conv_f, w_down, final_w, tgt, name="ffn_fwd")
    du, g_down, g_up, dcw = _ffn_bwd(dx2_b, h2, ug, uu, conv_f, w_down, name="ffn_bwd")
    token = emit("ffn", w_down=g_down, w_up=g_up.reshape(N_DEV, FF_SLAB, -1), conv_f=dcw.reshape(N_DEV, FFN_CONV, -1))
    dx1, d_norm2 = _mm_slabs(du.reshape(N_DEV, -1, FF_SLAB), w_up, norm_bwd=(x1, norm2_w, dx2), after=token, name="ffn_up_dx")
    d_oab = _mm(dx1, w_out, tb=True, name="out_proj_dx", tn=D_MODEL, tk=1024)
    token = emit("out", w_out=wgrad(oab, dx1, name="out_proj_dw", tm=D_MODEL, tn=D_MODEL))
    dz, d_gnw, du, dwk, dat, dqd, dke, dgl = _gdn_scan_bwd(d_oab, o, proj_z, gnw, sp, u, at, wkb, qdb, keb, gcb, after=token, name="gdn_scan_bwd")
    dqn, dkn, dv, dg, dbeta = _gdn_chunk_bwd(qn, kn, gcb, bb, tmat, uv, wk, du, dwk, dat, dqd, dke, dgl, name="gdn_chunk_bwd")
    dc, dba, d_small = _gdn_prep_bwd(dqn, dkn, dv, dg, dbeta, proj_a, conv_a, a_log, dt_bias, name="gdn_prep_bwd")
    d_pa, d_conv_a = _gdn_conv_bwd(dc, dba, proj_a, conv_a, name="gdn_conv_bwd")
    d_pb = _attn_bwd(proj_b, oab, d_oab, lse, name="attn_bwd")
    g_a = wgrad(d_pa, h1, name="proj_a_dw", tm=A_COLS, tn=D_MODEL)
    g_z = wgrad(dz, h1, name="proj_z_dw", tn=D_MODEL)
    g_b = wgrad(d_pb, h1, name="proj_b_dw", tm=768, tn=D_MODEL)
    token = emit("in", w_a=g_a, w_z=g_z, w_b=g_b, conv_a=d_conv_a)
    grad_x, d_norm1 = _in_proj_dx((d_pa, dz, d_pb), w_land, x, norm1_w, dx1, after=token, name="in_proj_dx")
    small = dict(norm1=d_norm1, small=d_small, gnw=d_gnw, norm2=d_norm2, final=d_final)
    return loss, grad_x, small


_O1 = 3 * GDN_WIDTH
_O2 = _O1 + GDN_WIDTH
_O3 = _O2 + 2 * GDN_HEADS


_W_IN_ROWS = (A_COLS, GDN_WIDTH, 3 * DIL_WIDTH)
_IN_ROWS = (_O3 + 3 * DIL_WIDTH) // N_DEV
_ROW_TILE = 16
_IN_STEP = _IN_ROWS - _IN_ROWS % _ROW_TILE
_GAP = 8
_LAND_ROWS = 464
assert _O3 % _ROW_TILE == _ROW_TILE - _GAP and N_DEV - 1 + _GAP + _IN_ROWS <= _LAND_ROWS and _LAND_ROWS % _ROW_TILE == 0


def _padded_row(r):
    return r + (_GAP if r >= _O3 else 0)


def _shifted_slab(w_rows, j, *, name):
    q = w_rows.shape[0] // _IN_ROWS

    def body(j_ref, w_ref, o_ref, pad_ref):
        pad_ref[...] = jnp.zeros_like(pad_ref)
        for dev in range(N_DEV):
            @pl.when(j_ref[0] == dev)
            def _(dev=dev):
                p = lax.broadcasted_iota(jnp.int32, (_LAND_ROWS, 1), 0) + _IN_STEP * dev
                for s in range(q):
                    pad_ref[0:_IN_ROWS, :] = w_ref[pl.ds(s, _IN_ROWS, stride=q), :]
                    rows = pad_ref[...]
                    a = pltpu.roll(rows, dev, 0) if dev else rows
                    b = pltpu.roll(rows, dev + _GAP, 0)
                    o_ref[:, 128 * s:128 * (s + 1)] = jnp.where(p < _O3, a, jnp.where(p >= _O3 + _GAP, b, 0.0)).astype(BF16)

    vm = pl.BlockSpec(memory_space=pltpu.VMEM)
    return pl.pallas_call(
        body, name=name, in_specs=[pl.BlockSpec(memory_space=pltpu.SMEM), vm], out_specs=vm,
        out_shape=_sds((_LAND_ROWS, 128 * q), BF16), scratch_shapes=[pltpu.VMEM((_LAND_ROWS, 128), F32)],
    )(j, w_rows)


def _w_in_plan():
    def dest(p):
        if p < _O1:
            return 0, p
        if p < _O2:
            return 1, p - _O1
        if p < _O2 + _ROW_TILE:
            return 0, _O1
        q = p - _O3 - _GAP
        t, pair = divmod(q // 128, DIL_PAIRS)
        return 2, (3 * pair + t) * 128 + q % 128

    spans = [(_padded_row(_IN_ROWS * j), _padded_row(_IN_ROWS * (j + 1) - 1) + 1) for j in range(N_DEV)]
    runs, seams = [], []
    for p in range(0, spans[-1][1], _ROW_TILE):
        owners = [j for j, (lo, hi) in enumerate(spans) if lo < p + _ROW_TILE and hi > p]
        w, r = dest(p)
        if len(owners) == 2:
            seams.append((w, r, owners[0], p - _IN_STEP * owners[0], owners[1], p - _IN_STEP * owners[1]))
            continue
        (j,) = owners
        last = runs[-1] if runs else None
        if last and last[0] == j and last[2] == w and last[3] + last[4] == r and last[1] + last[4] == p - _IN_STEP * j:
            runs[-1] = last[:4] + (last[4] + _ROW_TILE,)
        else:
            runs.append((j, p - _IN_STEP * j, w, r, _ROW_TILE))
    return runs, seams


def _w_in_scratch(d):
    return [pltpu.VMEM((n, d), BF16) for n in _W_IN_ROWS] + [pltpu.VMEM((N_DEV - 1, _ROW_TILE, d), BF16),
                                                              pltpu.SemaphoreType.DMA(())]


def _fetch_w_in(land_ref, wa_ref, wz_ref, wb_ref, seam_ref, sem):
    w_refs = (wa_ref, wz_ref, wb_ref)
    runs, seams = _w_in_plan()
    copies = [pltpu.make_async_copy(land_ref.at[j, pl.ds(s, n)], w_refs[w].at[pl.ds(r, n)], sem) for j, s, w, r, n in runs]
    for k, (w, r, j0, s0, j1, s1) in enumerate(seams):
        copies.append(pltpu.make_async_copy(land_ref.at[j0, pl.ds(s0, _ROW_TILE)], w_refs[w].at[pl.ds(r, _ROW_TILE)], sem))
        copies.append(pltpu.make_async_copy(land_ref.at[j1, pl.ds(s1, _ROW_TILE)], seam_ref.at[k], sem))
    for cp in copies:
        cp.start()
    tail = _O1 + _ROW_TILE
    wa_ref[tail:, :] = jnp.zeros((A_COLS - tail, wa_ref.shape[1]), BF16)
    for cp in copies:
        cp.wait()
    for k, (w, r, *_) in enumerate(seams):
        both = w_refs[w][r:r + _ROW_TILE, :].astype(F32) + seam_ref[k].astype(F32)
        w_refs[w][r:r + _ROW_TILE, :] = both.astype(BF16)


def _merge_g_in(g_a, g_z, g_b):
    d = g_a.shape[1]
    g_b = g_b.reshape(DIL_PAIRS, 3, 128, d).transpose(1, 0, 2, 3).reshape(3 * DIL_WIDTH, d)
    return jnp.concatenate([g_a[:_O1], g_z, g_a[_O1:_O1 + 2 * GDN_HEADS], g_b], axis=0)


MESH = pl.DeviceIdType.MESH
ANY = pl.BlockSpec(memory_space=pl.ANY)


def _position():
    return lax.axis_index("x"), lax.axis_index("y"), lax.axis_index("c")


def _slot(p):
    return 4 * p[0] + 2 * p[1] + p[2]


def _all_gather(blocks, *, name):
    n = len(blocks)

    def body(*refs):
        ins, outs = refs[:n], refs[n:2 * n]
        send_sems, recv_sems, local_sems = refs[2 * n:]
        x, y, c = _position()
        me, sibling = (x, y, c), (x, y, 1 - c)
        chips = [(1 - x, y), (x, 1 - y), (1 - x, 1 - y)]

        def copy(a, k, block, to, src=None):
            dst = outs[a].at[_slot(block)]
            return pltpu.make_async_remote_copy(
                src_ref=dst if src is None else src, dst_ref=dst, send_sem=send_sems.at[a, k], recv_sem=recv_sems.at[a, k],
                device_id=to, device_id_type=MESH)

        mine = [pltpu.make_async_copy(ins[a], outs[a].at[_slot(me)], local_sems.at[a]) for a in range(n)]
        for cp in mine:
            cp.start()
        first = []
        for a in range(n):
            first.append(copy(a, 0, me, sibling, src=ins[a]))
            first += [copy(a, 1 + j, me, (*chip, c), src=ins[a]) for j, chip in enumerate(chips)]
        for cp in first:
            cp.start()
        passed = []
        for j, chip in enumerate(chips):
            for a in range(n):
                copy(a, 1 + j, (*chip, c), me).wait_recv()
                fwd = copy(a, 4 + j, (*chip, c), sibling)
                fwd.start()
                passed.append(fwd)
        for a in range(n):
            copy(a, 0, sibling, me).wait_recv()
            for j, chip in enumerate(chips):
                copy(a, 4 + j, (*chip, 1 - c), me).wait_recv()
        for cp in first + passed:
            cp.wait_send()
        for cp in mine:
            cp.wait()

    return pl.pallas_call(
        body, name=name, in_specs=[ANY] * n, out_specs=[ANY] * n,
        out_shape=[_sds((N_DEV,) + b.shape, b.dtype) for b in blocks],
        scratch_shapes=[pltpu.SemaphoreType.DMA((n, 7)), pltpu.SemaphoreType.DMA((n, 7)), pltpu.SemaphoreType.DMA((n,))],
    )(*blocks)


def _gather_direct(block, *, name, after=()):
    def body(in_ref, *rest):
        out_ref, send_sems, recv_sems, local_sem = rest[len(after):]
        x, y, c = _position()
        me = _slot((x, y, c))
        mine = pltpu.make_async_copy(in_ref, out_ref.at[me], local_sem)
        mine.start()
        copies = [pltpu.make_async_remote_copy(
            src_ref=in_ref, dst_ref=out_ref.at[me], send_sem=send_sems.at[k - 1], recv_sem=recv_sems.at[k - 1],
            device_id=_peer_of(k, x, y, c), device_id_type=MESH) for k in range(1, N_DEV)]
        for cp in copies:
            cp.start()
        for cp in copies:
            cp.wait()
        mine.wait()

    return pl.pallas_call(
        body, name=name, in_specs=[pl.BlockSpec(memory_space=pltpu.VMEM)] + [ANY] * len(after),
        out_specs=pl.BlockSpec(memory_space=pltpu.VMEM),
        out_shape=_sds((N_DEV,) + block.shape, block.dtype),
        scratch_shapes=[pltpu.SemaphoreType.DMA((N_DEV - 1,)), pltpu.SemaphoreType.DMA((N_DEV - 1,)), pltpu.SemaphoreType.DMA],
    )(block, *after)


HBM = pl.BlockSpec(memory_space=pltpu.HBM)
SEM = pl.BlockSpec(memory_space=pltpu.SEMAPHORE)
EFFECT = pltpu.SideEffectType.DATAFLOW_SIDE_EFFECTING


def _peer_of(k, x, y, c):
    return (1 - x if k & 4 else x, 1 - y if k & 2 else y, 1 - c if k & 1 else c)


def _flight(a, k):
    return a * (N_DEV - 1) + k - 1


def _exchange_start(arrays, *, name, broadcast=False, paired=()):
    n = len(arrays)

    def body(*refs):
        ins, lands = refs[:n], refs[n:2 * n]
        send_sems, recv_sems = refs[2 * n:2 * n + 2]
        token = refs[-1]
        x, y, c = _position()
        me = _slot((x, y, c))
        for k in range(1, N_DEV):
            peer = _peer_of(k, x, y, c)
            for a in range(n):
                at = _pair_slot(_slot(peer)) if a in paired else _slot(peer)
                pltpu.make_async_remote_copy(
                    src_ref=ins[a] if broadcast else ins[a].at[at], dst_ref=lands[a].at[me],
                    send_sem=send_sems.at[_flight(a, k)], recv_sem=recv_sems.at[_flight(a, k)],
                    device_id=peer, device_id_type=MESH).start()
        token[...] = jnp.zeros_like(token)

    land_shapes = [((N_DEV,) + s.shape) if broadcast else s.shape for s in arrays]
    lands = [pltpu.with_memory_space_constraint(lax.empty(shp, s.dtype), pltpu.HBM) for shp, s in zip(land_shapes, arrays)]
    srcs = [pltpu.with_memory_space_constraint(s, pltpu.HBM) for s in arrays]
    outs = pl.pallas_call(
        body, name=name, in_specs=[HBM] * (2 * n),
        out_specs=[SEM, SEM] + [HBM] * (2 * n) + [pl.BlockSpec(memory_space=pltpu.VMEM)],
        out_shape=[pltpu.SemaphoreType.DMA((n * (N_DEV - 1),)), pltpu.SemaphoreType.DMA((n * (N_DEV - 1),))]
        + [pltpu.HBM(s.shape, s.dtype) for s in arrays] + [pltpu.HBM(shp, s.dtype) for shp, s in zip(land_shapes, arrays)]
        + [_sds((8, 128))],
        input_output_aliases={i: 2 + i for i in range(2 * n)},
        compiler_params=pltpu.CompilerParams(has_side_effects=EFFECT),
    )(*srcs, *lands)
    return outs[0], outs[1], outs[2:2 + n], outs[2 + n:2 + 2 * n], outs[-1]


def _exchange_wait(send_sems, recv_sems, srcs, lands, after, *, name, broadcast=False):
    n = len(srcs)

    def body(*refs):
        ins, lnd = refs[:n], refs[n:2 * n]
        send_ref, recv_ref = refs[2 * n:2 * n + 2]
        x, y, c = _position()
        for k in range(1, N_DEV):
            for a in range(n):
                cp = pltpu.make_async_remote_copy(
                    src_ref=ins[a] if broadcast else ins[a].at[0], dst_ref=lnd[a].at[0], send_sem=send_ref.at[_flight(a, k)],
                    recv_sem=recv_ref.at[_flight(a, k)], device_id=_peer_of(k, x, y, c), device_id_type=MESH)
                cp.wait_send()
                cp.wait_recv()

    outs = pl.pallas_call(
        body, name=name, in_specs=[HBM] * (2 * n) + [SEM, SEM, ANY], out_specs=[HBM] * (2 * n),
        out_shape=[pltpu.HBM(s.shape, s.dtype) for s in srcs] + [pltpu.HBM(s.shape, s.dtype) for s in lands],
        input_output_aliases={i: i for i in range(2 * n)},
        compiler_params=pltpu.CompilerParams(has_side_effects=EFFECT),
    )(*srcs, *lands, send_sems, recv_sems, after)
    return outs[:n], outs[n:]


def _with_own(landed, srcs, me):
    return [lax.dynamic_update_index_in_dim(l, o, me, 0) for l, o in zip(landed, srcs)]


def _adam_update(g, w, m, v):
    c1 = 1.0 - ADAM_B1 ** ADAM_STEP
    c2 = 1.0 - ADAM_B2 ** ADAM_STEP
    nm = ADAM_B1 * m + (1.0 - ADAM_B1) * g
    nv = ADAM_B2 * v + (1.0 - ADAM_B2) * (g * g)
    return -ADAM_LR * ((nm / c1) / (jnp.sqrt(nv / c2) + ADAM_EPS) + ADAM_WD * w), nm, nv


def _adamw(landed, sent, me, w, m, v, *, name, tr=None, tc=None):
    R, C = w.shape
    tr = R if tr is None else tr
    tc = C if tc is None else tc
    assert R % tr == 0 and C % tc == 0

    def body(me_ref, own_ref, p_ref, w_ref, m_ref, v_ref, g_ref, d_ref, nm_ref, nv_ref, token_ref):
        token_ref[...] = jnp.zeros_like(token_ref)
        g = own_ref[...].astype(F32)
        for s in range(N_DEV):
            g = g + jnp.where(me_ref[1] == s, 0.0, p_ref[s].astype(F32))
        delta, nm, nv = _adam_update(g, w_ref[...], m_ref[...], v_ref[...])
        g_ref[...] = g
        nm_ref[...] = nm
        nv_ref[...] = nv
        d_ref[...] = delta

    blk = pl.BlockSpec((tr, tc), lambda i, j, me_ref: (i, j))
    return pl.pallas_call(
        body, name=name,
        grid_spec=pltpu.PrefetchScalarGridSpec(
            num_scalar_prefetch=1, grid=(R // tr, C // tc),
            in_specs=[pl.BlockSpec((None, tr, tc), lambda i, j, me_ref: (me_ref[0], i, j)),
                      pl.BlockSpec((N_DEV, tr, tc), lambda i, j, me_ref: (0, i, j)), blk, blk, blk],
            out_specs=[blk] * 4 + [pl.BlockSpec((8, 128), lambda i, j, me_ref: (0, 0))]),
        out_shape=[_sds((R, C))] * 4 + [_sds((8, 128))],
        compiler_params=_params(("arbitrary", "arbitrary")),
    )(me, sent, landed, w, m, v)


def _adamw_rowwise(landed, sent, me, w, m, v, *, name, tr=128):
    _, R, C = landed.shape
    q = C // 128

    def body(me_ref, own_ref, p_ref, w_ref, m_ref, v_ref, g_ref, d_ref, nm_ref, nv_ref):
        g = own_ref[...].astype(F32)
        for s in range(N_DEV):
            g = g + jnp.where(me_ref[1] == s, 0.0, p_ref[s].astype(F32))
        for s in range(q):
            rows = pl.ds(s, tr, stride=q)
            gs = g[:, 128 * s:128 * (s + 1)]
            delta, nm, nv = _adam_update(gs, w_ref[rows, :], m_ref[rows, :], v_ref[rows, :])
            g_ref[rows, :] = gs
            nm_ref[rows, :] = nm
            nv_ref[rows, :] = nv
            d_ref[rows, :] = delta

    blk = pl.BlockSpec((tr * q, 128), lambda i, me_ref: (i, 0))
    return pl.pallas_call(
        body, name=name,
        grid_spec=pltpu.PrefetchScalarGridSpec(
            num_scalar_prefetch=1, grid=(pl.cdiv(R, tr),),
            in_specs=[pl.BlockSpec((None, tr, C), lambda i, me_ref: (me_ref[0], i, 0)),
                      pl.BlockSpec((N_DEV, tr, C), lambda i, me_ref: (0, i, 0)), blk, blk, blk],
            out_specs=[blk] * 4),
        out_shape=[_sds((R * q, 128))] * 4,
        compiler_params=_params(("arbitrary",)),
    )(me, sent, landed, w, m, v)


_SMALL_ROWS = 8
_SMALL_SLOTS =((0, 0, D_MODEL), (1, 0, D_MODEL), (2, 0, D_MODEL), (3, 0, GDN_DIM), (3, GDN_DIM, GDN_HEADS),
                (3, GDN_DIM + GDN_HEADS, GDN_HEADS))
_LOSS_LANE = 2 * GDN_DIM


def _pack_small(norm1, norm2, final, gnw, a_log, dt_bias, loss):
    row3 = jnp.concatenate([gnw, a_log, dt_bias, jnp.zeros((1, 128 - 2 * GDN_HEADS), F32), loss,
                            jnp.zeros((1, D_MODEL - 3 * 128), F32)], axis=1)
    return jnp.concatenate([norm1, norm2, final, row3, jnp.zeros((_SMALL_ROWS - 4, D_MODEL), F32)], axis=0)


def _adamw_small(packs, ws, ms, vs, *, name):
    n = len(ws)

    def body(p_ref, *refs):
        w_refs, m_refs, v_refs = refs[:n], refs[n:2 * n], refs[2 * n:3 * n]
        outs = refs[3 * n:]
        g_all = p_ref[0]
        for s in range(1, N_DEV):
            g_all = g_all + p_ref[s]
        for i, (row, lane, width) in enumerate(_SMALL_SLOTS):
            g = g_all[row:row + 1, lane:lane + width]
            delta, nm, nv = _adam_update(g, w_refs[i][...], m_refs[i][...], v_refs[i][...])
            for o_ref, val in zip(outs[4 * i:4 * i + 4], (g, delta, nm, nv)):
                o_ref[...] = val
        outs[-1][...] = g_all[3:4, _LOSS_LANE:_LOSS_LANE + 128]

    vm = pl.BlockSpec(memory_space=pltpu.VMEM)
    outs = pl.pallas_call(
        body, name=name, in_specs=[vm] * (1 + 3 * n), out_specs=[vm] * (4 * n + 1),
        out_shape=[_sds(w.shape) for w in ws for _ in range(4)] + [_sds((1, 128))],
    )(packs, *ws, *ms, *vs)
    return [outs[4 * i:4 * i + 4] for i in range(n)], outs[-1]


def _slabs_by_cols(g):
    r = g.shape[0]
    return g.reshape(r, N_DEV, -1).transpose(1, 0, 2)


def _cols_from_slabs(s):
    return s.transpose(1, 0, 2).reshape(s.shape[1], -1)


def kernel(x, norm1_w, w_in, conv_qkv_w, a_log, dt_bias, gdn_norm_w, w_out, norm2_w, w_up, ffn_conv_w, w_down, final_norm_w, loss_target, m_norm1_w, m_w_in, m_conv_qkv_w, m_a_log, m_dt_bias, m_gdn_norm_w, m_w_out, m_norm2_w, m_w_up, m_ffn_conv_w, m_w_down, m_final_norm_w, v_norm1_w, v_w_in, v_conv_qkv_w, v_a_log, v_dt_bias, v_gdn_norm_w, v_w_out, v_norm2_w, v_w_up, v_ffn_conv_w, v_w_down, v_final_norm_w):
    bf = lambda a: a.astype(BF16)
    me = _slot(_position())
    me1 = jnp.reshape(me, (1,)).astype(jnp.int32)
    t_in = lambda a: a[0].T
    rows = lambda a: a.reshape(D_MODEL // 128, 128, -1).transpose(2, 0, 1).reshape(-1, 128)
    gw_in, g_conv_a = _all_gather([_shifted_slab(rows(w_in), me1, name="shift_w_in"), conv_qkv_w[0]], name="gather_w_in")
    late_src, _ = lax.optimization_barrier(([bf(w_out[0]), bf(t_in(w_up)), bf(w_down[0]), ffn_conv_w[0]], gw_in))
    l_send, l_recv, l_srcs, l_lands, l_token = _exchange_start(late_src, name="weights_start", broadcast=True)

    def late_weights(after):
        srcs, landed = _exchange_wait(l_send, l_recv, l_srcs, l_lands, after, name="weights_wait", broadcast=True)
        gw_out, gw_up, gw_down, g_conv_f = _with_own(landed, srcs, me)
        return gw_out.reshape(D_MODEL, D_MODEL), gw_up, g_conv_f, gw_down.reshape(D_FF, D_MODEL)

    flights = {}

    def emit(group, **grads):
        paired = ()
        if group == "in":
            slabs = dict(w_in=_merge_g_in(grads["w_a"], grads["w_z"], grads["w_b"]).reshape(N_DEV, -1, D_MODEL),
                         conv_a=_slabs_by_cols(grads["conv_a"]))
        elif group == "ffn":
            slabs = dict(w_down=grads["w_down"].reshape(N_DEV, -1, D_MODEL), w_up=grads["w_up"], conv_f=grads["conv_f"])
            paired = (1, 2)
        else:
            slabs = {k: v.reshape(N_DEV, -1, D_MODEL) for k, v in grads.items()}
        names = list(slabs)
        *flight, token = _exchange_start([slabs[k] for k in names], paired=paired, name="grads_start_" + group)
        flights[group] = (names, flight)
        return (token,)

    loss, grad_x, g = _local_step(
        x[0], loss_target[0], norm1_w, gw_in, _cols_from_slabs(g_conv_a), a_log, dt_bias,
        gdn_norm_w, norm2_w, final_norm_w[None], late_weights, emit, start_after=(l_token,))
    got = {}

    def collect(group, after):
        names, (send_sems, recv_sems, srcs, lands) = flights[group]
        srcs, landed = _exchange_wait(send_sems, recv_sems, srcs, lands, after, name="grads_wait_" + group)
        got.update(zip(names, zip(landed, srcs)))

    def update(key, w, m, v, paired=False, **tiles):
        where = jnp.concatenate([_pair_slot(me1) if paired else me1, me1])
        return _adamw(*got[key], where, w, m, v, name="adamw_" + key, **tiles)

    collect("ffn", grad_x)
    collect("out", grad_x)
    *o_out, t1 = update("w_out", w_out[0], m_w_out[0], v_w_out[0])
    *o_up, t2 = update("w_up", t_in(w_up), t_in(m_w_up), t_in(v_w_up), paired=True, tr=176)
    o_up = [o.T for o in o_up]
    *o_down, t3 = update("w_down", w_down[0], m_w_down[0], v_w_down[0], tr=176)
    *o_cf, t4 = update("conv_f", ffn_conv_w[0], m_ffn_conv_w[0], v_ffn_conv_w[0], paired=True)
    pack = _pack_small(g["norm1"], g["norm2"], g["final"], g["gnw"], g["small"][:, 0:GDN_HEADS],
                       g["small"][:, GDN_HEADS:2 * GDN_HEADS], loss)
    small_all = _gather_direct(pack, after=(t1, t2, t3, t4), name="gather_small")
    collect("in", small_all)
    o_in =[o.reshape(-1, D_MODEL // 128, 128).transpose(1, 2, 0).reshape(D_MODEL, -1) for o in _adamw_rowwise(
        *got["w_in"], jnp.concatenate([me1, me1]), rows(w_in), rows(m_w_in), rows(v_w_in), name="adamw_w_in")]
    o_ca = update("conv_a", conv_qkv_w[0], m_conv_qkv_w[0], v_conv_qkv_w[0])
    (o_n1, o_n2, o_fin, o_gn, o_al, o_dt), total = _adamw_small(
        small_all, (norm1_w, norm2_w, final_norm_w[None], gdn_norm_w, a_log, dt_bias),
        (m_norm1_w, m_norm2_w, m_final_norm_w[None], m_gdn_norm_w, m_a_log, m_dt_bias),
        (v_norm1_w, v_norm2_w, v_final_norm_w[None], v_gdn_norm_w, v_a_log, v_dt_bias), name="adamw_small")
    outs = [total[0, 0], grad_x[None]]
    for k in range(4):
        outs += [o_n1[k], o_in[k][None], o_ca[k][None], o_al[k], o_dt[k], o_gn[k], o_out[k][None], o_n2[k], o_up[k][None],
                 o_cf[k][None], o_down[k][None], o_fin[k][0]]
    return tuple(outs)
```

```python
import functools

import jax
import jax.numpy as jnp
from jax import lax
from jax.experimental import pallas as pl
from jax.experimental.pallas import tpu as pltpu

F32 = jnp.float32
BF16 = jnp.bfloat16

N_DEV = 8
D_MODEL = 1024
GDN_HEADS = 4
GDN_DIM = 128
GDN_WIDTH = GDN_HEADS * GDN_DIM
GDN_CONV = 4
CHUNK = 64
CHUNKS_PER_STEP = 4
DIL_HEADS = 8
DIL_DIM = 64
DIL_WIDTH = DIL_HEADS * DIL_DIM
DIL_PAIRS = DIL_HEADS // 2
DILATIONS = (1, 4, 16)
BAND = 128
D_FF = 2816
FFN_CONV = 3
EPS = 1e-6
A_COLS = 3 * GDN_WIDTH + 128
HALO = 8

ADAM_LR = 0.001
ADAM_B1 = 0.9
ADAM_B2 = 0.999
ADAM_EPS = 1e-08
ADAM_WD = 0.01
ADAM_STEP = 10

VMEM_LIMIT_BYTES = 56 * 1024 * 1024
NEG_BIG = -1e30


def _params(sem=None):
    return pltpu.CompilerParams(dimension_semantics=sem, vmem_limit_bytes=VMEM_LIMIT_BYTES)


def _sds(shape, dtype=F32):
    return jax.ShapeDtypeStruct(shape, dtype)


def _bdot(a, b):
    return jnp.dot(a.astype(BF16), b.astype(BF16), preferred_element_type=F32)


def _bdot_nt(a, b):
    return lax.dot_general(a.astype(BF16), b.astype(BF16), (((1,), (1,)), ((), ())), preferred_element_type=F32)


def _bdot_tn(a, b):
    return lax.dot_general(a.astype(BF16), b.astype(BF16), (((0,), (0,)), ((), ())), preferred_element_type=F32)


def _split(a):
    hi = a.astype(BF16)
    lo = (a - hi.astype(F32)).astype(BF16)
    return hi, lo


def _dot3(a, b, dims):
    ah, al = _split(a)
    bh, bl = _split(b)
    d = functools.partial(lax.dot_general, dimension_numbers=(dims, ((), ())), preferred_element_type=F32)
    return d(ah, bh) + (d(al, bh) + d(ah, bl))


def _exact_tri_dot(tri, g):
    g1 = g.astype(BF16)
    r1 = g - g1.astype(F32)
    g2 = r1.astype(BF16)
    g3 = (r1 - g2.astype(F32)).astype(BF16)
    t = tri.astype(BF16)
    d = functools.partial(jnp.dot, preferred_element_type=F32)
    return d(t, g1) + (d(t, g2) + d(t, g3))


def _sigmoid(x):
    return 1.0 / (1.0 + jnp.exp(-x))


def _dsilu(x, sg):
    return sg * (1.0 + x * (1.0 - sg))


def _rms_bwd_rows(dh, x, w):
    r = lax.rsqrt(jnp.mean(x * x, axis=-1, keepdims=True) + EPS)
    xh = x * r
    gw = dh * w
    return r * (gw - xh * jnp.mean(gw * xh, axis=-1, keepdims=True)), jnp.sum(dh * xh, axis=0, keepdims=True)


def _mm(a, b, *, name, ta=False, tb=False, res=None, norm_bwd=None, after=(), out_dtype=F32, tm=512, tn=512, tk=512):
    if ta:
        K, M = a.shape
    else:
        M, K = a.shape
    if tb:
        N, Kb = b.shape
    else:
        Kb, N = b.shape
    assert K == Kb, (a.shape, b.shape)
    tm, tn, tk = min(tm, M), min(tn, N), min(tk, K)
    assert M % tm == 0 and N % tn == 0 and K % tk == 0, (name, M, N, K, tm, tn, tk)
    nk = K // tk
    dims = (((0 if ta else 1,), (1 if tb else 0,)), ((), ()))
    has_res = res is not None
    has_norm = norm_bwd is not None
    assert not has_norm or tn == N

    def body(*refs):
        a_ref, b_ref = refs[:2]
        r_ref = refs[2] if has_res else None
        if has_norm:
            x_ref, w_ref, skip_ref = refs[2 + has_res:5 + has_res]
            o_ref, dw_ref, acc_ref = refs[-3:]
        else:
            o_ref, acc_ref = refs[-2:]
        i, k = pl.program_id(0), pl.program_id(2)
        part = lax.dot_general(a_ref[...].astype(BF16), b_ref[...].astype(BF16), dims, preferred_element_type=F32)

        @pl.when(k == 0)
        def _():
            acc_ref[...] = part

        @pl.when(k > 0)
        def _():
            acc_ref[...] += part

        @pl.when(k == nk - 1)
        def _():
            r = acc_ref[...]
            if has_res:
                r = r + r_ref[...]
            if has_norm:
                dx, dw = _rms_bwd_rows(r, x_ref[...], w_ref[...])
                o_ref[...] = skip_ref[...] + dx

                @pl.when(i == 0)
                def _():
                    dw_ref[...] = dw

                @pl.when(i > 0)
                def _():
                    dw_ref[...] += dw
            else:
                o_ref[...] = r.astype(out_dtype)

    a_spec = pl.BlockSpec((tk, tm), lambda i, j, k: (k, i)) if ta else pl.BlockSpec((tm, tk), lambda i, j, k: (i, k))
    b_spec = pl.BlockSpec((tn, tk), lambda i, j, k: (j, k)) if tb else pl.BlockSpec((tk, tn), lambda i, j, k: (k, j))
    o_spec = pl.BlockSpec((tm, tn), lambda i, j, k: (i, j))
    one = pl.BlockSpec((1, tn), lambda i, j, k: (0, 0))
    in_specs = [a_spec, b_spec] + [o_spec] * has_res + ([o_spec, one, o_spec] if has_norm else []) + [ANY] * len(after)
    args = (a, b) + ((res,) if has_res else ()) + (tuple(norm_bwd) if has_norm else ()) + tuple(after)
    return pl.pallas_call(
        body, name=name, grid=(M // tm, N // tn, nk), in_specs=in_specs,
        out_specs=[o_spec, one] if has_norm else o_spec,
        out_shape=[_sds((M, N)), _sds((1, N))] if has_norm else _sds((M, N), out_dtype),
        scratch_shapes=[pltpu.VMEM((tm, tn), F32)],
        compiler_params=_params(("arbitrary" if has_norm else "parallel", "parallel", "arbitrary")),
    )(*args)


def _in_proj(x, norm_w, w_land, *, name, after=(), tm=512):
    S, D = x.shape

    def body(x_ref, nw_ref, land_ref, *rest):
        h_ref, pa_ref, pz_ref, pb_ref, *scratch = rest[len(after):]

        @pl.when(pl.program_id(0) == 0)
        def _():
            _fetch_w_in(land_ref, *scratch)

        xv = x_ref[...]
        r = lax.rsqrt(jnp.mean(xv * xv, axis=-1, keepdims=True) + EPS)
        h = (xv * r * nw_ref[...]).astype(BF16)
        h_ref[...] = h
        for w_ref, p_ref in zip(scratch[:3], (pa_ref, pz_ref, pb_ref)):
            p_ref[...] = lax.dot_general(h, w_ref[...], (((1,), (1,)), ((), ())), preferred_element_type=F32)

    row = lambda n: pl.BlockSpec((tm, n), lambda i: (i, 0))
    full = lambda a: pl.BlockSpec(a.shape, lambda i: (0, 0))
    return pl.pallas_call(
        body, name=name, grid=(S // tm,), in_specs=[row(D), full(norm_w), ANY] + [ANY] * len(after),
        out_specs=[row(D)] + [row(n) for n in _W_IN_ROWS],
        out_shape=[_sds((S, D), BF16)] + [_sds((S, n)) for n in _W_IN_ROWS],
        scratch_shapes=_w_in_scratch(D), compiler_params=_params(("arbitrary",)),
    )(x, norm_w, w_land, *after)


def _in_proj_dx(ds, w_land, x, norm_w, skip, *, name, after=(), tm=512):
    S, D = x.shape
    n = len(ds)

    def body(*refs):
        d_refs, land_ref = refs[:n], refs[n]
        x_ref, nw_ref, skip_ref = refs[n + 1:n + 4]
        o_ref, dw_ref, *scratch = refs[n + 4 + len(after):]
        w_refs = scratch[:n]
        i = pl.program_id(0)

        @pl.when(i == 0)
        def _():
            _fetch_w_in(land_ref, *scratch)

        dh = jnp.dot(d_refs[0][...], w_refs[0][...], preferred_element_type=F32)
        for d_ref, w_ref in zip(d_refs[1:], w_refs[1:]):
            dh = dh + jnp.dot(d_ref[...], w_ref[...], preferred_element_type=F32)
        dx, dw = _rms_bwd_rows(dh, x_ref[...], nw_ref[...])
        o_ref[...] = skip_ref[...] + dx

        @pl.when(i == 0)
        def _():
            dw_ref[...] = dw

        @pl.when(i > 0)
        def _():
            dw_ref[...] += dw

    row = lambda c: pl.BlockSpec((tm, c), lambda i: (i, 0))
    full = lambda a: pl.BlockSpec(a.shape, lambda i: (0, 0))
    return pl.pallas_call(
        body, name=name, grid=(S // tm,),
        in_specs=[row(d.shape[1]) for d in ds] + [ANY, row(D), full(norm_w), row(D)] + [ANY] * len(after),
        out_specs=[row(D), pl.BlockSpec((1, D), lambda i: (0, 0))], out_shape=[_sds((S, D)), _sds((1, D))],
        scratch_shapes=_w_in_scratch(D), compiler_params=_params(("arbitrary",)),
    )(*ds, w_land, x, norm_w, skip, *after)


def _out_proj_norm(a, w, x, norm_w, *, name, tm=512):
    S, D = x.shape

    def body(a_ref, w_ref, x_ref, nw_ref, x1_ref, h_ref):
        x1 = x_ref[...] + jnp.dot(a_ref[...], w_ref[...], preferred_element_type=F32)
        x1_ref[...] = x1
        r = lax.rsqrt(jnp.mean(x1 * x1, axis=-1, keepdims=True) + EPS)
        h_ref[...] = (x1 * r * nw_ref[...]).astype(BF16)

    row = pl.BlockSpec((tm, D), lambda i: (i, 0))
    return pl.pallas_call(
        body, name=name, grid=(S // tm,),
        in_specs=[pl.BlockSpec((tm, a.shape[1]), lambda i: (i, 0)), pl.BlockSpec(w.shape, lambda i: (0, 0)), row,
                  pl.BlockSpec((1, D), lambda i: (0, 0))],
        out_specs=[row, row], out_shape=[_sds((S, D)), _sds((S, D), BF16)], compiler_params=_params(("parallel",)),
    )(a, w, x, norm_w)


def _shifted(x, start, n):
    aligned = -(-start // HALO) * HALO
    assert aligned + n <= x.shape[0], (start, n, x.shape)
    return (x if aligned == start else pltpu.roll(x, aligned - start, axis=0))[aligned:aligned + n]


def _conv_rows(prev, cur, w, taps):
    n = cur.shape[0]
    xs = jnp.concatenate([prev, cur], axis=0)
    base = HALO - (taps - 1)
    out = _shifted(xs, base, n) * w[0:1]
    for i in range(1, taps):
        out = out + _shifted(xs, base + i, n) * w[i:i + 1]
    return out


def _conv_rows_bwd(cur_d, next_d, prev_x, cur_x, w, taps):
    n = cur_d.shape[0]
    ds = jnp.concatenate([cur_d, next_d], axis=0)
    dx = _shifted(ds, taps - 1, n) * w[0:1]
    for i in range(1, taps):
        dx = dx + _shifted(ds, taps - 1 - i, n) * w[i:i + 1]
    xs = jnp.concatenate([prev_x, cur_x], axis=0)
    base = HALO - (taps - 1)
    dws = [jnp.sum(cur_d * _shifted(xs, base + i, n), axis=0, keepdims=True) for i in range(taps)]
    return dx, jnp.concatenate(dws, axis=0)


def _halo_specs(tm, width, col, nblk):
    per = tm // HALO
    prev = pl.BlockSpec((HALO, width), lambda i, *_: (jnp.maximum(i * per - 1, 0), col))
    nxt = pl.BlockSpec((HALO, width), lambda i, *_: (jnp.minimum((i + 1) * per, nblk * per - 1), col))
    return prev, nxt


def _softplus(x):
    return jnp.maximum(x, 0.0) + jnp.log1p(jnp.exp(-jnp.abs(x)))


def _chunk_tri(tm, upper=False):
    r = lax.broadcasted_iota(jnp.int32, (tm, tm), 0)
    c = lax.broadcasted_iota(jnp.int32, (tm, tm), 1)
    same = lax.div(r, CHUNK) == lax.div(c, CHUNK)
    order = (c >= r) if upper else (c <= r)
    return jnp.where(same & order, 1.0, 0.0)


def _gdn_prep_fwd(proj_a, conv_w, a_log, dt_bias, *, name, tm=256):
    S = proj_a.shape[0]
    nblk = S // tm
    W3 = 3 * GDN_WIDTH

    def body(cur_ref, prev_ref, ba_ref, cw_ref, al_ref, dt_ref, qn_ref, kn_ref, v_ref, gcb_ref, bb_ref):
        i = pl.program_id(0)
        prev = jnp.where(i > 0, prev_ref[...], 0.0)
        c = _conv_rows(prev, cur_ref[...], cw_ref[...], GDN_CONV)
        a = c * _sigmoid(c)
        ba = ba_ref[...]
        lane = lax.broadcasted_iota(jnp.int32, (tm, 128), 1)
        g4 = jnp.zeros((tm, 128), F32)
        for h in range(GDN_HEADS):
            sl = slice(GDN_DIM * h, GDN_DIM * (h + 1))
            qh = a[:, GDN_DIM * h:GDN_DIM * (h + 1)]
            kh = a[:, GDN_WIDTH + GDN_DIM * h:GDN_WIDTH + GDN_DIM * (h + 1)]
            qn_ref[:, sl] = qh * (lax.rsqrt(jnp.sum(qh * qh, axis=-1, keepdims=True) + EPS) * (GDN_DIM ** -0.5))
            kn_ref[:, sl] = kh * lax.rsqrt(jnp.sum(kh * kh, axis=-1, keepdims=True) + EPS)
            beta = _sigmoid(ba[:, h:h + 1])
            bb_ref[:, sl] = jnp.broadcast_to(beta, (tm, GDN_DIM))
            g = -jnp.exp(al_ref[0:1, h:h + 1]) * _softplus(ba[:, GDN_HEADS + h:GDN_HEADS + h + 1] + dt_ref[0:1, h:h + 1])
            g4 = jnp.where(lane == h, g, g4)
        v_ref[...] = a[:, 2 * GDN_WIDTH:]
        gc = _exact_tri_dot(_chunk_tri(tm), g4)
        for h in range(GDN_HEADS):
            gcb_ref[:, GDN_DIM * h:GDN_DIM * (h + 1)] = jnp.broadcast_to(gc[:, h:h + 1], (tm, GDN_DIM))

    prev_spec, _ = _halo_specs(tm, W3, 0, nblk)
    row = pl.BlockSpec((tm, GDN_WIDTH), lambda i: (i, 0))
    small = lambda a: pl.BlockSpec(a.shape, lambda i: (0, 0))
    return pl.pallas_call(
        body, name=name, grid=(nblk,),
        in_specs=[pl.BlockSpec((tm, W3), lambda i: (i, 0)), prev_spec,
                  pl.BlockSpec((tm, 128), lambda i: (i, W3 // 128)), small(conv_w), small(a_log), small(dt_bias)],
        out_specs=[row] * 5, out_shape=[_sds((S, GDN_WIDTH))] * 5, compiler_params=_params(("parallel",)),
    )(proj_a, proj_a, proj_a, conv_w, a_log, dt_bias)


GDN_STACK = GDN_HEADS * CHUNK


def _stack(ref, rows):
    return jnp.concatenate([ref[rows, GDN_DIM * h:GDN_DIM * (h + 1)] for h in range(GDN_HEADS)], axis=0)


def _unstack_to(ref, rows, x):
    for h in range(GDN_HEADS):
        ref[rows, GDN_DIM * h:GDN_DIM * (h + 1)] = x[CHUNK * h:CHUNK * (h + 1)].astype(ref.dtype)


def _stack_masks():
    r = lax.broadcasted_iota(jnp.int32, (GDN_STACK, GDN_STACK), 0)
    c = lax.broadcasted_iota(jnp.int32, (GDN_STACK, GDN_STACK), 1)
    same = (r & -CHUNK) == (c & -CHUNK)
    return same & (r >= c), same & (r > c), r == c


def _stack_decay(gs, bs, incl):
    g2 = jnp.concatenate([gs, gs], axis=1)
    diff = g2 - g2.T
    dec = jnp.where(incl, jnp.exp(jnp.where(incl, diff, 0.0)), 0.0)
    return dec, jnp.concatenate([bs, bs], axis=1).T


def _head_mask():
    r = lax.broadcasted_iota(jnp.int32, (GDN_STACK, GDN_WIDTH), 0)
    c = lax.broadcasted_iota(jnp.int32, (GDN_STACK, GDN_WIDTH), 1)
    return (r & -CHUNK) * (GDN_DIM // CHUNK) == (c & -GDN_DIM)


def _head_spread(x):
    return jnp.where(_head_mask(), jnp.concatenate([x] * GDN_HEADS, axis=1), 0.0)


def _head_diag(x):
    xm = jnp.where(_head_mask(), x, 0.0)
    out = xm[:, 0:GDN_DIM]
    for h in range(1, GDN_HEADS):
        out = out + xm[:, GDN_DIM * h:GDN_DIM * (h + 1)]
    return out


def _last_rows(gs, n):
    return jnp.concatenate([jnp.broadcast_to(gs[CHUNK * (h + 1) - 1:CHUNK * (h + 1)], (n, GDN_DIM)) for h in range(GDN_HEADS)], axis=0)


def _gdn_chunk_fwd(qn, kn, v, gcb, bb, *, name):
    S = qn.shape[0]

    def body(qn_ref, kn_ref, v_ref, gcb_ref, bb_ref, uv_ref, wk_ref, at_ref, t_ref, wkb_ref, qdb_ref, keb_ref):
        incl, strict, diag = _stack_masks()
        for c in range(CHUNKS_PER_STEP):
            rows = slice(CHUNK * c, CHUNK * (c + 1))
            srows = slice(GDN_STACK * c, GDN_STACK * (c + 1))
            q, k, vv, gs, bs = [_stack(r, rows) for r in (qn_ref, kn_ref, v_ref, gcb_ref, bb_ref)]
            dec, bt = _stack_decay(gs, bs, incl)
            p = -jnp.where(strict, dec * _bdot_nt(k, k) * bt, 0.0)
            t = jnp.where(diag, 1.0, 0.0) + p
            for _ in range(5):
                p = _bdot(p, p)
                t = t + _bdot(t, p)
            sol = _dot3(t, jnp.concatenate([vv, jnp.exp(gs) * k], axis=1), ((1,), (0,)))
            _unstack_to(uv_ref, rows, sol[:, :GDN_DIM])
            _unstack_to(wk_ref, rows, sol[:, GDN_DIM:])
            at_ref[srows, :] = dec * _bdot_nt(q, k) * bt
            t_ref[srows, :] = t
            wkb_ref[srows, :] = _head_spread(sol[:, GDN_DIM:]).astype(BF16)
            qdb_ref[srows, :] = _head_spread(q * jnp.exp(gs)).astype(BF16)
            keb_ref[srows, :] = _head_spread(k * jnp.exp(_last_rows(gs, CHUNK) - gs) * bs).astype(BF16)

    step = CHUNKS_PER_STEP * CHUNK
    row = pl.BlockSpec((step, GDN_WIDTH), lambda n: (n, 0))
    sq = pl.BlockSpec((CHUNKS_PER_STEP * GDN_STACK, GDN_STACK), lambda n: (n, 0))
    wide = pl.BlockSpec((CHUNKS_PER_STEP * GDN_STACK, GDN_WIDTH), lambda n: (n, 0))
    nsq = S // CHUNK * GDN_STACK
    return pl.pallas_call(
        body, name=name, grid=(S // step,), in_specs=[row] * 5, out_specs=[row, row, sq, sq, wide, wide, wide],
        out_shape=[_sds((S, GDN_WIDTH)), _sds((S, GDN_WIDTH)), _sds((nsq, GDN_STACK)), _sds((nsq, GDN_STACK))]
        + [_sds((nsq, GDN_WIDTH), BF16)] * 3,
        compiler_params=_params(("parallel",)),
    )(qn, kn, v, gcb, bb)


SCAN_CHUNKS = 8


def _gdn_scan_fwd(uv, at, wkb, qdb, keb, gcb, proj_z, gnw, *, name):
    S = uv.shape[0]
    nc = S // CHUNK

    def body(uv_ref, at_ref, wkb_ref, qdb_ref, keb_ref, gcb_ref, z_ref, gnw_ref, o_ref, u_ref, sp_ref, oa_ref, st_ref):
        n = pl.program_id(0)

        @pl.when(n == 0)
        def _():
            st_ref[...] = jnp.zeros_like(st_ref)

        for c in range(SCAN_CHUNKS):
            rows = slice(CHUNK * c, CHUNK * (c + 1))
            srows = slice(GDN_STACK * c, GDN_STACK * (c + 1))
            st = st_ref[...]
            sp_ref[GDN_WIDTH * c:GDN_WIDTH * (c + 1), :] = st
            uv, gs, z = [_stack(r, rows) for r in (uv_ref, gcb_ref, z_ref)]
            u = uv - _bdot(wkb_ref[srows, :], st)
            o = _bdot(qdb_ref[srows, :], st) + _bdot(at_ref[srows, :], u)
            st_ref[...] = jnp.exp(_last_rows(gs, GDN_DIM)) * st + _bdot_tn(keb_ref[srows, :], u)
            _unstack_to(u_ref, rows, u)
            _unstack_to(o_ref, rows, o)
            r = lax.rsqrt(jnp.mean(o * o, axis=-1, keepdims=True) + EPS)
            oa = o * r * gnw_ref[...] * (z * _sigmoid(z))
            oa_ref[rows, :] = jnp.concatenate([oa[CHUNK * h:CHUNK * (h + 1)] for h in range(GDN_HEADS)], axis=1).astype(BF16)

    row = pl.BlockSpec((SCAN_CHUNKS * CHUNK, GDN_WIDTH), lambda n: (n, 0))
    sq = pl.BlockSpec((SCAN_CHUNKS * GDN_STACK, GDN_STACK), lambda n: (n, 0))
    wide = pl.BlockSpec((SCAN_CHUNKS * GDN_STACK, GDN_WIDTH), lambda n: (n, 0))
    return pl.pallas_call(
        body, name=name, grid=(nc // SCAN_CHUNKS,),
        in_specs=[row, sq, wide, wide, wide, row, row, pl.BlockSpec((1, GDN_DIM), lambda n: (0, 0))],
        out_specs=[row, row, pl.BlockSpec((SCAN_CHUNKS * GDN_WIDTH, GDN_DIM), lambda n: (n, 0)), row],
        out_shape=[_sds((S, GDN_WIDTH)), _sds((S, GDN_WIDTH)), _sds((nc * GDN_WIDTH, GDN_DIM)), _sds((S, 2 * GDN_WIDTH), BF16)],
        scratch_shapes=[pltpu.VMEM((GDN_WIDTH, GDN_DIM), F32)],
        compiler_params=_params(("arbitrary",)),
    )(uv, at, wkb, qdb, keb, gcb, proj_z, gnw)


def _gdn_scan_bwd(d_oab, o, proj_z, gnw, sp, u, at, wkb, qdb, keb, gcb, *, name, after=()):
    S = o.shape[0]
    nc = S // CHUNK
    ns = nc // SCAN_CHUNKS

    def body(do_ref, o_ref, z_ref, gnw_ref, sp_ref, u_ref, at_ref, wkb_ref, qdb_ref, keb_ref, gcb_ref, *rest):
        dz_ref, dgn_ref, du_ref, dwk_ref, dat_ref, dqd_ref, dke_ref, dgl_ref, ds_ref = rest[len(after):]
        n = pl.program_id(0)

        @pl.when(n == 0)
        def _():
            ds_ref[...] = jnp.zeros_like(ds_ref)
            dgn_ref[...] = jnp.zeros_like(dgn_ref)

        gw = gnw_ref[...]
        for c in reversed(range(SCAN_CHUNKS)):
            rows = slice(CHUNK * c, CHUNK * (c + 1))
            srows = slice(GDN_STACK * c, GDN_STACK * (c + 1))
            d_oa, oo, z, uu, gs = [_stack(r, rows) for r in (do_ref, o_ref, z_ref, u_ref, gcb_ref)]
            sg = _sigmoid(z)
            r = lax.rsqrt(jnp.mean(oo * oo, axis=-1, keepdims=True) + EPS)
            xh = oo * r
            dy = d_oa * (z * sg)
            _unstack_to(dz_ref, rows, d_oa * (xh * gw) * _dsilu(z, sg))
            dgn_ref[...] += jnp.sum(dy * xh, axis=0, keepdims=True)
            dxh = dy * gw
            do = r * (dxh - xh * jnp.mean(dxh * xh, axis=-1, keepdims=True))

            st = sp_ref[GDN_WIDTH * c:GDN_WIDTH * (c + 1), :]
            dst = ds_ref[...]
            ge = jnp.exp(_last_rows(gs, GDN_DIM))
            _unstack_to(dqd_ref, rows, _head_diag(_bdot_nt(do, st)))
            dat_ref[srows, :] = _bdot_nt(do, uu)
            du = _bdot_tn(at_ref[srows, :], do) + _bdot(keb_ref[srows, :], dst)
            _unstack_to(dke_ref, rows, _head_diag(_bdot_nt(uu, dst)))
            prod = dst * st
            for h in range(GDN_HEADS):
                blk = prod[GDN_DIM * h:GDN_DIM * (h + 1)]
                dge = jnp.sum(jnp.sum(blk, axis=1, keepdims=True), axis=0, keepdims=True)
                dgl_ref[c, :, GDN_DIM * h:GDN_DIM * (h + 1)] = jnp.broadcast_to(dge * ge[GDN_DIM * h:GDN_DIM * h + 1], (8, GDN_DIM))
            ds_ref[...] = _bdot_tn(qdb_ref[srows, :], do) + ge * dst - _bdot_tn(wkb_ref[srows, :], du)
            _unstack_to(du_ref, rows, du)
            _unstack_to(dwk_ref, rows, -_head_diag(_bdot_nt(du, st)))

    rev = lambda n: (ns - 1 - n, 0)
    row = pl.BlockSpec((SCAN_CHUNKS * CHUNK, GDN_WIDTH), rev)
    sq = pl.BlockSpec((SCAN_CHUNKS * GDN_STACK, GDN_STACK), rev)
    wide = pl.BlockSpec((SCAN_CHUNKS * GDN_STACK, GDN_WIDTH), rev)
    one = pl.BlockSpec((1, GDN_DIM), lambda n: (0, 0))
    return pl.pallas_call(
        body, name=name, grid=(ns,),
        in_specs=[row, row, row, one, pl.BlockSpec((SCAN_CHUNKS * GDN_WIDTH, GDN_DIM), rev), row, sq, wide, wide, wide, row]
        + [ANY] * len(after),
        out_specs=[row, one, row, row, sq, row, row, pl.BlockSpec((SCAN_CHUNKS, 8, GDN_WIDTH), lambda n: (ns - 1 - n, 0, 0))],
        out_shape=[_sds((S, GDN_WIDTH), BF16), _sds((1, GDN_DIM)), _sds((S, GDN_WIDTH)), _sds((S, GDN_WIDTH)),
                   _sds((nc * GDN_STACK, GDN_STACK)), _sds((S, GDN_WIDTH)), _sds((S, GDN_WIDTH)), _sds((nc, 8, GDN_WIDTH))],
        scratch_shapes=[pltpu.VMEM((GDN_WIDTH, GDN_DIM), F32)],
        compiler_params=_params(("arbitrary",)),
    )(d_oab, o, proj_z, gnw, sp, u, at, wkb, qdb, keb, gcb, *after)


def _gdn_chunk_bwd(qn, kn, gcb, bb, tmat, uv, wk, du, dwk, dat, dqd, dke, dgl, *, name):
    S = qn.shape[0]

    def body(qn_ref, kn_ref, gcb_ref, bb_ref, t_ref, uv_ref, wk_ref, du_ref, dwk_ref, dat_ref, dqd_ref, dke_ref,
             dgl_ref, dq_ref, dk_ref, dv_ref, dg_ref, dbeta_ref):
        incl, strict, _ = _stack_masks()
        lane = lax.broadcasted_iota(jnp.int32, (CHUNK, 128), 1)
        rowi = lax.broadcasted_iota(jnp.int32, (CHUNK, 1), 0)
        rsum = lambda x: jnp.sum(x, axis=-1, keepdims=True)
        for c in range(CHUNKS_PER_STEP):
            rows = slice(CHUNK * c, CHUNK * (c + 1))
            srows = slice(GDN_STACK * c, GDN_STACK * (c + 1))
            q, k, gs, bs, uv, wk, du, dwk, dqd, dke = [
                _stack(r, rows) for r in (qn_ref, kn_ref, gcb_ref, bb_ref, uv_ref, wk_ref, du_ref, dwk_ref, dqd_ref, dke_ref)]
            dec, bt = _stack_decay(gs, bs, incl)
            kk = _bdot_nt(k, k)
            qk = _bdot_nt(q, k)
            d_rhs = _dot3(t_ref[srows, :], jnp.concatenate([du, dwk], axis=1), ((0,), (0,)))
            sol = jnp.concatenate([uv, wk], axis=1)
            d_l = jnp.where(strict, -_dot3(d_rhs, sol, ((1,), (1,))), 0.0)
            d_a = jnp.where(incl, dat_ref[srows, :], 0.0)
            gam = jnp.exp(gs)
            e = jnp.exp(_last_rows(gs, CHUNK) - gs)
            d_gk = d_rhs[:, GDN_DIM:]
            ml = d_l * dec * bt
            ma = d_a * dec * bt
            _unstack_to(dq_ref, rows, _bdot(ma, k) + dqd * gam)
            _unstack_to(dk_ref, rows, _bdot(ml + ml.T, k) + _bdot_tn(ma, q) + d_gk * gam + dke * (e * bs))
            _unstack_to(dv_ref, rows, d_rhs[:, :GDN_DIM])
            wb = d_l * dec * kk + d_a * dec * qk
            ew = wb * bt
            s_ke = rsum(dke * k * (e * bs))
            dbeta = rsum(wb.T) + rsum(dke * k * e)
            dgc = rsum(ew) - rsum(ew.T) + rsum(dqd * q * gam) + rsum(d_gk * k * gam) - s_ke
            dgc4 = jnp.zeros((CHUNK, 128), F32)
            db4 = jnp.zeros((CHUNK, 128), F32)
            for h in range(GDN_HEADS):
                hr = slice(CHUNK * h, CHUNK * (h + 1))
                tail = jnp.sum(s_ke[hr], axis=0, keepdims=True) + dgl_ref[c, 0:1, GDN_DIM * h:GDN_DIM * h + 1]
                dgc4 = jnp.where(lane == h, dgc[hr] + jnp.where(rowi == CHUNK - 1, tail, 0.0), dgc4)
                db4 = jnp.where(lane == h, dbeta[hr], db4)
            dg_ref[rows, :] = _exact_tri_dot(_chunk_tri(CHUNK, upper=True), dgc4)
            dbeta_ref[rows, :] = db4

    step = CHUNKS_PER_STEP * CHUNK
    row = pl.BlockSpec((step, GDN_WIDTH), lambda n: (n, 0))
    sq = pl.BlockSpec((CHUNKS_PER_STEP * GDN_STACK, GDN_STACK), lambda n: (n, 0))
    col = pl.BlockSpec((step, 128), lambda n: (n, 0))
    return pl.pallas_call(
        body, name=name, grid=(S // step,),
        in_specs=[row] * 4 + [sq, row, row, row, row, sq, row, row,
                              pl.BlockSpec((CHUNKS_PER_STEP, 8, GDN_WIDTH), lambda n: (n, 0, 0))],
        out_specs=[row, row, row, col, col],
        out_shape=[_sds((S, GDN_WIDTH))] * 3 + [_sds((S, 128))] * 2, compiler_params=_params(("parallel",)),
    )(qn, kn, gcb, bb, tmat, uv, wk, du, dwk, dat, dqd, dke, dgl)


def _gdn_prep_bwd(dqn, dkn, dv, dg, dbeta, proj_a, conv_w, a_log, dt_bias, *, name, tm=256):
    S = proj_a.shape[0]
    nblk = S // tm
    W3 = 3 * GDN_WIDTH

    def body(dqn_ref, dkn_ref, dv_ref, dg_ref, dbeta_ref, cur_ref, prev_ref, ba_ref, cw_ref, al_ref, dt_ref,
             dc_ref, dba_ref, sm_ref):
        i = pl.program_id(0)
        prev = jnp.where(i > 0, prev_ref[...], 0.0)
        c = _conv_rows(prev, cur_ref[...], cw_ref[...], GDN_CONV)
        sg = _sigmoid(c)
        a = c * sg
        dsl = _dsilu(c, sg)
        ba = ba_ref[...]
        lane = lax.broadcasted_iota(jnp.int32, (tm, 128), 1)
        lane1 = lax.broadcasted_iota(jnp.int32, (1, 128), 1)
        dba = jnp.zeros((tm, 128), F32)
        sm = jnp.zeros((1, 128), F32)
        for h in range(GDN_HEADS):
            sl = slice(GDN_DIM * h, GDN_DIM * (h + 1))
            ks = slice(GDN_WIDTH + GDN_DIM * h, GDN_WIDTH + GDN_DIM * (h + 1))
            qh, kh = a[:, sl], a[:, ks]
            rq = lax.rsqrt(jnp.sum(qh * qh, axis=-1, keepdims=True) + EPS)
            rk = lax.rsqrt(jnp.sum(kh * kh, axis=-1, keepdims=True) + EPS)
            qhat, khat = qh * rq, kh * rk
            dyq = dqn_ref[:, sl] * (GDN_DIM ** -0.5)
            dyk = dkn_ref[:, sl]
            dq = rq * (dyq - qhat * jnp.sum(dyq * qhat, axis=-1, keepdims=True))
            dk = rk * (dyk - khat * jnp.sum(dyk * khat, axis=-1, keepdims=True))
            dc_ref[:, sl] = dq * dsl[:, sl]
            dc_ref[:, ks] = dk * dsl[:, ks]
            beta = _sigmoid(ba[:, h:h + 1])
            db = dbeta_ref[:, h:h + 1] * beta * (1.0 - beta)
            aneg = -jnp.exp(al_ref[0:1, h:h + 1])
            xa = ba[:, GDN_HEADS + h:GDN_HEADS + h + 1] + dt_ref[0:1, h:h + 1]
            dgh = dg_ref[:, h:h + 1]
            dxa = dgh * aneg * _sigmoid(xa)
            dba = jnp.where(lane == h, db, dba)
            dba = jnp.where(lane == GDN_HEADS + h, dxa, dba)
            d_alog = jnp.sum(dgh * _softplus(xa), axis=0, keepdims=True) * aneg
            sm = jnp.where(lane1 == h, d_alog, sm)
            sm = jnp.where(lane1 == GDN_HEADS + h, jnp.sum(dxa, axis=0, keepdims=True), sm)
        vs = slice(2 * GDN_WIDTH, W3)
        dc_ref[:, vs] = dv_ref[...] * dsl[:, vs]
        dba_ref[...] = dba

        @pl.when(i == 0)
        def _():
            sm_ref[...] = sm

        @pl.when(i > 0)
        def _():
            sm_ref[...] += sm

    prev_spec, _ = _halo_specs(tm, W3, 0, nblk)
    row = pl.BlockSpec((tm, GDN_WIDTH), lambda i: (i, 0))
    col = pl.BlockSpec((tm, 128), lambda i: (i, 0))
    small = lambda a: pl.BlockSpec(a.shape, lambda i: (0, 0))
    return pl.pallas_call(
        body, name=name, grid=(nblk,),
        in_specs=[row, row, row, col, col, pl.BlockSpec((tm, W3), lambda i: (i, 0)), prev_spec,
                  pl.BlockSpec((tm, 128), lambda i: (i, W3 // 128)), small(conv_w), small(a_log), small(dt_bias)],
        out_specs=[pl.BlockSpec((tm, W3), lambda i: (i, 0)), col, pl.BlockSpec((1, 128), lambda i: (0, 0))],
        out_shape=[_sds((S, W3)), _sds((S, 128)), _sds((1, 128))], compiler_params=_params(("arbitrary",)),
    )(dqn, dkn, dv, dg, dbeta, proj_a, proj_a, proj_a, conv_w, a_log, dt_bias)


def _gdn_conv_bwd(dc, dba, proj_a, conv_w, *, name, tm=256):
    S = proj_a.shape[0]
    nblk = S // tm
    W3 = 3 * GDN_WIDTH

    def body(dc_ref, dnext_ref, dba_ref, cur_ref, prev_ref, cw_ref, da_ref, dcw_ref):
        i = pl.program_id(0)
        prev = jnp.where(i > 0, prev_ref[...], 0.0)
        nxt = jnp.where(i < nblk - 1, dnext_ref[...], 0.0)
        dx, dw = _conv_rows_bwd(dc_ref[...], nxt, prev, cur_ref[...], cw_ref[...], GDN_CONV)
        da_ref[:, 0:W3] = dx.astype(BF16)
        da_ref[:, W3:] = dba_ref[...].astype(BF16)

        @pl.when(i == 0)
        def _():
            dcw_ref[...] = dw

        @pl.when(i > 0)
        def _():
            dcw_ref[...] += dw

    prev_spec, next_spec = _halo_specs(tm, W3, 0, nblk)
    wide = pl.BlockSpec((tm, W3), lambda i: (i, 0))
    return pl.pallas_call(
        body, name=name, grid=(nblk,),
        in_specs=[wide, next_spec, pl.BlockSpec((tm, 128), lambda i: (i, 0)), wide, prev_spec,
                  pl.BlockSpec(conv_w.shape, lambda i: (0, 0))],
        out_specs=[pl.BlockSpec((tm, A_COLS), lambda i: (i, 0)), pl.BlockSpec(conv_w.shape, lambda i: (0, 0))],
        out_shape=[_sds((S, A_COLS), BF16), _sds(conv_w.shape)], compiler_params=_params(("arbitrary",)),
    )(dc, dc, dba, proj_a, proj_a, conv_w)


def _band_mask(nk):
    i = lax.broadcasted_iota(jnp.int32, (2 * BAND, nk), 0) & (BAND - 1)
    j = lax.broadcasted_iota(jnp.int32, (2 * BAND, nk), 1)
    if nk == BAND:
        return j <= i
    return (j >= i) & (j <= i + BAND)


def _stack_heads(x, lo):
    return jnp.concatenate([jnp.where(lo, x, 0.0), jnp.where(lo, 0.0, x)], axis=0)


def _stack_cols(x):
    return jnp.concatenate([x[:, 0:1], x[:, DIL_DIM:DIL_DIM + 1]], axis=0)


def _unstack(x, lo):
    return jnp.where(lo, x[0:BAND], x[BAND:2 * BAND])


def _rows(start, size, stride):
    return pl.ds(start, size) if stride == 1 else pl.ds(start, size, stride=stride)


ATTN_LANES = 4


def _attn_blocks(S, visit_many, lanes=ATTN_LANES):
    for d in DILATIONS:
        nb = S // (d * BAND)
        if d == 1:
            half = nb // 2
            visit_many(d, [(0, 0, True), (0, half, False)])

            def pair(n, c):
                visit_many(1, [(0, n, False), (0, n + half, False)])
                return c
            lax.fori_loop(1, half, pair, 0)
        elif nb > 1:
            for r0 in range(0, d, lanes):
                visit_many(d, [(r0 + t, 0, True) for t in range(lanes)])

                def column(n, c, d=d, r0=r0):
                    visit_many(d, [(r0 + t, n, False) for t in range(lanes)])
                    return c
                lax.fori_loop(1, nb, column, 0)
        else:
            def group(g, c, d=d):
                visit_many(d, [(g * lanes + t, 0, True) for t in range(lanes)])
                return c
            lax.fori_loop(0, d // lanes, group, 0)


def _attn_fwd(proj_b, oab, *, name):
    S = proj_b.shape[0]
    scale = DIL_DIM ** -0.5

    def body(q_ref, k_ref, v_ref, oab_in_ref, ob_ref, lse_ref, m_ref, l_ref, acc_ref):
        del oab_in_ref
        lane = lax.broadcasted_iota(jnp.int32, (BAND, 128), 1)
        lo = lane < DIL_DIM
        m_ref[...] = jnp.full_like(m_ref, NEG_BIG)
        l_ref[...] = jnp.zeros_like(l_ref)
        acc_ref[...] = jnp.zeros_like(acc_ref)

        def load(d, r, n, first):
            nk = BAND if first else 2 * BAND
            qrows = _rows(r + n * (BAND * d), BAND, d)
            krows = _rows(r if first else r + (n - 1) * (BAND * d), nk, d)
            return dict(nk=nk, qrows=qrows, q=q_ref[qrows, :] * scale, k=k_ref[krows, :].astype(BF16),
                        v=v_ref[krows, :].astype(BF16), m=m_ref[qrows, :], l=l_ref[qrows, :], acc=acc_ref[qrows, :])

        def compute(b):
            q, k, v = b["q"], b["k"], b["v"]
            s = jnp.where(_band_mask(b["nk"]), _bdot_nt(_stack_heads(q, lo), k), NEG_BIG)
            m_old = _stack_cols(b["m"])
            m_new = jnp.maximum(m_old, jnp.max(s, axis=-1, keepdims=True))
            p = jnp.exp(s - m_new)
            alpha = _unstack(jnp.exp(m_old - m_new), lo)
            l_new = alpha * b["l"] + _unstack(jnp.sum(p, axis=-1, keepdims=True), lo)
            return _unstack(m_new, lo), l_new, alpha * b["acc"] + _unstack(_bdot(p, v), lo)

        def visit_many(d, blocks):
            loaded = [load(d, *blk) for blk in blocks]
            done = [compute(b) for b in loaded]
            for b, (m_new, l_new, acc_new) in zip(loaded, done):
                m_ref[b["qrows"], :] = m_new
                l_ref[b["qrows"], :] = l_new
                acc_ref[b["qrows"], :] = acc_new

        _attn_blocks(S, visit_many)
        ob_ref[...] = (acc_ref[...] / l_ref[...]).astype(BF16)
        lse_ref[...] = m_ref[...] + jnp.log(l_ref[...])

    part = lambda t: pl.BlockSpec((S, 128), lambda p: (0, 3 * p + t))
    return pl.pallas_call(
        body, name=name, grid=(DIL_PAIRS,),
        in_specs=[part(0), part(1), part(2), pl.BlockSpec(memory_space=pl.ANY)],
        out_specs=[pl.BlockSpec((S, 128), lambda p: (0, GDN_WIDTH // 128 + p)), pl.BlockSpec((S, 128), lambda p: (0, p))],
        out_shape=[_sds(oab.shape, BF16), _sds((S, DIL_WIDTH))],
        scratch_shapes=[pltpu.VMEM((S, 128), F32)] * 3, input_output_aliases={3: 0},
        compiler_params=_params(("parallel",)),
    )(proj_b, proj_b, proj_b, oab)


def _attn_bwd(proj_b, oab, d_oab, lse, *, name):
    S = proj_b.shape[0]
    scale = DIL_DIM ** -0.5

    def body(q_ref, k_ref, v_ref, o_ref, do_ref, lse_ref, dqkv_ref, dq_ref, dk_ref, dv_ref, delta_ref):
        lane = lax.broadcasted_iota(jnp.int32, (BAND, 128), 1)
        lo = lane < DIL_DIM
        dq_ref[...] = jnp.zeros_like(dq_ref)
        dk_ref[...] = jnp.zeros_like(dk_ref)
        dv_ref[...] = jnp.zeros_like(dv_ref)
        prod = do_ref[...] * o_ref[...].astype(F32)
        lo_all = lax.broadcasted_iota(jnp.int32, (S, 128), 1) < DIL_DIM
        delta_ref[...] = jnp.where(lo_all, jnp.sum(jnp.where(lo_all, prod, 0.0), axis=-1, keepdims=True),
                                   jnp.sum(jnp.where(lo_all, 0.0, prod), axis=-1, keepdims=True))

        def load(d, r, n, first):
            nk = BAND if first else 2 * BAND
            qrows = _rows(r + n * (BAND * d), BAND, d)
            krows = _rows(r if first else r + (n - 1) * (BAND * d), nk, d)
            return dict(nk=nk, qrows=qrows, krows=krows, q=q_ref[qrows, :] * scale, k=k_ref[krows, :], v=v_ref[krows, :],
                        do=do_ref[qrows, :], delta=delta_ref[qrows, :], lse=lse_ref[qrows, :],
                        dq=dq_ref[qrows, :], dk=dk_ref[krows, :], dv=dv_ref[krows, :])

        def compute(b):
            q, k, v, do = b["q"], b["k"], b["v"], b["do"]
            qs, dos = _stack_heads(q, lo), _stack_heads(do, lo)
            p = jnp.where(_band_mask(b["nk"]), jnp.exp(_bdot_nt(qs, k) - _stack_cols(b["lse"])), 0.0)
            ds = p * (_bdot_nt(dos, v) - _stack_cols(b["delta"]))
            dq = b["dq"] + _unstack(_bdot(ds, k), lo) * scale
            return dq, b["dk"] + _bdot_tn(ds, qs), b["dv"] + _bdot_tn(p, dos)

        def visit_many(d, blocks):
            loaded = [load(d, *blk) for blk in blocks]
            done = [compute(b) for b in loaded]
            for b, (dq, dk, dv) in zip(loaded, done):
                dq_ref[b["qrows"], :] = dq
                dk_ref[b["krows"], :] = dk
                dv_ref[b["krows"], :] = dv

        _attn_blocks(S, visit_many, lanes=2)
        dqkv_ref[:, 0:128] = dq_ref[...].astype(BF16)
        dqkv_ref[:, 128:256] = dk_ref[...].astype(BF16)
        dqkv_ref[:, 256:384] = dv_ref[...].astype(BF16)

    half = lambda p: (0, GDN_WIDTH // 128 + p)
    part = lambda t: pl.BlockSpec((S, 128), lambda p: (0, 3 * p + t))
    return pl.pallas_call(
        body, name=name, grid=(DIL_PAIRS,),
        in_specs=[part(0), part(1), part(2), pl.BlockSpec((S, 128), half), pl.BlockSpec((S, 128), half),
                  pl.BlockSpec((S, 128), lambda p: (0, p))],
        out_specs=pl.BlockSpec((S, 384), lambda p: (0, p)), out_shape=_sds((S, 3 * DIL_WIDTH), BF16),
        scratch_shapes=[pltpu.VMEM((S, 128), F32)] * 4, compiler_params=_params(("parallel",)),
    )(proj_b, proj_b, proj_b, oab, d_oab, lse)


FF_SLAB = 2 * D_FF // N_DEV
FF_PAIRS = N_DEV // 2
ROWS16 = 16


def _taps(w, x, base, n):
    out = _shifted(x, base, n) * w[0:1]
    for t in range(1, FFN_CONV):
        out = out + _shifted(x, base + t, n) * w[t:t + 1]
    return out


def _ffn_fwd(h2, x1, w_up, conv_w, w_down, final_w, tgt, *, name, tm=512):
    S, D = h2.shape
    ni = S // tm
    per = tm // ROWS16

    def body(h_ref, hp_ref, x1_ref, wg_ref, wu_ref, cg_ref, cu_ref, wd_ref, fw_ref, t_ref,
             dx_ref, dxb_ref, dfw_ref, loss_ref, ug_ref, uu_ref, x2_ref):
        i, j = pl.program_id(0), pl.program_id(1)
        hv = jnp.concatenate([hp_ref[...], h_ref[...]], axis=0)
        row = lax.broadcasted_iota(jnp.int32, (tm + ROWS16, 1), 0)
        keep = (i > 0) | (row >= ROWS16)

        def branch(w_ref, c_ref, u_ref):
            u = lax.dot_general(hv, w_ref[...], (((1,), (1,)), ((), ())), preferred_element_type=F32).astype(BF16)
            u_ref[...] = u[ROWS16:]
            return _taps(c_ref[...], jnp.where(keep, u.astype(F32), 0.0), ROWS16 - (FFN_CONV - 1), tm)

        gate = branch(wg_ref, cg_ref, ug_ref)
        up = branch(wu_ref, cu_ref, uu_ref)
        act = (gate * _sigmoid(gate) * up).astype(BF16)
        part = jnp.dot(act, wd_ref[...], preferred_element_type=F32)

        @pl.when(j == 0)
        def _():
            x2_ref[...] = x1_ref[...] + part

        @pl.when((j > 0) & (j < FF_PAIRS - 1))
        def _():
            x2_ref[...] += part

        @pl.when(j == FF_PAIRS - 1)
        def _():
            xv = x2_ref[...] + part
            wv = fw_ref[...]
            r = lax.rsqrt(jnp.mean(xv * xv, axis=-1, keepdims=True) + EPS)
            err = xv * r * wv - t_ref[...]
            lsum = jnp.sum(jnp.sum(err * err, axis=-1, keepdims=True), axis=0, keepdims=True) * (0.5 / D)
            g = err * (1.0 / D)
            xh = xv * r
            gw = g * wv
            dx = r * (gw - xh * jnp.mean(gw * xh, axis=-1, keepdims=True))
            dx_ref[...] = dx
            dxb_ref[...] = dx.astype(BF16)
            dfw = jnp.sum(g * xh, axis=0, keepdims=True)
            lpart = jnp.broadcast_to(lsum, (1, 128))

            @pl.when(i == 0)
            def _():
                dfw_ref[...] = dfw
                loss_ref[...] = lpart

            @pl.when(i > 0)
            def _():
                dfw_ref[...] += dfw
                loss_ref[...] += lpart

    rows = pl.BlockSpec((tm, D), lambda i, j: (i, 0))
    slab = lambda off: pl.BlockSpec((None, FF_SLAB, D), lambda i, j: (j + off, 0, 0))
    cslab = lambda off: pl.BlockSpec((None, FFN_CONV, FF_SLAB), lambda i, j: (j + off, 0, 0))
    uspec = pl.BlockSpec((None, tm, FF_SLAB), lambda i, j: (j, i, 0))
    return pl.pallas_call(
        body, name=name, grid=(ni, FF_PAIRS),
        in_specs=[rows, pl.BlockSpec((ROWS16, D), lambda i, j: (jnp.maximum(i * per - 1, 0), 0)), rows,
                  slab(0), slab(FF_PAIRS), cslab(0), cslab(FF_PAIRS), pl.BlockSpec((FF_SLAB, D), lambda i, j: (j, 0)),
                  pl.BlockSpec((1, D), lambda i, j: (0, 0)), rows],
        out_specs=[rows, rows, pl.BlockSpec((1, D), lambda i, j: (0, 0)), pl.BlockSpec((1, 128), lambda i, j: (0, 0)), uspec, uspec],
        out_shape=[_sds((S, D)), _sds((S, D), BF16), _sds((1, D)), _sds((1, 128)),
                   _sds((FF_PAIRS, S, FF_SLAB), BF16), _sds((FF_PAIRS, S, FF_SLAB), BF16)],
        scratch_shapes=[pltpu.VMEM((tm, D), F32)],
        compiler_params=_params(("arbitrary", "arbitrary")),
    )(h2, h2, x1, w_up, w_up, conv_w, conv_w, w_down, final_w, tgt)


def _ffn_bwd(dx2, h2, ug, uu, conv_w, w_down, *, name, tm=512):
    S, D = h2.shape
    ni = S // tm
    per = tm // ROWS16
    ext = tm + ROWS16

    def body(dx_ref, dxn_ref, h_ref, ug_ref, ugp_ref, ugn_ref, uu_ref, uup_ref, uun_ref, cg_ref, cu_ref, wd_ref,
             du_ref, gd_ref, gup_ref, dcw_ref, acc_d, acc_g, acc_u, acc_cg, acc_cu):
        i = pl.program_id(1)

        @pl.when(i == 0)
        def _():
            acc_d[...] = jnp.zeros_like(acc_d)
            acc_g[...] = jnp.zeros_like(acc_g)
            acc_u[...] = jnp.zeros_like(acc_u)
            acc_cg[...] = jnp.zeros_like(acc_cg)
            acc_cu[...] = jnp.zeros_like(acc_cu)

        dx = dx_ref[...]
        dxe = jnp.concatenate([dx, dxn_ref[...]], axis=0)
        row = lax.broadcasted_iota(jnp.int32, (ext, 1), 0)
        live = (i < ni - 1) | (row < tm)
        d_act = jnp.where(live, lax.dot_general(dxe, wd_ref[...], (((1,), (1,)), ((), ())), preferred_element_type=F32), 0.0)
        rowp = lax.broadcasted_iota(jnp.int32, (ext + ROWS16, 1), 0)
        keep = (i > 0) | (rowp >= ROWS16)

        def pre(cur, prev, nxt):
            return jnp.where(keep, jnp.concatenate([prev[...], cur[...], nxt[...]], axis=0).astype(F32), 0.0)

        uge, uue = pre(ug_ref, ugp_ref, ugn_ref), pre(uu_ref, uup_ref, uun_ref)
        cg, cu = cg_ref[...], cu_ref[...]
        base = ROWS16 - (FFN_CONV - 1)
        gate = _taps(cg, uge, base, ext)
        up = _taps(cu, uue, base, ext)
        sg = _sigmoid(gate)
        silu = gate * sg
        dgc = d_act * up * _dsilu(gate, sg)
        duc = d_act * silu

        def conv_t(w, dc):
            out = _shifted(dc, FFN_CONV - 1, tm) * w[0:1]
            for t in range(1, FFN_CONV):
                out = out + _shifted(dc, FFN_CONV - 1 - t, tm) * w[t:t + 1]
            return out.astype(BF16)

        du_g, du_u = conv_t(cg, dgc), conv_t(cu, duc)
        du_ref[0] = du_g
        du_ref[1] = du_u
        dcw = lambda dc, xe: jnp.concatenate(
            [jnp.sum(dc[0:tm] * _shifted(xe, base + t, tm), axis=0, keepdims=True) for t in range(FFN_CONV)], axis=0)
        acc_cg[0:FFN_CONV, :] += dcw(dgc, uge)
        acc_cu[0:FFN_CONV, :] += dcw(duc, uue)
        tn = (((0,), (0,)), ((), ()))
        act = (silu[0:tm] * up[0:tm]).astype(BF16)
        acc_d[...] += lax.dot_general(act, dx, tn, preferred_element_type=F32)
        hv = h_ref[...]
        acc_g[...] += lax.dot_general(du_g, hv, tn, preferred_element_type=F32)
        acc_u[...] += lax.dot_general(du_u, hv, tn, preferred_element_type=F32)

        @pl.when(i == ni - 1)
        def _():
            gd_ref[...] = acc_d[...].astype(BF16)
            gup_ref[0] = acc_g[...].astype(BF16)
            gup_ref[1] = acc_u[...].astype(BF16)
            dcw_ref[0] = acc_cg[0:FFN_CONV, :]
            dcw_ref[1] = acc_cu[0:FFN_CONV, :]

    last16 = S // ROWS16 - 1
    rows = pl.BlockSpec((tm, D), lambda j, i: (i, 0))
    rows_next = pl.BlockSpec((ROWS16, D), lambda j, i: (jnp.minimum((i + 1) * per, last16), 0))
    u_cur = pl.BlockSpec((None, tm, FF_SLAB), lambda j, i: (j, i, 0))
    u_prev = pl.BlockSpec((None, ROWS16, FF_SLAB), lambda j, i: (j, jnp.maximum(i * per - 1, 0), 0))
    u_next = pl.BlockSpec((None, ROWS16, FF_SLAB), lambda j, i: (j, jnp.minimum((i + 1) * per, last16), 0))
    cslab = lambda off: pl.BlockSpec((None, FFN_CONV, FF_SLAB), lambda j, i: (j + off, 0, 0))
    return pl.pallas_call(
        body, name=name, grid=(FF_PAIRS, ni),
        in_specs=[rows, rows_next, rows, u_cur, u_prev, u_next, u_cur, u_prev, u_next, cslab(0), cslab(FF_PAIRS),
                  pl.BlockSpec((FF_SLAB, D), lambda j, i: (j, 0))],
        out_specs=[pl.BlockSpec((None, 2, tm, FF_SLAB), lambda j, i: (j, 0, i, 0)), pl.BlockSpec((FF_SLAB, D), lambda j, i: (j, 0)),
                   pl.BlockSpec((None, 2, FF_SLAB, D), lambda j, i: (j, 0, 0, 0)),
                   pl.BlockSpec((None, 2, FFN_CONV, FF_SLAB), lambda j, i: (j, 0, 0, 0))],
        out_shape=[_sds((FF_PAIRS, 2, S, FF_SLAB), BF16), _sds((D_FF, D), BF16), _sds((FF_PAIRS, 2, FF_SLAB, D), BF16),
                   _sds((FF_PAIRS, 2, FFN_CONV, FF_SLAB))],
        scratch_shapes=[pltpu.VMEM((FF_SLAB, D), F32), pltpu.VMEM((FF_SLAB, D), F32), pltpu.VMEM((FF_SLAB, D), F32),
                        pltpu.VMEM((8, FF_SLAB), F32), pltpu.VMEM((8, FF_SLAB), F32)],
        compiler_params=_params(("parallel", "arbitrary")),
    )(dx2, dx2, h2, ug, ug, ug, uu, uu, uu, conv_w, conv_w, w_down)


def _pair_slot(p):
    return 2 * (p & (FF_PAIRS - 1)) + (p >> 2)


def _mm_slabs(a, w, *, name, res=None, norm_bwd=None, after=(), tm=1024, tn=1024):
    nk, S, _ = a.shape
    D = w.shape[2]
    has_res = res is not None
    has_norm = norm_bwd is not None
    assert not has_norm or tn == D

    def body(*refs):
        a_ref, w_ref = refs[:2]
        r_ref = refs[2] if has_res else None
        if has_norm:
            x_ref, nw_ref, skip_ref = refs[2 + has_res:5 + has_res]
            o_ref, dw_ref, acc_ref = refs[-3:]
        else:
            o_ref, acc_ref = refs[-2:]
        i, k = pl.program_id(0), pl.program_id(2)
        part = jnp.dot(a_ref[...], w_ref[...], preferred_element_type=F32)

        @pl.when(k == 0)
        def _():
            acc_ref[...] = part

        @pl.when(k > 0)
        def _():
            acc_ref[...] += part

        @pl.when(k == nk - 1)
        def _():
            r = acc_ref[...] + r_ref[...] if has_res else acc_ref[...]
            if has_norm:
                dx, dw = _rms_bwd_rows(r, x_ref[...], nw_ref[...])
                o_ref[...] = skip_ref[...] + dx

                @pl.when(i == 0)
                def _():
                    dw_ref[...] = dw

                @pl.when(i > 0)
                def _():
                    dw_ref[...] += dw
            else:
                o_ref[...] = r

    o_spec = pl.BlockSpec((tm, tn), lambda i, j, k: (i, j))
    one = pl.BlockSpec((1, tn), lambda i, j, k: (0, 0))
    return pl.pallas_call(
        body, name=name, grid=(S // tm, D // tn, nk),
        in_specs=[pl.BlockSpec((None, tm, FF_SLAB), lambda i, j, k: (k, i, 0)),
                  pl.BlockSpec((None, FF_SLAB, tn), lambda i, j, k: (FF_PAIRS * (k & 1) + (k >> 1), 0, j))] + [o_spec] * has_res
        + ([o_spec, one, o_spec] if has_norm else []) + [ANY] * len(after),
        out_specs=[o_spec, one] if has_norm else o_spec, out_shape=[_sds((S, D)), _sds((1, D))] if has_norm else _sds((S, D)),
        scratch_shapes=[pltpu.VMEM((tm, tn), F32)],
        compiler_params=_params(("arbitrary" if has_norm else "parallel", "parallel", "arbitrary")),
    )(*((a, w) + ((res,) if has_res else ()) + (tuple(norm_bwd) if has_norm else ()) + tuple(after)))


def _local_step(x, tgt, norm1_w, w_land, conv_a, a_log, dt_bias, gnw, norm2_w, final_w, late_weights, emit, start_after=()):
    wgrad = functools.partial(_mm, ta=True, out_dtype=BF16)
    h1, proj_a, proj_z, proj_b = _in_proj(x, norm1_w, w_land, after=start_after, name="in_proj")
    qn, kn, v, gcb, bb = _gdn_prep_fwd(proj_a, conv_a, a_log, dt_bias, name="gdn_prep_fwd")
    uv, wk, at, tmat, wkb, qdb, keb = _gdn_chunk_fwd(qn, kn, v, gcb, bb, name="gdn_chunk_fwd")
    o, u, sp, oab = _gdn_scan_fwd(uv, at, wkb, qdb, keb, gcb, proj_z, gnw, name="gdn_scan_fwd")
    oab, lse = _attn_fwd(proj_b, oab, name="attn_fwd")
    w_out, w_up, conv_f, w_down = late_weights(oab)
    x1, h2 = _out_proj_norm(oab, w_out, x, norm2_w, name="out_proj")
    dx2, dx2_b, d_final, loss, ug, uu = _ffn_fwd(h2, x1, w_up, conv_f, w_down, final_w, tgt, name="ffn_fwd")
    du, g_down, g_up, dcw = _ffn_bwd(dx2_b, h2, ug, uu, conv_f, w_down, name="ffn_bwd")
    token = emit("ffn", w_down=g_down, w_up=g_up.reshape(N_DEV, FF_SLAB, -1), conv_f=dcw.reshape(N_DEV, FFN_CONV, -1))
    dx1, d_norm2 = _mm_slabs(du.reshape(N_DEV, -1, FF_SLAB), w_up, norm_bwd=(x1, norm2_w, dx2), after=token, name="ffn_up_dx")
    d_oab = _mm(dx1, w_out, tb=True, name="out_proj_dx", tn=D_MODEL, tk=1024)
    token = emit("out", w_out=wgrad(oab, dx1, name="out_proj_dw", tm=D_MODEL, tn=D_MODEL))
    dz, d_gnw, du, dwk, dat, dqd, dke, dgl = _gdn_scan_bwd(d_oab, o, proj_z, gnw, sp, u, at, wkb, qdb, keb, gcb, after=token, name="gdn_scan_bwd")
    dqn, dkn, dv, dg, dbeta = _gdn_chunk_bwd(qn, kn, gcb, bb, tmat, uv, wk, du, dwk, dat, dqd, dke, dgl, name="gdn_chunk_bwd")
    dc, dba, d_small = _gdn_prep_bwd(dqn, dkn, dv, dg, dbeta, proj_a, conv_a, a_log, dt_bias, name="gdn_prep_bwd")
    d_pa, d_conv_a = _gdn_conv_bwd(dc, dba, proj_a, conv_a, name="gdn_conv_bwd")
    d_pb = _attn_bwd(proj_b, oab, d_oab, lse, name="attn_bwd")
    g_a = wgrad(d_pa, h1, name="proj_a_dw", tm=A_COLS, tn=D_MODEL)
    g_z = wgrad(dz, h1, name="proj_z_dw", tn=D_MODEL)
    g_b = wgrad(d_pb, h1, name="proj_b_dw", tm=768, tn=D_MODEL)
    token = emit("in", w_a=g_a, w_z=g_z, w_b=g_b, conv_a=d_conv_a)
    grad_x, d_norm1 = _in_proj_dx((d_pa, dz, d_pb), w_land, x, norm1_w, dx1, after=token, name="in_proj_dx")
    small = dict(norm1=d_norm1, small=d_small, gnw=d_gnw, norm2=d_norm2, final=d_final)
    return loss, grad_x, small


_O1 = 3 * GDN_WIDTH
_O2 = _O1 + GDN_WIDTH
_O3 = _O2 + 2 * GDN_HEADS


_W_IN_ROWS = (A_COLS, GDN_WIDTH, 3 * DIL_WIDTH)
_IN_ROWS = (_O3 + 3 * DIL_WIDTH) // N_DEV
_ROW_TILE = 16
_IN_STEP = _IN_ROWS - _IN_ROWS % _ROW_TILE
_GAP = 8
_LAND_ROWS = 464
assert _O3 % _ROW_TILE == _ROW_TILE - _GAP and N_DEV - 1 + _GAP + _IN_ROWS <= _LAND_ROWS and _LAND_ROWS % _ROW_TILE == 0


def _padded_row(r):
    return r + (_GAP if r >= _O3 else 0)


def _shifted_slab(w_rows, j, *, name):
    q = w_rows.shape[0] // _IN_ROWS

    def body(j_ref, w_ref, o_ref, pad_ref):
        pad_ref[...] = jnp.zeros_like(pad_ref)
        for dev in range(N_DEV):
            @pl.when(j_ref[0] == dev)
            def _(dev=dev):
                p = lax.broadcasted_iota(jnp.int32, (_LAND_ROWS, 1), 0) + _IN_STEP * dev
                for s in range(q):
                    pad_ref[0:_IN_ROWS, :] = w_ref[pl.ds(s, _IN_ROWS, stride=q), :]
                    rows = pad_ref[...]
                    a = pltpu.roll(rows, dev, 0) if dev else rows
                    b = pltpu.roll(rows, dev + _GAP, 0)
                    o_ref[:, 128 * s:128 * (s + 1)] = jnp.where(p < _O3, a, jnp.where(p >= _O3 + _GAP, b, 0.0)).astype(BF16)

    vm = pl.BlockSpec(memory_space=pltpu.VMEM)
    return pl.pallas_call(
        body, name=name, in_specs=[pl.BlockSpec(memory_space=pltpu.SMEM), vm], out_specs=vm,
        out_shape=_sds((_LAND_ROWS, 128 * q), BF16), scratch_shapes=[pltpu.VMEM((_LAND_ROWS, 128), F32)],
    )(j, w_rows)


def _w_in_plan():
    def dest(p):
        if p < _O1:
            return 0, p
        if p < _O2:
            return 1, p - _O1
        if p < _O2 + _ROW_TILE:
            return 0, _O1
        q = p - _O3 - _GAP
        t, pair = divmod(q // 128, DIL_PAIRS)
        return 2, (3 * pair + t) * 128 + q % 128

    spans = [(_padded_row(_IN_ROWS * j), _padded_row(_IN_ROWS * (j + 1) - 1) + 1) for j in range(N_DEV)]
    runs, seams = [], []
    for p in range(0, spans[-1][1], _ROW_TILE):
        owners = [j for j, (lo, hi) in enumerate(spans) if lo < p + _ROW_TILE and hi > p]
        w, r = dest(p)
        if len(owners) == 2:
            seams.append((w, r, owners[0], p - _IN_STEP * owners[0], owners[1], p - _IN_STEP * owners[1]))
            continue
        (j,) = owners
        last = runs[-1] if runs else None
        if last and last[0] == j and last[2] == w and last[3] + last[4] == r and last[1] + last[4] == p - _IN_STEP * j:
            runs[-1] = last[:4] + (last[4] + _ROW_TILE,)
        else:
            runs.append((j, p - _IN_STEP * j, w, r, _ROW_TILE))
    return runs, seams


def _w_in_scratch(d):
    return [pltpu.VMEM((n, d), BF16) for n in _W_IN_ROWS] + [pltpu.VMEM((N_DEV - 1, _ROW_TILE, d), BF16),
                                                              pltpu.SemaphoreType.DMA(())]


def _fetch_w_in(land_ref, wa_ref, wz_ref, wb_ref, seam_ref, sem):
    w_refs = (wa_ref, wz_ref, wb_ref)
    runs, seams = _w_in_plan()
    copies = [pltpu.make_async_copy(land_ref.at[j, pl.ds(s, n)], w_refs[w].at[pl.ds(r, n)], sem) for j, s, w, r, n in runs]
    for k, (w, r, j0, s0, j1, s1) in enumerate(seams):
        copies.append(pltpu.make_async_copy(land_ref.at[j0, pl.ds(s0, _ROW_TILE)], w_refs[w].at[pl.ds(r, _ROW_TILE)], sem))
        copies.append(pltpu.make_async_copy(land_ref.at[j1, pl.ds(s1, _ROW_TILE)], seam_ref.at[k], sem))
    for cp in copies:
        cp.start()
    tail = _O1 + _ROW_TILE
    wa_ref[tail:, :] = jnp.zeros((A_COLS - tail, wa_ref.shape[1]), BF16)
    for cp in copies:
        cp.wait()
    for k, (w, r, *_) in enumerate(seams):
        both = w_refs[w][r:r + _ROW_TILE, :].astype(F32) + seam_ref[k].astype(F32)
        w_refs[w][r:r + _ROW_TILE, :] = both.astype(BF16)


MESH = pl.DeviceIdType.MESH
ANY = pl.BlockSpec(memory_space=pl.ANY)


def _position():
    return lax.axis_index("x"), lax.axis_index("y"), lax.axis_index("c")


def _slot(p):
    return 4 * p[0] + 2 * p[1] + p[2]


def _all_gather(blocks, *, name):
    n = len(blocks)

    def body(*refs):
        ins, outs = refs[:n], refs[n:2 * n]
        send_sems, recv_sems, local_sems = refs[2 * n:]
        x, y, c = _position()
        me, sibling = (x, y, c), (x, y, 1 - c)
        chips = [(1 - x, y), (x, 1 - y), (1 - x, 1 - y)]

        def copy(a, k, block, to, src=None):
            dst = outs[a].at[_slot(block)]
            return pltpu.make_async_remote_copy(
                src_ref=dst if src is None else src, dst_ref=dst, send_sem=send_sems.at[a, k], recv_sem=recv_sems.at[a, k],
                device_id=to, device_id_type=MESH)

        mine = [pltpu.make_async_copy(ins[a], outs[a].at[_slot(me)], local_sems.at[a]) for a in range(n)]
        for cp in mine:
            cp.start()
        first = []
        for a in range(n):
            first.append(copy(a, 0, me, sibling, src=ins[a]))
            first += [copy(a, 1 + j, me, (*chip, c), src=ins[a]) for j, chip in enumerate(chips)]
        for cp in first:
            cp.start()
        passed = []
        for j, chip in enumerate(chips):
            for a in range(n):
                copy(a, 1 + j, (*chip, c), me).wait_recv()
                fwd = copy(a, 4 + j, (*chip, c), sibling)
                fwd.start()
                passed.append(fwd)
        for a in range(n):
            copy(a, 0, sibling, me).wait_recv()
            for j, chip in enumerate(chips):
                copy(a, 4 + j, (*chip, 1 - c), me).wait_recv()
        for cp in first + passed:
            cp.wait_send()
        for cp in mine:
            cp.wait()

    return pl.pallas_call(
        body, name=name, in_specs=[ANY] * n, out_specs=[ANY] * n,
        out_shape=[_sds((N_DEV,) + b.shape, b.dtype) for b in blocks],
        scratch_shapes=[pltpu.SemaphoreType.DMA((n, 7)), pltpu.SemaphoreType.DMA((n, 7)), pltpu.SemaphoreType.DMA((n,))],
    )(*blocks)


def _gather_direct(block, *, name, after=()):
    def body(in_ref, *rest):
        out_ref, send_sems, recv_sems, local_sem = rest[len(after):]
        x, y, c = _position()
        me = _slot((x, y, c))
        mine = pltpu.make_async_copy(in_ref, out_ref.at[me], local_sem)
        mine.start()
        copies = [pltpu.make_async_remote_copy(
            src_ref=in_ref, dst_ref=out_ref.at[me], send_sem=send_sems.at[k - 1], recv_sem=recv_sems.at[k - 1],
            device_id=_peer_of(k, x, y, c), device_id_type=MESH) for k in range(1, N_DEV)]
        for cp in copies:
            cp.start()
        for cp in copies:
            cp.wait()
        mine.wait()

    return pl.pallas_call(
        body, name=name, in_specs=[pl.BlockSpec(memory_space=pltpu.VMEM)] + [ANY] * len(after),
        out_specs=pl.BlockSpec(memory_space=pltpu.VMEM),
        out_shape=_sds((N_DEV,) + block.shape, block.dtype),
        scratch_shapes=[pltpu.SemaphoreType.DMA((N_DEV - 1,)), pltpu.SemaphoreType.DMA((N_DEV - 1,)), pltpu.SemaphoreType.DMA],
    )(block, *after)


HBM = pl.BlockSpec(memory_space=pltpu.HBM)
SEM = pl.BlockSpec(memory_space=pltpu.SEMAPHORE)
EFFECT = pltpu.SideEffectType.DATAFLOW_SIDE_EFFECTING


def _peer_of(k, x, y, c):
    return (1 - x if k & 4 else x, 1 - y if k & 2 else y, 1 - c if k & 1 else c)


def _flight(a, k):
    return a * (N_DEV - 1) + k - 1


def _exchange_start(arrays, *, name, broadcast=False, paired=()):
    n = len(arrays)

    def body(*refs):
        ins, lands = refs[:n], refs[n:2 * n]
        send_sems, recv_sems = refs[2 * n:2 * n + 2]
        token = refs[-1]
        x, y, c = _position()
        me = _slot((x, y, c))
        for k in range(1, N_DEV):
            peer = _peer_of(k, x, y, c)
            for a in range(n):
                at = _pair_slot(_slot(peer)) if a in paired else _slot(peer)
                pltpu.make_async_remote_copy(
                    src_ref=ins[a] if broadcast else ins[a].at[at], dst_ref=lands[a].at[me],
                    send_sem=send_sems.at[_flight(a, k)], recv_sem=recv_sems.at[_flight(a, k)],
                    device_id=peer, device_id_type=MESH).start()
        token[...] = jnp.zeros_like(token)

    land_shapes = [((N_DEV,) + s.shape) if broadcast else s.shape for s in arrays]
    lands = [pltpu.with_memory_space_constraint(lax.empty(shp, s.dtype), pltpu.HBM) for shp, s in zip(land_shapes, arrays)]
    srcs = [pltpu.with_memory_space_constraint(s, pltpu.HBM) for s in arrays]
    outs = pl.pallas_call(
        body, name=name, in_specs=[HBM] * (2 * n),
        out_specs=[SEM, SEM] + [HBM] * (2 * n) + [pl.BlockSpec(memory_space=pltpu.VMEM)],
        out_shape=[pltpu.SemaphoreType.DMA((n * (N_DEV - 1),)), pltpu.SemaphoreType.DMA((n * (N_DEV - 1),))]
        + [pltpu.HBM(s.shape, s.dtype) for s in arrays] + [pltpu.HBM(shp, s.dtype) for shp, s in zip(land_shapes, arrays)]
        + [_sds((8, 128))],
        input_output_aliases={i: 2 + i for i in range(2 * n)},
        compiler_params=pltpu.CompilerParams(has_side_effects=EFFECT),
    )(*srcs, *lands)
    return outs[0], outs[1], outs[2:2 + n], outs[2 + n:2 + 2 * n], outs[-1]


def _exchange_wait(send_sems, recv_sems, srcs, lands, after, *, name, broadcast=False):
    n = len(srcs)

    def body(*refs):
        ins, lnd = refs[:n], refs[n:2 * n]
        send_ref, recv_ref = refs[2 * n:2 * n + 2]
        x, y, c = _position()
        for k in range(1, N_DEV):
            for a in range(n):
                cp = pltpu.make_async_remote_copy(
                    src_ref=ins[a] if broadcast else ins[a].at[0], dst_ref=lnd[a].at[0], send_sem=send_ref.at[_flight(a, k)],
                    recv_sem=recv_ref.at[_flight(a, k)], device_id=_peer_of(k, x, y, c), device_id_type=MESH)
                cp.wait_send()
                cp.wait_recv()

    outs = pl.pallas_call(
        body, name=name, in_specs=[HBM] * (2 * n) + [SEM, SEM, ANY], out_specs=[HBM] * (2 * n),
        out_shape=[pltpu.HBM(s.shape, s.dtype) for s in srcs] + [pltpu.HBM(s.shape, s.dtype) for s in lands],
        input_output_aliases={i: i for i in range(2 * n)},
        compiler_params=pltpu.CompilerParams(has_side_effects=EFFECT),
    )(*srcs, *lands, send_sems, recv_sems, after)
    return outs[:n], outs[n:]


_G_LAND_ROWS = 528


def _g_in_pieces(dev):
    runs, seams = _w_in_plan()
    pieces = [(w, r, n, s) for j, s, w, r, n in runs if j == dev]
    pieces += [(w, r, _ROW_TILE, s0) for w, r, j0, s0, j1, s1 in seams if j0 == dev]
    pieces += [(w, r, _ROW_TILE, s1) for w, r, j0, s0, j1, s1 in seams if j1 == dev]
    return pieces


def _coords(dev):
    return tuple(jnp.int32(v) for v in (dev >> 2, (dev >> 1) & 1, dev & 1))


def _exchange_start_in(g_ws, conv_slabs, *, name):
    srcs = list(g_ws) + [conv_slabs]
    n = len(srcs)

    def body(*refs):
        g_refs, cv_ref, land, land_cv = refs[:n - 1], refs[n - 1], refs[n], refs[n + 1]
        send_sems, recv_sems = refs[n + 2:n + 4]
        token, own_sem = refs[-2:]
        me = _slot(_position())
        for dev in range(N_DEV):
            @pl.when(me != dev)
            def _(dev=dev):
                k = me ^ dev
                for w, r, rows, s in _g_in_pieces(dev):
                    pltpu.make_async_remote_copy(
                        src_ref=g_refs[w].at[pl.ds(r, rows)], dst_ref=land.at[me, pl.ds(s, rows)],
                        send_sem=send_sems.at[_flight(0, k)], recv_sem=recv_sems.at[_flight(0, k)],
                        device_id=_coords(dev), device_id_type=MESH).start()
                pltpu.make_async_remote_copy(
                    src_ref=cv_ref.at[dev], dst_ref=land_cv.at[me], send_sem=send_sems.at[_flight(1, k)],
                    recv_sem=recv_sems.at[_flight(1, k)], device_id=_coords(dev), device_id_type=MESH).start()

            @pl.when(me == dev)
            def _(dev=dev):
                own = [pltpu.make_async_copy(g_refs[w].at[pl.ds(r, rows)], land.at[dev, pl.ds(s, rows)], own_sem)
                       for w, r, rows, s in _g_in_pieces(dev)]
                for cp in own:
                    cp.start()
                for cp in own:
                    cp.wait()
        token[...] = jnp.zeros_like(token)

    lands = [lax.empty((N_DEV, _G_LAND_ROWS, g_ws[0].shape[1]), g_ws[0].dtype), lax.empty(conv_slabs.shape, conv_slabs.dtype)]
    ops = [pltpu.with_memory_space_constraint(a, pltpu.HBM) for a in srcs + lands]
    outs = pl.pallas_call(
        body, name=name, in_specs=[HBM] * len(ops),
        out_specs=[SEM, SEM] + [HBM] * len(ops) + [pl.BlockSpec(memory_space=pltpu.VMEM)],
        out_shape=[pltpu.SemaphoreType.DMA((2 * (N_DEV - 1),)), pltpu.SemaphoreType.DMA((2 * (N_DEV - 1),))]
        + [pltpu.HBM(a.shape, a.dtype) for a in ops] + [_sds((8, 128))],
        input_output_aliases={i: 2 + i for i in range(len(ops))},
        scratch_shapes=[pltpu.SemaphoreType.DMA(())],
        compiler_params=pltpu.CompilerParams(has_side_effects=EFFECT),
    )(*ops)
    return outs[0], outs[1], outs[2:2 + n], outs[2 + n:4 + n], outs[-1]


def _exchange_wait_in(send_sems, recv_sems, srcs, lands, after, *, name):
    n = len(srcs)

    def body(*refs):
        g_refs, cv_ref, land, land_cv = refs[:n - 1], refs[n - 1], refs[n], refs[n + 1]
        send_ref, recv_ref = refs[n + 2:n + 4]
        me = _slot(_position())

        def copies(dev, k):
            cps = [pltpu.make_async_remote_copy(
                src_ref=g_refs[w].at[pl.ds(r, rows)], dst_ref=land.at[0, pl.ds(s, rows)], send_sem=send_ref.at[_flight(0, k)],
                recv_sem=recv_ref.at[_flight(0, k)], device_id=_coords(dev), device_id_type=MESH)
                for w, r, rows, s in _g_in_pieces(dev)]
            return cps + [pltpu.make_async_remote_copy(
                src_ref=cv_ref.at[0], dst_ref=land_cv.at[0], send_sem=send_ref.at[_flight(1, k)],
                recv_sem=recv_ref.at[_flight(1, k)], device_id=_coords(dev), device_id_type=MESH)]

        for dev in range(N_DEV):
            @pl.when(me != dev)
            def _(dev=dev):
                for cp in copies(dev, me ^ dev):
                    cp.wait_send()

            @pl.when(me == dev)
            def _(dev=dev):
                for k in range(1, N_DEV):
                    for cp in copies(dev, k):
                        cp.wait_recv()

    ops = list(srcs) + list(lands)
    outs = pl.pallas_call(
        body, name=name, in_specs=[HBM] * len(ops) + [SEM, SEM, ANY], out_specs=[HBM] * len(ops),
        out_shape=[pltpu.HBM(a.shape, a.dtype) for a in ops],
        input_output_aliases={i: i for i in range(len(ops))},
        compiler_params=pltpu.CompilerParams(has_side_effects=EFFECT),
    )(*ops, send_sems, recv_sems, after)
    return outs[:n], outs[n:]


def _with_own(landed, srcs, me):
    return [lax.dynamic_update_index_in_dim(l, o, me, 0) for l, o in zip(landed, srcs)]


def _adam_update(g, w, m, v):
    c1 = 1.0 - ADAM_B1 ** ADAM_STEP
    c2 = 1.0 - ADAM_B2 ** ADAM_STEP
    nm = ADAM_B1 * m + (1.0 - ADAM_B1) * g
    nv = ADAM_B2 * v + (1.0 - ADAM_B2) * (g * g)
    return -ADAM_LR * ((nm / c1) / (jnp.sqrt(nv / c2) + ADAM_EPS) + ADAM_WD * w), nm, nv


def _adamw(landed, sent, me, w, m, v, *, name, tr=None, tc=None):
    R, C = w.shape
    tr = R if tr is None else tr
    tc = C if tc is None else tc
    assert R % tr == 0 and C % tc == 0

    def body(me_ref, own_ref, p_ref, w_ref, m_ref, v_ref, g_ref, d_ref, nm_ref, nv_ref, token_ref):
        token_ref[...] = jnp.zeros_like(token_ref)
        g = own_ref[...].astype(F32)
        for s in range(N_DEV):
            g = g + jnp.where(me_ref[1] == s, 0.0, p_ref[s].astype(F32))
        delta, nm, nv = _adam_update(g, w_ref[...], m_ref[...], v_ref[...])
        g_ref[...] = g
        nm_ref[...] = nm
        nv_ref[...] = nv
        d_ref[...] = delta

    blk = pl.BlockSpec((tr, tc), lambda i, j, me_ref: (i, j))
    return pl.pallas_call(
        body, name=name,
        grid_spec=pltpu.PrefetchScalarGridSpec(
            num_scalar_prefetch=1, grid=(R // tr, C // tc),
            in_specs=[pl.BlockSpec((None, tr, tc), lambda i, j, me_ref: (me_ref[0], i, j)),
                      pl.BlockSpec((N_DEV, tr, tc), lambda i, j, me_ref: (0, i, j)), blk, blk, blk],
            out_specs=[blk] * 4 + [pl.BlockSpec((8, 128), lambda i, j, me_ref: (0, 0))]),
        out_shape=[_sds((R, C))] * 4 + [_sds((8, 128))],
        compiler_params=_params(("arbitrary", "arbitrary")),
    )(me, sent, landed, w, m, v)


def _adamw_rowwise(landed, me, w, m, v, *, name, tr=128):
    C = landed.shape[2]
    R, q, extra = _IN_ROWS, C // 128, _ROW_TILE
    assert tr % extra == 0 and (pl.cdiv(R, tr) * tr + extra) <= landed.shape[1] and N_DEV - 1 + _GAP < extra

    def body(me_ref, a_ref, b_ref, w_ref, m_ref, v_ref, g_ref, d_ref, nm_ref, nv_ref, g_scr):
        i = pl.program_id(0)
        total = lambda ref: functools.reduce(lambda x, y: x + y, [ref[s].astype(F32) for s in range(N_DEV)])
        slab = jnp.concatenate([total(a_ref), total(b_ref)], axis=0)
        for dev in range(N_DEV):
            @pl.when(me_ref[0] == dev)
            def _(dev=dev):
                lo, hi = slab[dev:dev + tr], slab[dev + _GAP:dev + _GAP + tr]
                if _IN_ROWS * (dev + 1) <= _O3:
                    g_scr[...] = lo
                elif _IN_ROWS * dev >= _O3:
                    g_scr[...] = hi
                else:
                    r = _IN_ROWS * dev + tr * i + lax.broadcasted_iota(jnp.int32, (tr, 1), 0)
                    g_scr[...] = jnp.where(r < _O3, lo, hi)
        g = g_scr[...]
        for s in range(q):
            rows = pl.ds(s, tr, stride=q)
            gs = g[:, 128 * s:128 * (s + 1)]
            delta, nm, nv = _adam_update(gs, w_ref[rows, :], m_ref[rows, :], v_ref[rows, :])
            g_ref[rows, :] = gs
            nm_ref[rows, :] = nm
            nv_ref[rows, :] = nv
            d_ref[rows, :] = delta

    blk = pl.BlockSpec((tr * q, 128), lambda i, me_ref: (i, 0))
    return pl.pallas_call(
        body, name=name,
        grid_spec=pltpu.PrefetchScalarGridSpec(
            num_scalar_prefetch=1, grid=(pl.cdiv(R, tr),),
            in_specs=[pl.BlockSpec((N_DEV, tr, C), lambda i, me_ref: (0, i, 0)),
                      pl.BlockSpec((N_DEV, extra, C), lambda i, me_ref: (0, (tr // extra) * (i + 1), 0)), blk, blk, blk],
            out_specs=[blk] * 4, scratch_shapes=[pltpu.VMEM((tr, C), F32)]),
        out_shape=[_sds((R * q, 128))] * 4,
        compiler_params=_params(("arbitrary",)),
    )(me, landed, landed, w, m, v)


_SMALL_ROWS = 8
_SMALL_SLOTS = ((0, 0, D_MODEL), (1, 0, D_MODEL), (2, 0, D_MODEL), (3, 0, GDN_DIM), (3, GDN_DIM, GDN_HEADS),
                (3, GDN_DIM + GDN_HEADS, GDN_HEADS))
_LOSS_LANE = 2 * GDN_DIM


def _pack_small(norm1, norm2, final, gnw, a_log, dt_bias, loss):
    row3 = jnp.concatenate([gnw, a_log, dt_bias, jnp.zeros((1, 128 - 2 * GDN_HEADS), F32), loss,
                            jnp.zeros((1, D_MODEL - 3 * 128), F32)], axis=1)
    return jnp.concatenate([norm1, norm2, final, row3, jnp.zeros((_SMALL_ROWS - 4, D_MODEL), F32)], axis=0)


def _adamw_small(packs, ws, ms, vs, *, name):
    n = len(ws)

    def body(p_ref, *refs):
        w_refs, m_refs, v_refs = refs[:n], refs[n:2 * n], refs[2 * n:3 * n]
        outs = refs[3 * n:]
        g_all = p_ref[0]
        for s in range(1, N_DEV):
            g_all = g_all + p_ref[s]
        for i, (row, lane, width) in enumerate(_SMALL_SLOTS):
            g = g_all[row:row + 1, lane:lane + width]
            delta, nm, nv = _adam_update(g, w_refs[i][...], m_refs[i][...], v_refs[i][...])
            for o_ref, val in zip(outs[4 * i:4 * i + 4], (g, delta, nm, nv)):
                o_ref[...] = val
        outs[-1][...] = g_all[3:4, _LOSS_LANE:_LOSS_LANE + 128]

    vm = pl.BlockSpec(memory_space=pltpu.VMEM)
    outs = pl.pallas_call(
        body, name=name, in_specs=[vm] * (1 + 3 * n), out_specs=[vm] * (4 * n + 1),
        out_shape=[_sds(w.shape) for w in ws for _ in range(4)] + [_sds((1, 128))],
    )(packs, *ws, *ms, *vs)
    return [outs[4 * i:4 * i + 4] for i in range(n)], outs[-1]


def _slabs_by_cols(g):
    r = g.shape[0]
    return g.reshape(r, N_DEV, -1).transpose(1, 0, 2)


def _cols_from_slabs(s):
    return s.transpose(1, 0, 2).reshape(s.shape[1], -1)


def kernel(x, norm1_w, w_in, conv_qkv_w, a_log, dt_bias, gdn_norm_w, w_out, norm2_w, w_up, ffn_conv_w, w_down, final_norm_w, loss_target, m_norm1_w, m_w_in, m_conv_qkv_w, m_a_log, m_dt_bias, m_gdn_norm_w, m_w_out, m_norm2_w, m_w_up, m_ffn_conv_w, m_w_down, m_final_norm_w, v_norm1_w, v_w_in, v_conv_qkv_w, v_a_log, v_dt_bias, v_gdn_norm_w, v_w_out, v_norm2_w, v_w_up, v_ffn_conv_w, v_w_down, v_final_norm_w):
    bf = lambda a: a.astype(BF16)
    me = _slot(_position())
    me1 = jnp.reshape(me, (1,)).astype(jnp.int32)
    t_in = lambda a: a[0].T
    rows = lambda a: a.reshape(D_MODEL // 128, 128, -1).transpose(2, 0, 1).reshape(-1, 128)
    gw_in, g_conv_a = _all_gather([_shifted_slab(rows(w_in), me1, name="shift_w_in"), conv_qkv_w[0]], name="gather_w_in")
    late_src, _ = lax.optimization_barrier(([bf(w_out[0]), bf(t_in(w_up)), bf(w_down[0]), ffn_conv_w[0]], gw_in))
    l_send, l_recv, l_srcs, l_lands, l_token = _exchange_start(late_src, name="weights_start", broadcast=True)

    def late_weights(after):
        srcs, landed = _exchange_wait(l_send, l_recv, l_srcs, l_lands, after, name="weights_wait", broadcast=True)
        gw_out, gw_up, gw_down, g_conv_f = _with_own(landed, srcs, me)
        return gw_out.reshape(D_MODEL, D_MODEL), gw_up, g_conv_f, gw_down.reshape(D_FF, D_MODEL)

    flights = {}

    def emit(group, **grads):
        paired = ()
        if group == "in":
            *flight, token = _exchange_start_in([grads["w_a"], grads["w_z"], grads["w_b"]], _slabs_by_cols(grads["conv_a"]),
                                                name="grads_start_in")
            flights[group] = flight
            return (token,)
        if group == "ffn":
            slabs = dict(w_down=grads["w_down"].reshape(N_DEV, -1, D_MODEL), w_up=grads["w_up"], conv_f=grads["conv_f"])
            paired = (1, 2)
        else:
            slabs = {k: v.reshape(N_DEV, -1, D_MODEL) for k, v in grads.items()}
        names = list(slabs)
        *flight, token = _exchange_start([slabs[k] for k in names], paired=paired, name="grads_start_" + group)
        flights[group] = (names, flight)
        return (token,)

    loss, grad_x, g = _local_step(
        x[0], loss_target[0], norm1_w, gw_in, _cols_from_slabs(g_conv_a), a_log, dt_bias,
        gdn_norm_w, norm2_w, final_norm_w[None], late_weights, emit, start_after=(l_token,))
    got = {}

    def collect(group, after):
        names, (send_sems, recv_sems, srcs, lands) = flights[group]
        srcs, landed = _exchange_wait(send_sems, recv_sems, srcs, lands, after, name="grads_wait_" + group)
        got.update(zip(names, zip(landed, srcs)))

    def update(key, w, m, v, paired=False, **tiles):
        where = jnp.concatenate([_pair_slot(me1) if paired else me1, me1])
        return _adamw(*got[key], where, w, m, v, name="adamw_" + key, **tiles)

    collect("ffn", grad_x)
    collect("out", grad_x)
    *o_out, t1 = update("w_out", w_out[0], m_w_out[0], v_w_out[0])
    *o_up, t2 = update("w_up", t_in(w_up), t_in(m_w_up), t_in(v_w_up), paired=True, tr=176)
    o_up = [o.T for o in o_up]
    *o_down, t3 = update("w_down", w_down[0], m_w_down[0], v_w_down[0], tr=176)
    *o_cf, t4 = update("conv_f", ffn_conv_w[0], m_ffn_conv_w[0], v_ffn_conv_w[0], paired=True)
    pack = _pack_small(g["norm1"], g["norm2"], g["final"], g["gnw"], g["small"][:, 0:GDN_HEADS],
                       g["small"][:, GDN_HEADS:2 * GDN_HEADS], loss)
    small_all = _gather_direct(pack, after=(t1, t2, t3, t4), name="gather_small")
    send_sems, recv_sems, srcs, lands = flights["in"]
    srcs, (g_land, conv_land) = _exchange_wait_in(send_sems, recv_sems, srcs, lands, small_all, name="grads_wait_in")
    got["conv_a"] = (conv_land, srcs[-1])
    o_in = [o.reshape(-1, D_MODEL // 128, 128).transpose(1, 2, 0).reshape(D_MODEL, -1) for o in _adamw_rowwise(
        g_land, me1, rows(w_in), rows(m_w_in), rows(v_w_in), name="adamw_w_in")]
    o_ca = update("conv_a", conv_qkv_w[0], m_conv_qkv_w[0], v_conv_qkv_w[0])
    (o_n1, o_n2, o_fin, o_gn, o_al, o_dt), total = _adamw_small(
        small_all, (norm1_w, norm2_w, final_norm_w[None], gdn_norm_w, a_log, dt_bias),
        (m_norm1_w, m_norm2_w, m_final_norm_w[None], m_gdn_norm_w, m_a_log, m_dt_bias),
        (v_norm1_w, v_norm2_w, v_final_norm_w[None], v_gdn_norm_w, v_a_log, v_dt_bias), name="adamw_small")
    outs = [total[0, 0], grad_x[None]]
    for k in range(4):
        outs += [o_n1[k], o_in[k][None], o_ca[k][None], o_al[k], o_dt[k], o_gn[k], o_out[k][None], o_n2[k], o_up[k][None],
                 o_cf[k][None], o_down[k][None], o_fin[k][0]]
    return tuple(outs)
```

```python
import functools

import jax
import jax.numpy as jnp
from jax import lax
from jax.experimental import pallas as pl
from jax.experimental.pallas import tpu as pltpu

F32 = jnp.float32
BF16 = jnp.bfloat16

N_DEV = 8
D_MODEL = 1024
GDN_HEADS = 4
GDN_DIM = 128
GDN_WIDTH = GDN_HEADS * GDN_DIM
GDN_CONV = 4
CHUNK = 64
CHUNKS_PER_STEP = 4
DIL_HEADS = 8
DIL_DIM = 64
DIL_WIDTH = DIL_HEADS * DIL_DIM
DIL_PAIRS = DIL_HEADS // 2
DILATIONS = (1, 4, 16)
BAND = 128
D_FF = 2816
FFN_CONV = 3
EPS = 1e-6
A_COLS = 3 * GDN_WIDTH + 128
HALO = 8

ADAM_LR = 0.001
ADAM_B1 = 0.9
ADAM_B2 = 0.999
ADAM_EPS = 1e-08
ADAM_WD = 0.01
ADAM_STEP = 10

VMEM_LIMIT_BYTES = 56 * 1024 * 1024
NEG_BIG = -1e30


def _params(sem=None):
    return pltpu.CompilerParams(dimension_semantics=sem, vmem_limit_bytes=VMEM_LIMIT_BYTES)


def _sds(shape, dtype=F32):
    return jax.ShapeDtypeStruct(shape, dtype)


def _bdot(a, b):
    return jnp.dot(a.astype(BF16), b.astype(BF16), preferred_element_type=F32)


def _bdot_nt(a, b):
    return lax.dot_general(a.astype(BF16), b.astype(BF16), (((1,), (1,)), ((), ())), preferred_element_type=F32)


def _bdot_tn(a, b):
    return lax.dot_general(a.astype(BF16), b.astype(BF16), (((0,), (0,)), ((), ())), preferred_element_type=F32)


def _split(a):
    hi = a.astype(BF16)
    lo = (a - hi.astype(F32)).astype(BF16)
    return hi, lo


def _dot3(a, b, dims):
    ah, al = _split(a)
    bh, bl = _split(b)
    d = functools.partial(lax.dot_general, dimension_numbers=(dims, ((), ())), preferred_element_type=F32)
    return d(ah, bh) + (d(al, bh) + d(ah, bl))


def _exact_tri_dot(tri, g):
    g1 = g.astype(BF16)
    r1 = g - g1.astype(F32)
    g2 = r1.astype(BF16)
    g3 = (r1 - g2.astype(F32)).astype(BF16)
    t = tri.astype(BF16)
    d = functools.partial(jnp.dot, preferred_element_type=F32)
    return d(t, g1) + (d(t, g2) + d(t, g3))


def _sigmoid(x):
    return 1.0 / (1.0 + jnp.exp(-x))


def _dsilu(x, sg):
    return sg * (1.0 + x * (1.0 - sg))


def _rms_bwd_rows(dh, x, w):
    r = lax.rsqrt(jnp.mean(x * x, axis=-1, keepdims=True) + EPS)
    xh = x * r
    gw = dh * w
    return r * (gw - xh * jnp.mean(gw * xh, axis=-1, keepdims=True)), jnp.sum(dh * xh, axis=0, keepdims=True)


def _mm(a, b, *, name, ta=False, tb=False, res=None, norm_bwd=None, after=(), out_dtype=F32, tm=512, tn=512, tk=512):
    if ta:
        K, M = a.shape
    else:
        M, K = a.shape
    if tb:
        N, Kb = b.shape
    else:
        Kb, N = b.shape
    assert K == Kb, (a.shape, b.shape)
    tm, tn, tk = min(tm, M), min(tn, N), min(tk, K)
    assert M % tm == 0 and N % tn == 0 and K % tk == 0, (name, M, N, K, tm, tn, tk)
    nk = K // tk
    dims = (((0 if ta else 1,), (1 if tb else 0,)), ((), ()))
    has_res = res is not None
    has_norm = norm_bwd is not None
    assert not has_norm or tn == N

    def body(*refs):
        a_ref, b_ref = refs[:2]
        r_ref = refs[2] if has_res else None
        if has_norm:
            x_ref, w_ref, skip_ref = refs[2 + has_res:5 + has_res]
            o_ref, dw_ref, acc_ref = refs[-3:]
        else:
            o_ref, acc_ref = refs[-2:]
        i, k = pl.program_id(0), pl.program_id(2)
        part = lax.dot_general(a_ref[...].astype(BF16), b_ref[...].astype(BF16), dims, preferred_element_type=F32)

        @pl.when(k == 0)
        def _():
            acc_ref[...] = part

        @pl.when(k > 0)
        def _():
            acc_ref[...] += part

        @pl.when(k == nk - 1)
        def _():
            r = acc_ref[...]
            if has_res:
                r = r + r_ref[...]
            if has_norm:
                dx, dw = _rms_bwd_rows(r, x_ref[...], w_ref[...])
                o_ref[...] = skip_ref[...] + dx

                @pl.when(i == 0)
                def _():
                    dw_ref[...] = dw

                @pl.when(i > 0)
                def _():
                    dw_ref[...] += dw
            else:
                o_ref[...] = r.astype(out_dtype)

    a_spec = pl.BlockSpec((tk, tm), lambda i, j, k: (k, i)) if ta else pl.BlockSpec((tm, tk), lambda i, j, k: (i, k))
    b_spec = pl.BlockSpec((tn, tk), lambda i, j, k: (j, k)) if tb else pl.BlockSpec((tk, tn), lambda i, j, k: (k, j))
    o_spec = pl.BlockSpec((tm, tn), lambda i, j, k: (i, j))
    one = pl.BlockSpec((1, tn), lambda i, j, k: (0, 0))
    in_specs = [a_spec, b_spec] + [o_spec] * has_res + ([o_spec, one, o_spec] if has_norm else []) + [ANY] * len(after)
    args = (a, b) + ((res,) if has_res else ()) + (tuple(norm_bwd) if has_norm else ()) + tuple(after)
    return pl.pallas_call(
        body, name=name, grid=(M // tm, N // tn, nk), in_specs=in_specs,
        out_specs=[o_spec, one] if has_norm else o_spec,
        out_shape=[_sds((M, N)), _sds((1, N))] if has_norm else _sds((M, N), out_dtype),
        scratch_shapes=[pltpu.VMEM((tm, tn), F32)],
        compiler_params=_params(("arbitrary" if has_norm else "parallel", "parallel", "arbitrary")),
    )(*args)


def _in_proj(x, norm_w, w_land, *, name, after=(), tm=512):
    S, D = x.shape

    def body(x_ref, nw_ref, land_ref, *rest):
        h_ref, pa_ref, pz_ref, pb_ref, *scratch = rest[len(after):]

        @pl.when(pl.program_id(0) == 0)
        def _():
            _fetch_w_in(land_ref, *scratch)

        xv = x_ref[...]
        r = lax.rsqrt(jnp.mean(xv * xv, axis=-1, keepdims=True) + EPS)
        h = (xv * r * nw_ref[...]).astype(BF16)
        h_ref[...] = h
        for w_ref, p_ref in zip(scratch[:3], (pa_ref, pz_ref, pb_ref)):
            p_ref[...] = lax.dot_general(h, w_ref[...], (((1,), (1,)), ((), ())), preferred_element_type=F32)

    row = lambda n: pl.BlockSpec((tm, n), lambda i: (i, 0))
    full = lambda a: pl.BlockSpec(a.shape, lambda i: (0, 0))
    return pl.pallas_call(
        body, name=name, grid=(S // tm,), in_specs=[row(D), full(norm_w), ANY] + [ANY] * len(after),
        out_specs=[row(D)] + [row(n) for n in _W_IN_ROWS],
        out_shape=[_sds((S, D), BF16)] + [_sds((S, n)) for n in _W_IN_ROWS],
        scratch_shapes=_w_in_scratch(D), compiler_params=_params(("arbitrary",)),
    )(x, norm_w, w_land, *after)


def _in_proj_dx(ds, w_land, x, norm_w, skip, *, name, after=(), tm=512):
    S, D = x.shape
    n = len(ds)

    def body(*refs):
        d_refs, land_ref = refs[:n], refs[n]
        x_ref, nw_ref, skip_ref = refs[n + 1:n + 4]
        o_ref, dw_ref, *scratch = refs[n + 4 + len(after):]
        w_refs = scratch[:n]
        i = pl.program_id(0)

        @pl.when(i == 0)
        def _():
            _fetch_w_in(land_ref, *scratch)

        dh = jnp.dot(d_refs[0][...], w_refs[0][...], preferred_element_type=F32)
        for d_ref, w_ref in zip(d_refs[1:], w_refs[1:]):
            dh = dh + jnp.dot(d_ref[...], w_ref[...], preferred_element_type=F32)
        dx, dw = _rms_bwd_rows(dh, x_ref[...], nw_ref[...])
        o_ref[...] = skip_ref[...] + dx

        @pl.when(i == 0)
        def _():
            dw_ref[...] = dw

        @pl.when(i > 0)
        def _():
            dw_ref[...] += dw

    row = lambda c: pl.BlockSpec((tm, c), lambda i: (i, 0))
    full = lambda a: pl.BlockSpec(a.shape, lambda i: (0, 0))
    return pl.pallas_call(
        body, name=name, grid=(S // tm,),
        in_specs=[row(d.shape[1]) for d in ds] + [ANY, row(D), full(norm_w), row(D)] + [ANY] * len(after),
        out_specs=[row(D), pl.BlockSpec((1, D), lambda i: (0, 0))], out_shape=[_sds((S, D)), _sds((1, D))],
        scratch_shapes=_w_in_scratch(D), compiler_params=_params(("arbitrary",)),
    )(*ds, w_land, x, norm_w, skip, *after)


def _out_proj_norm(a, w, x, norm_w, *, name, tm=512):
    S, D = x.shape

    def body(a_ref, w_ref, x_ref, nw_ref, x1_ref, h_ref):
        x1 = x_ref[...] + jnp.dot(a_ref[...], w_ref[...], preferred_element_type=F32)
        x1_ref[...] = x1
        r = lax.rsqrt(jnp.mean(x1 * x1, axis=-1, keepdims=True) + EPS)
        h_ref[...] = (x1 * r * nw_ref[...]).astype(BF16)

    row = pl.BlockSpec((tm, D), lambda i: (i, 0))
    return pl.pallas_call(
        body, name=name, grid=(S // tm,),
        in_specs=[pl.BlockSpec((tm, a.shape[1]), lambda i: (i, 0)), pl.BlockSpec(w.shape, lambda i: (0, 0)), row,
                  pl.BlockSpec((1, D), lambda i: (0, 0))],
        out_specs=[row, row], out_shape=[_sds((S, D)), _sds((S, D), BF16)], compiler_params=_params(("parallel",)),
    )(a, w, x, norm_w)


def _shifted(x, start, n):
    aligned = -(-start // HALO) * HALO
    assert aligned + n <= x.shape[0], (start, n, x.shape)
    return (x if aligned == start else pltpu.roll(x, aligned - start, axis=0))[aligned:aligned + n]


def _conv_rows(prev, cur, w, taps):
    n = cur.shape[0]
    xs = jnp.concatenate([prev, cur], axis=0)
    base = HALO - (taps - 1)
    out = _shifted(xs, base, n) * w[0:1]
    for i in range(1, taps):
        out = out + _shifted(xs, base + i, n) * w[i:i + 1]
    return out


def _conv_rows_bwd(cur_d, next_d, prev_x, cur_x, w, taps):
    n = cur_d.shape[0]
    ds = jnp.concatenate([cur_d, next_d], axis=0)
    dx = _shifted(ds, taps - 1, n) * w[0:1]
    for i in range(1, taps):
        dx = dx + _shifted(ds, taps - 1 - i, n) * w[i:i + 1]
    xs = jnp.concatenate([prev_x, cur_x], axis=0)
    base = HALO - (taps - 1)
    dws = [jnp.sum(cur_d * _shifted(xs, base + i, n), axis=0, keepdims=True) for i in range(taps)]
    return dx, jnp.concatenate(dws, axis=0)


def _halo_specs(tm, width, col, nblk):
    per = tm // HALO
    prev = pl.BlockSpec((HALO, width), lambda i, *_: (jnp.maximum(i * per - 1, 0), col))
    nxt = pl.BlockSpec((HALO, width), lambda i, *_: (jnp.minimum((i + 1) * per, nblk * per - 1), col))
    return prev, nxt


def _softplus(x):
    return jnp.maximum(x, 0.0) + jnp.log1p(jnp.exp(-jnp.abs(x)))


def _chunk_tri(tm, upper=False):
    r = lax.broadcasted_iota(jnp.int32, (tm, tm), 0)
    c = lax.broadcasted_iota(jnp.int32, (tm, tm), 1)
    same = lax.div(r, CHUNK) == lax.div(c, CHUNK)
    order = (c >= r) if upper else (c <= r)
    return jnp.where(same & order, 1.0, 0.0)


def _gdn_prep_fwd(proj_a, conv_w, a_log, dt_bias, *, name, tm=256):
    S = proj_a.shape[0]
    nblk = S // tm
    W3 = 3 * GDN_WIDTH

    def body(cur_ref, prev_ref, ba_ref, cw_ref, al_ref, dt_ref, qn_ref, kn_ref, v_ref, gcb_ref, bb_ref):
        i = pl.program_id(0)
        prev = jnp.where(i > 0, prev_ref[...], 0.0)
        c = _conv_rows(prev, cur_ref[...], cw_ref[...], GDN_CONV)
        a = c * _sigmoid(c)
        ba = ba_ref[...]
        lane = lax.broadcasted_iota(jnp.int32, (tm, 128), 1)
        g4 = jnp.zeros((tm, 128), F32)
        for h in range(GDN_HEADS):
            sl = slice(GDN_DIM * h, GDN_DIM * (h + 1))
            qh = a[:, GDN_DIM * h:GDN_DIM * (h + 1)]
            kh = a[:, GDN_WIDTH + GDN_DIM * h:GDN_WIDTH + GDN_DIM * (h + 1)]
            qn_ref[:, sl] = qh * (lax.rsqrt(jnp.sum(qh * qh, axis=-1, keepdims=True) + EPS) * (GDN_DIM ** -0.5))
            kn_ref[:, sl] = kh * lax.rsqrt(jnp.sum(kh * kh, axis=-1, keepdims=True) + EPS)
            beta = _sigmoid(ba[:, h:h + 1])
            bb_ref[:, sl] = jnp.broadcast_to(beta, (tm, GDN_DIM))
            g = -jnp.exp(al_ref[0:1, h:h + 1]) * _softplus(ba[:, GDN_HEADS + h:GDN_HEADS + h + 1] + dt_ref[0:1, h:h + 1])
            g4 = jnp.where(lane == h, g, g4)
        v_ref[...] = a[:, 2 * GDN_WIDTH:]
        gc = _exact_tri_dot(_chunk_tri(tm), g4)
        for h in range(GDN_HEADS):
            gcb_ref[:, GDN_DIM * h:GDN_DIM * (h + 1)] = jnp.broadcast_to(gc[:, h:h + 1], (tm, GDN_DIM))

    prev_spec, _ = _halo_specs(tm, W3, 0, nblk)
    row = pl.BlockSpec((tm, GDN_WIDTH), lambda i: (i, 0))
    small = lambda a: pl.BlockSpec(a.shape, lambda i: (0, 0))
    return pl.pallas_call(
        body, name=name, grid=(nblk,),
        in_specs=[pl.BlockSpec((tm, W3), lambda i: (i, 0)), prev_spec,
                  pl.BlockSpec((tm, 128), lambda i: (i, W3 // 128)), small(conv_w), small(a_log), small(dt_bias)],
        out_specs=[row] * 5, out_shape=[_sds((S, GDN_WIDTH))] * 5, compiler_params=_params(("parallel",)),
    )(proj_a, proj_a, proj_a, conv_w, a_log, dt_bias)


GDN_STACK = GDN_HEADS * CHUNK


def _stack(ref, rows):
    return jnp.concatenate([ref[rows, GDN_DIM * h:GDN_DIM * (h + 1)] for h in range(GDN_HEADS)], axis=0)


def _unstack_to(ref, rows, x):
    for h in range(GDN_HEADS):
        ref[rows, GDN_DIM * h:GDN_DIM * (h + 1)] = x[CHUNK * h:CHUNK * (h + 1)].astype(ref.dtype)


def _stack_masks():
    r = lax.broadcasted_iota(jnp.int32, (GDN_STACK, GDN_STACK), 0)
    c = lax.broadcasted_iota(jnp.int32, (GDN_STACK, GDN_STACK), 1)
    same = (r & -CHUNK) == (c & -CHUNK)
    return same & (r >= c), same & (r > c), r == c


def _stack_decay(gs, bs, incl):
    g2 = jnp.concatenate([gs, gs], axis=1)
    diff = g2 - g2.T
    dec = jnp.where(incl, jnp.exp(jnp.where(incl, diff, 0.0)), 0.0)
    return dec, jnp.concatenate([bs, bs], axis=1).T


def _head_mask():
    r = lax.broadcasted_iota(jnp.int32, (GDN_STACK, GDN_WIDTH), 0)
    c = lax.broadcasted_iota(jnp.int32, (GDN_STACK, GDN_WIDTH), 1)
    return (r & -CHUNK) * (GDN_DIM // CHUNK) == (c & -GDN_DIM)


def _head_spread(x):
    return jnp.where(_head_mask(), jnp.concatenate([x] * GDN_HEADS, axis=1), 0.0)


def _head_diag(x):
    xm = jnp.where(_head_mask(), x, 0.0)
    out = xm[:, 0:GDN_DIM]
    for h in range(1, GDN_HEADS):
        out = out + xm[:, GDN_DIM * h:GDN_DIM * (h + 1)]
    return out


def _last_rows(gs, n):
    return jnp.concatenate([jnp.broadcast_to(gs[CHUNK * (h + 1) - 1:CHUNK * (h + 1)], (n, GDN_DIM)) for h in range(GDN_HEADS)], axis=0)


def _gdn_chunk_fwd(qn, kn, v, gcb, bb, *, name):
    S = qn.shape[0]

    def body(qn_ref, kn_ref, v_ref, gcb_ref, bb_ref, uv_ref, wk_ref, at_ref, t_ref, wkb_ref, qdb_ref, keb_ref):
        incl, strict, diag = _stack_masks()
        for c in range(CHUNKS_PER_STEP):
            rows = slice(CHUNK * c, CHUNK * (c + 1))
            srows = slice(GDN_STACK * c, GDN_STACK * (c + 1))
            q, k, vv, gs, bs = [_stack(r, rows) for r in (qn_ref, kn_ref, v_ref, gcb_ref, bb_ref)]
            dec, bt = _stack_decay(gs, bs, incl)
            p = -jnp.where(strict, dec * _bdot_nt(k, k) * bt, 0.0)
            t = jnp.where(diag, 1.0, 0.0) + p
            for _ in range(5):
                p = _bdot(p, p)
                t = t + _bdot(t, p)
            sol = _dot3(t, jnp.concatenate([vv, jnp.exp(gs) * k], axis=1), ((1,), (0,)))
            _unstack_to(uv_ref, rows, sol[:, :GDN_DIM])
            _unstack_to(wk_ref, rows, sol[:, GDN_DIM:])
            at_ref[srows, :] = dec * _bdot_nt(q, k) * bt
            t_ref[srows, :] = t
            wkb_ref[srows, :] = _head_spread(sol[:, GDN_DIM:]).astype(BF16)
            qdb_ref[srows, :] = _head_spread(q * jnp.exp(gs)).astype(BF16)
            keb_ref[srows, :] = _head_spread(k * jnp.exp(_last_rows(gs, CHUNK) - gs) * bs).astype(BF16)

    step = CHUNKS_PER_STEP * CHUNK
    row = pl.BlockSpec((step, GDN_WIDTH), lambda n: (n, 0))
    sq = pl.BlockSpec((CHUNKS_PER_STEP * GDN_STACK, GDN_STACK), lambda n: (n, 0))
    wide = pl.BlockSpec((CHUNKS_PER_STEP * GDN_STACK, GDN_WIDTH), lambda n: (n, 0))
    nsq = S // CHUNK * GDN_STACK
    return pl.pallas_call(
        body, name=name, grid=(S // step,), in_specs=[row] * 5, out_specs=[row, row, sq, sq, wide, wide, wide],
        out_shape=[_sds((S, GDN_WIDTH)), _sds((S, GDN_WIDTH)), _sds((nsq, GDN_STACK)), _sds((nsq, GDN_STACK))]
        + [_sds((nsq, GDN_WIDTH), BF16)] * 3,
        compiler_params=_params(("parallel",)),
    )(qn, kn, v, gcb, bb)


SCAN_CHUNKS = 8


def _gdn_scan_fwd(uv, at, wkb, qdb, keb, gcb, proj_z, gnw, *, name):
    S = uv.shape[0]
    nc = S // CHUNK

    def body(uv_ref, at_ref, wkb_ref, qdb_ref, keb_ref, gcb_ref, z_ref, gnw_ref, o_ref, u_ref, sp_ref, oa_ref, st_ref):
        n = pl.program_id(0)

        @pl.when(n == 0)
        def _():
            st_ref[...] = jnp.zeros_like(st_ref)

        for c in range(SCAN_CHUNKS):
            rows = slice(CHUNK * c, CHUNK * (c + 1))
            srows = slice(GDN_STACK * c, GDN_STACK * (c + 1))
            st = st_ref[...]
            sp_ref[GDN_WIDTH * c:GDN_WIDTH * (c + 1), :] = st
            uv, gs, z = [_stack(r, rows) for r in (uv_ref, gcb_ref, z_ref)]
            u = uv - _bdot(wkb_ref[srows, :], st)
            o = _bdot(qdb_ref[srows, :], st) + _bdot(at_ref[srows, :], u)
            st_ref[...] = jnp.exp(_last_rows(gs, GDN_DIM)) * st + _bdot_tn(keb_ref[srows, :], u)
            _unstack_to(u_ref, rows, u)
            _unstack_to(o_ref, rows, o)
            r = lax.rsqrt(jnp.mean(o * o, axis=-1, keepdims=True) + EPS)
            oa = o * r * gnw_ref[...] * (z * _sigmoid(z))
            oa_ref[rows, :] = jnp.concatenate([oa[CHUNK * h:CHUNK * (h + 1)] for h in range(GDN_HEADS)], axis=1).astype(BF16)

    row = pl.BlockSpec((SCAN_CHUNKS * CHUNK, GDN_WIDTH), lambda n: (n, 0))
    sq = pl.BlockSpec((SCAN_CHUNKS * GDN_STACK, GDN_STACK), lambda n: (n, 0))
    wide = pl.BlockSpec((SCAN_CHUNKS * GDN_STACK, GDN_WIDTH), lambda n: (n, 0))
    return pl.pallas_call(
        body, name=name, grid=(nc // SCAN_CHUNKS,),
        in_specs=[row, sq, wide, wide, wide, row, row, pl.BlockSpec((1, GDN_DIM), lambda n: (0, 0))],
        out_specs=[row, row, pl.BlockSpec((SCAN_CHUNKS * GDN_WIDTH, GDN_DIM), lambda n: (n, 0)), row],
        out_shape=[_sds((S, GDN_WIDTH)), _sds((S, GDN_WIDTH)), _sds((nc * GDN_WIDTH, GDN_DIM)), _sds((S, 2 * GDN_WIDTH), BF16)],
        scratch_shapes=[pltpu.VMEM((GDN_WIDTH, GDN_DIM), F32)],
        compiler_params=_params(("arbitrary",)),
    )(uv, at, wkb, qdb, keb, gcb, proj_z, gnw)


def _gdn_scan_bwd(d_oab, o, proj_z, gnw, sp, u, at, wkb, qdb, keb, gcb, *, name, after=()):
    S = o.shape[0]
    nc = S // CHUNK
    ns = nc // SCAN_CHUNKS

    def body(do_ref, o_ref, z_ref, gnw_ref, sp_ref, u_ref, at_ref, wkb_ref, qdb_ref, keb_ref, gcb_ref, *rest):
        dz_ref, dgn_ref, du_ref, dwk_ref, dat_ref, dqd_ref, dke_ref, dgl_ref, ds_ref = rest[len(after):]
        n = pl.program_id(0)

        @pl.when(n == 0)
        def _():
            ds_ref[...] = jnp.zeros_like(ds_ref)
            dgn_ref[...] = jnp.zeros_like(dgn_ref)

        gw = gnw_ref[...]
        for c in reversed(range(SCAN_CHUNKS)):
            rows = slice(CHUNK * c, CHUNK * (c + 1))
            srows = slice(GDN_STACK * c, GDN_STACK * (c + 1))
            d_oa, oo, z, uu, gs = [_stack(r, rows) for r in (do_ref, o_ref, z_ref, u_ref, gcb_ref)]
            sg = _sigmoid(z)
            r = lax.rsqrt(jnp.mean(oo * oo, axis=-1, keepdims=True) + EPS)
            xh = oo * r
            dy = d_oa * (z * sg)
            _unstack_to(dz_ref, rows, d_oa * (xh * gw) * _dsilu(z, sg))
            dgn_ref[...] += jnp.sum(dy * xh, axis=0, keepdims=True)
            dxh = dy * gw
            do = r * (dxh - xh * jnp.mean(dxh * xh, axis=-1, keepdims=True))

            st = sp_ref[GDN_WIDTH * c:GDN_WIDTH * (c + 1), :]
            dst = ds_ref[...]
            ge = jnp.exp(_last_rows(gs, GDN_DIM))
            _unstack_to(dqd_ref, rows, _head_diag(_bdot_nt(do, st)))
            dat_ref[srows, :] = _bdot_nt(do, uu)
            du = _bdot_tn(at_ref[srows, :], do) + _bdot(keb_ref[srows, :], dst)
            _unstack_to(dke_ref, rows, _head_diag(_bdot_nt(uu, dst)))
            prod = dst * st
            for h in range(GDN_HEADS):
                blk = prod[GDN_DIM * h:GDN_DIM * (h + 1)]
                dge = jnp.sum(jnp.sum(blk, axis=1, keepdims=True), axis=0, keepdims=True)
                dgl_ref[c, :, GDN_DIM * h:GDN_DIM * (h + 1)] = jnp.broadcast_to(dge * ge[GDN_DIM * h:GDN_DIM * h + 1], (8, GDN_DIM))
            ds_ref[...] = _bdot_tn(qdb_ref[srows, :], do) + ge * dst - _bdot_tn(wkb_ref[srows, :], du)
            _unstack_to(du_ref, rows, du)
            _unstack_to(dwk_ref, rows, -_head_diag(_bdot_nt(du, st)))

    rev = lambda n: (ns - 1 - n, 0)
    row = pl.BlockSpec((SCAN_CHUNKS * CHUNK, GDN_WIDTH), rev)
    sq = pl.BlockSpec((SCAN_CHUNKS * GDN_STACK, GDN_STACK), rev)
    wide = pl.BlockSpec((SCAN_CHUNKS * GDN_STACK, GDN_WIDTH), rev)
    one = pl.BlockSpec((1, GDN_DIM), lambda n: (0, 0))
    return pl.pallas_call(
        body, name=name, grid=(ns,),
        in_specs=[row, row, row, one, pl.BlockSpec((SCAN_CHUNKS * GDN_WIDTH, GDN_DIM), rev), row, sq, wide, wide, wide, row]
        + [ANY] * len(after),
        out_specs=[row, one, row, row, sq, row, row, pl.BlockSpec((SCAN_CHUNKS, 8, GDN_WIDTH), lambda n: (ns - 1 - n, 0, 0))],
        out_shape=[_sds((S, GDN_WIDTH), BF16), _sds((1, GDN_DIM)), _sds((S, GDN_WIDTH)), _sds((S, GDN_WIDTH)),
                   _sds((nc * GDN_STACK, GDN_STACK)), _sds((S, GDN_WIDTH)), _sds((S, GDN_WIDTH)), _sds((nc, 8, GDN_WIDTH))],
        scratch_shapes=[pltpu.VMEM((GDN_WIDTH, GDN_DIM), F32)],
        compiler_params=_params(("arbitrary",)),
    )(d_oab, o, proj_z, gnw, sp, u, at, wkb, qdb, keb, gcb, *after)


def _gdn_chunk_bwd(qn, kn, gcb, bb, tmat, uv, wk, du, dwk, dat, dqd, dke, dgl, *, name):
    S = qn.shape[0]

    def body(qn_ref, kn_ref, gcb_ref, bb_ref, t_ref, uv_ref, wk_ref, du_ref, dwk_ref, dat_ref, dqd_ref, dke_ref,
             dgl_ref, dq_ref, dk_ref, dv_ref, dg_ref, dbeta_ref):
        incl, strict, _ = _stack_masks()
        lane = lax.broadcasted_iota(jnp.int32, (CHUNK, 128), 1)
        rowi = lax.broadcasted_iota(jnp.int32, (CHUNK, 1), 0)
        rsum = lambda x: jnp.sum(x, axis=-1, keepdims=True)
        for c in range(CHUNKS_PER_STEP):
            rows = slice(CHUNK * c, CHUNK * (c + 1))
            srows = slice(GDN_STACK * c, GDN_STACK * (c + 1))
            q, k, gs, bs, uv, wk, du, dwk, dqd, dke = [
                _stack(r, rows) for r in (qn_ref, kn_ref, gcb_ref, bb_ref, uv_ref, wk_ref, du_ref, dwk_ref, dqd_ref, dke_ref)]
            dec, bt = _stack_decay(gs, bs, incl)
            kk = _bdot_nt(k, k)
            qk = _bdot_nt(q, k)
            d_rhs = _dot3(t_ref[srows, :], jnp.concatenate([du, dwk], axis=1), ((0,), (0,)))
            sol = jnp.concatenate([uv, wk], axis=1)
            d_l = jnp.where(strict, -_dot3(d_rhs, sol, ((1,), (1,))), 0.0)
            d_a = jnp.where(incl, dat_ref[srows, :], 0.0)
            gam = jnp.exp(gs)
            e = jnp.exp(_last_rows(gs, CHUNK) - gs)
            d_gk = d_rhs[:, GDN_DIM:]
            ml = d_l * dec * bt
            ma = d_a * dec * bt
            _unstack_to(dq_ref, rows, _bdot(ma, k) + dqd * gam)
            _unstack_to(dk_ref, rows, _bdot(ml + ml.T, k) + _bdot_tn(ma, q) + d_gk * gam + dke * (e * bs))
            _unstack_to(dv_ref, rows, d_rhs[:, :GDN_DIM])
            wb = d_l * dec * kk + d_a * dec * qk
            ew = wb * bt
            s_ke = rsum(dke * k * (e * bs))
            dbeta = rsum(wb.T) + rsum(dke * k * e)
            dgc = rsum(ew) - rsum(ew.T) + rsum(dqd * q * gam) + rsum(d_gk * k * gam) - s_ke
            dgc4 = jnp.zeros((CHUNK, 128), F32)
            db4 = jnp.zeros((CHUNK, 128), F32)
            for h in range(GDN_HEADS):
                hr = slice(CHUNK * h, CHUNK * (h + 1))
                tail = jnp.sum(s_ke[hr], axis=0, keepdims=True) + dgl_ref[c, 0:1, GDN_DIM * h:GDN_DIM * h + 1]
                dgc4 = jnp.where(lane == h, dgc[hr] + jnp.where(rowi == CHUNK - 1, tail, 0.0), dgc4)
                db4 = jnp.where(lane == h, dbeta[hr], db4)
            dg_ref[rows, :] = _exact_tri_dot(_chunk_tri(CHUNK, upper=True), dgc4)
            dbeta_ref[rows, :] = db4

    step = CHUNKS_PER_STEP * CHUNK
    row = pl.BlockSpec((step, GDN_WIDTH), lambda n: (n, 0))
    sq = pl.BlockSpec((CHUNKS_PER_STEP * GDN_STACK, GDN_STACK), lambda n: (n, 0))
    col = pl.BlockSpec((step, 128), lambda n: (n, 0))
    return pl.pallas_call(
        body, name=name, grid=(S // step,),
        in_specs=[row] * 4 + [sq, row, row, row, row, sq, row, row,
                              pl.BlockSpec((CHUNKS_PER_STEP, 8, GDN_WIDTH), lambda n: (n, 0, 0))],
        out_specs=[row, row, row, col, col],
        out_shape=[_sds((S, GDN_WIDTH))] * 3 + [_sds((S, 128))] * 2, compiler_params=_params(("parallel",)),
    )(qn, kn, gcb, bb, tmat, uv, wk, du, dwk, dat, dqd, dke, dgl)


def _gdn_prep_bwd(dqn, dkn, dv, dg, dbeta, proj_a, conv_w, a_log, dt_bias, *, name, tm=256):
    S = proj_a.shape[0]
    nblk = S // tm
    W3 = 3 * GDN_WIDTH

    def body(dqn_ref, dkn_ref, dv_ref, dg_ref, dbeta_ref, cur_ref, prev_ref, ba_ref, cw_ref, al_ref, dt_ref,
             dc_ref, dba_ref, sm_ref):
        i = pl.program_id(0)
        prev = jnp.where(i > 0, prev_ref[...], 0.0)
        c = _conv_rows(prev, cur_ref[...], cw_ref[...], GDN_CONV)
        sg = _sigmoid(c)
        a = c * sg
        dsl = _dsilu(c, sg)
        ba = ba_ref[...]
        lane = lax.broadcasted_iota(jnp.int32, (tm, 128), 1)
        lane1 = lax.broadcasted_iota(jnp.int32, (1, 128), 1)
        dba = jnp.zeros((tm, 128), F32)
        sm = jnp.zeros((1, 128), F32)
        for h in range(GDN_HEADS):
            sl = slice(GDN_DIM * h, GDN_DIM * (h + 1))
            ks = slice(GDN_WIDTH + GDN_DIM * h, GDN_WIDTH + GDN_DIM * (h + 1))
            qh, kh = a[:, sl], a[:, ks]
            rq = lax.rsqrt(jnp.sum(qh * qh, axis=-1, keepdims=True) + EPS)
            rk = lax.rsqrt(jnp.sum(kh * kh, axis=-1, keepdims=True) + EPS)
            qhat, khat = qh * rq, kh * rk
            dyq = dqn_ref[:, sl] * (GDN_DIM ** -0.5)
            dyk = dkn_ref[:, sl]
            dq = rq * (dyq - qhat * jnp.sum(dyq * qhat, axis=-1, keepdims=True))
            dk = rk * (dyk - khat * jnp.sum(dyk * khat, axis=-1, keepdims=True))
            dc_ref[:, sl] = dq * dsl[:, sl]
            dc_ref[:, ks] = dk * dsl[:, ks]
            beta = _sigmoid(ba[:, h:h + 1])
            db = dbeta_ref[:, h:h + 1] * beta * (1.0 - beta)
            aneg = -jnp.exp(al_ref[0:1, h:h + 1])
            xa = ba[:, GDN_HEADS + h:GDN_HEADS + h + 1] + dt_ref[0:1, h:h + 1]
            dgh = dg_ref[:, h:h + 1]
            dxa = dgh * aneg * _sigmoid(xa)
            dba = jnp.where(lane == h, db, dba)
            dba = jnp.where(lane == GDN_HEADS + h, dxa, dba)
            d_alog = jnp.sum(dgh * _softplus(xa), axis=0, keepdims=True) * aneg
            sm = jnp.where(lane1 == h, d_alog, sm)
            sm = jnp.where(lane1 == GDN_HEADS + h, jnp.sum(dxa, axis=0, keepdims=True), sm)
        vs = slice(2 * GDN_WIDTH, W3)
        dc_ref[:, vs] = dv_ref[...] * dsl[:, vs]
        dba_ref[...] = dba

        @pl.when(i == 0)
        def _():
            sm_ref[...] = sm

        @pl.when(i > 0)
        def _():
            sm_ref[...] += sm

    prev_spec, _ = _halo_specs(tm, W3, 0, nblk)
    row = pl.BlockSpec((tm, GDN_WIDTH), lambda i: (i, 0))
    col = pl.BlockSpec((tm, 128), lambda i: (i, 0))
    small = lambda a: pl.BlockSpec(a.shape, lambda i: (0, 0))
    return pl.pallas_call(
        body, name=name, grid=(nblk,),
        in_specs=[row, row, row, col, col, pl.BlockSpec((tm, W3), lambda i: (i, 0)), prev_spec,
                  pl.BlockSpec((tm, 128), lambda i: (i, W3 // 128)), small(conv_w), small(a_log), small(dt_bias)],
        out_specs=[pl.BlockSpec((tm, W3), lambda i: (i, 0)), col, pl.BlockSpec((1, 128), lambda i: (0, 0))],
        out_shape=[_sds((S, W3)), _sds((S, 128)), _sds((1, 128))], compiler_params=_params(("arbitrary",)),
    )(dqn, dkn, dv, dg, dbeta, proj_a, proj_a, proj_a, conv_w, a_log, dt_bias)


def _gdn_conv_bwd(dc, dba, proj_a, conv_w, *, name, tm=256):
    S = proj_a.shape[0]
    nblk = S // tm
    W3 = 3 * GDN_WIDTH

    def body(dc_ref, dnext_ref, dba_ref, cur_ref, prev_ref, cw_ref, da_ref, dcw_ref):
        i = pl.program_id(0)
        prev = jnp.where(i > 0, prev_ref[...], 0.0)
        nxt = jnp.where(i < nblk - 1, dnext_ref[...], 0.0)
        dx, dw = _conv_rows_bwd(dc_ref[...], nxt, prev, cur_ref[...], cw_ref[...], GDN_CONV)
        da_ref[:, 0:W3] = dx.astype(BF16)
        da_ref[:, W3:] = dba_ref[...].astype(BF16)

        @pl.when(i == 0)
        def _():
            dcw_ref[...] = dw

        @pl.when(i > 0)
        def _():
            dcw_ref[...] += dw

    prev_spec, next_spec = _halo_specs(tm, W3, 0, nblk)
    wide = pl.BlockSpec((tm, W3), lambda i: (i, 0))
    return pl.pallas_call(
        body, name=name, grid=(nblk,),
        in_specs=[wide, next_spec, pl.BlockSpec((tm, 128), lambda i: (i, 0)), wide, prev_spec,
                  pl.BlockSpec(conv_w.shape, lambda i: (0, 0))],
        out_specs=[pl.BlockSpec((tm, A_COLS), lambda i: (i, 0)), pl.BlockSpec(conv_w.shape, lambda i: (0, 0))],
        out_shape=[_sds((S, A_COLS), BF16), _sds(conv_w.shape)], compiler_params=_params(("arbitrary",)),
    )(dc, dc, dba, proj_a, proj_a, conv_w)


def _band_mask(nk):
    i = lax.broadcasted_iota(jnp.int32, (2 * BAND, nk), 0) & (BAND - 1)
    j = lax.broadcasted_iota(jnp.int32, (2 * BAND, nk), 1)
    if nk == BAND:
        return j <= i
    return (j >= i) & (j <= i + BAND)


def _stack_heads(x, lo):
    return jnp.concatenate([jnp.where(lo, x, 0.0), jnp.where(lo, 0.0, x)], axis=0)


def _stack_cols(x):
    return jnp.concatenate([x[:, 0:1], x[:, DIL_DIM:DIL_DIM + 1]], axis=0)


def _unstack(x, lo):
    return jnp.where(lo, x[0:BAND], x[BAND:2 * BAND])


def _rows(start, size, stride):
    return pl.ds(start, size) if stride == 1 else pl.ds(start, size, stride=stride)


ATTN_LANES = 4


def _attn_blocks(S, visit_many, lanes=ATTN_LANES):
    for d in DILATIONS:
        nb = S // (d * BAND)
        if d == 1:
            half = nb // 2
            visit_many(d, [(0, 0, True), (0, half, False)])

            def pair(n, c):
                visit_many(1, [(0, n, False), (0, n + half, False)])
                return c
            lax.fori_loop(1, half, pair, 0)
        elif nb > 1:
            for r0 in range(0, d, lanes):
                visit_many(d, [(r0 + t, 0, True) for t in range(lanes)])

                def column(n, c, d=d, r0=r0):
                    visit_many(d, [(r0 + t, n, False) for t in range(lanes)])
                    return c
                lax.fori_loop(1, nb, column, 0)
        else:
            def group(g, c, d=d):
                visit_many(d, [(g * lanes + t, 0, True) for t in range(lanes)])
                return c
            lax.fori_loop(0, d // lanes, group, 0)


def _attn_fwd(proj_b, oab, *, name):
    S = proj_b.shape[0]
    scale = DIL_DIM ** -0.5

    def body(q_ref, k_ref, v_ref, oab_in_ref, ob_ref, lse_ref, m_ref, l_ref, acc_ref):
        del oab_in_ref
        lane = lax.broadcasted_iota(jnp.int32, (BAND, 128), 1)
        lo = lane < DIL_DIM
        m_ref[...] = jnp.full_like(m_ref, NEG_BIG)
        l_ref[...] = jnp.zeros_like(l_ref)
        acc_ref[...] = jnp.zeros_like(acc_ref)

        def load(d, r, n, first):
            nk = BAND if first else 2 * BAND
            qrows = _rows(r + n * (BAND * d), BAND, d)
            krows = _rows(r if first else r + (n - 1) * (BAND * d), nk, d)
            return dict(nk=nk, qrows=qrows, q=q_ref[qrows, :] * scale, k=k_ref[krows, :].astype(BF16),
                        v=v_ref[krows, :].astype(BF16), m=m_ref[qrows, :], l=l_ref[qrows, :], acc=acc_ref[qrows, :])

        def compute(b):
            q, k, v = b["q"], b["k"], b["v"]
            s = jnp.where(_band_mask(b["nk"]), _bdot_nt(_stack_heads(q, lo), k), NEG_BIG)
            m_old = _stack_cols(b["m"])
            m_new = jnp.maximum(m_old, jnp.max(s, axis=-1, keepdims=True))
            p = jnp.exp(s - m_new)
            alpha = _unstack(jnp.exp(m_old - m_new), lo)
            l_new = alpha * b["l"] + _unstack(jnp.sum(p, axis=-1, keepdims=True), lo)
            return _unstack(m_new, lo), l_new, alpha * b["acc"] + _unstack(_bdot(p, v), lo)

        def visit_many(d, blocks):
            loaded = [load(d, *blk) for blk in blocks]
            done = [compute(b) for b in loaded]
            for b, (m_new, l_new, acc_new) in zip(loaded, done):
                m_ref[b["qrows"], :] = m_new
                l_ref[b["qrows"], :] = l_new
                acc_ref[b["qrows"], :] = acc_new

        _attn_blocks(S, visit_many)
        ob_ref[...] = (acc_ref[...] / l_ref[...]).astype(BF16)
        lse_ref[...] = m_ref[...] + jnp.log(l_ref[...])

    part = lambda t: pl.BlockSpec((S, 128), lambda p: (0, 3 * p + t))
    return pl.pallas_call(
        body, name=name, grid=(DIL_PAIRS,),
        in_specs=[part(0), part(1), part(2), pl.BlockSpec(memory_space=pl.ANY)],
        out_specs=[pl.BlockSpec((S, 128), lambda p: (0, GDN_WIDTH // 128 + p)), pl.BlockSpec((S, 128), lambda p: (0, p))],
        out_shape=[_sds(oab.shape, BF16), _sds((S, DIL_WIDTH))],
        scratch_shapes=[pltpu.VMEM((S, 128), F32)] * 3, input_output_aliases={3: 0},
        compiler_params=_params(("parallel",)),
    )(proj_b, proj_b, proj_b, oab)


def _attn_bwd(proj_b, oab, d_oab, lse, *, name):
    S = proj_b.shape[0]
    scale = DIL_DIM ** -0.5

    def body(q_ref, k_ref, v_ref, o_ref, do_ref, lse_ref, dqkv_ref, dq_ref, dk_ref, dv_ref, delta_ref):
        lane = lax.broadcasted_iota(jnp.int32, (BAND, 128), 1)
        lo = lane < DIL_DIM
        dq_ref[...] = jnp.zeros_like(dq_ref)
        dk_ref[...] = jnp.zeros_like(dk_ref)
        dv_ref[...] = jnp.zeros_like(dv_ref)
        prod = do_ref[...] * o_ref[...].astype(F32)
        lo_all = lax.broadcasted_iota(jnp.int32, (S, 128), 1) < DIL_DIM
        delta_ref[...] = jnp.where(lo_all, jnp.sum(jnp.where(lo_all, prod, 0.0), axis=-1, keepdims=True),
                                   jnp.sum(jnp.where(lo_all, 0.0, prod), axis=-1, keepdims=True))

        def load(d, r, n, first):
            nk = BAND if first else 2 * BAND
            qrows = _rows(r + n * (BAND * d), BAND, d)
            krows = _rows(r if first else r + (n - 1) * (BAND * d), nk, d)
            return dict(nk=nk, qrows=qrows, krows=krows, q=q_ref[qrows, :] * scale, k=k_ref[krows, :], v=v_ref[krows, :],
                        do=do_ref[qrows, :], delta=delta_ref[qrows, :], lse=lse_ref[qrows, :],
                        dq=dq_ref[qrows, :], dk=dk_ref[krows, :], dv=dv_ref[krows, :])

        def compute(b):
            q, k, v, do = b["q"], b["k"], b["v"], b["do"]
            qs, dos = _stack_heads(q, lo), _stack_heads(do, lo)
            p = jnp.where(_band_mask(b["nk"]), jnp.exp(_bdot_nt(qs, k) - _stack_cols(b["lse"])), 0.0)
            ds = p * (_bdot_nt(dos, v) - _stack_cols(b["delta"]))
            dq = b["dq"] + _unstack(_bdot(ds, k), lo) * scale
            return dq, b["dk"] + _bdot_tn(ds, qs), b["dv"] + _bdot_tn(p, dos)

        def visit_many(d, blocks):
            loaded = [load(d, *blk) for blk in blocks]
            done = [compute(b) for b in loaded]
            for b, (dq, dk, dv) in zip(loaded, done):
                dq_ref[b["qrows"], :] = dq
                dk_ref[b["krows"], :] = dk
                dv_ref[b["krows"], :] = dv

        _attn_blocks(S, visit_many, lanes=2)
        dqkv_ref[:, 0:128] = dq_ref[...].astype(BF16)
        dqkv_ref[:, 128:256] = dk_ref[...].astype(BF16)
        dqkv_ref[:, 256:384] = dv_ref[...].astype(BF16)

    half = lambda p: (0, GDN_WIDTH // 128 + p)
    part = lambda t: pl.BlockSpec((S, 128), lambda p: (0, 3 * p + t))
    return pl.pallas_call(
        body, name=name, grid=(DIL_PAIRS,),
        in_specs=[part(0), part(1), part(2), pl.BlockSpec((S, 128), half), pl.BlockSpec((S, 128), half),
                  pl.BlockSpec((S, 128), lambda p: (0, p))],
        out_specs=pl.BlockSpec((S, 384), lambda p: (0, p)), out_shape=_sds((S, 3 * DIL_WIDTH), BF16),
        scratch_shapes=[pltpu.VMEM((S, 128), F32)] * 4, compiler_params=_params(("parallel",)),
    )(proj_b, proj_b, proj_b, oab, d_oab, lse)


FF_SLAB = 2 * D_FF // N_DEV
FF_PAIRS = N_DEV // 2
ROWS16 = 16


def _taps(w, x, base, n):
    out = _shifted(x, base, n) * w[0:1]
    for t in range(1, FFN_CONV):
        out = out + _shifted(x, base + t, n) * w[t:t + 1]
    return out


def _ffn_fwd(h2, x1, w_up, conv_w, w_down, final_w, tgt, *, name, tm=512):
    S, D = h2.shape
    ni = S // tm
    per = tm // ROWS16

    def body(h_ref, hp_ref, x1_ref, wg_ref, wu_ref, cg_ref, cu_ref, wd_ref, fw_ref, t_ref,
             dx_ref, dxb_ref, dfw_ref, loss_ref, ug_ref, uu_ref, x2_ref):
        i, j = pl.program_id(0), pl.program_id(1)
        hv = jnp.concatenate([hp_ref[...], h_ref[...]], axis=0)
        row = lax.broadcasted_iota(jnp.int32, (tm + ROWS16, 1), 0)
        keep = (i > 0) | (row >= ROWS16)

        def branch(w_ref, c_ref, u_ref):
            u = lax.dot_general(hv, w_ref[...], (((1,), (1,)), ((), ())), preferred_element_type=F32).astype(BF16)
            u_ref[...] = u[ROWS16:]
            return _taps(c_ref[...], jnp.where(keep, u.astype(F32), 0.0), ROWS16 - (FFN_CONV - 1), tm)

        gate = branch(wg_ref, cg_ref, ug_ref)
        up = branch(wu_ref, cu_ref, uu_ref)
        act = (gate * _sigmoid(gate) * up).astype(BF16)
        part = jnp.dot(act, wd_ref[...], preferred_element_type=F32)

        @pl.when(j == 0)
        def _():
            x2_ref[...] = x1_ref[...] + part

        @pl.when((j > 0) & (j < FF_PAIRS - 1))
        def _():
            x2_ref[...] += part

        @pl.when(j == FF_PAIRS - 1)
        def _():
            xv = x2_ref[...] + part
            wv = fw_ref[...]
            r = lax.rsqrt(jnp.mean(xv * xv, axis=-1, keepdims=True) + EPS)
            err = xv * r * wv - t_ref[...]
            lsum = jnp.sum(jnp.sum(err * err, axis=-1, keepdims=True), axis=0, keepdims=True) * (0.5 / D)
            g = err * (1.0 / D)
            xh = xv * r
            gw = g * wv
            dx = r * (gw - xh * jnp.mean(gw * xh, axis=-1, keepdims=True))
            dx_ref[...] = dx
            dxb_ref[...] = dx.astype(BF16)
            dfw = jnp.sum(g * xh, axis=0, keepdims=True)
            lpart = jnp.broadcast_to(lsum, (1, 128))

            @pl.when(i == 0)
            def _():
                dfw_ref[...] = dfw
                loss_ref[...] = lpart

            @pl.when(i > 0)
            def _():
                dfw_ref[...] += dfw
                loss_ref[...] += lpart

    rows = pl.BlockSpec((tm, D), lambda i, j: (i, 0))
    slab = lambda off: pl.BlockSpec((None, FF_SLAB, D), lambda i, j: (j + off, 0, 0))
    cslab = lambda off: pl.BlockSpec((None, FFN_CONV, FF_SLAB), lambda i, j: (j + off, 0, 0))
    uspec = pl.BlockSpec((None, tm, FF_SLAB), lambda i, j: (j, i, 0))
    return pl.pallas_call(
        body, name=name, grid=(ni, FF_PAIRS),
        in_specs=[rows, pl.BlockSpec((ROWS16, D), lambda i, j: (jnp.maximum(i * per - 1, 0), 0)), rows,
                  slab(0), slab(FF_PAIRS), cslab(0), cslab(FF_PAIRS), pl.BlockSpec((FF_SLAB, D), lambda i, j: (j, 0)),
                  pl.BlockSpec((1, D), lambda i, j: (0, 0)), rows],
        out_specs=[rows, rows, pl.BlockSpec((1, D), lambda i, j: (0, 0)), pl.BlockSpec((1, 128), lambda i, j: (0, 0)), uspec, uspec],
        out_shape=[_sds((S, D)), _sds((S, D), BF16), _sds((1, D)), _sds((1, 128)),
                   _sds((FF_PAIRS, S, FF_SLAB), BF16), _sds((FF_PAIRS, S, FF_SLAB), BF16)],
        scratch_shapes=[pltpu.VMEM((tm, D), F32)],
        compiler_params=_params(("arbitrary", "arbitrary")),
    )(h2, h2, x1, w_up, w_up, conv_w, conv_w, w_down, final_w, tgt)


def _ffn_bwd(dx2, h2, ug, uu, conv_w, w_down, *, name, tm=512):
    S, D = h2.shape
    ni = S // tm
    per = tm // ROWS16
    ext = tm + ROWS16

    def body(dx_ref, dxn_ref, h_ref, ug_ref, ugp_ref, ugn_ref, uu_ref, uup_ref, uun_ref, cg_ref, cu_ref, wd_ref,
             du_ref, gd_ref, gup_ref, dcw_ref, acc_d, acc_g, acc_u, acc_cg, acc_cu):
        i = pl.program_id(1)

        @pl.when(i == 0)
        def _():
            acc_d[...] = jnp.zeros_like(acc_d)
            acc_g[...] = jnp.zeros_like(acc_g)
            acc_u[...] = jnp.zeros_like(acc_u)
            acc_cg[...] = jnp.zeros_like(acc_cg)
            acc_cu[...] = jnp.zeros_like(acc_cu)

        dx = dx_ref[...]
        dxe = jnp.concatenate([dx, dxn_ref[...]], axis=0)
        row = lax.broadcasted_iota(jnp.int32, (ext, 1), 0)
        live = (i < ni - 1) | (row < tm)
        d_act = jnp.where(live, lax.dot_general(dxe, wd_ref[...], (((1,), (1,)), ((), ())), preferred_element_type=F32), 0.0)
        rowp = lax.broadcasted_iota(jnp.int32, (ext + ROWS16, 1), 0)
        keep = (i > 0) | (rowp >= ROWS16)

        def pre(cur, prev, nxt):
            return jnp.where(keep, jnp.concatenate([prev[...], cur[...], nxt[...]], axis=0).astype(F32), 0.0)

        uge, uue = pre(ug_ref, ugp_ref, ugn_ref), pre(uu_ref, uup_ref, uun_ref)
        cg, cu = cg_ref[...], cu_ref[...]
        base = ROWS16 - (FFN_CONV - 1)
        gate = _taps(cg, uge, base, ext)
        up = _taps(cu, uue, base, ext)
        sg = _sigmoid(gate)
        silu = gate * sg
        dgc = d_act * up * _dsilu(gate, sg)
        duc = d_act * silu

        def conv_t(w, dc):
            out = _shifted(dc, FFN_CONV - 1, tm) * w[0:1]
            for t in range(1, FFN_CONV):
                out = out + _shifted(dc, FFN_CONV - 1 - t, tm) * w[t:t + 1]
            return out.astype(BF16)

        du_g, du_u = conv_t(cg, dgc), conv_t(cu, duc)
        du_ref[0] = du_g
        du_ref[1] = du_u
        dcw = lambda dc, xe: jnp.concatenate(
            [jnp.sum(dc[0:tm] * _shifted(xe, base + t, tm), axis=0, keepdims=True) for t in range(FFN_CONV)], axis=0)
        acc_cg[0:FFN_CONV, :] += dcw(dgc, uge)
        acc_cu[0:FFN_CONV, :] += dcw(duc, uue)
        tn = (((0,), (0,)), ((), ()))
        act = (silu[0:tm] * up[0:tm]).astype(BF16)
        acc_d[...] += lax.dot_general(act, dx, tn, preferred_element_type=F32)
        hv = h_ref[...]
        acc_g[...] += lax.dot_general(du_g, hv, tn, preferred_element_type=F32)
        acc_u[...] += lax.dot_general(du_u, hv, tn, preferred_element_type=F32)

        @pl.when(i == ni - 1)
        def _():
            gd_ref[...] = acc_d[...].astype(BF16)
            gup_ref[0] = acc_g[...].astype(BF16)
            gup_ref[1] = acc_u[...].astype(BF16)
            dcw_ref[0] = acc_cg[0:FFN_CONV, :]
            dcw_ref[1] = acc_cu[0:FFN_CONV, :]

    last16 = S // ROWS16 - 1
    rows = pl.BlockSpec((tm, D), lambda j, i: (i, 0))
    rows_next = pl.BlockSpec((ROWS16, D), lambda j, i: (jnp.minimum((i + 1) * per, last16), 0))
    u_cur = pl.BlockSpec((None, tm, FF_SLAB), lambda j, i: (j, i, 0))
    u_prev = pl.BlockSpec((None, ROWS16, FF_SLAB), lambda j, i: (j, jnp.maximum(i * per - 1, 0), 0))
    u_next = pl.BlockSpec((None, ROWS16, FF_SLAB), lambda j, i: (j, jnp.minimum((i + 1) * per, last16), 0))
    cslab = lambda off: pl.BlockSpec((None, FFN_CONV, FF_SLAB), lambda j, i: (j + off, 0, 0))
    return pl.pallas_call(
        body, name=name, grid=(FF_PAIRS, ni),
        in_specs=[rows, rows_next, rows, u_cur, u_prev, u_next, u_cur, u_prev, u_next, cslab(0), cslab(FF_PAIRS),
                  pl.BlockSpec((FF_SLAB, D), lambda j, i: (j, 0))],
        out_specs=[pl.BlockSpec((None, 2, tm, FF_SLAB), lambda j, i: (j, 0, i, 0)), pl.BlockSpec((FF_SLAB, D), lambda j, i: (j, 0)),
                   pl.BlockSpec((None, 2, FF_SLAB, D), lambda j, i: (j, 0, 0, 0)),
                   pl.BlockSpec((None, 2, FFN_CONV, FF_SLAB), lambda j, i: (j, 0, 0, 0))],
        out_shape=[_sds((FF_PAIRS, 2, S, FF_SLAB), BF16), _sds((D_FF, D), BF16), _sds((FF_PAIRS, 2, FF_SLAB, D), BF16),
                   _sds((FF_PAIRS, 2, FFN_CONV, FF_SLAB))],
        scratch_shapes=[pltpu.VMEM((FF_SLAB, D), F32), pltpu.VMEM((FF_SLAB, D), F32), pltpu.VMEM((FF_SLAB, D), F32),
                        pltpu.VMEM((8, FF_SLAB), F32), pltpu.VMEM((8, FF_SLAB), F32)],
        compiler_params=_params(("parallel", "arbitrary")),
    )(dx2, dx2, h2, ug, ug, ug, uu, uu, uu, conv_w, conv_w, w_down)


def _pair_slot(p):
    return 2 * (p & (FF_PAIRS - 1)) + (p >> 2)


def _mm_slabs(a, w, *, name, res=None, norm_bwd=None, after=(), tm=1024, tn=1024):
    nk, S, _ = a.shape
    D = w.shape[2]
    has_res = res is not None
    has_norm = norm_bwd is not None
    assert not has_norm or tn == D

    def body(*refs):
        a_ref, w_ref = refs[:2]
        r_ref = refs[2] if has_res else None
        if has_norm:
            x_ref, nw_ref, skip_ref = refs[2 + has_res:5 + has_res]
            o_ref, dw_ref, acc_ref = refs[-3:]
        else:
            o_ref, acc_ref = refs[-2:]
        i, k = pl.program_id(0), pl.program_id(2)
        part = jnp.dot(a_ref[...], w_ref[...], preferred_element_type=F32)

        @pl.when(k == 0)
        def _():
            acc_ref[...] = part

        @pl.when(k > 0)
        def _():
            acc_ref[...] += part

        @pl.when(k == nk - 1)
        def _():
            r = acc_ref[...] + r_ref[...] if has_res else acc_ref[...]
            if has_norm:
                dx, dw = _rms_bwd_rows(r, x_ref[...], nw_ref[...])
                o_ref[...] = skip_ref[...] + dx

                @pl.when(i == 0)
                def _():
                    dw_ref[...] = dw

                @pl.when(i > 0)
                def _():
                    dw_ref[...] += dw
            else:
                o_ref[...] = r

    o_spec = pl.BlockSpec((tm, tn), lambda i, j, k: (i, j))
    one = pl.BlockSpec((1, tn), lambda i, j, k: (0, 0))
    return pl.pallas_call(
        body, name=name, grid=(S // tm, D // tn, nk),
        in_specs=[pl.BlockSpec((None, tm, FF_SLAB), lambda i, j, k: (k, i, 0)),
                  pl.BlockSpec((None, FF_SLAB, tn), lambda i, j, k: (FF_PAIRS * (k & 1) + (k >> 1), 0, j))] + [o_spec] * has_res
        + ([o_spec, one, o_spec] if has_norm else []) + [ANY] * len(after),
        out_specs=[o_spec, one] if has_norm else o_spec, out_shape=[_sds((S, D)), _sds((1, D))] if has_norm else _sds((S, D)),
        scratch_shapes=[pltpu.VMEM((tm, tn), F32)],
        compiler_params=_params(("arbitrary" if has_norm else "parallel", "parallel", "arbitrary")),
    )(*((a, w) + ((res,) if has_res else ()) + (tuple(norm_bwd) if has_norm else ()) + tuple(after)))


def _local_step(x, tgt, norm1_w, w_land, conv_a, a_log, dt_bias, gnw, norm2_w, final_w, late_weights, emit, start_after=()):
    wgrad = functools.partial(_mm, ta=True, out_dtype=BF16)
    h1, proj_a, proj_z, proj_b = _in_proj(x, norm1_w, w_land, after=start_after, name="in_proj")
    qn, kn, v, gcb, bb = _gdn_prep_fwd(proj_a, conv_a, a_log, dt_bias, name="gdn_prep_fwd")
    uv, wk, at, tmat, wkb, qdb, keb = _gdn_chunk_fwd(qn, kn, v, gcb, bb, name="gdn_chunk_fwd")
    o, u, sp, oab = _gdn_scan_fwd(uv, at, wkb, qdb, keb, gcb, proj_z, gnw, name="gdn_scan_fwd")
    oab, lse = _attn_fwd(proj_b, oab, name="attn_fwd")
    w_out, w_up, conv_f, w_down = late_weights(oab)
    x1, h2 = _out_proj_norm(oab, w_out, x, norm2_w, name="out_proj")
    dx2, dx2_b, d_final, loss, ug, uu = _ffn_fwd(h2, x1, w_up, conv_f, w_down, final_w, tgt, name="ffn_fwd")
    du, g_down, g_up, dcw = _ffn_bwd(dx2_b, h2, ug, uu, conv_f, w_down, name="ffn_bwd")
    token = emit("ffn", w_down=g_down, w_up=g_up.reshape(N_DEV, FF_SLAB, -1), conv_f=dcw.reshape(N_DEV, FFN_CONV, -1))
    dx1, d_norm2 = _mm_slabs(du.reshape(N_DEV, -1, FF_SLAB), w_up, norm_bwd=(x1, norm2_w, dx2), after=token, name="ffn_up_dx")
    d_oab = _mm(dx1, w_out, tb=True, name="out_proj_dx", tn=D_MODEL, tk=1024)
    token = emit("out", w_out=wgrad(oab, dx1, name="out_proj_dw", tm=D_MODEL, tn=D_MODEL))
    dz, d_gnw, du, dwk, dat, dqd, dke, dgl = _gdn_scan_bwd(d_oab, o, proj_z, gnw, sp, u, at, wkb, qdb, keb, gcb, after=token, name="gdn_scan_bwd")
    dqn, dkn, dv, dg, dbeta = _gdn_chunk_bwd(qn, kn, gcb, bb, tmat, uv, wk, du, dwk, dat, dqd, dke, dgl, name="gdn_chunk_bwd")
    dc, dba, d_small = _gdn_prep_bwd(dqn, dkn, dv, dg, dbeta, proj_a, conv_a, a_log, dt_bias, name="gdn_prep_bwd")
    d_pa, d_conv_a = _gdn_conv_bwd(dc, dba, proj_a, conv_a, name="gdn_conv_bwd")
    d_pb = _attn_bwd(proj_b, oab, d_oab, lse, name="attn_bwd")
    g_a = wgrad(d_pa, h1, name="proj_a_dw", tm=A_COLS, tn=D_MODEL)
    g_z = wgrad(dz, h1, name="proj_z_dw", tn=D_MODEL)
    g_b = wgrad(d_pb, h1, name="proj_b_dw", tm=768, tn=D_MODEL)
    token = emit("in", w_a=g_a, w_z=g_z, w_b=g_b, conv_a=d_conv_a)
    grad_x, d_norm1 = _in_proj_dx((d_pa, dz, d_pb), w_land, x, norm1_w, dx1, after=token, name="in_proj_dx")
    small = dict(norm1=d_norm1, small=d_small, gnw=d_gnw, norm2=d_norm2, final=d_final)
    return loss, grad_x, small


_O1 = 3 * GDN_WIDTH
_O2 = _O1 + GDN_WIDTH
_O3 = _O2 + 2 * GDN_HEADS


_W_IN_ROWS = (A_COLS, GDN_WIDTH, 3 * DIL_WIDTH)
_IN_ROWS = (_O3 + 3 * DIL_WIDTH) // N_DEV
_ROW_TILE = 16
_IN_STEP = _IN_ROWS - _IN_ROWS % _ROW_TILE
_GAP = 8
_LAND_ROWS = 464
assert _O3 % _ROW_TILE == _ROW_TILE - _GAP and N_DEV - 1 + _GAP + _IN_ROWS <= _LAND_ROWS and _LAND_ROWS % _ROW_TILE == 0


def _padded_row(r):
    return r + (_GAP if r >= _O3 else 0)


def _shifted_slab(w_rows, j, *, name):
    q = w_rows.shape[0] // _IN_ROWS

    def body(j_ref, w_ref, o_ref, pad_ref):
        pad_ref[...] = jnp.zeros_like(pad_ref)
        for dev in range(N_DEV):
            @pl.when(j_ref[0] == dev)
            def _(dev=dev):
                p = lax.broadcasted_iota(jnp.int32, (_LAND_ROWS, 1), 0) + _IN_STEP * dev
                for s in range(q):
                    pad_ref[0:_IN_ROWS, :] = w_ref[pl.ds(s, _IN_ROWS, stride=q), :]
                    rows = pad_ref[...]
                    a = pltpu.roll(rows, dev, 0) if dev else rows
                    b = pltpu.roll(rows, dev + _GAP, 0)
                    o_ref[:, 128 * s:128 * (s + 1)] = jnp.where(p < _O3, a, jnp.where(p >= _O3 + _GAP, b, 0.0)).astype(BF16)

    vm = pl.BlockSpec(memory_space=pltpu.VMEM)
    return pl.pallas_call(
        body, name=name, in_specs=[pl.BlockSpec(memory_space=pltpu.SMEM), vm], out_specs=vm,
        out_shape=_sds((_LAND_ROWS, 128 * q), BF16), scratch_shapes=[pltpu.VMEM((_LAND_ROWS, 128), F32)],
    )(j, w_rows)


def _w_in_plan():
    def dest(p):
        if p < _O1:
            return 0, p
        if p < _O2:
            return 1, p - _O1
        if p < _O2 + _ROW_TILE:
            return 0, _O1
        q = p - _O3 - _GAP
        t, pair = divmod(q // 128, DIL_PAIRS)
        return 2, (3 * pair + t) * 128 + q % 128

    spans = [(_padded_row(_IN_ROWS * j), _padded_row(_IN_ROWS * (j + 1) - 1) + 1) for j in range(N_DEV)]
    runs, seams = [], []
    for p in range(0, spans[-1][1], _ROW_TILE):
        owners = [j for j, (lo, hi) in enumerate(spans) if lo < p + _ROW_TILE and hi > p]
        w, r = dest(p)
        if len(owners) == 2:
            seams.append((w, r, owners[0], p - _IN_STEP * owners[0], owners[1], p - _IN_STEP * owners[1]))
            continue
        (j,) = owners
        last = runs[-1] if runs else None
        if last and last[0] == j and last[2] == w and last[3] + last[4] == r and last[1] + last[4] == p - _IN_STEP * j:
            runs[-1] = last[:4] + (last[4] + _ROW_TILE,)
        else:
            runs.append((j, p - _IN_STEP * j, w, r, _ROW_TILE))
    return runs, seams


def _w_in_scratch(d):
    return [pltpu.VMEM((n, d), BF16) for n in _W_IN_ROWS] + [pltpu.VMEM((N_DEV - 1, _ROW_TILE, d), BF16),
                                                              pltpu.SemaphoreType.DMA(())]


def _fetch_w_in(land_ref, wa_ref, wz_ref, wb_ref, seam_ref, sem):
    w_refs = (wa_ref, wz_ref, wb_ref)
    runs, seams = _w_in_plan()
    copies = [pltpu.make_async_copy(land_ref.at[j, pl.ds(s, n)], w_refs[w].at[pl.ds(r, n)], sem) for j, s, w, r, n in runs]
    for k, (w, r, j0, s0, j1, s1) in enumerate(seams):
        copies.append(pltpu.make_async_copy(land_ref.at[j0, pl.ds(s0, _ROW_TILE)], w_refs[w].at[pl.ds(r, _ROW_TILE)], sem))
        copies.append(pltpu.make_async_copy(land_ref.at[j1, pl.ds(s1, _ROW_TILE)], seam_ref.at[k], sem))
    for cp in copies:
        cp.start()
    tail = _O1 + _ROW_TILE
    wa_ref[tail:, :] = jnp.zeros((A_COLS - tail, wa_ref.shape[1]), BF16)
    for cp in copies:
        cp.wait()
    for k, (w, r, *_) in enumerate(seams):
        both = w_refs[w][r:r + _ROW_TILE, :].astype(F32) + seam_ref[k].astype(F32)
        w_refs[w][r:r + _ROW_TILE, :] = both.astype(BF16)


MESH = pl.DeviceIdType.MESH
ANY = pl.BlockSpec(memory_space=pl.ANY)


def _position():
    return lax.axis_index("x"), lax.axis_index("y"), lax.axis_index("c")


def _slot(p):
    return 4 * p[0] + 2 * p[1] + p[2]


def _all_gather(blocks, *, name):
    n = len(blocks)

    def body(*refs):
        ins, outs = refs[:n], refs[n:2 * n]
        send_sems, recv_sems, local_sems = refs[2 * n:]
        x, y, c = _position()
        me, sibling = (x, y, c), (x, y, 1 - c)
        chips = [(1 - x, y), (x, 1 - y), (1 - x, 1 - y)]

        def copy(a, k, block, to, src=None):
            dst = outs[a].at[_slot(block)]
            return pltpu.make_async_remote_copy(
                src_ref=dst if src is None else src, dst_ref=dst, send_sem=send_sems.at[a, k], recv_sem=recv_sems.at[a, k],
                device_id=to, device_id_type=MESH)

        mine = [pltpu.make_async_copy(ins[a], outs[a].at[_slot(me)], local_sems.at[a]) for a in range(n)]
        for cp in mine:
            cp.start()
        first = []
        for a in range(n):
            first.append(copy(a, 0, me, sibling, src=ins[a]))
            first += [copy(a, 1 + j, me, (*chip, c), src=ins[a]) for j, chip in enumerate(chips)]
        for cp in first:
            cp.start()
        passed = []
        for j, chip in enumerate(chips):
            for a in range(n):
                copy(a, 1 + j, (*chip, c), me).wait_recv()
                fwd = copy(a, 4 + j, (*chip, c), sibling)
                fwd.start()
                passed.append(fwd)
        for a in range(n):
            copy(a, 0, sibling, me).wait_recv()
            for j, chip in enumerate(chips):
                copy(a, 4 + j, (*chip, 1 - c), me).wait_recv()
        for cp in first + passed:
            cp.wait_send()
        for cp in mine:
            cp.wait()

    return pl.pallas_call(
        body, name=name, in_specs=[ANY] * n, out_specs=[ANY] * n,
        out_shape=[_sds((N_DEV,) + b.shape, b.dtype) for b in blocks],
        scratch_shapes=[pltpu.SemaphoreType.DMA((n, 7)), pltpu.SemaphoreType.DMA((n, 7)), pltpu.SemaphoreType.DMA((n,))],
    )(*blocks)


def _gather_direct(block, *, name, after=()):
    def body(in_ref, *rest):
        out_ref, send_sems, recv_sems, local_sem = rest[len(after):]
        x, y, c = _position()
        me = _slot((x, y, c))
        mine = pltpu.make_async_copy(in_ref, out_ref.at[me], local_sem)
        mine.start()
        copies = [pltpu.make_async_remote_copy(
            src_ref=in_ref, dst_ref=out_ref.at[me], send_sem=send_sems.at[k - 1], recv_sem=recv_sems.at[k - 1],
            device_id=_peer_of(k, x, y, c), device_id_type=MESH) for k in range(1, N_DEV)]
        for cp in copies:
            cp.start()
        for cp in copies:
            cp.wait()
        mine.wait()

    return pl.pallas_call(
        body, name=name, in_specs=[pl.BlockSpec(memory_space=pltpu.VMEM)] + [ANY] * len(after),
        out_specs=pl.BlockSpec(memory_space=pltpu.VMEM),
        out_shape=_sds((N_DEV,) + block.shape, block.dtype),
        scratch_shapes=[pltpu.SemaphoreType.DMA((N_DEV - 1,)), pltpu.SemaphoreType.DMA((N_DEV - 1,)), pltpu.SemaphoreType.DMA],
    )(block, *after)


HBM = pl.BlockSpec(memory_space=pltpu.HBM)
SEM = pl.BlockSpec(memory_space=pltpu.SEMAPHORE)
EFFECT = pltpu.SideEffectType.DATAFLOW_SIDE_EFFECTING


def _peer_of(k, x, y, c):
    return (1 - x if k & 4 else x, 1 - y if k & 2 else y, 1 - c if k & 1 else c)


def _flight(a, k):
    return a * (N_DEV - 1) + k - 1


def _exchange_start(arrays, *, name, broadcast=False, paired=()):
    n = len(arrays)

    def body(*refs):
        ins, lands = refs[:n], refs[n:2 * n]
        send_sems, recv_sems = refs[2 * n:2 * n + 2]
        token = refs[-1]
        x, y, c = _position()
        me = _slot((x, y, c))
        for k in range(1, N_DEV):
            peer = _peer_of(k, x, y, c)
            for a in range(n):
                at = _pair_slot(_slot(peer)) if a in paired else _slot(peer)
                pltpu.make_async_remote_copy(
                    src_ref=ins[a] if broadcast else ins[a].at[at], dst_ref=lands[a].at[me],
                    send_sem=send_sems.at[_flight(a, k)], recv_sem=recv_sems.at[_flight(a, k)],
                    device_id=peer, device_id_type=MESH).start()
        token[...] = jnp.zeros_like(token)

    land_shapes = [((N_DEV,) + s.shape) if broadcast else s.shape for s in arrays]
    lands = [pltpu.with_memory_space_constraint(lax.empty(shp, s.dtype), pltpu.HBM) for shp, s in zip(land_shapes, arrays)]
    srcs = [pltpu.with_memory_space_constraint(s, pltpu.HBM) for s in arrays]
    outs = pl.pallas_call(
        body, name=name, in_specs=[HBM] * (2 * n),
        out_specs=[SEM, SEM] + [HBM] * (2 * n) + [pl.BlockSpec(memory_space=pltpu.VMEM)],
        out_shape=[pltpu.SemaphoreType.DMA((n * (N_DEV - 1),)), pltpu.SemaphoreType.DMA((n * (N_DEV - 1),))]
        + [pltpu.HBM(s.shape, s.dtype) for s in arrays] + [pltpu.HBM(shp, s.dtype) for shp, s in zip(land_shapes, arrays)]
        + [_sds((8, 128))],
        input_output_aliases={i: 2 + i for i in range(2 * n)},
        compiler_params=pltpu.CompilerParams(has_side_effects=EFFECT),
    )(*srcs, *lands)
    return outs[0], outs[1], outs[2:2 + n], outs[2 + n:2 + 2 * n], outs[-1]


def _exchange_wait(send_sems, recv_sems, srcs, lands, after, *, name, broadcast=False):
    n = len(srcs)

    def body(*refs):
        ins, lnd = refs[:n], refs[n:2 * n]
        send_ref, recv_ref = refs[2 * n:2 * n + 2]
        x, y, c = _position()
        for k in range(1, N_DEV):
            for a in range(n):
                cp = pltpu.make_async_remote_copy(
                    src_ref=ins[a] if broadcast else ins[a].at[0], dst_ref=lnd[a].at[0], send_sem=send_ref.at[_flight(a, k)],
                    recv_sem=recv_ref.at[_flight(a, k)], device_id=_peer_of(k, x, y, c), device_id_type=MESH)
                cp.wait_send()
                cp.wait_recv()

    outs = pl.pallas_call(
        body, name=name, in_specs=[HBM] * (2 * n) + [SEM, SEM, ANY], out_specs=[HBM] * (2 * n),
        out_shape=[pltpu.HBM(s.shape, s.dtype) for s in srcs] + [pltpu.HBM(s.shape, s.dtype) for s in lands],
        input_output_aliases={i: i for i in range(2 * n)},
        compiler_params=pltpu.CompilerParams(has_side_effects=EFFECT),
    )(*srcs, *lands, send_sems, recv_sems, after)
    return outs[:n], outs[n:]


_G_LAND_ROWS = 528


def _g_in_pieces(dev):
    runs, seams = _w_in_plan()
    pieces = [(w, r, n, s) for j, s, w, r, n in runs if j == dev]
    pieces += [(w, r, _ROW_TILE, s0) for w, r, j0, s0, j1, s1 in seams if j0 == dev]
    pieces += [(w, r, _ROW_TILE, s1) for w, r, j0, s0, j1, s1 in seams if j1 == dev]
    return pieces


def _coords(dev):
    return tuple(jnp.int32(v) for v in (dev >> 2, (dev >> 1) & 1, dev & 1))


def _exchange_start_in(g_ws, conv_slabs, *, name):
    srcs = list(g_ws) + [conv_slabs]
    n = len(srcs)

    def body(*refs):
        g_refs, cv_ref, land, land_cv = refs[:n - 1], refs[n - 1], refs[n], refs[n + 1]
        send_sems, recv_sems = refs[n + 2:n + 4]
        token, own_sem = refs[-2:]
        me = _slot(_position())

        def own(dev):
            return [pltpu.make_async_copy(g_refs[w].at[pl.ds(r, rows)], land.at[dev, pl.ds(s, rows)], own_sem)
                    for w, r, rows, s in _g_in_pieces(dev)]

        for dev in range(N_DEV):
            @pl.when(me == dev)
            def _(dev=dev):
                for cp in own(dev):
                    cp.start()

        for dev in range(N_DEV):
            @pl.when(me != dev)
            def _(dev=dev):
                k = me ^ dev
                for w, r, rows, s in _g_in_pieces(dev):
                    pltpu.make_async_remote_copy(
                        src_ref=g_refs[w].at[pl.ds(r, rows)], dst_ref=land.at[me, pl.ds(s, rows)],
                        send_sem=send_sems.at[_flight(0, k)], recv_sem=recv_sems.at[_flight(0, k)],
                        device_id=_coords(dev), device_id_type=MESH).start()
                pltpu.make_async_remote_copy(
                    src_ref=cv_ref.at[dev], dst_ref=land_cv.at[me], send_sem=send_sems.at[_flight(1, k)],
                    recv_sem=recv_sems.at[_flight(1, k)], device_id=_coords(dev), device_id_type=MESH).start()

        for dev in range(N_DEV):
            @pl.when(me == dev)
            def _(dev=dev):
                for cp in own(dev):
                    cp.wait()

        token[...] = jnp.zeros_like(token)

    lands = [lax.empty((N_DEV, _G_LAND_ROWS, g_ws[0].shape[1]), g_ws[0].dtype), lax.empty(conv_slabs.shape, conv_slabs.dtype)]
    ops = [pltpu.with_memory_space_constraint(a, pltpu.HBM) for a in srcs + lands]
    outs = pl.pallas_call(
        body, name=name, in_specs=[HBM] * len(ops),
        out_specs=[SEM, SEM] + [HBM] * len(ops) + [pl.BlockSpec(memory_space=pltpu.VMEM)],
        out_shape=[pltpu.SemaphoreType.DMA((2 * (N_DEV - 1),)), pltpu.SemaphoreType.DMA((2 * (N_DEV - 1),))]
        + [pltpu.HBM(a.shape, a.dtype) for a in ops] + [_sds((8, 128))],
        input_output_aliases={i: 2 + i for i in range(len(ops))},
        scratch_shapes=[pltpu.SemaphoreType.DMA(())],
        compiler_params=pltpu.CompilerParams(has_side_effects=EFFECT),
    )(*ops)
    return outs[0], outs[1], outs[2:2 + n], outs[2 + n:4 + n], outs[-1]


def _exchange_wait_in(send_sems, recv_sems, srcs, lands, after, *, name):
    n = len(srcs)

    def body(*refs):
        g_refs, cv_ref, land, land_cv = refs[:n - 1], refs[n - 1], refs[n], refs[n + 1]
        send_ref, recv_ref = refs[n + 2:n + 4]
        me = _slot(_position())

        def copies(dev, k):
            cps = [pltpu.make_async_remote_copy(
                src_ref=g_refs[w].at[pl.ds(r, rows)], dst_ref=land.at[0, pl.ds(s, rows)], send_sem=send_ref.at[_flight(0, k)],
                recv_sem=recv_ref.at[_flight(0, k)], device_id=_coords(dev), device_id_type=MESH)
                for w, r, rows, s in _g_in_pieces(dev)]
            return cps + [pltpu.make_async_remote_copy(
                src_ref=cv_ref.at[0], dst_ref=land_cv.at[0], send_sem=send_ref.at[_flight(1, k)],
                recv_sem=recv_ref.at[_flight(1, k)], device_id=_coords(dev), device_id_type=MESH)]

        for dev in range(N_DEV):
            @pl.when(me != dev)
            def _(dev=dev):
                for cp in copies(dev, me ^ dev):
                    cp.wait_send()

            @pl.when(me == dev)
            def _(dev=dev):
                for k in range(1, N_DEV):
                    for cp in copies(dev, k):
                        cp.wait_recv()

    ops = list(srcs) + list(lands)
    outs = pl.pallas_call(
        body, name=name, in_specs=[HBM] * len(ops) + [SEM, SEM, ANY], out_specs=[HBM] * len(ops),
        out_shape=[pltpu.HBM(a.shape, a.dtype) for a in ops],
        input_output_aliases={i: i for i in range(len(ops))},
        compiler_params=pltpu.CompilerParams(has_side_effects=EFFECT),
    )(*ops, send_sems, recv_sems, after)
    return outs[:n], outs[n:]


def _with_own(landed, srcs, me):
    return [lax.dynamic_update_index_in_dim(l, o, me, 0) for l, o in zip(landed, srcs)]


def _adam_update(g, w, m, v):
    c1 = 1.0 - ADAM_B1 ** ADAM_STEP
    c2 = 1.0 - ADAM_B2 ** ADAM_STEP
    nm = ADAM_B1 * m + (1.0 - ADAM_B1) * g
    nv = ADAM_B2 * v + (1.0 - ADAM_B2) * (g * g)
    return -ADAM_LR * ((nm / c1) / (jnp.sqrt(nv / c2) + ADAM_EPS) + ADAM_WD * w), nm, nv


def _adamw(landed, sent, me, w, m, v, *, name, tr=None, tc=None):
    R, C = w.shape
    tr = R if tr is None else tr
    tc = C if tc is None else tc
    assert R % tr == 0 and C % tc == 0

    def body(me_ref, own_ref, p_ref, w_ref, m_ref, v_ref, g_ref, d_ref, nm_ref, nv_ref, token_ref):
        token_ref[...] = jnp.zeros_like(token_ref)
        g = own_ref[...].astype(F32)
        for s in range(N_DEV):
            g = g + jnp.where(me_ref[1] == s, 0.0, p_ref[s].astype(F32))
        delta, nm, nv = _adam_update(g, w_ref[...], m_ref[...], v_ref[...])
        g_ref[...] = g
        nm_ref[...] = nm
        nv_ref[...] = nv
        d_ref[...] = delta

    blk = pl.BlockSpec((tr, tc), lambda i, j, me_ref: (i, j))
    return pl.pallas_call(
        body, name=name,
        grid_spec=pltpu.PrefetchScalarGridSpec(
            num_scalar_prefetch=1, grid=(R // tr, C // tc),
            in_specs=[pl.BlockSpec((None, tr, tc), lambda i, j, me_ref: (me_ref[0], i, j)),
                      pl.BlockSpec((N_DEV, tr, tc), lambda i, j, me_ref: (0, i, j)), blk, blk, blk],
            out_specs=[blk] * 4 + [pl.BlockSpec((8, 128), lambda i, j, me_ref: (0, 0))]),
        out_shape=[_sds((R, C))] * 4 + [_sds((8, 128))],
        compiler_params=_params(("arbitrary", "arbitrary")),
    )(me, sent, landed, w, m, v)


def _adamw_rowwise(landed, me, w, m, v, *, name, tr=128):
    C = landed.shape[2]
    R, q, extra = _IN_ROWS, C // 128, _ROW_TILE
    assert tr % extra == 0 and (pl.cdiv(R, tr) * tr + extra) <= landed.shape[1] and N_DEV - 1 + _GAP < extra

    def body(me_ref, a_ref, b_ref, w_ref, m_ref, v_ref, g_ref, d_ref, nm_ref, nv_ref, g_scr):
        i = pl.program_id(0)
        total = lambda ref: functools.reduce(lambda x, y: x + y, [ref[s].astype(F32) for s in range(N_DEV)])
        slab = jnp.concatenate([total(a_ref), total(b_ref)], axis=0)
        for dev in range(N_DEV):
            @pl.when(me_ref[0] == dev)
            def _(dev=dev):
                lo, hi = slab[dev:dev + tr], slab[dev + _GAP:dev + _GAP + tr]
                if _IN_ROWS * (dev + 1) <= _O3:
                    g_scr[...] = lo
                elif _IN_ROWS * dev >= _O3:
                    g_scr[...] = hi
                else:
                    r = _IN_ROWS * dev + tr * i + lax.broadcasted_iota(jnp.int32, (tr, 1), 0)
                    g_scr[...] = jnp.where(r < _O3, lo, hi)
        g = g_scr[...]
        for s in range(q):
            rows = pl.ds(s, tr, stride=q)
            gs = g[:, 128 * s:128 * (s + 1)]
            delta, nm, nv = _adam_update(gs, w_ref[rows, :], m_ref[rows, :], v_ref[rows, :])
            g_ref[rows, :] = gs
            nm_ref[rows, :] = nm
            nv_ref[rows, :] = nv
            d_ref[rows, :] = delta

    blk = pl.BlockSpec((tr * q, 128), lambda i, me_ref: (i, 0))
    return pl.pallas_call(
        body, name=name,
        grid_spec=pltpu.PrefetchScalarGridSpec(
            num_scalar_prefetch=1, grid=(pl.cdiv(R, tr),),
            in_specs=[pl.BlockSpec((N_DEV, tr, C), lambda i, me_ref: (0, i, 0)),
                      pl.BlockSpec((N_DEV, extra, C), lambda i, me_ref: (0, (tr // extra) * (i + 1), 0)), blk, blk, blk],
            out_specs=[blk] * 4, scratch_shapes=[pltpu.VMEM((tr, C), F32)]),
        out_shape=[_sds((R * q, 128))] * 4,
        compiler_params=_params(("arbitrary",)),
    )(me, landed, landed, w, m, v)


_SMALL_ROWS = 8
_SMALL_SLOTS = ((0, 0, D_MODEL), (1, 0, D_MODEL), (2, 0, D_MODEL), (3, 0, GDN_DIM), (3, GDN_DIM, GDN_HEADS),
                (3, GDN_DIM + GDN_HEADS, GDN_HEADS))
_LOSS_LANE = 2 * GDN_DIM


def _pack_small(norm1, norm2, final, gnw, a_log, dt_bias, loss):
    row3 = jnp.concatenate([gnw, a_log, dt_bias, jnp.zeros((1, 128 - 2 * GDN_HEADS), F32), loss,
                            jnp.zeros((1, D_MODEL - 3 * 128), F32)], axis=1)
    return jnp.concatenate([norm1, norm2, final, row3, jnp.zeros((_SMALL_ROWS - 4, D_MODEL), F32)], axis=0)


def _adamw_small(packs, ws, ms, vs, *, name):
    n = len(ws)

    def body(p_ref, *refs):
        w_refs, m_refs, v_refs = refs[:n], refs[n:2 * n], refs[2 * n:3 * n]
        outs = refs[3 * n:]
        g_all = p_ref[0]
        for s in range(1, N_DEV):
            g_all = g_all + p_ref[s]
        for i, (row, lane, width) in enumerate(_SMALL_SLOTS):
            g = g_all[row:row + 1, lane:lane + width]
            delta, nm, nv = _adam_update(g, w_refs[i][...], m_refs[i][...], v_refs[i][...])
            for o_ref, val in zip(outs[4 * i:4 * i + 4], (g, delta, nm, nv)):
                o_ref[...] = val
        outs[-1][...] = g_all[3:4, _LOSS_LANE:_LOSS_LANE + 128]

    vm = pl.BlockSpec(memory_space=pltpu.VMEM)
    outs = pl.pallas_call(
        body, name=name, in_specs=[vm] * (1 + 3 * n), out_specs=[vm] * (4 * n + 1),
        out_shape=[_sds(w.shape) for w in ws for _ in range(4)] + [_sds((1, 128))],
    )(packs, *ws, *ms, *vs)
    return [outs[4 * i:4 * i + 4] for i in range(n)], outs[-1]


def _slabs_by_cols(g):
    r = g.shape[0]
    return g.reshape(r, N_DEV, -1).transpose(1, 0, 2)


def _cols_from_slabs(s):
    return s.transpose(1, 0, 2).reshape(s.shape[1], -1)


def kernel(x, norm1_w, w_in, conv_qkv_w, a_log, dt_bias, gdn_norm_w, w_out, norm2_w, w_up, ffn_conv_w, w_down, final_norm_w, loss_target, m_norm1_w, m_w_in, m_conv_qkv_w, m_a_log, m_dt_bias, m_gdn_norm_w, m_w_out, m_norm2_w, m_w_up, m_ffn_conv_w, m_w_down, m_final_norm_w, v_norm1_w, v_w_in, v_conv_qkv_w, v_a_log, v_dt_bias, v_gdn_norm_w, v_w_out, v_norm2_w, v_w_up, v_ffn_conv_w, v_w_down, v_final_norm_w):
    bf = lambda a: a.astype(BF16)
    me = _slot(_position())
    me1 = jnp.reshape(me, (1,)).astype(jnp.int32)
    t_in = lambda a: a[0].T
    rows = lambda a: a.reshape(D_MODEL // 128, 128, -1).transpose(2, 0, 1).reshape(-1, 128)
    gw_in, g_conv_a = _all_gather([_shifted_slab(rows(w_in), me1, name="shift_w_in"), conv_qkv_w[0]], name="gather_w_in")
    late_src, _ = lax.optimization_barrier(([bf(w_out[0]), bf(t_in(w_up)), bf(w_down[0]), ffn_conv_w[0]], gw_in))
    l_send, l_recv, l_srcs, l_lands, l_token = _exchange_start(late_src, name="weights_start", broadcast=True)

    def late_weights(after):
        srcs, landed = _exchange_wait(l_send, l_recv, l_srcs, l_lands, after, name="weights_wait", broadcast=True)
        gw_out, gw_up, gw_down, g_conv_f = _with_own(landed, srcs, me)
        return gw_out.reshape(D_MODEL, D_MODEL), gw_up, g_conv_f, gw_down.reshape(D_FF, D_MODEL)

    flights = {}

    def emit(group, **grads):
        paired = ()
        if group == "in":
            *flight, token = _exchange_start_in([grads["w_a"], grads["w_z"], grads["w_b"]], _slabs_by_cols(grads["conv_a"]),
                                                name="grads_start_in")
            flights[group] = flight
            return (token,)
        if group == "ffn":
            slabs = dict(w_down=grads["w_down"].reshape(N_DEV, -1, D_MODEL), w_up=grads["w_up"], conv_f=grads["conv_f"])
            paired = (1, 2)
        else:
            slabs = {k: v.reshape(N_DEV, -1, D_MODEL) for k, v in grads.items()}
        names = list(slabs)
        *flight, token = _exchange_start([slabs[k] for k in names], paired=paired, name="grads_start_" + group)
        flights[group] = (names, flight)
        return (token,)

    loss, grad_x, g = _local_step(
        x[0], loss_target[0], norm1_w, gw_in, _cols_from_slabs(g_conv_a), a_log, dt_bias,
        gdn_norm_w, norm2_w, final_norm_w[None], late_weights, emit, start_after=(l_token,))
    got = {}

    def collect(group, after):
        names, (send_sems, recv_sems, srcs, lands) = flights[group]
        srcs, landed = _exchange_wait(send_sems, recv_sems, srcs, lands, after, name="grads_wait_" + group)
        got.update(zip(names, zip(landed, srcs)))

    def update(key, w, m, v, paired=False, **tiles):
        where = jnp.concatenate([_pair_slot(me1) if paired else me1, me1])
        return _adamw(*got[key], where, w, m, v, name="adamw_" + key, **tiles)

    collect("ffn", grad_x)
    collect("out", grad_x)
    *o_out, t1 = update("w_out", w_out[0], m_w_out[0], v_w_out[0])
    *o_up, t2 = update("w_up", t_in(w_up), t_in(m_w_up), t_in(v_w_up), paired=True, tr=176)
    o_up = [o.T for o in o_up]
    *o_down, t3 = update("w_down", w_down[0], m_w_down[0], v_w_down[0], tr=176)
    *o_cf, t4 = update("conv_f", ffn_conv_w[0], m_ffn_conv_w[0], v_ffn_conv_w[0], paired=True)
    pack = _pack_small(g["norm1"], g["norm2"], g["final"], g["gnw"], g["small"][:, 0:GDN_HEADS],
                       g["small"][:, GDN_HEADS:2 * GDN_HEADS], loss)
    small_all = _gather_direct(pack, after=(t1, t2, t3, t4), name="gather_small")
    send_sems, recv_sems, srcs, lands = flights["in"]
    srcs, (g_land, conv_land) = _exchange_wait_in(send_sems, recv_sems, srcs, lands, small_all, name="grads_wait_in")
    got["conv_a"] = (conv_land, srcs[-1])
    o_in = [o.reshape(-1, D_MODEL // 128, 128).transpose(1, 2, 0).reshape(D_MODEL, -1) for o in _adamw_rowwise(
        g_land, me1, rows(w_in), rows(m_w_in), rows(v_w_in), name="adamw_w_in")]
    o_ca = update("conv_a", conv_qkv_w[0], m_conv_qkv_w[0], v_conv_qkv_w[0])
    (o_n1, o_n2, o_fin, o_gn, o_al, o_dt), total = _adamw_small(
        small_all, (norm1_w, norm2_w, final_norm_w[None], gdn_norm_w, a_log, dt_bias),
        (m_norm1_w, m_norm2_w, m_final_norm_w[None], m_gdn_norm_w, m_a_log, m_dt_bias),
        (v_norm1_w, v_norm2_w, v_final_norm_w[None], v_gdn_norm_w, v_a_log, v_dt_bias), name="adamw_small")
    outs = [total[0, 0], grad_x[None]]
    for k in range(4):
        outs += [o_n1[k], o_in[k][None], o_ca[k][None], o_al[k], o_dt[k], o_gn[k], o_out[k][None], o_n2[k], o_up[k][None],
                 o_cf[k][None], o_down[k][None], o_fin[k][0]]
    return tuple(outs)
```

```python
import functools

import jax
import jax.numpy as jnp
from jax import lax
from jax.experimental import pallas as pl
from jax.experimental.pallas import tpu as pltpu

F32 = jnp.float32
BF16 = jnp.bfloat16

N_DEV = 8
D_MODEL = 1024
GDN_HEADS = 4
GDN_DIM = 128
GDN_WIDTH = GDN_HEADS * GDN_DIM
GDN_CONV = 4
CHUNK = 64
CHUNKS_PER_STEP = 4
DIL_HEADS = 8
DIL_DIM = 64
DIL_WIDTH = DIL_HEADS * DIL_DIM
DIL_PAIRS = DIL_HEADS // 2
DILATIONS = (1, 4, 16)
BAND = 128
D_FF = 2816
FFN_CONV = 3
EPS = 1e-6
A_COLS = 3 * GDN_WIDTH + 128
HALO = 8

ADAM_LR = 0.001
ADAM_B1 = 0.9
ADAM_B2 = 0.999
ADAM_EPS = 1e-08
ADAM_WD = 0.01
ADAM_STEP = 10

VMEM_LIMIT_BYTES = 56 * 1024 * 1024
NEG_BIG = -1e30


def _params(sem=None):
    return pltpu.CompilerParams(dimension_semantics=sem, vmem_limit_bytes=VMEM_LIMIT_BYTES)


def _sds(shape, dtype=F32):
    return jax.ShapeDtypeStruct(shape, dtype)


def _bdot(a, b):
    return jnp.dot(a.astype(BF16), b.astype(BF16), preferred_element_type=F32)


def _bdot_nt(a, b):
    return lax.dot_general(a.astype(BF16), b.astype(BF16), (((1,), (1,)), ((), ())), preferred_element_type=F32)


def _bdot_tn(a, b):
    return lax.dot_general(a.astype(BF16), b.astype(BF16), (((0,), (0,)), ((), ())), preferred_element_type=F32)


def _split(a):
    hi = a.astype(BF16)
    lo = (a - hi.astype(F32)).astype(BF16)
    return hi, lo


def _dot3(a, b, dims):
    ah, al = _split(a)
    bh, bl = _split(b)
    d = functools.partial(lax.dot_general, dimension_numbers=(dims, ((), ())), preferred_element_type=F32)
    return d(ah, bh) + (d(al, bh) + d(ah, bl))


def _exact_tri_dot(tri, g):
    g1 = g.astype(BF16)
    r1 = g - g1.astype(F32)
    g2 = r1.astype(BF16)
    g3 = (r1 - g2.astype(F32)).astype(BF16)
    t = tri.astype(BF16)
    d = functools.partial(jnp.dot, preferred_element_type=F32)
    return d(t, g1) + (d(t, g2) + d(t, g3))


def _sigmoid(x):
    return 1.0 / (1.0 + jnp.exp(-x))


def _dsilu(x, sg):
    return sg * (1.0 + x * (1.0 - sg))


def _rms_bwd_rows(dh, x, w):
    r = lax.rsqrt(jnp.mean(x * x, axis=-1, keepdims=True) + EPS)
    xh = x * r
    gw = dh * w
    return r * (gw - xh * jnp.mean(gw * xh, axis=-1, keepdims=True)), jnp.sum(dh * xh, axis=0, keepdims=True)


def _mm(a, b, *, name, ta=False, tb=False, res=None, norm_bwd=None, after=(), out_dtype=F32, tm=512, tn=512, tk=512):
    if ta:
        K, M = a.shape
    else:
        M, K = a.shape
    if tb:
        N, Kb = b.shape
    else:
        Kb, N = b.shape
    assert K == Kb, (a.shape, b.shape)
    tm, tn, tk = min(tm, M), min(tn, N), min(tk, K)
    assert M % tm == 0 and N % tn == 0 and K % tk == 0, (name, M, N, K, tm, tn, tk)
    nk = K // tk
    dims = (((0 if ta else 1,), (1 if tb else 0,)), ((), ()))
    has_res = res is not None
    has_norm = norm_bwd is not None
    assert not has_norm or tn == N

    def body(*refs):
        a_ref, b_ref = refs[:2]
        r_ref = refs[2] if has_res else None
        if has_norm:
            x_ref, w_ref, skip_ref = refs[2 + has_res:5 + has_res]
            o_ref, dw_ref, acc_ref = refs[-3:]
        else:
            o_ref, acc_ref = refs[-2:]
        i, k = pl.program_id(0), pl.program_id(2)
        part = lax.dot_general(a_ref[...].astype(BF16), b_ref[...].astype(BF16), dims, preferred_element_type=F32)

        @pl.when(k == 0)
        def _():
            acc_ref[...] = part

        @pl.when(k > 0)
        def _():
            acc_ref[...] += part

        @pl.when(k == nk - 1)
        def _():
            r = acc_ref[...]
            if has_res:
                r = r + r_ref[...]
            if has_norm:
                dx, dw = _rms_bwd_rows(r, x_ref[...], w_ref[...])
                o_ref[...] = skip_ref[...] + dx

                @pl.when(i == 0)
                def _():
                    dw_ref[...] = dw

                @pl.when(i > 0)
                def _():
                    dw_ref[...] += dw
            else:
                o_ref[...] = r.astype(out_dtype)

    a_spec = pl.BlockSpec((tk, tm), lambda i, j, k: (k, i)) if ta else pl.BlockSpec((tm, tk), lambda i, j, k: (i, k))
    b_spec = pl.BlockSpec((tn, tk), lambda i, j, k: (j, k)) if tb else pl.BlockSpec((tk, tn), lambda i, j, k: (k, j))
    o_spec = pl.BlockSpec((tm, tn), lambda i, j, k: (i, j))
    one = pl.BlockSpec((1, tn), lambda i, j, k: (0, 0))
    in_specs = [a_spec, b_spec] + [o_spec] * has_res + ([o_spec, one, o_spec] if has_norm else []) + [ANY] * len(after)
    args = (a, b) + ((res,) if has_res else ()) + (tuple(norm_bwd) if has_norm else ()) + tuple(after)
    return pl.pallas_call(
        body, name=name, grid=(M // tm, N // tn, nk), in_specs=in_specs,
        out_specs=[o_spec, one] if has_norm else o_spec,
        out_shape=[_sds((M, N)), _sds((1, N))] if has_norm else _sds((M, N), out_dtype),
        scratch_shapes=[pltpu.VMEM((tm, tn), F32)],
        compiler_params=_params(("arbitrary" if has_norm else "parallel", "parallel", "arbitrary")),
    )(*args)


def _in_proj(x, norm_w, w_land, *, name, after=(), tm=512):
    S, D = x.shape

    def body(x_ref, nw_ref, land_ref, *rest):
        h_ref, pa_ref, pz_ref, pb_ref, *scratch = rest[len(after):]

        @pl.when(pl.program_id(0) == 0)
        def _():
            _fetch_w_in(land_ref, *scratch)

        xv = x_ref[...]
        r = lax.rsqrt(jnp.mean(xv * xv, axis=-1, keepdims=True) + EPS)
        h = (xv * r * nw_ref[...]).astype(BF16)
        h_ref[...] = h
        for w_ref, p_ref in zip(scratch[:3], (pa_ref, pz_ref, pb_ref)):
            p_ref[...] = lax.dot_general(h, w_ref[...], (((1,), (1,)), ((), ())), preferred_element_type=F32)

    row = lambda n: pl.BlockSpec((tm, n), lambda i: (i, 0))
    full = lambda a: pl.BlockSpec(a.shape, lambda i: (0, 0))
    return pl.pallas_call(
        body, name=name, grid=(S // tm,), in_specs=[row(D), full(norm_w), ANY] + [ANY] * len(after),
        out_specs=[row(D)] + [row(n) for n in _W_IN_ROWS],
        out_shape=[_sds((S, D), BF16)] + [_sds((S, n)) for n in _W_IN_ROWS],
        scratch_shapes=_w_in_scratch(D), compiler_params=_params(("arbitrary",)),
    )(x, norm_w, w_land, *after)


def _in_proj_dx(ds, w_land, x, norm_w, skip, *, name, after=(), tm=512):
    S, D = x.shape
    n = len(ds)

    def body(*refs):
        d_refs, land_ref = refs[:n], refs[n]
        x_ref, nw_ref, skip_ref = refs[n + 1:n + 4]
        o_ref, dw_ref, *scratch = refs[n + 4 + len(after):]
        w_refs = scratch[:n]
        i = pl.program_id(0)

        @pl.when(i == 0)
        def _():
            _fetch_w_in(land_ref, *scratch)

        dh = jnp.dot(d_refs[0][...], w_refs[0][...], preferred_element_type=F32)
        for d_ref, w_ref in zip(d_refs[1:], w_refs[1:]):
            dh = dh + jnp.dot(d_ref[...], w_ref[...], preferred_element_type=F32)
        dx, dw = _rms_bwd_rows(dh, x_ref[...], nw_ref[...])
        o_ref[...] = skip_ref[...] + dx

        @pl.when(i == 0)
        def _():
            dw_ref[...] = dw

        @pl.when(i > 0)
        def _():
            dw_ref[...] += dw

    row = lambda c: pl.BlockSpec((tm, c), lambda i: (i, 0))
    full = lambda a: pl.BlockSpec(a.shape, lambda i: (0, 0))
    return pl.pallas_call(
        body, name=name, grid=(S // tm,),
        in_specs=[row(d.shape[1]) for d in ds] + [ANY, row(D), full(norm_w), row(D)] + [ANY] * len(after),
        out_specs=[row(D), pl.BlockSpec((1, D), lambda i: (0, 0))], out_shape=[_sds((S, D)), _sds((1, D))],
        scratch_shapes=_w_in_scratch(D), compiler_params=_params(("arbitrary",)),
    )(*ds, w_land, x, norm_w, skip, *after)


def _out_proj_norm(a, w, x, norm_w, *, name, tm=512):
    S, D = x.shape

    def body(a_ref, w_ref, x_ref, nw_ref, x1_ref, h_ref):
        x1 = x_ref[...] + jnp.dot(a_ref[...], w_ref[...], preferred_element_type=F32)
        x1_ref[...] = x1
        r = lax.rsqrt(jnp.mean(x1 * x1, axis=-1, keepdims=True) + EPS)
        h_ref[...] = (x1 * r * nw_ref[...]).astype(BF16)

    row = pl.BlockSpec((tm, D), lambda i: (i, 0))
    return pl.pallas_call(
        body, name=name, grid=(S // tm,),
        in_specs=[pl.BlockSpec((tm, a.shape[1]), lambda i: (i, 0)), pl.BlockSpec(w.shape, lambda i: (0, 0)), row,
                  pl.BlockSpec((1, D), lambda i: (0, 0))],
        out_specs=[row, row], out_shape=[_sds((S, D)), _sds((S, D), BF16)], compiler_params=_params(("parallel",)),
    )(a, w, x, norm_w)


def _shifted(x, start, n):
    aligned = -(-start // HALO) * HALO
    assert aligned + n <= x.shape[0], (start, n, x.shape)
    return (x if aligned == start else pltpu.roll(x, aligned - start, axis=0))[aligned:aligned + n]


def _conv_rows(prev, cur, w, taps):
    n = cur.shape[0]
    xs = jnp.concatenate([prev, cur], axis=0)
    base = HALO - (taps - 1)
    out = _shifted(xs, base, n) * w[0:1]
    for i in range(1, taps):
        out = out + _shifted(xs, base + i, n) * w[i:i + 1]
    return out


def _conv_rows_bwd(cur_d, next_d, prev_x, cur_x, w, taps):
    n = cur_d.shape[0]
    ds = jnp.concatenate([cur_d, next_d], axis=0)
    dx = _shifted(ds, taps - 1, n) * w[0:1]
    for i in range(1, taps):
        dx = dx + _shifted(ds, taps - 1 - i, n) * w[i:i + 1]
    xs = jnp.concatenate([prev_x, cur_x], axis=0)
    base = HALO - (taps - 1)
    dws = [jnp.sum(cur_d * _shifted(xs, base + i, n), axis=0, keepdims=True) for i in range(taps)]
    return dx, jnp.concatenate(dws, axis=0)


def _halo_specs(tm, width, col, nblk):
    per = tm // HALO
    prev = pl.BlockSpec((HALO, width), lambda i, *_: (jnp.maximum(i * per - 1, 0), col))
    nxt = pl.BlockSpec((HALO, width), lambda i, *_: (jnp.minimum((i + 1) * per, nblk * per - 1), col))
    return prev, nxt


def _softplus(x):
    return jnp.maximum(x, 0.0) + jnp.log1p(jnp.exp(-jnp.abs(x)))


def _chunk_tri(tm, upper=False):
    r = lax.broadcasted_iota(jnp.int32, (tm, tm), 0)
    c = lax.broadcasted_iota(jnp.int32, (tm, tm), 1)
    same = lax.div(r, CHUNK) == lax.div(c, CHUNK)
    order = (c >= r) if upper else (c <= r)
    return jnp.where(same & order, 1.0, 0.0)


def _gdn_prep_fwd(proj_a, conv_w, a_log, dt_bias, *, name, tm=256):
    S = proj_a.shape[0]
    nblk = S // tm
    W3 = 3 * GDN_WIDTH

    def body(cur_ref, prev_ref, ba_ref, cw_ref, al_ref, dt_ref, qn_ref, kn_ref, v_ref, gcb_ref, bb_ref):
        i = pl.program_id(0)
        prev = jnp.where(i > 0, prev_ref[...], 0.0)
        c = _conv_rows(prev, cur_ref[...], cw_ref[...], GDN_CONV)
        a = c * _sigmoid(c)
        ba = ba_ref[...]
        lane = lax.broadcasted_iota(jnp.int32, (tm, 128), 1)
        g4 = jnp.zeros((tm, 128), F32)
        for h in range(GDN_HEADS):
            sl = slice(GDN_DIM * h, GDN_DIM * (h + 1))
            qh = a[:, GDN_DIM * h:GDN_DIM * (h + 1)]
            kh = a[:, GDN_WIDTH + GDN_DIM * h:GDN_WIDTH + GDN_DIM * (h + 1)]
            qn_ref[:, sl] = qh * (lax.rsqrt(jnp.sum(qh * qh, axis=-1, keepdims=True) + EPS) * (GDN_DIM ** -0.5))
            kn_ref[:, sl] = kh * lax.rsqrt(jnp.sum(kh * kh, axis=-1, keepdims=True) + EPS)
            beta = _sigmoid(ba[:, h:h + 1])
            bb_ref[:, sl] = jnp.broadcast_to(beta, (tm, GDN_DIM))
            g = -jnp.exp(al_ref[0:1, h:h + 1]) * _softplus(ba[:, GDN_HEADS + h:GDN_HEADS + h + 1] + dt_ref[0:1, h:h + 1])
            g4 = jnp.where(lane == h, g, g4)
        v_ref[...] = a[:, 2 * GDN_WIDTH:]
        gc = _exact_tri_dot(_chunk_tri(tm), g4)
        for h in range(GDN_HEADS):
            gcb_ref[:, GDN_DIM * h:GDN_DIM * (h + 1)] = jnp.broadcast_to(gc[:, h:h + 1], (tm, GDN_DIM))

    prev_spec, _ = _halo_specs(tm, W3, 0, nblk)
    row = pl.BlockSpec((tm, GDN_WIDTH), lambda i: (i, 0))
    small = lambda a: pl.BlockSpec(a.shape, lambda i: (0, 0))
    return pl.pallas_call(
        body, name=name, grid=(nblk,),
        in_specs=[pl.BlockSpec((tm, W3), lambda i: (i, 0)), prev_spec,
                  pl.BlockSpec((tm, 128), lambda i: (i, W3 // 128)), small(conv_w), small(a_log), small(dt_bias)],
        out_specs=[row] * 5, out_shape=[_sds((S, GDN_WIDTH))] * 5, compiler_params=_params(("parallel",)),
    )(proj_a, proj_a, proj_a, conv_w, a_log, dt_bias)


GDN_STACK = GDN_HEADS * CHUNK


def _stack(ref, rows):
    return jnp.concatenate([ref[rows, GDN_DIM * h:GDN_DIM * (h + 1)] for h in range(GDN_HEADS)], axis=0)


def _unstack_to(ref, rows, x):
    for h in range(GDN_HEADS):
        ref[rows, GDN_DIM * h:GDN_DIM * (h + 1)] = x[CHUNK * h:CHUNK * (h + 1)].astype(ref.dtype)


def _stack_masks():
    r = lax.broadcasted_iota(jnp.int32, (GDN_STACK, GDN_STACK), 0)
    c = lax.broadcasted_iota(jnp.int32, (GDN_STACK, GDN_STACK), 1)
    same = (r & -CHUNK) == (c & -CHUNK)
    return same & (r >= c), same & (r > c), r == c


def _stack_decay(gs, bs, incl):
    g2 = jnp.concatenate([gs, gs], axis=1)
    diff = g2 - g2.T
    dec = jnp.where(incl, jnp.exp(jnp.where(incl, diff, 0.0)), 0.0)
    return dec, jnp.concatenate([bs, bs], axis=1).T


def _head_mask():
    r = lax.broadcasted_iota(jnp.int32, (GDN_STACK, GDN_WIDTH), 0)
    c = lax.broadcasted_iota(jnp.int32, (GDN_STACK, GDN_WIDTH), 1)
    return (r & -CHUNK) * (GDN_DIM // CHUNK) == (c & -GDN_DIM)


def _head_spread(x):
    return jnp.where(_head_mask(), jnp.concatenate([x] * GDN_HEADS, axis=1), 0.0)


def _head_diag(x):
    xm = jnp.where(_head_mask(), x, 0.0)
    out = xm[:, 0:GDN_DIM]
    for h in range(1, GDN_HEADS):
        out = out + xm[:, GDN_DIM * h:GDN_DIM * (h + 1)]
    return out


def _last_rows(gs, n):
    return jnp.concatenate([jnp.broadcast_to(gs[CHUNK * (h + 1) - 1:CHUNK * (h + 1)], (n, GDN_DIM)) for h in range(GDN_HEADS)], axis=0)


def _gdn_chunk_fwd(qn, kn, v, gcb, bb, *, name):
    S = qn.shape[0]

    def body(qn_ref, kn_ref, v_ref, gcb_ref, bb_ref, uv_ref, wk_ref, at_ref, t_ref, wkb_ref, qdb_ref, keb_ref):
        incl, strict, diag = _stack_masks()
        for c in range(CHUNKS_PER_STEP):
            rows = slice(CHUNK * c, CHUNK * (c + 1))
            srows = slice(GDN_STACK * c, GDN_STACK * (c + 1))
            q, k, vv, gs, bs = [_stack(r, rows) for r in (qn_ref, kn_ref, v_ref, gcb_ref, bb_ref)]
            dec, bt = _stack_decay(gs, bs, incl)
            p = -jnp.where(strict, dec * _bdot_nt(k, k) * bt, 0.0)
            t = jnp.where(diag, 1.0, 0.0) + p
            for _ in range(5):
                p = _bdot(p, p)
                t = t + _bdot(t, p)
            sol = _dot3(t, jnp.concatenate([vv, jnp.exp(gs) * k], axis=1), ((1,), (0,)))
            _unstack_to(uv_ref, rows, sol[:, :GDN_DIM])
            _unstack_to(wk_ref, rows, sol[:, GDN_DIM:])
            at_ref[srows, :] = dec * _bdot_nt(q, k) * bt
            t_ref[srows, :] = t
            wkb_ref[srows, :] = _head_spread(sol[:, GDN_DIM:]).astype(BF16)
            qdb_ref[srows, :] = _head_spread(q * jnp.exp(gs)).astype(BF16)
            keb_ref[srows, :] = _head_spread(k * jnp.exp(_last_rows(gs, CHUNK) - gs) * bs).astype(BF16)

    step = CHUNKS_PER_STEP * CHUNK
    row = pl.BlockSpec((step, GDN_WIDTH), lambda n: (n, 0))
    sq = pl.BlockSpec((CHUNKS_PER_STEP * GDN_STACK, GDN_STACK), lambda n: (n, 0))
    wide = pl.BlockSpec((CHUNKS_PER_STEP * GDN_STACK, GDN_WIDTH), lambda n: (n, 0))
    nsq = S // CHUNK * GDN_STACK
    return pl.pallas_call(
        body, name=name, grid=(S // step,), in_specs=[row] * 5, out_specs=[row, row, sq, sq, wide, wide, wide],
        out_shape=[_sds((S, GDN_WIDTH)), _sds((S, GDN_WIDTH)), _sds((nsq, GDN_STACK)), _sds((nsq, GDN_STACK))]
        + [_sds((nsq, GDN_WIDTH), BF16)] * 3,
        compiler_params=_params(("parallel",)),
    )(qn, kn, v, gcb, bb)


SCAN_CHUNKS = 8


def _gdn_scan_fwd(uv, at, wkb, qdb, keb, gcb, proj_z, gnw, *, name):
    S = uv.shape[0]
    nc = S // CHUNK

    def body(uv_ref, at_ref, wkb_ref, qdb_ref, keb_ref, gcb_ref, z_ref, gnw_ref, o_ref, u_ref, sp_ref, oa_ref, st_ref):
        n = pl.program_id(0)

        @pl.when(n == 0)
        def _():
            st_ref[...] = jnp.zeros_like(st_ref)

        for c in range(SCAN_CHUNKS):
            rows = slice(CHUNK * c, CHUNK * (c + 1))
            srows = slice(GDN_STACK * c, GDN_STACK * (c + 1))
            st = st_ref[...]
            sp_ref[GDN_WIDTH * c:GDN_WIDTH * (c + 1), :] = st
            uv, gs, z = [_stack(r, rows) for r in (uv_ref, gcb_ref, z_ref)]
            u = uv - _bdot(wkb_ref[srows, :], st)
            o = _bdot(qdb_ref[srows, :], st) + _bdot(at_ref[srows, :], u)
            st_ref[...] = jnp.exp(_last_rows(gs, GDN_DIM)) * st + _bdot_tn(keb_ref[srows, :], u)
            _unstack_to(u_ref, rows, u)
            _unstack_to(o_ref, rows, o)
            r = lax.rsqrt(jnp.mean(o * o, axis=-1, keepdims=True) + EPS)
            oa = o * r * gnw_ref[...] * (z * _sigmoid(z))
            oa_ref[rows, :] = jnp.concatenate([oa[CHUNK * h:CHUNK * (h + 1)] for h in range(GDN_HEADS)], axis=1).astype(BF16)

    row = pl.BlockSpec((SCAN_CHUNKS * CHUNK, GDN_WIDTH), lambda n: (n, 0))
    sq = pl.BlockSpec((SCAN_CHUNKS * GDN_STACK, GDN_STACK), lambda n: (n, 0))
    wide = pl.BlockSpec((SCAN_CHUNKS * GDN_STACK, GDN_WIDTH), lambda n: (n, 0))
    return pl.pallas_call(
        body, name=name, grid=(nc // SCAN_CHUNKS,),
        in_specs=[row, sq, wide, wide, wide, row, row, pl.BlockSpec((1, GDN_DIM), lambda n: (0, 0))],
        out_specs=[row, row, pl.BlockSpec((SCAN_CHUNKS * GDN_WIDTH, GDN_DIM), lambda n: (n, 0)), row],
        out_shape=[_sds((S, GDN_WIDTH)), _sds((S, GDN_WIDTH)), _sds((nc * GDN_WIDTH, GDN_DIM)), _sds((S, 2 * GDN_WIDTH), BF16)],
        scratch_shapes=[pltpu.VMEM((GDN_WIDTH, GDN_DIM), F32)],
        compiler_params=_params(("arbitrary",)),
    )(uv, at, wkb, qdb, keb, gcb, proj_z, gnw)


def _gdn_scan_bwd(d_oab, o, proj_z, gnw, sp, u, at, wkb, qdb, keb, gcb, *, name, after=()):
    S = o.shape[0]
    nc = S // CHUNK
    ns = nc // SCAN_CHUNKS

    def body(do_ref, o_ref, z_ref, gnw_ref, sp_ref, u_ref, at_ref, wkb_ref, qdb_ref, keb_ref, gcb_ref, *rest):
        dz_ref, dgn_ref, du_ref, dwk_ref, dat_ref, dqd_ref, dke_ref, dgl_ref, ds_ref = rest[len(after):]
        n = pl.program_id(0)

        @pl.when(n == 0)
        def _():
            ds_ref[...] = jnp.zeros_like(ds_ref)
            dgn_ref[...] = jnp.zeros_like(dgn_ref)

        gw = gnw_ref[...]
        for c in reversed(range(SCAN_CHUNKS)):
            rows = slice(CHUNK * c, CHUNK * (c + 1))
            srows = slice(GDN_STACK * c, GDN_STACK * (c + 1))
            d_oa, oo, z, uu, gs = [_stack(r, rows) for r in (do_ref, o_ref, z_ref, u_ref, gcb_ref)]
            sg = _sigmoid(z)
            r = lax.rsqrt(jnp.mean(oo * oo, axis=-1, keepdims=True) + EPS)
            xh = oo * r
            dy = d_oa * (z * sg)
            _unstack_to(dz_ref, rows, d_oa * (xh * gw) * _dsilu(z, sg))
            dgn_ref[...] += jnp.sum(dy * xh, axis=0, keepdims=True)
            dxh = dy * gw
            do = r * (dxh - xh * jnp.mean(dxh * xh, axis=-1, keepdims=True))

            st = sp_ref[GDN_WIDTH * c:GDN_WIDTH * (c + 1), :]
            dst = ds_ref[...]
            ge = jnp.exp(_last_rows(gs, GDN_DIM))
            _unstack_to(dqd_ref, rows, _head_diag(_bdot_nt(do, st)))
            dat_ref[srows, :] = _bdot_nt(do, uu)
            du = _bdot_tn(at_ref[srows, :], do) + _bdot(keb_ref[srows, :], dst)
            _unstack_to(dke_ref, rows, _head_diag(_bdot_nt(uu, dst)))
            prod = dst * st
            for h in range(GDN_HEADS):
                blk = prod[GDN_DIM * h:GDN_DIM * (h + 1)]
                dge = jnp.sum(jnp.sum(blk, axis=1, keepdims=True), axis=0, keepdims=True)
                dgl_ref[c, :, GDN_DIM * h:GDN_DIM * (h + 1)] = jnp.broadcast_to(dge * ge[GDN_DIM * h:GDN_DIM * h + 1], (8, GDN_DIM))
            ds_ref[...] = _bdot_tn(qdb_ref[srows, :], do) + ge * dst - _bdot_tn(wkb_ref[srows, :], du)
            _unstack_to(du_ref, rows, du)
            _unstack_to(dwk_ref, rows, -_head_diag(_bdot_nt(du, st)))

    rev = lambda n: (ns - 1 - n, 0)
    row = pl.BlockSpec((SCAN_CHUNKS * CHUNK, GDN_WIDTH), rev)
    sq = pl.BlockSpec((SCAN_CHUNKS * GDN_STACK, GDN_STACK), rev)
    wide = pl.BlockSpec((SCAN_CHUNKS * GDN_STACK, GDN_WIDTH), rev)
    one = pl.BlockSpec((1, GDN_DIM), lambda n: (0, 0))
    return pl.pallas_call(
        body, name=name, grid=(ns,),
        in_specs=[row, row, row, one, pl.BlockSpec((SCAN_CHUNKS * GDN_WIDTH, GDN_DIM), rev), row, sq, wide, wide, wide, row]
        + [ANY] * len(after),
        out_specs=[row, one, row, row, sq, row, row, pl.BlockSpec((SCAN_CHUNKS, 8, GDN_WIDTH), lambda n: (ns - 1 - n, 0, 0))],
        out_shape=[_sds((S, GDN_WIDTH), BF16), _sds((1, GDN_DIM)), _sds((S, GDN_WIDTH)), _sds((S, GDN_WIDTH)),
                   _sds((nc * GDN_STACK, GDN_STACK)), _sds((S, GDN_WIDTH)), _sds((S, GDN_WIDTH)), _sds((nc, 8, GDN_WIDTH))],
        scratch_shapes=[pltpu.VMEM((GDN_WIDTH, GDN_DIM), F32)],
        compiler_params=_params(("arbitrary",)),
    )(d_oab, o, proj_z, gnw, sp, u, at, wkb, qdb, keb, gcb, *after)


def _gdn_chunk_bwd(qn, kn, gcb, bb, tmat, uv, wk, du, dwk, dat, dqd, dke, dgl, *, name):
    S = qn.shape[0]

    def body(qn_ref, kn_ref, gcb_ref, bb_ref, t_ref, uv_ref, wk_ref, du_ref, dwk_ref, dat_ref, dqd_ref, dke_ref,
             dgl_ref, dq_ref, dk_ref, dv_ref, dg_ref, dbeta_ref):
        incl, strict, _ = _stack_masks()
        lane = lax.broadcasted_iota(jnp.int32, (CHUNK, 128), 1)
        rowi = lax.broadcasted_iota(jnp.int32, (CHUNK, 1), 0)
        rsum = lambda x: jnp.sum(x, axis=-1, keepdims=True)
        for c in range(CHUNKS_PER_STEP):
            rows = slice(CHUNK * c, CHUNK * (c + 1))
            srows = slice(GDN_STACK * c, GDN_STACK * (c + 1))
            q, k, gs, bs, uv, wk, du, dwk, dqd, dke = [
                _stack(r, rows) for r in (qn_ref, kn_ref, gcb_ref, bb_ref, uv_ref, wk_ref, du_ref, dwk_ref, dqd_ref, dke_ref)]
            dec, bt = _stack_decay(gs, bs, incl)
            kk = _bdot_nt(k, k)
            qk = _bdot_nt(q, k)
            d_rhs = _dot3(t_ref[srows, :], jnp.concatenate([du, dwk], axis=1), ((0,), (0,)))
            sol = jnp.concatenate([uv, wk], axis=1)
            d_l = jnp.where(strict, -_dot3(d_rhs, sol, ((1,), (1,))), 0.0)
            d_a = jnp.where(incl, dat_ref[srows, :], 0.0)
            gam = jnp.exp(gs)
            e = jnp.exp(_last_rows(gs, CHUNK) - gs)
            d_gk = d_rhs[:, GDN_DIM:]
            ml = d_l * dec * bt
            ma = d_a * dec * bt
            _unstack_to(dq_ref, rows, _bdot(ma, k) + dqd * gam)
            _unstack_to(dk_ref, rows, _bdot(ml + ml.T, k) + _bdot_tn(ma, q) + d_gk * gam + dke * (e * bs))
            _unstack_to(dv_ref, rows, d_rhs[:, :GDN_DIM])
            wb = d_l * dec * kk + d_a * dec * qk
            ew = wb * bt
            s_ke = rsum(dke * k * (e * bs))
            dbeta = rsum(wb.T) + rsum(dke * k * e)
            dgc = rsum(ew) - rsum(ew.T) + rsum(dqd * q * gam) + rsum(d_gk * k * gam) - s_ke
            dgc4 = jnp.zeros((CHUNK, 128), F32)
            db4 = jnp.zeros((CHUNK, 128), F32)
            for h in range(GDN_HEADS):
                hr = slice(CHUNK * h, CHUNK * (h + 1))
                tail = jnp.sum(s_ke[hr], axis=0, keepdims=True) + dgl_ref[c, 0:1, GDN_DIM * h:GDN_DIM * h + 1]
                dgc4 = jnp.where(lane == h, dgc[hr] + jnp.where(rowi == CHUNK - 1, tail, 0.0), dgc4)
                db4 = jnp.where(lane == h, dbeta[hr], db4)
            dg_ref[rows, :] = _exact_tri_dot(_chunk_tri(CHUNK, upper=True), dgc4)
            dbeta_ref[rows, :] = db4

    step = CHUNKS_PER_STEP * CHUNK
    row = pl.BlockSpec((step, GDN_WIDTH), lambda n: (n, 0))
    sq = pl.BlockSpec((CHUNKS_PER_STEP * GDN_STACK, GDN_STACK), lambda n: (n, 0))
    col = pl.BlockSpec((step, 128), lambda n: (n, 0))
    return pl.pallas_call(
        body, name=name, grid=(S // step,),
        in_specs=[row] * 4 + [sq, row, row, row, row, sq, row, row,
                              pl.BlockSpec((CHUNKS_PER_STEP, 8, GDN_WIDTH), lambda n: (n, 0, 0))],
        out_specs=[row, row, row, col, col],
        out_shape=[_sds((S, GDN_WIDTH))] * 3 + [_sds((S, 128))] * 2, compiler_params=_params(("parallel",)),
    )(qn, kn, gcb, bb, tmat, uv, wk, du, dwk, dat, dqd, dke, dgl)


def _gdn_prep_bwd(dqn, dkn, dv, dg, dbeta, proj_a, conv_w, a_log, dt_bias, *, name, tm=256):
    S = proj_a.shape[0]
    nblk = S // tm
    W3 = 3 * GDN_WIDTH

    def body(dqn_ref, dkn_ref, dv_ref, dg_ref, dbeta_ref, cur_ref, prev_ref, ba_ref, cw_ref, al_ref, dt_ref,
             dc_ref, dba_ref, sm_ref):
        i = pl.program_id(0)
        prev = jnp.where(i > 0, prev_ref[...], 0.0)
        c = _conv_rows(prev, cur_ref[...], cw_ref[...], GDN_CONV)
        sg = _sigmoid(c)
        a = c * sg
        dsl = _dsilu(c, sg)
        ba = ba_ref[...]
        lane = lax.broadcasted_iota(jnp.int32, (tm, 128), 1)
        lane1 = lax.broadcasted_iota(jnp.int32, (1, 128), 1)
        dba = jnp.zeros((tm, 128), F32)
        sm = jnp.zeros((1, 128), F32)
        for h in range(GDN_HEADS):
            sl = slice(GDN_DIM * h, GDN_DIM * (h + 1))
            ks = slice(GDN_WIDTH + GDN_DIM * h, GDN_WIDTH + GDN_DIM * (h + 1))
            qh, kh = a[:, sl], a[:, ks]
            rq = lax.rsqrt(jnp.sum(qh * qh, axis=-1, keepdims=True) + EPS)
            rk = lax.rsqrt(jnp.sum(kh * kh, axis=-1, keepdims=True) + EPS)
            qhat, khat = qh * rq, kh * rk
            dyq = dqn_ref[:, sl] * (GDN_DIM ** -0.5)
            dyk = dkn_ref[:, sl]
            dq = rq * (dyq - qhat * jnp.sum(dyq * qhat, axis=-1, keepdims=True))
            dk = rk * (dyk - khat * jnp.sum(dyk * khat, axis=-1, keepdims=True))
            dc_ref[:, sl] = dq * dsl[:, sl]
            dc_ref[:, ks] = dk * dsl[:, ks]
            beta = _sigmoid(ba[:, h:h + 1])
            db = dbeta_ref[:, h:h + 1] * beta * (1.0 - beta)
            aneg = -jnp.exp(al_ref[0:1, h:h + 1])
            xa = ba[:, GDN_HEADS + h:GDN_HEADS + h + 1] + dt_ref[0:1, h:h + 1]
            dgh = dg_ref[:, h:h + 1]
            dxa = dgh * aneg * _sigmoid(xa)
            dba = jnp.where(lane == h, db, dba)
            dba = jnp.where(lane == GDN_HEADS + h, dxa, dba)
            d_alog = jnp.sum(dgh * _softplus(xa), axis=0, keepdims=True) * aneg
            sm = jnp.where(lane1 == h, d_alog, sm)
            sm = jnp.where(lane1 == GDN_HEADS + h, jnp.sum(dxa, axis=0, keepdims=True), sm)
        vs = slice(2 * GDN_WIDTH, W3)
        dc_ref[:, vs] = dv_ref[...] * dsl[:, vs]
        dba_ref[...] = dba

        @pl.when(i == 0)
        def _():
            sm_ref[...] = sm

        @pl.when(i > 0)
        def _():
            sm_ref[...] += sm

    prev_spec, _ = _halo_specs(tm, W3, 0, nblk)
    row = pl.BlockSpec((tm, GDN_WIDTH), lambda i: (i, 0))
    col = pl.BlockSpec((tm, 128), lambda i: (i, 0))
    small = lambda a: pl.BlockSpec(a.shape, lambda i: (0, 0))
    return pl.pallas_call(
        body, name=name, grid=(nblk,),
        in_specs=[row, row, row, col, col, pl.BlockSpec((tm, W3), lambda i: (i, 0)), prev_spec,
                  pl.BlockSpec((tm, 128), lambda i: (i, W3 // 128)), small(conv_w), small(a_log), small(dt_bias)],
        out_specs=[pl.BlockSpec((tm, W3), lambda i: (i, 0)), col, pl.BlockSpec((1, 128), lambda i: (0, 0))],
        out_shape=[_sds((S, W3)), _sds((S, 128)), _sds((1, 128))], compiler_params=_params(("arbitrary",)),
    )(dqn, dkn, dv, dg, dbeta, proj_a, proj_a, proj_a, conv_w, a_log, dt_bias)


def _gdn_conv_bwd(dc, dba, proj_a, conv_w, *, name, tm=256):
    S = proj_a.shape[0]
    nblk = S // tm
    W3 = 3 * GDN_WIDTH

    def body(dc_ref, dnext_ref, dba_ref, cur_ref, prev_ref, cw_ref, da_ref, dcw_ref):
        i = pl.program_id(0)
        prev = jnp.where(i > 0, prev_ref[...], 0.0)
        nxt = jnp.where(i < nblk - 1, dnext_ref[...], 0.0)
        dx, dw = _conv_rows_bwd(dc_ref[...], nxt, prev, cur_ref[...], cw_ref[...], GDN_CONV)
        da_ref[:, 0:W3] = dx.astype(BF16)
        da_ref[:, W3:] = dba_ref[...].astype(BF16)

        @pl.when(i == 0)
        def _():
            dcw_ref[...] = dw

        @pl.when(i > 0)
        def _():
            dcw_ref[...] += dw

    prev_spec, next_spec = _halo_specs(tm, W3, 0, nblk)
    wide = pl.BlockSpec((tm, W3), lambda i: (i, 0))
    return pl.pallas_call(
        body, name=name, grid=(nblk,),
        in_specs=[wide, next_spec, pl.BlockSpec((tm, 128), lambda i: (i, 0)), wide, prev_spec,
                  pl.BlockSpec(conv_w.shape, lambda i: (0, 0))],
        out_specs=[pl.BlockSpec((tm, A_COLS), lambda i: (i, 0)), pl.BlockSpec(conv_w.shape, lambda i: (0, 0))],
        out_shape=[_sds((S, A_COLS), BF16), _sds(conv_w.shape)], compiler_params=_params(("arbitrary",)),
    )(dc, dc, dba, proj_a, proj_a, conv_w)


def _band_mask(nk):
    i = lax.broadcasted_iota(jnp.int32, (2 * BAND, nk), 0) & (BAND - 1)
    j = lax.broadcasted_iota(jnp.int32, (2 * BAND, nk), 1)
    if nk == BAND:
        return j <= i
    return (j >= i) & (j <= i + BAND)


def _stack_heads(x, lo):
    return jnp.concatenate([jnp.where(lo, x, 0.0), jnp.where(lo, 0.0, x)], axis=0)


def _stack_cols(x):
    return jnp.concatenate([x[:, 0:1], x[:, DIL_DIM:DIL_DIM + 1]], axis=0)


def _unstack(x, lo):
    return jnp.where(lo, x[0:BAND], x[BAND:2 * BAND])


def _rows(start, size, stride):
    return pl.ds(start, size) if stride == 1 else pl.ds(start, size, stride=stride)


ATTN_LANES = 4


def _attn_blocks(S, visit_many, lanes=ATTN_LANES):
    for d in DILATIONS:
        nb = S // (d * BAND)
        if d == 1:
            half = nb // 2
            visit_many(d, [(0, 0, True), (0, half, False)])

            def pair(n, c):
                visit_many(1, [(0, n, False), (0, n + half, False)])
                return c
            lax.fori_loop(1, half, pair, 0)
        elif nb > 1:
            for r0 in range(0, d, lanes):
                visit_many(d, [(r0 + t, 0, True) for t in range(lanes)])

                def column(n, c, d=d, r0=r0):
                    visit_many(d, [(r0 + t, n, False) for t in range(lanes)])
                    return c
                lax.fori_loop(1, nb, column, 0)
        else:
            def group(g, c, d=d):
                visit_many(d, [(g * lanes + t, 0, True) for t in range(lanes)])
                return c
            lax.fori_loop(0, d // lanes, group, 0)


def _attn_fwd(proj_b, oab, *, name):
    S = proj_b.shape[0]
    scale = DIL_DIM ** -0.5

    def body(q_ref, k_ref, v_ref, oab_in_ref, ob_ref, lse_ref, m_ref, l_ref, acc_ref):
        del oab_in_ref
        lane = lax.broadcasted_iota(jnp.int32, (BAND, 128), 1)
        lo = lane < DIL_DIM
        m_ref[...] = jnp.full_like(m_ref, NEG_BIG)
        l_ref[...] = jnp.zeros_like(l_ref)
        acc_ref[...] = jnp.zeros_like(acc_ref)

        def load(d, r, n, first):
            nk = BAND if first else 2 * BAND
            qrows = _rows(r + n * (BAND * d), BAND, d)
            krows = _rows(r if first else r + (n - 1) * (BAND * d), nk, d)
            return dict(nk=nk, qrows=qrows, q=q_ref[qrows, :] * scale, k=k_ref[krows, :].astype(BF16),
                        v=v_ref[krows, :].astype(BF16), m=m_ref[qrows, :], l=l_ref[qrows, :], acc=acc_ref[qrows, :])

        def compute(b):
            q, k, v = b["q"], b["k"], b["v"]
            s = jnp.where(_band_mask(b["nk"]), _bdot_nt(_stack_heads(q, lo), k), NEG_BIG)
            m_old = _stack_cols(b["m"])
            m_new = jnp.maximum(m_old, jnp.max(s, axis=-1, keepdims=True))
            p = jnp.exp(s - m_new)
            alpha = _unstack(jnp.exp(m_old - m_new), lo)
            l_new = alpha * b["l"] + _unstack(jnp.sum(p, axis=-1, keepdims=True), lo)
            return _unstack(m_new, lo), l_new, alpha * b["acc"] + _unstack(_bdot(p, v), lo)

        def visit_many(d, blocks):
            loaded = [load(d, *blk) for blk in blocks]
            done = [compute(b) for b in loaded]
            for b, (m_new, l_new, acc_new) in zip(loaded, done):
                m_ref[b["qrows"], :] = m_new
                l_ref[b["qrows"], :] = l_new
                acc_ref[b["qrows"], :] = acc_new

        _attn_blocks(S, visit_many)
        ob_ref[...] = (acc_ref[...] / l_ref[...]).astype(BF16)
        lse_ref[...] = m_ref[...] + jnp.log(l_ref[...])

    part = lambda t: pl.BlockSpec((S, 128), lambda p: (0, 3 * p + t))
    return pl.pallas_call(
        body, name=name, grid=(DIL_PAIRS,),
        in_specs=[part(0), part(1), part(2), pl.BlockSpec(memory_space=pl.ANY)],
        out_specs=[pl.BlockSpec((S, 128), lambda p: (0, GDN_WIDTH // 128 + p)), pl.BlockSpec((S, 128), lambda p: (0, p))],
        out_shape=[_sds(oab.shape, BF16), _sds((S, DIL_WIDTH))],
        scratch_shapes=[pltpu.VMEM((S, 128), F32)] * 3, input_output_aliases={3: 0},
        compiler_params=_params(("parallel",)),
    )(proj_b, proj_b, proj_b, oab)


def _attn_bwd(proj_b, oab, d_oab, lse, *, name):
    S = proj_b.shape[0]
    scale = DIL_DIM ** -0.5

    def body(q_ref, k_ref, v_ref, o_ref, do_ref, lse_ref, dqkv_ref, dq_ref, dk_ref, dv_ref, delta_ref):
        lane = lax.broadcasted_iota(jnp.int32, (BAND, 128), 1)
        lo = lane < DIL_DIM
        dq_ref[...] = jnp.zeros_like(dq_ref)
        dk_ref[...] = jnp.zeros_like(dk_ref)
        dv_ref[...] = jnp.zeros_like(dv_ref)
        prod = do_ref[...] * o_ref[...].astype(F32)
        lo_all = lax.broadcasted_iota(jnp.int32, (S, 128), 1) < DIL_DIM
        delta_ref[...] = jnp.where(lo_all, jnp.sum(jnp.where(lo_all, prod, 0.0), axis=-1, keepdims=True),
                                   jnp.sum(jnp.where(lo_all, 0.0, prod), axis=-1, keepdims=True))

        def load(d, r, n, first):
            nk = BAND if first else 2 * BAND
            qrows = _rows(r + n * (BAND * d), BAND, d)
            krows = _rows(r if first else r + (n - 1) * (BAND * d), nk, d)
            return dict(nk=nk, qrows=qrows, krows=krows, q=q_ref[qrows, :] * scale, k=k_ref[krows, :], v=v_ref[krows, :],
                        do=do_ref[qrows, :], delta=delta_ref[qrows, :], lse=lse_ref[qrows, :],
                        dq=dq_ref[qrows, :], dk=dk_ref[krows, :], dv=dv_ref[krows, :])

        def compute(b):
            q, k, v, do = b["q"], b["k"], b["v"], b["do"]
            qs, dos = _stack_heads(q, lo), _stack_heads(do, lo)
            p = jnp.where(_band_mask(b["nk"]), jnp.exp(_bdot_nt(qs, k) - _stack_cols(b["lse"])), 0.0)
            ds = p * (_bdot_nt(dos, v) - _stack_cols(b["delta"]))
            dq = b["dq"] + _unstack(_bdot(ds, k), lo) * scale
            return dq, b["dk"] + _bdot_tn(ds, qs), b["dv"] + _bdot_tn(p, dos)

        def visit_many(d, blocks):
            loaded = [load(d, *blk) for blk in blocks]
            done = [compute(b) for b in loaded]
            for b, (dq, dk, dv) in zip(loaded, done):
                dq_ref[b["qrows"], :] = dq
                dk_ref[b["krows"], :] = dk
                dv_ref[b["krows"], :] = dv

        _attn_blocks(S, visit_many, lanes=2)
        dqkv_ref[:, 0:128] = dq_ref[...].astype(BF16)
        dqkv_ref[:, 128:256] = dk_ref[...].astype(BF16)
        dqkv_ref[:, 256:384] = dv_ref[...].astype(BF16)

    half = lambda p: (0, GDN_WIDTH // 128 + p)
    part = lambda t: pl.BlockSpec((S, 128), lambda p: (0, 3 * p + t))
    return pl.pallas_call(
        body, name=name, grid=(DIL_PAIRS,),
        in_specs=[part(0), part(1), part(2), pl.BlockSpec((S, 128), half), pl.BlockSpec((S, 128), half),
                  pl.BlockSpec((S, 128), lambda p: (0, p))],
        out_specs=pl.BlockSpec((S, 384), lambda p: (0, p)), out_shape=_sds((S, 3 * DIL_WIDTH), BF16),
        scratch_shapes=[pltpu.VMEM((S, 128), F32)] * 4, compiler_params=_params(("parallel",)),
    )(proj_b, proj_b, proj_b, oab, d_oab, lse)


FF_SLAB = 2 * D_FF // N_DEV
FF_PAIRS = N_DEV // 2
ROWS16 = 16


def _taps(w, x, base, n):
    out = _shifted(x, base, n) * w[0:1]
    for t in range(1, FFN_CONV):
        out = out + _shifted(x, base + t, n) * w[t:t + 1]
    return out


def _ffn_fwd(h2, x1, w_up, conv_w, w_down, final_w, tgt, *, name, tm=512):
    S, D = h2.shape
    ni = S // tm
    per = tm // ROWS16

    def body(h_ref, hp_ref, x1_ref, wg_ref, wu_ref, cg_ref, cu_ref, wd_ref, fw_ref, t_ref,
             dx_ref, dxb_ref, dfw_ref, loss_ref, ug_ref, uu_ref, x2_ref):
        i, j = pl.program_id(0), pl.program_id(1)
        hv = jnp.concatenate([hp_ref[...], h_ref[...]], axis=0)
        row = lax.broadcasted_iota(jnp.int32, (tm + ROWS16, 1), 0)
        keep = (i > 0) | (row >= ROWS16)

        def branch(w_ref, c_ref, u_ref):
            u = lax.dot_general(hv, w_ref[...], (((1,), (1,)), ((), ())), preferred_element_type=F32).astype(BF16)
            u_ref[...] = u[ROWS16:]
            return _taps(c_ref[...], jnp.where(keep, u.astype(F32), 0.0), ROWS16 - (FFN_CONV - 1), tm)

        gate = branch(wg_ref, cg_ref, ug_ref)
        up = branch(wu_ref, cu_ref, uu_ref)
        act = (gate * _sigmoid(gate) * up).astype(BF16)
        part = jnp.dot(act, wd_ref[...], preferred_element_type=F32)

        @pl.when(j == 0)
        def _():
            x2_ref[...] = x1_ref[...] + part

        @pl.when((j > 0) & (j < FF_PAIRS - 1))
        def _():
            x2_ref[...] += part

        @pl.when(j == FF_PAIRS - 1)
        def _():
            xv = x2_ref[...] + part
            wv = fw_ref[...]
            r = lax.rsqrt(jnp.mean(xv * xv, axis=-1, keepdims=True) + EPS)
            err = xv * r * wv - t_ref[...]
            lsum = jnp.sum(jnp.sum(err * err, axis=-1, keepdims=True), axis=0, keepdims=True) * (0.5 / D)
            g = err * (1.0 / D)
            xh = xv * r
            gw = g * wv
            dx = r * (gw - xh * jnp.mean(gw * xh, axis=-1, keepdims=True))
            dx_ref[...] = dx
            dxb_ref[...] = dx.astype(BF16)
            dfw = jnp.sum(g * xh, axis=0, keepdims=True)
            lpart = jnp.broadcast_to(lsum, (1, 128))

            @pl.when(i == 0)
            def _():
                dfw_ref[...] = dfw
                loss_ref[...] = lpart

            @pl.when(i > 0)
            def _():
                dfw_ref[...] += dfw
                loss_ref[...] += lpart

    rows = pl.BlockSpec((tm, D), lambda i, j: (i, 0))
    slab = lambda off: pl.BlockSpec((None, FF_SLAB, D), lambda i, j: (j + off, 0, 0))
    cslab = lambda off: pl.BlockSpec((None, FFN_CONV, FF_SLAB), lambda i, j: (j + off, 0, 0))
    uspec = pl.BlockSpec((None, tm, FF_SLAB), lambda i, j: (j, i, 0))
    return pl.pallas_call(
        body, name=name, grid=(ni, FF_PAIRS),
        in_specs=[rows, pl.BlockSpec((ROWS16, D), lambda i, j: (jnp.maximum(i * per - 1, 0), 0)), rows,
                  slab(0), slab(FF_PAIRS), cslab(0), cslab(FF_PAIRS), pl.BlockSpec((FF_SLAB, D), lambda i, j: (j, 0)),
                  pl.BlockSpec((1, D), lambda i, j: (0, 0)), rows],
        out_specs=[rows, rows, pl.BlockSpec((1, D), lambda i, j: (0, 0)), pl.BlockSpec((1, 128), lambda i, j: (0, 0)), uspec, uspec],
        out_shape=[_sds((S, D)), _sds((S, D), BF16), _sds((1, D)), _sds((1, 128)),
                   _sds((FF_PAIRS, S, FF_SLAB), BF16), _sds((FF_PAIRS, S, FF_SLAB), BF16)],
        scratch_shapes=[pltpu.VMEM((tm, D), F32)],
        compiler_params=_params(("arbitrary", "arbitrary")),
    )(h2, h2, x1, w_up, w_up, conv_w, conv_w, w_down, final_w, tgt)


def _ffn_bwd(dx2, h2, ug, uu, conv_w, w_down, *, name, tm=512):
    S, D = h2.shape
    ni = S // tm
    per = tm // ROWS16
    ext = tm + ROWS16

    def body(dx_ref, dxn_ref, h_ref, ug_ref, ugp_ref, ugn_ref, uu_ref, uup_ref, uun_ref, cg_ref, cu_ref, wd_ref,
             du_ref, gd_ref, gup_ref, dcw_ref, acc_d, acc_g, acc_u, acc_cg, acc_cu):
        i = pl.program_id(1)

        @pl.when(i == 0)
        def _():
            acc_d[...] = jnp.zeros_like(acc_d)
            acc_g[...] = jnp.zeros_like(acc_g)
            acc_u[...] = jnp.zeros_like(acc_u)
            acc_cg[...] = jnp.zeros_like(acc_cg)
            acc_cu[...] = jnp.zeros_like(acc_cu)

        dx = dx_ref[...]
        dxe = jnp.concatenate([dx, dxn_ref[...]], axis=0)
        row = lax.broadcasted_iota(jnp.int32, (ext, 1), 0)
        live = (i < ni - 1) | (row < tm)
        d_act = jnp.where(live, lax.dot_general(dxe, wd_ref[...], (((1,), (1,)), ((), ())), preferred_element_type=F32), 0.0)
        rowp = lax.broadcasted_iota(jnp.int32, (ext + ROWS16, 1), 0)
        keep = (i > 0) | (rowp >= ROWS16)

        def pre(cur, prev, nxt):
            return jnp.where(keep, jnp.concatenate([prev[...], cur[...], nxt[...]], axis=0).astype(F32), 0.0)

        uge, uue = pre(ug_ref, ugp_ref, ugn_ref), pre(uu_ref, uup_ref, uun_ref)
        cg, cu = cg_ref[...], cu_ref[...]
        base = ROWS16 - (FFN_CONV - 1)
        gate = _taps(cg, uge, base, ext)
        up = _taps(cu, uue, base, ext)
        sg = _sigmoid(gate)
        silu = gate * sg
        dgc = d_act * up * _dsilu(gate, sg)
        duc = d_act * silu

        def conv_t(w, dc):
            out = _shifted(dc, FFN_CONV - 1, tm) * w[0:1]
            for t in range(1, FFN_CONV):
                out = out + _shifted(dc, FFN_CONV - 1 - t, tm) * w[t:t + 1]
            return out.astype(BF16)

        du_g, du_u = conv_t(cg, dgc), conv_t(cu, duc)
        du_ref[0] = du_g
        du_ref[1] = du_u
        dcw = lambda dc, xe: jnp.concatenate(
            [jnp.sum(dc[0:tm] * _shifted(xe, base + t, tm), axis=0, keepdims=True) for t in range(FFN_CONV)], axis=0)
        acc_cg[0:FFN_CONV, :] += dcw(dgc, uge)
        acc_cu[0:FFN_CONV, :] += dcw(duc, uue)
        tn = (((0,), (0,)), ((), ()))
        act = (silu[0:tm] * up[0:tm]).astype(BF16)
        acc_d[...] += lax.dot_general(act, dx, tn, preferred_element_type=F32)
        hv = h_ref[...]
        acc_g[...] += lax.dot_general(du_g, hv, tn, preferred_element_type=F32)
        acc_u[...] += lax.dot_general(du_u, hv, tn, preferred_element_type=F32)

        @pl.when(i == ni - 1)
        def _():
            gd_ref[...] = acc_d[...].astype(BF16)
            gup_ref[0] = acc_g[...].astype(BF16)
            gup_ref[1] = acc_u[...].astype(BF16)
            dcw_ref[0] = acc_cg[0:FFN_CONV, :]
            dcw_ref[1] = acc_cu[0:FFN_CONV, :]

    last16 = S // ROWS16 - 1
    rows = pl.BlockSpec((tm, D), lambda j, i: (i, 0))
    rows_next = pl.BlockSpec((ROWS16, D), lambda j, i: (jnp.minimum((i + 1) * per, last16), 0))
    u_cur = pl.BlockSpec((None, tm, FF_SLAB), lambda j, i: (j, i, 0))
    u_prev = pl.BlockSpec((None, ROWS16, FF_SLAB), lambda j, i: (j, jnp.maximum(i * per - 1, 0), 0))
    u_next = pl.BlockSpec((None, ROWS16, FF_SLAB), lambda j, i: (j, jnp.minimum((i + 1) * per, last16), 0))
    cslab = lambda off: pl.BlockSpec((None, FFN_CONV, FF_SLAB), lambda j, i: (j + off, 0, 0))
    return pl.pallas_call(
        body, name=name, grid=(FF_PAIRS, ni),
        in_specs=[rows, rows_next, rows, u_cur, u_prev, u_next, u_cur, u_prev, u_next, cslab(0), cslab(FF_PAIRS),
                  pl.BlockSpec((FF_SLAB, D), lambda j, i: (j, 0))],
        out_specs=[pl.BlockSpec((None, 2, tm, FF_SLAB), lambda j, i: (j, 0, i, 0)), pl.BlockSpec((FF_SLAB, D), lambda j, i: (j, 0)),
                   pl.BlockSpec((None, 2, FF_SLAB, D), lambda j, i: (j, 0, 0, 0)),
                   pl.BlockSpec((None, 2, FFN_CONV, FF_SLAB), lambda j, i: (j, 0, 0, 0))],
        out_shape=[_sds((FF_PAIRS, 2, S, FF_SLAB), BF16), _sds((D_FF, D), BF16), _sds((FF_PAIRS, 2, FF_SLAB, D), BF16),
                   _sds((FF_PAIRS, 2, FFN_CONV, FF_SLAB))],
        scratch_shapes=[pltpu.VMEM((FF_SLAB, D), F32), pltpu.VMEM((FF_SLAB, D), F32), pltpu.VMEM((FF_SLAB, D), F32),
                        pltpu.VMEM((8, FF_SLAB), F32), pltpu.VMEM((8, FF_SLAB), F32)],
        compiler_params=_params(("parallel", "arbitrary")),
    )(dx2, dx2, h2, ug, ug, ug, uu, uu, uu, conv_w, conv_w, w_down)


def _pair_slot(p):
    return 2 * (p & (FF_PAIRS - 1)) + (p >> 2)


def _mm_slabs(a, w, *, name, res=None, norm_bwd=None, after=(), tm=1024, tn=1024):
    nk, S, _ = a.shape
    D = w.shape[2]
    has_res = res is not None
    has_norm = norm_bwd is not None
    assert not has_norm or tn == D

    def body(*refs):
        a_ref, w_ref = refs[:2]
        r_ref = refs[2] if has_res else None
        if has_norm:
            x_ref, nw_ref, skip_ref = refs[2 + has_res:5 + has_res]
            o_ref, dw_ref, acc_ref = refs[-3:]
        else:
            o_ref, acc_ref = refs[-2:]
        i, k = pl.program_id(0), pl.program_id(2)
        part = jnp.dot(a_ref[...], w_ref[...], preferred_element_type=F32)

        @pl.when(k == 0)
        def _():
            acc_ref[...] = part

        @pl.when(k > 0)
        def _():
            acc_ref[...] += part

        @pl.when(k == nk - 1)
        def _():
            r = acc_ref[...] + r_ref[...] if has_res else acc_ref[...]
            if has_norm:
                dx, dw = _rms_bwd_rows(r, x_ref[...], nw_ref[...])
                o_ref[...] = skip_ref[...] + dx

                @pl.when(i == 0)
                def _():
                    dw_ref[...] = dw

                @pl.when(i > 0)
                def _():
                    dw_ref[...] += dw
            else:
                o_ref[...] = r

    o_spec = pl.BlockSpec((tm, tn), lambda i, j, k: (i, j))
    one = pl.BlockSpec((1, tn), lambda i, j, k: (0, 0))
    return pl.pallas_call(
        body, name=name, grid=(S // tm, D // tn, nk),
        in_specs=[pl.BlockSpec((None, tm, FF_SLAB), lambda i, j, k: (k, i, 0)),
                  pl.BlockSpec((None, FF_SLAB, tn), lambda i, j, k: (FF_PAIRS * (k & 1) + (k >> 1), 0, j))] + [o_spec] * has_res
        + ([o_spec, one, o_spec] if has_norm else []) + [ANY] * len(after),
        out_specs=[o_spec, one] if has_norm else o_spec, out_shape=[_sds((S, D)), _sds((1, D))] if has_norm else _sds((S, D)),
        scratch_shapes=[pltpu.VMEM((tm, tn), F32)],
        compiler_params=_params(("arbitrary" if has_norm else "parallel", "parallel", "arbitrary")),
    )(*((a, w) + ((res,) if has_res else ()) + (tuple(norm_bwd) if has_norm else ()) + tuple(after)))


def _local_step(x, tgt, norm1_w, w_land, conv_a, a_log, dt_bias, gnw, norm2_w, final_w, late_weights, emit, start_after=()):
    wgrad = functools.partial(_mm, ta=True, out_dtype=BF16)
    h1, proj_a, proj_z, proj_b = _in_proj(x, norm1_w, w_land, after=start_after, name="in_proj")
    qn, kn, v, gcb, bb = _gdn_prep_fwd(proj_a, conv_a, a_log, dt_bias, name="gdn_prep_fwd")
    uv, wk, at, tmat, wkb, qdb, keb = _gdn_chunk_fwd(qn, kn, v, gcb, bb, name="gdn_chunk_fwd")
    o, u, sp, oab = _gdn_scan_fwd(uv, at, wkb, qdb, keb, gcb, proj_z, gnw, name="gdn_scan_fwd")
    oab, lse = _attn_fwd(proj_b, oab, name="attn_fwd")
    w_out, w_up, conv_f, w_down = late_weights(oab)
    x1, h2 = _out_proj_norm(oab, w_out, x, norm2_w, name="out_proj")
    dx2, dx2_b, d_final, loss, ug, uu = _ffn_fwd(h2, x1, w_up, conv_f, w_down, final_w, tgt, name="ffn_fwd")
    du, g_down, g_up, dcw = _ffn_bwd(dx2_b, h2, ug, uu, conv_f, w_down, name="ffn_bwd")
    token = emit("ffn", w_down=g_down, w_up=g_up.reshape(N_DEV, FF_SLAB, -1), conv_f=dcw.reshape(N_DEV, FFN_CONV, -1))
    dx1, d_norm2 = _mm_slabs(du.reshape(N_DEV, -1, FF_SLAB), w_up, norm_bwd=(x1, norm2_w, dx2), after=token, name="ffn_up_dx")
    d_oab = _mm(dx1, w_out, tb=True, name="out_proj_dx", tn=D_MODEL, tk=1024)
    token = emit("out", w_out=wgrad(oab, dx1, name="out_proj_dw", tm=D_MODEL, tn=D_MODEL))
    dz, d_gnw, du, dwk, dat, dqd, dke, dgl = _gdn_scan_bwd(d_oab, o, proj_z, gnw, sp, u, at, wkb, qdb, keb, gcb, after=token, name="gdn_scan_bwd")
    dqn, dkn, dv, dg, dbeta = _gdn_chunk_bwd(qn, kn, gcb, bb, tmat, uv, wk, du, dwk, dat, dqd, dke, dgl, name="gdn_chunk_bwd")
    dc, dba, d_small = _gdn_prep_bwd(dqn, dkn, dv, dg, dbeta, proj_a, conv_a, a_log, dt_bias, name="gdn_prep_bwd")
    d_pa, d_conv_a = _gdn_conv_bwd(dc, dba, proj_a, conv_a, name="gdn_conv_bwd")
    d_pb = _attn_bwd(proj_b, oab, d_oab, lse, name="attn_bwd")
    g_a = wgrad(d_pa, h1, name="proj_a_dw", tm=A_COLS, tn=D_MODEL)
    g_z = wgrad(dz, h1, name="proj_z_dw", tn=D_MODEL)
    g_b = wgrad(d_pb, h1, name="proj_b_dw", tm=768, tn=D_MODEL)
    token = emit("in", w_a=g_a, w_z=g_z, w_b=g_b, conv_a=d_conv_a)
    grad_x, d_norm1 = _in_proj_dx((d_pa, dz, d_pb), w_land, x, norm1_w, dx1, after=token, name="in_proj_dx")
    small = dict(norm1=d_norm1, small=d_small, gnw=d_gnw, norm2=d_norm2, final=d_final)
    return loss, grad_x, small


_O1 = 3 * GDN_WIDTH
_O2 = _O1 + GDN_WIDTH
_O3 = _O2 + 2 * GDN_HEADS


_W_IN_ROWS = (A_COLS, GDN_WIDTH, 3 * DIL_WIDTH)
_IN_ROWS = (_O3 + 3 * DIL_WIDTH) // N_DEV
_ROW_TILE = 16
_IN_STEP = _IN_ROWS - _IN_ROWS % _ROW_TILE
_GAP = 8
_LAND_ROWS = 464
assert _O3 % _ROW_TILE == _ROW_TILE - _GAP and N_DEV - 1 + _GAP + _IN_ROWS <= _LAND_ROWS and _LAND_ROWS % _ROW_TILE == 0


def _padded_row(r):
    return r + (_GAP if r >= _O3 else 0)


def _shifted_slab(w_rows, j, *, name):
    q = w_rows.shape[0] // _IN_ROWS

    def body(j_ref, w_ref, o_ref, pad_ref):
        pad_ref[...] = jnp.zeros_like(pad_ref)
        for dev in range(N_DEV):
            @pl.when(j_ref[0] == dev)
            def _(dev=dev):
                p = lax.broadcasted_iota(jnp.int32, (_LAND_ROWS, 1), 0) + _IN_STEP * dev
                for s in range(q):
                    pad_ref[0:_IN_ROWS, :] = w_ref[pl.ds(s, _IN_ROWS, stride=q), :]
                    rows = pad_ref[...]
                    a = pltpu.roll(rows, dev, 0) if dev else rows
                    b = pltpu.roll(rows, dev + _GAP, 0)
                    o_ref[:, 128 * s:128 * (s + 1)] = jnp.where(p < _O3, a, jnp.where(p >= _O3 + _GAP, b, 0.0)).astype(BF16)

    vm = pl.BlockSpec(memory_space=pltpu.VMEM)
    return pl.pallas_call(
        body, name=name, in_specs=[pl.BlockSpec(memory_space=pltpu.SMEM), vm], out_specs=vm,
        out_shape=_sds((_LAND_ROWS, 128 * q), BF16), scratch_shapes=[pltpu.VMEM((_LAND_ROWS, 128), F32)],
    )(j, w_rows)


def _w_in_plan():
    def dest(p):
        if p < _O1:
            return 0, p
        if p < _O2:
            return 1, p - _O1
        if p < _O2 + _ROW_TILE:
            return 0, _O1
        q = p - _O3 - _GAP
        t, pair = divmod(q // 128, DIL_PAIRS)
        return 2, (3 * pair + t) * 128 + q % 128

    spans = [(_padded_row(_IN_ROWS * j), _padded_row(_IN_ROWS * (j + 1) - 1) + 1) for j in range(N_DEV)]
    runs, seams = [], []
    for p in range(0, spans[-1][1], _ROW_TILE):
        owners = [j for j, (lo, hi) in enumerate(spans) if lo < p + _ROW_TILE and hi > p]
        w, r = dest(p)
        if len(owners) == 2:
            seams.append((w, r, owners[0], p - _IN_STEP * owners[0], owners[1], p - _IN_STEP * owners[1]))
            continue
        (j,) = owners
        last = runs[-1] if runs else None
        if last and last[0] == j and last[2] == w and last[3] + last[4] == r and last[1] + last[4] == p - _IN_STEP * j:
            runs[-1] = last[:4] + (last[4] + _ROW_TILE,)
        else:
            runs.append((j, p - _IN_STEP * j, w, r, _ROW_TILE))
    return runs, seams


def _w_in_scratch(d):
    return [pltpu.VMEM((n, d), BF16) for n in _W_IN_ROWS] + [pltpu.VMEM((N_DEV - 1, _ROW_TILE, d), BF16),
                                                              pltpu.SemaphoreType.DMA(())]


def _fetch_w_in(land_ref, wa_ref, wz_ref, wb_ref, seam_ref, sem):
    w_refs = (wa_ref, wz_ref, wb_ref)
    runs, seams = _w_in_plan()
    copies = [pltpu.make_async_copy(land_ref.at[j, pl.ds(s, n)], w_refs[w].at[pl.ds(r, n)], sem) for j, s, w, r, n in runs]
    for k, (w, r, j0, s0, j1, s1) in enumerate(seams):
        copies.append(pltpu.make_async_copy(land_ref.at[j0, pl.ds(s0, _ROW_TILE)], w_refs[w].at[pl.ds(r, _ROW_TILE)], sem))
        copies.append(pltpu.make_async_copy(land_ref.at[j1, pl.ds(s1, _ROW_TILE)], seam_ref.at[k], sem))
    for cp in copies:
        cp.start()
    tail = _O1 + _ROW_TILE
    wa_ref[tail:, :] = jnp.zeros((A_COLS - tail, wa_ref.shape[1]), BF16)
    for cp in copies:
        cp.wait()
    for k, (w, r, *_) in enumerate(seams):
        both = w_refs[w][r:r + _ROW_TILE, :].astype(F32) + seam_ref[k].astype(F32)
        w_refs[w][r:r + _ROW_TILE, :] = both.astype(BF16)


MESH = pl.DeviceIdType.MESH
ANY = pl.BlockSpec(memory_space=pl.ANY)


def _position():
    return lax.axis_index("x"), lax.axis_index("y"), lax.axis_index("c")


def _slot(p):
    return 4 * p[0] + 2 * p[1] + p[2]


def _all_gather(blocks, *, name):
    n = len(blocks)

    def body(*refs):
        ins, outs = refs[:n], refs[n:2 * n]
        send_sems, recv_sems, local_sems = refs[2 * n:]
        x, y, c = _position()
        me, sibling = (x, y, c), (x, y, 1 - c)
        chips = [(1 - x, y), (x, 1 - y), (1 - x, 1 - y)]

        def copy(a, k, block, to, src=None):
            dst = outs[a].at[_slot(block)]
            return pltpu.make_async_remote_copy(
                src_ref=dst if src is None else src, dst_ref=dst, send_sem=send_sems.at[a, k], recv_sem=recv_sems.at[a, k],
                device_id=to, device_id_type=MESH)

        mine = [pltpu.make_async_copy(ins[a], outs[a].at[_slot(me)], local_sems.at[a]) for a in range(n)]
        for cp in mine:
            cp.start()
        first = []
        for a in range(n):
            first.append(copy(a, 0, me, sibling, src=ins[a]))
            first += [copy(a, 1 + j, me, (*chip, c), src=ins[a]) for j, chip in enumerate(chips)]
        for cp in first:
            cp.start()
        passed = []
        for j, chip in enumerate(chips):
            for a in range(n):
                copy(a, 1 + j, (*chip, c), me).wait_recv()
                fwd = copy(a, 4 + j, (*chip, c), sibling)
                fwd.start()
                passed.append(fwd)
        for a in range(n):
            copy(a, 0, sibling, me).wait_recv()
            for j, chip in enumerate(chips):
                copy(a, 4 + j, (*chip, 1 - c), me).wait_recv()
        for cp in first + passed:
            cp.wait_send()
        for cp in mine:
            cp.wait()

    return pl.pallas_call(
        body, name=name, in_specs=[ANY] * n, out_specs=[ANY] * n,
        out_shape=[_sds((N_DEV,) + b.shape, b.dtype) for b in blocks],
        scratch_shapes=[pltpu.SemaphoreType.DMA((n, 7)), pltpu.SemaphoreType.DMA((n, 7)), pltpu.SemaphoreType.DMA((n,))],
    )(*blocks)


def _gather_direct(block, *, name, after=()):
    def body(in_ref, *rest):
        out_ref, send_sems, recv_sems, local_sem = rest[len(after):]
        x, y, c = _position()
        me = _slot((x, y, c))
        mine = pltpu.make_async_copy(in_ref, out_ref.at[me], local_sem)
        mine.start()
        copies = [pltpu.make_async_remote_copy(
            src_ref=in_ref, dst_ref=out_ref.at[me], send_sem=send_sems.at[k - 1], recv_sem=recv_sems.at[k - 1],
            device_id=_peer_of(k, x, y, c), device_id_type=MESH) for k in range(1, N_DEV)]
        for cp in copies:
            cp.start()
        for cp in copies:
            cp.wait()
        mine.wait()

    return pl.pallas_call(
        body, name=name, in_specs=[pl.BlockSpec(memory_space=pltpu.VMEM)] + [ANY] * len(after),
        out_specs=pl.BlockSpec(memory_space=pltpu.VMEM),
        out_shape=_sds((N_DEV,) + block.shape, block.dtype),
        scratch_shapes=[pltpu.SemaphoreType.DMA((N_DEV - 1,)), pltpu.SemaphoreType.DMA((N_DEV - 1,)), pltpu.SemaphoreType.DMA],
    )(block, *after)


HBM = pl.BlockSpec(memory_space=pltpu.HBM)
SEM = pl.BlockSpec(memory_space=pltpu.SEMAPHORE)
EFFECT = pltpu.SideEffectType.DATAFLOW_SIDE_EFFECTING


def _peer_of(k, x, y, c):
    return (1 - x if k & 4 else x, 1 - y if k & 2 else y, 1 - c if k & 1 else c)


def _flight(a, k):
    return a * (N_DEV - 1) + k - 1


def _exchange_start(arrays, *, name, broadcast=False, paired=()):
    n = len(arrays)

    def body(*refs):
        ins, lands = refs[:n], refs[n:2 * n]
        send_sems, recv_sems = refs[2 * n:2 * n + 2]
        token = refs[-1]
        x, y, c = _position()
        me = _slot((x, y, c))
        for k in range(1, N_DEV):
            peer = _peer_of(k, x, y, c)
            for a in range(n):
                at = _pair_slot(_slot(peer)) if a in paired else _slot(peer)
                pltpu.make_async_remote_copy(
                    src_ref=ins[a] if broadcast else ins[a].at[at], dst_ref=lands[a].at[me],
                    send_sem=send_sems.at[_flight(a, k)], recv_sem=recv_sems.at[_flight(a, k)],
                    device_id=peer, device_id_type=MESH).start()
        token[...] = jnp.zeros_like(token)

    land_shapes = [((N_DEV,) + s.shape) if broadcast else s.shape for s in arrays]
    lands = [pltpu.with_memory_space_constraint(lax.empty(shp, s.dtype), pltpu.HBM) for shp, s in zip(land_shapes, arrays)]
    srcs = [pltpu.with_memory_space_constraint(s, pltpu.HBM) for s in arrays]
    outs = pl.pallas_call(
        body, name=name, in_specs=[HBM] * (2 * n),
        out_specs=[SEM, SEM] + [HBM] * (2 * n) + [pl.BlockSpec(memory_space=pltpu.VMEM)],
        out_shape=[pltpu.SemaphoreType.DMA((n * (N_DEV - 1),)), pltpu.SemaphoreType.DMA((n * (N_DEV - 1),))]
        + [pltpu.HBM(s.shape, s.dtype) for s in arrays] + [pltpu.HBM(shp, s.dtype) for shp, s in zip(land_shapes, arrays)]
        + [_sds((8, 128))],
        input_output_aliases={i: 2 + i for i in range(2 * n)},
        compiler_params=pltpu.CompilerParams(has_side_effects=EFFECT),
    )(*srcs, *lands)
    return outs[0], outs[1], outs[2:2 + n], outs[2 + n:2 + 2 * n], outs[-1]


def _exchange_wait(send_sems, recv_sems, srcs, lands, after, *, name, broadcast=False):
    n = len(srcs)

    def body(*refs):
        ins, lnd = refs[:n], refs[n:2 * n]
        send_ref, recv_ref = refs[2 * n:2 * n + 2]
        x, y, c = _position()
        for k in range(1, N_DEV):
            for a in range(n):
                cp = pltpu.make_async_remote_copy(
                    src_ref=ins[a] if broadcast else ins[a].at[0], dst_ref=lnd[a].at[0], send_sem=send_ref.at[_flight(a, k)],
                    recv_sem=recv_ref.at[_flight(a, k)], device_id=_peer_of(k, x, y, c), device_id_type=MESH)
                cp.wait_send()
                cp.wait_recv()

    outs = pl.pallas_call(
        body, name=name, in_specs=[HBM] * (2 * n) + [SEM, SEM, ANY], out_specs=[HBM] * (2 * n),
        out_shape=[pltpu.HBM(s.shape, s.dtype) for s in srcs] + [pltpu.HBM(s.shape, s.dtype) for s in lands],
        input_output_aliases={i: i for i in range(2 * n)},
        compiler_params=pltpu.CompilerParams(has_side_effects=EFFECT),
    )(*srcs, *lands, send_sems, recv_sems, after)
    return outs[:n], outs[n:]


_G_LAND_ROWS = 528


def _g_in_pieces(dev):
    runs, seams = _w_in_plan()
    pieces = [(w, r, n, s) for j, s, w, r, n in runs if j == dev]
    pieces += [(w, r, _ROW_TILE, s0) for w, r, j0, s0, j1, s1 in seams if j0 == dev]
    pieces += [(w, r, _ROW_TILE, s1) for w, r, j0, s0, j1, s1 in seams if j1 == dev]
    merged = []
    for w, r, n, s in sorted(pieces, key=lambda p: p[3]):
        if merged and merged[-1][0] == w and merged[-1][1] + merged[-1][2] == r and merged[-1][3] + merged[-1][2] == s:
            merged[-1] = (w, merged[-1][1], merged[-1][2] + n, merged[-1][3])
        else:
            merged.append((w, r, n, s))
    return merged


def _coords(dev):
    return tuple(jnp.int32(v) for v in (dev >> 2, (dev >> 1) & 1, dev & 1))


def _exchange_start_in(g_ws, conv_slabs, *, name):
    srcs = list(g_ws) + [conv_slabs]
    n = len(srcs)

    def body(*refs):
        g_refs, cv_ref, land, land_cv = refs[:n - 1], refs[n - 1], refs[n], refs[n + 1]
        send_sems, recv_sems = refs[n + 2:n + 4]
        token = refs[-1]
        me = _slot(_position())
        pieces = [_g_in_pieces(dev) for dev in range(N_DEV)]
        for i in range(max(len(p) for p in pieces)):
            for dev in range(N_DEV):
                if i < len(pieces[dev]):
                    @pl.when(me != dev)
                    def _(dev=dev, i=i):
                        w, r, rows, s = pieces[dev][i]
                        k = me ^ dev
                        pltpu.make_async_remote_copy(
                            src_ref=g_refs[w].at[pl.ds(r, rows)], dst_ref=land.at[me, pl.ds(s, rows)],
                            send_sem=send_sems.at[_flight(0, k)], recv_sem=recv_sems.at[_flight(0, k)],
                            device_id=_coords(dev), device_id_type=MESH).start()
        for dev in range(N_DEV):
            @pl.when(me != dev)
            def _(dev=dev):
                k = me ^ dev
                pltpu.make_async_remote_copy(
                    src_ref=cv_ref.at[dev], dst_ref=land_cv.at[me], send_sem=send_sems.at[_flight(1, k)],
                    recv_sem=recv_sems.at[_flight(1, k)], device_id=_coords(dev), device_id_type=MESH).start()
        token[...] = jnp.zeros_like(token)

    lands = [lax.empty((N_DEV, _G_LAND_ROWS, g_ws[0].shape[1]), g_ws[0].dtype), lax.empty(conv_slabs.shape, conv_slabs.dtype)]
    ops = [pltpu.with_memory_space_constraint(a, pltpu.HBM) for a in srcs + lands]
    outs = pl.pallas_call(
        body, name=name, in_specs=[HBM] * len(ops),
        out_specs=[SEM, SEM] + [HBM] * len(ops) + [pl.BlockSpec(memory_space=pltpu.VMEM)],
        out_shape=[pltpu.SemaphoreType.DMA((2 * (N_DEV - 1),)), pltpu.SemaphoreType.DMA((2 * (N_DEV - 1),))]
        + [pltpu.HBM(a.shape, a.dtype) for a in ops] + [_sds((8, 128))],
        input_output_aliases={i: 2 + i for i in range(len(ops))},
        compiler_params=pltpu.CompilerParams(has_side_effects=EFFECT),
    )(*ops)
    return outs[0], outs[1], outs[2:2 + n], outs[2 + n:4 + n], outs[-1]


def _own_pieces(g_ws, land, after, *, name):
    n = len(g_ws)

    def body(*refs):
        g_refs, land_ref, sem = refs[:n], refs[n], refs[-1]
        me = _slot(_position())
        for dev in range(N_DEV):
            @pl.when(me == dev)
            def _(dev=dev):
                own = [pltpu.make_async_copy(g_refs[w].at[pl.ds(r, rows)], land_ref.at[dev, pl.ds(s, rows)], sem)
                       for w, r, rows, s in _g_in_pieces(dev)]
                for cp in own:
                    cp.start()
                for cp in own:
                    cp.wait()

    return pl.pallas_call(
        body, name=name, in_specs=[HBM] * (n + 1) + [ANY], out_specs=HBM, out_shape=pltpu.HBM(land.shape, land.dtype),
        input_output_aliases={n: 0}, scratch_shapes=[pltpu.SemaphoreType.DMA(())],
    )(*g_ws, land, after)


def _exchange_wait_in(send_sems, recv_sems, srcs, lands, after, *, name):
    n = len(srcs)

    def body(*refs):
        g_refs, cv_ref, land, land_cv = refs[:n - 1], refs[n - 1], refs[n], refs[n + 1]
        send_ref, recv_ref = refs[n + 2:n + 4]
        me = _slot(_position())

        def copies(dev, k):
            cps = [pltpu.make_async_remote_copy(
                src_ref=g_refs[w].at[pl.ds(r, rows)], dst_ref=land.at[0, pl.ds(s, rows)], send_sem=send_ref.at[_flight(0, k)],
                recv_sem=recv_ref.at[_flight(0, k)], device_id=_coords(dev), device_id_type=MESH)
                for w, r, rows, s in _g_in_pieces(dev)]
            return cps + [pltpu.make_async_remote_copy(
                src_ref=cv_ref.at[0], dst_ref=land_cv.at[0], send_sem=send_ref.at[_flight(1, k)],
                recv_sem=recv_ref.at[_flight(1, k)], device_id=_coords(dev), device_id_type=MESH)]

        for dev in range(N_DEV):
            @pl.when(me != dev)
            def _(dev=dev):
                for cp in copies(dev, me ^ dev):
                    cp.wait_send()

            @pl.when(me == dev)
            def _(dev=dev):
                for k in range(1, N_DEV):
                    for cp in copies(dev, k):
                        cp.wait_recv()

    ops = list(srcs) + list(lands)
    outs = pl.pallas_call(
        body, name=name, in_specs=[HBM] * len(ops) + [SEM, SEM, ANY], out_specs=[HBM] * len(ops),
        out_shape=[pltpu.HBM(a.shape, a.dtype) for a in ops],
        input_output_aliases={i: i for i in range(len(ops))},
        compiler_params=pltpu.CompilerParams(has_side_effects=EFFECT),
    )(*ops, send_sems, recv_sems, after)
    return outs[:n], outs[n:]


def _with_own(landed, srcs, me):
    return [lax.dynamic_update_index_in_dim(l, o, me, 0) for l, o in zip(landed, srcs)]


def _adam_update(g, w, m, v):
    c1 = 1.0 - ADAM_B1 ** ADAM_STEP
    c2 = 1.0 - ADAM_B2 ** ADAM_STEP
    nm = ADAM_B1 * m + (1.0 - ADAM_B1) * g
    nv = ADAM_B2 * v + (1.0 - ADAM_B2) * (g * g)
    return -ADAM_LR * ((nm / c1) / (jnp.sqrt(nv / c2) + ADAM_EPS) + ADAM_WD * w), nm, nv


def _adamw(landed, sent, me, w, m, v, *, name, tr=None, tc=None):
    R, C = w.shape
    tr = R if tr is None else tr
    tc = C if tc is None else tc
    assert R % tr == 0 and C % tc == 0

    def body(me_ref, own_ref, p_ref, w_ref, m_ref, v_ref, g_ref, d_ref, nm_ref, nv_ref, token_ref):
        token_ref[...] = jnp.zeros_like(token_ref)
        g = own_ref[...].astype(F32)
        for s in range(N_DEV):
            g = g + jnp.where(me_ref[1] == s, 0.0, p_ref[s].astype(F32))
        delta, nm, nv = _adam_update(g, w_ref[...], m_ref[...], v_ref[...])
        g_ref[...] = g
        nm_ref[...] = nm
        nv_ref[...] = nv
        d_ref[...] = delta

    blk = pl.BlockSpec((tr, tc), lambda i, j, me_ref: (i, j))
    return pl.pallas_call(
        body, name=name,
        grid_spec=pltpu.PrefetchScalarGridSpec(
            num_scalar_prefetch=1, grid=(R // tr, C // tc),
            in_specs=[pl.BlockSpec((None, tr, tc), lambda i, j, me_ref: (me_ref[0], i, j)),
                      pl.BlockSpec((N_DEV, tr, tc), lambda i, j, me_ref: (0, i, j)), blk, blk, blk],
            out_specs=[blk] * 4 + [pl.BlockSpec((8, 128), lambda i, j, me_ref: (0, 0))]),
        out_shape=[_sds((R, C))] * 4 + [_sds((8, 128))],
        compiler_params=_params(("arbitrary", "arbitrary")),
    )(me, sent, landed, w, m, v)


def _adamw_rowwise(landed, me, w, m, v, *, name, tr=128):
    C = landed.shape[2]
    R, q, extra = _IN_ROWS, C // 128, _ROW_TILE
    assert tr % extra == 0 and (pl.cdiv(R, tr) * tr + extra) <= landed.shape[1] and N_DEV - 1 + _GAP < extra

    def body(me_ref, a_ref, b_ref, w_ref, m_ref, v_ref, g_ref, d_ref, nm_ref, nv_ref, g_scr):
        i = pl.program_id(0)
        total = lambda ref: functools.reduce(lambda x, y: x + y, [ref[s].astype(F32) for s in range(N_DEV)])
        slab = jnp.concatenate([total(a_ref), total(b_ref)], axis=0)
        for dev in range(N_DEV):
            @pl.when(me_ref[0] == dev)
            def _(dev=dev):
                lo, hi = slab[dev:dev + tr], slab[dev + _GAP:dev + _GAP + tr]
                if _IN_ROWS * (dev + 1) <= _O3:
                    g_scr[...] = lo
                elif _IN_ROWS * dev >= _O3:
                    g_scr[...] = hi
                else:
                    r = _IN_ROWS * dev + tr * i + lax.broadcasted_iota(jnp.int32, (tr, 1), 0)
                    g_scr[...] = jnp.where(r < _O3, lo, hi)
        g = g_scr[...]
        for s in range(q):
            rows = pl.ds(s, tr, stride=q)
            gs = g[:, 128 * s:128 * (s + 1)]
            delta, nm, nv = _adam_update(gs, w_ref[rows, :], m_ref[rows, :], v_ref[rows, :])
            g_ref[rows, :] = gs
            nm_ref[rows, :] = nm
            nv_ref[rows, :] = nv
            d_ref[rows, :] = delta

    blk = pl.BlockSpec((tr * q, 128), lambda i, me_ref: (i, 0))
    return pl.pallas_call(
        body, name=name,
        grid_spec=pltpu.PrefetchScalarGridSpec(
            num_scalar_prefetch=1, grid=(pl.cdiv(R, tr),),
            in_specs=[pl.BlockSpec((N_DEV, tr, C), lambda i, me_ref: (0, i, 0)),
                      pl.BlockSpec((N_DEV, extra, C), lambda i, me_ref: (0, (tr // extra) * (i + 1), 0)), blk, blk, blk],
            out_specs=[blk] * 4, scratch_shapes=[pltpu.VMEM((tr, C), F32)]),
        out_shape=[_sds((R * q, 128))] * 4,
        compiler_params=_params(("arbitrary",)),
    )(me, landed, landed, w, m, v)


_SMALL_ROWS = 8
_SMALL_SLOTS = ((0, 0, D_MODEL), (1, 0, D_MODEL), (2, 0, D_MODEL), (3, 0, GDN_DIM), (3, GDN_DIM, GDN_HEADS),
                (3, GDN_DIM + GDN_HEADS, GDN_HEADS))
_LOSS_LANE = 2 * GDN_DIM


def _pack_small(norm1, norm2, final, gnw, a_log, dt_bias, loss):
    row3 = jnp.concatenate([gnw, a_log, dt_bias, jnp.zeros((1, 128 - 2 * GDN_HEADS), F32), loss,
                            jnp.zeros((1, D_MODEL - 3 * 128), F32)], axis=1)
    return jnp.concatenate([norm1, norm2, final, row3, jnp.zeros((_SMALL_ROWS - 4, D_MODEL), F32)], axis=0)


def _adamw_small(packs, ws, ms, vs, *, name):
    n = len(ws)

    def body(p_ref, *refs):
        w_refs, m_refs, v_refs = refs[:n], refs[n:2 * n], refs[2 * n:3 * n]
        outs = refs[3 * n:]
        g_all = p_ref[0]
        for s in range(1, N_DEV):
            g_all = g_all + p_ref[s]
        for i, (row, lane, width) in enumerate(_SMALL_SLOTS):
            g = g_all[row:row + 1, lane:lane + width]
            delta, nm, nv = _adam_update(g, w_refs[i][...], m_refs[i][...], v_refs[i][...])
            for o_ref, val in zip(outs[4 * i:4 * i + 4], (g, delta, nm, nv)):
                o_ref[...] = val
        outs[-1][...] = g_all[3:4, _LOSS_LANE:_LOSS_LANE + 128]

    vm = pl.BlockSpec(memory_space=pltpu.VMEM)
    outs = pl.pallas_call(
        body, name=name, in_specs=[vm] * (1 + 3 * n), out_specs=[vm] * (4 * n + 1),
        out_shape=[_sds(w.shape) for w in ws for _ in range(4)] + [_sds((1, 128))],
    )(packs, *ws, *ms, *vs)
    return [outs[4 * i:4 * i + 4] for i in range(n)], outs[-1]


def _slabs_by_cols(g):
    r = g.shape[0]
    return g.reshape(r, N_DEV, -1).transpose(1, 0, 2)


def _cols_from_slabs(s):
    return s.transpose(1, 0, 2).reshape(s.shape[1], -1)


def kernel(x, norm1_w, w_in, conv_qkv_w, a_log, dt_bias, gdn_norm_w, w_out, norm2_w, w_up, ffn_conv_w, w_down, final_norm_w, loss_target, m_norm1_w, m_w_in, m_conv_qkv_w, m_a_log, m_dt_bias, m_gdn_norm_w, m_w_out, m_norm2_w, m_w_up, m_ffn_conv_w, m_w_down, m_final_norm_w, v_norm1_w, v_w_in, v_conv_qkv_w, v_a_log, v_dt_bias, v_gdn_norm_w, v_w_out, v_norm2_w, v_w_up, v_ffn_conv_w, v_w_down, v_final_norm_w):
    bf = lambda a: a.astype(BF16)
    me = _slot(_position())
    me1 = jnp.reshape(me, (1,)).astype(jnp.int32)
    t_in = lambda a: a[0].T
    rows = lambda a: a.reshape(D_MODEL // 128, 128, -1).transpose(2, 0, 1).reshape(-1, 128)
    gw_in, g_conv_a = _all_gather([_shifted_slab(rows(w_in), me1, name="shift_w_in"), conv_qkv_w[0]], name="gather_w_in")
    late_src, _ = lax.optimization_barrier(([bf(w_out[0]), bf(t_in(w_up)), bf(w_down[0]), ffn_conv_w[0]], gw_in))
    l_send, l_recv, l_srcs, l_lands, l_token = _exchange_start(late_src, name="weights_start", broadcast=True)

    def late_weights(after):
        srcs, landed = _exchange_wait(l_send, l_recv, l_srcs, l_lands, after, name="weights_wait", broadcast=True)
        gw_out, gw_up, gw_down, g_conv_f = _with_own(landed, srcs, me)
        return gw_out.reshape(D_MODEL, D_MODEL), gw_up, g_conv_f, gw_down.reshape(D_FF, D_MODEL)

    flights = {}

    def emit(group, **grads):
        paired = ()
        if group == "in":
            *flight, token = _exchange_start_in([grads["w_a"], grads["w_z"], grads["w_b"]], _slabs_by_cols(grads["conv_a"]),
                                                name="grads_start_in")
            flights[group] = flight
            return (token,)
        if group == "ffn":
            slabs = dict(w_down=grads["w_down"].reshape(N_DEV, -1, D_MODEL), w_up=grads["w_up"], conv_f=grads["conv_f"])
            paired = (1, 2)
        else:
            slabs = {k: v.reshape(N_DEV, -1, D_MODEL) for k, v in grads.items()}
        names = list(slabs)
        *flight, token = _exchange_start([slabs[k] for k in names], paired=paired, name="grads_start_" + group)
        flights[group] = (names, flight)
        return (token,)

    loss, grad_x, g = _local_step(
        x[0], loss_target[0], norm1_w, gw_in, _cols_from_slabs(g_conv_a), a_log, dt_bias,
        gdn_norm_w, norm2_w, final_norm_w[None], late_weights, emit, start_after=(l_token,))
    got = {}

    def collect(group, after):
        names, (send_sems, recv_sems, srcs, lands) = flights[group]
        srcs, landed = _exchange_wait(send_sems, recv_sems, srcs, lands, after, name="grads_wait_" + group)
        got.update(zip(names, zip(landed, srcs)))

    def update(key, w, m, v, paired=False, **tiles):
        where = jnp.concatenate([_pair_slot(me1) if paired else me1, me1])
        return _adamw(*got[key], where, w, m, v, name="adamw_" + key, **tiles)

    collect("ffn", grad_x)
    collect("out", grad_x)
    *o_out, t1 = update("w_out", w_out[0], m_w_out[0], v_w_out[0])
    *o_up, t2 = update("w_up", t_in(w_up), t_in(m_w_up), t_in(v_w_up), paired=True, tr=176)
    o_up = [o.T for o in o_up]
    *o_down, t3 = update("w_down", w_down[0], m_w_down[0], v_w_down[0], tr=176)
    *o_cf, t4 = update("conv_f", ffn_conv_w[0], m_ffn_conv_w[0], v_ffn_conv_w[0], paired=True)
    pack = _pack_small(g["norm1"], g["norm2"], g["final"], g["gnw"], g["small"][:, 0:GDN_HEADS],
                       g["small"][:, GDN_HEADS:2 * GDN_HEADS], loss)
    small_all = _gather_direct(pack, after=(t1, t2, t3, t4), name="gather_small")
    send_sems, recv_sems, srcs, lands = flights["in"]
    lands = [_own_pieces(srcs[:3], lands[0], grad_x, name="grads_own_in"), lands[1]]
    srcs, (g_land, conv_land) = _exchange_wait_in(send_sems, recv_sems, srcs, lands, small_all, name="grads_wait_in")
    got["conv_a"] = (conv_land, srcs[-1])
    o_in = [o.reshape(-1, D_MODEL // 128, 128).transpose(1, 2, 0).reshape(D_MODEL, -1) for o in _adamw_rowwise(
        g_land, me1, rows(w_in), rows(m_w_in), rows(v_w_in), name="adamw_w_in")]
    o_ca = update("conv_a", conv_qkv_w[0], m_conv_qkv_w[0], v_conv_qkv_w[0])
    (o_n1, o_n2, o_fin, o_gn, o_al, o_dt), total = _adamw_small(
        small_all, (norm1_w, norm2_w, final_norm_w[None], gdn_norm_w, a_log, dt_bias),
        (m_norm1_w, m_norm2_w, m_final_norm_w[None], m_gdn_norm_w, m_a_log, m_dt_bias),
        (v_norm1_w, v_norm2_w, v_final_norm_w[None], v_gdn_norm_w, v_a_log, v_dt_bias), name="adamw_small")
    outs = [total[0, 0], grad_x[None]]
    for k in range(4):
        outs += [o_n1[k], o_in[k][None], o_ca[k][None], o_al[k], o_dt[k], o_gn[k], o_out[k][None], o_n2[k], o_up[k][None],
                 o_cf[k][None], o_down[k][None], o_fin[k][0]]
    return tuple(outs)
```

```python
import functools

import jax
import jax.numpy as jnp
from jax import lax
from jax.experimental import pallas as pl
from jax.experimental.pallas import tpu as pltpu

F32 = jnp.float32
BF16 = jnp.bfloat16

N_DEV = 8
D_MODEL = 1024
GDN_HEADS = 4
GDN_DIM = 128
GDN_WIDTH = GDN_HEADS * GDN_DIM
GDN_CONV = 4
CHUNK = 64
CHUNKS_PER_STEP = 4
DIL_HEADS = 8
DIL_DIM = 64
DIL_WIDTH = DIL_HEADS * DIL_DIM
DIL_PAIRS = DIL_HEADS // 2
DILATIONS = (1, 4, 16)
BAND = 128
D_FF = 2816
FFN_CONV = 3
EPS = 1e-6
A_COLS = 3 * GDN_WIDTH + 128
HALO = 8

ADAM_LR = 0.001
ADAM_B1 = 0.9
ADAM_B2 = 0.999
ADAM_EPS = 1e-08
ADAM_WD = 0.01
ADAM_STEP = 10

VMEM_LIMIT_BYTES = 56 * 1024 * 1024
NEG_BIG = -1e30


def _params(sem=None):
    return pltpu.CompilerParams(dimension_semantics=sem, vmem_limit_bytes=VMEM_LIMIT_BYTES)


def _sds(shape, dtype=F32):
    return jax.ShapeDtypeStruct(shape, dtype)


def _bdot(a, b):
    return jnp.dot(a.astype(BF16), b.astype(BF16), preferred_element_type=F32)


def _bdot_nt(a, b):
    return lax.dot_general(a.astype(BF16), b.astype(BF16), (((1,), (1,)), ((), ())), preferred_element_type=F32)


def _bdot_tn(a, b):
    return lax.dot_general(a.astype(BF16), b.astype(BF16), (((0,), (0,)), ((), ())), preferred_element_type=F32)


def _split(a):
    hi = a.astype(BF16)
    lo = (a - hi.astype(F32)).astype(BF16)
    return hi, lo


def _dot3(a, b, dims):
    ah, al = _split(a)
    bh, bl = _split(b)
    d = functools.partial(lax.dot_general, dimension_numbers=(dims, ((), ())), preferred_element_type=F32)
    return d(ah, bh) + (d(al, bh) + d(ah, bl))


def _exact_tri_dot(tri, g):
    g1 = g.astype(BF16)
    r1 = g - g1.astype(F32)
    g2 = r1.astype(BF16)
    g3 = (r1 - g2.astype(F32)).astype(BF16)
    t = tri.astype(BF16)
    d = functools.partial(jnp.dot, preferred_element_type=F32)
    return d(t, g1) + (d(t, g2) + d(t, g3))


def _sigmoid(x):
    return 1.0 / (1.0 + jnp.exp(-x))


def _dsilu(x, sg):
    return sg * (1.0 + x * (1.0 - sg))


def _rms_bwd_rows(dh, x, w):
    r = lax.rsqrt(jnp.mean(x * x, axis=-1, keepdims=True) + EPS)
    xh = x * r
    gw = dh * w
    return r * (gw - xh * jnp.mean(gw * xh, axis=-1, keepdims=True)), jnp.sum(dh * xh, axis=0, keepdims=True)


def _mm(a, b, *, name, ta=False, tb=False, res=None, norm_bwd=None, after=(), out_dtype=F32, tm=512, tn=512, tk=512):
    if ta:
        K, M = a.shape
    else:
        M, K = a.shape
    if tb:
        N, Kb = b.shape
    else:
        Kb, N = b.shape
    assert K == Kb, (a.shape, b.shape)
    tm, tn, tk = min(tm, M), min(tn, N), min(tk, K)
    assert M % tm == 0 and N % tn == 0 and K % tk == 0, (name, M, N, K, tm, tn, tk)
    nk = K // tk
    dims = (((0 if ta else 1,), (1 if tb else 0,)), ((), ()))
    has_res = res is not None
    has_norm = norm_bwd is not None
    assert not has_norm or tn == N

    def body(*refs):
        a_ref, b_ref = refs[:2]
        r_ref = refs[2] if has_res else None
        if has_norm:
            x_ref, w_ref, skip_ref = refs[2 + has_res:5 + has_res]
            o_ref, dw_ref, acc_ref = refs[-3:]
        else:
            o_ref, acc_ref = refs[-2:]
        i, k = pl.program_id(0), pl.program_id(2)
        part = lax.dot_general(a_ref[...].astype(BF16), b_ref[...].astype(BF16), dims, preferred_element_type=F32)

        @pl.when(k == 0)
        def _():
            acc_ref[...] = part

        @pl.when(k > 0)
        def _():
            acc_ref[...] += part

        @pl.when(k == nk - 1)
        def _():
            r = acc_ref[...]
            if has_res:
                r = r + r_ref[...]
            if has_norm:
                dx, dw = _rms_bwd_rows(r, x_ref[...], w_ref[...])
                o_ref[...] = skip_ref[...] + dx

                @pl.when(i == 0)
                def _():
                    dw_ref[...] = dw

                @pl.when(i > 0)
                def _():
                    dw_ref[...] += dw
            else:
                o_ref[...] = r.astype(out_dtype)

    a_spec = pl.BlockSpec((tk, tm), lambda i, j, k: (k, i)) if ta else pl.BlockSpec((tm, tk), lambda i, j, k: (i, k))
    b_spec = pl.BlockSpec((tn, tk), lambda i, j, k: (j, k)) if tb else pl.BlockSpec((tk, tn), lambda i, j, k: (k, j))
    o_spec = pl.BlockSpec((tm, tn), lambda i, j, k: (i, j))
    one = pl.BlockSpec((1, tn), lambda i, j, k: (0, 0))
    in_specs = [a_spec, b_spec] + [o_spec] * has_res + ([o_spec, one, o_spec] if has_norm else []) + [ANY] * len(after)
    args = (a, b) + ((res,) if has_res else ()) + (tuple(norm_bwd) if has_norm else ()) + tuple(after)
    return pl.pallas_call(
        body, name=name, grid=(M // tm, N // tn, nk), in_specs=in_specs,
        out_specs=[o_spec, one] if has_norm else o_spec,
        out_shape=[_sds((M, N)), _sds((1, N))] if has_norm else _sds((M, N), out_dtype),
        scratch_shapes=[pltpu.VMEM((tm, tn), F32)],
        compiler_params=_params(("arbitrary" if has_norm else "parallel", "parallel", "arbitrary")),
    )(*args)


def _in_proj(x, norm_w, w_land, *, name, after=(), tm=512):
    S, D = x.shape

    def body(x_ref, nw_ref, land_ref, *rest):
        h_ref, pa_ref, pz_ref, pb_ref, *scratch = rest[len(after):]

        @pl.when(pl.program_id(0) == 0)
        def _():
            _fetch_w_in(land_ref, *scratch)

        xv = x_ref[...]
        r = lax.rsqrt(jnp.mean(xv * xv, axis=-1, keepdims=True) + EPS)
        h = (xv * r * nw_ref[...]).astype(BF16)
        h_ref[...] = h
        for w_ref, p_ref in zip(scratch[:3], (pa_ref, pz_ref, pb_ref)):
            p_ref[...] = lax.dot_general(h, w_ref[...], (((1,), (1,)), ((), ())), preferred_element_type=F32)

    row = lambda n: pl.BlockSpec((tm, n), lambda i: (i, 0))
    full = lambda a: pl.BlockSpec(a.shape, lambda i: (0, 0))
    return pl.pallas_call(
        body, name=name, grid=(S // tm,), in_specs=[row(D), full(norm_w), ANY] + [ANY] * len(after),
        out_specs=[row(D)] + [row(n) for n in _W_IN_ROWS],
        out_shape=[_sds((S, D), BF16)] + [_sds((S, n)) for n in _W_IN_ROWS],
        scratch_shapes=_w_in_scratch(D), compiler_params=_params(("arbitrary",)),
    )(x, norm_w, w_land, *after)


def _in_proj_dx(ds, w_land, x, norm_w, skip, *, name, after=(), tm=512):
    S, D = x.shape
    n = len(ds)

    def body(*refs):
        d_refs, land_ref = refs[:n], refs[n]
        x_ref, nw_ref, skip_ref = refs[n + 1:n + 4]
        o_ref, dw_ref, *scratch = refs[n + 4 + len(after):]
        w_refs = scratch[:n]
        i = pl.program_id(0)

        @pl.when(i == 0)
        def _():
            _fetch_w_in(land_ref, *scratch)

        dh = jnp.dot(d_refs[0][...], w_refs[0][...], preferred_element_type=F32)
        for d_ref, w_ref in zip(d_refs[1:], w_refs[1:]):
            dh = dh + jnp.dot(d_ref[...], w_ref[...], preferred_element_type=F32)
        dx, dw = _rms_bwd_rows(dh, x_ref[...], nw_ref[...])
        o_ref[...] = skip_ref[...] + dx

        @pl.when(i == 0)
        def _():
            dw_ref[...] = dw

        @pl.when(i > 0)
        def _():
            dw_ref[...] += dw

    row = lambda c: pl.BlockSpec((tm, c), lambda i: (i, 0))
    full = lambda a: pl.BlockSpec(a.shape, lambda i: (0, 0))
    return pl.pallas_call(
        body, name=name, grid=(S // tm,),
        in_specs=[row(d.shape[1]) for d in ds] + [ANY, row(D), full(norm_w), row(D)] + [ANY] * len(after),
        out_specs=[row(D), pl.BlockSpec((1, D), lambda i: (0, 0))], out_shape=[_sds((S, D)), _sds((1, D))],
        scratch_shapes=_w_in_scratch(D), compiler_params=_params(("arbitrary",)),
    )(*ds, w_land, x, norm_w, skip, *after)


def _out_proj_norm(a, w, x, norm_w, *, name, tm=512):
    S, D = x.shape

    def body(a_ref, w_ref, x_ref, nw_ref, x1_ref, h_ref):
        x1 = x_ref[...] + jnp.dot(a_ref[...], w_ref[...], preferred_element_type=F32)
        x1_ref[...] = x1
        r = lax.rsqrt(jnp.mean(x1 * x1, axis=-1, keepdims=True) + EPS)
        h_ref[...] = (x1 * r * nw_ref[...]).astype(BF16)

    row = pl.BlockSpec((tm, D), lambda i: (i, 0))
    return pl.pallas_call(
        body, name=name, grid=(S // tm,),
        in_specs=[pl.BlockSpec((tm, a.shape[1]), lambda i: (i, 0)), pl.BlockSpec(w.shape, lambda i: (0, 0)), row,
                  pl.BlockSpec((1, D), lambda i: (0, 0))],
        out_specs=[row, row], out_shape=[_sds((S, D)), _sds((S, D), BF16)], compiler_params=_params(("parallel",)),
    )(a, w, x, norm_w)


def _shifted(x, start, n):
    aligned = -(-start // HALO) * HALO
    assert aligned + n <= x.shape[0], (start, n, x.shape)
    return (x if aligned == start else pltpu.roll(x, aligned - start, axis=0))[aligned:aligned + n]


def _conv_rows(prev, cur, w, taps):
    n = cur.shape[0]
    xs = jnp.concatenate([prev, cur], axis=0)
    base = HALO - (taps - 1)
    out = _shifted(xs, base, n) * w[0:1]
    for i in range(1, taps):
        out = out + _shifted(xs, base + i, n) * w[i:i + 1]
    return out


def _conv_rows_bwd(cur_d, next_d, prev_x, cur_x, w, taps):
    n = cur_d.shape[0]
    ds = jnp.concatenate([cur_d, next_d], axis=0)
    dx = _shifted(ds, taps - 1, n) * w[0:1]
    for i in range(1, taps):
        dx = dx + _shifted(ds, taps - 1 - i, n) * w[i:i + 1]
    xs = jnp.concatenate([prev_x, cur_x], axis=0)
    base = HALO - (taps - 1)
    dws = [jnp.sum(cur_d * _shifted(xs, base + i, n), axis=0, keepdims=True) for i in range(taps)]
    return dx, jnp.concatenate(dws, axis=0)


def _halo_specs(tm, width, col, nblk):
    per = tm // HALO
    prev = pl.BlockSpec((HALO, width), lambda i, *_: (jnp.maximum(i * per - 1, 0), col))
    nxt = pl.BlockSpec((HALO, width), lambda i, *_: (jnp.minimum((i + 1) * per, nblk * per - 1), col))
    return prev, nxt


def _softplus(x):
    return jnp.maximum(x, 0.0) + jnp.log1p(jnp.exp(-jnp.abs(x)))


def _chunk_tri(tm, upper=False):
    r = lax.broadcasted_iota(jnp.int32, (tm, tm), 0)
    c = lax.broadcasted_iota(jnp.int32, (tm, tm), 1)
    same = lax.div(r, CHUNK) == lax.div(c, CHUNK)
    order = (c >= r) if upper else (c <= r)
    return jnp.where(same & order, 1.0, 0.0)


def _gdn_prep_fwd(proj_a, conv_w, a_log, dt_bias, *, name, tm=256):
    S = proj_a.shape[0]
    nblk = S // tm
    W3 = 3 * GDN_WIDTH

    def body(cur_ref, prev_ref, ba_ref, cw_ref, al_ref, dt_ref, qn_ref, kn_ref, v_ref, gcb_ref, bb_ref):
        i = pl.program_id(0)
        prev = jnp.where(i > 0, prev_ref[...], 0.0)
        c = _conv_rows(prev, cur_ref[...], cw_ref[...], GDN_CONV)
        a = c * _sigmoid(c)
        ba = ba_ref[...]
        lane = lax.broadcasted_iota(jnp.int32, (tm, 128), 1)
        g4 = jnp.zeros((tm, 128), F32)
        for h in range(GDN_HEADS):
            sl = slice(GDN_DIM * h, GDN_DIM * (h + 1))
            qh = a[:, GDN_DIM * h:GDN_DIM * (h + 1)]
            kh = a[:, GDN_WIDTH + GDN_DIM * h:GDN_WIDTH + GDN_DIM * (h + 1)]
            qn_ref[:, sl] = qh * (lax.rsqrt(jnp.sum(qh * qh, axis=-1, keepdims=True) + EPS) * (GDN_DIM ** -0.5))
            kn_ref[:, sl] = kh * lax.rsqrt(jnp.sum(kh * kh, axis=-1, keepdims=True) + EPS)
            beta = _sigmoid(ba[:, h:h + 1])
            bb_ref[:, sl] = jnp.broadcast_to(beta, (tm, GDN_DIM))
            g = -jnp.exp(al_ref[0:1, h:h + 1]) * _softplus(ba[:, GDN_HEADS + h:GDN_HEADS + h + 1] + dt_ref[0:1, h:h + 1])
            g4 = jnp.where(lane == h, g, g4)
        v_ref[...] = a[:, 2 * GDN_WIDTH:]
        gc = _exact_tri_dot(_chunk_tri(tm), g4)
        for h in range(GDN_HEADS):
            gcb_ref[:, GDN_DIM * h:GDN_DIM * (h + 1)] = jnp.broadcast_to(gc[:, h:h + 1], (tm, GDN_DIM))

    prev_spec, _ = _halo_specs(tm, W3, 0, nblk)
    row = pl.BlockSpec((tm, GDN_WIDTH), lambda i: (i, 0))
    small = lambda a: pl.BlockSpec(a.shape, lambda i: (0, 0))
    return pl.pallas_call(
        body, name=name, grid=(nblk,),
        in_specs=[pl.BlockSpec((tm, W3), lambda i: (i, 0)), prev_spec,
                  pl.BlockSpec((tm, 128), lambda i: (i, W3 // 128)), small(conv_w), small(a_log), small(dt_bias)],
        out_specs=[row] * 5, out_shape=[_sds((S, GDN_WIDTH))] * 5, compiler_params=_params(("parallel",)),
    )(proj_a, proj_a, proj_a, conv_w, a_log, dt_bias)


GDN_STACK = GDN_HEADS * CHUNK


def _stack(ref, rows):
    return jnp.concatenate([ref[rows, GDN_DIM * h:GDN_DIM * (h + 1)] for h in range(GDN_HEADS)], axis=0)


def _unstack_to(ref, rows, x):
    for h in range(GDN_HEADS):
        ref[rows, GDN_DIM * h:GDN_DIM * (h + 1)] = x[CHUNK * h:CHUNK * (h + 1)].astype(ref.dtype)


def _stack_masks():
    r = lax.broadcasted_iota(jnp.int32, (GDN_STACK, GDN_STACK), 0)
    c = lax.broadcasted_iota(jnp.int32, (GDN_STACK, GDN_STACK), 1)
    same = (r & -CHUNK) == (c & -CHUNK)
    return same & (r >= c), same & (r > c), r == c


def _stack_decay(gs, bs, incl):
    g2 = jnp.concatenate([gs, gs], axis=1)
    diff = g2 - g2.T
    dec = jnp.where(incl, jnp.exp(jnp.where(incl, diff, 0.0)), 0.0)
    return dec, jnp.concatenate([bs, bs], axis=1).T


def _head_mask():
    r = lax.broadcasted_iota(jnp.int32, (GDN_STACK, GDN_WIDTH), 0)
    c = lax.broadcasted_iota(jnp.int32, (GDN_STACK, GDN_WIDTH), 1)
    return (r & -CHUNK) * (GDN_DIM // CHUNK) == (c & -GDN_DIM)


def _head_spread(x):
    return jnp.where(_head_mask(), jnp.concatenate([x] * GDN_HEADS, axis=1), 0.0)


def _head_diag(x):
    xm = jnp.where(_head_mask(), x, 0.0)
    out = xm[:, 0:GDN_DIM]
    for h in range(1, GDN_HEADS):
        out = out + xm[:, GDN_DIM * h:GDN_DIM * (h + 1)]
    return out


def _last_rows(gs, n):
    return jnp.concatenate([jnp.broadcast_to(gs[CHUNK * (h + 1) - 1:CHUNK * (h + 1)], (n, GDN_DIM)) for h in range(GDN_HEADS)], axis=0)


def _gdn_chunk_fwd(qn, kn, v, gcb, bb, *, name):
    S = qn.shape[0]

    def body(qn_ref, kn_ref, v_ref, gcb_ref, bb_ref, uv_ref, wk_ref, at_ref, t_ref, wkb_ref, qdb_ref, keb_ref):
        incl, strict, diag = _stack_masks()
        for c in range(CHUNKS_PER_STEP):
            rows = slice(CHUNK * c, CHUNK * (c + 1))
            srows = slice(GDN_STACK * c, GDN_STACK * (c + 1))
            q, k, vv, gs, bs = [_stack(r, rows) for r in (qn_ref, kn_ref, v_ref, gcb_ref, bb_ref)]
            dec, bt = _stack_decay(gs, bs, incl)
            p = -jnp.where(strict, dec * _bdot_nt(k, k) * bt, 0.0)
            t = jnp.where(diag, 1.0, 0.0) + p
            for _ in range(5):
                p = _bdot(p, p)
                t = t + _bdot(t, p)
            sol = _dot3(t, jnp.concatenate([vv, jnp.exp(gs) * k], axis=1), ((1,), (0,)))
            _unstack_to(uv_ref, rows, sol[:, :GDN_DIM])
            _unstack_to(wk_ref, rows, sol[:, GDN_DIM:])
            at_ref[srows, :] = dec * _bdot_nt(q, k) * bt
            t_ref[srows, :] = t
            wkb_ref[srows, :] = _head_spread(sol[:, GDN_DIM:]).astype(BF16)
            qdb_ref[srows, :] = _head_spread(q * jnp.exp(gs)).astype(BF16)
            keb_ref[srows, :] = _head_spread(k * jnp.exp(_last_rows(gs, CHUNK) - gs) * bs).astype(BF16)

    step = CHUNKS_PER_STEP * CHUNK
    row = pl.BlockSpec((step, GDN_WIDTH), lambda n: (n, 0))
    sq = pl.BlockSpec((CHUNKS_PER_STEP * GDN_STACK, GDN_STACK), lambda n: (n, 0))
    wide = pl.BlockSpec((CHUNKS_PER_STEP * GDN_STACK, GDN_WIDTH), lambda n: (n, 0))
    nsq = S // CHUNK * GDN_STACK
    return pl.pallas_call(
        body, name=name, grid=(S // step,), in_specs=[row] * 5, out_specs=[row, row, sq, sq, wide, wide, wide],
        out_shape=[_sds((S, GDN_WIDTH)), _sds((S, GDN_WIDTH)), _sds((nsq, GDN_STACK)), _sds((nsq, GDN_STACK))]
        + [_sds((nsq, GDN_WIDTH), BF16)] * 3,
        compiler_params=_params(("parallel",)),
    )(qn, kn, v, gcb, bb)


SCAN_CHUNKS = 8


def _gdn_scan_fwd(uv, at, wkb, qdb, keb, gcb, proj_z, gnw, *, name):
    S = uv.shape[0]
    nc = S // CHUNK

    def body(uv_ref, at_ref, wkb_ref, qdb_ref, keb_ref, gcb_ref, z_ref, gnw_ref, o_ref, u_ref, sp_ref, oa_ref, st_ref):
        n = pl.program_id(0)

        @pl.when(n == 0)
        def _():
            st_ref[...] = jnp.zeros_like(st_ref)

        for c in range(SCAN_CHUNKS):
            rows = slice(CHUNK * c, CHUNK * (c + 1))
            srows = slice(GDN_STACK * c, GDN_STACK * (c + 1))
            st = st_ref[...]
            sp_ref[GDN_WIDTH * c:GDN_WIDTH * (c + 1), :] = st
            uv, gs, z = [_stack(r, rows) for r in (uv_ref, gcb_ref, z_ref)]
            u = uv - _bdot(wkb_ref[srows, :], st)
            o = _bdot(qdb_ref[srows, :], st) + _bdot(at_ref[srows, :], u)
            st_ref[...] = jnp.exp(_last_rows(gs, GDN_DIM)) * st + _bdot_tn(keb_ref[srows, :], u)
            _unstack_to(u_ref, rows, u)
            _unstack_to(o_ref, rows, o)
            r = lax.rsqrt(jnp.mean(o * o, axis=-1, keepdims=True) + EPS)
            oa = o * r * gnw_ref[...] * (z * _sigmoid(z))
            oa_ref[rows, :] = jnp.concatenate([oa[CHUNK * h:CHUNK * (h + 1)] for h in range(GDN_HEADS)], axis=1).astype(BF16)

    row = pl.BlockSpec((SCAN_CHUNKS * CHUNK, GDN_WIDTH), lambda n: (n, 0))
    sq = pl.BlockSpec((SCAN_CHUNKS * GDN_STACK, GDN_STACK), lambda n: (n, 0))
    wide = pl.BlockSpec((SCAN_CHUNKS * GDN_STACK, GDN_WIDTH), lambda n: (n, 0))
    return pl.pallas_call(
        body, name=name, grid=(nc // SCAN_CHUNKS,),
        in_specs=[row, sq, wide, wide, wide, row, row, pl.BlockSpec((1, GDN_DIM), lambda n: (0, 0))],
        out_specs=[row, row, pl.BlockSpec((SCAN_CHUNKS * GDN_WIDTH, GDN_DIM), lambda n: (n, 0)), row],
        out_shape=[_sds((S, GDN_WIDTH)), _sds((S, GDN_WIDTH)), _sds((nc * GDN_WIDTH, GDN_DIM)), _sds((S, 2 * GDN_WIDTH), BF16)],
        scratch_shapes=[pltpu.VMEM((GDN_WIDTH, GDN_DIM), F32)],
        compiler_params=_params(("arbitrary",)),
    )(uv, at, wkb, qdb, keb, gcb, proj_z, gnw)


def _gdn_scan_bwd(d_oab, o, proj_z, gnw, sp, u, at, wkb, qdb, keb, gcb, *, name, after=()):
    S = o.shape[0]
    nc = S // CHUNK
    ns = nc // SCAN_CHUNKS

    def body(do_ref, o_ref, z_ref, gnw_ref, sp_ref, u_ref, at_ref, wkb_ref, qdb_ref, keb_ref, gcb_ref, *rest):
        dz_ref, dgn_ref, du_ref, dwk_ref, dat_ref, dqd_ref, dke_ref, dgl_ref, ds_ref = rest[len(after):]
        n = pl.program_id(0)

        @pl.when(n == 0)
        def _():
            ds_ref[...] = jnp.zeros_like(ds_ref)
            dgn_ref[...] = jnp.zeros_like(dgn_ref)

        gw = gnw_ref[...]
        for c in reversed(range(SCAN_CHUNKS)):
            rows = slice(CHUNK * c, CHUNK * (c + 1))
            srows = slice(GDN_STACK * c, GDN_STACK * (c + 1))
            d_oa, oo, z, uu, gs = [_stack(r, rows) for r in (do_ref, o_ref, z_ref, u_ref, gcb_ref)]
            sg = _sigmoid(z)
            r = lax.rsqrt(jnp.mean(oo * oo, axis=-1, keepdims=True) + EPS)
            xh = oo * r
            dy = d_oa * (z * sg)
            _unstack_to(dz_ref, rows, d_oa * (xh * gw) * _dsilu(z, sg))
            dgn_ref[...] += jnp.sum(dy * xh, axis=0, keepdims=True)
            dxh = dy * gw
            do = r * (dxh - xh * jnp.mean(dxh * xh, axis=-1, keepdims=True))

            st = sp_ref[GDN_WIDTH * c:GDN_WIDTH * (c + 1), :]
            dst = ds_ref[...]
            ge = jnp.exp(_last_rows(gs, GDN_DIM))
            _unstack_to(dqd_ref, rows, _head_diag(_bdot_nt(do, st)))
            dat_ref[srows, :] = _bdot_nt(do, uu)
            du = _bdot_tn(at_ref[srows, :], do) + _bdot(keb_ref[srows, :], dst)
            _unstack_to(dke_ref, rows, _head_diag(_bdot_nt(uu, dst)))
            prod = dst * st
            for h in range(GDN_HEADS):
                blk = prod[GDN_DIM * h:GDN_DIM * (h + 1)]
                dge = jnp.sum(jnp.sum(blk, axis=1, keepdims=True), axis=0, keepdims=True)
                dgl_ref[c, :, GDN_DIM * h:GDN_DIM * (h + 1)] = jnp.broadcast_to(dge * ge[GDN_DIM * h:GDN_DIM * h + 1], (8, GDN_DIM))
            ds_ref[...] = _bdot_tn(qdb_ref[srows, :], do) + ge * dst - _bdot_tn(wkb_ref[srows, :], du)
            _unstack_to(du_ref, rows, du)
            _unstack_to(dwk_ref, rows, -_head_diag(_bdot_nt(du, st)))

    rev = lambda n: (ns - 1 - n, 0)
    row = pl.BlockSpec((SCAN_CHUNKS * CHUNK, GDN_WIDTH), rev)
    sq = pl.BlockSpec((SCAN_CHUNKS * GDN_STACK, GDN_STACK), rev)
    wide = pl.BlockSpec((SCAN_CHUNKS * GDN_STACK, GDN_WIDTH), rev)
    one = pl.BlockSpec((1, GDN_DIM), lambda n: (0, 0))
    return pl.pallas_call(
        body, name=name, grid=(ns,),
        in_specs=[row, row, row, one, pl.BlockSpec((SCAN_CHUNKS * GDN_WIDTH, GDN_DIM), rev), row, sq, wide, wide, wide, row]
        + [ANY] * len(after),
        out_specs=[row, one, row, row, sq, row, row, pl.BlockSpec((SCAN_CHUNKS, 8, GDN_WIDTH), lambda n: (ns - 1 - n, 0, 0))],
        out_shape=[_sds((S, GDN_WIDTH), BF16), _sds((1, GDN_DIM)), _sds((S, GDN_WIDTH)), _sds((S, GDN_WIDTH)),
                   _sds((nc * GDN_STACK, GDN_STACK)), _sds((S, GDN_WIDTH)), _sds((S, GDN_WIDTH)), _sds((nc, 8, GDN_WIDTH))],
        scratch_shapes=[pltpu.VMEM((GDN_WIDTH, GDN_DIM), F32)],
        compiler_params=_params(("arbitrary",)),
    )(d_oab, o, proj_z, gnw, sp, u, at, wkb, qdb, keb, gcb, *after)


def _gdn_chunk_bwd(qn, kn, gcb, bb, tmat, uv, wk, du, dwk, dat, dqd, dke, dgl, *, name):
    S = qn.shape[0]

    def body(qn_ref, kn_ref, gcb_ref, bb_ref, t_ref, uv_ref, wk_ref, du_ref, dwk_ref, dat_ref, dqd_ref, dke_ref,
             dgl_ref, dq_ref, dk_ref, dv_ref, dg_ref, dbeta_ref):
        incl, strict, _ = _stack_masks()
        lane = lax.broadcasted_iota(jnp.int32, (CHUNK, 128), 1)
        rowi = lax.broadcasted_iota(jnp.int32, (CHUNK, 1), 0)
        rsum = lambda x: jnp.sum(x, axis=-1, keepdims=True)
        for c in range(CHUNKS_PER_STEP):
            rows = slice(CHUNK * c, CHUNK * (c + 1))
            srows = slice(GDN_STACK * c, GDN_STACK * (c + 1))
            q, k, gs, bs, uv, wk, du, dwk, dqd, dke = [
                _stack(r, rows) for r in (qn_ref, kn_ref, gcb_ref, bb_ref, uv_ref, wk_ref, du_ref, dwk_ref, dqd_ref, dke_ref)]
            dec, bt = _stack_decay(gs, bs, incl)
            kk = _bdot_nt(k, k)
            qk = _bdot_nt(q, k)
            d_rhs = _dot3(t_ref[srows, :], jnp.concatenate([du, dwk], axis=1), ((0,), (0,)))
            sol = jnp.concatenate([uv, wk], axis=1)
            d_l = jnp.where(strict, -_dot3(d_rhs, sol, ((1,), (1,))), 0.0)
            d_a = jnp.where(incl, dat_ref[srows, :], 0.0)
            gam = jnp.exp(gs)
            e = jnp.exp(_last_rows(gs, CHUNK) - gs)
            d_gk = d_rhs[:, GDN_DIM:]
            ml = d_l * dec * bt
            ma = d_a * dec * bt
            _unstack_to(dq_ref, rows, _bdot(ma, k) + dqd * gam)
            _unstack_to(dk_ref, rows, _bdot(ml + ml.T, k) + _bdot_tn(ma, q) + d_gk * gam + dke * (e * bs))
            _unstack_to(dv_ref, rows, d_rhs[:, :GDN_DIM])
            wb = d_l * dec * kk + d_a * dec * qk
            ew = wb * bt
            s_ke = rsum(dke * k * (e * bs))
            dbeta = rsum(wb.T) + rsum(dke * k * e)
            dgc = rsum(ew) - rsum(ew.T) + rsum(dqd * q * gam) + rsum(d_gk * k * gam) - s_ke
            dgc4 = jnp.zeros((CHUNK, 128), F32)
            db4 = jnp.zeros((CHUNK, 128), F32)
            for h in range(GDN_HEADS):
                hr = slice(CHUNK * h, CHUNK * (h + 1))
                tail = jnp.sum(s_ke[hr], axis=0, keepdims=True) + dgl_ref[c, 0:1, GDN_DIM * h:GDN_DIM * h + 1]
                dgc4 = jnp.where(lane == h, dgc[hr] + jnp.where(rowi == CHUNK - 1, tail, 0.0), dgc4)
                db4 = jnp.where(lane == h, dbeta[hr], db4)
            dg_ref[rows, :] = _exact_tri_dot(_chunk_tri(CHUNK, upper=True), dgc4)
            dbeta_ref[rows, :] = db4

    step = CHUNKS_PER_STEP * CHUNK
    row = pl.BlockSpec((step, GDN_WIDTH), lambda n: (n, 0))
    sq = pl.BlockSpec((CHUNKS_PER_STEP * GDN_STACK, GDN_STACK), lambda n: (n, 0))
    col = pl.BlockSpec((step, 128), lambda n: (n, 0))
    return pl.pallas_call(
        body, name=name, grid=(S // step,),
        in_specs=[row] * 4 + [sq, row, row, row, row, sq, row, row,
                              pl.BlockSpec((CHUNKS_PER_STEP, 8, GDN_WIDTH), lambda n: (n, 0, 0))],
        out_specs=[row, row, row, col, col],
        out_shape=[_sds((S, GDN_WIDTH))] * 3 + [_sds((S, 128))] * 2, compiler_params=_params(("parallel",)),
    )(qn, kn, gcb, bb, tmat, uv, wk, du, dwk, dat, dqd, dke, dgl)


def _gdn_prep_bwd(dqn, dkn, dv, dg, dbeta, proj_a, conv_w, a_log, dt_bias, *, name, tm=256):
    S = proj_a.shape[0]
    nblk = S // tm
    W3 = 3 * GDN_WIDTH

    def body(dqn_ref, dkn_ref, dv_ref, dg_ref, dbeta_ref, cur_ref, prev_ref, ba_ref, cw_ref, al_ref, dt_ref,
             dc_ref, dba_ref, sm_ref):
        i = pl.program_id(0)
        prev = jnp.where(i > 0, prev_ref[...], 0.0)
        c = _conv_rows(prev, cur_ref[...], cw_ref[...], GDN_CONV)
        sg = _sigmoid(c)
        a = c * sg
        dsl = _dsilu(c, sg)
        ba = ba_ref[...]
        lane = lax.broadcasted_iota(jnp.int32, (tm, 128), 1)
        lane1 = lax.broadcasted_iota(jnp.int32, (1, 128), 1)
        dba = jnp.zeros((tm, 128), F32)
        sm = jnp.zeros((1, 128), F32)
        for h in range(GDN_HEADS):
            sl = slice(GDN_DIM * h, GDN_DIM * (h + 1))
            ks = slice(GDN_WIDTH + GDN_DIM * h, GDN_WIDTH + GDN_DIM * (h + 1))
            qh, kh = a[:, sl], a[:, ks]
            rq = lax.rsqrt(jnp.sum(qh * qh, axis=-1, keepdims=True) + EPS)
            rk = lax.rsqrt(jnp.sum(kh * kh, axis=-1, keepdims=True) + EPS)
            qhat, khat = qh * rq, kh * rk
            dyq = dqn_ref[:, sl] * (GDN_DIM ** -0.5)
            dyk = dkn_ref[:, sl]
            dq = rq * (dyq - qhat * jnp.sum(dyq * qhat, axis=-1, keepdims=True))
            dk = rk * (dyk - khat * jnp.sum(dyk * khat, axis=-1, keepdims=True))
            dc_ref[:, sl] = dq * dsl[:, sl]
            dc_ref[:, ks] = dk * dsl[:, ks]
            beta = _sigmoid(ba[:, h:h + 1])
            db = dbeta_ref[:, h:h + 1] * beta * (1.0 - beta)
            aneg = -jnp.exp(al_ref[0:1, h:h + 1])
            xa = ba[:, GDN_HEADS + h:GDN_HEADS + h + 1] + dt_ref[0:1, h:h + 1]
            dgh = dg_ref[:, h:h + 1]
            dxa = dgh * aneg * _sigmoid(xa)
            dba = jnp.where(lane == h, db, dba)
            dba = jnp.where(lane == GDN_HEADS + h, dxa, dba)
            d_alog = jnp.sum(dgh * _softplus(xa), axis=0, keepdims=True) * aneg
            sm = jnp.where(lane1 == h, d_alog, sm)
            sm = jnp.where(lane1 == GDN_HEADS + h, jnp.sum(dxa, axis=0, keepdims=True), sm)
        vs = slice(2 * GDN_WIDTH, W3)
        dc_ref[:, vs] = dv_ref[...] * dsl[:, vs]
        dba_ref[...] = dba

        @pl.when(i == 0)
        def _():
            sm_ref[...] = sm

        @pl.when(i > 0)
        def _():
            sm_ref[...] += sm

    prev_spec, _ = _halo_specs(tm, W3, 0, nblk)
    row = pl.BlockSpec((tm, GDN_WIDTH), lambda i: (i, 0))
    col = pl.BlockSpec((tm, 128), lambda i: (i, 0))
    small = lambda a: pl.BlockSpec(a.shape, lambda i: (0, 0))
    return pl.pallas_call(
        body, name=name, grid=(nblk,),
        in_specs=[row, row, row, col, col, pl.BlockSpec((tm, W3), lambda i: (i, 0)), prev_spec,
                  pl.BlockSpec((tm, 128), lambda i: (i, W3 // 128)), small(conv_w), small(a_log), small(dt_bias)],
        out_specs=[pl.BlockSpec((tm, W3), lambda i: (i, 0)), col, pl.BlockSpec((1, 128), lambda i: (0, 0))],
        out_shape=[_sds((S, W3)), _sds((S, 128)), _sds((1, 128))], compiler_params=_params(("arbitrary",)),
    )(dqn, dkn, dv, dg, dbeta, proj_a, proj_a, proj_a, conv_w, a_log, dt_bias)


def _gdn_conv_bwd(dc, dba, proj_a, conv_w, *, name, tm=256):
    S = proj_a.shape[0]
    nblk = S // tm
    W3 = 3 * GDN_WIDTH

    def body(dc_ref, dnext_ref, dba_ref, cur_ref, prev_ref, cw_ref, da_ref, dcw_ref):
        i = pl.program_id(0)
        prev = jnp.where(i > 0, prev_ref[...], 0.0)
        nxt = jnp.where(i < nblk - 1, dnext_ref[...], 0.0)
        dx, dw = _conv_rows_bwd(dc_ref[...], nxt, prev, cur_ref[...], cw_ref[...], GDN_CONV)
        da_ref[:, 0:W3] = dx.astype(BF16)
        da_ref[:, W3:] = dba_ref[...].astype(BF16)

        @pl.when(i == 0)
        def _():
            dcw_ref[...] = dw

        @pl.when(i > 0)
        def _():
            dcw_ref[...] += dw

    prev_spec, next_spec = _halo_specs(tm, W3, 0, nblk)
    wide = pl.BlockSpec((tm, W3), lambda i: (i, 0))
    return pl.pallas_call(
        body, name=name, grid=(nblk,),
        in_specs=[wide, next_spec, pl.BlockSpec((tm, 128), lambda i: (i, 0)), wide, prev_spec,
                  pl.BlockSpec(conv_w.shape, lambda i: (0, 0))],
        out_specs=[pl.BlockSpec((tm, A_COLS), lambda i: (i, 0)), pl.BlockSpec(conv_w.shape, lambda i: (0, 0))],
        out_shape=[_sds((S, A_COLS), BF16), _sds(conv_w.shape)], compiler_params=_params(("arbitrary",)),
    )(dc, dc, dba, proj_a, proj_a, conv_w)


def _band_mask(nk):
    i = lax.broadcasted_iota(jnp.int32, (2 * BAND, nk), 0) & (BAND - 1)
    j = lax.broadcasted_iota(jnp.int32, (2 * BAND, nk), 1)
    if nk == BAND:
        return j <= i
    return (j >= i) & (j <= i + BAND)


def _stack_heads(x, lo):
    return jnp.concatenate([jnp.where(lo, x, 0.0), jnp.where(lo, 0.0, x)], axis=0)


def _stack_cols(x):
    return jnp.concatenate([x[:, 0:1], x[:, DIL_DIM:DIL_DIM + 1]], axis=0)


def _unstack(x, lo):
    return jnp.where(lo, x[0:BAND], x[BAND:2 * BAND])


def _rows(start, size, stride):
    return pl.ds(start, size) if stride == 1 else pl.ds(start, size, stride=stride)


ATTN_LANES = 4


def _attn_blocks(S, visit_many, lanes=ATTN_LANES):
    for d in DILATIONS:
        nb = S // (d * BAND)
        if d == 1:
            half = nb // 2
            visit_many(d, [(0, 0, True), (0, half, False)])

            def pair(n, c):
                visit_many(1, [(0, n, False), (0, n + half, False)])
                return c
            lax.fori_loop(1, half, pair, 0)
        elif nb > 1:
            for r0 in range(0, d, lanes):
                visit_many(d, [(r0 + t, 0, True) for t in range(lanes)])

                def column(n, c, d=d, r0=r0):
                    visit_many(d, [(r0 + t, n, False) for t in range(lanes)])
                    return c
                lax.fori_loop(1, nb, column, 0)
        else:
            def group(g, c, d=d):
                visit_many(d, [(g * lanes + t, 0, True) for t in range(lanes)])
                return c
            lax.fori_loop(0, d // lanes, group, 0)


def _attn_fwd(proj_b, oab, *, name):
    S = proj_b.shape[0]
    scale = DIL_DIM ** -0.5

    def body(q_ref, k_ref, v_ref, oab_in_ref, ob_ref, lse_ref, m_ref, l_ref, acc_ref):
        del oab_in_ref
        lane = lax.broadcasted_iota(jnp.int32, (BAND, 128), 1)
        lo = lane < DIL_DIM
        m_ref[...] = jnp.full_like(m_ref, NEG_BIG)
        l_ref[...] = jnp.zeros_like(l_ref)
        acc_ref[...] = jnp.zeros_like(acc_ref)

        def load(d, r, n, first):
            nk = BAND if first else 2 * BAND
            qrows = _rows(r + n * (BAND * d), BAND, d)
            krows = _rows(r if first else r + (n - 1) * (BAND * d), nk, d)
            return dict(nk=nk, qrows=qrows, q=q_ref[qrows, :] * scale, k=k_ref[krows, :].astype(BF16),
                        v=v_ref[krows, :].astype(BF16), m=m_ref[qrows, :], l=l_ref[qrows, :], acc=acc_ref[qrows, :])

        def compute(b):
            q, k, v = b["q"], b["k"], b["v"]
            s = jnp.where(_band_mask(b["nk"]), _bdot_nt(_stack_heads(q, lo), k), NEG_BIG)
            m_old = _stack_cols(b["m"])
            m_new = jnp.maximum(m_old, jnp.max(s, axis=-1, keepdims=True))
            p = jnp.exp(s - m_new)
            alpha = _unstack(jnp.exp(m_old - m_new), lo)
            l_new = alpha * b["l"] + _unstack(jnp.sum(p, axis=-1, keepdims=True), lo)
            return _unstack(m_new, lo), l_new, alpha * b["acc"] + _unstack(_bdot(p, v), lo)

        def visit_many(d, blocks):
            loaded = [load(d, *blk) for blk in blocks]
            done = [compute(b) for b in loaded]
            for b, (m_new, l_new, acc_new) in zip(loaded, done):
                m_ref[b["qrows"], :] = m_new
                l_ref[b["qrows"], :] = l_new
                acc_ref[b["qrows"], :] = acc_new

        _attn_blocks(S, visit_many)
        ob_ref[...] = (acc_ref[...] / l_ref[...]).astype(BF16)
        lse_ref[...] = m_ref[...] + jnp.log(l_ref[...])

    part = lambda t: pl.BlockSpec((S, 128), lambda p: (0, 3 * p + t))
    return pl.pallas_call(
        body, name=name, grid=(DIL_PAIRS,),
        in_specs=[part(0), part(1), part(2), pl.BlockSpec(memory_space=pl.ANY)],
        out_specs=[pl.BlockSpec((S, 128), lambda p: (0, GDN_WIDTH // 128 + p)), pl.BlockSpec((S, 128), lambda p: (0, p))],
        out_shape=[_sds(oab.shape, BF16), _sds((S, DIL_WIDTH))],
        scratch_shapes=[pltpu.VMEM((S, 128), F32)] * 3, input_output_aliases={3: 0},
        compiler_params=_params(("parallel",)),
    )(proj_b, proj_b, proj_b, oab)


def _attn_bwd(proj_b, oab, d_oab, lse, *, name):
    S = proj_b.shape[0]
    scale = DIL_DIM ** -0.5

    def body(q_ref, k_ref, v_ref, o_ref, do_ref, lse_ref, dqkv_ref, dq_ref, dk_ref, dv_ref, delta_ref):
        lane = lax.broadcasted_iota(jnp.int32, (BAND, 128), 1)
        lo = lane < DIL_DIM
        dq_ref[...] = jnp.zeros_like(dq_ref)
        dk_ref[...] = jnp.zeros_like(dk_ref)
        dv_ref[...] = jnp.zeros_like(dv_ref)
        prod = do_ref[...] * o_ref[...].astype(F32)
        lo_all = lax.broadcasted_iota(jnp.int32, (S, 128), 1) < DIL_DIM
        delta_ref[...] = jnp.where(lo_all, jnp.sum(jnp.where(lo_all, prod, 0.0), axis=-1, keepdims=True),
                                   jnp.sum(jnp.where(lo_all, 0.0, prod), axis=-1, keepdims=True))

        def load(d, r, n, first):
            nk = BAND if first else 2 * BAND
            qrows = _rows(r + n * (BAND * d), BAND, d)
            krows = _rows(r if first else r + (n - 1) * (BAND * d), nk, d)
            return dict(nk=nk, qrows=qrows, krows=krows, q=q_ref[qrows, :] * scale, k=k_ref[krows, :], v=v_ref[krows, :],
                        do=do_ref[qrows, :], delta=delta_ref[qrows, :], lse=lse_ref[qrows, :],
                        dq=dq_ref[qrows, :], dk=dk_ref[krows, :], dv=dv_ref[krows, :])

        def compute(b):
            q, k, v, do = b["q"], b["k"], b["v"], b["do"]
            qs, dos = _stack_heads(q, lo), _stack_heads(do, lo)
            p = jnp.where(_band_mask(b["nk"]), jnp.exp(_bdot_nt(qs, k) - _stack_cols(b["lse"])), 0.0)
            ds = p * (_bdot_nt(dos, v) - _stack_cols(b["delta"]))
            dq = b["dq"] + _unstack(_bdot(ds, k), lo) * scale
            return dq, b["dk"] + _bdot_tn(ds, qs), b["dv"] + _bdot_tn(p, dos)

        def visit_many(d, blocks):
            loaded = [load(d, *blk) for blk in blocks]
            done = [compute(b) for b in loaded]
            for b, (dq, dk, dv) in zip(loaded, done):
                dq_ref[b["qrows"], :] = dq
                dk_ref[b["krows"], :] = dk
                dv_ref[b["krows"], :] = dv

        _attn_blocks(S, visit_many, lanes=2)
        dqkv_ref[:, 0:128] = dq_ref[...].astype(BF16)
        dqkv_ref[:, 128:256] = dk_ref[...].astype(BF16)
        dqkv_ref[:, 256:384] = dv_ref[...].astype(BF16)

    half = lambda p: (0, GDN_WIDTH // 128 + p)
    part = lambda t: pl.BlockSpec((S, 128), lambda p: (0, 3 * p + t))
    return pl.pallas_call(
        body, name=name, grid=(DIL_PAIRS,),
        in_specs=[part(0), part(1), part(2), pl.BlockSpec((S, 128), half), pl.BlockSpec((S, 128), half),
                  pl.BlockSpec((S, 128), lambda p: (0, p))],
        out_specs=pl.BlockSpec((S, 384), lambda p: (0, p)), out_shape=_sds((S, 3 * DIL_WIDTH), BF16),
        scratch_shapes=[pltpu.VMEM((S, 128), F32)] * 4, compiler_params=_params(("parallel",)),
    )(proj_b, proj_b, proj_b, oab, d_oab, lse)


FF_SLAB = 2 * D_FF // N_DEV
FF_PAIRS = N_DEV // 2
ROWS16 = 16


def _taps(w, x, base, n):
    out = _shifted(x, base, n) * w[0:1]
    for t in range(1, FFN_CONV):
        out = out + _shifted(x, base + t, n) * w[t:t + 1]
    return out


def _ffn_fwd(h2, x1, w_up, conv_w, w_down, final_w, tgt, *, name, tm=512):
    S, D = h2.shape
    ni = S // tm
    per = tm // ROWS16

    def body(h_ref, hp_ref, x1_ref, wg_ref, wu_ref, cg_ref, cu_ref, wd_ref, fw_ref, t_ref,
             dx_ref, dxb_ref, dfw_ref, loss_ref, ug_ref, uu_ref, x2_ref):
        i, j = pl.program_id(0), pl.program_id(1)
        hv = jnp.concatenate([hp_ref[...], h_ref[...]], axis=0)
        row = lax.broadcasted_iota(jnp.int32, (tm + ROWS16, 1), 0)
        keep = (i > 0) | (row >= ROWS16)

        def branch(w_ref, c_ref, u_ref):
            u = lax.dot_general(hv, w_ref[...], (((1,), (1,)), ((), ())), preferred_element_type=F32).astype(BF16)
            u_ref[...] = u[ROWS16:]
            return _taps(c_ref[...], jnp.where(keep, u.astype(F32), 0.0), ROWS16 - (FFN_CONV - 1), tm)

        gate = branch(wg_ref, cg_ref, ug_ref)
        up = branch(wu_ref, cu_ref, uu_ref)
        act = (gate * _sigmoid(gate) * up).astype(BF16)
        part = jnp.dot(act, wd_ref[...], preferred_element_type=F32)

        @pl.when(j == 0)
        def _():
            x2_ref[...] = x1_ref[...] + part

        @pl.when((j > 0) & (j < FF_PAIRS - 1))
        def _():
            x2_ref[...] += part

        @pl.when(j == FF_PAIRS - 1)
        def _():
            xv = x2_ref[...] + part
            wv = fw_ref[...]
            r = lax.rsqrt(jnp.mean(xv * xv, axis=-1, keepdims=True) + EPS)
            err = xv * r * wv - t_ref[...]
            lsum = jnp.sum(jnp.sum(err * err, axis=-1, keepdims=True), axis=0, keepdims=True) * (0.5 / D)
            g = err * (1.0 / D)
            xh = xv * r
            gw = g * wv
            dx = r * (gw - xh * jnp.mean(gw * xh, axis=-1, keepdims=True))
            dx_ref[...] = dx
            dxb_ref[...] = dx.astype(BF16)
            dfw = jnp.sum(g * xh, axis=0, keepdims=True)
            lpart = jnp.broadcast_to(lsum, (1, 128))

            @pl.when(i == 0)
            def _():
                dfw_ref[...] = dfw
                loss_ref[...] = lpart

            @pl.when(i > 0)
            def _():
                dfw_ref[...] += dfw
                loss_ref[...] += lpart

    rows = pl.BlockSpec((tm, D), lambda i, j: (i, 0))
    slab = lambda off: pl.BlockSpec((None, FF_SLAB, D), lambda i, j: (j + off, 0, 0))
    cslab = lambda off: pl.BlockSpec((None, FFN_CONV, FF_SLAB), lambda i, j: (j + off, 0, 0))
    uspec = pl.BlockSpec((None, tm, FF_SLAB), lambda i, j: (j, i, 0))
    return pl.pallas_call(
        body, name=name, grid=(ni, FF_PAIRS),
        in_specs=[rows, pl.BlockSpec((ROWS16, D), lambda i, j: (jnp.maximum(i * per - 1, 0), 0)), rows,
                  slab(0), slab(FF_PAIRS), cslab(0), cslab(FF_PAIRS), pl.BlockSpec((FF_SLAB, D), lambda i, j: (j, 0)),
                  pl.BlockSpec((1, D), lambda i, j: (0, 0)), rows],
        out_specs=[rows, rows, pl.BlockSpec((1, D), lambda i, j: (0, 0)), pl.BlockSpec((1, 128), lambda i, j: (0, 0)), uspec, uspec],
        out_shape=[_sds((S, D)), _sds((S, D), BF16), _sds((1, D)), _sds((1, 128)),
                   _sds((FF_PAIRS, S, FF_SLAB), BF16), _sds((FF_PAIRS, S, FF_SLAB), BF16)],
        scratch_shapes=[pltpu.VMEM((tm, D), F32)],
        compiler_params=_params(("arbitrary", "arbitrary")),
    )(h2, h2, x1, w_up, w_up, conv_w, conv_w, w_down, final_w, tgt)


def _ffn_bwd(dx2, h2, ug, uu, conv_w, w_down, *, name, tm=512):
    S, D = h2.shape
    ni = S // tm
    per = tm // ROWS16
    ext = tm + ROWS16

    def body(dx_ref, dxn_ref, h_ref, ug_ref, ugp_ref, ugn_ref, uu_ref, uup_ref, uun_ref, cg_ref, cu_ref, wd_ref,
             du_ref, gd_ref, gup_ref, dcw_ref, acc_d, acc_g, acc_u, acc_cg, acc_cu):
        i = pl.program_id(1)

        @pl.when(i == 0)
        def _():
            acc_d[...] = jnp.zeros_like(acc_d)
            acc_g[...] = jnp.zeros_like(acc_g)
            acc_u[...] = jnp.zeros_like(acc_u)
            acc_cg[...] = jnp.zeros_like(acc_cg)
            acc_cu[...] = jnp.zeros_like(acc_cu)

        dx = dx_ref[...]
        dxe = jnp.concatenate([dx, dxn_ref[...]], axis=0)
        row = lax.broadcasted_iota(jnp.int32, (ext, 1), 0)
        live = (i < ni - 1) | (row < tm)
        d_act = jnp.where(live, lax.dot_general(dxe, wd_ref[...], (((1,), (1,)), ((), ())), preferred_element_type=F32), 0.0)
        rowp = lax.broadcasted_iota(jnp.int32, (ext + ROWS16, 1), 0)
        keep = (i > 0) | (rowp >= ROWS16)

        def pre(cur, prev, nxt):
            return jnp.where(keep, jnp.concatenate([prev[...], cur[...], nxt[...]], axis=0).astype(F32), 0.0)

        uge, uue = pre(ug_ref, ugp_ref, ugn_ref), pre(uu_ref, uup_ref, uun_ref)
        cg, cu = cg_ref[...], cu_ref[...]
        base = ROWS16 - (FFN_CONV - 1)
        gate = _taps(cg, uge, base, ext)
        up = _taps(cu, uue, base, ext)
        sg = _sigmoid(gate)
        silu = gate * sg
        dgc = d_act * up * _dsilu(gate, sg)
        duc = d_act * silu

        def conv_t(w, dc):
            out = _shifted(dc, FFN_CONV - 1, tm) * w[0:1]
            for t in range(1, FFN_CONV):
                out = out + _shifted(dc, FFN_CONV - 1 - t, tm) * w[t:t + 1]
            return out.astype(BF16)

        du_g, du_u = conv_t(cg, dgc), conv_t(cu, duc)
        du_ref[0] = du_g
        du_ref[1] = du_u
        dcw = lambda dc, xe: jnp.concatenate(
            [jnp.sum(dc[0:tm] * _shifted(xe, base + t, tm), axis=0, keepdims=True) for t in range(FFN_CONV)], axis=0)
        acc_cg[0:FFN_CONV, :] += dcw(dgc, uge)
        acc_cu[0:FFN_CONV, :] += dcw(duc, uue)
        tn = (((0,), (0,)), ((), ()))
        act = (silu[0:tm] * up[0:tm]).astype(BF16)
        acc_d[...] += lax.dot_general(act, dx, tn, preferred_element_type=F32)
        hv = h_ref[...]
        acc_g[...] += lax.dot_general(du_g, hv, tn, preferred_element_type=F32)
        acc_u[...] += lax.dot_general(du_u, hv, tn, preferred_element_type=F32)

        @pl.when(i == ni - 1)
        def _():
            gd_ref[...] = acc_d[...].astype(BF16)
            gup_ref[0] = acc_g[...].astype(BF16)
            gup_ref[1] = acc_u[...].astype(BF16)
            dcw_ref[0] = acc_cg[0:FFN_CONV, :]
            dcw_ref[1] = acc_cu[0:FFN_CONV, :]

    last16 = S // ROWS16 - 1
    rows = pl.BlockSpec((tm, D), lambda j, i: (i, 0))
    rows_next = pl.BlockSpec((ROWS16, D), lambda j, i: (jnp.minimum((i + 1) * per, last16), 0))
    u_cur = pl.BlockSpec((None, tm, FF_SLAB), lambda j, i: (j, i, 0))
    u_prev = pl.BlockSpec((None, ROWS16, FF_SLAB), lambda j, i: (j, jnp.maximum(i * per - 1, 0), 0))
    u_next = pl.BlockSpec((None, ROWS16, FF_SLAB), lambda j, i: (j, jnp.minimum((i + 1) * per, last16), 0))
    cslab = lambda off: pl.BlockSpec((None, FFN_CONV, FF_SLAB), lambda j, i: (j + off, 0, 0))
    return pl.pallas_call(
        body, name=name, grid=(FF_PAIRS, ni),
        in_specs=[rows, rows_next, rows, u_cur, u_prev, u_next, u_cur, u_prev, u_next, cslab(0), cslab(FF_PAIRS),
                  pl.BlockSpec((FF_SLAB, D), lambda j, i: (j, 0))],
        out_specs=[pl.BlockSpec((None, 2, tm, FF_SLAB), lambda j, i: (j, 0, i, 0)), pl.BlockSpec((FF_SLAB, D), lambda j, i: (j, 0)),
                   pl.BlockSpec((None, 2, FF_SLAB, D), lambda j, i: (j, 0, 0, 0)),
                   pl.BlockSpec((None, 2, FFN_CONV, FF_SLAB), lambda j, i: (j, 0, 0, 0))],
        out_shape=[_sds((FF_PAIRS, 2, S, FF_SLAB), BF16), _sds((D_FF, D), BF16), _sds((FF_PAIRS, 2, FF_SLAB, D), BF16),
                   _sds((FF_PAIRS, 2, FFN_CONV, FF_SLAB))],
        scratch_shapes=[pltpu.VMEM((FF_SLAB, D), F32), pltpu.VMEM((FF_SLAB, D), F32), pltpu.VMEM((FF_SLAB, D), F32),
                        pltpu.VMEM((8, FF_SLAB), F32), pltpu.VMEM((8, FF_SLAB), F32)],
        compiler_params=_params(("parallel", "arbitrary")),
    )(dx2, dx2, h2, ug, ug, ug, uu, uu, uu, conv_w, conv_w, w_down)


def _pair_slot(p):
    return 2 * (p & (FF_PAIRS - 1)) + (p >> 2)


def _mm_slabs(a, w, *, name, res=None, norm_bwd=None, after=(), tm=1024, tn=1024):
    nk, S, _ = a.shape
    D = w.shape[2]
    has_res = res is not None
    has_norm = norm_bwd is not None
    assert not has_norm or tn == D

    def body(*refs):
        a_ref, w_ref = refs[:2]
        r_ref = refs[2] if has_res else None
        if has_norm:
            x_ref, nw_ref, skip_ref = refs[2 + has_res:5 + has_res]
            o_ref, dw_ref, acc_ref = refs[-3:]
        else:
            o_ref, acc_ref = refs[-2:]
        i, k = pl.program_id(0), pl.program_id(2)
        part = jnp.dot(a_ref[...], w_ref[...], preferred_element_type=F32)

        @pl.when(k == 0)
        def _():
            acc_ref[...] = part

        @pl.when(k > 0)
        def _():
            acc_ref[...] += part

        @pl.when(k == nk - 1)
        def _():
            r = acc_ref[...] + r_ref[...] if has_res else acc_ref[...]
            if has_norm:
                dx, dw = _rms_bwd_rows(r, x_ref[...], nw_ref[...])
                o_ref[...] = skip_ref[...] + dx

                @pl.when(i == 0)
                def _():
                    dw_ref[...] = dw

                @pl.when(i > 0)
                def _():
                    dw_ref[...] += dw
            else:
                o_ref[...] = r

    o_spec = pl.BlockSpec((tm, tn), lambda i, j, k: (i, j))
    one = pl.BlockSpec((1, tn), lambda i, j, k: (0, 0))
    return pl.pallas_call(
        body, name=name, grid=(S // tm, D // tn, nk),
        in_specs=[pl.BlockSpec((None, tm, FF_SLAB), lambda i, j, k: (k, i, 0)),
                  pl.BlockSpec((None, FF_SLAB, tn), lambda i, j, k: (FF_PAIRS * (k & 1) + (k >> 1), 0, j))] + [o_spec] * has_res
        + ([o_spec, one, o_spec] if has_norm else []) + [ANY] * len(after),
        out_specs=[o_spec, one] if has_norm else o_spec, out_shape=[_sds((S, D)), _sds((1, D))] if has_norm else _sds((S, D)),
        scratch_shapes=[pltpu.VMEM((tm, tn), F32)],
        compiler_params=_params(("arbitrary" if has_norm else "parallel", "parallel", "arbitrary")),
    )(*((a, w) + ((res,) if has_res else ()) + (tuple(norm_bwd) if has_norm else ()) + tuple(after)))


def _local_step(x, tgt, norm1_w, w_land, conv_a, a_log, dt_bias, gnw, norm2_w, final_w, late_weights, emit, start_after=()):
    wgrad = functools.partial(_mm, ta=True, out_dtype=BF16)
    h1, proj_a, proj_z, proj_b = _in_proj(x, norm1_w, w_land, after=start_after, name="in_proj")
    qn, kn, v, gcb, bb = _gdn_prep_fwd(proj_a, conv_a, a_log, dt_bias, name="gdn_prep_fwd")
    uv, wk, at, tmat, wkb, qdb, keb = _gdn_chunk_fwd(qn, kn, v, gcb, bb, name="gdn_chunk_fwd")
    o, u, sp, oab = _gdn_scan_fwd(uv, at, wkb, qdb, keb, gcb, proj_z, gnw, name="gdn_scan_fwd")
    oab, lse = _attn_fwd(proj_b, oab, name="attn_fwd")
    w_out, w_up, conv_f, w_down = late_weights(oab)
    x1, h2 = _out_proj_norm(oab, w_out, x, norm2_w, name="out_proj")
    dx2, dx2_b, d_final, loss, ug, uu = _ffn_fwd(h2, x1, w_up, conv_f, w_down, final_w, tgt, name="ffn_fwd")
    du, g_down, g_up, dcw = _ffn_bwd(dx2_b, h2, ug, uu, conv_f, w_down, name="ffn_bwd")
    token = emit("ffn", w_down=g_down, w_up=g_up.reshape(N_DEV, FF_SLAB, -1), conv_f=dcw.reshape(N_DEV, FFN_CONV, -1))
    dx1, d_norm2 = _mm_slabs(du.reshape(N_DEV, -1, FF_SLAB), w_up, norm_bwd=(x1, norm2_w, dx2), after=token, name="ffn_up_dx")
    d_oab = _mm(dx1, w_out, tb=True, name="out_proj_dx", tn=D_MODEL, tk=1024)
    token = emit("out", w_out=wgrad(oab, dx1, name="out_proj_dw", tm=D_MODEL, tn=D_MODEL))
    dz, d_gnw, du, dwk, dat, dqd, dke, dgl = _gdn_scan_bwd(d_oab, o, proj_z, gnw, sp, u, at, wkb, qdb, keb, gcb, after=token, name="gdn_scan_bwd")
    dqn, dkn, dv, dg, dbeta = _gdn_chunk_bwd(qn, kn, gcb, bb, tmat, uv, wk, du, dwk, dat, dqd, dke, dgl, name="gdn_chunk_bwd")
    dc, dba, d_small = _gdn_prep_bwd(dqn, dkn, dv, dg, dbeta, proj_a, conv_a, a_log, dt_bias, name="gdn_prep_bwd")
    d_pa, d_conv_a = _gdn_conv_bwd(dc, dba, proj_a, conv_a, name="gdn_conv_bwd")
    d_pb = _attn_bwd(proj_b, oab, d_oab, lse, name="attn_bwd")
    g_a = wgrad(d_pa, h1, name="proj_a_dw", tm=A_COLS, tn=D_MODEL)
    g_z = wgrad(dz, h1, name="proj_z_dw", tn=D_MODEL)
    g_b = wgrad(d_pb, h1, name="proj_b_dw", tm=768, tn=D_MODEL)
    token = emit("in", w_a=g_a, w_z=g_z, w_b=g_b, conv_a=d_conv_a)
    grad_x, d_norm1 = _in_proj_dx((d_pa, dz, d_pb), w_land, x, norm1_w, dx1, after=token, name="in_proj_dx")
    small = dict(norm1=d_norm1, small=d_small, gnw=d_gnw, norm2=d_norm2, final=d_final)
    return loss, grad_x, small


_O1 = 3 * GDN_WIDTH
_O2 = _O1 + GDN_WIDTH
_O3 = _O2 + 2 * GDN_HEADS


_W_IN_ROWS = (A_COLS, GDN_WIDTH, 3 * DIL_WIDTH)
_IN_ROWS = (_O3 + 3 * DIL_WIDTH) // N_DEV
_ROW_TILE = 16
_IN_STEP = _IN_ROWS - _IN_ROWS % _ROW_TILE
_GAP = 8
_LAND_ROWS = 464
assert _O3 % _ROW_TILE == _ROW_TILE - _GAP and N_DEV - 1 + _GAP + _IN_ROWS <= _LAND_ROWS and _LAND_ROWS % _ROW_TILE == 0


def _padded_row(r):
    return r + (_GAP if r >= _O3 else 0)


def _shifted_slab(w_rows, j, *, name):
    q = w_rows.shape[0] // _IN_ROWS

    def body(j_ref, w_ref, o_ref, pad_ref):
        pad_ref[...] = jnp.zeros_like(pad_ref)
        for dev in range(N_DEV):
            @pl.when(j_ref[0] == dev)
            def _(dev=dev):
                p = lax.broadcasted_iota(jnp.int32, (_LAND_ROWS, 1), 0) + _IN_STEP * dev
                for s in range(q):
                    pad_ref[0:_IN_ROWS, :] = w_ref[pl.ds(s, _IN_ROWS, stride=q), :]
                    rows = pad_ref[...]
                    a = pltpu.roll(rows, dev, 0) if dev else rows
                    b = pltpu.roll(rows, dev + _GAP, 0)
                    o_ref[:, 128 * s:128 * (s + 1)] = jnp.where(p < _O3, a, jnp.where(p >= _O3 + _GAP, b, 0.0)).astype(BF16)

    vm = pl.BlockSpec(memory_space=pltpu.VMEM)
    return pl.pallas_call(
        body, name=name, in_specs=[pl.BlockSpec(memory_space=pltpu.SMEM), vm], out_specs=vm,
        out_shape=_sds((_LAND_ROWS, 128 * q), BF16), scratch_shapes=[pltpu.VMEM((_LAND_ROWS, 128), F32)],
    )(j, w_rows)


def _w_in_plan():
    def dest(p):
        if p < _O1:
            return 0, p
        if p < _O2:
            return 1, p - _O1
        if p < _O2 + _ROW_TILE:
            return 0, _O1
        q = p - _O3 - _GAP
        t, pair = divmod(q // 128, DIL_PAIRS)
        return 2, (3 * pair + t) * 128 + q % 128

    spans = [(_padded_row(_IN_ROWS * j), _padded_row(_IN_ROWS * (j + 1) - 1) + 1) for j in range(N_DEV)]
    runs, seams = [], []
    for p in range(0, spans[-1][1], _ROW_TILE):
        owners = [j for j, (lo, hi) in enumerate(spans) if lo < p + _ROW_TILE and hi > p]
        w, r = dest(p)
        if len(owners) == 2:
            seams.append((w, r, owners[0], p - _IN_STEP * owners[0], owners[1], p - _IN_STEP * owners[1]))
            continue
        (j,) = owners
        last = runs[-1] if runs else None
        if last and last[0] == j and last[2] == w and last[3] + last[4] == r and last[1] + last[4] == p - _IN_STEP * j:
            runs[-1] = last[:4] + (last[4] + _ROW_TILE,)
        else:
            runs.append((j, p - _IN_STEP * j, w, r, _ROW_TILE))
    return runs, seams


def _w_in_scratch(d):
    return [pltpu.VMEM((n, d), BF16) for n in _W_IN_ROWS] + [pltpu.VMEM((N_DEV - 1, _ROW_TILE, d), BF16),
                                                              pltpu.SemaphoreType.DMA(())]


def _fetch_w_in(land_ref, wa_ref, wz_ref, wb_ref, seam_ref, sem):
    w_refs = (wa_ref, wz_ref, wb_ref)
    runs, seams = _w_in_plan()
    copies = [pltpu.make_async_copy(land_ref.at[j, pl.ds(s, n)], w_refs[w].at[pl.ds(r, n)], sem) for j, s, w, r, n in runs]
    for k, (w, r, j0, s0, j1, s1) in enumerate(seams):
        copies.append(pltpu.make_async_copy(land_ref.at[j0, pl.ds(s0, _ROW_TILE)], w_refs[w].at[pl.ds(r, _ROW_TILE)], sem))
        copies.append(pltpu.make_async_copy(land_ref.at[j1, pl.ds(s1, _ROW_TILE)], seam_ref.at[k], sem))
    for cp in copies:
        cp.start()
    tail = _O1 + _ROW_TILE
    wa_ref[tail:, :] = jnp.zeros((A_COLS - tail, wa_ref.shape[1]), BF16)
    for cp in copies:
        cp.wait()
    for k, (w, r, *_) in enumerate(seams):
        both = w_refs[w][r:r + _ROW_TILE, :].astype(F32) + seam_ref[k].astype(F32)
        w_refs[w][r:r + _ROW_TILE, :] = both.astype(BF16)


MESH = pl.DeviceIdType.MESH
ANY = pl.BlockSpec(memory_space=pl.ANY)


def _position():
    return lax.axis_index("x"), lax.axis_index("y"), lax.axis_index("c")


def _slot(p):
    return 4 * p[0] + 2 * p[1] + p[2]


def _all_gather(blocks, *, name):
    n = len(blocks)

    def body(*refs):
        ins, outs = refs[:n], refs[n:2 * n]
        send_sems, recv_sems, local_sems = refs[2 * n:]
        x, y, c = _position()
        me, sibling = (x, y, c), (x, y, 1 - c)
        chips = [(1 - x, y), (x, 1 - y), (1 - x, 1 - y)]

        def copy(a, k, block, to, src=None):
            dst = outs[a].at[_slot(block)]
            return pltpu.make_async_remote_copy(
                src_ref=dst if src is None else src, dst_ref=dst, send_sem=send_sems.at[a, k], recv_sem=recv_sems.at[a, k],
                device_id=to, device_id_type=MESH)

        mine = [pltpu.make_async_copy(ins[a], outs[a].at[_slot(me)], local_sems.at[a]) for a in range(n)]
        for cp in mine:
            cp.start()
        first = []
        for a in range(n):
            first.append(copy(a, 0, me, sibling, src=ins[a]))
            first += [copy(a, 1 + j, me, (*chip, c), src=ins[a]) for j, chip in enumerate(chips)]
        for cp in first:
            cp.start()
        passed = []
        for j, chip in enumerate(chips):
            for a in range(n):
                copy(a, 1 + j, (*chip, c), me).wait_recv()
                fwd = copy(a, 4 + j, (*chip, c), sibling)
                fwd.start()
                passed.append(fwd)
        for a in range(n):
            copy(a, 0, sibling, me).wait_recv()
            for j, chip in enumerate(chips):
                copy(a, 4 + j, (*chip, 1 - c), me).wait_recv()
        for cp in first + passed:
            cp.wait_send()
        for cp in mine:
            cp.wait()

    return pl.pallas_call(
        body, name=name, in_specs=[ANY] * n, out_specs=[ANY] * n,
        out_shape=[_sds((N_DEV,) + b.shape, b.dtype) for b in blocks],
        scratch_shapes=[pltpu.SemaphoreType.DMA((n, 7)), pltpu.SemaphoreType.DMA((n, 7)), pltpu.SemaphoreType.DMA((n,))],
    )(*blocks)


def _gather_direct(block, *, name, after=()):
    def body(in_ref, *rest):
        out_ref, send_sems, recv_sems, local_sem = rest[len(after):]
        x, y, c = _position()
        me = _slot((x, y, c))
        mine = pltpu.make_async_copy(in_ref, out_ref.at[me], local_sem)
        mine.start()
        copies = [pltpu.make_async_remote_copy(
            src_ref=in_ref, dst_ref=out_ref.at[me], send_sem=send_sems.at[k - 1], recv_sem=recv_sems.at[k - 1],
            device_id=_peer_of(k, x, y, c), device_id_type=MESH) for k in range(1, N_DEV)]
        for cp in copies:
            cp.start()
        for cp in copies:
            cp.wait()
        mine.wait()

    return pl.pallas_call(
        body, name=name, in_specs=[pl.BlockSpec(memory_space=pltpu.VMEM)] + [ANY] * len(after),
        out_specs=pl.BlockSpec(memory_space=pltpu.VMEM),
        out_shape=_sds((N_DEV,) + block.shape, block.dtype),
        scratch_shapes=[pltpu.SemaphoreType.DMA((N_DEV - 1,)), pltpu.SemaphoreType.DMA((N_DEV - 1,)), pltpu.SemaphoreType.DMA],
    )(block, *after)


HBM = pl.BlockSpec(memory_space=pltpu.HBM)
SEM = pl.BlockSpec(memory_space=pltpu.SEMAPHORE)
EFFECT = pltpu.SideEffectType.DATAFLOW_SIDE_EFFECTING


def _peer_of(k, x, y, c):
    return (1 - x if k & 4 else x, 1 - y if k & 2 else y, 1 - c if k & 1 else c)


def _flight(a, k):
    return a * (N_DEV - 1) + k - 1


def _exchange_start(arrays, *, name, broadcast=False, paired=()):
    n = len(arrays)

    def body(*refs):
        ins, lands = refs[:n], refs[n:2 * n]
        send_sems, recv_sems = refs[2 * n:2 * n + 2]
        token = refs[-1]
        x, y, c = _position()
        me = _slot((x, y, c))
        for k in range(1, N_DEV):
            peer = _peer_of(k, x, y, c)
            for a in range(n):
                at = _pair_slot(_slot(peer)) if a in paired else _slot(peer)
                pltpu.make_async_remote_copy(
                    src_ref=ins[a] if broadcast else ins[a].at[at], dst_ref=lands[a].at[me],
                    send_sem=send_sems.at[_flight(a, k)], recv_sem=recv_sems.at[_flight(a, k)],
                    device_id=peer, device_id_type=MESH).start()
        token[...] = jnp.zeros_like(token)

    land_shapes = [((N_DEV,) + s.shape) if broadcast else s.shape for s in arrays]
    lands = [pltpu.with_memory_space_constraint(lax.empty(shp, s.dtype), pltpu.HBM) for shp, s in zip(land_shapes, arrays)]
    srcs = [pltpu.with_memory_space_constraint(s, pltpu.HBM) for s in arrays]
    outs = pl.pallas_call(
        body, name=name, in_specs=[HBM] * (2 * n),
        out_specs=[SEM, SEM] + [HBM] * (2 * n) + [pl.BlockSpec(memory_space=pltpu.VMEM)],
        out_shape=[pltpu.SemaphoreType.DMA((n * (N_DEV - 1),)), pltpu.SemaphoreType.DMA((n * (N_DEV - 1),))]
        + [pltpu.HBM(s.shape, s.dtype) for s in arrays] + [pltpu.HBM(shp, s.dtype) for shp, s in zip(land_shapes, arrays)]
        + [_sds((8, 128))],
        input_output_aliases={i: 2 + i for i in range(2 * n)},
        compiler_params=pltpu.CompilerParams(has_side_effects=EFFECT),
    )(*srcs, *lands)
    return outs[0], outs[1], outs[2:2 + n], outs[2 + n:2 + 2 * n], outs[-1]


def _exchange_wait(send_sems, recv_sems, srcs, lands, after, *, name, broadcast=False):
    n = len(srcs)

    def body(*refs):
        ins, lnd = refs[:n], refs[n:2 * n]
        send_ref, recv_ref = refs[2 * n:2 * n + 2]
        x, y, c = _position()
        for k in range(1, N_DEV):
            for a in range(n):
                cp = pltpu.make_async_remote_copy(
                    src_ref=ins[a] if broadcast else ins[a].at[0], dst_ref=lnd[a].at[0], send_sem=send_ref.at[_flight(a, k)],
                    recv_sem=recv_ref.at[_flight(a, k)], device_id=_peer_of(k, x, y, c), device_id_type=MESH)
                cp.wait_send()
                cp.wait_recv()

    outs = pl.pallas_call(
        body, name=name, in_specs=[HBM] * (2 * n) + [SEM, SEM, ANY], out_specs=[HBM] * (2 * n),
        out_shape=[pltpu.HBM(s.shape, s.dtype) for s in srcs] + [pltpu.HBM(s.shape, s.dtype) for s in lands],
        input_output_aliases={i: i for i in range(2 * n)},
        compiler_params=pltpu.CompilerParams(has_side_effects=EFFECT),
    )(*srcs, *lands, send_sems, recv_sems, after)
    return outs[:n], outs[n:]


_G_LAND_ROWS = 528


def _g_in_pieces(dev):
    runs, seams = _w_in_plan()
    pieces = [(w, r, n, s) for j, s, w, r, n in runs if j == dev]
    pieces += [(w, r, _ROW_TILE, s0) for w, r, j0, s0, j1, s1 in seams if j0 == dev]
    pieces += [(w, r, _ROW_TILE, s1) for w, r, j0, s0, j1, s1 in seams if j1 == dev]
    merged = []
    for w, r, n, s in sorted(pieces, key=lambda p: p[3]):
        if merged and merged[-1][0] == w and merged[-1][1] + merged[-1][2] == r and merged[-1][3] + merged[-1][2] == s:
            merged[-1] = (w, merged[-1][1], merged[-1][2] + n, merged[-1][3])
        else:
            merged.append((w, r, n, s))
    return merged


def _coords(dev):
    return tuple(jnp.int32(v) for v in (dev >> 2, (dev >> 1) & 1, dev & 1))


def _exchange_start_in(g_ws, conv_slabs, *, name):
    srcs = list(g_ws) + [conv_slabs]
    n = len(srcs)

    def body(*refs):
        g_refs, cv_ref, land, land_cv = refs[:n - 1], refs[n - 1], refs[n], refs[n + 1]
        send_sems, recv_sems = refs[n + 2:n + 4]
        token = refs[-1]
        me = _slot(_position())
        pieces = [_g_in_pieces(dev) for dev in range(N_DEV)]
        for i in range(max(len(p) for p in pieces)):
            for dev in range(N_DEV):
                if i < len(pieces[dev]):
                    @pl.when(me != dev)
                    def _(dev=dev, i=i):
                        w, r, rows, s = pieces[dev][i]
                        k = me ^ dev
                        pltpu.make_async_remote_copy(
                            src_ref=g_refs[w].at[pl.ds(r, rows)], dst_ref=land.at[me, pl.ds(s, rows)],
                            send_sem=send_sems.at[_flight(0, k)], recv_sem=recv_sems.at[_flight(0, k)],
                            device_id=_coords(dev), device_id_type=MESH).start()
        for dev in range(N_DEV):
            @pl.when(me != dev)
            def _(dev=dev):
                k = me ^ dev
                pltpu.make_async_remote_copy(
                    src_ref=cv_ref.at[dev], dst_ref=land_cv.at[me], send_sem=send_sems.at[_flight(1, k)],
                    recv_sem=recv_sems.at[_flight(1, k)], device_id=_coords(dev), device_id_type=MESH).start()
        token[...] = jnp.zeros_like(token)

    lands = [lax.empty((N_DEV, _G_LAND_ROWS, g_ws[0].shape[1]), g_ws[0].dtype), lax.empty(conv_slabs.shape, conv_slabs.dtype)]
    ops = [pltpu.with_memory_space_constraint(a, pltpu.HBM) for a in srcs + lands]
    outs = pl.pallas_call(
        body, name=name, in_specs=[HBM] * len(ops),
        out_specs=[SEM, SEM] + [HBM] * len(ops) + [pl.BlockSpec(memory_space=pltpu.VMEM)],
        out_shape=[pltpu.SemaphoreType.DMA((2 * (N_DEV - 1),)), pltpu.SemaphoreType.DMA((2 * (N_DEV - 1),))]
        + [pltpu.HBM(a.shape, a.dtype) for a in ops] + [_sds((8, 128))],
        input_output_aliases={i: 2 + i for i in range(len(ops))},
        compiler_params=pltpu.CompilerParams(has_side_effects=EFFECT),
    )(*ops)
    return outs[0], outs[1], outs[2:2 + n], outs[2 + n:4 + n], outs[-1]


def _own_pieces(g_ws, land, after, *, name):
    n = len(g_ws)

    def body(*refs):
        g_refs, land_ref = refs[:n], refs[n]
        token, slab, sem = refs[-3:]
        me = _slot(_position())
        for dev in range(N_DEV):
            @pl.when(me == dev)
            def _(dev=dev):
                own = [pltpu.make_async_copy(g_refs[w].at[pl.ds(r, rows)], slab.at[pl.ds(s, rows)], sem)
                       for w, r, rows, s in _g_in_pieces(dev)]
                for cp in own:
                    cp.start()
                for cp in own:
                    cp.wait()
                out = pltpu.make_async_copy(slab, land_ref.at[dev, pl.ds(0, _LAND_ROWS)], sem)
                out.start()
                out.wait()
        token[...] = jnp.zeros_like(token)

    return pl.pallas_call(
        body, name=name, in_specs=[HBM] * (n + 1) + [ANY], out_specs=[HBM, pl.BlockSpec(memory_space=pltpu.VMEM)],
        out_shape=[pltpu.HBM(land.shape, land.dtype), _sds((8, 128))], input_output_aliases={n: 0},
        scratch_shapes=[pltpu.VMEM((_LAND_ROWS, land.shape[2]), land.dtype), pltpu.SemaphoreType.DMA(())],
    )(*g_ws, land, after)


def _exchange_wait_in(send_sems, recv_sems, srcs, lands, after, *, name):
    n = len(srcs)

    def body(*refs):
        g_refs, cv_ref, land, land_cv = refs[:n - 1], refs[n - 1], refs[n], refs[n + 1]
        send_ref, recv_ref = refs[n + 2:n + 4]
        me = _slot(_position())

        def copies(dev, k):
            cps = [pltpu.make_async_remote_copy(
                src_ref=g_refs[w].at[pl.ds(r, rows)], dst_ref=land.at[0, pl.ds(s, rows)], send_sem=send_ref.at[_flight(0, k)],
                recv_sem=recv_ref.at[_flight(0, k)], device_id=_coords(dev), device_id_type=MESH)
                for w, r, rows, s in _g_in_pieces(dev)]
            return cps + [pltpu.make_async_remote_copy(
                src_ref=cv_ref.at[0], dst_ref=land_cv.at[0], send_sem=send_ref.at[_flight(1, k)],
                recv_sem=recv_ref.at[_flight(1, k)], device_id=_coords(dev), device_id_type=MESH)]

        for dev in range(N_DEV):
            @pl.when(me != dev)
            def _(dev=dev):
                for cp in copies(dev, me ^ dev):
                    cp.wait_send()

            @pl.when(me == dev)
            def _(dev=dev):
                for k in range(1, N_DEV):
                    for cp in copies(dev, k):
                        cp.wait_recv()

    ops = list(srcs) + list(lands)
    outs = pl.pallas_call(
        body, name=name, in_specs=[HBM] * len(ops) + [SEM, SEM, ANY], out_specs=[HBM] * len(ops),
        out_shape=[pltpu.HBM(a.shape, a.dtype) for a in ops],
        input_output_aliases={i: i for i in range(len(ops))},
        compiler_params=pltpu.CompilerParams(has_side_effects=EFFECT),
    )(*ops, send_sems, recv_sems, after)
    return outs[:n], outs[n:]


def _with_own(landed, srcs, me):
    return [lax.dynamic_update_index_in_dim(l, o, me, 0) for l, o in zip(landed, srcs)]


def _adam_update(g, w, m, v):
    c1 = 1.0 - ADAM_B1 ** ADAM_STEP
    c2 = 1.0 - ADAM_B2 ** ADAM_STEP
    nm = ADAM_B1 * m + (1.0 - ADAM_B1) * g
    nv = ADAM_B2 * v + (1.0 - ADAM_B2) * (g * g)
    return -ADAM_LR * ((nm / c1) / (jnp.sqrt(nv / c2) + ADAM_EPS) + ADAM_WD * w), nm, nv


def _adamw(landed, sent, me, w, m, v, *, name, tr=None, tc=None):
    R, C = w.shape
    tr = R if tr is None else tr
    tc = C if tc is None else tc
    assert R % tr == 0 and C % tc == 0

    def body(me_ref, own_ref, p_ref, w_ref, m_ref, v_ref, g_ref, d_ref, nm_ref, nv_ref, token_ref):
        token_ref[...] = jnp.zeros_like(token_ref)
        g = own_ref[...].astype(F32)
        for s in range(N_DEV):
            g = g + jnp.where(me_ref[1] == s, 0.0, p_ref[s].astype(F32))
        delta, nm, nv = _adam_update(g, w_ref[...], m_ref[...], v_ref[...])
        g_ref[...] = g
        nm_ref[...] = nm
        nv_ref[...] = nv
        d_ref[...] = delta

    blk = pl.BlockSpec((tr, tc), lambda i, j, me_ref: (i, j))
    return pl.pallas_call(
        body, name=name,
        grid_spec=pltpu.PrefetchScalarGridSpec(
            num_scalar_prefetch=1, grid=(R // tr, C // tc),
            in_specs=[pl.BlockSpec((None, tr, tc), lambda i, j, me_ref: (me_ref[0], i, j)),
                      pl.BlockSpec((N_DEV, tr, tc), lambda i, j, me_ref: (0, i, j)), blk, blk, blk],
            out_specs=[blk] * 4 + [pl.BlockSpec((8, 128), lambda i, j, me_ref: (0, 0))]),
        out_shape=[_sds((R, C))] * 4 + [_sds((8, 128))],
        compiler_params=_params(("arbitrary", "arbitrary")),
    )(me, sent, landed, w, m, v)


def _adamw_rowwise(landed, me, w, m, v, *, name, tr=128):
    C = landed.shape[2]
    R, q, extra = _IN_ROWS, C // 128, _ROW_TILE
    assert tr % extra == 0 and (pl.cdiv(R, tr) * tr + extra) <= landed.shape[1] and N_DEV - 1 + _GAP < extra

    def body(me_ref, a_ref, b_ref, w_ref, m_ref, v_ref, g_ref, d_ref, nm_ref, nv_ref, g_scr):
        i = pl.program_id(0)
        total = lambda ref: functools.reduce(lambda x, y: x + y, [ref[s].astype(F32) for s in range(N_DEV)])
        slab = jnp.concatenate([total(a_ref), total(b_ref)], axis=0)
        for dev in range(N_DEV):
            @pl.when(me_ref[0] == dev)
            def _(dev=dev):
                lo, hi = slab[dev:dev + tr], slab[dev + _GAP:dev + _GAP + tr]
                if _IN_ROWS * (dev + 1) <= _O3:
                    g_scr[...] = lo
                elif _IN_ROWS * dev >= _O3:
                    g_scr[...] = hi
                else:
                    r = _IN_ROWS * dev + tr * i + lax.broadcasted_iota(jnp.int32, (tr, 1), 0)
                    g_scr[...] = jnp.where(r < _O3, lo, hi)
        g = g_scr[...]
        for s in range(q):
            rows = pl.ds(s, tr, stride=q)
            gs = g[:, 128 * s:128 * (s + 1)]
            delta, nm, nv = _adam_update(gs, w_ref[rows, :], m_ref[rows, :], v_ref[rows, :])
            g_ref[rows, :] = gs
            nm_ref[rows, :] = nm
            nv_ref[rows, :] = nv
            d_ref[rows, :] = delta

    blk = pl.BlockSpec((tr * q, 128), lambda i, me_ref: (i, 0))
    return pl.pallas_call(
        body, name=name,
        grid_spec=pltpu.PrefetchScalarGridSpec(
            num_scalar_prefetch=1, grid=(pl.cdiv(R, tr),),
            in_specs=[pl.BlockSpec((N_DEV, tr, C), lambda i, me_ref: (0, i, 0)),
                      pl.BlockSpec((N_DEV, extra, C), lambda i, me_ref: (0, (tr // extra) * (i + 1), 0)), blk, blk, blk],
            out_specs=[blk] * 4, scratch_shapes=[pltpu.VMEM((tr, C), F32)]),
        out_shape=[_sds((R * q, 128))] * 4,
        compiler_params=_params(("arbitrary",)),
    )(me, landed, landed, w, m, v)


_SMALL_ROWS = 8
_SMALL_SLOTS = ((0, 0, D_MODEL), (1, 0, D_MODEL), (2, 0, D_MODEL), (3, 0, GDN_DIM), (3, GDN_DIM, GDN_HEADS),
                (3, GDN_DIM + GDN_HEADS, GDN_HEADS))
_LOSS_LANE = 2 * GDN_DIM


def _pack_small(norm1, norm2, final, gnw, a_log, dt_bias, loss):
    row3 = jnp.concatenate([gnw, a_log, dt_bias, jnp.zeros((1, 128 - 2 * GDN_HEADS), F32), loss,
                            jnp.zeros((1, D_MODEL - 3 * 128), F32)], axis=1)
    return jnp.concatenate([norm1, norm2, final, row3, jnp.zeros((_SMALL_ROWS - 4, D_MODEL), F32)], axis=0)


def _adamw_small(packs, ws, ms, vs, *, name):
    n = len(ws)

    def body(p_ref, *refs):
        w_refs, m_refs, v_refs = refs[:n], refs[n:2 * n], refs[2 * n:3 * n]
        outs = refs[3 * n:]
        g_all = p_ref[0]
        for s in range(1, N_DEV):
            g_all = g_all + p_ref[s]
        for i, (row, lane, width) in enumerate(_SMALL_SLOTS):
            g = g_all[row:row + 1, lane:lane + width]
            delta, nm, nv = _adam_update(g, w_refs[i][...], m_refs[i][...], v_refs[i][...])
            for o_ref, val in zip(outs[4 * i:4 * i + 4], (g, delta, nm, nv)):
                o_ref[...] = val
        outs[-1][...] = g_all[3:4, _LOSS_LANE:_LOSS_LANE + 128]

    vm = pl.BlockSpec(memory_space=pltpu.VMEM)
    outs = pl.pallas_call(
        body, name=name, in_specs=[vm] * (1 + 3 * n), out_specs=[vm] * (4 * n + 1),
        out_shape=[_sds(w.shape) for w in ws for _ in range(4)] + [_sds((1, 128))],
    )(packs, *ws, *ms, *vs)
    return [outs[4 * i:4 * i + 4] for i in range(n)], outs[-1]


def _slabs_by_cols(g):
    r = g.shape[0]
    return g.reshape(r, N_DEV, -1).transpose(1, 0, 2)


def _cols_from_slabs(s):
    return s.transpose(1, 0, 2).reshape(s.shape[1], -1)


def kernel(x, norm1_w, w_in, conv_qkv_w, a_log, dt_bias, gdn_norm_w, w_out, norm2_w, w_up, ffn_conv_w, w_down, final_norm_w, loss_target, m_norm1_w, m_w_in, m_conv_qkv_w, m_a_log, m_dt_bias, m_gdn_norm_w, m_w_out, m_norm2_w, m_w_up, m_ffn_conv_w, m_w_down, m_final_norm_w, v_norm1_w, v_w_in, v_conv_qkv_w, v_a_log, v_dt_bias, v_gdn_norm_w, v_w_out, v_norm2_w, v_w_up, v_ffn_conv_w, v_w_down, v_final_norm_w):
    bf = lambda a: a.astype(BF16)
    me = _slot(_position())
    me1 = jnp.reshape(me, (1,)).astype(jnp.int32)
    t_in = lambda a: a[0].T
    rows = lambda a: a.reshape(D_MODEL // 128, 128, -1).transpose(2, 0, 1).reshape(-1, 128)
    gw_in, g_conv_a = _all_gather([_shifted_slab(rows(w_in), me1, name="shift_w_in"), conv_qkv_w[0]], name="gather_w_in")
    late_src, _ = lax.optimization_barrier(([bf(w_out[0]), bf(t_in(w_up)), bf(w_down[0]), ffn_conv_w[0]], gw_in))
    l_send, l_recv, l_srcs, l_lands, l_token = _exchange_start(late_src, name="weights_start", broadcast=True)

    def late_weights(after):
        srcs, landed = _exchange_wait(l_send, l_recv, l_srcs, l_lands, after, name="weights_wait", broadcast=True)
        gw_out, gw_up, gw_down, g_conv_f = _with_own(landed, srcs, me)
        return gw_out.reshape(D_MODEL, D_MODEL), gw_up, g_conv_f, gw_down.reshape(D_FF, D_MODEL)

    flights = {}

    def emit(group, **grads):
        paired = ()
        if group == "in":
            *flight, token = _exchange_start_in([grads["w_a"], grads["w_z"], grads["w_b"]], _slabs_by_cols(grads["conv_a"]),
                                                name="grads_start_in")
            flights[group] = flight
            return (token,)
        if group == "ffn":
            slabs = dict(w_down=grads["w_down"].reshape(N_DEV, -1, D_MODEL), w_up=grads["w_up"], conv_f=grads["conv_f"])
            paired = (1, 2)
        else:
            slabs = {k: v.reshape(N_DEV, -1, D_MODEL) for k, v in grads.items()}
        names = list(slabs)
        *flight, token = _exchange_start([slabs[k] for k in names], paired=paired, name="grads_start_" + group)
        flights[group] = (names, flight)
        return (token,)

    loss, grad_x, g = _local_step(
        x[0], loss_target[0], norm1_w, gw_in, _cols_from_slabs(g_conv_a), a_log, dt_bias,
        gdn_norm_w, norm2_w, final_norm_w[None], late_weights, emit, start_after=(l_token,))
    got = {}

    def collect(group, after):
        names, (send_sems, recv_sems, srcs, lands) = flights[group]
        srcs, landed = _exchange_wait(send_sems, recv_sems, srcs, lands, after, name="grads_wait_" + group)
        got.update(zip(names, zip(landed, srcs)))

    def update(key, w, m, v, paired=False, **tiles):
        where = jnp.concatenate([_pair_slot(me1) if paired else me1, me1])
        return _adamw(*got[key], where, w, m, v, name="adamw_" + key, **tiles)

    in_sems, in_srcs, in_lands = flights["in"][:2], flights["in"][2], flights["in"][3]
    own_land, own_token = _own_pieces(in_srcs[:3], in_lands[0], grad_x, name="grads_own_in")
    collect("ffn", own_token)
    collect("out", own_token)
    *o_out, t1 = update("w_out", w_out[0], m_w_out[0], v_w_out[0])
    *o_up, t2 = update("w_up", t_in(w_up), t_in(m_w_up), t_in(v_w_up), paired=True, tr=176)
    o_up = [o.T for o in o_up]
    *o_down, t3 = update("w_down", w_down[0], m_w_down[0], v_w_down[0], tr=176)
    *o_cf, t4 = update("conv_f", ffn_conv_w[0], m_ffn_conv_w[0], v_ffn_conv_w[0], paired=True)
    pack = _pack_small(g["norm1"], g["norm2"], g["final"], g["gnw"], g["small"][:, 0:GDN_HEADS],
                       g["small"][:, GDN_HEADS:2 * GDN_HEADS], loss)
    small_all = _gather_direct(pack, after=(t1, t2, t3, t4), name="gather_small")
    srcs, (g_land, conv_land) = _exchange_wait_in(*in_sems, in_srcs, [own_land, in_lands[1]], small_all, name="grads_wait_in")
    got["conv_a"] = (conv_land, srcs[-1])
    o_in = [o.reshape(-1, D_MODEL // 128, 128).transpose(1, 2, 0).reshape(D_MODEL, -1) for o in _adamw_rowwise(
        g_land, me1, rows(w_in), rows(m_w_in), rows(v_w_in), name="adamw_w_in")]
    o_ca = update("conv_a", conv_qkv_w[0], m_conv_qkv_w[0], v_conv_qkv_w[0])
    (o_n1, o_n2, o_fin, o_gn, o_al, o_dt), total = _adamw_small(
        small_all, (norm1_w, norm2_w, final_norm_w[None], gdn_norm_w, a_log, dt_bias),
        (m_norm1_w, m_norm2_w, m_final_norm_w[None], m_gdn_norm_w, m_a_log, m_dt_bias),
        (v_norm1_w, v_norm2_w, v_final_norm_w[None], v_gdn_norm_w, v_a_log, v_dt_bias), name="adamw_small")
    outs = [total[0, 0], grad_x[None]]
    for k in range(4):
        outs += [o_n1[k], o_in[k][None], o_ca[k][None], o_al[k], o_dt[k], o_gn[k], o_out[k][None], o_n2[k], o_up[k][None],
                 o_cf[k][None], o_down[k][None], o_fin[k][0]]
    return tuple(outs)
```

```python
import functools

import jax
import jax.numpy as jnp
from jax import lax
from jax.experimental import pallas as pl
from jax.experimental.pallas import tpu as pltpu

F32 = jnp.float32
BF16 = jnp.bfloat16

N_DEV = 8
D_MODEL = 1024
GDN_HEADS = 4
GDN_DIM = 128
GDN_WIDTH = GDN_HEADS * GDN_DIM
GDN_CONV = 4
CHUNK = 64
CHUNKS_PER_STEP = 4
DIL_HEADS = 8
DIL_DIM = 64
DIL_WIDTH = DIL_HEADS * DIL_DIM
DIL_PAIRS = DIL_HEADS // 2
DILATIONS = (1, 4, 16)
BAND = 128
D_FF = 2816
FFN_CONV = 3
EPS = 1e-6
A_COLS = 3 * GDN_WIDTH + 128
HALO = 8

ADAM_LR = 0.001
ADAM_B1 = 0.9
ADAM_B2 = 0.999
ADAM_EPS = 1e-08
ADAM_WD = 0.01
ADAM_STEP = 10

VMEM_LIMIT_BYTES = 56 * 1024 * 1024
NEG_BIG = -1e30


def _params(sem=None):
    return pltpu.CompilerParams(dimension_semantics=sem, vmem_limit_bytes=VMEM_LIMIT_BYTES)


def _sds(shape, dtype=F32):
    return jax.ShapeDtypeStruct(shape, dtype)


def _bdot(a, b):
    return jnp.dot(a.astype(BF16), b.astype(BF16), preferred_element_type=F32)


def _bdot_nt(a, b):
    return lax.dot_general(a.astype(BF16), b.astype(BF16), (((1,), (1,)), ((), ())), preferred_element_type=F32)


def _bdot_tn(a, b):
    return lax.dot_general(a.astype(BF16), b.astype(BF16), (((0,), (0,)), ((), ())), preferred_element_type=F32)


def _split(a):
    hi = a.astype(BF16)
    lo = (a - hi.astype(F32)).astype(BF16)
    return hi, lo


def _dot3(a, b, dims):
    ah, al = _split(a)
    bh, bl = _split(b)
    d = functools.partial(lax.dot_general, dimension_numbers=(dims, ((), ())), preferred_element_type=F32)
    return d(ah, bh) + (d(al, bh) + d(ah, bl))


def _exact_tri_dot(tri, g):
    g1 = g.astype(BF16)
    r1 = g - g1.astype(F32)
    g2 = r1.astype(BF16)
    g3 = (r1 - g2.astype(F32)).astype(BF16)
    t = tri.astype(BF16)
    d = functools.partial(jnp.dot, preferred_element_type=F32)
    return d(t, g1) + (d(t, g2) + d(t, g3))


def _sigmoid(x):
    return 1.0 / (1.0 + jnp.exp(-x))


def _dsilu(x, sg):
    return sg * (1.0 + x * (1.0 - sg))


def _rms_bwd_rows(dh, x, w):
    r = lax.rsqrt(jnp.mean(x * x, axis=-1, keepdims=True) + EPS)
    xh = x * r
    gw = dh * w
    return r * (gw - xh * jnp.mean(gw * xh, axis=-1, keepdims=True)), jnp.sum(dh * xh, axis=0, keepdims=True)


def _mm(a, b, *, name, ta=False, tb=False, res=None, norm_bwd=None, after=(), out_dtype=F32, tm=512, tn=512, tk=512):
    if ta:
        K, M = a.shape
    else:
        M, K = a.shape
    if tb:
        N, Kb = b.shape
    else:
        Kb, N = b.shape
    assert K == Kb, (a.shape, b.shape)
    tm, tn, tk = min(tm, M), min(tn, N), min(tk, K)
    assert M % tm == 0 and N % tn == 0 and K % tk == 0, (name, M, N, K, tm, tn, tk)
    nk = K // tk
    dims = (((0 if ta else 1,), (1 if tb else 0,)), ((), ()))
    has_res = res is not None
    has_norm = norm_bwd is not None
    assert not has_norm or tn == N

    def body(*refs):
        a_ref, b_ref = refs[:2]
        r_ref = refs[2] if has_res else None
        if has_norm:
            x_ref, w_ref, skip_ref = refs[2 + has_res:5 + has_res]
            o_ref, dw_ref, acc_ref = refs[-3:]
        else:
            o_ref, acc_ref = refs[-2:]
        i, k = pl.program_id(0), pl.program_id(2)
        part = lax.dot_general(a_ref[...].astype(BF16), b_ref[...].astype(BF16), dims, preferred_element_type=F32)

        @pl.when(k == 0)
        def _():
            acc_ref[...] = part

        @pl.when(k > 0)
        def _():
            acc_ref[...] += part

        @pl.when(k == nk - 1)
        def _():
            r = acc_ref[...]
            if has_res:
                r = r + r_ref[...]
            if has_norm:
                dx, dw = _rms_bwd_rows(r, x_ref[...], w_ref[...])
                o_ref[...] = skip_ref[...] + dx

                @pl.when(i == 0)
                def _():
                    dw_ref[...] = dw

                @pl.when(i > 0)
                def _():
                    dw_ref[...] += dw
            else:
                o_ref[...] = r.astype(out_dtype)

    a_spec = pl.BlockSpec((tk, tm), lambda i, j, k: (k, i)) if ta else pl.BlockSpec((tm, tk), lambda i, j, k: (i, k))
    b_spec = pl.BlockSpec((tn, tk), lambda i, j, k: (j, k)) if tb else pl.BlockSpec((tk, tn), lambda i, j, k: (k, j))
    o_spec = pl.BlockSpec((tm, tn), lambda i, j, k: (i, j))
    one = pl.BlockSpec((1, tn), lambda i, j, k: (0, 0))
    in_specs = [a_spec, b_spec] + [o_spec] * has_res + ([o_spec, one, o_spec] if has_norm else []) + [ANY] * len(after)
    args = (a, b) + ((res,) if has_res else ()) + (tuple(norm_bwd) if has_norm else ()) + tuple(after)
    return pl.pallas_call(
        body, name=name, grid=(M // tm, N // tn, nk), in_specs=in_specs,
        out_specs=[o_spec, one] if has_norm else o_spec,
        out_shape=[_sds((M, N)), _sds((1, N))] if has_norm else _sds((M, N), out_dtype),
        scratch_shapes=[pltpu.VMEM((tm, tn), F32)],
        compiler_params=_params(("arbitrary" if has_norm else "parallel", "parallel", "arbitrary")),
    )(*args)


def _in_proj(x, norm_w, w_land, *, name, after=(), tm=512):
    S, D = x.shape

    def body(x_ref, nw_ref, land_ref, *rest):
        h_ref, pa_ref, pz_ref, pb_ref, *scratch = rest[len(after):]

        @pl.when(pl.program_id(0) == 0)
        def _():
            _fetch_w_in(land_ref, *scratch)

        xv = x_ref[...]
        r = lax.rsqrt(jnp.mean(xv * xv, axis=-1, keepdims=True) + EPS)
        h = (xv * r * nw_ref[...]).astype(BF16)
        h_ref[...] = h
        for w_ref, p_ref in zip(scratch[:3], (pa_ref, pz_ref, pb_ref)):
            p_ref[...] = lax.dot_general(h, w_ref[...], (((1,), (1,)), ((), ())), preferred_element_type=F32)

    row = lambda n: pl.BlockSpec((tm, n), lambda i: (i, 0))
    full = lambda a: pl.BlockSpec(a.shape, lambda i: (0, 0))
    return pl.pallas_call(
        body, name=name, grid=(S // tm,), in_specs=[row(D), full(norm_w), ANY] + [ANY] * len(after),
        out_specs=[row(D)] + [row(n) for n in _W_IN_ROWS],
        out_shape=[_sds((S, D), BF16)] + [_sds((S, n)) for n in _W_IN_ROWS],
        scratch_shapes=_w_in_scratch(D), compiler_params=_params(("arbitrary",)),
    )(x, norm_w, w_land, *after)


def _in_proj_dx(ds, w_land, x, norm_w, skip, *, name, after=(), tm=512):
    S, D = x.shape
    n = len(ds)

    def body(*refs):
        d_refs, land_ref = refs[:n], refs[n]
        x_ref, nw_ref, skip_ref = refs[n + 1:n + 4]
        o_ref, dw_ref, *scratch = refs[n + 4 + len(after):]
        w_refs = scratch[:n]
        i = pl.program_id(0)

        @pl.when(i == 0)
        def _():
            _fetch_w_in(land_ref, *scratch)

        dh = jnp.dot(d_refs[0][...], w_refs[0][...], preferred_element_type=F32)
        for d_ref, w_ref in zip(d_refs[1:], w_refs[1:]):
            dh = dh + jnp.dot(d_ref[...], w_ref[...], preferred_element_type=F32)
        dx, dw = _rms_bwd_rows(dh, x_ref[...], nw_ref[...])
        o_ref[...] = skip_ref[...] + dx

        @pl.when(i == 0)
        def _():
            dw_ref[...] = dw

        @pl.when(i > 0)
        def _():
            dw_ref[...] += dw

    row = lambda c: pl.BlockSpec((tm, c), lambda i: (i, 0))
    full = lambda a: pl.BlockSpec(a.shape, lambda i: (0, 0))
    return pl.pallas_call(
        body, name=name, grid=(S // tm,),
        in_specs=[row(d.shape[1]) for d in ds] + [ANY, row(D), full(norm_w), row(D)] + [ANY] * len(after),
        out_specs=[row(D), pl.BlockSpec((1, D), lambda i: (0, 0))], out_shape=[_sds((S, D)), _sds((1, D))],
        scratch_shapes=_w_in_scratch(D), compiler_params=_params(("arbitrary",)),
    )(*ds, w_land, x, norm_w, skip, *after)


def _out_proj_norm(a, w, x, norm_w, *, name, tm=512):
    S, D = x.shape

    def body(a_ref, w_ref, x_ref, nw_ref, x1_ref, h_ref):
        x1 = x_ref[...] + jnp.dot(a_ref[...], w_ref[...], preferred_element_type=F32)
        x1_ref[...] = x1
        r = lax.rsqrt(jnp.mean(x1 * x1, axis=-1, keepdims=True) + EPS)
        h_ref[...] = (x1 * r * nw_ref[...]).astype(BF16)

    row = pl.BlockSpec((tm, D), lambda i: (i, 0))
    return pl.pallas_call(
        body, name=name, grid=(S // tm,),
        in_specs=[pl.BlockSpec((tm, a.shape[1]), lambda i: (i, 0)), pl.BlockSpec(w.shape, lambda i: (0, 0)), row,
                  pl.BlockSpec((1, D), lambda i: (0, 0))],
        out_specs=[row, row], out_shape=[_sds((S, D)), _sds((S, D), BF16)], compiler_params=_params(("parallel",)),
    )(a, w, x, norm_w)


def _shifted(x, start, n):
    aligned = -(-start // HALO) * HALO
    assert aligned + n <= x.shape[0], (start, n, x.shape)
    return (x if aligned == start else pltpu.roll(x, aligned - start, axis=0))[aligned:aligned + n]


def _conv_rows(prev, cur, w, taps):
    n = cur.shape[0]
    xs = jnp.concatenate([prev, cur], axis=0)
    base = HALO - (taps - 1)
    out = _shifted(xs, base, n) * w[0:1]
    for i in range(1, taps):
        out = out + _shifted(xs, base + i, n) * w[i:i + 1]
    return out


def _conv_rows_bwd(cur_d, next_d, prev_x, cur_x, w, taps):
    n = cur_d.shape[0]
    ds = jnp.concatenate([cur_d, next_d], axis=0)
    dx = _shifted(ds, taps - 1, n) * w[0:1]
    for i in range(1, taps):
        dx = dx + _shifted(ds, taps - 1 - i, n) * w[i:i + 1]
    xs = jnp.concatenate([prev_x, cur_x], axis=0)
    base = HALO - (taps - 1)
    dws = [jnp.sum(cur_d * _shifted(xs, base + i, n), axis=0, keepdims=True) for i in range(taps)]
    return dx, jnp.concatenate(dws, axis=0)


def _halo_specs(tm, width, col, nblk):
    per = tm // HALO
    prev = pl.BlockSpec((HALO, width), lambda i, *_: (jnp.maximum(i * per - 1, 0), col))
    nxt = pl.BlockSpec((HALO, width), lambda i, *_: (jnp.minimum((i + 1) * per, nblk * per - 1), col))
    return prev, nxt


def _softplus(x):
    return jnp.maximum(x, 0.0) + jnp.log1p(jnp.exp(-jnp.abs(x)))


def _chunk_tri(tm, upper=False):
    r = lax.broadcasted_iota(jnp.int32, (tm, tm), 0)
    c = lax.broadcasted_iota(jnp.int32, (tm, tm), 1)
    same = lax.div(r, CHUNK) == lax.div(c, CHUNK)
    order = (c >= r) if upper else (c <= r)
    return jnp.where(same & order, 1.0, 0.0)


def _gdn_prep_fwd(proj_a, conv_w, a_log, dt_bias, *, name, tm=256):
    S = proj_a.shape[0]
    nblk = S // tm
    W3 = 3 * GDN_WIDTH

    def body(cur_ref, prev_ref, ba_ref, cw_ref, al_ref, dt_ref, qn_ref, kn_ref, v_ref, gcb_ref, bb_ref):
        i = pl.program_id(0)
        prev = jnp.where(i > 0, prev_ref[...], 0.0)
        c = _conv_rows(prev, cur_ref[...], cw_ref[...], GDN_CONV)
        a = c * _sigmoid(c)
        ba = ba_ref[...]
        lane = lax.broadcasted_iota(jnp.int32, (tm, 128), 1)
        g4 = jnp.zeros((tm, 128), F32)
        for h in range(GDN_HEADS):
            sl = slice(GDN_DIM * h, GDN_DIM * (h + 1))
            qh = a[:, GDN_DIM * h:GDN_DIM * (h + 1)]
            kh = a[:, GDN_WIDTH + GDN_DIM * h:GDN_WIDTH + GDN_DIM * (h + 1)]
            qn_ref[:, sl] = qh * (lax.rsqrt(jnp.sum(qh * qh, axis=-1, keepdims=True) + EPS) * (GDN_DIM ** -0.5))
            kn_ref[:, sl] = kh * lax.rsqrt(jnp.sum(kh * kh, axis=-1, keepdims=True) + EPS)
            beta = _sigmoid(ba[:, h:h + 1])
            bb_ref[:, sl] = jnp.broadcast_to(beta, (tm, GDN_DIM))
            g = -jnp.exp(al_ref[0:1, h:h + 1]) * _softplus(ba[:, GDN_HEADS + h:GDN_HEADS + h + 1] + dt_ref[0:1, h:h + 1])
            g4 = jnp.where(lane == h, g, g4)
        v_ref[...] = a[:, 2 * GDN_WIDTH:]
        gc = _exact_tri_dot(_chunk_tri(tm), g4)
        for h in range(GDN_HEADS):
            gcb_ref[:, GDN_DIM * h:GDN_DIM * (h + 1)] = jnp.broadcast_to(gc[:, h:h + 1], (tm, GDN_DIM))

    prev_spec, _ = _halo_specs(tm, W3, 0, nblk)
    row = pl.BlockSpec((tm, GDN_WIDTH), lambda i: (i, 0))
    small = lambda a: pl.BlockSpec(a.shape, lambda i: (0, 0))
    return pl.pallas_call(
        body, name=name, grid=(nblk,),
        in_specs=[pl.BlockSpec((tm, W3), lambda i: (i, 0)), prev_spec,
                  pl.BlockSpec((tm, 128), lambda i: (i, W3 // 128)), small(conv_w), small(a_log), small(dt_bias)],
        out_specs=[row] * 5, out_shape=[_sds((S, GDN_WIDTH))] * 5, compiler_params=_params(("parallel",)),
    )(proj_a, proj_a, proj_a, conv_w, a_log, dt_bias)


GDN_STACK = GDN_HEADS * CHUNK


def _stack(ref, rows):
    return jnp.concatenate([ref[rows, GDN_DIM * h:GDN_DIM * (h + 1)] for h in range(GDN_HEADS)], axis=0)


def _unstack_to(ref, rows, x):
    for h in range(GDN_HEADS):
        ref[rows, GDN_DIM * h:GDN_DIM * (h + 1)] = x[CHUNK * h:CHUNK * (h + 1)].astype(ref.dtype)


def _stack_masks():
    r = lax.broadcasted_iota(jnp.int32, (GDN_STACK, GDN_STACK), 0)
    c = lax.broadcasted_iota(jnp.int32, (GDN_STACK, GDN_STACK), 1)
    same = (r & -CHUNK) == (c & -CHUNK)
    return same & (r >= c), same & (r > c), r == c


def _stack_decay(gs, bs, incl):
    g2 = jnp.concatenate([gs, gs], axis=1)
    diff = g2 - g2.T
    dec = jnp.where(incl, jnp.exp(jnp.where(incl, diff, 0.0)), 0.0)
    return dec, jnp.concatenate([bs, bs], axis=1).T


def _head_mask():
    r = lax.broadcasted_iota(jnp.int32, (GDN_STACK, GDN_WIDTH), 0)
    c = lax.broadcasted_iota(jnp.int32, (GDN_STACK, GDN_WIDTH), 1)
    return (r & -CHUNK) * (GDN_DIM // CHUNK) == (c & -GDN_DIM)


def _head_spread(x):
    return jnp.where(_head_mask(), jnp.concatenate([x] * GDN_HEADS, axis=1), 0.0)


def _head_diag(x):
    xm = jnp.where(_head_mask(), x, 0.0)
    out = xm[:, 0:GDN_DIM]
    for h in range(1, GDN_HEADS):
        out = out + xm[:, GDN_DIM * h:GDN_DIM * (h + 1)]
    return out


def _last_rows(gs, n):
    return jnp.concatenate([jnp.broadcast_to(gs[CHUNK * (h + 1) - 1:CHUNK * (h + 1)], (n, GDN_DIM)) for h in range(GDN_HEADS)], axis=0)


def _gdn_chunk_fwd(qn, kn, v, gcb, bb, *, name):
    S = qn.shape[0]

    def body(qn_ref, kn_ref, v_ref, gcb_ref, bb_ref, uv_ref, wk_ref, at_ref, t_ref, wkb_ref, qdb_ref, keb_ref):
        incl, strict, diag = _stack_masks()
        for c in range(CHUNKS_PER_STEP):
            rows = slice(CHUNK * c, CHUNK * (c + 1))
            srows = slice(GDN_STACK * c, GDN_STACK * (c + 1))
            q, k, vv, gs, bs = [_stack(r, rows) for r in (qn_ref, kn_ref, v_ref, gcb_ref, bb_ref)]
            dec, bt = _stack_decay(gs, bs, incl)
            p = -jnp.where(strict, dec * _bdot_nt(k, k) * bt, 0.0)
            t = jnp.where(diag, 1.0, 0.0) + p
            for _ in range(5):
                p = _bdot(p, p)
                t = t + _bdot(t, p)
            sol = _dot3(t, jnp.concatenate([vv, jnp.exp(gs) * k], axis=1), ((1,), (0,)))
            _unstack_to(uv_ref, rows, sol[:, :GDN_DIM])
            _unstack_to(wk_ref, rows, sol[:, GDN_DIM:])
            at_ref[srows, :] = dec * _bdot_nt(q, k) * bt
            t_ref[srows, :] = t
            wkb_ref[srows, :] = _head_spread(sol[:, GDN_DIM:]).astype(BF16)
            qdb_ref[srows, :] = _head_spread(q * jnp.exp(gs)).astype(BF16)
            keb_ref[srows, :] = _head_spread(k * jnp.exp(_last_rows(gs, CHUNK) - gs) * bs).astype(BF16)

    step = CHUNKS_PER_STEP * CHUNK
    row = pl.BlockSpec((step, GDN_WIDTH), lambda n: (n, 0))
    sq = pl.BlockSpec((CHUNKS_PER_STEP * GDN_STACK, GDN_STACK), lambda n: (n, 0))
    wide = pl.BlockSpec((CHUNKS_PER_STEP * GDN_STACK, GDN_WIDTH), lambda n: (n, 0))
    nsq = S // CHUNK * GDN_STACK
    return pl.pallas_call(
        body, name=name, grid=(S // step,), in_specs=[row] * 5, out_specs=[row, row, sq, sq, wide, wide, wide],
        out_shape=[_sds((S, GDN_WIDTH)), _sds((S, GDN_WIDTH)), _sds((nsq, GDN_STACK)), _sds((nsq, GDN_STACK))]
        + [_sds((nsq, GDN_WIDTH), BF16)] * 3,
        compiler_params=_params(("parallel",)),
    )(qn, kn, v, gcb, bb)


SCAN_CHUNKS = 8


def _gdn_scan_fwd(uv, at, wkb, qdb, keb, gcb, proj_z, gnw, *, name):
    S = uv.shape[0]
    nc = S // CHUNK

    def body(uv_ref, at_ref, wkb_ref, qdb_ref, keb_ref, gcb_ref, z_ref, gnw_ref, o_ref, u_ref, sp_ref, oa_ref, st_ref):
        n = pl.program_id(0)

        @pl.when(n == 0)
        def _():
            st_ref[...] = jnp.zeros_like(st_ref)

        for c in range(SCAN_CHUNKS):
            rows = slice(CHUNK * c, CHUNK * (c + 1))
            srows = slice(GDN_STACK * c, GDN_STACK * (c + 1))
            st = st_ref[...]
            sp_ref[GDN_WIDTH * c:GDN_WIDTH * (c + 1), :] = st
            uv, gs, z = [_stack(r, rows) for r in (uv_ref, gcb_ref, z_ref)]
            u = uv - _bdot(wkb_ref[srows, :], st)
            o = _bdot(qdb_ref[srows, :], st) + _bdot(at_ref[srows, :], u)
            st_ref[...] = jnp.exp(_last_rows(gs, GDN_DIM)) * st + _bdot_tn(keb_ref[srows, :], u)
            _unstack_to(u_ref, rows, u)
            _unstack_to(o_ref, rows, o)
            r = lax.rsqrt(jnp.mean(o * o, axis=-1, keepdims=True) + EPS)
            oa = o * r * gnw_ref[...] * (z * _sigmoid(z))
            oa_ref[rows, :] = jnp.concatenate([oa[CHUNK * h:CHUNK * (h + 1)] for h in range(GDN_HEADS)], axis=1).astype(BF16)

    row = pl.BlockSpec((SCAN_CHUNKS * CHUNK, GDN_WIDTH), lambda n: (n, 0))
    sq = pl.BlockSpec((SCAN_CHUNKS * GDN_STACK, GDN_STACK), lambda n: (n, 0))
    wide = pl.BlockSpec((SCAN_CHUNKS * GDN_STACK, GDN_WIDTH), lambda n: (n, 0))
    return pl.pallas_call(
        body, name=name, grid=(nc // SCAN_CHUNKS,),
        in_specs=[row, sq, wide, wide, wide, row, row, pl.BlockSpec((1, GDN_DIM), lambda n: (0, 0))],
        out_specs=[row, row, pl.BlockSpec((SCAN_CHUNKS * GDN_WIDTH, GDN_DIM), lambda n: (n, 0)), row],
        out_shape=[_sds((S, GDN_WIDTH)), _sds((S, GDN_WIDTH)), _sds((nc * GDN_WIDTH, GDN_DIM)), _sds((S, 2 * GDN_WIDTH), BF16)],
        scratch_shapes=[pltpu.VMEM((GDN_WIDTH, GDN_DIM), F32)],
        compiler_params=_params(("arbitrary",)),
    )(uv, at, wkb, qdb, keb, gcb, proj_z, gnw)


def _gdn_scan_bwd(d_oab, o, proj_z, gnw, sp, u, at, wkb, qdb, keb, gcb, *, name, after=()):
    S = o.shape[0]
    nc = S // CHUNK
    ns = nc // SCAN_CHUNKS

    def body(do_ref, o_ref, z_ref, gnw_ref, sp_ref, u_ref, at_ref, wkb_ref, qdb_ref, keb_ref, gcb_ref, *rest):
        dz_ref, dgn_ref, du_ref, dwk_ref, dat_ref, dqd_ref, dke_ref, dgl_ref, ds_ref = rest[len(after):]
        n = pl.program_id(0)

        @pl.when(n == 0)
        def _():
            ds_ref[...] = jnp.zeros_like(ds_ref)
            dgn_ref[...] = jnp.zeros_like(dgn_ref)

        gw = gnw_ref[...]
        for c in reversed(range(SCAN_CHUNKS)):
            rows = slice(CHUNK * c, CHUNK * (c + 1))
            srows = slice(GDN_STACK * c, GDN_STACK * (c + 1))
            d_oa, oo, z, uu, gs = [_stack(r, rows) for r in (do_ref, o_ref, z_ref, u_ref, gcb_ref)]
            sg = _sigmoid(z)
            r = lax.rsqrt(jnp.mean(oo * oo, axis=-1, keepdims=True) + EPS)
            xh = oo * r
            dy = d_oa * (z * sg)
            _unstack_to(dz_ref, rows, d_oa * (xh * gw) * _dsilu(z, sg))
            dgn_ref[...] += jnp.sum(dy * xh, axis=0, keepdims=True)
            dxh = dy * gw
            do = r * (dxh - xh * jnp.mean(dxh * xh, axis=-1, keepdims=True))

            st = sp_ref[GDN_WIDTH * c:GDN_WIDTH * (c + 1), :]
            dst = ds_ref[...]
            ge = jnp.exp(_last_rows(gs, GDN_DIM))
            _unstack_to(dqd_ref, rows, _head_diag(_bdot_nt(do, st)))
            dat_ref[srows, :] = _bdot_nt(do, uu)
            du = _bdot_tn(at_ref[srows, :], do) + _bdot(keb_ref[srows, :], dst)
            _unstack_to(dke_ref, rows, _head_diag(_bdot_nt(uu, dst)))
            prod = dst * st
            for h in range(GDN_HEADS):
                blk = prod[GDN_DIM * h:GDN_DIM * (h + 1)]
                dge = jnp.sum(jnp.sum(blk, axis=1, keepdims=True), axis=0, keepdims=True)
                dgl_ref[c, :, GDN_DIM * h:GDN_DIM * (h + 1)] = jnp.broadcast_to(dge * ge[GDN_DIM * h:GDN_DIM * h + 1], (8, GDN_DIM))
            ds_ref[...] = _bdot_tn(qdb_ref[srows, :], do) + ge * dst - _bdot_tn(wkb_ref[srows, :], du)
            _unstack_to(du_ref, rows, du)
            _unstack_to(dwk_ref, rows, -_head_diag(_bdot_nt(du, st)))

    rev = lambda n: (ns - 1 - n, 0)
    row = pl.BlockSpec((SCAN_CHUNKS * CHUNK, GDN_WIDTH), rev)
    sq = pl.BlockSpec((SCAN_CHUNKS * GDN_STACK, GDN_STACK), rev)
    wide = pl.BlockSpec((SCAN_CHUNKS * GDN_STACK, GDN_WIDTH), rev)
    one = pl.BlockSpec((1, GDN_DIM), lambda n: (0, 0))
    return pl.pallas_call(
        body, name=name, grid=(ns,),
        in_specs=[row, row, row, one, pl.BlockSpec((SCAN_CHUNKS * GDN_WIDTH, GDN_DIM), rev), row, sq, wide, wide, wide, row]
        + [ANY] * len(after),
        out_specs=[row, one, row, row, sq, row, row, pl.BlockSpec((SCAN_CHUNKS, 8, GDN_WIDTH), lambda n: (ns - 1 - n, 0, 0))],
        out_shape=[_sds((S, GDN_WIDTH), BF16), _sds((1, GDN_DIM)), _sds((S, GDN_WIDTH)), _sds((S, GDN_WIDTH)),
                   _sds((nc * GDN_STACK, GDN_STACK)), _sds((S, GDN_WIDTH)), _sds((S, GDN_WIDTH)), _sds((nc, 8, GDN_WIDTH))],
        scratch_shapes=[pltpu.VMEM((GDN_WIDTH, GDN_DIM), F32)],
        compiler_params=_params(("arbitrary",)),
    )(d_oab, o, proj_z, gnw, sp, u, at, wkb, qdb, keb, gcb, *after)


def _gdn_chunk_bwd(qn, kn, gcb, bb, tmat, uv, wk, du, dwk, dat, dqd, dke, dgl, *, name):
    S = qn.shape[0]

    def body(qn_ref, kn_ref, gcb_ref, bb_ref, t_ref, uv_ref, wk_ref, du_ref, dwk_ref, dat_ref, dqd_ref, dke_ref,
             dgl_ref, dq_ref, dk_ref, dv_ref, dg_ref, dbeta_ref):
        incl, strict, _ = _stack_masks()
        lane = lax.broadcasted_iota(jnp.int32, (CHUNK, 128), 1)
        rowi = lax.broadcasted_iota(jnp.int32, (CHUNK, 1), 0)
        rsum = lambda x: jnp.sum(x, axis=-1, keepdims=True)
        for c in range(CHUNKS_PER_STEP):
            rows = slice(CHUNK * c, CHUNK * (c + 1))
            srows = slice(GDN_STACK * c, GDN_STACK * (c + 1))
            q, k, gs, bs, uv, wk, du, dwk, dqd, dke = [
                _stack(r, rows) for r in (qn_ref, kn_ref, gcb_ref, bb_ref, uv_ref, wk_ref, du_ref, dwk_ref, dqd_ref, dke_ref)]
            dec, bt = _stack_decay(gs, bs, incl)
            kk = _bdot_nt(k, k)
            qk = _bdot_nt(q, k)
            d_rhs = _dot3(t_ref[srows, :], jnp.concatenate([du, dwk], axis=1), ((0,), (0,)))
            sol = jnp.concatenate([uv, wk], axis=1)
            d_l = jnp.where(strict, -_dot3(d_rhs, sol, ((1,), (1,))), 0.0)
            d_a = jnp.where(incl, dat_ref[srows, :], 0.0)
            gam = jnp.exp(gs)
            e = jnp.exp(_last_rows(gs, CHUNK) - gs)
            d_gk = d_rhs[:, GDN_DIM:]
            ml = d_l * dec * bt
            ma = d_a * dec * bt
            _unstack_to(dq_ref, rows, _bdot(ma, k) + dqd * gam)
            _unstack_to(dk_ref, rows, _bdot(ml + ml.T, k) + _bdot_tn(ma, q) + d_gk * gam + dke * (e * bs))
            _unstack_to(dv_ref, rows, d_rhs[:, :GDN_DIM])
            wb = d_l * dec * kk + d_a * dec * qk
            ew = wb * bt
            s_ke = rsum(dke * k * (e * bs))
            dbeta = rsum(wb.T) + rsum(dke * k * e)
            dgc = rsum(ew) - rsum(ew.T) + rsum(dqd * q * gam) + rsum(d_gk * k * gam) - s_ke
            dgc4 = jnp.zeros((CHUNK, 128), F32)
            db4 = jnp.zeros((CHUNK, 128), F32)
            for h in range(GDN_HEADS):
                hr = slice(CHUNK * h, CHUNK * (h + 1))
                tail = jnp.sum(s_ke[hr], axis=0, keepdims=True) + dgl_ref[c, 0:1, GDN_DIM * h:GDN_DIM * h + 1]
                dgc4 = jnp.where(lane == h, dgc[hr] + jnp.where(rowi == CHUNK - 1, tail, 0.0), dgc4)
                db4 = jnp.where(lane == h, dbeta[hr], db4)
            dg_ref[rows, :] = _exact_tri_dot(_chunk_tri(CHUNK, upper=True), dgc4)
            dbeta_ref[rows, :] = db4

    step = CHUNKS_PER_STEP * CHUNK
    row = pl.BlockSpec((step, GDN_WIDTH), lambda n: (n, 0))
    sq = pl.BlockSpec((CHUNKS_PER_STEP * GDN_STACK, GDN_STACK), lambda n: (n, 0))
    col = pl.BlockSpec((step, 128), lambda n: (n, 0))
    return pl.pallas_call(
        body, name=name, grid=(S // step,),
        in_specs=[row] * 4 + [sq, row, row, row, row, sq, row, row,
                              pl.BlockSpec((CHUNKS_PER_STEP, 8, GDN_WIDTH), lambda n: (n, 0, 0))],
        out_specs=[row, row, row, col, col],
        out_shape=[_sds((S, GDN_WIDTH))] * 3 + [_sds((S, 128))] * 2, compiler_params=_params(("parallel",)),
    )(qn, kn, gcb, bb, tmat, uv, wk, du, dwk, dat, dqd, dke, dgl)


def _gdn_prep_bwd(dqn, dkn, dv, dg, dbeta, proj_a, conv_w, a_log, dt_bias, *, name, tm=256):
    S = proj_a.shape[0]
    nblk = S // tm
    W3 = 3 * GDN_WIDTH

    def body(dqn_ref, dkn_ref, dv_ref, dg_ref, dbeta_ref, cur_ref, prev_ref, ba_ref, cw_ref, al_ref, dt_ref,
             dc_ref, dba_ref, sm_ref):
        i = pl.program_id(0)
        prev = jnp.where(i > 0, prev_ref[...], 0.0)
        c = _conv_rows(prev, cur_ref[...], cw_ref[...], GDN_CONV)
        sg = _sigmoid(c)
        a = c * sg
        dsl = _dsilu(c, sg)
        ba = ba_ref[...]
        lane = lax.broadcasted_iota(jnp.int32, (tm, 128), 1)
        lane1 = lax.broadcasted_iota(jnp.int32, (1, 128), 1)
        dba = jnp.zeros((tm, 128), F32)
        sm = jnp.zeros((1, 128), F32)
        for h in range(GDN_HEADS):
            sl = slice(GDN_DIM * h, GDN_DIM * (h + 1))
            ks = slice(GDN_WIDTH + GDN_DIM * h, GDN_WIDTH + GDN_DIM * (h + 1))
            qh, kh = a[:, sl], a[:, ks]
            rq = lax.rsqrt(jnp.sum(qh * qh, axis=-1, keepdims=True) + EPS)
            rk = lax.rsqrt(jnp.sum(kh * kh, axis=-1, keepdims=True) + EPS)
            qhat, khat = qh * rq, kh * rk
            dyq = dqn_ref[:, sl] * (GDN_DIM ** -0.5)
            dyk = dkn_ref[:, sl]
            dq = rq * (dyq - qhat * jnp.sum(dyq * qhat, axis=-1, keepdims=True))
            dk = rk * (dyk - khat * jnp.sum(dyk * khat, axis=-1, keepdims=True))
            dc_ref[:, sl] = dq * dsl[:, sl]
            dc_ref[:, ks] = dk * dsl[:, ks]
            beta = _sigmoid(ba[:, h:h + 1])
            db = dbeta_ref[:, h:h + 1] * beta * (1.0 - beta)
            aneg = -jnp.exp(al_ref[0:1, h:h + 1])
            xa = ba[:, GDN_HEADS + h:GDN_HEADS + h + 1] + dt_ref[0:1, h:h + 1]
            dgh = dg_ref[:, h:h + 1]
            dxa = dgh * aneg * _sigmoid(xa)
            dba = jnp.where(lane == h, db, dba)
            dba = jnp.where(lane == GDN_HEADS + h, dxa, dba)
            d_alog = jnp.sum(dgh * _softplus(xa), axis=0, keepdims=True) * aneg
            sm = jnp.where(lane1 == h, d_alog, sm)
            sm = jnp.where(lane1 == GDN_HEADS + h, jnp.sum(dxa, axis=0, keepdims=True), sm)
        vs = slice(2 * GDN_WIDTH, W3)
        dc_ref[:, vs] = dv_ref[...] * dsl[:, vs]
        dba_ref[...] = dba

        @pl.when(i == 0)
        def _():
            sm_ref[...] = sm

        @pl.when(i > 0)
        def _():
            sm_ref[...] += sm

    prev_spec, _ = _halo_specs(tm, W3, 0, nblk)
    row = pl.BlockSpec((tm, GDN_WIDTH), lambda i: (i, 0))
    col = pl.BlockSpec((tm, 128), lambda i: (i, 0))
    small = lambda a: pl.BlockSpec(a.shape, lambda i: (0, 0))
    return pl.pallas_call(
        body, name=name, grid=(nblk,),
        in_specs=[row, row, row, col, col, pl.BlockSpec((tm, W3), lambda i: (i, 0)), prev_spec,
                  pl.BlockSpec((tm, 128), lambda i: (i, W3 // 128)), small(conv_w), small(a_log), small(dt_bias)],
        out_specs=[pl.BlockSpec((tm, W3), lambda i: (i, 0)), col, pl.BlockSpec((1, 128), lambda i: (0, 0))],
        out_shape=[_sds((S, W3)), _sds((S, 128)), _sds((1, 128))], compiler_params=_params(("arbitrary",)),
    )(dqn, dkn, dv, dg, dbeta, proj_a, proj_a, proj_a, conv_w, a_log, dt_bias)


def _gdn_conv_bwd(dc, dba, proj_a, conv_w, *, name, tm=256):
    S = proj_a.shape[0]
    nblk = S // tm
    W3 = 3 * GDN_WIDTH

    def body(dc_ref, dnext_ref, dba_ref, cur_ref, prev_ref, cw_ref, da_ref, dcw_ref):
        i = pl.program_id(0)
        prev = jnp.where(i > 0, prev_ref[...], 0.0)
        nxt = jnp.where(i < nblk - 1, dnext_ref[...], 0.0)
        dx, dw = _conv_rows_bwd(dc_ref[...], nxt, prev, cur_ref[...], cw_ref[...], GDN_CONV)
        da_ref[:, 0:W3] = dx.astype(BF16)
        da_ref[:, W3:] = dba_ref[...].astype(BF16)

        @pl.when(i == 0)
        def _():
            dcw_ref[...] = dw

        @pl.when(i > 0)
        def _():
            dcw_ref[...] += dw

    prev_spec, next_spec = _halo_specs(tm, W3, 0, nblk)
    wide = pl.BlockSpec((tm, W3), lambda i: (i, 0))
    return pl.pallas_call(
        body, name=name, grid=(nblk,),
        in_specs=[wide, next_spec, pl.BlockSpec((tm, 128), lambda i: (i, 0)), wide, prev_spec,
                  pl.BlockSpec(conv_w.shape, lambda i: (0, 0))],
        out_specs=[pl.BlockSpec((tm, A_COLS), lambda i: (i, 0)), pl.BlockSpec(conv_w.shape, lambda i: (0, 0))],
        out_shape=[_sds((S, A_COLS), BF16), _sds(conv_w.shape)], compiler_params=_params(("arbitrary",)),
    )(dc, dc, dba, proj_a, proj_a, conv_w)


def _band_mask(nk):
    i = lax.broadcasted_iota(jnp.int32, (2 * BAND, nk), 0) & (BAND - 1)
    j = lax.broadcasted_iota(jnp.int32, (2 * BAND, nk), 1)
    if nk == BAND:
        return j <= i
    return (j >= i) & (j <= i + BAND)


def _stack_heads(x, lo):
    return jnp.concatenate([jnp.where(lo, x, 0.0), jnp.where(lo, 0.0, x)], axis=0)


def _stack_cols(x):
    return jnp.concatenate([x[:, 0:1], x[:, DIL_DIM:DIL_DIM + 1]], axis=0)


def _unstack(x, lo):
    return jnp.where(lo, x[0:BAND], x[BAND:2 * BAND])


def _rows(start, size, stride):
    return pl.ds(start, size) if stride == 1 else pl.ds(start, size, stride=stride)


ATTN_LANES = 4


def _attn_blocks(S, visit_many, lanes=ATTN_LANES):
    for d in DILATIONS:
        nb = S // (d * BAND)
        if d == 1:
            half = nb // 2
            visit_many(d, [(0, 0, True), (0, half, False)])

            def pair(n, c):
                visit_many(1, [(0, n, False), (0, n + half, False)])
                return c
            lax.fori_loop(1, half, pair, 0)
        elif nb > 1:
            for r0 in range(0, d, lanes):
                visit_many(d, [(r0 + t, 0, True) for t in range(lanes)])

                def column(n, c, d=d, r0=r0):
                    visit_many(d, [(r0 + t, n, False) for t in range(lanes)])
                    return c
                lax.fori_loop(1, nb, column, 0)
        else:
            def group(g, c, d=d):
                visit_many(d, [(g * lanes + t, 0, True) for t in range(lanes)])
                return c
            lax.fori_loop(0, d // lanes, group, 0)


def _attn_fwd(proj_b, oab, *, name):
    S = proj_b.shape[0]
    scale = DIL_DIM ** -0.5

    def body(q_ref, k_ref, v_ref, oab_in_ref, ob_ref, lse_ref, m_ref, l_ref, acc_ref):
        del oab_in_ref
        lane = lax.broadcasted_iota(jnp.int32, (BAND, 128), 1)
        lo = lane < DIL_DIM
        m_ref[...] = jnp.full_like(m_ref, NEG_BIG)
        l_ref[...] = jnp.zeros_like(l_ref)
        acc_ref[...] = jnp.zeros_like(acc_ref)

        def load(d, r, n, first):
            nk = BAND if first else 2 * BAND
            qrows = _rows(r + n * (BAND * d), BAND, d)
            krows = _rows(r if first else r + (n - 1) * (BAND * d), nk, d)
            return dict(nk=nk, qrows=qrows, q=q_ref[qrows, :] * scale, k=k_ref[krows, :].astype(BF16),
                        v=v_ref[krows, :].astype(BF16), m=m_ref[qrows, :], l=l_ref[qrows, :], acc=acc_ref[qrows, :])

        def compute(b):
            q, k, v = b["q"], b["k"], b["v"]
            s = jnp.where(_band_mask(b["nk"]), _bdot_nt(_stack_heads(q, lo), k), NEG_BIG)
            m_old = _stack_cols(b["m"])
            m_new = jnp.maximum(m_old, jnp.max(s, axis=-1, keepdims=True))
            p = jnp.exp(s - m_new)
            alpha = _unstack(jnp.exp(m_old - m_new), lo)
            l_new = alpha * b["l"] + _unstack(jnp.sum(p, axis=-1, keepdims=True), lo)
            return _unstack(m_new, lo), l_new, alpha * b["acc"] + _unstack(_bdot(p, v), lo)

        def visit_many(d, blocks):
            loaded = [load(d, *blk) for blk in blocks]
            done = [compute(b) for b in loaded]
            for b, (m_new, l_new, acc_new) in zip(loaded, done):
                m_ref[b["qrows"], :] = m_new
                l_ref[b["qrows"], :] = l_new
                acc_ref[b["qrows"], :] = acc_new

        _attn_blocks(S, visit_many)
        ob_ref[...] = (acc_ref[...] / l_ref[...]).astype(BF16)
        lse_ref[...] = m_ref[...] + jnp.log(l_ref[...])

    part = lambda t: pl.BlockSpec((S, 128), lambda p: (0, 3 * p + t))
    return pl.pallas_call(
        body, name=name, grid=(DIL_PAIRS,),
        in_specs=[part(0), part(1), part(2), pl.BlockSpec(memory_space=pl.ANY)],
        out_specs=[pl.BlockSpec((S, 128), lambda p: (0, GDN_WIDTH // 128 + p)), pl.BlockSpec((S, 128), lambda p: (0, p))],
        out_shape=[_sds(oab.shape, BF16), _sds((S, DIL_WIDTH))],
        scratch_shapes=[pltpu.VMEM((S, 128), F32)] * 3, input_output_aliases={3: 0},
        compiler_params=_params(("parallel",)),
    )(proj_b, proj_b, proj_b, oab)


def _attn_bwd(proj_b, oab, d_oab, lse, *, name):
    S = proj_b.shape[0]
    scale = DIL_DIM ** -0.5

    def body(q_ref, k_ref, v_ref, o_ref, do_ref, lse_ref, dqkv_ref, dq_ref, dk_ref, dv_ref, delta_ref):
        lane = lax.broadcasted_iota(jnp.int32, (BAND, 128), 1)
        lo = lane < DIL_DIM
        dq_ref[...] = jnp.zeros_like(dq_ref)
        dk_ref[...] = jnp.zeros_like(dk_ref)
        dv_ref[...] = jnp.zeros_like(dv_ref)
        prod = do_ref[...] * o_ref[...].astype(F32)
        lo_all = lax.broadcasted_iota(jnp.int32, (S, 128), 1) < DIL_DIM
        delta_ref[...] = jnp.where(lo_all, jnp.sum(jnp.where(lo_all, prod, 0.0), axis=-1, keepdims=True),
                                   jnp.sum(jnp.where(lo_all, 0.0, prod), axis=-1, keepdims=True))

        def load(d, r, n, first):
            nk = BAND if first else 2 * BAND
            qrows = _rows(r + n * (BAND * d), BAND, d)
            krows = _rows(r if first else r + (n - 1) * (BAND * d), nk, d)
            return dict(nk=nk, qrows=qrows, krows=krows, q=q_ref[qrows, :] * scale, k=k_ref[krows, :], v=v_ref[krows, :],
                        do=do_ref[qrows, :], delta=delta_ref[qrows, :], lse=lse_ref[qrows, :],
                        dq=dq_ref[qrows, :], dk=dk_ref[krows, :], dv=dv_ref[krows, :])

        def compute(b):
            q, k, v, do = b["q"], b["k"], b["v"], b["do"]
            qs, dos = _stack_heads(q, lo), _stack_heads(do, lo)
            p = jnp.where(_band_mask(b["nk"]), jnp.exp(_bdot_nt(qs, k) - _stack_cols(b["lse"])), 0.0)
            ds = p * (_bdot_nt(dos, v) - _stack_cols(b["delta"]))
            dq = b["dq"] + _unstack(_bdot(ds, k), lo) * scale
            return dq, b["dk"] + _bdot_tn(ds, qs), b["dv"] + _bdot_tn(p, dos)

        def visit_many(d, blocks):
            loaded = [load(d, *blk) for blk in blocks]
            done = [compute(b) for b in loaded]
            for b, (dq, dk, dv) in zip(loaded, done):
                dq_ref[b["qrows"], :] = dq
                dk_ref[b["krows"], :] = dk
                dv_ref[b["krows"], :] = dv

        _attn_blocks(S, visit_many, lanes=2)
        dqkv_ref[:, 0:128] = dq_ref[...].astype(BF16)
        dqkv_ref[:, 128:256] = dk_ref[...].astype(BF16)
        dqkv_ref[:, 256:384] = dv_ref[...].astype(BF16)

    half = lambda p: (0, GDN_WIDTH // 128 + p)
    part = lambda t: pl.BlockSpec((S, 128), lambda p: (0, 3 * p + t))
    return pl.pallas_call(
        body, name=name, grid=(DIL_PAIRS,),
        in_specs=[part(0), part(1), part(2), pl.BlockSpec((S, 128), half), pl.BlockSpec((S, 128), half),
                  pl.BlockSpec((S, 128), lambda p: (0, p))],
        out_specs=pl.BlockSpec((S, 384), lambda p: (0, p)), out_shape=_sds((S, 3 * DIL_WIDTH), BF16),
        scratch_shapes=[pltpu.VMEM((S, 128), F32)] * 4, compiler_params=_params(("parallel",)),
    )(proj_b, proj_b, proj_b, oab, d_oab, lse)


FF_SLAB = 2 * D_FF // N_DEV
FF_PAIRS = N_DEV // 2
ROWS16 = 16


def _taps(w, x, base, n):
    out = _shifted(x, base, n) * w[0:1]
    for t in range(1, FFN_CONV):
        out = out + _shifted(x, base + t, n) * w[t:t + 1]
    return out


def _ffn_fwd(h2, x1, w_up, conv_w, w_down, final_w, tgt, *, name, tm=512):
    S, D = h2.shape
    ni = S // tm
    per = tm // ROWS16

    def body(h_ref, hp_ref, x1_ref, wg_ref, wu_ref, cg_ref, cu_ref, wd_ref, fw_ref, t_ref,
             dx_ref, dxb_ref, dfw_ref, loss_ref, ug_ref, uu_ref, x2_ref):
        i, j = pl.program_id(0), pl.program_id(1)
        hv = jnp.concatenate([hp_ref[...], h_ref[...]], axis=0)
        row = lax.broadcasted_iota(jnp.int32, (tm + ROWS16, 1), 0)
        keep = (i > 0) | (row >= ROWS16)

        def branch(w_ref, c_ref, u_ref):
            u = lax.dot_general(hv, w_ref[...], (((1,), (1,)), ((), ())), preferred_element_type=F32).astype(BF16)
            u_ref[...] = u[ROWS16:]
            return _taps(c_ref[...], jnp.where(keep, u.astype(F32), 0.0), ROWS16 - (FFN_CONV - 1), tm)

        gate = branch(wg_ref, cg_ref, ug_ref)
        up = branch(wu_ref, cu_ref, uu_ref)
        act = (gate * _sigmoid(gate) * up).astype(BF16)
        part = jnp.dot(act, wd_ref[...], preferred_element_type=F32)

        @pl.when(j == 0)
        def _():
            x2_ref[...] = x1_ref[...] + part

        @pl.when((j > 0) & (j < FF_PAIRS - 1))
        def _():
            x2_ref[...] += part

        @pl.when(j == FF_PAIRS - 1)
        def _():
            xv = x2_ref[...] + part
            wv = fw_ref[...]
            r = lax.rsqrt(jnp.mean(xv * xv, axis=-1, keepdims=True) + EPS)
            err = xv * r * wv - t_ref[...]
            lsum = jnp.sum(jnp.sum(err * err, axis=-1, keepdims=True), axis=0, keepdims=True) * (0.5 / D)
            g = err * (1.0 / D)
            xh = xv * r
            gw = g * wv
            dx = r * (gw - xh * jnp.mean(gw * xh, axis=-1, keepdims=True))
            dx_ref[...] = dx
            dxb_ref[...] = dx.astype(BF16)
            dfw = jnp.sum(g * xh, axis=0, keepdims=True)
            lpart = jnp.broadcast_to(lsum, (1, 128))

            @pl.when(i == 0)
            def _():
                dfw_ref[...] = dfw
                loss_ref[...] = lpart

            @pl.when(i > 0)
            def _():
                dfw_ref[...] += dfw
                loss_ref[...] += lpart

    rows = pl.BlockSpec((tm, D), lambda i, j: (i, 0))
    slab = lambda off: pl.BlockSpec((None, FF_SLAB, D), lambda i, j: (j + off, 0, 0))
    cslab = lambda off: pl.BlockSpec((None, FFN_CONV, FF_SLAB), lambda i, j: (j + off, 0, 0))
    uspec = pl.BlockSpec((None, tm, FF_SLAB), lambda i, j: (j, i, 0))
    return pl.pallas_call(
        body, name=name, grid=(ni, FF_PAIRS),
        in_specs=[rows, pl.BlockSpec((ROWS16, D), lambda i, j: (jnp.maximum(i * per - 1, 0), 0)), rows,
                  slab(0), slab(FF_PAIRS), cslab(0), cslab(FF_PAIRS), pl.BlockSpec((FF_SLAB, D), lambda i, j: (j, 0)),
                  pl.BlockSpec((1, D), lambda i, j: (0, 0)), rows],
        out_specs=[rows, rows, pl.BlockSpec((1, D), lambda i, j: (0, 0)), pl.BlockSpec((1, 128), lambda i, j: (0, 0)), uspec, uspec],
        out_shape=[_sds((S, D)), _sds((S, D), BF16), _sds((1, D)), _sds((1, 128)),
                   _sds((FF_PAIRS, S, FF_SLAB), BF16), _sds((FF_PAIRS, S, FF_SLAB), BF16)],
        scratch_shapes=[pltpu.VMEM((tm, D), F32)],
        compiler_params=_params(("arbitrary", "arbitrary")),
    )(h2, h2, x1, w_up, w_up, conv_w, conv_w, w_down, final_w, tgt)


def _ffn_bwd(dx2, h2, ug, uu, conv_w, w_down, *, name, tm=512):
    S, D = h2.shape
    ni = S // tm
    per = tm // ROWS16
    ext = tm + ROWS16

    def body(dx_ref, dxn_ref, h_ref, ug_ref, ugp_ref, ugn_ref, uu_ref, uup_ref, uun_ref, cg_ref, cu_ref, wd_ref,
             du_ref, gd_ref, gup_ref, dcw_ref, acc_d, acc_g, acc_u, acc_cg, acc_cu):
        i = pl.program_id(1)

        @pl.when(i == 0)
        def _():
            acc_d[...] = jnp.zeros_like(acc_d)
            acc_g[...] = jnp.zeros_like(acc_g)
            acc_u[...] = jnp.zeros_like(acc_u)
            acc_cg[...] = jnp.zeros_like(acc_cg)
            acc_cu[...] = jnp.zeros_like(acc_cu)

        dx = dx_ref[...]
        dxe = jnp.concatenate([dx, dxn_ref[...]], axis=0)
        row = lax.broadcasted_iota(jnp.int32, (ext, 1), 0)
        live = (i < ni - 1) | (row < tm)
        d_act = jnp.where(live, lax.dot_general(dxe, wd_ref[...], (((1,), (1,)), ((), ())), preferred_element_type=F32), 0.0)
        rowp = lax.broadcasted_iota(jnp.int32, (ext + ROWS16, 1), 0)
        keep = (i > 0) | (rowp >= ROWS16)

        def pre(cur, prev, nxt):
            return jnp.where(keep, jnp.concatenate([prev[...], cur[...], nxt[...]], axis=0).astype(F32), 0.0)

        uge, uue = pre(ug_ref, ugp_ref, ugn_ref), pre(uu_ref, uup_ref, uun_ref)
        cg, cu = cg_ref[...], cu_ref[...]
        base = ROWS16 - (FFN_CONV - 1)
        gate = _taps(cg, uge, base, ext)
        up = _taps(cu, uue, base, ext)
        sg = _sigmoid(gate)
        silu = gate * sg
        dgc = d_act * up * _dsilu(gate, sg)
        duc = d_act * silu

        def conv_t(w, dc):
            out = _shifted(dc, FFN_CONV - 1, tm) * w[0:1]
            for t in range(1, FFN_CONV):
                out = out + _shifted(dc, FFN_CONV - 1 - t, tm) * w[t:t + 1]
            return out.astype(BF16)

        du_g, du_u = conv_t(cg, dgc), conv_t(cu, duc)
        du_ref[0] = du_g
        du_ref[1] = du_u
        dcw = lambda dc, xe: jnp.concatenate(
            [jnp.sum(dc[0:tm] * _shifted(xe, base + t, tm), axis=0, keepdims=True) for t in range(FFN_CONV)], axis=0)
        acc_cg[0:FFN_CONV, :] += dcw(dgc, uge)
        acc_cu[0:FFN_CONV, :] += dcw(duc, uue)
        tn = (((0,), (0,)), ((), ()))
        act = (silu[0:tm] * up[0:tm]).astype(BF16)
        acc_d[...] += lax.dot_general(act, dx, tn, preferred_element_type=F32)
        hv = h_ref[...]
        acc_g[...] += lax.dot_general(du_g, hv, tn, preferred_element_type=F32)
        acc_u[...] += lax.dot_general(du_u, hv, tn, preferred_element_type=F32)

        @pl.when(i == ni - 1)
        def _():
            gd_ref[...] = acc_d[...].astype(BF16)
            gup_ref[0] = acc_g[...].astype(BF16)
            gup_ref[1] = acc_u[...].astype(BF16)
            dcw_ref[0] = acc_cg[0:FFN_CONV, :]
            dcw_ref[1] = acc_cu[0:FFN_CONV, :]

    last16 = S // ROWS16 - 1
    rows = pl.BlockSpec((tm, D), lambda j, i: (i, 0))
    rows_next = pl.BlockSpec((ROWS16, D), lambda j, i: (jnp.minimum((i + 1) * per, last16), 0))
    u_cur = pl.BlockSpec((None, tm, FF_SLAB), lambda j, i: (j, i, 0))
    u_prev = pl.BlockSpec((None, ROWS16, FF_SLAB), lambda j, i: (j, jnp.maximum(i * per - 1, 0), 0))
    u_next = pl.BlockSpec((None, ROWS16, FF_SLAB), lambda j, i: (j, jnp.minimum((i + 1) * per, last16), 0))
    cslab = lambda off: pl.BlockSpec((None, FFN_CONV, FF_SLAB), lambda j, i: (j + off, 0, 0))
    return pl.pallas_call(
        body, name=name, grid=(FF_PAIRS, ni),
        in_specs=[rows, rows_next, rows, u_cur, u_prev, u_next, u_cur, u_prev, u_next, cslab(0), cslab(FF_PAIRS),
                  pl.BlockSpec((FF_SLAB, D), lambda j, i: (j, 0))],
        out_specs=[pl.BlockSpec((None, 2, tm, FF_SLAB), lambda j, i: (j, 0, i, 0)), pl.BlockSpec((FF_SLAB, D), lambda j, i: (j, 0)),
                   pl.BlockSpec((None, 2, FF_SLAB, D), lambda j, i: (j, 0, 0, 0)),
                   pl.BlockSpec((None, 2, FFN_CONV, FF_SLAB), lambda j, i: (j, 0, 0, 0))],
        out_shape=[_sds((FF_PAIRS, 2, S, FF_SLAB), BF16), _sds((D_FF, D), BF16), _sds((FF_PAIRS, 2, FF_SLAB, D), BF16),
                   _sds((FF_PAIRS, 2, FFN_CONV, FF_SLAB))],
        scratch_shapes=[pltpu.VMEM((FF_SLAB, D), F32), pltpu.VMEM((FF_SLAB, D), F32), pltpu.VMEM((FF_SLAB, D), F32),
                        pltpu.VMEM((8, FF_SLAB), F32), pltpu.VMEM((8, FF_SLAB), F32)],
        compiler_params=_params(("parallel", "arbitrary")),
    )(dx2, dx2, h2, ug, ug, ug, uu, uu, uu, conv_w, conv_w, w_down)


def _pair_slot(p):
    return 2 * (p & (FF_PAIRS - 1)) + (p >> 2)


def _mm_slabs(a, w, *, name, res=None, norm_bwd=None, after=(), tm=1024, tn=1024):
    nk, S, _ = a.shape
    D = w.shape[2]
    has_res = res is not None
    has_norm = norm_bwd is not None
    assert not has_norm or tn == D

    def body(*refs):
        a_ref, w_ref = refs[:2]
        r_ref = refs[2] if has_res else None
        if has_norm:
            x_ref, nw_ref, skip_ref = refs[2 + has_res:5 + has_res]
            o_ref, dw_ref, acc_ref = refs[-3:]
        else:
            o_ref, acc_ref = refs[-2:]
        i, k = pl.program_id(0), pl.program_id(2)
        part = jnp.dot(a_ref[...], w_ref[...], preferred_element_type=F32)

        @pl.when(k == 0)
        def _():
            acc_ref[...] = part

        @pl.when(k > 0)
        def _():
            acc_ref[...] += part

        @pl.when(k == nk - 1)
        def _():
            r = acc_ref[...] + r_ref[...] if has_res else acc_ref[...]
            if has_norm:
                dx, dw = _rms_bwd_rows(r, x_ref[...], nw_ref[...])
                o_ref[...] = skip_ref[...] + dx

                @pl.when(i == 0)
                def _():
                    dw_ref[...] = dw

                @pl.when(i > 0)
                def _():
                    dw_ref[...] += dw
            else:
                o_ref[...] = r

    o_spec = pl.BlockSpec((tm, tn), lambda i, j, k: (i, j))
    one = pl.BlockSpec((1, tn), lambda i, j, k: (0, 0))
    return pl.pallas_call(
        body, name=name, grid=(S // tm, D // tn, nk),
        in_specs=[pl.BlockSpec((None, tm, FF_SLAB), lambda i, j, k: (k, i, 0)),
                  pl.BlockSpec((None, FF_SLAB, tn), lambda i, j, k: (FF_PAIRS * (k & 1) + (k >> 1), 0, j))] + [o_spec] * has_res
        + ([o_spec, one, o_spec] if has_norm else []) + [ANY] * len(after),
        out_specs=[o_spec, one] if has_norm else o_spec, out_shape=[_sds((S, D)), _sds((1, D))] if has_norm else _sds((S, D)),
        scratch_shapes=[pltpu.VMEM((tm, tn), F32)],
        compiler_params=_params(("arbitrary" if has_norm else "parallel", "parallel", "arbitrary")),
    )(*((a, w) + ((res,) if has_res else ()) + (tuple(norm_bwd) if has_norm else ()) + tuple(after)))


def _local_step(x, tgt, norm1_w, w_land, conv_a, a_log, dt_bias, gnw, norm2_w, final_w, late_weights, emit, start_after=()):
    wgrad = functools.partial(_mm, ta=True, out_dtype=BF16)
    h1, proj_a, proj_z, proj_b = _in_proj(x, norm1_w, w_land, after=start_after, name="in_proj")
    qn, kn, v, gcb, bb = _gdn_prep_fwd(proj_a, conv_a, a_log, dt_bias, name="gdn_prep_fwd")
    uv, wk, at, tmat, wkb, qdb, keb = _gdn_chunk_fwd(qn, kn, v, gcb, bb, name="gdn_chunk_fwd")
    o, u, sp, oab = _gdn_scan_fwd(uv, at, wkb, qdb, keb, gcb, proj_z, gnw, name="gdn_scan_fwd")
    oab, lse = _attn_fwd(proj_b, oab, name="attn_fwd")
    w_out, w_up, conv_f, w_down = late_weights(oab)
    x1, h2 = _out_proj_norm(oab, w_out, x, norm2_w, name="out_proj")
    dx2, dx2_b, d_final, loss, ug, uu = _ffn_fwd(h2, x1, w_up, conv_f, w_down, final_w, tgt, name="ffn_fwd")
    du, g_down, g_up, dcw = _ffn_bwd(dx2_b, h2, ug, uu, conv_f, w_down, name="ffn_bwd")
    token = emit("ffn", w_down=g_down, w_up=g_up.reshape(N_DEV, FF_SLAB, -1), conv_f=dcw.reshape(N_DEV, FFN_CONV, -1))
    dx1, d_norm2 = _mm_slabs(du.reshape(N_DEV, -1, FF_SLAB), w_up, norm_bwd=(x1, norm2_w, dx2), after=token, name="ffn_up_dx")
    d_oab = _mm(dx1, w_out, tb=True, name="out_proj_dx", tn=D_MODEL, tk=1024)
    token = emit("out", w_out=wgrad(oab, dx1, name="out_proj_dw", tm=D_MODEL, tn=D_MODEL))
    dz, d_gnw, du, dwk, dat, dqd, dke, dgl = _gdn_scan_bwd(d_oab, o, proj_z, gnw, sp, u, at, wkb, qdb, keb, gcb, after=token, name="gdn_scan_bwd")
    dqn, dkn, dv, dg, dbeta = _gdn_chunk_bwd(qn, kn, gcb, bb, tmat, uv, wk, du, dwk, dat, dqd, dke, dgl, name="gdn_chunk_bwd")
    dc, dba, d_small = _gdn_prep_bwd(dqn, dkn, dv, dg, dbeta, proj_a, conv_a, a_log, dt_bias, name="gdn_prep_bwd")
    d_pa, d_conv_a = _gdn_conv_bwd(dc, dba, proj_a, conv_a, name="gdn_conv_bwd")
    d_pb = _attn_bwd(proj_b, oab, d_oab, lse, name="attn_bwd")
    g_a = wgrad(d_pa, h1, name="proj_a_dw", tm=A_COLS, tn=D_MODEL)
    g_z = wgrad(dz, h1, name="proj_z_dw", tn=D_MODEL)
    g_b = wgrad(d_pb, h1, name="proj_b_dw", tm=768, tn=D_MODEL)
    token = emit("in", w_a=g_a, w_z=g_z, w_b=g_b, conv_a=d_conv_a)
    grad_x, d_norm1 = _in_proj_dx((d_pa, dz, d_pb), w_land, x, norm1_w, dx1, after=token, name="in_proj_dx")
    small = dict(norm1=d_norm1, small=d_small, gnw=d_gnw, norm2=d_norm2, final=d_final)
    return loss, grad_x, small


_O1 = 3 * GDN_WIDTH
_O2 = _O1 + GDN_WIDTH
_O3 = _O2 + 2 * GDN_HEADS


_W_IN_ROWS = (A_COLS, GDN_WIDTH, 3 * DIL_WIDTH)
_IN_ROWS = (_O3 + 3 * DIL_WIDTH) // N_DEV
_ROW_TILE = 16
_IN_STEP = _IN_ROWS - _IN_ROWS % _ROW_TILE
_GAP = 8
_LAND_ROWS = 464
assert _O3 % _ROW_TILE == _ROW_TILE - _GAP and N_DEV - 1 + _GAP + _IN_ROWS <= _LAND_ROWS and _LAND_ROWS % _ROW_TILE == 0


def _padded_row(r):
    return r + (_GAP if r >= _O3 else 0)


def _shifted_slab(w_rows, j, *, name):
    q = w_rows.shape[0] // _IN_ROWS

    def body(j_ref, w_ref, o_ref, pad_ref):
        pad_ref[...] = jnp.zeros_like(pad_ref)
        for dev in range(N_DEV):
            @pl.when(j_ref[0] == dev)
            def _(dev=dev):
                p = lax.broadcasted_iota(jnp.int32, (_LAND_ROWS, 1), 0) + _IN_STEP * dev
                for s in range(q):
                    pad_ref[0:_IN_ROWS, :] = w_ref[pl.ds(s, _IN_ROWS, stride=q), :]
                    rows = pad_ref[...]
                    a = pltpu.roll(rows, dev, 0) if dev else rows
                    b = pltpu.roll(rows, dev + _GAP, 0)
                    o_ref[:, 128 * s:128 * (s + 1)] = jnp.where(p < _O3, a, jnp.where(p >= _O3 + _GAP, b, 0.0)).astype(BF16)

    vm = pl.BlockSpec(memory_space=pltpu.VMEM)
    return pl.pallas_call(
        body, name=name, in_specs=[pl.BlockSpec(memory_space=pltpu.SMEM), vm], out_specs=vm,
        out_shape=_sds((_LAND_ROWS, 128 * q), BF16), scratch_shapes=[pltpu.VMEM((_LAND_ROWS, 128), F32)],
    )(j, w_rows)


def _w_in_plan():
    def dest(p):
        if p < _O1:
            return 0, p
        if p < _O2:
            return 1, p - _O1
        if p < _O2 + _ROW_TILE:
            return 0, _O1
        q = p - _O3 - _GAP
        t, pair = divmod(q // 128, DIL_PAIRS)
        return 2, (3 * pair + t) * 128 + q % 128

    spans = [(_padded_row(_IN_ROWS * j), _padded_row(_IN_ROWS * (j + 1) - 1) + 1) for j in range(N_DEV)]
    runs, seams = [], []
    for p in range(0, spans[-1][1], _ROW_TILE):
        owners = [j for j, (lo, hi) in enumerate(spans) if lo < p + _ROW_TILE and hi > p]
        w, r = dest(p)
        if len(owners) == 2:
            seams.append((w, r, owners[0], p - _IN_STEP * owners[0], owners[1], p - _IN_STEP * owners[1]))
            continue
        (j,) = owners
        last = runs[-1] if runs else None
        if last and last[0] == j and last[2] == w and last[3] + last[4] == r and last[1] + last[4] == p - _IN_STEP * j:
            runs[-1] = last[:4] + (last[4] + _ROW_TILE,)
        else:
            runs.append((j, p - _IN_STEP * j, w, r, _ROW_TILE))
    return runs, seams


def _w_in_scratch(d):
    return [pltpu.VMEM((n, d), BF16) for n in _W_IN_ROWS] + [pltpu.VMEM((N_DEV - 1, _ROW_TILE, d), BF16),
                                                              pltpu.SemaphoreType.DMA(())]


def _fetch_w_in(land_ref, wa_ref, wz_ref, wb_ref, seam_ref, sem):
    w_refs = (wa_ref, wz_ref, wb_ref)
    runs, seams = _w_in_plan()
    copies = [pltpu.make_async_copy(land_ref.at[j, pl.ds(s, n)], w_refs[w].at[pl.ds(r, n)], sem) for j, s, w, r, n in runs]
    for k, (w, r, j0, s0, j1, s1) in enumerate(seams):
        copies.append(pltpu.make_async_copy(land_ref.at[j0, pl.ds(s0, _ROW_TILE)], w_refs[w].at[pl.ds(r, _ROW_TILE)], sem))
        copies.append(pltpu.make_async_copy(land_ref.at[j1, pl.ds(s1, _ROW_TILE)], seam_ref.at[k], sem))
    for cp in copies:
        cp.start()
    tail = _O1 + _ROW_TILE
    wa_ref[tail:, :] = jnp.zeros((A_COLS - tail, wa_ref.shape[1]), BF16)
    for cp in copies:
        cp.wait()
    for k, (w, r, *_) in enumerate(seams):
        both = w_refs[w][r:r + _ROW_TILE, :].astype(F32) + seam_ref[k].astype(F32)
        w_refs[w][r:r + _ROW_TILE, :] = both.astype(BF16)


MESH = pl.DeviceIdType.MESH
ANY = pl.BlockSpec(memory_space=pl.ANY)


def _position():
    return lax.axis_index("x"), lax.axis_index("y"), lax.axis_index("c")


def _slot(p):
    return 4 * p[0] + 2 * p[1] + p[2]


def _all_gather(blocks, *, name, casts=()):
    n, nc = len(blocks), len(casts)

    def body(*refs):
        ins, outs = refs[:n], refs[n + nc:2 * n + nc]
        c_ins, c_outs = refs[n:n + nc], refs[2 * n + nc:2 * n + 2 * nc]
        send_sems, recv_sems, local_sems = refs[2 * n + 2 * nc:2 * n + 2 * nc + 3]
        c_f32, c_bf, c_sems = (refs[-3 * nc:-2 * nc], refs[-2 * nc:-nc], refs[-nc:]) if nc else ((), (), ())
        x, y, c = _position()
        me, sibling = (x, y, c), (x, y, 1 - c)
        chips = [(1 - x, y), (x, 1 - y), (1 - x, 1 - y)]

        def copy(a, k, block, to, src=None):
            dst = outs[a].at[_slot(block)]
            return pltpu.make_async_remote_copy(
                src_ref=dst if src is None else src, dst_ref=dst, send_sem=send_sems.at[a, k], recv_sem=recv_sems.at[a, k],
                device_id=to, device_id_type=MESH)

        mine = [pltpu.make_async_copy(ins[a], outs[a].at[_slot(me)], local_sems.at[a]) for a in range(n)]
        for cp in mine:
            cp.start()
        first = []
        for a in range(n):
            first.append(copy(a, 0, me, sibling, src=ins[a]))
            first += [copy(a, 1 + j, me, (*chip, c), src=ins[a]) for j, chip in enumerate(chips)]
        for cp in first:
            cp.start()
        loads = [pltpu.make_async_copy(c_ins[i], c_f32[i], c_sems[i]) for i in range(nc)]
        for cp in loads:
            cp.start()
        stores = []
        for i in range(nc):
            loads[i].wait()
            c_bf[i][...] = c_f32[i][...].astype(BF16)
            stores.append(pltpu.make_async_copy(c_bf[i], c_outs[i], c_sems[i]))
            stores[-1].start()
        passed = []
        for j, chip in enumerate(chips):
            for a in range(n):
                copy(a, 1 + j, (*chip, c), me).wait_recv()
                fwd = copy(a, 4 + j, (*chip, c), sibling)
                fwd.start()
                passed.append(fwd)
        for a in range(n):
            copy(a, 0, sibling, me).wait_recv()
            for j, chip in enumerate(chips):
                copy(a, 4 + j, (*chip, 1 - c), me).wait_recv()
        for cp in first + passed:
            cp.wait_send()
        for cp in mine + stores:
            cp.wait()

    return pl.pallas_call(
        body, name=name, in_specs=[ANY] * (n + nc), out_specs=[ANY] * (n + nc),
        out_shape=[_sds((N_DEV,) + b.shape, b.dtype) for b in blocks] + [_sds(a.shape, BF16) for a in casts],
        scratch_shapes=[pltpu.SemaphoreType.DMA((n, 7)), pltpu.SemaphoreType.DMA((n, 7)), pltpu.SemaphoreType.DMA((n,))]
        + [pltpu.VMEM(a.shape, F32) for a in casts] + [pltpu.VMEM(a.shape, BF16) for a in casts]
        + [pltpu.SemaphoreType.DMA(()) for _ in casts],
        compiler_params=_params(),
    )(*blocks, *casts)


def _gather_direct(block, *, name, after=()):
    def body(in_ref, *rest):
        out_ref, send_sems, recv_sems, local_sem = rest[len(after):]
        x, y, c = _position()
        me = _slot((x, y, c))
        mine = pltpu.make_async_copy(in_ref, out_ref.at[me], local_sem)
        mine.start()
        copies = [pltpu.make_async_remote_copy(
            src_ref=in_ref, dst_ref=out_ref.at[me], send_sem=send_sems.at[k - 1], recv_sem=recv_sems.at[k - 1],
            device_id=_peer_of(k, x, y, c), device_id_type=MESH) for k in range(1, N_DEV)]
        for cp in copies:
            cp.start()
        for cp in copies:
            cp.wait()
        mine.wait()

    return pl.pallas_call(
        body, name=name, in_specs=[pl.BlockSpec(memory_space=pltpu.VMEM)] + [ANY] * len(after),
        out_specs=pl.BlockSpec(memory_space=pltpu.VMEM),
        out_shape=_sds((N_DEV,) + block.shape, block.dtype),
        scratch_shapes=[pltpu.SemaphoreType.DMA((N_DEV - 1,)), pltpu.SemaphoreType.DMA((N_DEV - 1,)), pltpu.SemaphoreType.DMA],
    )(block, *after)


HBM = pl.BlockSpec(memory_space=pltpu.HBM)
SEM = pl.BlockSpec(memory_space=pltpu.SEMAPHORE)
EFFECT = pltpu.SideEffectType.DATAFLOW_SIDE_EFFECTING


def _peer_of(k, x, y, c):
    return (1 - x if k & 4 else x, 1 - y if k & 2 else y, 1 - c if k & 1 else c)


def _flight(a, k):
    return a * (N_DEV - 1) + k - 1


def _exchange_start(arrays, *, name, broadcast=False, paired=()):
    n = len(arrays)

    def body(*refs):
        ins, lands = refs[:n], refs[n:2 * n]
        send_sems, recv_sems = refs[2 * n:2 * n + 2]
        token = refs[-1]
        x, y, c = _position()
        me = _slot((x, y, c))
        for k in range(1, N_DEV):
            peer = _peer_of(k, x, y, c)
            for a in range(n):
                at = _pair_slot(_slot(peer)) if a in paired else _slot(peer)
                pltpu.make_async_remote_copy(
                    src_ref=ins[a] if broadcast else ins[a].at[at], dst_ref=lands[a].at[me],
                    send_sem=send_sems.at[_flight(a, k)], recv_sem=recv_sems.at[_flight(a, k)],
                    device_id=peer, device_id_type=MESH).start()
        token[...] = jnp.zeros_like(token)

    land_shapes = [((N_DEV,) + s.shape) if broadcast else s.shape for s in arrays]
    lands = [pltpu.with_memory_space_constraint(lax.empty(shp, s.dtype), pltpu.HBM) for shp, s in zip(land_shapes, arrays)]
    srcs = [pltpu.with_memory_space_constraint(s, pltpu.HBM) for s in arrays]
    outs = pl.pallas_call(
        body, name=name, in_specs=[HBM] * (2 * n),
        out_specs=[SEM, SEM] + [HBM] * (2 * n) + [pl.BlockSpec(memory_space=pltpu.VMEM)],
        out_shape=[pltpu.SemaphoreType.DMA((n * (N_DEV - 1),)), pltpu.SemaphoreType.DMA((n * (N_DEV - 1),))]
        + [pltpu.HBM(s.shape, s.dtype) for s in arrays] + [pltpu.HBM(shp, s.dtype) for shp, s in zip(land_shapes, arrays)]
        + [_sds((8, 128))],
        input_output_aliases={i: 2 + i for i in range(2 * n)},
        compiler_params=pltpu.CompilerParams(has_side_effects=EFFECT),
    )(*srcs, *lands)
    return outs[0], outs[1], outs[2:2 + n], outs[2 + n:2 + 2 * n], outs[-1]


def _exchange_wait(send_sems, recv_sems, srcs, lands, after, *, name, broadcast=False):
    n = len(srcs)

    def body(*refs):
        ins, lnd = refs[:n], refs[n:2 * n]
        send_ref, recv_ref = refs[2 * n:2 * n + 2]
        x, y, c = _position()
        for k in range(1, N_DEV):
            for a in range(n):
                cp = pltpu.make_async_remote_copy(
                    src_ref=ins[a] if broadcast else ins[a].at[0], dst_ref=lnd[a].at[0], send_sem=send_ref.at[_flight(a, k)],
                    recv_sem=recv_ref.at[_flight(a, k)], device_id=_peer_of(k, x, y, c), device_id_type=MESH)
                cp.wait_send()
                cp.wait_recv()

    outs = pl.pallas_call(
        body, name=name, in_specs=[HBM] * (2 * n) + [SEM, SEM, ANY], out_specs=[HBM] * (2 * n),
        out_shape=[pltpu.HBM(s.shape, s.dtype) for s in srcs] + [pltpu.HBM(s.shape, s.dtype) for s in lands],
        input_output_aliases={i: i for i in range(2 * n)},
        compiler_params=pltpu.CompilerParams(has_side_effects=EFFECT),
    )(*srcs, *lands, send_sems, recv_sems, after)
    return outs[:n], outs[n:]


_G_LAND_ROWS = 528


def _g_in_pieces(dev):
    runs, seams = _w_in_plan()
    pieces = [(w, r, n, s) for j, s, w, r, n in runs if j == dev]
    pieces += [(w, r, _ROW_TILE, s0) for w, r, j0, s0, j1, s1 in seams if j0 == dev]
    pieces += [(w, r, _ROW_TILE, s1) for w, r, j0, s0, j1, s1 in seams if j1 == dev]
    merged = []
    for w, r, n, s in sorted(pieces, key=lambda p: p[3]):
        if merged and merged[-1][0] == w and merged[-1][1] + merged[-1][2] == r and merged[-1][3] + merged[-1][2] == s:
            merged[-1] = (w, merged[-1][1], merged[-1][2] + n, merged[-1][3])
        else:
            merged.append((w, r, n, s))
    return merged


def _coords(dev):
    return tuple(jnp.int32(v) for v in (dev >> 2, (dev >> 1) & 1, dev & 1))


def _exchange_start_in(g_ws, conv_slabs, *, name):
    srcs = list(g_ws) + [conv_slabs]
    n = len(srcs)

    def body(*refs):
        g_refs, cv_ref, land, land_cv = refs[:n - 1], refs[n - 1], refs[n], refs[n + 1]
        send_sems, recv_sems = refs[n + 2:n + 4]
        token = refs[-1]
        me = _slot(_position())
        pieces = [_g_in_pieces(dev) for dev in range(N_DEV)]
        for i in range(max(len(p) for p in pieces)):
            for dev in range(N_DEV):
                if i < len(pieces[dev]):
                    @pl.when(me != dev)
                    def _(dev=dev, i=i):
                        w, r, rows, s = pieces[dev][i]
                        k = me ^ dev
                        pltpu.make_async_remote_copy(
                            src_ref=g_refs[w].at[pl.ds(r, rows)], dst_ref=land.at[me, pl.ds(s, rows)],
                            send_sem=send_sems.at[_flight(0, k)], recv_sem=recv_sems.at[_flight(0, k)],
                            device_id=_coords(dev), device_id_type=MESH).start()
        for dev in range(N_DEV):
            @pl.when(me != dev)
            def _(dev=dev):
                k = me ^ dev
                pltpu.make_async_remote_copy(
                    src_ref=cv_ref.at[dev], dst_ref=land_cv.at[me], send_sem=send_sems.at[_flight(1, k)],
                    recv_sem=recv_sems.at[_flight(1, k)], device_id=_coords(dev), device_id_type=MESH).start()
        token[...] = jnp.zeros_like(token)

    lands = [lax.empty((N_DEV, _G_LAND_ROWS, g_ws[0].shape[1]), g_ws[0].dtype), lax.empty(conv_slabs.shape, conv_slabs.dtype)]
    ops = [pltpu.with_memory_space_constraint(a, pltpu.HBM) for a in srcs + lands]
    outs = pl.pallas_call(
        body, name=name, in_specs=[HBM] * len(ops),
        out_specs=[SEM, SEM] + [HBM] * len(ops) + [pl.BlockSpec(memory_space=pltpu.VMEM)],
        out_shape=[pltpu.SemaphoreType.DMA((2 * (N_DEV - 1),)), pltpu.SemaphoreType.DMA((2 * (N_DEV - 1),))]
        + [pltpu.HBM(a.shape, a.dtype) for a in ops] + [_sds((8, 128))],
        input_output_aliases={i: 2 + i for i in range(len(ops))},
        compiler_params=pltpu.CompilerParams(has_side_effects=EFFECT),
    )(*ops)
    return outs[0], outs[1], outs[2:2 + n], outs[2 + n:4 + n], outs[-1]


def _own_pieces(g_ws, land, after, *, name):
    n = len(g_ws)

    def body(*refs):
        g_refs, land_ref = refs[:n], refs[n]
        token, slab, sem = refs[-3:]
        me = _slot(_position())
        for dev in range(N_DEV):
            @pl.when(me == dev)
            def _(dev=dev):
                own = [pltpu.make_async_copy(g_refs[w].at[pl.ds(r, rows)], slab.at[pl.ds(s, rows)], sem)
                       for w, r, rows, s in _g_in_pieces(dev)]
                for cp in own:
                    cp.start()
                for cp in own:
                    cp.wait()
                out = pltpu.make_async_copy(slab, land_ref.at[dev, pl.ds(0, _LAND_ROWS)], sem)
                out.start()
                out.wait()
        token[...] = jnp.zeros_like(token)

    return pl.pallas_call(
        body, name=name, in_specs=[HBM] * (n + 1) + [ANY], out_specs=[HBM, pl.BlockSpec(memory_space=pltpu.VMEM)],
        out_shape=[pltpu.HBM(land.shape, land.dtype), _sds((8, 128))], input_output_aliases={n: 0},
        scratch_shapes=[pltpu.VMEM((_LAND_ROWS, land.shape[2]), land.dtype), pltpu.SemaphoreType.DMA(())],
    )(*g_ws, land, after)


def _exchange_wait_in(send_sems, recv_sems, srcs, lands, after, *, name):
    n = len(srcs)

    def body(*refs):
        g_refs, cv_ref, land, land_cv = refs[:n - 1], refs[n - 1], refs[n], refs[n + 1]
        send_ref, recv_ref = refs[n + 2:n + 4]
        me = _slot(_position())

        def copies(dev, k):
            cps = [pltpu.make_async_remote_copy(
                src_ref=g_refs[w].at[pl.ds(r, rows)], dst_ref=land.at[0, pl.ds(s, rows)], send_sem=send_ref.at[_flight(0, k)],
                recv_sem=recv_ref.at[_flight(0, k)], device_id=_coords(dev), device_id_type=MESH)
                for w, r, rows, s in _g_in_pieces(dev)]
            return cps + [pltpu.make_async_remote_copy(
                src_ref=cv_ref.at[0], dst_ref=land_cv.at[0], send_sem=send_ref.at[_flight(1, k)],
                recv_sem=recv_ref.at[_flight(1, k)], device_id=_coords(dev), device_id_type=MESH)]

        for dev in range(N_DEV):
            @pl.when(me != dev)
            def _(dev=dev):
                for cp in copies(dev, me ^ dev):
                    cp.wait_send()

            @pl.when(me == dev)
            def _(dev=dev):
                for k in range(1, N_DEV):
                    for cp in copies(dev, k):
                        cp.wait_recv()

    ops = list(srcs) + list(lands)
    outs = pl.pallas_call(
        body, name=name, in_specs=[HBM] * len(ops) + [SEM, SEM, ANY], out_specs=[HBM] * len(ops),
        out_shape=[pltpu.HBM(a.shape, a.dtype) for a in ops],
        input_output_aliases={i: i for i in range(len(ops))},
        compiler_params=pltpu.CompilerParams(has_side_effects=EFFECT),
    )(*ops, send_sems, recv_sems, after)
    return outs[:n], outs[n:]


def _with_own(landed, srcs, me):
    return [lax.dynamic_update_index_in_dim(l, o, me, 0) for l, o in zip(landed, srcs)]


def _adam_update(g, w, m, v):
    c1 = 1.0 - ADAM_B1 ** ADAM_STEP
    c2 = 1.0 - ADAM_B2 ** ADAM_STEP
    nm = ADAM_B1 * m + (1.0 - ADAM_B1) * g
    nv = ADAM_B2 * v + (1.0 - ADAM_B2) * (g * g)
    return -ADAM_LR * ((nm / c1) / (jnp.sqrt(nv / c2) + ADAM_EPS) + ADAM_WD * w), nm, nv


def _adamw(landed, sent, me, w, m, v, *, name, tr=None, tc=None):
    R, C = w.shape
    tr = R if tr is None else tr
    tc = C if tc is None else tc
    assert R % tr == 0 and C % tc == 0

    def body(me_ref, own_ref, p_ref, w_ref, m_ref, v_ref, g_ref, d_ref, nm_ref, nv_ref, token_ref):
        token_ref[...] = jnp.zeros_like(token_ref)
        g = own_ref[...].astype(F32)
        for s in range(N_DEV):
            g = g + jnp.where(me_ref[1] == s, 0.0, p_ref[s].astype(F32))
        delta, nm, nv = _adam_update(g, w_ref[...], m_ref[...], v_ref[...])
        g_ref[...] = g
        nm_ref[...] = nm
        nv_ref[...] = nv
        d_ref[...] = delta

    blk = pl.BlockSpec((tr, tc), lambda i, j, me_ref: (i, j))
    return pl.pallas_call(
        body, name=name,
        grid_spec=pltpu.PrefetchScalarGridSpec(
            num_scalar_prefetch=1, grid=(R // tr, C // tc),
            in_specs=[pl.BlockSpec((None, tr, tc), lambda i, j, me_ref: (me_ref[0], i, j)),
                      pl.BlockSpec((N_DEV, tr, tc), lambda i, j, me_ref: (0, i, j)), blk, blk, blk],
            out_specs=[blk] * 4 + [pl.BlockSpec((8, 128), lambda i, j, me_ref: (0, 0))]),
        out_shape=[_sds((R, C))] * 4 + [_sds((8, 128))],
        compiler_params=_params(("arbitrary", "arbitrary")),
    )(me, sent, landed, w, m, v)


def _adamw_rowwise(landed, me, w, m, v, *, name, tr=128):
    C = landed.shape[2]
    R, q, extra = _IN_ROWS, C // 128, _ROW_TILE
    assert tr % extra == 0 and (pl.cdiv(R, tr) * tr + extra) <= landed.shape[1] and N_DEV - 1 + _GAP < extra

    def body(me_ref, a_ref, b_ref, w_ref, m_ref, v_ref, g_ref, d_ref, nm_ref, nv_ref, g_scr):
        i = pl.program_id(0)
        total = lambda ref: functools.reduce(lambda x, y: x + y, [ref[s].astype(F32) for s in range(N_DEV)])
        slab = jnp.concatenate([total(a_ref), total(b_ref)], axis=0)
        for dev in range(N_DEV):
            @pl.when(me_ref[0] == dev)
            def _(dev=dev):
                lo, hi = slab[dev:dev + tr], slab[dev + _GAP:dev + _GAP + tr]
                if _IN_ROWS * (dev + 1) <= _O3:
                    g_scr[...] = lo
                elif _IN_ROWS * dev >= _O3:
                    g_scr[...] = hi
                else:
                    r = _IN_ROWS * dev + tr * i + lax.broadcasted_iota(jnp.int32, (tr, 1), 0)
                    g_scr[...] = jnp.where(r < _O3, lo, hi)
        g = g_scr[...]
        for s in range(q):
            rows = pl.ds(s, tr, stride=q)
            gs = g[:, 128 * s:128 * (s + 1)]
            delta, nm, nv = _adam_update(gs, w_ref[rows, :], m_ref[rows, :], v_ref[rows, :])
            g_ref[rows, :] = gs
            nm_ref[rows, :] = nm
            nv_ref[rows, :] = nv
            d_ref[rows, :] = delta

    blk = pl.BlockSpec((tr * q, 128), lambda i, me_ref: (i, 0))
    return pl.pallas_call(
        body, name=name,
        grid_spec=pltpu.PrefetchScalarGridSpec(
            num_scalar_prefetch=1, grid=(pl.cdiv(R, tr),),
            in_specs=[pl.BlockSpec((N_DEV, tr, C), lambda i, me_ref: (0, i, 0)),
                      pl.BlockSpec((N_DEV, extra, C), lambda i, me_ref: (0, (tr // extra) * (i + 1), 0)), blk, blk, blk],
            out_specs=[blk] * 4, scratch_shapes=[pltpu.VMEM((tr, C), F32)]),
        out_shape=[_sds((R * q, 128))] * 4,
        compiler_params=_params(("arbitrary",)),
    )(me, landed, landed, w, m, v)


_SMALL_ROWS = 8
_SMALL_SLOTS = ((0, 0, D_MODEL), (1, 0, D_MODEL), (2, 0, D_MODEL), (3, 0, GDN_DIM), (3, GDN_DIM, GDN_HEADS),
                (3, GDN_DIM + GDN_HEADS, GDN_HEADS))
_LOSS_LANE = 2 * GDN_DIM


def _pack_small(norm1, norm2, final, gnw, a_log, dt_bias, loss):
    row3 = jnp.concatenate([gnw, a_log, dt_bias, jnp.zeros((1, 128 - 2 * GDN_HEADS), F32), loss,
                            jnp.zeros((1, D_MODEL - 3 * 128), F32)], axis=1)
    return jnp.concatenate([norm1, norm2, final, row3, jnp.zeros((_SMALL_ROWS - 4, D_MODEL), F32)], axis=0)


def _adamw_small(packs, ws, ms, vs, *, name):
    n = len(ws)

    def body(p_ref, *refs):
        w_refs, m_refs, v_refs = refs[:n], refs[n:2 * n], refs[2 * n:3 * n]
        outs = refs[3 * n:]
        g_all = p_ref[0]
        for s in range(1, N_DEV):
            g_all = g_all + p_ref[s]
        for i, (row, lane, width) in enumerate(_SMALL_SLOTS):
            g = g_all[row:row + 1, lane:lane + width]
            delta, nm, nv = _adam_update(g, w_refs[i][...], m_refs[i][...], v_refs[i][...])
            for o_ref, val in zip(outs[4 * i:4 * i + 4], (g, delta, nm, nv)):
                o_ref[...] = val
        outs[-1][...] = g_all[3:4, _LOSS_LANE:_LOSS_LANE + 128]

    vm = pl.BlockSpec(memory_space=pltpu.VMEM)
    outs = pl.pallas_call(
        body, name=name, in_specs=[vm] * (1 + 3 * n), out_specs=[vm] * (4 * n + 1),
        out_shape=[_sds(w.shape) for w in ws for _ in range(4)] + [_sds((1, 128))],
    )(packs, *ws, *ms, *vs)
    return [outs[4 * i:4 * i + 4] for i in range(n)], outs[-1]


def _slabs_by_cols(g):
    r = g.shape[0]
    return g.reshape(r, N_DEV, -1).transpose(1, 0, 2)


def _cols_from_slabs(s):
    return s.transpose(1, 0, 2).reshape(s.shape[1], -1)


def kernel(x, norm1_w, w_in, conv_qkv_w, a_log, dt_bias, gdn_norm_w, w_out, norm2_w, w_up, ffn_conv_w, w_down, final_norm_w, loss_target, m_norm1_w, m_w_in, m_conv_qkv_w, m_a_log, m_dt_bias, m_gdn_norm_w, m_w_out, m_norm2_w, m_w_up, m_ffn_conv_w, m_w_down, m_final_norm_w, v_norm1_w, v_w_in, v_conv_qkv_w, v_a_log, v_dt_bias, v_gdn_norm_w, v_w_out, v_norm2_w, v_w_up, v_ffn_conv_w, v_w_down, v_final_norm_w):
    me = _slot(_position())
    me1 = jnp.reshape(me, (1,)).astype(jnp.int32)
    t_in = lambda a: a[0].T
    rows = lambda a: a.reshape(D_MODEL // 128, 128, -1).transpose(2, 0, 1).reshape(-1, 128)
    gw_in, g_conv_a, b_out, b_up, b_down = _all_gather(
        [_shifted_slab(rows(w_in), me1, name="shift_w_in"), conv_qkv_w[0]], casts=[w_out[0], t_in(w_up), w_down[0]],
        name="gather_w_in")
    late_src, _ = lax.optimization_barrier(([b_out, b_up, b_down, ffn_conv_w[0]], gw_in))
    l_send, l_recv, l_srcs, l_lands, l_token = _exchange_start(late_src, name="weights_start", broadcast=True)

    def late_weights(after):
        srcs, landed = _exchange_wait(l_send, l_recv, l_srcs, l_lands, after, name="weights_wait", broadcast=True)
        gw_out, gw_up, gw_down, g_conv_f = _with_own(landed, srcs, me)
        return gw_out.reshape(D_MODEL, D_MODEL), gw_up, g_conv_f, gw_down.reshape(D_FF, D_MODEL)

    flights = {}

    def emit(group, **grads):
        paired = ()
        if group == "in":
            *flight, token = _exchange_start_in([grads["w_a"], grads["w_z"], grads["w_b"]], _slabs_by_cols(grads["conv_a"]),
                                                name="grads_start_in")
            flights[group] = flight
            return (token,)
        if group == "ffn":
            slabs = dict(w_down=grads["w_down"].reshape(N_DEV, -1, D_MODEL), w_up=grads["w_up"], conv_f=grads["conv_f"])
            paired = (1, 2)
        else:
            slabs = {k: v.reshape(N_DEV, -1, D_MODEL) for k, v in grads.items()}
        names = list(slabs)
        *flight, token = _exchange_start([slabs[k] for k in names], paired=paired, name="grads_start_" + group)
        flights[group] = (names, flight)
        return (token,)

    loss, grad_x, g = _local_step(
        x[0], loss_target[0], norm1_w, gw_in, _cols_from_slabs(g_conv_a), a_log, dt_bias,
        gdn_norm_w, norm2_w, final_norm_w[None], late_weights, emit, start_after=(l_token,))
    got = {}

    def collect(group, after):
        names, (send_sems, recv_sems, srcs, lands) = flights[group]
        srcs, landed = _exchange_wait(send_sems, recv_sems, srcs, lands, after, name="grads_wait_" + group)
        got.update(zip(names, zip(landed, srcs)))

    def update(key, w, m, v, paired=False, **tiles):
        where = jnp.concatenate([_pair_slot(me1) if paired else me1, me1])
        return _adamw(*got[key], where, w, m, v, name="adamw_" + key, **tiles)

    in_sems, in_srcs, in_lands = flights["in"][:2], flights["in"][2], flights["in"][3]
    own_land, own_token = _own_pieces(in_srcs[:3], in_lands[0], grad_x, name="grads_own_in")
    collect("ffn", own_token)
    collect("out", own_token)
    *o_out, t1 = update("w_out", w_out[0], m_w_out[0], v_w_out[0])
    *o_up, t2 = update("w_up", t_in(w_up), t_in(m_w_up), t_in(v_w_up), paired=True, tr=176)
    o_up = [o.T for o in o_up]
    *o_down, t3 = update("w_down", w_down[0], m_w_down[0], v_w_down[0], tr=176)
    *o_cf, t4 = update("conv_f", ffn_conv_w[0], m_ffn_conv_w[0], v_ffn_conv_w[0], paired=True)
    pack = _pack_small(g["norm1"], g["norm2"], g["final"], g["gnw"], g["small"][:, 0:GDN_HEADS],
                       g["small"][:, GDN_HEADS:2 * GDN_HEADS], loss)
    small_all = _gather_direct(pack, after=(t1, t2, t3, t4), name="gather_small")
    srcs, (g_land, conv_land) = _exchange_wait_in(*in_sems, in_srcs, [own_land, in_lands[1]], small_all, name="grads_wait_in")
    got["conv_a"] = (conv_land, srcs[-1])
    o_in = [o.reshape(-1, D_MODEL // 128, 128).transpose(1, 2, 0).reshape(D_MODEL, -1) for o in _adamw_rowwise(
        g_land, me1, rows(w_in), rows(m_w_in), rows(v_w_in), name="adamw_w_in")]
    o_ca = update("conv_a", conv_qkv_w[0], m_conv_qkv_w[0], v_conv_qkv_w[0])
    (o_n1, o_n2, o_fin, o_gn, o_al, o_dt), total = _adamw_small(
        small_all, (norm1_w, norm2_w, final_norm_w[None], gdn_norm_w, a_log, dt_bias),
        (m_norm1_w, m_norm2_w, m_final_norm_w[None], m_gdn_norm_w, m_a_log, m_dt_bias),
        (v_norm1_w, v_norm2_w, v_final_norm_w[None], v_gdn_norm_w, v_a_log, v_dt_bias), name="adamw_small")
    outs = [total[0, 0], grad_x[None]]
    for k in range(4):
        outs += [o_n1[k], o_in[k][None], o_ca[k][None], o_al[k], o_dt[k], o_gn[k], o_out[k][None], o_n2[k], o_up[k][None],
                 o_cf[k][None], o_down[k][None], o_fin[k][0]]
    return tuple(outs)
```

```python
import functools

import jax
import jax.numpy as jnp
from jax import lax
from jax.experimental import pallas as pl
from jax.experimental.pallas import tpu as pltpu

F32 = jnp.float32
BF16 = jnp.bfloat16

N_DEV = 8
D_MODEL = 1024
GDN_HEADS = 4
GDN_DIM = 128
GDN_WIDTH = GDN_HEADS * GDN_DIM
GDN_CONV = 4
CHUNK = 64
CHUNKS_PER_STEP = 4
DIL_HEADS = 8
DIL_DIM = 64
DIL_WIDTH = DIL_HEADS * DIL_DIM
DIL_PAIRS = DIL_HEADS // 2
DILATIONS = (1, 4, 16)
BAND = 128
D_FF = 2816
FFN_CONV = 3
EPS = 1e-6
A_COLS = 3 * GDN_WIDTH + 128
HALO = 8

ADAM_LR = 0.001
ADAM_B1 = 0.9
ADAM_B2 = 0.999
ADAM_EPS = 1e-08
ADAM_WD = 0.01
ADAM_STEP = 10

VMEM_LIMIT_BYTES = 56 * 1024 * 1024
NEG_BIG = -1e30


def _params(sem=None):
    return pltpu.CompilerParams(dimension_semantics=sem, vmem_limit_bytes=VMEM_LIMIT_BYTES)


def _sds(shape, dtype=F32):
    return jax.ShapeDtypeStruct(shape, dtype)


def _bdot(a, b):
    return jnp.dot(a.astype(BF16), b.astype(BF16), preferred_element_type=F32)


def _bdot_nt(a, b):
    return lax.dot_general(a.astype(BF16), b.astype(BF16), (((1,), (1,)), ((), ())), preferred_element_type=F32)


def _bdot_tn(a, b):
    return lax.dot_general(a.astype(BF16), b.astype(BF16), (((0,), (0,)), ((), ())), preferred_element_type=F32)


def _split(a):
    hi = a.astype(BF16)
    lo = (a - hi.astype(F32)).astype(BF16)
    return hi, lo


def _dot3(a, b, dims):
    ah, al = _split(a)
    bh, bl = _split(b)
    d = functools.partial(lax.dot_general, dimension_numbers=(dims, ((), ())), preferred_element_type=F32)
    return d(ah, bh) + (d(al, bh) + d(ah, bl))


def _exact_tri_dot(tri, g):
    g1 = g.astype(BF16)
    r1 = g - g1.astype(F32)
    g2 = r1.astype(BF16)
    g3 = (r1 - g2.astype(F32)).astype(BF16)
    t = tri.astype(BF16)
    d = functools.partial(jnp.dot, preferred_element_type=F32)
    return d(t, g1) + (d(t, g2) + d(t, g3))


def _sigmoid(x):
    return 1.0 / (1.0 + jnp.exp(-x))


def _dsilu(x, sg):
    return sg * (1.0 + x * (1.0 - sg))


def _rms_bwd_rows(dh, x, w):
    r = lax.rsqrt(jnp.mean(x * x, axis=-1, keepdims=True) + EPS)
    xh = x * r
    gw = dh * w
    return r * (gw - xh * jnp.mean(gw * xh, axis=-1, keepdims=True)), jnp.sum(dh * xh, axis=0, keepdims=True)


def _mm(a, b, *, name, ta=False, tb=False, res=None, norm_bwd=None, after=(), out_dtype=F32, tm=512, tn=512, tk=512):
    if ta:
        K, M = a.shape
    else:
        M, K = a.shape
    if tb:
        N, Kb = b.shape
    else:
        Kb, N = b.shape
    assert K == Kb, (a.shape, b.shape)
    tm, tn, tk = min(tm, M), min(tn, N), min(tk, K)
    assert M % tm == 0 and N % tn == 0 and K % tk == 0, (name, M, N, K, tm, tn, tk)
    nk = K // tk
    dims = (((0 if ta else 1,), (1 if tb else 0,)), ((), ()))
    has_res = res is not None
    has_norm = norm_bwd is not None
    assert not has_norm or tn == N

    def body(*refs):
        a_ref, b_ref = refs[:2]
        r_ref = refs[2] if has_res else None
        if has_norm:
            x_ref, w_ref, skip_ref = refs[2 + has_res:5 + has_res]
            o_ref, dw_ref, acc_ref = refs[-3:]
        else:
            o_ref, acc_ref = refs[-2:]
        i, k = pl.program_id(0), pl.program_id(2)
        part = lax.dot_general(a_ref[...].astype(BF16), b_ref[...].astype(BF16), dims, preferred_element_type=F32)

        @pl.when(k == 0)
        def _():
            acc_ref[...] = part

        @pl.when(k > 0)
        def _():
            acc_ref[...] += part

        @pl.when(k == nk - 1)
        def _():
            r = acc_ref[...]
            if has_res:
                r = r + r_ref[...]
            if has_norm:
                dx, dw = _rms_bwd_rows(r, x_ref[...], w_ref[...])
                o_ref[...] = skip_ref[...] + dx

                @pl.when(i == 0)
                def _():
                    dw_ref[...] = dw

                @pl.when(i > 0)
                def _():
                    dw_ref[...] += dw
            else:
                o_ref[...] = r.astype(out_dtype)

    a_spec = pl.BlockSpec((tk, tm), lambda i, j, k: (k, i)) if ta else pl.BlockSpec((tm, tk), lambda i, j, k: (i, k))
    b_spec = pl.BlockSpec((tn, tk), lambda i, j, k: (j, k)) if tb else pl.BlockSpec((tk, tn), lambda i, j, k: (k, j))
    o_spec = pl.BlockSpec((tm, tn), lambda i, j, k: (i, j))
    one = pl.BlockSpec((1, tn), lambda i, j, k: (0, 0))
    in_specs = [a_spec, b_spec] + [o_spec] * has_res + ([o_spec, one, o_spec] if has_norm else []) + [ANY] * len(after)
    args = (a, b) + ((res,) if has_res else ()) + (tuple(norm_bwd) if has_norm else ()) + tuple(after)
    return pl.pallas_call(
        body, name=name, grid=(M // tm, N // tn, nk), in_specs=in_specs,
        out_specs=[o_spec, one] if has_norm else o_spec,
        out_shape=[_sds((M, N)), _sds((1, N))] if has_norm else _sds((M, N), out_dtype),
        scratch_shapes=[pltpu.VMEM((tm, tn), F32)],
        compiler_params=_params(("arbitrary" if has_norm else "parallel", "parallel", "arbitrary")),
    )(*args)


def _in_proj(x, norm_w, w_land, *, name, after=(), tm=512):
    S, D = x.shape

    def body(x_ref, nw_ref, land_ref, *rest):
        h_ref, pa_ref, pz_ref, pb_ref, *scratch = rest[len(after):]

        @pl.when(pl.program_id(0) == 0)
        def _():
            _fetch_w_in(land_ref, *scratch)

        xv = x_ref[...]
        r = lax.rsqrt(jnp.mean(xv * xv, axis=-1, keepdims=True) + EPS)
        h = (xv * r * nw_ref[...]).astype(BF16)
        h_ref[...] = h
        for w_ref, p_ref in zip(scratch[:3], (pa_ref, pz_ref, pb_ref)):
            p_ref[...] = lax.dot_general(h, w_ref[...], (((1,), (1,)), ((), ())), preferred_element_type=F32)

    row = lambda n: pl.BlockSpec((tm, n), lambda i: (i, 0))
    full = lambda a: pl.BlockSpec(a.shape, lambda i: (0, 0))
    return pl.pallas_call(
        body, name=name, grid=(S // tm,), in_specs=[row(D), full(norm_w), ANY] + [ANY] * len(after),
        out_specs=[row(D)] + [row(n) for n in _W_IN_ROWS],
        out_shape=[_sds((S, D), BF16)] + [_sds((S, n)) for n in _W_IN_ROWS],
        scratch_shapes=_w_in_scratch(D), compiler_params=_params(("arbitrary",)),
    )(x, norm_w, w_land, *after)


def _in_proj_dx(ds, w_land, x, norm_w, skip, *, name, after=(), tm=512):
    S, D = x.shape
    n = len(ds)

    def body(*refs):
        d_refs, land_ref = refs[:n], refs[n]
        x_ref, nw_ref, skip_ref = refs[n + 1:n + 4]
        o_ref, dw_ref, *scratch = refs[n + 4 + len(after):]
        w_refs = scratch[:n]
        i = pl.program_id(0)

        @pl.when(i == 0)
        def _():
            _fetch_w_in(land_ref, *scratch)

        dh = jnp.dot(d_refs[0][...], w_refs[0][...], preferred_element_type=F32)
        for d_ref, w_ref in zip(d_refs[1:], w_refs[1:]):
            dh = dh + jnp.dot(d_ref[...], w_ref[...], preferred_element_type=F32)
        dx, dw = _rms_bwd_rows(dh, x_ref[...], nw_ref[...])
        o_ref[...] = skip_ref[...] + dx

        @pl.when(i == 0)
        def _():
            dw_ref[...] = dw

        @pl.when(i > 0)
        def _():
            dw_ref[...] += dw

    row = lambda c: pl.BlockSpec((tm, c), lambda i: (i, 0))
    full = lambda a: pl.BlockSpec(a.shape, lambda i: (0, 0))
    return pl.pallas_call(
        body, name=name, grid=(S // tm,),
        in_specs=[row(d.shape[1]) for d in ds] + [ANY, row(D), full(norm_w), row(D)] + [ANY] * len(after),
        out_specs=[row(D), pl.BlockSpec((1, D), lambda i: (0, 0))], out_shape=[_sds((S, D)), _sds((1, D))],
        scratch_shapes=_w_in_scratch(D), compiler_params=_params(("arbitrary",)),
    )(*ds, w_land, x, norm_w, skip, *after)


def _out_proj_norm(a, w, x, norm_w, *, name, tm=512):
    S, D = x.shape

    def body(a_ref, w_ref, x_ref, nw_ref, x1_ref, h_ref):
        x1 = x_ref[...] + jnp.dot(a_ref[...], w_ref[...], preferred_element_type=F32)
        x1_ref[...] = x1
        r = lax.rsqrt(jnp.mean(x1 * x1, axis=-1, keepdims=True) + EPS)
        h_ref[...] = (x1 * r * nw_ref[...]).astype(BF16)

    row = pl.BlockSpec((tm, D), lambda i: (i, 0))
    return pl.pallas_call(
        body, name=name, grid=(S // tm,),
        in_specs=[pl.BlockSpec((tm, a.shape[1]), lambda i: (i, 0)), pl.BlockSpec(w.shape, lambda i: (0, 0)), row,
                  pl.BlockSpec((1, D), lambda i: (0, 0))],
        out_specs=[row, row], out_shape=[_sds((S, D)), _sds((S, D), BF16)], compiler_params=_params(("parallel",)),
    )(a, w, x, norm_w)


def _shifted(x, start, n):
    aligned = -(-start // HALO) * HALO
    assert aligned + n <= x.shape[0], (start, n, x.shape)
    return (x if aligned == start else pltpu.roll(x, aligned - start, axis=0))[aligned:aligned + n]


def _conv_rows(prev, cur, w, taps):
    n = cur.shape[0]
    xs = jnp.concatenate([prev, cur], axis=0)
    base = HALO - (taps - 1)
    out = _shifted(xs, base, n) * w[0:1]
    for i in range(1, taps):
        out = out + _shifted(xs, base + i, n) * w[i:i + 1]
    return out


def _conv_rows_bwd(cur_d, next_d, prev_x, cur_x, w, taps):
    n = cur_d.shape[0]
    ds = jnp.concatenate([cur_d, next_d], axis=0)
    dx = _shifted(ds, taps - 1, n) * w[0:1]
    for i in range(1, taps):
        dx = dx + _shifted(ds, taps - 1 - i, n) * w[i:i + 1]
    xs = jnp.concatenate([prev_x, cur_x], axis=0)
    base = HALO - (taps - 1)
    dws = [jnp.sum(cur_d * _shifted(xs, base + i, n), axis=0, keepdims=True) for i in range(taps)]
    return dx, jnp.concatenate(dws, axis=0)


def _halo_specs(tm, width, col, nblk):
    per = tm // HALO
    prev = pl.BlockSpec((HALO, width), lambda i, *_: (jnp.maximum(i * per - 1, 0), col))
    nxt = pl.BlockSpec((HALO, width), lambda i, *_: (jnp.minimum((i + 1) * per, nblk * per - 1), col))
    return prev, nxt


def _softplus(x):
    return jnp.maximum(x, 0.0) + jnp.log1p(jnp.exp(-jnp.abs(x)))


def _chunk_tri(tm, upper=False):
    r = lax.broadcasted_iota(jnp.int32, (tm, tm), 0)
    c = lax.broadcasted_iota(jnp.int32, (tm, tm), 1)
    same = lax.div(r, CHUNK) == lax.div(c, CHUNK)
    order = (c >= r) if upper else (c <= r)
    return jnp.where(same & order, 1.0, 0.0)


def _gdn_prep_fwd(proj_a, conv_w, a_log, dt_bias, *, name, tm=256):
    S = proj_a.shape[0]
    nblk = S // tm
    W3 = 3 * GDN_WIDTH

    def body(cur_ref, prev_ref, ba_ref, cw_ref, al_ref, dt_ref, qn_ref, kn_ref, v_ref, gcb_ref, bb_ref):
        i = pl.program_id(0)
        prev = jnp.where(i > 0, prev_ref[...], 0.0)
        c = _conv_rows(prev, cur_ref[...], cw_ref[...], GDN_CONV)
        a = c * _sigmoid(c)
        ba = ba_ref[...]
        lane = lax.broadcasted_iota(jnp.int32, (tm, 128), 1)
        g4 = jnp.zeros((tm, 128), F32)
        for h in range(GDN_HEADS):
            sl = slice(GDN_DIM * h, GDN_DIM * (h + 1))
            qh = a[:, GDN_DIM * h:GDN_DIM * (h + 1)]
            kh = a[:, GDN_WIDTH + GDN_DIM * h:GDN_WIDTH + GDN_DIM * (h + 1)]
            qn_ref[:, sl] = qh * (lax.rsqrt(jnp.sum(qh * qh, axis=-1, keepdims=True) + EPS) * (GDN_DIM ** -0.5))
            kn_ref[:, sl] = kh * lax.rsqrt(jnp.sum(kh * kh, axis=-1, keepdims=True) + EPS)
            beta = _sigmoid(ba[:, h:h + 1])
            bb_ref[:, sl] = jnp.broadcast_to(beta, (tm, GDN_DIM))
            g = -jnp.exp(al_ref[0:1, h:h + 1]) * _softplus(ba[:, GDN_HEADS + h:GDN_HEADS + h + 1] + dt_ref[0:1, h:h + 1])
            g4 = jnp.where(lane == h, g, g4)
        v_ref[...] = a[:, 2 * GDN_WIDTH:]
        gc = _exact_tri_dot(_chunk_tri(tm), g4)
        for h in range(GDN_HEADS):
            gcb_ref[:, GDN_DIM * h:GDN_DIM * (h + 1)] = jnp.broadcast_to(gc[:, h:h + 1], (tm, GDN_DIM))

    prev_spec, _ = _halo_specs(tm, W3, 0, nblk)
    row = pl.BlockSpec((tm, GDN_WIDTH), lambda i: (i, 0))
    small = lambda a: pl.BlockSpec(a.shape, lambda i: (0, 0))
    return pl.pallas_call(
        body, name=name, grid=(nblk,),
        in_specs=[pl.BlockSpec((tm, W3), lambda i: (i, 0)), prev_spec,
                  pl.BlockSpec((tm, 128), lambda i: (i, W3 // 128)), small(conv_w), small(a_log), small(dt_bias)],
        out_specs=[row] * 5, out_shape=[_sds((S, GDN_WIDTH))] * 5, compiler_params=_params(("parallel",)),
    )(proj_a, proj_a, proj_a, conv_w, a_log, dt_bias)


GDN_STACK = GDN_HEADS * CHUNK


def _stack(ref, rows):
    return jnp.concatenate([ref[rows, GDN_DIM * h:GDN_DIM * (h + 1)] for h in range(GDN_HEADS)], axis=0)


def _unstack_to(ref, rows, x):
    for h in range(GDN_HEADS):
        ref[rows, GDN_DIM * h:GDN_DIM * (h + 1)] = x[CHUNK * h:CHUNK * (h + 1)].astype(ref.dtype)


def _stack_masks():
    r = lax.broadcasted_iota(jnp.int32, (GDN_STACK, GDN_STACK), 0)
    c = lax.broadcasted_iota(jnp.int32, (GDN_STACK, GDN_STACK), 1)
    same = (r & -CHUNK) == (c & -CHUNK)
    return same & (r >= c), same & (r > c), r == c


def _stack_decay(gs, bs, incl):
    g2 = jnp.concatenate([gs, gs], axis=1)
    diff = g2 - g2.T
    dec = jnp.where(incl, jnp.exp(jnp.where(incl, diff, 0.0)), 0.0)
    return dec, jnp.concatenate([bs, bs], axis=1).T


def _head_mask():
    r = lax.broadcasted_iota(jnp.int32, (GDN_STACK, GDN_WIDTH), 0)
    c = lax.broadcasted_iota(jnp.int32, (GDN_STACK, GDN_WIDTH), 1)
    return (r & -CHUNK) * (GDN_DIM // CHUNK) == (c & -GDN_DIM)


def _head_spread(x):
    return jnp.where(_head_mask(), jnp.concatenate([x] * GDN_HEADS, axis=1), 0.0)


def _head_diag(x):
    xm = jnp.where(_head_mask(), x, 0.0)
    out = xm[:, 0:GDN_DIM]
    for h in range(1, GDN_HEADS):
        out = out + xm[:, GDN_DIM * h:GDN_DIM * (h + 1)]
    return out


def _last_rows(gs, n):
    return jnp.concatenate([jnp.broadcast_to(gs[CHUNK * (h + 1) - 1:CHUNK * (h + 1)], (n, GDN_DIM)) for h in range(GDN_HEADS)], axis=0)


def _gdn_chunk_fwd(qn, kn, v, gcb, bb, *, name):
    S = qn.shape[0]

    def body(qn_ref, kn_ref, v_ref, gcb_ref, bb_ref, uv_ref, wk_ref, at_ref, t_ref, wkb_ref, qdb_ref, keb_ref):
        incl, strict, diag = _stack_masks()
        for c in range(CHUNKS_PER_STEP):
            rows = slice(CHUNK * c, CHUNK * (c + 1))
            srows = slice(GDN_STACK * c, GDN_STACK * (c + 1))
            q, k, vv, gs, bs = [_stack(r, rows) for r in (qn_ref, kn_ref, v_ref, gcb_ref, bb_ref)]
            dec, bt = _stack_decay(gs, bs, incl)
            p = -jnp.where(strict, dec * _bdot_nt(k, k) * bt, 0.0)
            t = jnp.where(diag, 1.0, 0.0) + p
            for _ in range(5):
                p = _bdot(p, p)
                t = t + _bdot(t, p)
            sol = _dot3(t, jnp.concatenate([vv, jnp.exp(gs) * k], axis=1), ((1,), (0,)))
            _unstack_to(uv_ref, rows, sol[:, :GDN_DIM])
            _unstack_to(wk_ref, rows, sol[:, GDN_DIM:])
            at_ref[srows, :] = dec * _bdot_nt(q, k) * bt
            t_ref[srows, :] = t
            wkb_ref[srows, :] = _head_spread(sol[:, GDN_DIM:]).astype(BF16)
            qdb_ref[srows, :] = _head_spread(q * jnp.exp(gs)).astype(BF16)
            keb_ref[srows, :] = _head_spread(k * jnp.exp(_last_rows(gs, CHUNK) - gs) * bs).astype(BF16)

    step = CHUNKS_PER_STEP * CHUNK
    row = pl.BlockSpec((step, GDN_WIDTH), lambda n: (n, 0))
    sq = pl.BlockSpec((CHUNKS_PER_STEP * GDN_STACK, GDN_STACK), lambda n: (n, 0))
    wide = pl.BlockSpec((CHUNKS_PER_STEP * GDN_STACK, GDN_WIDTH), lambda n: (n, 0))
    nsq = S // CHUNK * GDN_STACK
    return pl.pallas_call(
        body, name=name, grid=(S // step,), in_specs=[row] * 5, out_specs=[row, row, sq, sq, wide, wide, wide],
        out_shape=[_sds((S, GDN_WIDTH)), _sds((S, GDN_WIDTH)), _sds((nsq, GDN_STACK)), _sds((nsq, GDN_STACK))]
        + [_sds((nsq, GDN_WIDTH), BF16)] * 3,
        compiler_params=_params(("parallel",)),
    )(qn, kn, v, gcb, bb)


SCAN_CHUNKS = 8


def _gdn_scan_fwd(uv, at, wkb, qdb, keb, gcb, proj_z, gnw, *, name):
    S = uv.shape[0]
    nc = S // CHUNK

    def body(uv_ref, at_ref, wkb_ref, qdb_ref, keb_ref, gcb_ref, z_ref, gnw_ref, o_ref, u_ref, sp_ref, oa_ref, st_ref):
        n = pl.program_id(0)

        @pl.when(n == 0)
        def _():
            st_ref[...] = jnp.zeros_like(st_ref)

        for c in range(SCAN_CHUNKS):
            rows = slice(CHUNK * c, CHUNK * (c + 1))
            srows = slice(GDN_STACK * c, GDN_STACK * (c + 1))
            st = st_ref[...]
            sp_ref[GDN_WIDTH * c:GDN_WIDTH * (c + 1), :] = st
            uv, gs, z = [_stack(r, rows) for r in (uv_ref, gcb_ref, z_ref)]
            u = uv - _bdot(wkb_ref[srows, :], st)
            o = _bdot(qdb_ref[srows, :], st) + _bdot(at_ref[srows, :], u)
            st_ref[...] = jnp.exp(_last_rows(gs, GDN_DIM)) * st + _bdot_tn(keb_ref[srows, :], u)
            _unstack_to(u_ref, rows, u)
            _unstack_to(o_ref, rows, o)
            r = lax.rsqrt(jnp.mean(o * o, axis=-1, keepdims=True) + EPS)
            oa = o * r * gnw_ref[...] * (z * _sigmoid(z))
            oa_ref[rows, :] = jnp.concatenate([oa[CHUNK * h:CHUNK * (h + 1)] for h in range(GDN_HEADS)], axis=1).astype(BF16)

    row = pl.BlockSpec((SCAN_CHUNKS * CHUNK, GDN_WIDTH), lambda n: (n, 0))
    sq = pl.BlockSpec((SCAN_CHUNKS * GDN_STACK, GDN_STACK), lambda n: (n, 0))
    wide = pl.BlockSpec((SCAN_CHUNKS * GDN_STACK, GDN_WIDTH), lambda n: (n, 0))
    return pl.pallas_call(
        body, name=name, grid=(nc // SCAN_CHUNKS,),
        in_specs=[row, sq, wide, wide, wide, row, row, pl.BlockSpec((1, GDN_DIM), lambda n: (0, 0))],
        out_specs=[row, row, pl.BlockSpec((SCAN_CHUNKS * GDN_WIDTH, GDN_DIM), lambda n: (n, 0)), row],
        out_shape=[_sds((S, GDN_WIDTH)), _sds((S, GDN_WIDTH)), _sds((nc * GDN_WIDTH, GDN_DIM)), _sds((S, 2 * GDN_WIDTH), BF16)],
        scratch_shapes=[pltpu.VMEM((GDN_WIDTH, GDN_DIM), F32)],
        compiler_params=_params(("arbitrary",)),
    )(uv, at, wkb, qdb, keb, gcb, proj_z, gnw)


def _gdn_scan_bwd(d_oab, o, proj_z, gnw, sp, u, at, wkb, qdb, keb, gcb, *, name, after=()):
    S = o.shape[0]
    nc = S // CHUNK
    ns = nc // SCAN_CHUNKS

    def body(do_ref, o_ref, z_ref, gnw_ref, sp_ref, u_ref, at_ref, wkb_ref, qdb_ref, keb_ref, gcb_ref, *rest):
        dz_ref, dgn_ref, du_ref, dwk_ref, dat_ref, dqd_ref, dke_ref, dgl_ref, ds_ref = rest[len(after):]
        n = pl.program_id(0)

        @pl.when(n == 0)
        def _():
            ds_ref[...] = jnp.zeros_like(ds_ref)
            dgn_ref[...] = jnp.zeros_like(dgn_ref)

        gw = gnw_ref[...]
        for c in reversed(range(SCAN_CHUNKS)):
            rows = slice(CHUNK * c, CHUNK * (c + 1))
            srows = slice(GDN_STACK * c, GDN_STACK * (c + 1))
            d_oa, oo, z, uu, gs = [_stack(r, rows) for r in (do_ref, o_ref, z_ref, u_ref, gcb_ref)]
            sg = _sigmoid(z)
            r = lax.rsqrt(jnp.mean(oo * oo, axis=-1, keepdims=True) + EPS)
            xh = oo * r
            dy = d_oa * (z * sg)
            _unstack_to(dz_ref, rows, d_oa * (xh * gw) * _dsilu(z, sg))
            dgn_ref[...] += jnp.sum(dy * xh, axis=0, keepdims=True)
            dxh = dy * gw
            do = r * (dxh - xh * jnp.mean(dxh * xh, axis=-1, keepdims=True))

            st = sp_ref[GDN_WIDTH * c:GDN_WIDTH * (c + 1), :]
            dst = ds_ref[...]
            ge = jnp.exp(_last_rows(gs, GDN_DIM))
            _unstack_to(dqd_ref, rows, _head_diag(_bdot_nt(do, st)))
            dat_ref[srows, :] = _bdot_nt(do, uu)
            du = _bdot_tn(at_ref[srows, :], do) + _bdot(keb_ref[srows, :], dst)
            _unstack_to(dke_ref, rows, _head_diag(_bdot_nt(uu, dst)))
            prod = dst * st
            for h in range(GDN_HEADS):
                blk = prod[GDN_DIM * h:GDN_DIM * (h + 1)]
                dge = jnp.sum(jnp.sum(blk, axis=1, keepdims=True), axis=0, keepdims=True)
                dgl_ref[c, :, GDN_DIM * h:GDN_DIM * (h + 1)] = jnp.broadcast_to(dge * ge[GDN_DIM * h:GDN_DIM * h + 1], (8, GDN_DIM))
            ds_ref[...] = _bdot_tn(qdb_ref[srows, :], do) + ge * dst - _bdot_tn(wkb_ref[srows, :], du)
            _unstack_to(du_ref, rows, du)
            _unstack_to(dwk_ref, rows, -_head_diag(_bdot_nt(du, st)))

    rev = lambda n: (ns - 1 - n, 0)
    row = pl.BlockSpec((SCAN_CHUNKS * CHUNK, GDN_WIDTH), rev)
    sq = pl.BlockSpec((SCAN_CHUNKS * GDN_STACK, GDN_STACK), rev)
    wide = pl.BlockSpec((SCAN_CHUNKS * GDN_STACK, GDN_WIDTH), rev)
    one = pl.BlockSpec((1, GDN_DIM), lambda n: (0, 0))
    return pl.pallas_call(
        body, name=name, grid=(ns,),
        in_specs=[row, row, row, one, pl.BlockSpec((SCAN_CHUNKS * GDN_WIDTH, GDN_DIM), rev), row, sq, wide, wide, wide, row]
        + [ANY] * len(after),
        out_specs=[row, one, row, row, sq, row, row, pl.BlockSpec((SCAN_CHUNKS, 8, GDN_WIDTH), lambda n: (ns - 1 - n, 0, 0))],
        out_shape=[_sds((S, GDN_WIDTH), BF16), _sds((1, GDN_DIM)), _sds((S, GDN_WIDTH)), _sds((S, GDN_WIDTH)),
                   _sds((nc * GDN_STACK, GDN_STACK)), _sds((S, GDN_WIDTH)), _sds((S, GDN_WIDTH)), _sds((nc, 8, GDN_WIDTH))],
        scratch_shapes=[pltpu.VMEM((GDN_WIDTH, GDN_DIM), F32)],
        compiler_params=_params(("arbitrary",)),
    )(d_oab, o, proj_z, gnw, sp, u, at, wkb, qdb, keb, gcb, *after)


def _gdn_chunk_bwd(qn, kn, gcb, bb, tmat, uv, wk, du, dwk, dat, dqd, dke, dgl, *, name):
    S = qn.shape[0]

    def body(qn_ref, kn_ref, gcb_ref, bb_ref, t_ref, uv_ref, wk_ref, du_ref, dwk_ref, dat_ref, dqd_ref, dke_ref,
             dgl_ref, dq_ref, dk_ref, dv_ref, dg_ref, dbeta_ref):
        incl, strict, _ = _stack_masks()
        lane = lax.broadcasted_iota(jnp.int32, (CHUNK, 128), 1)
        rowi = lax.broadcasted_iota(jnp.int32, (CHUNK, 1), 0)
        rsum = lambda x: jnp.sum(x, axis=-1, keepdims=True)
        for c in range(CHUNKS_PER_STEP):
            rows = slice(CHUNK * c, CHUNK * (c + 1))
            srows = slice(GDN_STACK * c, GDN_STACK * (c + 1))
            q, k, gs, bs, uv, wk, du, dwk, dqd, dke = [
                _stack(r, rows) for r in (qn_ref, kn_ref, gcb_ref, bb_ref, uv_ref, wk_ref, du_ref, dwk_ref, dqd_ref, dke_ref)]
            dec, bt = _stack_decay(gs, bs, incl)
            kk = _bdot_nt(k, k)
            qk = _bdot_nt(q, k)
            d_rhs = _dot3(t_ref[srows, :], jnp.concatenate([du, dwk], axis=1), ((0,), (0,)))
            sol = jnp.concatenate([uv, wk], axis=1)
            d_l = jnp.where(strict, -_dot3(d_rhs, sol, ((1,), (1,))), 0.0)
            d_a = jnp.where(incl, dat_ref[srows, :], 0.0)
            gam = jnp.exp(gs)
            e = jnp.exp(_last_rows(gs, CHUNK) - gs)
            d_gk = d_rhs[:, GDN_DIM:]
            ml = d_l * dec * bt
            ma = d_a * dec * bt
            _unstack_to(dq_ref, rows, _bdot(ma, k) + dqd * gam)
            _unstack_to(dk_ref, rows, _bdot(ml + ml.T, k) + _bdot_tn(ma, q) + d_gk * gam + dke * (e * bs))
            _unstack_to(dv_ref, rows, d_rhs[:, :GDN_DIM])
            wb = d_l * dec * kk + d_a * dec * qk
            ew = wb * bt
            s_ke = rsum(dke * k * (e * bs))
            dbeta = rsum(wb.T) + rsum(dke * k * e)
            dgc = rsum(ew) - rsum(ew.T) + rsum(dqd * q * gam) + rsum(d_gk * k * gam) - s_ke
            dgc4 = jnp.zeros((CHUNK, 128), F32)
            db4 = jnp.zeros((CHUNK, 128), F32)
            for h in range(GDN_HEADS):
                hr = slice(CHUNK * h, CHUNK * (h + 1))
                tail = jnp.sum(s_ke[hr], axis=0, keepdims=True) + dgl_ref[c, 0:1, GDN_DIM * h:GDN_DIM * h + 1]
                dgc4 = jnp.where(lane == h, dgc[hr] + jnp.where(rowi == CHUNK - 1, tail, 0.0), dgc4)
                db4 = jnp.where(lane == h, dbeta[hr], db4)
            dg_ref[rows, :] = _exact_tri_dot(_chunk_tri(CHUNK, upper=True), dgc4)
            dbeta_ref[rows, :] = db4

    step = CHUNKS_PER_STEP * CHUNK
    row = pl.BlockSpec((step, GDN_WIDTH), lambda n: (n, 0))
    sq = pl.BlockSpec((CHUNKS_PER_STEP * GDN_STACK, GDN_STACK), lambda n: (n, 0))
    col = pl.BlockSpec((step, 128), lambda n: (n, 0))
    return pl.pallas_call(
        body, name=name, grid=(S // step,),
        in_specs=[row] * 4 + [sq, row, row, row, row, sq, row, row,
                              pl.BlockSpec((CHUNKS_PER_STEP, 8, GDN_WIDTH), lambda n: (n, 0, 0))],
        out_specs=[row, row, row, col, col],
        out_shape=[_sds((S, GDN_WIDTH))] * 3 + [_sds((S, 128))] * 2, compiler_params=_params(("parallel",)),
    )(qn, kn, gcb, bb, tmat, uv, wk, du, dwk, dat, dqd, dke, dgl)


def _gdn_prep_bwd(dqn, dkn, dv, dg, dbeta, proj_a, conv_w, a_log, dt_bias, *, name, tm=256):
    S = proj_a.shape[0]
    nblk = S // tm
    W3 = 3 * GDN_WIDTH

    def body(dqn_ref, dkn_ref, dv_ref, dg_ref, dbeta_ref, cur_ref, prev_ref, ba_ref, cw_ref, al_ref, dt_ref,
             dc_ref, dba_ref, sm_ref):
        i = pl.program_id(0)
        prev = jnp.where(i > 0, prev_ref[...], 0.0)
        c = _conv_rows(prev, cur_ref[...], cw_ref[...], GDN_CONV)
        sg = _sigmoid(c)
        a = c * sg
        dsl = _dsilu(c, sg)
        ba = ba_ref[...]
        lane = lax.broadcasted_iota(jnp.int32, (tm, 128), 1)
        lane1 = lax.broadcasted_iota(jnp.int32, (1, 128), 1)
        dba = jnp.zeros((tm, 128), F32)
        sm = jnp.zeros((1, 128), F32)
        for h in range(GDN_HEADS):
            sl = slice(GDN_DIM * h, GDN_DIM * (h + 1))
            ks = slice(GDN_WIDTH + GDN_DIM * h, GDN_WIDTH + GDN_DIM * (h + 1))
            qh, kh = a[:, sl], a[:, ks]
            rq = lax.rsqrt(jnp.sum(qh * qh, axis=-1, keepdims=True) + EPS)
            rk = lax.rsqrt(jnp.sum(kh * kh, axis=-1, keepdims=True) + EPS)
            qhat, khat = qh * rq, kh * rk
            dyq = dqn_ref[:, sl] * (GDN_DIM ** -0.5)
            dyk = dkn_ref[:, sl]
            dq = rq * (dyq - qhat * jnp.sum(dyq * qhat, axis=-1, keepdims=True))
            dk = rk * (dyk - khat * jnp.sum(dyk * khat, axis=-1, keepdims=True))
            dc_ref[:, sl] = dq * dsl[:, sl]
            dc_ref[:, ks] = dk * dsl[:, ks]
            beta = _sigmoid(ba[:, h:h + 1])
            db = dbeta_ref[:, h:h + 1] * beta * (1.0 - beta)
            aneg = -jnp.exp(al_ref[0:1, h:h + 1])
            xa = ba[:, GDN_HEADS + h:GDN_HEADS + h + 1] + dt_ref[0:1, h:h + 1]
            dgh = dg_ref[:, h:h + 1]
            dxa = dgh * aneg * _sigmoid(xa)
            dba = jnp.where(lane == h, db, dba)
            dba = jnp.where(lane == GDN_HEADS + h, dxa, dba)
            d_alog = jnp.sum(dgh * _softplus(xa), axis=0, keepdims=True) * aneg
            sm = jnp.where(lane1 == h, d_alog, sm)
            sm = jnp.where(lane1 == GDN_HEADS + h, jnp.sum(dxa, axis=0, keepdims=True), sm)
        vs = slice(2 * GDN_WIDTH, W3)
        dc_ref[:, vs] = dv_ref[...] * dsl[:, vs]
        dba_ref[...] = dba

        @pl.when(i == 0)
        def _():
            sm_ref[...] = sm

        @pl.when(i > 0)
        def _():
            sm_ref[...] += sm

    prev_spec, _ = _halo_specs(tm, W3, 0, nblk)
    row = pl.BlockSpec((tm, GDN_WIDTH), lambda i: (i, 0))
    col = pl.BlockSpec((tm, 128), lambda i: (i, 0))
    small = lambda a: pl.BlockSpec(a.shape, lambda i: (0, 0))
    return pl.pallas_call(
        body, name=name, grid=(nblk,),
        in_specs=[row, row, row, col, col, pl.BlockSpec((tm, W3), lambda i: (i, 0)), prev_spec,
                  pl.BlockSpec((tm, 128), lambda i: (i, W3 // 128)), small(conv_w), small(a_log), small(dt_bias)],
        out_specs=[pl.BlockSpec((tm, W3), lambda i: (i, 0)), col, pl.BlockSpec((1, 128), lambda i: (0, 0))],
        out_shape=[_sds((S, W3)), _sds((S, 128)), _sds((1, 128))], compiler_params=_params(("arbitrary",)),
    )(dqn, dkn, dv, dg, dbeta, proj_a, proj_a, proj_a, conv_w, a_log, dt_bias)


def _gdn_conv_bwd(dc, dba, proj_a, conv_w, *, name, tm=256):
    S = proj_a.shape[0]
    nblk = S // tm
    W3 = 3 * GDN_WIDTH

    def body(dc_ref, dnext_ref, dba_ref, cur_ref, prev_ref, cw_ref, da_ref, dcw_ref):
        i = pl.program_id(0)
        prev = jnp.where(i > 0, prev_ref[...], 0.0)
        nxt = jnp.where(i < nblk - 1, dnext_ref[...], 0.0)
        dx, dw = _conv_rows_bwd(dc_ref[...], nxt, prev, cur_ref[...], cw_ref[...], GDN_CONV)
        da_ref[:, 0:W3] = dx.astype(BF16)
        da_ref[:, W3:] = dba_ref[...].astype(BF16)

        @pl.when(i == 0)
        def _():
            dcw_ref[...] = dw

        @pl.when(i > 0)
        def _():
            dcw_ref[...] += dw

    prev_spec, next_spec = _halo_specs(tm, W3, 0, nblk)
    wide = pl.BlockSpec((tm, W3), lambda i: (i, 0))
    return pl.pallas_call(
        body, name=name, grid=(nblk,),
        in_specs=[wide, next_spec, pl.BlockSpec((tm, 128), lambda i: (i, 0)), wide, prev_spec,
                  pl.BlockSpec(conv_w.shape, lambda i: (0, 0))],
        out_specs=[pl.BlockSpec((tm, A_COLS), lambda i: (i, 0)), pl.BlockSpec(conv_w.shape, lambda i: (0, 0))],
        out_shape=[_sds((S, A_COLS), BF16), _sds(conv_w.shape)], compiler_params=_params(("arbitrary",)),
    )(dc, dc, dba, proj_a, proj_a, conv_w)


def _band_mask(nk):
    i = lax.broadcasted_iota(jnp.int32, (2 * BAND, nk), 0) & (BAND - 1)
    j = lax.broadcasted_iota(jnp.int32, (2 * BAND, nk), 1)
    if nk == BAND:
        return j <= i
    return (j >= i) & (j <= i + BAND)


def _stack_heads(x, lo):
    return jnp.concatenate([jnp.where(lo, x, 0.0), jnp.where(lo, 0.0, x)], axis=0)


def _stack_cols(x):
    return jnp.concatenate([x[:, 0:1], x[:, DIL_DIM:DIL_DIM + 1]], axis=0)


def _unstack(x, lo):
    return jnp.where(lo, x[0:BAND], x[BAND:2 * BAND])


def _rows(start, size, stride):
    return pl.ds(start, size) if stride == 1 else pl.ds(start, size, stride=stride)


ATTN_LANES = 4


def _attn_blocks(S, visit_many, lanes=ATTN_LANES):
    for d in DILATIONS:
        nb = S // (d * BAND)
        if d == 1:
            half = nb // 2
            visit_many(d, [(0, 0, True), (0, half, False)])

            def pair(n, c):
                visit_many(1, [(0, n, False), (0, n + half, False)])
                return c
            lax.fori_loop(1, half, pair, 0)
        elif nb > 1:
            for r0 in range(0, d, lanes):
                visit_many(d, [(r0 + t, 0, True) for t in range(lanes)])

                def column(n, c, d=d, r0=r0):
                    visit_many(d, [(r0 + t, n, False) for t in range(lanes)])
                    return c
                lax.fori_loop(1, nb, column, 0)
        else:
            def group(g, c, d=d):
                visit_many(d, [(g * lanes + t, 0, True) for t in range(lanes)])
                return c
            lax.fori_loop(0, d // lanes, group, 0)


def _attn_fwd(proj_b, oab, *, name):
    S = proj_b.shape[0]
    scale = DIL_DIM ** -0.5

    def body(q_ref, k_ref, v_ref, oab_in_ref, ob_ref, lse_ref, m_ref, l_ref, acc_ref):
        del oab_in_ref
        lane = lax.broadcasted_iota(jnp.int32, (BAND, 128), 1)
        lo = lane < DIL_DIM
        m_ref[...] = jnp.full_like(m_ref, NEG_BIG)
        l_ref[...] = jnp.zeros_like(l_ref)
        acc_ref[...] = jnp.zeros_like(acc_ref)

        def load(d, r, n, first):
            nk = BAND if first else 2 * BAND
            qrows = _rows(r + n * (BAND * d), BAND, d)
            krows = _rows(r if first else r + (n - 1) * (BAND * d), nk, d)
            return dict(nk=nk, qrows=qrows, q=q_ref[qrows, :] * scale, k=k_ref[krows, :].astype(BF16),
                        v=v_ref[krows, :].astype(BF16), m=m_ref[qrows, :], l=l_ref[qrows, :], acc=acc_ref[qrows, :])

        def compute(b):
            q, k, v = b["q"], b["k"], b["v"]
            s = jnp.where(_band_mask(b["nk"]), _bdot_nt(_stack_heads(q, lo), k), NEG_BIG)
            m_old = _stack_cols(b["m"])
            m_new = jnp.maximum(m_old, jnp.max(s, axis=-1, keepdims=True))
            p = jnp.exp(s - m_new)
            alpha = _unstack(jnp.exp(m_old - m_new), lo)
            l_new = alpha * b["l"] + _unstack(jnp.sum(p, axis=-1, keepdims=True), lo)
            return _unstack(m_new, lo), l_new, alpha * b["acc"] + _unstack(_bdot(p, v), lo)

        def visit_many(d, blocks):
            loaded = [load(d, *blk) for blk in blocks]
            done = [compute(b) for b in loaded]
            for b, (m_new, l_new, acc_new) in zip(loaded, done):
                m_ref[b["qrows"], :] = m_new
                l_ref[b["qrows"], :] = l_new
                acc_ref[b["qrows"], :] = acc_new

        _attn_blocks(S, visit_many)
        ob_ref[...] = (acc_ref[...] / l_ref[...]).astype(BF16)
        lse_ref[...] = m_ref[...] + jnp.log(l_ref[...])

    part = lambda t: pl.BlockSpec((S, 128), lambda p: (0, 3 * p + t))
    return pl.pallas_call(
        body, name=name, grid=(DIL_PAIRS,),
        in_specs=[part(0), part(1), part(2), pl.BlockSpec(memory_space=pl.ANY)],
        out_specs=[pl.BlockSpec((S, 128), lambda p: (0, GDN_WIDTH // 128 + p)), pl.BlockSpec((S, 128), lambda p: (0, p))],
        out_shape=[_sds(oab.shape, BF16), _sds((S, DIL_WIDTH))],
        scratch_shapes=[pltpu.VMEM((S, 128), F32)] * 3, input_output_aliases={3: 0},
        compiler_params=_params(("parallel",)),
    )(proj_b, proj_b, proj_b, oab)


def _attn_bwd(proj_b, oab, d_oab, lse, *, name):
    S = proj_b.shape[0]
    scale = DIL_DIM ** -0.5

    def body(q_ref, k_ref, v_ref, o_ref, do_ref, lse_ref, dqkv_ref, dq_ref, dk_ref, dv_ref, delta_ref):
        lane = lax.broadcasted_iota(jnp.int32, (BAND, 128), 1)
        lo = lane < DIL_DIM
        dq_ref[...] = jnp.zeros_like(dq_ref)
        dk_ref[...] = jnp.zeros_like(dk_ref)
        dv_ref[...] = jnp.zeros_like(dv_ref)
        prod = do_ref[...] * o_ref[...].astype(F32)
        lo_all = lax.broadcasted_iota(jnp.int32, (S, 128), 1) < DIL_DIM
        delta_ref[...] = jnp.where(lo_all, jnp.sum(jnp.where(lo_all, prod, 0.0), axis=-1, keepdims=True),
                                   jnp.sum(jnp.where(lo_all, 0.0, prod), axis=-1, keepdims=True))

        def load(d, r, n, first):
            nk = BAND if first else 2 * BAND
            qrows = _rows(r + n * (BAND * d), BAND, d)
            krows = _rows(r if first else r + (n - 1) * (BAND * d), nk, d)
            return dict(nk=nk, qrows=qrows, krows=krows, q=q_ref[qrows, :] * scale, k=k_ref[krows, :], v=v_ref[krows, :],
                        do=do_ref[qrows, :], delta=delta_ref[qrows, :], lse=lse_ref[qrows, :],
                        dq=dq_ref[qrows, :], dk=dk_ref[krows, :], dv=dv_ref[krows, :])

        def compute(b):
            q, k, v, do = b["q"], b["k"], b["v"], b["do"]
            qs, dos = _stack_heads(q, lo), _stack_heads(do, lo)
            p = jnp.where(_band_mask(b["nk"]), jnp.exp(_bdot_nt(qs, k) - _stack_cols(b["lse"])), 0.0)
            ds = p * (_bdot_nt(dos, v) - _stack_cols(b["delta"]))
            dq = b["dq"] + _unstack(_bdot(ds, k), lo) * scale
            return dq, b["dk"] + _bdot_tn(ds, qs), b["dv"] + _bdot_tn(p, dos)

        def visit_many(d, blocks):
            loaded = [load(d, *blk) for blk in blocks]
            done = [compute(b) for b in loaded]
            for b, (dq, dk, dv) in zip(loaded, done):
                dq_ref[b["qrows"], :] = dq
                dk_ref[b["krows"], :] = dk
                dv_ref[b["krows"], :] = dv

        _attn_blocks(S, visit_many, lanes=2)
        dqkv_ref[:, 0:128] = dq_ref[...].astype(BF16)
        dqkv_ref[:, 128:256] = dk_ref[...].astype(BF16)
        dqkv_ref[:, 256:384] = dv_ref[...].astype(BF16)

    half = lambda p: (0, GDN_WIDTH // 128 + p)
    part = lambda t: pl.BlockSpec((S, 128), lambda p: (0, 3 * p + t))
    return pl.pallas_call(
        body, name=name, grid=(DIL_PAIRS,),
        in_specs=[part(0), part(1), part(2), pl.BlockSpec((S, 128), half), pl.BlockSpec((S, 128), half),
                  pl.BlockSpec((S, 128), lambda p: (0, p))],
        out_specs=pl.BlockSpec((S, 384), lambda p: (0, p)), out_shape=_sds((S, 3 * DIL_WIDTH), BF16),
        scratch_shapes=[pltpu.VMEM((S, 128), F32)] * 4, compiler_params=_params(("parallel",)),
    )(proj_b, proj_b, proj_b, oab, d_oab, lse)


FF_SLAB = 2 * D_FF // N_DEV
FF_PAIRS = N_DEV // 2
ROWS16 = 16


def _taps(w, x, base, n):
    out = _shifted(x, base, n) * w[0:1]
    for t in range(1, FFN_CONV):
        out = out + _shifted(x, base + t, n) * w[t:t + 1]
    return out


def _ffn_fwd(h2, x1, w_up, conv_w, w_down, final_w, tgt, *, name, tm=512):
    S, D = h2.shape
    ni = S // tm
    per = tm // ROWS16

    def body(h_ref, hp_ref, x1_ref, wg_ref, wu_ref, cg_ref, cu_ref, wd_ref, fw_ref, t_ref,
             dx_ref, dxb_ref, dfw_ref, loss_ref, ug_ref, uu_ref, x2_ref):
        i, j = pl.program_id(0), pl.program_id(1)
        hv = jnp.concatenate([hp_ref[...], h_ref[...]], axis=0)
        row = lax.broadcasted_iota(jnp.int32, (tm + ROWS16, 1), 0)
        keep = (i > 0) | (row >= ROWS16)

        def branch(w_ref, c_ref, u_ref):
            u = lax.dot_general(hv, w_ref[...], (((1,), (1,)), ((), ())), preferred_element_type=F32).astype(BF16)
            u_ref[...] = u[ROWS16:]
            return _taps(c_ref[...], jnp.where(keep, u.astype(F32), 0.0), ROWS16 - (FFN_CONV - 1), tm)

        gate = branch(wg_ref, cg_ref, ug_ref)
        up = branch(wu_ref, cu_ref, uu_ref)
        act = (gate * _sigmoid(gate) * up).astype(BF16)
        part = jnp.dot(act, wd_ref[...], preferred_element_type=F32)

        @pl.when(j == 0)
        def _():
            x2_ref[...] = x1_ref[...] + part

        @pl.when((j > 0) & (j < FF_PAIRS - 1))
        def _():
            x2_ref[...] += part

        @pl.when(j == FF_PAIRS - 1)
        def _():
            xv = x2_ref[...] + part
            wv = fw_ref[...]
            r = lax.rsqrt(jnp.mean(xv * xv, axis=-1, keepdims=True) + EPS)
            err = xv * r * wv - t_ref[...]
            lsum = jnp.sum(jnp.sum(err * err, axis=-1, keepdims=True), axis=0, keepdims=True) * (0.5 / D)
            g = err * (1.0 / D)
            xh = xv * r
            gw = g * wv
            dx = r * (gw - xh * jnp.mean(gw * xh, axis=-1, keepdims=True))
            dx_ref[...] = dx
            dxb_ref[...] = dx.astype(BF16)
            dfw = jnp.sum(g * xh, axis=0, keepdims=True)
            lpart = jnp.broadcast_to(lsum, (1, 128))

            @pl.when(i == 0)
            def _():
                dfw_ref[...] = dfw
                loss_ref[...] = lpart

            @pl.when(i > 0)
            def _():
                dfw_ref[...] += dfw
                loss_ref[...] += lpart

    rows = pl.BlockSpec((tm, D), lambda i, j: (i, 0))
    slab = lambda off: pl.BlockSpec((None, FF_SLAB, D), lambda i, j: (j + off, 0, 0))
    cslab = lambda off: pl.BlockSpec((None, FFN_CONV, FF_SLAB), lambda i, j: (j + off, 0, 0))
    uspec = pl.BlockSpec((None, tm, FF_SLAB), lambda i, j: (j, i, 0))
    return pl.pallas_call(
        body, name=name, grid=(ni, FF_PAIRS),
        in_specs=[rows, pl.BlockSpec((ROWS16, D), lambda i, j: (jnp.maximum(i * per - 1, 0), 0)), rows,
                  slab(0), slab(FF_PAIRS), cslab(0), cslab(FF_PAIRS), pl.BlockSpec((FF_SLAB, D), lambda i, j: (j, 0)),
                  pl.BlockSpec((1, D), lambda i, j: (0, 0)), rows],
        out_specs=[rows, rows, pl.BlockSpec((1, D), lambda i, j: (0, 0)), pl.BlockSpec((1, 128), lambda i, j: (0, 0)), uspec, uspec],
        out_shape=[_sds((S, D)), _sds((S, D), BF16), _sds((1, D)), _sds((1, 128)),
                   _sds((FF_PAIRS, S, FF_SLAB), BF16), _sds((FF_PAIRS, S, FF_SLAB), BF16)],
        scratch_shapes=[pltpu.VMEM((tm, D), F32)],
        compiler_params=_params(("arbitrary", "arbitrary")),
    )(h2, h2, x1, w_up, w_up, conv_w, conv_w, w_down, final_w, tgt)


def _ffn_bwd(dx2, h2, ug, uu, conv_w, w_down, *, name, tm=512):
    S, D = h2.shape
    ni = S // tm
    per = tm // ROWS16
    ext = tm + ROWS16

    def body(dx_ref, dxn_ref, h_ref, ug_ref, ugp_ref, ugn_ref, uu_ref, uup_ref, uun_ref, cg_ref, cu_ref, wd_ref,
             du_ref, gd_ref, gup_ref, dcw_ref, acc_d, acc_g, acc_u, acc_cg, acc_cu):
        i = pl.program_id(1)

        @pl.when(i == 0)
        def _():
            acc_d[...] = jnp.zeros_like(acc_d)
            acc_g[...] = jnp.zeros_like(acc_g)
            acc_u[...] = jnp.zeros_like(acc_u)
            acc_cg[...] = jnp.zeros_like(acc_cg)
            acc_cu[...] = jnp.zeros_like(acc_cu)

        dx = dx_ref[...]
        dxe = jnp.concatenate([dx, dxn_ref[...]], axis=0)
        row = lax.broadcasted_iota(jnp.int32, (ext, 1), 0)
        live = (i < ni - 1) | (row < tm)
        d_act = jnp.where(live, lax.dot_general(dxe, wd_ref[...], (((1,), (1,)), ((), ())), preferred_element_type=F32), 0.0)
        rowp = lax.broadcasted_iota(jnp.int32, (ext + ROWS16, 1), 0)
        keep = (i > 0) | (rowp >= ROWS16)

        def pre(cur, prev, nxt):
            return jnp.where(keep, jnp.concatenate([prev[...], cur[...], nxt[...]], axis=0).astype(F32), 0.0)

        uge, uue = pre(ug_ref, ugp_ref, ugn_ref), pre(uu_ref, uup_ref, uun_ref)
        cg, cu = cg_ref[...], cu_ref[...]
        base = ROWS16 - (FFN_CONV - 1)
        gate = _taps(cg, uge, base, ext)
        up = _taps(cu, uue, base, ext)
        sg = _sigmoid(gate)
        silu = gate * sg
        dgc = d_act * up * _dsilu(gate, sg)
        duc = d_act * silu

        def conv_t(w, dc):
            out = _shifted(dc, FFN_CONV - 1, tm) * w[0:1]
            for t in range(1, FFN_CONV):
                out = out + _shifted(dc, FFN_CONV - 1 - t, tm) * w[t:t + 1]
            return out.astype(BF16)

        du_g, du_u = conv_t(cg, dgc), conv_t(cu, duc)
        du_ref[0] = du_g
        du_ref[1] = du_u
        dcw = lambda dc, xe: jnp.concatenate(
            [jnp.sum(dc[0:tm] * _shifted(xe, base + t, tm), axis=0, keepdims=True) for t in range(FFN_CONV)], axis=0)
        acc_cg[0:FFN_CONV, :] += dcw(dgc, uge)
        acc_cu[0:FFN_CONV, :] += dcw(duc, uue)
        tn = (((0,), (0,)), ((), ()))
        act = (silu[0:tm] * up[0:tm]).astype(BF16)
        acc_d[...] += lax.dot_general(act, dx, tn, preferred_element_type=F32)
        hv = h_ref[...]
        acc_g[...] += lax.dot_general(du_g, hv, tn, preferred_element_type=F32)
        acc_u[...] += lax.dot_general(du_u, hv, tn, preferred_element_type=F32)

        @pl.when(i == ni - 1)
        def _():
            gd_ref[...] = acc_d[...].astype(BF16)
            gup_ref[0] = acc_g[...].astype(BF16)
            gup_ref[1] = acc_u[...].astype(BF16)
            dcw_ref[0] = acc_cg[0:FFN_CONV, :]
            dcw_ref[1] = acc_cu[0:FFN_CONV, :]

    last16 = S // ROWS16 - 1
    rows = pl.BlockSpec((tm, D), lambda j, i: (i, 0))
    rows_next = pl.BlockSpec((ROWS16, D), lambda j, i: (jnp.minimum((i + 1) * per, last16), 0))
    u_cur = pl.BlockSpec((None, tm, FF_SLAB), lambda j, i: (j, i, 0))
    u_prev = pl.BlockSpec((None, ROWS16, FF_SLAB), lambda j, i: (j, jnp.maximum(i * per - 1, 0), 0))
    u_next = pl.BlockSpec((None, ROWS16, FF_SLAB), lambda j, i: (j, jnp.minimum((i + 1) * per, last16), 0))
    cslab = lambda off: pl.BlockSpec((None, FFN_CONV, FF_SLAB), lambda j, i: (j + off, 0, 0))
    return pl.pallas_call(
        body, name=name, grid=(FF_PAIRS, ni),
        in_specs=[rows, rows_next, rows, u_cur, u_prev, u_next, u_cur, u_prev, u_next, cslab(0), cslab(FF_PAIRS),
                  pl.BlockSpec((FF_SLAB, D), lambda j, i: (j, 0))],
        out_specs=[pl.BlockSpec((None, 2, tm, FF_SLAB), lambda j, i: (j, 0, i, 0)), pl.BlockSpec((FF_SLAB, D), lambda j, i: (j, 0)),
                   pl.BlockSpec((None, 2, FF_SLAB, D), lambda j, i: (j, 0, 0, 0)),
                   pl.BlockSpec((None, 2, FFN_CONV, FF_SLAB), lambda j, i: (j, 0, 0, 0))],
        out_shape=[_sds((FF_PAIRS, 2, S, FF_SLAB), BF16), _sds((D_FF, D), BF16), _sds((FF_PAIRS, 2, FF_SLAB, D), BF16),
                   _sds((FF_PAIRS, 2, FFN_CONV, FF_SLAB))],
        scratch_shapes=[pltpu.VMEM((FF_SLAB, D), F32), pltpu.VMEM((FF_SLAB, D), F32), pltpu.VMEM((FF_SLAB, D), F32),
                        pltpu.VMEM((8, FF_SLAB), F32), pltpu.VMEM((8, FF_SLAB), F32)],
        compiler_params=_params(("parallel", "arbitrary")),
    )(dx2, dx2, h2, ug, ug, ug, uu, uu, uu, conv_w, conv_w, w_down)


def _pair_slot(p):
    return 2 * (p & (FF_PAIRS - 1)) + (p >> 2)


def _mm_slabs(a, w, *, name, res=None, norm_bwd=None, after=(), tm=1024, tn=1024):
    nk, S, _ = a.shape
    D = w.shape[2]
    has_res = res is not None
    has_norm = norm_bwd is not None
    assert not has_norm or tn == D

    def body(*refs):
        a_ref, w_ref = refs[:2]
        r_ref = refs[2] if has_res else None
        if has_norm:
            x_ref, nw_ref, skip_ref = refs[2 + has_res:5 + has_res]
            o_ref, dw_ref, acc_ref = refs[-3:]
        else:
            o_ref, acc_ref = refs[-2:]
        i, k = pl.program_id(0), pl.program_id(2)
        part = jnp.dot(a_ref[...], w_ref[...], preferred_element_type=F32)

        @pl.when(k == 0)
        def _():
            acc_ref[...] = part

        @pl.when(k > 0)
        def _():
            acc_ref[...] += part

        @pl.when(k == nk - 1)
        def _():
            r = acc_ref[...] + r_ref[...] if has_res else acc_ref[...]
            if has_norm:
                dx, dw = _rms_bwd_rows(r, x_ref[...], nw_ref[...])
                o_ref[...] = skip_ref[...] + dx

                @pl.when(i == 0)
                def _():
                    dw_ref[...] = dw

                @pl.when(i > 0)
                def _():
                    dw_ref[...] += dw
            else:
                o_ref[...] = r

    o_spec = pl.BlockSpec((tm, tn), lambda i, j, k: (i, j))
    one = pl.BlockSpec((1, tn), lambda i, j, k: (0, 0))
    return pl.pallas_call(
        body, name=name, grid=(S // tm, D // tn, nk),
        in_specs=[pl.BlockSpec((None, tm, FF_SLAB), lambda i, j, k: (k, i, 0)),
                  pl.BlockSpec((None, FF_SLAB, tn), lambda i, j, k: (FF_PAIRS * (k & 1) + (k >> 1), 0, j))] + [o_spec] * has_res
        + ([o_spec, one, o_spec] if has_norm else []) + [ANY] * len(after),
        out_specs=[o_spec, one] if has_norm else o_spec, out_shape=[_sds((S, D)), _sds((1, D))] if has_norm else _sds((S, D)),
        scratch_shapes=[pltpu.VMEM((tm, tn), F32)],
        compiler_params=_params(("arbitrary" if has_norm else "parallel", "parallel", "arbitrary")),
    )(*((a, w) + ((res,) if has_res else ()) + (tuple(norm_bwd) if has_norm else ()) + tuple(after)))


def _local_step(x, tgt, norm1_w, w_land, conv_a, a_log, dt_bias, gnw, norm2_w, final_w, late_weights, emit, start_after=()):
    wgrad = functools.partial(_mm, ta=True, out_dtype=BF16)
    h1, proj_a, proj_z, proj_b = _in_proj(x, norm1_w, w_land, after=start_after, name="in_proj")
    qn, kn, v, gcb, bb = _gdn_prep_fwd(proj_a, conv_a, a_log, dt_bias, name="gdn_prep_fwd")
    uv, wk, at, tmat, wkb, qdb, keb = _gdn_chunk_fwd(qn, kn, v, gcb, bb, name="gdn_chunk_fwd")
    o, u, sp, oab = _gdn_scan_fwd(uv, at, wkb, qdb, keb, gcb, proj_z, gnw, name="gdn_scan_fwd")
    oab, lse = _attn_fwd(proj_b, oab, name="attn_fwd")
    w_out, w_up, conv_f, w_down = late_weights(oab)
    x1, h2 = _out_proj_norm(oab, w_out, x, norm2_w, name="out_proj")
    dx2, dx2_b, d_final, loss, ug, uu = _ffn_fwd(h2, x1, w_up, conv_f, w_down, final_w, tgt, name="ffn_fwd")
    du, g_down, g_up, dcw = _ffn_bwd(dx2_b, h2, ug, uu, conv_f, w_down, name="ffn_bwd")
    token = emit("ffn", w_down=g_down, w_up=g_up.reshape(N_DEV, FF_SLAB, -1), conv_f=dcw.reshape(N_DEV, FFN_CONV, -1))
    dx1, d_norm2 = _mm_slabs(du.reshape(N_DEV, -1, FF_SLAB), w_up, norm_bwd=(x1, norm2_w, dx2), after=token, name="ffn_up_dx")
    d_oab = _mm(dx1, w_out, tb=True, name="out_proj_dx", tn=D_MODEL, tk=1024)
    token = emit("out", w_out=wgrad(oab, dx1, name="out_proj_dw", tm=D_MODEL, tn=D_MODEL))
    dz, d_gnw, du, dwk, dat, dqd, dke, dgl = _gdn_scan_bwd(d_oab, o, proj_z, gnw, sp, u, at, wkb, qdb, keb, gcb, after=token, name="gdn_scan_bwd")
    dqn, dkn, dv, dg, dbeta = _gdn_chunk_bwd(qn, kn, gcb, bb, tmat, uv, wk, du, dwk, dat, dqd, dke, dgl, name="gdn_chunk_bwd")
    dc, dba, d_small = _gdn_prep_bwd(dqn, dkn, dv, dg, dbeta, proj_a, conv_a, a_log, dt_bias, name="gdn_prep_bwd")
    d_pa, d_conv_a = _gdn_conv_bwd(dc, dba, proj_a, conv_a, name="gdn_conv_bwd")
    d_pb = _attn_bwd(proj_b, oab, d_oab, lse, name="attn_bwd")
    g_a = wgrad(d_pa, h1, name="proj_a_dw", tm=A_COLS, tn=D_MODEL)
    g_z = wgrad(dz, h1, name="proj_z_dw", tn=D_MODEL)
    g_b = wgrad(d_pb, h1, name="proj_b_dw", tm=768, tn=D_MODEL)
    token = emit("in", w_a=g_a, w_z=g_z, w_b=g_b, conv_a=d_conv_a)
    grad_x, d_norm1 = _in_proj_dx((d_pa, dz, d_pb), w_land, x, norm1_w, dx1, after=token, name="in_proj_dx")
    small = dict(norm1=d_norm1, small=d_small, gnw=d_gnw, norm2=d_norm2, final=d_final)
    return loss, grad_x, small


_O1 = 3 * GDN_WIDTH
_O2 = _O1 + GDN_WIDTH
_O3 = _O2 + 2 * GDN_HEADS


_W_IN_ROWS = (A_COLS, GDN_WIDTH, 3 * DIL_WIDTH)
_IN_ROWS = (_O3 + 3 * DIL_WIDTH) // N_DEV
_ROW_TILE = 16
_IN_STEP = _IN_ROWS - _IN_ROWS % _ROW_TILE
_GAP = 8
_LAND_ROWS = 464
assert _O3 % _ROW_TILE == _ROW_TILE - _GAP and N_DEV - 1 + _GAP + _IN_ROWS <= _LAND_ROWS and _LAND_ROWS % _ROW_TILE == 0


def _padded_row(r):
    return r + (_GAP if r >= _O3 else 0)


def _shifted_slab(w_rows, j, *, name):
    q = w_rows.shape[0] // _IN_ROWS

    def body(j_ref, w_ref, o_ref, pad_ref):
        pad_ref[...] = jnp.zeros_like(pad_ref)
        for dev in range(N_DEV):
            @pl.when(j_ref[0] == dev)
            def _(dev=dev):
                p = lax.broadcasted_iota(jnp.int32, (_LAND_ROWS, 1), 0) + _IN_STEP * dev
                for s in range(q):
                    pad_ref[0:_IN_ROWS, :] = w_ref[pl.ds(s, _IN_ROWS, stride=q), :]
                    rows = pad_ref[...]
                    a = pltpu.roll(rows, dev, 0) if dev else rows
                    b = pltpu.roll(rows, dev + _GAP, 0)
                    o_ref[:, 128 * s:128 * (s + 1)] = jnp.where(p < _O3, a, jnp.where(p >= _O3 + _GAP, b, 0.0)).astype(BF16)

    vm = pl.BlockSpec(memory_space=pltpu.VMEM)
    return pl.pallas_call(
        body, name=name, in_specs=[pl.BlockSpec(memory_space=pltpu.SMEM), vm], out_specs=vm,
        out_shape=_sds((_LAND_ROWS, 128 * q), BF16), scratch_shapes=[pltpu.VMEM((_LAND_ROWS, 128), F32)],
    )(j, w_rows)


def _w_in_plan():
    def dest(p):
        if p < _O1:
            return 0, p
        if p < _O2:
            return 1, p - _O1
        if p < _O2 + _ROW_TILE:
            return 0, _O1
        q = p - _O3 - _GAP
        t, pair = divmod(q // 128, DIL_PAIRS)
        return 2, (3 * pair + t) * 128 + q % 128

    spans = [(_padded_row(_IN_ROWS * j), _padded_row(_IN_ROWS * (j + 1) - 1) + 1) for j in range(N_DEV)]
    runs, seams = [], []
    for p in range(0, spans[-1][1], _ROW_TILE):
        owners = [j for j, (lo, hi) in enumerate(spans) if lo < p + _ROW_TILE and hi > p]
        w, r = dest(p)
        if len(owners) == 2:
            seams.append((w, r, owners[0], p - _IN_STEP * owners[0], owners[1], p - _IN_STEP * owners[1]))
            continue
        (j,) = owners
        last = runs[-1] if runs else None
        if last and last[0] == j and last[2] == w and last[3] + last[4] == r and last[1] + last[4] == p - _IN_STEP * j:
            runs[-1] = last[:4] + (last[4] + _ROW_TILE,)
        else:
            runs.append((j, p - _IN_STEP * j, w, r, _ROW_TILE))
    return runs, seams


def _w_in_scratch(d):
    return [pltpu.VMEM((n, d), BF16) for n in _W_IN_ROWS] + [pltpu.VMEM((N_DEV - 1, _ROW_TILE, d), BF16),
                                                              pltpu.SemaphoreType.DMA(())]


def _fetch_w_in(land_ref, wa_ref, wz_ref, wb_ref, seam_ref, sem):
    w_refs = (wa_ref, wz_ref, wb_ref)
    runs, seams = _w_in_plan()
    copies = [pltpu.make_async_copy(land_ref.at[j, pl.ds(s, n)], w_refs[w].at[pl.ds(r, n)], sem) for j, s, w, r, n in runs]
    for k, (w, r, j0, s0, j1, s1) in enumerate(seams):
        copies.append(pltpu.make_async_copy(land_ref.at[j0, pl.ds(s0, _ROW_TILE)], w_refs[w].at[pl.ds(r, _ROW_TILE)], sem))
        copies.append(pltpu.make_async_copy(land_ref.at[j1, pl.ds(s1, _ROW_TILE)], seam_ref.at[k], sem))
    for cp in copies:
        cp.start()
    tail = _O1 + _ROW_TILE
    wa_ref[tail:, :] = jnp.zeros((A_COLS - tail, wa_ref.shape[1]), BF16)
    for cp in copies:
        cp.wait()
    for k, (w, r, *_) in enumerate(seams):
        both = w_refs[w][r:r + _ROW_TILE, :].astype(F32) + seam_ref[k].astype(F32)
        w_refs[w][r:r + _ROW_TILE, :] = both.astype(BF16)


MESH = pl.DeviceIdType.MESH
ANY = pl.BlockSpec(memory_space=pl.ANY)


def _position():
    return lax.axis_index("x"), lax.axis_index("y"), lax.axis_index("c")


def _slot(p):
    return 4 * p[0] + 2 * p[1] + p[2]


def _all_gather(blocks, *, name, casts=()):
    n, nc = len(blocks), len(casts)

    def body(*refs):
        ins, outs = refs[:n], refs[n + nc:2 * n + nc]
        c_ins, c_outs = refs[n:n + nc], refs[2 * n + nc:2 * n + 2 * nc]
        send_sems, recv_sems, local_sems = refs[2 * n + 2 * nc:2 * n + 2 * nc + 3]
        c_f32, c_bf, c_sems = (refs[-3 * nc:-2 * nc], refs[-2 * nc:-nc], refs[-nc:]) if nc else ((), (), ())
        loads = [pltpu.make_async_copy(c_ins[i], c_f32[i], c_sems[i]) for i in range(nc)]
        for cp in loads:
            cp.start()
        x, y, c = _position()
        me, sibling = (x, y, c), (x, y, 1 - c)
        chips = [(1 - x, y), (x, 1 - y), (1 - x, 1 - y)]

        def copy(a, k, block, to, src=None):
            dst = outs[a].at[_slot(block)]
            return pltpu.make_async_remote_copy(
                src_ref=dst if src is None else src, dst_ref=dst, send_sem=send_sems.at[a, k], recv_sem=recv_sems.at[a, k],
                device_id=to, device_id_type=MESH)

        mine = [pltpu.make_async_copy(ins[a], outs[a].at[_slot(me)], local_sems.at[a]) for a in range(n)]
        for cp in mine:
            cp.start()
        first = []
        for a in range(n):
            first.append(copy(a, 0, me, sibling, src=ins[a]))
            first += [copy(a, 1 + j, me, (*chip, c), src=ins[a]) for j, chip in enumerate(chips)]
        for cp in first:
            cp.start()
        stores = []
        for i in range(nc):
            loads[i].wait()
            c_bf[i][...] = c_f32[i][...].astype(BF16)
            stores.append(pltpu.make_async_copy(c_bf[i], c_outs[i], c_sems[i]))
            stores[-1].start()
        passed = []
        for j, chip in enumerate(chips):
            for a in range(n):
                copy(a, 1 + j, (*chip, c), me).wait_recv()
                fwd = copy(a, 4 + j, (*chip, c), sibling)
                fwd.start()
                passed.append(fwd)
        for a in range(n):
            copy(a, 0, sibling, me).wait_recv()
            for j, chip in enumerate(chips):
                copy(a, 4 + j, (*chip, 1 - c), me).wait_recv()
        for cp in first + passed:
            cp.wait_send()
        for cp in mine + stores:
            cp.wait()

    return pl.pallas_call(
        body, name=name, in_specs=[ANY] * (n + nc), out_specs=[ANY] * (n + nc),
        out_shape=[_sds((N_DEV,) + b.shape, b.dtype) for b in blocks] + [_sds(a.shape, BF16) for a in casts],
        scratch_shapes=[pltpu.SemaphoreType.DMA((n, 7)), pltpu.SemaphoreType.DMA((n, 7)), pltpu.SemaphoreType.DMA((n,))]
        + [pltpu.VMEM(a.shape, F32) for a in casts] + [pltpu.VMEM(a.shape, BF16) for a in casts]
        + [pltpu.SemaphoreType.DMA(()) for _ in casts],
        compiler_params=_params(),
    )(*blocks, *casts)


def _gather_direct(block, *, name, after=()):
    def body(in_ref, *rest):
        out_ref, send_sems, recv_sems, local_sem = rest[len(after):]
        x, y, c = _position()
        me = _slot((x, y, c))
        mine = pltpu.make_async_copy(in_ref, out_ref.at[me], local_sem)
        mine.start()
        copies = [pltpu.make_async_remote_copy(
            src_ref=in_ref, dst_ref=out_ref.at[me], send_sem=send_sems.at[k - 1], recv_sem=recv_sems.at[k - 1],
            device_id=_peer_of(k, x, y, c), device_id_type=MESH) for k in range(1, N_DEV)]
        for cp in copies:
            cp.start()
        for cp in copies:
            cp.wait()
        mine.wait()

    return pl.pallas_call(
        body, name=name, in_specs=[pl.BlockSpec(memory_space=pltpu.VMEM)] + [ANY] * len(after),
        out_specs=pl.BlockSpec(memory_space=pltpu.VMEM),
        out_shape=_sds((N_DEV,) + block.shape, block.dtype),
        scratch_shapes=[pltpu.SemaphoreType.DMA((N_DEV - 1,)), pltpu.SemaphoreType.DMA((N_DEV - 1,)), pltpu.SemaphoreType.DMA],
    )(block, *after)


HBM = pl.BlockSpec(memory_space=pltpu.HBM)
SEM = pl.BlockSpec(memory_space=pltpu.SEMAPHORE)
EFFECT = pltpu.SideEffectType.DATAFLOW_SIDE_EFFECTING


def _peer_of(k, x, y, c):
    return (1 - x if k & 4 else x, 1 - y if k & 2 else y, 1 - c if k & 1 else c)


def _flight(a, k):
    return a * (N_DEV - 1) + k - 1


def _exchange_start(arrays, *, name, broadcast=False, paired=()):
    n = len(arrays)

    def body(*refs):
        ins, lands = refs[:n], refs[n:2 * n]
        send_sems, recv_sems = refs[2 * n:2 * n + 2]
        token = refs[-1]
        x, y, c = _position()
        me = _slot((x, y, c))
        for k in range(1, N_DEV):
            peer = _peer_of(k, x, y, c)
            for a in range(n):
                at = _pair_slot(_slot(peer)) if a in paired else _slot(peer)
                pltpu.make_async_remote_copy(
                    src_ref=ins[a] if broadcast else ins[a].at[at], dst_ref=lands[a].at[me],
                    send_sem=send_sems.at[_flight(a, k)], recv_sem=recv_sems.at[_flight(a, k)],
                    device_id=peer, device_id_type=MESH).start()
        token[...] = jnp.zeros_like(token)

    land_shapes = [((N_DEV,) + s.shape) if broadcast else s.shape for s in arrays]
    lands = [pltpu.with_memory_space_constraint(lax.empty(shp, s.dtype), pltpu.HBM) for shp, s in zip(land_shapes, arrays)]
    srcs = [pltpu.with_memory_space_constraint(s, pltpu.HBM) for s in arrays]
    outs = pl.pallas_call(
        body, name=name, in_specs=[HBM] * (2 * n),
        out_specs=[SEM, SEM] + [HBM] * (2 * n) + [pl.BlockSpec(memory_space=pltpu.VMEM)],
        out_shape=[pltpu.SemaphoreType.DMA((n * (N_DEV - 1),)), pltpu.SemaphoreType.DMA((n * (N_DEV - 1),))]
        + [pltpu.HBM(s.shape, s.dtype) for s in arrays] + [pltpu.HBM(shp, s.dtype) for shp, s in zip(land_shapes, arrays)]
        + [_sds((8, 128))],
        input_output_aliases={i: 2 + i for i in range(2 * n)},
        compiler_params=pltpu.CompilerParams(has_side_effects=EFFECT),
    )(*srcs, *lands)
    return outs[0], outs[1], outs[2:2 + n], outs[2 + n:2 + 2 * n], outs[-1]


def _exchange_wait(send_sems, recv_sems, srcs, lands, after, *, name, broadcast=False):
    n = len(srcs)

    def body(*refs):
        ins, lnd = refs[:n], refs[n:2 * n]
        send_ref, recv_ref = refs[2 * n:2 * n + 2]
        x, y, c = _position()
        for k in range(1, N_DEV):
            for a in range(n):
                cp = pltpu.make_async_remote_copy(
                    src_ref=ins[a] if broadcast else ins[a].at[0], dst_ref=lnd[a].at[0], send_sem=send_ref.at[_flight(a, k)],
                    recv_sem=recv_ref.at[_flight(a, k)], device_id=_peer_of(k, x, y, c), device_id_type=MESH)
                cp.wait_send()
                cp.wait_recv()

    outs = pl.pallas_call(
        body, name=name, in_specs=[HBM] * (2 * n) + [SEM, SEM, ANY], out_specs=[HBM] * (2 * n),
        out_shape=[pltpu.HBM(s.shape, s.dtype) for s in srcs] + [pltpu.HBM(s.shape, s.dtype) for s in lands],
        input_output_aliases={i: i for i in range(2 * n)},
        compiler_params=pltpu.CompilerParams(has_side_effects=EFFECT),
    )(*srcs, *lands, send_sems, recv_sems, after)
    return outs[:n], outs[n:]


_G_LAND_ROWS = 528


def _g_in_pieces(dev):
    runs, seams = _w_in_plan()
    pieces = [(w, r, n, s) for j, s, w, r, n in runs if j == dev]
    pieces += [(w, r, _ROW_TILE, s0) for w, r, j0, s0, j1, s1 in seams if j0 == dev]
    pieces += [(w, r, _ROW_TILE, s1) for w, r, j0, s0, j1, s1 in seams if j1 == dev]
    merged = []
    for w, r, n, s in sorted(pieces, key=lambda p: p[3]):
        if merged and merged[-1][0] == w and merged[-1][1] + merged[-1][2] == r and merged[-1][3] + merged[-1][2] == s:
            merged[-1] = (w, merged[-1][1], merged[-1][2] + n, merged[-1][3])
        else:
            merged.append((w, r, n, s))
    return merged


def _coords(dev):
    return tuple(jnp.int32(v) for v in (dev >> 2, (dev >> 1) & 1, dev & 1))


def _exchange_start_in(g_ws, conv_slabs, *, name):
    srcs = list(g_ws) + [conv_slabs]
    n = len(srcs)

    def body(*refs):
        g_refs, cv_ref, land, land_cv = refs[:n - 1], refs[n - 1], refs[n], refs[n + 1]
        send_sems, recv_sems = refs[n + 2:n + 4]
        token = refs[-1]
        me = _slot(_position())
        pieces = [_g_in_pieces(dev) for dev in range(N_DEV)]
        for i in range(max(len(p) for p in pieces)):
            for dev in range(N_DEV):
                if i < len(pieces[dev]):
                    @pl.when(me != dev)
                    def _(dev=dev, i=i):
                        w, r, rows, s = pieces[dev][i]
                        k = me ^ dev
                        pltpu.make_async_remote_copy(
                            src_ref=g_refs[w].at[pl.ds(r, rows)], dst_ref=land.at[me, pl.ds(s, rows)],
                            send_sem=send_sems.at[_flight(0, k)], recv_sem=recv_sems.at[_flight(0, k)],
                            device_id=_coords(dev), device_id_type=MESH).start()
        for dev in range(N_DEV):
            @pl.when(me != dev)
            def _(dev=dev):
                k = me ^ dev
                pltpu.make_async_remote_copy(
                    src_ref=cv_ref.at[dev], dst_ref=land_cv.at[me], send_sem=send_sems.at[_flight(1, k)],
                    recv_sem=recv_sems.at[_flight(1, k)], device_id=_coords(dev), device_id_type=MESH).start()
        token[...] = jnp.zeros_like(token)

    lands = [lax.empty((N_DEV, _G_LAND_ROWS, g_ws[0].shape[1]), g_ws[0].dtype), lax.empty(conv_slabs.shape, conv_slabs.dtype)]
    ops = [pltpu.with_memory_space_constraint(a, pltpu.HBM) for a in srcs + lands]
    outs = pl.pallas_call(
        body, name=name, in_specs=[HBM] * len(ops),
        out_specs=[SEM, SEM] + [HBM] * len(ops) + [pl.BlockSpec(memory_space=pltpu.VMEM)],
        out_shape=[pltpu.SemaphoreType.DMA((2 * (N_DEV - 1),)), pltpu.SemaphoreType.DMA((2 * (N_DEV - 1),))]
        + [pltpu.HBM(a.shape, a.dtype) for a in ops] + [_sds((8, 128))],
        input_output_aliases={i: 2 + i for i in range(len(ops))},
        compiler_params=pltpu.CompilerParams(has_side_effects=EFFECT),
    )(*ops)
    return outs[0], outs[1], outs[2:2 + n], outs[2 + n:4 + n], outs[-1]


def _own_pieces(g_ws, land, after, *, name):
    n = len(g_ws)

    def body(*refs):
        g_refs, land_ref = refs[:n], refs[n]
        token, slab, sem = refs[-3:]
        me = _slot(_position())
        for dev in range(N_DEV):
            @pl.when(me == dev)
            def _(dev=dev):
                own = [pltpu.make_async_copy(g_refs[w].at[pl.ds(r, rows)], slab.at[pl.ds(s, rows)], sem)
                       for w, r, rows, s in _g_in_pieces(dev)]
                for cp in own:
                    cp.start()
                for cp in own:
                    cp.wait()
                out = pltpu.make_async_copy(slab, land_ref.at[dev, pl.ds(0, _LAND_ROWS)], sem)
                out.start()
                out.wait()
        token[...] = jnp.zeros_like(token)

    return pl.pallas_call(
        body, name=name, in_specs=[HBM] * (n + 1) + [ANY], out_specs=[HBM, pl.BlockSpec(memory_space=pltpu.VMEM)],
        out_shape=[pltpu.HBM(land.shape, land.dtype), _sds((8, 128))], input_output_aliases={n: 0},
        scratch_shapes=[pltpu.VMEM((_LAND_ROWS, land.shape[2]), land.dtype), pltpu.SemaphoreType.DMA(())],
    )(*g_ws, land, after)


def _exchange_wait_in(send_sems, recv_sems, srcs, lands, after, *, name):
    n = len(srcs)

    def body(*refs):
        g_refs, cv_ref, land, land_cv = refs[:n - 1], refs[n - 1], refs[n], refs[n + 1]
        send_ref, recv_ref = refs[n + 2:n + 4]
        me = _slot(_position())

        def copies(dev, k):
            cps = [pltpu.make_async_remote_copy(
                src_ref=g_refs[w].at[pl.ds(r, rows)], dst_ref=land.at[0, pl.ds(s, rows)], send_sem=send_ref.at[_flight(0, k)],
                recv_sem=recv_ref.at[_flight(0, k)], device_id=_coords(dev), device_id_type=MESH)
                for w, r, rows, s in _g_in_pieces(dev)]
            return cps + [pltpu.make_async_remote_copy(
                src_ref=cv_ref.at[0], dst_ref=land_cv.at[0], send_sem=send_ref.at[_flight(1, k)],
                recv_sem=recv_ref.at[_flight(1, k)], device_id=_coords(dev), device_id_type=MESH)]

        for dev in range(N_DEV):
            @pl.when(me != dev)
            def _(dev=dev):
                for cp in copies(dev, me ^ dev):
                    cp.wait_send()

            @pl.when(me == dev)
            def _(dev=dev):
                for k in range(1, N_DEV):
                    for cp in copies(dev, k):
                        cp.wait_recv()

    ops = list(srcs) + list(lands)
    outs = pl.pallas_call(
        body, name=name, in_specs=[HBM] * len(ops) + [SEM, SEM, ANY], out_specs=[HBM] * len(ops),
        out_shape=[pltpu.HBM(a.shape, a.dtype) for a in ops],
        input_output_aliases={i: i for i in range(len(ops))},
        compiler_params=pltpu.CompilerParams(has_side_effects=EFFECT),
    )(*ops, send_sems, recv_sems, after)
    return outs[:n], outs[n:]


def _with_own(landed, srcs, me):
    return [lax.dynamic_update_index_in_dim(l, o, me, 0) for l, o in zip(landed, srcs)]


def _adam_update(g, w, m, v):
    c1 = 1.0 - ADAM_B1 ** ADAM_STEP
    c2 = 1.0 - ADAM_B2 ** ADAM_STEP
    nm = ADAM_B1 * m + (1.0 - ADAM_B1) * g
    nv = ADAM_B2 * v + (1.0 - ADAM_B2) * (g * g)
    return -ADAM_LR * ((nm / c1) / (jnp.sqrt(nv / c2) + ADAM_EPS) + ADAM_WD * w), nm, nv


def _adamw(landed, sent, me, w, m, v, *, name, tr=None, tc=None):
    R, C = w.shape
    tr = R if tr is None else tr
    tc = C if tc is None else tc
    assert R % tr == 0 and C % tc == 0

    def body(me_ref, own_ref, p_ref, w_ref, m_ref, v_ref, g_ref, d_ref, nm_ref, nv_ref, token_ref):
        token_ref[...] = jnp.zeros_like(token_ref)
        g = own_ref[...].astype(F32)
        for s in range(N_DEV):
            g = g + jnp.where(me_ref[1] == s, 0.0, p_ref[s].astype(F32))
        delta, nm, nv = _adam_update(g, w_ref[...], m_ref[...], v_ref[...])
        g_ref[...] = g
        nm_ref[...] = nm
        nv_ref[...] = nv
        d_ref[...] = delta

    blk = pl.BlockSpec((tr, tc), lambda i, j, me_ref: (i, j))
    return pl.pallas_call(
        body, name=name,
        grid_spec=pltpu.PrefetchScalarGridSpec(
            num_scalar_prefetch=1, grid=(R // tr, C // tc),
            in_specs=[pl.BlockSpec((None, tr, tc), lambda i, j, me_ref: (me_ref[0], i, j)),
                      pl.BlockSpec((N_DEV, tr, tc), lambda i, j, me_ref: (0, i, j)), blk, blk, blk],
            out_specs=[blk] * 4 + [pl.BlockSpec((8, 128), lambda i, j, me_ref: (0, 0))]),
        out_shape=[_sds((R, C))] * 4 + [_sds((8, 128))],
        compiler_params=_params(("arbitrary", "arbitrary")),
    )(me, sent, landed, w, m, v)


def _adamw_rowwise(landed, me, w, m, v, *, name, tr=128):
    C = landed.shape[2]
    R, q, extra = _IN_ROWS, C // 128, _ROW_TILE
    assert tr % extra == 0 and (pl.cdiv(R, tr) * tr + extra) <= landed.shape[1] and N_DEV - 1 + _GAP < extra

    def body(me_ref, a_ref, b_ref, w_ref, m_ref, v_ref, g_ref, d_ref, nm_ref, nv_ref, g_scr):
        i = pl.program_id(0)
        total = lambda ref: functools.reduce(lambda x, y: x + y, [ref[s].astype(F32) for s in range(N_DEV)])
        slab = jnp.concatenate([total(a_ref), total(b_ref)], axis=0)
        for dev in range(N_DEV):
            @pl.when(me_ref[0] == dev)
            def _(dev=dev):
                lo, hi = slab[dev:dev + tr], slab[dev + _GAP:dev + _GAP + tr]
                if _IN_ROWS * (dev + 1) <= _O3:
                    g_scr[...] = lo
                elif _IN_ROWS * dev >= _O3:
                    g_scr[...] = hi
                else:
                    r = _IN_ROWS * dev + tr * i + lax.broadcasted_iota(jnp.int32, (tr, 1), 0)
                    g_scr[...] = jnp.where(r < _O3, lo, hi)
        g = g_scr[...]
        for s in range(q):
            rows = pl.ds(s, tr, stride=q)
            gs = g[:, 128 * s:128 * (s + 1)]
            delta, nm, nv = _adam_update(gs, w_ref[rows, :], m_ref[rows, :], v_ref[rows, :])
            g_ref[rows, :] = gs
            nm_ref[rows, :] = nm
            nv_ref[rows, :] = nv
            d_ref[rows, :] = delta

    blk = pl.BlockSpec((tr * q, 128), lambda i, me_ref: (i, 0))
    return pl.pallas_call(
        body, name=name,
        grid_spec=pltpu.PrefetchScalarGridSpec(
            num_scalar_prefetch=1, grid=(pl.cdiv(R, tr),),
            in_specs=[pl.BlockSpec((N_DEV, tr, C), lambda i, me_ref: (0, i, 0)),
                      pl.BlockSpec((N_DEV, extra, C), lambda i, me_ref: (0, (tr // extra) * (i + 1), 0)), blk, blk, blk],
            out_specs=[blk] * 4, scratch_shapes=[pltpu.VMEM((tr, C), F32)]),
        out_shape=[_sds((R * q, 128))] * 4,
        compiler_params=_params(("arbitrary",)),
    )(me, landed, landed, w, m, v)


_SMALL_ROWS = 8
_SMALL_SLOTS = ((0, 0, D_MODEL), (1, 0, D_MODEL), (2, 0, D_MODEL), (3, 0, GDN_DIM), (3, GDN_DIM, GDN_HEADS),
                (3, GDN_DIM + GDN_HEADS, GDN_HEADS))
_LOSS_LANE = 2 * GDN_DIM


def _pack_small(norm1, norm2, final, gnw, a_log, dt_bias, loss):
    row3 = jnp.concatenate([gnw, a_log, dt_bias, jnp.zeros((1, 128 - 2 * GDN_HEADS), F32), loss,
                            jnp.zeros((1, D_MODEL - 3 * 128), F32)], axis=1)
    return jnp.concatenate([norm1, norm2, final, row3, jnp.zeros((_SMALL_ROWS - 4, D_MODEL), F32)], axis=0)


def _adamw_small(packs, ws, ms, vs, *, name):
    n = len(ws)

    def body(p_ref, *refs):
        w_refs, m_refs, v_refs = refs[:n], refs[n:2 * n], refs[2 * n:3 * n]
        outs = refs[3 * n:]
        g_all = p_ref[0]
        for s in range(1, N_DEV):
            g_all = g_all + p_ref[s]
        for i, (row, lane, width) in enumerate(_SMALL_SLOTS):
            g = g_all[row:row + 1, lane:lane + width]
            delta, nm, nv = _adam_update(g, w_refs[i][...], m_refs[i][...], v_refs[i][...])
            for o_ref, val in zip(outs[4 * i:4 * i + 4], (g, delta, nm, nv)):
                o_ref[...] = val
        outs[-1][...] = g_all[3:4, _LOSS_LANE:_LOSS_LANE + 128]

    vm = pl.BlockSpec(memory_space=pltpu.VMEM)
    outs = pl.pallas_call(
        body, name=name, in_specs=[vm] * (1 + 3 * n), out_specs=[vm] * (4 * n + 1),
        out_shape=[_sds(w.shape) for w in ws for _ in range(4)] + [_sds((1, 128))],
    )(packs, *ws, *ms, *vs)
    return [outs[4 * i:4 * i + 4] for i in range(n)], outs[-1]


def _slabs_by_cols(g):
    r = g.shape[0]
    return g.reshape(r, N_DEV, -1).transpose(1, 0, 2)


def _cols_from_slabs(s):
    return s.transpose(1, 0, 2).reshape(s.shape[1], -1)


def kernel(x, norm1_w, w_in, conv_qkv_w, a_log, dt_bias, gdn_norm_w, w_out, norm2_w, w_up, ffn_conv_w, w_down, final_norm_w, loss_target, m_norm1_w, m_w_in, m_conv_qkv_w, m_a_log, m_dt_bias, m_gdn_norm_w, m_w_out, m_norm2_w, m_w_up, m_ffn_conv_w, m_w_down, m_final_norm_w, v_norm1_w, v_w_in, v_conv_qkv_w, v_a_log, v_dt_bias, v_gdn_norm_w, v_w_out, v_norm2_w, v_w_up, v_ffn_conv_w, v_w_down, v_final_norm_w):
    me = _slot(_position())
    me1 = jnp.reshape(me, (1,)).astype(jnp.int32)
    t_in = lambda a: a[0].T
    rows = lambda a: a.reshape(D_MODEL // 128, 128, -1).transpose(2, 0, 1).reshape(-1, 128)
    gw_in, g_conv_a, b_out, b_up, b_down = _all_gather(
        [_shifted_slab(rows(w_in), me1, name="shift_w_in"), conv_qkv_w[0]], casts=[w_out[0], t_in(w_up), w_down[0]],
        name="gather_w_in")
    late_src, _ = lax.optimization_barrier(([b_out, b_up, b_down, ffn_conv_w[0]], gw_in))
    l_send, l_recv, l_srcs, l_lands, l_token = _exchange_start(late_src, name="weights_start", broadcast=True)

    def late_weights(after):
        srcs, landed = _exchange_wait(l_send, l_recv, l_srcs, l_lands, after, name="weights_wait", broadcast=True)
        gw_out, gw_up, gw_down, g_conv_f = _with_own(landed, srcs, me)
        return gw_out.reshape(D_MODEL, D_MODEL), gw_up, g_conv_f, gw_down.reshape(D_FF, D_MODEL)

    flights = {}

    def emit(group, **grads):
        paired = ()
        if group == "in":
            *flight, token = _exchange_start_in([grads["w_a"], grads["w_z"], grads["w_b"]], _slabs_by_cols(grads["conv_a"]),
                                                name="grads_start_in")
            flights[group] = flight
            return (token,)
        if group == "ffn":
            slabs = dict(w_down=grads["w_down"].reshape(N_DEV, -1, D_MODEL), w_up=grads["w_up"], conv_f=grads["conv_f"])
            paired = (1, 2)
        else:
            slabs = {k: v.reshape(N_DEV, -1, D_MODEL) for k, v in grads.items()}
        names = list(slabs)
        *flight, token = _exchange_start([slabs[k] for k in names], paired=paired, name="grads_start_" + group)
        flights[group] = (names, flight)
        return (token,)

    loss, grad_x, g = _local_step(
        x[0], loss_target[0], norm1_w, gw_in, _cols_from_slabs(g_conv_a), a_log, dt_bias,
        gdn_norm_w, norm2_w, final_norm_w[None], late_weights, emit, start_after=(l_token,))
    got = {}

    def collect(group, after):
        names, (send_sems, recv_sems, srcs, lands) = flights[group]
        srcs, landed = _exchange_wait(send_sems, recv_sems, srcs, lands, after, name="grads_wait_" + group)
        got.update(zip(names, zip(landed, srcs)))

    def update(key, w, m, v, paired=False, **tiles):
        where = jnp.concatenate([_pair_slot(me1) if paired else me1, me1])
        return _adamw(*got[key], where, w, m, v, name="adamw_" + key, **tiles)

    in_sems, in_srcs, in_lands = flights["in"][:2], flights["in"][2], flights["in"][3]
    own_land, own_token = _own_pieces(in_srcs[:3], in_lands[0], grad_x, name="grads_own_in")
    collect("ffn", own_token)
    collect("out", own_token)
    *o_out, t1 = update("w_out", w_out[0], m_w_out[0], v_w_out[0])
    *o_up, t2 = update("w_up", t_in(w_up), t_in(m_w_up), t_in(v_w_up), paired=True, tr=176)
    o_up = [o.T for o in o_up]
    *o_down, t3 = update("w_down", w_down[0], m_w_down[0], v_w_down[0], tr=176)
    *o_cf, t4 = update("conv_f", ffn_conv_w[0], m_ffn_conv_w[0], v_ffn_conv_w[0], paired=True)
    pack = _pack_small(g["norm1"], g["norm2"], g["final"], g["gnw"], g["small"][:, 0:GDN_HEADS],
                       g["small"][:, GDN_HEADS:2 * GDN_HEADS], loss)
    small_all = _gather_direct(pack, after=(t1, t2, t3, t4), name="gather_small")
    srcs, (g_land, conv_land) = _exchange_wait_in(*in_sems, in_srcs, [own_land, in_lands[1]], small_all, name="grads_wait_in")
    got["conv_a"] = (conv_land, srcs[-1])
    o_in = [o.reshape(-1, D_MODEL // 128, 128).transpose(1, 2, 0).reshape(D_MODEL, -1) for o in _adamw_rowwise(
        g_land, me1, rows(w_in), rows(m_w_in), rows(v_w_in), name="adamw_w_in")]
    o_ca = update("conv_a", conv_qkv_w[0], m_conv_qkv_w[0], v_conv_qkv_w[0])
    (o_n1, o_n2, o_fin, o_gn, o_al, o_dt), total = _adamw_small(
        small_all, (norm1_w, norm2_w, final_norm_w[None], gdn_norm_w, a_log, dt_bias),
        (m_norm1_w, m_norm2_w, m_final_norm_w[None], m_gdn_norm_w, m_a_log, m_dt_bias),
        (v_norm1_w, v_norm2_w, v_final_norm_w[None], v_gdn_norm_w, v_a_log, v_dt_bias), name="adamw_small")
    outs = [total[0, 0], grad_x[None]]
    for k in range(4):
        outs += [o_n1[k], o_in[k][None], o_ca[k][None], o_al[k], o_dt[k], o_gn[k], o_out[k][None], o_n2[k], o_up[k][None],
                 o_cf[k][None], o_down[k][None], o_fin[k][0]]
    return tuple(outs)
```

```python
import functools

import jax
import jax.numpy as jnp
from jax import lax
from jax.experimental import pallas as pl
from jax.experimental.pallas import tpu as pltpu

F32 = jnp.float32
BF16 = jnp.bfloat16

N_DEV = 8
D_MODEL = 1024
GDN_HEADS = 4
GDN_DIM = 128
GDN_WIDTH = GDN_HEADS * GDN_DIM
GDN_CONV = 4
CHUNK = 64
CHUNKS_PER_STEP = 4
DIL_HEADS = 8
DIL_DIM = 64
DIL_WIDTH = DIL_HEADS * DIL_DIM
DIL_PAIRS = DIL_HEADS // 2
DILATIONS = (1, 4, 16)
BAND = 128
D_FF = 2816
FFN_CONV = 3
EPS = 1e-6
A_COLS = 3 * GDN_WIDTH + 128
HALO = 8

ADAM_LR = 0.001
ADAM_B1 = 0.9
ADAM_B2 = 0.999
ADAM_EPS = 1e-08
ADAM_WD = 0.01
ADAM_STEP = 10

VMEM_LIMIT_BYTES = 56 * 1024 * 1024
NEG_BIG = -1e30


def _params(sem=None):
    return pltpu.CompilerParams(dimension_semantics=sem, vmem_limit_bytes=VMEM_LIMIT_BYTES)


def _sds(shape, dtype=F32):
    return jax.ShapeDtypeStruct(shape, dtype)


def _bdot(a, b):
    return jnp.dot(a.astype(BF16), b.astype(BF16), preferred_element_type=F32)


def _bdot_nt(a, b):
    return lax.dot_general(a.astype(BF16), b.astype(BF16), (((1,), (1,)), ((), ())), preferred_element_type=F32)


def _bdot_tn(a, b):
    return lax.dot_general(a.astype(BF16), b.astype(BF16), (((0,), (0,)), ((), ())), preferred_element_type=F32)


def _split(a):
    hi = a.astype(BF16)
    lo = (a - hi.astype(F32)).astype(BF16)
    return hi, lo


def _dot3(a, b, dims):
    ah, al = _split(a)
    bh, bl = _split(b)
    d = functools.partial(lax.dot_general, dimension_numbers=(dims, ((), ())), preferred_element_type=F32)
    return d(ah, bh) + (d(al, bh) + d(ah, bl))


def _exact_tri_dot(tri, g):
    g1 = g.astype(BF16)
    r1 = g - g1.astype(F32)
    g2 = r1.astype(BF16)
    g3 = (r1 - g2.astype(F32)).astype(BF16)
    t = tri.astype(BF16)
    d = functools.partial(jnp.dot, preferred_element_type=F32)
    return d(t, g1) + (d(t, g2) + d(t, g3))


def _sigmoid(x):
    return 1.0 / (1.0 + jnp.exp(-x))


def _dsilu(x, sg):
    return sg * (1.0 + x * (1.0 - sg))


def _rms_bwd_rows(dh, x, w):
    r = lax.rsqrt(jnp.mean(x * x, axis=-1, keepdims=True) + EPS)
    xh = x * r
    gw = dh * w
    return r * (gw - xh * jnp.mean(gw * xh, axis=-1, keepdims=True)), jnp.sum(dh * xh, axis=0, keepdims=True)


def _mm(a, b, *, name, ta=False, tb=False, res=None, norm_bwd=None, after=(), out_dtype=F32, tm=512, tn=512, tk=512):
    if ta:
        K, M = a.shape
    else:
        M, K = a.shape
    if tb:
        N, Kb = b.shape
    else:
        Kb, N = b.shape
    assert K == Kb, (a.shape, b.shape)
    tm, tn, tk = min(tm, M), min(tn, N), min(tk, K)
    assert M % tm == 0 and N % tn == 0 and K % tk == 0, (name, M, N, K, tm, tn, tk)
    nk = K // tk
    dims = (((0 if ta else 1,), (1 if tb else 0,)), ((), ()))
    has_res = res is not None
    has_norm = norm_bwd is not None
    assert not has_norm or tn == N

    def body(*refs):
        a_ref, b_ref = refs[:2]
        r_ref = refs[2] if has_res else None
        if has_norm:
            x_ref, w_ref, skip_ref = refs[2 + has_res:5 + has_res]
            o_ref, dw_ref, acc_ref = refs[-3:]
        else:
            o_ref, acc_ref = refs[-2:]
        i, k = pl.program_id(0), pl.program_id(2)
        part = lax.dot_general(a_ref[...].astype(BF16), b_ref[...].astype(BF16), dims, preferred_element_type=F32)

        @pl.when(k == 0)
        def _():
            acc_ref[...] = part

        @pl.when(k > 0)
        def _():
            acc_ref[...] += part

        @pl.when(k == nk - 1)
        def _():
            r = acc_ref[...]
            if has_res:
                r = r + r_ref[...]
            if has_norm:
                dx, dw = _rms_bwd_rows(r, x_ref[...], w_ref[...])
                o_ref[...] = skip_ref[...] + dx

                @pl.when(i == 0)
                def _():
                    dw_ref[...] = dw

                @pl.when(i > 0)
                def _():
                    dw_ref[...] += dw
            else:
                o_ref[...] = r.astype(out_dtype)

    a_spec = pl.BlockSpec((tk, tm), lambda i, j, k: (k, i)) if ta else pl.BlockSpec((tm, tk), lambda i, j, k: (i, k))
    b_spec = pl.BlockSpec((tn, tk), lambda i, j, k: (j, k)) if tb else pl.BlockSpec((tk, tn), lambda i, j, k: (k, j))
    o_spec = pl.BlockSpec((tm, tn), lambda i, j, k: (i, j))
    one = pl.BlockSpec((1, tn), lambda i, j, k: (0, 0))
    in_specs = [a_spec, b_spec] + [o_spec] * has_res + ([o_spec, one, o_spec] if has_norm else []) + [ANY] * len(after)
    args = (a, b) + ((res,) if has_res else ()) + (tuple(norm_bwd) if has_norm else ()) + tuple(after)
    return pl.pallas_call(
        body, name=name, grid=(M // tm, N // tn, nk), in_specs=in_specs,
        out_specs=[o_spec, one] if has_norm else o_spec,
        out_shape=[_sds((M, N)), _sds((1, N))] if has_norm else _sds((M, N), out_dtype),
        scratch_shapes=[pltpu.VMEM((tm, tn), F32)],
        compiler_params=_params(("arbitrary" if has_norm else "parallel", "parallel", "arbitrary")),
    )(*args)


def _in_proj(x, norm_w, w_land, *, name, after=(), tm=512):
    S, D = x.shape

    def body(x_ref, nw_ref, land_ref, *rest):
        h_ref, pa_ref, pz_ref, pb_ref, *scratch = rest[len(after):]

        @pl.when(pl.program_id(0) == 0)
        def _():
            _fetch_w_in(land_ref, *scratch)

        xv = x_ref[...]
        r = lax.rsqrt(jnp.mean(xv * xv, axis=-1, keepdims=True) + EPS)
        h = (xv * r * nw_ref[...]).astype(BF16)
        h_ref[...] = h
        for w_ref, p_ref in zip(scratch[:3], (pa_ref, pz_ref, pb_ref)):
            p_ref[...] = lax.dot_general(h, w_ref[...], (((1,), (1,)), ((), ())), preferred_element_type=F32)

    row = lambda n: pl.BlockSpec((tm, n), lambda i: (i, 0))
    full = lambda a: pl.BlockSpec(a.shape, lambda i: (0, 0))
    return pl.pallas_call(
        body, name=name, grid=(S // tm,), in_specs=[row(D), full(norm_w), ANY] + [ANY] * len(after),
        out_specs=[row(D)] + [row(n) for n in _W_IN_ROWS],
        out_shape=[_sds((S, D), BF16)] + [_sds((S, n)) for n in _W_IN_ROWS],
        scratch_shapes=_w_in_scratch(D), compiler_params=_params(("arbitrary",)),
    )(x, norm_w, w_land, *after)


def _in_proj_dx(ds, w_land, x, norm_w, skip, *, name, after=(), tm=512):
    S, D = x.shape
    n = len(ds)

    def body(*refs):
        d_refs, land_ref = refs[:n], refs[n]
        x_ref, nw_ref, skip_ref = refs[n + 1:n + 4]
        o_ref, dw_ref, *scratch = refs[n + 4 + len(after):]
        w_refs = scratch[:n]
        i = pl.program_id(0)

        @pl.when(i == 0)
        def _():
            _fetch_w_in(land_ref, *scratch)

        dh = jnp.dot(d_refs[0][...], w_refs[0][...], preferred_element_type=F32)
        for d_ref, w_ref in zip(d_refs[1:], w_refs[1:]):
            dh = dh + jnp.dot(d_ref[...], w_ref[...], preferred_element_type=F32)
        dx, dw = _rms_bwd_rows(dh, x_ref[...], nw_ref[...])
        o_ref[...] = skip_ref[...] + dx

        @pl.when(i == 0)
        def _():
            dw_ref[...] = dw

        @pl.when(i > 0)
        def _():
            dw_ref[...] += dw

    row = lambda c: pl.BlockSpec((tm, c), lambda i: (i, 0))
    full = lambda a: pl.BlockSpec(a.shape, lambda i: (0, 0))
    return pl.pallas_call(
        body, name=name, grid=(S // tm,),
        in_specs=[row(d.shape[1]) for d in ds] + [ANY, row(D), full(norm_w), row(D)] + [ANY] * len(after),
        out_specs=[row(D), pl.BlockSpec((1, D), lambda i: (0, 0))], out_shape=[_sds((S, D)), _sds((1, D))],
        scratch_shapes=_w_in_scratch(D), compiler_params=_params(("arbitrary",)),
    )(*ds, w_land, x, norm_w, skip, *after)


def _out_proj_norm(a, w, x, norm_w, *, name, tm=512):
    S, D = x.shape

    def body(a_ref, w_ref, x_ref, nw_ref, x1_ref, h_ref):
        x1 = x_ref[...] + jnp.dot(a_ref[...], w_ref[...], preferred_element_type=F32)
        x1_ref[...] = x1
        r = lax.rsqrt(jnp.mean(x1 * x1, axis=-1, keepdims=True) + EPS)
        h_ref[...] = (x1 * r * nw_ref[...]).astype(BF16)

    row = pl.BlockSpec((tm, D), lambda i: (i, 0))
    return pl.pallas_call(
        body, name=name, grid=(S // tm,),
        in_specs=[pl.BlockSpec((tm, a.shape[1]), lambda i: (i, 0)), pl.BlockSpec(w.shape, lambda i: (0, 0)), row,
                  pl.BlockSpec((1, D), lambda i: (0, 0))],
        out_specs=[row, row], out_shape=[_sds((S, D)), _sds((S, D), BF16)], compiler_params=_params(("parallel",)),
    )(a, w, x, norm_w)


def _shifted(x, start, n):
    aligned = -(-start // HALO) * HALO
    assert aligned + n <= x.shape[0], (start, n, x.shape)
    return (x if aligned == start else pltpu.roll(x, aligned - start, axis=0))[aligned:aligned + n]


def _conv_rows(prev, cur, w, taps):
    n = cur.shape[0]
    xs = jnp.concatenate([prev, cur], axis=0)
    base = HALO - (taps - 1)
    out = _shifted(xs, base, n) * w[0:1]
    for i in range(1, taps):
        out = out + _shifted(xs, base + i, n) * w[i:i + 1]
    return out


def _conv_rows_bwd(cur_d, next_d, prev_x, cur_x, w, taps):
    n = cur_d.shape[0]
    ds = jnp.concatenate([cur_d, next_d], axis=0)
    dx = _shifted(ds, taps - 1, n) * w[0:1]
    for i in range(1, taps):
        dx = dx + _shifted(ds, taps - 1 - i, n) * w[i:i + 1]
    xs = jnp.concatenate([prev_x, cur_x], axis=0)
    base = HALO - (taps - 1)
    dws = [jnp.sum(cur_d * _shifted(xs, base + i, n), axis=0, keepdims=True) for i in range(taps)]
    return dx, jnp.concatenate(dws, axis=0)


def _halo_specs(tm, width, col, nblk):
    per = tm // HALO
    prev = pl.BlockSpec((HALO, width), lambda i, *_: (jnp.maximum(i * per - 1, 0), col))
    nxt = pl.BlockSpec((HALO, width), lambda i, *_: (jnp.minimum((i + 1) * per, nblk * per - 1), col))
    return prev, nxt


def _softplus(x):
    return jnp.maximum(x, 0.0) + jnp.log1p(jnp.exp(-jnp.abs(x)))


def _chunk_tri(tm, upper=False):
    r = lax.broadcasted_iota(jnp.int32, (tm, tm), 0)
    c = lax.broadcasted_iota(jnp.int32, (tm, tm), 1)
    same = lax.div(r, CHUNK) == lax.div(c, CHUNK)
    order = (c >= r) if upper else (c <= r)
    return jnp.where(same & order, 1.0, 0.0)


def _gdn_prep_fwd(proj_a, conv_w, a_log, dt_bias, *, name, tm=256):
    S = proj_a.shape[0]
    nblk = S // tm
    W3 = 3 * GDN_WIDTH

    def body(cur_ref, prev_ref, ba_ref, cw_ref, al_ref, dt_ref, qn_ref, kn_ref, v_ref, gcb_ref, bb_ref):
        i = pl.program_id(0)
        prev = jnp.where(i > 0, prev_ref[...], 0.0)
        c = _conv_rows(prev, cur_ref[...], cw_ref[...], GDN_CONV)
        a = c * _sigmoid(c)
        ba = ba_ref[...]
        lane = lax.broadcasted_iota(jnp.int32, (tm, 128), 1)
        g4 = jnp.zeros((tm, 128), F32)
        for h in range(GDN_HEADS):
            sl = slice(GDN_DIM * h, GDN_DIM * (h + 1))
            qh = a[:, GDN_DIM * h:GDN_DIM * (h + 1)]
            kh = a[:, GDN_WIDTH + GDN_DIM * h:GDN_WIDTH + GDN_DIM * (h + 1)]
            qn_ref[:, sl] = qh * (lax.rsqrt(jnp.sum(qh * qh, axis=-1, keepdims=True) + EPS) * (GDN_DIM ** -0.5))
            kn_ref[:, sl] = kh * lax.rsqrt(jnp.sum(kh * kh, axis=-1, keepdims=True) + EPS)
            beta = _sigmoid(ba[:, h:h + 1])
            bb_ref[:, sl] = jnp.broadcast_to(beta, (tm, GDN_DIM))
            g = -jnp.exp(al_ref[0:1, h:h + 1]) * _softplus(ba[:, GDN_HEADS + h:GDN_HEADS + h + 1] + dt_ref[0:1, h:h + 1])
            g4 = jnp.where(lane == h, g, g4)
        v_ref[...] = a[:, 2 * GDN_WIDTH:]
        gc = _exact_tri_dot(_chunk_tri(tm), g4)
        for h in range(GDN_HEADS):
            gcb_ref[:, GDN_DIM * h:GDN_DIM * (h + 1)] = jnp.broadcast_to(gc[:, h:h + 1], (tm, GDN_DIM))

    prev_spec, _ = _halo_specs(tm, W3, 0, nblk)
    row = pl.BlockSpec((tm, GDN_WIDTH), lambda i: (i, 0))
    small = lambda a: pl.BlockSpec(a.shape, lambda i: (0, 0))
    return pl.pallas_call(
        body, name=name, grid=(nblk,),
        in_specs=[pl.BlockSpec((tm, W3), lambda i: (i, 0)), prev_spec,
                  pl.BlockSpec((tm, 128), lambda i: (i, W3 // 128)), small(conv_w), small(a_log), small(dt_bias)],
        out_specs=[row] * 5, out_shape=[_sds((S, GDN_WIDTH))] * 5, compiler_params=_params(("parallel",)),
    )(proj_a, proj_a, proj_a, conv_w, a_log, dt_bias)


GDN_STACK = GDN_HEADS * CHUNK


def _stack(ref, rows):
    return jnp.concatenate([ref[rows, GDN_DIM * h:GDN_DIM * (h + 1)] for h in range(GDN_HEADS)], axis=0)


def _unstack_to(ref, rows, x):
    for h in range(GDN_HEADS):
        ref[rows, GDN_DIM * h:GDN_DIM * (h + 1)] = x[CHUNK * h:CHUNK * (h + 1)].astype(ref.dtype)


def _stack_masks():
    r = lax.broadcasted_iota(jnp.int32, (GDN_STACK, GDN_STACK), 0)
    c = lax.broadcasted_iota(jnp.int32, (GDN_STACK, GDN_STACK), 1)
    same = (r & -CHUNK) == (c & -CHUNK)
    return same & (r >= c), same & (r > c), r == c


def _stack_decay(gs, bs, incl):
    g2 = jnp.concatenate([gs, gs], axis=1)
    diff = g2 - g2.T
    dec = jnp.where(incl, jnp.exp(jnp.where(incl, diff, 0.0)), 0.0)
    return dec, jnp.concatenate([bs, bs], axis=1).T


def _head_mask():
    r = lax.broadcasted_iota(jnp.int32, (GDN_STACK, GDN_WIDTH), 0)
    c = lax.broadcasted_iota(jnp.int32, (GDN_STACK, GDN_WIDTH), 1)
    return (r & -CHUNK) * (GDN_DIM // CHUNK) == (c & -GDN_DIM)


def _head_spread(x):
    return jnp.where(_head_mask(), jnp.concatenate([x] * GDN_HEADS, axis=1), 0.0)


def _head_diag(x):
    xm = jnp.where(_head_mask(), x, 0.0)
    out = xm[:, 0:GDN_DIM]
    for h in range(1, GDN_HEADS):
        out = out + xm[:, GDN_DIM * h:GDN_DIM * (h + 1)]
    return out


def _last_rows(gs, n):
    return jnp.concatenate([jnp.broadcast_to(gs[CHUNK * (h + 1) - 1:CHUNK * (h + 1)], (n, GDN_DIM)) for h in range(GDN_HEADS)], axis=0)


def _gdn_chunk_fwd(qn, kn, v, gcb, bb, *, name):
    S = qn.shape[0]

    def body(qn_ref, kn_ref, v_ref, gcb_ref, bb_ref, uv_ref, wk_ref, at_ref, t_ref, wkb_ref, qdb_ref, keb_ref):
        incl, strict, diag = _stack_masks()
        for c in range(CHUNKS_PER_STEP):
            rows = slice(CHUNK * c, CHUNK * (c + 1))
            srows = slice(GDN_STACK * c, GDN_STACK * (c + 1))
            q, k, vv, gs, bs = [_stack(r, rows) for r in (qn_ref, kn_ref, v_ref, gcb_ref, bb_ref)]
            dec, bt = _stack_decay(gs, bs, incl)
            p = -jnp.where(strict, dec * _bdot_nt(k, k) * bt, 0.0)
            t = jnp.where(diag, 1.0, 0.0) + p
            for _ in range(5):
                p = _bdot(p, p)
                t = t + _bdot(t, p)
            sol = _dot3(t, jnp.concatenate([vv, jnp.exp(gs) * k], axis=1), ((1,), (0,)))
            _unstack_to(uv_ref, rows, sol[:, :GDN_DIM])
            _unstack_to(wk_ref, rows, sol[:, GDN_DIM:])
            at_ref[srows, :] = dec * _bdot_nt(q, k) * bt
            t_ref[srows, :] = t
            wkb_ref[srows, :] = _head_spread(sol[:, GDN_DIM:]).astype(BF16)
            qdb_ref[srows, :] = _head_spread(q * jnp.exp(gs)).astype(BF16)
            keb_ref[srows, :] = _head_spread(k * jnp.exp(_last_rows(gs, CHUNK) - gs) * bs).astype(BF16)

    step = CHUNKS_PER_STEP * CHUNK
    row = pl.BlockSpec((step, GDN_WIDTH), lambda n: (n, 0))
    sq = pl.BlockSpec((CHUNKS_PER_STEP * GDN_STACK, GDN_STACK), lambda n: (n, 0))
    wide = pl.BlockSpec((CHUNKS_PER_STEP * GDN_STACK, GDN_WIDTH), lambda n: (n, 0))
    nsq = S // CHUNK * GDN_STACK
    return pl.pallas_call(
        body, name=name, grid=(S // step,), in_specs=[row] * 5, out_specs=[row, row, sq, sq, wide, wide, wide],
        out_shape=[_sds((S, GDN_WIDTH)), _sds((S, GDN_WIDTH)), _sds((nsq, GDN_STACK)), _sds((nsq, GDN_STACK))]
        + [_sds((nsq, GDN_WIDTH), BF16)] * 3,
        compiler_params=_params(("parallel",)),
    )(qn, kn, v, gcb, bb)


SCAN_CHUNKS = 8


def _gdn_scan_fwd(uv, at, wkb, qdb, keb, gcb, proj_z, gnw, *, name):
    S = uv.shape[0]
    nc = S // CHUNK

    def body(uv_ref, at_ref, wkb_ref, qdb_ref, keb_ref, gcb_ref, z_ref, gnw_ref, o_ref, u_ref, sp_ref, oa_ref, st_ref):
        n = pl.program_id(0)

        @pl.when(n == 0)
        def _():
            st_ref[...] = jnp.zeros_like(st_ref)

        for c in range(SCAN_CHUNKS):
            rows = slice(CHUNK * c, CHUNK * (c + 1))
            srows = slice(GDN_STACK * c, GDN_STACK * (c + 1))
            st = st_ref[...]
            sp_ref[GDN_WIDTH * c:GDN_WIDTH * (c + 1), :] = st
            uv, gs, z = [_stack(r, rows) for r in (uv_ref, gcb_ref, z_ref)]
            u = uv - _bdot(wkb_ref[srows, :], st)
            o = _bdot(qdb_ref[srows, :], st) + _bdot(at_ref[srows, :], u)
            st_ref[...] = jnp.exp(_last_rows(gs, GDN_DIM)) * st + _bdot_tn(keb_ref[srows, :], u)
            _unstack_to(u_ref, rows, u)
            _unstack_to(o_ref, rows, o)
            r = lax.rsqrt(jnp.mean(o * o, axis=-1, keepdims=True) + EPS)
            oa = o * r * gnw_ref[...] * (z * _sigmoid(z))
            oa_ref[rows, :] = jnp.concatenate([oa[CHUNK * h:CHUNK * (h + 1)] for h in range(GDN_HEADS)], axis=1).astype(BF16)

    row = pl.BlockSpec((SCAN_CHUNKS * CHUNK, GDN_WIDTH), lambda n: (n, 0))
    sq = pl.BlockSpec((SCAN_CHUNKS * GDN_STACK, GDN_STACK), lambda n: (n, 0))
    wide = pl.BlockSpec((SCAN_CHUNKS * GDN_STACK, GDN_WIDTH), lambda n: (n, 0))
    return pl.pallas_call(
        body, name=name, grid=(nc // SCAN_CHUNKS,),
        in_specs=[row, sq, wide, wide, wide, row, row, pl.BlockSpec((1, GDN_DIM), lambda n: (0, 0))],
        out_specs=[row, row, pl.BlockSpec((SCAN_CHUNKS * GDN_WIDTH, GDN_DIM), lambda n: (n, 0)), row],
        out_shape=[_sds((S, GDN_WIDTH)), _sds((S, GDN_WIDTH)), _sds((nc * GDN_WIDTH, GDN_DIM)), _sds((S, 2 * GDN_WIDTH), BF16)],
        scratch_shapes=[pltpu.VMEM((GDN_WIDTH, GDN_DIM), F32)],
        compiler_params=_params(("arbitrary",)),
    )(uv, at, wkb, qdb, keb, gcb, proj_z, gnw)


def _gdn_scan_bwd(d_oab, o, proj_z, gnw, sp, u, at, wkb, qdb, keb, gcb, *, name, after=()):
    S = o.shape[0]
    nc = S // CHUNK
    ns = nc // SCAN_CHUNKS

    def body(do_ref, o_ref, z_ref, gnw_ref, sp_ref, u_ref, at_ref, wkb_ref, qdb_ref, keb_ref, gcb_ref, *rest):
        dz_ref, dgn_ref, du_ref, dwk_ref, dat_ref, dqd_ref, dke_ref, dgl_ref, ds_ref = rest[len(after):]
        n = pl.program_id(0)

        @pl.when(n == 0)
        def _():
            ds_ref[...] = jnp.zeros_like(ds_ref)
            dgn_ref[...] = jnp.zeros_like(dgn_ref)

        gw = gnw_ref[...]
        for c in reversed(range(SCAN_CHUNKS)):
            rows = slice(CHUNK * c, CHUNK * (c + 1))
            srows = slice(GDN_STACK * c, GDN_STACK * (c + 1))
            d_oa, oo, z, uu, gs = [_stack(r, rows) for r in (do_ref, o_ref, z_ref, u_ref, gcb_ref)]
            sg = _sigmoid(z)
            r = lax.rsqrt(jnp.mean(oo * oo, axis=-1, keepdims=True) + EPS)
            xh = oo * r
            dy = d_oa * (z * sg)
            _unstack_to(dz_ref, rows, d_oa * (xh * gw) * _dsilu(z, sg))
            dgn_ref[...] += jnp.sum(dy * xh, axis=0, keepdims=True)
            dxh = dy * gw
            do = r * (dxh - xh * jnp.mean(dxh * xh, axis=-1, keepdims=True))

            st = sp_ref[GDN_WIDTH * c:GDN_WIDTH * (c + 1), :]
            dst = ds_ref[...]
            ge = jnp.exp(_last_rows(gs, GDN_DIM))
            _unstack_to(dqd_ref, rows, _head_diag(_bdot_nt(do, st)))
            dat_ref[srows, :] = _bdot_nt(do, uu)
            du = _bdot_tn(at_ref[srows, :], do) + _bdot(keb_ref[srows, :], dst)
            _unstack_to(dke_ref, rows, _head_diag(_bdot_nt(uu, dst)))
            prod = dst * st
            for h in range(GDN_HEADS):
                blk = prod[GDN_DIM * h:GDN_DIM * (h + 1)]
                dge = jnp.sum(jnp.sum(blk, axis=1, keepdims=True), axis=0, keepdims=True)
                dgl_ref[c, :, GDN_DIM * h:GDN_DIM * (h + 1)] = jnp.broadcast_to(dge * ge[GDN_DIM * h:GDN_DIM * h + 1], (8, GDN_DIM))
            ds_ref[...] = _bdot_tn(qdb_ref[srows, :], do) + ge * dst - _bdot_tn(wkb_ref[srows, :], du)
            _unstack_to(du_ref, rows, du)
            _unstack_to(dwk_ref, rows, -_head_diag(_bdot_nt(du, st)))

    rev = lambda n: (ns - 1 - n, 0)
    row = pl.BlockSpec((SCAN_CHUNKS * CHUNK, GDN_WIDTH), rev)
    sq = pl.BlockSpec((SCAN_CHUNKS * GDN_STACK, GDN_STACK), rev)
    wide = pl.BlockSpec((SCAN_CHUNKS * GDN_STACK, GDN_WIDTH), rev)
    one = pl.BlockSpec((1, GDN_DIM), lambda n: (0, 0))
    return pl.pallas_call(
        body, name=name, grid=(ns,),
        in_specs=[row, row, row, one, pl.BlockSpec((SCAN_CHUNKS * GDN_WIDTH, GDN_DIM), rev), row, sq, wide, wide, wide, row]
        + [ANY] * len(after),
        out_specs=[row, one, row, row, sq, row, row, pl.BlockSpec((SCAN_CHUNKS, 8, GDN_WIDTH), lambda n: (ns - 1 - n, 0, 0))],
        out_shape=[_sds((S, GDN_WIDTH), BF16), _sds((1, GDN_DIM)), _sds((S, GDN_WIDTH)), _sds((S, GDN_WIDTH)),
                   _sds((nc * GDN_STACK, GDN_STACK)), _sds((S, GDN_WIDTH)), _sds((S, GDN_WIDTH)), _sds((nc, 8, GDN_WIDTH))],
        scratch_shapes=[pltpu.VMEM((GDN_WIDTH, GDN_DIM), F32)],
        compiler_params=_params(("arbitrary",)),
    )(d_oab, o, proj_z, gnw, sp, u, at, wkb, qdb, keb, gcb, *after)


def _gdn_chunk_bwd(qn, kn, gcb, bb, tmat, uv, wk, du, dwk, dat, dqd, dke, dgl, *, name):
    S = qn.shape[0]

    def body(qn_ref, kn_ref, gcb_ref, bb_ref, t_ref, uv_ref, wk_ref, du_ref, dwk_ref, dat_ref, dqd_ref, dke_ref,
             dgl_ref, dq_ref, dk_ref, dv_ref, dg_ref, dbeta_ref):
        incl, strict, _ = _stack_masks()
        lane = lax.broadcasted_iota(jnp.int32, (CHUNK, 128), 1)
        rowi = lax.broadcasted_iota(jnp.int32, (CHUNK, 1), 0)
        rsum = lambda x: jnp.sum(x, axis=-1, keepdims=True)
        for c in range(CHUNKS_PER_STEP):
            rows = slice(CHUNK * c, CHUNK * (c + 1))
            srows = slice(GDN_STACK * c, GDN_STACK * (c + 1))
            q, k, gs, bs, uv, wk, du, dwk, dqd, dke = [
                _stack(r, rows) for r in (qn_ref, kn_ref, gcb_ref, bb_ref, uv_ref, wk_ref, du_ref, dwk_ref, dqd_ref, dke_ref)]
            dec, bt = _stack_decay(gs, bs, incl)
            kk = _bdot_nt(k, k)
            qk = _bdot_nt(q, k)
            d_rhs = _dot3(t_ref[srows, :], jnp.concatenate([du, dwk], axis=1), ((0,), (0,)))
            sol = jnp.concatenate([uv, wk], axis=1)
            d_l = jnp.where(strict, -_dot3(d_rhs, sol, ((1,), (1,))), 0.0)
            d_a = jnp.where(incl, dat_ref[srows, :], 0.0)
            gam = jnp.exp(gs)
            e = jnp.exp(_last_rows(gs, CHUNK) - gs)
            d_gk = d_rhs[:, GDN_DIM:]
            ml = d_l * dec * bt
            ma = d_a * dec * bt
            _unstack_to(dq_ref, rows, _bdot(ma, k) + dqd * gam)
            _unstack_to(dk_ref, rows, _bdot(ml + ml.T, k) + _bdot_tn(ma, q) + d_gk * gam + dke * (e * bs))
            _unstack_to(dv_ref, rows, d_rhs[:, :GDN_DIM])
            wb = d_l * dec * kk + d_a * dec * qk
            ew = wb * bt
            s_ke = rsum(dke * k * (e * bs))
            dbeta = rsum(wb.T) + rsum(dke * k * e)
            dgc = rsum(ew) - rsum(ew.T) + rsum(dqd * q * gam) + rsum(d_gk * k * gam) - s_ke
            dgc4 = jnp.zeros((CHUNK, 128), F32)
            db4 = jnp.zeros((CHUNK, 128), F32)
            for h in range(GDN_HEADS):
                hr = slice(CHUNK * h, CHUNK * (h + 1))
                tail = jnp.sum(s_ke[hr], axis=0, keepdims=True) + dgl_ref[c, 0:1, GDN_DIM * h:GDN_DIM * h + 1]
                dgc4 = jnp.where(lane == h, dgc[hr] + jnp.where(rowi == CHUNK - 1, tail, 0.0), dgc4)
                db4 = jnp.where(lane == h, dbeta[hr], db4)
            dg_ref[rows, :] = _exact_tri_dot(_chunk_tri(CHUNK, upper=True), dgc4)
            dbeta_ref[rows, :] = db4

    step = CHUNKS_PER_STEP * CHUNK
    row = pl.BlockSpec((step, GDN_WIDTH), lambda n: (n, 0))
    sq = pl.BlockSpec((CHUNKS_PER_STEP * GDN_STACK, GDN_STACK), lambda n: (n, 0))
    col = pl.BlockSpec((step, 128), lambda n: (n, 0))
    return pl.pallas_call(
        body, name=name, grid=(S // step,),
        in_specs=[row] * 4 + [sq, row, row, row, row, sq, row, row,
                              pl.BlockSpec((CHUNKS_PER_STEP, 8, GDN_WIDTH), lambda n: (n, 0, 0))],
        out_specs=[row, row, row, col, col],
        out_shape=[_sds((S, GDN_WIDTH))] * 3 + [_sds((S, 128))] * 2, compiler_params=_params(("parallel",)),
    )(qn, kn, gcb, bb, tmat, uv, wk, du, dwk, dat, dqd, dke, dgl)


def _gdn_prep_bwd(dqn, dkn, dv, dg, dbeta, proj_a, conv_w, a_log, dt_bias, *, name, tm=256):
    S = proj_a.shape[0]
    nblk = S // tm
    W3 = 3 * GDN_WIDTH

    def body(dqn_ref, dkn_ref, dv_ref, dg_ref, dbeta_ref, cur_ref, prev_ref, ba_ref, cw_ref, al_ref, dt_ref,
             dc_ref, dba_ref, sm_ref):
        i = pl.program_id(0)
        prev = jnp.where(i > 0, prev_ref[...], 0.0)
        c = _conv_rows(prev, cur_ref[...], cw_ref[...], GDN_CONV)
        sg = _sigmoid(c)
        a = c * sg
        dsl = _dsilu(c, sg)
        ba = ba_ref[...]
        lane = lax.broadcasted_iota(jnp.int32, (tm, 128), 1)
        lane1 = lax.broadcasted_iota(jnp.int32, (1, 128), 1)
        dba = jnp.zeros((tm, 128), F32)
        sm = jnp.zeros((1, 128), F32)
        for h in range(GDN_HEADS):
            sl = slice(GDN_DIM * h, GDN_DIM * (h + 1))
            ks = slice(GDN_WIDTH + GDN_DIM * h, GDN_WIDTH + GDN_DIM * (h + 1))
            qh, kh = a[:, sl], a[:, ks]
            rq = lax.rsqrt(jnp.sum(qh * qh, axis=-1, keepdims=True) + EPS)
            rk = lax.rsqrt(jnp.sum(kh * kh, axis=-1, keepdims=True) + EPS)
            qhat, khat = qh * rq, kh * rk
            dyq = dqn_ref[:, sl] * (GDN_DIM ** -0.5)
            dyk = dkn_ref[:, sl]
            dq = rq * (dyq - qhat * jnp.sum(dyq * qhat, axis=-1, keepdims=True))
            dk = rk * (dyk - khat * jnp.sum(dyk * khat, axis=-1, keepdims=True))
            dc_ref[:, sl] = dq * dsl[:, sl]
            dc_ref[:, ks] = dk * dsl[:, ks]
            beta = _sigmoid(ba[:, h:h + 1])
            db = dbeta_ref[:, h:h + 1] * beta * (1.0 - beta)
            aneg = -jnp.exp(al_ref[0:1, h:h + 1])
            xa = ba[:, GDN_HEADS + h:GDN_HEADS + h + 1] + dt_ref[0:1, h:h + 1]
            dgh = dg_ref[:, h:h + 1]
            dxa = dgh * aneg * _sigmoid(xa)
            dba = jnp.where(lane == h, db, dba)
            dba = jnp.where(lane == GDN_HEADS + h, dxa, dba)
            d_alog = jnp.sum(dgh * _softplus(xa), axis=0, keepdims=True) * aneg
            sm = jnp.where(lane1 == h, d_alog, sm)
            sm = jnp.where(lane1 == GDN_HEADS + h, jnp.sum(dxa, axis=0, keepdims=True), sm)
        vs = slice(2 * GDN_WIDTH, W3)
        dc_ref[:, vs] = dv_ref[...] * dsl[:, vs]
        dba_ref[...] = dba

        @pl.when(i == 0)
        def _():
            sm_ref[...] = sm

        @pl.when(i > 0)
        def _():
            sm_ref[...] += sm

    prev_spec, _ = _halo_specs(tm, W3, 0, nblk)
    row = pl.BlockSpec((tm, GDN_WIDTH), lambda i: (i, 0))
    col = pl.BlockSpec((tm, 128), lambda i: (i, 0))
    small = lambda a: pl.BlockSpec(a.shape, lambda i: (0, 0))
    return pl.pallas_call(
        body, name=name, grid=(nblk,),
        in_specs=[row, row, row, col, col, pl.BlockSpec((tm, W3), lambda i: (i, 0)), prev_spec,
                  pl.BlockSpec((tm, 128), lambda i: (i, W3 // 128)), small(conv_w), small(a_log), small(dt_bias)],
        out_specs=[pl.BlockSpec((tm, W3), lambda i: (i, 0)), col, pl.BlockSpec((1, 128), lambda i: (0, 0))],
        out_shape=[_sds((S, W3)), _sds((S, 128)), _sds((1, 128))], compiler_params=_params(("arbitrary",)),
    )(dqn, dkn, dv, dg, dbeta, proj_a, proj_a, proj_a, conv_w, a_log, dt_bias)


def _gdn_conv_bwd(dc, dba, proj_a, conv_w, *, name, tm=256):
    S = proj_a.shape[0]
    nblk = S // tm
    W3 = 3 * GDN_WIDTH

    def body(dc_ref, dnext_ref, dba_ref, cur_ref, prev_ref, cw_ref, da_ref, dcw_ref):
        i = pl.program_id(0)
        prev = jnp.where(i > 0, prev_ref[...], 0.0)
        nxt = jnp.where(i < nblk - 1, dnext_ref[...], 0.0)
        dx, dw = _conv_rows_bwd(dc_ref[...], nxt, prev, cur_ref[...], cw_ref[...], GDN_CONV)
        da_ref[:, 0:W3] = dx.astype(BF16)
        da_ref[:, W3:] = dba_ref[...].astype(BF16)

        @pl.when(i == 0)
        def _():
            dcw_ref[...] = dw

        @pl.when(i > 0)
        def _():
            dcw_ref[...] += dw

    prev_spec, next_spec = _halo_specs(tm, W3, 0, nblk)
    wide = pl.BlockSpec((tm, W3), lambda i: (i, 0))
    return pl.pallas_call(
        body, name=name, grid=(nblk,),
        in_specs=[wide, next_spec, pl.BlockSpec((tm, 128), lambda i: (i, 0)), wide, prev_spec,
                  pl.BlockSpec(conv_w.shape, lambda i: (0, 0))],
        out_specs=[pl.BlockSpec((tm, A_COLS), lambda i: (i, 0)), pl.BlockSpec(conv_w.shape, lambda i: (0, 0))],
        out_shape=[_sds((S, A_COLS), BF16), _sds(conv_w.shape)], compiler_params=_params(("arbitrary",)),
    )(dc, dc, dba, proj_a, proj_a, conv_w)


def _band_mask(nk):
    i = lax.broadcasted_iota(jnp.int32, (2 * BAND, nk), 0) & (BAND - 1)
    j = lax.broadcasted_iota(jnp.int32, (2 * BAND, nk), 1)
    if nk == BAND:
        return j <= i
    return (j >= i) & (j <= i + BAND)


def _stack_heads(x, lo):
    return jnp.concatenate([jnp.where(lo, x, 0.0), jnp.where(lo, 0.0, x)], axis=0)


def _stack_cols(x):
    return jnp.concatenate([x[:, 0:1], x[:, DIL_DIM:DIL_DIM + 1]], axis=0)


def _unstack(x, lo):
    return jnp.where(lo, x[0:BAND], x[BAND:2 * BAND])


def _rows(start, size, stride):
    return pl.ds(start, size) if stride == 1 else pl.ds(start, size, stride=stride)


ATTN_LANES = 4


def _attn_blocks(S, visit_many, lanes=ATTN_LANES):
    for d in DILATIONS:
        nb = S // (d * BAND)
        if d == 1:
            half = nb // 2
            visit_many(d, [(0, 0, True), (0, half, False)])

            def pair(n, c):
                visit_many(1, [(0, n, False), (0, n + half, False)])
                return c
            lax.fori_loop(1, half, pair, 0)
        elif nb > 1:
            for r0 in range(0, d, lanes):
                visit_many(d, [(r0 + t, 0, True) for t in range(lanes)])

                def column(n, c, d=d, r0=r0):
                    visit_many(d, [(r0 + t, n, False) for t in range(lanes)])
                    return c
                lax.fori_loop(1, nb, column, 0)
        else:
            def group(g, c, d=d):
                visit_many(d, [(g * lanes + t, 0, True) for t in range(lanes)])
                return c
            lax.fori_loop(0, d // lanes, group, 0)


def _attn_fwd(proj_b, oab, *, name):
    S = proj_b.shape[0]
    scale = DIL_DIM ** -0.5

    def body(q_ref, k_ref, v_ref, oab_in_ref, ob_ref, lse_ref, m_ref, l_ref, acc_ref):
        del oab_in_ref
        lane = lax.broadcasted_iota(jnp.int32, (BAND, 128), 1)
        lo = lane < DIL_DIM
        m_ref[...] = jnp.full_like(m_ref, NEG_BIG)
        l_ref[...] = jnp.zeros_like(l_ref)
        acc_ref[...] = jnp.zeros_like(acc_ref)

        def load(d, r, n, first):
            nk = BAND if first else 2 * BAND
            qrows = _rows(r + n * (BAND * d), BAND, d)
            krows = _rows(r if first else r + (n - 1) * (BAND * d), nk, d)
            return dict(nk=nk, qrows=qrows, q=q_ref[qrows, :] * scale, k=k_ref[krows, :].astype(BF16),
                        v=v_ref[krows, :].astype(BF16), m=m_ref[qrows, :], l=l_ref[qrows, :], acc=acc_ref[qrows, :])

        def compute(b):
            q, k, v = b["q"], b["k"], b["v"]
            s = jnp.where(_band_mask(b["nk"]), _bdot_nt(_stack_heads(q, lo), k), NEG_BIG)
            m_old = _stack_cols(b["m"])
            m_new = jnp.maximum(m_old, jnp.max(s, axis=-1, keepdims=True))
            p = jnp.exp(s - m_new)
            alpha = _unstack(jnp.exp(m_old - m_new), lo)
            l_new = alpha * b["l"] + _unstack(jnp.sum(p, axis=-1, keepdims=True), lo)
            return _unstack(m_new, lo), l_new, alpha * b["acc"] + _unstack(_bdot(p, v), lo)

        def visit_many(d, blocks):
            loaded = [load(d, *blk) for blk in blocks]
            done = [compute(b) for b in loaded]
            for b, (m_new, l_new, acc_new) in zip(loaded, done):
                m_ref[b["qrows"], :] = m_new
                l_ref[b["qrows"], :] = l_new
                acc_ref[b["qrows"], :] = acc_new

        _attn_blocks(S, visit_many)
        ob_ref[...] = (acc_ref[...] / l_ref[...]).astype(BF16)
        lse_ref[...] = m_ref[...] + jnp.log(l_ref[...])

    part = lambda t: pl.BlockSpec((S, 128), lambda p: (0, 3 * p + t))
    return pl.pallas_call(
        body, name=name, grid=(DIL_PAIRS,),
        in_specs=[part(0), part(1), part(2), pl.BlockSpec(memory_space=pl.ANY)],
        out_specs=[pl.BlockSpec((S, 128), lambda p: (0, GDN_WIDTH // 128 + p)), pl.BlockSpec((S, 128), lambda p: (0, p))],
        out_shape=[_sds(oab.shape, BF16), _sds((S, DIL_WIDTH))],
        scratch_shapes=[pltpu.VMEM((S, 128), F32)] * 3, input_output_aliases={3: 0},
        compiler_params=_params(("parallel",)),
    )(proj_b, proj_b, proj_b, oab)


def _attn_bwd(proj_b, oab, d_oab, lse, *, name):
    S = proj_b.shape[0]
    scale = DIL_DIM ** -0.5

    def body(q_ref, k_ref, v_ref, o_ref, do_ref, lse_ref, dqkv_ref, dq_ref, dk_ref, dv_ref, delta_ref):
        lane = lax.broadcasted_iota(jnp.int32, (BAND, 128), 1)
        lo = lane < DIL_DIM
        dq_ref[...] = jnp.zeros_like(dq_ref)
        dk_ref[...] = jnp.zeros_like(dk_ref)
        dv_ref[...] = jnp.zeros_like(dv_ref)
        prod = do_ref[...] * o_ref[...].astype(F32)
        lo_all = lax.broadcasted_iota(jnp.int32, (S, 128), 1) < DIL_DIM
        delta_ref[...] = jnp.where(lo_all, jnp.sum(jnp.where(lo_all, prod, 0.0), axis=-1, keepdims=True),
                                   jnp.sum(jnp.where(lo_all, 0.0, prod), axis=-1, keepdims=True))

        def load(d, r, n, first):
            nk = BAND if first else 2 * BAND
            qrows = _rows(r + n * (BAND * d), BAND, d)
            krows = _rows(r if first else r + (n - 1) * (BAND * d), nk, d)
            return dict(nk=nk, qrows=qrows, krows=krows, q=q_ref[qrows, :] * scale, k=k_ref[krows, :], v=v_ref[krows, :],
                        do=do_ref[qrows, :], delta=delta_ref[qrows, :], lse=lse_ref[qrows, :],
                        dq=dq_ref[qrows, :], dk=dk_ref[krows, :], dv=dv_ref[krows, :])

        def compute(b):
            q, k, v, do = b["q"], b["k"], b["v"], b["do"]
            qs, dos = _stack_heads(q, lo), _stack_heads(do, lo)
            p = jnp.where(_band_mask(b["nk"]), jnp.exp(_bdot_nt(qs, k) - _stack_cols(b["lse"])), 0.0)
            ds = p * (_bdot_nt(dos, v) - _stack_cols(b["delta"]))
            dq = b["dq"] + _unstack(_bdot(ds, k), lo) * scale
            return dq, b["dk"] + _bdot_tn(ds, qs), b["dv"] + _bdot_tn(p, dos)

        def visit_many(d, blocks):
            loaded = [load(d, *blk) for blk in blocks]
            done = [compute(b) for b in loaded]
            for b, (dq, dk, dv) in zip(loaded, done):
                dq_ref[b["qrows"], :] = dq
                dk_ref[b["krows"], :] = dk
                dv_ref[b["krows"], :] = dv

        _attn_blocks(S, visit_many, lanes=2)
        dqkv_ref[:, 0:128] = dq_ref[...].astype(BF16)
        dqkv_ref[:, 128:256] = dk_ref[...].astype(BF16)
        dqkv_ref[:, 256:384] = dv_ref[...].astype(BF16)

    half = lambda p: (0, GDN_WIDTH // 128 + p)
    part = lambda t: pl.BlockSpec((S, 128), lambda p: (0, 3 * p + t))
    return pl.pallas_call(
        body, name=name, grid=(DIL_PAIRS,),
        in_specs=[part(0), part(1), part(2), pl.BlockSpec((S, 128), half), pl.BlockSpec((S, 128), half),
                  pl.BlockSpec((S, 128), lambda p: (0, p))],
        out_specs=pl.BlockSpec((S, 384), lambda p: (0, p)), out_shape=_sds((S, 3 * DIL_WIDTH), BF16),
        scratch_shapes=[pltpu.VMEM((S, 128), F32)] * 4, compiler_params=_params(("parallel",)),
    )(proj_b, proj_b, proj_b, oab, d_oab, lse)


FF_SLAB = 2 * D_FF // N_DEV
FF_PAIRS = N_DEV // 2
ROWS16 = 16


def _taps(w, x, base, n):
    out = _shifted(x, base, n) * w[0:1]
    for t in range(1, FFN_CONV):
        out = out + _shifted(x, base + t, n) * w[t:t + 1]
    return out


def _ffn_fwd(h2, x1, w_up, conv_w, w_down, final_w, tgt, *, name, tm=512):
    S, D = h2.shape
    ni = S // tm
    per = tm // ROWS16

    def body(h_ref, hp_ref, x1_ref, wg_ref, wu_ref, cg_ref, cu_ref, wd_ref, fw_ref, t_ref,
             dx_ref, dxb_ref, dfw_ref, loss_ref, ug_ref, uu_ref, x2_ref):
        i, j = pl.program_id(0), pl.program_id(1)
        hv = jnp.concatenate([hp_ref[...], h_ref[...]], axis=0)
        row = lax.broadcasted_iota(jnp.int32, (tm + ROWS16, 1), 0)
        keep = (i > 0) | (row >= ROWS16)

        def branch(w_ref, c_ref, u_ref):
            u = lax.dot_general(hv, w_ref[...], (((1,), (1,)), ((), ())), preferred_element_type=F32).astype(BF16)
            u_ref[...] = u[ROWS16:]
            return _taps(c_ref[...], jnp.where(keep, u.astype(F32), 0.0), ROWS16 - (FFN_CONV - 1), tm)

        gate = branch(wg_ref, cg_ref, ug_ref)
        up = branch(wu_ref, cu_ref, uu_ref)
        act = (gate * _sigmoid(gate) * up).astype(BF16)
        part = jnp.dot(act, wd_ref[...], preferred_element_type=F32)

        @pl.when(j == 0)
        def _():
            x2_ref[...] = x1_ref[...] + part

        @pl.when((j > 0) & (j < FF_PAIRS - 1))
        def _():
            x2_ref[...] += part

        @pl.when(j == FF_PAIRS - 1)
        def _():
            xv = x2_ref[...] + part
            wv = fw_ref[...]
            r = lax.rsqrt(jnp.mean(xv * xv, axis=-1, keepdims=True) + EPS)
            err = xv * r * wv - t_ref[...]
            lsum = jnp.sum(jnp.sum(err * err, axis=-1, keepdims=True), axis=0, keepdims=True) * (0.5 / D)
            g = err * (1.0 / D)
            xh = xv * r
            gw = g * wv
            dx = r * (gw - xh * jnp.mean(gw * xh, axis=-1, keepdims=True))
            dx_ref[...] = dx
            dxb_ref[...] = dx.astype(BF16)
            dfw = jnp.sum(g * xh, axis=0, keepdims=True)
            lpart = jnp.broadcast_to(lsum, (1, 128))

            @pl.when(i == 0)
            def _():
                dfw_ref[...] = dfw
                loss_ref[...] = lpart

            @pl.when(i > 0)
            def _():
                dfw_ref[...] += dfw
                loss_ref[...] += lpart

    rows = pl.BlockSpec((tm, D), lambda i, j: (i, 0))
    slab = lambda off: pl.BlockSpec((None, FF_SLAB, D), lambda i, j: (j + off, 0, 0))
    cslab = lambda off: pl.BlockSpec((None, FFN_CONV, FF_SLAB), lambda i, j: (j + off, 0, 0))
    uspec = pl.BlockSpec((None, tm, FF_SLAB), lambda i, j: (j, i, 0))
    return pl.pallas_call(
        body, name=name, grid=(ni, FF_PAIRS),
        in_specs=[rows, pl.BlockSpec((ROWS16, D), lambda i, j: (jnp.maximum(i * per - 1, 0), 0)), rows,
                  slab(0), slab(FF_PAIRS), cslab(0), cslab(FF_PAIRS), pl.BlockSpec((FF_SLAB, D), lambda i, j: (j, 0)),
                  pl.BlockSpec((1, D), lambda i, j: (0, 0)), rows],
        out_specs=[rows, rows, pl.BlockSpec((1, D), lambda i, j: (0, 0)), pl.BlockSpec((1, 128), lambda i, j: (0, 0)), uspec, uspec],
        out_shape=[_sds((S, D)), _sds((S, D), BF16), _sds((1, D)), _sds((1, 128)),
                   _sds((FF_PAIRS, S, FF_SLAB), BF16), _sds((FF_PAIRS, S, FF_SLAB), BF16)],
        scratch_shapes=[pltpu.VMEM((tm, D), F32)],
        compiler_params=_params(("arbitrary", "arbitrary")),
    )(h2, h2, x1, w_up, w_up, conv_w, conv_w, w_down, final_w, tgt)


def _ffn_bwd(dx2, h2, ug, uu, conv_w, w_down, *, name, tm=512):
    S, D = h2.shape
    ni = S // tm
    per = tm // ROWS16
    ext = tm + ROWS16

    def body(dx_ref, dxn_ref, h_ref, ug_ref, ugp_ref, ugn_ref, uu_ref, uup_ref, uun_ref, cg_ref, cu_ref, wd_ref,
             du_ref, gd_ref, gup_ref, dcw_ref, acc_d, acc_g, acc_u, acc_cg, acc_cu):
        i = pl.program_id(1)

        @pl.when(i == 0)
        def _():
            acc_d[...] = jnp.zeros_like(acc_d)
            acc_g[...] = jnp.zeros_like(acc_g)
            acc_u[...] = jnp.zeros_like(acc_u)
            acc_cg[...] = jnp.zeros_like(acc_cg)
            acc_cu[...] = jnp.zeros_like(acc_cu)

        dx = dx_ref[...]
        dxe = jnp.concatenate([dx, dxn_ref[...]], axis=0)
        row = lax.broadcasted_iota(jnp.int32, (ext, 1), 0)
        live = (i < ni - 1) | (row < tm)
        d_act = jnp.where(live, lax.dot_general(dxe, wd_ref[...], (((1,), (1,)), ((), ())), preferred_element_type=F32), 0.0)
        rowp = lax.broadcasted_iota(jnp.int32, (ext + ROWS16, 1), 0)
        keep = (i > 0) | (rowp >= ROWS16)

        def pre(cur, prev, nxt):
            return jnp.where(keep, jnp.concatenate([prev[...], cur[...], nxt[...]], axis=0).astype(F32), 0.0)

        uge, uue = pre(ug_ref, ugp_ref, ugn_ref), pre(uu_ref, uup_ref, uun_ref)
        cg, cu = cg_ref[...], cu_ref[...]
        base = ROWS16 - (FFN_CONV - 1)
        gate = _taps(cg, uge, base, ext)
        up = _taps(cu, uue, base, ext)
        sg = _sigmoid(gate)
        silu = gate * sg
        dgc = d_act * up * _dsilu(gate, sg)
        duc = d_act * silu

        def conv_t(w, dc):
            out = _shifted(dc, FFN_CONV - 1, tm) * w[0:1]
            for t in range(1, FFN_CONV):
                out = out + _shifted(dc, FFN_CONV - 1 - t, tm) * w[t:t + 1]
            return out.astype(BF16)

        du_g, du_u = conv_t(cg, dgc), conv_t(cu, duc)
        du_ref[0] = du_g
        du_ref[1] = du_u
        dcw = lambda dc, xe: jnp.concatenate(
            [jnp.sum(dc[0:tm] * _shifted(xe, base + t, tm), axis=0, keepdims=True) for t in range(FFN_CONV)], axis=0)
        acc_cg[0:FFN_CONV, :] += dcw(dgc, uge)
        acc_cu[0:FFN_CONV, :] += dcw(duc, uue)
        tn = (((0,), (0,)), ((), ()))
        act = (silu[0:tm] * up[0:tm]).astype(BF16)
        acc_d[...] += lax.dot_general(act, dx, tn, preferred_element_type=F32)
        hv = h_ref[...]
        acc_g[...] += lax.dot_general(du_g, hv, tn, preferred_element_type=F32)
        acc_u[...] += lax.dot_general(du_u, hv, tn, preferred_element_type=F32)

        @pl.when(i == ni - 1)
        def _():
            gd_ref[...] = acc_d[...].astype(BF16)
            gup_ref[0] = acc_g[...].astype(BF16)
            gup_ref[1] = acc_u[...].astype(BF16)
            dcw_ref[0] = acc_cg[0:FFN_CONV, :]
            dcw_ref[1] = acc_cu[0:FFN_CONV, :]

    last16 = S // ROWS16 - 1
    rows = pl.BlockSpec((tm, D), lambda j, i: (i, 0))
    rows_next = pl.BlockSpec((ROWS16, D), lambda j, i: (jnp.minimum((i + 1) * per, last16), 0))
    u_cur = pl.BlockSpec((None, tm, FF_SLAB), lambda j, i: (j, i, 0))
    u_prev = pl.BlockSpec((None, ROWS16, FF_SLAB), lambda j, i: (j, jnp.maximum(i * per - 1, 0), 0))
    u_next = pl.BlockSpec((None, ROWS16, FF_SLAB), lambda j, i: (j, jnp.minimum((i + 1) * per, last16), 0))
    cslab = lambda off: pl.BlockSpec((None, FFN_CONV, FF_SLAB), lambda j, i: (j + off, 0, 0))
    return pl.pallas_call(
        body, name=name, grid=(FF_PAIRS, ni),
        in_specs=[rows, rows_next, rows, u_cur, u_prev, u_next, u_cur, u_prev, u_next, cslab(0), cslab(FF_PAIRS),
                  pl.BlockSpec((FF_SLAB, D), lambda j, i: (j, 0))],
        out_specs=[pl.BlockSpec((None, 2, tm, FF_SLAB), lambda j, i: (j, 0, i, 0)), pl.BlockSpec((FF_SLAB, D), lambda j, i: (j, 0)),
                   pl.BlockSpec((None, 2, FF_SLAB, D), lambda j, i: (j, 0, 0, 0)),
                   pl.BlockSpec((None, 2, FFN_CONV, FF_SLAB), lambda j, i: (j, 0, 0, 0))],
        out_shape=[_sds((FF_PAIRS, 2, S, FF_SLAB), BF16), _sds((D_FF, D), BF16), _sds((FF_PAIRS, 2, FF_SLAB, D), BF16),
                   _sds((FF_PAIRS, 2, FFN_CONV, FF_SLAB))],
        scratch_shapes=[pltpu.VMEM((FF_SLAB, D), F32), pltpu.VMEM((FF_SLAB, D), F32), pltpu.VMEM((FF_SLAB, D), F32),
                        pltpu.VMEM((8, FF_SLAB), F32), pltpu.VMEM((8, FF_SLAB), F32)],
        compiler_params=_params(("parallel", "arbitrary")),
    )(dx2, dx2, h2, ug, ug, ug, uu, uu, uu, conv_w, conv_w, w_down)


def _pair_slot(p):
    return 2 * (p & (FF_PAIRS - 1)) + (p >> 2)


def _mm_slabs(a, w, *, name, res=None, norm_bwd=None, after=(), tm=1024, tn=1024):
    nk, S, _ = a.shape
    D = w.shape[2]
    has_res = res is not None
    has_norm = norm_bwd is not None
    assert not has_norm or tn == D

    def body(*refs):
        a_ref, w_ref = refs[:2]
        r_ref = refs[2] if has_res else None
        if has_norm:
            x_ref, nw_ref, skip_ref = refs[2 + has_res:5 + has_res]
            o_ref, dw_ref, acc_ref = refs[-3:]
        else:
            o_ref, acc_ref = refs[-2:]
        i, k = pl.program_id(0), pl.program_id(2)
        part = jnp.dot(a_ref[...], w_ref[...], preferred_element_type=F32)

        @pl.when(k == 0)
        def _():
            acc_ref[...] = part

        @pl.when(k > 0)
        def _():
            acc_ref[...] += part

        @pl.when(k == nk - 1)
        def _():
            r = acc_ref[...] + r_ref[...] if has_res else acc_ref[...]
            if has_norm:
                dx, dw = _rms_bwd_rows(r, x_ref[...], nw_ref[...])
                o_ref[...] = skip_ref[...] + dx

                @pl.when(i == 0)
                def _():
                    dw_ref[...] = dw

                @pl.when(i > 0)
                def _():
                    dw_ref[...] += dw
            else:
                o_ref[...] = r

    o_spec = pl.BlockSpec((tm, tn), lambda i, j, k: (i, j))
    one = pl.BlockSpec((1, tn), lambda i, j, k: (0, 0))
    return pl.pallas_call(
        body, name=name, grid=(S // tm, D // tn, nk),
        in_specs=[pl.BlockSpec((None, tm, FF_SLAB), lambda i, j, k: (k, i, 0)),
                  pl.BlockSpec((None, FF_SLAB, tn), lambda i, j, k: (FF_PAIRS * (k & 1) + (k >> 1), 0, j))] + [o_spec] * has_res
        + ([o_spec, one, o_spec] if has_norm else []) + [ANY] * len(after),
        out_specs=[o_spec, one] if has_norm else o_spec, out_shape=[_sds((S, D)), _sds((1, D))] if has_norm else _sds((S, D)),
        scratch_shapes=[pltpu.VMEM((tm, tn), F32)],
        compiler_params=_params(("arbitrary" if has_norm else "parallel", "parallel", "arbitrary")),
    )(*((a, w) + ((res,) if has_res else ()) + (tuple(norm_bwd) if has_norm else ()) + tuple(after)))


def _local_step(x, tgt, norm1_w, w_land, conv_a, a_log, dt_bias, gnw, norm2_w, final_w, late_weights, emit, start_after=()):
    wgrad = functools.partial(_mm, ta=True, out_dtype=BF16)
    h1, proj_a, proj_z, proj_b = _in_proj(x, norm1_w, w_land, after=start_after, name="in_proj")
    qn, kn, v, gcb, bb = _gdn_prep_fwd(proj_a, conv_a, a_log, dt_bias, name="gdn_prep_fwd")
    uv, wk, at, tmat, wkb, qdb, keb = _gdn_chunk_fwd(qn, kn, v, gcb, bb, name="gdn_chunk_fwd")
    o, u, sp, oab = _gdn_scan_fwd(uv, at, wkb, qdb, keb, gcb, proj_z, gnw, name="gdn_scan_fwd")
    oab, lse = _attn_fwd(proj_b, oab, name="attn_fwd")
    w_out, w_up, conv_f, w_down = late_weights(oab)
    x1, h2 = _out_proj_norm(oab, w_out, x, norm2_w, name="out_proj")
    dx2, dx2_b, d_final, loss, ug, uu = _ffn_fwd(h2, x1, w_up, conv_f, w_down, final_w, tgt, name="ffn_fwd")
    du, g_down, g_up, dcw = _ffn_bwd(dx2_b, h2, ug, uu, conv_f, w_down, name="ffn_bwd")
    token = emit("ffn", w_down=g_down, w_up=g_up.reshape(N_DEV, FF_SLAB, -1), conv_f=dcw.reshape(N_DEV, FFN_CONV, -1))
    dx1, d_norm2 = _mm_slabs(du.reshape(N_DEV, -1, FF_SLAB), w_up, norm_bwd=(x1, norm2_w, dx2), after=token, name="ffn_up_dx")
    d_oab = _mm(dx1, w_out, tb=True, name="out_proj_dx", tn=D_MODEL, tk=1024)
    token = emit("out", w_out=wgrad(oab, dx1, name="out_proj_dw", tm=D_MODEL, tn=D_MODEL))
    dz, d_gnw, du, dwk, dat, dqd, dke, dgl = _gdn_scan_bwd(d_oab, o, proj_z, gnw, sp, u, at, wkb, qdb, keb, gcb, after=token, name="gdn_scan_bwd")
    dqn, dkn, dv, dg, dbeta = _gdn_chunk_bwd(qn, kn, gcb, bb, tmat, uv, wk, du, dwk, dat, dqd, dke, dgl, name="gdn_chunk_bwd")
    dc, dba, d_small = _gdn_prep_bwd(dqn, dkn, dv, dg, dbeta, proj_a, conv_a, a_log, dt_bias, name="gdn_prep_bwd")
    d_pa, d_conv_a = _gdn_conv_bwd(dc, dba, proj_a, conv_a, name="gdn_conv_bwd")
    d_pb = _attn_bwd(proj_b, oab, d_oab, lse, name="attn_bwd")
    g_a = wgrad(d_pa, h1, name="proj_a_dw", tm=A_COLS, tn=D_MODEL)
    g_z = wgrad(dz, h1, name="proj_z_dw", tn=D_MODEL)
    g_b = wgrad(d_pb, h1, name="proj_b_dw", tm=768, tn=D_MODEL)
    token = emit("in", w_a=g_a, w_z=g_z, w_b=g_b, conv_a=d_conv_a)
    grad_x, d_norm1 = _in_proj_dx((d_pa, dz, d_pb), w_land, x, norm1_w, dx1, after=token, name="in_proj_dx")
    small = dict(norm1=d_norm1, small=d_small, gnw=d_gnw, norm2=d_norm2, final=d_final)
    return loss, grad_x, small


_O1 = 3 * GDN_WIDTH
_O2 = _O1 + GDN_WIDTH
_O3 = _O2 + 2 * GDN_HEADS


_W_IN_ROWS = (A_COLS, GDN_WIDTH, 3 * DIL_WIDTH)
_IN_ROWS = (_O3 + 3 * DIL_WIDTH) // N_DEV
_ROW_TILE = 16
_IN_STEP = _IN_ROWS - _IN_ROWS % _ROW_TILE
_GAP = 8
_LAND_ROWS = 464
assert _O3 % _ROW_TILE == _ROW_TILE - _GAP and N_DEV - 1 + _GAP + _IN_ROWS <= _LAND_ROWS and _LAND_ROWS % _ROW_TILE == 0


def _padded_row(r):
    return r + (_GAP if r >= _O3 else 0)


def _shifted_slab(w_rows, j, *, name):
    q = w_rows.shape[0] // _IN_ROWS

    def body(j_ref, w_ref, o_ref, pad_ref):
        pad_ref[...] = jnp.zeros_like(pad_ref)
        for dev in range(N_DEV):
            @pl.when(j_ref[0] == dev)
            def _(dev=dev):
                p = lax.broadcasted_iota(jnp.int32, (_LAND_ROWS, 1), 0) + _IN_STEP * dev
                for s in range(q):
                    pad_ref[0:_IN_ROWS, :] = w_ref[pl.ds(s, _IN_ROWS, stride=q), :]
                    rows = pad_ref[...]
                    a = pltpu.roll(rows, dev, 0) if dev else rows
                    b = pltpu.roll(rows, dev + _GAP, 0)
                    o_ref[:, 128 * s:128 * (s + 1)] = jnp.where(p < _O3, a, jnp.where(p >= _O3 + _GAP, b, 0.0)).astype(BF16)

    vm = pl.BlockSpec(memory_space=pltpu.VMEM)
    return pl.pallas_call(
        body, name=name, in_specs=[pl.BlockSpec(memory_space=pltpu.SMEM), vm], out_specs=vm,
        out_shape=_sds((_LAND_ROWS, 128 * q), BF16), scratch_shapes=[pltpu.VMEM((_LAND_ROWS, 128), F32)],
    )(j, w_rows)


def _w_in_plan():
    def dest(p):
        if p < _O1:
            return 0, p
        if p < _O2:
            return 1, p - _O1
        if p < _O2 + _ROW_TILE:
            return 0, _O1
        q = p - _O3 - _GAP
        t, pair = divmod(q // 128, DIL_PAIRS)
        return 2, (3 * pair + t) * 128 + q % 128

    spans = [(_padded_row(_IN_ROWS * j), _padded_row(_IN_ROWS * (j + 1) - 1) + 1) for j in range(N_DEV)]
    runs, seams = [], []
    for p in range(0, spans[-1][1], _ROW_TILE):
        owners = [j for j, (lo, hi) in enumerate(spans) if lo < p + _ROW_TILE and hi > p]
        w, r = dest(p)
        if len(owners) == 2:
            seams.append((w, r, owners[0], p - _IN_STEP * owners[0], owners[1], p - _IN_STEP * owners[1]))
            continue
        (j,) = owners
        last = runs[-1] if runs else None
        if last and last[0] == j and last[2] == w and last[3] + last[4] == r and last[1] + last[4] == p - _IN_STEP * j:
            runs[-1] = last[:4] + (last[4] + _ROW_TILE,)
        else:
            runs.append((j, p - _IN_STEP * j, w, r, _ROW_TILE))
    return runs, seams


def _w_in_scratch(d):
    return [pltpu.VMEM((n, d), BF16) for n in _W_IN_ROWS] + [pltpu.VMEM((N_DEV - 1, _ROW_TILE, d), BF16),
                                                              pltpu.SemaphoreType.DMA(())]


def _fetch_w_in(land_ref, wa_ref, wz_ref, wb_ref, seam_ref, sem):
    w_refs = (wa_ref, wz_ref, wb_ref)
    runs, seams = _w_in_plan()
    copies = [pltpu.make_async_copy(land_ref.at[j, pl.ds(s, n)], w_refs[w].at[pl.ds(r, n)], sem) for j, s, w, r, n in runs]
    for k, (w, r, j0, s0, j1, s1) in enumerate(seams):
        copies.append(pltpu.make_async_copy(land_ref.at[j0, pl.ds(s0, _ROW_TILE)], w_refs[w].at[pl.ds(r, _ROW_TILE)], sem))
        copies.append(pltpu.make_async_copy(land_ref.at[j1, pl.ds(s1, _ROW_TILE)], seam_ref.at[k], sem))
    for cp in copies:
        cp.start()
    tail = _O1 + _ROW_TILE
    wa_ref[tail:, :] = jnp.zeros((A_COLS - tail, wa_ref.shape[1]), BF16)
    for cp in copies:
        cp.wait()
    for k, (w, r, *_) in enumerate(seams):
        both = w_refs[w][r:r + _ROW_TILE, :].astype(F32) + seam_ref[k].astype(F32)
        w_refs[w][r:r + _ROW_TILE, :] = both.astype(BF16)


MESH = pl.DeviceIdType.MESH
ANY = pl.BlockSpec(memory_space=pl.ANY)


def _position():
    return lax.axis_index("x"), lax.axis_index("y"), lax.axis_index("c")


def _slot(p):
    return 4 * p[0] + 2 * p[1] + p[2]


def _all_gather(blocks, *, name):
    n = len(blocks)

    def body(*refs):
        ins, outs = refs[:n], refs[n:2 * n]
        send_sems, recv_sems, local_sems = refs[2 * n:]
        x, y, c = _position()
        me, sibling = (x, y, c), (x, y, 1 - c)
        chips = [(1 - x, y), (x, 1 - y), (1 - x, 1 - y)]

        def copy(a, k, block, to, src=None):
            dst = outs[a].at[_slot(block)]
            return pltpu.make_async_remote_copy(
                src_ref=dst if src is None else src, dst_ref=dst, send_sem=send_sems.at[a, k], recv_sem=recv_sems.at[a, k],
                device_id=to, device_id_type=MESH)

        mine = [pltpu.make_async_copy(ins[a], outs[a].at[_slot(me)], local_sems.at[a]) for a in range(n)]
        for cp in mine:
            cp.start()
        first = []
        for a in range(n):
            first.append(copy(a, 0, me, sibling, src=ins[a]))
            first += [copy(a, 1 + j, me, (*chip, c), src=ins[a]) for j, chip in enumerate(chips)]
        for cp in first:
            cp.start()
        passed = []
        for j, chip in enumerate(chips):
            for a in range(n):
                copy(a, 1 + j, (*chip, c), me).wait_recv()
                fwd = copy(a, 4 + j, (*chip, c), sibling)
                fwd.start()
                passed.append(fwd)
        for a in range(n):
            copy(a, 0, sibling, me).wait_recv()
            for j, chip in enumerate(chips):
                copy(a, 4 + j, (*chip, 1 - c), me).wait_recv()
        for cp in first + passed:
            cp.wait_send()
        for cp in mine:
            cp.wait()

    return pl.pallas_call(
        body, name=name, in_specs=[ANY] * n, out_specs=[ANY] * n,
        out_shape=[_sds((N_DEV,) + b.shape, b.dtype) for b in blocks],
        scratch_shapes=[pltpu.SemaphoreType.DMA((n, 7)), pltpu.SemaphoreType.DMA((n, 7)), pltpu.SemaphoreType.DMA((n,))],
    )(*blocks)


def _gather_direct(block, *, name, after=()):
    def body(in_ref, *rest):
        out_ref, send_sems, recv_sems, local_sem = rest[len(after):]
        x, y, c = _position()
        me = _slot((x, y, c))
        mine = pltpu.make_async_copy(in_ref, out_ref.at[me], local_sem)
        mine.start()
        copies = [pltpu.make_async_remote_copy(
            src_ref=in_ref, dst_ref=out_ref.at[me], send_sem=send_sems.at[k - 1], recv_sem=recv_sems.at[k - 1],
            device_id=_peer_of(k, x, y, c), device_id_type=MESH) for k in range(1, N_DEV)]
        for cp in copies:
            cp.start()
        for cp in copies:
            cp.wait()
        mine.wait()

    return pl.pallas_call(
        body, name=name, in_specs=[pl.BlockSpec(memory_space=pltpu.VMEM)] + [ANY] * len(after),
        out_specs=pl.BlockSpec(memory_space=pltpu.VMEM),
        out_shape=_sds((N_DEV,) + block.shape, block.dtype),
        scratch_shapes=[pltpu.SemaphoreType.DMA((N_DEV - 1,)), pltpu.SemaphoreType.DMA((N_DEV - 1,)), pltpu.SemaphoreType.DMA],
    )(block, *after)


HBM = pl.BlockSpec(memory_space=pltpu.HBM)
SEM = pl.BlockSpec(memory_space=pltpu.SEMAPHORE)
EFFECT = pltpu.SideEffectType.DATAFLOW_SIDE_EFFECTING


def _peer_of(k, x, y, c):
    return (1 - x if k & 4 else x, 1 - y if k & 2 else y, 1 - c if k & 1 else c)


def _flight(a, k):
    return a * (N_DEV - 1) + k - 1


def _exchange_start(arrays, *, name, broadcast=False, paired=()):
    n = len(arrays)

    def body(*refs):
        ins, lands = refs[:n], refs[n:2 * n]
        send_sems, recv_sems = refs[2 * n:2 * n + 2]
        token = refs[2 * n + 2 + 2 * n]
        x, y, c = _position()
        me = _slot((x, y, c))
        for k in range(1, N_DEV):
            peer = _peer_of(k, x, y, c)
            for a in range(n):
                at = _pair_slot(_slot(peer)) if a in paired else _slot(peer)
                pltpu.make_async_remote_copy(
                    src_ref=ins[a] if broadcast else ins[a].at[at], dst_ref=lands[a].at[me],
                    send_sem=send_sems.at[_flight(a, k)], recv_sem=recv_sems.at[_flight(a, k)],
                    device_id=peer, device_id_type=MESH).start()
        if broadcast:
            for a in range(n):
                pltpu.make_async_copy(ins[a], lands[a].at[me], refs[-1].at[a]).start()
        token[...] = jnp.zeros_like(token)

    land_shapes = [((N_DEV,) + s.shape) if broadcast else s.shape for s in arrays]
    lands = [pltpu.with_memory_space_constraint(lax.empty(shp, s.dtype), pltpu.HBM) for shp, s in zip(land_shapes, arrays)]
    srcs = [pltpu.with_memory_space_constraint(s, pltpu.HBM) for s in arrays]
    outs = pl.pallas_call(
        body, name=name, in_specs=[HBM] * (2 * n),
        out_specs=[SEM, SEM] + [HBM] * (2 * n) + [pl.BlockSpec(memory_space=pltpu.VMEM)] + [SEM] * broadcast,
        out_shape=[pltpu.SemaphoreType.DMA((n * (N_DEV - 1),)), pltpu.SemaphoreType.DMA((n * (N_DEV - 1),))]
        + [pltpu.HBM(s.shape, s.dtype) for s in arrays] + [pltpu.HBM(shp, s.dtype) for shp, s in zip(land_shapes, arrays)]
        + [_sds((8, 128))] + [pltpu.SemaphoreType.DMA((n,))] * broadcast,
        input_output_aliases={i: 2 + i for i in range(2 * n)},
        compiler_params=pltpu.CompilerParams(has_side_effects=EFFECT),
    )(*srcs, *lands)
    flight = (outs[0], outs[1], outs[2:2 + n], outs[2 + n:2 + 2 * n], outs[2 + 2 * n])
    return flight + (outs[-1],) if broadcast else flight


def _exchange_wait(send_sems, recv_sems, srcs, lands, after, *, name, broadcast=False, own_sems=None):
    n = len(srcs)

    def body(*refs):
        ins, lnd = refs[:n], refs[n:2 * n]
        send_ref, recv_ref = refs[2 * n:2 * n + 2]
        x, y, c = _position()
        for k in range(1, N_DEV):
            for a in range(n):
                cp = pltpu.make_async_remote_copy(
                    src_ref=ins[a] if broadcast else ins[a].at[0], dst_ref=lnd[a].at[0], send_sem=send_ref.at[_flight(a, k)],
                    recv_sem=recv_ref.at[_flight(a, k)], device_id=_peer_of(k, x, y, c), device_id_type=MESH)
                cp.wait_send()
                cp.wait_recv()
        if broadcast:
            for a in range(n):
                pltpu.make_async_copy(ins[a], lnd[a].at[0], refs[2 * n + 3].at[a]).wait()

    outs = pl.pallas_call(
        body, name=name, in_specs=[HBM] * (2 * n) + [SEM, SEM, ANY] + [SEM] * broadcast, out_specs=[HBM] * (2 * n),
        out_shape=[pltpu.HBM(s.shape, s.dtype) for s in srcs] + [pltpu.HBM(s.shape, s.dtype) for s in lands],
        input_output_aliases={i: i for i in range(2 * n)},
        compiler_params=pltpu.CompilerParams(has_side_effects=EFFECT),
    )(*srcs, *lands, send_sems, recv_sems, after, *([own_sems] if broadcast else []))
    return outs[:n], outs[n:]


_G_LAND_ROWS = 528


def _g_in_pieces(dev):
    runs, seams = _w_in_plan()
    pieces = [(w, r, n, s) for j, s, w, r, n in runs if j == dev]
    pieces += [(w, r, _ROW_TILE, s0) for w, r, j0, s0, j1, s1 in seams if j0 == dev]
    pieces += [(w, r, _ROW_TILE, s1) for w, r, j0, s0, j1, s1 in seams if j1 == dev]
    merged = []
    for w, r, n, s in sorted(pieces, key=lambda p: p[3]):
        if merged and merged[-1][0] == w and merged[-1][1] + merged[-1][2] == r and merged[-1][3] + merged[-1][2] == s:
            merged[-1] = (w, merged[-1][1], merged[-1][2] + n, merged[-1][3])
        else:
            merged.append((w, r, n, s))
    return merged


def _coords(dev):
    return tuple(jnp.int32(v) for v in (dev >> 2, (dev >> 1) & 1, dev & 1))


def _exchange_start_in(g_ws, conv_slabs, *, name):
    srcs = list(g_ws) + [conv_slabs]
    n = len(srcs)

    def body(*refs):
        g_refs, cv_ref, land, land_cv = refs[:n - 1], refs[n - 1], refs[n], refs[n + 1]
        send_sems, recv_sems = refs[n + 2:n + 4]
        token = refs[-1]
        me = _slot(_position())
        pieces = [_g_in_pieces(dev) for dev in range(N_DEV)]
        for i in range(max(len(p) for p in pieces)):
            for dev in range(N_DEV):
                if i < len(pieces[dev]):
                    @pl.when(me != dev)
                    def _(dev=dev, i=i):
                        w, r, rows, s = pieces[dev][i]
                        k = me ^ dev
                        pltpu.make_async_remote_copy(
                            src_ref=g_refs[w].at[pl.ds(r, rows)], dst_ref=land.at[me, pl.ds(s, rows)],
                            send_sem=send_sems.at[_flight(0, k)], recv_sem=recv_sems.at[_flight(0, k)],
                            device_id=_coords(dev), device_id_type=MESH).start()
        for dev in range(N_DEV):
            @pl.when(me != dev)
            def _(dev=dev):
                k = me ^ dev
                pltpu.make_async_remote_copy(
                    src_ref=cv_ref.at[dev], dst_ref=land_cv.at[me], send_sem=send_sems.at[_flight(1, k)],
                    recv_sem=recv_sems.at[_flight(1, k)], device_id=_coords(dev), device_id_type=MESH).start()
        token[...] = jnp.zeros_like(token)

    lands = [lax.empty((N_DEV, _G_LAND_ROWS, g_ws[0].shape[1]), g_ws[0].dtype), lax.empty(conv_slabs.shape, conv_slabs.dtype)]
    ops = [pltpu.with_memory_space_constraint(a, pltpu.HBM) for a in srcs + lands]
    outs = pl.pallas_call(
        body, name=name, in_specs=[HBM] * len(ops),
        out_specs=[SEM, SEM] + [HBM] * len(ops) + [pl.BlockSpec(memory_space=pltpu.VMEM)],
        out_shape=[pltpu.SemaphoreType.DMA((2 * (N_DEV - 1),)), pltpu.SemaphoreType.DMA((2 * (N_DEV - 1),))]
        + [pltpu.HBM(a.shape, a.dtype) for a in ops] + [_sds((8, 128))],
        input_output_aliases={i: 2 + i for i in range(len(ops))},
        compiler_params=pltpu.CompilerParams(has_side_effects=EFFECT),
    )(*ops)
    return outs[0], outs[1], outs[2:2 + n], outs[2 + n:4 + n], outs[-1]


def _own_pieces(g_ws, land, after, *, name):
    n = len(g_ws)

    def body(*refs):
        g_refs, land_ref = refs[:n], refs[n]
        token, slab, sem = refs[-3:]
        me = _slot(_position())
        for dev in range(N_DEV):
            @pl.when(me == dev)
            def _(dev=dev):
                own = [pltpu.make_async_copy(g_refs[w].at[pl.ds(r, rows)], slab.at[pl.ds(s, rows)], sem)
                       for w, r, rows, s in _g_in_pieces(dev)]
                for cp in own:
                    cp.start()
                for cp in own:
                    cp.wait()
                out = pltpu.make_async_copy(slab, land_ref.at[dev, pl.ds(0, _LAND_ROWS)], sem)
                out.start()
                out.wait()
        token[...] = jnp.zeros_like(token)

    return pl.pallas_call(
        body, name=name, in_specs=[HBM] * (n + 1) + [ANY], out_specs=[HBM, pl.BlockSpec(memory_space=pltpu.VMEM)],
        out_shape=[pltpu.HBM(land.shape, land.dtype), _sds((8, 128))], input_output_aliases={n: 0},
        scratch_shapes=[pltpu.VMEM((_LAND_ROWS, land.shape[2]), land.dtype), pltpu.SemaphoreType.DMA(())],
    )(*g_ws, land, after)


def _exchange_wait_in(send_sems, recv_sems, srcs, lands, after, *, name):
    n = len(srcs)

    def body(*refs):
        g_refs, cv_ref, land, land_cv = refs[:n - 1], refs[n - 1], refs[n], refs[n + 1]
        send_ref, recv_ref = refs[n + 2:n + 4]
        me = _slot(_position())

        def copies(dev, k):
            cps = [pltpu.make_async_remote_copy(
                src_ref=g_refs[w].at[pl.ds(r, rows)], dst_ref=land.at[0, pl.ds(s, rows)], send_sem=send_ref.at[_flight(0, k)],
                recv_sem=recv_ref.at[_flight(0, k)], device_id=_coords(dev), device_id_type=MESH)
                for w, r, rows, s in _g_in_pieces(dev)]
            return cps + [pltpu.make_async_remote_copy(
                src_ref=cv_ref.at[0], dst_ref=land_cv.at[0], send_sem=send_ref.at[_flight(1, k)],
                recv_sem=recv_ref.at[_flight(1, k)], device_id=_coords(dev), device_id_type=MESH)]

        for dev in range(N_DEV):
            @pl.when(me != dev)
            def _(dev=dev):
                for cp in copies(dev, me ^ dev):
                    cp.wait_send()

            @pl.when(me == dev)
            def _(dev=dev):
                for k in range(1, N_DEV):
                    for cp in copies(dev, k):
                        cp.wait_recv()

    ops = list(srcs) + list(lands)
    outs = pl.pallas_call(
        body, name=name, in_specs=[HBM] * len(ops) + [SEM, SEM, ANY], out_specs=[HBM] * len(ops),
        out_shape=[pltpu.HBM(a.shape, a.dtype) for a in ops],
        input_output_aliases={i: i for i in range(len(ops))},
        compiler_params=pltpu.CompilerParams(has_side_effects=EFFECT),
    )(*ops, send_sems, recv_sems, after)
    return outs[:n], outs[n:]


def _adam_update(g, w, m, v):
    c1 = 1.0 - ADAM_B1 ** ADAM_STEP
    c2 = 1.0 - ADAM_B2 ** ADAM_STEP
    nm = ADAM_B1 * m + (1.0 - ADAM_B1) * g
    nv = ADAM_B2 * v + (1.0 - ADAM_B2) * (g * g)
    return -ADAM_LR * ((nm / c1) / (jnp.sqrt(nv / c2) + ADAM_EPS) + ADAM_WD * w), nm, nv


def _adamw(landed, sent, me, w, m, v, *, name, tr=None, tc=None):
    R, C = w.shape
    tr = R if tr is None else tr
    tc = C if tc is None else tc
    assert R % tr == 0 and C % tc == 0

    def body(me_ref, own_ref, p_ref, w_ref, m_ref, v_ref, g_ref, d_ref, nm_ref, nv_ref, token_ref):
        token_ref[...] = jnp.zeros_like(token_ref)
        g = own_ref[...].astype(F32)
        for s in range(N_DEV):
            g = g + jnp.where(me_ref[1] == s, 0.0, p_ref[s].astype(F32))
        delta, nm, nv = _adam_update(g, w_ref[...], m_ref[...], v_ref[...])
        g_ref[...] = g
        nm_ref[...] = nm
        nv_ref[...] = nv
        d_ref[...] = delta

    blk = pl.BlockSpec((tr, tc), lambda i, j, me_ref: (i, j))
    return pl.pallas_call(
        body, name=name,
        grid_spec=pltpu.PrefetchScalarGridSpec(
            num_scalar_prefetch=1, grid=(R // tr, C // tc),
            in_specs=[pl.BlockSpec((None, tr, tc), lambda i, j, me_ref: (me_ref[0], i, j)),
                      pl.BlockSpec((N_DEV, tr, tc), lambda i, j, me_ref: (0, i, j)), blk, blk, blk],
            out_specs=[blk] * 4 + [pl.BlockSpec((8, 128), lambda i, j, me_ref: (0, 0))]),
        out_shape=[_sds((R, C))] * 4 + [_sds((8, 128))],
        compiler_params=_params(("arbitrary", "arbitrary")),
    )(me, sent, landed, w, m, v)


def _adamw_rowwise(landed, me, w, m, v, *, name, tr=128):
    C = landed.shape[2]
    R, q, extra = _IN_ROWS, C // 128, _ROW_TILE
    assert tr % extra == 0 and (pl.cdiv(R, tr) * tr + extra) <= landed.shape[1] and N_DEV - 1 + _GAP < extra

    def body(me_ref, a_ref, b_ref, w_ref, m_ref, v_ref, g_ref, d_ref, nm_ref, nv_ref, g_scr):
        i = pl.program_id(0)
        total = lambda ref: functools.reduce(lambda x, y: x + y, [ref[s].astype(F32) for s in range(N_DEV)])
        slab = jnp.concatenate([total(a_ref), total(b_ref)], axis=0)
        for dev in range(N_DEV):
            @pl.when(me_ref[0] == dev)
            def _(dev=dev):
                lo, hi = slab[dev:dev + tr], slab[dev + _GAP:dev + _GAP + tr]
                if _IN_ROWS * (dev + 1) <= _O3:
                    g_scr[...] = lo
                elif _IN_ROWS * dev >= _O3:
                    g_scr[...] = hi
                else:
                    r = _IN_ROWS * dev + tr * i + lax.broadcasted_iota(jnp.int32, (tr, 1), 0)
                    g_scr[...] = jnp.where(r < _O3, lo, hi)
        g = g_scr[...]
        for s in range(q):
            rows = pl.ds(s, tr, stride=q)
            gs = g[:, 128 * s:128 * (s + 1)]
            delta, nm, nv = _adam_update(gs, w_ref[rows, :], m_ref[rows, :], v_ref[rows, :])
            g_ref[rows, :] = gs
            nm_ref[rows, :] = nm
            nv_ref[rows, :] = nv
            d_ref[rows, :] = delta

    blk = pl.BlockSpec((tr * q, 128), lambda i, me_ref: (i, 0))
    return pl.pallas_call(
        body, name=name,
        grid_spec=pltpu.PrefetchScalarGridSpec(
            num_scalar_prefetch=1, grid=(pl.cdiv(R, tr),),
            in_specs=[pl.BlockSpec((N_DEV, tr, C), lambda i, me_ref: (0, i, 0)),
                      pl.BlockSpec((N_DEV, extra, C), lambda i, me_ref: (0, (tr // extra) * (i + 1), 0)), blk, blk, blk],
            out_specs=[blk] * 4, scratch_shapes=[pltpu.VMEM((tr, C), F32)]),
        out_shape=[_sds((R * q, 128))] * 4,
        compiler_params=_params(("arbitrary",)),
    )(me, landed, landed, w, m, v)


_SMALL_ROWS = 8
_SMALL_SLOTS = ((0, 0, D_MODEL), (1, 0, D_MODEL), (2, 0, D_MODEL), (3, 0, GDN_DIM), (3, GDN_DIM, GDN_HEADS),
                (3, GDN_DIM + GDN_HEADS, GDN_HEADS))
_LOSS_LANE = 2 * GDN_DIM


def _pack_small(norm1, norm2, final, gnw, a_log, dt_bias, loss):
    row3 = jnp.concatenate([gnw, a_log, dt_bias, jnp.zeros((1, 128 - 2 * GDN_HEADS), F32), loss,
                            jnp.zeros((1, D_MODEL - 3 * 128), F32)], axis=1)
    return jnp.concatenate([norm1, norm2, final, row3, jnp.zeros((_SMALL_ROWS - 4, D_MODEL), F32)], axis=0)


def _adamw_small(packs, ws, ms, vs, *, name):
    n = len(ws)

    def body(p_ref, *refs):
        w_refs, m_refs, v_refs = refs[:n], refs[n:2 * n], refs[2 * n:3 * n]
        outs = refs[3 * n:]
        g_all = p_ref[0]
        for s in range(1, N_DEV):
            g_all = g_all + p_ref[s]
        for i, (row, lane, width) in enumerate(_SMALL_SLOTS):
            g = g_all[row:row + 1, lane:lane + width]
            delta, nm, nv = _adam_update(g, w_refs[i][...], m_refs[i][...], v_refs[i][...])
            for o_ref, val in zip(outs[4 * i:4 * i + 4], (g, delta, nm, nv)):
                o_ref[...] = val
        outs[-1][...] = g_all[3:4, _LOSS_LANE:_LOSS_LANE + 128]

    vm = pl.BlockSpec(memory_space=pltpu.VMEM)
    outs = pl.pallas_call(
        body, name=name, in_specs=[vm] * (1 + 3 * n), out_specs=[vm] * (4 * n + 1),
        out_shape=[_sds(w.shape) for w in ws for _ in range(4)] + [_sds((1, 128))],
    )(packs, *ws, *ms, *vs)
    return [outs[4 * i:4 * i + 4] for i in range(n)], outs[-1]


def _slabs_by_cols(g):
    r = g.shape[0]
    return g.reshape(r, N_DEV, -1).transpose(1, 0, 2)


def _cols_from_slabs(s):
    return s.transpose(1, 0, 2).reshape(s.shape[1], -1)


def kernel(x, norm1_w, w_in, conv_qkv_w, a_log, dt_bias, gdn_norm_w, w_out, norm2_w, w_up, ffn_conv_w, w_down, final_norm_w, loss_target, m_norm1_w, m_w_in, m_conv_qkv_w, m_a_log, m_dt_bias, m_gdn_norm_w, m_w_out, m_norm2_w, m_w_up, m_ffn_conv_w, m_w_down, m_final_norm_w, v_norm1_w, v_w_in, v_conv_qkv_w, v_a_log, v_dt_bias, v_gdn_norm_w, v_w_out, v_norm2_w, v_w_up, v_ffn_conv_w, v_w_down, v_final_norm_w):
    bf = lambda a: a.astype(BF16)
    me = _slot(_position())
    me1 = jnp.reshape(me, (1,)).astype(jnp.int32)
    t_in = lambda a: a[0].T
    rows = lambda a: a.reshape(D_MODEL // 128, 128, -1).transpose(2, 0, 1).reshape(-1, 128)
    gw_in, g_conv_a = _all_gather([_shifted_slab(rows(w_in), me1, name="shift_w_in"), conv_qkv_w[0]], name="gather_w_in")
    late_src, _ = lax.optimization_barrier(([bf(w_out[0]), bf(t_in(w_up)), bf(w_down[0]), ffn_conv_w[0]], gw_in))
    l_send, l_recv, l_srcs, l_lands, l_token, l_own = _exchange_start(late_src, name="weights_start", broadcast=True)

    def late_weights(after):
        _, (gw_out, gw_up, gw_down, g_conv_f) = _exchange_wait(
            l_send, l_recv, l_srcs, l_lands, after, name="weights_wait", broadcast=True, own_sems=l_own)
        return gw_out.reshape(D_MODEL, D_MODEL), gw_up, g_conv_f, gw_down.reshape(D_FF, D_MODEL)

    flights = {}

    def emit(group, **grads):
        paired = ()
        if group == "in":
            *flight, token = _exchange_start_in([grads["w_a"], grads["w_z"], grads["w_b"]], _slabs_by_cols(grads["conv_a"]),
                                                name="grads_start_in")
            flights[group] = flight
            return (token,)
        if group == "ffn":
            slabs = dict(w_down=grads["w_down"].reshape(N_DEV, -1, D_MODEL), w_up=grads["w_up"], conv_f=grads["conv_f"])
            paired = (1, 2)
        else:
            slabs = {k: v.reshape(N_DEV, -1, D_MODEL) for k, v in grads.items()}
        names = list(slabs)
        *flight, token = _exchange_start([slabs[k] for k in names], paired=paired, name="grads_start_" + group)
        flights[group] = (names, flight)
        return (token,)

    loss, grad_x, g = _local_step(
        x[0], loss_target[0], norm1_w, gw_in, _cols_from_slabs(g_conv_a), a_log, dt_bias,
        gdn_norm_w, norm2_w, final_norm_w[None], late_weights, emit, start_after=(l_token,))
    got = {}

    def collect(group, after):
        names, (send_sems, recv_sems, srcs, lands) = flights[group]
        srcs, landed = _exchange_wait(send_sems, recv_sems, srcs, lands, after, name="grads_wait_" + group)
        got.update(zip(names, zip(landed, srcs)))

    def update(key, w, m, v, paired=False, **tiles):
        where = jnp.concatenate([_pair_slot(me1) if paired else me1, me1])
        return _adamw(*got[key], where, w, m, v, name="adamw_" + key, **tiles)

    in_sems, in_srcs, in_lands = flights["in"][:2], flights["in"][2], flights["in"][3]
    own_land, own_token = _own_pieces(in_srcs[:3], in_lands[0], grad_x, name="grads_own_in")
    collect("ffn", own_token)
    collect("out", own_token)
    *o_out, t1 = update("w_out", w_out[0], m_w_out[0], v_w_out[0])
    *o_up, t2 = update("w_up", t_in(w_up), t_in(m_w_up), t_in(v_w_up), paired=True, tr=176)
    o_up = [o.T for o in o_up]
    *o_down, t3 = update("w_down", w_down[0], m_w_down[0], v_w_down[0], tr=176)
    *o_cf, t4 = update("conv_f", ffn_conv_w[0], m_ffn_conv_w[0], v_ffn_conv_w[0], paired=True)
    pack = _pack_small(g["norm1"], g["norm2"], g["final"], g["gnw"], g["small"][:, 0:GDN_HEADS],
                       g["small"][:, GDN_HEADS:2 * GDN_HEADS], loss)
    small_all = _gather_direct(pack, after=(t1, t2, t3, t4), name="gather_small")
    srcs, (g_land, conv_land) = _exchange_wait_in(*in_sems, in_srcs, [own_land, in_lands[1]], small_all, name="grads_wait_in")
    got["conv_a"] = (conv_land, srcs[-1])
    o_in = [o.reshape(-1, D_MODEL // 128, 128).transpose(1, 2, 0).reshape(D_MODEL, -1) for o in _adamw_rowwise(
        g_land, me1, rows(w_in), rows(m_w_in), rows(v_w_in), name="adamw_w_in")]
    o_ca = update("conv_a", conv_qkv_w[0], m_conv_qkv_w[0], v_conv_qkv_w[0])
    (o_n1, o_n2, o_fin, o_gn, o_al, o_dt), total = _adamw_small(
        small_all, (norm1_w, norm2_w, final_norm_w[None], gdn_norm_w, a_log, dt_bias),
        (m_norm1_w, m_norm2_w, m_final_norm_w[None], m_gdn_norm_w, m_a_log, m_dt_bias),
        (v_norm1_w, v_norm2_w, v_final_norm_w[None], v_gdn_norm_w, v_a_log, v_dt_bias), name="adamw_small")
    outs = [total[0, 0], grad_x[None]]
    for k in range(4):
        outs += [o_n1[k], o_in[k][None], o_ca[k][None], o_al[k], o_dt[k], o_gn[k], o_out[k][None], o_n2[k], o_up[k][None],
                 o_cf[k][None], o_down[k][None], o_fin[k][0]]
    return tuple(outs)
```

```python
import functools

import jax
import jax.numpy as jnp
from jax import lax
from jax.experimental import pallas as pl
from jax.experimental.pallas import tpu as pltpu

F32 = jnp.float32
BF16 = jnp.bfloat16

N_DEV = 8
D_MODEL = 1024
GDN_HEADS = 4
GDN_DIM = 128
GDN_WIDTH = GDN_HEADS * GDN_DIM
GDN_CONV = 4
CHUNK = 64
CHUNKS_PER_STEP = 4
DIL_HEADS = 8
DIL_DIM = 64
DIL_WIDTH = DIL_HEADS * DIL_DIM
DIL_PAIRS = DIL_HEADS // 2
DILATIONS = (1, 4, 16)
BAND = 128
D_FF = 2816
FFN_CONV = 3
EPS = 1e-6
A_COLS = 3 * GDN_WIDTH + 128
HALO = 8

ADAM_LR = 0.001
ADAM_B1 = 0.9
ADAM_B2 = 0.999
ADAM_EPS = 1e-08
ADAM_WD = 0.01
ADAM_STEP = 10

VMEM_LIMIT_BYTES = 56 * 1024 * 1024
NEG_BIG = -1e30


def _params(sem=None):
    return pltpu.CompilerParams(dimension_semantics=sem, vmem_limit_bytes=VMEM_LIMIT_BYTES)


def _sds(shape, dtype=F32):
    return jax.ShapeDtypeStruct(shape, dtype)


def _bdot(a, b):
    return jnp.dot(a.astype(BF16), b.astype(BF16), preferred_element_type=F32)


def _bdot_nt(a, b):
    return lax.dot_general(a.astype(BF16), b.astype(BF16), (((1,), (1,)), ((), ())), preferred_element_type=F32)


def _bdot_tn(a, b):
    return lax.dot_general(a.astype(BF16), b.astype(BF16), (((0,), (0,)), ((), ())), preferred_element_type=F32)


def _split(a):
    hi = a.astype(BF16)
    lo = (a - hi.astype(F32)).astype(BF16)
    return hi, lo


def _dot3(a, b, dims):
    ah, al = _split(a)
    bh, bl = _split(b)
    d = functools.partial(lax.dot_general, dimension_numbers=(dims, ((), ())), preferred_element_type=F32)
    return d(ah, bh) + (d(al, bh) + d(ah, bl))


def _exact_tri_dot(tri, g):
    g1 = g.astype(BF16)
    r1 = g - g1.astype(F32)
    g2 = r1.astype(BF16)
    g3 = (r1 - g2.astype(F32)).astype(BF16)
    t = tri.astype(BF16)
    d = functools.partial(jnp.dot, preferred_element_type=F32)
    return d(t, g1) + (d(t, g2) + d(t, g3))


def _sigmoid(x):
    return 1.0 / (1.0 + jnp.exp(-x))


def _dsilu(x, sg):
    return sg * (1.0 + x * (1.0 - sg))


def _rms_bwd_rows(dh, x, w):
    r = lax.rsqrt(jnp.mean(x * x, axis=-1, keepdims=True) + EPS)
    xh = x * r
    gw = dh * w
    return r * (gw - xh * jnp.mean(gw * xh, axis=-1, keepdims=True)), jnp.sum(dh * xh, axis=0, keepdims=True)


def _mm(a, b, *, name, ta=False, tb=False, res=None, norm_bwd=None, after=(), out_dtype=F32, tm=512, tn=512, tk=512):
    if ta:
        K, M = a.shape
    else:
        M, K = a.shape
    if tb:
        N, Kb = b.shape
    else:
        Kb, N = b.shape
    assert K == Kb, (a.shape, b.shape)
    tm, tn, tk = min(tm, M), min(tn, N), min(tk, K)
    assert M % tm == 0 and N % tn == 0 and K % tk == 0, (name, M, N, K, tm, tn, tk)
    nk = K // tk
    dims = (((0 if ta else 1,), (1 if tb else 0,)), ((), ()))
    has_res = res is not None
    has_norm = norm_bwd is not None
    assert not has_norm or tn == N

    def body(*refs):
        a_ref, b_ref = refs[:2]
        r_ref = refs[2] if has_res else None
        if has_norm:
            x_ref, w_ref, skip_ref = refs[2 + has_res:5 + has_res]
            o_ref, dw_ref, acc_ref = refs[-3:]
        else:
            o_ref, acc_ref = refs[-2:]
        i, k = pl.program_id(0), pl.program_id(2)
        part = lax.dot_general(a_ref[...].astype(BF16), b_ref[...].astype(BF16), dims, preferred_element_type=F32)

        @pl.when(k == 0)
        def _():
            acc_ref[...] = part

        @pl.when(k > 0)
        def _():
            acc_ref[...] += part

        @pl.when(k == nk - 1)
        def _():
            r = acc_ref[...]
            if has_res:
                r = r + r_ref[...]
            if has_norm:
                dx, dw = _rms_bwd_rows(r, x_ref[...], w_ref[...])
                o_ref[...] = skip_ref[...] + dx

                @pl.when(i == 0)
                def _():
                    dw_ref[...] = dw

                @pl.when(i > 0)
                def _():
                    dw_ref[...] += dw
            else:
                o_ref[...] = r.astype(out_dtype)

    a_spec = pl.BlockSpec((tk, tm), lambda i, j, k: (k, i)) if ta else pl.BlockSpec((tm, tk), lambda i, j, k: (i, k))
    b_spec = pl.BlockSpec((tn, tk), lambda i, j, k: (j, k)) if tb else pl.BlockSpec((tk, tn), lambda i, j, k: (k, j))
    o_spec = pl.BlockSpec((tm, tn), lambda i, j, k: (i, j))
    one = pl.BlockSpec((1, tn), lambda i, j, k: (0, 0))
    in_specs = [a_spec, b_spec] + [o_spec] * has_res + ([o_spec, one, o_spec] if has_norm else []) + [ANY] * len(after)
    args = (a, b) + ((res,) if has_res else ()) + (tuple(norm_bwd) if has_norm else ()) + tuple(after)
    return pl.pallas_call(
        body, name=name, grid=(M // tm, N // tn, nk), in_specs=in_specs,
        out_specs=[o_spec, one] if has_norm else o_spec,
        out_shape=[_sds((M, N)), _sds((1, N))] if has_norm else _sds((M, N), out_dtype),
        scratch_shapes=[pltpu.VMEM((tm, tn), F32)],
        compiler_params=_params(("arbitrary" if has_norm else "parallel", "parallel", "arbitrary")),
    )(*args)


def _in_proj(x, norm_w, w_land, *, name, after=(), tm=512):
    S, D = x.shape

    def body(x_ref, nw_ref, land_ref, *rest):
        h_ref, pa_ref, pz_ref, pb_ref, *scratch = rest[len(after):]

        @pl.when(pl.program_id(0) == 0)
        def _():
            _fetch_w_in(land_ref, *scratch)

        xv = x_ref[...]
        r = lax.rsqrt(jnp.mean(xv * xv, axis=-1, keepdims=True) + EPS)
        h = (xv * r * nw_ref[...]).astype(BF16)
        h_ref[...] = h
        for w_ref, p_ref in zip(scratch[:3], (pa_ref, pz_ref, pb_ref)):
            p_ref[...] = lax.dot_general(h, w_ref[...], (((1,), (1,)), ((), ())), preferred_element_type=F32)

    row = lambda n: pl.BlockSpec((tm, n), lambda i: (i, 0))
    full = lambda a: pl.BlockSpec(a.shape, lambda i: (0, 0))
    return pl.pallas_call(
        body, name=name, grid=(S // tm,), in_specs=[row(D), full(norm_w), ANY] + [ANY] * len(after),
        out_specs=[row(D)] + [row(n) for n in _W_IN_ROWS],
        out_shape=[_sds((S, D), BF16)] + [_sds((S, n)) for n in _W_IN_ROWS],
        scratch_shapes=_w_in_scratch(D), compiler_params=_params(("arbitrary",)),
    )(x, norm_w, w_land, *after)


def _in_proj_dx(ds, w_land, x, norm_w, skip, *, name, after=(), tm=512):
    S, D = x.shape
    n = len(ds)

    def body(*refs):
        d_refs, land_ref = refs[:n], refs[n]
        x_ref, nw_ref, skip_ref = refs[n + 1:n + 4]
        o_ref, dw_ref, *scratch = refs[n + 4 + len(after):]
        w_refs = scratch[:n]
        i = pl.program_id(0)

        @pl.when(i == 0)
        def _():
            _fetch_w_in(land_ref, *scratch)

        dh = jnp.dot(d_refs[0][...], w_refs[0][...], preferred_element_type=F32)
        for d_ref, w_ref in zip(d_refs[1:], w_refs[1:]):
            dh = dh + jnp.dot(d_ref[...], w_ref[...], preferred_element_type=F32)
        dx, dw = _rms_bwd_rows(dh, x_ref[...], nw_ref[...])
        o_ref[...] = skip_ref[...] + dx

        @pl.when(i == 0)
        def _():
            dw_ref[...] = dw

        @pl.when(i > 0)
        def _():
            dw_ref[...] += dw

    row = lambda c: pl.BlockSpec((tm, c), lambda i: (i, 0))
    full = lambda a: pl.BlockSpec(a.shape, lambda i: (0, 0))
    return pl.pallas_call(
        body, name=name, grid=(S // tm,),
        in_specs=[row(d.shape[1]) for d in ds] + [ANY, row(D), full(norm_w), row(D)] + [ANY] * len(after),
        out_specs=[row(D), pl.BlockSpec((1, D), lambda i: (0, 0))], out_shape=[_sds((S, D)), _sds((1, D))],
        scratch_shapes=_w_in_scratch(D), compiler_params=_params(("arbitrary",)),
    )(*ds, w_land, x, norm_w, skip, *after)


def _out_proj_norm(a, w, x, norm_w, *, name, tm=512):
    S, D = x.shape

    def body(a_ref, w_ref, x_ref, nw_ref, x1_ref, h_ref):
        x1 = x_ref[...] + jnp.dot(a_ref[...], w_ref[...], preferred_element_type=F32)
        x1_ref[...] = x1
        r = lax.rsqrt(jnp.mean(x1 * x1, axis=-1, keepdims=True) + EPS)
        h_ref[...] = (x1 * r * nw_ref[...]).astype(BF16)

    row = pl.BlockSpec((tm, D), lambda i: (i, 0))
    return pl.pallas_call(
        body, name=name, grid=(S // tm,),
        in_specs=[pl.BlockSpec((tm, a.shape[1]), lambda i: (i, 0)), pl.BlockSpec(w.shape, lambda i: (0, 0)), row,
                  pl.BlockSpec((1, D), lambda i: (0, 0))],
        out_specs=[row, row], out_shape=[_sds((S, D)), _sds((S, D), BF16)], compiler_params=_params(("parallel",)),
    )(a, w, x, norm_w)


def _shifted(x, start, n):
    aligned = -(-start // HALO) * HALO
    assert aligned + n <= x.shape[0], (start, n, x.shape)
    return (x if aligned == start else pltpu.roll(x, aligned - start, axis=0))[aligned:aligned + n]


def _conv_rows(prev, cur, w, taps):
    n = cur.shape[0]
    xs = jnp.concatenate([prev, cur], axis=0)
    base = HALO - (taps - 1)
    out = _shifted(xs, base, n) * w[0:1]
    for i in range(1, taps):
        out = out + _shifted(xs, base + i, n) * w[i:i + 1]
    return out


def _conv_rows_bwd(cur_d, next_d, prev_x, cur_x, w, taps):
    n = cur_d.shape[0]
    ds = jnp.concatenate([cur_d, next_d], axis=0)
    dx = _shifted(ds, taps - 1, n) * w[0:1]
    for i in range(1, taps):
        dx = dx + _shifted(ds, taps - 1 - i, n) * w[i:i + 1]
    xs = jnp.concatenate([prev_x, cur_x], axis=0)
    base = HALO - (taps - 1)
    dws = [jnp.sum(cur_d * _shifted(xs, base + i, n), axis=0, keepdims=True) for i in range(taps)]
    return dx, jnp.concatenate(dws, axis=0)


def _halo_specs(tm, width, col, nblk):
    per = tm // HALO
    prev = pl.BlockSpec((HALO, width), lambda i, *_: (jnp.maximum(i * per - 1, 0), col))
    nxt = pl.BlockSpec((HALO, width), lambda i, *_: (jnp.minimum((i + 1) * per, nblk * per - 1), col))
    return prev, nxt


def _softplus(x):
    return jnp.maximum(x, 0.0) + jnp.log1p(jnp.exp(-jnp.abs(x)))


def _chunk_tri(tm, upper=False):
    r = lax.broadcasted_iota(jnp.int32, (tm, tm), 0)
    c = lax.broadcasted_iota(jnp.int32, (tm, tm), 1)
    same = lax.div(r, CHUNK) == lax.div(c, CHUNK)
    order = (c >= r) if upper else (c <= r)
    return jnp.where(same & order, 1.0, 0.0)


def _gdn_prep_fwd(proj_a, conv_w, a_log, dt_bias, *, name, tm=256):
    S = proj_a.shape[0]
    nblk = S // tm
    W3 = 3 * GDN_WIDTH

    def body(cur_ref, prev_ref, ba_ref, cw_ref, al_ref, dt_ref, qn_ref, kn_ref, v_ref, gcb_ref, bb_ref):
        i = pl.program_id(0)
        prev = jnp.where(i > 0, prev_ref[...], 0.0)
        c = _conv_rows(prev, cur_ref[...], cw_ref[...], GDN_CONV)
        a = c * _sigmoid(c)
        ba = ba_ref[...]
        lane = lax.broadcasted_iota(jnp.int32, (tm, 128), 1)
        g4 = jnp.zeros((tm, 128), F32)
        for h in range(GDN_HEADS):
            sl = slice(GDN_DIM * h, GDN_DIM * (h + 1))
            qh = a[:, GDN_DIM * h:GDN_DIM * (h + 1)]
            kh = a[:, GDN_WIDTH + GDN_DIM * h:GDN_WIDTH + GDN_DIM * (h + 1)]
            qn_ref[:, sl] = qh * (lax.rsqrt(jnp.sum(qh * qh, axis=-1, keepdims=True) + EPS) * (GDN_DIM ** -0.5))
            kn_ref[:, sl] = kh * lax.rsqrt(jnp.sum(kh * kh, axis=-1, keepdims=True) + EPS)
            beta = _sigmoid(ba[:, h:h + 1])
            bb_ref[:, sl] = jnp.broadcast_to(beta, (tm, GDN_DIM))
            g = -jnp.exp(al_ref[0:1, h:h + 1]) * _softplus(ba[:, GDN_HEADS + h:GDN_HEADS + h + 1] + dt_ref[0:1, h:h + 1])
            g4 = jnp.where(lane == h, g, g4)
        v_ref[...] = a[:, 2 * GDN_WIDTH:]
        gc = _exact_tri_dot(_chunk_tri(tm), g4)
        for h in range(GDN_HEADS):
            gcb_ref[:, GDN_DIM * h:GDN_DIM * (h + 1)] = jnp.broadcast_to(gc[:, h:h + 1], (tm, GDN_DIM))

    prev_spec, _ = _halo_specs(tm, W3, 0, nblk)
    row = pl.BlockSpec((tm, GDN_WIDTH), lambda i: (i, 0))
    small = lambda a: pl.BlockSpec(a.shape, lambda i: (0, 0))
    return pl.pallas_call(
        body, name=name, grid=(nblk,),
        in_specs=[pl.BlockSpec((tm, W3), lambda i: (i, 0)), prev_spec,
                  pl.BlockSpec((tm, 128), lambda i: (i, W3 // 128)), small(conv_w), small(a_log), small(dt_bias)],
        out_specs=[row] * 5, out_shape=[_sds((S, GDN_WIDTH))] * 5, compiler_params=_params(("parallel",)),
    )(proj_a, proj_a, proj_a, conv_w, a_log, dt_bias)


GDN_STACK = GDN_HEADS * CHUNK


def _stack(ref, rows):
    return jnp.concatenate([ref[rows, GDN_DIM * h:GDN_DIM * (h + 1)] for h in range(GDN_HEADS)], axis=0)


def _unstack_to(ref, rows, x):
    for h in range(GDN_HEADS):
        ref[rows, GDN_DIM * h:GDN_DIM * (h + 1)] = x[CHUNK * h:CHUNK * (h + 1)].astype(ref.dtype)


def _stack_masks():
    r = lax.broadcasted_iota(jnp.int32, (GDN_STACK, GDN_STACK), 0)
    c = lax.broadcasted_iota(jnp.int32, (GDN_STACK, GDN_STACK), 1)
    same = (r & -CHUNK) == (c & -CHUNK)
    return same & (r >= c), same & (r > c), r == c


def _stack_decay(gs, bs, incl):
    g2 = jnp.concatenate([gs, gs], axis=1)
    diff = g2 - g2.T
    dec = jnp.where(incl, jnp.exp(jnp.where(incl, diff, 0.0)), 0.0)
    return dec, jnp.concatenate([bs, bs], axis=1).T


def _head_mask():
    r = lax.broadcasted_iota(jnp.int32, (GDN_STACK, GDN_WIDTH), 0)
    c = lax.broadcasted_iota(jnp.int32, (GDN_STACK, GDN_WIDTH), 1)
    return (r & -CHUNK) * (GDN_DIM // CHUNK) == (c & -GDN_DIM)


def _head_spread(x):
    return jnp.where(_head_mask(), jnp.concatenate([x] * GDN_HEADS, axis=1), 0.0)


def _head_diag(x):
    xm = jnp.where(_head_mask(), x, 0.0)
    out = xm[:, 0:GDN_DIM]
    for h in range(1, GDN_HEADS):
        out = out + xm[:, GDN_DIM * h:GDN_DIM * (h + 1)]
    return out


def _last_rows(gs, n):
    return jnp.concatenate([jnp.broadcast_to(gs[CHUNK * (h + 1) - 1:CHUNK * (h + 1)], (n, GDN_DIM)) for h in range(GDN_HEADS)], axis=0)


def _gdn_chunk_fwd(qn, kn, v, gcb, bb, *, name):
    S = qn.shape[0]

    def body(qn_ref, kn_ref, v_ref, gcb_ref, bb_ref, uv_ref, wk_ref, at_ref, t_ref, wkb_ref, qdb_ref, keb_ref):
        incl, strict, diag = _stack_masks()
        for c in range(CHUNKS_PER_STEP):
            rows = slice(CHUNK * c, CHUNK * (c + 1))
            srows = slice(GDN_STACK * c, GDN_STACK * (c + 1))
            q, k, vv, gs, bs = [_stack(r, rows) for r in (qn_ref, kn_ref, v_ref, gcb_ref, bb_ref)]
            dec, bt = _stack_decay(gs, bs, incl)
            p = -jnp.where(strict, dec * _bdot_nt(k, k) * bt, 0.0)
            t = jnp.where(diag, 1.0, 0.0) + p
            for _ in range(5):
                p = _bdot(p, p)
                t = t + _bdot(t, p)
            sol = _dot3(t, jnp.concatenate([vv, jnp.exp(gs) * k], axis=1), ((1,), (0,)))
            _unstack_to(uv_ref, rows, sol[:, :GDN_DIM])
            _unstack_to(wk_ref, rows, sol[:, GDN_DIM:])
            at_ref[srows, :] = dec * _bdot_nt(q, k) * bt
            t_ref[srows, :] = t
            wkb_ref[srows, :] = _head_spread(sol[:, GDN_DIM:]).astype(BF16)
            qdb_ref[srows, :] = _head_spread(q * jnp.exp(gs)).astype(BF16)
            keb_ref[srows, :] = _head_spread(k * jnp.exp(_last_rows(gs, CHUNK) - gs) * bs).astype(BF16)

    step = CHUNKS_PER_STEP * CHUNK
    row = pl.BlockSpec((step, GDN_WIDTH), lambda n: (n, 0))
    sq = pl.BlockSpec((CHUNKS_PER_STEP * GDN_STACK, GDN_STACK), lambda n: (n, 0))
    wide = pl.BlockSpec((CHUNKS_PER_STEP * GDN_STACK, GDN_WIDTH), lambda n: (n, 0))
    nsq = S // CHUNK * GDN_STACK
    return pl.pallas_call(
        body, name=name, grid=(S // step,), in_specs=[row] * 5, out_specs=[row, row, sq, sq, wide, wide, wide],
        out_shape=[_sds((S, GDN_WIDTH)), _sds((S, GDN_WIDTH)), _sds((nsq, GDN_STACK)), _sds((nsq, GDN_STACK))]
        + [_sds((nsq, GDN_WIDTH), BF16)] * 3,
        compiler_params=_params(("parallel",)),
    )(qn, kn, v, gcb, bb)


SCAN_CHUNKS = 8


def _gdn_scan_fwd(uv, at, wkb, qdb, keb, gcb, proj_z, gnw, *, name):
    S = uv.shape[0]
    nc = S // CHUNK

    def body(uv_ref, at_ref, wkb_ref, qdb_ref, keb_ref, gcb_ref, z_ref, gnw_ref, o_ref, u_ref, sp_ref, oa_ref, st_ref):
        n = pl.program_id(0)

        @pl.when(n == 0)
        def _():
            st_ref[...] = jnp.zeros_like(st_ref)

        for c in range(SCAN_CHUNKS):
            rows = slice(CHUNK * c, CHUNK * (c + 1))
            srows = slice(GDN_STACK * c, GDN_STACK * (c + 1))
            st = st_ref[...]
            sp_ref[GDN_WIDTH * c:GDN_WIDTH * (c + 1), :] = st
            uv, gs, z = [_stack(r, rows) for r in (uv_ref, gcb_ref, z_ref)]
            u = uv - _bdot(wkb_ref[srows, :], st)
            o = _bdot(qdb_ref[srows, :], st) + _bdot(at_ref[srows, :], u)
            st_ref[...] = jnp.exp(_last_rows(gs, GDN_DIM)) * st + _bdot_tn(keb_ref[srows, :], u)
            _unstack_to(u_ref, rows, u)
            _unstack_to(o_ref, rows, o)
            r = lax.rsqrt(jnp.mean(o * o, axis=-1, keepdims=True) + EPS)
            oa = o * r * gnw_ref[...] * (z * _sigmoid(z))
            oa_ref[rows, :] = jnp.concatenate([oa[CHUNK * h:CHUNK * (h + 1)] for h in range(GDN_HEADS)], axis=1).astype(BF16)

    row = pl.BlockSpec((SCAN_CHUNKS * CHUNK, GDN_WIDTH), lambda n: (n, 0))
    sq = pl.BlockSpec((SCAN_CHUNKS * GDN_STACK, GDN_STACK), lambda n: (n, 0))
    wide = pl.BlockSpec((SCAN_CHUNKS * GDN_STACK, GDN_WIDTH), lambda n: (n, 0))
    return pl.pallas_call(
        body, name=name, grid=(nc // SCAN_CHUNKS,),
        in_specs=[row, sq, wide, wide, wide, row, row, pl.BlockSpec((1, GDN_DIM), lambda n: (0, 0))],
        out_specs=[row, row, pl.BlockSpec((SCAN_CHUNKS * GDN_WIDTH, GDN_DIM), lambda n: (n, 0)), row],
        out_shape=[_sds((S, GDN_WIDTH)), _sds((S, GDN_WIDTH)), _sds((nc * GDN_WIDTH, GDN_DIM)), _sds((S, 2 * GDN_WIDTH), BF16)],
        scratch_shapes=[pltpu.VMEM((GDN_WIDTH, GDN_DIM), F32)],
        compiler_params=_params(("arbitrary",)),
    )(uv, at, wkb, qdb, keb, gcb, proj_z, gnw)


def _gdn_scan_bwd(d_oab, o, proj_z, gnw, sp, u, at, wkb, qdb, keb, gcb, *, name, after=()):
    S = o.shape[0]
    nc = S // CHUNK
    ns = nc // SCAN_CHUNKS

    def body(do_ref, o_ref, z_ref, gnw_ref, sp_ref, u_ref, at_ref, wkb_ref, qdb_ref, keb_ref, gcb_ref, *rest):
        dz_ref, dgn_ref, du_ref, dwk_ref, dat_ref, dqd_ref, dke_ref, dgl_ref, ds_ref = rest[len(after):]
        n = pl.program_id(0)

        @pl.when(n == 0)
        def _():
            ds_ref[...] = jnp.zeros_like(ds_ref)
            dgn_ref[...] = jnp.zeros_like(dgn_ref)

        gw = gnw_ref[...]
        for c in reversed(range(SCAN_CHUNKS)):
            rows = slice(CHUNK * c, CHUNK * (c + 1))
            srows = slice(GDN_STACK * c, GDN_STACK * (c + 1))
            d_oa, oo, z, uu, gs = [_stack(r, rows) for r in (do_ref, o_ref, z_ref, u_ref, gcb_ref)]
            sg = _sigmoid(z)
            r = lax.rsqrt(jnp.mean(oo * oo, axis=-1, keepdims=True) + EPS)
            xh = oo * r
            dy = d_oa * (z * sg)
            _unstack_to(dz_ref, rows, d_oa * (xh * gw) * _dsilu(z, sg))
            dgn_ref[...] += jnp.sum(dy * xh, axis=0, keepdims=True)
            dxh = dy * gw
            do = r * (dxh - xh * jnp.mean(dxh * xh, axis=-1, keepdims=True))

            st = sp_ref[GDN_WIDTH * c:GDN_WIDTH * (c + 1), :]
            dst = ds_ref[...]
            ge = jnp.exp(_last_rows(gs, GDN_DIM))
            _unstack_to(dqd_ref, rows, _head_diag(_bdot_nt(do, st)))
            dat_ref[srows, :] = _bdot_nt(do, uu)
            du = _bdot_tn(at_ref[srows, :], do) + _bdot(keb_ref[srows, :], dst)
            _unstack_to(dke_ref, rows, _head_diag(_bdot_nt(uu, dst)))
            prod = dst * st
            for h in range(GDN_HEADS):
                blk = prod[GDN_DIM * h:GDN_DIM * (h + 1)]
                dge = jnp.sum(jnp.sum(blk, axis=1, keepdims=True), axis=0, keepdims=True)
                dgl_ref[c, :, GDN_DIM * h:GDN_DIM * (h + 1)] = jnp.broadcast_to(dge * ge[GDN_DIM * h:GDN_DIM * h + 1], (8, GDN_DIM))
            ds_ref[...] = _bdot_tn(qdb_ref[srows, :], do) + ge * dst - _bdot_tn(wkb_ref[srows, :], du)
            _unstack_to(du_ref, rows, du)
            _unstack_to(dwk_ref, rows, -_head_diag(_bdot_nt(du, st)))

    rev = lambda n: (ns - 1 - n, 0)
    row = pl.BlockSpec((SCAN_CHUNKS * CHUNK, GDN_WIDTH), rev)
    sq = pl.BlockSpec((SCAN_CHUNKS * GDN_STACK, GDN_STACK), rev)
    wide = pl.BlockSpec((SCAN_CHUNKS * GDN_STACK, GDN_WIDTH), rev)
    one = pl.BlockSpec((1, GDN_DIM), lambda n: (0, 0))
    return pl.pallas_call(
        body, name=name, grid=(ns,),
        in_specs=[row, row, row, one, pl.BlockSpec((SCAN_CHUNKS * GDN_WIDTH, GDN_DIM), rev), row, sq, wide, wide, wide, row]
        + [ANY] * len(after),
        out_specs=[row, one, row, row, sq, row, row, pl.BlockSpec((SCAN_CHUNKS, 8, GDN_WIDTH), lambda n: (ns - 1 - n, 0, 0))],
        out_shape=[_sds((S, GDN_WIDTH), BF16), _sds((1, GDN_DIM)), _sds((S, GDN_WIDTH)), _sds((S, GDN_WIDTH)),
                   _sds((nc * GDN_STACK, GDN_STACK)), _sds((S, GDN_WIDTH)), _sds((S, GDN_WIDTH)), _sds((nc, 8, GDN_WIDTH))],
        scratch_shapes=[pltpu.VMEM((GDN_WIDTH, GDN_DIM), F32)],
        compiler_params=_params(("arbitrary",)),
    )(d_oab, o, proj_z, gnw, sp, u, at, wkb, qdb, keb, gcb, *after)


def _gdn_chunk_bwd(qn, kn, gcb, bb, tmat, uv, wk, du, dwk, dat, dqd, dke, dgl, *, name):
    S = qn.shape[0]

    def body(qn_ref, kn_ref, gcb_ref, bb_ref, t_ref, uv_ref, wk_ref, du_ref, dwk_ref, dat_ref, dqd_ref, dke_ref,
             dgl_ref, dq_ref, dk_ref, dv_ref, dg_ref, dbeta_ref):
        incl, strict, _ = _stack_masks()
        lane = lax.broadcasted_iota(jnp.int32, (CHUNK, 128), 1)
        rowi = lax.broadcasted_iota(jnp.int32, (CHUNK, 1), 0)
        rsum = lambda x: jnp.sum(x, axis=-1, keepdims=True)
        for c in range(CHUNKS_PER_STEP):
            rows = slice(CHUNK * c, CHUNK * (c + 1))
            srows = slice(GDN_STACK * c, GDN_STACK * (c + 1))
            q, k, gs, bs, uv, wk, du, dwk, dqd, dke = [
                _stack(r, rows) for r in (qn_ref, kn_ref, gcb_ref, bb_ref, uv_ref, wk_ref, du_ref, dwk_ref, dqd_ref, dke_ref)]
            dec, bt = _stack_decay(gs, bs, incl)
            kk = _bdot_nt(k, k)
            qk = _bdot_nt(q, k)
            d_rhs = _dot3(t_ref[srows, :], jnp.concatenate([du, dwk], axis=1), ((0,), (0,)))
            sol = jnp.concatenate([uv, wk], axis=1)
            d_l = jnp.where(strict, -_dot3(d_rhs, sol, ((1,), (1,))), 0.0)
            d_a = jnp.where(incl, dat_ref[srows, :], 0.0)
            gam = jnp.exp(gs)
            e = jnp.exp(_last_rows(gs, CHUNK) - gs)
            d_gk = d_rhs[:, GDN_DIM:]
            ml = d_l * dec * bt
            ma = d_a * dec * bt
            _unstack_to(dq_ref, rows, _bdot(ma, k) + dqd * gam)
            _unstack_to(dk_ref, rows, _bdot(ml + ml.T, k) + _bdot_tn(ma, q) + d_gk * gam + dke * (e * bs))
            _unstack_to(dv_ref, rows, d_rhs[:, :GDN_DIM])
            wb = d_l * dec * kk + d_a * dec * qk
            ew = wb * bt
            s_ke = rsum(dke * k * (e * bs))
            dbeta = rsum(wb.T) + rsum(dke * k * e)
            dgc = rsum(ew) - rsum(ew.T) + rsum(dqd * q * gam) + rsum(d_gk * k * gam) - s_ke
            dgc4 = jnp.zeros((CHUNK, 128), F32)
            db4 = jnp.zeros((CHUNK, 128), F32)
            for h in range(GDN_HEADS):
                hr = slice(CHUNK * h, CHUNK * (h + 1))
                tail = jnp.sum(s_ke[hr], axis=0, keepdims=True) + dgl_ref[c, 0:1, GDN_DIM * h:GDN_DIM * h + 1]
                dgc4 = jnp.where(lane == h, dgc[hr] + jnp.where(rowi == CHUNK - 1, tail, 0.0), dgc4)
                db4 = jnp.where(lane == h, dbeta[hr], db4)
            dg_ref[rows, :] = _exact_tri_dot(_chunk_tri(CHUNK, upper=True), dgc4)
            dbeta_ref[rows, :] = db4

    step = CHUNKS_PER_STEP * CHUNK
    row = pl.BlockSpec((step, GDN_WIDTH), lambda n: (n, 0))
    sq = pl.BlockSpec((CHUNKS_PER_STEP * GDN_STACK, GDN_STACK), lambda n: (n, 0))
    col = pl.BlockSpec((step, 128), lambda n: (n, 0))
    return pl.pallas_call(
        body, name=name, grid=(S // step,),
        in_specs=[row] * 4 + [sq, row, row, row, row, sq, row, row,
                              pl.BlockSpec((CHUNKS_PER_STEP, 8, GDN_WIDTH), lambda n: (n, 0, 0))],
        out_specs=[row, row, row, col, col],
        out_shape=[_sds((S, GDN_WIDTH))] * 3 + [_sds((S, 128))] * 2, compiler_params=_params(("parallel",)),
    )(qn, kn, gcb, bb, tmat, uv, wk, du, dwk, dat, dqd, dke, dgl)


def _gdn_prep_bwd(dqn, dkn, dv, dg, dbeta, proj_a, conv_w, a_log, dt_bias, *, name, tm=256):
    S = proj_a.shape[0]
    nblk = S // tm
    W3 = 3 * GDN_WIDTH

    def body(dqn_ref, dkn_ref, dv_ref, dg_ref, dbeta_ref, cur_ref, prev_ref, ba_ref, cw_ref, al_ref, dt_ref,
             dc_ref, dba_ref, sm_ref):
        i = pl.program_id(0)
        prev = jnp.where(i > 0, prev_ref[...], 0.0)
        c = _conv_rows(prev, cur_ref[...], cw_ref[...], GDN_CONV)
        sg = _sigmoid(c)
        a = c * sg
        dsl = _dsilu(c, sg)
        ba = ba_ref[...]
        lane = lax.broadcasted_iota(jnp.int32, (tm, 128), 1)
        lane1 = lax.broadcasted_iota(jnp.int32, (1, 128), 1)
        dba = jnp.zeros((tm, 128), F32)
        sm = jnp.zeros((1, 128), F32)
        for h in range(GDN_HEADS):
            sl = slice(GDN_DIM * h, GDN_DIM * (h + 1))
            ks = slice(GDN_WIDTH + GDN_DIM * h, GDN_WIDTH + GDN_DIM * (h + 1))
            qh, kh = a[:, sl], a[:, ks]
            rq = lax.rsqrt(jnp.sum(qh * qh, axis=-1, keepdims=True) + EPS)
            rk = lax.rsqrt(jnp.sum(kh * kh, axis=-1, keepdims=True) + EPS)
            qhat, khat = qh * rq, kh * rk
            dyq = dqn_ref[:, sl] * (GDN_DIM ** -0.5)
            dyk = dkn_ref[:, sl]
            dq = rq * (dyq - qhat * jnp.sum(dyq * qhat, axis=-1, keepdims=True))
            dk = rk * (dyk - khat * jnp.sum(dyk * khat, axis=-1, keepdims=True))
            dc_ref[:, sl] = dq * dsl[:, sl]
            dc_ref[:, ks] = dk * dsl[:, ks]
            beta = _sigmoid(ba[:, h:h + 1])
            db = dbeta_ref[:, h:h + 1] * beta * (1.0 - beta)
            aneg = -jnp.exp(al_ref[0:1, h:h + 1])
            xa = ba[:, GDN_HEADS + h:GDN_HEADS + h + 1] + dt_ref[0:1, h:h + 1]
            dgh = dg_ref[:, h:h + 1]
            dxa = dgh * aneg * _sigmoid(xa)
            dba = jnp.where(lane == h, db, dba)
            dba = jnp.where(lane == GDN_HEADS + h, dxa, dba)
            d_alog = jnp.sum(dgh * _softplus(xa), axis=0, keepdims=True) * aneg
            sm = jnp.where(lane1 == h, d_alog, sm)
            sm = jnp.where(lane1 == GDN_HEADS + h, jnp.sum(dxa, axis=0, keepdims=True), sm)
        vs = slice(2 * GDN_WIDTH, W3)
        dc_ref[:, vs] = dv_ref[...] * dsl[:, vs]
        dba_ref[...] = dba

        @pl.when(i == 0)
        def _():
            sm_ref[...] = sm

        @pl.when(i > 0)
        def _():
            sm_ref[...] += sm

    prev_spec, _ = _halo_specs(tm, W3, 0, nblk)
    row = pl.BlockSpec((tm, GDN_WIDTH), lambda i: (i, 0))
    col = pl.BlockSpec((tm, 128), lambda i: (i, 0))
    small = lambda a: pl.BlockSpec(a.shape, lambda i: (0, 0))
    return pl.pallas_call(
        body, name=name, grid=(nblk,),
        in_specs=[row, row, row, col, col, pl.BlockSpec((tm, W3), lambda i: (i, 0)), prev_spec,
                  pl.BlockSpec((tm, 128), lambda i: (i, W3 // 128)), small(conv_w), small(a_log), small(dt_bias)],
        out_specs=[pl.BlockSpec((tm, W3), lambda i: (i, 0)), col, pl.BlockSpec((1, 128), lambda i: (0, 0))],
        out_shape=[_sds((S, W3)), _sds((S, 128)), _sds((1, 128))], compiler_params=_params(("arbitrary",)),
    )(dqn, dkn, dv, dg, dbeta, proj_a, proj_a, proj_a, conv_w, a_log, dt_bias)


def _gdn_conv_bwd(dc, dba, proj_a, conv_w, *, name, tm=256):
    S = proj_a.shape[0]
    nblk = S // tm
    W3 = 3 * GDN_WIDTH

    def body(dc_ref, dnext_ref, dba_ref, cur_ref, prev_ref, cw_ref, da_ref, dcw_ref):
        i = pl.program_id(0)
        prev = jnp.where(i > 0, prev_ref[...], 0.0)
        nxt = jnp.where(i < nblk - 1, dnext_ref[...], 0.0)
        dx, dw = _conv_rows_bwd(dc_ref[...], nxt, prev, cur_ref[...], cw_ref[...], GDN_CONV)
        da_ref[:, 0:W3] = dx.astype(BF16)
        da_ref[:, W3:] = dba_ref[...].astype(BF16)

        @pl.when(i == 0)
        def _():
            dcw_ref[...] = dw

        @pl.when(i > 0)
        def _():
            dcw_ref[...] += dw

    prev_spec, next_spec = _halo_specs(tm, W3, 0, nblk)
    wide = pl.BlockSpec((tm, W3), lambda i: (i, 0))
    return pl.pallas_call(
        body, name=name, grid=(nblk,),
        in_specs=[wide, next_spec, pl.BlockSpec((tm, 128), lambda i: (i, 0)), wide, prev_spec,
                  pl.BlockSpec(conv_w.shape, lambda i: (0, 0))],
        out_specs=[pl.BlockSpec((tm, A_COLS), lambda i: (i, 0)), pl.BlockSpec(conv_w.shape, lambda i: (0, 0))],
        out_shape=[_sds((S, A_COLS), BF16), _sds(conv_w.shape)], compiler_params=_params(("arbitrary",)),
    )(dc, dc, dba, proj_a, proj_a, conv_w)


def _band_mask(nk):
    i = lax.broadcasted_iota(jnp.int32, (2 * BAND, nk), 0) & (BAND - 1)
    j = lax.broadcasted_iota(jnp.int32, (2 * BAND, nk), 1)
    if nk == BAND:
        return j <= i
    return (j >= i) & (j <= i + BAND)


def _stack_heads(x, lo):
    return jnp.concatenate([jnp.where(lo, x, 0.0), jnp.where(lo, 0.0, x)], axis=0)


def _stack_cols(x):
    return jnp.concatenate([x[:, 0:1], x[:, DIL_DIM:DIL_DIM + 1]], axis=0)


def _unstack(x, lo):
    return jnp.where(lo, x[0:BAND], x[BAND:2 * BAND])


def _rows(start, size, stride):
    return pl.ds(start, size) if stride == 1 else pl.ds(start, size, stride=stride)


ATTN_LANES = 4


def _attn_blocks(S, visit_many, lanes=ATTN_LANES):
    for d in DILATIONS:
        nb = S // (d * BAND)
        if d == 1:
            half = nb // 2
            visit_many(d, [(0, 0, True), (0, half, False)])

            def pair(n, c):
                visit_many(1, [(0, n, False), (0, n + half, False)])
                return c
            lax.fori_loop(1, half, pair, 0)
        elif nb > 1:
            for r0 in range(0, d, lanes):
                visit_many(d, [(r0 + t, 0, True) for t in range(lanes)])

                def column(n, c, d=d, r0=r0):
                    visit_many(d, [(r0 + t, n, False) for t in range(lanes)])
                    return c
                lax.fori_loop(1, nb, column, 0)
        else:
            def group(g, c, d=d):
                visit_many(d, [(g * lanes + t, 0, True) for t in range(lanes)])
                return c
            lax.fori_loop(0, d // lanes, group, 0)


def _attn_fwd(proj_b, oab, *, name):
    S = proj_b.shape[0]
    scale = DIL_DIM ** -0.5

    def body(q_ref, k_ref, v_ref, oab_in_ref, ob_ref, lse_ref, m_ref, l_ref, acc_ref):
        del oab_in_ref
        lane = lax.broadcasted_iota(jnp.int32, (BAND, 128), 1)
        lo = lane < DIL_DIM
        m_ref[...] = jnp.full_like(m_ref, NEG_BIG)
        l_ref[...] = jnp.zeros_like(l_ref)
        acc_ref[...] = jnp.zeros_like(acc_ref)

        def load(d, r, n, first):
            nk = BAND if first else 2 * BAND
            qrows = _rows(r + n * (BAND * d), BAND, d)
            krows = _rows(r if first else r + (n - 1) * (BAND * d), nk, d)
            return dict(nk=nk, qrows=qrows, q=q_ref[qrows, :] * scale, k=k_ref[krows, :].astype(BF16),
                        v=v_ref[krows, :].astype(BF16), m=m_ref[qrows, :], l=l_ref[qrows, :], acc=acc_ref[qrows, :])

        def compute(b):
            q, k, v = b["q"], b["k"], b["v"]
            s = jnp.where(_band_mask(b["nk"]), _bdot_nt(_stack_heads(q, lo), k), NEG_BIG)
            m_old = _stack_cols(b["m"])
            m_new = jnp.maximum(m_old, jnp.max(s, axis=-1, keepdims=True))
            p = jnp.exp(s - m_new)
            alpha = _unstack(jnp.exp(m_old - m_new), lo)
            l_new = alpha * b["l"] + _unstack(jnp.sum(p, axis=-1, keepdims=True), lo)
            return _unstack(m_new, lo), l_new, alpha * b["acc"] + _unstack(_bdot(p, v), lo)

        def visit_many(d, blocks):
            loaded = [load(d, *blk) for blk in blocks]
            done = [compute(b) for b in loaded]
            for b, (m_new, l_new, acc_new) in zip(loaded, done):
                m_ref[b["qrows"], :] = m_new
                l_ref[b["qrows"], :] = l_new
                acc_ref[b["qrows"], :] = acc_new

        _attn_blocks(S, visit_many)
        ob_ref[...] = (acc_ref[...] / l_ref[...]).astype(BF16)
        lse_ref[...] = m_ref[...] + jnp.log(l_ref[...])

    part = lambda t: pl.BlockSpec((S, 128), lambda p: (0, 3 * p + t))
    return pl.pallas_call(
        body, name=name, grid=(DIL_PAIRS,),
        in_specs=[part(0), part(1), part(2), pl.BlockSpec(memory_space=pl.ANY)],
        out_specs=[pl.BlockSpec((S, 128), lambda p: (0, GDN_WIDTH // 128 + p)), pl.BlockSpec((S, 128), lambda p: (0, p))],
        out_shape=[_sds(oab.shape, BF16), _sds((S, DIL_WIDTH))],
        scratch_shapes=[pltpu.VMEM((S, 128), F32)] * 3, input_output_aliases={3: 0},
        compiler_params=_params(("parallel",)),
    )(proj_b, proj_b, proj_b, oab)


def _attn_bwd(proj_b, oab, d_oab, lse, *, name):
    S = proj_b.shape[0]
    scale = DIL_DIM ** -0.5

    def body(q_ref, k_ref, v_ref, o_ref, do_ref, lse_ref, dqkv_ref, dq_ref, dk_ref, dv_ref, delta_ref):
        lane = lax.broadcasted_iota(jnp.int32, (BAND, 128), 1)
        lo = lane < DIL_DIM
        dq_ref[...] = jnp.zeros_like(dq_ref)
        dk_ref[...] = jnp.zeros_like(dk_ref)
        dv_ref[...] = jnp.zeros_like(dv_ref)
        prod = do_ref[...] * o_ref[...].astype(F32)
        lo_all = lax.broadcasted_iota(jnp.int32, (S, 128), 1) < DIL_DIM
        delta_ref[...] = jnp.where(lo_all, jnp.sum(jnp.where(lo_all, prod, 0.0), axis=-1, keepdims=True),
                                   jnp.sum(jnp.where(lo_all, 0.0, prod), axis=-1, keepdims=True))

        def load(d, r, n, first):
            nk = BAND if first else 2 * BAND
            qrows = _rows(r + n * (BAND * d), BAND, d)
            krows = _rows(r if first else r + (n - 1) * (BAND * d), nk, d)
            return dict(nk=nk, qrows=qrows, krows=krows, q=q_ref[qrows, :] * scale, k=k_ref[krows, :], v=v_ref[krows, :],
                        do=do_ref[qrows, :], delta=delta_ref[qrows, :], lse=lse_ref[qrows, :],
                        dq=dq_ref[qrows, :], dk=dk_ref[krows, :], dv=dv_ref[krows, :])

        def compute(b):
            q, k, v, do = b["q"], b["k"], b["v"], b["do"]
            qs, dos = _stack_heads(q, lo), _stack_heads(do, lo)
            p = jnp.where(_band_mask(b["nk"]), jnp.exp(_bdot_nt(qs, k) - _stack_cols(b["lse"])), 0.0)
            ds = p * (_bdot_nt(dos, v) - _stack_cols(b["delta"]))
            dq = b["dq"] + _unstack(_bdot(ds, k), lo) * scale
            return dq, b["dk"] + _bdot_tn(ds, qs), b["dv"] + _bdot_tn(p, dos)

        def visit_many(d, blocks):
            loaded = [load(d, *blk) for blk in blocks]
            done = [compute(b) for b in loaded]
            for b, (dq, dk, dv) in zip(loaded, done):
                dq_ref[b["qrows"], :] = dq
                dk_ref[b["krows"], :] = dk
                dv_ref[b["krows"], :] = dv

        _attn_blocks(S, visit_many, lanes=2)
        dqkv_ref[:, 0:128] = dq_ref[...].astype(BF16)
        dqkv_ref[:, 128:256] = dk_ref[...].astype(BF16)
        dqkv_ref[:, 256:384] = dv_ref[...].astype(BF16)

    half = lambda p: (0, GDN_WIDTH // 128 + p)
    part = lambda t: pl.BlockSpec((S, 128), lambda p: (0, 3 * p + t))
    return pl.pallas_call(
        body, name=name, grid=(DIL_PAIRS,),
        in_specs=[part(0), part(1), part(2), pl.BlockSpec((S, 128), half), pl.BlockSpec((S, 128), half),
                  pl.BlockSpec((S, 128), lambda p: (0, p))],
        out_specs=pl.BlockSpec((S, 384), lambda p: (0, p)), out_shape=_sds((S, 3 * DIL_WIDTH), BF16),
        scratch_shapes=[pltpu.VMEM((S, 128), F32)] * 4, compiler_params=_params(("parallel",)),
    )(proj_b, proj_b, proj_b, oab, d_oab, lse)


FF_SLAB = 2 * D_FF // N_DEV
FF_PAIRS = N_DEV // 2
ROWS16 = 16


def _taps(w, x, base, n):
    out = _shifted(x, base, n) * w[0:1]
    for t in range(1, FFN_CONV):
        out = out + _shifted(x, base + t, n) * w[t:t + 1]
    return out


def _ffn_fwd(h2, x1, w_up, conv_w, w_down, final_w, tgt, *, name, tm=512):
    S, D = h2.shape
    ni = S // tm
    per = tm // ROWS16

    def body(h_ref, hp_ref, x1_ref, wg_ref, wu_ref, cg_ref, cu_ref, wd_ref, fw_ref, t_ref,
             dx_ref, dxb_ref, dfw_ref, loss_ref, ug_ref, uu_ref, x2_ref):
        i, j = pl.program_id(0), pl.program_id(1)
        hv = jnp.concatenate([hp_ref[...], h_ref[...]], axis=0)
        row = lax.broadcasted_iota(jnp.int32, (tm + ROWS16, 1), 0)
        keep = (i > 0) | (row >= ROWS16)

        def branch(w_ref, c_ref, u_ref):
            u = lax.dot_general(hv, w_ref[...], (((1,), (1,)), ((), ())), preferred_element_type=F32).astype(BF16)
            u_ref[...] = u[ROWS16:]
            return _taps(c_ref[...], jnp.where(keep, u.astype(F32), 0.0), ROWS16 - (FFN_CONV - 1), tm)

        gate = branch(wg_ref, cg_ref, ug_ref)
        up = branch(wu_ref, cu_ref, uu_ref)
        act = (gate * _sigmoid(gate) * up).astype(BF16)
        part = jnp.dot(act, wd_ref[...], preferred_element_type=F32)

        @pl.when(j == 0)
        def _():
            x2_ref[...] = x1_ref[...] + part

        @pl.when((j > 0) & (j < FF_PAIRS - 1))
        def _():
            x2_ref[...] += part

        @pl.when(j == FF_PAIRS - 1)
        def _():
            xv = x2_ref[...] + part
            wv = fw_ref[...]
            r = lax.rsqrt(jnp.mean(xv * xv, axis=-1, keepdims=True) + EPS)
            err = xv * r * wv - t_ref[...]
            lsum = jnp.sum(jnp.sum(err * err, axis=-1, keepdims=True), axis=0, keepdims=True) * (0.5 / D)
            g = err * (1.0 / D)
            xh = xv * r
            gw = g * wv
            dx = r * (gw - xh * jnp.mean(gw * xh, axis=-1, keepdims=True))
            dx_ref[...] = dx
            dxb_ref[...] = dx.astype(BF16)
            dfw = jnp.sum(g * xh, axis=0, keepdims=True)
            lpart = jnp.broadcast_to(lsum, (1, 128))

            @pl.when(i == 0)
            def _():
                dfw_ref[...] = dfw
                loss_ref[...] = lpart

            @pl.when(i > 0)
            def _():
                dfw_ref[...] += dfw
                loss_ref[...] += lpart

    rows = pl.BlockSpec((tm, D), lambda i, j: (i, 0))
    slab = lambda off: pl.BlockSpec((None, FF_SLAB, D), lambda i, j: (j + off, 0, 0))
    cslab = lambda off: pl.BlockSpec((None, FFN_CONV, FF_SLAB), lambda i, j: (j + off, 0, 0))
    uspec = pl.BlockSpec((None, tm, FF_SLAB), lambda i, j: (j, i, 0))
    return pl.pallas_call(
        body, name=name, grid=(ni, FF_PAIRS),
        in_specs=[rows, pl.BlockSpec((ROWS16, D), lambda i, j: (jnp.maximum(i * per - 1, 0), 0)), rows,
                  slab(0), slab(FF_PAIRS), cslab(0), cslab(FF_PAIRS), pl.BlockSpec((FF_SLAB, D), lambda i, j: (j, 0)),
                  pl.BlockSpec((1, D), lambda i, j: (0, 0)), rows],
        out_specs=[rows, rows, pl.BlockSpec((1, D), lambda i, j: (0, 0)), pl.BlockSpec((1, 128), lambda i, j: (0, 0)), uspec, uspec],
        out_shape=[_sds((S, D)), _sds((S, D), BF16), _sds((1, D)), _sds((1, 128)),
                   _sds((FF_PAIRS, S, FF_SLAB), BF16), _sds((FF_PAIRS, S, FF_SLAB), BF16)],
        scratch_shapes=[pltpu.VMEM((tm, D), F32)],
        compiler_params=_params(("arbitrary", "arbitrary")),
    )(h2, h2, x1, w_up, w_up, conv_w, conv_w, w_down, final_w, tgt)


def _ffn_bwd(dx2, h2, ug, uu, conv_w, w_down, *, name, tm=512):
    S, D = h2.shape
    ni = S // tm
    per = tm // ROWS16
    ext = tm + ROWS16

    def body(dx_ref, dxn_ref, h_ref, ug_ref, ugp_ref, ugn_ref, uu_ref, uup_ref, uun_ref, cg_ref, cu_ref, wd_ref,
             du_ref, gd_ref, gup_ref, dcw_ref, acc_d, acc_g, acc_u, acc_cg, acc_cu):
        i = pl.program_id(1)

        @pl.when(i == 0)
        def _():
            acc_d[...] = jnp.zeros_like(acc_d)
            acc_g[...] = jnp.zeros_like(acc_g)
            acc_u[...] = jnp.zeros_like(acc_u)
            acc_cg[...] = jnp.zeros_like(acc_cg)
            acc_cu[...] = jnp.zeros_like(acc_cu)

        dx = dx_ref[...]
        dxe = jnp.concatenate([dx, dxn_ref[...]], axis=0)
        row = lax.broadcasted_iota(jnp.int32, (ext, 1), 0)
        live = (i < ni - 1) | (row < tm)
        d_act = jnp.where(live, lax.dot_general(dxe, wd_ref[...], (((1,), (1,)), ((), ())), preferred_element_type=F32), 0.0)
        rowp = lax.broadcasted_iota(jnp.int32, (ext + ROWS16, 1), 0)
        keep = (i > 0) | (rowp >= ROWS16)

        def pre(cur, prev, nxt):
            return jnp.where(keep, jnp.concatenate([prev[...], cur[...], nxt[...]], axis=0).astype(F32), 0.0)

        uge, uue = pre(ug_ref, ugp_ref, ugn_ref), pre(uu_ref, uup_ref, uun_ref)
        cg, cu = cg_ref[...], cu_ref[...]
        base = ROWS16 - (FFN_CONV - 1)
        gate = _taps(cg, uge, base, ext)
        up = _taps(cu, uue, base, ext)
        sg = _sigmoid(gate)
        silu = gate * sg
        dgc = d_act * up * _dsilu(gate, sg)
        duc = d_act * silu

        def conv_t(w, dc):
            out = _shifted(dc, FFN_CONV - 1, tm) * w[0:1]
            for t in range(1, FFN_CONV):
                out = out + _shifted(dc, FFN_CONV - 1 - t, tm) * w[t:t + 1]
            return out.astype(BF16)

        du_g, du_u = conv_t(cg, dgc), conv_t(cu, duc)
        du_ref[0] = du_g
        du_ref[1] = du_u
        dcw = lambda dc, xe: jnp.concatenate(
            [jnp.sum(dc[0:tm] * _shifted(xe, base + t, tm), axis=0, keepdims=True) for t in range(FFN_CONV)], axis=0)
        acc_cg[0:FFN_CONV, :] += dcw(dgc, uge)
        acc_cu[0:FFN_CONV, :] += dcw(duc, uue)
        tn = (((0,), (0,)), ((), ()))
        act = (silu[0:tm] * up[0:tm]).astype(BF16)
        acc_d[...] += lax.dot_general(act, dx, tn, preferred_element_type=F32)
        hv = h_ref[...]
        acc_g[...] += lax.dot_general(du_g, hv, tn, preferred_element_type=F32)
        acc_u[...] += lax.dot_general(du_u, hv, tn, preferred_element_type=F32)

        @pl.when(i == ni - 1)
        def _():
            gd_ref[...] = acc_d[...].astype(BF16)
            gup_ref[0] = acc_g[...].astype(BF16)
            gup_ref[1] = acc_u[...].astype(BF16)
            dcw_ref[0] = acc_cg[0:FFN_CONV, :]
            dcw_ref[1] = acc_cu[0:FFN_CONV, :]

    last16 = S // ROWS16 - 1
    rows = pl.BlockSpec((tm, D), lambda j, i: (i, 0))
    rows_next = pl.BlockSpec((ROWS16, D), lambda j, i: (jnp.minimum((i + 1) * per, last16), 0))
    u_cur = pl.BlockSpec((None, tm, FF_SLAB), lambda j, i: (j, i, 0))
    u_prev = pl.BlockSpec((None, ROWS16, FF_SLAB), lambda j, i: (j, jnp.maximum(i * per - 1, 0), 0))
    u_next = pl.BlockSpec((None, ROWS16, FF_SLAB), lambda j, i: (j, jnp.minimum((i + 1) * per, last16), 0))
    cslab = lambda off: pl.BlockSpec((None, FFN_CONV, FF_SLAB), lambda j, i: (j + off, 0, 0))
    return pl.pallas_call(
        body, name=name, grid=(FF_PAIRS, ni),
        in_specs=[rows, rows_next, rows, u_cur, u_prev, u_next, u_cur, u_prev, u_next, cslab(0), cslab(FF_PAIRS),
                  pl.BlockSpec((FF_SLAB, D), lambda j, i: (j, 0))],
        out_specs=[pl.BlockSpec((None, 2, tm, FF_SLAB), lambda j, i: (j, 0, i, 0)), pl.BlockSpec((FF_SLAB, D), lambda j, i: (j, 0)),
                   pl.BlockSpec((None, 2, FF_SLAB, D), lambda j, i: (j, 0, 0, 0)),
                   pl.BlockSpec((None, 2, FFN_CONV, FF_SLAB), lambda j, i: (j, 0, 0, 0))],
        out_shape=[_sds((FF_PAIRS, 2, S, FF_SLAB), BF16), _sds((D_FF, D), BF16), _sds((FF_PAIRS, 2, FF_SLAB, D), BF16),
                   _sds((FF_PAIRS, 2, FFN_CONV, FF_SLAB))],
        scratch_shapes=[pltpu.VMEM((FF_SLAB, D), F32), pltpu.VMEM((FF_SLAB, D), F32), pltpu.VMEM((FF_SLAB, D), F32),
                        pltpu.VMEM((8, FF_SLAB), F32), pltpu.VMEM((8, FF_SLAB), F32)],
        compiler_params=_params(("parallel", "arbitrary")),
    )(dx2, dx2, h2, ug, ug, ug, uu, uu, uu, conv_w, conv_w, w_down)


def _pair_slot(p):
    return 2 * (p & (FF_PAIRS - 1)) + (p >> 2)


def _mm_slabs(a, w, *, name, res=None, norm_bwd=None, after=(), tm=1024, tn=1024):
    nk, S, _ = a.shape
    D = w.shape[2]
    has_res = res is not None
    has_norm = norm_bwd is not None
    assert not has_norm or tn == D

    def body(*refs):
        a_ref, w_ref = refs[:2]
        r_ref = refs[2] if has_res else None
        if has_norm:
            x_ref, nw_ref, skip_ref = refs[2 + has_res:5 + has_res]
            o_ref, dw_ref, acc_ref = refs[-3:]
        else:
            o_ref, acc_ref = refs[-2:]
        i, k = pl.program_id(0), pl.program_id(2)
        part = jnp.dot(a_ref[...], w_ref[...], preferred_element_type=F32)

        @pl.when(k == 0)
        def _():
            acc_ref[...] = part

        @pl.when(k > 0)
        def _():
            acc_ref[...] += part

        @pl.when(k == nk - 1)
        def _():
            r = acc_ref[...] + r_ref[...] if has_res else acc_ref[...]
            if has_norm:
                dx, dw = _rms_bwd_rows(r, x_ref[...], nw_ref[...])
                o_ref[...] = skip_ref[...] + dx

                @pl.when(i == 0)
                def _():
                    dw_ref[...] = dw

                @pl.when(i > 0)
                def _():
                    dw_ref[...] += dw
            else:
                o_ref[...] = r

    o_spec = pl.BlockSpec((tm, tn), lambda i, j, k: (i, j))
    one = pl.BlockSpec((1, tn), lambda i, j, k: (0, 0))
    return pl.pallas_call(
        body, name=name, grid=(S // tm, D // tn, nk),
        in_specs=[pl.BlockSpec((None, tm, FF_SLAB), lambda i, j, k: (k, i, 0)),
                  pl.BlockSpec((None, FF_SLAB, tn), lambda i, j, k: (FF_PAIRS * (k & 1) + (k >> 1), 0, j))] + [o_spec] * has_res
        + ([o_spec, one, o_spec] if has_norm else []) + [ANY] * len(after),
        out_specs=[o_spec, one] if has_norm else o_spec, out_shape=[_sds((S, D)), _sds((1, D))] if has_norm else _sds((S, D)),
        scratch_shapes=[pltpu.VMEM((tm, tn), F32)],
        compiler_params=_params(("arbitrary" if has_norm else "parallel", "parallel", "arbitrary")),
    )(*((a, w) + ((res,) if has_res else ()) + (tuple(norm_bwd) if has_norm else ()) + tuple(after)))


def _local_step(x, tgt, norm1_w, w_land, conv_a, a_log, dt_bias, gnw, norm2_w, final_w, late_weights, emit, start_after=()):
    wgrad = functools.partial(_mm, ta=True, out_dtype=BF16)
    h1, proj_a, proj_z, proj_b = _in_proj(x, norm1_w, w_land, after=start_after, name="in_proj")
    qn, kn, v, gcb, bb = _gdn_prep_fwd(proj_a, conv_a, a_log, dt_bias, name="gdn_prep_fwd")
    uv, wk, at, tmat, wkb, qdb, keb = _gdn_chunk_fwd(qn, kn, v, gcb, bb, name="gdn_chunk_fwd")
    o, u, sp, oab = _gdn_scan_fwd(uv, at, wkb, qdb, keb, gcb, proj_z, gnw, name="gdn_scan_fwd")
    oab, lse = _attn_fwd(proj_b, oab, name="attn_fwd")
    w_out, w_up, conv_f, w_down = late_weights(oab)
    x1, h2 = _out_proj_norm(oab, w_out, x, norm2_w, name="out_proj")
    dx2, dx2_b, d_final, loss, ug, uu = _ffn_fwd(h2, x1, w_up, conv_f, w_down, final_w, tgt, name="ffn_fwd")
    du, g_down, g_up, dcw = _ffn_bwd(dx2_b, h2, ug, uu, conv_f, w_down, name="ffn_bwd")
    token = emit("ffn", w_down=g_down, w_up=g_up.reshape(N_DEV, FF_SLAB, -1), conv_f=dcw.reshape(N_DEV, FFN_CONV, -1))
    dx1, d_norm2 = _mm_slabs(du.reshape(N_DEV, -1, FF_SLAB), w_up, norm_bwd=(x1, norm2_w, dx2), after=token, name="ffn_up_dx")
    d_oab = _mm(dx1, w_out, tb=True, name="out_proj_dx", tn=D_MODEL, tk=1024)
    token = emit("out", w_out=wgrad(oab, dx1, name="out_proj_dw", tm=D_MODEL, tn=D_MODEL))
    dz, d_gnw, du, dwk, dat, dqd, dke, dgl = _gdn_scan_bwd(d_oab, o, proj_z, gnw, sp, u, at, wkb, qdb, keb, gcb, after=token, name="gdn_scan_bwd")
    dqn, dkn, dv, dg, dbeta = _gdn_chunk_bwd(qn, kn, gcb, bb, tmat, uv, wk, du, dwk, dat, dqd, dke, dgl, name="gdn_chunk_bwd")
    dc, dba, d_small = _gdn_prep_bwd(dqn, dkn, dv, dg, dbeta, proj_a, conv_a, a_log, dt_bias, name="gdn_prep_bwd")
    d_pa, d_conv_a = _gdn_conv_bwd(dc, dba, proj_a, conv_a, name="gdn_conv_bwd")
    d_pb = _attn_bwd(proj_b, oab, d_oab, lse, name="attn_bwd")
    g_a = wgrad(d_pa, h1, name="proj_a_dw", tm=A_COLS, tn=D_MODEL)
    g_z = wgrad(dz, h1, name="proj_z_dw", tn=D_MODEL)
    g_b = wgrad(d_pb, h1, name="proj_b_dw", tm=768, tn=D_MODEL)
    token = emit("in", w_a=g_a, w_z=g_z, w_b=g_b, conv_a=d_conv_a)
    grad_x, d_norm1 = _in_proj_dx((d_pa, dz, d_pb), w_land, x, norm1_w, dx1, after=token, name="in_proj_dx")
    small = dict(norm1=d_norm1, small=d_small, gnw=d_gnw, norm2=d_norm2, final=d_final)
    return loss, grad_x, small


_O1 = 3 * GDN_WIDTH
_O2 = _O1 + GDN_WIDTH
_O3 = _O2 + 2 * GDN_HEADS


_W_IN_ROWS = (A_COLS, GDN_WIDTH, 3 * DIL_WIDTH)
_IN_ROWS = (_O3 + 3 * DIL_WIDTH) // N_DEV
_ROW_TILE = 16
_IN_STEP = _IN_ROWS - _IN_ROWS % _ROW_TILE
_GAP = 8
_LAND_ROWS = 464
assert _O3 % _ROW_TILE == _ROW_TILE - _GAP and N_DEV - 1 + _GAP + _IN_ROWS <= _LAND_ROWS and _LAND_ROWS % _ROW_TILE == 0


def _padded_row(r):
    return r + (_GAP if r >= _O3 else 0)


def _shifted_slab(w_rows, j, *, name):
    q = w_rows.shape[0] // _IN_ROWS

    def body(j_ref, w_ref, o_ref, pad_ref):
        pad_ref[...] = jnp.zeros_like(pad_ref)
        for dev in range(N_DEV):
            @pl.when(j_ref[0] == dev)
            def _(dev=dev):
                p = lax.broadcasted_iota(jnp.int32, (_LAND_ROWS, 1), 0) + _IN_STEP * dev
                for s in range(q):
                    pad_ref[0:_IN_ROWS, :] = w_ref[pl.ds(s, _IN_ROWS, stride=q), :]
                    rows = pad_ref[...]
                    a = pltpu.roll(rows, dev, 0) if dev else rows
                    b = pltpu.roll(rows, dev + _GAP, 0)
                    o_ref[:, 128 * s:128 * (s + 1)] = jnp.where(p < _O3, a, jnp.where(p >= _O3 + _GAP, b, 0.0)).astype(BF16)

    vm = pl.BlockSpec(memory_space=pltpu.VMEM)
    return pl.pallas_call(
        body, name=name, in_specs=[pl.BlockSpec(memory_space=pltpu.SMEM), vm], out_specs=vm,
        out_shape=_sds((_LAND_ROWS, 128 * q), BF16), scratch_shapes=[pltpu.VMEM((_LAND_ROWS, 128), F32)],
    )(j, w_rows)


def _w_in_plan():
    def dest(p):
        if p < _O1:
            return 0, p
        if p < _O2:
            return 1, p - _O1
        if p < _O2 + _ROW_TILE:
            return 0, _O1
        q = p - _O3 - _GAP
        t, pair = divmod(q // 128, DIL_PAIRS)
        return 2, (3 * pair + t) * 128 + q % 128

    spans = [(_padded_row(_IN_ROWS * j), _padded_row(_IN_ROWS * (j + 1) - 1) + 1) for j in range(N_DEV)]
    runs, seams = [], []
    for p in range(0, spans[-1][1], _ROW_TILE):
        owners = [j for j, (lo, hi) in enumerate(spans) if lo < p + _ROW_TILE and hi > p]
        w, r = dest(p)
        if len(owners) == 2:
            seams.append((w, r, owners[0], p - _IN_STEP * owners[0], owners[1], p - _IN_STEP * owners[1]))
            continue
        (j,) = owners
        last = runs[-1] if runs else None
        if last and last[0] == j and last[2] == w and last[3] + last[4] == r and last[1] + last[4] == p - _IN_STEP * j:
            runs[-1] = last[:4] + (last[4] + _ROW_TILE,)
        else:
            runs.append((j, p - _IN_STEP * j, w, r, _ROW_TILE))
    return runs, seams


def _w_in_scratch(d):
    return [pltpu.VMEM((n, d), BF16) for n in _W_IN_ROWS] + [pltpu.VMEM((N_DEV - 1, _ROW_TILE, d), BF16),
                                                              pltpu.SemaphoreType.DMA(())]


def _fetch_w_in(land_ref, wa_ref, wz_ref, wb_ref, seam_ref, sem):
    w_refs = (wa_ref, wz_ref, wb_ref)
    runs, seams = _w_in_plan()
    copies = [pltpu.make_async_copy(land_ref.at[j, pl.ds(s, n)], w_refs[w].at[pl.ds(r, n)], sem) for j, s, w, r, n in runs]
    for k, (w, r, j0, s0, j1, s1) in enumerate(seams):
        copies.append(pltpu.make_async_copy(land_ref.at[j0, pl.ds(s0, _ROW_TILE)], w_refs[w].at[pl.ds(r, _ROW_TILE)], sem))
        copies.append(pltpu.make_async_copy(land_ref.at[j1, pl.ds(s1, _ROW_TILE)], seam_ref.at[k], sem))
    for i, cp in enumerate(copies):
        cp.start(priority=i % 2)
    tail = _O1 + _ROW_TILE
    wa_ref[tail:, :] = jnp.zeros((A_COLS - tail, wa_ref.shape[1]), BF16)
    for cp in copies:
        cp.wait()
    for k, (w, r, *_) in enumerate(seams):
        both = w_refs[w][r:r + _ROW_TILE, :].astype(F32) + seam_ref[k].astype(F32)
        w_refs[w][r:r + _ROW_TILE, :] = both.astype(BF16)


MESH = pl.DeviceIdType.MESH
ANY = pl.BlockSpec(memory_space=pl.ANY)


def _position():
    return lax.axis_index("x"), lax.axis_index("y"), lax.axis_index("c")


def _slot(p):
    return 4 * p[0] + 2 * p[1] + p[2]


def _all_gather(blocks, *, name):
    n = len(blocks)

    def body(*refs):
        ins, outs = refs[:n], refs[n:2 * n]
        send_sems, recv_sems, local_sems = refs[2 * n:]
        x, y, c = _position()
        me, sibling = (x, y, c), (x, y, 1 - c)
        chips = [(1 - x, y), (x, 1 - y), (1 - x, 1 - y)]

        def copy(a, k, block, to, src=None):
            dst = outs[a].at[_slot(block)]
            return pltpu.make_async_remote_copy(
                src_ref=dst if src is None else src, dst_ref=dst, send_sem=send_sems.at[a, k], recv_sem=recv_sems.at[a, k],
                device_id=to, device_id_type=MESH)

        mine = [pltpu.make_async_copy(ins[a], outs[a].at[_slot(me)], local_sems.at[a]) for a in range(n)]
        for cp in mine:
            cp.start()
        first = []
        for a in range(n):
            first.append(copy(a, 0, me, sibling, src=ins[a]))
            first += [copy(a, 1 + j, me, (*chip, c), src=ins[a]) for j, chip in enumerate(chips)]
        for cp in first:
            cp.start()
        passed = []
        for j, chip in enumerate(chips):
            for a in range(n):
                copy(a, 1 + j, (*chip, c), me).wait_recv()
                fwd = copy(a, 4 + j, (*chip, c), sibling)
                fwd.start()
                passed.append(fwd)
        for a in range(n):
            copy(a, 0, sibling, me).wait_recv()
            for j, chip in enumerate(chips):
                copy(a, 4 + j, (*chip, 1 - c), me).wait_recv()
        for cp in first + passed:
            cp.wait_send()
        for cp in mine:
            cp.wait()

    return pl.pallas_call(
        body, name=name, in_specs=[ANY] * n, out_specs=[ANY] * n,
        out_shape=[_sds((N_DEV,) + b.shape, b.dtype) for b in blocks],
        scratch_shapes=[pltpu.SemaphoreType.DMA((n, 7)), pltpu.SemaphoreType.DMA((n, 7)), pltpu.SemaphoreType.DMA((n,))],
    )(*blocks)


def _gather_direct(block, *, name, after=()):
    def body(in_ref, *rest):
        out_ref, send_sems, recv_sems, local_sem = rest[len(after):]
        x, y, c = _position()
        me = _slot((x, y, c))
        mine = pltpu.make_async_copy(in_ref, out_ref.at[me], local_sem)
        mine.start()
        copies = [pltpu.make_async_remote_copy(
            src_ref=in_ref, dst_ref=out_ref.at[me], send_sem=send_sems.at[k - 1], recv_sem=recv_sems.at[k - 1],
            device_id=_peer_of(k, x, y, c), device_id_type=MESH) for k in range(1, N_DEV)]
        for cp in copies:
            cp.start()
        for cp in copies:
            cp.wait()
        mine.wait()

    return pl.pallas_call(
        body, name=name, in_specs=[pl.BlockSpec(memory_space=pltpu.VMEM)] + [ANY] * len(after),
        out_specs=pl.BlockSpec(memory_space=pltpu.VMEM),
        out_shape=_sds((N_DEV,) + block.shape, block.dtype),
        scratch_shapes=[pltpu.SemaphoreType.DMA((N_DEV - 1,)), pltpu.SemaphoreType.DMA((N_DEV - 1,)), pltpu.SemaphoreType.DMA],
    )(block, *after)


HBM = pl.BlockSpec(memory_space=pltpu.HBM)
SEM = pl.BlockSpec(memory_space=pltpu.SEMAPHORE)
EFFECT = pltpu.SideEffectType.DATAFLOW_SIDE_EFFECTING


def _peer_of(k, x, y, c):
    return (1 - x if k & 4 else x, 1 - y if k & 2 else y, 1 - c if k & 1 else c)


def _flight(a, k):
    return a * (N_DEV - 1) + k - 1


def _exchange_start(arrays, *, name, broadcast=False, paired=()):
    n = len(arrays)

    def body(*refs):
        ins, lands = refs[:n], refs[n:2 * n]
        send_sems, recv_sems = refs[2 * n:2 * n + 2]
        token = refs[2 * n + 2 + 2 * n]
        x, y, c = _position()
        me = _slot((x, y, c))
        for k in range(1, N_DEV):
            peer = _peer_of(k, x, y, c)
            for a in range(n):
                at = _pair_slot(_slot(peer)) if a in paired else _slot(peer)
                pltpu.make_async_remote_copy(
                    src_ref=ins[a] if broadcast else ins[a].at[at], dst_ref=lands[a].at[me],
                    send_sem=send_sems.at[_flight(a, k)], recv_sem=recv_sems.at[_flight(a, k)],
                    device_id=peer, device_id_type=MESH).start()
        if broadcast:
            for a in range(n):
                pltpu.make_async_copy(ins[a], lands[a].at[me], refs[-1].at[a]).start()
        token[...] = jnp.zeros_like(token)

    land_shapes = [((N_DEV,) + s.shape) if broadcast else s.shape for s in arrays]
    lands = [pltpu.with_memory_space_constraint(lax.empty(shp, s.dtype), pltpu.HBM) for shp, s in zip(land_shapes, arrays)]
    srcs = [pltpu.with_memory_space_constraint(s, pltpu.HBM) for s in arrays]
    outs = pl.pallas_call(
        body, name=name, in_specs=[HBM] * (2 * n),
        out_specs=[SEM, SEM] + [HBM] * (2 * n) + [pl.BlockSpec(memory_space=pltpu.VMEM)] + [SEM] * broadcast,
        out_shape=[pltpu.SemaphoreType.DMA((n * (N_DEV - 1),)), pltpu.SemaphoreType.DMA((n * (N_DEV - 1),))]
        + [pltpu.HBM(s.shape, s.dtype) for s in arrays] + [pltpu.HBM(shp, s.dtype) for shp, s in zip(land_shapes, arrays)]
        + [_sds((8, 128))] + [pltpu.SemaphoreType.DMA((n,))] * broadcast,
        input_output_aliases={i: 2 + i for i in range(2 * n)},
        compiler_params=pltpu.CompilerParams(has_side_effects=EFFECT),
    )(*srcs, *lands)
    flight = (outs[0], outs[1], outs[2:2 + n], outs[2 + n:2 + 2 * n], outs[2 + 2 * n])
    return flight + (outs[-1],) if broadcast else flight


def _exchange_wait(send_sems, recv_sems, srcs, lands, after, *, name, broadcast=False, own_sems=None):
    n = len(srcs)

    def body(*refs):
        ins, lnd = refs[:n], refs[n:2 * n]
        send_ref, recv_ref = refs[2 * n:2 * n + 2]
        x, y, c = _position()
        for k in range(1, N_DEV):
            for a in range(n):
                cp = pltpu.make_async_remote_copy(
                    src_ref=ins[a] if broadcast else ins[a].at[0], dst_ref=lnd[a].at[0], send_sem=send_ref.at[_flight(a, k)],
                    recv_sem=recv_ref.at[_flight(a, k)], device_id=_peer_of(k, x, y, c), device_id_type=MESH)
                cp.wait_send()
                cp.wait_recv()
        if broadcast:
            for a in range(n):
                pltpu.make_async_copy(ins[a], lnd[a].at[0], refs[2 * n + 3].at[a]).wait()

    outs = pl.pallas_call(
        body, name=name, in_specs=[HBM] * (2 * n) + [SEM, SEM, ANY] + [SEM] * broadcast, out_specs=[HBM] * (2 * n),
        out_shape=[pltpu.HBM(s.shape, s.dtype) for s in srcs] + [pltpu.HBM(s.shape, s.dtype) for s in lands],
        input_output_aliases={i: i for i in range(2 * n)},
        compiler_params=pltpu.CompilerParams(has_side_effects=EFFECT),
    )(*srcs, *lands, send_sems, recv_sems, after, *([own_sems] if broadcast else []))
    return outs[:n], outs[n:]


_G_LAND_ROWS = 528


def _g_in_pieces(dev):
    runs, seams = _w_in_plan()
    pieces = [(w, r, n, s) for j, s, w, r, n in runs if j == dev]
    pieces += [(w, r, _ROW_TILE, s0) for w, r, j0, s0, j1, s1 in seams if j0 == dev]
    pieces += [(w, r, _ROW_TILE, s1) for w, r, j0, s0, j1, s1 in seams if j1 == dev]
    merged = []
    for w, r, n, s in sorted(pieces, key=lambda p: p[3]):
        if merged and merged[-1][0] == w and merged[-1][1] + merged[-1][2] == r and merged[-1][3] + merged[-1][2] == s:
            merged[-1] = (w, merged[-1][1], merged[-1][2] + n, merged[-1][3])
        else:
            merged.append((w, r, n, s))
    return merged


def _coords(dev):
    return tuple(jnp.int32(v) for v in (dev >> 2, (dev >> 1) & 1, dev & 1))


def _exchange_start_in(g_ws, conv_slabs, *, name):
    srcs = list(g_ws) + [conv_slabs]
    n = len(srcs)

    def body(*refs):
        g_refs, cv_ref, land, land_cv = refs[:n - 1], refs[n - 1], refs[n], refs[n + 1]
        send_sems, recv_sems = refs[n + 2:n + 4]
        token = refs[-1]
        me = _slot(_position())
        pieces = [_g_in_pieces(dev) for dev in range(N_DEV)]
        for i in range(max(len(p) for p in pieces)):
            for dev in range(N_DEV):
                if i < len(pieces[dev]):
                    @pl.when(me != dev)
                    def _(dev=dev, i=i):
                        w, r, rows, s = pieces[dev][i]
                        k = me ^ dev
                        pltpu.make_async_remote_copy(
                            src_ref=g_refs[w].at[pl.ds(r, rows)], dst_ref=land.at[me, pl.ds(s, rows)],
                            send_sem=send_sems.at[_flight(0, k)], recv_sem=recv_sems.at[_flight(0, k)],
                            device_id=_coords(dev), device_id_type=MESH).start()
        for dev in range(N_DEV):
            @pl.when(me != dev)
            def _(dev=dev):
                k = me ^ dev
                pltpu.make_async_remote_copy(
                    src_ref=cv_ref.at[dev], dst_ref=land_cv.at[me], send_sem=send_sems.at[_flight(1, k)],
                    recv_sem=recv_sems.at[_flight(1, k)], device_id=_coords(dev), device_id_type=MESH).start()
        token[...] = jnp.zeros_like(token)

    lands = [lax.empty((N_DEV, _G_LAND_ROWS, g_ws[0].shape[1]), g_ws[0].dtype), lax.empty(conv_slabs.shape, conv_slabs.dtype)]
    ops = [pltpu.with_memory_space_constraint(a, pltpu.HBM) for a in srcs + lands]
    outs = pl.pallas_call(
        body, name=name, in_specs=[HBM] * len(ops),
        out_specs=[SEM, SEM] + [HBM] * len(ops) + [pl.BlockSpec(memory_space=pltpu.VMEM)],
        out_shape=[pltpu.SemaphoreType.DMA((2 * (N_DEV - 1),)), pltpu.SemaphoreType.DMA((2 * (N_DEV - 1),))]
        + [pltpu.HBM(a.shape, a.dtype) for a in ops] + [_sds((8, 128))],
        input_output_aliases={i: 2 + i for i in range(len(ops))},
        compiler_params=pltpu.CompilerParams(has_side_effects=EFFECT),
    )(*ops)
    return outs[0], outs[1], outs[2:2 + n], outs[2 + n:4 + n], outs[-1]


def _own_pieces(g_ws, land, after, *, name):
    n = len(g_ws)

    def body(*refs):
        g_refs, land_ref = refs[:n], refs[n]
        token, slab, sem = refs[-3:]
        me = _slot(_position())
        for dev in range(N_DEV):
            @pl.when(me == dev)
            def _(dev=dev):
                own = [pltpu.make_async_copy(g_refs[w].at[pl.ds(r, rows)], slab.at[pl.ds(s, rows)], sem)
                       for w, r, rows, s in _g_in_pieces(dev)]
                for cp in own:
                    cp.start()
                for cp in own:
                    cp.wait()
                out = pltpu.make_async_copy(slab, land_ref.at[dev, pl.ds(0, _LAND_ROWS)], sem)
                out.start()
                out.wait()
        token[...] = jnp.zeros_like(token)

    return pl.pallas_call(
        body, name=name, in_specs=[HBM] * (n + 1) + [ANY], out_specs=[HBM, pl.BlockSpec(memory_space=pltpu.VMEM)],
        out_shape=[pltpu.HBM(land.shape, land.dtype), _sds((8, 128))], input_output_aliases={n: 0},
        scratch_shapes=[pltpu.VMEM((_LAND_ROWS, land.shape[2]), land.dtype), pltpu.SemaphoreType.DMA(())],
    )(*g_ws, land, after)


def _exchange_wait_in(send_sems, recv_sems, srcs, lands, after, *, name):
    n = len(srcs)

    def body(*refs):
        g_refs, cv_ref, land, land_cv = refs[:n - 1], refs[n - 1], refs[n], refs[n + 1]
        send_ref, recv_ref = refs[n + 2:n + 4]
        me = _slot(_position())

        def copies(dev, k):
            cps = [pltpu.make_async_remote_copy(
                src_ref=g_refs[w].at[pl.ds(r, rows)], dst_ref=land.at[0, pl.ds(s, rows)], send_sem=send_ref.at[_flight(0, k)],
                recv_sem=recv_ref.at[_flight(0, k)], device_id=_coords(dev), device_id_type=MESH)
                for w, r, rows, s in _g_in_pieces(dev)]
            return cps + [pltpu.make_async_remote_copy(
                src_ref=cv_ref.at[0], dst_ref=land_cv.at[0], send_sem=send_ref.at[_flight(1, k)],
                recv_sem=recv_ref.at[_flight(1, k)], device_id=_coords(dev), device_id_type=MESH)]

        for dev in range(N_DEV):
            @pl.when(me != dev)
            def _(dev=dev):
                for cp in copies(dev, me ^ dev):
                    cp.wait_send()

            @pl.when(me == dev)
            def _(dev=dev):
                for k in range(1, N_DEV):
                    for cp in copies(dev, k):
                        cp.wait_recv()

    ops = list(srcs) + list(lands)
    outs = pl.pallas_call(
        body, name=name, in_specs=[HBM] * len(ops) + [SEM, SEM, ANY], out_specs=[HBM] * len(ops),
        out_shape=[pltpu.HBM(a.shape, a.dtype) for a in ops],
        input_output_aliases={i: i for i in range(len(ops))},
        compiler_params=pltpu.CompilerParams(has_side_effects=EFFECT),
    )(*ops, send_sems, recv_sems, after)
    return outs[:n], outs[n:]


def _adam_update(g, w, m, v):
    c1 = 1.0 - ADAM_B1 ** ADAM_STEP
    c2 = 1.0 - ADAM_B2 ** ADAM_STEP
    nm = ADAM_B1 * m + (1.0 - ADAM_B1) * g
    nv = ADAM_B2 * v + (1.0 - ADAM_B2) * (g * g)
    return -ADAM_LR * ((nm / c1) / (jnp.sqrt(nv / c2) + ADAM_EPS) + ADAM_WD * w), nm, nv


def _adamw(landed, sent, me, w, m, v, *, name, tr=None, tc=None):
    R, C = w.shape
    tr = R if tr is None else tr
    tc = C if tc is None else tc
    assert R % tr == 0 and C % tc == 0

    def body(me_ref, own_ref, p_ref, w_ref, m_ref, v_ref, g_ref, d_ref, nm_ref, nv_ref, token_ref):
        token_ref[...] = jnp.zeros_like(token_ref)
        g = own_ref[...].astype(F32)
        for s in range(N_DEV):
            g = g + jnp.where(me_ref[1] == s, 0.0, p_ref[s].astype(F32))
        delta, nm, nv = _adam_update(g, w_ref[...], m_ref[...], v_ref[...])
        g_ref[...] = g
        nm_ref[...] = nm
        nv_ref[...] = nv
        d_ref[...] = delta

    blk = pl.BlockSpec((tr, tc), lambda i, j, me_ref: (i, j))
    return pl.pallas_call(
        body, name=name,
        grid_spec=pltpu.PrefetchScalarGridSpec(
            num_scalar_prefetch=1, grid=(R // tr, C // tc),
            in_specs=[pl.BlockSpec((None, tr, tc), lambda i, j, me_ref: (me_ref[0], i, j)),
                      pl.BlockSpec((N_DEV, tr, tc), lambda i, j, me_ref: (0, i, j)), blk, blk, blk],
            out_specs=[blk] * 4 + [pl.BlockSpec((8, 128), lambda i, j, me_ref: (0, 0))]),
        out_shape=[_sds((R, C))] * 4 + [_sds((8, 128))],
        compiler_params=_params(("arbitrary", "arbitrary")),
    )(me, sent, landed, w, m, v)


def _adamw_rowwise(landed, me, w, m, v, *, name, tr=128):
    C = landed.shape[2]
    R, q, extra = _IN_ROWS, C // 128, _ROW_TILE
    assert tr % extra == 0 and (pl.cdiv(R, tr) * tr + extra) <= landed.shape[1] and N_DEV - 1 + _GAP < extra

    def body(me_ref, a_ref, b_ref, w_ref, m_ref, v_ref, g_ref, d_ref, nm_ref, nv_ref, g_scr):
        i = pl.program_id(0)
        total = lambda ref: functools.reduce(lambda x, y: x + y, [ref[s].astype(F32) for s in range(N_DEV)])
        slab = jnp.concatenate([total(a_ref), total(b_ref)], axis=0)
        for dev in range(N_DEV):
            @pl.when(me_ref[0] == dev)
            def _(dev=dev):
                lo, hi = slab[dev:dev + tr], slab[dev + _GAP:dev + _GAP + tr]
                if _IN_ROWS * (dev + 1) <= _O3:
                    g_scr[...] = lo
                elif _IN_ROWS * dev >= _O3:
                    g_scr[...] = hi
                else:
                    r = _IN_ROWS * dev + tr * i + lax.broadcasted_iota(jnp.int32, (tr, 1), 0)
                    g_scr[...] = jnp.where(r < _O3, lo, hi)
        g = g_scr[...]
        for s in range(q):
            rows = pl.ds(s, tr, stride=q)
            gs = g[:, 128 * s:128 * (s + 1)]
            delta, nm, nv = _adam_update(gs, w_ref[rows, :], m_ref[rows, :], v_ref[rows, :])
            g_ref[rows, :] = gs
            nm_ref[rows, :] = nm
            nv_ref[rows, :] = nv
            d_ref[rows, :] = delta

    blk = pl.BlockSpec((tr * q, 128), lambda i, me_ref: (i, 0))
    return pl.pallas_call(
        body, name=name,
        grid_spec=pltpu.PrefetchScalarGridSpec(
            num_scalar_prefetch=1, grid=(pl.cdiv(R, tr),),
            in_specs=[pl.BlockSpec((N_DEV, tr, C), lambda i, me_ref: (0, i, 0)),
                      pl.BlockSpec((N_DEV, extra, C), lambda i, me_ref: (0, (tr // extra) * (i + 1), 0)), blk, blk, blk],
            out_specs=[blk] * 4, scratch_shapes=[pltpu.VMEM((tr, C), F32)]),
        out_shape=[_sds((R * q, 128))] * 4,
        compiler_params=_params(("arbitrary",)),
    )(me, landed, landed, w, m, v)


_SMALL_ROWS = 8
_SMALL_SLOTS = ((0, 0, D_MODEL), (1, 0, D_MODEL), (2, 0, D_MODEL), (3, 0, GDN_DIM), (3, GDN_DIM, GDN_HEADS),
                (3, GDN_DIM + GDN_HEADS, GDN_HEADS))
_LOSS_LANE = 2 * GDN_DIM


def _pack_small(norm1, norm2, final, gnw, a_log, dt_bias, loss):
    row3 = jnp.concatenate([gnw, a_log, dt_bias, jnp.zeros((1, 128 - 2 * GDN_HEADS), F32), loss,
                            jnp.zeros((1, D_MODEL - 3 * 128), F32)], axis=1)
    return jnp.concatenate([norm1, norm2, final, row3, jnp.zeros((_SMALL_ROWS - 4, D_MODEL), F32)], axis=0)


def _adamw_small(packs, ws, ms, vs, *, name):
    n = len(ws)

    def body(p_ref, *refs):
        w_refs, m_refs, v_refs = refs[:n], refs[n:2 * n], refs[2 * n:3 * n]
        outs = refs[3 * n:]
        g_all = p_ref[0]
        for s in range(1, N_DEV):
            g_all = g_all + p_ref[s]
        for i, (row, lane, width) in enumerate(_SMALL_SLOTS):
            g = g_all[row:row + 1, lane:lane + width]
            delta, nm, nv = _adam_update(g, w_refs[i][...], m_refs[i][...], v_refs[i][...])
            for o_ref, val in zip(outs[4 * i:4 * i + 4], (g, delta, nm, nv)):
                o_ref[...] = val
        outs[-1][...] = g_all[3:4, _LOSS_LANE:_LOSS_LANE + 128]

    vm = pl.BlockSpec(memory_space=pltpu.VMEM)
    outs = pl.pallas_call(
        body, name=name, in_specs=[vm] * (1 + 3 * n), out_specs=[vm] * (4 * n + 1),
        out_shape=[_sds(w.shape) for w in ws for _ in range(4)] + [_sds((1, 128))],
    )(packs, *ws, *ms, *vs)
    return [outs[4 * i:4 * i + 4] for i in range(n)], outs[-1]


def _slabs_by_cols(g):
    r = g.shape[0]
    return g.reshape(r, N_DEV, -1).transpose(1, 0, 2)


def _cols_from_slabs(s):
    return s.transpose(1, 0, 2).reshape(s.shape[1], -1)


def kernel(x, norm1_w, w_in, conv_qkv_w, a_log, dt_bias, gdn_norm_w, w_out, norm2_w, w_up, ffn_conv_w, w_down, final_norm_w, loss_target, m_norm1_w, m_w_in, m_conv_qkv_w, m_a_log, m_dt_bias, m_gdn_norm_w, m_w_out, m_norm2_w, m_w_up, m_ffn_conv_w, m_w_down, m_final_norm_w, v_norm1_w, v_w_in, v_conv_qkv_w, v_a_log, v_dt_bias, v_gdn_norm_w, v_w_out, v_norm2_w, v_w_up, v_ffn_conv_w, v_w_down, v_final_norm_w):
    bf = lambda a: a.astype(BF16)
    me = _slot(_position())
    me1 = jnp.reshape(me, (1,)).astype(jnp.int32)
    t_in = lambda a: a[0].T
    rows = lambda a: a.reshape(D_MODEL // 128, 128, -1).transpose(2, 0, 1).reshape(-1, 128)
    gw_in, g_conv_a = _all_gather([_shifted_slab(rows(w_in), me1, name="shift_w_in"), conv_qkv_w[0]], name="gather_w_in")
    late_src, _ = lax.optimization_barrier(([bf(w_out[0]), bf(t_in(w_up)), bf(w_down[0]), ffn_conv_w[0]], gw_in))
    l_send, l_recv, l_srcs, l_lands, l_token, l_own = _exchange_start(late_src, name="weights_start", broadcast=True)

    def late_weights(after):
        _, (gw_out, gw_up, gw_down, g_conv_f) = _exchange_wait(
            l_send, l_recv, l_srcs, l_lands, after, name="weights_wait", broadcast=True, own_sems=l_own)
        return gw_out.reshape(D_MODEL, D_MODEL), gw_up, g_conv_f, gw_down.reshape(D_FF, D_MODEL)

    flights = {}

    def emit(group, **grads):
        paired = ()
        if group == "in":
            *flight, token = _exchange_start_in([grads["w_a"], grads["w_z"], grads["w_b"]], _slabs_by_cols(grads["conv_a"]),
                                                name="grads_start_in")
            flights[group] = flight
            return (token,)
        if group == "ffn":
            slabs = dict(w_down=grads["w_down"].reshape(N_DEV, -1, D_MODEL), w_up=grads["w_up"], conv_f=grads["conv_f"])
            paired = (1, 2)
        else:
            slabs = {k: v.reshape(N_DEV, -1, D_MODEL) for k, v in grads.items()}
        names = list(slabs)
        *flight, token = _exchange_start([slabs[k] for k in names], paired=paired, name="grads_start_" + group)
        flights[group] = (names, flight)
        return (token,)

    loss, grad_x, g = _local_step(
        x[0], loss_target[0], norm1_w, gw_in, _cols_from_slabs(g_conv_a), a_log, dt_bias,
        gdn_norm_w, norm2_w, final_norm_w[None], late_weights, emit, start_after=(l_token,))
    got = {}

    def collect(group, after):
        names, (send_sems, recv_sems, srcs, lands) = flights[group]
        srcs, landed = _exchange_wait(send_sems, recv_sems, srcs, lands, after, name="grads_wait_" + group)
        got.update(zip(names, zip(landed, srcs)))

    def update(key, w, m, v, paired=False, **tiles):
        where = jnp.concatenate([_pair_slot(me1) if paired else me1, me1])
        return _adamw(*got[key], where, w, m, v, name="adamw_" + key, **tiles)

    in_sems, in_srcs, in_lands = flights["in"][:2], flights["in"][2], flights["in"][3]
    own_land, own_token = _own_pieces(in_srcs[:3], in_lands[0], grad_x, name="grads_own_in")
    collect("ffn", own_token)
    collect("out", own_token)
    *o_out, t1 = update("w_out", w_out[0], m_w_out[0], v_w_out[0])
    *o_up, t2 = update("w_up", t_in(w_up), t_in(m_w_up), t_in(v_w_up), paired=True, tr=176)
    o_up = [o.T for o in o_up]
    *o_down, t3 = update("w_down", w_down[0], m_w_down[0], v_w_down[0], tr=176)
    *o_cf, t4 = update("conv_f", ffn_conv_w[0], m_ffn_conv_w[0], v_ffn_conv_w[0], paired=True)
    pack = _pack_small(g["norm1"], g["norm2"], g["final"], g["gnw"], g["small"][:, 0:GDN_HEADS],
                       g["small"][:, GDN_HEADS:2 * GDN_HEADS], loss)
    small_all = _gather_direct(pack, after=(t1, t2, t3, t4), name="gather_small")
    srcs, (g_land, conv_land) = _exchange_wait_in(*in_sems, in_srcs, [own_land, in_lands[1]], small_all, name="grads_wait_in")
    got["conv_a"] = (conv_land, srcs[-1])
    o_in = [o.reshape(-1, D_MODEL // 128, 128).transpose(1, 2, 0).reshape(D_MODEL, -1) for o in _adamw_rowwise(
        g_land, me1, rows(w_in), rows(m_w_in), rows(v_w_in), name="adamw_w_in")]
    o_ca = update("conv_a", conv_qkv_w[0], m_conv_qkv_w[0], v_conv_qkv_w[0])
    (o_n1, o_n2, o_fin, o_gn, o_al, o_dt), total = _adamw_small(
        small_all, (norm1_w, norm2_w, final_norm_w[None], gdn_norm_w, a_log, dt_bias),
        (m_norm1_w, m_norm2_w, m_final_norm_w[None], m_gdn_norm_w, m_a_log, m_dt_bias),
        (v_norm1_w, v_norm2_w, v_final_norm_w[None], v_gdn_norm_w, v_a_log, v_dt_bias), name="adamw_small")
    outs = [total[0, 0], grad_x[None]]
    for k in range(4):
        outs += [o_n1[k], o_in[k][None], o_ca[k][None], o_al[k], o_dt[k], o_gn[k], o_out[k][None], o_n2[k], o_up[k][None],
                 o_cf[k][None], o_down[k][None], o_fin[k][0]]
    return tuple(outs)
```

```python
import functools

import jax
import jax.numpy as jnp
from jax import lax
from jax.experimental import pallas as pl
from jax.experimental.pallas import tpu as pltpu

F32 = jnp.float32
BF16 = jnp.bfloat16

N_DEV = 8
D_MODEL = 1024
GDN_HEADS = 4
GDN_DIM = 128
GDN_WIDTH = GDN_HEADS * GDN_DIM
GDN_CONV = 4
CHUNK = 64
CHUNKS_PER_STEP = 4
DIL_HEADS = 8
DIL_DIM = 64
DIL_WIDTH = DIL_HEADS * DIL_DIM
DIL_PAIRS = DIL_HEADS // 2
DILATIONS = (1, 4, 16)
BAND = 128
D_FF = 2816
FFN_CONV = 3
EPS = 1e-6
A_COLS = 3 * GDN_WIDTH + 128
HALO = 8

ADAM_LR = 0.001
ADAM_B1 = 0.9
ADAM_B2 = 0.999
ADAM_EPS = 1e-08
ADAM_WD = 0.01
ADAM_STEP = 10

VMEM_LIMIT_BYTES = 56 * 1024 * 1024
NEG_BIG = -1e30


def _params(sem=None):
    return pltpu.CompilerParams(dimension_semantics=sem, vmem_limit_bytes=VMEM_LIMIT_BYTES)


def _sds(shape, dtype=F32):
    return jax.ShapeDtypeStruct(shape, dtype)


def _bdot(a, b):
    return jnp.dot(a.astype(BF16), b.astype(BF16), preferred_element_type=F32)


def _bdot_nt(a, b):
    return lax.dot_general(a.astype(BF16), b.astype(BF16), (((1,), (1,)), ((), ())), preferred_element_type=F32)


def _bdot_tn(a, b):
    return lax.dot_general(a.astype(BF16), b.astype(BF16), (((0,), (0,)), ((), ())), preferred_element_type=F32)


def _split(a):
    hi = a.astype(BF16)
    lo = (a - hi.astype(F32)).astype(BF16)
    return hi, lo


def _dot3(a, b, dims):
    ah, al = _split(a)
    bh, bl = _split(b)
    d = functools.partial(lax.dot_general, dimension_numbers=(dims, ((), ())), preferred_element_type=F32)
    return d(ah, bh) + (d(al, bh) + d(ah, bl))


def _exact_tri_dot(tri, g):
    g1 = g.astype(BF16)
    r1 = g - g1.astype(F32)
    g2 = r1.astype(BF16)
    g3 = (r1 - g2.astype(F32)).astype(BF16)
    t = tri.astype(BF16)
    d = functools.partial(jnp.dot, preferred_element_type=F32)
    return d(t, g1) + (d(t, g2) + d(t, g3))


def _sigmoid(x):
    return 1.0 / (1.0 + jnp.exp(-x))


def _dsilu(x, sg):
    return sg * (1.0 + x * (1.0 - sg))


def _rms_bwd_rows(dh, x, w):
    r = lax.rsqrt(jnp.mean(x * x, axis=-1, keepdims=True) + EPS)
    xh = x * r
    gw = dh * w
    return r * (gw - xh * jnp.mean(gw * xh, axis=-1, keepdims=True)), jnp.sum(dh * xh, axis=0, keepdims=True)


def _mm(a, b, *, name, ta=False, tb=False, res=None, norm_bwd=None, after=(), out_dtype=F32, tm=512, tn=512, tk=512):
    if ta:
        K, M = a.shape
    else:
        M, K = a.shape
    if tb:
        N, Kb = b.shape
    else:
        Kb, N = b.shape
    assert K == Kb, (a.shape, b.shape)
    tm, tn, tk = min(tm, M), min(tn, N), min(tk, K)
    assert M % tm == 0 and N % tn == 0 and K % tk == 0, (name, M, N, K, tm, tn, tk)
    nk = K // tk
    dims = (((0 if ta else 1,), (1 if tb else 0,)), ((), ()))
    has_res = res is not None
    has_norm = norm_bwd is not None
    assert not has_norm or tn == N

    def body(*refs):
        a_ref, b_ref = refs[:2]
        r_ref = refs[2] if has_res else None
        if has_norm:
            x_ref, w_ref, skip_ref = refs[2 + has_res:5 + has_res]
            o_ref, dw_ref, acc_ref = refs[-3:]
        else:
            o_ref, acc_ref = refs[-2:]
        i, k = pl.program_id(0), pl.program_id(2)
        part = lax.dot_general(a_ref[...].astype(BF16), b_ref[...].astype(BF16), dims, preferred_element_type=F32)

        @pl.when(k == 0)
        def _():
            acc_ref[...] = part

        @pl.when(k > 0)
        def _():
            acc_ref[...] += part

        @pl.when(k == nk - 1)
        def _():
            r = acc_ref[...]
            if has_res:
                r = r + r_ref[...]
            if has_norm:
                dx, dw = _rms_bwd_rows(r, x_ref[...], w_ref[...])
                o_ref[...] = skip_ref[...] + dx

                @pl.when(i == 0)
                def _():
                    dw_ref[...] = dw

                @pl.when(i > 0)
                def _():
                    dw_ref[...] += dw
            else:
                o_ref[...] = r.astype(out_dtype)

    a_spec = pl.BlockSpec((tk, tm), lambda i, j, k: (k, i)) if ta else pl.BlockSpec((tm, tk), lambda i, j, k: (i, k))
    b_spec = pl.BlockSpec((tn, tk), lambda i, j, k: (j, k)) if tb else pl.BlockSpec((tk, tn), lambda i, j, k: (k, j))
    o_spec = pl.BlockSpec((tm, tn), lambda i, j, k: (i, j))
    one = pl.BlockSpec((1, tn), lambda i, j, k: (0, 0))
    in_specs = [a_spec, b_spec] + [o_spec] * has_res + ([o_spec, one, o_spec] if has_norm else []) + [ANY] * len(after)
    args = (a, b) + ((res,) if has_res else ()) + (tuple(norm_bwd) if has_norm else ()) + tuple(after)
    return pl.pallas_call(
        body, name=name, grid=(M // tm, N // tn, nk), in_specs=in_specs,
        out_specs=[o_spec, one] if has_norm else o_spec,
        out_shape=[_sds((M, N)), _sds((1, N))] if has_norm else _sds((M, N), out_dtype),
        scratch_shapes=[pltpu.VMEM((tm, tn), F32)],
        compiler_params=_params(("arbitrary" if has_norm else "parallel", "parallel", "arbitrary")),
    )(*args)


def _in_proj(x, norm_w, w_land, *, name, after=(), tm=512):
    S, D = x.shape

    def body(x_ref, nw_ref, land_ref, *rest):
        h_ref, pa_ref, pz_ref, pb_ref, *scratch = rest[len(after):]

        @pl.when(pl.program_id(0) == 0)
        def _():
            _fetch_w_in(land_ref, *scratch)

        xv = x_ref[...]
        r = lax.rsqrt(jnp.mean(xv * xv, axis=-1, keepdims=True) + EPS)
        h = (xv * r * nw_ref[...]).astype(BF16)
        h_ref[...] = h
        for w_ref, p_ref in zip(scratch[:3], (pa_ref, pz_ref, pb_ref)):
            p_ref[...] = lax.dot_general(h, w_ref[...], (((1,), (1,)), ((), ())), preferred_element_type=F32)

    row = lambda n: pl.BlockSpec((tm, n), lambda i: (i, 0))
    full = lambda a: pl.BlockSpec(a.shape, lambda i: (0, 0))
    return pl.pallas_call(
        body, name=name, grid=(S // tm,), in_specs=[row(D), full(norm_w), ANY] + [ANY] * len(after),
        out_specs=[row(D)] + [row(n) for n in _W_IN_ROWS],
        out_shape=[_sds((S, D), BF16)] + [_sds((S, n)) for n in _W_IN_ROWS],
        scratch_shapes=_w_in_scratch(D), compiler_params=_params(("arbitrary",)),
    )(x, norm_w, w_land, *after)


def _in_proj_dx(ds, w_land, x, norm_w, skip, *, name, after=(), tm=512):
    S, D = x.shape
    n = len(ds)

    def body(*refs):
        d_refs, land_ref = refs[:n], refs[n]
        x_ref, nw_ref, skip_ref = refs[n + 1:n + 4]
        o_ref, dw_ref, *scratch = refs[n + 4 + len(after):]
        w_refs = scratch[:n]
        i = pl.program_id(0)

        @pl.when(i == 0)
        def _():
            _fetch_w_in(land_ref, *scratch)

        dh = jnp.dot(d_refs[0][...], w_refs[0][...], preferred_element_type=F32)
        for d_ref, w_ref in zip(d_refs[1:], w_refs[1:]):
            dh = dh + jnp.dot(d_ref[...], w_ref[...], preferred_element_type=F32)
        dx, dw = _rms_bwd_rows(dh, x_ref[...], nw_ref[...])
        o_ref[...] = skip_ref[...] + dx

        @pl.when(i == 0)
        def _():
            dw_ref[...] = dw

        @pl.when(i > 0)
        def _():
            dw_ref[...] += dw

    row = lambda c: pl.BlockSpec((tm, c), lambda i: (i, 0))
    full = lambda a: pl.BlockSpec(a.shape, lambda i: (0, 0))
    return pl.pallas_call(
        body, name=name, grid=(S // tm,),
        in_specs=[row(d.shape[1]) for d in ds] + [ANY, row(D), full(norm_w), row(D)] + [ANY] * len(after),
        out_specs=[row(D), pl.BlockSpec((1, D), lambda i: (0, 0))], out_shape=[_sds((S, D)), _sds((1, D))],
        scratch_shapes=_w_in_scratch(D), compiler_params=_params(("arbitrary",)),
    )(*ds, w_land, x, norm_w, skip, *after)


def _out_proj_norm(a, w, x, norm_w, *, name, tm=512):
    S, D = x.shape

    def body(a_ref, w_ref, x_ref, nw_ref, x1_ref, h_ref):
        x1 = x_ref[...] + jnp.dot(a_ref[...], w_ref[...], preferred_element_type=F32)
        x1_ref[...] = x1
        r = lax.rsqrt(jnp.mean(x1 * x1, axis=-1, keepdims=True) + EPS)
        h_ref[...] = (x1 * r * nw_ref[...]).astype(BF16)

    row = pl.BlockSpec((tm, D), lambda i: (i, 0))
    return pl.pallas_call(
        body, name=name, grid=(S // tm,),
        in_specs=[pl.BlockSpec((tm, a.shape[1]), lambda i: (i, 0)), pl.BlockSpec(w.shape, lambda i: (0, 0)), row,
                  pl.BlockSpec((1, D), lambda i: (0, 0))],
        out_specs=[row, row], out_shape=[_sds((S, D)), _sds((S, D), BF16)], compiler_params=_params(("parallel",)),
    )(a, w, x, norm_w)


def _shifted(x, start, n):
    aligned = -(-start // HALO) * HALO
    assert aligned + n <= x.shape[0], (start, n, x.shape)
    return (x if aligned == start else pltpu.roll(x, aligned - start, axis=0))[aligned:aligned + n]


def _conv_rows(prev, cur, w, taps):
    n = cur.shape[0]
    xs = jnp.concatenate([prev, cur], axis=0)
    base = HALO - (taps - 1)
    out = _shifted(xs, base, n) * w[0:1]
    for i in range(1, taps):
        out = out + _shifted(xs, base + i, n) * w[i:i + 1]
    return out


def _conv_rows_bwd(cur_d, next_d, prev_x, cur_x, w, taps):
    n = cur_d.shape[0]
    ds = jnp.concatenate([cur_d, next_d], axis=0)
    dx = _shifted(ds, taps - 1, n) * w[0:1]
    for i in range(1, taps):
        dx = dx + _shifted(ds, taps - 1 - i, n) * w[i:i + 1]
    xs = jnp.concatenate([prev_x, cur_x], axis=0)
    base = HALO - (taps - 1)
    dws = [jnp.sum(cur_d * _shifted(xs, base + i, n), axis=0, keepdims=True) for i in range(taps)]
    return dx, jnp.concatenate(dws, axis=0)


def _halo_specs(tm, width, col, nblk):
    per = tm // HALO
    prev = pl.BlockSpec((HALO, width), lambda i, *_: (jnp.maximum(i * per - 1, 0), col))
    nxt = pl.BlockSpec((HALO, width), lambda i, *_: (jnp.minimum((i + 1) * per, nblk * per - 1), col))
    return prev, nxt


def _softplus(x):
    return jnp.maximum(x, 0.0) + jnp.log1p(jnp.exp(-jnp.abs(x)))


def _chunk_tri(tm, upper=False):
    r = lax.broadcasted_iota(jnp.int32, (tm, tm), 0)
    c = lax.broadcasted_iota(jnp.int32, (tm, tm), 1)
    same = lax.div(r, CHUNK) == lax.div(c, CHUNK)
    order = (c >= r) if upper else (c <= r)
    return jnp.where(same & order, 1.0, 0.0)


def _gdn_prep_fwd(proj_a, conv_w, a_log, dt_bias, *, name, tm=256):
    S = proj_a.shape[0]
    nblk = S // tm
    W3 = 3 * GDN_WIDTH

    def body(cur_ref, prev_ref, ba_ref, cw_ref, al_ref, dt_ref, qn_ref, kn_ref, v_ref, gcb_ref, bb_ref):
        i = pl.program_id(0)
        prev = jnp.where(i > 0, prev_ref[...], 0.0)
        c = _conv_rows(prev, cur_ref[...], cw_ref[...], GDN_CONV)
        a = c * _sigmoid(c)
        ba = ba_ref[...]
        lane = lax.broadcasted_iota(jnp.int32, (tm, 128), 1)
        g4 = jnp.zeros((tm, 128), F32)
        for h in range(GDN_HEADS):
            sl = slice(GDN_DIM * h, GDN_DIM * (h + 1))
            qh = a[:, GDN_DIM * h:GDN_DIM * (h + 1)]
            kh = a[:, GDN_WIDTH + GDN_DIM * h:GDN_WIDTH + GDN_DIM * (h + 1)]
            qn_ref[:, sl] = qh * (lax.rsqrt(jnp.sum(qh * qh, axis=-1, keepdims=True) + EPS) * (GDN_DIM ** -0.5))
            kn_ref[:, sl] = kh * lax.rsqrt(jnp.sum(kh * kh, axis=-1, keepdims=True) + EPS)
            beta = _sigmoid(ba[:, h:h + 1])
            bb_ref[:, sl] = jnp.broadcast_to(beta, (tm, GDN_DIM))
            g = -jnp.exp(al_ref[0:1, h:h + 1]) * _softplus(ba[:, GDN_HEADS + h:GDN_HEADS + h + 1] + dt_ref[0:1, h:h + 1])
            g4 = jnp.where(lane == h, g, g4)
        v_ref[...] = a[:, 2 * GDN_WIDTH:]
        gc = _exact_tri_dot(_chunk_tri(tm), g4)
        for h in range(GDN_HEADS):
            gcb_ref[:, GDN_DIM * h:GDN_DIM * (h + 1)] = jnp.broadcast_to(gc[:, h:h + 1], (tm, GDN_DIM))

    prev_spec, _ = _halo_specs(tm, W3, 0, nblk)
    row = pl.BlockSpec((tm, GDN_WIDTH), lambda i: (i, 0))
    small = lambda a: pl.BlockSpec(a.shape, lambda i: (0, 0))
    return pl.pallas_call(
        body, name=name, grid=(nblk,),
        in_specs=[pl.BlockSpec((tm, W3), lambda i: (i, 0)), prev_spec,
                  pl.BlockSpec((tm, 128), lambda i: (i, W3 // 128)), small(conv_w), small(a_log), small(dt_bias)],
        out_specs=[row] * 5, out_shape=[_sds((S, GDN_WIDTH))] * 5, compiler_params=_params(("parallel",)),
    )(proj_a, proj_a, proj_a, conv_w, a_log, dt_bias)


GDN_STACK = GDN_HEADS * CHUNK


def _stack(ref, rows):
    return jnp.concatenate([ref[rows, GDN_DIM * h:GDN_DIM * (h + 1)] for h in range(GDN_HEADS)], axis=0)


def _unstack_to(ref, rows, x):
    for h in range(GDN_HEADS):
        ref[rows, GDN_DIM * h:GDN_DIM * (h + 1)] = x[CHUNK * h:CHUNK * (h + 1)].astype(ref.dtype)


def _stack_masks():
    r = lax.broadcasted_iota(jnp.int32, (GDN_STACK, GDN_STACK), 0)
    c = lax.broadcasted_iota(jnp.int32, (GDN_STACK, GDN_STACK), 1)
    same = (r & -CHUNK) == (c & -CHUNK)
    return same & (r >= c), same & (r > c), r == c


def _stack_decay(gs, bs, incl):
    g2 = jnp.concatenate([gs, gs], axis=1)
    diff = g2 - g2.T
    dec = jnp.where(incl, jnp.exp(jnp.where(incl, diff, 0.0)), 0.0)
    return dec, jnp.concatenate([bs, bs], axis=1).T


def _head_mask():
    r = lax.broadcasted_iota(jnp.int32, (GDN_STACK, GDN_WIDTH), 0)
    c = lax.broadcasted_iota(jnp.int32, (GDN_STACK, GDN_WIDTH), 1)
    return (r & -CHUNK) * (GDN_DIM // CHUNK) == (c & -GDN_DIM)


def _head_spread(x):
    return jnp.where(_head_mask(), jnp.concatenate([x] * GDN_HEADS, axis=1), 0.0)


def _head_diag(x):
    xm = jnp.where(_head_mask(), x, 0.0)
    out = xm[:, 0:GDN_DIM]
    for h in range(1, GDN_HEADS):
        out = out + xm[:, GDN_DIM * h:GDN_DIM * (h + 1)]
    return out


def _last_rows(gs, n):
    return jnp.concatenate([jnp.broadcast_to(gs[CHUNK * (h + 1) - 1:CHUNK * (h + 1)], (n, GDN_DIM)) for h in range(GDN_HEADS)], axis=0)


def _gdn_chunk_fwd(qn, kn, v, gcb, bb, *, name):
    S = qn.shape[0]

    def body(qn_ref, kn_ref, v_ref, gcb_ref, bb_ref, uv_ref, wk_ref, at_ref, t_ref, wkb_ref, qdb_ref, keb_ref):
        incl, strict, diag = _stack_masks()
        for c in range(CHUNKS_PER_STEP):
            rows = slice(CHUNK * c, CHUNK * (c + 1))
            srows = slice(GDN_STACK * c, GDN_STACK * (c + 1))
            q, k, vv, gs, bs = [_stack(r, rows) for r in (qn_ref, kn_ref, v_ref, gcb_ref, bb_ref)]
            dec, bt = _stack_decay(gs, bs, incl)
            p = -jnp.where(strict, dec * _bdot_nt(k, k) * bt, 0.0)
            t = jnp.where(diag, 1.0, 0.0) + p
            for _ in range(5):
                p = _bdot(p, p)
                t = t + _bdot(t, p)
            sol = _dot3(t, jnp.concatenate([vv, jnp.exp(gs) * k], axis=1), ((1,), (0,)))
            _unstack_to(uv_ref, rows, sol[:, :GDN_DIM])
            _unstack_to(wk_ref, rows, sol[:, GDN_DIM:])
            at_ref[srows, :] = dec * _bdot_nt(q, k) * bt
            t_ref[srows, :] = t
            wkb_ref[srows, :] = _head_spread(sol[:, GDN_DIM:]).astype(BF16)
            qdb_ref[srows, :] = _head_spread(q * jnp.exp(gs)).astype(BF16)
            keb_ref[srows, :] = _head_spread(k * jnp.exp(_last_rows(gs, CHUNK) - gs) * bs).astype(BF16)

    step = CHUNKS_PER_STEP * CHUNK
    row = pl.BlockSpec((step, GDN_WIDTH), lambda n: (n, 0))
    sq = pl.BlockSpec((CHUNKS_PER_STEP * GDN_STACK, GDN_STACK), lambda n: (n, 0))
    wide = pl.BlockSpec((CHUNKS_PER_STEP * GDN_STACK, GDN_WIDTH), lambda n: (n, 0))
    nsq = S // CHUNK * GDN_STACK
    return pl.pallas_call(
        body, name=name, grid=(S // step,), in_specs=[row] * 5, out_specs=[row, row, sq, sq, wide, wide, wide],
        out_shape=[_sds((S, GDN_WIDTH)), _sds((S, GDN_WIDTH)), _sds((nsq, GDN_STACK)), _sds((nsq, GDN_STACK))]
        + [_sds((nsq, GDN_WIDTH), BF16)] * 3,
        compiler_params=_params(("parallel",)),
    )(qn, kn, v, gcb, bb)


SCAN_CHUNKS = 8


def _gdn_scan_fwd(uv, at, wkb, qdb, keb, gcb, proj_z, gnw, *, name):
    S = uv.shape[0]
    nc = S // CHUNK

    def body(uv_ref, at_ref, wkb_ref, qdb_ref, keb_ref, gcb_ref, z_ref, gnw_ref, o_ref, u_ref, sp_ref, oa_ref, st_ref):
        n = pl.program_id(0)

        @pl.when(n == 0)
        def _():
            st_ref[...] = jnp.zeros_like(st_ref)

        for c in range(SCAN_CHUNKS):
            rows = slice(CHUNK * c, CHUNK * (c + 1))
            srows = slice(GDN_STACK * c, GDN_STACK * (c + 1))
            st = st_ref[...]
            sp_ref[GDN_WIDTH * c:GDN_WIDTH * (c + 1), :] = st
            uv, gs, z = [_stack(r, rows) for r in (uv_ref, gcb_ref, z_ref)]
            u = uv - _bdot(wkb_ref[srows, :], st)
            o = _bdot(qdb_ref[srows, :], st) + _bdot(at_ref[srows, :], u)
            st_ref[...] = jnp.exp(_last_rows(gs, GDN_DIM)) * st + _bdot_tn(keb_ref[srows, :], u)
            _unstack_to(u_ref, rows, u)
            _unstack_to(o_ref, rows, o)
            r = lax.rsqrt(jnp.mean(o * o, axis=-1, keepdims=True) + EPS)
            oa = o * r * gnw_ref[...] * (z * _sigmoid(z))
            oa_ref[rows, :] = jnp.concatenate([oa[CHUNK * h:CHUNK * (h + 1)] for h in range(GDN_HEADS)], axis=1).astype(BF16)

    row = pl.BlockSpec((SCAN_CHUNKS * CHUNK, GDN_WIDTH), lambda n: (n, 0))
    sq = pl.BlockSpec((SCAN_CHUNKS * GDN_STACK, GDN_STACK), lambda n: (n, 0))
    wide = pl.BlockSpec((SCAN_CHUNKS * GDN_STACK, GDN_WIDTH), lambda n: (n, 0))
    return pl.pallas_call(
        body, name=name, grid=(nc // SCAN_CHUNKS,),
        in_specs=[row, sq, wide, wide, wide, row, row, pl.BlockSpec((1, GDN_DIM), lambda n: (0, 0))],
        out_specs=[row, row, pl.BlockSpec((SCAN_CHUNKS * GDN_WIDTH, GDN_DIM), lambda n: (n, 0)), row],
        out_shape=[_sds((S, GDN_WIDTH)), _sds((S, GDN_WIDTH)), _sds((nc * GDN_WIDTH, GDN_DIM)), _sds((S, 2 * GDN_WIDTH), BF16)],
        scratch_shapes=[pltpu.VMEM((GDN_WIDTH, GDN_DIM), F32)],
        compiler_params=_params(("arbitrary",)),
    )(uv, at, wkb, qdb, keb, gcb, proj_z, gnw)


def _gdn_scan_bwd(d_oab, o, proj_z, gnw, sp, u, at, wkb, qdb, keb, gcb, *, name, after=()):
    S = o.shape[0]
    nc = S // CHUNK
    ns = nc // SCAN_CHUNKS

    def body(do_ref, o_ref, z_ref, gnw_ref, sp_ref, u_ref, at_ref, wkb_ref, qdb_ref, keb_ref, gcb_ref, *rest):
        dz_ref, dgn_ref, du_ref, dwk_ref, dat_ref, dqd_ref, dke_ref, dgl_ref, ds_ref = rest[len(after):]
        n = pl.program_id(0)

        @pl.when(n == 0)
        def _():
            ds_ref[...] = jnp.zeros_like(ds_ref)
            dgn_ref[...] = jnp.zeros_like(dgn_ref)

        gw = gnw_ref[...]
        for c in reversed(range(SCAN_CHUNKS)):
            rows = slice(CHUNK * c, CHUNK * (c + 1))
            srows = slice(GDN_STACK * c, GDN_STACK * (c + 1))
            d_oa, oo, z, uu, gs = [_stack(r, rows) for r in (do_ref, o_ref, z_ref, u_ref, gcb_ref)]
            sg = _sigmoid(z)
            r = lax.rsqrt(jnp.mean(oo * oo, axis=-1, keepdims=True) + EPS)
            xh = oo * r
            dy = d_oa * (z * sg)
            _unstack_to(dz_ref, rows, d_oa * (xh * gw) * _dsilu(z, sg))
            dgn_ref[...] += jnp.sum(dy * xh, axis=0, keepdims=True)
            dxh = dy * gw
            do = r * (dxh - xh * jnp.mean(dxh * xh, axis=-1, keepdims=True))

            st = sp_ref[GDN_WIDTH * c:GDN_WIDTH * (c + 1), :]
            dst = ds_ref[...]
            ge = jnp.exp(_last_rows(gs, GDN_DIM))
            _unstack_to(dqd_ref, rows, _head_diag(_bdot_nt(do, st)))
            dat_ref[srows, :] = _bdot_nt(do, uu)
            du = _bdot_tn(at_ref[srows, :], do) + _bdot(keb_ref[srows, :], dst)
            _unstack_to(dke_ref, rows, _head_diag(_bdot_nt(uu, dst)))
            prod = dst * st
            for h in range(GDN_HEADS):
                blk = prod[GDN_DIM * h:GDN_DIM * (h + 1)]
                dge = jnp.sum(jnp.sum(blk, axis=1, keepdims=True), axis=0, keepdims=True)
                dgl_ref[c, :, GDN_DIM * h:GDN_DIM * (h + 1)] = jnp.broadcast_to(dge * ge[GDN_DIM * h:GDN_DIM * h + 1], (8, GDN_DIM))
            ds_ref[...] = _bdot_tn(qdb_ref[srows, :], do) + ge * dst - _bdot_tn(wkb_ref[srows, :], du)
            _unstack_to(du_ref, rows, du)
            _unstack_to(dwk_ref, rows, -_head_diag(_bdot_nt(du, st)))

    rev = lambda n: (ns - 1 - n, 0)
    row = pl.BlockSpec((SCAN_CHUNKS * CHUNK, GDN_WIDTH), rev)
    sq = pl.BlockSpec((SCAN_CHUNKS * GDN_STACK, GDN_STACK), rev)
    wide = pl.BlockSpec((SCAN_CHUNKS * GDN_STACK, GDN_WIDTH), rev)
    one = pl.BlockSpec((1, GDN_DIM), lambda n: (0, 0))
    return pl.pallas_call(
        body, name=name, grid=(ns,),
        in_specs=[row, row, row, one, pl.BlockSpec((SCAN_CHUNKS * GDN_WIDTH, GDN_DIM), rev), row, sq, wide, wide, wide, row]
        + [ANY] * len(after),
        out_specs=[row, one, row, row, sq, row, row, pl.BlockSpec((SCAN_CHUNKS, 8, GDN_WIDTH), lambda n: (ns - 1 - n, 0, 0))],
        out_shape=[_sds((S, GDN_WIDTH), BF16), _sds((1, GDN_DIM)), _sds((S, GDN_WIDTH)), _sds((S, GDN_WIDTH)),
                   _sds((nc * GDN_STACK, GDN_STACK)), _sds((S, GDN_WIDTH)), _sds((S, GDN_WIDTH)), _sds((nc, 8, GDN_WIDTH))],
        scratch_shapes=[pltpu.VMEM((GDN_WIDTH, GDN_DIM), F32)],
        compiler_params=_params(("arbitrary",)),
    )(d_oab, o, proj_z, gnw, sp, u, at, wkb, qdb, keb, gcb, *after)


def _gdn_chunk_bwd(qn, kn, gcb, bb, tmat, uv, wk, du, dwk, dat, dqd, dke, dgl, *, name):
    S = qn.shape[0]

    def body(qn_ref, kn_ref, gcb_ref, bb_ref, t_ref, uv_ref, wk_ref, du_ref, dwk_ref, dat_ref, dqd_ref, dke_ref,
             dgl_ref, dq_ref, dk_ref, dv_ref, dg_ref, dbeta_ref):
        incl, strict, _ = _stack_masks()
        lane = lax.broadcasted_iota(jnp.int32, (CHUNK, 128), 1)
        rowi = lax.broadcasted_iota(jnp.int32, (CHUNK, 1), 0)
        rsum = lambda x: jnp.sum(x, axis=-1, keepdims=True)
        for c in range(CHUNKS_PER_STEP):
            rows = slice(CHUNK * c, CHUNK * (c + 1))
            srows = slice(GDN_STACK * c, GDN_STACK * (c + 1))
            q, k, gs, bs, uv, wk, du, dwk, dqd, dke = [
                _stack(r, rows) for r in (qn_ref, kn_ref, gcb_ref, bb_ref, uv_ref, wk_ref, du_ref, dwk_ref, dqd_ref, dke_ref)]
            dec, bt = _stack_decay(gs, bs, incl)
            kk = _bdot_nt(k, k)
            qk = _bdot_nt(q, k)
            d_rhs = _dot3(t_ref[srows, :], jnp.concatenate([du, dwk], axis=1), ((0,), (0,)))
            sol = jnp.concatenate([uv, wk], axis=1)
            d_l = jnp.where(strict, -_dot3(d_rhs, sol, ((1,), (1,))), 0.0)
            d_a = jnp.where(incl, dat_ref[srows, :], 0.0)
            gam = jnp.exp(gs)
            e = jnp.exp(_last_rows(gs, CHUNK) - gs)
            d_gk = d_rhs[:, GDN_DIM:]
            ml = d_l * dec * bt
            ma = d_a * dec * bt
            _unstack_to(dq_ref, rows, _bdot(ma, k) + dqd * gam)
            _unstack_to(dk_ref, rows, _bdot(ml + ml.T, k) + _bdot_tn(ma, q) + d_gk * gam + dke * (e * bs))
            _unstack_to(dv_ref, rows, d_rhs[:, :GDN_DIM])
            wb = d_l * dec * kk + d_a * dec * qk
            ew = wb * bt
            s_ke = rsum(dke * k * (e * bs))
            dbeta = rsum(wb.T) + rsum(dke * k * e)
            dgc = rsum(ew) - rsum(ew.T) + rsum(dqd * q * gam) + rsum(d_gk * k * gam) - s_ke
            dgc4 = jnp.zeros((CHUNK, 128), F32)
            db4 = jnp.zeros((CHUNK, 128), F32)
            for h in range(GDN_HEADS):
                hr = slice(CHUNK * h, CHUNK * (h + 1))
                tail = jnp.sum(s_ke[hr], axis=0, keepdims=True) + dgl_ref[c, 0:1, GDN_DIM * h:GDN_DIM * h + 1]
                dgc4 = jnp.where(lane == h, dgc[hr] + jnp.where(rowi == CHUNK - 1, tail, 0.0), dgc4)
                db4 = jnp.where(lane == h, dbeta[hr], db4)
            dg_ref[rows, :] = _exact_tri_dot(_chunk_tri(CHUNK, upper=True), dgc4)
            dbeta_ref[rows, :] = db4

    step = CHUNKS_PER_STEP * CHUNK
    row = pl.BlockSpec((step, GDN_WIDTH), lambda n: (n, 0))
    sq = pl.BlockSpec((CHUNKS_PER_STEP * GDN_STACK, GDN_STACK), lambda n: (n, 0))
    col = pl.BlockSpec((step, 128), lambda n: (n, 0))
    return pl.pallas_call(
        body, name=name, grid=(S // step,),
        in_specs=[row] * 4 + [sq, row, row, row, row, sq, row, row,
                              pl.BlockSpec((CHUNKS_PER_STEP, 8, GDN_WIDTH), lambda n: (n, 0, 0))],
        out_specs=[row, row, row, col, col],
        out_shape=[_sds((S, GDN_WIDTH))] * 3 + [_sds((S, 128))] * 2, compiler_params=_params(("parallel",)),
    )(qn, kn, gcb, bb, tmat, uv, wk, du, dwk, dat, dqd, dke, dgl)


def _gdn_prep_bwd(dqn, dkn, dv, dg, dbeta, proj_a, conv_w, a_log, dt_bias, *, name, tm=256):
    S = proj_a.shape[0]
    nblk = S // tm
    W3 = 3 * GDN_WIDTH

    def body(dqn_ref, dkn_ref, dv_ref, dg_ref, dbeta_ref, cur_ref, prev_ref, ba_ref, cw_ref, al_ref, dt_ref,
             dc_ref, dba_ref, sm_ref):
        i = pl.program_id(0)
        prev = jnp.where(i > 0, prev_ref[...], 0.0)
        c = _conv_rows(prev, cur_ref[...], cw_ref[...], GDN_CONV)
        sg = _sigmoid(c)
        a = c * sg
        dsl = _dsilu(c, sg)
        ba = ba_ref[...]
        lane = lax.broadcasted_iota(jnp.int32, (tm, 128), 1)
        lane1 = lax.broadcasted_iota(jnp.int32, (1, 128), 1)
        dba = jnp.zeros((tm, 128), F32)
        sm = jnp.zeros((1, 128), F32)
        for h in range(GDN_HEADS):
            sl = slice(GDN_DIM * h, GDN_DIM * (h + 1))
            ks = slice(GDN_WIDTH + GDN_DIM * h, GDN_WIDTH + GDN_DIM * (h + 1))
            qh, kh = a[:, sl], a[:, ks]
            rq = lax.rsqrt(jnp.sum(qh * qh, axis=-1, keepdims=True) + EPS)
            rk = lax.rsqrt(jnp.sum(kh * kh, axis=-1, keepdims=True) + EPS)
            qhat, khat = qh * rq, kh * rk
            dyq = dqn_ref[:, sl] * (GDN_DIM ** -0.5)
            dyk = dkn_ref[:, sl]
            dq = rq * (dyq - qhat * jnp.sum(dyq * qhat, axis=-1, keepdims=True))
            dk = rk * (dyk - khat * jnp.sum(dyk * khat, axis=-1, keepdims=True))
            dc_ref[:, sl] = dq * dsl[:, sl]
            dc_ref[:, ks] = dk * dsl[:, ks]
            beta = _sigmoid(ba[:, h:h + 1])
            db = dbeta_ref[:, h:h + 1] * beta * (1.0 - beta)
            aneg = -jnp.exp(al_ref[0:1, h:h + 1])
            xa = ba[:, GDN_HEADS + h:GDN_HEADS + h + 1] + dt_ref[0:1, h:h + 1]
            dgh = dg_ref[:, h:h + 1]
            dxa = dgh * aneg * _sigmoid(xa)
            dba = jnp.where(lane == h, db, dba)
            dba = jnp.where(lane == GDN_HEADS + h, dxa, dba)
            d_alog = jnp.sum(dgh * _softplus(xa), axis=0, keepdims=True) * aneg
            sm = jnp.where(lane1 == h, d_alog, sm)
            sm = jnp.where(lane1 == GDN_HEADS + h, jnp.sum(dxa, axis=0, keepdims=True), sm)
        vs = slice(2 * GDN_WIDTH, W3)
        dc_ref[:, vs] = dv_ref[...] * dsl[:, vs]
        dba_ref[...] = dba

        @pl.when(i == 0)
        def _():
            sm_ref[...] = sm

        @pl.when(i > 0)
        def _():
            sm_ref[...] += sm

    prev_spec, _ = _halo_specs(tm, W3, 0, nblk)
    row = pl.BlockSpec((tm, GDN_WIDTH), lambda i: (i, 0))
    col = pl.BlockSpec((tm, 128), lambda i: (i, 0))
    small = lambda a: pl.BlockSpec(a.shape, lambda i: (0, 0))
    return pl.pallas_call(
        body, name=name, grid=(nblk,),
        in_specs=[row, row, row, col, col, pl.BlockSpec((tm, W3), lambda i: (i, 0)), prev_spec,
                  pl.BlockSpec((tm, 128), lambda i: (i, W3 // 128)), small(conv_w), small(a_log), small(dt_bias)],
        out_specs=[pl.BlockSpec((tm, W3), lambda i: (i, 0)), col, pl.BlockSpec((1, 128), lambda i: (0, 0))],
        out_shape=[_sds((S, W3)), _sds((S, 128)), _sds((1, 128))], compiler_params=_params(("arbitrary",)),
    )(dqn, dkn, dv, dg, dbeta, proj_a, proj_a, proj_a, conv_w, a_log, dt_bias)


def _gdn_conv_bwd(dc, dba, proj_a, conv_w, *, name, tm=256):
    S = proj_a.shape[0]
    nblk = S // tm
    W3 = 3 * GDN_WIDTH

    def body(dc_ref, dnext_ref, dba_ref, cur_ref, prev_ref, cw_ref, da_ref, dcw_ref):
        i = pl.program_id(0)
        prev = jnp.where(i > 0, prev_ref[...], 0.0)
        nxt = jnp.where(i < nblk - 1, dnext_ref[...], 0.0)
        dx, dw = _conv_rows_bwd(dc_ref[...], nxt, prev, cur_ref[...], cw_ref[...], GDN_CONV)
        da_ref[:, 0:W3] = dx.astype(BF16)
        da_ref[:, W3:] = dba_ref[...].astype(BF16)

        @pl.when(i == 0)
        def _():
            dcw_ref[...] = dw

        @pl.when(i > 0)
        def _():
            dcw_ref[...] += dw

    prev_spec, next_spec = _halo_specs(tm, W3, 0, nblk)
    wide = pl.BlockSpec((tm, W3), lambda i: (i, 0))
    return pl.pallas_call(
        body, name=name, grid=(nblk,),
        in_specs=[wide, next_spec, pl.BlockSpec((tm, 128), lambda i: (i, 0)), wide, prev_spec,
                  pl.BlockSpec(conv_w.shape, lambda i: (0, 0))],
        out_specs=[pl.BlockSpec((tm, A_COLS), lambda i: (i, 0)), pl.BlockSpec(conv_w.shape, lambda i: (0, 0))],
        out_shape=[_sds((S, A_COLS), BF16), _sds(conv_w.shape)], compiler_params=_params(("arbitrary",)),
    )(dc, dc, dba, proj_a, proj_a, conv_w)


def _band_mask(nk):
    i = lax.broadcasted_iota(jnp.int32, (2 * BAND, nk), 0) & (BAND - 1)
    j = lax.broadcasted_iota(jnp.int32, (2 * BAND, nk), 1)
    if nk == BAND:
        return j <= i
    return (j >= i) & (j <= i + BAND)


def _stack_heads(x, lo):
    return jnp.concatenate([jnp.where(lo, x, 0.0), jnp.where(lo, 0.0, x)], axis=0)


def _stack_cols(x):
    return jnp.concatenate([x[:, 0:1], x[:, DIL_DIM:DIL_DIM + 1]], axis=0)


def _unstack(x, lo):
    return jnp.where(lo, x[0:BAND], x[BAND:2 * BAND])


def _rows(start, size, stride):
    return pl.ds(start, size) if stride == 1 else pl.ds(start, size, stride=stride)


ATTN_LANES = 4


def _attn_blocks(S, visit_many, lanes=ATTN_LANES):
    for d in DILATIONS:
        nb = S // (d * BAND)
        if d == 1:
            half = nb // 2
            visit_many(d, [(0, 0, True), (0, half, False)])

            def pair(n, c):
                visit_many(1, [(0, n, False), (0, n + half, False)])
                return c
            lax.fori_loop(1, half, pair, 0)
        elif nb > 1:
            for r0 in range(0, d, lanes):
                visit_many(d, [(r0 + t, 0, True) for t in range(lanes)])

                def column(n, c, d=d, r0=r0):
                    visit_many(d, [(r0 + t, n, False) for t in range(lanes)])
                    return c
                lax.fori_loop(1, nb, column, 0)
        else:
            def group(g, c, d=d):
                visit_many(d, [(g * lanes + t, 0, True) for t in range(lanes)])
                return c
            lax.fori_loop(0, d // lanes, group, 0)


def _attn_fwd(proj_b, oab, *, name):
    S = proj_b.shape[0]
    scale = DIL_DIM ** -0.5

    def body(q_ref, k_ref, v_ref, oab_in_ref, ob_ref, lse_ref, m_ref, l_ref, acc_ref):
        del oab_in_ref
        lane = lax.broadcasted_iota(jnp.int32, (BAND, 128), 1)
        lo = lane < DIL_DIM
        m_ref[...] = jnp.full_like(m_ref, NEG_BIG)
        l_ref[...] = jnp.zeros_like(l_ref)
        acc_ref[...] = jnp.zeros_like(acc_ref)

        def load(d, r, n, first):
            nk = BAND if first else 2 * BAND
            qrows = _rows(r + n * (BAND * d), BAND, d)
            krows = _rows(r if first else r + (n - 1) * (BAND * d), nk, d)
            return dict(nk=nk, qrows=qrows, q=q_ref[qrows, :] * scale, k=k_ref[krows, :].astype(BF16),
                        v=v_ref[krows, :].astype(BF16), m=m_ref[qrows, :], l=l_ref[qrows, :], acc=acc_ref[qrows, :])

        def compute(b):
            q, k, v = b["q"], b["k"], b["v"]
            s = jnp.where(_band_mask(b["nk"]), _bdot_nt(_stack_heads(q, lo), k), NEG_BIG)
            m_old = _stack_cols(b["m"])
            m_new = jnp.maximum(m_old, jnp.max(s, axis=-1, keepdims=True))
            p = jnp.exp(s - m_new)
            alpha = _unstack(jnp.exp(m_old - m_new), lo)
            l_new = alpha * b["l"] + _unstack(jnp.sum(p, axis=-1, keepdims=True), lo)
            return _unstack(m_new, lo), l_new, alpha * b["acc"] + _unstack(_bdot(p, v), lo)

        def visit_many(d, blocks):
            loaded = [load(d, *blk) for blk in blocks]
            done = [compute(b) for b in loaded]
            for b, (m_new, l_new, acc_new) in zip(loaded, done):
                m_ref[b["qrows"], :] = m_new
                l_ref[b["qrows"], :] = l_new
                acc_ref[b["qrows"], :] = acc_new

        _attn_blocks(S, visit_many)
        ob_ref[...] = (acc_ref[...] / l_ref[...]).astype(BF16)
        lse_ref[...] = m_ref[...] + jnp.log(l_ref[...])

    part = lambda t: pl.BlockSpec((S, 128), lambda p: (0, 3 * p + t))
    return pl.pallas_call(
        body, name=name, grid=(DIL_PAIRS,),
        in_specs=[part(0), part(1), part(2), pl.BlockSpec(memory_space=pl.ANY)],
        out_specs=[pl.BlockSpec((S, 128), lambda p: (0, GDN_WIDTH // 128 + p)), pl.BlockSpec((S, 128), lambda p: (0, p))],
        out_shape=[_sds(oab.shape, BF16), _sds((S, DIL_WIDTH))],
        scratch_shapes=[pltpu.VMEM((S, 128), F32)] * 3, input_output_aliases={3: 0},
        compiler_params=_params(("parallel",)),
    )(proj_b, proj_b, proj_b, oab)


def _attn_bwd(proj_b, oab, d_oab, lse, *, name):
    S = proj_b.shape[0]
    scale = DIL_DIM ** -0.5

    def body(q_ref, k_ref, v_ref, o_ref, do_ref, lse_ref, dqkv_ref, dq_ref, dk_ref, dv_ref, delta_ref):
        lane = lax.broadcasted_iota(jnp.int32, (BAND, 128), 1)
        lo = lane < DIL_DIM
        dq_ref[...] = jnp.zeros_like(dq_ref)
        dk_ref[...] = jnp.zeros_like(dk_ref)
        dv_ref[...] = jnp.zeros_like(dv_ref)
        prod = do_ref[...] * o_ref[...].astype(F32)
        lo_all = lax.broadcasted_iota(jnp.int32, (S, 128), 1) < DIL_DIM
        delta_ref[...] = jnp.where(lo_all, jnp.sum(jnp.where(lo_all, prod, 0.0), axis=-1, keepdims=True),
                                   jnp.sum(jnp.where(lo_all, 0.0, prod), axis=-1, keepdims=True))

        def load(d, r, n, first):
            nk = BAND if first else 2 * BAND
            qrows = _rows(r + n * (BAND * d), BAND, d)
            krows = _rows(r if first else r + (n - 1) * (BAND * d), nk, d)
            return dict(nk=nk, qrows=qrows, krows=krows, q=q_ref[qrows, :] * scale, k=k_ref[krows, :], v=v_ref[krows, :],
                        do=do_ref[qrows, :], delta=delta_ref[qrows, :], lse=lse_ref[qrows, :],
                        dq=dq_ref[qrows, :], dk=dk_ref[krows, :], dv=dv_ref[krows, :])

        def compute(b):
            q, k, v, do = b["q"], b["k"], b["v"], b["do"]
            qs, dos = _stack_heads(q, lo), _stack_heads(do, lo)
            p = jnp.where(_band_mask(b["nk"]), jnp.exp(_bdot_nt(qs, k) - _stack_cols(b["lse"])), 0.0)
            ds = p * (_bdot_nt(dos, v) - _stack_cols(b["delta"]))
            dq = b["dq"] + _unstack(_bdot(ds, k), lo) * scale
            return dq, b["dk"] + _bdot_tn(ds, qs), b["dv"] + _bdot_tn(p, dos)

        def visit_many(d, blocks):
            loaded = [load(d, *blk) for blk in blocks]
            done = [compute(b) for b in loaded]
            for b, (dq, dk, dv) in zip(loaded, done):
                dq_ref[b["qrows"], :] = dq
                dk_ref[b["krows"], :] = dk
                dv_ref[b["krows"], :] = dv

        _attn_blocks(S, visit_many, lanes=2)
        dqkv_ref[:, 0:128] = dq_ref[...].astype(BF16)
        dqkv_ref[:, 128:256] = dk_ref[...].astype(BF16)
        dqkv_ref[:, 256:384] = dv_ref[...].astype(BF16)

    half = lambda p: (0, GDN_WIDTH // 128 + p)
    part = lambda t: pl.BlockSpec((S, 128), lambda p: (0, 3 * p + t))
    return pl.pallas_call(
        body, name=name, grid=(DIL_PAIRS,),
        in_specs=[part(0), part(1), part(2), pl.BlockSpec((S, 128), half), pl.BlockSpec((S, 128), half),
                  pl.BlockSpec((S, 128), lambda p: (0, p))],
        out_specs=pl.BlockSpec((S, 384), lambda p: (0, p)), out_shape=_sds((S, 3 * DIL_WIDTH), BF16),
        scratch_shapes=[pltpu.VMEM((S, 128), F32)] * 4, compiler_params=_params(("parallel",)),
    )(proj_b, proj_b, proj_b, oab, d_oab, lse)


FF_SLAB = 2 * D_FF // N_DEV
FF_PAIRS = N_DEV // 2
ROWS16 = 16


def _taps(w, x, base, n):
    out = _shifted(x, base, n) * w[0:1]
    for t in range(1, FFN_CONV):
        out = out + _shifted(x, base + t, n) * w[t:t + 1]
    return out


def _ffn_fwd(h2, x1, w_up, conv_w, w_down, final_w, tgt, *, name, tm=512):
    S, D = h2.shape
    ni = S // tm
    per = tm // ROWS16

    def body(h_ref, hp_ref, x1_ref, wg_ref, wu_ref, cg_ref, cu_ref, wd_ref, fw_ref, t_ref,
             dx_ref, dxb_ref, dfw_ref, loss_ref, ug_ref, uu_ref, x2_ref):
        i, j = pl.program_id(0), pl.program_id(1)
        hv = jnp.concatenate([hp_ref[...], h_ref[...]], axis=0)
        row = lax.broadcasted_iota(jnp.int32, (tm + ROWS16, 1), 0)
        keep = (i > 0) | (row >= ROWS16)

        def branch(w_ref, c_ref, u_ref):
            u = lax.dot_general(hv, w_ref[...], (((1,), (1,)), ((), ())), preferred_element_type=F32).astype(BF16)
            u_ref[...] = u[ROWS16:]
            return _taps(c_ref[...], jnp.where(keep, u.astype(F32), 0.0), ROWS16 - (FFN_CONV - 1), tm)

        gate = branch(wg_ref, cg_ref, ug_ref)
        up = branch(wu_ref, cu_ref, uu_ref)
        act = (gate * _sigmoid(gate) * up).astype(BF16)
        part = jnp.dot(act, wd_ref[...], preferred_element_type=F32)

        @pl.when(j == 0)
        def _():
            x2_ref[...] = x1_ref[...] + part

        @pl.when((j > 0) & (j < FF_PAIRS - 1))
        def _():
            x2_ref[...] += part

        @pl.when(j == FF_PAIRS - 1)
        def _():
            xv = x2_ref[...] + part
            wv = fw_ref[...]
            r = lax.rsqrt(jnp.mean(xv * xv, axis=-1, keepdims=True) + EPS)
            err = xv * r * wv - t_ref[...]
            lsum = jnp.sum(jnp.sum(err * err, axis=-1, keepdims=True), axis=0, keepdims=True) * (0.5 / D)
            g = err * (1.0 / D)
            xh = xv * r
            gw = g * wv
            dx = r * (gw - xh * jnp.mean(gw * xh, axis=-1, keepdims=True))
            dx_ref[...] = dx
            dxb_ref[...] = dx.astype(BF16)
            dfw = jnp.sum(g * xh, axis=0, keepdims=True)
            lpart = jnp.broadcast_to(lsum, (1, 128))

            @pl.when(i == 0)
            def _():
                dfw_ref[...] = dfw
                loss_ref[...] = lpart

            @pl.when(i > 0)
            def _():
                dfw_ref[...] += dfw
                loss_ref[...] += lpart

    rows = pl.BlockSpec((tm, D), lambda i, j: (i, 0))
    slab = lambda off: pl.BlockSpec((None, FF_SLAB, D), lambda i, j: (j + off, 0, 0))
    cslab = lambda off: pl.BlockSpec((None, FFN_CONV, FF_SLAB), lambda i, j: (j + off, 0, 0))
    uspec = pl.BlockSpec((None, tm, FF_SLAB), lambda i, j: (j, i, 0))
    return pl.pallas_call(
        body, name=name, grid=(ni, FF_PAIRS),
        in_specs=[rows, pl.BlockSpec((ROWS16, D), lambda i, j: (jnp.maximum(i * per - 1, 0), 0)), rows,
                  slab(0), slab(FF_PAIRS), cslab(0), cslab(FF_PAIRS), pl.BlockSpec((FF_SLAB, D), lambda i, j: (j, 0)),
                  pl.BlockSpec((1, D), lambda i, j: (0, 0)), rows],
        out_specs=[rows, rows, pl.BlockSpec((1, D), lambda i, j: (0, 0)), pl.BlockSpec((1, 128), lambda i, j: (0, 0)), uspec, uspec],
        out_shape=[_sds((S, D)), _sds((S, D), BF16), _sds((1, D)), _sds((1, 128)),
                   _sds((FF_PAIRS, S, FF_SLAB), BF16), _sds((FF_PAIRS, S, FF_SLAB), BF16)],
        scratch_shapes=[pltpu.VMEM((tm, D), F32)],
        compiler_params=_params(("arbitrary", "arbitrary")),
    )(h2, h2, x1, w_up, w_up, conv_w, conv_w, w_down, final_w, tgt)


def _ffn_bwd(dx2, h2, ug, uu, conv_w, w_down, *, name, tm=512):
    S, D = h2.shape
    ni = S // tm
    per = tm // ROWS16
    ext = tm + ROWS16

    def body(dx_ref, dxn_ref, h_ref, ug_ref, ugp_ref, ugn_ref, uu_ref, uup_ref, uun_ref, cg_ref, cu_ref, wd_ref,
             du_ref, gd_ref, gup_ref, dcw_ref, acc_d, acc_g, acc_u, acc_cg, acc_cu):
        i = pl.program_id(1)

        @pl.when(i == 0)
        def _():
            acc_d[...] = jnp.zeros_like(acc_d)
            acc_g[...] = jnp.zeros_like(acc_g)
            acc_u[...] = jnp.zeros_like(acc_u)
            acc_cg[...] = jnp.zeros_like(acc_cg)
            acc_cu[...] = jnp.zeros_like(acc_cu)

        dx = dx_ref[...]
        dxe = jnp.concatenate([dx, dxn_ref[...]], axis=0)
        row = lax.broadcasted_iota(jnp.int32, (ext, 1), 0)
        live = (i < ni - 1) | (row < tm)
        d_act = jnp.where(live, lax.dot_general(dxe, wd_ref[...], (((1,), (1,)), ((), ())), preferred_element_type=F32), 0.0)
        rowp = lax.broadcasted_iota(jnp.int32, (ext + ROWS16, 1), 0)
        keep = (i > 0) | (rowp >= ROWS16)

        def pre(cur, prev, nxt):
            return jnp.where(keep, jnp.concatenate([prev[...], cur[...], nxt[...]], axis=0).astype(F32), 0.0)

        uge, uue = pre(ug_ref, ugp_ref, ugn_ref), pre(uu_ref, uup_ref, uun_ref)
        cg, cu = cg_ref[...], cu_ref[...]
        base = ROWS16 - (FFN_CONV - 1)
        gate = _taps(cg, uge, base, ext)
        up = _taps(cu, uue, base, ext)
        sg = _sigmoid(gate)
        silu = gate * sg
        dgc = d_act * up * _dsilu(gate, sg)
        duc = d_act * silu

        def conv_t(w, dc):
            out = _shifted(dc, FFN_CONV - 1, tm) * w[0:1]
            for t in range(1, FFN_CONV):
                out = out + _shifted(dc, FFN_CONV - 1 - t, tm) * w[t:t + 1]
            return out.astype(BF16)

        du_g, du_u = conv_t(cg, dgc), conv_t(cu, duc)
        du_ref[0] = du_g
        du_ref[1] = du_u
        dcw = lambda dc, xe: jnp.concatenate(
            [jnp.sum(dc[0:tm] * _shifted(xe, base + t, tm), axis=0, keepdims=True) for t in range(FFN_CONV)], axis=0)
        acc_cg[0:FFN_CONV, :] += dcw(dgc, uge)
        acc_cu[0:FFN_CONV, :] += dcw(duc, uue)
        tn = (((0,), (0,)), ((), ()))
        act = (silu[0:tm] * up[0:tm]).astype(BF16)
        acc_d[...] += lax.dot_general(act, dx, tn, preferred_element_type=F32)
        hv = h_ref[...]
        acc_g[...] += lax.dot_general(du_g, hv, tn, preferred_element_type=F32)
        acc_u[...] += lax.dot_general(du_u, hv, tn, preferred_element_type=F32)

        @pl.when(i == ni - 1)
        def _():
            gd_ref[...] = acc_d[...].astype(BF16)
            gup_ref[0] = acc_g[...].astype(BF16)
            gup_ref[1] = acc_u[...].astype(BF16)
            dcw_ref[0] = acc_cg[0:FFN_CONV, :]
            dcw_ref[1] = acc_cu[0:FFN_CONV, :]

    last16 = S // ROWS16 - 1
    rows = pl.BlockSpec((tm, D), lambda j, i: (i, 0))
    rows_next = pl.BlockSpec((ROWS16, D), lambda j, i: (jnp.minimum((i + 1) * per, last16), 0))
    u_cur = pl.BlockSpec((None, tm, FF_SLAB), lambda j, i: (j, i, 0))
    u_prev = pl.BlockSpec((None, ROWS16, FF_SLAB), lambda j, i: (j, jnp.maximum(i * per - 1, 0), 0))
    u_next = pl.BlockSpec((None, ROWS16, FF_SLAB), lambda j, i: (j, jnp.minimum((i + 1) * per, last16), 0))
    cslab = lambda off: pl.BlockSpec((None, FFN_CONV, FF_SLAB), lambda j, i: (j + off, 0, 0))
    return pl.pallas_call(
        body, name=name, grid=(FF_PAIRS, ni),
        in_specs=[rows, rows_next, rows, u_cur, u_prev, u_next, u_cur, u_prev, u_next, cslab(0), cslab(FF_PAIRS),
                  pl.BlockSpec((FF_SLAB, D), lambda j, i: (j, 0))],
        out_specs=[pl.BlockSpec((None, 2, tm, FF_SLAB), lambda j, i: (j, 0, i, 0)), pl.BlockSpec((FF_SLAB, D), lambda j, i: (j, 0)),
                   pl.BlockSpec((None, 2, FF_SLAB, D), lambda j, i: (j, 0, 0, 0)),
                   pl.BlockSpec((None, 2, FFN_CONV, FF_SLAB), lambda j, i: (j, 0, 0, 0))],
        out_shape=[_sds((FF_PAIRS, 2, S, FF_SLAB), BF16), _sds((D_FF, D), BF16), _sds((FF_PAIRS, 2, FF_SLAB, D), BF16),
                   _sds((FF_PAIRS, 2, FFN_CONV, FF_SLAB))],
        scratch_shapes=[pltpu.VMEM((FF_SLAB, D), F32), pltpu.VMEM((FF_SLAB, D), F32), pltpu.VMEM((FF_SLAB, D), F32),
                        pltpu.VMEM((8, FF_SLAB), F32), pltpu.VMEM((8, FF_SLAB), F32)],
        compiler_params=_params(("parallel", "arbitrary")),
    )(dx2, dx2, h2, ug, ug, ug, uu, uu, uu, conv_w, conv_w, w_down)


def _pair_slot(p):
    return 2 * (p & (FF_PAIRS - 1)) + (p >> 2)


def _mm_slabs(a, w, *, name, res=None, norm_bwd=None, after=(), tm=1024, tn=1024):
    nk, S, _ = a.shape
    D = w.shape[2]
    has_res = res is not None
    has_norm = norm_bwd is not None
    assert not has_norm or tn == D

    def body(*refs):
        a_ref, w_ref = refs[:2]
        r_ref = refs[2] if has_res else None
        if has_norm:
            x_ref, nw_ref, skip_ref = refs[2 + has_res:5 + has_res]
            o_ref, dw_ref, acc_ref = refs[-3:]
        else:
            o_ref, acc_ref = refs[-2:]
        i, k = pl.program_id(0), pl.program_id(2)
        part = jnp.dot(a_ref[...], w_ref[...], preferred_element_type=F32)

        @pl.when(k == 0)
        def _():
            acc_ref[...] = part

        @pl.when(k > 0)
        def _():
            acc_ref[...] += part

        @pl.when(k == nk - 1)
        def _():
            r = acc_ref[...] + r_ref[...] if has_res else acc_ref[...]
            if has_norm:
                dx, dw = _rms_bwd_rows(r, x_ref[...], nw_ref[...])
                o_ref[...] = skip_ref[...] + dx

                @pl.when(i == 0)
                def _():
                    dw_ref[...] = dw

                @pl.when(i > 0)
                def _():
                    dw_ref[...] += dw
            else:
                o_ref[...] = r

    o_spec = pl.BlockSpec((tm, tn), lambda i, j, k: (i, j))
    one = pl.BlockSpec((1, tn), lambda i, j, k: (0, 0))
    return pl.pallas_call(
        body, name=name, grid=(S // tm, D // tn, nk),
        in_specs=[pl.BlockSpec((None, tm, FF_SLAB), lambda i, j, k: (k, i, 0)),
                  pl.BlockSpec((None, FF_SLAB, tn), lambda i, j, k: (FF_PAIRS * (k & 1) + (k >> 1), 0, j))] + [o_spec] * has_res
        + ([o_spec, one, o_spec] if has_norm else []) + [ANY] * len(after),
        out_specs=[o_spec, one] if has_norm else o_spec, out_shape=[_sds((S, D)), _sds((1, D))] if has_norm else _sds((S, D)),
        scratch_shapes=[pltpu.VMEM((tm, tn), F32)],
        compiler_params=_params(("arbitrary" if has_norm else "parallel", "parallel", "arbitrary")),
    )(*((a, w) + ((res,) if has_res else ()) + (tuple(norm_bwd) if has_norm else ()) + tuple(after)))


def _local_step(x, tgt, norm1_w, w_land, conv_a, a_log, dt_bias, gnw, norm2_w, final_w, late_weights, emit, start_after=()):
    wgrad = functools.partial(_mm, ta=True, out_dtype=BF16)
    h1, proj_a, proj_z, proj_b = _in_proj(x, norm1_w, w_land, after=start_after, name="in_proj")
    qn, kn, v, gcb, bb = _gdn_prep_fwd(proj_a, conv_a, a_log, dt_bias, name="gdn_prep_fwd")
    uv, wk, at, tmat, wkb, qdb, keb = _gdn_chunk_fwd(qn, kn, v, gcb, bb, name="gdn_chunk_fwd")
    o, u, sp, oab = _gdn_scan_fwd(uv, at, wkb, qdb, keb, gcb, proj_z, gnw, name="gdn_scan_fwd")
    oab, lse = _attn_fwd(proj_b, oab, name="attn_fwd")
    w_out, w_up, conv_f, w_down = late_weights(oab)
    x1, h2 = _out_proj_norm(oab, w_out, x, norm2_w, name="out_proj")
    dx2, dx2_b, d_final, loss, ug, uu = _ffn_fwd(h2, x1, w_up, conv_f, w_down, final_w, tgt, name="ffn_fwd")
    du, g_down, g_up, dcw = _ffn_bwd(dx2_b, h2, ug, uu, conv_f, w_down, name="ffn_bwd")
    token = emit("ffn", w_down=g_down, w_up=g_up.reshape(N_DEV, FF_SLAB, -1), conv_f=dcw.reshape(N_DEV, FFN_CONV, -1))
    dx1, d_norm2 = _mm_slabs(du.reshape(N_DEV, -1, FF_SLAB), w_up, norm_bwd=(x1, norm2_w, dx2), after=token, name="ffn_up_dx")
    d_oab = _mm(dx1, w_out, tb=True, name="out_proj_dx", tn=D_MODEL, tk=1024)
    token = emit("out", w_out=wgrad(oab, dx1, name="out_proj_dw", tm=D_MODEL, tn=D_MODEL))
    dz, d_gnw, du, dwk, dat, dqd, dke, dgl = _gdn_scan_bwd(d_oab, o, proj_z, gnw, sp, u, at, wkb, qdb, keb, gcb, after=token, name="gdn_scan_bwd")
    dqn, dkn, dv, dg, dbeta = _gdn_chunk_bwd(qn, kn, gcb, bb, tmat, uv, wk, du, dwk, dat, dqd, dke, dgl, name="gdn_chunk_bwd")
    dc, dba, d_small = _gdn_prep_bwd(dqn, dkn, dv, dg, dbeta, proj_a, conv_a, a_log, dt_bias, name="gdn_prep_bwd")
    d_pa, d_conv_a = _gdn_conv_bwd(dc, dba, proj_a, conv_a, name="gdn_conv_bwd")
    d_pb = _attn_bwd(proj_b, oab, d_oab, lse, name="attn_bwd")
    g_a = wgrad(d_pa, h1, name="proj_a_dw", tm=A_COLS, tn=D_MODEL)
    g_z = wgrad(dz, h1, name="proj_z_dw", tn=D_MODEL)
    g_b = wgrad(d_pb, h1, name="proj_b_dw", tm=768, tn=D_MODEL)
    token = emit("in", w_a=g_a, w_z=g_z, w_b=g_b, conv_a=d_conv_a)
    grad_x, d_norm1 = _in_proj_dx((d_pa, dz, d_pb), w_land, x, norm1_w, dx1, after=token, name="in_proj_dx")
    small = dict(norm1=d_norm1, small=d_small, gnw=d_gnw, norm2=d_norm2, final=d_final)
    return loss, grad_x, small


_O1 = 3 * GDN_WIDTH
_O2 = _O1 + GDN_WIDTH
_O3 = _O2 + 2 * GDN_HEADS


_W_IN_ROWS = (A_COLS, GDN_WIDTH, 3 * DIL_WIDTH)
_IN_ROWS = (_O3 + 3 * DIL_WIDTH) // N_DEV
_ROW_TILE = 16
_IN_STEP = _IN_ROWS - _IN_ROWS % _ROW_TILE
_GAP = 8
_LAND_ROWS = 464
assert _O3 % _ROW_TILE == _ROW_TILE - _GAP and N_DEV - 1 + _GAP + _IN_ROWS <= _LAND_ROWS and _LAND_ROWS % _ROW_TILE == 0


def _padded_row(r):
    return r + (_GAP if r >= _O3 else 0)


def _shifted_slab(w_rows, j, *, name):
    q = w_rows.shape[0] // _IN_ROWS

    def body(j_ref, w_ref, o_ref, pad_ref):
        pad_ref[...] = jnp.zeros_like(pad_ref)
        for dev in range(N_DEV):
            @pl.when(j_ref[0] == dev)
            def _(dev=dev):
                p = lax.broadcasted_iota(jnp.int32, (_LAND_ROWS, 1), 0) + _IN_STEP * dev
                for s in range(q):
                    pad_ref[0:_IN_ROWS, :] = w_ref[pl.ds(s, _IN_ROWS, stride=q), :]
                    rows = pad_ref[...]
                    a = pltpu.roll(rows, dev, 0) if dev else rows
                    b = pltpu.roll(rows, dev + _GAP, 0)
                    o_ref[:, 128 * s:128 * (s + 1)] = jnp.where(p < _O3, a, jnp.where(p >= _O3 + _GAP, b, 0.0)).astype(BF16)

    vm = pl.BlockSpec(memory_space=pltpu.VMEM)
    return pl.pallas_call(
        body, name=name, in_specs=[pl.BlockSpec(memory_space=pltpu.SMEM), vm], out_specs=vm,
        out_shape=_sds((_LAND_ROWS, 128 * q), BF16), scratch_shapes=[pltpu.VMEM((_LAND_ROWS, 128), F32)],
    )(j, w_rows)


def _w_in_plan():
    def dest(p):
        if p < _O1:
            return 0, p
        if p < _O2:
            return 1, p - _O1
        if p < _O2 + _ROW_TILE:
            return 0, _O1
        q = p - _O3 - _GAP
        t, pair = divmod(q // 128, DIL_PAIRS)
        return 2, (3 * pair + t) * 128 + q % 128

    spans = [(_padded_row(_IN_ROWS * j), _padded_row(_IN_ROWS * (j + 1) - 1) + 1) for j in range(N_DEV)]
    runs, seams = [], []
    for p in range(0, spans[-1][1], _ROW_TILE):
        owners = [j for j, (lo, hi) in enumerate(spans) if lo < p + _ROW_TILE and hi > p]
        w, r = dest(p)
        if len(owners) == 2:
            seams.append((w, r, owners[0], p - _IN_STEP * owners[0], owners[1], p - _IN_STEP * owners[1]))
            continue
        (j,) = owners
        last = runs[-1] if runs else None
        if last and last[0] == j and last[2] == w and last[3] + last[4] == r and last[1] + last[4] == p - _IN_STEP * j:
            runs[-1] = last[:4] + (last[4] + _ROW_TILE,)
        else:
            runs.append((j, p - _IN_STEP * j, w, r, _ROW_TILE))
    return runs, seams


def _w_in_scratch(d):
    return [pltpu.VMEM((n, d), BF16) for n in _W_IN_ROWS] + [pltpu.VMEM((N_DEV - 1, _ROW_TILE, d), BF16),
                                                              pltpu.SemaphoreType.DMA(())]


def _fetch_w_in(land_ref, wa_ref, wz_ref, wb_ref, seam_ref, sem):
    w_refs = (wa_ref, wz_ref, wb_ref)
    runs, seams = _w_in_plan()
    copies = [pltpu.make_async_copy(land_ref.at[j, pl.ds(s, n)], w_refs[w].at[pl.ds(r, n)], sem) for j, s, w, r, n in runs]
    for k, (w, r, j0, s0, j1, s1) in enumerate(seams):
        copies.append(pltpu.make_async_copy(land_ref.at[j0, pl.ds(s0, _ROW_TILE)], w_refs[w].at[pl.ds(r, _ROW_TILE)], sem))
        copies.append(pltpu.make_async_copy(land_ref.at[j1, pl.ds(s1, _ROW_TILE)], seam_ref.at[k], sem))
    for i, cp in enumerate(copies):
        cp.start(priority=i % 2)
    tail = _O1 + _ROW_TILE
    wa_ref[tail:, :] = jnp.zeros((A_COLS - tail, wa_ref.shape[1]), BF16)
    for cp in copies:
        cp.wait()
    for k, (w, r, *_) in enumerate(seams):
        both = w_refs[w][r:r + _ROW_TILE, :].astype(F32) + seam_ref[k].astype(F32)
        w_refs[w][r:r + _ROW_TILE, :] = both.astype(BF16)


MESH = pl.DeviceIdType.MESH
ANY = pl.BlockSpec(memory_space=pl.ANY)


def _position():
    return lax.axis_index("x"), lax.axis_index("y"), lax.axis_index("c")


def _slot(p):
    return 4 * p[0] + 2 * p[1] + p[2]


def _all_gather(blocks, *, name):
    n = len(blocks)

    def body(*refs):
        ins, outs = refs[:n], refs[n:2 * n]
        send_sems, recv_sems, local_sems = refs[2 * n:]
        x, y, c = _position()
        me, sibling = (x, y, c), (x, y, 1 - c)
        chips = [(1 - x, y), (x, 1 - y), (1 - x, 1 - y)]

        def copy(a, k, block, to, src=None):
            dst = outs[a].at[_slot(block)]
            return pltpu.make_async_remote_copy(
                src_ref=dst if src is None else src, dst_ref=dst, send_sem=send_sems.at[a, k], recv_sem=recv_sems.at[a, k],
                device_id=to, device_id_type=MESH)

        mine = [pltpu.make_async_copy(ins[a], outs[a].at[_slot(me)], local_sems.at[a]) for a in range(n)]
        for cp in mine:
            cp.start()
        first = []
        for a in range(n):
            first.append(copy(a, 0, me, sibling, src=ins[a]))
            first += [copy(a, 1 + j, me, (*chip, c), src=ins[a]) for j, chip in enumerate(chips)]
        for cp in first:
            cp.start()
        passed = []
        for j, chip in enumerate(chips):
            for a in range(n):
                copy(a, 1 + j, (*chip, c), me).wait_recv()
                fwd = copy(a, 4 + j, (*chip, c), sibling)
                fwd.start()
                passed.append(fwd)
        for a in range(n):
            copy(a, 0, sibling, me).wait_recv()
            for j, chip in enumerate(chips):
                copy(a, 4 + j, (*chip, 1 - c), me).wait_recv()
        for cp in first + passed:
            cp.wait_send()
        for cp in mine:
            cp.wait()

    return pl.pallas_call(
        body, name=name, in_specs=[ANY] * n, out_specs=[ANY] * n,
        out_shape=[_sds((N_DEV,) + b.shape, b.dtype) for b in blocks],
        scratch_shapes=[pltpu.SemaphoreType.DMA((n, 7)), pltpu.SemaphoreType.DMA((n, 7)), pltpu.SemaphoreType.DMA((n,))],
    )(*blocks)


def _gather_direct(block, *, name, after=()):
    def body(in_ref, *rest):
        out_ref, send_sems, recv_sems, local_sem = rest[len(after):]
        x, y, c = _position()
        me = _slot((x, y, c))
        mine = pltpu.make_async_copy(in_ref, out_ref.at[me], local_sem)
        mine.start()
        copies = [pltpu.make_async_remote_copy(
            src_ref=in_ref, dst_ref=out_ref.at[me], send_sem=send_sems.at[k - 1], recv_sem=recv_sems.at[k - 1],
            device_id=_peer_of(k, x, y, c), device_id_type=MESH) for k in range(1, N_DEV)]
        for cp in copies:
            cp.start()
        for cp in copies:
            cp.wait()
        mine.wait()

    return pl.pallas_call(
        body, name=name, in_specs=[pl.BlockSpec(memory_space=pltpu.VMEM)] + [ANY] * len(after),
        out_specs=pl.BlockSpec(memory_space=pltpu.VMEM),
        out_shape=_sds((N_DEV,) + block.shape, block.dtype),
        scratch_shapes=[pltpu.SemaphoreType.DMA((N_DEV - 1,)), pltpu.SemaphoreType.DMA((N_DEV - 1,)), pltpu.SemaphoreType.DMA],
    )(block, *after)


HBM = pl.BlockSpec(memory_space=pltpu.HBM)
SEM = pl.BlockSpec(memory_space=pltpu.SEMAPHORE)
EFFECT = pltpu.SideEffectType.DATAFLOW_SIDE_EFFECTING


def _peer_of(k, x, y, c):
    return (1 - x if k & 4 else x, 1 - y if k & 2 else y, 1 - c if k & 1 else c)


def _flight(a, k):
    return a * (N_DEV - 1) + k - 1


def _exchange_start(arrays, *, name, broadcast=False, paired=()):
    n = len(arrays)

    def body(*refs):
        ins, lands = refs[:n], refs[n:2 * n]
        send_sems, recv_sems = refs[2 * n:2 * n + 2]
        token = refs[2 * n + 2 + 2 * n]
        x, y, c = _position()
        me = _slot((x, y, c))
        for k in range(1, N_DEV):
            peer = _peer_of(k, x, y, c)
            for a in range(n):
                at = _pair_slot(_slot(peer)) if a in paired else _slot(peer)
                pltpu.make_async_remote_copy(
                    src_ref=ins[a] if broadcast else ins[a].at[at], dst_ref=lands[a].at[me],
                    send_sem=send_sems.at[_flight(a, k)], recv_sem=recv_sems.at[_flight(a, k)],
                    device_id=peer, device_id_type=MESH).start()
        if broadcast:
            for a in range(n):
                pltpu.make_async_copy(ins[a], lands[a].at[me], refs[-1].at[a]).start()
        token[...] = jnp.zeros_like(token)

    land_shapes = [((N_DEV,) + s.shape) if broadcast else s.shape for s in arrays]
    lands = [pltpu.with_memory_space_constraint(lax.empty(shp, s.dtype), pltpu.HBM) for shp, s in zip(land_shapes, arrays)]
    srcs = [pltpu.with_memory_space_constraint(s, pltpu.HBM) for s in arrays]
    outs = pl.pallas_call(
        body, name=name, in_specs=[HBM] * (2 * n),
        out_specs=[SEM, SEM] + [HBM] * (2 * n) + [pl.BlockSpec(memory_space=pltpu.VMEM)] + [SEM] * broadcast,
        out_shape=[pltpu.SemaphoreType.DMA((n * (N_DEV - 1),)), pltpu.SemaphoreType.DMA((n * (N_DEV - 1),))]
        + [pltpu.HBM(s.shape, s.dtype) for s in arrays] + [pltpu.HBM(shp, s.dtype) for shp, s in zip(land_shapes, arrays)]
        + [_sds((8, 128))] + [pltpu.SemaphoreType.DMA((n,))] * broadcast,
        input_output_aliases={i: 2 + i for i in range(2 * n)},
        compiler_params=pltpu.CompilerParams(has_side_effects=EFFECT),
    )(*srcs, *lands)
    flight = (outs[0], outs[1], outs[2:2 + n], outs[2 + n:2 + 2 * n], outs[2 + 2 * n])
    return flight + (outs[-1],) if broadcast else flight


def _exchange_wait(send_sems, recv_sems, srcs, lands, after, *, name, broadcast=False, own_sems=None):
    n = len(srcs)

    def body(*refs):
        ins, lnd = refs[:n], refs[n:2 * n]
        send_ref, recv_ref = refs[2 * n:2 * n + 2]
        x, y, c = _position()
        for k in range(1, N_DEV):
            for a in range(n):
                cp = pltpu.make_async_remote_copy(
                    src_ref=ins[a] if broadcast else ins[a].at[0], dst_ref=lnd[a].at[0], send_sem=send_ref.at[_flight(a, k)],
                    recv_sem=recv_ref.at[_flight(a, k)], device_id=_peer_of(k, x, y, c), device_id_type=MESH)
                cp.wait_send()
                cp.wait_recv()
        if broadcast:
            for a in range(n):
                pltpu.make_async_copy(ins[a], lnd[a].at[0], refs[2 * n + 3].at[a]).wait()

    outs = pl.pallas_call(
        body, name=name, in_specs=[HBM] * (2 * n) + [SEM, SEM, ANY] + [SEM] * broadcast, out_specs=[HBM] * (2 * n),
        out_shape=[pltpu.HBM(s.shape, s.dtype) for s in srcs] + [pltpu.HBM(s.shape, s.dtype) for s in lands],
        input_output_aliases={i: i for i in range(2 * n)},
        compiler_params=pltpu.CompilerParams(has_side_effects=EFFECT),
    )(*srcs, *lands, send_sems, recv_sems, after, *([own_sems] if broadcast else []))
    return outs[:n], outs[n:]


_G_LAND_ROWS = 528


def _g_in_pieces(dev):
    runs, seams = _w_in_plan()
    pieces = [(w, r, n, s) for j, s, w, r, n in runs if j == dev]
    pieces += [(w, r, _ROW_TILE, s0) for w, r, j0, s0, j1, s1 in seams if j0 == dev]
    pieces += [(w, r, _ROW_TILE, s1) for w, r, j0, s0, j1, s1 in seams if j1 == dev]
    merged = []
    for w, r, n, s in sorted(pieces, key=lambda p: p[3]):
        if merged and merged[-1][0] == w and merged[-1][1] + merged[-1][2] == r and merged[-1][3] + merged[-1][2] == s:
            merged[-1] = (w, merged[-1][1], merged[-1][2] + n, merged[-1][3])
        else:
            merged.append((w, r, n, s))
    return merged


def _coords(dev):
    return tuple(jnp.int32(v) for v in (dev >> 2, (dev >> 1) & 1, dev & 1))


def _exchange_start_in(g_ws, conv_slabs, *, name):
    srcs = list(g_ws) + [conv_slabs]
    n = len(srcs)

    def body(*refs):
        g_refs, cv_ref, land, land_cv = refs[:n - 1], refs[n - 1], refs[n], refs[n + 1]
        send_sems, recv_sems = refs[n + 2:n + 4]
        token = refs[-1]
        me = _slot(_position())
        pieces = [_g_in_pieces(dev) for dev in range(N_DEV)]
        for i in range(max(len(p) for p in pieces)):
            for dev in range(N_DEV):
                if i < len(pieces[dev]):
                    @pl.when(me != dev)
                    def _(dev=dev, i=i):
                        w, r, rows, s = pieces[dev][i]
                        k = me ^ dev
                        pltpu.make_async_remote_copy(
                            src_ref=g_refs[w].at[pl.ds(r, rows)], dst_ref=land.at[me, pl.ds(s, rows)],
                            send_sem=send_sems.at[_flight(0, k)], recv_sem=recv_sems.at[_flight(0, k)],
                            device_id=_coords(dev), device_id_type=MESH).start()
        for dev in range(N_DEV):
            @pl.when(me != dev)
            def _(dev=dev):
                k = me ^ dev
                pltpu.make_async_remote_copy(
                    src_ref=cv_ref.at[dev], dst_ref=land_cv.at[me], send_sem=send_sems.at[_flight(1, k)],
                    recv_sem=recv_sems.at[_flight(1, k)], device_id=_coords(dev), device_id_type=MESH).start()
        token[...] = jnp.zeros_like(token)

    lands = [lax.empty((N_DEV, _G_LAND_ROWS, g_ws[0].shape[1]), g_ws[0].dtype), lax.empty(conv_slabs.shape, conv_slabs.dtype)]
    ops = [pltpu.with_memory_space_constraint(a, pltpu.HBM) for a in srcs + lands]
    outs = pl.pallas_call(
        body, name=name, in_specs=[HBM] * len(ops),
        out_specs=[SEM, SEM] + [HBM] * len(ops) + [pl.BlockSpec(memory_space=pltpu.VMEM)],
        out_shape=[pltpu.SemaphoreType.DMA((2 * (N_DEV - 1),)), pltpu.SemaphoreType.DMA((2 * (N_DEV - 1),))]
        + [pltpu.HBM(a.shape, a.dtype) for a in ops] + [_sds((8, 128))],
        input_output_aliases={i: 2 + i for i in range(len(ops))},
        compiler_params=pltpu.CompilerParams(has_side_effects=EFFECT),
    )(*ops)
    return outs[0], outs[1], outs[2:2 + n], outs[2 + n:4 + n], outs[-1]


def _own_pieces(g_ws, land, after, *, name):
    n = len(g_ws)

    def body(*refs):
        g_refs, land_ref = refs[:n], refs[n]
        token, slab, sem = refs[-3:]
        me = _slot(_position())
        for dev in range(N_DEV):
            @pl.when(me == dev)
            def _(dev=dev):
                own = [pltpu.make_async_copy(g_refs[w].at[pl.ds(r, rows)], slab.at[pl.ds(s, rows)], sem)
                       for w, r, rows, s in _g_in_pieces(dev)]
                for cp in own:
                    cp.start()
                for cp in own:
                    cp.wait()
                out = pltpu.make_async_copy(slab, land_ref.at[dev, pl.ds(0, _LAND_ROWS)], sem)
                out.start()
                out.wait()
        token[...] = jnp.zeros_like(token)

    return pl.pallas_call(
        body, name=name, in_specs=[HBM] * (n + 1) + [ANY], out_specs=[HBM, pl.BlockSpec(memory_space=pltpu.VMEM)],
        out_shape=[pltpu.HBM(land.shape, land.dtype), _sds((8, 128))], input_output_aliases={n: 0},
        scratch_shapes=[pltpu.VMEM((_LAND_ROWS, land.shape[2]), land.dtype), pltpu.SemaphoreType.DMA(())],
    )(*g_ws, land, after)


def _exchange_wait_in(send_sems, recv_sems, srcs, lands, after, *, name):
    n = len(srcs)

    def body(*refs):
        g_refs, cv_ref, land, land_cv = refs[:n - 1], refs[n - 1], refs[n], refs[n + 1]
        send_ref, recv_ref = refs[n + 2:n + 4]
        me = _slot(_position())

        def copies(dev, k):
            cps = [pltpu.make_async_remote_copy(
                src_ref=g_refs[w].at[pl.ds(r, rows)], dst_ref=land.at[0, pl.ds(s, rows)], send_sem=send_ref.at[_flight(0, k)],
                recv_sem=recv_ref.at[_flight(0, k)], device_id=_coords(dev), device_id_type=MESH)
                for w, r, rows, s in _g_in_pieces(dev)]
            return cps + [pltpu.make_async_remote_copy(
                src_ref=cv_ref.at[0], dst_ref=land_cv.at[0], send_sem=send_ref.at[_flight(1, k)],
                recv_sem=recv_ref.at[_flight(1, k)], device_id=_coords(dev), device_id_type=MESH)]

        for dev in range(N_DEV):
            @pl.when(me != dev)
            def _(dev=dev):
                for cp in copies(dev, me ^ dev):
                    cp.wait_send()

            @pl.when(me == dev)
            def _(dev=dev):
                for k in range(1, N_DEV):
                    for cp in copies(dev, k):
                        cp.wait_recv()

    ops = list(srcs) + list(lands)
    outs = pl.pallas_call(
        body, name=name, in_specs=[HBM] * len(ops) + [SEM, SEM, ANY], out_specs=[HBM] * len(ops),
        out_shape=[pltpu.HBM(a.shape, a.dtype) for a in ops],
        input_output_aliases={i: i for i in range(len(ops))},
        compiler_params=pltpu.CompilerParams(has_side_effects=EFFECT),
    )(*ops, send_sems, recv_sems, after)
    return outs[:n], outs[n:]


def _adam_update(g, w, m, v):
    c1 = 1.0 - ADAM_B1 ** ADAM_STEP
    c2 = 1.0 - ADAM_B2 ** ADAM_STEP
    nm = ADAM_B1 * m + (1.0 - ADAM_B1) * g
    nv = ADAM_B2 * v + (1.0 - ADAM_B2) * (g * g)
    return -ADAM_LR * ((nm / c1) / (jnp.sqrt(nv / c2) + ADAM_EPS) + ADAM_WD * w), nm, nv


def _adamw(landed, sent, me, w, m, v, *, name, tr=None, tc=None):
    R, C = w.shape
    tr = R if tr is None else tr
    tc = C if tc is None else tc
    assert R % tr == 0 and C % tc == 0

    def body(me_ref, own_ref, p_ref, w_ref, m_ref, v_ref, g_ref, d_ref, nm_ref, nv_ref, token_ref):
        token_ref[...] = jnp.zeros_like(token_ref)
        g = own_ref[...].astype(F32)
        for s in range(N_DEV):
            g = g + jnp.where(me_ref[1] == s, 0.0, p_ref[s].astype(F32))
        delta, nm, nv = _adam_update(g, w_ref[...], m_ref[...], v_ref[...])
        g_ref[...] = g
        nm_ref[...] = nm
        nv_ref[...] = nv
        d_ref[...] = delta

    blk = pl.BlockSpec((tr, tc), lambda i, j, me_ref: (i, j))
    return pl.pallas_call(
        body, name=name,
        grid_spec=pltpu.PrefetchScalarGridSpec(
            num_scalar_prefetch=1, grid=(R // tr, C // tc),
            in_specs=[pl.BlockSpec((None, tr, tc), lambda i, j, me_ref: (me_ref[0], i, j)),
                      pl.BlockSpec((N_DEV, tr, tc), lambda i, j, me_ref: (0, i, j)), blk, blk, blk],
            out_specs=[blk] * 4 + [pl.BlockSpec((8, 128), lambda i, j, me_ref: (0, 0))]),
        out_shape=[_sds((R, C))] * 4 + [_sds((8, 128))],
        compiler_params=_params(("arbitrary", "arbitrary")),
    )(me, sent, landed, w, m, v)


def _adamw_rowwise(landed, me, w, m, v, *, name, tr=128):
    C = landed.shape[2]
    R, q, extra = _IN_ROWS, C // 128, _ROW_TILE
    assert tr % extra == 0 and (pl.cdiv(R, tr) * tr + extra) <= landed.shape[1] and N_DEV - 1 + _GAP < extra

    steps = pl.cdiv(R, tr)

    def body(me_ref, land_ref, w_ref, m_ref, v_ref, g_ref, d_ref, nm_ref, nv_ref, g_scr, zone, sems):
        i = pl.program_id(0)

        def head(j):
            return pltpu.make_async_copy(land_ref.at[:, pl.ds(tr * j, extra)], zone.at[:, pl.ds(tr * j, extra)], sems.at[2 * j])

        def rest(j):
            rows = pl.ds(tr * j + extra, tr - extra)
            return pltpu.make_async_copy(land_ref.at[:, rows], zone.at[:, rows], sems.at[2 * j + 1])

        @pl.when(i == 0)
        def _():
            for j in range(steps):
                head(j).start()
                rest(j).start()
            head(0).wait()

        for j in range(steps):
            @pl.when(i == j)
            def _(j=j):
                rest(j).wait()
                if j + 1 < steps:
                    head(j + 1).wait()

        first = pl.multiple_of(tr * i, tr)
        slab = functools.reduce(lambda x, y: x + y, [zone[s, pl.ds(first, tr + extra), :].astype(F32) for s in range(N_DEV)])
        for dev in range(N_DEV):
            @pl.when(me_ref[0] == dev)
            def _(dev=dev):
                lo, hi = slab[dev:dev + tr], slab[dev + _GAP:dev + _GAP + tr]
                if _IN_ROWS * (dev + 1) <= _O3:
                    g_scr[...] = lo
                elif _IN_ROWS * dev >= _O3:
                    g_scr[...] = hi
                else:
                    r = _IN_ROWS * dev + tr * i + lax.broadcasted_iota(jnp.int32, (tr, 1), 0)
                    g_scr[...] = jnp.where(r < _O3, lo, hi)
        g = g_scr[...]
        for s in range(q):
            rows = pl.ds(s, tr, stride=q)
            gs = g[:, 128 * s:128 * (s + 1)]
            delta, nm, nv = _adam_update(gs, w_ref[rows, :], m_ref[rows, :], v_ref[rows, :])
            g_ref[rows, :] = gs
            nm_ref[rows, :] = nm
            nv_ref[rows, :] = nv
            d_ref[rows, :] = delta

    blk = pl.BlockSpec((tr * q, 128), lambda i, me_ref: (i, 0))
    return pl.pallas_call(
        body, name=name,
        grid_spec=pltpu.PrefetchScalarGridSpec(
            num_scalar_prefetch=1, grid=(pl.cdiv(R, tr),),
            in_specs=[ANY, blk, blk, blk], out_specs=[blk] * 4,
            scratch_shapes=[pltpu.VMEM((tr, C), F32), pltpu.VMEM((N_DEV, steps * tr + extra, C), landed.dtype),
                            pltpu.SemaphoreType.DMA((2 * steps,))]),
        out_shape=[_sds((R * q, 128))] * 4,
        compiler_params=_params(("arbitrary",)),
    )(me, landed, w, m, v)


_SMALL_ROWS = 8
_SMALL_SLOTS = ((0, 0, D_MODEL), (1, 0, D_MODEL), (2, 0, D_MODEL), (3, 0, GDN_DIM), (3, GDN_DIM, GDN_HEADS),
                (3, GDN_DIM + GDN_HEADS, GDN_HEADS))
_LOSS_LANE = 2 * GDN_DIM


def _pack_small(norm1, norm2, final, gnw, a_log, dt_bias, loss):
    row3 = jnp.concatenate([gnw, a_log, dt_bias, jnp.zeros((1, 128 - 2 * GDN_HEADS), F32), loss,
                            jnp.zeros((1, D_MODEL - 3 * 128), F32)], axis=1)
    return jnp.concatenate([norm1, norm2, final, row3, jnp.zeros((_SMALL_ROWS - 4, D_MODEL), F32)], axis=0)


def _adamw_small(packs, ws, ms, vs, *, name):
    n = len(ws)

    def body(p_ref, *refs):
        w_refs, m_refs, v_refs = refs[:n], refs[n:2 * n], refs[2 * n:3 * n]
        outs = refs[3 * n:]
        g_all = p_ref[0]
        for s in range(1, N_DEV):
            g_all = g_all + p_ref[s]
        for i, (row, lane, width) in enumerate(_SMALL_SLOTS):
            g = g_all[row:row + 1, lane:lane + width]
            delta, nm, nv = _adam_update(g, w_refs[i][...], m_refs[i][...], v_refs[i][...])
            for o_ref, val in zip(outs[4 * i:4 * i + 4], (g, delta, nm, nv)):
                o_ref[...] = val
        outs[-1][...] = g_all[3:4, _LOSS_LANE:_LOSS_LANE + 128]

    vm = pl.BlockSpec(memory_space=pltpu.VMEM)
    outs = pl.pallas_call(
        body, name=name, in_specs=[vm] * (1 + 3 * n), out_specs=[vm] * (4 * n + 1),
        out_shape=[_sds(w.shape) for w in ws for _ in range(4)] + [_sds((1, 128))],
    )(packs, *ws, *ms, *vs)
    return [outs[4 * i:4 * i + 4] for i in range(n)], outs[-1]


def _slabs_by_cols(g):
    r = g.shape[0]
    return g.reshape(r, N_DEV, -1).transpose(1, 0, 2)


def _cols_from_slabs(s):
    return s.transpose(1, 0, 2).reshape(s.shape[1], -1)


def kernel(x, norm1_w, w_in, conv_qkv_w, a_log, dt_bias, gdn_norm_w, w_out, norm2_w, w_up, ffn_conv_w, w_down, final_norm_w, loss_target, m_norm1_w, m_w_in, m_conv_qkv_w, m_a_log, m_dt_bias, m_gdn_norm_w, m_w_out, m_norm2_w, m_w_up, m_ffn_conv_w, m_w_down, m_final_norm_w, v_norm1_w, v_w_in, v_conv_qkv_w, v_a_log, v_dt_bias, v_gdn_norm_w, v_w_out, v_norm2_w, v_w_up, v_ffn_conv_w, v_w_down, v_final_norm_w):
    bf = lambda a: a.astype(BF16)
    me = _slot(_position())
    me1 = jnp.reshape(me, (1,)).astype(jnp.int32)
    t_in = lambda a: a[0].T
    rows = lambda a: a.reshape(D_MODEL // 128, 128, -1).transpose(2, 0, 1).reshape(-1, 128)
    gw_in, g_conv_a = _all_gather([_shifted_slab(rows(w_in), me1, name="shift_w_in"), conv_qkv_w[0]], name="gather_w_in")
    late_src, _ = lax.optimization_barrier(([bf(w_out[0]), bf(t_in(w_up)), bf(w_down[0]), ffn_conv_w[0]], gw_in))
    l_send, l_recv, l_srcs, l_lands, l_token, l_own = _exchange_start(late_src, name="weights_start", broadcast=True)

    def late_weights(after):
        _, (gw_out, gw_up, gw_down, g_conv_f) = _exchange_wait(
            l_send, l_recv, l_srcs, l_lands, after, name="weights_wait", broadcast=True, own_sems=l_own)
        return gw_out.reshape(D_MODEL, D_MODEL), gw_up, g_conv_f, gw_down.reshape(D_FF, D_MODEL)

    flights = {}

    def emit(group, **grads):
        paired = ()
        if group == "in":
            *flight, token = _exchange_start_in([grads["w_a"], grads["w_z"], grads["w_b"]], _slabs_by_cols(grads["conv_a"]),
                                                name="grads_start_in")
            flights[group] = flight
            return (token,)
        if group == "ffn":
            slabs = dict(w_down=grads["w_down"].reshape(N_DEV, -1, D_MODEL), w_up=grads["w_up"], conv_f=grads["conv_f"])
            paired = (1, 2)
        else:
            slabs = {k: v.reshape(N_DEV, -1, D_MODEL) for k, v in grads.items()}
        names = list(slabs)
        *flight, token = _exchange_start([slabs[k] for k in names], paired=paired, name="grads_start_" + group)
        flights[group] = (names, flight)
        return (token,)

    loss, grad_x, g = _local_step(
        x[0], loss_target[0], norm1_w, gw_in, _cols_from_slabs(g_conv_a), a_log, dt_bias,
        gdn_norm_w, norm2_w, final_norm_w[None], late_weights, emit, start_after=(l_token,))
    got = {}

    def collect(group, after):
        names, (send_sems, recv_sems, srcs, lands) = flights[group]
        srcs, landed = _exchange_wait(send_sems, recv_sems, srcs, lands, after, name="grads_wait_" + group)
        got.update(zip(names, zip(landed, srcs)))

    def update(key, w, m, v, paired=False, **tiles):
        where = jnp.concatenate([_pair_slot(me1) if paired else me1, me1])
        return _adamw(*got[key], where, w, m, v, name="adamw_" + key, **tiles)

    in_sems, in_srcs, in_lands = flights["in"][:2], flights["in"][2], flights["in"][3]
    own_land, own_token = _own_pieces(in_srcs[:3], in_lands[0], grad_x, name="grads_own_in")
    collect("ffn", own_token)
    collect("out", own_token)
    *o_out, t1 = update("w_out", w_out[0], m_w_out[0], v_w_out[0])
    *o_up, t2 = update("w_up", t_in(w_up), t_in(m_w_up), t_in(v_w_up), paired=True, tr=176)
    o_up = [o.T for o in o_up]
    *o_down, t3 = update("w_down", w_down[0], m_w_down[0], v_w_down[0], tr=176)
    *o_cf, t4 = update("conv_f", ffn_conv_w[0], m_ffn_conv_w[0], v_ffn_conv_w[0], paired=True)
    pack = _pack_small(g["norm1"], g["norm2"], g["final"], g["gnw"], g["small"][:, 0:GDN_HEADS],
                       g["small"][:, GDN_HEADS:2 * GDN_HEADS], loss)
    small_all = _gather_direct(pack, after=(t1, t2, t3, t4), name="gather_small")
    srcs, (g_land, conv_land) = _exchange_wait_in(*in_sems, in_srcs, [own_land, in_lands[1]], small_all, name="grads_wait_in")
    got["conv_a"] = (conv_land, srcs[-1])
    o_in = [o.reshape(-1, D_MODEL // 128, 128).transpose(1, 2, 0).reshape(D_MODEL, -1) for o in _adamw_rowwise(
        g_land, me1, rows(w_in), rows(m_w_in), rows(v_w_in), name="adamw_w_in")]
    o_ca = update("conv_a", conv_qkv_w[0], m_conv_qkv_w[0], v_conv_qkv_w[0])
    (o_n1, o_n2, o_fin, o_gn, o_al, o_dt), total = _adamw_small(
        small_all, (norm1_w, norm2_w, final_norm_w[None], gdn_norm_w, a_log, dt_bias),
        (m_norm1_w, m_norm2_w, m_final_norm_w[None], m_gdn_norm_w, m_a_log, m_dt_bias),
        (v_norm1_w, v_norm2_w, v_final_norm_w[None], v_gdn_norm_w, v_a_log, v_dt_bias), name="adamw_small")
    outs = [total[0, 0], grad_x[None]]
    for k in range(4):
        outs += [o_n1[k], o_in[k][None], o_ca[k][None], o_al[k], o_dt[k], o_gn[k], o_out[k][None], o_n2[k], o_up[k][None],
                 o_cf[k][None], o_down[k][None], o_fin[k][0]]
    return tuple(outs)
```
